```python
import math
import jax, jax.numpy as jnp
from jax import lax
import numpy as np

D_MODEL = 1024
BATCH = 8
SEQ = 2048
DEPTH = 1

PLE_DIM = 256
D_MIX = 2 * D_MODEL
LRU_WIDTH = D_MIX // 2
LRU_HEADS = 16
LRU_HEAD_DIM = LRU_WIDTH // LRU_HEADS
LRU_C = 8.0
SSD_WIDTH = D_MIX - LRU_WIDTH
SSD_HEAD_DIM = 64
SSD_HEADS = SSD_WIDTH // SSD_HEAD_DIM
SSD_GROUPS = 4
SSD_STATE = 128
SSD_CHUNK = 128
SSD_XBC = SSD_WIDTH + 2 * SSD_GROUPS * SSD_STATE
CONV_WIDTH = 4
D_FF = 4 * D_MODEL
ALPHA = (2.0 * DEPTH) ** 0.25
BETA = (8.0 * DEPTH) ** -0.25
LN_EPS = 1e-5
RMS_EPS = 1e-5
IN_SPLIT_POINTS = (LRU_WIDTH, 2 * LRU_WIDTH, 2 * LRU_WIDTH + SSD_WIDTH, 2 * LRU_WIDTH + SSD_WIDTH + SSD_XBC)
D_IN_PROJ = 2 * LRU_WIDTH + SSD_WIDTH + SSD_XBC + SSD_HEADS

kernel_name = "hymba_style_rglru_ssd_deepnorm_block"


def layer_norm(x, g, b):
    xf = x.astype(jnp.float32)
    mu = jnp.mean(xf, axis=-1, keepdims=True)
    xc = xf - mu
    var = jnp.mean(xc * xc, axis=-1, keepdims=True)
    y = xc * lax.rsqrt(var + LN_EPS) * g.astype(jnp.float32) + b.astype(jnp.float32)
    return y.astype(x.dtype)


def causal_depthwise_conv(x, w, b):
    c = x.shape[-1]
    y = lax.conv_general_dilated(x, w[:, None, :], window_strides=(1,), padding=[(CONV_WIDTH - 1, 0)],
                                 dimension_numbers=("NWC", "WIO", "NWC"), feature_group_count=c)
    return y + b


def rg_lru(x, w_a, b_a, w_x, b_x, a_param):
    bsz, s, _ = x.shape
    xf = x.astype(jnp.float32)
    xh = xf.reshape(bsz, s, LRU_HEADS, LRU_HEAD_DIM)
    r = jax.nn.sigmoid(jnp.einsum("bshi,hij->bshj", xh, w_a.astype(jnp.float32)) + b_a.astype(jnp.float32))
    i = jax.nn.sigmoid(jnp.einsum("bshi,hij->bshj", xh, w_x.astype(jnp.float32)) + b_x.astype(jnp.float32))
    r = r.reshape(bsz, s, LRU_WIDTH)
    i = i.reshape(bsz, s, LRU_WIDTH)
    log_a = -LRU_C * r * jax.nn.softplus(-a_param.astype(jnp.float32))
    a = jnp.exp(log_a)
    mult = jnp.sqrt(-jnp.expm1(2.0 * log_a))
    first = (jnp.arange(s) == 0)[None, :, None]
    mult = jnp.where(first, 1.0, mult)
    u = xf * i * mult

    def combine(left, right):
        a_l, u_l = left
        a_r, u_r = right
        return a_l * a_r, a_r * u_l + u_r

    _, h = lax.associative_scan(combine, (a, u), axis=1)
    return h


def ssd_chunked(x, dt, A, Bm, Cm):
    bsz, s, nh, hp = x.shape
    g, n = Bm.shape[2], Bm.shape[3]
    j = nh // g
    c = s // SSD_CHUNK
    l = SSD_CHUNK
    xdt = (x * dt[..., None]).reshape(bsz, c, l, g, j, hp)
    a = (dt * A).reshape(bsz, c, l, g, j).transpose(0, 3, 4, 1, 2)
    Bc = Bm.reshape(bsz, c, l, g, n)
    Cc = Cm.reshape(bsz, c, l, g, n)
    a_cs = jnp.cumsum(a, axis=-1)
    causal = jnp.tril(jnp.ones((l, l), dtype=bool))
    seg = a_cs[..., :, None] - a_cs[..., None, :]
    decay_mat = jnp.exp(jnp.where(causal, seg, -jnp.inf))
    cb = jnp.einsum("bclgn,bcsgn->bgcls", Cc, Bc)
    scores = cb[:, :, None] * decay_mat
    y_diag = jnp.einsum("bgjcls,bcsgjp->bclgjp", scores, xdt)
    decay_states = jnp.exp(a_cs[..., -1:] - a_cs)
    states = jnp.einsum("bcsgn,bgjcs,bcsgjp->bcgjpn", Bc, decay_states, xdt)
    chunk_decay = jnp.exp(a_cs[..., -1])

    def step(carry, inp):
        st, dec = inp
        new = dec[..., None, None] * carry + st
        return new, carry

    init = jnp.zeros((bsz, g, j, hp, n), jnp.float32)
    _, prev = lax.scan(step, init, (states.transpose(1, 0, 2, 3, 4, 5), chunk_decay.transpose(3, 0, 1, 2)))
    prev = prev.transpose(1, 0, 2, 3, 4, 5)
    y_off = jnp.einsum("bclgn,bcgjpn,bgjcl->bclgjp", Cc, prev, jnp.exp(a_cs))
    return (y_diag + y_off).reshape(bsz, s, nh, hp)


def gated_rmsnorm(y, z, w):
    bsz, s, _ = y.shape
    yf = y * jax.nn.silu(z.astype(jnp.float32))
    yg = yf.reshape(bsz, s, SSD_GROUPS, -1)
    yg = yg * lax.rsqrt(jnp.mean(yg * yg, axis=-1, keepdims=True) + RMS_EPS)
    return yg.reshape(bsz, s, -1) * w.astype(jnp.float32)


def _fwd_setup_inputs(seed: int = 0) -> dict:
    key = jax.random.key(seed)
    ks = jax.random.split(key, 32)
    f32 = jnp.float32
    L = DEPTH
    nrm = lambda k, shape, scale: jax.random.normal(k, shape, f32) * scale
    x = jax.random.normal(ks[0], (BATCH, SEQ, D_MODEL), f32)
    p = jax.random.normal(ks[1], (DEPTH, BATCH, SEQ, PLE_DIM), f32)
    w_in = nrm(ks[2], (L, D_MODEL, D_IN_PROJ), D_MODEL ** -0.5)
    lru_conv_w = nrm(ks[3], (L, CONV_WIDTH, LRU_WIDTH), CONV_WIDTH ** -0.5)
    lru_conv_b = nrm(ks[4], (L, LRU_WIDTH), 0.02)
    lru_gate_a_w = nrm(ks[5], (L, LRU_HEADS, LRU_HEAD_DIM, LRU_HEAD_DIM), LRU_HEAD_DIM ** -0.5)
    lru_gate_a_b = nrm(ks[6], (L, LRU_HEADS, LRU_HEAD_DIM), 0.02)
    lru_gate_x_w = nrm(ks[7], (L, LRU_HEADS, LRU_HEAD_DIM, LRU_HEAD_DIM), LRU_HEAD_DIM ** -0.5)
    lru_gate_x_b = nrm(ks[8], (L, LRU_HEADS, LRU_HEAD_DIM), 0.02)
    a_pow = jax.random.uniform(ks[9], (L, LRU_WIDTH), f32, 0.9, 0.999) ** (1.0 / LRU_C)
    lru_a_param = jnp.log(a_pow) - jnp.log1p(-a_pow)
    ssd_conv_w = nrm(ks[10], (L, CONV_WIDTH, SSD_XBC), CONV_WIDTH ** -0.5)
    ssd_conv_b = nrm(ks[11], (L, SSD_XBC), 0.02)
    dt0 = jnp.exp(jax.random.uniform(ks[12], (L, SSD_HEADS), f32, math.log(1e-3), math.log(1e-1)))
    ssd_dt_bias = dt0 + jnp.log(-jnp.expm1(-dt0))
    ssd_a_log = jnp.log(jax.random.uniform(ks[13], (L, SSD_HEADS), f32, 1.0, 16.0))
    ssd_d = 1.0 + nrm(ks[14], (L, SSD_HEADS), 0.02)
    ssd_norm_w = 1.0 + nrm(ks[15], (L, SSD_WIDTH), 0.02)
    w_out = nrm(ks[16], (L, D_MIX, D_MODEL), BETA * D_MIX ** -0.5)
    ln1_g = 1.0 + nrm(ks[17], (L, D_MODEL), 0.02)
    ln1_b = nrm(ks[18], (L, D_MODEL), 0.02)
    w_ff1 = nrm(ks[19], (L, D_MODEL, D_FF), D_MODEL ** -0.5)
    w_ff2 = nrm(ks[20], (L, D_FF, D_MODEL), BETA * D_FF ** -0.5)
    ln2_g = 1.0 + nrm(ks[21], (L, D_MODEL), 0.02)
    ln2_b = nrm(ks[22], (L, D_MODEL), 0.02)
    w_ple_gate = nrm(ks[23], (L, D_MODEL, D_MODEL), D_MODEL ** -0.5)
    w_ple = nrm(ks[24], (L, PLE_DIM, D_MODEL), BETA * PLE_DIM ** -0.5)
    ln3_g = 1.0 + nrm(ks[25], (L, D_MODEL), 0.02)
    ln3_b = nrm(ks[26], (L, D_MODEL), 0.02)
    return {"x": x, "p": p, "w_in": w_in, "lru_conv_w": lru_conv_w, "lru_conv_b": lru_conv_b,
            "lru_gate_a_w": lru_gate_a_w, "lru_gate_a_b": lru_gate_a_b, "lru_gate_x_w": lru_gate_x_w,
            "lru_gate_x_b": lru_gate_x_b, "lru_a_param": lru_a_param, "ssd_conv_w": ssd_conv_w,
            "ssd_conv_b": ssd_conv_b, "ssd_dt_bias": ssd_dt_bias, "ssd_a_log": ssd_a_log, "ssd_d": ssd_d,
            "ssd_norm_w": ssd_norm_w, "w_out": w_out, "ln1_g": ln1_g, "ln1_b": ln1_b, "w_ff1": w_ff1,
            "w_ff2": w_ff2, "ln2_g": ln2_g, "ln2_b": ln2_b, "w_ple_gate": w_ple_gate, "w_ple": w_ple,
            "ln3_g": ln3_g, "ln3_b": ln3_b}


def _fwd_reference(x, p, w_in, lru_conv_w, lru_conv_b, lru_gate_a_w, lru_gate_a_b, lru_gate_x_w, lru_gate_x_b,
              lru_a_param, ssd_conv_w, ssd_conv_b, ssd_dt_bias, ssd_a_log, ssd_d, ssd_norm_w, w_out,
              ln1_g, ln1_b, w_ff1, w_ff2, ln2_g, ln2_b, w_ple_gate, w_ple, ln3_g, ln3_b):
    bsz, s, _ = x.shape
    for i in range(DEPTH):
        proj = jnp.einsum("bsd,de->bse", x, w_in[i])
        x_lru, g_lru, z, xbc, dt_raw = jnp.split(proj, IN_SPLIT_POINTS, axis=-1)
        xl = causal_depthwise_conv(x_lru, lru_conv_w[i], lru_conv_b[i])
        h = rg_lru(xl, lru_gate_a_w[i], lru_gate_a_b[i], lru_gate_x_w[i], lru_gate_x_b[i], lru_a_param[i])
        y_lru = (jax.nn.gelu(g_lru.astype(jnp.float32)) * h).astype(x.dtype)
        xbc = jax.nn.silu(causal_depthwise_conv(xbc, ssd_conv_w[i], ssd_conv_b[i]).astype(jnp.float32))
        xs, Bm, Cm = jnp.split(xbc, (SSD_WIDTH, SSD_WIDTH + SSD_GROUPS * SSD_STATE), axis=-1)
        xs = xs.reshape(bsz, s, SSD_HEADS, SSD_HEAD_DIM)
        Bm = Bm.reshape(bsz, s, SSD_GROUPS, SSD_STATE)
        Cm = Cm.reshape(bsz, s, SSD_GROUPS, SSD_STATE)
        dt = jax.nn.softplus(dt_raw.astype(jnp.float32) + ssd_dt_bias[i].astype(jnp.float32))
        A = -jnp.exp(ssd_a_log[i].astype(jnp.float32))
        y = ssd_chunked(xs, dt, A, Bm, Cm) + xs * ssd_d[i].astype(jnp.float32)[:, None]
        y_ssd = gated_rmsnorm(y.reshape(bsz, s, SSD_WIDTH), z, ssd_norm_w[i]).astype(x.dtype)
        mix = jnp.einsum("bse,ed->bsd", jnp.concatenate([y_lru, y_ssd], axis=-1), w_out[i])
        x = layer_norm(ALPHA * x + mix, ln1_g[i], ln1_b[i])
        hid = jnp.square(jax.nn.relu(jnp.einsum("bsd,df->bsf", x, w_ff1[i])))
        ff = jnp.einsum("bsf,fd->bsd", hid, w_ff2[i])
        x = layer_norm(ALPHA * x + ff, ln2_g[i], ln2_b[i])
        gate = jax.nn.sigmoid(jnp.einsum("bsd,de->bse", x, w_ple_gate[i]).astype(jnp.float32))
        ple = jnp.einsum("bsk,kd->bsd", p[i], w_ple[i]).astype(jnp.float32)
        x = layer_norm(ALPHA * x + (gate * ple).astype(x.dtype), ln3_g[i], ln3_b[i])
    return x


import jax as _jax
import jax.numpy as _jnp

TWIN_FORMAT = 'train_step'
FWD_PARAMS = ['x', 'p', 'w_in', 'lru_conv_w', 'lru_conv_b', 'lru_gate_a_w', 'lru_gate_a_b', 'lru_gate_x_w', 'lru_gate_x_b', 'lru_a_param', 'ssd_conv_w', 'ssd_conv_b', 'ssd_dt_bias', 'ssd_a_log', 'ssd_d', 'ssd_norm_w', 'w_out', 'ln1_g', 'ln1_b', 'w_ff1', 'w_ff2', 'ln2_g', 'ln2_b', 'w_ple_gate', 'w_ple', 'ln3_g', 'ln3_b']
TWIN_WEIGHTS = ['w_in', 'lru_conv_w', 'lru_conv_b', 'lru_gate_a_w', 'lru_gate_a_b', 'lru_gate_x_w', 'lru_gate_x_b', 'lru_a_param', 'ssd_conv_w', 'ssd_conv_b', 'ssd_dt_bias', 'ssd_a_log', 'ssd_d', 'ssd_norm_w', 'w_out', 'ln1_g', 'ln1_b', 'w_ff1', 'w_ff2', 'ln2_g', 'ln2_b', 'w_ple_gate', 'w_ple', 'ln3_g', 'ln3_b']
TWIN_DIFF_INPUT = 'x'
TWIN_INPUTS = ['x', 'p', 'w_in', 'lru_conv_w', 'lru_conv_b', 'lru_gate_a_w', 'lru_gate_a_b', 'lru_gate_x_w', 'lru_gate_x_b', 'lru_a_param', 'ssd_conv_w', 'ssd_conv_b', 'ssd_dt_bias', 'ssd_a_log', 'ssd_d', 'ssd_norm_w', 'w_out', 'ln1_g', 'ln1_b', 'w_ff1', 'w_ff2', 'ln2_g', 'ln2_b', 'w_ple_gate', 'w_ple', 'ln3_g', 'ln3_b', 'loss_target', 'm_w_in', 'm_lru_conv_w', 'm_lru_conv_b', 'm_lru_gate_a_w', 'm_lru_gate_a_b', 'm_lru_gate_x_w', 'm_lru_gate_x_b', 'm_lru_a_param', 'm_ssd_conv_w', 'm_ssd_conv_b', 'm_ssd_dt_bias', 'm_ssd_a_log', 'm_ssd_d', 'm_ssd_norm_w', 'm_w_out', 'm_ln1_g', 'm_ln1_b', 'm_w_ff1', 'm_w_ff2', 'm_ln2_g', 'm_ln2_b', 'm_w_ple_gate', 'm_w_ple', 'm_ln3_g', 'm_ln3_b', 'v_w_in', 'v_lru_conv_w', 'v_lru_conv_b', 'v_lru_gate_a_w', 'v_lru_gate_a_b', 'v_lru_gate_x_w', 'v_lru_gate_x_b', 'v_lru_a_param', 'v_ssd_conv_w', 'v_ssd_conv_b', 'v_ssd_dt_bias', 'v_ssd_a_log', 'v_ssd_d', 'v_ssd_norm_w', 'v_w_out', 'v_ln1_g', 'v_ln1_b', 'v_w_ff1', 'v_w_ff2', 'v_ln2_g', 'v_ln2_b', 'v_w_ple_gate', 'v_w_ple', 'v_ln3_g', 'v_ln3_b']
TWIN_OUTPUTS = ['loss', 'grad_x', 'grad_w_in', 'grad_lru_conv_w', 'grad_lru_conv_b', 'grad_lru_gate_a_w', 'grad_lru_gate_a_b', 'grad_lru_gate_x_w', 'grad_lru_gate_x_b', 'grad_lru_a_param', 'grad_ssd_conv_w', 'grad_ssd_conv_b', 'grad_ssd_dt_bias', 'grad_ssd_a_log', 'grad_ssd_d', 'grad_ssd_norm_w', 'grad_w_out', 'grad_ln1_g', 'grad_ln1_b', 'grad_w_ff1', 'grad_w_ff2', 'grad_ln2_g', 'grad_ln2_b', 'grad_w_ple_gate', 'grad_w_ple', 'grad_ln3_g', 'grad_ln3_b', 'delta_w_in', 'delta_lru_conv_w', 'delta_lru_conv_b', 'delta_lru_gate_a_w', 'delta_lru_gate_a_b', 'delta_lru_gate_x_w', 'delta_lru_gate_x_b', 'delta_lru_a_param', 'delta_ssd_conv_w', 'delta_ssd_conv_b', 'delta_ssd_dt_bias', 'delta_ssd_a_log', 'delta_ssd_d', 'delta_ssd_norm_w', 'delta_w_out', 'delta_ln1_g', 'delta_ln1_b', 'delta_w_ff1', 'delta_w_ff2', 'delta_ln2_g', 'delta_ln2_b', 'delta_w_ple_gate', 'delta_w_ple', 'delta_ln3_g', 'delta_ln3_b', 'new_m_w_in', 'new_m_lru_conv_w', 'new_m_lru_conv_b', 'new_m_lru_gate_a_w', 'new_m_lru_gate_a_b', 'new_m_lru_gate_x_w', 'new_m_lru_gate_x_b', 'new_m_lru_a_param', 'new_m_ssd_conv_w', 'new_m_ssd_conv_b', 'new_m_ssd_dt_bias', 'new_m_ssd_a_log', 'new_m_ssd_d', 'new_m_ssd_norm_w', 'new_m_w_out', 'new_m_ln1_g', 'new_m_ln1_b', 'new_m_w_ff1', 'new_m_w_ff2', 'new_m_ln2_g', 'new_m_ln2_b', 'new_m_w_ple_gate', 'new_m_w_ple', 'new_m_ln3_g', 'new_m_ln3_b', 'new_v_w_in', 'new_v_lru_conv_w', 'new_v_lru_conv_b', 'new_v_lru_gate_a_w', 'new_v_lru_gate_a_b', 'new_v_lru_gate_x_w', 'new_v_lru_gate_x_b', 'new_v_lru_a_param', 'new_v_ssd_conv_w', 'new_v_ssd_conv_b', 'new_v_ssd_dt_bias', 'new_v_ssd_a_log', 'new_v_ssd_d', 'new_v_ssd_norm_w', 'new_v_w_out', 'new_v_ln1_g', 'new_v_ln1_b', 'new_v_w_ff1', 'new_v_w_ff2', 'new_v_ln2_g', 'new_v_ln2_b', 'new_v_w_ple_gate', 'new_v_w_ple', 'new_v_ln3_g', 'new_v_ln3_b']
TWIN_LEAF_KINDS = {'loss': 'loss', 'grad_x': 'grad_x', 'grad_w_in': 'grad_w', 'grad_lru_conv_w': 'grad_w', 'grad_lru_conv_b': 'grad_w', 'grad_lru_gate_a_w': 'grad_w', 'grad_lru_gate_a_b': 'grad_w', 'grad_lru_gate_x_w': 'grad_w', 'grad_lru_gate_x_b': 'grad_w', 'grad_lru_a_param': 'grad_w', 'grad_ssd_conv_w': 'grad_w', 'grad_ssd_conv_b': 'grad_w', 'grad_ssd_dt_bias': 'grad_w', 'grad_ssd_a_log': 'grad_w', 'grad_ssd_d': 'grad_w', 'grad_ssd_norm_w': 'grad_w', 'grad_w_out': 'grad_w', 'grad_ln1_g': 'grad_w', 'grad_ln1_b': 'grad_w', 'grad_w_ff1': 'grad_w', 'grad_w_ff2': 'grad_w', 'grad_ln2_g': 'grad_w', 'grad_ln2_b': 'grad_w', 'grad_w_ple_gate': 'grad_w', 'grad_w_ple': 'grad_w', 'grad_ln3_g': 'grad_w', 'grad_ln3_b': 'grad_w', 'delta_w_in': 'delta_w', 'delta_lru_conv_w': 'delta_w', 'delta_lru_conv_b': 'delta_w', 'delta_lru_gate_a_w': 'delta_w', 'delta_lru_gate_a_b': 'delta_w', 'delta_lru_gate_x_w': 'delta_w', 'delta_lru_gate_x_b': 'delta_w', 'delta_lru_a_param': 'delta_w', 'delta_ssd_conv_w': 'delta_w', 'delta_ssd_conv_b': 'delta_w', 'delta_ssd_dt_bias': 'delta_w', 'delta_ssd_a_log': 'delta_w', 'delta_ssd_d': 'delta_w', 'delta_ssd_norm_w': 'delta_w', 'delta_w_out': 'delta_w', 'delta_ln1_g': 'delta_w', 'delta_ln1_b': 'delta_w', 'delta_w_ff1': 'delta_w', 'delta_w_ff2': 'delta_w', 'delta_ln2_g': 'delta_w', 'delta_ln2_b': 'delta_w', 'delta_w_ple_gate': 'delta_w', 'delta_w_ple': 'delta_w', 'delta_ln3_g': 'delta_w', 'delta_ln3_b': 'delta_w', 'new_m_w_in': 'new_m', 'new_m_lru_conv_w': 'new_m', 'new_m_lru_conv_b': 'new_m', 'new_m_lru_gate_a_w': 'new_m', 'new_m_lru_gate_a_b': 'new_m', 'new_m_lru_gate_x_w': 'new_m', 'new_m_lru_gate_x_b': 'new_m', 'new_m_lru_a_param': 'new_m', 'new_m_ssd_conv_w': 'new_m', 'new_m_ssd_conv_b': 'new_m', 'new_m_ssd_dt_bias': 'new_m', 'new_m_ssd_a_log': 'new_m', 'new_m_ssd_d': 'new_m', 'new_m_ssd_norm_w': 'new_m', 'new_m_w_out': 'new_m', 'new_m_ln1_g': 'new_m', 'new_m_ln1_b': 'new_m', 'new_m_w_ff1': 'new_m', 'new_m_w_ff2': 'new_m', 'new_m_ln2_g': 'new_m', 'new_m_ln2_b': 'new_m', 'new_m_w_ple_gate': 'new_m', 'new_m_w_ple': 'new_m', 'new_m_ln3_g': 'new_m', 'new_m_ln3_b': 'new_m', 'new_v_w_in': 'new_v', 'new_v_lru_conv_w': 'new_v', 'new_v_lru_conv_b': 'new_v', 'new_v_lru_gate_a_w': 'new_v', 'new_v_lru_gate_a_b': 'new_v', 'new_v_lru_gate_x_w': 'new_v', 'new_v_lru_gate_x_b': 'new_v', 'new_v_lru_a_param': 'new_v', 'new_v_ssd_conv_w': 'new_v', 'new_v_ssd_conv_b': 'new_v', 'new_v_ssd_dt_bias': 'new_v', 'new_v_ssd_a_log': 'new_v', 'new_v_ssd_d': 'new_v', 'new_v_ssd_norm_w': 'new_v', 'new_v_w_out': 'new_v', 'new_v_ln1_g': 'new_v', 'new_v_ln1_b': 'new_v', 'new_v_w_ff1': 'new_v', 'new_v_w_ff2': 'new_v', 'new_v_ln2_g': 'new_v', 'new_v_ln2_b': 'new_v', 'new_v_w_ple_gate': 'new_v', 'new_v_w_ple': 'new_v', 'new_v_ln3_g': 'new_v', 'new_v_ln3_b': 'new_v'}


def _forward(args):
    return _fwd_reference(*[args[k] for k in FWD_PARAMS])


def _output_shape():
    out = _jax.eval_shape(lambda: _forward(_fwd_setup_inputs(0)))
    return out.shape, out.dtype

N_MICROBATCH = 1
ADAM_LR = 0.001
ADAM_B1 = 0.9
ADAM_B2 = 0.999
ADAM_EPS = 1e-08
ADAM_WD = 0.01
ADAM_STEP = 10
PER_EXAMPLE_BATCH_AXIS = {'x': 0, 'p': 1, 'loss_target': 0}
SHARED_INPUTS = []
_WEIGHT_DTYPES = {'w_in': _jnp.float32, 'lru_conv_w': _jnp.float32, 'lru_conv_b': _jnp.float32, 'lru_gate_a_w': _jnp.float32, 'lru_gate_a_b': _jnp.float32, 'lru_gate_x_w': _jnp.float32, 'lru_gate_x_b': _jnp.float32, 'lru_a_param': _jnp.float32, 'ssd_conv_w': _jnp.float32, 'ssd_conv_b': _jnp.float32, 'ssd_dt_bias': _jnp.float32, 'ssd_a_log': _jnp.float32, 'ssd_d': _jnp.float32, 'ssd_norm_w': _jnp.float32, 'w_out': _jnp.float32, 'ln1_g': _jnp.float32, 'ln1_b': _jnp.float32, 'w_ff1': _jnp.float32, 'w_ff2': _jnp.float32, 'ln2_g': _jnp.float32, 'ln2_b': _jnp.float32, 'w_ple_gate': _jnp.float32, 'w_ple': _jnp.float32, 'ln3_g': _jnp.float32, 'ln3_b': _jnp.float32}
MOMENT_SCALE = {'w_in': 3.045880e-02, 'lru_conv_w': 2.103027e-02, 'lru_conv_b': 2.261299e-01, 'lru_gate_a_w': 7.955849e-03, 'lru_gate_a_b': 6.997264e-03, 'lru_gate_x_w': 1.436724e-02, 'lru_gate_x_b': 7.481901e-03, 'lru_a_param': 1.041023e-02, 'ssd_conv_w': 3.236747e-02, 'ssd_conv_b': 4.597297e-02, 'ssd_dt_bias': 7.345439e-02, 'ssd_a_log': 8.449688e-02, 'ssd_d': 1.970228e-01, 'ssd_norm_w': 4.243275e-02, 'w_out': 8.124274e-02, 'ln1_g': 3.859154e-01, 'ln1_b': 2.736933e-01, 'w_ff1': 3.677441e-02, 'w_ff2': 1.372456e-01, 'ln2_g': 5.780246e-01, 'ln2_b': 2.968095e-01, 'w_ple_gate': 1.304882e-02, 'w_ple': 5.491157e-02, 'ln3_g': 1.604859e+01, 'ln3_b': 3.385057e+00}


def _to_microbatches(a, axis):
    t = _jnp.moveaxis(a, axis, 0)
    t = t.reshape((N_MICROBATCH, t.shape[0] // N_MICROBATCH) + t.shape[1:])
    return _jnp.moveaxis(t, 1, axis + 1)


def setup_inputs(seed: int = 0) -> dict:
    inp = _fwd_setup_inputs(seed)
    key = _jax.random.fold_in(_jax.random.key(seed), 7919)
    shape, _ = _output_shape()
    out = dict(inp)
    out["loss_target"] = _jax.random.normal(_jax.random.fold_in(key, 0), shape, _jnp.float32)
    for i, name in enumerate(TWIN_WEIGHTS):
        w = inp[name].astype(_jnp.float32)
        if MOMENT_SCALE is None:
            s = _jnp.sqrt(_jnp.mean(_jnp.square(w)) + 1e-30)
        else:
            s = MOMENT_SCALE[name]
        km, kv = _jax.random.split(_jax.random.fold_in(key, i + 1))
        out[name] = w
        out["m_" + name] = s * _jax.random.normal(km, w.shape, _jnp.float32)
        out["v_" + name] = (s * s) * _jax.random.uniform(kv, w.shape, _jnp.float32, 0.5, 1.5)
    if N_MICROBATCH > 1:
        for name, axis in PER_EXAMPLE_BATCH_AXIS.items():
            out[name] = _to_microbatches(out[name], axis)
    return {'x': out['x'], 'p': out['p'], 'w_in': out['w_in'], 'lru_conv_w': out['lru_conv_w'], 'lru_conv_b': out['lru_conv_b'], 'lru_gate_a_w': out['lru_gate_a_w'], 'lru_gate_a_b': out['lru_gate_a_b'], 'lru_gate_x_w': out['lru_gate_x_w'], 'lru_gate_x_b': out['lru_gate_x_b'], 'lru_a_param': out['lru_a_param'], 'ssd_conv_w': out['ssd_conv_w'], 'ssd_conv_b': out['ssd_conv_b'], 'ssd_dt_bias': out['ssd_dt_bias'], 'ssd_a_log': out['ssd_a_log'], 'ssd_d': out['ssd_d'], 'ssd_norm_w': out['ssd_norm_w'], 'w_out': out['w_out'], 'ln1_g': out['ln1_g'], 'ln1_b': out['ln1_b'], 'w_ff1': out['w_ff1'], 'w_ff2': out['w_ff2'], 'ln2_g': out['ln2_g'], 'ln2_b': out['ln2_b'], 'w_ple_gate': out['w_ple_gate'], 'w_ple': out['w_ple'], 'ln3_g': out['ln3_g'], 'ln3_b': out['ln3_b'], 'loss_target': out['loss_target'], 'm_w_in': out['m_w_in'], 'm_lru_conv_w': out['m_lru_conv_w'], 'm_lru_conv_b': out['m_lru_conv_b'], 'm_lru_gate_a_w': out['m_lru_gate_a_w'], 'm_lru_gate_a_b': out['m_lru_gate_a_b'], 'm_lru_gate_x_w': out['m_lru_gate_x_w'], 'm_lru_gate_x_b': out['m_lru_gate_x_b'], 'm_lru_a_param': out['m_lru_a_param'], 'm_ssd_conv_w': out['m_ssd_conv_w'], 'm_ssd_conv_b': out['m_ssd_conv_b'], 'm_ssd_dt_bias': out['m_ssd_dt_bias'], 'm_ssd_a_log': out['m_ssd_a_log'], 'm_ssd_d': out['m_ssd_d'], 'm_ssd_norm_w': out['m_ssd_norm_w'], 'm_w_out': out['m_w_out'], 'm_ln1_g': out['m_ln1_g'], 'm_ln1_b': out['m_ln1_b'], 'm_w_ff1': out['m_w_ff1'], 'm_w_ff2': out['m_w_ff2'], 'm_ln2_g': out['m_ln2_g'], 'm_ln2_b': out['m_ln2_b'], 'm_w_ple_gate': out['m_w_ple_gate'], 'm_w_ple': out['m_w_ple'], 'm_ln3_g': out['m_ln3_g'], 'm_ln3_b': out['m_ln3_b'], 'v_w_in': out['v_w_in'], 'v_lru_conv_w': out['v_lru_conv_w'], 'v_lru_conv_b': out['v_lru_conv_b'], 'v_lru_gate_a_w': out['v_lru_gate_a_w'], 'v_lru_gate_a_b': out['v_lru_gate_a_b'], 'v_lru_gate_x_w': out['v_lru_gate_x_w'], 'v_lru_gate_x_b': out['v_lru_gate_x_b'], 'v_lru_a_param': out['v_lru_a_param'], 'v_ssd_conv_w': out['v_ssd_conv_w'], 'v_ssd_conv_b': out['v_ssd_conv_b'], 'v_ssd_dt_bias': out['v_ssd_dt_bias'], 'v_ssd_a_log': out['v_ssd_a_log'], 'v_ssd_d': out['v_ssd_d'], 'v_ssd_norm_w': out['v_ssd_norm_w'], 'v_w_out': out['v_w_out'], 'v_ln1_g': out['v_ln1_g'], 'v_ln1_b': out['v_ln1_b'], 'v_w_ff1': out['v_w_ff1'], 'v_w_ff2': out['v_w_ff2'], 'v_ln2_g': out['v_ln2_g'], 'v_ln2_b': out['v_ln2_b'], 'v_w_ple_gate': out['v_w_ple_gate'], 'v_w_ple': out['v_w_ple'], 'v_ln3_g': out['v_ln3_g'], 'v_ln3_b': out['v_ln3_b']}


def _loss(weights, diff, rest, loss_target):
    with _jax.named_scope("forward"):
        args = {**rest, TWIN_DIFF_INPUT: diff, **{k: w.astype(_WEIGHT_DTYPES[k]) for k, w in weights.items()}}
        y = _forward(args)
    with _jax.named_scope("loss_head"):
        err = _jnp.square(y.astype(_jnp.float32) - loss_target)
        return 0.5 * _jnp.sum(_jnp.mean(err, axis=-1)) if err.ndim else 0.5 * err


def _adamw(w, g, m, v):
    m = ADAM_B1 * m + (1.0 - ADAM_B1) * g
    v = ADAM_B2 * v + (1.0 - ADAM_B2) * _jnp.square(g)
    m_hat = m / (1.0 - ADAM_B1 ** ADAM_STEP)
    v_hat = v / (1.0 - ADAM_B2 ** ADAM_STEP)
    delta = -ADAM_LR * (m_hat / (_jnp.sqrt(v_hat) + ADAM_EPS) + ADAM_WD * w)
    return delta, m, v


def reference(x, p, w_in, lru_conv_w, lru_conv_b, lru_gate_a_w, lru_gate_a_b, lru_gate_x_w, lru_gate_x_b, lru_a_param, ssd_conv_w, ssd_conv_b, ssd_dt_bias, ssd_a_log, ssd_d, ssd_norm_w, w_out, ln1_g, ln1_b, w_ff1, w_ff2, ln2_g, ln2_b, w_ple_gate, w_ple, ln3_g, ln3_b, loss_target, m_w_in, m_lru_conv_w, m_lru_conv_b, m_lru_gate_a_w, m_lru_gate_a_b, m_lru_gate_x_w, m_lru_gate_x_b, m_lru_a_param, m_ssd_conv_w, m_ssd_conv_b, m_ssd_dt_bias, m_ssd_a_log, m_ssd_d, m_ssd_norm_w, m_w_out, m_ln1_g, m_ln1_b, m_w_ff1, m_w_ff2, m_ln2_g, m_ln2_b, m_w_ple_gate, m_w_ple, m_ln3_g, m_ln3_b, v_w_in, v_lru_conv_w, v_lru_conv_b, v_lru_gate_a_w, v_lru_gate_a_b, v_lru_gate_x_w, v_lru_gate_x_b, v_lru_a_param, v_ssd_conv_w, v_ssd_conv_b, v_ssd_dt_bias, v_ssd_a_log, v_ssd_d, v_ssd_norm_w, v_w_out, v_ln1_g, v_ln1_b, v_w_ff1, v_w_ff2, v_ln2_g, v_ln2_b, v_w_ple_gate, v_w_ple, v_ln3_g, v_ln3_b):
    given = dict(x=x, p=p, w_in=w_in, lru_conv_w=lru_conv_w, lru_conv_b=lru_conv_b, lru_gate_a_w=lru_gate_a_w, lru_gate_a_b=lru_gate_a_b, lru_gate_x_w=lru_gate_x_w, lru_gate_x_b=lru_gate_x_b, lru_a_param=lru_a_param, ssd_conv_w=ssd_conv_w, ssd_conv_b=ssd_conv_b, ssd_dt_bias=ssd_dt_bias, ssd_a_log=ssd_a_log, ssd_d=ssd_d, ssd_norm_w=ssd_norm_w, w_out=w_out, ln1_g=ln1_g, ln1_b=ln1_b, w_ff1=w_ff1, w_ff2=w_ff2, ln2_g=ln2_g, ln2_b=ln2_b, w_ple_gate=w_ple_gate, w_ple=w_ple, ln3_g=ln3_g, ln3_b=ln3_b, loss_target=loss_target, m_w_in=m_w_in, m_lru_conv_w=m_lru_conv_w, m_lru_conv_b=m_lru_conv_b, m_lru_gate_a_w=m_lru_gate_a_w, m_lru_gate_a_b=m_lru_gate_a_b, m_lru_gate_x_w=m_lru_gate_x_w, m_lru_gate_x_b=m_lru_gate_x_b, m_lru_a_param=m_lru_a_param, m_ssd_conv_w=m_ssd_conv_w, m_ssd_conv_b=m_ssd_conv_b, m_ssd_dt_bias=m_ssd_dt_bias, m_ssd_a_log=m_ssd_a_log, m_ssd_d=m_ssd_d, m_ssd_norm_w=m_ssd_norm_w, m_w_out=m_w_out, m_ln1_g=m_ln1_g, m_ln1_b=m_ln1_b, m_w_ff1=m_w_ff1, m_w_ff2=m_w_ff2, m_ln2_g=m_ln2_g, m_ln2_b=m_ln2_b, m_w_ple_gate=m_w_ple_gate, m_w_ple=m_w_ple, m_ln3_g=m_ln3_g, m_ln3_b=m_ln3_b, v_w_in=v_w_in, v_lru_conv_w=v_lru_conv_w, v_lru_conv_b=v_lru_conv_b, v_lru_gate_a_w=v_lru_gate_a_w, v_lru_gate_a_b=v_lru_gate_a_b, v_lru_gate_x_w=v_lru_gate_x_w, v_lru_gate_x_b=v_lru_gate_x_b, v_lru_a_param=v_lru_a_param, v_ssd_conv_w=v_ssd_conv_w, v_ssd_conv_b=v_ssd_conv_b, v_ssd_dt_bias=v_ssd_dt_bias, v_ssd_a_log=v_ssd_a_log, v_ssd_d=v_ssd_d, v_ssd_norm_w=v_ssd_norm_w, v_w_out=v_w_out, v_ln1_g=v_ln1_g, v_ln1_b=v_ln1_b, v_w_ff1=v_w_ff1, v_w_ff2=v_w_ff2, v_ln2_g=v_ln2_g, v_ln2_b=v_ln2_b, v_w_ple_gate=v_w_ple_gate, v_w_ple=v_w_ple, v_ln3_g=v_ln3_g, v_ln3_b=v_ln3_b)
    weights = {n: given[n] for n in TWIN_WEIGHTS}
    shared = {n: given[n] for n in SHARED_INPUTS}
    per_example = {n: given[n] for n in ['x', 'p']}
    grad_fn = _jax.value_and_grad(_loss, argnums=(0, 1))

    def one_microbatch(ex, loss_target):
        ex = dict(ex)
        diff = ex.pop(TWIN_DIFF_INPUT)
        return grad_fn(weights, diff, {**shared, **ex}, loss_target)

    if N_MICROBATCH == 1:
        loss, (grad_w, grad_x) = one_microbatch(per_example, given["loss_target"])
    else:
        def body(carry, xs):
            loss_sum, grad_sum = carry
            l_k, (gw_k, gx_k) = one_microbatch(xs[0], xs[1])
            with _jax.named_scope("update"):
                return (loss_sum + l_k, _jax.tree.map(_jnp.add, grad_sum, gw_k)), gx_k

        init = (_jnp.zeros((), _jnp.float32), _jax.tree.map(_jnp.zeros_like, weights))
        (loss, grad_w), grad_x = _jax.lax.scan(body, init, (per_example, given["loss_target"]))
    with _jax.named_scope("update"):
        delta_w, new_m, new_v = {}, {}, {}
        for n in TWIN_WEIGHTS:
            delta_w[n], new_m[n], new_v[n] = _adamw(weights[n], grad_w[n], given["m_" + n], given["v_" + n])
    return (loss, grad_x, *[grad_w[n] for n in TWIN_WEIGHTS], *[delta_w[n] for n in TWIN_WEIGHTS],
            *[new_m[n] for n in TWIN_WEIGHTS], *[new_v[n] for n in TWIN_WEIGHTS])
```

```python
import functools
import math

import jax
import jax.numpy as jnp
from jax import lax
from jax.experimental import pallas as pl
from jax.experimental.pallas import tpu as pltpu

F32 = jnp.float32
BF16 = jnp.bfloat16
HI = lax.Precision.HIGHEST

N_DEV = 8
D_MODEL = 1024
LRU_W = 1024
SSD_W = 1024
XBC = 2048
N_HEAD = 16
HEAD_P = 64
N_GROUP = 4
GROUP_W = 256
N_STATE = 128
CHUNK = 128
D_FF = 4096
PLE_DIM = 256
D_IN = 5136
D_IN_PAD = 5632
COL_G = 1024
COL_Z = 2048
COL_XBC = 3072
COL_DT = 5120
LRU_C = 8.0
ALPHA = 2.0 ** 0.25
LN_EPS = 1e-5
RMS_EPS = 1e-5
ADAM_LR = 0.001
ADAM_B1 = 0.9
ADAM_B2 = 0.999
ADAM_EPS = 1e-08
ADAM_WD = 0.01
ADAM_STEP = 10
GELU_C = math.sqrt(2.0 / math.pi)
LANE = 128
SUBLANE = 8
VMEM_LIMIT = 48 * 1024 * 1024
MESH_T = pl.DeviceIdType.MESH
NEG_BIG = -1e30


def _pcall(body, **kw):
    return pl.pallas_call(body, **kw)


def _cparams(sem):
    return pltpu.CompilerParams(dimension_semantics=sem, vmem_limit_bytes=VMEM_LIMIT)


def _dot(a, b):
    return jnp.dot(a.astype(BF16), b.astype(BF16), preferred_element_type=F32)


def _dot_nt(a, b):
    return lax.dot_general(a.astype(BF16), b.astype(BF16), (((1,), (1,)), ((), ())), preferred_element_type=F32)


def _dot_tn(a, b):
    return lax.dot_general(a.astype(BF16), b.astype(BF16), (((0,), (0,)), ((), ())), preferred_element_type=F32)


def _dotx(a, b):
    return jnp.dot(a, b, precision=HI, preferred_element_type=F32)


def _sigmoid(x):
    return jax.nn.sigmoid(x)


def _softplus(v):
    return jnp.maximum(v, 0.0) + jnp.log1p(jnp.exp(-jnp.abs(v)))


def _gelu(x):
    th = jnp.tanh(GELU_C * (x + 0.044715 * x * x * x))
    return 0.5 * x * (1.0 + th), th


def _gelu_grad(x, th):
    return 0.5 * (1.0 + th) + 0.5 * x * (1.0 - th * th) * GELU_C * (1.0 + 3.0 * 0.044715 * x * x)


def _iota(shape, dim):
    return lax.broadcasted_iota(jnp.int32, shape, dim)


def _mm(a, b, mode, *, tm, tn, name, a_fn=None, extra=None, epi=None, out_dtype=F32):
    m = a.shape[1] if mode == "tn" else a.shape[0]
    n = b.shape[0] if mode == "nt" else b.shape[1]
    tm, tn = min(tm, m), min(tn, n)
    if mode == "nn":
        m, k = a.shape
        _, n = b.shape
        a_spec = pl.BlockSpec((tm, k), lambda i, j: (i, 0))
        b_spec = pl.BlockSpec((k, tn), lambda i, j: (0, j))
        dims = ((1,), (0,))
    elif mode == "nt":
        m, k = a.shape
        n, _ = b.shape
        a_spec = pl.BlockSpec((tm, k), lambda i, j: (i, 0))
        b_spec = pl.BlockSpec((tn, k), lambda i, j: (j, 0))
        dims = ((1,), (1,))
    else:
        k, m = a.shape
        _, n = b.shape
        a_spec = pl.BlockSpec((k, tm), lambda i, j: (0, i))
        b_spec = pl.BlockSpec((k, tn), lambda i, j: (0, j))
        dims = ((0,), (0,))
    assert m % tm == 0 and n % tn == 0, (name, m, n, tm, tn)
    o_spec = pl.BlockSpec((tm, tn), lambda i, j: (i, j))
    in_specs = [a_spec, b_spec]
    args = [a, b]
    if extra is not None:
        in_specs.append(o_spec)
        args.append(extra)

    def body(*refs):
        a_ref, b_ref, o_ref = refs[0], refs[1], refs[-1]
        av = a_ref[...]
        if a_fn is not None:
            av = a_fn(av)
        acc = lax.dot_general(av.astype(BF16), b_ref[...].astype(BF16), (dims, ((), ())), preferred_element_type=F32)
        if epi is not None:
            acc = epi(acc, refs[2][...])
        o_ref[...] = acc.astype(out_dtype)

    return _pcall(
        body, grid=(m // tm, n // tn), in_specs=in_specs, out_specs=o_spec,
        out_shape=jax.ShapeDtypeStruct((m, n), out_dtype), name=name,
        compiler_params=_cparams(("parallel", "parallel")),
    )(*args)


def _relu2(v):
    r = jnp.maximum(v, 0.0)
    return r * r


ROW_TILE = 256


def _ln_stats(t):
    mu = jnp.mean(t, axis=-1, keepdims=True)
    xc = t - mu
    var = jnp.mean(xc * xc, axis=-1, keepdims=True)
    rstd = lax.rsqrt(var + LN_EPS)
    return xc * rstd, rstd


def _ln_bwd_rows(dy, xhat, rstd, g):
    dxh = dy * g
    m1 = jnp.mean(dxh, axis=-1, keepdims=True)
    m2 = jnp.mean(dxh * xhat, axis=-1, keepdims=True)
    return rstd * (dxh - m1 - xhat * m2)


def _ln_fwd(a, b, g, beta, *, name):
    s, d = a.shape
    row = pl.BlockSpec((ROW_TILE, d), lambda i: (i, 0))
    par = pl.BlockSpec((1, d), lambda i: (0, 0))

    def body(a_ref, b_ref, g_ref, be_ref, y_ref):
        xhat, _ = _ln_stats(ALPHA * a_ref[...] + b_ref[...])
        y_ref[...] = xhat * g_ref[...] + be_ref[...]

    return _pcall(body, grid=(s // ROW_TILE,), in_specs=[row, row, par, par], out_specs=row,
                  out_shape=jax.ShapeDtypeStruct((s, d), F32), name=name,
                  compiler_params=_cparams(("parallel",)))(a, b, g, beta)


def _ln_bwd(a, b, g, dys, coefs, *, name):
    s, d = a.shape
    row = pl.BlockSpec((ROW_TILE, d), lambda i: (i, 0))
    par = pl.BlockSpec((1, d), lambda i: (0, 0))
    n = len(dys)

    def body(*refs):
        a_ref, b_ref, g_ref = refs[:3]
        dy_refs = refs[3:3 + n]
        dt_ref, dg_ref, db_ref = refs[3 + n:]
        xhat, rstd = _ln_stats(ALPHA * a_ref[...] + b_ref[...])
        dy = coefs[0] * dy_refs[0][...]
        for q in range(1, n):
            dy = dy + coefs[q] * dy_refs[q][...]
        dt_ref[...] = _ln_bwd_rows(dy, xhat, rstd, g_ref[...])

        @pl.when(pl.program_id(0) == 0)
        def _():
            dg_ref[...] = jnp.zeros_like(dg_ref)
            db_ref[...] = jnp.zeros_like(db_ref)

        dg_ref[...] += jnp.sum(dy * xhat, axis=0, keepdims=True)
        db_ref[...] += jnp.sum(dy, axis=0, keepdims=True)

    return _pcall(body, grid=(s // ROW_TILE,), in_specs=[row, row, par] + [row] * n, out_specs=(row, par, par),
                  out_shape=(jax.ShapeDtypeStruct((s, d), F32), jax.ShapeDtypeStruct((1, d), F32),
                             jax.ShapeDtypeStruct((1, d), F32)),
                  name=name, compiler_params=_cparams(("arbitrary",)))(a, b, g, *dys)


def _head(x2, gpre, ple, g, beta, tgt, *, name):
    s, d = x2.shape
    row = pl.BlockSpec((ROW_TILE, d), lambda i: (i, 0))
    par = pl.BlockSpec((1, d), lambda i: (0, 0))
    lsp = pl.BlockSpec((1, LANE), lambda i: (0, 0))

    def body(x2_ref, gp_ref, ple_ref, g_ref, be_ref, t_ref, loss_ref, dgp_ref, dple_ref, dt_ref, dg_ref, db_ref):
        gate = _sigmoid(gp_ref[...])
        ple_v = ple_ref[...]
        xhat, rstd = _ln_stats(ALPHA * x2_ref[...] + gate * ple_v)
        err = xhat * g_ref[...] + be_ref[...] - t_ref[...]
        dy = err * (1.0 / d)
        dt = _ln_bwd_rows(dy, xhat, rstd, g_ref[...])
        dt_ref[...] = dt
        dgp_ref[...] = dt * ple_v * gate * (1.0 - gate)
        dple_ref[...] = dt * gate

        @pl.when(pl.program_id(0) == 0)
        def _():
            loss_ref[...] = jnp.zeros_like(loss_ref)
            dg_ref[...] = jnp.zeros_like(dg_ref)
            db_ref[...] = jnp.zeros_like(db_ref)

        loss_ref[...] += 0.5 * jnp.sum(jnp.mean(err * err, axis=-1, keepdims=True))
        dg_ref[...] += jnp.sum(dy * xhat, axis=0, keepdims=True)
        db_ref[...] += jnp.sum(dy, axis=0, keepdims=True)

    sd = jax.ShapeDtypeStruct((s, d), F32)
    pd = jax.ShapeDtypeStruct((1, d), F32)
    return _pcall(body, grid=(s // ROW_TILE,), in_specs=[row, row, row, par, par, row],
                  out_specs=(lsp, row, row, row, par, par),
                  out_shape=(jax.ShapeDtypeStruct((1, LANE), F32), sd, sd, sd, pd, pd),
                  name=name, compiler_params=_cparams(("arbitrary",)))(x2, gpre, ple, g, beta, tgt)


CONV_R = 256
PAD = SUBLANE


def _shift_down(ext, s):
    if s == 0:
        return ext[PAD:, :]
    return pltpu.roll(ext, s, 0)[PAD:, :]


def _shift_up(ext, s):
    r = ext.shape[0] - PAD
    if s == 0:
        return ext[:r, :]
    return pltpu.roll(ext, r + PAD - s, 0)[:r, :]


def _conv_rows(xpad_ref, r0, w_ref):
    ext = xpad_ref[pl.ds(r0, CONV_R + PAD), :]
    acc = _shift_down(ext, 0) * w_ref[3:4, :]
    for k in range(3):
        acc = acc + _shift_down(ext, 3 - k) * w_ref[k:k + 1, :]
    return acc, ext


def _fill_front_padded(dst_ref, src_ref, s):
    dst_ref[0:PAD, :] = jnp.zeros((PAD, dst_ref.shape[1]), F32)

    def cp(q, _):
        r0 = pl.multiple_of(q * CONV_R, CONV_R)
        dst_ref[pl.ds(pl.multiple_of(PAD + r0, PAD), CONV_R), :] = src_ref[pl.ds(r0, CONV_R), :]
        return 0

    lax.fori_loop(0, s // CONV_R, cp, 0)


def _conv_silu_fwd(proj, w8, b, *, col0, width, ct, name):
    s = proj.shape[0]
    nb = col0 // ct

    def body(x_ref, w_ref, b_ref, o_ref, xpad):
        _fill_front_padded(xpad, x_ref, s)

        def step(q, _):
            r0 = pl.multiple_of(q * CONV_R, CONV_R)
            acc, _e = _conv_rows(xpad, r0, w_ref)
            pre = acc + b_ref[...]
            o_ref[pl.ds(r0, CONV_R), :] = pre * _sigmoid(pre)
            return 0

        lax.fori_loop(0, s // CONV_R, step, 0)

    return _pcall(
        body, grid=(width // ct,),
        in_specs=[pl.BlockSpec((s, ct), lambda j: (0, nb + j)), pl.BlockSpec((SUBLANE, ct), lambda j: (0, j)),
                  pl.BlockSpec((1, ct), lambda j: (0, j))],
        out_specs=pl.BlockSpec((s, ct), lambda j: (0, j)),
        out_shape=jax.ShapeDtypeStruct((s, width), F32),
        scratch_shapes=[pltpu.VMEM((s + PAD, ct), F32)], name=name,
        compiler_params=_cparams(("parallel",)))(proj, w8, b)


def _conv_bwd_rows(dpad_ref, r0, w_ref):
    ext = dpad_ref[pl.ds(r0, CONV_R + PAD), :]
    acc = _shift_up(ext, 0) * w_ref[3:4, :]
    for k in range(3):
        acc = acc + _shift_up(ext, 3 - k) * w_ref[k:k + 1, :]
    return acc


def _conv_silu_bwd(proj, dact, w8, b, *, col0, width, ct, name):
    s = proj.shape[0]
    nb = col0 // ct

    def body(x_ref, d_ref, w_ref, b_ref, dx_ref, dwb_ref, xpad, dpad):
        _fill_front_padded(xpad, x_ref, s)
        dpad[pl.ds(s, PAD), :] = jnp.zeros((PAD, ct), F32)
        dwb_ref[...] = jnp.zeros_like(dwb_ref)

        def step(q, _):
            r0 = pl.multiple_of(q * CONV_R, CONV_R)
            acc, ext = _conv_rows(xpad, r0, w_ref)
            pre = acc + b_ref[...]
            sg = _sigmoid(pre)
            dpre = d_ref[pl.ds(r0, CONV_R), :] * sg * (1.0 + pre * (1.0 - sg))
            dpad[pl.ds(r0, CONV_R), :] = dpre
            for k in range(4):
                dwb_ref[k:k + 1, :] += jnp.sum(dpre * _shift_down(ext, 3 - k), axis=0, keepdims=True)
            dwb_ref[4:5, :] += jnp.sum(dpre, axis=0, keepdims=True)
            return 0

        lax.fori_loop(0, s // CONV_R, step, 0)

        def step2(q, _):
            r0 = pl.multiple_of(q * CONV_R, CONV_R)
            dx_ref[pl.ds(r0, CONV_R), :] = _conv_bwd_rows(dpad, r0, w_ref)
            return 0

        lax.fori_loop(0, s // CONV_R, step2, 0)

    colb = pl.BlockSpec((s, ct), lambda j: (0, j))
    return _pcall(
        body, grid=(width // ct,),
        in_specs=[pl.BlockSpec((s, ct), lambda j: (0, nb + j)), colb, pl.BlockSpec((SUBLANE, ct), lambda j: (0, j)),
                  pl.BlockSpec((1, ct), lambda j: (0, j))],
        out_specs=(colb, pl.BlockSpec((SUBLANE, ct), lambda j: (0, j))),
        out_shape=(jax.ShapeDtypeStruct((s, width), F32), jax.ShapeDtypeStruct((SUBLANE, width), F32)),
        scratch_shapes=[pltpu.VMEM((s + PAD, ct), F32), pltpu.VMEM((s + PAD, ct), F32)], name=name,
        compiler_params=_cparams(("parallel",)))(proj, dact, w8, b)


LRU_CT = 128


def _lru_chunk(xpad, r0, cw_ref, cb, wa, ba, wx, bx, sp):
    acc, ext = _conv_rows(xpad, r0, cw_ref)
    xl = acc + cb
    r = _sigmoid(_dot(xl, wa) + ba)
    i = _sigmoid(_dot(xl, wx) + bx)
    la = -LRU_C * r * sp
    a = jnp.exp(la)
    a2 = jnp.exp(2.0 * la)
    mult = jnp.sqrt(-jnp.tanh(la) * (a2 + 1.0))
    first = (r0 + _iota((CONV_R, 1), 0)) == 0
    mult = jnp.where(first, 1.0, mult)
    return ext, xl, r, i, a, a2, mult, first


def _lru_specs(s):
    ct = LRU_CT
    nb_g = COL_G // ct
    return dict(
        x=pl.BlockSpec((s, ct), lambda j: (0, j)),
        g=pl.BlockSpec((s, ct), lambda j: (0, nb_g + j)),
        col=pl.BlockSpec((s, ct), lambda j: (0, j)),
        cw=pl.BlockSpec((SUBLANE, ct), lambda j: (0, j)),
        vec=pl.BlockSpec((1, ct), lambda j: (0, j)),
        gate=pl.BlockSpec((None, ct, ct), lambda j: (j, 0, 0)),
    )


def _lru_fwd(proj, cw8, cb, wa_bd, ba, wx_bd, bx, ap, *, name):
    s = proj.shape[0]
    ct = LRU_CT
    sp_ = _lru_specs(s)

    def body(x_ref, g_ref, cw_ref, cb_ref, wa_ref, ba_ref, wx_ref, bx_ref, ap_ref, y_ref, xpad, a_s, u_s):
        _fill_front_padded(xpad, x_ref, s)
        sp = _softplus(-ap_ref[...])

        def step(q, _):
            r0 = pl.multiple_of(q * CONV_R, CONV_R)
            _e, xl, _r, i, a, _a2, mult, _f = _lru_chunk(xpad, r0, cw_ref, cb_ref[...], wa_ref[...], ba_ref[...],
                                                       wx_ref[...], bx_ref[...], sp)
            a_s[pl.ds(r0, CONV_R), :] = a
            u_s[pl.ds(r0, CONV_R), :] = xl * i * mult
            return 0

        lax.fori_loop(0, s // CONV_R, step, 0)

        def scan(t, h):
            h = a_s[pl.ds(t, 1), :] * h + u_s[pl.ds(t, 1), :]
            u_s[pl.ds(t, 1), :] = h
            return h

        lax.fori_loop(0, s, scan, jnp.zeros((1, ct), F32), unroll=8)

        def gate(q, _):
            r0 = pl.multiple_of(q * CONV_R, CONV_R)
            ge, _th = _gelu(g_ref[pl.ds(r0, CONV_R), :])
            y_ref[pl.ds(r0, CONV_R), :] = ge * u_s[pl.ds(r0, CONV_R), :]
            return 0

        lax.fori_loop(0, s // CONV_R, gate, 0)

    return _pcall(
        body, grid=(LRU_W // ct,),
        in_specs=[sp_["x"], sp_["g"], sp_["cw"], sp_["vec"], sp_["gate"], sp_["vec"], sp_["gate"], sp_["vec"], sp_["vec"]],
        out_specs=sp_["col"], out_shape=jax.ShapeDtypeStruct((s, LRU_W), F32),
        scratch_shapes=[pltpu.VMEM((s + PAD, ct), F32), pltpu.VMEM((s, ct), F32), pltpu.VMEM((s, ct), F32)],
        name=name, compiler_params=_cparams(("parallel",)))(proj, proj, cw8, cb, wa_bd, ba, wx_bd, bx, ap)


def _lru_bwd(proj, dy, cw8, cb, wa_bd, ba, wx_bd, bx, ap, *, name):
    s = proj.shape[0]
    ct = LRU_CT
    sp_ = _lru_specs(s)

    def body(x_ref, g_ref, dy_ref, cw_ref, cb_ref, wa_ref, ba_ref, wx_ref, bx_ref, ap_ref,
             dx_ref, dg_ref, dcwb_ref, dwa_ref, dwx_ref, xpad, a_s, hpad, gr_s, dpad):
        _fill_front_padded(xpad, x_ref, s)
        apv = ap_ref[...]
        sp = _softplus(-apv)
        cb_v, wa, ba_v, wx, bx_v = cb_ref[...], wa_ref[...], ba_ref[...], wx_ref[...], bx_ref[...]
        hpad[0:PAD, :] = jnp.zeros((PAD, ct), F32)
        dpad[pl.ds(s, PAD), :] = jnp.zeros((PAD, ct), F32)
        dcwb_ref[...] = jnp.zeros_like(dcwb_ref)
        dwa_ref[...] = jnp.zeros_like(dwa_ref)
        dwx_ref[...] = jnp.zeros_like(dwx_ref)

        def step(q, _):
            r0 = pl.multiple_of(q * CONV_R, CONV_R)
            _e, xl, _r, i, a, _a2, mult, _f = _lru_chunk(xpad, r0, cw_ref, cb_v, wa, ba_v, wx, bx_v, sp)
            a_s[pl.ds(r0, CONV_R), :] = a
            hpad[pl.ds(pl.multiple_of(PAD + r0, PAD), CONV_R), :] = xl * i * mult
            return 0

        lax.fori_loop(0, s // CONV_R, step, 0)

        def scan(t, h):
            h = a_s[pl.ds(t, 1), :] * h + hpad[pl.ds(PAD + t, 1), :]
            hpad[pl.ds(PAD + t, 1), :] = h
            return h

        lax.fori_loop(0, s, scan, jnp.zeros((1, ct), F32), unroll=8)

        def gate(q, _):
            r0 = pl.multiple_of(q * CONV_R, CONV_R)
            gv = g_ref[pl.ds(r0, CONV_R), :]
            dyv = dy_ref[pl.ds(r0, CONV_R), :]
            ge, th = _gelu(gv)
            dg_ref[pl.ds(r0, CONV_R), :] = dyv * hpad[pl.ds(pl.multiple_of(PAD + r0, PAD), CONV_R), :] * _gelu_grad(gv, th)
            gr_s[pl.ds(r0, CONV_R), :] = dyv * ge
            return 0

        lax.fori_loop(0, s // CONV_R, gate, 0)

        def rscan(k, carry):
            t = s - 1 - k
            gval = gr_s[pl.ds(t, 1), :] + carry
            gr_s[pl.ds(t, 1), :] = gval
            return a_s[pl.ds(t, 1), :] * gval

        lax.fori_loop(0, s, rscan, jnp.zeros((1, ct), F32), unroll=8)

        def back(q, _):
            r0 = pl.multiple_of(q * CONV_R, CONV_R)
            ext, xl, r, i, a, a2, mult, first = _lru_chunk(xpad, r0, cw_ref, cb_v, wa, ba_v, wx, bx_v, sp)
            gval = gr_s[pl.ds(r0, CONV_R), :]
            hprev = _shift_down(hpad[pl.ds(r0, CONV_R + PAD), :], 1)
            da = gval * hprev
            dxl = gval * i * mult
            di = gval * xl * mult
            dmult = jnp.where(first, 0.0, gval * xl * i)
            dla = da * a - dmult * a2 / mult
            dr = dla * (-LRU_C) * sp
            dcwb_ref[7:8, :] += jnp.sum(dla * (-LRU_C) * r, axis=0, keepdims=True)
            dpr = dr * r * (1.0 - r)
            dpi = di * i * (1.0 - i)
            dxl = dxl + _dot_nt(dpr, wa) + _dot_nt(dpi, wx)
            dwa_ref[...] += _dot_tn(xl, dpr)
            dwx_ref[...] += _dot_tn(xl, dpi)
            dcwb_ref[5:6, :] += jnp.sum(dpr, axis=0, keepdims=True)
            dcwb_ref[6:7, :] += jnp.sum(dpi, axis=0, keepdims=True)
            dpad[pl.ds(r0, CONV_R), :] = dxl
            for k in range(4):
                dcwb_ref[k:k + 1, :] += jnp.sum(dxl * _shift_down(ext, 3 - k), axis=0, keepdims=True)
            dcwb_ref[4:5, :] += jnp.sum(dxl, axis=0, keepdims=True)
            return 0

        lax.fori_loop(0, s // CONV_R, back, 0)
        dcwb_ref[7:8, :] = dcwb_ref[7:8, :] * (-_sigmoid(-apv))

        def step2(q, _):
            r0 = pl.multiple_of(q * CONV_R, CONV_R)
            dx_ref[pl.ds(r0, CONV_R), :] = _conv_bwd_rows(dpad, r0, cw_ref)
            return 0

        lax.fori_loop(0, s // CONV_R, step2, 0)

    nt = LRU_W // ct
    return _pcall(
        body, grid=(nt,),
        in_specs=[sp_["x"], sp_["g"], sp_["col"], sp_["cw"], sp_["vec"], sp_["gate"], sp_["vec"], sp_["gate"], sp_["vec"],
                  sp_["vec"]],
        out_specs=(sp_["col"], sp_["col"], sp_["cw"], sp_["gate"], sp_["gate"]),
        out_shape=(jax.ShapeDtypeStruct((s, LRU_W), F32), jax.ShapeDtypeStruct((s, LRU_W), F32),
                   jax.ShapeDtypeStruct((SUBLANE, LRU_W), F32), jax.ShapeDtypeStruct((nt, ct, ct), F32),
                   jax.ShapeDtypeStruct((nt, ct, ct), F32)),
        scratch_shapes=[pltpu.VMEM((s + PAD, ct), F32), pltpu.VMEM((s, ct), F32), pltpu.VMEM((s + PAD, ct), F32),
                        pltpu.VMEM((s, ct), F32), pltpu.VMEM((s + PAD, ct), F32)],
        name=name, compiler_params=_cparams(("parallel",)))(proj, proj, dy, cw8, cb, wa_bd, ba, wx_bd, bx, ap)


def _ssd_prep(dtr, bias, alog_pad, alogx):
    l = CHUNK
    lane = _iota((1, LANE), 1)
    a_head = jnp.where(lane < N_HEAD, -jnp.exp(alog_pad), 0.0)
    dt = _softplus(dtr + bias)
    tril = (_iota((l, l), 1) <= _iota((l, l), 0)).astype(F32)
    cs = _dotx(tril, dt * a_head)
    expand = (jnp.right_shift(_iota((LANE, SSD_W), 1), 6) == _iota((LANE, SSD_W), 0)).astype(F32)
    dtx = _dotx(dt, expand)
    ax = -jnp.exp(alogx)
    csx = _dotx(tril, dtx * ax)
    totx = jnp.sum(dtx * ax, axis=0, keepdims=True)
    return dict(a_head=a_head, dt=dt, tril=tril, cs=cs, expand=expand, dtx=dtx, ax=ax, csx=csx, totx=totx)


def _decay_mat(cs, cst_ref, h, causal):
    lane = _iota((CHUNK, LANE), 1)
    col = jnp.sum(jnp.where(lane == h, cs, 0.0), axis=1, keepdims=True)
    row = cst_ref[h:h + 1, :]
    return jnp.exp(jnp.where(causal, col - row, NEG_BIG))


def _head_mask(j):
    lane = _iota((CHUNK, GROUP_W), 1)
    return (lane >= j * HEAD_P) & (lane < (j + 1) * HEAD_P)


def _ssd_group_fwd(q, g, xs_g, bg, cg, ht_g, cst_ref, causal, dx_g):
    sl = slice(g * GROUP_W, (g + 1) * GROUP_W)
    dtx_g, csx_g, totx_g = q["dtx"][:, sl], q["csx"][:, sl], q["totx"][:, sl]
    xdt = xs_g * dtx_g
    ex = jnp.exp(csx_g)
    cb = _dot_nt(cg, bg)
    yoff = _dot(cg, ht_g) * ex
    ydiag = jnp.zeros((CHUNK, GROUP_W), F32)
    for j in range(4):
        sc = cb * _decay_mat(q["cs"], cst_ref, 4 * g + j, causal)
        ydiag = jnp.where(_head_mask(j), _dot(sc, xdt), ydiag)
    y = ydiag + yoff + xs_g * dx_g
    dsx = jnp.exp(totx_g - csx_g)
    return y, dict(xdt=xdt, ex=ex, cb=cb, yoff=yoff, dsx=dsx, dtx=dtx_g, totx=totx_g)


def _gated_norm_fwd(y_g, z_g, w_g):
    sz = _sigmoid(z_g)
    silu = z_g * sz
    yf = y_g * silu
    rs = lax.rsqrt(jnp.mean(yf * yf, axis=1, keepdims=True) + RMS_EPS)
    yn = yf * rs
    return yn * w_g, (sz, silu, rs, yn)


def _ssd_fwd(xact, proj, bias_pad, alog_pad, alogx, dxp, normw, *, name):
    s = xact.shape[0]
    nc = s // CHUNK

    def body(xa_ref, dt_ref, z_ref, bias_ref, alp_ref, alx_ref, dx_ref, nw_ref, y_ref, hp_ref, ht, cst):
        @pl.when(pl.program_id(0) == 0)
        def _():
            ht[...] = jnp.zeros_like(ht)

        hp_ref[...] = ht[...]
        q = _ssd_prep(dt_ref[...], bias_ref[...], alp_ref[...], alx_ref[...])
        cst[...] = q["cs"].T
        causal = q["tril"] > 0.0
        for g in range(N_GROUP):
            sl = slice(g * GROUP_W, (g + 1) * GROUP_W)
            xs_g = xa_ref[:, sl]
            bg = xa_ref[:, SSD_W + g * N_STATE:SSD_W + (g + 1) * N_STATE]
            cg = xa_ref[:, SSD_W + N_GROUP * N_STATE + g * N_STATE:SSD_W + N_GROUP * N_STATE + (g + 1) * N_STATE]
            ht_g = ht[:, sl]
            y, f = _ssd_group_fwd(q, g, xs_g, bg, cg, ht_g, cst, causal, dx_ref[:, sl])
            out, _ = _gated_norm_fwd(y, z_ref[:, sl], nw_ref[:, sl])
            y_ref[:, sl] = out
            ht[:, sl] = jnp.exp(f["totx"]) * ht_g + _dot_tn(bg, f["xdt"] * f["dsx"])

    par = lambda w: pl.BlockSpec((1, w), lambda c: (0, 0))
    return _pcall(
        body, grid=(nc,),
        in_specs=[pl.BlockSpec((CHUNK, XBC), lambda c: (c, 0)),
                  pl.BlockSpec((CHUNK, LANE), lambda c: (c, COL_DT // LANE)),
                  pl.BlockSpec((CHUNK, SSD_W), lambda c: (c, COL_Z // SSD_W)),
                  par(LANE), par(LANE), par(SSD_W), par(SSD_W), par(SSD_W)],
        out_specs=(pl.BlockSpec((CHUNK, SSD_W), lambda c: (c, 0)),
                   pl.BlockSpec((None, N_STATE, SSD_W), lambda c: (c, 0, 0))),
        out_shape=(jax.ShapeDtypeStruct((s, SSD_W), F32), jax.ShapeDtypeStruct((nc, N_STATE, SSD_W), F32)),
        scratch_shapes=[pltpu.VMEM((N_STATE, SSD_W), F32), pltpu.VMEM((CHUNK, LANE), F32)],
        name=name, compiler_params=_cparams(("arbitrary",)))(xact, proj, proj, bias_pad, alog_pad, alogx, dxp, normw)


def _ssd_bwd(xact, proj, dycat, hprev, bias_pad, alog_pad, alogx, dxp, normw, *, name):
    s = xact.shape[0]
    nc = s // CHUNK
    l = CHUNK

    def body(xa_ref, dt_ref, z_ref, dy_ref, hp_ref, bias_ref, alp_ref, alx_ref, dx_ref, nw_ref,
             dxa_ref, ddt_ref, dz_ref, dnw_ref, small_ref, dht, cst, accx, dcsx_s, ddtx_s):
        step = pl.program_id(0)

        @pl.when(step == 0)
        def _():
            dht[...] = jnp.zeros_like(dht)
            accx[...] = jnp.zeros_like(accx)
            dnw_ref[...] = jnp.zeros_like(dnw_ref)
            small_ref[...] = jnp.zeros_like(small_ref)

        dtr = dt_ref[...]
        q = _ssd_prep(dtr, bias_ref[...], alp_ref[...], alx_ref[...])
        cst[...] = q["cs"].T
        causal = q["tril"] > 0.0
        eye = _iota((l, l), 0) == _iota((l, l), 1)
        lane = _iota((l, LANE), 1)
        dcs_head = jnp.zeros((l, LANE), F32)
        for g in range(N_GROUP):
            sl = slice(g * GROUP_W, (g + 1) * GROUP_W)
            slb = slice(SSD_W + g * N_STATE, SSD_W + (g + 1) * N_STATE)
            slc = slice(SSD_W + N_GROUP * N_STATE + g * N_STATE, SSD_W + N_GROUP * N_STATE + (g + 1) * N_STATE)
            xs_g, bg, cg = xa_ref[:, sl], xa_ref[:, slb], xa_ref[:, slc]
            ht_g = hp_ref[:, sl]
            dxp_g = dx_ref[:, sl]
            y, f = _ssd_group_fwd(q, g, xs_g, bg, cg, ht_g, cst, causal, dxp_g)
            z_g, nw_g = z_ref[:, sl], nw_ref[:, sl]
            _o, (sz, silu, rs, yn) = _gated_norm_fwd(y, z_g, nw_g)
            dout = dy_ref[:, sl]
            dnw_ref[:, sl] += jnp.sum(dout * yn, axis=0, keepdims=True)
            dyn = dout * nw_g
            dyf = rs * (dyn - yn * jnp.mean(dyn * yn, axis=1, keepdims=True))
            dy = dyf * silu
            dz_ref[:, sl] = dyf * y * sz * (1.0 + z_g * (1.0 - sz))
            accx[0:1, sl] += jnp.sum(dy * xs_g, axis=0, keepdims=True)
            dyo = dy * f["ex"]
            dcg = _dot_nt(dyo, ht_g)
            dht_prev = _dot_tn(cg, dyo)
            dcsx = dy * f["yoff"]
            xdt = f["xdt"]
            dxdt = jnp.zeros((l, GROUP_W), F32)
            dcb = jnp.zeros((l, l), F32)
            for j in range(4):
                h = 4 * g + j
                lm = _decay_mat(q["cs"], cst, h, causal)
                sc = f["cb"] * lm
                mask = _head_mask(j)
                ds_ = jnp.where(causal, _dot_nt(jnp.where(mask, dy, 0.0), xdt), 0.0)
                dxdt = jnp.where(mask, _dot_tn(sc, dy), dxdt)
                dcb = dcb + ds_ * lm
                m = ds_ * sc
                rsum = jnp.sum(m, axis=1, keepdims=True)
                csum = jnp.sum(m, axis=0, keepdims=True)
                csum_col = jnp.sum(jnp.where(eye, csum, 0.0), axis=1, keepdims=True)
                dcs_head = dcs_head + jnp.where(lane == h, rsum - csum_col, 0.0)
            dhn = dht[:, sl]
            etot = jnp.exp(f["totx"])
            dxd = _dot(bg, dhn)
            dbg = _dot_nt(xdt * f["dsx"], dhn)
            dxdt = dxdt + dxd * f["dsx"]
            qq = dxd * xdt * f["dsx"]
            dcsx = dcsx - qq
            dtot = jnp.sum(qq, axis=0, keepdims=True) + jnp.sum(dhn * ht_g, axis=0, keepdims=True) * etot
            dht[:, sl] = etot * dhn + dht_prev
            dcg = dcg + _dot(dcb, bg)
            dbg = dbg + _dot_tn(dcb, cg)
            dxa_ref[:, sl] = dxdt * f["dtx"] + dy * dxp_g
            dxa_ref[:, slb] = dbg
            dxa_ref[:, slc] = dcg
            dcsx_s[:, sl] = dcsx
            ddtx_s[:, sl] = dxdt * xs_g
            accx[2:3, sl] = dtot
        triu = (_iota((l, l), 1) >= _iota((l, l), 0)).astype(F32)
        dax = _dotx(triu, dcsx_s[...]) + accx[2:3, :]
        accx[1:2, :] += jnp.sum(dax * q["dtx"], axis=0, keepdims=True)
        reduce = (jnp.right_shift(_iota((SSD_W, LANE), 0), 6) == _iota((SSD_W, LANE), 1)).astype(F32)
        ddt = _dotx(ddtx_s[...] + dax * q["ax"], reduce)
        da_head = _dotx(triu, dcs_head)
        ddt = ddt + da_head * q["a_head"]
        small_ref[1:2, :] += jnp.sum(da_head * q["dt"], axis=0, keepdims=True)
        ddtr = ddt * _sigmoid(dtr + bias_ref[...])
        ddt_ref[...] = ddtr
        small_ref[0:1, :] += jnp.sum(ddtr, axis=0, keepdims=True)

        @pl.when(step == nc - 1)
        def _():
            red = _dotx(accx[...], reduce)
            d_a = small_ref[1:2, :] + red[1:2, :]
            small_ref[1:2, :] = d_a * q["a_head"]
            small_ref[2:3, :] = red[0:1, :]

    rev = lambda c: nc - 1 - c
    par = lambda w: pl.BlockSpec((1, w), lambda c: (0, 0))
    return _pcall(
        body, grid=(nc,),
        in_specs=[pl.BlockSpec((CHUNK, XBC), lambda c: (rev(c), 0)),
                  pl.BlockSpec((CHUNK, LANE), lambda c: (rev(c), COL_DT // LANE)),
                  pl.BlockSpec((CHUNK, SSD_W), lambda c: (rev(c), COL_Z // SSD_W)),
                  pl.BlockSpec((CHUNK, SSD_W), lambda c: (rev(c), 1)),
                  pl.BlockSpec((None, N_STATE, SSD_W), lambda c: (rev(c), 0, 0)),
                  par(LANE), par(LANE), par(SSD_W), par(SSD_W), par(SSD_W)],
        out_specs=(pl.BlockSpec((CHUNK, XBC), lambda c: (rev(c), 0)),
                   pl.BlockSpec((CHUNK, LANE), lambda c: (rev(c), 0)),
                   pl.BlockSpec((CHUNK, SSD_W), lambda c: (rev(c), 0)),
                   par(SSD_W), pl.BlockSpec((SUBLANE, LANE), lambda c: (0, 0))),
        out_shape=(jax.ShapeDtypeStruct((s, XBC), F32), jax.ShapeDtypeStruct((s, LANE), F32),
                   jax.ShapeDtypeStruct((s, SSD_W), F32), jax.ShapeDtypeStruct((1, SSD_W), F32),
                   jax.ShapeDtypeStruct((SUBLANE, LANE), F32)),
        scratch_shapes=[pltpu.VMEM((N_STATE, SSD_W), F32), pltpu.VMEM((CHUNK, LANE), F32),
                        pltpu.VMEM((SUBLANE, SSD_W), F32), pltpu.VMEM((CHUNK, SSD_W), F32),
                        pltpu.VMEM((CHUNK, SSD_W), F32)],
        name=name, compiler_params=_cparams(("arbitrary",)))(
            xact, proj, proj, dycat, hprev, bias_pad, alog_pad, alogx, dxp, normw)


def _blockdiag(w):
    w2 = w.reshape(N_HEAD // 2, 2, HEAD_P, HEAD_P)
    z = jnp.zeros((N_HEAD // 2, HEAD_P, HEAD_P), w.dtype)
    top = jnp.concatenate([w2[:, 0], z], axis=2)
    bot = jnp.concatenate([z, w2[:, 1]], axis=2)
    return jnp.concatenate([top, bot], axis=1)


def _unblockdiag(wbd):
    a = wbd[:, :HEAD_P, :HEAD_P]
    b = wbd[:, HEAD_P:, HEAD_P:]
    return jnp.stack([a, b], axis=1).reshape(N_HEAD, HEAD_P, HEAD_P)


def _pad_rows8(w):
    return jnp.concatenate([w, jnp.zeros((SUBLANE - w.shape[0], w.shape[1]), w.dtype)], axis=0)


def _pad_lane(v):
    return jnp.concatenate([v, jnp.zeros((1, LANE - v.shape[1]), v.dtype)], axis=1)


def _local_step(x, p, tgt, w):
    cw_l = _pad_rows8(w["lru_conv_w"])
    cw_s = _pad_rows8(w["ssd_conv_w"])
    wa_bd = _blockdiag(w["lru_gate_a_w"])
    wx_bd = _blockdiag(w["lru_gate_x_w"])
    ba = w["lru_gate_a_b"].reshape(1, LRU_W)
    bx = w["lru_gate_x_b"].reshape(1, LRU_W)
    bias_pad = _pad_lane(w["ssd_dt_bias"])
    alog_pad = _pad_lane(w["ssd_a_log"])
    alogx = jnp.repeat(w["ssd_a_log"], HEAD_P, axis=1)
    dxp = jnp.repeat(w["ssd_d"], HEAD_P, axis=1)

    proj = _mm(x, w["w_in"], "nn", tm=512, tn=512, name="in_proj")
    y_lru = _lru_fwd(proj, cw_l, w["lru_conv_b"], wa_bd, ba, wx_bd, bx, w["lru_a_param"], name="lru_fwd")
    xact = _conv_silu_fwd(proj, cw_s, w["ssd_conv_b"], col0=COL_XBC, width=XBC, ct=256, name="ssd_conv_fwd")
    y_ssd, hprev = _ssd_fwd(xact, proj, bias_pad, alog_pad, alogx, dxp, w["ssd_norm_w"], name="ssd_fwd")
    ycat = jnp.concatenate([y_lru, y_ssd], axis=1)
    mix = _mm(ycat, w["w_out"], "nn", tm=512, tn=512, name="out_proj")
    x1 = _ln_fwd(x, mix, w["ln1_g"], w["ln1_b"], name="ln1_fwd")
    pre = _mm(x1, w["w_ff1"], "nn", tm=512, tn=512, name="ff1")
    ff = _mm(pre, w["w_ff2"], "nn", tm=512, tn=512, a_fn=_relu2, name="ff2")
    x2 = _ln_fwd(x1, ff, w["ln2_g"], w["ln2_b"], name="ln2_fwd")
    gpre = _mm(x2, w["w_ple_gate"], "nn", tm=512, tn=512, name="ple_gate")
    ple = _mm(p, w["w_ple"], "nn", tm=512, tn=512, name="ple_proj")
    loss, dgpre, dple, dt3, dg3, db3 = _head(x2, gpre, ple, w["ln3_g"], w["ln3_b"], tgt, name="head")

    g = {}
    g["ln3_g"], g["ln3_b"] = dg3, db3
    g["w_ple_gate"] = _mm(x2, dgpre, "tn", tm=512, tn=512, name="d_w_ple_gate")
    g["w_ple"] = _mm(p, dple, "tn", tm=256, tn=512, name="d_w_ple")
    dx2_mm = _mm(dgpre, w["w_ple_gate"], "nt", tm=512, tn=512, name="d_x2")
    dt2, g["ln2_g"], g["ln2_b"] = _ln_bwd(x1, ff, w["ln2_g"], [dt3, dx2_mm], [ALPHA, 1.0], name="ln2_bwd")
    g["w_ff2"] = _mm(pre, dt2, "tn", tm=512, tn=512, a_fn=_relu2, name="d_w_ff2")
    dpre = _mm(dt2, w["w_ff2"], "nt", tm=512, tn=512, extra=pre,
               epi=lambda acc, pv: acc * 2.0 * jnp.maximum(pv, 0.0), name="d_pre")
    g["w_ff1"] = _mm(x1, dpre, "tn", tm=512, tn=512, name="d_w_ff1")
    dx1_mm = _mm(dpre, w["w_ff1"], "nt", tm=512, tn=512, name="d_x1")
    dt1, g["ln1_g"], g["ln1_b"] = _ln_bwd(x, mix, w["ln1_g"], [dt2, dx1_mm], [ALPHA, 1.0], name="ln1_bwd")
    g["w_out"] = _mm(ycat, dt1, "tn", tm=512, tn=512, name="d_w_out")
    dycat = _mm(dt1, w["w_out"], "nt", tm=512, tn=512, name="d_ycat")
    dxl, dgl, dcwb_l, dwa, dwx = _lru_bwd(proj, dycat, cw_l, w["lru_conv_b"], wa_bd, ba, wx_bd, bx, w["lru_a_param"],
                                          name="lru_bwd")
    dxact, ddt, dz, g["ssd_norm_w"], small = _ssd_bwd(xact, proj, dycat, hprev, bias_pad, alog_pad, alogx, dxp,
                                                       w["ssd_norm_w"], name="ssd_bwd")
    dxbc, dcwb_s = _conv_silu_bwd(proj, dxact, cw_s, w["ssd_conv_b"], col0=COL_XBC, width=XBC, ct=256,
                                  name="ssd_conv_bwd")
    s = x.shape[0]
    dproj = jnp.concatenate([dxl, dgl, dz, dxbc, ddt, jnp.zeros((s, D_IN_PAD - COL_DT - LANE), F32)], axis=1)
    g["w_in"] = _mm(x, dproj, "tn", tm=512, tn=512, name="d_w_in")
    grad_x = _mm(dproj, w["w_in"], "nt", tm=256, tn=512, extra=dt1, epi=lambda acc, e: acc + ALPHA * e, name="d_x")

    g["lru_conv_w"] = dcwb_l[0:4]
    g["lru_conv_b"] = dcwb_l[4:5]
    g["lru_gate_a_b"] = dcwb_l[5:6]
    g["lru_gate_x_b"] = dcwb_l[6:7]
    g["lru_a_param"] = dcwb_l[7:8]
    g["lru_gate_a_w"] = _unblockdiag(dwa)
    g["lru_gate_x_w"] = _unblockdiag(dwx)
    g["ssd_conv_w"] = dcwb_s[0:4]
    g["ssd_conv_b"] = dcwb_s[4:5]
    g["ssd_dt_bias"] = small[0:1, :N_HEAD]
    g["ssd_a_log"] = small[1:2, :N_HEAD]
    g["ssd_d"] = small[2:3, :N_HEAD]
    return loss[0, 0], grad_x, g


ANY_SPEC = pl.BlockSpec(memory_space=pl.ANY)


def _mesh_pos():
    return lax.axis_index("x"), lax.axis_index("y"), lax.axis_index("c")


def _all_gather(arrs, *, name):
    n = len(arrs)

    def body(*refs):
        ins, outs = refs[:n], refs[n:2 * n]
        send_sems, recv_sems, loc_sems = refs[2 * n:]
        x, y, c = _mesh_pos()
        me, sibling = (x, y, c), (x, y, 1 - c)
        chips = [(1 - x, y), (x, 1 - y), (1 - x, 1 - y)]

        def blk(a, px, py, pc):
            return outs[a].at[4 * px + 2 * py + pc]

        def copy(a, k, block, to, src=None):
            dst = blk(a, *block)
            return pltpu.make_async_remote_copy(src_ref=dst if src is None else src, dst_ref=dst,
                                                send_sem=send_sems.at[a, k], recv_sem=recv_sems.at[a, k],
                                                device_id=to, device_id_type=MESH_T)

        local = [pltpu.make_async_copy(ins[a], blk(a, *me), loc_sems.at[a]) for a in range(n)]
        for cp in local:
            cp.start()
        first = []
        for a in range(n):
            first.append(copy(a, 0, me, sibling, src=ins[a]))
            first += [copy(a, 1 + j, me, (*chip, c), src=ins[a]) for j, chip in enumerate(chips)]
        for cp in first:
            cp.start()
        passed = []
        for a in range(n):
            for j, chip in enumerate(chips):
                copy(a, 1 + j, (*chip, c), me).wait_recv()
                fwd = copy(a, 4 + j, (*chip, c), sibling)
                fwd.start()
                passed.append(fwd)
        for a in range(n):
            copy(a, 0, sibling, me).wait_recv()
            for j, chip in enumerate(chips):
                copy(a, 4 + j, (*chip, 1 - c), me).wait_recv()
        for cp in first + passed:
            cp.wait_send()
        for cp in local:
            cp.wait()

    return _pcall(
        body, in_specs=[ANY_SPEC] * n, out_specs=[ANY_SPEC] * n,
        out_shape=[jax.ShapeDtypeStruct((N_DEV,) + a.shape, a.dtype) for a in arrs],
        scratch_shapes=[pltpu.SemaphoreType.DMA((n, 7)), pltpu.SemaphoreType.DMA((n, 7)), pltpu.SemaphoreType.DMA((n,))],
        name=name)(*arrs)


def _pair_exchange(arrs, *, name):
    n = len(arrs)

    def body(*refs):
        ins, outs = refs[:n], refs[n:2 * n]
        send_sems, recv_sems = refs[2 * n:]
        x, y, c = _mesh_pos()
        copies = []
        for a in range(n):
            for k in range(4):
                copies.append(pltpu.make_async_remote_copy(
                    src_ref=ins[a].at[2 * k + (1 - c)], dst_ref=outs[a].at[k],
                    send_sem=send_sems.at[a, k], recv_sem=recv_sems.at[a, k],
                    device_id=(x, y, 1 - c), device_id_type=MESH_T))
        for cp in copies:
            cp.start()
        for cp in copies:
            cp.wait()

    return _pcall(
        body, in_specs=[ANY_SPEC] * n, out_specs=[ANY_SPEC] * n,
        out_shape=[jax.ShapeDtypeStruct((4,) + a.shape[1:], a.dtype) for a in arrs],
        scratch_shapes=[pltpu.SemaphoreType.DMA((n, 4)), pltpu.SemaphoreType.DMA((n, 4))],
        name=name)(*arrs)


def _chip_exchange(arrs, *, name):
    n = len(arrs)

    def body(*refs):
        ins, outs = refs[:n], refs[n:2 * n]
        send_sems, recv_sems, loc_sems = refs[2 * n:]
        x, y, c = _mesh_pos()
        kme = 2 * x + y
        chips = [(1 - x, y), (x, 1 - y), (1 - x, 1 - y)]
        local = [pltpu.make_async_copy(ins[a].at[kme], outs[a].at[kme], loc_sems.at[a]) for a in range(n)]
        for cp in local:
            cp.start()
        sends, recvs = [], []
        for a in range(n):
            for j, (tx, ty) in enumerate(chips):
                kt = 2 * tx + ty
                sends.append(pltpu.make_async_remote_copy(
                    src_ref=ins[a].at[kt], dst_ref=outs[a].at[kme],
                    send_sem=send_sems.at[a, j], recv_sem=recv_sems.at[a, j],
                    device_id=(tx, ty, c), device_id_type=MESH_T))
                recvs.append(pltpu.make_async_remote_copy(
                    src_ref=ins[a].at[kme], dst_ref=outs[a].at[kt],
                    send_sem=send_sems.at[a, j], recv_sem=recv_sems.at[a, j],
                    device_id=(tx, ty, c), device_id_type=MESH_T))
        for cp in sends:
            cp.start()
        for cp in recvs:
            cp.wait_recv()
        for cp in sends:
            cp.wait_send()
        for cp in local:
            cp.wait()

    return _pcall(
        body, in_specs=[ANY_SPEC] * n, out_specs=[ANY_SPEC] * n,
        out_shape=[jax.ShapeDtypeStruct(a.shape, a.dtype) for a in arrs],
        scratch_shapes=[pltpu.SemaphoreType.DMA((n, 3)), pltpu.SemaphoreType.DMA((n, 3)), pltpu.SemaphoreType.DMA((n,))],
        name=name)(*arrs)


def _pair_add(g8, r4, cidx, *, name):
    _, r, c = g8.shape
    tr = min(r, ROW_TILE)

    def body(c_ref, g_ref, r_ref, o_ref):
        o_ref[...] = (g_ref[...] + r_ref[...]).astype(BF16)

    return _pcall(
        body,
        grid_spec=pltpu.PrefetchScalarGridSpec(
            num_scalar_prefetch=1, grid=(4, r // tr),
            in_specs=[pl.BlockSpec((None, tr, c), lambda k, i, cr: (2 * k + cr[0], i, 0)),
                      pl.BlockSpec((None, tr, c), lambda k, i, cr: (k, i, 0))],
            out_specs=pl.BlockSpec((None, tr, c), lambda k, i, cr: (k, i, 0))),
        out_shape=jax.ShapeDtypeStruct((4, r, c), BF16), name=name,
        compiler_params=_cparams(("parallel", "parallel")))(cidx, g8, r4)


def _adamw(gsrc, w, m, v, *, name):
    k, r, c = gsrc.shape
    tr = ROW_TILE if r % ROW_TILE == 0 else r
    c1 = 1.0 - ADAM_B1 ** ADAM_STEP
    c2 = 1.0 - ADAM_B2 ** ADAM_STEP

    def body(gs_ref, w_ref, m_ref, v_ref, g_ref, d_ref, mo_ref, vo_ref):
        g = gs_ref[0].astype(F32)
        for q in range(1, k):
            g = g + gs_ref[q].astype(F32)
        m2 = ADAM_B1 * m_ref[...] + (1.0 - ADAM_B1) * g
        v2 = ADAM_B2 * v_ref[...] + (1.0 - ADAM_B2) * (g * g)
        g_ref[...] = g
        mo_ref[...] = m2
        vo_ref[...] = v2
        d_ref[...] = -ADAM_LR * ((m2 / c1) / (jnp.sqrt(v2 / c2) + ADAM_EPS) + ADAM_WD * w_ref[...])

    row = pl.BlockSpec((tr, c), lambda i: (i, 0))
    sd = jax.ShapeDtypeStruct((r, c), F32)
    return _pcall(body, grid=(r // tr,), in_specs=[pl.BlockSpec((k, tr, c), lambda i: (0, i, 0)), row, row, row],
                  out_specs=(row, row, row, row), out_shape=(sd, sd, sd, sd), name=name,
                  compiler_params=_cparams(("parallel",)))(gsrc, w, m, v)


WEIGHTS = ['w_in', 'lru_conv_w', 'lru_conv_b', 'lru_gate_a_w', 'lru_gate_a_b', 'lru_gate_x_w', 'lru_gate_x_b',
           'lru_a_param', 'ssd_conv_w', 'ssd_conv_b', 'ssd_dt_bias', 'ssd_a_log', 'ssd_d', 'ssd_norm_w', 'w_out',
           'ln1_g', 'ln1_b', 'w_ff1', 'w_ff2', 'ln2_g', 'ln2_b', 'w_ple_gate', 'w_ple', 'ln3_g', 'ln3_b']
BIG = ['w_in', 'w_out', 'w_ff1', 'w_ff2', 'w_ple_gate', 'w_ple']
COL_SHARDED = ('w_in', 'w_ff1', 'w_ple')
CONV = ['lru_conv_w', 'ssd_conv_w']
REPL = [n for n in WEIGHTS if n not in BIG and n not in CONV]
CONV_CH = {'lru_conv_w': LRU_W, 'ssd_conv_w': XBC}


def _pack_rows(pieces):
    rows = []
    for a in pieces:
        flat = a.reshape(-1)
        padn = (-flat.shape[0]) % LANE
        if padn:
            flat = jnp.concatenate([flat, jnp.zeros((padn,), flat.dtype)])
        rows.append(flat.reshape(-1, LANE))
    out = jnp.concatenate(rows, axis=0)
    padr = (-out.shape[0]) % SUBLANE
    if padr:
        out = jnp.concatenate([out, jnp.zeros((padr, LANE), out.dtype)], axis=0)
    return out


def _unpack_rows(packed, shapes):
    outs, r0 = [], 0
    for shp in shapes:
        size = math.prod(shp)
        nrow = -(-size // LANE)
        outs.append(packed[r0:r0 + nrow].reshape(-1)[:size].reshape(shp))
        r0 += nrow
    return outs


def _to_dest_major(name, gfull):
    if name == 'w_in':
        gfull = gfull[:, :D_IN]
    if name in COL_SHARDED:
        r, cfull = gfull.shape
        return gfull.reshape(r, N_DEV, cfull // N_DEV).transpose(1, 0, 2)
    rfull, cdim = gfull.shape
    return gfull.reshape(N_DEV, rfull // N_DEV, cdim)


def _from_gathered(name, gathered):
    if name in COL_SHARDED:
        _, r, cs = gathered.shape
        return gathered.transpose(1, 0, 2).reshape(r, N_DEV * cs)
    _, rs, cdim = gathered.shape
    return gathered.reshape(N_DEV * rs, cdim)


def kernel(x, p, w_in, lru_conv_w, lru_conv_b, lru_gate_a_w, lru_gate_a_b, lru_gate_x_w, lru_gate_x_b, lru_a_param, ssd_conv_w, ssd_conv_b, ssd_dt_bias, ssd_a_log, ssd_d, ssd_norm_w, w_out, ln1_g, ln1_b, w_ff1, w_ff2, ln2_g, ln2_b, w_ple_gate, w_ple, ln3_g, ln3_b, loss_target, m_w_in, m_lru_conv_w, m_lru_conv_b, m_lru_gate_a_w, m_lru_gate_a_b, m_lru_gate_x_w, m_lru_gate_x_b, m_lru_a_param, m_ssd_conv_w, m_ssd_conv_b, m_ssd_dt_bias, m_ssd_a_log, m_ssd_d, m_ssd_norm_w, m_w_out, m_ln1_g, m_ln1_b, m_w_ff1, m_w_ff2, m_ln2_g, m_ln2_b, m_w_ple_gate, m_w_ple, m_ln3_g, m_ln3_b, v_w_in, v_lru_conv_w, v_lru_conv_b, v_lru_gate_a_w, v_lru_gate_a_b, v_lru_gate_x_w, v_lru_gate_x_b, v_lru_a_param, v_ssd_conv_w, v_ssd_conv_b, v_ssd_dt_bias, v_ssd_a_log, v_ssd_d, v_ssd_norm_w, v_w_out, v_ln1_g, v_ln1_b, v_w_ff1, v_w_ff2, v_ln2_g, v_ln2_b, v_w_ple_gate, v_w_ple, v_ln3_g, v_ln3_b):
    given = dict(locals())
    wsh = {n: given[n][0] for n in WEIGHTS}
    msh = {n: given["m_" + n][0] for n in WEIGHTS}
    vsh = {n: given["v_" + n][0] for n in WEIGHTS}
    xi, yi, ci = _mesh_pos()
    me = 4 * xi + 2 * yi + ci

    conv_pack = jnp.concatenate([_pad_rows8(wsh[n]) for n in CONV], axis=1)
    gathered = _all_gather([wsh[n].astype(BF16) for n in BIG] + [conv_pack], name="ag_weights")
    full = {n: _from_gathered(n, ga) for n, ga in zip(BIG, gathered[:-1])}
    full['w_in'] = jnp.concatenate([full['w_in'], jnp.zeros((D_MODEL, D_IN_PAD - D_IN), BF16)], axis=1)
    gconv = gathered[-1]
    c0 = 0
    for n in CONV:
        cw = CONV_CH[n] // N_DEV
        full[n] = gconv[:, :4, c0:c0 + cw].transpose(1, 0, 2).reshape(4, CONV_CH[n])
        c0 += cw
    for n in REPL:
        full[n] = wsh[n].reshape(1, -1) if wsh[n].ndim == 1 else wsh[n]

    loss_local, grad_x, g = _local_step(x[0], p[0, 0], loss_target[0], full)
    loss = lax.psum(loss_local, ("x", "y", "c"))

    dest = [_to_dest_major(n, g[n]) for n in BIG]
    sib = _pair_exchange(dest, name="rs_pair_exchange")
    cidx = jnp.reshape(ci, (1,)).astype(jnp.int32)
    part = [_pair_add(d8, r4, cidx, name="rs_pair_add_" + n) for n, d8, r4 in zip(BIG, dest, sib)]
    summed = _chip_exchange(part, name="rs_chip_exchange")

    repl_shapes = [wsh[n].shape for n in REPL]
    pack_a = _pack_rows([g[n] for n in REPL])
    pack_b = _pack_rows([g[n] for n in CONV])
    ga, gb = _all_gather([pack_a, pack_b], name="ag_small_grads")

    outs = {}
    for n, s4 in zip(BIG, summed):
        outs[n] = _adamw(s4, wsh[n], msh[n], vsh[n], name="adamw_" + n)
    res = _adamw(ga, _pack_rows([wsh[n] for n in REPL]), _pack_rows([msh[n] for n in REPL]),
                 _pack_rows([vsh[n] for n in REPL]), name="adamw_small")
    unp = [_unpack_rows(r, repl_shapes) for r in res]
    for i, n in enumerate(REPL):
        outs[n] = tuple(u[i] for u in unp)
    r0, mine = 0, []
    for n in CONV:
        ch = CONV_CH[n]
        nrow = 4 * ch // LANE
        gfull = gb[:, r0:r0 + nrow].reshape(N_DEV, 4, ch)
        cw = ch // N_DEV
        mine.append(lax.dynamic_slice_in_dim(gfull, me * cw, cw, axis=2))
        r0 += nrow
    gmine = jnp.concatenate(mine, axis=2)
    gmine = jnp.concatenate([gmine, jnp.zeros_like(gmine)], axis=1)
    pk = lambda d: jnp.concatenate([_pad_rows8(d[n]) for n in CONV], axis=1)
    res = _adamw(gmine, pk(wsh), pk(msh), pk(vsh), name="adamw_conv")
    c0 = 0
    for n in CONV:
        cw = CONV_CH[n] // N_DEV
        outs[n] = tuple(r[:4, c0:c0 + cw] for r in res)
        c0 += cw

    ex = lambda a: a[None]
    return (loss, grad_x[None],
            *[ex(outs[n][0]) for n in WEIGHTS], *[ex(outs[n][1]) for n in WEIGHTS],
            *[ex(outs[n][2]) for n in WEIGHTS], *[ex(outs[n][3]) for n in WEIGHTS])
```

```python
import math

import jax
import jax.numpy as jnp
from jax import lax
from jax.experimental import pallas as pl
from jax.experimental.pallas import tpu as pltpu

F32 = jnp.float32
BF16 = jnp.bfloat16
HI = lax.Precision.HIGHEST

N_DEV = 8
D_MODEL = 1024
LRU_W = 1024
SSD_W = 1024
XBC = 2048
N_HEAD = 16
HEAD_P = 64
N_GROUP = 4
GROUP_W = 256
N_STATE = 128
CHUNK = 128
D_FF = 4096
PLE_DIM = 256
D_IN = 5136
D_IN_PAD = 5632
COL_G = 1024
COL_Z = 2048
COL_XBC = 3072
COL_DT = 5120
LRU_C = 8.0
ALPHA = 2.0 ** 0.25
LN_EPS = 1e-5
RMS_EPS = 1e-5
ADAM_LR = 0.001
ADAM_B1 = 0.9
ADAM_B2 = 0.999
ADAM_EPS = 1e-08
ADAM_WD = 0.01
ADAM_STEP = 10
GELU_C = math.sqrt(2.0 / math.pi)
LANE = 128
SUBLANE = 8
VMEM_LIMIT = 48 * 1024 * 1024
MESH_T = pl.DeviceIdType.MESH
NEG_BIG = -1e30


def _pcall(body, **kw):
    return pl.pallas_call(body, **kw)


def _cparams(sem):
    return pltpu.CompilerParams(dimension_semantics=sem, vmem_limit_bytes=VMEM_LIMIT)


def _dot(a, b):
    return jnp.dot(a.astype(BF16), b.astype(BF16), preferred_element_type=F32)


def _dot_nt(a, b):
    return lax.dot_general(a.astype(BF16), b.astype(BF16), (((1,), (1,)), ((), ())), preferred_element_type=F32)


def _dot_tn(a, b):
    return lax.dot_general(a.astype(BF16), b.astype(BF16), (((0,), (0,)), ((), ())), preferred_element_type=F32)


def _dotx(a, b):
    return jnp.dot(a, b, precision=HI, preferred_element_type=F32)


def _sigmoid(x):
    return jax.nn.sigmoid(x)


def _softplus(v):
    return jnp.maximum(v, 0.0) + jnp.log1p(jnp.exp(-jnp.abs(v)))


def _gelu(x):
    th = jnp.tanh(GELU_C * (x + 0.044715 * x * x * x))
    return 0.5 * x * (1.0 + th), th


def _gelu_grad(x, th):
    return 0.5 * (1.0 + th) + 0.5 * x * (1.0 - th * th) * GELU_C * (1.0 + 3.0 * 0.044715 * x * x)


def _iota(shape, dim):
    return lax.broadcasted_iota(jnp.int32, shape, dim)


def _mm(a, b, mode, *, tm, tn, name, a_fn=None, extra=None, epi=None, out_dtype=F32, dest_major=False):
    m = a.shape[1] if mode == "tn" else a.shape[0]
    n = b.shape[0] if mode == "nt" else b.shape[1]
    tm, tn = min(tm, m), min(tn, n)
    if dest_major:
        tn = n // N_DEV
    if mode == "nn":
        m, k = a.shape
        _, n = b.shape
        a_spec = pl.BlockSpec((tm, k), lambda i, j: (i, 0))
        b_spec = pl.BlockSpec((k, tn), lambda i, j: (0, j))
        dims = ((1,), (0,))
    elif mode == "nt":
        m, k = a.shape
        n, _ = b.shape
        a_spec = pl.BlockSpec((tm, k), lambda i, j: (i, 0))
        b_spec = pl.BlockSpec((tn, k), lambda i, j: (j, 0))
        dims = ((1,), (1,))
    else:
        k, m = a.shape
        _, n = b.shape
        a_spec = pl.BlockSpec((k, tm), lambda i, j: (0, i))
        b_spec = pl.BlockSpec((k, tn), lambda i, j: (0, j))
        dims = ((0,), (0,))
    assert m % tm == 0 and n % tn == 0, (name, m, n, tm, tn)
    o_spec = pl.BlockSpec((tm, tn), lambda i, j: (i, j))
    in_specs = [a_spec, b_spec]
    args = [a, b]
    if extra is not None:
        in_specs.append(o_spec)
        args.append(extra)

    def body(*refs):
        a_ref, b_ref, o_ref = refs[0], refs[1], refs[-1]
        av = a_ref[...]
        if a_fn is not None:
            av = a_fn(av)
        acc = lax.dot_general(av.astype(BF16), b_ref[...].astype(BF16), (dims, ((), ())), preferred_element_type=F32)
        if epi is not None:
            acc = epi(acc, refs[2][...])
        o_ref[...] = acc.astype(out_dtype)

    out_shape = jax.ShapeDtypeStruct((m, n), out_dtype)
    if dest_major:
        assert extra is None
        o_spec = pl.BlockSpec((None, tm, tn), lambda i, j: (j, i, 0))
        out_shape = jax.ShapeDtypeStruct((N_DEV, m, tn), out_dtype)
    return _pcall(
        body, grid=(m // tm, n // tn), in_specs=in_specs, out_specs=o_spec, out_shape=out_shape, name=name,
        compiler_params=_cparams(("parallel", "parallel")),
    )(*args)


def _relu2(v):
    r = jnp.maximum(v, 0.0)
    return r * r


ROW_TILE = 256


def _ln_stats(t):
    mu = jnp.mean(t, axis=-1, keepdims=True)
    xc = t - mu
    var = jnp.mean(xc * xc, axis=-1, keepdims=True)
    rstd = lax.rsqrt(var + LN_EPS)
    return xc * rstd, rstd


def _ln_bwd_rows(dy, xhat, rstd, g):
    dxh = dy * g
    m1 = jnp.mean(dxh, axis=-1, keepdims=True)
    m2 = jnp.mean(dxh * xhat, axis=-1, keepdims=True)
    return rstd * (dxh - m1 - xhat * m2)


def _ln_fwd(a, b, g, beta, *, name):
    s, d = a.shape
    row = pl.BlockSpec((ROW_TILE, d), lambda i: (i, 0))
    par = pl.BlockSpec((1, d), lambda i: (0, 0))

    def body(a_ref, b_ref, g_ref, be_ref, y_ref):
        xhat, _ = _ln_stats(ALPHA * a_ref[...] + b_ref[...])
        y_ref[...] = xhat * g_ref[...] + be_ref[...]

    return _pcall(body, grid=(s // ROW_TILE,), in_specs=[row, row, par, par], out_specs=row,
                  out_shape=jax.ShapeDtypeStruct((s, d), F32), name=name,
                  compiler_params=_cparams(("parallel",)))(a, b, g, beta)


def _ln_bwd(a, b, g, dys, coefs, *, name):
    s, d = a.shape
    row = pl.BlockSpec((ROW_TILE, d), lambda i: (i, 0))
    par = pl.BlockSpec((1, d), lambda i: (0, 0))
    n = len(dys)

    def body(*refs):
        a_ref, b_ref, g_ref = refs[:3]
        dy_refs = refs[3:3 + n]
        dt_ref, dg_ref, db_ref = refs[3 + n:]
        xhat, rstd = _ln_stats(ALPHA * a_ref[...] + b_ref[...])
        dy = coefs[0] * dy_refs[0][...]
        for q in range(1, n):
            dy = dy + coefs[q] * dy_refs[q][...]
        dt_ref[...] = _ln_bwd_rows(dy, xhat, rstd, g_ref[...])

        @pl.when(pl.program_id(0) == 0)
        def _():
            dg_ref[...] = jnp.zeros_like(dg_ref)
            db_ref[...] = jnp.zeros_like(db_ref)

        dg_ref[...] += jnp.sum(dy * xhat, axis=0, keepdims=True)
        db_ref[...] += jnp.sum(dy, axis=0, keepdims=True)

    return _pcall(body, grid=(s // ROW_TILE,), in_specs=[row, row, par] + [row] * n, out_specs=(row, par, par),
                  out_shape=(jax.ShapeDtypeStruct((s, d), F32), jax.ShapeDtypeStruct((1, d), F32),
                             jax.ShapeDtypeStruct((1, d), F32)),
                  name=name, compiler_params=_cparams(("arbitrary",)))(a, b, g, *dys)


def _head(x2, gpre, ple, g, beta, tgt, *, name):
    s, d = x2.shape
    row = pl.BlockSpec((ROW_TILE, d), lambda i: (i, 0))
    par = pl.BlockSpec((1, d), lambda i: (0, 0))
    lsp = pl.BlockSpec((1, LANE), lambda i: (0, 0))

    def body(x2_ref, gp_ref, ple_ref, g_ref, be_ref, t_ref, loss_ref, dgp_ref, dple_ref, dt_ref, dg_ref, db_ref):
        gate = _sigmoid(gp_ref[...])
        ple_v = ple_ref[...]
        xhat, rstd = _ln_stats(ALPHA * x2_ref[...] + gate * ple_v)
        err = xhat * g_ref[...] + be_ref[...] - t_ref[...]
        dy = err * (1.0 / d)
        dt = _ln_bwd_rows(dy, xhat, rstd, g_ref[...])
        dt_ref[...] = dt
        dgp_ref[...] = dt * ple_v * gate * (1.0 - gate)
        dple_ref[...] = dt * gate

        @pl.when(pl.program_id(0) == 0)
        def _():
            loss_ref[...] = jnp.zeros_like(loss_ref)
            dg_ref[...] = jnp.zeros_like(dg_ref)
            db_ref[...] = jnp.zeros_like(db_ref)

        loss_ref[...] += 0.5 * jnp.sum(jnp.mean(err * err, axis=-1, keepdims=True))
        dg_ref[...] += jnp.sum(dy * xhat, axis=0, keepdims=True)
        db_ref[...] += jnp.sum(dy, axis=0, keepdims=True)

    sd = jax.ShapeDtypeStruct((s, d), F32)
    pd = jax.ShapeDtypeStruct((1, d), F32)
    return _pcall(body, grid=(s // ROW_TILE,), in_specs=[row, row, row, par, par, row],
                  out_specs=(lsp, row, row, row, par, par),
                  out_shape=(jax.ShapeDtypeStruct((1, LANE), F32), sd, sd, sd, pd, pd),
                  name=name, compiler_params=_cparams(("arbitrary",)))(x2, gpre, ple, g, beta, tgt)


CONV_R = 256
PAD = SUBLANE


def _shift_down(ext, s):
    if s == 0:
        return ext[PAD:, :]
    return pltpu.roll(ext, s, 0)[PAD:, :]


def _shift_up(ext, s):
    r = ext.shape[0] - PAD
    if s == 0:
        return ext[:r, :]
    return pltpu.roll(ext, r + PAD - s, 0)[:r, :]


def _conv_rows(xpad_ref, r0, w_ref):
    ext = xpad_ref[pl.ds(r0, CONV_R + PAD), :]
    acc = _shift_down(ext, 0) * w_ref[3:4, :]
    for k in range(3):
        acc = acc + _shift_down(ext, 3 - k) * w_ref[k:k + 1, :]
    return acc, ext


def _fill_front_padded(dst_ref, src_ref, s):
    dst_ref[0:PAD, :] = jnp.zeros((PAD, dst_ref.shape[1]), F32)

    def cp(q, _):
        r0 = pl.multiple_of(q * CONV_R, CONV_R)
        dst_ref[pl.ds(pl.multiple_of(PAD + r0, PAD), CONV_R), :] = src_ref[pl.ds(r0, CONV_R), :]
        return 0

    lax.fori_loop(0, s // CONV_R, cp, 0)


def _conv_silu_fwd(proj, w8, b, *, col0, width, ct, name):
    s = proj.shape[0]
    nb = col0 // ct

    def body(x_ref, w_ref, b_ref, o_ref, xpad):
        _fill_front_padded(xpad, x_ref, s)

        def step(q, _):
            r0 = pl.multiple_of(q * CONV_R, CONV_R)
            acc, _e = _conv_rows(xpad, r0, w_ref)
            pre = acc + b_ref[...]
            o_ref[pl.ds(r0, CONV_R), :] = pre * _sigmoid(pre)
            return 0

        lax.fori_loop(0, s // CONV_R, step, 0)

    return _pcall(
        body, grid=(width // ct,),
        in_specs=[pl.BlockSpec((s, ct), lambda j: (0, nb + j)), pl.BlockSpec((SUBLANE, ct), lambda j: (0, j)),
                  pl.BlockSpec((1, ct), lambda j: (0, j))],
        out_specs=pl.BlockSpec((s, ct), lambda j: (0, j)),
        out_shape=jax.ShapeDtypeStruct((s, width), F32),
        scratch_shapes=[pltpu.VMEM((s + PAD, ct), F32)], name=name,
        compiler_params=_cparams(("parallel",)))(proj, w8, b)


def _conv_bwd_rows(dpad_ref, r0, w_ref):
    return _conv_bwd_ext(dpad_ref[pl.ds(r0, CONV_R + PAD), :], w_ref)


def _conv_bwd_ext(ext, w_ref):
    acc = _shift_up(ext, 0) * w_ref[3:4, :]
    for k in range(3):
        acc = acc + _shift_up(ext, 3 - k) * w_ref[k:k + 1, :]
    return acc


def _conv_silu_bwd(proj, dact, w8, b, *, col0, width, ct, name):
    s = proj.shape[0]
    nb = col0 // ct

    def body(x_ref, d_ref, w_ref, b_ref, dx_ref, dwb_ref, xpad, dpad):
        _fill_front_padded(xpad, x_ref, s)
        dpad[pl.ds(s, PAD), :] = jnp.zeros((PAD, ct), F32)
        dwb_ref[...] = jnp.zeros_like(dwb_ref)

        def step(q, _):
            r0 = pl.multiple_of(q * CONV_R, CONV_R)
            acc, ext = _conv_rows(xpad, r0, w_ref)
            pre = acc + b_ref[...]
            sg = _sigmoid(pre)
            dpre = d_ref[pl.ds(r0, CONV_R), :] * sg * (1.0 + pre * (1.0 - sg))
            dpad[pl.ds(r0, CONV_R), :] = dpre
            for k in range(4):
                dwb_ref[k:k + 1, :] += jnp.sum(dpre * _shift_down(ext, 3 - k), axis=0, keepdims=True)
            dwb_ref[4:5, :] += jnp.sum(dpre, axis=0, keepdims=True)
            return 0

        lax.fori_loop(0, s // CONV_R, step, 0)

        def step2(q, _):
            r0 = pl.multiple_of(q * CONV_R, CONV_R)
            dx_ref[pl.ds(r0, CONV_R), :] = _conv_bwd_rows(dpad, r0, w_ref)
            return 0

        lax.fori_loop(0, s // CONV_R, step2, 0)

    colb = pl.BlockSpec((s, ct), lambda j: (0, j))
    return _pcall(
        body, grid=(width // ct,),
        in_specs=[pl.BlockSpec((s, ct), lambda j: (0, nb + j)), colb, pl.BlockSpec((SUBLANE, ct), lambda j: (0, j)),
                  pl.BlockSpec((1, ct), lambda j: (0, j))],
        out_specs=(colb, pl.BlockSpec((SUBLANE, ct), lambda j: (0, j))),
        out_shape=(jax.ShapeDtypeStruct((s, width), F32), jax.ShapeDtypeStruct((SUBLANE, width), F32)),
        scratch_shapes=[pltpu.VMEM((s + PAD, ct), F32), pltpu.VMEM((s + PAD, ct), F32)], name=name,
        compiler_params=_cparams(("parallel",)))(proj, dact, w8, b)


LRU_CT = 128


def _row_of(v, r):
    return jnp.sum(jnp.where(_iota((v.shape[0], 1), 0) == r, v, 0.0), axis=0, keepdims=True)


def _scan_fwd(a, u):
    r = a.shape[0]
    row = _iota((r, 1), 0)
    d = 1
    while d < r:
        valid = row >= d
        u = jnp.where(valid, a * pltpu.roll(u, d, 0) + u, u)
        a = jnp.where(valid, a * pltpu.roll(a, d, 0), a)
        d *= 2
    return a, u


def _scan_rev(b, u):
    r = b.shape[0]
    row = _iota((r, 1), 0)
    d = 1
    while d < r:
        valid = row < r - d
        u = jnp.where(valid, b * pltpu.roll(u, r - d, 0) + u, u)
        b = jnp.where(valid, b * pltpu.roll(b, r - d, 0), b)
        d *= 2
    return b, u


def _lru_chunk(xpad, r0, cw_ref, cb, wa, ba, wx, bx, sp):
    acc, ext = _conv_rows(xpad, r0, cw_ref)
    xl = acc + cb
    r = _sigmoid(_dot(xl, wa) + ba)
    i = _sigmoid(_dot(xl, wx) + bx)
    la = -LRU_C * r * sp
    a = jnp.exp(la)
    a2 = jnp.exp(2.0 * la)
    mult = jnp.sqrt(-jnp.tanh(la) * (a2 + 1.0))
    first = (r0 + _iota((CONV_R, 1), 0)) == 0
    mult = jnp.where(first, 1.0, mult)
    return ext, xl, r, i, a, a2, mult, first


def _lru_specs(s):
    ct = LRU_CT
    nb_g = COL_G // ct
    return dict(
        x=pl.BlockSpec((s, ct), lambda j: (0, j)),
        g=pl.BlockSpec((s, ct), lambda j: (0, nb_g + j)),
        col=pl.BlockSpec((s, ct), lambda j: (0, j)),
        cw=pl.BlockSpec((SUBLANE, ct), lambda j: (0, j)),
        vec=pl.BlockSpec((1, ct), lambda j: (0, j)),
        gate=pl.BlockSpec((None, ct, ct), lambda j: (j, 0, 0)),
    )


def _lru_fwd(proj, cw8, cb, wa_bd, ba, wx_bd, bx, ap, *, name):
    s = proj.shape[0]
    ct = LRU_CT
    sp_ = _lru_specs(s)

    def body(x_ref, g_ref, cw_ref, cb_ref, wa_ref, ba_ref, wx_ref, bx_ref, ap_ref, y_ref, h_ref, xpad):
        _fill_front_padded(xpad, x_ref, s)
        sp = _softplus(-ap_ref[...])

        def step(q, carry):
            r0 = pl.multiple_of(q * CONV_R, CONV_R)
            _e, xl, _r, i, a, _a2, mult, _f = _lru_chunk(xpad, r0, cw_ref, cb_ref[...], wa_ref[...], ba_ref[...],
                                                       wx_ref[...], bx_ref[...], sp)
            acum, ucum = _scan_fwd(a, xl * i * mult)
            h = acum * carry + ucum
            h_ref[pl.ds(r0, CONV_R), :] = h
            ge, _th = _gelu(g_ref[pl.ds(r0, CONV_R), :])
            y_ref[pl.ds(r0, CONV_R), :] = ge * h
            return _row_of(h, CONV_R - 1)

        lax.fori_loop(0, s // CONV_R, step, jnp.zeros((1, ct), F32))

    return _pcall(
        body, grid=(LRU_W // ct,),
        in_specs=[sp_["x"], sp_["g"], sp_["cw"], sp_["vec"], sp_["gate"], sp_["vec"], sp_["gate"], sp_["vec"], sp_["vec"]],
        out_specs=(sp_["col"], sp_["col"]),
        out_shape=(jax.ShapeDtypeStruct((s, LRU_W + SSD_W), F32), jax.ShapeDtypeStruct((s, LRU_W), F32)),
        scratch_shapes=[pltpu.VMEM((s + PAD, ct), F32)],
        name=name, compiler_params=_cparams(("parallel",)))(proj, proj, cw8, cb, wa_bd, ba, wx_bd, bx, ap)


def _lru_bwd(proj, dy, hs, cw8, cb, wa_bd, ba, wx_bd, bx, ap, *, name):
    s = proj.shape[0]
    ct = LRU_CT
    sp_ = _lru_specs(s)

    nq = s // CONV_R

    def body(x_ref, g_ref, dy_ref, h_ref, cw_ref, cb_ref, wa_ref, ba_ref, wx_ref, bx_ref, ap_ref,
             dx_ref, dg_ref, dcwb_ref, dwa_ref, dwx_ref, xpad, hpad):
        _fill_front_padded(xpad, x_ref, s)
        _fill_front_padded(hpad, h_ref, s)
        apv = ap_ref[...]
        sp = _softplus(-apv)
        cb_v, wa, ba_v, wx, bx_v = cb_ref[...], wa_ref[...], ba_ref[...], wx_ref[...], bx_ref[...]
        dcwb_ref[...] = jnp.zeros_like(dcwb_ref)
        dwa_ref[...] = jnp.zeros_like(dwa_ref)
        dwx_ref[...] = jnp.zeros_like(dwx_ref)

        def back(k, carry):
            g_next, a_next, dxl_next = carry
            last_row = _iota((CONV_R, 1), 0) == CONV_R - 1
            r0 = pl.multiple_of((nq - 1 - k) * CONV_R, CONV_R)
            ext, xl, r, i, a, a2, mult, first = _lru_chunk(xpad, r0, cw_ref, cb_v, wa, ba_v, wx, bx_v, sp)
            gv = g_ref[pl.ds(r0, CONV_R), :]
            dyv = dy_ref[pl.ds(r0, CONV_R), :]
            hext = hpad[pl.ds(r0, CONV_R + PAD), :]
            ge, th = _gelu(gv)
            dg_ref[pl.ds(r0, CONV_R), :] = dyv * _shift_down(hext, 0) * _gelu_grad(gv, th)
            b = jnp.where(last_row, a_next, pltpu.roll(a, CONV_R - 1, 0))
            bcum, dcum = _scan_rev(b, dyv * ge)
            gval = dcum + bcum * g_next
            hprev = _shift_down(hext, 1)
            da = gval * hprev
            dxl = gval * i * mult
            di = gval * xl * mult
            dmult = jnp.where(first, 0.0, gval * xl * i)
            dla = da * a - dmult * a2 / mult
            dr = dla * (-LRU_C) * sp
            dcwb_ref[7:8, :] += jnp.sum(dla * (-LRU_C) * r, axis=0, keepdims=True)
            dpr = dr * r * (1.0 - r)
            dpi = di * i * (1.0 - i)
            dxl = dxl + _dot_nt(dpr, wa) + _dot_nt(dpi, wx)
            dwa_ref[...] += _dot_tn(xl, dpr)
            dwx_ref[...] += _dot_tn(xl, dpi)
            dcwb_ref[5:6, :] += jnp.sum(dpr, axis=0, keepdims=True)
            dcwb_ref[6:7, :] += jnp.sum(dpi, axis=0, keepdims=True)
            for tap in range(4):
                dcwb_ref[tap:tap + 1, :] += jnp.sum(dxl * _shift_down(ext, 3 - tap), axis=0, keepdims=True)
            dcwb_ref[4:5, :] += jnp.sum(dxl, axis=0, keepdims=True)
            dx_ref[pl.ds(r0, CONV_R), :] = _conv_bwd_ext(jnp.concatenate([dxl, dxl_next], axis=0), cw_ref)
            return _row_of(gval, 0), _row_of(a, 0), dxl[:PAD, :]

        zero = jnp.zeros((1, ct), F32)
        lax.fori_loop(0, nq, back, (zero, zero, jnp.zeros((PAD, ct), F32)))
        dcwb_ref[7:8, :] = dcwb_ref[7:8, :] * (-_sigmoid(-apv))

    nt = LRU_W // ct
    return _pcall(
        body, grid=(nt,),
        in_specs=[sp_["x"], sp_["g"], sp_["col"], sp_["col"], sp_["cw"], sp_["vec"], sp_["gate"], sp_["vec"], sp_["gate"],
                  sp_["vec"], sp_["vec"]],
        out_specs=(sp_["col"], sp_["col"], sp_["cw"], sp_["gate"], sp_["gate"]),
        out_shape=(jax.ShapeDtypeStruct((s, LRU_W), F32), jax.ShapeDtypeStruct((s, LRU_W), F32),
                   jax.ShapeDtypeStruct((SUBLANE, LRU_W), F32), jax.ShapeDtypeStruct((nt, ct, ct), F32),
                   jax.ShapeDtypeStruct((nt, ct, ct), F32)),
        scratch_shapes=[pltpu.VMEM((s + PAD, ct), F32), pltpu.VMEM((s + PAD, ct), F32)],
        name=name, compiler_params=_cparams(("parallel",)))(proj, proj, dy, hs, cw8, cb, wa_bd, ba, wx_bd, bx, ap)


def _ssd_prep(dtr, bias, alog_pad, alogx):
    l = CHUNK
    lane = _iota((1, LANE), 1)
    a_head = jnp.where(lane < N_HEAD, -jnp.exp(alog_pad), 0.0)
    dt = _softplus(dtr + bias)
    tril = (_iota((l, l), 1) <= _iota((l, l), 0)).astype(F32)
    cs = _dotx(tril, dt * a_head)
    expand = (jnp.right_shift(_iota((LANE, SSD_W), 1), 6) == _iota((LANE, SSD_W), 0)).astype(F32)
    dtx = _dotx(dt, expand)
    ax = -jnp.exp(alogx)
    csx = _dotx(tril, dtx * ax)
    totx = jnp.sum(dtx * ax, axis=0, keepdims=True)
    return dict(a_head=a_head, dt=dt, tril=tril, cs=cs, expand=expand, dtx=dtx, ax=ax, csx=csx, totx=totx)


def _decay_mat(cs, cst_ref, h, causal):
    lane = _iota((CHUNK, LANE), 1)
    col = jnp.sum(jnp.where(lane == h, cs, 0.0), axis=1, keepdims=True)
    row = cst_ref[h:h + 1, :]
    return jnp.exp(jnp.where(causal, col - row, NEG_BIG))


def _head_mask(j):
    lane = _iota((CHUNK, GROUP_W), 1)
    return (lane >= j * HEAD_P) & (lane < (j + 1) * HEAD_P)


def _ssd_group_fwd(q, g, xs_g, bg, cg, ht_g, cst_ref, causal, dx_g):
    sl = slice(g * GROUP_W, (g + 1) * GROUP_W)
    dtx_g, csx_g, totx_g = q["dtx"][:, sl], q["csx"][:, sl], q["totx"][:, sl]
    xdt = xs_g * dtx_g
    ex = jnp.exp(csx_g)
    cb = _dot_nt(cg, bg)
    yoff = _dot(cg, ht_g) * ex
    ydiag = jnp.zeros((CHUNK, GROUP_W), F32)
    for j in range(4):
        sc = cb * _decay_mat(q["cs"], cst_ref, 4 * g + j, causal)
        ydiag = jnp.where(_head_mask(j), _dot(sc, xdt), ydiag)
    y = ydiag + yoff + xs_g * dx_g
    dsx = jnp.exp(totx_g - csx_g)
    return y, dict(xdt=xdt, ex=ex, cb=cb, yoff=yoff, dsx=dsx, dtx=dtx_g, totx=totx_g)


def _gated_norm_fwd(y_g, z_g, w_g):
    sz = _sigmoid(z_g)
    silu = z_g * sz
    yf = y_g * silu
    rs = lax.rsqrt(jnp.mean(yf * yf, axis=1, keepdims=True) + RMS_EPS)
    yn = yf * rs
    return yn * w_g, (sz, silu, rs, yn)


def _ssd_fwd(xact, proj, ymix, bias_pad, alog_pad, alogx, dxp, normw, *, name):
    s = xact.shape[0]
    nc = s // CHUNK

    def body(xa_ref, dt_ref, z_ref, _ymix_ref, bias_ref, alp_ref, alx_ref, dx_ref, nw_ref, y_ref, hp_ref, ht, cst):
        @pl.when(pl.program_id(0) == 0)
        def _():
            ht[...] = jnp.zeros_like(ht)

        hp_ref[...] = ht[...]
        q = _ssd_prep(dt_ref[...], bias_ref[...], alp_ref[...], alx_ref[...])
        cst[...] = q["cs"].T
        causal = q["tril"] > 0.0
        for g in range(N_GROUP):
            sl = slice(g * GROUP_W, (g + 1) * GROUP_W)
            xs_g = xa_ref[:, sl]
            bg = xa_ref[:, SSD_W + g * N_STATE:SSD_W + (g + 1) * N_STATE]
            cg = xa_ref[:, SSD_W + N_GROUP * N_STATE + g * N_STATE:SSD_W + N_GROUP * N_STATE + (g + 1) * N_STATE]
            ht_g = ht[:, sl]
            y, f = _ssd_group_fwd(q, g, xs_g, bg, cg, ht_g, cst, causal, dx_ref[:, sl])
            out, _ = _gated_norm_fwd(y, z_ref[:, sl], nw_ref[:, sl])
            y_ref[:, sl] = out
            ht[:, sl] = jnp.exp(f["totx"]) * ht_g + _dot_tn(bg, f["xdt"] * f["dsx"])

    par = lambda w: pl.BlockSpec((1, w), lambda c: (0, 0))
    return _pcall(
        body, grid=(nc,),
        in_specs=[pl.BlockSpec((CHUNK, XBC), lambda c: (c, 0)),
                  pl.BlockSpec((CHUNK, LANE), lambda c: (c, COL_DT // LANE)),
                  pl.BlockSpec((CHUNK, SSD_W), lambda c: (c, COL_Z // SSD_W)),
                  ANY_SPEC, par(LANE), par(LANE), par(SSD_W), par(SSD_W), par(SSD_W)],
        out_specs=(pl.BlockSpec((CHUNK, SSD_W), lambda c: (c, LRU_W // SSD_W)),
                   pl.BlockSpec((None, N_STATE, SSD_W), lambda c: (c, 0, 0))),
        out_shape=(jax.ShapeDtypeStruct(ymix.shape, F32), jax.ShapeDtypeStruct((nc, N_STATE, SSD_W), F32)),
        scratch_shapes=[pltpu.VMEM((N_STATE, SSD_W), F32), pltpu.VMEM((CHUNK, LANE), F32)],
        input_output_aliases={3: 0},
        name=name, compiler_params=_cparams(("arbitrary",)))(xact, proj, proj, ymix, bias_pad, alog_pad, alogx, dxp, normw)


def _ssd_bwd(xact, proj, dycat, hprev, bias_pad, alog_pad, alogx, dxp, normw, *, name):
    s = xact.shape[0]
    nc = s // CHUNK
    l = CHUNK

    def body(xa_ref, dt_ref, z_ref, dy_ref, hp_ref, bias_ref, alp_ref, alx_ref, dx_ref, nw_ref,
             dxa_ref, ddt_ref, dz_ref, dnw_ref, small_ref, dht, cst, accx, dcsx_s, ddtx_s):
        step = pl.program_id(0)

        @pl.when(step == 0)
        def _():
            dht[...] = jnp.zeros_like(dht)
            accx[...] = jnp.zeros_like(accx)
            dnw_ref[...] = jnp.zeros_like(dnw_ref)
            small_ref[...] = jnp.zeros_like(small_ref)

        dtr = dt_ref[...]
        q = _ssd_prep(dtr, bias_ref[...], alp_ref[...], alx_ref[...])
        cst[...] = q["cs"].T
        causal = q["tril"] > 0.0
        eye = _iota((l, l), 0) == _iota((l, l), 1)
        lane = _iota((l, LANE), 1)
        dcs_head = jnp.zeros((l, LANE), F32)
        for g in range(N_GROUP):
            sl = slice(g * GROUP_W, (g + 1) * GROUP_W)
            slb = slice(SSD_W + g * N_STATE, SSD_W + (g + 1) * N_STATE)
            slc = slice(SSD_W + N_GROUP * N_STATE + g * N_STATE, SSD_W + N_GROUP * N_STATE + (g + 1) * N_STATE)
            xs_g, bg, cg = xa_ref[:, sl], xa_ref[:, slb], xa_ref[:, slc]
            ht_g = hp_ref[:, sl]
            dxp_g = dx_ref[:, sl]
            y, f = _ssd_group_fwd(q, g, xs_g, bg, cg, ht_g, cst, causal, dxp_g)
            z_g, nw_g = z_ref[:, sl], nw_ref[:, sl]
            _o, (sz, silu, rs, yn) = _gated_norm_fwd(y, z_g, nw_g)
            dout = dy_ref[:, sl]
            dnw_ref[:, sl] += jnp.sum(dout * yn, axis=0, keepdims=True)
            dyn = dout * nw_g
            dyf = rs * (dyn - yn * jnp.mean(dyn * yn, axis=1, keepdims=True))
            dy = dyf * silu
            dz_ref[:, sl] = dyf * y * sz * (1.0 + z_g * (1.0 - sz))
            accx[0:1, sl] += jnp.sum(dy * xs_g, axis=0, keepdims=True)
            dyo = dy * f["ex"]
            dcg = _dot_nt(dyo, ht_g)
            dht_prev = _dot_tn(cg, dyo)
            dcsx = dy * f["yoff"]
            xdt = f["xdt"]
            dxdt = jnp.zeros((l, GROUP_W), F32)
            dcb = jnp.zeros((l, l), F32)
            for j in range(4):
                h = 4 * g + j
                lm = _decay_mat(q["cs"], cst, h, causal)
                sc = f["cb"] * lm
                mask = _head_mask(j)
                ds_ = jnp.where(causal, _dot_nt(jnp.where(mask, dy, 0.0), xdt), 0.0)
                dxdt = jnp.where(mask, _dot_tn(sc, dy), dxdt)
                dcb = dcb + ds_ * lm
                m = ds_ * sc
                rsum = jnp.sum(m, axis=1, keepdims=True)
                csum = jnp.sum(m, axis=0, keepdims=True)
                csum_col = jnp.sum(jnp.where(eye, csum, 0.0), axis=1, keepdims=True)
                dcs_head = dcs_head + jnp.where(lane == h, rsum - csum_col, 0.0)
            dhn = dht[:, sl]
            etot = jnp.exp(f["totx"])
            dxd = _dot(bg, dhn)
            dbg = _dot_nt(xdt * f["dsx"], dhn)
            dxdt = dxdt + dxd * f["dsx"]
            qq = dxd * xdt * f["dsx"]
            dcsx = dcsx - qq
            dtot = jnp.sum(qq, axis=0, keepdims=True) + jnp.sum(dhn * ht_g, axis=0, keepdims=True) * etot
            dht[:, sl] = etot * dhn + dht_prev
            dcg = dcg + _dot(dcb, bg)
            dbg = dbg + _dot_tn(dcb, cg)
            dxa_ref[:, sl] = dxdt * f["dtx"] + dy * dxp_g
            dxa_ref[:, slb] = dbg
            dxa_ref[:, slc] = dcg
            dcsx_s[:, sl] = dcsx
            ddtx_s[:, sl] = dxdt * xs_g
            accx[2:3, sl] = dtot
        triu = (_iota((l, l), 1) >= _iota((l, l), 0)).astype(F32)
        dax = _dotx(triu, dcsx_s[...]) + accx[2:3, :]
        accx[1:2, :] += jnp.sum(dax * q["dtx"], axis=0, keepdims=True)
        reduce = (jnp.right_shift(_iota((SSD_W, LANE), 0), 6) == _iota((SSD_W, LANE), 1)).astype(F32)
        ddt = _dotx(ddtx_s[...] + dax * q["ax"], reduce)
        da_head = _dotx(triu, dcs_head)
        ddt = ddt + da_head * q["a_head"]
        small_ref[1:2, :] += jnp.sum(da_head * q["dt"], axis=0, keepdims=True)
        ddtr = ddt * _sigmoid(dtr + bias_ref[...])
        ddt_ref[...] = ddtr
        small_ref[0:1, :] += jnp.sum(ddtr, axis=0, keepdims=True)

        @pl.when(step == nc - 1)
        def _():
            red = _dotx(accx[...], reduce)
            d_a = small_ref[1:2, :] + red[1:2, :]
            small_ref[1:2, :] = d_a * q["a_head"]
            small_ref[2:3, :] = red[0:1, :]

    rev = lambda c: nc - 1 - c
    par = lambda w: pl.BlockSpec((1, w), lambda c: (0, 0))
    return _pcall(
        body, grid=(nc,),
        in_specs=[pl.BlockSpec((CHUNK, XBC), lambda c: (rev(c), 0)),
                  pl.BlockSpec((CHUNK, LANE), lambda c: (rev(c), COL_DT // LANE)),
                  pl.BlockSpec((CHUNK, SSD_W), lambda c: (rev(c), COL_Z // SSD_W)),
                  pl.BlockSpec((CHUNK, SSD_W), lambda c: (rev(c), 1)),
                  pl.BlockSpec((None, N_STATE, SSD_W), lambda c: (rev(c), 0, 0)),
                  par(LANE), par(LANE), par(SSD_W), par(SSD_W), par(SSD_W)],
        out_specs=(pl.BlockSpec((CHUNK, XBC), lambda c: (rev(c), 0)),
                   pl.BlockSpec((CHUNK, LANE), lambda c: (rev(c), 0)),
                   pl.BlockSpec((CHUNK, SSD_W), lambda c: (rev(c), 0)),
                   par(SSD_W), pl.BlockSpec((SUBLANE, LANE), lambda c: (0, 0))),
        out_shape=(jax.ShapeDtypeStruct((s, XBC), F32), jax.ShapeDtypeStruct((s, LANE), F32),
                   jax.ShapeDtypeStruct((s, SSD_W), F32), jax.ShapeDtypeStruct((1, SSD_W), F32),
                   jax.ShapeDtypeStruct((SUBLANE, LANE), F32)),
        scratch_shapes=[pltpu.VMEM((N_STATE, SSD_W), F32), pltpu.VMEM((CHUNK, LANE), F32),
                        pltpu.VMEM((SUBLANE, SSD_W), F32), pltpu.VMEM((CHUNK, SSD_W), F32),
                        pltpu.VMEM((CHUNK, SSD_W), F32)],
        name=name, compiler_params=_cparams(("arbitrary",)))(
            xact, proj, proj, dycat, hprev, bias_pad, alog_pad, alogx, dxp, normw)


def _blockdiag(w):
    w2 = w.reshape(N_HEAD // 2, 2, HEAD_P, HEAD_P)
    z = jnp.zeros((N_HEAD // 2, HEAD_P, HEAD_P), w.dtype)
    top = jnp.concatenate([w2[:, 0], z], axis=2)
    bot = jnp.concatenate([z, w2[:, 1]], axis=2)
    return jnp.concatenate([top, bot], axis=1)


def _unblockdiag(wbd):
    a = wbd[:, :HEAD_P, :HEAD_P]
    b = wbd[:, HEAD_P:, HEAD_P:]
    return jnp.stack([a, b], axis=1).reshape(N_HEAD, HEAD_P, HEAD_P)


def _pad_rows8(w):
    return jnp.concatenate([w, jnp.zeros((SUBLANE - w.shape[0], w.shape[1]), w.dtype)], axis=0)


def _pad_lane(v):
    return jnp.concatenate([v, jnp.zeros((1, LANE - v.shape[1]), v.dtype)], axis=1)


def _local_step(x, p, tgt, w):
    cw_l = _pad_rows8(w["lru_conv_w"])
    cw_s = _pad_rows8(w["ssd_conv_w"])
    wa_bd = _blockdiag(w["lru_gate_a_w"])
    wx_bd = _blockdiag(w["lru_gate_x_w"])
    ba = w["lru_gate_a_b"].reshape(1, LRU_W)
    bx = w["lru_gate_x_b"].reshape(1, LRU_W)
    bias_pad = _pad_lane(w["ssd_dt_bias"])
    alog_pad = _pad_lane(w["ssd_a_log"])
    alogx = jnp.repeat(w["ssd_a_log"], HEAD_P, axis=1)
    dxp = jnp.repeat(w["ssd_d"], HEAD_P, axis=1)

    proj = _mm(x, w["w_in"], "nn", tm=512, tn=512, name="in_proj")
    ymix, h_lru = _lru_fwd(proj, cw_l, w["lru_conv_b"], wa_bd, ba, wx_bd, bx, w["lru_a_param"], name="lru_fwd")
    xact = _conv_silu_fwd(proj, cw_s, w["ssd_conv_b"], col0=COL_XBC, width=XBC, ct=256, name="ssd_conv_fwd")
    ycat, hprev = _ssd_fwd(xact, proj, ymix, bias_pad, alog_pad, alogx, dxp, w["ssd_norm_w"], name="ssd_fwd")
    mix = _mm(ycat, w["w_out"], "nn", tm=512, tn=512, name="out_proj")
    x1 = _ln_fwd(x, mix, w["ln1_g"], w["ln1_b"], name="ln1_fwd")
    pre = _mm(x1, w["w_ff1"], "nn", tm=512, tn=512, name="ff1")
    ff = _mm(pre, w["w_ff2"], "nn", tm=512, tn=512, a_fn=_relu2, name="ff2")
    x2 = _ln_fwd(x1, ff, w["ln2_g"], w["ln2_b"], name="ln2_fwd")
    gpre = _mm(x2, w["w_ple_gate"], "nn", tm=512, tn=512, name="ple_gate")
    ple = _mm(p, w["w_ple"], "nn", tm=512, tn=512, name="ple_proj")
    loss, dgpre, dple, dt3, dg3, db3 = _head(x2, gpre, ple, w["ln3_g"], w["ln3_b"], tgt, name="head")

    g = {}
    g["ln3_g"], g["ln3_b"] = dg3, db3
    g["w_ple_gate"] = _mm(x2, dgpre, "tn", tm=512, tn=512, name="d_w_ple_gate")
    g["w_ple"] = _mm(p, dple, "tn", tm=256, tn=512, dest_major=True, name="d_w_ple")
    dx2_mm = _mm(dgpre, w["w_ple_gate"], "nt", tm=512, tn=512, name="d_x2")
    dt2, g["ln2_g"], g["ln2_b"] = _ln_bwd(x1, ff, w["ln2_g"], [dt3, dx2_mm], [ALPHA, 1.0], name="ln2_bwd")
    g["w_ff2"] = _mm(pre, dt2, "tn", tm=512, tn=512, a_fn=_relu2, name="d_w_ff2")
    dpre = _mm(dt2, w["w_ff2"], "nt", tm=512, tn=512, extra=pre,
               epi=lambda acc, pv: acc * 2.0 * jnp.maximum(pv, 0.0), name="d_pre")
    g["w_ff1"] = _mm(x1, dpre, "tn", tm=512, tn=512, dest_major=True, name="d_w_ff1")
    dx1_mm = _mm(dpre, w["w_ff1"], "nt", tm=512, tn=512, name="d_x1")
    dt1, g["ln1_g"], g["ln1_b"] = _ln_bwd(x, mix, w["ln1_g"], [dt2, dx1_mm], [ALPHA, 1.0], name="ln1_bwd")
    g["w_out"] = _mm(ycat, dt1, "tn", tm=512, tn=512, name="d_w_out")
    dycat = _mm(dt1, w["w_out"], "nt", tm=512, tn=512, name="d_ycat")
    dxl, dgl, dcwb_l, dwa, dwx = _lru_bwd(proj, dycat, h_lru, cw_l, w["lru_conv_b"], wa_bd, ba, wx_bd, bx, w["lru_a_param"],
                                          name="lru_bwd")
    dxact, ddt, dz, g["ssd_norm_w"], small = _ssd_bwd(xact, proj, dycat, hprev, bias_pad, alog_pad, alogx, dxp,
                                                       w["ssd_norm_w"], name="ssd_bwd")
    dxbc, dcwb_s = _conv_silu_bwd(proj, dxact, cw_s, w["ssd_conv_b"], col0=COL_XBC, width=XBC, ct=256,
                                  name="ssd_conv_bwd")
    s = x.shape[0]
    dproj = jnp.concatenate([dxl, dgl, dz, dxbc, ddt, jnp.zeros((s, D_IN_PAD - COL_DT - LANE), F32)], axis=1)
    g["w_in"] = _mm(x, dproj, "tn", tm=512, tn=512, name="d_w_in")
    grad_x = _mm(dproj, w["w_in"], "nt", tm=256, tn=512, extra=dt1, epi=lambda acc, e: acc + ALPHA * e, name="d_x")

    g["lru_conv_w"] = dcwb_l[0:4]
    g["lru_conv_b"] = dcwb_l[4:5]
    g["lru_gate_a_b"] = dcwb_l[5:6]
    g["lru_gate_x_b"] = dcwb_l[6:7]
    g["lru_a_param"] = dcwb_l[7:8]
    g["lru_gate_a_w"] = _unblockdiag(dwa)
    g["lru_gate_x_w"] = _unblockdiag(dwx)
    g["ssd_conv_w"] = dcwb_s[0:4]
    g["ssd_conv_b"] = dcwb_s[4:5]
    g["ssd_dt_bias"] = small[0:1, :N_HEAD]
    g["ssd_a_log"] = small[1:2, :N_HEAD]
    g["ssd_d"] = small[2:3, :N_HEAD]
    rows = jnp.concatenate([g[n] for n in ("ssd_norm_w", "ln1_g", "ln1_b", "ln2_g", "ln2_b", "ln3_g", "ln3_b")]
                           + [jnp.zeros((1, D_MODEL), F32)], axis=0)
    raw = dict(lru=dcwb_l, ssd=dcwb_s, gate_a=g["lru_gate_a_w"].reshape(N_HEAD * HEAD_P, HEAD_P),
               gate_x=g["lru_gate_x_w"].reshape(N_HEAD * HEAD_P, HEAD_P), heads=small, rows=rows)
    return loss[0, 0], grad_x, g, raw


ANY_SPEC = pl.BlockSpec(memory_space=pl.ANY)


def _mesh_pos():
    return lax.axis_index("x"), lax.axis_index("y"), lax.axis_index("c")


def _all_gather(arrs, *, name):
    n = len(arrs)

    def body(*refs):
        ins, outs = refs[:n], refs[n:2 * n]
        send_sems, recv_sems, loc_sems = refs[2 * n:]
        x, y, c = _mesh_pos()
        me, sibling = (x, y, c), (x, y, 1 - c)
        chips = [(1 - x, y), (x, 1 - y), (1 - x, 1 - y)]

        def blk(a, px, py, pc):
            return outs[a].at[4 * px + 2 * py + pc]

        def copy(a, k, block, to, src=None):
            dst = blk(a, *block)
            return pltpu.make_async_remote_copy(src_ref=dst if src is None else src, dst_ref=dst,
                                                send_sem=send_sems.at[a, k], recv_sem=recv_sems.at[a, k],
                                                device_id=to, device_id_type=MESH_T)

        local = [pltpu.make_async_copy(ins[a], blk(a, *me), loc_sems.at[a]) for a in range(n)]
        for cp in local:
            cp.start()
        first = []
        for a in range(n):
            first.append(copy(a, 0, me, sibling, src=ins[a]))
            first += [copy(a, 1 + j, me, (*chip, c), src=ins[a]) for j, chip in enumerate(chips)]
        for cp in first:
            cp.start()
        passed = []
        for a in range(n):
            for j, chip in enumerate(chips):
                copy(a, 1 + j, (*chip, c), me).wait_recv()
                fwd = copy(a, 4 + j, (*chip, c), sibling)
                fwd.start()
                passed.append(fwd)
        for a in range(n):
            copy(a, 0, sibling, me).wait_recv()
            for j, chip in enumerate(chips):
                copy(a, 4 + j, (*chip, 1 - c), me).wait_recv()
        for cp in first + passed:
            cp.wait_send()
        for cp in local:
            cp.wait()

    return _pcall(
        body, in_specs=[ANY_SPEC] * n, out_specs=[ANY_SPEC] * n,
        out_shape=[jax.ShapeDtypeStruct((N_DEV,) + a.shape, a.dtype) for a in arrs],
        scratch_shapes=[pltpu.SemaphoreType.DMA((n, 7)), pltpu.SemaphoreType.DMA((n, 7)), pltpu.SemaphoreType.DMA((n,))],
        name=name)(*arrs)


def _pair_exchange(arrs, *, name):
    n = len(arrs)

    def body(*refs):
        ins, outs = refs[:n], refs[n:2 * n]
        send_sems, recv_sems = refs[2 * n:]
        x, y, c = _mesh_pos()
        copies = []
        for a in range(n):
            for k in range(4):
                copies.append(pltpu.make_async_remote_copy(
                    src_ref=ins[a].at[2 * k + (1 - c)], dst_ref=outs[a].at[k],
                    send_sem=send_sems.at[a, k], recv_sem=recv_sems.at[a, k],
                    device_id=(x, y, 1 - c), device_id_type=MESH_T))
        for cp in copies:
            cp.start()
        for cp in copies:
            cp.wait()

    return _pcall(
        body, in_specs=[ANY_SPEC] * n, out_specs=[ANY_SPEC] * n,
        out_shape=[jax.ShapeDtypeStruct((4,) + a.shape[1:], a.dtype) for a in arrs],
        scratch_shapes=[pltpu.SemaphoreType.DMA((n, 4)), pltpu.SemaphoreType.DMA((n, 4))],
        name=name)(*arrs)


def _chip_exchange(arrs, *, name):
    n = len(arrs)

    def body(*refs):
        ins, outs = refs[:n], refs[n:2 * n]
        send_sems, recv_sems, loc_sems = refs[2 * n:]
        x, y, c = _mesh_pos()
        kme = 2 * x + y
        chips = [(1 - x, y), (x, 1 - y), (1 - x, 1 - y)]
        local = [pltpu.make_async_copy(ins[a].at[kme], outs[a].at[kme], loc_sems.at[a]) for a in range(n)]
        for cp in local:
            cp.start()
        sends, recvs = [], []
        for a in range(n):
            for j, (tx, ty) in enumerate(chips):
                kt = 2 * tx + ty
                sends.append(pltpu.make_async_remote_copy(
                    src_ref=ins[a].at[kt], dst_ref=outs[a].at[kme],
                    send_sem=send_sems.at[a, j], recv_sem=recv_sems.at[a, j],
                    device_id=(tx, ty, c), device_id_type=MESH_T))
                recvs.append(pltpu.make_async_remote_copy(
                    src_ref=ins[a].at[kme], dst_ref=outs[a].at[kt],
                    send_sem=send_sems.at[a, j], recv_sem=recv_sems.at[a, j],
                    device_id=(tx, ty, c), device_id_type=MESH_T))
        for cp in sends:
            cp.start()
        for cp in recvs:
            cp.wait_recv()
        for cp in sends:
            cp.wait_send()
        for cp in local:
            cp.wait()

    return _pcall(
        body, in_specs=[ANY_SPEC] * n, out_specs=[ANY_SPEC] * n,
        out_shape=[jax.ShapeDtypeStruct(a.shape, a.dtype) for a in arrs],
        scratch_shapes=[pltpu.SemaphoreType.DMA((n, 3)), pltpu.SemaphoreType.DMA((n, 3)), pltpu.SemaphoreType.DMA((n,))],
        name=name)(*arrs)


def _pair_add(g8, r4, cidx, *, name):
    _, r, c = g8.shape
    tr = min(r, ROW_TILE)

    def body(c_ref, g_ref, r_ref, o_ref):
        o_ref[...] = (g_ref[...] + r_ref[...]).astype(BF16)

    return _pcall(
        body,
        grid_spec=pltpu.PrefetchScalarGridSpec(
            num_scalar_prefetch=1, grid=(4, r // tr),
            in_specs=[pl.BlockSpec((None, tr, c), lambda k, i, cr: (2 * k + cr[0], i, 0)),
                      pl.BlockSpec((None, tr, c), lambda k, i, cr: (k, i, 0))],
            out_specs=pl.BlockSpec((None, tr, c), lambda k, i, cr: (k, i, 0))),
        out_shape=jax.ShapeDtypeStruct((4, r, c), BF16), name=name,
        compiler_params=_cparams(("parallel", "parallel")))(cidx, g8, r4)


def _adam_update(g, w_ref, m_ref, v_ref, g_ref, d_ref, mo_ref, vo_ref):
    c1 = 1.0 - ADAM_B1 ** ADAM_STEP
    c2 = 1.0 - ADAM_B2 ** ADAM_STEP
    m2 = ADAM_B1 * m_ref[...] + (1.0 - ADAM_B1) * g
    v2 = ADAM_B2 * v_ref[...] + (1.0 - ADAM_B2) * (g * g)
    g_ref[...] = g
    mo_ref[...] = m2
    vo_ref[...] = v2
    d_ref[...] = -ADAM_LR * ((m2 / c1) / (jnp.sqrt(v2 / c2) + ADAM_EPS) + ADAM_WD * w_ref[...])


def _adamw_rows(srcs, items, own_cols, me1, *, name):
    ns, ni, no = len(srcs), len(items), len(own_cols)
    full = lambda a: pl.BlockSpec(a.shape, lambda i, me: (0,) * a.ndim)
    in_specs = [full(a) for a in srcs]
    args = list(srcs)
    for (si, _r0, w, _m, _v) in own_cols:
        a = srcs[si]
        in_specs.append(pl.BlockSpec((N_DEV, a.shape[1], w.shape[1]), lambda i, me: (0, 0, me[0])))
        args.append(a)
    out_specs, out_shape = [], []
    for (_si, _r0, w, m, v) in list(items) + list(own_cols):
        in_specs += [full(w)] * 3
        args += [w, m, v]
        out_specs += [full(w)] * 4
        out_shape += [jax.ShapeDtypeStruct(w.shape, F32)] * 4

    def body(me_ref, *refs):
        src_refs, own_refs = refs[:ns], refs[ns:ns + no]
        wmv = refs[ns + no:ns + no + 3 * (ni + no)]
        outs = refs[ns + no + 3 * (ni + no):]
        for q, (si, r0, w, _m, _v) in enumerate(list(items) + list(own_cols)):
            nr, cw = w.shape
            gref = src_refs[si] if q < ni else own_refs[q - ni]
            g = gref[0, r0:r0 + nr, 0:cw]
            for d in range(1, N_DEV):
                g = g + gref[d, r0:r0 + nr, 0:cw]
            _adam_update(g, *wmv[3 * q:3 * q + 3], *outs[4 * q:4 * q + 4])

    res = _pcall(
        body,
        grid_spec=pltpu.PrefetchScalarGridSpec(num_scalar_prefetch=1, grid=(1,), in_specs=in_specs, out_specs=out_specs),
        out_shape=out_shape, name=name, compiler_params=_cparams(("arbitrary",)))(me1, *args)
    return [tuple(res[4 * q:4 * q + 4]) for q in range(ni + no)]


def _adamw(gsrc, w, m, v, *, name):
    k, r, c = gsrc.shape
    tr = ROW_TILE if r % ROW_TILE == 0 else r

    def body(gs_ref, w_ref, m_ref, v_ref, g_ref, d_ref, mo_ref, vo_ref):
        g = gs_ref[0].astype(F32)
        for q in range(1, k):
            g = g + gs_ref[q].astype(F32)
        _adam_update(g, w_ref, m_ref, v_ref, g_ref, d_ref, mo_ref, vo_ref)

    row = pl.BlockSpec((tr, c), lambda i: (i, 0))
    sd = jax.ShapeDtypeStruct((r, c), F32)
    return _pcall(body, grid=(r // tr,), in_specs=[pl.BlockSpec((k, tr, c), lambda i: (0, i, 0)), row, row, row],
                  out_specs=(row, row, row, row), out_shape=(sd, sd, sd, sd), name=name,
                  compiler_params=_cparams(("parallel",)))(gsrc, w, m, v)


WEIGHTS = ['w_in', 'lru_conv_w', 'lru_conv_b', 'lru_gate_a_w', 'lru_gate_a_b', 'lru_gate_x_w', 'lru_gate_x_b',
           'lru_a_param', 'ssd_conv_w', 'ssd_conv_b', 'ssd_dt_bias', 'ssd_a_log', 'ssd_d', 'ssd_norm_w', 'w_out',
           'ln1_g', 'ln1_b', 'w_ff1', 'w_ff2', 'ln2_g', 'ln2_b', 'w_ple_gate', 'w_ple', 'ln3_g', 'ln3_b']
BIG = ['w_in', 'w_out', 'w_ff1', 'w_ff2', 'w_ple_gate', 'w_ple']
COL_SHARDED = ('w_in', 'w_ff1', 'w_ple')
CONV = ['lru_conv_w', 'ssd_conv_w']
REPL = [n for n in WEIGHTS if n not in BIG and n not in CONV]
CONV_CH = {'lru_conv_w': LRU_W, 'ssd_conv_w': XBC}


def _to_dest_major(name, gfull):
    if name == 'w_in':
        gfull = gfull[:, :D_IN]
    if name in COL_SHARDED:
        r, cfull = gfull.shape
        return gfull.reshape(r, N_DEV, cfull // N_DEV).transpose(1, 0, 2)
    rfull, cdim = gfull.shape
    return gfull.reshape(N_DEV, rfull // N_DEV, cdim)


def _from_gathered(name, gathered):
    if name in COL_SHARDED:
        _, r, cs = gathered.shape
        return gathered.transpose(1, 0, 2).reshape(r, N_DEV * cs)
    _, rs, cdim = gathered.shape
    return gathered.reshape(N_DEV * rs, cdim)


def kernel(x, p, w_in, lru_conv_w, lru_conv_b, lru_gate_a_w, lru_gate_a_b, lru_gate_x_w, lru_gate_x_b, lru_a_param, ssd_conv_w, ssd_conv_b, ssd_dt_bias, ssd_a_log, ssd_d, ssd_norm_w, w_out, ln1_g, ln1_b, w_ff1, w_ff2, ln2_g, ln2_b, w_ple_gate, w_ple, ln3_g, ln3_b, loss_target, m_w_in, m_lru_conv_w, m_lru_conv_b, m_lru_gate_a_w, m_lru_gate_a_b, m_lru_gate_x_w, m_lru_gate_x_b, m_lru_a_param, m_ssd_conv_w, m_ssd_conv_b, m_ssd_dt_bias, m_ssd_a_log, m_ssd_d, m_ssd_norm_w, m_w_out, m_ln1_g, m_ln1_b, m_w_ff1, m_w_ff2, m_ln2_g, m_ln2_b, m_w_ple_gate, m_w_ple, m_ln3_g, m_ln3_b, v_w_in, v_lru_conv_w, v_lru_conv_b, v_lru_gate_a_w, v_lru_gate_a_b, v_lru_gate_x_w, v_lru_gate_x_b, v_lru_a_param, v_ssd_conv_w, v_ssd_conv_b, v_ssd_dt_bias, v_ssd_a_log, v_ssd_d, v_ssd_norm_w, v_w_out, v_ln1_g, v_ln1_b, v_w_ff1, v_w_ff2, v_ln2_g, v_ln2_b, v_w_ple_gate, v_w_ple, v_ln3_g, v_ln3_b):
    given = dict(locals())
    wsh = {n: given[n][0] for n in WEIGHTS}
    msh = {n: given["m_" + n][0] for n in WEIGHTS}
    vsh = {n: given["v_" + n][0] for n in WEIGHTS}
    xi, yi, ci = _mesh_pos()
    me = 4 * xi + 2 * yi + ci

    conv_pack = jnp.concatenate([_pad_rows8(wsh[n]) for n in CONV], axis=1)
    gathered = _all_gather([wsh[n].astype(BF16) for n in BIG] + [conv_pack], name="ag_weights")
    full = {n: _from_gathered(n, ga) for n, ga in zip(BIG, gathered[:-1])}
    full['w_in'] = jnp.concatenate([full['w_in'], jnp.zeros((D_MODEL, D_IN_PAD - D_IN), BF16)], axis=1)
    gconv = gathered[-1]
    c0 = 0
    for n in CONV:
        cw = CONV_CH[n] // N_DEV
        full[n] = gconv[:, :4, c0:c0 + cw].transpose(1, 0, 2).reshape(4, CONV_CH[n])
        c0 += cw
    for n in REPL:
        full[n] = wsh[n].reshape(1, -1) if wsh[n].ndim == 1 else wsh[n]

    loss_local, grad_x, g, raw = _local_step(x[0], p[0, 0], loss_target[0], full)
    loss = lax.psum(loss_local, ("x", "y", "c"))

    dest = [g[n] if g[n].ndim == 3 else _to_dest_major(n, g[n]) for n in BIG]
    sib = _pair_exchange(dest, name="rs_pair_exchange")
    cidx = jnp.reshape(ci, (1,)).astype(jnp.int32)
    part = [_pair_add(d8, r4, cidx, name="rs_pair_add_" + n) for n, d8, r4 in zip(BIG, dest, sib)]
    summed = _chip_exchange(part, name="rs_chip_exchange")

    src_names = ("lru", "ssd", "heads", "rows", "gate_a", "gate_x")
    gat = dict(zip(src_names, _all_gather([raw[k] for k in src_names], name="ag_small_grads")))

    outs = {}
    for n, s4 in zip(BIG, summed):
        outs[n] = _adamw(s4, wsh[n], msh[n], vsh[n], name="adamw_" + n)
    for n, k in (("lru_gate_a_w", "gate_a"), ("lru_gate_x_w", "gate_x")):
        flat = lambda a: a.reshape(N_HEAD * HEAD_P, HEAD_P)
        res = _adamw(gat[k], flat(wsh[n]), flat(msh[n]), flat(vsh[n]), name="adamw_" + n)
        outs[n] = tuple(r.reshape(N_HEAD, HEAD_P, HEAD_P) for r in res)
    row_items = [("lru_conv_b", 0, 4), ("lru_gate_a_b", 0, 5), ("lru_gate_x_b", 0, 6), ("lru_a_param", 0, 7),
                 ("ssd_conv_b", 1, 4), ("ssd_dt_bias", 2, 0), ("ssd_a_log", 2, 1), ("ssd_d", 2, 2),
                 ("ssd_norm_w", 3, 0), ("ln1_g", 3, 1), ("ln1_b", 3, 2), ("ln2_g", 3, 3), ("ln2_b", 3, 4),
                 ("ln3_g", 3, 5), ("ln3_b", 3, 6)]
    vec = lambda a: a.reshape(1, -1)
    items = [(si, r0, vec(wsh[n]), vec(msh[n]), vec(vsh[n])) for n, si, r0 in row_items]
    own = [(si, 0, wsh[n], msh[n], vsh[n]) for n, si in (("lru_conv_w", 0), ("ssd_conv_w", 1))]
    me1 = jnp.reshape(me, (1,)).astype(jnp.int32)
    res = _adamw_rows([gat[k] for k in src_names[:4]], items, own, me1, name="adamw_small")
    for (n, _si, _r0), r4 in zip(row_items, res[:len(row_items)]):
        outs[n] = tuple(r.reshape(wsh[n].shape) for r in r4)
    for n, r4 in zip(CONV, res[len(row_items):]):
        outs[n] = r4

    ex = lambda a: a[None]
    return (loss, grad_x[None],
            *[ex(outs[n][0]) for n in WEIGHTS], *[ex(outs[n][1]) for n in WEIGHTS],
            *[ex(outs[n][2]) for n in WEIGHTS], *[ex(outs[n][3]) for n in WEIGHTS])
```

```python
import math

import jax
import jax.numpy as jnp
from jax import lax
from jax.experimental import pallas as pl
from jax.experimental.pallas import tpu as pltpu

F32 = jnp.float32
BF16 = jnp.bfloat16
HI = lax.Precision.HIGHEST

N_DEV = 8
D_MODEL = 1024
LRU_W = 1024
SSD_W = 1024
XBC = 2048
N_HEAD = 16
HEAD_P = 64
N_GROUP = 4
GROUP_W = 256
N_STATE = 128
CHUNK = 128
D_FF = 4096
PLE_DIM = 256
D_IN = 5136
D_IN_PAD = 5632
COL_G = 1024
COL_Z = 2048
COL_XBC = 3072
COL_DT = 5120
LRU_C = 8.0
ALPHA = 2.0 ** 0.25
LN_EPS = 1e-5
RMS_EPS = 1e-5
ADAM_LR = 0.001
ADAM_B1 = 0.9
ADAM_B2 = 0.999
ADAM_EPS = 1e-08
ADAM_WD = 0.01
ADAM_STEP = 10
GELU_C = math.sqrt(2.0 / math.pi)
LANE = 128
SUBLANE = 8
VMEM_LIMIT = 48 * 1024 * 1024
MESH_T = pl.DeviceIdType.MESH
NEG_BIG = -1e30


def _pcall(body, **kw):
    return pl.pallas_call(body, **kw)


def _cparams(sem):
    return pltpu.CompilerParams(dimension_semantics=sem, vmem_limit_bytes=VMEM_LIMIT)


def _dot(a, b):
    return jnp.dot(a.astype(BF16), b.astype(BF16), preferred_element_type=F32)


def _dot_nt(a, b):
    return lax.dot_general(a.astype(BF16), b.astype(BF16), (((1,), (1,)), ((), ())), preferred_element_type=F32)


def _dot_tn(a, b):
    return lax.dot_general(a.astype(BF16), b.astype(BF16), (((0,), (0,)), ((), ())), preferred_element_type=F32)


def _dotx(a, b):
    return jnp.dot(a, b, precision=HI, preferred_element_type=F32)


def _sigmoid(x):
    return jax.nn.sigmoid(x)


def _softplus(v):
    return jnp.maximum(v, 0.0) + jnp.log1p(jnp.exp(-jnp.abs(v)))


def _gelu(x):
    th = jnp.tanh(GELU_C * (x + 0.044715 * x * x * x))
    return 0.5 * x * (1.0 + th), th


def _gelu_grad(x, th):
    return 0.5 * (1.0 + th) + 0.5 * x * (1.0 - th * th) * GELU_C * (1.0 + 3.0 * 0.044715 * x * x)


def _iota(shape, dim):
    return lax.broadcasted_iota(jnp.int32, shape, dim)


def _mm(a, b, mode, *, tm, tn, name, a_fn=None, extra=None, epi=None, out_dtype=F32, dest_major=False, jobs=()):
    m = a.shape[1] if mode == "tn" else a.shape[0]
    n = b.shape[0] if mode == "nt" else b.shape[1]
    tm, tn = min(tm, m), min(tn, n)
    if dest_major:
        tn = n // N_DEV
    if mode == "nn":
        m, k = a.shape
        _, n = b.shape
        a_spec = pl.BlockSpec((tm, k), lambda i, j: (i, 0))
        b_spec = pl.BlockSpec((k, tn), lambda i, j: (0, j))
        dims = ((1,), (0,))
    elif mode == "nt":
        m, k = a.shape
        n, _ = b.shape
        a_spec = pl.BlockSpec((tm, k), lambda i, j: (i, 0))
        b_spec = pl.BlockSpec((tn, k), lambda i, j: (j, 0))
        dims = ((1,), (1,))
    else:
        k, m = a.shape
        _, n = b.shape
        a_spec = pl.BlockSpec((k, tm), lambda i, j: (0, i))
        b_spec = pl.BlockSpec((k, tn), lambda i, j: (0, j))
        dims = ((0,), (0,))
    assert m % tm == 0 and n % tn == 0, (name, m, n, tm, tn)
    o_spec = pl.BlockSpec((tm, tn), lambda i, j: (i, j))
    in_specs = [a_spec, b_spec]
    args = [a, b]
    if extra is not None:
        in_specs.append(o_spec)
        args.append(extra)

    def body(*refs):
        a_ref, b_ref, o_ref = refs[0], refs[1], refs[-1]
        av = a_ref[...]
        if a_fn is not None:
            av = a_fn(av)
        acc = lax.dot_general(av.astype(BF16), b_ref[...].astype(BF16), (dims, ((), ())), preferred_element_type=F32)
        if epi is not None:
            acc = epi(acc, refs[2][...])
        o_ref[...] = acc.astype(out_dtype)

    out_shape = jax.ShapeDtypeStruct((m, n), out_dtype)
    if dest_major:
        assert extra is None
        o_spec = pl.BlockSpec((None, tm, tn), lambda i, j: (j, i, 0))
        out_shape = jax.ShapeDtypeStruct((N_DEV, m, tn), out_dtype)
    (out,), jouts = _hosted(body, jobs, grid=(m // tm, n // tn), in_specs=in_specs, out_specs=[o_spec],
                            out_shape=[out_shape], args=args, name=name)
    return (out, jouts) if jobs else out


def _relu2(v):
    r = jnp.maximum(v, 0.0)
    return r * r


ROW_TILE = 256


def _ln_stats(t):
    mu = jnp.mean(t, axis=-1, keepdims=True)
    xc = t - mu
    var = jnp.mean(xc * xc, axis=-1, keepdims=True)
    rstd = lax.rsqrt(var + LN_EPS)
    return xc * rstd, rstd


def _ln_bwd_rows(dy, xhat, rstd, g):
    dxh = dy * g
    m1 = jnp.mean(dxh, axis=-1, keepdims=True)
    m2 = jnp.mean(dxh * xhat, axis=-1, keepdims=True)
    return rstd * (dxh - m1 - xhat * m2)


def _ln_fwd(a, b, g, beta, *, name):
    s, d = a.shape
    row = pl.BlockSpec((ROW_TILE, d), lambda i: (i, 0))
    par = pl.BlockSpec((1, d), lambda i: (0, 0))

    def body(a_ref, b_ref, g_ref, be_ref, y_ref):
        xhat, _ = _ln_stats(ALPHA * a_ref[...] + b_ref[...])
        y_ref[...] = xhat * g_ref[...] + be_ref[...]

    return _pcall(body, grid=(s // ROW_TILE,), in_specs=[row, row, par, par], out_specs=row,
                  out_shape=jax.ShapeDtypeStruct((s, d), F32), name=name,
                  compiler_params=_cparams(("parallel",)))(a, b, g, beta)


def _ln_bwd(a, b, g, dys, coefs, *, name):
    s, d = a.shape
    row = pl.BlockSpec((ROW_TILE, d), lambda i: (i, 0))
    par = pl.BlockSpec((1, d), lambda i: (0, 0))
    n = len(dys)

    def body(*refs):
        a_ref, b_ref, g_ref = refs[:3]
        dy_refs = refs[3:3 + n]
        dt_ref, dg_ref, db_ref = refs[3 + n:]
        xhat, rstd = _ln_stats(ALPHA * a_ref[...] + b_ref[...])
        dy = coefs[0] * dy_refs[0][...]
        for q in range(1, n):
            dy = dy + coefs[q] * dy_refs[q][...]
        dt_ref[...] = _ln_bwd_rows(dy, xhat, rstd, g_ref[...])

        @pl.when(pl.program_id(0) == 0)
        def _():
            dg_ref[...] = jnp.zeros_like(dg_ref)
            db_ref[...] = jnp.zeros_like(db_ref)

        dg_ref[...] += jnp.sum(dy * xhat, axis=0, keepdims=True)
        db_ref[...] += jnp.sum(dy, axis=0, keepdims=True)

    return _pcall(body, grid=(s // ROW_TILE,), in_specs=[row, row, par] + [row] * n, out_specs=(row, par, par),
                  out_shape=(jax.ShapeDtypeStruct((s, d), F32), jax.ShapeDtypeStruct((1, d), F32),
                             jax.ShapeDtypeStruct((1, d), F32)),
                  name=name, compiler_params=_cparams(("arbitrary",)))(a, b, g, *dys)


def _head(x2, gpre, ple, g, beta, tgt, *, name):
    s, d = x2.shape
    row = pl.BlockSpec((ROW_TILE, d), lambda i: (i, 0))
    par = pl.BlockSpec((1, d), lambda i: (0, 0))
    lsp = pl.BlockSpec((1, LANE), lambda i: (0, 0))

    def body(x2_ref, gp_ref, ple_ref, g_ref, be_ref, t_ref, loss_ref, dgp_ref, dple_ref, dt_ref, dg_ref, db_ref):
        gate = _sigmoid(gp_ref[...])
        ple_v = ple_ref[...]
        xhat, rstd = _ln_stats(ALPHA * x2_ref[...] + gate * ple_v)
        err = xhat * g_ref[...] + be_ref[...] - t_ref[...]
        dy = err * (1.0 / d)
        dt = _ln_bwd_rows(dy, xhat, rstd, g_ref[...])
        dt_ref[...] = dt
        dgp_ref[...] = dt * ple_v * gate * (1.0 - gate)
        dple_ref[...] = dt * gate

        @pl.when(pl.program_id(0) == 0)
        def _():
            loss_ref[...] = jnp.zeros_like(loss_ref)
            dg_ref[...] = jnp.zeros_like(dg_ref)
            db_ref[...] = jnp.zeros_like(db_ref)

        loss_ref[...] += 0.5 * jnp.sum(jnp.mean(err * err, axis=-1, keepdims=True))
        dg_ref[...] += jnp.sum(dy * xhat, axis=0, keepdims=True)
        db_ref[...] += jnp.sum(dy, axis=0, keepdims=True)

    sd = jax.ShapeDtypeStruct((s, d), F32)
    pd = jax.ShapeDtypeStruct((1, d), F32)
    return _pcall(body, grid=(s // ROW_TILE,), in_specs=[row, row, row, par, par, row],
                  out_specs=(lsp, row, row, row, par, par),
                  out_shape=(jax.ShapeDtypeStruct((1, LANE), F32), sd, sd, sd, pd, pd),
                  name=name, compiler_params=_cparams(("arbitrary",)))(x2, gpre, ple, g, beta, tgt)


CONV_R = 256
PAD = SUBLANE


def _shift_down(ext, s):
    if s == 0:
        return ext[PAD:, :]
    return pltpu.roll(ext, s, 0)[PAD:, :]


def _shift_up(ext, s):
    r = ext.shape[0] - PAD
    if s == 0:
        return ext[:r, :]
    return pltpu.roll(ext, r + PAD - s, 0)[:r, :]


def _conv_rows(xpad_ref, r0, w_ref):
    ext = xpad_ref[pl.ds(r0, CONV_R + PAD), :]
    acc = _shift_down(ext, 0) * w_ref[3:4, :]
    for k in range(3):
        acc = acc + _shift_down(ext, 3 - k) * w_ref[k:k + 1, :]
    return acc, ext


def _fill_front_padded(dst_ref, src_ref, s):
    dst_ref[0:PAD, :] = jnp.zeros((PAD, dst_ref.shape[1]), F32)

    def cp(q, _):
        r0 = pl.multiple_of(q * CONV_R, CONV_R)
        dst_ref[pl.ds(pl.multiple_of(PAD + r0, PAD), CONV_R), :] = src_ref[pl.ds(r0, CONV_R), :]
        return 0

    lax.fori_loop(0, s // CONV_R, cp, 0)


def _conv_silu_fwd(proj, w8, b, *, col0, width, ct, name):
    s = proj.shape[0]
    nb = col0 // ct

    def body(x_ref, w_ref, b_ref, o_ref, xpad):
        _fill_front_padded(xpad, x_ref, s)

        def step(q, _):
            r0 = pl.multiple_of(q * CONV_R, CONV_R)
            acc, _e = _conv_rows(xpad, r0, w_ref)
            pre = acc + b_ref[...]
            o_ref[pl.ds(r0, CONV_R), :] = pre * _sigmoid(pre)
            return 0

        lax.fori_loop(0, s // CONV_R, step, 0)

    return _pcall(
        body, grid=(width // ct,),
        in_specs=[pl.BlockSpec((s, ct), lambda j: (0, nb + j)), pl.BlockSpec((SUBLANE, ct), lambda j: (0, j)),
                  pl.BlockSpec((1, ct), lambda j: (0, j))],
        out_specs=pl.BlockSpec((s, ct), lambda j: (0, j)),
        out_shape=jax.ShapeDtypeStruct((s, width), F32),
        scratch_shapes=[pltpu.VMEM((s + PAD, ct), F32)], name=name,
        compiler_params=_cparams(("parallel",)))(proj, w8, b)


def _conv_bwd_rows(dpad_ref, r0, w_ref):
    return _conv_bwd_ext(dpad_ref[pl.ds(r0, CONV_R + PAD), :], w_ref)


def _conv_bwd_ext(ext, w_ref):
    acc = _shift_up(ext, 0) * w_ref[3:4, :]
    for k in range(3):
        acc = acc + _shift_up(ext, 3 - k) * w_ref[k:k + 1, :]
    return acc


def _conv_silu_bwd(proj, dact, w8, b, *, col0, width, ct, name):
    s = proj.shape[0]
    nb = col0 // ct

    def body(x_ref, d_ref, w_ref, b_ref, dx_ref, dwb_ref, xpad, dpad):
        _fill_front_padded(xpad, x_ref, s)
        dpad[pl.ds(s, PAD), :] = jnp.zeros((PAD, ct), F32)
        dwb_ref[...] = jnp.zeros_like(dwb_ref)

        def step(q, _):
            r0 = pl.multiple_of(q * CONV_R, CONV_R)
            acc, ext = _conv_rows(xpad, r0, w_ref)
            pre = acc + b_ref[...]
            sg = _sigmoid(pre)
            dpre = d_ref[pl.ds(r0, CONV_R), :] * sg * (1.0 + pre * (1.0 - sg))
            dpad[pl.ds(r0, CONV_R), :] = dpre
            for k in range(4):
                dwb_ref[k:k + 1, :] += jnp.sum(dpre * _shift_down(ext, 3 - k), axis=0, keepdims=True)
            dwb_ref[4:5, :] += jnp.sum(dpre, axis=0, keepdims=True)
            return 0

        lax.fori_loop(0, s // CONV_R, step, 0)

        def step2(q, _):
            r0 = pl.multiple_of(q * CONV_R, CONV_R)
            dx_ref[pl.ds(r0, CONV_R), :] = _conv_bwd_rows(dpad, r0, w_ref)
            return 0

        lax.fori_loop(0, s // CONV_R, step2, 0)

    colb = pl.BlockSpec((s, ct), lambda j: (0, j))
    return _pcall(
        body, grid=(width // ct,),
        in_specs=[pl.BlockSpec((s, ct), lambda j: (0, nb + j)), colb, pl.BlockSpec((SUBLANE, ct), lambda j: (0, j)),
                  pl.BlockSpec((1, ct), lambda j: (0, j))],
        out_specs=(colb, pl.BlockSpec((SUBLANE, ct), lambda j: (0, j))),
        out_shape=(jax.ShapeDtypeStruct((s, width), F32), jax.ShapeDtypeStruct((SUBLANE, width), F32)),
        scratch_shapes=[pltpu.VMEM((s + PAD, ct), F32), pltpu.VMEM((s + PAD, ct), F32)], name=name,
        compiler_params=_cparams(("parallel",)))(proj, dact, w8, b)


LRU_CT = 128


def _row_of(v, r):
    return jnp.sum(jnp.where(_iota((v.shape[0], 1), 0) == r, v, 0.0), axis=0, keepdims=True)


def _scan_fwd(a, u):
    r = a.shape[0]
    row = _iota((r, 1), 0)
    d = 1
    while d < r:
        valid = row >= d
        u = jnp.where(valid, a * pltpu.roll(u, d, 0) + u, u)
        a = jnp.where(valid, a * pltpu.roll(a, d, 0), a)
        d *= 2
    return a, u


def _scan_rev(b, u):
    r = b.shape[0]
    row = _iota((r, 1), 0)
    d = 1
    while d < r:
        valid = row < r - d
        u = jnp.where(valid, b * pltpu.roll(u, r - d, 0) + u, u)
        b = jnp.where(valid, b * pltpu.roll(b, r - d, 0), b)
        d *= 2
    return b, u


def _lru_chunk(xpad, r0, cw_ref, cb, wa, ba, wx, bx, sp):
    acc, ext = _conv_rows(xpad, r0, cw_ref)
    xl = acc + cb
    r = _sigmoid(_dot(xl, wa) + ba)
    i = _sigmoid(_dot(xl, wx) + bx)
    la = -LRU_C * r * sp
    a = jnp.exp(la)
    a2 = jnp.exp(2.0 * la)
    mult = jnp.sqrt(-jnp.tanh(la) * (a2 + 1.0))
    first = (r0 + _iota((CONV_R, 1), 0)) == 0
    mult = jnp.where(first, 1.0, mult)
    return ext, xl, r, i, a, a2, mult, first


def _lru_specs(s):
    ct = LRU_CT
    nb_g = COL_G // ct
    return dict(
        x=pl.BlockSpec((s, ct), lambda j: (0, j)),
        g=pl.BlockSpec((s, ct), lambda j: (0, nb_g + j)),
        col=pl.BlockSpec((s, ct), lambda j: (0, j)),
        cw=pl.BlockSpec((SUBLANE, ct), lambda j: (0, j)),
        vec=pl.BlockSpec((1, ct), lambda j: (0, j)),
        gate=pl.BlockSpec((None, ct, ct), lambda j: (j, 0, 0)),
    )


def _lru_fwd(proj, cw8, cb, wa_bd, ba, wx_bd, bx, ap, *, name, jobs=()):
    s = proj.shape[0]
    ct = LRU_CT
    sp_ = _lru_specs(s)

    def body(x_ref, g_ref, cw_ref, cb_ref, wa_ref, ba_ref, wx_ref, bx_ref, ap_ref, y_ref, h_ref, xpad):
        _fill_front_padded(xpad, x_ref, s)
        sp = _softplus(-ap_ref[...])

        def step(q, carry):
            r0 = pl.multiple_of(q * CONV_R, CONV_R)
            _e, xl, _r, i, a, _a2, mult, _f = _lru_chunk(xpad, r0, cw_ref, cb_ref[...], wa_ref[...], ba_ref[...],
                                                       wx_ref[...], bx_ref[...], sp)
            acum, ucum = _scan_fwd(a, xl * i * mult)
            h = acum * carry + ucum
            h_ref[pl.ds(r0, CONV_R), :] = h
            ge, _th = _gelu(g_ref[pl.ds(r0, CONV_R), :])
            y_ref[pl.ds(r0, CONV_R), :] = ge * h
            return _row_of(h, CONV_R - 1)

        lax.fori_loop(0, s // CONV_R, step, jnp.zeros((1, ct), F32))

    (ymix, hs), jouts = _hosted(
        body, jobs, grid=(LRU_W // ct,),
        in_specs=[sp_["x"], sp_["g"], sp_["cw"], sp_["vec"], sp_["gate"], sp_["vec"], sp_["gate"], sp_["vec"], sp_["vec"]],
        out_specs=(sp_["col"], sp_["col"]),
        out_shape=(jax.ShapeDtypeStruct((s, LRU_W + SSD_W), F32), jax.ShapeDtypeStruct((s, LRU_W), F32)),
        scratch_shapes=[pltpu.VMEM((s + PAD, ct), F32)],
        name=name, args=(proj, proj, cw8, cb, wa_bd, ba, wx_bd, bx, ap))
    return ((ymix, hs), jouts) if jobs else (ymix, hs)


def _lru_bwd(proj, dy, hs, cw8, cb, wa_bd, ba, wx_bd, bx, ap, *, name, jobs=()):
    s = proj.shape[0]
    ct = LRU_CT
    sp_ = _lru_specs(s)

    nq = s // CONV_R

    def body(x_ref, g_ref, dy_ref, h_ref, cw_ref, cb_ref, wa_ref, ba_ref, wx_ref, bx_ref, ap_ref,
             dx_ref, dg_ref, dcwb_ref, dwa_ref, dwx_ref, xpad, hpad):
        _fill_front_padded(xpad, x_ref, s)
        _fill_front_padded(hpad, h_ref, s)
        apv = ap_ref[...]
        sp = _softplus(-apv)
        cb_v, wa, ba_v, wx, bx_v = cb_ref[...], wa_ref[...], ba_ref[...], wx_ref[...], bx_ref[...]
        dcwb_ref[...] = jnp.zeros_like(dcwb_ref)
        dwa_ref[...] = jnp.zeros_like(dwa_ref)
        dwx_ref[...] = jnp.zeros_like(dwx_ref)

        def back(k, carry):
            g_next, a_next, dxl_next = carry
            last_row = _iota((CONV_R, 1), 0) == CONV_R - 1
            r0 = pl.multiple_of((nq - 1 - k) * CONV_R, CONV_R)
            ext, xl, r, i, a, a2, mult, first = _lru_chunk(xpad, r0, cw_ref, cb_v, wa, ba_v, wx, bx_v, sp)
            gv = g_ref[pl.ds(r0, CONV_R), :]
            dyv = dy_ref[pl.ds(r0, CONV_R), :]
            hext = hpad[pl.ds(r0, CONV_R + PAD), :]
            ge, th = _gelu(gv)
            dg_ref[pl.ds(r0, CONV_R), :] = dyv * _shift_down(hext, 0) * _gelu_grad(gv, th)
            b = jnp.where(last_row, a_next, pltpu.roll(a, CONV_R - 1, 0))
            bcum, dcum = _scan_rev(b, dyv * ge)
            gval = dcum + bcum * g_next
            hprev = _shift_down(hext, 1)
            da = gval * hprev
            dxl = gval * i * mult
            di = gval * xl * mult
            dmult = jnp.where(first, 0.0, gval * xl * i)
            dla = da * a - dmult * a2 / mult
            dr = dla * (-LRU_C) * sp
            dcwb_ref[7:8, :] += jnp.sum(dla * (-LRU_C) * r, axis=0, keepdims=True)
            dpr = dr * r * (1.0 - r)
            dpi = di * i * (1.0 - i)
            dxl = dxl + _dot_nt(dpr, wa) + _dot_nt(dpi, wx)
            dwa_ref[...] += _dot_tn(xl, dpr)
            dwx_ref[...] += _dot_tn(xl, dpi)
            dcwb_ref[5:6, :] += jnp.sum(dpr, axis=0, keepdims=True)
            dcwb_ref[6:7, :] += jnp.sum(dpi, axis=0, keepdims=True)
            for tap in range(4):
                dcwb_ref[tap:tap + 1, :] += jnp.sum(dxl * _shift_down(ext, 3 - tap), axis=0, keepdims=True)
            dcwb_ref[4:5, :] += jnp.sum(dxl, axis=0, keepdims=True)
            dx_ref[pl.ds(r0, CONV_R), :] = _conv_bwd_ext(jnp.concatenate([dxl, dxl_next], axis=0), cw_ref)
            return _row_of(gval, 0), _row_of(a, 0), dxl[:PAD, :]

        zero = jnp.zeros((1, ct), F32)
        lax.fori_loop(0, nq, back, (zero, zero, jnp.zeros((PAD, ct), F32)))
        dcwb_ref[7:8, :] = dcwb_ref[7:8, :] * (-_sigmoid(-apv))

    nt = LRU_W // ct
    outs, jouts = _hosted(
        body, jobs, grid=(nt,),
        in_specs=[sp_["x"], sp_["g"], sp_["col"], sp_["col"], sp_["cw"], sp_["vec"], sp_["gate"], sp_["vec"], sp_["gate"],
                  sp_["vec"], sp_["vec"]],
        out_specs=(sp_["col"], sp_["col"], sp_["cw"], sp_["gate"], sp_["gate"]),
        out_shape=(jax.ShapeDtypeStruct((s, LRU_W), F32), jax.ShapeDtypeStruct((s, LRU_W), F32),
                   jax.ShapeDtypeStruct((SUBLANE, LRU_W), F32), jax.ShapeDtypeStruct((nt, ct, ct), F32),
                   jax.ShapeDtypeStruct((nt, ct, ct), F32)),
        scratch_shapes=[pltpu.VMEM((s + PAD, ct), F32), pltpu.VMEM((s + PAD, ct), F32)],
        name=name, args=(proj, proj, dy, hs, cw8, cb, wa_bd, ba, wx_bd, bx, ap))
    return (tuple(outs), jouts) if jobs else tuple(outs)


def _ssd_prep(dtr, bias, alog_pad, alogx):
    l = CHUNK
    lane = _iota((1, LANE), 1)
    a_head = jnp.where(lane < N_HEAD, -jnp.exp(alog_pad), 0.0)
    dt = _softplus(dtr + bias)
    tril = (_iota((l, l), 1) <= _iota((l, l), 0)).astype(F32)
    cs = _dotx(tril, dt * a_head)
    expand = (jnp.right_shift(_iota((LANE, SSD_W), 1), 6) == _iota((LANE, SSD_W), 0)).astype(F32)
    dtx = _dotx(dt, expand)
    ax = -jnp.exp(alogx)
    csx = _dotx(tril, dtx * ax)
    totx = jnp.sum(dtx * ax, axis=0, keepdims=True)
    return dict(a_head=a_head, dt=dt, tril=tril, cs=cs, expand=expand, dtx=dtx, ax=ax, csx=csx, totx=totx)


def _decay_mat(cs, cst_ref, h, causal):
    lane = _iota((CHUNK, LANE), 1)
    col = jnp.sum(jnp.where(lane == h, cs, 0.0), axis=1, keepdims=True)
    row = cst_ref[h:h + 1, :]
    return jnp.exp(jnp.where(causal, col - row, NEG_BIG))


def _head_mask(j):
    lane = _iota((CHUNK, GROUP_W), 1)
    return (lane >= j * HEAD_P) & (lane < (j + 1) * HEAD_P)


def _ssd_group_fwd(q, g, xs_g, bg, cg, ht_g, cst_ref, causal, dx_g):
    sl = slice(g * GROUP_W, (g + 1) * GROUP_W)
    dtx_g, csx_g, totx_g = q["dtx"][:, sl], q["csx"][:, sl], q["totx"][:, sl]
    xdt = xs_g * dtx_g
    ex = jnp.exp(csx_g)
    cb = _dot_nt(cg, bg)
    yoff = _dot(cg, ht_g) * ex
    ydiag = jnp.zeros((CHUNK, GROUP_W), F32)
    for j in range(4):
        sc = cb * _decay_mat(q["cs"], cst_ref, 4 * g + j, causal)
        ydiag = jnp.where(_head_mask(j), _dot(sc, xdt), ydiag)
    y = ydiag + yoff + xs_g * dx_g
    dsx = jnp.exp(totx_g - csx_g)
    return y, dict(xdt=xdt, ex=ex, cb=cb, yoff=yoff, dsx=dsx, dtx=dtx_g, totx=totx_g)


def _gated_norm_fwd(y_g, z_g, w_g):
    sz = _sigmoid(z_g)
    silu = z_g * sz
    yf = y_g * silu
    rs = lax.rsqrt(jnp.mean(yf * yf, axis=1, keepdims=True) + RMS_EPS)
    yn = yf * rs
    return yn * w_g, (sz, silu, rs, yn)


def _ssd_fwd(xact, proj, ymix, bias_pad, alog_pad, alogx, dxp, normw, *, name, jobs=()):
    s = xact.shape[0]
    nc = s // CHUNK

    def body(xa_ref, dt_ref, z_ref, _ymix_ref, bias_ref, alp_ref, alx_ref, dx_ref, nw_ref, y_ref, hp_ref, ht, cst):
        @pl.when(pl.program_id(0) == 0)
        def _():
            ht[...] = jnp.zeros_like(ht)

        hp_ref[...] = ht[...]
        q = _ssd_prep(dt_ref[...], bias_ref[...], alp_ref[...], alx_ref[...])
        cst[...] = q["cs"].T
        causal = q["tril"] > 0.0
        for g in range(N_GROUP):
            sl = slice(g * GROUP_W, (g + 1) * GROUP_W)
            xs_g = xa_ref[:, sl]
            bg = xa_ref[:, SSD_W + g * N_STATE:SSD_W + (g + 1) * N_STATE]
            cg = xa_ref[:, SSD_W + N_GROUP * N_STATE + g * N_STATE:SSD_W + N_GROUP * N_STATE + (g + 1) * N_STATE]
            ht_g = ht[:, sl]
            y, f = _ssd_group_fwd(q, g, xs_g, bg, cg, ht_g, cst, causal, dx_ref[:, sl])
            out, _ = _gated_norm_fwd(y, z_ref[:, sl], nw_ref[:, sl])
            y_ref[:, sl] = out
            ht[:, sl] = jnp.exp(f["totx"]) * ht_g + _dot_tn(bg, f["xdt"] * f["dsx"])

    par = lambda w: pl.BlockSpec((1, w), lambda c: (0, 0))
    (ycat, hprev), jouts = _hosted(
        body, jobs, grid=(nc,),
        in_specs=[pl.BlockSpec((CHUNK, XBC), lambda c: (c, 0)),
                  pl.BlockSpec((CHUNK, LANE), lambda c: (c, COL_DT // LANE)),
                  pl.BlockSpec((CHUNK, SSD_W), lambda c: (c, COL_Z // SSD_W)),
                  ANY_SPEC, par(LANE), par(LANE), par(SSD_W), par(SSD_W), par(SSD_W)],
        out_specs=(pl.BlockSpec((CHUNK, SSD_W), lambda c: (c, LRU_W // SSD_W)),
                   pl.BlockSpec((None, N_STATE, SSD_W), lambda c: (c, 0, 0))),
        out_shape=(jax.ShapeDtypeStruct(ymix.shape, F32), jax.ShapeDtypeStruct((nc, N_STATE, SSD_W), F32)),
        scratch_shapes=[pltpu.VMEM((N_STATE, SSD_W), F32), pltpu.VMEM((CHUNK, LANE), F32)],
        aliases={3: 0}, name=name, args=(xact, proj, proj, ymix, bias_pad, alog_pad, alogx, dxp, normw))
    return ((ycat, hprev), jouts) if jobs else (ycat, hprev)


def _ssd_bwd(xact, proj, dycat, hprev, bias_pad, alog_pad, alogx, dxp, normw, *, name):
    s = xact.shape[0]
    nc = s // CHUNK
    l = CHUNK

    def body(xa_ref, dt_ref, z_ref, dy_ref, hp_ref, bias_ref, alp_ref, alx_ref, dx_ref, nw_ref,
             dxa_ref, ddt_ref, dz_ref, dnw_ref, small_ref, dht, cst, accx, dcsx_s, ddtx_s):
        step = pl.program_id(0)

        @pl.when(step == 0)
        def _():
            dht[...] = jnp.zeros_like(dht)
            accx[...] = jnp.zeros_like(accx)
            dnw_ref[...] = jnp.zeros_like(dnw_ref)
            small_ref[...] = jnp.zeros_like(small_ref)

        dtr = dt_ref[...]
        q = _ssd_prep(dtr, bias_ref[...], alp_ref[...], alx_ref[...])
        cst[...] = q["cs"].T
        causal = q["tril"] > 0.0
        eye = _iota((l, l), 0) == _iota((l, l), 1)
        lane = _iota((l, LANE), 1)
        dcs_head = jnp.zeros((l, LANE), F32)
        for g in range(N_GROUP):
            sl = slice(g * GROUP_W, (g + 1) * GROUP_W)
            slb = slice(SSD_W + g * N_STATE, SSD_W + (g + 1) * N_STATE)
            slc = slice(SSD_W + N_GROUP * N_STATE + g * N_STATE, SSD_W + N_GROUP * N_STATE + (g + 1) * N_STATE)
            xs_g, bg, cg = xa_ref[:, sl], xa_ref[:, slb], xa_ref[:, slc]
            ht_g = hp_ref[:, sl]
            dxp_g = dx_ref[:, sl]
            y, f = _ssd_group_fwd(q, g, xs_g, bg, cg, ht_g, cst, causal, dxp_g)
            z_g, nw_g = z_ref[:, sl], nw_ref[:, sl]
            _o, (sz, silu, rs, yn) = _gated_norm_fwd(y, z_g, nw_g)
            dout = dy_ref[:, sl]
            dnw_ref[:, sl] += jnp.sum(dout * yn, axis=0, keepdims=True)
            dyn = dout * nw_g
            dyf = rs * (dyn - yn * jnp.mean(dyn * yn, axis=1, keepdims=True))
            dy = dyf * silu
            dz_ref[:, sl] = dyf * y * sz * (1.0 + z_g * (1.0 - sz))
            accx[0:1, sl] += jnp.sum(dy * xs_g, axis=0, keepdims=True)
            dyo = dy * f["ex"]
            dcg = _dot_nt(dyo, ht_g)
            dht_prev = _dot_tn(cg, dyo)
            dcsx = dy * f["yoff"]
            xdt = f["xdt"]
            dxdt = jnp.zeros((l, GROUP_W), F32)
            dcb = jnp.zeros((l, l), F32)
            for j in range(4):
                h = 4 * g + j
                lm = _decay_mat(q["cs"], cst, h, causal)
                sc = f["cb"] * lm
                mask = _head_mask(j)
                ds_ = jnp.where(causal, _dot_nt(jnp.where(mask, dy, 0.0), xdt), 0.0)
                dxdt = jnp.where(mask, _dot_tn(sc, dy), dxdt)
                dcb = dcb + ds_ * lm
                m = ds_ * sc
                rsum = jnp.sum(m, axis=1, keepdims=True)
                csum = jnp.sum(m, axis=0, keepdims=True)
                csum_col = jnp.sum(jnp.where(eye, csum, 0.0), axis=1, keepdims=True)
                dcs_head = dcs_head + jnp.where(lane == h, rsum - csum_col, 0.0)
            dhn = dht[:, sl]
            etot = jnp.exp(f["totx"])
            dxd = _dot(bg, dhn)
            dbg = _dot_nt(xdt * f["dsx"], dhn)
            dxdt = dxdt + dxd * f["dsx"]
            qq = dxd * xdt * f["dsx"]
            dcsx = dcsx - qq
            dtot = jnp.sum(qq, axis=0, keepdims=True) + jnp.sum(dhn * ht_g, axis=0, keepdims=True) * etot
            dht[:, sl] = etot * dhn + dht_prev
            dcg = dcg + _dot(dcb, bg)
            dbg = dbg + _dot_tn(dcb, cg)
            dxa_ref[:, sl] = dxdt * f["dtx"] + dy * dxp_g
            dxa_ref[:, slb] = dbg
            dxa_ref[:, slc] = dcg
            dcsx_s[:, sl] = dcsx
            ddtx_s[:, sl] = dxdt * xs_g
            accx[2:3, sl] = dtot
        triu = (_iota((l, l), 1) >= _iota((l, l), 0)).astype(F32)
        dax = _dotx(triu, dcsx_s[...]) + accx[2:3, :]
        accx[1:2, :] += jnp.sum(dax * q["dtx"], axis=0, keepdims=True)
        reduce = (jnp.right_shift(_iota((SSD_W, LANE), 0), 6) == _iota((SSD_W, LANE), 1)).astype(F32)
        ddt = _dotx(ddtx_s[...] + dax * q["ax"], reduce)
        da_head = _dotx(triu, dcs_head)
        ddt = ddt + da_head * q["a_head"]
        small_ref[1:2, :] += jnp.sum(da_head * q["dt"], axis=0, keepdims=True)
        ddtr = ddt * _sigmoid(dtr + bias_ref[...])
        ddt_ref[...] = ddtr
        small_ref[0:1, :] += jnp.sum(ddtr, axis=0, keepdims=True)

        @pl.when(step == nc - 1)
        def _():
            red = _dotx(accx[...], reduce)
            d_a = small_ref[1:2, :] + red[1:2, :]
            small_ref[1:2, :] = d_a * q["a_head"]
            small_ref[2:3, :] = red[0:1, :]

    rev = lambda c: nc - 1 - c
    par = lambda w: pl.BlockSpec((1, w), lambda c: (0, 0))
    return _pcall(
        body, grid=(nc,),
        in_specs=[pl.BlockSpec((CHUNK, XBC), lambda c: (rev(c), 0)),
                  pl.BlockSpec((CHUNK, LANE), lambda c: (rev(c), COL_DT // LANE)),
                  pl.BlockSpec((CHUNK, SSD_W), lambda c: (rev(c), COL_Z // SSD_W)),
                  pl.BlockSpec((CHUNK, SSD_W), lambda c: (rev(c), 1)),
                  pl.BlockSpec((None, N_STATE, SSD_W), lambda c: (rev(c), 0, 0)),
                  par(LANE), par(LANE), par(SSD_W), par(SSD_W), par(SSD_W)],
        out_specs=(pl.BlockSpec((CHUNK, XBC), lambda c: (rev(c), 0)),
                   pl.BlockSpec((CHUNK, LANE), lambda c: (rev(c), 0)),
                   pl.BlockSpec((CHUNK, SSD_W), lambda c: (rev(c), 0)),
                   par(SSD_W), pl.BlockSpec((SUBLANE, LANE), lambda c: (0, 0))),
        out_shape=(jax.ShapeDtypeStruct((s, XBC), F32), jax.ShapeDtypeStruct((s, LANE), F32),
                   jax.ShapeDtypeStruct((s, SSD_W), F32), jax.ShapeDtypeStruct((1, SSD_W), F32),
                   jax.ShapeDtypeStruct((SUBLANE, LANE), F32)),
        scratch_shapes=[pltpu.VMEM((N_STATE, SSD_W), F32), pltpu.VMEM((CHUNK, LANE), F32),
                        pltpu.VMEM((SUBLANE, SSD_W), F32), pltpu.VMEM((CHUNK, SSD_W), F32),
                        pltpu.VMEM((CHUNK, SSD_W), F32)],
        name=name, compiler_params=_cparams(("arbitrary",)))(
            xact, proj, proj, dycat, hprev, bias_pad, alog_pad, alogx, dxp, normw)


def _blockdiag(w):
    w2 = w.reshape(N_HEAD // 2, 2, HEAD_P, HEAD_P)
    z = jnp.zeros((N_HEAD // 2, HEAD_P, HEAD_P), w.dtype)
    top = jnp.concatenate([w2[:, 0], z], axis=2)
    bot = jnp.concatenate([z, w2[:, 1]], axis=2)
    return jnp.concatenate([top, bot], axis=1)


def _unblockdiag(wbd):
    a = wbd[:, :HEAD_P, :HEAD_P]
    b = wbd[:, HEAD_P:, HEAD_P:]
    return jnp.stack([a, b], axis=1).reshape(N_HEAD, HEAD_P, HEAD_P)


def _pad_rows8(w):
    return jnp.concatenate([w, jnp.zeros((SUBLANE - w.shape[0], w.shape[1]), w.dtype)], axis=0)


def _pad_lane(v):
    return jnp.concatenate([v, jnp.zeros((1, LANE - v.shape[1]), v.dtype)], axis=1)


class _NoExchange:
    def ride(self, host):
        return []

    def done(self, jobs, outs, w):
        pass

    def grad(self, name, val):
        pass

    def small(self, raw):
        pass


def _local_step(x, p, tgt, w, hooks=_NoExchange()):
    cw_l = _pad_rows8(w["lru_conv_w"])
    cw_s = _pad_rows8(w["ssd_conv_w"])
    wa_bd = _blockdiag(w["lru_gate_a_w"])
    wx_bd = _blockdiag(w["lru_gate_x_w"])
    ba = w["lru_gate_a_b"].reshape(1, LRU_W)
    bx = w["lru_gate_x_b"].reshape(1, LRU_W)
    bias_pad = _pad_lane(w["ssd_dt_bias"])
    alog_pad = _pad_lane(w["ssd_a_log"])
    alogx = jnp.repeat(w["ssd_a_log"], HEAD_P, axis=1)
    dxp = jnp.repeat(w["ssd_d"], HEAD_P, axis=1)

    def host(fn, *a, name, **k):
        jobs = hooks.ride(name)
        res = fn(*a, name=name, jobs=jobs, **k)
        if jobs:
            res, jouts = res
            hooks.done(jobs, jouts, w)
        return res

    def grad(n, val):
        g[n] = val
        hooks.grad(n, val)

    proj = host(_mm, x, w["w_in"], "nn", tm=512, tn=512, name="in_proj")
    ymix, h_lru = host(_lru_fwd, proj, cw_l, w["lru_conv_b"], wa_bd, ba, wx_bd, bx, w["lru_a_param"], name="lru_fwd")
    xact = _conv_silu_fwd(proj, cw_s, w["ssd_conv_b"], col0=COL_XBC, width=XBC, ct=256, name="ssd_conv_fwd")
    ycat, hprev = host(_ssd_fwd, xact, proj, ymix, bias_pad, alog_pad, alogx, dxp, w["ssd_norm_w"], name="ssd_fwd")
    mix = _mm(ycat, w["w_out"], "nn", tm=512, tn=512, name="out_proj")
    x1 = _ln_fwd(x, mix, w["ln1_g"], w["ln1_b"], name="ln1_fwd")
    pre = _mm(x1, w["w_ff1"], "nn", tm=512, tn=512, name="ff1")
    ff = _mm(pre, w["w_ff2"], "nn", tm=512, tn=512, a_fn=_relu2, name="ff2")
    x2 = _ln_fwd(x1, ff, w["ln2_g"], w["ln2_b"], name="ln2_fwd")
    gpre = _mm(x2, w["w_ple_gate"], "nn", tm=512, tn=512, name="ple_gate")
    ple = _mm(p, w["w_ple"], "nn", tm=512, tn=512, name="ple_proj")
    loss, dgpre, dple, dt3, dg3, db3 = _head(x2, gpre, ple, w["ln3_g"], w["ln3_b"], tgt, name="head")

    g = {}
    g["ln3_g"], g["ln3_b"] = dg3, db3
    grad("w_ple_gate", _mm(x2, dgpre, "tn", tm=512, tn=512, name="d_w_ple_gate"))
    grad("w_ple", _mm(p, dple, "tn", tm=256, tn=512, dest_major=True, name="d_w_ple"))
    dx2_mm = host(_mm, dgpre, w["w_ple_gate"], "nt", tm=512, tn=512, name="d_x2")
    dt2, g["ln2_g"], g["ln2_b"] = _ln_bwd(x1, ff, w["ln2_g"], [dt3, dx2_mm], [ALPHA, 1.0], name="ln2_bwd")
    grad("w_ff2", host(_mm, pre, dt2, "tn", tm=512, tn=512, a_fn=_relu2, name="d_w_ff2"))
    dpre = host(_mm, dt2, w["w_ff2"], "nt", tm=512, tn=512, extra=pre,
                epi=lambda acc, pv: acc * 2.0 * jnp.maximum(pv, 0.0), name="d_pre")
    grad("w_ff1", host(_mm, x1, dpre, "tn", tm=512, tn=512, dest_major=True, name="d_w_ff1"))
    dx1_mm = host(_mm, dpre, w["w_ff1"], "nt", tm=512, tn=512, name="d_x1")
    dt1, g["ln1_g"], g["ln1_b"] = _ln_bwd(x, mix, w["ln1_g"], [dt2, dx1_mm], [ALPHA, 1.0], name="ln1_bwd")
    grad("w_out", host(_mm, ycat, dt1, "tn", tm=512, tn=512, name="d_w_out"))
    dycat = host(_mm, dt1, w["w_out"], "nt", tm=512, tn=512, name="d_ycat")
    dxl, dgl, dcwb_l, dwa, dwx = host(_lru_bwd, proj, dycat, h_lru, cw_l, w["lru_conv_b"], wa_bd, ba, wx_bd, bx,
                                      w["lru_a_param"], name="lru_bwd")
    dxact, ddt, dz, g["ssd_norm_w"], small = _ssd_bwd(xact, proj, dycat, hprev, bias_pad, alog_pad, alogx, dxp,
                                                       w["ssd_norm_w"], name="ssd_bwd")
    dxbc, dcwb_s = _conv_silu_bwd(proj, dxact, cw_s, w["ssd_conv_b"], col0=COL_XBC, width=XBC, ct=256,
                                  name="ssd_conv_bwd")
    s = x.shape[0]
    dproj = jnp.concatenate([dxl, dgl, dz, dxbc, ddt, jnp.zeros((s, D_IN_PAD - COL_DT - LANE), F32)], axis=1)
    grad("w_in", _mm(x, dproj, "tn", tm=512, tn=512, name="d_w_in"))

    g["lru_conv_w"] = dcwb_l[0:4]
    g["lru_conv_b"] = dcwb_l[4:5]
    g["lru_gate_a_b"] = dcwb_l[5:6]
    g["lru_gate_x_b"] = dcwb_l[6:7]
    g["lru_a_param"] = dcwb_l[7:8]
    g["lru_gate_a_w"] = _unblockdiag(dwa)
    g["lru_gate_x_w"] = _unblockdiag(dwx)
    g["ssd_conv_w"] = dcwb_s[0:4]
    g["ssd_conv_b"] = dcwb_s[4:5]
    g["ssd_dt_bias"] = small[0:1, :N_HEAD]
    g["ssd_a_log"] = small[1:2, :N_HEAD]
    g["ssd_d"] = small[2:3, :N_HEAD]
    rows = jnp.concatenate([g[n] for n in ("ssd_norm_w", "ln1_g", "ln1_b", "ln2_g", "ln2_b", "ln3_g", "ln3_b")]
                           + [jnp.zeros((1, D_MODEL), F32)], axis=0)
    raw = dict(lru=dcwb_l, ssd=dcwb_s, gate_a=g["lru_gate_a_w"].reshape(N_HEAD * HEAD_P, HEAD_P),
               gate_x=g["lru_gate_x_w"].reshape(N_HEAD * HEAD_P, HEAD_P), heads=small, rows=rows)
    hooks.small(raw)
    grad_x = host(_mm, dproj, w["w_in"], "nt", tm=256, tn=512, extra=dt1, epi=lambda acc, e: acc + ALPHA * e,
                  name="d_x")
    return loss[0, 0], grad_x, g, raw


ANY_SPEC = pl.BlockSpec(memory_space=pl.ANY)


def _mesh_pos():
    return lax.axis_index("x"), lax.axis_index("y"), lax.axis_index("c")


def _remote(src, dst, send, recv, k, to):
    return pltpu.make_async_remote_copy(src_ref=src, dst_ref=dst, send_sem=send.at[k], recv_sem=recv.at[k],
                                        device_id=to, device_id_type=MESH_T)


class _Job:
    N_SEM = 7

    def __init__(self, kind, inp):
        self.kind, self.inp = kind, inp
        shape = {"gather": (N_DEV,) + inp.shape, "pair": (4,) + inp.shape[1:], "chip": inp.shape}[kind]
        self.out = jax.ShapeDtypeStruct(shape, inp.dtype)

    def _places(self):
        x, y, c = _mesh_pos()
        return (x, y, c), (x, y, 1 - c), [(1 - x, y), (x, 1 - y), (1 - x, 1 - y)]

    def start(self, inp, out, send, recv, loc):
        me, sibling, chips = self._places()
        x, y, c = me
        if self.kind == "gather":
            mine = out.at[4 * x + 2 * y + c]
            pltpu.make_async_copy(inp, mine, loc.at[0]).start()
            _remote(inp, mine, send, recv, 0, sibling).start()
            for j, chip in enumerate(chips):
                _remote(inp, mine, send, recv, 1 + j, (*chip, c)).start()
        elif self.kind == "pair":
            for k in range(4):
                _remote(inp.at[2 * k + (1 - c)], out.at[k], send, recv, k, sibling).start()
        else:
            kme = 2 * x + y
            pltpu.make_async_copy(inp.at[kme], out.at[kme], loc.at[0]).start()
            for j, (tx, ty) in enumerate(chips):
                _remote(inp.at[2 * tx + ty], out.at[kme], send, recv, j, (tx, ty, c)).start()

    def finish(self, inp, out, send, recv, loc):
        me, sibling, chips = self._places()
        x, y, c = me
        if self.kind == "gather":
            blk = lambda px, py, pc: out.at[4 * px + 2 * py + pc]
            mine = blk(*me)
            for j, chip in enumerate(chips):
                landed = blk(*chip, c)
                _remote(landed, landed, send, recv, 1 + j, me).wait_recv()
                _remote(landed, landed, send, recv, 4 + j, sibling).start()
            _remote(inp, blk(*sibling), send, recv, 0, me).wait_recv()
            for j, chip in enumerate(chips):
                _remote(inp, blk(*chip, 1 - c), send, recv, 4 + j, me).wait_recv()
            for k in range(7):
                _remote(inp, mine, send, recv, k, sibling).wait_send()
            pltpu.make_async_copy(inp, mine, loc.at[0]).wait()
        elif self.kind == "pair":
            for k in range(4):
                _remote(inp.at[2 * k + (1 - c)], out.at[k], send, recv, k, sibling).wait()
        else:
            kme = 2 * x + y
            for j, (tx, ty) in enumerate(chips):
                _remote(inp.at[kme], out.at[2 * tx + ty], send, recv, j, (tx, ty, c)).wait_recv()
            for j, (tx, ty) in enumerate(chips):
                _remote(inp.at[2 * tx + ty], out.at[kme], send, recv, j, (tx, ty, c)).wait_send()
            pltpu.make_async_copy(inp.at[kme], out.at[kme], loc.at[0]).wait()


def _job_scratch(jobs):
    sem = pltpu.SemaphoreType.DMA
    return [s for _ in jobs for s in (sem((_Job.N_SEM,)), sem((_Job.N_SEM,)), sem((1,)))]


def _run_jobs(jobs, method, jins, jouts, jsems):
    for q, job in enumerate(jobs):
        getattr(job, method)(jins[q], jouts[q], *jsems[3 * q:3 * q + 3])


def _exchange(jobs, *, name):
    n = len(jobs)

    def body(*refs):
        jins, jouts, jsems = refs[:n], refs[n:2 * n], refs[2 * n:]
        _run_jobs(jobs, "start", jins, jouts, jsems)
        _run_jobs(jobs, "finish", jins, jouts, jsems)

    return _pcall(body, in_specs=[ANY_SPEC] * n, out_specs=[ANY_SPEC] * n, out_shape=[j.out for j in jobs],
                  scratch_shapes=_job_scratch(jobs), name=name)(*[j.inp for j in jobs])


def _hosted(body, jobs, *, grid, in_specs, out_specs, out_shape, args, name, scratch_shapes=(), aliases=None):
    in_specs, out_specs, out_shape = list(in_specs), list(out_specs), list(out_shape)
    scratch_shapes = list(scratch_shapes)
    n_in, n_out, n_scr, nj = len(in_specs), len(out_specs), len(scratch_shapes), len(jobs)
    sem = ("arbitrary",) * len(grid)
    kw = dict(input_output_aliases=aliases) if aliases else {}
    if not jobs:
        res = _pcall(body, grid=grid, in_specs=in_specs, out_specs=out_specs, out_shape=out_shape,
                     scratch_shapes=scratch_shapes, name=name, compiler_params=_cparams(sem), **kw)(*args)
        return list(res), []

    def full(*refs):
        ins, jins = refs[:n_in], refs[n_in:n_in + nj]
        o0 = n_in + nj
        outs, jouts = refs[o0:o0 + n_out], refs[o0 + n_out:o0 + n_out + nj]
        s0 = o0 + n_out + nj
        scr, jsems = refs[s0:s0 + n_scr], refs[s0 + n_scr:]
        first = pl.program_id(0) == 0
        last = pl.program_id(0) == grid[0] - 1
        for ax in range(1, len(grid)):
            first = jnp.logical_and(first, pl.program_id(ax) == 0)
            last = jnp.logical_and(last, pl.program_id(ax) == grid[ax] - 1)

        @pl.when(first)
        def _():
            _run_jobs(jobs, "start", jins, jouts, jsems)

        body(*ins, *outs, *scr)

        @pl.when(last)
        def _():
            _run_jobs(jobs, "finish", jins, jouts, jsems)

    res = _pcall(full, grid=grid, in_specs=in_specs + [ANY_SPEC] * nj, out_specs=out_specs + [ANY_SPEC] * nj,
                 out_shape=out_shape + [j.out for j in jobs], scratch_shapes=scratch_shapes + _job_scratch(jobs),
                 name=name, compiler_params=_cparams(sem), **kw)(*args, *[j.inp for j in jobs])
    return list(res[:n_out]), list(res[n_out:])


def _pair_add(g8, r4, cidx, *, name):
    _, r, c = g8.shape
    tr = min(r, ROW_TILE)

    def body(c_ref, g_ref, r_ref, o_ref):
        o_ref[...] = (g_ref[...] + r_ref[...]).astype(BF16)

    return _pcall(
        body,
        grid_spec=pltpu.PrefetchScalarGridSpec(
            num_scalar_prefetch=1, grid=(4, r // tr),
            in_specs=[pl.BlockSpec((None, tr, c), lambda k, i, cr: (2 * k + cr[0], i, 0)),
                      pl.BlockSpec((None, tr, c), lambda k, i, cr: (k, i, 0))],
            out_specs=pl.BlockSpec((None, tr, c), lambda k, i, cr: (k, i, 0))),
        out_shape=jax.ShapeDtypeStruct((4, r, c), BF16), name=name,
        compiler_params=_cparams(("parallel", "parallel")))(cidx, g8, r4)


def _adam_update(g, w_ref, m_ref, v_ref, g_ref, d_ref, mo_ref, vo_ref):
    c1 = 1.0 - ADAM_B1 ** ADAM_STEP
    c2 = 1.0 - ADAM_B2 ** ADAM_STEP
    m2 = ADAM_B1 * m_ref[...] + (1.0 - ADAM_B1) * g
    v2 = ADAM_B2 * v_ref[...] + (1.0 - ADAM_B2) * (g * g)
    g_ref[...] = g
    mo_ref[...] = m2
    vo_ref[...] = v2
    d_ref[...] = -ADAM_LR * ((m2 / c1) / (jnp.sqrt(v2 / c2) + ADAM_EPS) + ADAM_WD * w_ref[...])


def _adamw_rows(srcs, items, own_cols, me1, *, name):
    ns, ni, no = len(srcs), len(items), len(own_cols)
    full = lambda a: pl.BlockSpec(a.shape, lambda i, me: (0,) * a.ndim)
    in_specs = [full(a) for a in srcs]
    args = list(srcs)
    for (si, _r0, w, _m, _v) in own_cols:
        a = srcs[si]
        in_specs.append(pl.BlockSpec((N_DEV, a.shape[1], w.shape[1]), lambda i, me: (0, 0, me[0])))
        args.append(a)
    out_specs, out_shape = [], []
    for (_si, _r0, w, m, v) in list(items) + list(own_cols):
        in_specs += [full(w)] * 3
        args += [w, m, v]
        out_specs += [full(w)] * 4
        out_shape += [jax.ShapeDtypeStruct(w.shape, F32)] * 4

    def body(me_ref, *refs):
        src_refs, own_refs = refs[:ns], refs[ns:ns + no]
        wmv = refs[ns + no:ns + no + 3 * (ni + no)]
        outs = refs[ns + no + 3 * (ni + no):]
        for q, (si, r0, w, _m, _v) in enumerate(list(items) + list(own_cols)):
            nr, cw = w.shape
            gref = src_refs[si] if q < ni else own_refs[q - ni]
            g = gref[0, r0:r0 + nr, 0:cw]
            for d in range(1, N_DEV):
                g = g + gref[d, r0:r0 + nr, 0:cw]
            _adam_update(g, *wmv[3 * q:3 * q + 3], *outs[4 * q:4 * q + 4])

    res = _pcall(
        body,
        grid_spec=pltpu.PrefetchScalarGridSpec(num_scalar_prefetch=1, grid=(1,), in_specs=in_specs, out_specs=out_specs),
        out_shape=out_shape, name=name, compiler_params=_cparams(("arbitrary",)))(me1, *args)
    return [tuple(res[4 * q:4 * q + 4]) for q in range(ni + no)]


def _adamw(gsrc, w, m, v, *, name):
    k, r, c = gsrc.shape
    tr = ROW_TILE if r % ROW_TILE == 0 else r

    def body(gs_ref, w_ref, m_ref, v_ref, g_ref, d_ref, mo_ref, vo_ref):
        g = gs_ref[0].astype(F32)
        for q in range(1, k):
            g = g + gs_ref[q].astype(F32)
        _adam_update(g, w_ref, m_ref, v_ref, g_ref, d_ref, mo_ref, vo_ref)

    row = pl.BlockSpec((tr, c), lambda i: (i, 0))
    sd = jax.ShapeDtypeStruct((r, c), F32)
    return _pcall(body, grid=(r // tr,), in_specs=[pl.BlockSpec((k, tr, c), lambda i: (0, i, 0)), row, row, row],
                  out_specs=(row, row, row, row), out_shape=(sd, sd, sd, sd), name=name,
                  compiler_params=_cparams(("parallel",)))(gsrc, w, m, v)


WEIGHTS = ['w_in', 'lru_conv_w', 'lru_conv_b', 'lru_gate_a_w', 'lru_gate_a_b', 'lru_gate_x_w', 'lru_gate_x_b',
           'lru_a_param', 'ssd_conv_w', 'ssd_conv_b', 'ssd_dt_bias', 'ssd_a_log', 'ssd_d', 'ssd_norm_w', 'w_out',
           'ln1_g', 'ln1_b', 'w_ff1', 'w_ff2', 'ln2_g', 'ln2_b', 'w_ple_gate', 'w_ple', 'ln3_g', 'ln3_b']
BIG = ['w_in', 'w_out', 'w_ff1', 'w_ff2', 'w_ple_gate', 'w_ple']
COL_SHARDED = ('w_in', 'w_ff1', 'w_ple')
CONV = ['lru_conv_w', 'ssd_conv_w']
REPL = [n for n in WEIGHTS if n not in BIG and n not in CONV]
CONV_CH = {'lru_conv_w': LRU_W, 'ssd_conv_w': XBC}


def _to_dest_major(name, gfull):
    if name == 'w_in':
        gfull = gfull[:, :D_IN]
    if name in COL_SHARDED:
        r, cfull = gfull.shape
        return gfull.reshape(r, N_DEV, cfull // N_DEV).transpose(1, 0, 2)
    rfull, cdim = gfull.shape
    return gfull.reshape(N_DEV, rfull // N_DEV, cdim)


def _full_weight(name, gathered):
    if name in COL_SHARDED:
        _, r, cs = gathered.shape
        full = gathered.transpose(1, 0, 2).reshape(r, N_DEV * cs)
    else:
        _, rs, cdim = gathered.shape
        full = gathered.reshape(N_DEV * rs, cdim)
    if name == 'w_in':
        full = jnp.concatenate([full, jnp.zeros((D_MODEL, D_IN_PAD - D_IN), full.dtype)], axis=1)
    return full


SMALL_SRC = ("lru", "ssd", "heads", "rows", "gate_a", "gate_x")
AG_HOSTS = {"in_proj": ("w_out", "w_ple_gate", "w_ple"), "lru_fwd": ("w_ff1",), "ssd_fwd": ("w_ff2",)}
RS_HOSTS = ("d_x2", "d_w_ff2", "d_pre", "d_w_ff1", "d_x1", "d_w_out", "d_ycat", "lru_bwd", "d_x")


class _Schedule:
    def __init__(self, shards, cidx):
        self.shards, self.cidx = shards, cidx
        self.pair, self.chip, self.small_jobs = [], [], []
        self.dest, self.summed, self.gathered_small = {}, {}, {}
        self.tags = []

    def ride(self, host):
        tags = []
        if host in AG_HOSTS:
            tags = [("weight", n, self.shards[n]) for n in AG_HOSTS[host]]
        elif host in RS_HOSTS or host == "flush":
            tags = [("pair", n, a) for n, a in self.pair] + [("chip", n, a) for n, a in self.chip]
            self.pair, self.chip = [], []
            if host == "d_x":
                tags += [("small", n, a) for n, a in self.small_jobs]
        self.tags = tags
        return [_Job({"weight": "gather", "small": "gather"}.get(kind, kind), a) for kind, _n, a in tags]

    def done(self, jobs, outs, w):
        for (kind, n, _a), o in zip(self.tags, outs):
            if kind == "weight":
                w[n] = _full_weight(n, o)
            elif kind == "small":
                self.gathered_small[n] = o
            elif kind == "pair":
                self.chip.append((n, _pair_add(self.dest[n], o, self.cidx, name="rs_pair_add_" + n)))
            else:
                self.summed[n] = o

    def grad(self, name, val):
        self.dest[name] = val if val.ndim == 3 else _to_dest_major(name, val)
        self.pair.append((name, self.dest[name]))

    def small(self, raw):
        self.small_jobs = [(k, raw[k]) for k in SMALL_SRC]

    def flush(self):
        step = 0
        while self.pair or self.chip:
            jobs = self.ride("flush")
            self.done(jobs, _exchange(jobs, name="rs_flush_%d" % step), None)
            step += 1


def kernel(x, p, w_in, lru_conv_w, lru_conv_b, lru_gate_a_w, lru_gate_a_b, lru_gate_x_w, lru_gate_x_b, lru_a_param, ssd_conv_w, ssd_conv_b, ssd_dt_bias, ssd_a_log, ssd_d, ssd_norm_w, w_out, ln1_g, ln1_b, w_ff1, w_ff2, ln2_g, ln2_b, w_ple_gate, w_ple, ln3_g, ln3_b, loss_target, m_w_in, m_lru_conv_w, m_lru_conv_b, m_lru_gate_a_w, m_lru_gate_a_b, m_lru_gate_x_w, m_lru_gate_x_b, m_lru_a_param, m_ssd_conv_w, m_ssd_conv_b, m_ssd_dt_bias, m_ssd_a_log, m_ssd_d, m_ssd_norm_w, m_w_out, m_ln1_g, m_ln1_b, m_w_ff1, m_w_ff2, m_ln2_g, m_ln2_b, m_w_ple_gate, m_w_ple, m_ln3_g, m_ln3_b, v_w_in, v_lru_conv_w, v_lru_conv_b, v_lru_gate_a_w, v_lru_gate_a_b, v_lru_gate_x_w, v_lru_gate_x_b, v_lru_a_param, v_ssd_conv_w, v_ssd_conv_b, v_ssd_dt_bias, v_ssd_a_log, v_ssd_d, v_ssd_norm_w, v_w_out, v_ln1_g, v_ln1_b, v_w_ff1, v_w_ff2, v_ln2_g, v_ln2_b, v_w_ple_gate, v_w_ple, v_ln3_g, v_ln3_b):
    given = dict(locals())
    wsh = {n: given[n][0] for n in WEIGHTS}
    msh = {n: given["m_" + n][0] for n in WEIGHTS}
    vsh = {n: given["v_" + n][0] for n in WEIGHTS}
    xi, yi, ci = _mesh_pos()
    me = 4 * xi + 2 * yi + ci

    shards = {n: wsh[n].astype(BF16) for n in BIG}
    conv_pack = jnp.concatenate([_pad_rows8(wsh[n]) for n in CONV], axis=1)
    g_in, gconv = _exchange([_Job("gather", shards['w_in']), _Job("gather", conv_pack)], name="ag_first")
    full = {'w_in': _full_weight('w_in', g_in)}
    c0 = 0
    for n in CONV:
        cw = CONV_CH[n] // N_DEV
        full[n] = gconv[:, :4, c0:c0 + cw].transpose(1, 0, 2).reshape(4, CONV_CH[n])
        c0 += cw
    for n in REPL:
        full[n] = wsh[n].reshape(1, -1) if wsh[n].ndim == 1 else wsh[n]

    sched = _Schedule(shards, jnp.reshape(ci, (1,)).astype(jnp.int32))
    loss_local, grad_x, g, raw = _local_step(x[0], p[0, 0], loss_target[0], full, sched)
    loss = lax.psum(loss_local, ("x", "y", "c"))
    sched.flush()
    summed, gat = sched.summed, sched.gathered_small

    outs = {}
    for n in BIG:
        outs[n] = _adamw(summed[n], wsh[n], msh[n], vsh[n], name="adamw_" + n)
    for n, k in (("lru_gate_a_w", "gate_a"), ("lru_gate_x_w", "gate_x")):
        flat = lambda a: a.reshape(N_HEAD * HEAD_P, HEAD_P)
        res = _adamw(gat[k], flat(wsh[n]), flat(msh[n]), flat(vsh[n]), name="adamw_" + n)
        outs[n] = tuple(r.reshape(N_HEAD, HEAD_P, HEAD_P) for r in res)
    row_items = [("lru_conv_b", 0, 4), ("lru_gate_a_b", 0, 5), ("lru_gate_x_b", 0, 6), ("lru_a_param", 0, 7),
                 ("ssd_conv_b", 1, 4), ("ssd_dt_bias", 2, 0), ("ssd_a_log", 2, 1), ("ssd_d", 2, 2),
                 ("ssd_norm_w", 3, 0), ("ln1_g", 3, 1), ("ln1_b", 3, 2), ("ln2_g", 3, 3), ("ln2_b", 3, 4),
                 ("ln3_g", 3, 5), ("ln3_b", 3, 6)]
    vec = lambda a: a.reshape(1, -1)
    items = [(si, r0, vec(wsh[n]), vec(msh[n]), vec(vsh[n])) for n, si, r0 in row_items]
    own = [(si, 0, wsh[n], msh[n], vsh[n]) for n, si in (("lru_conv_w", 0), ("ssd_conv_w", 1))]
    me1 = jnp.reshape(me, (1,)).astype(jnp.int32)
    res = _adamw_rows([gat[k] for k in SMALL_SRC[:4]], items, own, me1, name="adamw_small")
    for (n, _si, _r0), r4 in zip(row_items, res[:len(row_items)]):
        outs[n] = tuple(r.reshape(wsh[n].shape) for r in r4)
    for n, r4 in zip(CONV, res[len(row_items):]):
        outs[n] = r4

    ex = lambda a: a[None]
    return (loss, grad_x[None],
            *[ex(outs[n][0]) for n in WEIGHTS], *[ex(outs[n][1]) for n in WEIGHTS],
            *[ex(outs[n][2]) for n in WEIGHTS], *[ex(outs[n][3]) for n in WEIGHTS])
```

```python
import math

import jax
import jax.numpy as jnp
from jax import lax
from jax.experimental import pallas as pl
from jax.experimental.pallas import tpu as pltpu

F32 = jnp.float32
BF16 = jnp.bfloat16
HI = lax.Precision.HIGHEST

N_DEV = 8
D_MODEL = 1024
LRU_W = 1024
SSD_W = 1024
XBC = 2048
N_HEAD = 16
HEAD_P = 64
N_GROUP = 4
GROUP_W = 256
N_STATE = 128
CHUNK = 128
D_FF = 4096
PLE_DIM = 256
D_IN = 5136
D_IN_PAD = 5632
COL_G = 1024
COL_Z = 2048
COL_XBC = 3072
COL_DT = 5120
LRU_C = 8.0
ALPHA = 2.0 ** 0.25
LN_EPS = 1e-5
RMS_EPS = 1e-5
ADAM_LR = 0.001
ADAM_B1 = 0.9
ADAM_B2 = 0.999
ADAM_EPS = 1e-08
ADAM_WD = 0.01
ADAM_STEP = 10
GELU_C = math.sqrt(2.0 / math.pi)
LANE = 128
SUBLANE = 8
VMEM_LIMIT = 48 * 1024 * 1024
MESH_T = pl.DeviceIdType.MESH
NEG_BIG = -1e30


def _pcall(body, **kw):
    return pl.pallas_call(body, **kw)


def _cparams(sem):
    return pltpu.CompilerParams(dimension_semantics=sem, vmem_limit_bytes=VMEM_LIMIT)


def _dot(a, b):
    return jnp.dot(a.astype(BF16), b.astype(BF16), preferred_element_type=F32)


def _dot_nt(a, b):
    return lax.dot_general(a.astype(BF16), b.astype(BF16), (((1,), (1,)), ((), ())), preferred_element_type=F32)


def _dot_tn(a, b):
    return lax.dot_general(a.astype(BF16), b.astype(BF16), (((0,), (0,)), ((), ())), preferred_element_type=F32)


def _dotx(a, b):
    return jnp.dot(a, b, precision=HI, preferred_element_type=F32)


def _sigmoid(x):
    return jax.nn.sigmoid(x)


def _softplus(v):
    return jnp.maximum(v, 0.0) + jnp.log1p(jnp.exp(-jnp.abs(v)))


def _gelu(x):
    th = jnp.tanh(GELU_C * (x + 0.044715 * x * x * x))
    return 0.5 * x * (1.0 + th), th


def _gelu_grad(x, th):
    return 0.5 * (1.0 + th) + 0.5 * x * (1.0 - th * th) * GELU_C * (1.0 + 3.0 * 0.044715 * x * x)


def _iota(shape, dim):
    return lax.broadcasted_iota(jnp.int32, shape, dim)


def _mm(a, b, mode, *, tm, tn, name, a_fn=None, extra=None, epi=None, out_dtype=F32, dest_major=False, jobs=()):
    m = a.shape[1] if mode == "tn" else a.shape[0]
    n = b.shape[0] if mode == "nt" else b.shape[1]
    tm, tn = min(tm, m), min(tn, n)
    if dest_major:
        tn = n // N_DEV
    if mode == "nn":
        m, k = a.shape
        _, n = b.shape
        a_spec = pl.BlockSpec((tm, k), lambda i, j: (i, 0))
        b_spec = pl.BlockSpec((k, tn), lambda i, j: (0, j))
        dims = ((1,), (0,))
    elif mode == "nt":
        m, k = a.shape
        n, _ = b.shape
        a_spec = pl.BlockSpec((tm, k), lambda i, j: (i, 0))
        b_spec = pl.BlockSpec((tn, k), lambda i, j: (j, 0))
        dims = ((1,), (1,))
    else:
        k, m = a.shape
        _, n = b.shape
        a_spec = pl.BlockSpec((k, tm), lambda i, j: (0, i))
        b_spec = pl.BlockSpec((k, tn), lambda i, j: (0, j))
        dims = ((0,), (0,))
    assert m % tm == 0 and n % tn == 0, (name, m, n, tm, tn)
    o_spec = pl.BlockSpec((tm, tn), lambda i, j: (i, j))
    in_specs = [a_spec, b_spec]
    args = [a, b]
    if extra is not None:
        in_specs.append(o_spec)
        args.append(extra)

    def body(*refs):
        a_ref, b_ref, o_ref = refs[0], refs[1], refs[-1]
        av = a_ref[...]
        if a_fn is not None:
            av = a_fn(av)
        acc = lax.dot_general(av.astype(BF16), b_ref[...].astype(BF16), (dims, ((), ())), preferred_element_type=F32)
        if epi is not None:
            acc = epi(acc, refs[2][...])
        o_ref[...] = acc.astype(out_dtype)

    out_shape = jax.ShapeDtypeStruct((m, n), out_dtype)
    if dest_major:
        assert extra is None
        o_spec = pl.BlockSpec((None, tm, tn), lambda i, j: (j, i, 0))
        out_shape = jax.ShapeDtypeStruct((N_DEV, m, tn), out_dtype)
    (out,), jouts = _hosted(body, jobs, grid=(m // tm, n // tn), in_specs=in_specs, out_specs=[o_spec],
                            out_shape=[out_shape], args=args, name=name)
    return (out, jouts) if jobs else out


def _relu2(v):
    r = jnp.maximum(v, 0.0)
    return r * r


ROW_TILE = 256


def _ln_stats(t):
    mu = jnp.mean(t, axis=-1, keepdims=True)
    xc = t - mu
    var = jnp.mean(xc * xc, axis=-1, keepdims=True)
    rstd = lax.rsqrt(var + LN_EPS)
    return xc * rstd, rstd


def _ln_bwd_rows(dy, xhat, rstd, g):
    dxh = dy * g
    m1 = jnp.mean(dxh, axis=-1, keepdims=True)
    m2 = jnp.mean(dxh * xhat, axis=-1, keepdims=True)
    return rstd * (dxh - m1 - xhat * m2)


def _ln_fwd(a, b, g, beta, *, name):
    s, d = a.shape
    row = pl.BlockSpec((ROW_TILE, d), lambda i: (i, 0))
    par = pl.BlockSpec((1, d), lambda i: (0, 0))

    def body(a_ref, b_ref, g_ref, be_ref, y_ref):
        xhat, _ = _ln_stats(ALPHA * a_ref[...] + b_ref[...])
        y_ref[...] = xhat * g_ref[...] + be_ref[...]

    return _pcall(body, grid=(s // ROW_TILE,), in_specs=[row, row, par, par], out_specs=row,
                  out_shape=jax.ShapeDtypeStruct((s, d), F32), name=name,
                  compiler_params=_cparams(("parallel",)))(a, b, g, beta)


def _ln_bwd(a, b, g, dys, coefs, *, name):
    s, d = a.shape
    row = pl.BlockSpec((ROW_TILE, d), lambda i: (i, 0))
    par = pl.BlockSpec((1, d), lambda i: (0, 0))
    n = len(dys)

    def body(*refs):
        a_ref, b_ref, g_ref = refs[:3]
        dy_refs = refs[3:3 + n]
        dt_ref, dg_ref, db_ref = refs[3 + n:]
        xhat, rstd = _ln_stats(ALPHA * a_ref[...] + b_ref[...])
        dy = coefs[0] * dy_refs[0][...]
        for q in range(1, n):
            dy = dy + coefs[q] * dy_refs[q][...]
        dt_ref[...] = _ln_bwd_rows(dy, xhat, rstd, g_ref[...])

        @pl.when(pl.program_id(0) == 0)
        def _():
            dg_ref[...] = jnp.zeros_like(dg_ref)
            db_ref[...] = jnp.zeros_like(db_ref)

        dg_ref[...] += jnp.sum(dy * xhat, axis=0, keepdims=True)
        db_ref[...] += jnp.sum(dy, axis=0, keepdims=True)

    return _pcall(body, grid=(s // ROW_TILE,), in_specs=[row, row, par] + [row] * n, out_specs=(row, par, par),
                  out_shape=(jax.ShapeDtypeStruct((s, d), F32), jax.ShapeDtypeStruct((1, d), F32),
                             jax.ShapeDtypeStruct((1, d), F32)),
                  name=name, compiler_params=_cparams(("arbitrary",)))(a, b, g, *dys)


def _head(x2, gpre, ple, g, beta, tgt, *, name):
    s, d = x2.shape
    row = pl.BlockSpec((ROW_TILE, d), lambda i: (i, 0))
    par = pl.BlockSpec((1, d), lambda i: (0, 0))
    lsp = pl.BlockSpec((1, LANE), lambda i: (0, 0))

    def body(x2_ref, gp_ref, ple_ref, g_ref, be_ref, t_ref, loss_ref, dgp_ref, dple_ref, dt_ref, dg_ref, db_ref):
        gate = _sigmoid(gp_ref[...])
        ple_v = ple_ref[...]
        xhat, rstd = _ln_stats(ALPHA * x2_ref[...] + gate * ple_v)
        err = xhat * g_ref[...] + be_ref[...] - t_ref[...]
        dy = err * (1.0 / d)
        dt = _ln_bwd_rows(dy, xhat, rstd, g_ref[...])
        dt_ref[...] = dt
        dgp_ref[...] = dt * ple_v * gate * (1.0 - gate)
        dple_ref[...] = dt * gate

        @pl.when(pl.program_id(0) == 0)
        def _():
            loss_ref[...] = jnp.zeros_like(loss_ref)
            dg_ref[...] = jnp.zeros_like(dg_ref)
            db_ref[...] = jnp.zeros_like(db_ref)

        loss_ref[...] += 0.5 * jnp.sum(jnp.mean(err * err, axis=-1, keepdims=True))
        dg_ref[...] += jnp.sum(dy * xhat, axis=0, keepdims=True)
        db_ref[...] += jnp.sum(dy, axis=0, keepdims=True)

    sd = jax.ShapeDtypeStruct((s, d), F32)
    pd = jax.ShapeDtypeStruct((1, d), F32)
    return _pcall(body, grid=(s // ROW_TILE,), in_specs=[row, row, row, par, par, row],
                  out_specs=(lsp, row, row, row, par, par),
                  out_shape=(jax.ShapeDtypeStruct((1, LANE), F32), sd, sd, sd, pd, pd),
                  name=name, compiler_params=_cparams(("arbitrary",)))(x2, gpre, ple, g, beta, tgt)


CONV_R = 256
PAD = SUBLANE


def _shift_down(ext, s):
    if s == 0:
        return ext[PAD:, :]
    return pltpu.roll(ext, s, 0)[PAD:, :]


def _shift_up(ext, s):
    r = ext.shape[0] - PAD
    if s == 0:
        return ext[:r, :]
    return pltpu.roll(ext, r + PAD - s, 0)[:r, :]


def _conv_rows(xpad_ref, r0, w_ref):
    ext = xpad_ref[pl.ds(r0, CONV_R + PAD), :]
    acc = _shift_down(ext, 0) * w_ref[3:4, :]
    for k in range(3):
        acc = acc + _shift_down(ext, 3 - k) * w_ref[k:k + 1, :]
    return acc, ext


def _fill_front_padded(dst_ref, src_ref, s):
    dst_ref[0:PAD, :] = jnp.zeros((PAD, dst_ref.shape[1]), F32)

    def cp(q, _):
        r0 = pl.multiple_of(q * CONV_R, CONV_R)
        dst_ref[pl.ds(pl.multiple_of(PAD + r0, PAD), CONV_R), :] = src_ref[pl.ds(r0, CONV_R), :]
        return 0

    lax.fori_loop(0, s // CONV_R, cp, 0)


def _conv_silu_fwd(proj, w8, b, *, col0, width, ct, name):
    s = proj.shape[0]
    nb = col0 // ct

    def body(x_ref, w_ref, b_ref, o_ref, xpad):
        _fill_front_padded(xpad, x_ref, s)

        def step(q, _):
            r0 = pl.multiple_of(q * CONV_R, CONV_R)
            acc, _e = _conv_rows(xpad, r0, w_ref)
            pre = acc + b_ref[...]
            o_ref[pl.ds(r0, CONV_R), :] = pre * _sigmoid(pre)
            return 0

        lax.fori_loop(0, s // CONV_R, step, 0)

    return _pcall(
        body, grid=(width // ct,),
        in_specs=[pl.BlockSpec((s, ct), lambda j: (0, nb + j)), pl.BlockSpec((SUBLANE, ct), lambda j: (0, j)),
                  pl.BlockSpec((1, ct), lambda j: (0, j))],
        out_specs=pl.BlockSpec((s, ct), lambda j: (0, j)),
        out_shape=jax.ShapeDtypeStruct((s, width), F32),
        scratch_shapes=[pltpu.VMEM((s + PAD, ct), F32)], name=name,
        compiler_params=_cparams(("parallel",)))(proj, w8, b)


def _conv_bwd_rows(dpad_ref, r0, w_ref):
    return _conv_bwd_ext(dpad_ref[pl.ds(r0, CONV_R + PAD), :], w_ref)


def _conv_bwd_ext(ext, w_ref):
    acc = _shift_up(ext, 0) * w_ref[3:4, :]
    for k in range(3):
        acc = acc + _shift_up(ext, 3 - k) * w_ref[k:k + 1, :]
    return acc


def _conv_silu_bwd(proj, dact, w8, b, *, col0, width, ct, name):
    s = proj.shape[0]
    nb = col0 // ct

    def body(x_ref, d_ref, w_ref, b_ref, dx_ref, dwb_ref, xpad, dpad):
        _fill_front_padded(xpad, x_ref, s)
        dpad[pl.ds(s, PAD), :] = jnp.zeros((PAD, ct), F32)
        dwb_ref[...] = jnp.zeros_like(dwb_ref)

        def step(q, _):
            r0 = pl.multiple_of(q * CONV_R, CONV_R)
            acc, ext = _conv_rows(xpad, r0, w_ref)
            pre = acc + b_ref[...]
            sg = _sigmoid(pre)
            dpre = d_ref[pl.ds(r0, CONV_R), :] * sg * (1.0 + pre * (1.0 - sg))
            dpad[pl.ds(r0, CONV_R), :] = dpre
            for k in range(4):
                dwb_ref[k:k + 1, :] += jnp.sum(dpre * _shift_down(ext, 3 - k), axis=0, keepdims=True)
            dwb_ref[4:5, :] += jnp.sum(dpre, axis=0, keepdims=True)
            return 0

        lax.fori_loop(0, s // CONV_R, step, 0)

        def step2(q, _):
            r0 = pl.multiple_of(q * CONV_R, CONV_R)
            dx_ref[pl.ds(r0, CONV_R), :] = _conv_bwd_rows(dpad, r0, w_ref)
            return 0

        lax.fori_loop(0, s // CONV_R, step2, 0)

    colb = pl.BlockSpec((s, ct), lambda j: (0, j))
    return _pcall(
        body, grid=(width // ct,),
        in_specs=[pl.BlockSpec((s, ct), lambda j: (0, nb + j)), colb, pl.BlockSpec((SUBLANE, ct), lambda j: (0, j)),
                  pl.BlockSpec((1, ct), lambda j: (0, j))],
        out_specs=(colb, pl.BlockSpec((SUBLANE, ct), lambda j: (0, j))),
        out_shape=(jax.ShapeDtypeStruct((s, width), F32), jax.ShapeDtypeStruct((SUBLANE, width), F32)),
        scratch_shapes=[pltpu.VMEM((s + PAD, ct), F32), pltpu.VMEM((s + PAD, ct), F32)], name=name,
        compiler_params=_cparams(("parallel",)))(proj, dact, w8, b)


LRU_CT = 128


def _row_of(v, r):
    return jnp.sum(jnp.where(_iota((v.shape[0], 1), 0) == r, v, 0.0), axis=0, keepdims=True)


def _scan_fwd(a, u):
    r = a.shape[0]
    row = _iota((r, 1), 0)
    d = 1
    while d < r:
        valid = row >= d
        u = jnp.where(valid, a * pltpu.roll(u, d, 0) + u, u)
        a = jnp.where(valid, a * pltpu.roll(a, d, 0), a)
        d *= 2
    return a, u


def _scan_rev(b, u):
    r = b.shape[0]
    row = _iota((r, 1), 0)
    d = 1
    while d < r:
        valid = row < r - d
        u = jnp.where(valid, b * pltpu.roll(u, r - d, 0) + u, u)
        b = jnp.where(valid, b * pltpu.roll(b, r - d, 0), b)
        d *= 2
    return b, u


def _lru_chunk(xpad, r0, cw_ref, cb, wa, ba, wx, bx, sp):
    acc, ext = _conv_rows(xpad, r0, cw_ref)
    xl = acc + cb
    r = _sigmoid(_dot(xl, wa) + ba)
    i = _sigmoid(_dot(xl, wx) + bx)
    la = -LRU_C * r * sp
    a = jnp.exp(la)
    a2 = jnp.exp(2.0 * la)
    mult = jnp.sqrt(-jnp.tanh(la) * (a2 + 1.0))
    first = (r0 + _iota((CONV_R, 1), 0)) == 0
    mult = jnp.where(first, 1.0, mult)
    return ext, xl, r, i, a, a2, mult, first


def _lru_specs(s):
    ct = LRU_CT
    nb_g = COL_G // ct
    return dict(
        x=pl.BlockSpec((s, ct), lambda j: (0, j)),
        g=pl.BlockSpec((s, ct), lambda j: (0, nb_g + j)),
        col=pl.BlockSpec((s, ct), lambda j: (0, j)),
        cw=pl.BlockSpec((SUBLANE, ct), lambda j: (0, j)),
        vec=pl.BlockSpec((1, ct), lambda j: (0, j)),
        gate=pl.BlockSpec((None, ct, ct), lambda j: (j, 0, 0)),
    )


def _lru_fwd(proj, cw8, cb, wa_bd, ba, wx_bd, bx, ap, *, name, jobs=()):
    s = proj.shape[0]
    ct = LRU_CT
    sp_ = _lru_specs(s)

    def body(x_ref, g_ref, cw_ref, cb_ref, wa_ref, ba_ref, wx_ref, bx_ref, ap_ref, y_ref, h_ref, xpad):
        _fill_front_padded(xpad, x_ref, s)
        sp = _softplus(-ap_ref[...])

        def step(q, carry):
            r0 = pl.multiple_of(q * CONV_R, CONV_R)
            _e, xl, _r, i, a, _a2, mult, _f = _lru_chunk(xpad, r0, cw_ref, cb_ref[...], wa_ref[...], ba_ref[...],
                                                       wx_ref[...], bx_ref[...], sp)
            acum, ucum = _scan_fwd(a, xl * i * mult)
            h = acum * carry + ucum
            h_ref[pl.ds(r0, CONV_R), :] = h
            ge, _th = _gelu(g_ref[pl.ds(r0, CONV_R), :])
            y_ref[pl.ds(r0, CONV_R), :] = ge * h
            return _row_of(h, CONV_R - 1)

        lax.fori_loop(0, s // CONV_R, step, jnp.zeros((1, ct), F32))

    (ymix, hs), jouts = _hosted(
        body, jobs, grid=(LRU_W // ct,),
        in_specs=[sp_["x"], sp_["g"], sp_["cw"], sp_["vec"], sp_["gate"], sp_["vec"], sp_["gate"], sp_["vec"], sp_["vec"]],
        out_specs=(sp_["col"], sp_["col"]),
        out_shape=(jax.ShapeDtypeStruct((s, LRU_W + SSD_W), F32), jax.ShapeDtypeStruct((s, LRU_W), F32)),
        scratch_shapes=[pltpu.VMEM((s + PAD, ct), F32)],
        name=name, args=(proj, proj, cw8, cb, wa_bd, ba, wx_bd, bx, ap))
    return ((ymix, hs), jouts) if jobs else (ymix, hs)


def _lru_bwd(proj, dy, hs, cw8, cb, wa_bd, ba, wx_bd, bx, ap, *, name, jobs=()):
    s = proj.shape[0]
    ct = LRU_CT
    sp_ = _lru_specs(s)

    nq = s // CONV_R

    def body(x_ref, g_ref, dy_ref, h_ref, cw_ref, cb_ref, wa_ref, ba_ref, wx_ref, bx_ref, ap_ref,
             dx_ref, dg_ref, dcwb_ref, dwa_ref, dwx_ref, xpad, hpad):
        _fill_front_padded(xpad, x_ref, s)
        _fill_front_padded(hpad, h_ref, s)
        apv = ap_ref[...]
        sp = _softplus(-apv)
        cb_v, wa, ba_v, wx, bx_v = cb_ref[...], wa_ref[...], ba_ref[...], wx_ref[...], bx_ref[...]
        dcwb_ref[...] = jnp.zeros_like(dcwb_ref)
        dwa_ref[...] = jnp.zeros_like(dwa_ref)
        dwx_ref[...] = jnp.zeros_like(dwx_ref)

        def back(k, carry):
            g_next, a_next, dxl_next = carry
            last_row = _iota((CONV_R, 1), 0) == CONV_R - 1
            r0 = pl.multiple_of((nq - 1 - k) * CONV_R, CONV_R)
            ext, xl, r, i, a, a2, mult, first = _lru_chunk(xpad, r0, cw_ref, cb_v, wa, ba_v, wx, bx_v, sp)
            gv = g_ref[pl.ds(r0, CONV_R), :]
            dyv = dy_ref[pl.ds(r0, CONV_R), :]
            hext = hpad[pl.ds(r0, CONV_R + PAD), :]
            ge, th = _gelu(gv)
            dg_ref[pl.ds(r0, CONV_R), :] = dyv * _shift_down(hext, 0) * _gelu_grad(gv, th)
            b = jnp.where(last_row, a_next, pltpu.roll(a, CONV_R - 1, 0))
            bcum, dcum = _scan_rev(b, dyv * ge)
            gval = dcum + bcum * g_next
            hprev = _shift_down(hext, 1)
            da = gval * hprev
            dxl = gval * i * mult
            di = gval * xl * mult
            dmult = jnp.where(first, 0.0, gval * xl * i)
            dla = da * a - dmult * a2 / mult
            dr = dla * (-LRU_C) * sp
            dcwb_ref[7:8, :] += jnp.sum(dla * (-LRU_C) * r, axis=0, keepdims=True)
            dpr = dr * r * (1.0 - r)
            dpi = di * i * (1.0 - i)
            dxl = dxl + _dot_nt(dpr, wa) + _dot_nt(dpi, wx)
            dwa_ref[...] += _dot_tn(xl, dpr)
            dwx_ref[...] += _dot_tn(xl, dpi)
            dcwb_ref[5:6, :] += jnp.sum(dpr, axis=0, keepdims=True)
            dcwb_ref[6:7, :] += jnp.sum(dpi, axis=0, keepdims=True)
            for tap in range(4):
                dcwb_ref[tap:tap + 1, :] += jnp.sum(dxl * _shift_down(ext, 3 - tap), axis=0, keepdims=True)
            dcwb_ref[4:5, :] += jnp.sum(dxl, axis=0, keepdims=True)
            dx_ref[pl.ds(r0, CONV_R), :] = _conv_bwd_ext(jnp.concatenate([dxl, dxl_next], axis=0), cw_ref)
            return _row_of(gval, 0), _row_of(a, 0), dxl[:PAD, :]

        zero = jnp.zeros((1, ct), F32)
        lax.fori_loop(0, nq, back, (zero, zero, jnp.zeros((PAD, ct), F32)))
        dcwb_ref[7:8, :] = dcwb_ref[7:8, :] * (-_sigmoid(-apv))

    nt = LRU_W // ct
    outs, jouts = _hosted(
        body, jobs, grid=(nt,),
        in_specs=[sp_["x"], sp_["g"], sp_["col"], sp_["col"], sp_["cw"], sp_["vec"], sp_["gate"], sp_["vec"], sp_["gate"],
                  sp_["vec"], sp_["vec"]],
        out_specs=(sp_["col"], sp_["col"], sp_["cw"], sp_["gate"], sp_["gate"]),
        out_shape=(jax.ShapeDtypeStruct((s, LRU_W), F32), jax.ShapeDtypeStruct((s, LRU_W), F32),
                   jax.ShapeDtypeStruct((SUBLANE, LRU_W), F32), jax.ShapeDtypeStruct((nt, ct, ct), F32),
                   jax.ShapeDtypeStruct((nt, ct, ct), F32)),
        scratch_shapes=[pltpu.VMEM((s + PAD, ct), F32), pltpu.VMEM((s + PAD, ct), F32)],
        name=name, args=(proj, proj, dy, hs, cw8, cb, wa_bd, ba, wx_bd, bx, ap))
    return (tuple(outs), jouts) if jobs else tuple(outs)


def _ssd_prep(dtr, bias, alog_pad, alogx):
    l = CHUNK
    lane = _iota((1, LANE), 1)
    a_head = jnp.where(lane < N_HEAD, -jnp.exp(alog_pad), 0.0)
    dt = _softplus(dtr + bias)
    tril = (_iota((l, l), 1) <= _iota((l, l), 0)).astype(F32)
    cs = _dotx(tril, dt * a_head)
    expand = (jnp.right_shift(_iota((LANE, SSD_W), 1), 6) == _iota((LANE, SSD_W), 0)).astype(F32)
    dtx = _dotx(dt, expand)
    ax = -jnp.exp(alogx)
    csx = _dotx(tril, dtx * ax)
    totx = jnp.sum(dtx * ax, axis=0, keepdims=True)
    return dict(a_head=a_head, dt=dt, tril=tril, cs=cs, expand=expand, dtx=dtx, ax=ax, csx=csx, totx=totx)


def _decay_mat(cs, cst_ref, h, causal):
    lane = _iota((CHUNK, LANE), 1)
    col = jnp.sum(jnp.where(lane == h, cs, 0.0), axis=1, keepdims=True)
    row = cst_ref[h:h + 1, :]
    return jnp.exp(jnp.where(causal, col - row, NEG_BIG))


def _head_mask(j):
    lane = _iota((CHUNK, GROUP_W), 1)
    return (lane >= j * HEAD_P) & (lane < (j + 1) * HEAD_P)


def _ssd_group_fwd(q, g, xs_g, bg, cg, ht_g, cst_ref, causal, dx_g):
    sl = slice(g * GROUP_W, (g + 1) * GROUP_W)
    dtx_g, csx_g, totx_g = q["dtx"][:, sl], q["csx"][:, sl], q["totx"][:, sl]
    xdt = xs_g * dtx_g
    ex = jnp.exp(csx_g)
    cb = _dot_nt(cg, bg)
    yoff = _dot(cg, ht_g) * ex
    ydiag = jnp.zeros((CHUNK, GROUP_W), F32)
    for j in range(4):
        sc = cb * _decay_mat(q["cs"], cst_ref, 4 * g + j, causal)
        ydiag = jnp.where(_head_mask(j), _dot(sc, xdt), ydiag)
    y = ydiag + yoff + xs_g * dx_g
    dsx = jnp.exp(totx_g - csx_g)
    return y, dict(xdt=xdt, ex=ex, cb=cb, yoff=yoff, dsx=dsx, dtx=dtx_g, totx=totx_g)


def _gated_norm_fwd(y_g, z_g, w_g):
    sz = _sigmoid(z_g)
    silu = z_g * sz
    yf = y_g * silu
    rs = lax.rsqrt(jnp.mean(yf * yf, axis=1, keepdims=True) + RMS_EPS)
    yn = yf * rs
    return yn * w_g, (sz, silu, rs, yn)


def _ssd_fwd(xact, proj, ymix, bias_pad, alog_pad, alogx, dxp, normw, *, name, jobs=()):
    s = xact.shape[0]
    nc = s // CHUNK

    def body(xa_ref, dt_ref, z_ref, _ymix_ref, bias_ref, alp_ref, alx_ref, dx_ref, nw_ref, y_ref, hp_ref, ht, cst):
        @pl.when(pl.program_id(0) == 0)
        def _():
            ht[...] = jnp.zeros_like(ht)

        hp_ref[...] = ht[...]
        q = _ssd_prep(dt_ref[...], bias_ref[...], alp_ref[...], alx_ref[...])
        cst[...] = q["cs"].T
        causal = q["tril"] > 0.0
        for g in range(N_GROUP):
            sl = slice(g * GROUP_W, (g + 1) * GROUP_W)
            xs_g = xa_ref[:, sl]
            bg = xa_ref[:, SSD_W + g * N_STATE:SSD_W + (g + 1) * N_STATE]
            cg = xa_ref[:, SSD_W + N_GROUP * N_STATE + g * N_STATE:SSD_W + N_GROUP * N_STATE + (g + 1) * N_STATE]
            ht_g = ht[:, sl]
            y, f = _ssd_group_fwd(q, g, xs_g, bg, cg, ht_g, cst, causal, dx_ref[:, sl])
            out, _ = _gated_norm_fwd(y, z_ref[:, sl], nw_ref[:, sl])
            y_ref[:, sl] = out
            ht[:, sl] = jnp.exp(f["totx"]) * ht_g + _dot_tn(bg, f["xdt"] * f["dsx"])

    par = lambda w: pl.BlockSpec((1, w), lambda c: (0, 0))
    (ycat, hprev), jouts = _hosted(
        body, jobs, grid=(nc,),
        in_specs=[pl.BlockSpec((CHUNK, XBC), lambda c: (c, 0)),
                  pl.BlockSpec((CHUNK, LANE), lambda c: (c, COL_DT // LANE)),
                  pl.BlockSpec((CHUNK, SSD_W), lambda c: (c, COL_Z // SSD_W)),
                  ANY_SPEC, par(LANE), par(LANE), par(SSD_W), par(SSD_W), par(SSD_W)],
        out_specs=(pl.BlockSpec((CHUNK, SSD_W), lambda c: (c, LRU_W // SSD_W)),
                   pl.BlockSpec((None, N_STATE, SSD_W), lambda c: (c, 0, 0))),
        out_shape=(jax.ShapeDtypeStruct(ymix.shape, F32), jax.ShapeDtypeStruct((nc, N_STATE, SSD_W), F32)),
        scratch_shapes=[pltpu.VMEM((N_STATE, SSD_W), F32), pltpu.VMEM((CHUNK, LANE), F32)],
        aliases={3: 0}, name=name, args=(xact, proj, proj, ymix, bias_pad, alog_pad, alogx, dxp, normw))
    return ((ycat, hprev), jouts) if jobs else (ycat, hprev)


def _ssd_bwd(xact, proj, dycat, hprev, bias_pad, alog_pad, alogx, dxp, normw, *, name, jobs=()):
    s = xact.shape[0]
    nc = s // CHUNK
    l = CHUNK

    def body(xa_ref, dt_ref, z_ref, dy_ref, hp_ref, bias_ref, alp_ref, alx_ref, dx_ref, nw_ref,
             dxa_ref, ddt_ref, dz_ref, dnw_ref, small_ref, dht, cst, accx, dcsx_s, ddtx_s):
        step = pl.program_id(0)

        @pl.when(step == 0)
        def _():
            dht[...] = jnp.zeros_like(dht)
            accx[...] = jnp.zeros_like(accx)
            dnw_ref[...] = jnp.zeros_like(dnw_ref)
            small_ref[...] = jnp.zeros_like(small_ref)

        dtr = dt_ref[...]
        q = _ssd_prep(dtr, bias_ref[...], alp_ref[...], alx_ref[...])
        cst[...] = q["cs"].T
        causal = q["tril"] > 0.0
        eye = _iota((l, l), 0) == _iota((l, l), 1)
        lane = _iota((l, LANE), 1)
        dcs_head = jnp.zeros((l, LANE), F32)
        for g in range(N_GROUP):
            sl = slice(g * GROUP_W, (g + 1) * GROUP_W)
            slb = slice(SSD_W + g * N_STATE, SSD_W + (g + 1) * N_STATE)
            slc = slice(SSD_W + N_GROUP * N_STATE + g * N_STATE, SSD_W + N_GROUP * N_STATE + (g + 1) * N_STATE)
            xs_g, bg, cg = xa_ref[:, sl], xa_ref[:, slb], xa_ref[:, slc]
            ht_g = hp_ref[:, sl]
            dxp_g = dx_ref[:, sl]
            y, f = _ssd_group_fwd(q, g, xs_g, bg, cg, ht_g, cst, causal, dxp_g)
            z_g, nw_g = z_ref[:, sl], nw_ref[:, sl]
            _o, (sz, silu, rs, yn) = _gated_norm_fwd(y, z_g, nw_g)
            dout = dy_ref[:, sl]
            dnw_ref[:, sl] += jnp.sum(dout * yn, axis=0, keepdims=True)
            dyn = dout * nw_g
            dyf = rs * (dyn - yn * jnp.mean(dyn * yn, axis=1, keepdims=True))
            dy = dyf * silu
            dz_ref[:, sl] = dyf * y * sz * (1.0 + z_g * (1.0 - sz))
            accx[0:1, sl] += jnp.sum(dy * xs_g, axis=0, keepdims=True)
            dyo = dy * f["ex"]
            dcg = _dot_nt(dyo, ht_g)
            dht_prev = _dot_tn(cg, dyo)
            dcsx = dy * f["yoff"]
            xdt = f["xdt"]
            dxdt = jnp.zeros((l, GROUP_W), F32)
            dcb = jnp.zeros((l, l), F32)
            for j in range(4):
                h = 4 * g + j
                lm = _decay_mat(q["cs"], cst, h, causal)
                sc = f["cb"] * lm
                mask = _head_mask(j)
                ds_ = jnp.where(causal, _dot_nt(jnp.where(mask, dy, 0.0), xdt), 0.0)
                dxdt = jnp.where(mask, _dot_tn(sc, dy), dxdt)
                dcb = dcb + ds_ * lm
                m = ds_ * sc
                rsum = jnp.sum(m, axis=1, keepdims=True)
                csum = jnp.sum(m, axis=0, keepdims=True)
                csum_col = jnp.sum(jnp.where(eye, csum, 0.0), axis=1, keepdims=True)
                dcs_head = dcs_head + jnp.where(lane == h, rsum - csum_col, 0.0)
            dhn = dht[:, sl]
            etot = jnp.exp(f["totx"])
            dxd = _dot(bg, dhn)
            dbg = _dot_nt(xdt * f["dsx"], dhn)
            dxdt = dxdt + dxd * f["dsx"]
            qq = dxd * xdt * f["dsx"]
            dcsx = dcsx - qq
            dtot = jnp.sum(qq, axis=0, keepdims=True) + jnp.sum(dhn * ht_g, axis=0, keepdims=True) * etot
            dht[:, sl] = etot * dhn + dht_prev
            dcg = dcg + _dot(dcb, bg)
            dbg = dbg + _dot_tn(dcb, cg)
            dxa_ref[:, sl] = dxdt * f["dtx"] + dy * dxp_g
            dxa_ref[:, slb] = dbg
            dxa_ref[:, slc] = dcg
            dcsx_s[:, sl] = dcsx
            ddtx_s[:, sl] = dxdt * xs_g
            accx[2:3, sl] = dtot
        triu = (_iota((l, l), 1) >= _iota((l, l), 0)).astype(F32)
        dax = _dotx(triu, dcsx_s[...]) + accx[2:3, :]
        accx[1:2, :] += jnp.sum(dax * q["dtx"], axis=0, keepdims=True)
        reduce = (jnp.right_shift(_iota((SSD_W, LANE), 0), 6) == _iota((SSD_W, LANE), 1)).astype(F32)
        ddt = _dotx(ddtx_s[...] + dax * q["ax"], reduce)
        da_head = _dotx(triu, dcs_head)
        ddt = ddt + da_head * q["a_head"]
        small_ref[1:2, :] += jnp.sum(da_head * q["dt"], axis=0, keepdims=True)
        ddtr = ddt * _sigmoid(dtr + bias_ref[...])
        ddt_ref[...] = ddtr
        small_ref[0:1, :] += jnp.sum(ddtr, axis=0, keepdims=True)

        @pl.when(step == nc - 1)
        def _():
            red = _dotx(accx[...], reduce)
            d_a = small_ref[1:2, :] + red[1:2, :]
            small_ref[1:2, :] = d_a * q["a_head"]
            small_ref[2:3, :] = red[0:1, :]

    rev = lambda c: nc - 1 - c
    par = lambda w: pl.BlockSpec((1, w), lambda c: (0, 0))
    outs, jouts = _hosted(
        body, jobs, grid=(nc,),
        in_specs=[pl.BlockSpec((CHUNK, XBC), lambda c: (rev(c), 0)),
                  pl.BlockSpec((CHUNK, LANE), lambda c: (rev(c), COL_DT // LANE)),
                  pl.BlockSpec((CHUNK, SSD_W), lambda c: (rev(c), COL_Z // SSD_W)),
                  pl.BlockSpec((CHUNK, SSD_W), lambda c: (rev(c), 1)),
                  pl.BlockSpec((None, N_STATE, SSD_W), lambda c: (rev(c), 0, 0)),
                  par(LANE), par(LANE), par(SSD_W), par(SSD_W), par(SSD_W)],
        out_specs=(pl.BlockSpec((CHUNK, XBC), lambda c: (rev(c), 0)),
                   pl.BlockSpec((CHUNK, LANE), lambda c: (rev(c), 0)),
                   pl.BlockSpec((CHUNK, SSD_W), lambda c: (rev(c), 0)),
                   par(SSD_W), pl.BlockSpec((SUBLANE, LANE), lambda c: (0, 0))),
        out_shape=(jax.ShapeDtypeStruct((s, XBC), F32), jax.ShapeDtypeStruct((s, LANE), F32),
                   jax.ShapeDtypeStruct((s, SSD_W), F32), jax.ShapeDtypeStruct((1, SSD_W), F32),
                   jax.ShapeDtypeStruct((SUBLANE, LANE), F32)),
        scratch_shapes=[pltpu.VMEM((N_STATE, SSD_W), F32), pltpu.VMEM((CHUNK, LANE), F32),
                        pltpu.VMEM((SUBLANE, SSD_W), F32), pltpu.VMEM((CHUNK, SSD_W), F32),
                        pltpu.VMEM((CHUNK, SSD_W), F32)],
        name=name, args=(xact, proj, proj, dycat, hprev, bias_pad, alog_pad, alogx, dxp, normw))
    return (tuple(outs), jouts) if jobs else tuple(outs)


def _blockdiag(w):
    w2 = w.reshape(N_HEAD // 2, 2, HEAD_P, HEAD_P)
    z = jnp.zeros((N_HEAD // 2, HEAD_P, HEAD_P), w.dtype)
    top = jnp.concatenate([w2[:, 0], z], axis=2)
    bot = jnp.concatenate([z, w2[:, 1]], axis=2)
    return jnp.concatenate([top, bot], axis=1)


def _unblockdiag(wbd):
    a = wbd[:, :HEAD_P, :HEAD_P]
    b = wbd[:, HEAD_P:, HEAD_P:]
    return jnp.stack([a, b], axis=1).reshape(N_HEAD, HEAD_P, HEAD_P)


def _pad_rows8(w):
    return jnp.concatenate([w, jnp.zeros((SUBLANE - w.shape[0], w.shape[1]), w.dtype)], axis=0)


def _pad_lane(v):
    return jnp.concatenate([v, jnp.zeros((1, LANE - v.shape[1]), v.dtype)], axis=1)


class _NoExchange:
    def ride(self, host):
        return []

    def done(self, jobs, outs, w):
        pass

    def grad(self, name, val):
        pass

    def small(self, raw):
        pass


def _local_step(x, p, tgt, w, hooks=_NoExchange()):
    cw_l = _pad_rows8(w["lru_conv_w"])
    cw_s = _pad_rows8(w["ssd_conv_w"])
    wa_bd = _blockdiag(w["lru_gate_a_w"])
    wx_bd = _blockdiag(w["lru_gate_x_w"])
    ba = w["lru_gate_a_b"].reshape(1, LRU_W)
    bx = w["lru_gate_x_b"].reshape(1, LRU_W)
    bias_pad = _pad_lane(w["ssd_dt_bias"])
    alog_pad = _pad_lane(w["ssd_a_log"])
    alogx = jnp.repeat(w["ssd_a_log"], HEAD_P, axis=1)
    dxp = jnp.repeat(w["ssd_d"], HEAD_P, axis=1)

    def host(fn, *a, name, **k):
        jobs = hooks.ride(name)
        res = fn(*a, name=name, jobs=jobs, **k)
        if jobs:
            res, jouts = res
            hooks.done(jobs, jouts, w)
        return res

    def grad(n, val):
        g[n] = val
        hooks.grad(n, val)

    proj = host(_mm, x, w["w_in"], "nn", tm=512, tn=512, name="in_proj")
    ymix, h_lru = host(_lru_fwd, proj, cw_l, w["lru_conv_b"], wa_bd, ba, wx_bd, bx, w["lru_a_param"], name="lru_fwd")
    xact = _conv_silu_fwd(proj, cw_s, w["ssd_conv_b"], col0=COL_XBC, width=XBC, ct=256, name="ssd_conv_fwd")
    ycat, hprev = _ssd_fwd(xact, proj, ymix, bias_pad, alog_pad, alogx, dxp, w["ssd_norm_w"], name="ssd_fwd")
    mix = _mm(ycat, w["w_out"], "nn", tm=512, tn=512, name="out_proj")
    x1 = _ln_fwd(x, mix, w["ln1_g"], w["ln1_b"], name="ln1_fwd")
    pre = host(_mm, x1, w["w_ff1"], "nn", tm=512, tn=512, name="ff1")
    ff = _mm(pre, w["w_ff2"], "nn", tm=512, tn=512, a_fn=_relu2, name="ff2")
    x2 = _ln_fwd(x1, ff, w["ln2_g"], w["ln2_b"], name="ln2_fwd")
    gpre = _mm(x2, w["w_ple_gate"], "nn", tm=512, tn=512, name="ple_gate")
    ple = _mm(p, w["w_ple"], "nn", tm=512, tn=512, name="ple_proj")
    loss, dgpre, dple, dt3, dg3, db3 = _head(x2, gpre, ple, w["ln3_g"], w["ln3_b"], tgt, name="head")

    g = {}
    g["ln3_g"], g["ln3_b"] = dg3, db3
    grad("w_ple_gate", _mm(x2, dgpre, "tn", tm=512, tn=512, out_dtype=BF16, name="d_w_ple_gate"))
    grad("w_ple", _mm(p, dple, "tn", tm=256, tn=512, dest_major=True, out_dtype=BF16, name="d_w_ple"))
    dx2_mm = host(_mm, dgpre, w["w_ple_gate"], "nt", tm=512, tn=512, name="d_x2")
    dt2, g["ln2_g"], g["ln2_b"] = _ln_bwd(x1, ff, w["ln2_g"], [dt3, dx2_mm], [ALPHA, 1.0], name="ln2_bwd")
    grad("w_ff2", host(_mm, pre, dt2, "tn", tm=512, tn=512, a_fn=_relu2, out_dtype=BF16, name="d_w_ff2"))
    dpre = host(_mm, dt2, w["w_ff2"], "nt", tm=512, tn=512, extra=pre,
                epi=lambda acc, pv: acc * 2.0 * jnp.maximum(pv, 0.0), name="d_pre")
    grad("w_ff1", host(_mm, x1, dpre, "tn", tm=512, tn=512, dest_major=True, out_dtype=BF16, name="d_w_ff1"))
    dx1_mm = host(_mm, dpre, w["w_ff1"], "nt", tm=512, tn=512, name="d_x1")
    dt1, g["ln1_g"], g["ln1_b"] = _ln_bwd(x, mix, w["ln1_g"], [dt2, dx1_mm], [ALPHA, 1.0], name="ln1_bwd")
    grad("w_out", host(_mm, ycat, dt1, "tn", tm=512, tn=512, out_dtype=BF16, name="d_w_out"))
    dycat = host(_mm, dt1, w["w_out"], "nt", tm=512, tn=512, name="d_ycat")
    dxl, dgl, dcwb_l, dwa, dwx = host(_lru_bwd, proj, dycat, h_lru, cw_l, w["lru_conv_b"], wa_bd, ba, wx_bd, bx,
                                      w["lru_a_param"], name="lru_bwd")
    dxact, ddt, dz, g["ssd_norm_w"], small = host(_ssd_bwd, xact, proj, dycat, hprev, bias_pad, alog_pad, alogx, dxp,
                                                   w["ssd_norm_w"], name="ssd_bwd")
    dxbc, dcwb_s = _conv_silu_bwd(proj, dxact, cw_s, w["ssd_conv_b"], col0=COL_XBC, width=XBC, ct=256,
                                  name="ssd_conv_bwd")
    s = x.shape[0]
    dproj = jnp.concatenate([dxl, dgl, dz, dxbc, ddt, jnp.zeros((s, D_IN_PAD - COL_DT - LANE), F32)], axis=1)

    g["lru_conv_w"] = dcwb_l[0:4]
    g["lru_conv_b"] = dcwb_l[4:5]
    g["lru_gate_a_b"] = dcwb_l[5:6]
    g["lru_gate_x_b"] = dcwb_l[6:7]
    g["lru_a_param"] = dcwb_l[7:8]
    g["lru_gate_a_w"] = _unblockdiag(dwa)
    g["lru_gate_x_w"] = _unblockdiag(dwx)
    g["ssd_conv_w"] = dcwb_s[0:4]
    g["ssd_conv_b"] = dcwb_s[4:5]
    g["ssd_dt_bias"] = small[0:1, :N_HEAD]
    g["ssd_a_log"] = small[1:2, :N_HEAD]
    g["ssd_d"] = small[2:3, :N_HEAD]
    rows = jnp.concatenate([g[n] for n in ("ssd_norm_w", "ln1_g", "ln1_b", "ln2_g", "ln2_b", "ln3_g", "ln3_b")]
                           + [jnp.zeros((1, D_MODEL), F32)], axis=0)
    raw = dict(lru=dcwb_l, ssd=dcwb_s, gate_a=g["lru_gate_a_w"].reshape(N_HEAD * HEAD_P, HEAD_P),
               gate_x=g["lru_gate_x_w"].reshape(N_HEAD * HEAD_P, HEAD_P), heads=small, rows=rows)
    hooks.small(raw)
    grad("w_in", host(_mm, x, dproj, "tn", tm=512, tn=512, out_dtype=BF16, name="d_w_in"))
    grad_x = host(_mm, dproj, w["w_in"], "nt", tm=256, tn=512, extra=dt1, epi=lambda acc, e: acc + ALPHA * e,
                  name="d_x")
    return loss[0, 0], grad_x, g, raw


ANY_SPEC = pl.BlockSpec(memory_space=pl.ANY)


def _mesh_pos():
    return lax.axis_index("x"), lax.axis_index("y"), lax.axis_index("c")


def _remote(src, dst, send, recv, k, to):
    return pltpu.make_async_remote_copy(src_ref=src, dst_ref=dst, send_sem=send.at[k], recv_sem=recv.at[k],
                                        device_id=to, device_id_type=MESH_T)


class _Job:
    N_SEM = 7

    def __init__(self, kind, inp):
        self.kind, self.inp = kind, inp
        shape = {"gather": (N_DEV,) + inp.shape, "pair": (4,) + inp.shape[1:], "chip": inp.shape}[kind]
        self.out = jax.ShapeDtypeStruct(shape, inp.dtype)

    def _places(self):
        x, y, c = _mesh_pos()
        return (x, y, c), (x, y, 1 - c), [(1 - x, y), (x, 1 - y), (1 - x, 1 - y)]

    def start(self, inp, out, send, recv, loc):
        me, sibling, chips = self._places()
        x, y, c = me
        if self.kind == "gather":
            mine = out.at[4 * x + 2 * y + c]
            pltpu.make_async_copy(inp, mine, loc.at[0]).start()
            _remote(inp, mine, send, recv, 0, sibling).start()
            for j, chip in enumerate(chips):
                _remote(inp, mine, send, recv, 1 + j, (*chip, c)).start()
        elif self.kind == "pair":
            for k in range(4):
                _remote(inp.at[2 * k + (1 - c)], out.at[k], send, recv, k, sibling).start()
        else:
            kme = 2 * x + y
            pltpu.make_async_copy(inp.at[kme], out.at[kme], loc.at[0]).start()
            for j, (tx, ty) in enumerate(chips):
                _remote(inp.at[2 * tx + ty], out.at[kme], send, recv, j, (tx, ty, c)).start()

    def finish(self, inp, out, send, recv, loc):
        me, sibling, chips = self._places()
        x, y, c = me
        if self.kind == "gather":
            blk = lambda px, py, pc: out.at[4 * px + 2 * py + pc]
            mine = blk(*me)
            for j, chip in enumerate(chips):
                landed = blk(*chip, c)
                _remote(landed, landed, send, recv, 1 + j, me).wait_recv()
                _remote(landed, landed, send, recv, 4 + j, sibling).start()
            _remote(inp, blk(*sibling), send, recv, 0, me).wait_recv()
            for j, chip in enumerate(chips):
                _remote(inp, blk(*chip, 1 - c), send, recv, 4 + j, me).wait_recv()
            for k in range(7):
                _remote(inp, mine, send, recv, k, sibling).wait_send()
            pltpu.make_async_copy(inp, mine, loc.at[0]).wait()
        elif self.kind == "pair":
            for k in range(4):
                _remote(inp.at[2 * k + (1 - c)], out.at[k], send, recv, k, sibling).wait()
        else:
            kme = 2 * x + y
            for j, (tx, ty) in enumerate(chips):
                _remote(inp.at[kme], out.at[2 * tx + ty], send, recv, j, (tx, ty, c)).wait_recv()
            for j, (tx, ty) in enumerate(chips):
                _remote(inp.at[2 * tx + ty], out.at[kme], send, recv, j, (tx, ty, c)).wait_send()
            pltpu.make_async_copy(inp.at[kme], out.at[kme], loc.at[0]).wait()


def _job_scratch(jobs):
    sem = pltpu.SemaphoreType.DMA
    return [s for _ in jobs for s in (sem((_Job.N_SEM,)), sem((_Job.N_SEM,)), sem((1,)))]


def _run_jobs(jobs, method, jins, jouts, jsems):
    for q, job in enumerate(jobs):
        getattr(job, method)(jins[q], jouts[q], *jsems[3 * q:3 * q + 3])


def _exchange(jobs, *, name):
    n = len(jobs)

    def body(*refs):
        jins, jouts, jsems = refs[:n], refs[n:2 * n], refs[2 * n:]
        _run_jobs(jobs, "start", jins, jouts, jsems)
        _run_jobs(jobs, "finish", jins, jouts, jsems)

    return _pcall(body, in_specs=[ANY_SPEC] * n, out_specs=[ANY_SPEC] * n, out_shape=[j.out for j in jobs],
                  scratch_shapes=_job_scratch(jobs), name=name)(*[j.inp for j in jobs])


def _hosted(body, jobs, *, grid, in_specs, out_specs, out_shape, args, name, scratch_shapes=(), aliases=None):
    in_specs, out_specs, out_shape = list(in_specs), list(out_specs), list(out_shape)
    scratch_shapes = list(scratch_shapes)
    n_in, n_out, n_scr, nj = len(in_specs), len(out_specs), len(scratch_shapes), len(jobs)
    sem = ("arbitrary",) * len(grid)
    kw = dict(input_output_aliases=aliases) if aliases else {}
    if not jobs:
        res = _pcall(body, grid=grid, in_specs=in_specs, out_specs=out_specs, out_shape=out_shape,
                     scratch_shapes=scratch_shapes, name=name, compiler_params=_cparams(sem), **kw)(*args)
        return list(res), []

    def full(*refs):
        ins, jins = refs[:n_in], refs[n_in:n_in + nj]
        o0 = n_in + nj
        outs, jouts = refs[o0:o0 + n_out], refs[o0 + n_out:o0 + n_out + nj]
        s0 = o0 + n_out + nj
        scr, jsems = refs[s0:s0 + n_scr], refs[s0 + n_scr:]
        first = pl.program_id(0) == 0
        last = pl.program_id(0) == grid[0] - 1
        for ax in range(1, len(grid)):
            first = jnp.logical_and(first, pl.program_id(ax) == 0)
            last = jnp.logical_and(last, pl.program_id(ax) == grid[ax] - 1)

        @pl.when(first)
        def _():
            _run_jobs(jobs, "start", jins, jouts, jsems)

        body(*ins, *outs, *scr)

        @pl.when(last)
        def _():
            _run_jobs(jobs, "finish", jins, jouts, jsems)

    res = _pcall(full, grid=grid, in_specs=in_specs + [ANY_SPEC] * nj, out_specs=out_specs + [ANY_SPEC] * nj,
                 out_shape=out_shape + [j.out for j in jobs], scratch_shapes=scratch_shapes + _job_scratch(jobs),
                 name=name, compiler_params=_cparams(sem), **kw)(*args, *[j.inp for j in jobs])
    return list(res[:n_out]), list(res[n_out:])


def _pair_add(g8, r4, cidx, *, name):
    _, r, c = g8.shape
    tr = min(r, ROW_TILE)

    def body(c_ref, g_ref, r_ref, o_ref):
        o_ref[...] = (g_ref[...].astype(F32) + r_ref[...].astype(F32)).astype(BF16)

    return _pcall(
        body,
        grid_spec=pltpu.PrefetchScalarGridSpec(
            num_scalar_prefetch=1, grid=(4, r // tr),
            in_specs=[pl.BlockSpec((None, tr, c), lambda k, i, cr: (2 * k + cr[0], i, 0)),
                      pl.BlockSpec((None, tr, c), lambda k, i, cr: (k, i, 0))],
            out_specs=pl.BlockSpec((None, tr, c), lambda k, i, cr: (k, i, 0))),
        out_shape=jax.ShapeDtypeStruct((4, r, c), BF16), name=name,
        compiler_params=_cparams(("parallel", "parallel")))(cidx, g8, r4)


def _adam_update(g, w_ref, m_ref, v_ref, g_ref, d_ref, mo_ref, vo_ref):
    c1 = 1.0 - ADAM_B1 ** ADAM_STEP
    c2 = 1.0 - ADAM_B2 ** ADAM_STEP
    m2 = ADAM_B1 * m_ref[...] + (1.0 - ADAM_B1) * g
    v2 = ADAM_B2 * v_ref[...] + (1.0 - ADAM_B2) * (g * g)
    g_ref[...] = g
    mo_ref[...] = m2
    vo_ref[...] = v2
    d_ref[...] = -ADAM_LR * ((m2 / c1) / (jnp.sqrt(v2 / c2) + ADAM_EPS) + ADAM_WD * w_ref[...])


def _adamw_rows(srcs, items, own_cols, me1, *, name):
    ns, ni, no = len(srcs), len(items), len(own_cols)
    full = lambda a: pl.BlockSpec(a.shape, lambda i, me: (0,) * a.ndim)
    in_specs = [full(a) for a in srcs]
    args = list(srcs)
    for (si, _r0, w, _m, _v) in own_cols:
        a = srcs[si]
        in_specs.append(pl.BlockSpec((N_DEV, a.shape[1], w.shape[1]), lambda i, me: (0, 0, me[0])))
        args.append(a)
    out_specs, out_shape = [], []
    for (_si, _r0, w, m, v) in list(items) + list(own_cols):
        in_specs += [full(w)] * 3
        args += [w, m, v]
        out_specs += [full(w)] * 4
        out_shape += [jax.ShapeDtypeStruct(w.shape, F32)] * 4

    def body(me_ref, *refs):
        src_refs, own_refs = refs[:ns], refs[ns:ns + no]
        wmv = refs[ns + no:ns + no + 3 * (ni + no)]
        outs = refs[ns + no + 3 * (ni + no):]
        for q, (si, r0, w, _m, _v) in enumerate(list(items) + list(own_cols)):
            nr, cw = w.shape
            gref = src_refs[si] if q < ni else own_refs[q - ni]
            g = gref[0, r0:r0 + nr, 0:cw]
            for d in range(1, N_DEV):
                g = g + gref[d, r0:r0 + nr, 0:cw]
            _adam_update(g, *wmv[3 * q:3 * q + 3], *outs[4 * q:4 * q + 4])

    res = _pcall(
        body,
        grid_spec=pltpu.PrefetchScalarGridSpec(num_scalar_prefetch=1, grid=(1,), in_specs=in_specs, out_specs=out_specs),
        out_shape=out_shape, name=name, compiler_params=_cparams(("arbitrary",)))(me1, *args)
    return [tuple(res[4 * q:4 * q + 4]) for q in range(ni + no)]


def _adamw(gsrc, w, m, v, *, name):
    k, r, c = gsrc.shape
    tr = ROW_TILE if r % ROW_TILE == 0 else r

    def body(gs_ref, w_ref, m_ref, v_ref, g_ref, d_ref, mo_ref, vo_ref):
        g = gs_ref[0].astype(F32)
        for q in range(1, k):
            g = g + gs_ref[q].astype(F32)
        _adam_update(g, w_ref, m_ref, v_ref, g_ref, d_ref, mo_ref, vo_ref)

    row = pl.BlockSpec((tr, c), lambda i: (i, 0))
    sd = jax.ShapeDtypeStruct((r, c), F32)
    return _pcall(body, grid=(r // tr,), in_specs=[pl.BlockSpec((k, tr, c), lambda i: (0, i, 0)), row, row, row],
                  out_specs=(row, row, row, row), out_shape=(sd, sd, sd, sd), name=name,
                  compiler_params=_cparams(("parallel",)))(gsrc, w, m, v)


WEIGHTS = ['w_in', 'lru_conv_w', 'lru_conv_b', 'lru_gate_a_w', 'lru_gate_a_b', 'lru_gate_x_w', 'lru_gate_x_b',
           'lru_a_param', 'ssd_conv_w', 'ssd_conv_b', 'ssd_dt_bias', 'ssd_a_log', 'ssd_d', 'ssd_norm_w', 'w_out',
           'ln1_g', 'ln1_b', 'w_ff1', 'w_ff2', 'ln2_g', 'ln2_b', 'w_ple_gate', 'w_ple', 'ln3_g', 'ln3_b']
BIG = ['w_in', 'w_out', 'w_ff1', 'w_ff2', 'w_ple_gate', 'w_ple']
COL_SHARDED = ('w_in', 'w_ff1', 'w_ple')
CONV = ['lru_conv_w', 'ssd_conv_w']
REPL = [n for n in WEIGHTS if n not in BIG and n not in CONV]
CONV_CH = {'lru_conv_w': LRU_W, 'ssd_conv_w': XBC}


def _to_dest_major(name, gfull):
    if name == 'w_in':
        gfull = gfull[:, :D_IN]
    if name in COL_SHARDED:
        r, cfull = gfull.shape
        return gfull.reshape(r, N_DEV, cfull // N_DEV).transpose(1, 0, 2)
    rfull, cdim = gfull.shape
    return gfull.reshape(N_DEV, rfull // N_DEV, cdim)


def _full_weight(name, gathered):
    if name in COL_SHARDED:
        _, r, cs = gathered.shape
        full = gathered.transpose(1, 0, 2).reshape(r, N_DEV * cs)
    else:
        _, rs, cdim = gathered.shape
        full = gathered.reshape(N_DEV * rs, cdim)
    if name == 'w_in':
        full = jnp.concatenate([full, jnp.zeros((D_MODEL, D_IN_PAD - D_IN), full.dtype)], axis=1)
    return full


SMALL_SRC = ("lru", "ssd", "heads", "rows", "gate_a", "gate_x")
AG_HOSTS = {"in_proj": ("w_ff1",), "lru_fwd": ("w_out", "w_ple_gate", "w_ple"), "ff1": ("w_ff2",)}
PAIR_HOSTS = ("d_x2", "d_pre", "d_x1", "d_ycat", "d_x")
CHIP_HOSTS = {"lru_bwd": ("w_ple_gate", "w_ple", "w_ff2"), "ssd_bwd": ("w_ff1", "w_out")}
SMALL_HOST = "d_w_in"


class _Schedule:
    def __init__(self, shards, cidx):
        self.shards, self.cidx = shards, cidx
        self.pair, self.chip, self.small_jobs = [], [], []
        self.dest, self.summed, self.gathered_small = {}, {}, {}
        self.tags = []

    def ride(self, host):
        tags = []
        if host in AG_HOSTS:
            tags = [("weight", n, self.shards[n]) for n in AG_HOSTS[host]]
        elif host in PAIR_HOSTS or host in CHIP_HOSTS or host == "flush":
            tags = [("pair", n, a) for n, a in self.pair]
            self.pair = []
            if host not in PAIR_HOSTS:
                take = [t for t in self.chip if host == "flush" or t[0] in CHIP_HOSTS[host]]
                tags += [("chip", n, a) for n, a in take]
                self.chip = [t for t in self.chip if not any(t is u for u in take)]
        elif host == SMALL_HOST:
            tags = [("small", n, a) for n, a in self.small_jobs]
        self.tags = tags
        return [_Job({"weight": "gather", "small": "gather"}.get(kind, kind), a) for kind, _n, a in tags]

    def done(self, jobs, outs, w):
        for (kind, n, _a), o in zip(self.tags, outs):
            if kind == "weight":
                w[n] = _full_weight(n, o)
            elif kind == "small":
                self.gathered_small[n] = o
            elif kind == "pair":
                self.chip.append((n, _pair_add(self.dest[n], o, self.cidx, name="rs_pair_add_" + n)))
            else:
                self.summed[n] = o

    def grad(self, name, val):
        self.dest[name] = val if val.ndim == 3 else _to_dest_major(name, val)
        self.pair.append((name, self.dest[name]))

    def small(self, raw):
        self.small_jobs = [(k, raw[k]) for k in SMALL_SRC]

    def flush(self):
        step = 0
        while self.pair or self.chip:
            jobs = self.ride("flush")
            self.done(jobs, _exchange(jobs, name="rs_flush_%d" % step), None)
            step += 1


def kernel(x, p, w_in, lru_conv_w, lru_conv_b, lru_gate_a_w, lru_gate_a_b, lru_gate_x_w, lru_gate_x_b, lru_a_param, ssd_conv_w, ssd_conv_b, ssd_dt_bias, ssd_a_log, ssd_d, ssd_norm_w, w_out, ln1_g, ln1_b, w_ff1, w_ff2, ln2_g, ln2_b, w_ple_gate, w_ple, ln3_g, ln3_b, loss_target, m_w_in, m_lru_conv_w, m_lru_conv_b, m_lru_gate_a_w, m_lru_gate_a_b, m_lru_gate_x_w, m_lru_gate_x_b, m_lru_a_param, m_ssd_conv_w, m_ssd_conv_b, m_ssd_dt_bias, m_ssd_a_log, m_ssd_d, m_ssd_norm_w, m_w_out, m_ln1_g, m_ln1_b, m_w_ff1, m_w_ff2, m_ln2_g, m_ln2_b, m_w_ple_gate, m_w_ple, m_ln3_g, m_ln3_b, v_w_in, v_lru_conv_w, v_lru_conv_b, v_lru_gate_a_w, v_lru_gate_a_b, v_lru_gate_x_w, v_lru_gate_x_b, v_lru_a_param, v_ssd_conv_w, v_ssd_conv_b, v_ssd_dt_bias, v_ssd_a_log, v_ssd_d, v_ssd_norm_w, v_w_out, v_ln1_g, v_ln1_b, v_w_ff1, v_w_ff2, v_ln2_g, v_ln2_b, v_w_ple_gate, v_w_ple, v_ln3_g, v_ln3_b):
    given = dict(locals())
    wsh = {n: given[n][0] for n in WEIGHTS}
    msh = {n: given["m_" + n][0] for n in WEIGHTS}
    vsh = {n: given["v_" + n][0] for n in WEIGHTS}
    xi, yi, ci = _mesh_pos()
    me = 4 * xi + 2 * yi + ci

    shards = {n: wsh[n].astype(BF16) for n in BIG}
    conv_pack = jnp.concatenate([_pad_rows8(wsh[n]) for n in CONV], axis=1)
    g_in, gconv = _exchange([_Job("gather", shards['w_in']), _Job("gather", conv_pack)], name="ag_first")
    full = {'w_in': _full_weight('w_in', g_in)}
    c0 = 0
    for n in CONV:
        cw = CONV_CH[n] // N_DEV
        full[n] = gconv[:, :4, c0:c0 + cw].transpose(1, 0, 2).reshape(4, CONV_CH[n])
        c0 += cw
    for n in REPL:
        full[n] = wsh[n].reshape(1, -1) if wsh[n].ndim == 1 else wsh[n]

    sched = _Schedule(shards, jnp.reshape(ci, (1,)).astype(jnp.int32))
    loss_local, grad_x, g, raw = _local_step(x[0], p[0, 0], loss_target[0], full, sched)
    loss = lax.psum(loss_local, ("x", "y", "c"))
    sched.flush()
    summed, gat = sched.summed, sched.gathered_small

    outs = {}
    for n in BIG:
        outs[n] = _adamw(summed[n], wsh[n], msh[n], vsh[n], name="adamw_" + n)
    for n, k in (("lru_gate_a_w", "gate_a"), ("lru_gate_x_w", "gate_x")):
        flat = lambda a: a.reshape(N_HEAD * HEAD_P, HEAD_P)
        res = _adamw(gat[k], flat(wsh[n]), flat(msh[n]), flat(vsh[n]), name="adamw_" + n)
        outs[n] = tuple(r.reshape(N_HEAD, HEAD_P, HEAD_P) for r in res)
    row_items = [("lru_conv_b", 0, 4), ("lru_gate_a_b", 0, 5), ("lru_gate_x_b", 0, 6), ("lru_a_param", 0, 7),
                 ("ssd_conv_b", 1, 4), ("ssd_dt_bias", 2, 0), ("ssd_a_log", 2, 1), ("ssd_d", 2, 2),
                 ("ssd_norm_w", 3, 0), ("ln1_g", 3, 1), ("ln1_b", 3, 2), ("ln2_g", 3, 3), ("ln2_b", 3, 4),
                 ("ln3_g", 3, 5), ("ln3_b", 3, 6)]
    vec = lambda a: a.reshape(1, -1)
    items = [(si, r0, vec(wsh[n]), vec(msh[n]), vec(vsh[n])) for n, si, r0 in row_items]
    own = [(si, 0, wsh[n], msh[n], vsh[n]) for n, si in (("lru_conv_w", 0), ("ssd_conv_w", 1))]
    me1 = jnp.reshape(me, (1,)).astype(jnp.int32)
    res = _adamw_rows([gat[k] for k in SMALL_SRC[:4]], items, own, me1, name="adamw_small")
    for (n, _si, _r0), r4 in zip(row_items, res[:len(row_items)]):
        outs[n] = tuple(r.reshape(wsh[n].shape) for r in r4)
    for n, r4 in zip(CONV, res[len(row_items):]):
        outs[n] = r4

    ex = lambda a: a[None]
    return (loss, grad_x[None],
            *[ex(outs[n][0]) for n in WEIGHTS], *[ex(outs[n][1]) for n in WEIGHTS],
            *[ex(outs[n][2]) for n in WEIGHTS], *[ex(outs[n][3]) for n in WEIGHTS])
```

```python
import math

import jax
import jax.numpy as jnp
from jax import lax
from jax.experimental import pallas as pl
from jax.experimental.pallas import tpu as pltpu

F32 = jnp.float32
BF16 = jnp.bfloat16
HI = lax.Precision.HIGHEST

N_DEV = 8
D_MODEL = 1024
LRU_W = 1024
SSD_W = 1024
XBC = 2048
N_HEAD = 16
HEAD_P = 64
N_GROUP = 4
GROUP_W = 256
N_STATE = 128
CHUNK = 128
D_FF = 4096
PLE_DIM = 256
D_IN = 5136
D_IN_PAD = 5632
COL_G = 1024
COL_Z = 2048
COL_XBC = 3072
COL_DT = 5120
LRU_C = 8.0
ALPHA = 2.0 ** 0.25
LN_EPS = 1e-5
RMS_EPS = 1e-5
ADAM_LR = 0.001
ADAM_B1 = 0.9
ADAM_B2 = 0.999
ADAM_EPS = 1e-08
ADAM_WD = 0.01
ADAM_STEP = 10
GELU_C = math.sqrt(2.0 / math.pi)
LANE = 128
SUBLANE = 8
VMEM_LIMIT = 48 * 1024 * 1024
MESH_T = pl.DeviceIdType.MESH
NEG_BIG = -1e30


def _pcall(body, **kw):
    return pl.pallas_call(body, **kw)


def _cparams(sem):
    return pltpu.CompilerParams(dimension_semantics=sem, vmem_limit_bytes=VMEM_LIMIT)


def _dot(a, b):
    return jnp.dot(a.astype(BF16), b.astype(BF16), preferred_element_type=F32)


def _dot_nt(a, b):
    return lax.dot_general(a.astype(BF16), b.astype(BF16), (((1,), (1,)), ((), ())), preferred_element_type=F32)


def _dot_tn(a, b):
    return lax.dot_general(a.astype(BF16), b.astype(BF16), (((0,), (0,)), ((), ())), preferred_element_type=F32)


def _dotx(a, b):
    return jnp.dot(a, b, precision=HI, preferred_element_type=F32)


def _sigmoid(x):
    return jax.nn.sigmoid(x)


def _softplus(v):
    return jnp.maximum(v, 0.0) + jnp.log1p(jnp.exp(-jnp.abs(v)))


def _gelu(x):
    th = jnp.tanh(GELU_C * (x + 0.044715 * x * x * x))
    return 0.5 * x * (1.0 + th), th


def _gelu_grad(x, th):
    return 0.5 * (1.0 + th) + 0.5 * x * (1.0 - th * th) * GELU_C * (1.0 + 3.0 * 0.044715 * x * x)


def _iota(shape, dim):
    return lax.broadcasted_iota(jnp.int32, shape, dim)


def _mm(a, b, mode, *, tm, tn, name, a_fn=None, extra=None, epi=None, out_dtype=F32, dest_major=False, jobs=()):
    m = a.shape[1] if mode == "tn" else a.shape[0]
    n = b.shape[0] if mode == "nt" else b.shape[1]
    tm, tn = min(tm, m), min(tn, n)
    if dest_major:
        tn = n // N_DEV
    if mode == "nn":
        m, k = a.shape
        _, n = b.shape
        a_spec = pl.BlockSpec((tm, k), lambda i, j: (i, 0))
        b_spec = pl.BlockSpec((k, tn), lambda i, j: (0, j))
        dims = ((1,), (0,))
    elif mode == "nt":
        m, k = a.shape
        n, _ = b.shape
        a_spec = pl.BlockSpec((tm, k), lambda i, j: (i, 0))
        b_spec = pl.BlockSpec((tn, k), lambda i, j: (j, 0))
        dims = ((1,), (1,))
    else:
        k, m = a.shape
        _, n = b.shape
        a_spec = pl.BlockSpec((k, tm), lambda i, j: (0, i))
        b_spec = pl.BlockSpec((k, tn), lambda i, j: (0, j))
        dims = ((0,), (0,))
    assert m % tm == 0 and n % tn == 0, (name, m, n, tm, tn)
    o_spec = pl.BlockSpec((tm, tn), lambda i, j: (i, j))
    in_specs = [a_spec, b_spec]
    args = [a, b]
    if extra is not None:
        in_specs.append(o_spec)
        args.append(extra)

    def body(*refs):
        a_ref, b_ref, o_ref = refs[0], refs[1], refs[-1]
        av = a_ref[...]
        if a_fn is not None:
            av = a_fn(av)
        acc = lax.dot_general(av.astype(BF16), b_ref[...].astype(BF16), (dims, ((), ())), preferred_element_type=F32)
        if epi is not None:
            acc = epi(acc, refs[2][...])
        o_ref[...] = acc.astype(out_dtype)

    out_shape = jax.ShapeDtypeStruct((m, n), out_dtype)
    if dest_major:
        assert extra is None
        o_spec = pl.BlockSpec((None, tm, tn), lambda i, j: (j, i, 0))
        out_shape = jax.ShapeDtypeStruct((N_DEV, m, tn), out_dtype)
    (out,), jouts = _hosted(body, jobs, grid=(m // tm, n // tn), in_specs=in_specs, out_specs=[o_spec],
                            out_shape=[out_shape], args=args, name=name)
    return (out, jouts) if jobs else out


def _relu2(v):
    r = jnp.maximum(v, 0.0)
    return r * r


ROW_TILE = 256


def _ln_stats(t):
    mu = jnp.mean(t, axis=-1, keepdims=True)
    xc = t - mu
    var = jnp.mean(xc * xc, axis=-1, keepdims=True)
    rstd = lax.rsqrt(var + LN_EPS)
    return xc * rstd, rstd


def _ln_bwd_rows(dy, xhat, rstd, g):
    dxh = dy * g
    m1 = jnp.mean(dxh, axis=-1, keepdims=True)
    m2 = jnp.mean(dxh * xhat, axis=-1, keepdims=True)
    return rstd * (dxh - m1 - xhat * m2)


def _ln_fwd(a, b, g, beta, *, name):
    s, d = a.shape
    row = pl.BlockSpec((ROW_TILE, d), lambda i: (i, 0))
    par = pl.BlockSpec((1, d), lambda i: (0, 0))

    def body(a_ref, b_ref, g_ref, be_ref, y_ref, yb_ref):
        xhat, _ = _ln_stats(ALPHA * a_ref[...] + b_ref[...])
        y = xhat * g_ref[...] + be_ref[...]
        y_ref[...] = y
        yb_ref[...] = y.astype(BF16)

    return _pcall(body, grid=(s // ROW_TILE,), in_specs=[row, row, par, par], out_specs=(row, row),
                  out_shape=(jax.ShapeDtypeStruct((s, d), F32), jax.ShapeDtypeStruct((s, d), BF16)), name=name,
                  compiler_params=_cparams(("parallel",)))(a, b, g, beta)


def _ln_bwd(a, b, g, dys, coefs, *, name):
    s, d = a.shape
    row = pl.BlockSpec((ROW_TILE, d), lambda i: (i, 0))
    par = pl.BlockSpec((1, d), lambda i: (0, 0))
    n = len(dys)

    def body(*refs):
        a_ref, b_ref, g_ref = refs[:3]
        dy_refs = refs[3:3 + n]
        dt_ref, dtb_ref, dg_ref, db_ref = refs[3 + n:]
        xhat, rstd = _ln_stats(ALPHA * a_ref[...] + b_ref[...])
        dy = coefs[0] * dy_refs[0][...]
        for q in range(1, n):
            dy = dy + coefs[q] * dy_refs[q][...]
        dt = _ln_bwd_rows(dy, xhat, rstd, g_ref[...])
        dt_ref[...] = dt
        dtb_ref[...] = dt.astype(BF16)

        @pl.when(pl.program_id(0) == 0)
        def _():
            dg_ref[...] = jnp.zeros_like(dg_ref)
            db_ref[...] = jnp.zeros_like(db_ref)

        dg_ref[...] += jnp.sum(dy * xhat, axis=0, keepdims=True)
        db_ref[...] += jnp.sum(dy, axis=0, keepdims=True)

    return _pcall(body, grid=(s // ROW_TILE,), in_specs=[row, row, par] + [row] * n, out_specs=(row, row, par, par),
                  out_shape=(jax.ShapeDtypeStruct((s, d), F32), jax.ShapeDtypeStruct((s, d), BF16),
                             jax.ShapeDtypeStruct((1, d), F32), jax.ShapeDtypeStruct((1, d), F32)),
                  name=name, compiler_params=_cparams(("arbitrary",)))(a, b, g, *dys)


def _head(x2, gpre, ple, g, beta, tgt, *, name):
    s, d = x2.shape
    row = pl.BlockSpec((ROW_TILE, d), lambda i: (i, 0))
    par = pl.BlockSpec((1, d), lambda i: (0, 0))
    lsp = pl.BlockSpec((1, LANE), lambda i: (0, 0))

    def body(x2_ref, gp_ref, ple_ref, g_ref, be_ref, t_ref, loss_ref, dgp_ref, dple_ref, dt_ref, dg_ref, db_ref):
        gate = _sigmoid(gp_ref[...])
        ple_v = ple_ref[...]
        xhat, rstd = _ln_stats(ALPHA * x2_ref[...] + gate * ple_v)
        err = xhat * g_ref[...] + be_ref[...] - t_ref[...]
        dy = err * (1.0 / d)
        dt = _ln_bwd_rows(dy, xhat, rstd, g_ref[...])
        dt_ref[...] = dt
        dgp_ref[...] = (dt * ple_v * gate * (1.0 - gate)).astype(BF16)
        dple_ref[...] = (dt * gate).astype(BF16)

        @pl.when(pl.program_id(0) == 0)
        def _():
            loss_ref[...] = jnp.zeros_like(loss_ref)
            dg_ref[...] = jnp.zeros_like(dg_ref)
            db_ref[...] = jnp.zeros_like(db_ref)

        loss_ref[...] += 0.5 * jnp.sum(jnp.mean(err * err, axis=-1, keepdims=True))
        dg_ref[...] += jnp.sum(dy * xhat, axis=0, keepdims=True)
        db_ref[...] += jnp.sum(dy, axis=0, keepdims=True)

    sd = jax.ShapeDtypeStruct((s, d), F32)
    sb = jax.ShapeDtypeStruct((s, d), BF16)
    pd = jax.ShapeDtypeStruct((1, d), F32)
    return _pcall(body, grid=(s // ROW_TILE,), in_specs=[row, row, row, par, par, row],
                  out_specs=(lsp, row, row, row, par, par),
                  out_shape=(jax.ShapeDtypeStruct((1, LANE), F32), sb, sb, sd, pd, pd),
                  name=name, compiler_params=_cparams(("arbitrary",)))(x2, gpre, ple, g, beta, tgt)


CONV_R = 256
PAD = SUBLANE


def _shift_down(ext, s):
    if s == 0:
        return ext[PAD:, :]
    return pltpu.roll(ext, s, 0)[PAD:, :]


def _shift_up(ext, s):
    r = ext.shape[0] - PAD
    if s == 0:
        return ext[:r, :]
    return pltpu.roll(ext, r + PAD - s, 0)[:r, :]


def _conv_rows(xpad_ref, r0, w_ref):
    ext = xpad_ref[pl.ds(r0, CONV_R + PAD), :]
    acc = _shift_down(ext, 0) * w_ref[3:4, :]
    for k in range(3):
        acc = acc + _shift_down(ext, 3 - k) * w_ref[k:k + 1, :]
    return acc, ext


def _fill_front_padded(dst_ref, src_ref, s):
    dst_ref[0:PAD, :] = jnp.zeros((PAD, dst_ref.shape[1]), F32)

    def cp(q, _):
        r0 = pl.multiple_of(q * CONV_R, CONV_R)
        dst_ref[pl.ds(pl.multiple_of(PAD + r0, PAD), CONV_R), :] = src_ref[pl.ds(r0, CONV_R), :]
        return 0

    lax.fori_loop(0, s // CONV_R, cp, 0)


def _conv_silu_fwd(proj, w8, b, *, col0, width, ct, name):
    s = proj.shape[0]
    nb = col0 // ct

    def body(x_ref, w_ref, b_ref, o_ref, xpad):
        _fill_front_padded(xpad, x_ref, s)

        def step(q, _):
            r0 = pl.multiple_of(q * CONV_R, CONV_R)
            acc, _e = _conv_rows(xpad, r0, w_ref)
            pre = acc + b_ref[...]
            o_ref[pl.ds(r0, CONV_R), :] = pre * _sigmoid(pre)
            return 0

        lax.fori_loop(0, s // CONV_R, step, 0)

    return _pcall(
        body, grid=(width // ct,),
        in_specs=[pl.BlockSpec((s, ct), lambda j: (0, nb + j)), pl.BlockSpec((SUBLANE, ct), lambda j: (0, j)),
                  pl.BlockSpec((1, ct), lambda j: (0, j))],
        out_specs=pl.BlockSpec((s, ct), lambda j: (0, j)),
        out_shape=jax.ShapeDtypeStruct((s, width), F32),
        scratch_shapes=[pltpu.VMEM((s + PAD, ct), F32)], name=name,
        compiler_params=_cparams(("parallel",)))(proj, w8, b)


def _conv_bwd_rows(dpad_ref, r0, w_ref):
    return _conv_bwd_ext(dpad_ref[pl.ds(r0, CONV_R + PAD), :], w_ref)


def _conv_bwd_ext(ext, w_ref):
    acc = _shift_up(ext, 0) * w_ref[3:4, :]
    for k in range(3):
        acc = acc + _shift_up(ext, 3 - k) * w_ref[k:k + 1, :]
    return acc


def _conv_silu_bwd(proj, dact, w8, b, *, col0, width, ct, name):
    s = proj.shape[0]
    nb = col0 // ct

    def body(x_ref, d_ref, w_ref, b_ref, dx_ref, dwb_ref, xpad, dpad):
        _fill_front_padded(xpad, x_ref, s)
        dpad[pl.ds(s, PAD), :] = jnp.zeros((PAD, ct), F32)
        dwb_ref[...] = jnp.zeros_like(dwb_ref)

        def step(q, _):
            r0 = pl.multiple_of(q * CONV_R, CONV_R)
            acc, ext = _conv_rows(xpad, r0, w_ref)
            pre = acc + b_ref[...]
            sg = _sigmoid(pre)
            dpre = d_ref[pl.ds(r0, CONV_R), :] * sg * (1.0 + pre * (1.0 - sg))
            dpad[pl.ds(r0, CONV_R), :] = dpre
            for k in range(4):
                dwb_ref[k:k + 1, :] += jnp.sum(dpre * _shift_down(ext, 3 - k), axis=0, keepdims=True)
            dwb_ref[4:5, :] += jnp.sum(dpre, axis=0, keepdims=True)
            return 0

        lax.fori_loop(0, s // CONV_R, step, 0)

        def step2(q, _):
            r0 = pl.multiple_of(q * CONV_R, CONV_R)
            dx_ref[pl.ds(r0, CONV_R), :] = _conv_bwd_rows(dpad, r0, w_ref).astype(BF16)
            return 0

        lax.fori_loop(0, s // CONV_R, step2, 0)

    colb = pl.BlockSpec((s, ct), lambda j: (0, j))
    return _pcall(
        body, grid=(width // ct,),
        in_specs=[pl.BlockSpec((s, ct), lambda j: (0, nb + j)), colb, pl.BlockSpec((SUBLANE, ct), lambda j: (0, j)),
                  pl.BlockSpec((1, ct), lambda j: (0, j))],
        out_specs=(colb, pl.BlockSpec((SUBLANE, ct), lambda j: (0, j))),
        out_shape=(jax.ShapeDtypeStruct((s, width), BF16), jax.ShapeDtypeStruct((SUBLANE, width), F32)),
        scratch_shapes=[pltpu.VMEM((s + PAD, ct), F32), pltpu.VMEM((s + PAD, ct), F32)], name=name,
        compiler_params=_cparams(("parallel",)))(proj, dact, w8, b)


LRU_CT = 128


def _row_of(v, r):
    return jnp.sum(jnp.where(_iota((v.shape[0], 1), 0) == r, v, 0.0), axis=0, keepdims=True)


def _scan_fwd(a, u):
    r = a.shape[0]
    row = _iota((r, 1), 0)
    d = 1
    while d < r:
        valid = row >= d
        u = jnp.where(valid, a * pltpu.roll(u, d, 0) + u, u)
        a = jnp.where(valid, a * pltpu.roll(a, d, 0), a)
        d *= 2
    return a, u


def _scan_rev(b, u):
    r = b.shape[0]
    row = _iota((r, 1), 0)
    d = 1
    while d < r:
        valid = row < r - d
        u = jnp.where(valid, b * pltpu.roll(u, r - d, 0) + u, u)
        b = jnp.where(valid, b * pltpu.roll(b, r - d, 0), b)
        d *= 2
    return b, u


def _lru_chunk(xpad, r0, cw_ref, cb, wa, ba, wx, bx, sp):
    acc, ext = _conv_rows(xpad, r0, cw_ref)
    xl = acc + cb
    r = _sigmoid(_dot(xl, wa) + ba)
    i = _sigmoid(_dot(xl, wx) + bx)
    la = -LRU_C * r * sp
    a = jnp.exp(la)
    a2 = jnp.exp(2.0 * la)
    mult = jnp.sqrt(-jnp.tanh(la) * (a2 + 1.0))
    first = (r0 + _iota((CONV_R, 1), 0)) == 0
    mult = jnp.where(first, 1.0, mult)
    return ext, xl, r, i, a, a2, mult, first


def _lru_specs(s):
    ct = LRU_CT
    nb_g = COL_G // ct
    return dict(
        x=pl.BlockSpec((s, ct), lambda j: (0, j)),
        g=pl.BlockSpec((s, ct), lambda j: (0, nb_g + j)),
        col=pl.BlockSpec((s, ct), lambda j: (0, j)),
        cw=pl.BlockSpec((SUBLANE, ct), lambda j: (0, j)),
        vec=pl.BlockSpec((1, ct), lambda j: (0, j)),
        gate=pl.BlockSpec((None, ct, ct), lambda j: (j, 0, 0)),
    )


def _lru_fwd(proj, cw8, cb, wa_bd, ba, wx_bd, bx, ap, *, name, jobs=()):
    s = proj.shape[0]
    ct = LRU_CT
    sp_ = _lru_specs(s)

    def body(x_ref, g_ref, cw_ref, cb_ref, wa_ref, ba_ref, wx_ref, bx_ref, ap_ref, y_ref, h_ref, xpad):
        _fill_front_padded(xpad, x_ref, s)
        sp = _softplus(-ap_ref[...])

        def step(q, carry):
            r0 = pl.multiple_of(q * CONV_R, CONV_R)
            _e, xl, _r, i, a, _a2, mult, _f = _lru_chunk(xpad, r0, cw_ref, cb_ref[...], wa_ref[...], ba_ref[...],
                                                       wx_ref[...], bx_ref[...], sp)
            acum, ucum = _scan_fwd(a, xl * i * mult)
            h = acum * carry + ucum
            h_ref[pl.ds(r0, CONV_R), :] = h
            ge, _th = _gelu(g_ref[pl.ds(r0, CONV_R), :])
            y_ref[pl.ds(r0, CONV_R), :] = (ge * h).astype(BF16)
            return _row_of(h, CONV_R - 1)

        lax.fori_loop(0, s // CONV_R, step, jnp.zeros((1, ct), F32))

    (ymix, hs), jouts = _hosted(
        body, jobs, grid=(LRU_W // ct,),
        in_specs=[sp_["x"], sp_["g"], sp_["cw"], sp_["vec"], sp_["gate"], sp_["vec"], sp_["gate"], sp_["vec"], sp_["vec"]],
        out_specs=(sp_["col"], sp_["col"]),
        out_shape=(jax.ShapeDtypeStruct((s, LRU_W + SSD_W), BF16), jax.ShapeDtypeStruct((s, LRU_W), F32)),
        scratch_shapes=[pltpu.VMEM((s + PAD, ct), F32)],
        name=name, args=(proj, proj, cw8, cb, wa_bd, ba, wx_bd, bx, ap))
    return ((ymix, hs), jouts) if jobs else (ymix, hs)


def _lru_bwd(proj, dy, hs, cw8, cb, wa_bd, ba, wx_bd, bx, ap, *, name, jobs=()):
    s = proj.shape[0]
    ct = LRU_CT
    sp_ = _lru_specs(s)

    nq = s // CONV_R

    def body(x_ref, g_ref, dy_ref, h_ref, cw_ref, cb_ref, wa_ref, ba_ref, wx_ref, bx_ref, ap_ref,
             dx_ref, dg_ref, dcwb_ref, dwa_ref, dwx_ref, xpad, hpad):
        _fill_front_padded(xpad, x_ref, s)
        _fill_front_padded(hpad, h_ref, s)
        apv = ap_ref[...]
        sp = _softplus(-apv)
        cb_v, wa, ba_v, wx, bx_v = cb_ref[...], wa_ref[...], ba_ref[...], wx_ref[...], bx_ref[...]
        dcwb_ref[...] = jnp.zeros_like(dcwb_ref)
        dwa_ref[...] = jnp.zeros_like(dwa_ref)
        dwx_ref[...] = jnp.zeros_like(dwx_ref)

        def back(k, carry):
            g_next, a_next, dxl_next = carry
            last_row = _iota((CONV_R, 1), 0) == CONV_R - 1
            r0 = pl.multiple_of((nq - 1 - k) * CONV_R, CONV_R)
            ext, xl, r, i, a, a2, mult, first = _lru_chunk(xpad, r0, cw_ref, cb_v, wa, ba_v, wx, bx_v, sp)
            gv = g_ref[pl.ds(r0, CONV_R), :]
            dyv = dy_ref[pl.ds(r0, CONV_R), :]
            hext = hpad[pl.ds(r0, CONV_R + PAD), :]
            ge, th = _gelu(gv)
            dg_ref[pl.ds(r0, CONV_R), :] = (dyv * _shift_down(hext, 0) * _gelu_grad(gv, th)).astype(BF16)
            b = jnp.where(last_row, a_next, pltpu.roll(a, CONV_R - 1, 0))
            bcum, dcum = _scan_rev(b, dyv * ge)
            gval = dcum + bcum * g_next
            hprev = _shift_down(hext, 1)
            da = gval * hprev
            dxl = gval * i * mult
            di = gval * xl * mult
            dmult = jnp.where(first, 0.0, gval * xl * i)
            dla = da * a - dmult * a2 / mult
            dr = dla * (-LRU_C) * sp
            dcwb_ref[7:8, :] += jnp.sum(dla * (-LRU_C) * r, axis=0, keepdims=True)
            dpr = dr * r * (1.0 - r)
            dpi = di * i * (1.0 - i)
            dxl = dxl + _dot_nt(dpr, wa) + _dot_nt(dpi, wx)
            dwa_ref[...] += _dot_tn(xl, dpr)
            dwx_ref[...] += _dot_tn(xl, dpi)
            dcwb_ref[5:6, :] += jnp.sum(dpr, axis=0, keepdims=True)
            dcwb_ref[6:7, :] += jnp.sum(dpi, axis=0, keepdims=True)
            for tap in range(4):
                dcwb_ref[tap:tap + 1, :] += jnp.sum(dxl * _shift_down(ext, 3 - tap), axis=0, keepdims=True)
            dcwb_ref[4:5, :] += jnp.sum(dxl, axis=0, keepdims=True)
            dx_ref[pl.ds(r0, CONV_R), :] = _conv_bwd_ext(jnp.concatenate([dxl, dxl_next], axis=0), cw_ref).astype(BF16)
            return _row_of(gval, 0), _row_of(a, 0), dxl[:PAD, :]

        zero = jnp.zeros((1, ct), F32)
        lax.fori_loop(0, nq, back, (zero, zero, jnp.zeros((PAD, ct), F32)))
        dcwb_ref[7:8, :] = dcwb_ref[7:8, :] * (-_sigmoid(-apv))

    nt = LRU_W // ct
    outs, jouts = _hosted(
        body, jobs, grid=(nt,),
        in_specs=[sp_["x"], sp_["g"], sp_["col"], sp_["col"], sp_["cw"], sp_["vec"], sp_["gate"], sp_["vec"], sp_["gate"],
                  sp_["vec"], sp_["vec"]],
        out_specs=(sp_["col"], sp_["col"], sp_["cw"], sp_["gate"], sp_["gate"]),
        out_shape=(jax.ShapeDtypeStruct((s, LRU_W), BF16), jax.ShapeDtypeStruct((s, LRU_W), BF16),
                   jax.ShapeDtypeStruct((SUBLANE, LRU_W), F32), jax.ShapeDtypeStruct((nt, ct, ct), F32),
                   jax.ShapeDtypeStruct((nt, ct, ct), F32)),
        scratch_shapes=[pltpu.VMEM((s + PAD, ct), F32), pltpu.VMEM((s + PAD, ct), F32)],
        name=name, args=(proj, proj, dy, hs, cw8, cb, wa_bd, ba, wx_bd, bx, ap))
    return (tuple(outs), jouts) if jobs else tuple(outs)


def _ssd_prep(dtr, bias, alog_pad, alogx):
    l = CHUNK
    lane = _iota((1, LANE), 1)
    a_head = jnp.where(lane < N_HEAD, -jnp.exp(alog_pad), 0.0)
    dt = _softplus(dtr + bias)
    tril = (_iota((l, l), 1) <= _iota((l, l), 0)).astype(F32)
    cs = _dotx(tril, dt * a_head)
    expand = (jnp.right_shift(_iota((LANE, SSD_W), 1), 6) == _iota((LANE, SSD_W), 0)).astype(F32)
    dtx = _dotx(dt, expand)
    ax = -jnp.exp(alogx)
    csx = _dotx(tril, dtx * ax)
    totx = jnp.sum(dtx * ax, axis=0, keepdims=True)
    return dict(a_head=a_head, dt=dt, tril=tril, cs=cs, expand=expand, dtx=dtx, ax=ax, csx=csx, totx=totx)


def _decay_mat(cs, cst_ref, h, causal):
    lane = _iota((CHUNK, LANE), 1)
    col = jnp.sum(jnp.where(lane == h, cs, 0.0), axis=1, keepdims=True)
    row = cst_ref[h:h + 1, :]
    return jnp.exp(jnp.where(causal, col - row, NEG_BIG))


def _head_mask(j):
    lane = _iota((CHUNK, GROUP_W), 1)
    return (lane >= j * HEAD_P) & (lane < (j + 1) * HEAD_P)


def _ssd_group_fwd(q, g, xs_g, bg, cg, ht_g, cst_ref, causal, dx_g):
    sl = slice(g * GROUP_W, (g + 1) * GROUP_W)
    dtx_g, csx_g, totx_g = q["dtx"][:, sl], q["csx"][:, sl], q["totx"][:, sl]
    xdt = xs_g * dtx_g
    ex = jnp.exp(csx_g)
    cb = _dot_nt(cg, bg)
    yoff = _dot(cg, ht_g) * ex
    ydiag = jnp.zeros((CHUNK, GROUP_W), F32)
    for j in range(4):
        sc = cb * _decay_mat(q["cs"], cst_ref, 4 * g + j, causal)
        ydiag = jnp.where(_head_mask(j), _dot(sc, xdt), ydiag)
    y = ydiag + yoff + xs_g * dx_g
    dsx = jnp.exp(totx_g - csx_g)
    return y, dict(xdt=xdt, ex=ex, cb=cb, yoff=yoff, dsx=dsx, dtx=dtx_g, totx=totx_g)


def _gated_norm_fwd(y_g, z_g, w_g):
    sz = _sigmoid(z_g)
    silu = z_g * sz
    yf = y_g * silu
    rs = lax.rsqrt(jnp.mean(yf * yf, axis=1, keepdims=True) + RMS_EPS)
    yn = yf * rs
    return yn * w_g, (sz, silu, rs, yn)


def _ssd_fwd(xact, proj, ymix, bias_pad, alog_pad, alogx, dxp, normw, *, name, jobs=()):
    s = xact.shape[0]
    nc = s // CHUNK

    def body(xa_ref, dt_ref, z_ref, _ymix_ref, bias_ref, alp_ref, alx_ref, dx_ref, nw_ref, y_ref, hp_ref, ht, cst):
        @pl.when(pl.program_id(0) == 0)
        def _():
            ht[...] = jnp.zeros_like(ht)

        hp_ref[...] = ht[...]
        q = _ssd_prep(dt_ref[...], bias_ref[...], alp_ref[...], alx_ref[...])
        cst[...] = q["cs"].T
        causal = q["tril"] > 0.0
        for g in range(N_GROUP):
            sl = slice(g * GROUP_W, (g + 1) * GROUP_W)
            xs_g = xa_ref[:, sl]
            bg = xa_ref[:, SSD_W + g * N_STATE:SSD_W + (g + 1) * N_STATE]
            cg = xa_ref[:, SSD_W + N_GROUP * N_STATE + g * N_STATE:SSD_W + N_GROUP * N_STATE + (g + 1) * N_STATE]
            ht_g = ht[:, sl]
            y, f = _ssd_group_fwd(q, g, xs_g, bg, cg, ht_g, cst, causal, dx_ref[:, sl])
            out, _ = _gated_norm_fwd(y, z_ref[:, sl], nw_ref[:, sl])
            y_ref[:, sl] = out.astype(BF16)
            ht[:, sl] = jnp.exp(f["totx"]) * ht_g + _dot_tn(bg, f["xdt"] * f["dsx"])

    par = lambda w: pl.BlockSpec((1, w), lambda c: (0, 0))
    (ycat, hprev), jouts = _hosted(
        body, jobs, grid=(nc,),
        in_specs=[pl.BlockSpec((CHUNK, XBC), lambda c: (c, 0)),
                  pl.BlockSpec((CHUNK, LANE), lambda c: (c, COL_DT // LANE)),
                  pl.BlockSpec((CHUNK, SSD_W), lambda c: (c, COL_Z // SSD_W)),
                  ANY_SPEC, par(LANE), par(LANE), par(SSD_W), par(SSD_W), par(SSD_W)],
        out_specs=(pl.BlockSpec((CHUNK, SSD_W), lambda c: (c, LRU_W // SSD_W)),
                   pl.BlockSpec((None, N_STATE, SSD_W), lambda c: (c, 0, 0))),
        out_shape=(jax.ShapeDtypeStruct(ymix.shape, ymix.dtype), jax.ShapeDtypeStruct((nc, N_STATE, SSD_W), F32)),
        scratch_shapes=[pltpu.VMEM((N_STATE, SSD_W), F32), pltpu.VMEM((CHUNK, LANE), F32)],
        aliases={3: 0}, name=name, args=(xact, proj, proj, ymix, bias_pad, alog_pad, alogx, dxp, normw))
    return ((ycat, hprev), jouts) if jobs else (ycat, hprev)


def _ssd_bwd(xact, proj, dycat, hprev, bias_pad, alog_pad, alogx, dxp, normw, *, name, jobs=()):
    s = xact.shape[0]
    nc = s // CHUNK
    l = CHUNK

    def body(xa_ref, dt_ref, z_ref, dy_ref, hp_ref, bias_ref, alp_ref, alx_ref, dx_ref, nw_ref,
             dxa_ref, ddt_ref, dz_ref, dnw_ref, small_ref, dht, cst, accx, dcsx_s, ddtx_s):
        step = pl.program_id(0)

        @pl.when(step == 0)
        def _():
            dht[...] = jnp.zeros_like(dht)
            accx[...] = jnp.zeros_like(accx)
            dnw_ref[...] = jnp.zeros_like(dnw_ref)
            small_ref[...] = jnp.zeros_like(small_ref)

        dtr = dt_ref[...]
        q = _ssd_prep(dtr, bias_ref[...], alp_ref[...], alx_ref[...])
        cst[...] = q["cs"].T
        causal = q["tril"] > 0.0
        eye = _iota((l, l), 0) == _iota((l, l), 1)
        lane = _iota((l, LANE), 1)
        dcs_head = jnp.zeros((l, LANE), F32)
        for g in range(N_GROUP):
            sl = slice(g * GROUP_W, (g + 1) * GROUP_W)
            slb = slice(SSD_W + g * N_STATE, SSD_W + (g + 1) * N_STATE)
            slc = slice(SSD_W + N_GROUP * N_STATE + g * N_STATE, SSD_W + N_GROUP * N_STATE + (g + 1) * N_STATE)
            xs_g, bg, cg = xa_ref[:, sl], xa_ref[:, slb], xa_ref[:, slc]
            ht_g = hp_ref[:, sl]
            dxp_g = dx_ref[:, sl]
            y, f = _ssd_group_fwd(q, g, xs_g, bg, cg, ht_g, cst, causal, dxp_g)
            z_g, nw_g = z_ref[:, sl], nw_ref[:, sl]
            _o, (sz, silu, rs, yn) = _gated_norm_fwd(y, z_g, nw_g)
            dout = dy_ref[:, sl]
            dnw_ref[:, sl] += jnp.sum(dout * yn, axis=0, keepdims=True)
            dyn = dout * nw_g
            dyf = rs * (dyn - yn * jnp.mean(dyn * yn, axis=1, keepdims=True))
            dy = dyf * silu
            dz_ref[:, sl] = (dyf * y * sz * (1.0 + z_g * (1.0 - sz))).astype(BF16)
            accx[0:1, sl] += jnp.sum(dy * xs_g, axis=0, keepdims=True)
            dyo = dy * f["ex"]
            dcg = _dot_nt(dyo, ht_g)
            dht_prev = _dot_tn(cg, dyo)
            dcsx = dy * f["yoff"]
            xdt = f["xdt"]
            dxdt = jnp.zeros((l, GROUP_W), F32)
            dcb = jnp.zeros((l, l), F32)
            for j in range(4):
                h = 4 * g + j
                lm = _decay_mat(q["cs"], cst, h, causal)
                sc = f["cb"] * lm
                mask = _head_mask(j)
                ds_ = jnp.where(causal, _dot_nt(jnp.where(mask, dy, 0.0), xdt), 0.0)
                dxdt = jnp.where(mask, _dot_tn(sc, dy), dxdt)
                dcb = dcb + ds_ * lm
                m = ds_ * sc
                rsum = jnp.sum(m, axis=1, keepdims=True)
                csum = jnp.sum(m, axis=0, keepdims=True)
                csum_col = jnp.sum(jnp.where(eye, csum, 0.0), axis=1, keepdims=True)
                dcs_head = dcs_head + jnp.where(lane == h, rsum - csum_col, 0.0)
            dhn = dht[:, sl]
            etot = jnp.exp(f["totx"])
            dxd = _dot(bg, dhn)
            dbg = _dot_nt(xdt * f["dsx"], dhn)
            dxdt = dxdt + dxd * f["dsx"]
            qq = dxd * xdt * f["dsx"]
            dcsx = dcsx - qq
            dtot = jnp.sum(qq, axis=0, keepdims=True) + jnp.sum(dhn * ht_g, axis=0, keepdims=True) * etot
            dht[:, sl] = etot * dhn + dht_prev
            dcg = dcg + _dot(dcb, bg)
            dbg = dbg + _dot_tn(dcb, cg)
            dxa_ref[:, sl] = dxdt * f["dtx"] + dy * dxp_g
            dxa_ref[:, slb] = dbg
            dxa_ref[:, slc] = dcg
            dcsx_s[:, sl] = dcsx
            ddtx_s[:, sl] = dxdt * xs_g
            accx[2:3, sl] = dtot
        triu = (_iota((l, l), 1) >= _iota((l, l), 0)).astype(F32)
        dax = _dotx(triu, dcsx_s[...]) + accx[2:3, :]
        accx[1:2, :] += jnp.sum(dax * q["dtx"], axis=0, keepdims=True)
        reduce = (jnp.right_shift(_iota((SSD_W, LANE), 0), 6) == _iota((SSD_W, LANE), 1)).astype(F32)
        ddt = _dotx(ddtx_s[...] + dax * q["ax"], reduce)
        da_head = _dotx(triu, dcs_head)
        ddt = ddt + da_head * q["a_head"]
        small_ref[1:2, :] += jnp.sum(da_head * q["dt"], axis=0, keepdims=True)
        ddtr = ddt * _sigmoid(dtr + bias_ref[...])
        ddt_ref[...] = ddtr.astype(BF16)
        small_ref[0:1, :] += jnp.sum(ddtr, axis=0, keepdims=True)

        @pl.when(step == nc - 1)
        def _():
            red = _dotx(accx[...], reduce)
            d_a = small_ref[1:2, :] + red[1:2, :]
            small_ref[1:2, :] = d_a * q["a_head"]
            small_ref[2:3, :] = red[0:1, :]

    rev = lambda c: nc - 1 - c
    par = lambda w: pl.BlockSpec((1, w), lambda c: (0, 0))
    outs, jouts = _hosted(
        body, jobs, grid=(nc,),
        in_specs=[pl.BlockSpec((CHUNK, XBC), lambda c: (rev(c), 0)),
                  pl.BlockSpec((CHUNK, LANE), lambda c: (rev(c), COL_DT // LANE)),
                  pl.BlockSpec((CHUNK, SSD_W), lambda c: (rev(c), COL_Z // SSD_W)),
                  pl.BlockSpec((CHUNK, SSD_W), lambda c: (rev(c), 1)),
                  pl.BlockSpec((None, N_STATE, SSD_W), lambda c: (rev(c), 0, 0)),
                  par(LANE), par(LANE), par(SSD_W), par(SSD_W), par(SSD_W)],
        out_specs=(pl.BlockSpec((CHUNK, XBC), lambda c: (rev(c), 0)),
                   pl.BlockSpec((CHUNK, LANE), lambda c: (rev(c), 0)),
                   pl.BlockSpec((CHUNK, SSD_W), lambda c: (rev(c), 0)),
                   par(SSD_W), pl.BlockSpec((SUBLANE, LANE), lambda c: (0, 0))),
        out_shape=(jax.ShapeDtypeStruct((s, XBC), F32), jax.ShapeDtypeStruct((s, LANE), BF16),
                   jax.ShapeDtypeStruct((s, SSD_W), BF16), jax.ShapeDtypeStruct((1, SSD_W), F32),
                   jax.ShapeDtypeStruct((SUBLANE, LANE), F32)),
        scratch_shapes=[pltpu.VMEM((N_STATE, SSD_W), F32), pltpu.VMEM((CHUNK, LANE), F32),
                        pltpu.VMEM((SUBLANE, SSD_W), F32), pltpu.VMEM((CHUNK, SSD_W), F32),
                        pltpu.VMEM((CHUNK, SSD_W), F32)],
        name=name, args=(xact, proj, proj, dycat, hprev, bias_pad, alog_pad, alogx, dxp, normw))
    return (tuple(outs), jouts) if jobs else tuple(outs)


def _blockdiag(w):
    w2 = w.reshape(N_HEAD // 2, 2, HEAD_P, HEAD_P)
    z = jnp.zeros((N_HEAD // 2, HEAD_P, HEAD_P), w.dtype)
    top = jnp.concatenate([w2[:, 0], z], axis=2)
    bot = jnp.concatenate([z, w2[:, 1]], axis=2)
    return jnp.concatenate([top, bot], axis=1)


def _unblockdiag(wbd):
    a = wbd[:, :HEAD_P, :HEAD_P]
    b = wbd[:, HEAD_P:, HEAD_P:]
    return jnp.stack([a, b], axis=1).reshape(N_HEAD, HEAD_P, HEAD_P)


def _pad_rows8(w):
    return jnp.concatenate([w, jnp.zeros((SUBLANE - w.shape[0], w.shape[1]), w.dtype)], axis=0)


def _pad_lane(v):
    return jnp.concatenate([v, jnp.zeros((1, LANE - v.shape[1]), v.dtype)], axis=1)


class _NoExchange:
    def ride(self, host):
        return []

    def done(self, jobs, outs, w):
        pass

    def grad(self, name, val):
        pass

    def small(self, raw):
        pass


def _local_step(x, p, tgt, w, hooks=_NoExchange()):
    cw_l = _pad_rows8(w["lru_conv_w"])
    cw_s = _pad_rows8(w["ssd_conv_w"])
    wa_bd = _blockdiag(w["lru_gate_a_w"])
    wx_bd = _blockdiag(w["lru_gate_x_w"])
    ba = w["lru_gate_a_b"].reshape(1, LRU_W)
    bx = w["lru_gate_x_b"].reshape(1, LRU_W)
    bias_pad = _pad_lane(w["ssd_dt_bias"])
    alog_pad = _pad_lane(w["ssd_a_log"])
    alogx = jnp.repeat(w["ssd_a_log"], HEAD_P, axis=1)
    dxp = jnp.repeat(w["ssd_d"], HEAD_P, axis=1)

    def host(fn, *a, name, **k):
        jobs = hooks.ride(name)
        res = fn(*a, name=name, jobs=jobs, **k)
        if jobs:
            res, jouts = res
            hooks.done(jobs, jouts, w)
        return res

    def grad(n, val):
        g[n] = val
        hooks.grad(n, val)

    xb = x.astype(BF16)
    proj = host(_mm, xb, w["w_in"], "nn", tm=1024, tn=512, name="in_proj")
    ymix, h_lru = host(_lru_fwd, proj, cw_l, w["lru_conv_b"], wa_bd, ba, wx_bd, bx, w["lru_a_param"], name="lru_fwd")
    xact = _conv_silu_fwd(proj, cw_s, w["ssd_conv_b"], col0=COL_XBC, width=XBC, ct=256, name="ssd_conv_fwd")
    ycat, hprev = _ssd_fwd(xact, proj, ymix, bias_pad, alog_pad, alogx, dxp, w["ssd_norm_w"], name="ssd_fwd")
    mix = _mm(ycat, w["w_out"], "nn", tm=1024, tn=1024, name="out_proj")
    x1, x1b = _ln_fwd(x, mix, w["ln1_g"], w["ln1_b"], name="ln1_fwd")
    pre = host(_mm, x1b, w["w_ff1"], "nn", tm=1024, tn=512, out_dtype=BF16, name="ff1")
    ff = _mm(pre, w["w_ff2"], "nn", tm=512, tn=1024, a_fn=_relu2, name="ff2")
    x2, x2b = _ln_fwd(x1, ff, w["ln2_g"], w["ln2_b"], name="ln2_fwd")
    gpre = _mm(x2b, w["w_ple_gate"], "nn", tm=1024, tn=1024, name="ple_gate")
    ple = _mm(p, w["w_ple"], "nn", tm=1024, tn=1024, name="ple_proj")
    loss, dgpre, dple, dt3, dg3, db3 = _head(x2, gpre, ple, w["ln3_g"], w["ln3_b"], tgt, name="head")

    g = {}
    g["ln3_g"], g["ln3_b"] = dg3, db3
    grad("w_ple_gate", _mm(x2b, dgpre, "tn", tm=512, tn=1024, out_dtype=BF16, name="d_w_ple_gate"))
    grad("w_ple", _mm(p, dple, "tn", tm=256, tn=512, dest_major=True, out_dtype=BF16, name="d_w_ple"))
    dx2_mm = host(_mm, dgpre, w["w_ple_gate"], "nt", tm=1024, tn=1024, name="d_x2")
    dt2, dt2b, g["ln2_g"], g["ln2_b"] = _ln_bwd(x1, ff, w["ln2_g"], [dt3, dx2_mm], [ALPHA, 1.0], name="ln2_bwd")
    grad("w_ff2", host(_mm, pre, dt2b, "tn", tm=512, tn=1024, a_fn=_relu2, out_dtype=BF16, name="d_w_ff2"))
    dpre = host(_mm, dt2b, w["w_ff2"], "nt", tm=1024, tn=512, extra=pre, out_dtype=BF16,
                epi=lambda acc, pv: acc * 2.0 * jnp.maximum(pv.astype(F32), 0.0), name="d_pre")
    grad("w_ff1", host(_mm, x1b, dpre, "tn", tm=1024, tn=512, dest_major=True, out_dtype=BF16, name="d_w_ff1"))
    dx1_mm = host(_mm, dpre, w["w_ff1"], "nt", tm=512, tn=1024, name="d_x1")
    dt1, dt1b, g["ln1_g"], g["ln1_b"] = _ln_bwd(x, mix, w["ln1_g"], [dt2, dx1_mm], [ALPHA, 1.0], name="ln1_bwd")
    grad("w_out", host(_mm, ycat, dt1b, "tn", tm=512, tn=1024, out_dtype=BF16, name="d_w_out"))
    dycat = host(_mm, dt1b, w["w_out"], "nt", tm=1024, tn=1024, name="d_ycat")
    dxl, dgl, dcwb_l, dwa, dwx = host(_lru_bwd, proj, dycat, h_lru, cw_l, w["lru_conv_b"], wa_bd, ba, wx_bd, bx,
                                      w["lru_a_param"], name="lru_bwd")
    dxact, ddt, dz, g["ssd_norm_w"], small = host(_ssd_bwd, xact, proj, dycat, hprev, bias_pad, alog_pad, alogx, dxp,
                                                   w["ssd_norm_w"], name="ssd_bwd")
    dxbc, dcwb_s = _conv_silu_bwd(proj, dxact, cw_s, w["ssd_conv_b"], col0=COL_XBC, width=XBC, ct=256,
                                  name="ssd_conv_bwd")
    s = x.shape[0]
    dproj = jnp.concatenate([dxl, dgl, dz, dxbc, ddt, jnp.zeros((s, D_IN_PAD - COL_DT - LANE), BF16)], axis=1)

    g["lru_conv_w"] = dcwb_l[0:4]
    g["lru_conv_b"] = dcwb_l[4:5]
    g["lru_gate_a_b"] = dcwb_l[5:6]
    g["lru_gate_x_b"] = dcwb_l[6:7]
    g["lru_a_param"] = dcwb_l[7:8]
    g["lru_gate_a_w"] = _unblockdiag(dwa)
    g["lru_gate_x_w"] = _unblockdiag(dwx)
    g["ssd_conv_w"] = dcwb_s[0:4]
    g["ssd_conv_b"] = dcwb_s[4:5]
    g["ssd_dt_bias"] = small[0:1, :N_HEAD]
    g["ssd_a_log"] = small[1:2, :N_HEAD]
    g["ssd_d"] = small[2:3, :N_HEAD]
    rows = jnp.concatenate([g[n] for n in ("ssd_norm_w", "ln1_g", "ln1_b", "ln2_g", "ln2_b", "ln3_g", "ln3_b")]
                           + [jnp.zeros((1, D_MODEL), F32)], axis=0)
    raw = dict(lru=dcwb_l, ssd=dcwb_s, gate_a=g["lru_gate_a_w"].reshape(N_HEAD * HEAD_P, HEAD_P),
               gate_x=g["lru_gate_x_w"].reshape(N_HEAD * HEAD_P, HEAD_P), heads=small, rows=rows)
    hooks.small(raw)
    grad("w_in", host(_mm, xb, dproj, "tn", tm=1024, tn=512, out_dtype=BF16, name="d_w_in"))
    grad_x = host(_mm, dproj, w["w_in"], "nt", tm=256, tn=1024, extra=dt1, epi=lambda acc, e: acc + ALPHA * e,
                  name="d_x")
    return loss[0, 0], grad_x, g, raw


ANY_SPEC = pl.BlockSpec(memory_space=pl.ANY)


def _mesh_pos():
    return lax.axis_index("x"), lax.axis_index("y"), lax.axis_index("c")


def _remote(src, dst, send, recv, k, to):
    return pltpu.make_async_remote_copy(src_ref=src, dst_ref=dst, send_sem=send.at[k], recv_sem=recv.at[k],
                                        device_id=to, device_id_type=MESH_T)


class _Job:
    N_SEM = 7

    def __init__(self, kind, inp):
        self.kind, self.inp = kind, inp
        shape = {"gather": (N_DEV,) + inp.shape, "pair": (4,) + inp.shape[1:], "chip": inp.shape}[kind]
        self.out = jax.ShapeDtypeStruct(shape, inp.dtype)

    def _places(self):
        x, y, c = _mesh_pos()
        return (x, y, c), (x, y, 1 - c), [(1 - x, y), (x, 1 - y), (1 - x, 1 - y)]

    def start(self, inp, out, send, recv, loc):
        me, sibling, chips = self._places()
        x, y, c = me
        if self.kind == "gather":
            mine = out.at[4 * x + 2 * y + c]
            pltpu.make_async_copy(inp, mine, loc.at[0]).start()
            _remote(inp, mine, send, recv, 0, sibling).start()
            for j, chip in enumerate(chips):
                _remote(inp, mine, send, recv, 1 + j, (*chip, c)).start()
        elif self.kind == "pair":
            for k in range(4):
                _remote(inp.at[2 * k + (1 - c)], out.at[k], send, recv, k, sibling).start()
        else:
            kme = 2 * x + y
            pltpu.make_async_copy(inp.at[kme], out.at[kme], loc.at[0]).start()
            for j, (tx, ty) in enumerate(chips):
                _remote(inp.at[2 * tx + ty], out.at[kme], send, recv, j, (tx, ty, c)).start()

    def finish(self, inp, out, send, recv, loc):
        me, sibling, chips = self._places()
        x, y, c = me
        if self.kind == "gather":
            blk = lambda px, py, pc: out.at[4 * px + 2 * py + pc]
            mine = blk(*me)
            for j, chip in enumerate(chips):
                landed = blk(*chip, c)
                _remote(landed, landed, send, recv, 1 + j, me).wait_recv()
                _remote(landed, landed, send, recv, 4 + j, sibling).start()
            _remote(inp, blk(*sibling), send, recv, 0, me).wait_recv()
            for j, chip in enumerate(chips):
                _remote(inp, blk(*chip, 1 - c), send, recv, 4 + j, me).wait_recv()
            for k in range(7):
                _remote(inp, mine, send, recv, k, sibling).wait_send()
            pltpu.make_async_copy(inp, mine, loc.at[0]).wait()
        elif self.kind == "pair":
            for k in range(4):
                _remote(inp.at[2 * k + (1 - c)], out.at[k], send, recv, k, sibling).wait()
        else:
            kme = 2 * x + y
            for j, (tx, ty) in enumerate(chips):
                _remote(inp.at[kme], out.at[2 * tx + ty], send, recv, j, (tx, ty, c)).wait_recv()
            for j, (tx, ty) in enumerate(chips):
                _remote(inp.at[2 * tx + ty], out.at[kme], send, recv, j, (tx, ty, c)).wait_send()
            pltpu.make_async_copy(inp.at[kme], out.at[kme], loc.at[0]).wait()


def _job_scratch(jobs):
    sem = pltpu.SemaphoreType.DMA
    return [s for _ in jobs for s in (sem((_Job.N_SEM,)), sem((_Job.N_SEM,)), sem((1,)))]


def _run_jobs(jobs, method, jins, jouts, jsems):
    for q, job in enumerate(jobs):
        getattr(job, method)(jins[q], jouts[q], *jsems[3 * q:3 * q + 3])


def _exchange(jobs, *, name):
    n = len(jobs)

    def body(*refs):
        jins, jouts, jsems = refs[:n], refs[n:2 * n], refs[2 * n:]
        _run_jobs(jobs, "start", jins, jouts, jsems)
        _run_jobs(jobs, "finish", jins, jouts, jsems)

    return _pcall(body, in_specs=[ANY_SPEC] * n, out_specs=[ANY_SPEC] * n, out_shape=[j.out for j in jobs],
                  scratch_shapes=_job_scratch(jobs), name=name)(*[j.inp for j in jobs])


def _hosted(body, jobs, *, grid, in_specs, out_specs, out_shape, args, name, scratch_shapes=(), aliases=None):
    in_specs, out_specs, out_shape = list(in_specs), list(out_specs), list(out_shape)
    scratch_shapes = list(scratch_shapes)
    n_in, n_out, n_scr, nj = len(in_specs), len(out_specs), len(scratch_shapes), len(jobs)
    sem = ("arbitrary",) * len(grid)
    kw = dict(input_output_aliases=aliases) if aliases else {}
    if not jobs:
        res = _pcall(body, grid=grid, in_specs=in_specs, out_specs=out_specs, out_shape=out_shape,
                     scratch_shapes=scratch_shapes, name=name, compiler_params=_cparams(sem), **kw)(*args)
        return list(res), []

    def full(*refs):
        ins, jins = refs[:n_in], refs[n_in:n_in + nj]
        o0 = n_in + nj
        outs, jouts = refs[o0:o0 + n_out], refs[o0 + n_out:o0 + n_out + nj]
        s0 = o0 + n_out + nj
        scr, jsems = refs[s0:s0 + n_scr], refs[s0 + n_scr:]
        first = pl.program_id(0) == 0
        last = pl.program_id(0) == grid[0] - 1
        for ax in range(1, len(grid)):
            first = jnp.logical_and(first, pl.program_id(ax) == 0)
            last = jnp.logical_and(last, pl.program_id(ax) == grid[ax] - 1)

        @pl.when(first)
        def _():
            _run_jobs(jobs, "start", jins, jouts, jsems)

        body(*ins, *outs, *scr)

        @pl.when(last)
        def _():
            _run_jobs(jobs, "finish", jins, jouts, jsems)

    res = _pcall(full, grid=grid, in_specs=in_specs + [ANY_SPEC] * nj, out_specs=out_specs + [ANY_SPEC] * nj,
                 out_shape=out_shape + [j.out for j in jobs], scratch_shapes=scratch_shapes + _job_scratch(jobs),
                 name=name, compiler_params=_cparams(sem), **kw)(*args, *[j.inp for j in jobs])
    return list(res[:n_out]), list(res[n_out:])


def _pair_add(g8, r4, cidx, *, name):
    _, r, c = g8.shape
    tr = min(r, ROW_TILE)

    def body(c_ref, g_ref, r_ref, o_ref):
        o_ref[...] = (g_ref[...].astype(F32) + r_ref[...].astype(F32)).astype(BF16)

    return _pcall(
        body,
        grid_spec=pltpu.PrefetchScalarGridSpec(
            num_scalar_prefetch=1, grid=(4, r // tr),
            in_specs=[pl.BlockSpec((None, tr, c), lambda k, i, cr: (2 * k + cr[0], i, 0)),
                      pl.BlockSpec((None, tr, c), lambda k, i, cr: (k, i, 0))],
            out_specs=pl.BlockSpec((None, tr, c), lambda k, i, cr: (k, i, 0))),
        out_shape=jax.ShapeDtypeStruct((4, r, c), BF16), name=name,
        compiler_params=_cparams(("parallel", "parallel")))(cidx, g8, r4)


def _adam_update(g, w_ref, m_ref, v_ref, g_ref, d_ref, mo_ref, vo_ref):
    c1 = 1.0 - ADAM_B1 ** ADAM_STEP
    c2 = 1.0 - ADAM_B2 ** ADAM_STEP
    m2 = ADAM_B1 * m_ref[...] + (1.0 - ADAM_B1) * g
    v2 = ADAM_B2 * v_ref[...] + (1.0 - ADAM_B2) * (g * g)
    g_ref[...] = g
    mo_ref[...] = m2
    vo_ref[...] = v2
    d_ref[...] = -ADAM_LR * ((m2 / c1) / (jnp.sqrt(v2 / c2) + ADAM_EPS) + ADAM_WD * w_ref[...])


def _adamw_rows(srcs, items, own_cols, me1, *, name):
    ns, ni, no = len(srcs), len(items), len(own_cols)
    full = lambda a: pl.BlockSpec(a.shape, lambda i, me: (0,) * a.ndim)
    in_specs = [full(a) for a in srcs]
    args = list(srcs)
    for (si, _r0, w, _m, _v) in own_cols:
        a = srcs[si]
        in_specs.append(pl.BlockSpec((N_DEV, a.shape[1], w.shape[1]), lambda i, me: (0, 0, me[0])))
        args.append(a)
    out_specs, out_shape = [], []
    for (_si, _r0, w, m, v) in list(items) + list(own_cols):
        in_specs += [full(w)] * 3
        args += [w, m, v]
        out_specs += [full(w)] * 4
        out_shape += [jax.ShapeDtypeStruct(w.shape, F32)] * 4

    def body(me_ref, *refs):
        src_refs, own_refs = refs[:ns], refs[ns:ns + no]
        wmv = refs[ns + no:ns + no + 3 * (ni + no)]
        outs = refs[ns + no + 3 * (ni + no):]
        for q, (si, r0, w, _m, _v) in enumerate(list(items) + list(own_cols)):
            nr, cw = w.shape
            gref = src_refs[si] if q < ni else own_refs[q - ni]
            g = gref[0, r0:r0 + nr, 0:cw]
            for d in range(1, N_DEV):
                g = g + gref[d, r0:r0 + nr, 0:cw]
            _adam_update(g, *wmv[3 * q:3 * q + 3], *outs[4 * q:4 * q + 4])

    res = _pcall(
        body,
        grid_spec=pltpu.PrefetchScalarGridSpec(num_scalar_prefetch=1, grid=(1,), in_specs=in_specs, out_specs=out_specs),
        out_shape=out_shape, name=name, compiler_params=_cparams(("arbitrary",)))(me1, *args)
    return [tuple(res[4 * q:4 * q + 4]) for q in range(ni + no)]


def _adamw(gsrc, w, m, v, *, name):
    k, r, c = gsrc.shape
    tr = ROW_TILE if r % ROW_TILE == 0 else r

    def body(gs_ref, w_ref, m_ref, v_ref, g_ref, d_ref, mo_ref, vo_ref):
        g = gs_ref[0].astype(F32)
        for q in range(1, k):
            g = g + gs_ref[q].astype(F32)
        _adam_update(g, w_ref, m_ref, v_ref, g_ref, d_ref, mo_ref, vo_ref)

    row = pl.BlockSpec((tr, c), lambda i: (i, 0))
    sd = jax.ShapeDtypeStruct((r, c), F32)
    return _pcall(body, grid=(r // tr,), in_specs=[pl.BlockSpec((k, tr, c), lambda i: (0, i, 0)), row, row, row],
                  out_specs=(row, row, row, row), out_shape=(sd, sd, sd, sd), name=name,
                  compiler_params=_cparams(("parallel",)))(gsrc, w, m, v)


WEIGHTS = ['w_in', 'lru_conv_w', 'lru_conv_b', 'lru_gate_a_w', 'lru_gate_a_b', 'lru_gate_x_w', 'lru_gate_x_b',
           'lru_a_param', 'ssd_conv_w', 'ssd_conv_b', 'ssd_dt_bias', 'ssd_a_log', 'ssd_d', 'ssd_norm_w', 'w_out',
           'ln1_g', 'ln1_b', 'w_ff1', 'w_ff2', 'ln2_g', 'ln2_b', 'w_ple_gate', 'w_ple', 'ln3_g', 'ln3_b']
BIG = ['w_in', 'w_out', 'w_ff1', 'w_ff2', 'w_ple_gate', 'w_ple']
COL_SHARDED = ('w_in', 'w_ff1', 'w_ple')
CONV = ['lru_conv_w', 'ssd_conv_w']
REPL = [n for n in WEIGHTS if n not in BIG and n not in CONV]
CONV_CH = {'lru_conv_w': LRU_W, 'ssd_conv_w': XBC}


def _to_dest_major(name, gfull):
    if name == 'w_in':
        gfull = gfull[:, :D_IN]
    if name in COL_SHARDED:
        r, cfull = gfull.shape
        return gfull.reshape(r, N_DEV, cfull // N_DEV).transpose(1, 0, 2)
    rfull, cdim = gfull.shape
    return gfull.reshape(N_DEV, rfull // N_DEV, cdim)


def _full_weight(name, gathered):
    if name in COL_SHARDED:
        _, r, cs = gathered.shape
        full = gathered.transpose(1, 0, 2).reshape(r, N_DEV * cs)
    else:
        _, rs, cdim = gathered.shape
        full = gathered.reshape(N_DEV * rs, cdim)
    if name == 'w_in':
        full = jnp.concatenate([full, jnp.zeros((D_MODEL, D_IN_PAD - D_IN), full.dtype)], axis=1)
    return full


SMALL_SRC = ("lru", "ssd", "heads", "rows", "gate_a", "gate_x")
AG_HOSTS = {"in_proj": ("w_ff1",), "lru_fwd": ("w_out", "w_ple_gate", "w_ple"), "ff1": ("w_ff2",)}
PAIR_HOSTS = ("d_x2", "d_pre", "d_x1", "d_ycat", "d_x")
CHIP_HOSTS = {"lru_bwd": ("w_ple_gate", "w_ple", "w_ff2"), "ssd_bwd": ("w_ff1", "w_out")}
SMALL_HOST = "d_w_in"


class _Schedule:
    def __init__(self, shards, cidx):
        self.shards, self.cidx = shards, cidx
        self.pair, self.chip, self.small_jobs = [], [], []
        self.dest, self.summed, self.gathered_small = {}, {}, {}
        self.tags = []

    def ride(self, host):
        tags = []
        if host in AG_HOSTS:
            tags = [("weight", n, self.shards[n]) for n in AG_HOSTS[host]]
        elif host in PAIR_HOSTS or host in CHIP_HOSTS or host == "flush":
            tags = [("pair", n, a) for n, a in self.pair]
            self.pair = []
            if host not in PAIR_HOSTS:
                take = [t for t in self.chip if host == "flush" or t[0] in CHIP_HOSTS[host]]
                tags += [("chip", n, a) for n, a in take]
                self.chip = [t for t in self.chip if not any(t is u for u in take)]
        elif host == SMALL_HOST:
            tags = [("small", n, a) for n, a in self.small_jobs]
        self.tags = tags
        return [_Job({"weight": "gather", "small": "gather"}.get(kind, kind), a) for kind, _n, a in tags]

    def done(self, jobs, outs, w):
        for (kind, n, _a), o in zip(self.tags, outs):
            if kind == "weight":
                w[n] = _full_weight(n, o)
            elif kind == "small":
                self.gathered_small[n] = o
            elif kind == "pair":
                self.chip.append((n, _pair_add(self.dest[n], o, self.cidx, name="rs_pair_add_" + n)))
            else:
                self.summed[n] = o

    def grad(self, name, val):
        self.dest[name] = val if val.ndim == 3 else _to_dest_major(name, val)
        self.pair.append((name, self.dest[name]))

    def small(self, raw):
        self.small_jobs = [(k, raw[k]) for k in SMALL_SRC]

    def flush(self):
        step = 0
        while self.pair or self.chip:
            jobs = self.ride("flush")
            self.done(jobs, _exchange(jobs, name="rs_flush_%d" % step), None)
            step += 1


def kernel(x, p, w_in, lru_conv_w, lru_conv_b, lru_gate_a_w, lru_gate_a_b, lru_gate_x_w, lru_gate_x_b, lru_a_param, ssd_conv_w, ssd_conv_b, ssd_dt_bias, ssd_a_log, ssd_d, ssd_norm_w, w_out, ln1_g, ln1_b, w_ff1, w_ff2, ln2_g, ln2_b, w_ple_gate, w_ple, ln3_g, ln3_b, loss_target, m_w_in, m_lru_conv_w, m_lru_conv_b, m_lru_gate_a_w, m_lru_gate_a_b, m_lru_gate_x_w, m_lru_gate_x_b, m_lru_a_param, m_ssd_conv_w, m_ssd_conv_b, m_ssd_dt_bias, m_ssd_a_log, m_ssd_d, m_ssd_norm_w, m_w_out, m_ln1_g, m_ln1_b, m_w_ff1, m_w_ff2, m_ln2_g, m_ln2_b, m_w_ple_gate, m_w_ple, m_ln3_g, m_ln3_b, v_w_in, v_lru_conv_w, v_lru_conv_b, v_lru_gate_a_w, v_lru_gate_a_b, v_lru_gate_x_w, v_lru_gate_x_b, v_lru_a_param, v_ssd_conv_w, v_ssd_conv_b, v_ssd_dt_bias, v_ssd_a_log, v_ssd_d, v_ssd_norm_w, v_w_out, v_ln1_g, v_ln1_b, v_w_ff1, v_w_ff2, v_ln2_g, v_ln2_b, v_w_ple_gate, v_w_ple, v_ln3_g, v_ln3_b):
    given = dict(locals())
    wsh = {n: given[n][0] for n in WEIGHTS}
    msh = {n: given["m_" + n][0] for n in WEIGHTS}
    vsh = {n: given["v_" + n][0] for n in WEIGHTS}
    xi, yi, ci = _mesh_pos()
    me = 4 * xi + 2 * yi + ci

    shards = {n: wsh[n].astype(BF16) for n in BIG}
    conv_pack = jnp.concatenate([_pad_rows8(wsh[n]) for n in CONV], axis=1)
    g_in, gconv = _exchange([_Job("gather", shards['w_in']), _Job("gather", conv_pack)], name="ag_first")
    full = {'w_in': _full_weight('w_in', g_in)}
    c0 = 0
    for n in CONV:
        cw = CONV_CH[n] // N_DEV
        full[n] = gconv[:, :4, c0:c0 + cw].transpose(1, 0, 2).reshape(4, CONV_CH[n])
        c0 += cw
    for n in REPL:
        full[n] = wsh[n].reshape(1, -1) if wsh[n].ndim == 1 else wsh[n]

    sched = _Schedule(shards, jnp.reshape(ci, (1,)).astype(jnp.int32))
    loss_local, grad_x, g, raw = _local_step(x[0], p[0, 0], loss_target[0], full, sched)
    loss = lax.psum(loss_local, ("x", "y", "c"))
    sched.flush()
    summed, gat = sched.summed, sched.gathered_small

    outs = {}
    for n in BIG:
        outs[n] = _adamw(summed[n], wsh[n], msh[n], vsh[n], name="adamw_" + n)
    for n, k in (("lru_gate_a_w", "gate_a"), ("lru_gate_x_w", "gate_x")):
        flat = lambda a: a.reshape(N_HEAD * HEAD_P, HEAD_P)
        res = _adamw(gat[k], flat(wsh[n]), flat(msh[n]), flat(vsh[n]), name="adamw_" + n)
        outs[n] = tuple(r.reshape(N_HEAD, HEAD_P, HEAD_P) for r in res)
    row_items = [("lru_conv_b", 0, 4), ("lru_gate_a_b", 0, 5), ("lru_gate_x_b", 0, 6), ("lru_a_param", 0, 7),
                 ("ssd_conv_b", 1, 4), ("ssd_dt_bias", 2, 0), ("ssd_a_log", 2, 1), ("ssd_d", 2, 2),
                 ("ssd_norm_w", 3, 0), ("ln1_g", 3, 1), ("ln1_b", 3, 2), ("ln2_g", 3, 3), ("ln2_b", 3, 4),
                 ("ln3_g", 3, 5), ("ln3_b", 3, 6)]
    vec = lambda a: a.reshape(1, -1)
    items = [(si, r0, vec(wsh[n]), vec(msh[n]), vec(vsh[n])) for n, si, r0 in row_items]
    own = [(si, 0, wsh[n], msh[n], vsh[n]) for n, si in (("lru_conv_w", 0), ("ssd_conv_w", 1))]
    me1 = jnp.reshape(me, (1,)).astype(jnp.int32)
    res = _adamw_rows([gat[k] for k in SMALL_SRC[:4]], items, own, me1, name="adamw_small")
    for (n, _si, _r0), r4 in zip(row_items, res[:len(row_items)]):
        outs[n] = tuple(r.reshape(wsh[n].shape) for r in r4)
    for n, r4 in zip(CONV, res[len(row_items):]):
        outs[n] = r4

    ex = lambda a: a[None]
    return (loss, grad_x[None],
            *[ex(outs[n][0]) for n in WEIGHTS], *[ex(outs[n][1]) for n in WEIGHTS],
            *[ex(outs[n][2]) for n in WEIGHTS], *[ex(outs[n][3]) for n in WEIGHTS])
```

```python
import math

import jax
import jax.numpy as jnp
from jax import lax
from jax.experimental import pallas as pl
from jax.experimental.pallas import tpu as pltpu

F32 = jnp.float32
BF16 = jnp.bfloat16
HI = lax.Precision.HIGHEST

N_DEV = 8
D_MODEL = 1024
LRU_W = 1024
SSD_W = 1024
XBC = 2048
N_HEAD = 16
HEAD_P = 64
N_GROUP = 4
GROUP_W = 256
N_STATE = 128
CHUNK = 128
D_FF = 4096
PLE_DIM = 256
D_IN = 5136
D_IN_PAD = 5632
COL_G = 1024
COL_Z = 2048
COL_XBC = 3072
COL_DT = 5120
LRU_C = 8.0
ALPHA = 2.0 ** 0.25
LN_EPS = 1e-5
RMS_EPS = 1e-5
ADAM_LR = 0.001
ADAM_B1 = 0.9
ADAM_B2 = 0.999
ADAM_EPS = 1e-08
ADAM_WD = 0.01
ADAM_STEP = 10
GELU_C = math.sqrt(2.0 / math.pi)
LANE = 128
SUBLANE = 8
VMEM_LIMIT = 48 * 1024 * 1024
MESH_T = pl.DeviceIdType.MESH
NEG_BIG = -1e30


def _pcall(body, **kw):
    return pl.pallas_call(body, **kw)


def _cparams(sem):
    return pltpu.CompilerParams(dimension_semantics=sem, vmem_limit_bytes=VMEM_LIMIT)


def _dot(a, b):
    return jnp.dot(a.astype(BF16), b.astype(BF16), preferred_element_type=F32)


def _dot_nt(a, b):
    return lax.dot_general(a.astype(BF16), b.astype(BF16), (((1,), (1,)), ((), ())), preferred_element_type=F32)


def _dot_tn(a, b):
    return lax.dot_general(a.astype(BF16), b.astype(BF16), (((0,), (0,)), ((), ())), preferred_element_type=F32)


def _dotx(a, b):
    return jnp.dot(a, b, precision=HI, preferred_element_type=F32)


def _sigmoid(x):
    return jax.nn.sigmoid(x)


def _softplus(v):
    return jnp.maximum(v, 0.0) + jnp.log1p(jnp.exp(-jnp.abs(v)))


def _gelu(x):
    th = jnp.tanh(GELU_C * (x + 0.044715 * x * x * x))
    return 0.5 * x * (1.0 + th), th


def _gelu_grad(x, th):
    return 0.5 * (1.0 + th) + 0.5 * x * (1.0 - th * th) * GELU_C * (1.0 + 3.0 * 0.044715 * x * x)


def _iota(shape, dim):
    return lax.broadcasted_iota(jnp.int32, shape, dim)


def _mm(a, b, mode, *, tm, tn, name, a_fn=None, extra=None, epi=None, out_dtype=F32, dest_major=False, jobs=()):
    m = a.shape[1] if mode == "tn" else a.shape[0]
    n = b.shape[0] if mode == "nt" else b.shape[1]
    tm, tn = min(tm, m), min(tn, n)
    if dest_major:
        tn = n // N_DEV
    if mode == "nn":
        m, k = a.shape
        _, n = b.shape
        a_spec = pl.BlockSpec((tm, k), lambda i, j: (i, 0))
        b_spec = pl.BlockSpec((k, tn), lambda i, j: (0, j))
        dims = ((1,), (0,))
    elif mode == "nt":
        m, k = a.shape
        n, _ = b.shape
        a_spec = pl.BlockSpec((tm, k), lambda i, j: (i, 0))
        b_spec = pl.BlockSpec((tn, k), lambda i, j: (j, 0))
        dims = ((1,), (1,))
    else:
        k, m = a.shape
        _, n = b.shape
        a_spec = pl.BlockSpec((k, tm), lambda i, j: (0, i))
        b_spec = pl.BlockSpec((k, tn), lambda i, j: (0, j))
        dims = ((0,), (0,))
    assert m % tm == 0 and n % tn == 0, (name, m, n, tm, tn)
    o_spec = pl.BlockSpec((tm, tn), lambda i, j: (i, j))
    in_specs = [a_spec, b_spec]
    args = [a, b]
    if extra is not None:
        in_specs.append(o_spec)
        args.append(extra)

    def body(*refs):
        a_ref, b_ref, o_ref = refs[0], refs[1], refs[-1]
        av = a_ref[...]
        if a_fn is not None:
            av = a_fn(av)
        acc = lax.dot_general(av.astype(BF16), b_ref[...].astype(BF16), (dims, ((), ())), preferred_element_type=F32)
        if epi is not None:
            acc = epi(acc, refs[2][...])
        o_ref[...] = acc.astype(out_dtype)

    out_shape = jax.ShapeDtypeStruct((m, n), out_dtype)
    if dest_major:
        assert extra is None
        o_spec = pl.BlockSpec((None, tm, tn), lambda i, j: (j, i, 0))
        out_shape = jax.ShapeDtypeStruct((N_DEV, m, tn), out_dtype)
    (out,), jouts = _hosted(body, jobs, grid=(m // tm, n // tn), in_specs=in_specs, out_specs=[o_spec],
                            out_shape=[out_shape], args=args, name=name)
    return (out, jouts) if jobs else out


def _relu2(v):
    r = jnp.maximum(v, 0.0)
    return r * r


ROW_TILE = 256


def _ln_stats(t):
    mu = jnp.mean(t, axis=-1, keepdims=True)
    xc = t - mu
    var = jnp.mean(xc * xc, axis=-1, keepdims=True)
    rstd = lax.rsqrt(var + LN_EPS)
    return xc * rstd, rstd


def _ln_bwd_rows(dy, xhat, rstd, g):
    dxh = dy * g
    m1 = jnp.mean(dxh, axis=-1, keepdims=True)
    m2 = jnp.mean(dxh * xhat, axis=-1, keepdims=True)
    return rstd * (dxh - m1 - xhat * m2)


def _ln_fwd(a, b, g, beta, *, name):
    s, d = a.shape
    row = pl.BlockSpec((ROW_TILE, d), lambda i: (i, 0))
    par = pl.BlockSpec((1, d), lambda i: (0, 0))

    def body(a_ref, b_ref, g_ref, be_ref, y_ref, yb_ref):
        xhat, _ = _ln_stats(ALPHA * a_ref[...] + b_ref[...])
        y = xhat * g_ref[...] + be_ref[...]
        y_ref[...] = y
        yb_ref[...] = y.astype(BF16)

    return _pcall(body, grid=(s // ROW_TILE,), in_specs=[row, row, par, par], out_specs=(row, row),
                  out_shape=(jax.ShapeDtypeStruct((s, d), F32), jax.ShapeDtypeStruct((s, d), BF16)), name=name,
                  compiler_params=_cparams(("parallel",)))(a, b, g, beta)


def _ln_bwd(a, b, g, dys, coefs, *, name):
    s, d = a.shape
    row = pl.BlockSpec((ROW_TILE, d), lambda i: (i, 0))
    par = pl.BlockSpec((1, d), lambda i: (0, 0))
    n = len(dys)

    def body(*refs):
        a_ref, b_ref, g_ref = refs[:3]
        dy_refs = refs[3:3 + n]
        dt_ref, dtb_ref, dg_ref, db_ref = refs[3 + n:]
        xhat, rstd = _ln_stats(ALPHA * a_ref[...] + b_ref[...])
        dy = coefs[0] * dy_refs[0][...]
        for q in range(1, n):
            dy = dy + coefs[q] * dy_refs[q][...]
        dt = _ln_bwd_rows(dy, xhat, rstd, g_ref[...])
        dt_ref[...] = dt
        dtb_ref[...] = dt.astype(BF16)

        @pl.when(pl.program_id(0) == 0)
        def _():
            dg_ref[...] = jnp.zeros_like(dg_ref)
            db_ref[...] = jnp.zeros_like(db_ref)

        dg_ref[...] += jnp.sum(dy * xhat, axis=0, keepdims=True)
        db_ref[...] += jnp.sum(dy, axis=0, keepdims=True)

    return _pcall(body, grid=(s // ROW_TILE,), in_specs=[row, row, par] + [row] * n, out_specs=(row, row, par, par),
                  out_shape=(jax.ShapeDtypeStruct((s, d), F32), jax.ShapeDtypeStruct((s, d), BF16),
                             jax.ShapeDtypeStruct((1, d), F32), jax.ShapeDtypeStruct((1, d), F32)),
                  name=name, compiler_params=_cparams(("arbitrary",)))(a, b, g, *dys)


def _head(x2, gpre, ple, g, beta, tgt, *, name):
    s, d = x2.shape
    row = pl.BlockSpec((ROW_TILE, d), lambda i: (i, 0))
    par = pl.BlockSpec((1, d), lambda i: (0, 0))
    lsp = pl.BlockSpec((1, LANE), lambda i: (0, 0))

    def body(x2_ref, gp_ref, ple_ref, g_ref, be_ref, t_ref, loss_ref, dgp_ref, dple_ref, dt_ref, dg_ref, db_ref):
        gate = _sigmoid(gp_ref[...])
        ple_v = ple_ref[...]
        xhat, rstd = _ln_stats(ALPHA * x2_ref[...] + gate * ple_v)
        err = xhat * g_ref[...] + be_ref[...] - t_ref[...]
        dy = err * (1.0 / d)
        dt = _ln_bwd_rows(dy, xhat, rstd, g_ref[...])
        dt_ref[...] = dt
        dgp_ref[...] = (dt * ple_v * gate * (1.0 - gate)).astype(BF16)
        dple_ref[...] = (dt * gate).astype(BF16)

        @pl.when(pl.program_id(0) == 0)
        def _():
            loss_ref[...] = jnp.zeros_like(loss_ref)
            dg_ref[...] = jnp.zeros_like(dg_ref)
            db_ref[...] = jnp.zeros_like(db_ref)

        loss_ref[...] += 0.5 * jnp.sum(jnp.mean(err * err, axis=-1, keepdims=True))
        dg_ref[...] += jnp.sum(dy * xhat, axis=0, keepdims=True)
        db_ref[...] += jnp.sum(dy, axis=0, keepdims=True)

    sd = jax.ShapeDtypeStruct((s, d), F32)
    sb = jax.ShapeDtypeStruct((s, d), BF16)
    pd = jax.ShapeDtypeStruct((1, d), F32)
    return _pcall(body, grid=(s // ROW_TILE,), in_specs=[row, row, row, par, par, row],
                  out_specs=(lsp, row, row, row, par, par),
                  out_shape=(jax.ShapeDtypeStruct((1, LANE), F32), sb, sb, sd, pd, pd),
                  name=name, compiler_params=_cparams(("arbitrary",)))(x2, gpre, ple, g, beta, tgt)


CONV_R = 256
PAD = SUBLANE


def _shift_down(ext, s):
    if s == 0:
        return ext[PAD:, :]
    return pltpu.roll(ext, s, 0)[PAD:, :]


def _shift_up(ext, s):
    r = ext.shape[0] - PAD
    if s == 0:
        return ext[:r, :]
    return pltpu.roll(ext, r + PAD - s, 0)[:r, :]


def _conv_rows(xpad_ref, r0, w_ref):
    ext = xpad_ref[pl.ds(r0, CONV_R + PAD), :]
    acc = _shift_down(ext, 0) * w_ref[3:4, :]
    for k in range(3):
        acc = acc + _shift_down(ext, 3 - k) * w_ref[k:k + 1, :]
    return acc, ext


def _fill_front_padded(dst_ref, src_ref, s):
    dst_ref[0:PAD, :] = jnp.zeros((PAD, dst_ref.shape[1]), F32)

    def cp(q, _):
        r0 = pl.multiple_of(q * CONV_R, CONV_R)
        dst_ref[pl.ds(pl.multiple_of(PAD + r0, PAD), CONV_R), :] = src_ref[pl.ds(r0, CONV_R), :]
        return 0

    lax.fori_loop(0, s // CONV_R, cp, 0)


def _conv_silu_fwd(proj, w8, b, *, col0, width, ct, name):
    s = proj.shape[0]
    nb = col0 // ct

    def body(x_ref, w_ref, b_ref, o_ref, xpad):
        _fill_front_padded(xpad, x_ref, s)

        def step(q, _):
            r0 = pl.multiple_of(q * CONV_R, CONV_R)
            acc, _e = _conv_rows(xpad, r0, w_ref)
            pre = acc + b_ref[...]
            o_ref[pl.ds(r0, CONV_R), :] = pre * _sigmoid(pre)
            return 0

        lax.fori_loop(0, s // CONV_R, step, 0)

    return _pcall(
        body, grid=(width // ct,),
        in_specs=[pl.BlockSpec((s, ct), lambda j: (0, nb + j)), pl.BlockSpec((SUBLANE, ct), lambda j: (0, j)),
                  pl.BlockSpec((1, ct), lambda j: (0, j))],
        out_specs=pl.BlockSpec((s, ct), lambda j: (0, j)),
        out_shape=jax.ShapeDtypeStruct((s, width), F32),
        scratch_shapes=[pltpu.VMEM((s + PAD, ct), F32)], name=name,
        compiler_params=_cparams(("parallel",)))(proj, w8, b)


def _conv_bwd_rows(dpad_ref, r0, w_ref):
    return _conv_bwd_ext(dpad_ref[pl.ds(r0, CONV_R + PAD), :], w_ref)


def _conv_bwd_ext(ext, w_ref):
    acc = _shift_up(ext, 0) * w_ref[3:4, :]
    for k in range(3):
        acc = acc + _shift_up(ext, 3 - k) * w_ref[k:k + 1, :]
    return acc


def _conv_silu_bwd(proj, dact, w8, b, *, col0, width, ct, name):
    s = proj.shape[0]
    nb = col0 // ct

    def body(x_ref, d_ref, w_ref, b_ref, dx_ref, dwb_ref, xpad, dpad):
        _fill_front_padded(xpad, x_ref, s)
        dpad[pl.ds(s, PAD), :] = jnp.zeros((PAD, ct), F32)
        dwb_ref[...] = jnp.zeros_like(dwb_ref)

        def step(q, _):
            r0 = pl.multiple_of(q * CONV_R, CONV_R)
            acc, ext = _conv_rows(xpad, r0, w_ref)
            pre = acc + b_ref[...]
            sg = _sigmoid(pre)
            dpre = d_ref[pl.ds(r0, CONV_R), :] * sg * (1.0 + pre * (1.0 - sg))
            dpad[pl.ds(r0, CONV_R), :] = dpre
            for k in range(4):
                dwb_ref[k:k + 1, :] += jnp.sum(dpre * _shift_down(ext, 3 - k), axis=0, keepdims=True)
            dwb_ref[4:5, :] += jnp.sum(dpre, axis=0, keepdims=True)
            return 0

        lax.fori_loop(0, s // CONV_R, step, 0)

        def step2(q, _):
            r0 = pl.multiple_of(q * CONV_R, CONV_R)
            dx_ref[pl.ds(r0, CONV_R), :] = _conv_bwd_rows(dpad, r0, w_ref).astype(BF16)
            return 0

        lax.fori_loop(0, s // CONV_R, step2, 0)

    colb = pl.BlockSpec((s, ct), lambda j: (0, j))
    return _pcall(
        body, grid=(width // ct,),
        in_specs=[pl.BlockSpec((s, ct), lambda j: (0, nb + j)), colb, pl.BlockSpec((SUBLANE, ct), lambda j: (0, j)),
                  pl.BlockSpec((1, ct), lambda j: (0, j))],
        out_specs=(colb, pl.BlockSpec((SUBLANE, ct), lambda j: (0, j))),
        out_shape=(jax.ShapeDtypeStruct((s, width), BF16), jax.ShapeDtypeStruct((SUBLANE, width), F32)),
        scratch_shapes=[pltpu.VMEM((s + PAD, ct), F32), pltpu.VMEM((s + PAD, ct), F32)], name=name,
        compiler_params=_cparams(("parallel",)))(proj, dact, w8, b)


LRU_CT = 128


def _row_of(v, r):
    return jnp.sum(jnp.where(_iota((v.shape[0], 1), 0) == r, v, 0.0), axis=0, keepdims=True)


def _scan_fwd(a, u):
    r = a.shape[0]
    row = _iota((r, 1), 0)
    d = 1
    while d < r:
        valid = row >= d
        u = jnp.where(valid, a * pltpu.roll(u, d, 0) + u, u)
        a = jnp.where(valid, a * pltpu.roll(a, d, 0), a)
        d *= 2
    return a, u


def _scan_rev(b, u):
    r = b.shape[0]
    row = _iota((r, 1), 0)
    d = 1
    while d < r:
        valid = row < r - d
        u = jnp.where(valid, b * pltpu.roll(u, r - d, 0) + u, u)
        b = jnp.where(valid, b * pltpu.roll(b, r - d, 0), b)
        d *= 2
    return b, u


def _lru_chunk(xpad, r0, cw_ref, cb, wa, ba, wx, bx, sp):
    acc, ext = _conv_rows(xpad, r0, cw_ref)
    xl = acc + cb
    r = _sigmoid(_dot(xl, wa) + ba)
    i = _sigmoid(_dot(xl, wx) + bx)
    la = -LRU_C * r * sp
    a = jnp.exp(la)
    a2 = jnp.exp(2.0 * la)
    mult = jnp.sqrt(-jnp.tanh(la) * (a2 + 1.0))
    first = (r0 + _iota((CONV_R, 1), 0)) == 0
    mult = jnp.where(first, 1.0, mult)
    return ext, xl, r, i, a, a2, mult, first


def _lru_specs(s):
    ct = LRU_CT
    nb_g = COL_G // ct
    return dict(
        x=pl.BlockSpec((s, ct), lambda j: (0, j)),
        g=pl.BlockSpec((s, ct), lambda j: (0, nb_g + j)),
        col=pl.BlockSpec((s, ct), lambda j: (0, j)),
        cw=pl.BlockSpec((SUBLANE, ct), lambda j: (0, j)),
        vec=pl.BlockSpec((1, ct), lambda j: (0, j)),
        gate=pl.BlockSpec((None, ct, ct), lambda j: (j, 0, 0)),
    )


def _lru_fwd(proj, cw8, cb, wa_bd, ba, wx_bd, bx, ap, *, name, jobs=()):
    s = proj.shape[0]
    ct = LRU_CT
    sp_ = _lru_specs(s)

    def body(x_ref, g_ref, cw_ref, cb_ref, wa_ref, ba_ref, wx_ref, bx_ref, ap_ref, y_ref, h_ref, xpad):
        _fill_front_padded(xpad, x_ref, s)
        sp = _softplus(-ap_ref[...])

        def step(q, carry):
            r0 = pl.multiple_of(q * CONV_R, CONV_R)
            _e, xl, _r, i, a, _a2, mult, _f = _lru_chunk(xpad, r0, cw_ref, cb_ref[...], wa_ref[...], ba_ref[...],
                                                       wx_ref[...], bx_ref[...], sp)
            acum, ucum = _scan_fwd(a, xl * i * mult)
            h = acum * carry + ucum
            h_ref[pl.ds(r0, CONV_R), :] = h
            ge, _th = _gelu(g_ref[pl.ds(r0, CONV_R), :])
            y_ref[pl.ds(r0, CONV_R), :] = (ge * h).astype(BF16)
            return _row_of(h, CONV_R - 1)

        lax.fori_loop(0, s // CONV_R, step, jnp.zeros((1, ct), F32))

    (ymix, hs), jouts = _hosted(
        body, jobs, grid=(LRU_W // ct,),
        in_specs=[sp_["x"], sp_["g"], sp_["cw"], sp_["vec"], sp_["gate"], sp_["vec"], sp_["gate"], sp_["vec"], sp_["vec"]],
        out_specs=(sp_["col"], sp_["col"]),
        out_shape=(jax.ShapeDtypeStruct((s, LRU_W + SSD_W), BF16), jax.ShapeDtypeStruct((s, LRU_W), F32)),
        scratch_shapes=[pltpu.VMEM((s + PAD, ct), F32)],
        name=name, args=(proj, proj, cw8, cb, wa_bd, ba, wx_bd, bx, ap))
    return ((ymix, hs), jouts) if jobs else (ymix, hs)


def _lru_bwd(proj, dy, hs, cw8, cb, wa_bd, ba, wx_bd, bx, ap, *, name, jobs=()):
    s = proj.shape[0]
    ct = LRU_CT
    sp_ = _lru_specs(s)

    nq = s // CONV_R

    def body(x_ref, g_ref, dy_ref, h_ref, cw_ref, cb_ref, wa_ref, ba_ref, wx_ref, bx_ref, ap_ref,
             dx_ref, dg_ref, dcwb_ref, dwa_ref, dwx_ref, xpad, hpad):
        _fill_front_padded(xpad, x_ref, s)
        _fill_front_padded(hpad, h_ref, s)
        apv = ap_ref[...]
        sp = _softplus(-apv)
        cb_v, wa, ba_v, wx, bx_v = cb_ref[...], wa_ref[...], ba_ref[...], wx_ref[...], bx_ref[...]
        dcwb_ref[...] = jnp.zeros_like(dcwb_ref)
        dwa_ref[...] = jnp.zeros_like(dwa_ref)
        dwx_ref[...] = jnp.zeros_like(dwx_ref)

        def back(k, carry):
            g_next, a_next, dxl_next = carry
            last_row = _iota((CONV_R, 1), 0) == CONV_R - 1
            r0 = pl.multiple_of((nq - 1 - k) * CONV_R, CONV_R)
            ext, xl, r, i, a, a2, mult, first = _lru_chunk(xpad, r0, cw_ref, cb_v, wa, ba_v, wx, bx_v, sp)
            gv = g_ref[pl.ds(r0, CONV_R), :]
            dyv = dy_ref[pl.ds(r0, CONV_R), :]
            hext = hpad[pl.ds(r0, CONV_R + PAD), :]
            ge, th = _gelu(gv)
            dg_ref[pl.ds(r0, CONV_R), :] = (dyv * _shift_down(hext, 0) * _gelu_grad(gv, th)).astype(BF16)
            b = jnp.where(last_row, a_next, pltpu.roll(a, CONV_R - 1, 0))
            bcum, dcum = _scan_rev(b, dyv * ge)
            gval = dcum + bcum * g_next
            hprev = _shift_down(hext, 1)
            da = gval * hprev
            dxl = gval * i * mult
            di = gval * xl * mult
            dmult = jnp.where(first, 0.0, gval * xl * i)
            dla = da * a - dmult * a2 / mult
            dr = dla * (-LRU_C) * sp
            dcwb_ref[7:8, :] += jnp.sum(dla * (-LRU_C) * r, axis=0, keepdims=True)
            dpr = dr * r * (1.0 - r)
            dpi = di * i * (1.0 - i)
            dxl = dxl + _dot_nt(dpr, wa) + _dot_nt(dpi, wx)
            dwa_ref[...] += _dot_tn(xl, dpr)
            dwx_ref[...] += _dot_tn(xl, dpi)
            dcwb_ref[5:6, :] += jnp.sum(dpr, axis=0, keepdims=True)
            dcwb_ref[6:7, :] += jnp.sum(dpi, axis=0, keepdims=True)
            for tap in range(4):
                dcwb_ref[tap:tap + 1, :] += jnp.sum(dxl * _shift_down(ext, 3 - tap), axis=0, keepdims=True)
            dcwb_ref[4:5, :] += jnp.sum(dxl, axis=0, keepdims=True)
            dx_ref[pl.ds(r0, CONV_R), :] = _conv_bwd_ext(jnp.concatenate([dxl, dxl_next], axis=0), cw_ref).astype(BF16)
            return _row_of(gval, 0), _row_of(a, 0), dxl[:PAD, :]

        zero = jnp.zeros((1, ct), F32)
        lax.fori_loop(0, nq, back, (zero, zero, jnp.zeros((PAD, ct), F32)))
        dcwb_ref[7:8, :] = dcwb_ref[7:8, :] * (-_sigmoid(-apv))

    nt = LRU_W // ct
    outs, jouts = _hosted(
        body, jobs, grid=(nt,),
        in_specs=[sp_["x"], sp_["g"], sp_["col"], sp_["col"], sp_["cw"], sp_["vec"], sp_["gate"], sp_["vec"], sp_["gate"],
                  sp_["vec"], sp_["vec"]],
        out_specs=(sp_["col"], sp_["col"], sp_["cw"], sp_["gate"], sp_["gate"]),
        out_shape=(jax.ShapeDtypeStruct((s, LRU_W), BF16), jax.ShapeDtypeStruct((s, LRU_W), BF16),
                   jax.ShapeDtypeStruct((SUBLANE, LRU_W), F32), jax.ShapeDtypeStruct((nt, ct, ct), F32),
                   jax.ShapeDtypeStruct((nt, ct, ct), F32)),
        scratch_shapes=[pltpu.VMEM((s + PAD, ct), F32), pltpu.VMEM((s + PAD, ct), F32)],
        name=name, args=(proj, proj, dy, hs, cw8, cb, wa_bd, ba, wx_bd, bx, ap))
    return (tuple(outs), jouts) if jobs else tuple(outs)


def _ssd_prep(dtr, bias, alog_pad, alogx):
    l = CHUNK
    lane = _iota((1, LANE), 1)
    a_head = jnp.where(lane < N_HEAD, -jnp.exp(alog_pad), 0.0)
    dt = _softplus(dtr + bias)
    tril = (_iota((l, l), 1) <= _iota((l, l), 0)).astype(F32)
    cs = _dotx(tril, dt * a_head)
    expand = (jnp.right_shift(_iota((LANE, SSD_W), 1), 6) == _iota((LANE, SSD_W), 0)).astype(F32)
    dtx = _dotx(dt, expand)
    ax = -jnp.exp(alogx)
    csx = _dotx(tril, dtx * ax)
    totx = jnp.sum(dtx * ax, axis=0, keepdims=True)
    return dict(a_head=a_head, dt=dt, tril=tril, cs=cs, expand=expand, dtx=dtx, ax=ax, csx=csx, totx=totx)


def _decay_mat(cs, cst_ref, h, causal):
    lane = _iota((CHUNK, LANE), 1)
    col = jnp.sum(jnp.where(lane == h, cs, 0.0), axis=1, keepdims=True)
    row = cst_ref[h:h + 1, :]
    return jnp.exp(jnp.where(causal, col - row, NEG_BIG))


def _head_mask(j):
    lane = _iota((CHUNK, GROUP_W), 1)
    return (lane >= j * HEAD_P) & (lane < (j + 1) * HEAD_P)


def _ssd_group_fwd(q, g, xs_g, bg, cg, ht_g, cst_ref, causal, dx_g):
    sl = slice(g * GROUP_W, (g + 1) * GROUP_W)
    dtx_g, csx_g, totx_g = q["dtx"][:, sl], q["csx"][:, sl], q["totx"][:, sl]
    xdt = xs_g * dtx_g
    ex = jnp.exp(csx_g)
    cb = _dot_nt(cg, bg)
    yoff = _dot(cg, ht_g) * ex
    ydiag = jnp.zeros((CHUNK, GROUP_W), F32)
    for j in range(4):
        sc = cb * _decay_mat(q["cs"], cst_ref, 4 * g + j, causal)
        ydiag = jnp.where(_head_mask(j), _dot(sc, xdt), ydiag)
    y = ydiag + yoff + xs_g * dx_g
    dsx = jnp.exp(totx_g - csx_g)
    return y, dict(xdt=xdt, ex=ex, cb=cb, yoff=yoff, dsx=dsx, dtx=dtx_g, totx=totx_g)


def _gated_norm_fwd(y_g, z_g, w_g):
    sz = _sigmoid(z_g)
    silu = z_g * sz
    yf = y_g * silu
    rs = lax.rsqrt(jnp.mean(yf * yf, axis=1, keepdims=True) + RMS_EPS)
    yn = yf * rs
    return yn * w_g, (sz, silu, rs, yn)


def _ssd_fwd(xact, proj, ymix, bias_pad, alog_pad, alogx, dxp, normw, *, name, jobs=()):
    s = xact.shape[0]
    nc = s // CHUNK

    def body(xa_ref, dt_ref, z_ref, _ymix_ref, bias_ref, alp_ref, alx_ref, dx_ref, nw_ref, y_ref, hp_ref, ht, cst):
        @pl.when(pl.program_id(0) == 0)
        def _():
            ht[...] = jnp.zeros_like(ht)

        hp_ref[...] = ht[...]
        q = _ssd_prep(dt_ref[...], bias_ref[...], alp_ref[...], alx_ref[...])
        cst[...] = q["cs"].T
        causal = q["tril"] > 0.0
        for g in range(N_GROUP):
            sl = slice(g * GROUP_W, (g + 1) * GROUP_W)
            xs_g = xa_ref[:, sl]
            bg = xa_ref[:, SSD_W + g * N_STATE:SSD_W + (g + 1) * N_STATE]
            cg = xa_ref[:, SSD_W + N_GROUP * N_STATE + g * N_STATE:SSD_W + N_GROUP * N_STATE + (g + 1) * N_STATE]
            ht_g = ht[:, sl]
            y, f = _ssd_group_fwd(q, g, xs_g, bg, cg, ht_g, cst, causal, dx_ref[:, sl])
            out, _ = _gated_norm_fwd(y, z_ref[:, sl], nw_ref[:, sl])
            y_ref[:, sl] = out.astype(BF16)
            ht[:, sl] = jnp.exp(f["totx"]) * ht_g + _dot_tn(bg, f["xdt"] * f["dsx"])

    par = lambda w: pl.BlockSpec((1, w), lambda c: (0, 0))
    (ycat, hprev), jouts = _hosted(
        body, jobs, grid=(nc,),
        in_specs=[pl.BlockSpec((CHUNK, XBC), lambda c: (c, 0)),
                  pl.BlockSpec((CHUNK, LANE), lambda c: (c, COL_DT // LANE)),
                  pl.BlockSpec((CHUNK, SSD_W), lambda c: (c, COL_Z // SSD_W)),
                  ANY_SPEC, par(LANE), par(LANE), par(SSD_W), par(SSD_W), par(SSD_W)],
        out_specs=(pl.BlockSpec((CHUNK, SSD_W), lambda c: (c, LRU_W // SSD_W)),
                   pl.BlockSpec((None, N_STATE, SSD_W), lambda c: (c, 0, 0))),
        out_shape=(jax.ShapeDtypeStruct(ymix.shape, ymix.dtype), jax.ShapeDtypeStruct((nc, N_STATE, SSD_W), F32)),
        scratch_shapes=[pltpu.VMEM((N_STATE, SSD_W), F32), pltpu.VMEM((CHUNK, LANE), F32)],
        aliases={3: 0}, name=name, args=(xact, proj, proj, ymix, bias_pad, alog_pad, alogx, dxp, normw))
    return ((ycat, hprev), jouts) if jobs else (ycat, hprev)


def _ssd_bwd(xact, proj, dycat, hprev, bias_pad, alog_pad, alogx, dxp, normw, *, name, jobs=()):
    s = xact.shape[0]
    nc = s // CHUNK
    l = CHUNK

    def body(xa_ref, dt_ref, z_ref, dy_ref, hp_ref, bias_ref, alp_ref, alx_ref, dx_ref, nw_ref,
             dxa_ref, ddt_ref, dz_ref, dnw_ref, small_ref, dht, cst, accx, dcsx_s, ddtx_s):
        step = pl.program_id(0)

        @pl.when(step == 0)
        def _():
            dht[...] = jnp.zeros_like(dht)
            accx[...] = jnp.zeros_like(accx)
            dnw_ref[...] = jnp.zeros_like(dnw_ref)
            small_ref[...] = jnp.zeros_like(small_ref)

        dtr = dt_ref[...]
        q = _ssd_prep(dtr, bias_ref[...], alp_ref[...], alx_ref[...])
        cst[...] = q["cs"].T
        causal = q["tril"] > 0.0
        eye = _iota((l, l), 0) == _iota((l, l), 1)
        lane = _iota((l, LANE), 1)
        dcs_head = jnp.zeros((l, LANE), F32)
        for g in range(N_GROUP):
            sl = slice(g * GROUP_W, (g + 1) * GROUP_W)
            slb = slice(SSD_W + g * N_STATE, SSD_W + (g + 1) * N_STATE)
            slc = slice(SSD_W + N_GROUP * N_STATE + g * N_STATE, SSD_W + N_GROUP * N_STATE + (g + 1) * N_STATE)
            xs_g, bg, cg = xa_ref[:, sl], xa_ref[:, slb], xa_ref[:, slc]
            ht_g = hp_ref[:, sl]
            dxp_g = dx_ref[:, sl]
            y, f = _ssd_group_fwd(q, g, xs_g, bg, cg, ht_g, cst, causal, dxp_g)
            z_g, nw_g = z_ref[:, sl], nw_ref[:, sl]
            _o, (sz, silu, rs, yn) = _gated_norm_fwd(y, z_g, nw_g)
            dout = dy_ref[:, sl]
            dnw_ref[:, sl] += jnp.sum(dout * yn, axis=0, keepdims=True)
            dyn = dout * nw_g
            dyf = rs * (dyn - yn * jnp.mean(dyn * yn, axis=1, keepdims=True))
            dy = dyf * silu
            dz_ref[:, sl] = (dyf * y * sz * (1.0 + z_g * (1.0 - sz))).astype(BF16)
            accx[0:1, sl] += jnp.sum(dy * xs_g, axis=0, keepdims=True)
            dyo = dy * f["ex"]
            dcg = _dot_nt(dyo, ht_g)
            dht_prev = _dot_tn(cg, dyo)
            dcsx = dy * f["yoff"]
            xdt = f["xdt"]
            dxdt = jnp.zeros((l, GROUP_W), F32)
            dcb = jnp.zeros((l, l), F32)
            for j in range(4):
                h = 4 * g + j
                lm = _decay_mat(q["cs"], cst, h, causal)
                sc = f["cb"] * lm
                mask = _head_mask(j)
                ds_ = jnp.where(causal, _dot_nt(jnp.where(mask, dy, 0.0), xdt), 0.0)
                dxdt = jnp.where(mask, _dot_tn(sc, dy), dxdt)
                dcb = dcb + ds_ * lm
                m = ds_ * sc
                rsum = jnp.sum(m, axis=1, keepdims=True)
                csum = jnp.sum(m, axis=0, keepdims=True)
                csum_col = jnp.sum(jnp.where(eye, csum, 0.0), axis=1, keepdims=True)
                dcs_head = dcs_head + jnp.where(lane == h, rsum - csum_col, 0.0)
            dhn = dht[:, sl]
            etot = jnp.exp(f["totx"])
            dxd = _dot(bg, dhn)
            dbg = _dot_nt(xdt * f["dsx"], dhn)
            dxdt = dxdt + dxd * f["dsx"]
            qq = dxd * xdt * f["dsx"]
            dcsx = dcsx - qq
            dtot = jnp.sum(qq, axis=0, keepdims=True) + jnp.sum(dhn * ht_g, axis=0, keepdims=True) * etot
            dht[:, sl] = etot * dhn + dht_prev
            dcg = dcg + _dot(dcb, bg)
            dbg = dbg + _dot_tn(dcb, cg)
            dxa_ref[:, sl] = dxdt * f["dtx"] + dy * dxp_g
            dxa_ref[:, slb] = dbg
            dxa_ref[:, slc] = dcg
            dcsx_s[:, sl] = dcsx
            ddtx_s[:, sl] = dxdt * xs_g
            accx[2:3, sl] = dtot
        triu = (_iota((l, l), 1) >= _iota((l, l), 0)).astype(F32)
        dax = _dotx(triu, dcsx_s[...]) + accx[2:3, :]
        accx[1:2, :] += jnp.sum(dax * q["dtx"], axis=0, keepdims=True)
        reduce = (jnp.right_shift(_iota((SSD_W, LANE), 0), 6) == _iota((SSD_W, LANE), 1)).astype(F32)
        ddt = _dotx(ddtx_s[...] + dax * q["ax"], reduce)
        da_head = _dotx(triu, dcs_head)
        ddt = ddt + da_head * q["a_head"]
        small_ref[1:2, :] += jnp.sum(da_head * q["dt"], axis=0, keepdims=True)
        ddtr = ddt * _sigmoid(dtr + bias_ref[...])
        ddt_ref[...] = ddtr.astype(BF16)
        small_ref[0:1, :] += jnp.sum(ddtr, axis=0, keepdims=True)

        @pl.when(step == nc - 1)
        def _():
            red = _dotx(accx[...], reduce)
            d_a = small_ref[1:2, :] + red[1:2, :]
            small_ref[1:2, :] = d_a * q["a_head"]
            small_ref[2:3, :] = red[0:1, :]

    rev = lambda c: nc - 1 - c
    par = lambda w: pl.BlockSpec((1, w), lambda c: (0, 0))
    outs, jouts = _hosted(
        body, jobs, grid=(nc,),
        in_specs=[pl.BlockSpec((CHUNK, XBC), lambda c: (rev(c), 0)),
                  pl.BlockSpec((CHUNK, LANE), lambda c: (rev(c), COL_DT // LANE)),
                  pl.BlockSpec((CHUNK, SSD_W), lambda c: (rev(c), COL_Z // SSD_W)),
                  pl.BlockSpec((CHUNK, SSD_W), lambda c: (rev(c), 1)),
                  pl.BlockSpec((None, N_STATE, SSD_W), lambda c: (rev(c), 0, 0)),
                  par(LANE), par(LANE), par(SSD_W), par(SSD_W), par(SSD_W)],
        out_specs=(pl.BlockSpec((CHUNK, XBC), lambda c: (rev(c), 0)),
                   pl.BlockSpec((CHUNK, LANE), lambda c: (rev(c), 0)),
                   pl.BlockSpec((CHUNK, SSD_W), lambda c: (rev(c), 0)),
                   par(SSD_W), pl.BlockSpec((SUBLANE, LANE), lambda c: (0, 0))),
        out_shape=(jax.ShapeDtypeStruct((s, XBC), F32), jax.ShapeDtypeStruct((s, LANE), BF16),
                   jax.ShapeDtypeStruct((s, SSD_W), BF16), jax.ShapeDtypeStruct((1, SSD_W), F32),
                   jax.ShapeDtypeStruct((SUBLANE, LANE), F32)),
        scratch_shapes=[pltpu.VMEM((N_STATE, SSD_W), F32), pltpu.VMEM((CHUNK, LANE), F32),
                        pltpu.VMEM((SUBLANE, SSD_W), F32), pltpu.VMEM((CHUNK, SSD_W), F32),
                        pltpu.VMEM((CHUNK, SSD_W), F32)],
        name=name, args=(xact, proj, proj, dycat, hprev, bias_pad, alog_pad, alogx, dxp, normw))
    return (tuple(outs), jouts) if jobs else tuple(outs)


def _blockdiag(w):
    w2 = w.reshape(N_HEAD // 2, 2, HEAD_P, HEAD_P)
    z = jnp.zeros((N_HEAD // 2, HEAD_P, HEAD_P), w.dtype)
    top = jnp.concatenate([w2[:, 0], z], axis=2)
    bot = jnp.concatenate([z, w2[:, 1]], axis=2)
    return jnp.concatenate([top, bot], axis=1)


def _unblockdiag(wbd):
    a = wbd[:, :HEAD_P, :HEAD_P]
    b = wbd[:, HEAD_P:, HEAD_P:]
    return jnp.stack([a, b], axis=1).reshape(N_HEAD, HEAD_P, HEAD_P)


def _pad_rows8(w):
    return jnp.concatenate([w, jnp.zeros((SUBLANE - w.shape[0], w.shape[1]), w.dtype)], axis=0)


def _pad_lane(v):
    return jnp.concatenate([v, jnp.zeros((1, LANE - v.shape[1]), v.dtype)], axis=1)


class _NoExchange:
    def ride(self, host):
        return []

    def done(self, jobs, outs, w):
        pass

    def grad(self, name, val):
        pass

    def small(self, raw):
        pass


def _local_step(x, p, tgt, w, hooks=_NoExchange()):
    cw_l = _pad_rows8(w["lru_conv_w"])
    cw_s = _pad_rows8(w["ssd_conv_w"])
    wa_bd = _blockdiag(w["lru_gate_a_w"])
    wx_bd = _blockdiag(w["lru_gate_x_w"])
    ba = w["lru_gate_a_b"].reshape(1, LRU_W)
    bx = w["lru_gate_x_b"].reshape(1, LRU_W)
    bias_pad = _pad_lane(w["ssd_dt_bias"])
    alog_pad = _pad_lane(w["ssd_a_log"])
    alogx = jnp.repeat(w["ssd_a_log"], HEAD_P, axis=1)
    dxp = jnp.repeat(w["ssd_d"], HEAD_P, axis=1)

    def host(fn, *a, name, **k):
        jobs = hooks.ride(name)
        res = fn(*a, name=name, jobs=jobs, **k)
        if jobs:
            res, jouts = res
            hooks.done(jobs, jouts, w)
        return res

    def grad(n, val):
        g[n] = val
        hooks.grad(n, val)

    xb = x.astype(BF16)
    proj = host(_mm, xb, w["w_in"], "nn", tm=1024, tn=512, name="in_proj")
    ymix, h_lru = host(_lru_fwd, proj, cw_l, w["lru_conv_b"], wa_bd, ba, wx_bd, bx, w["lru_a_param"], name="lru_fwd")
    xact = _conv_silu_fwd(proj, cw_s, w["ssd_conv_b"], col0=COL_XBC, width=XBC, ct=256, name="ssd_conv_fwd")
    ycat, hprev = host(_ssd_fwd, xact, proj, ymix, bias_pad, alog_pad, alogx, dxp, w["ssd_norm_w"], name="ssd_fwd")
    mix = _mm(ycat, w["w_out"], "nn", tm=1024, tn=1024, name="out_proj")
    x1, x1b = _ln_fwd(x, mix, w["ln1_g"], w["ln1_b"], name="ln1_fwd")
    pre = _mm(x1b, w["w_ff1"], "nn", tm=1024, tn=512, out_dtype=BF16, name="ff1")
    ff = _mm(pre, w["w_ff2"], "nn", tm=512, tn=1024, a_fn=_relu2, name="ff2")
    x2, x2b = _ln_fwd(x1, ff, w["ln2_g"], w["ln2_b"], name="ln2_fwd")
    gpre = _mm(x2b, w["w_ple_gate"], "nn", tm=1024, tn=1024, name="ple_gate")
    ple = _mm(p, w["w_ple"], "nn", tm=1024, tn=1024, name="ple_proj")
    loss, dgpre, dple, dt3, dg3, db3 = _head(x2, gpre, ple, w["ln3_g"], w["ln3_b"], tgt, name="head")

    g = {}
    g["ln3_g"], g["ln3_b"] = dg3, db3
    grad("w_ple_gate", _mm(x2b, dgpre, "tn", tm=512, tn=1024, out_dtype=BF16, name="d_w_ple_gate"))
    grad("w_ple", _mm(p, dple, "tn", tm=256, tn=512, dest_major=True, out_dtype=BF16, name="d_w_ple"))
    dx2_mm = host(_mm, dgpre, w["w_ple_gate"], "nt", tm=1024, tn=1024, name="d_x2")
    dt2, dt2b, g["ln2_g"], g["ln2_b"] = _ln_bwd(x1, ff, w["ln2_g"], [dt3, dx2_mm], [ALPHA, 1.0], name="ln2_bwd")
    grad("w_ff2", host(_mm, pre, dt2b, "tn", tm=512, tn=1024, a_fn=_relu2, out_dtype=BF16, name="d_w_ff2"))
    dpre = host(_mm, dt2b, w["w_ff2"], "nt", tm=1024, tn=512, extra=pre, out_dtype=BF16,
                epi=lambda acc, pv: acc * 2.0 * jnp.maximum(pv.astype(F32), 0.0), name="d_pre")
    grad("w_ff1", host(_mm, x1b, dpre, "tn", tm=1024, tn=512, dest_major=True, out_dtype=BF16, name="d_w_ff1"))
    dx1_mm = host(_mm, dpre, w["w_ff1"], "nt", tm=512, tn=1024, name="d_x1")
    dt1, dt1b, g["ln1_g"], g["ln1_b"] = _ln_bwd(x, mix, w["ln1_g"], [dt2, dx1_mm], [ALPHA, 1.0], name="ln1_bwd")
    grad("w_out", host(_mm, ycat, dt1b, "tn", tm=512, tn=1024, out_dtype=BF16, name="d_w_out"))
    dycat = host(_mm, dt1b, w["w_out"], "nt", tm=1024, tn=1024, name="d_ycat")
    dxl, dgl, dcwb_l, dwa, dwx = host(_lru_bwd, proj, dycat, h_lru, cw_l, w["lru_conv_b"], wa_bd, ba, wx_bd, bx,
                                      w["lru_a_param"], name="lru_bwd")
    g["lru_gate_a_w"] = _unblockdiag(dwa)
    g["lru_gate_x_w"] = _unblockdiag(dwx)
    raw = dict(lru=dcwb_l, gate_a=g["lru_gate_a_w"].reshape(N_HEAD * HEAD_P, HEAD_P),
               gate_x=g["lru_gate_x_w"].reshape(N_HEAD * HEAD_P, HEAD_P))
    hooks.small(raw)
    dxact, ddt, dz, g["ssd_norm_w"], small = host(_ssd_bwd, xact, proj, dycat, hprev, bias_pad, alog_pad, alogx, dxp,
                                                   w["ssd_norm_w"], name="ssd_bwd")
    dxbc, dcwb_s = _conv_silu_bwd(proj, dxact, cw_s, w["ssd_conv_b"], col0=COL_XBC, width=XBC, ct=256,
                                  name="ssd_conv_bwd")
    s = x.shape[0]
    dproj = jnp.concatenate([dxl, dgl, dz, dxbc, ddt, jnp.zeros((s, D_IN_PAD - COL_DT - LANE), BF16)], axis=1)

    g["lru_conv_w"] = dcwb_l[0:4]
    g["lru_conv_b"] = dcwb_l[4:5]
    g["lru_gate_a_b"] = dcwb_l[5:6]
    g["lru_gate_x_b"] = dcwb_l[6:7]
    g["lru_a_param"] = dcwb_l[7:8]
    g["ssd_conv_w"] = dcwb_s[0:4]
    g["ssd_conv_b"] = dcwb_s[4:5]
    g["ssd_dt_bias"] = small[0:1, :N_HEAD]
    g["ssd_a_log"] = small[1:2, :N_HEAD]
    g["ssd_d"] = small[2:3, :N_HEAD]
    rows = jnp.concatenate([g[n] for n in ("ssd_norm_w", "ln1_g", "ln1_b", "ln2_g", "ln2_b", "ln3_g", "ln3_b")]
                           + [jnp.broadcast_to(loss[:, 0:1], (1, D_MODEL))], axis=0)
    late = dict(ssd=dcwb_s, heads=small, rows=rows)
    hooks.small(late)
    raw.update(late)
    grad("w_in", host(_mm, xb, dproj, "tn", tm=1024, tn=512, out_dtype=BF16, name="d_w_in"))
    grad_x = host(_mm, dproj, w["w_in"], "nt", tm=256, tn=1024, extra=dt1, epi=lambda acc, e: acc + ALPHA * e,
                  name="d_x")
    return loss[0, 0], grad_x, g, raw


ANY_SPEC = pl.BlockSpec(memory_space=pl.ANY)


def _mesh_pos():
    return lax.axis_index("x"), lax.axis_index("y"), lax.axis_index("c")


def _remote(src, dst, send, recv, k, to):
    return pltpu.make_async_remote_copy(src_ref=src, dst_ref=dst, send_sem=send.at[k], recv_sem=recv.at[k],
                                        device_id=to, device_id_type=MESH_T)


class _Job:
    N_SEM = 7

    def __init__(self, kind, inp):
        self.kind, self.inp = kind, inp
        shape = {"gather": (N_DEV,) + inp.shape, "pair": (4,) + inp.shape[1:], "chip": inp.shape}[kind]
        self.out = jax.ShapeDtypeStruct(shape, inp.dtype)

    def _places(self):
        x, y, c = _mesh_pos()
        return (x, y, c), (x, y, 1 - c), [(1 - x, y), (x, 1 - y), (1 - x, 1 - y)]

    def start(self, inp, out, send, recv, loc):
        me, sibling, chips = self._places()
        x, y, c = me
        if self.kind == "gather":
            mine = out.at[4 * x + 2 * y + c]
            pltpu.make_async_copy(inp, mine, loc.at[0]).start()
            _remote(inp, mine, send, recv, 0, sibling).start()
            for j, chip in enumerate(chips):
                _remote(inp, mine, send, recv, 1 + j, (*chip, c)).start()
        elif self.kind == "pair":
            for k in range(4):
                _remote(inp.at[2 * k + (1 - c)], out.at[k], send, recv, k, sibling).start()
        else:
            kme = 2 * x + y
            pltpu.make_async_copy(inp.at[kme], out.at[kme], loc.at[0]).start()
            for j, (tx, ty) in enumerate(chips):
                _remote(inp.at[2 * tx + ty], out.at[kme], send, recv, j, (tx, ty, c)).start()

    def finish(self, inp, out, send, recv, loc):
        me, sibling, chips = self._places()
        x, y, c = me
        if self.kind == "gather":
            blk = lambda px, py, pc: out.at[4 * px + 2 * py + pc]
            mine = blk(*me)
            for j, chip in enumerate(chips):
                landed = blk(*chip, c)
                _remote(landed, landed, send, recv, 1 + j, me).wait_recv()
                _remote(landed, landed, send, recv, 4 + j, sibling).start()
            _remote(inp, blk(*sibling), send, recv, 0, me).wait_recv()
            for j, chip in enumerate(chips):
                _remote(inp, blk(*chip, 1 - c), send, recv, 4 + j, me).wait_recv()
            for k in range(7):
                _remote(inp, mine, send, recv, k, sibling).wait_send()
            pltpu.make_async_copy(inp, mine, loc.at[0]).wait()
        elif self.kind == "pair":
            for k in range(4):
                _remote(inp.at[2 * k + (1 - c)], out.at[k], send, recv, k, sibling).wait()
        else:
            kme = 2 * x + y
            for j, (tx, ty) in enumerate(chips):
                _remote(inp.at[kme], out.at[2 * tx + ty], send, recv, j, (tx, ty, c)).wait_recv()
            for j, (tx, ty) in enumerate(chips):
                _remote(inp.at[2 * tx + ty], out.at[kme], send, recv, j, (tx, ty, c)).wait_send()
            pltpu.make_async_copy(inp.at[kme], out.at[kme], loc.at[0]).wait()


def _job_scratch(jobs):
    sem = pltpu.SemaphoreType.DMA
    return [s for _ in jobs for s in (sem((_Job.N_SEM,)), sem((_Job.N_SEM,)), sem((1,)))]


def _run_jobs(jobs, method, jins, jouts, jsems):
    for q, job in enumerate(jobs):
        getattr(job, method)(jins[q], jouts[q], *jsems[3 * q:3 * q + 3])


def _exchange(jobs, *, name):
    n = len(jobs)

    def body(*refs):
        jins, jouts, jsems = refs[:n], refs[n:2 * n], refs[2 * n:]
        _run_jobs(jobs, "start", jins, jouts, jsems)
        _run_jobs(jobs, "finish", jins, jouts, jsems)

    return _pcall(body, in_specs=[ANY_SPEC] * n, out_specs=[ANY_SPEC] * n, out_shape=[j.out for j in jobs],
                  scratch_shapes=_job_scratch(jobs), name=name)(*[j.inp for j in jobs])


def _hosted(body, jobs, *, grid, in_specs, out_specs, out_shape, args, name, scratch_shapes=(), aliases=None):
    in_specs, out_specs, out_shape = list(in_specs), list(out_specs), list(out_shape)
    scratch_shapes = list(scratch_shapes)
    n_in, n_out, n_scr, nj = len(in_specs), len(out_specs), len(scratch_shapes), len(jobs)
    sem = ("arbitrary",) * len(grid)
    kw = dict(input_output_aliases=aliases) if aliases else {}
    if not jobs:
        res = _pcall(body, grid=grid, in_specs=in_specs, out_specs=out_specs, out_shape=out_shape,
                     scratch_shapes=scratch_shapes, name=name, compiler_params=_cparams(sem), **kw)(*args)
        return list(res), []

    def full(*refs):
        ins, jins = refs[:n_in], refs[n_in:n_in + nj]
        o0 = n_in + nj
        outs, jouts = refs[o0:o0 + n_out], refs[o0 + n_out:o0 + n_out + nj]
        s0 = o0 + n_out + nj
        scr, jsems = refs[s0:s0 + n_scr], refs[s0 + n_scr:]
        first = pl.program_id(0) == 0
        last = pl.program_id(0) == grid[0] - 1
        for ax in range(1, len(grid)):
            first = jnp.logical_and(first, pl.program_id(ax) == 0)
            last = jnp.logical_and(last, pl.program_id(ax) == grid[ax] - 1)

        @pl.when(first)
        def _():
            _run_jobs(jobs, "start", jins, jouts, jsems)

        body(*ins, *outs, *scr)

        @pl.when(last)
        def _():
            _run_jobs(jobs, "finish", jins, jouts, jsems)

    res = _pcall(full, grid=grid, in_specs=in_specs + [ANY_SPEC] * nj, out_specs=out_specs + [ANY_SPEC] * nj,
                 out_shape=out_shape + [j.out for j in jobs], scratch_shapes=scratch_shapes + _job_scratch(jobs),
                 name=name, compiler_params=_cparams(sem), **kw)(*args, *[j.inp for j in jobs])
    return list(res[:n_out]), list(res[n_out:])


def _pair_add(g8, r4, cidx, *, name):
    _, r, c = g8.shape
    tr = min(r, ROW_TILE)

    def body(c_ref, g_ref, r_ref, o_ref):
        o_ref[...] = (g_ref[...].astype(F32) + r_ref[...].astype(F32)).astype(BF16)

    return _pcall(
        body,
        grid_spec=pltpu.PrefetchScalarGridSpec(
            num_scalar_prefetch=1, grid=(4, r // tr),
            in_specs=[pl.BlockSpec((None, tr, c), lambda k, i, cr: (2 * k + cr[0], i, 0)),
                      pl.BlockSpec((None, tr, c), lambda k, i, cr: (k, i, 0))],
            out_specs=pl.BlockSpec((None, tr, c), lambda k, i, cr: (k, i, 0))),
        out_shape=jax.ShapeDtypeStruct((4, r, c), BF16), name=name,
        compiler_params=_cparams(("parallel", "parallel")))(cidx, g8, r4)


def _adam_update(g, w_ref, m_ref, v_ref, g_ref, d_ref, mo_ref, vo_ref):
    c1 = 1.0 - ADAM_B1 ** ADAM_STEP
    c2 = 1.0 - ADAM_B2 ** ADAM_STEP
    m2 = ADAM_B1 * m_ref[...] + (1.0 - ADAM_B1) * g
    v2 = ADAM_B2 * v_ref[...] + (1.0 - ADAM_B2) * (g * g)
    g_ref[...] = g
    mo_ref[...] = m2
    vo_ref[...] = v2
    d_ref[...] = -ADAM_LR * ((m2 / c1) / (jnp.sqrt(v2 / c2) + ADAM_EPS) + ADAM_WD * w_ref[...])


def _adamw_rows(srcs, items, own_cols, me1, *, name):
    ns, ni, no = len(srcs), len(items), len(own_cols)
    full = lambda a: pl.BlockSpec(a.shape, lambda i, me: (0,) * a.ndim)
    in_specs = [full(a) for a in srcs]
    args = list(srcs)
    for (si, _r0, w, _m, _v) in own_cols:
        a = srcs[si]
        in_specs.append(pl.BlockSpec((N_DEV, a.shape[1], w.shape[1]), lambda i, me: (0, 0, me[0])))
        args.append(a)
    out_specs, out_shape = [], []
    for (_si, _r0, w, m, v) in list(items) + list(own_cols):
        in_specs += [full(w)] * 3
        args += [w, m, v]
        out_specs += [full(w)] * 4
        out_shape += [jax.ShapeDtypeStruct(w.shape, F32)] * 4

    def body(me_ref, *refs):
        src_refs, own_refs = refs[:ns], refs[ns:ns + no]
        wmv = refs[ns + no:ns + no + 3 * (ni + no)]
        outs = refs[ns + no + 3 * (ni + no):]
        for q, (si, r0, w, _m, _v) in enumerate(list(items) + list(own_cols)):
            nr, cw = w.shape
            gref = src_refs[si] if q < ni else own_refs[q - ni]
            g = gref[0, r0:r0 + nr, 0:cw]
            for d in range(1, N_DEV):
                g = g + gref[d, r0:r0 + nr, 0:cw]
            _adam_update(g, *wmv[3 * q:3 * q + 3], *outs[4 * q:4 * q + 4])

    res = _pcall(
        body,
        grid_spec=pltpu.PrefetchScalarGridSpec(num_scalar_prefetch=1, grid=(1,), in_specs=in_specs, out_specs=out_specs),
        out_shape=out_shape, name=name, compiler_params=_cparams(("arbitrary",)))(me1, *args)
    return [tuple(res[4 * q:4 * q + 4]) for q in range(ni + no)]


def _adamw(gsrc, w, m, v, *, name):
    k, r, c = gsrc.shape
    tr = ROW_TILE if r % ROW_TILE == 0 else r

    def body(gs_ref, w_ref, m_ref, v_ref, g_ref, d_ref, mo_ref, vo_ref):
        g = gs_ref[0].astype(F32)
        for q in range(1, k):
            g = g + gs_ref[q].astype(F32)
        _adam_update(g, w_ref, m_ref, v_ref, g_ref, d_ref, mo_ref, vo_ref)

    row = pl.BlockSpec((tr, c), lambda i: (i, 0))
    sd = jax.ShapeDtypeStruct((r, c), F32)
    return _pcall(body, grid=(r // tr,), in_specs=[pl.BlockSpec((k, tr, c), lambda i: (0, i, 0)), row, row, row],
                  out_specs=(row, row, row, row), out_shape=(sd, sd, sd, sd), name=name,
                  compiler_params=_cparams(("parallel",)))(gsrc, w, m, v)


WEIGHTS = ['w_in', 'lru_conv_w', 'lru_conv_b', 'lru_gate_a_w', 'lru_gate_a_b', 'lru_gate_x_w', 'lru_gate_x_b',
           'lru_a_param', 'ssd_conv_w', 'ssd_conv_b', 'ssd_dt_bias', 'ssd_a_log', 'ssd_d', 'ssd_norm_w', 'w_out',
           'ln1_g', 'ln1_b', 'w_ff1', 'w_ff2', 'ln2_g', 'ln2_b', 'w_ple_gate', 'w_ple', 'ln3_g', 'ln3_b']
BIG = ['w_in', 'w_out', 'w_ff1', 'w_ff2', 'w_ple_gate', 'w_ple']
COL_SHARDED = ('w_in', 'w_ff1', 'w_ple')
CONV = ['lru_conv_w', 'ssd_conv_w']
REPL = [n for n in WEIGHTS if n not in BIG and n not in CONV]
CONV_CH = {'lru_conv_w': LRU_W, 'ssd_conv_w': XBC}


def _to_dest_major(name, gfull):
    if name == 'w_in':
        gfull = gfull[:, :D_IN]
    if name in COL_SHARDED:
        r, cfull = gfull.shape
        return gfull.reshape(r, N_DEV, cfull // N_DEV).transpose(1, 0, 2)
    rfull, cdim = gfull.shape
    return gfull.reshape(N_DEV, rfull // N_DEV, cdim)


def _full_weight(name, gathered):
    if name in COL_SHARDED:
        _, r, cs = gathered.shape
        full = gathered.transpose(1, 0, 2).reshape(r, N_DEV * cs)
    else:
        _, rs, cdim = gathered.shape
        full = gathered.reshape(N_DEV * rs, cdim)
    if name == 'w_in':
        full = jnp.concatenate([full, jnp.zeros((D_MODEL, D_IN_PAD - D_IN), full.dtype)], axis=1)
    return full


SMALL_SRC = ("lru", "ssd", "heads", "rows", "gate_a", "gate_x")
AG_HOSTS = {"in_proj": ("w_ff1",), "lru_fwd": ("w_out", "w_ple_gate", "w_ple"), "ssd_fwd": ("w_ff2",)}
PAIR_HOSTS = ("d_x2", "d_pre", "d_x1", "d_ycat", "d_x")
CHIP_HOSTS = {"lru_bwd": ("w_ple_gate", "w_ple", "w_ff2"), "ssd_bwd": ("w_ff1", "w_out")}
SMALL_HOSTS = {"ssd_bwd": ("lru", "gate_a", "gate_x"), "d_w_in": ("ssd", "heads", "rows")}


class _Schedule:
    def __init__(self, shards, cidx):
        self.shards, self.cidx = shards, cidx
        self.pair, self.chip, self.small_jobs = [], [], []
        self.dest, self.summed, self.gathered_small = {}, {}, {}
        self.tags = []

    def ride(self, host):
        tags = []
        if host in AG_HOSTS:
            tags = [("weight", n, self.shards[n]) for n in AG_HOSTS[host]]
        elif host in PAIR_HOSTS or host in CHIP_HOSTS or host == "flush":
            tags = [("pair", n, a) for n, a in self.pair]
            self.pair = []
            if host not in PAIR_HOSTS:
                take = [t for t in self.chip if host == "flush" or t[0] in CHIP_HOSTS[host]]
                tags += [("chip", n, a) for n, a in take]
                self.chip = [t for t in self.chip if not any(t is u for u in take)]
        if host in SMALL_HOSTS:
            tags += [("small", n, a) for n, a in self.small_jobs if n in SMALL_HOSTS[host]]
            self.small_jobs = [t for t in self.small_jobs if t[0] not in SMALL_HOSTS[host]]
        self.tags = tags
        return [_Job({"weight": "gather", "small": "gather"}.get(kind, kind), a) for kind, _n, a in tags]

    def done(self, jobs, outs, w):
        for (kind, n, _a), o in zip(self.tags, outs):
            if kind == "weight":
                w[n] = _full_weight(n, o)
            elif kind == "small":
                self.gathered_small[n] = o
            elif kind == "pair":
                self.chip.append((n, _pair_add(self.dest[n], o, self.cidx, name="rs_pair_add_" + n)))
            else:
                self.summed[n] = o

    def grad(self, name, val):
        self.dest[name] = val if val.ndim == 3 else _to_dest_major(name, val)
        self.pair.append((name, self.dest[name]))

    def small(self, raw):
        self.small_jobs += list(raw.items())

    def flush(self):
        step = 0
        while self.pair or self.chip:
            jobs = self.ride("flush")
            self.done(jobs, _exchange(jobs, name="rs_flush_%d" % step), None)
            step += 1


def kernel(x, p, w_in, lru_conv_w, lru_conv_b, lru_gate_a_w, lru_gate_a_b, lru_gate_x_w, lru_gate_x_b, lru_a_param, ssd_conv_w, ssd_conv_b, ssd_dt_bias, ssd_a_log, ssd_d, ssd_norm_w, w_out, ln1_g, ln1_b, w_ff1, w_ff2, ln2_g, ln2_b, w_ple_gate, w_ple, ln3_g, ln3_b, loss_target, m_w_in, m_lru_conv_w, m_lru_conv_b, m_lru_gate_a_w, m_lru_gate_a_b, m_lru_gate_x_w, m_lru_gate_x_b, m_lru_a_param, m_ssd_conv_w, m_ssd_conv_b, m_ssd_dt_bias, m_ssd_a_log, m_ssd_d, m_ssd_norm_w, m_w_out, m_ln1_g, m_ln1_b, m_w_ff1, m_w_ff2, m_ln2_g, m_ln2_b, m_w_ple_gate, m_w_ple, m_ln3_g, m_ln3_b, v_w_in, v_lru_conv_w, v_lru_conv_b, v_lru_gate_a_w, v_lru_gate_a_b, v_lru_gate_x_w, v_lru_gate_x_b, v_lru_a_param, v_ssd_conv_w, v_ssd_conv_b, v_ssd_dt_bias, v_ssd_a_log, v_ssd_d, v_ssd_norm_w, v_w_out, v_ln1_g, v_ln1_b, v_w_ff1, v_w_ff2, v_ln2_g, v_ln2_b, v_w_ple_gate, v_w_ple, v_ln3_g, v_ln3_b):
    given = dict(locals())
    wsh = {n: given[n][0] for n in WEIGHTS}
    msh = {n: given["m_" + n][0] for n in WEIGHTS}
    vsh = {n: given["v_" + n][0] for n in WEIGHTS}
    xi, yi, ci = _mesh_pos()
    me = 4 * xi + 2 * yi + ci

    shards = {n: wsh[n].astype(BF16) for n in BIG}
    conv_pack = jnp.concatenate([_pad_rows8(wsh[n]) for n in CONV], axis=1)
    g_in, gconv = _exchange([_Job("gather", shards['w_in']), _Job("gather", conv_pack)], name="ag_first")
    full = {'w_in': _full_weight('w_in', g_in)}
    c0 = 0
    for n in CONV:
        cw = CONV_CH[n] // N_DEV
        full[n] = gconv[:, :4, c0:c0 + cw].transpose(1, 0, 2).reshape(4, CONV_CH[n])
        c0 += cw
    for n in REPL:
        full[n] = given[n] if given[n].ndim == 2 else wsh[n]

    sched = _Schedule(shards, jnp.reshape(ci, (1,)).astype(jnp.int32))
    loss_local, grad_x, g, raw = _local_step(x[0], p[0, 0], loss_target[0], full, sched)
    sched.flush()
    summed, gat = sched.summed, sched.gathered_small
    loss = gat["rows"][0, 7, 0]
    for d in range(1, N_DEV):
        loss = loss + gat["rows"][d, 7, 0]

    outs = {}
    for n in BIG:
        outs[n] = _adamw(summed[n], wsh[n], msh[n], vsh[n], name="adamw_" + n)
    for n, k in (("lru_gate_a_w", "gate_a"), ("lru_gate_x_w", "gate_x")):
        flat = lambda a: a.reshape(N_HEAD * HEAD_P, HEAD_P)
        res = _adamw(gat[k], flat(wsh[n]), flat(msh[n]), flat(vsh[n]), name="adamw_" + n)
        outs[n] = tuple(r.reshape(N_HEAD, HEAD_P, HEAD_P) for r in res)
    row_items = [("lru_conv_b", 0, 4), ("lru_gate_a_b", 0, 5), ("lru_gate_x_b", 0, 6), ("lru_a_param", 0, 7),
                 ("ssd_conv_b", 1, 4), ("ssd_dt_bias", 2, 0), ("ssd_a_log", 2, 1), ("ssd_d", 2, 2),
                 ("ssd_norm_w", 3, 0), ("ln1_g", 3, 1), ("ln1_b", 3, 2), ("ln2_g", 3, 3), ("ln2_b", 3, 4),
                 ("ln3_g", 3, 5), ("ln3_b", 3, 6)]
    vec = lambda a: a.reshape(1, -1)
    items = [(si, r0, vec(given[n]), vec(given["m_" + n]), vec(given["v_" + n])) for n, si, r0 in row_items]
    own = [(si, 0, wsh[n], msh[n], vsh[n]) for n, si in (("lru_conv_w", 0), ("ssd_conv_w", 1))]
    me1 = jnp.reshape(me, (1,)).astype(jnp.int32)
    res = _adamw_rows([gat[k] for k in SMALL_SRC[:4]], items, own, me1, name="adamw_small")
    for (n, _si, _r0), r4 in zip(row_items, res[:len(row_items)]):
        outs[n] = r4
    for n, r4 in zip(CONV, res[len(row_items):]):
        outs[n] = r4

    fin = lambda n, k: outs[n][k].reshape(given[n].shape)
    return (loss, grad_x[None],
            *[fin(n, 0) for n in WEIGHTS], *[fin(n, 1) for n in WEIGHTS],
            *[fin(n, 2) for n in WEIGHTS], *[fin(n, 3) for n in WEIGHTS])
```

```python
import math

import jax
import jax.numpy as jnp
from jax import lax
from jax.experimental import pallas as pl
from jax.experimental.pallas import tpu as pltpu

F32 = jnp.float32
BF16 = jnp.bfloat16
HI = lax.Precision.HIGHEST

N_DEV = 8
D_MODEL = 1024
LRU_W = 1024
SSD_W = 1024
XBC = 2048
N_HEAD = 16
HEAD_P = 64
N_GROUP = 4
GROUP_W = 256
N_STATE = 128
CHUNK = 128
D_FF = 4096
PLE_DIM = 256
D_IN = 5136
D_IN_PAD = 5632
COL_G = 1024
COL_Z = 2048
COL_XBC = 3072
COL_DT = 5120
LRU_C = 8.0
ALPHA = 2.0 ** 0.25
LN_EPS = 1e-5
RMS_EPS = 1e-5
ADAM_LR = 0.001
ADAM_B1 = 0.9
ADAM_B2 = 0.999
ADAM_EPS = 1e-08
ADAM_WD = 0.01
ADAM_STEP = 10
GELU_C = math.sqrt(2.0 / math.pi)
LANE = 128
SUBLANE = 8
VMEM_LIMIT = 48 * 1024 * 1024
MESH_T = pl.DeviceIdType.MESH
NEG_BIG = -1e30


def _pcall(body, **kw):
    return pl.pallas_call(body, **kw)


def _cparams(sem):
    return pltpu.CompilerParams(dimension_semantics=sem, vmem_limit_bytes=VMEM_LIMIT)


def _dot(a, b):
    return jnp.dot(a.astype(BF16), b.astype(BF16), preferred_element_type=F32)


def _dot_nt(a, b):
    return lax.dot_general(a.astype(BF16), b.astype(BF16), (((1,), (1,)), ((), ())), preferred_element_type=F32)


def _dot_tn(a, b):
    return lax.dot_general(a.astype(BF16), b.astype(BF16), (((0,), (0,)), ((), ())), preferred_element_type=F32)


def _dotx(a, b):
    return jnp.dot(a, b, precision=HI, preferred_element_type=F32)


def _sigmoid(x):
    return jax.nn.sigmoid(x)


def _softplus(v):
    return jnp.maximum(v, 0.0) + jnp.log1p(jnp.exp(-jnp.abs(v)))


def _gelu(x):
    th = jnp.tanh(GELU_C * (x + 0.044715 * x * x * x))
    return 0.5 * x * (1.0 + th), th


def _gelu_grad(x, th):
    return 0.5 * (1.0 + th) + 0.5 * x * (1.0 - th * th) * GELU_C * (1.0 + 3.0 * 0.044715 * x * x)


def _iota(shape, dim):
    return lax.broadcasted_iota(jnp.int32, shape, dim)


def _mm(a, b, mode, *, tm, tn, name, a_fn=None, extra=None, epi=None, out_dtype=F32, dest_major=False, jobs=()):
    m = a.shape[1] if mode == "tn" else a.shape[0]
    n = b.shape[0] if mode == "nt" else b.shape[1]
    tm, tn = min(tm, m), min(tn, n)
    if dest_major:
        tn = n // N_DEV
    if mode == "nn":
        m, k = a.shape
        _, n = b.shape
        a_spec = pl.BlockSpec((tm, k), lambda i, j: (i, 0))
        b_spec = pl.BlockSpec((k, tn), lambda i, j: (0, j))
        dims = ((1,), (0,))
    elif mode == "nt":
        m, k = a.shape
        n, _ = b.shape
        a_spec = pl.BlockSpec((tm, k), lambda i, j: (i, 0))
        b_spec = pl.BlockSpec((tn, k), lambda i, j: (j, 0))
        dims = ((1,), (1,))
    else:
        k, m = a.shape
        _, n = b.shape
        a_spec = pl.BlockSpec((k, tm), lambda i, j: (0, i))
        b_spec = pl.BlockSpec((k, tn), lambda i, j: (0, j))
        dims = ((0,), (0,))
    assert m % tm == 0 and n % tn == 0, (name, m, n, tm, tn)
    o_spec = pl.BlockSpec((tm, tn), lambda i, j: (i, j))
    in_specs = [a_spec, b_spec]
    args = [a, b]
    if extra is not None:
        in_specs.append(o_spec)
        args.append(extra)

    def body(*refs):
        a_ref, b_ref, o_ref = refs[0], refs[1], refs[-1]
        av = a_ref[...]
        if a_fn is not None:
            av = a_fn(av)
        acc = lax.dot_general(av.astype(BF16), b_ref[...].astype(BF16), (dims, ((), ())), preferred_element_type=F32)
        if epi is not None:
            acc = epi(acc, refs[2][...])
        o_ref[...] = acc.astype(out_dtype)

    out_shape = jax.ShapeDtypeStruct((m, n), out_dtype)
    if dest_major:
        assert extra is None
        o_spec = pl.BlockSpec((None, tm, tn), lambda i, j: (j, i, 0))
        out_shape = jax.ShapeDtypeStruct((N_DEV, m, tn), out_dtype)
    (out,), jouts = _hosted(body, jobs, grid=(m // tm, n // tn), in_specs=in_specs, out_specs=[o_spec],
                            out_shape=[out_shape], args=args, name=name)
    return (out, jouts) if jobs else out


def _mm_pieces(pieces, offsets, b, *, tm, name, extra, epi, jobs=()):
    m = pieces[0].shape[0]
    kb, n = b.shape
    tm = min(tm, m)
    row = lambda wdt: pl.BlockSpec((tm, wdt), lambda i: (i, 0))
    in_specs = [row(pc.shape[1]) for pc in pieces] + [pl.BlockSpec((kb, n), lambda i: (0, 0)), row(n)]
    np_ = len(pieces)

    def body(*refs):
        b_ref, e_ref, o_ref = refs[np_], refs[np_ + 1], refs[np_ + 2]
        acc = jnp.zeros((tm, n), F32)
        for q in range(np_):
            kq = pieces[q].shape[1]
            acc = acc + jnp.dot(refs[q][...].astype(BF16), b_ref[offsets[q]:offsets[q] + kq, :].astype(BF16),
                                preferred_element_type=F32)
        o_ref[...] = epi(acc, e_ref[...])

    (out,), jouts = _hosted(body, jobs, grid=(m // tm,), in_specs=in_specs, out_specs=[row(n)],
                            out_shape=[jax.ShapeDtypeStruct((m, n), F32)], args=list(pieces) + [b, extra], name=name)
    return (out, jouts) if jobs else out


def _relu2(v):
    r = jnp.maximum(v, 0.0)
    return r * r


ROW_TILE = 256


def _ln_stats(t):
    mu = jnp.mean(t, axis=-1, keepdims=True)
    xc = t - mu
    var = jnp.mean(xc * xc, axis=-1, keepdims=True)
    rstd = lax.rsqrt(var + LN_EPS)
    return xc * rstd, rstd


def _ln_bwd_rows(dy, xhat, rstd, g):
    dxh = dy * g
    m1 = jnp.mean(dxh, axis=-1, keepdims=True)
    m2 = jnp.mean(dxh * xhat, axis=-1, keepdims=True)
    return rstd * (dxh - m1 - xhat * m2)


def _ln_fwd(a, b, g, beta, *, name):
    s, d = a.shape
    row = pl.BlockSpec((ROW_TILE, d), lambda i: (i, 0))
    par = pl.BlockSpec((1, d), lambda i: (0, 0))

    def body(a_ref, b_ref, g_ref, be_ref, y_ref, yb_ref):
        xhat, _ = _ln_stats(ALPHA * a_ref[...] + b_ref[...])
        y = xhat * g_ref[...] + be_ref[...]
        y_ref[...] = y
        yb_ref[...] = y.astype(BF16)

    return _pcall(body, grid=(s // ROW_TILE,), in_specs=[row, row, par, par], out_specs=(row, row),
                  out_shape=(jax.ShapeDtypeStruct((s, d), F32), jax.ShapeDtypeStruct((s, d), BF16)), name=name,
                  compiler_params=_cparams(("parallel",)))(a, b, g, beta)


def _ln_bwd(a, b, g, dys, coefs, *, name):
    s, d = a.shape
    row = pl.BlockSpec((ROW_TILE, d), lambda i: (i, 0))
    par = pl.BlockSpec((1, d), lambda i: (0, 0))
    n = len(dys)

    def body(*refs):
        a_ref, b_ref, g_ref = refs[:3]
        dy_refs = refs[3:3 + n]
        dt_ref, dtb_ref, dg_ref, db_ref = refs[3 + n:]
        xhat, rstd = _ln_stats(ALPHA * a_ref[...] + b_ref[...])
        dy = coefs[0] * dy_refs[0][...]
        for q in range(1, n):
            dy = dy + coefs[q] * dy_refs[q][...]
        dt = _ln_bwd_rows(dy, xhat, rstd, g_ref[...])
        dt_ref[...] = dt
        dtb_ref[...] = dt.astype(BF16)

        @pl.when(pl.program_id(0) == 0)
        def _():
            dg_ref[...] = jnp.zeros_like(dg_ref)
            db_ref[...] = jnp.zeros_like(db_ref)

        dg_ref[...] += jnp.sum(dy * xhat, axis=0, keepdims=True)
        db_ref[...] += jnp.sum(dy, axis=0, keepdims=True)

    return _pcall(body, grid=(s // ROW_TILE,), in_specs=[row, row, par] + [row] * n, out_specs=(row, row, par, par),
                  out_shape=(jax.ShapeDtypeStruct((s, d), F32), jax.ShapeDtypeStruct((s, d), BF16),
                             jax.ShapeDtypeStruct((1, d), F32), jax.ShapeDtypeStruct((1, d), F32)),
                  name=name, compiler_params=_cparams(("arbitrary",)))(a, b, g, *dys)


def _head(x2, gpre, ple, g, beta, tgt, *, name):
    s, d = x2.shape
    row = pl.BlockSpec((ROW_TILE, d), lambda i: (i, 0))
    par = pl.BlockSpec((1, d), lambda i: (0, 0))
    lsp = pl.BlockSpec((1, LANE), lambda i: (0, 0))

    def body(x2_ref, gp_ref, ple_ref, g_ref, be_ref, t_ref, loss_ref, dgp_ref, dple_ref, dt_ref, dg_ref, db_ref):
        gate = _sigmoid(gp_ref[...])
        ple_v = ple_ref[...]
        xhat, rstd = _ln_stats(ALPHA * x2_ref[...] + gate * ple_v)
        err = xhat * g_ref[...] + be_ref[...] - t_ref[...]
        dy = err * (1.0 / d)
        dt = _ln_bwd_rows(dy, xhat, rstd, g_ref[...])
        dt_ref[...] = dt
        dgp_ref[...] = (dt * ple_v * gate * (1.0 - gate)).astype(BF16)
        dple_ref[...] = (dt * gate).astype(BF16)

        @pl.when(pl.program_id(0) == 0)
        def _():
            loss_ref[...] = jnp.zeros_like(loss_ref)
            dg_ref[...] = jnp.zeros_like(dg_ref)
            db_ref[...] = jnp.zeros_like(db_ref)

        loss_ref[...] += 0.5 * jnp.sum(jnp.mean(err * err, axis=-1, keepdims=True))
        dg_ref[...] += jnp.sum(dy * xhat, axis=0, keepdims=True)
        db_ref[...] += jnp.sum(dy, axis=0, keepdims=True)

    sd = jax.ShapeDtypeStruct((s, d), F32)
    sb = jax.ShapeDtypeStruct((s, d), BF16)
    pd = jax.ShapeDtypeStruct((1, d), F32)
    return _pcall(body, grid=(s // ROW_TILE,), in_specs=[row, row, row, par, par, row],
                  out_specs=(lsp, row, row, row, par, par),
                  out_shape=(jax.ShapeDtypeStruct((1, LANE), F32), sb, sb, sd, pd, pd),
                  name=name, compiler_params=_cparams(("arbitrary",)))(x2, gpre, ple, g, beta, tgt)


CONV_R = 256
PAD = SUBLANE


def _shift_down(ext, s):
    if s == 0:
        return ext[PAD:, :]
    return pltpu.roll(ext, s, 0)[PAD:, :]


def _shift_up(ext, s):
    r = ext.shape[0] - PAD
    if s == 0:
        return ext[:r, :]
    return pltpu.roll(ext, r + PAD - s, 0)[:r, :]


def _conv_rows(xpad_ref, r0, w_ref):
    ext = xpad_ref[pl.ds(r0, CONV_R + PAD), :]
    acc = _shift_down(ext, 0) * w_ref[3:4, :]
    for k in range(3):
        acc = acc + _shift_down(ext, 3 - k) * w_ref[k:k + 1, :]
    return acc, ext


def _fill_front_padded(dst_ref, src_ref, s):
    dst_ref[0:PAD, :] = jnp.zeros((PAD, dst_ref.shape[1]), F32)

    def cp(q, _):
        r0 = pl.multiple_of(q * CONV_R, CONV_R)
        dst_ref[pl.ds(pl.multiple_of(PAD + r0, PAD), CONV_R), :] = src_ref[pl.ds(r0, CONV_R), :]
        return 0

    lax.fori_loop(0, s // CONV_R, cp, 0)


def _conv_silu_fwd(proj, w8, b, *, col0, width, ct, name):
    s = proj.shape[0]
    nb = col0 // ct

    def body(x_ref, w_ref, b_ref, o_ref, xpad):
        _fill_front_padded(xpad, x_ref, s)

        def step(q, _):
            r0 = pl.multiple_of(q * CONV_R, CONV_R)
            acc, _e = _conv_rows(xpad, r0, w_ref)
            pre = acc + b_ref[...]
            o_ref[pl.ds(r0, CONV_R), :] = pre * _sigmoid(pre)
            return 0

        lax.fori_loop(0, s // CONV_R, step, 0)

    return _pcall(
        body, grid=(width // ct,),
        in_specs=[pl.BlockSpec((s, ct), lambda j: (0, nb + j)), pl.BlockSpec((SUBLANE, ct), lambda j: (0, j)),
                  pl.BlockSpec((1, ct), lambda j: (0, j))],
        out_specs=pl.BlockSpec((s, ct), lambda j: (0, j)),
        out_shape=jax.ShapeDtypeStruct((s, width), F32),
        scratch_shapes=[pltpu.VMEM((s + PAD, ct), F32)], name=name,
        compiler_params=_cparams(("parallel",)))(proj, w8, b)


def _conv_bwd_rows(dpad_ref, r0, w_ref):
    return _conv_bwd_ext(dpad_ref[pl.ds(r0, CONV_R + PAD), :], w_ref)


def _conv_bwd_ext(ext, w_ref):
    acc = _shift_up(ext, 0) * w_ref[3:4, :]
    for k in range(3):
        acc = acc + _shift_up(ext, 3 - k) * w_ref[k:k + 1, :]
    return acc


def _conv_silu_bwd(proj, dact, w8, b, *, col0, width, ct, name):
    s = proj.shape[0]
    nb = col0 // ct

    def body(x_ref, d_ref, w_ref, b_ref, dx_ref, dwb_ref, xpad, dpad):
        _fill_front_padded(xpad, x_ref, s)
        dpad[pl.ds(s, PAD), :] = jnp.zeros((PAD, ct), F32)
        dwb_ref[...] = jnp.zeros_like(dwb_ref)

        def step(q, _):
            r0 = pl.multiple_of(q * CONV_R, CONV_R)
            acc, ext = _conv_rows(xpad, r0, w_ref)
            pre = acc + b_ref[...]
            sg = _sigmoid(pre)
            dpre = d_ref[pl.ds(r0, CONV_R), :] * sg * (1.0 + pre * (1.0 - sg))
            dpad[pl.ds(r0, CONV_R), :] = dpre
            for k in range(4):
                dwb_ref[k:k + 1, :] += jnp.sum(dpre * _shift_down(ext, 3 - k), axis=0, keepdims=True)
            dwb_ref[4:5, :] += jnp.sum(dpre, axis=0, keepdims=True)
            return 0

        lax.fori_loop(0, s // CONV_R, step, 0)

        def step2(q, _):
            r0 = pl.multiple_of(q * CONV_R, CONV_R)
            dx_ref[pl.ds(r0, CONV_R), :] = _conv_bwd_rows(dpad, r0, w_ref).astype(BF16)
            return 0

        lax.fori_loop(0, s // CONV_R, step2, 0)

    colb = pl.BlockSpec((s, ct), lambda j: (0, j))
    return _pcall(
        body, grid=(width // ct,),
        in_specs=[pl.BlockSpec((s, ct), lambda j: (0, nb + j)), colb, pl.BlockSpec((SUBLANE, ct), lambda j: (0, j)),
                  pl.BlockSpec((1, ct), lambda j: (0, j))],
        out_specs=(colb, pl.BlockSpec((SUBLANE, ct), lambda j: (0, j))),
        out_shape=(jax.ShapeDtypeStruct((s, width), BF16), jax.ShapeDtypeStruct((SUBLANE, width), F32)),
        scratch_shapes=[pltpu.VMEM((s + PAD, ct), F32), pltpu.VMEM((s + PAD, ct), F32)], name=name,
        compiler_params=_cparams(("parallel",)))(proj, dact, w8, b)


LRU_CT = 128


def _row_of(v, r):
    return jnp.sum(jnp.where(_iota((v.shape[0], 1), 0) == r, v, 0.0), axis=0, keepdims=True)


def _scan_fwd(a, u):
    r = a.shape[0]
    row = _iota((r, 1), 0)
    d = 1
    while d < r:
        valid = row >= d
        u = jnp.where(valid, a * pltpu.roll(u, d, 0) + u, u)
        a = jnp.where(valid, a * pltpu.roll(a, d, 0), a)
        d *= 2
    return a, u


def _scan_rev(b, u):
    r = b.shape[0]
    row = _iota((r, 1), 0)
    d = 1
    while d < r:
        valid = row < r - d
        u = jnp.where(valid, b * pltpu.roll(u, r - d, 0) + u, u)
        b = jnp.where(valid, b * pltpu.roll(b, r - d, 0), b)
        d *= 2
    return b, u


def _lru_chunk(xpad, r0, cw_ref, cb, wa, ba, wx, bx, sp):
    acc, ext = _conv_rows(xpad, r0, cw_ref)
    xl = acc + cb
    r = _sigmoid(_dot(xl, wa) + ba)
    i = _sigmoid(_dot(xl, wx) + bx)
    la = -LRU_C * r * sp
    a = jnp.exp(la)
    a2 = jnp.exp(2.0 * la)
    mult = jnp.sqrt(-jnp.tanh(la) * (a2 + 1.0))
    first = (r0 + _iota((CONV_R, 1), 0)) == 0
    mult = jnp.where(first, 1.0, mult)
    return ext, xl, r, i, a, a2, mult, first


def _lru_specs(s):
    ct = LRU_CT
    nb_g = COL_G // ct
    return dict(
        x=pl.BlockSpec((s, ct), lambda j: (0, j)),
        g=pl.BlockSpec((s, ct), lambda j: (0, nb_g + j)),
        col=pl.BlockSpec((s, ct), lambda j: (0, j)),
        cw=pl.BlockSpec((SUBLANE, ct), lambda j: (0, j)),
        vec=pl.BlockSpec((1, ct), lambda j: (0, j)),
        gate=pl.BlockSpec((None, ct, ct), lambda j: (j, 0, 0)),
    )


def _lru_fwd(proj, cw8, cb, wa_bd, ba, wx_bd, bx, ap, *, name, jobs=()):
    s = proj.shape[0]
    ct = LRU_CT
    sp_ = _lru_specs(s)

    def body(x_ref, g_ref, cw_ref, cb_ref, wa_ref, ba_ref, wx_ref, bx_ref, ap_ref, y_ref, h_ref, xpad):
        _fill_front_padded(xpad, x_ref, s)
        sp = _softplus(-ap_ref[...])

        def step(q, carry):
            r0 = pl.multiple_of(q * CONV_R, CONV_R)
            _e, xl, _r, i, a, _a2, mult, _f = _lru_chunk(xpad, r0, cw_ref, cb_ref[...], wa_ref[...], ba_ref[...],
                                                       wx_ref[...], bx_ref[...], sp)
            acum, ucum = _scan_fwd(a, xl * i * mult)
            h = acum * carry + ucum
            h_ref[pl.ds(r0, CONV_R), :] = h
            ge, _th = _gelu(g_ref[pl.ds(r0, CONV_R), :])
            y_ref[pl.ds(r0, CONV_R), :] = (ge * h).astype(BF16)
            return _row_of(h, CONV_R - 1)

        lax.fori_loop(0, s // CONV_R, step, jnp.zeros((1, ct), F32))

    (ymix, hs), jouts = _hosted(
        body, jobs, grid=(LRU_W // ct,),
        in_specs=[sp_["x"], sp_["g"], sp_["cw"], sp_["vec"], sp_["gate"], sp_["vec"], sp_["gate"], sp_["vec"], sp_["vec"]],
        out_specs=(sp_["col"], sp_["col"]),
        out_shape=(jax.ShapeDtypeStruct((s, LRU_W + SSD_W), BF16), jax.ShapeDtypeStruct((s, LRU_W), F32)),
        scratch_shapes=[pltpu.VMEM((s + PAD, ct), F32)],
        name=name, args=(proj, proj, cw8, cb, wa_bd, ba, wx_bd, bx, ap))
    return ((ymix, hs), jouts) if jobs else (ymix, hs)


def _lru_bwd(proj, dy, hs, cw8, cb, wa_bd, ba, wx_bd, bx, ap, *, name, jobs=()):
    s = proj.shape[0]
    ct = LRU_CT
    sp_ = _lru_specs(s)

    nq = s // CONV_R

    def body(x_ref, g_ref, dy_ref, h_ref, cw_ref, cb_ref, wa_ref, ba_ref, wx_ref, bx_ref, ap_ref,
             dx_ref, dg_ref, dcwb_ref, dwa_ref, dwx_ref, xpad, hpad):
        _fill_front_padded(xpad, x_ref, s)
        _fill_front_padded(hpad, h_ref, s)
        apv = ap_ref[...]
        sp = _softplus(-apv)
        cb_v, wa, ba_v, wx, bx_v = cb_ref[...], wa_ref[...], ba_ref[...], wx_ref[...], bx_ref[...]
        dcwb_ref[...] = jnp.zeros_like(dcwb_ref)
        dwa_ref[...] = jnp.zeros_like(dwa_ref)
        dwx_ref[...] = jnp.zeros_like(dwx_ref)

        def back(k, carry):
            g_next, a_next, dxl_next = carry
            last_row = _iota((CONV_R, 1), 0) == CONV_R - 1
            r0 = pl.multiple_of((nq - 1 - k) * CONV_R, CONV_R)
            ext, xl, r, i, a, a2, mult, first = _lru_chunk(xpad, r0, cw_ref, cb_v, wa, ba_v, wx, bx_v, sp)
            gv = g_ref[pl.ds(r0, CONV_R), :]
            dyv = dy_ref[pl.ds(r0, CONV_R), :]
            hext = hpad[pl.ds(r0, CONV_R + PAD), :]
            ge, th = _gelu(gv)
            dg_ref[pl.ds(r0, CONV_R), :] = (dyv * _shift_down(hext, 0) * _gelu_grad(gv, th)).astype(BF16)
            b = jnp.where(last_row, a_next, pltpu.roll(a, CONV_R - 1, 0))
            bcum, dcum = _scan_rev(b, dyv * ge)
            gval = dcum + bcum * g_next
            hprev = _shift_down(hext, 1)
            da = gval * hprev
            dxl = gval * i * mult
            di = gval * xl * mult
            dmult = jnp.where(first, 0.0, gval * xl * i)
            dla = da * a - dmult * a2 / mult
            dr = dla * (-LRU_C) * sp
            dcwb_ref[7:8, :] += jnp.sum(dla * (-LRU_C) * r, axis=0, keepdims=True)
            dpr = dr * r * (1.0 - r)
            dpi = di * i * (1.0 - i)
            dxl = dxl + _dot_nt(dpr, wa) + _dot_nt(dpi, wx)
            dwa_ref[...] += _dot_tn(xl, dpr)
            dwx_ref[...] += _dot_tn(xl, dpi)
            dcwb_ref[5:6, :] += jnp.sum(dpr, axis=0, keepdims=True)
            dcwb_ref[6:7, :] += jnp.sum(dpi, axis=0, keepdims=True)
            for tap in range(4):
                dcwb_ref[tap:tap + 1, :] += jnp.sum(dxl * _shift_down(ext, 3 - tap), axis=0, keepdims=True)
            dcwb_ref[4:5, :] += jnp.sum(dxl, axis=0, keepdims=True)
            dx_ref[pl.ds(r0, CONV_R), :] = _conv_bwd_ext(jnp.concatenate([dxl, dxl_next], axis=0), cw_ref).astype(BF16)
            return _row_of(gval, 0), _row_of(a, 0), dxl[:PAD, :]

        zero = jnp.zeros((1, ct), F32)
        lax.fori_loop(0, nq, back, (zero, zero, jnp.zeros((PAD, ct), F32)))
        dcwb_ref[7:8, :] = dcwb_ref[7:8, :] * (-_sigmoid(-apv))

    nt = LRU_W // ct
    outs, jouts = _hosted(
        body, jobs, grid=(nt,),
        in_specs=[sp_["x"], sp_["g"], sp_["col"], sp_["col"], sp_["cw"], sp_["vec"], sp_["gate"], sp_["vec"], sp_["gate"],
                  sp_["vec"], sp_["vec"]],
        out_specs=(sp_["col"], sp_["col"], sp_["cw"], sp_["gate"], sp_["gate"]),
        out_shape=(jax.ShapeDtypeStruct((s, LRU_W), BF16), jax.ShapeDtypeStruct((s, LRU_W), BF16),
                   jax.ShapeDtypeStruct((SUBLANE, LRU_W), F32), jax.ShapeDtypeStruct((nt, ct, ct), F32),
                   jax.ShapeDtypeStruct((nt, ct, ct), F32)),
        scratch_shapes=[pltpu.VMEM((s + PAD, ct), F32), pltpu.VMEM((s + PAD, ct), F32)],
        name=name, args=(proj, proj, dy, hs, cw8, cb, wa_bd, ba, wx_bd, bx, ap))
    return (tuple(outs), jouts) if jobs else tuple(outs)


def _ssd_prep(dtr, bias, alog_pad, alogx):
    l = CHUNK
    lane = _iota((1, LANE), 1)
    a_head = jnp.where(lane < N_HEAD, -jnp.exp(alog_pad), 0.0)
    dt = _softplus(dtr + bias)
    tril = (_iota((l, l), 1) <= _iota((l, l), 0)).astype(F32)
    cs = _dotx(tril, dt * a_head)
    expand = (jnp.right_shift(_iota((LANE, SSD_W), 1), 6) == _iota((LANE, SSD_W), 0)).astype(F32)
    dtx = _dotx(dt, expand)
    ax = -jnp.exp(alogx)
    csx = _dotx(tril, dtx * ax)
    totx = jnp.sum(dtx * ax, axis=0, keepdims=True)
    return dict(a_head=a_head, dt=dt, tril=tril, cs=cs, expand=expand, dtx=dtx, ax=ax, csx=csx, totx=totx)


def _decay_mat(cs, cst_ref, h, causal):
    lane = _iota((CHUNK, LANE), 1)
    col = jnp.sum(jnp.where(lane == h, cs, 0.0), axis=1, keepdims=True)
    row = cst_ref[h:h + 1, :]
    return jnp.exp(jnp.where(causal, col - row, NEG_BIG))


def _head_mask(j):
    lane = _iota((CHUNK, GROUP_W), 1)
    return (lane >= j * HEAD_P) & (lane < (j + 1) * HEAD_P)


def _ssd_group_fwd(q, g, xs_g, bg, cg, ht_g, cst_ref, causal, dx_g):
    sl = slice(g * GROUP_W, (g + 1) * GROUP_W)
    dtx_g, csx_g, totx_g = q["dtx"][:, sl], q["csx"][:, sl], q["totx"][:, sl]
    xdt = xs_g * dtx_g
    ex = jnp.exp(csx_g)
    cb = _dot_nt(cg, bg)
    yoff = _dot(cg, ht_g) * ex
    ydiag = jnp.zeros((CHUNK, GROUP_W), F32)
    for j in range(4):
        sc = cb * _decay_mat(q["cs"], cst_ref, 4 * g + j, causal)
        ydiag = jnp.where(_head_mask(j), _dot(sc, xdt), ydiag)
    y = ydiag + yoff + xs_g * dx_g
    dsx = jnp.exp(totx_g - csx_g)
    return y, dict(xdt=xdt, ex=ex, cb=cb, yoff=yoff, dsx=dsx, dtx=dtx_g, totx=totx_g)


def _gated_norm_fwd(y_g, z_g, w_g):
    sz = _sigmoid(z_g)
    silu = z_g * sz
    yf = y_g * silu
    rs = lax.rsqrt(jnp.mean(yf * yf, axis=1, keepdims=True) + RMS_EPS)
    yn = yf * rs
    return yn * w_g, (sz, silu, rs, yn)


def _ssd_fwd(xact, proj, ymix, bias_pad, alog_pad, alogx, dxp, normw, *, name, jobs=()):
    s = xact.shape[0]
    nc = s // CHUNK

    def body(xa_ref, dt_ref, z_ref, _ymix_ref, bias_ref, alp_ref, alx_ref, dx_ref, nw_ref, y_ref, hp_ref, ht, cst):
        @pl.when(pl.program_id(0) == 0)
        def _():
            ht[...] = jnp.zeros_like(ht)

        hp_ref[...] = ht[...]
        q = _ssd_prep(dt_ref[...], bias_ref[...], alp_ref[...], alx_ref[...])
        cst[...] = q["cs"].T
        causal = q["tril"] > 0.0
        for g in range(N_GROUP):
            sl = slice(g * GROUP_W, (g + 1) * GROUP_W)
            xs_g = xa_ref[:, sl]
            bg = xa_ref[:, SSD_W + g * N_STATE:SSD_W + (g + 1) * N_STATE]
            cg = xa_ref[:, SSD_W + N_GROUP * N_STATE + g * N_STATE:SSD_W + N_GROUP * N_STATE + (g + 1) * N_STATE]
            ht_g = ht[:, sl]
            y, f = _ssd_group_fwd(q, g, xs_g, bg, cg, ht_g, cst, causal, dx_ref[:, sl])
            out, _ = _gated_norm_fwd(y, z_ref[:, sl], nw_ref[:, sl])
            y_ref[:, sl] = out.astype(BF16)
            ht[:, sl] = jnp.exp(f["totx"]) * ht_g + _dot_tn(bg, f["xdt"] * f["dsx"])

    par = lambda w: pl.BlockSpec((1, w), lambda c: (0, 0))
    (ycat, hprev), jouts = _hosted(
        body, jobs, grid=(nc,),
        in_specs=[pl.BlockSpec((CHUNK, XBC), lambda c: (c, 0)),
                  pl.BlockSpec((CHUNK, LANE), lambda c: (c, COL_DT // LANE)),
                  pl.BlockSpec((CHUNK, SSD_W), lambda c: (c, COL_Z // SSD_W)),
                  ANY_SPEC, par(LANE), par(LANE), par(SSD_W), par(SSD_W), par(SSD_W)],
        out_specs=(pl.BlockSpec((CHUNK, SSD_W), lambda c: (c, LRU_W // SSD_W)),
                   pl.BlockSpec((None, N_STATE, SSD_W), lambda c: (c, 0, 0))),
        out_shape=(jax.ShapeDtypeStruct(ymix.shape, ymix.dtype), jax.ShapeDtypeStruct((nc, N_STATE, SSD_W), F32)),
        scratch_shapes=[pltpu.VMEM((N_STATE, SSD_W), F32), pltpu.VMEM((CHUNK, LANE), F32)],
        aliases={3: 0}, name=name, args=(xact, proj, proj, ymix, bias_pad, alog_pad, alogx, dxp, normw))
    return ((ycat, hprev), jouts) if jobs else (ycat, hprev)


def _ssd_bwd(xact, proj, dycat, hprev, bias_pad, alog_pad, alogx, dxp, normw, *, name, jobs=()):
    s = xact.shape[0]
    nc = s // CHUNK
    l = CHUNK

    def body(xa_ref, dt_ref, z_ref, dy_ref, hp_ref, bias_ref, alp_ref, alx_ref, dx_ref, nw_ref,
             dxa_ref, ddt_ref, dz_ref, dnw_ref, small_ref, dht, cst, accx, dcsx_s, ddtx_s):
        step = pl.program_id(0)

        @pl.when(step == 0)
        def _():
            dht[...] = jnp.zeros_like(dht)
            accx[...] = jnp.zeros_like(accx)
            dnw_ref[...] = jnp.zeros_like(dnw_ref)
            small_ref[...] = jnp.zeros_like(small_ref)

        dtr = dt_ref[...]
        q = _ssd_prep(dtr, bias_ref[...], alp_ref[...], alx_ref[...])
        cst[...] = q["cs"].T
        causal = q["tril"] > 0.0
        eye = _iota((l, l), 0) == _iota((l, l), 1)
        lane = _iota((l, LANE), 1)
        dcs_head = jnp.zeros((l, LANE), F32)
        for g in range(N_GROUP):
            sl = slice(g * GROUP_W, (g + 1) * GROUP_W)
            slb = slice(SSD_W + g * N_STATE, SSD_W + (g + 1) * N_STATE)
            slc = slice(SSD_W + N_GROUP * N_STATE + g * N_STATE, SSD_W + N_GROUP * N_STATE + (g + 1) * N_STATE)
            xs_g, bg, cg = xa_ref[:, sl], xa_ref[:, slb], xa_ref[:, slc]
            ht_g = hp_ref[:, sl]
            dxp_g = dx_ref[:, sl]
            y, f = _ssd_group_fwd(q, g, xs_g, bg, cg, ht_g, cst, causal, dxp_g)
            z_g, nw_g = z_ref[:, sl], nw_ref[:, sl]
            _o, (sz, silu, rs, yn) = _gated_norm_fwd(y, z_g, nw_g)
            dout = dy_ref[:, sl]
            dnw_ref[:, sl] += jnp.sum(dout * yn, axis=0, keepdims=True)
            dyn = dout * nw_g
            dyf = rs * (dyn - yn * jnp.mean(dyn * yn, axis=1, keepdims=True))
            dy = dyf * silu
            dz_ref[:, sl] = (dyf * y * sz * (1.0 + z_g * (1.0 - sz))).astype(BF16)
            accx[0:1, sl] += jnp.sum(dy * xs_g, axis=0, keepdims=True)
            dyo = dy * f["ex"]
            dcg = _dot_nt(dyo, ht_g)
            dht_prev = _dot_tn(cg, dyo)
            dcsx = dy * f["yoff"]
            xdt = f["xdt"]
            dxdt = jnp.zeros((l, GROUP_W), F32)
            dcb = jnp.zeros((l, l), F32)
            for j in range(4):
                h = 4 * g + j
                lm = _decay_mat(q["cs"], cst, h, causal)
                sc = f["cb"] * lm
                mask = _head_mask(j)
                ds_ = jnp.where(causal, _dot_nt(jnp.where(mask, dy, 0.0), xdt), 0.0)
                dxdt = jnp.where(mask, _dot_tn(sc, dy), dxdt)
                dcb = dcb + ds_ * lm
                m = ds_ * sc
                rsum = jnp.sum(m, axis=1, keepdims=True)
                csum = jnp.sum(m, axis=0, keepdims=True)
                csum_col = jnp.sum(jnp.where(eye, csum, 0.0), axis=1, keepdims=True)
                dcs_head = dcs_head + jnp.where(lane == h, rsum - csum_col, 0.0)
            dhn = dht[:, sl]
            etot = jnp.exp(f["totx"])
            dxd = _dot(bg, dhn)
            dbg = _dot_nt(xdt * f["dsx"], dhn)
            dxdt = dxdt + dxd * f["dsx"]
            qq = dxd * xdt * f["dsx"]
            dcsx = dcsx - qq
            dtot = jnp.sum(qq, axis=0, keepdims=True) + jnp.sum(dhn * ht_g, axis=0, keepdims=True) * etot
            dht[:, sl] = etot * dhn + dht_prev
            dcg = dcg + _dot(dcb, bg)
            dbg = dbg + _dot_tn(dcb, cg)
            dxa_ref[:, sl] = dxdt * f["dtx"] + dy * dxp_g
            dxa_ref[:, slb] = dbg
            dxa_ref[:, slc] = dcg
            dcsx_s[:, sl] = dcsx
            ddtx_s[:, sl] = dxdt * xs_g
            accx[2:3, sl] = dtot
        triu = (_iota((l, l), 1) >= _iota((l, l), 0)).astype(F32)
        dax = _dotx(triu, dcsx_s[...]) + accx[2:3, :]
        accx[1:2, :] += jnp.sum(dax * q["dtx"], axis=0, keepdims=True)
        reduce = (jnp.right_shift(_iota((SSD_W, LANE), 0), 6) == _iota((SSD_W, LANE), 1)).astype(F32)
        ddt = _dotx(ddtx_s[...] + dax * q["ax"], reduce)
        da_head = _dotx(triu, dcs_head)
        ddt = ddt + da_head * q["a_head"]
        small_ref[1:2, :] += jnp.sum(da_head * q["dt"], axis=0, keepdims=True)
        ddtr = ddt * _sigmoid(dtr + bias_ref[...])
        ddt_ref[...] = ddtr.astype(BF16)
        small_ref[0:1, :] += jnp.sum(ddtr, axis=0, keepdims=True)

        @pl.when(step == nc - 1)
        def _():
            red = _dotx(accx[...], reduce)
            d_a = small_ref[1:2, :] + red[1:2, :]
            small_ref[1:2, :] = d_a * q["a_head"]
            small_ref[2:3, :] = red[0:1, :]

    rev = lambda c: nc - 1 - c
    par = lambda w: pl.BlockSpec((1, w), lambda c: (0, 0))
    outs, jouts = _hosted(
        body, jobs, grid=(nc,),
        in_specs=[pl.BlockSpec((CHUNK, XBC), lambda c: (rev(c), 0)),
                  pl.BlockSpec((CHUNK, LANE), lambda c: (rev(c), COL_DT // LANE)),
                  pl.BlockSpec((CHUNK, SSD_W), lambda c: (rev(c), COL_Z // SSD_W)),
                  pl.BlockSpec((CHUNK, SSD_W), lambda c: (rev(c), 1)),
                  pl.BlockSpec((None, N_STATE, SSD_W), lambda c: (rev(c), 0, 0)),
                  par(LANE), par(LANE), par(SSD_W), par(SSD_W), par(SSD_W)],
        out_specs=(pl.BlockSpec((CHUNK, XBC), lambda c: (rev(c), 0)),
                   pl.BlockSpec((CHUNK, LANE), lambda c: (rev(c), 0)),
                   pl.BlockSpec((CHUNK, SSD_W), lambda c: (rev(c), 0)),
                   par(SSD_W), pl.BlockSpec((SUBLANE, LANE), lambda c: (0, 0))),
        out_shape=(jax.ShapeDtypeStruct((s, XBC), F32), jax.ShapeDtypeStruct((s, LANE), BF16),
                   jax.ShapeDtypeStruct((s, SSD_W), BF16), jax.ShapeDtypeStruct((1, SSD_W), F32),
                   jax.ShapeDtypeStruct((SUBLANE, LANE), F32)),
        scratch_shapes=[pltpu.VMEM((N_STATE, SSD_W), F32), pltpu.VMEM((CHUNK, LANE), F32),
                        pltpu.VMEM((SUBLANE, SSD_W), F32), pltpu.VMEM((CHUNK, SSD_W), F32),
                        pltpu.VMEM((CHUNK, SSD_W), F32)],
        name=name, args=(xact, proj, proj, dycat, hprev, bias_pad, alog_pad, alogx, dxp, normw))
    return (tuple(outs), jouts) if jobs else tuple(outs)


def _blockdiag(w):
    w2 = w.reshape(N_HEAD // 2, 2, HEAD_P, HEAD_P)
    z = jnp.zeros((N_HEAD // 2, HEAD_P, HEAD_P), w.dtype)
    top = jnp.concatenate([w2[:, 0], z], axis=2)
    bot = jnp.concatenate([z, w2[:, 1]], axis=2)
    return jnp.concatenate([top, bot], axis=1)


def _unblockdiag(wbd):
    a = wbd[:, :HEAD_P, :HEAD_P]
    b = wbd[:, HEAD_P:, HEAD_P:]
    return jnp.stack([a, b], axis=1).reshape(N_HEAD, HEAD_P, HEAD_P)


def _pad_rows8(w):
    return jnp.concatenate([w, jnp.zeros((SUBLANE - w.shape[0], w.shape[1]), w.dtype)], axis=0)


def _pad_lane(v):
    return jnp.concatenate([v, jnp.zeros((1, LANE - v.shape[1]), v.dtype)], axis=1)


class _NoExchange:
    def ride(self, host):
        return []

    def done(self, jobs, outs, w):
        pass

    def grad(self, name, val):
        pass

    def small(self, raw):
        pass


def _local_step(x, p, tgt, w, hooks=_NoExchange()):
    cw_l = _pad_rows8(w["lru_conv_w"])
    cw_s = _pad_rows8(w["ssd_conv_w"])
    wa_bd = _blockdiag(w["lru_gate_a_w"])
    wx_bd = _blockdiag(w["lru_gate_x_w"])
    ba = w["lru_gate_a_b"].reshape(1, LRU_W)
    bx = w["lru_gate_x_b"].reshape(1, LRU_W)
    bias_pad = _pad_lane(w["ssd_dt_bias"])
    alog_pad = _pad_lane(w["ssd_a_log"])
    alogx = jnp.repeat(w["ssd_a_log"], HEAD_P, axis=1)
    dxp = jnp.repeat(w["ssd_d"], HEAD_P, axis=1)

    def host(fn, *a, name, **k):
        jobs = hooks.ride(name)
        res = fn(*a, name=name, jobs=jobs, **k)
        if jobs:
            res, jouts = res
            hooks.done(jobs, jouts, w)
        return res

    def grad(n, val):
        g[n] = val
        hooks.grad(n, val)

    xb = x.astype(BF16)
    proj = host(_mm, xb, w["w_in_t"], "nt", tm=1024, tn=512, name="in_proj")
    ymix, h_lru = host(_lru_fwd, proj, cw_l, w["lru_conv_b"], wa_bd, ba, wx_bd, bx, w["lru_a_param"], name="lru_fwd")
    xact = _conv_silu_fwd(proj, cw_s, w["ssd_conv_b"], col0=COL_XBC, width=XBC, ct=256, name="ssd_conv_fwd")
    ycat, hprev = host(_ssd_fwd, xact, proj, ymix, bias_pad, alog_pad, alogx, dxp, w["ssd_norm_w"], name="ssd_fwd")
    mix = _mm(ycat, w["w_out"], "nn", tm=1024, tn=1024, name="out_proj")
    x1, x1b = _ln_fwd(x, mix, w["ln1_g"], w["ln1_b"], name="ln1_fwd")
    pre = _mm(x1b, w["w_ff1"], "nn", tm=1024, tn=512, out_dtype=BF16, name="ff1")
    ff = _mm(pre, w["w_ff2"], "nn", tm=512, tn=1024, a_fn=_relu2, name="ff2")
    x2, x2b = _ln_fwd(x1, ff, w["ln2_g"], w["ln2_b"], name="ln2_fwd")
    gpre = _mm(x2b, w["w_ple_gate"], "nn", tm=1024, tn=1024, name="ple_gate")
    ple = _mm(p, w["w_ple"], "nn", tm=1024, tn=1024, name="ple_proj")
    loss, dgpre, dple, dt3, dg3, db3 = _head(x2, gpre, ple, w["ln3_g"], w["ln3_b"], tgt, name="head")

    g = {}
    g["ln3_g"], g["ln3_b"] = dg3, db3
    grad("w_ple_gate", _mm(x2b, dgpre, "tn", tm=512, tn=1024, out_dtype=BF16, name="d_w_ple_gate"))
    grad("w_ple", _mm(p, dple, "tn", tm=256, tn=512, dest_major=True, out_dtype=BF16, name="d_w_ple"))
    dx2_mm = host(_mm, dgpre, w["w_ple_gate"], "nt", tm=1024, tn=1024, name="d_x2")
    dt2, dt2b, g["ln2_g"], g["ln2_b"] = _ln_bwd(x1, ff, w["ln2_g"], [dt3, dx2_mm], [ALPHA, 1.0], name="ln2_bwd")
    grad("w_ff2", host(_mm, pre, dt2b, "tn", tm=512, tn=1024, a_fn=_relu2, out_dtype=BF16, name="d_w_ff2"))
    dpre = host(_mm, dt2b, w["w_ff2"], "nt", tm=1024, tn=512, extra=pre, out_dtype=BF16,
                epi=lambda acc, pv: acc * 2.0 * jnp.maximum(pv.astype(F32), 0.0), name="d_pre")
    grad("w_ff1", host(_mm, x1b, dpre, "tn", tm=1024, tn=512, dest_major=True, out_dtype=BF16, name="d_w_ff1"))
    dx1_mm = host(_mm, dpre, w["w_ff1"], "nt", tm=512, tn=1024, name="d_x1")
    dt1, dt1b, g["ln1_g"], g["ln1_b"] = _ln_bwd(x, mix, w["ln1_g"], [dt2, dx1_mm], [ALPHA, 1.0], name="ln1_bwd")
    grad("w_out", host(_mm, ycat, dt1b, "tn", tm=512, tn=1024, out_dtype=BF16, name="d_w_out"))
    dycat = host(_mm, dt1b, w["w_out"], "nt", tm=1024, tn=1024, name="d_ycat")
    dxl, dgl, dcwb_l, dwa, dwx = host(_lru_bwd, proj, dycat, h_lru, cw_l, w["lru_conv_b"], wa_bd, ba, wx_bd, bx,
                                      w["lru_a_param"], name="lru_bwd")
    g["lru_gate_a_w"] = _unblockdiag(dwa)
    g["lru_gate_x_w"] = _unblockdiag(dwx)
    raw = dict(lru=dcwb_l, gate_a=g["lru_gate_a_w"].reshape(N_HEAD * HEAD_P, HEAD_P),
               gate_x=g["lru_gate_x_w"].reshape(N_HEAD * HEAD_P, HEAD_P))
    hooks.small(raw)
    dxact, ddt, dz, g["ssd_norm_w"], small = host(_ssd_bwd, xact, proj, dycat, hprev, bias_pad, alog_pad, alogx, dxp,
                                                   w["ssd_norm_w"], name="ssd_bwd")
    dxbc, dcwb_s = _conv_silu_bwd(proj, dxact, cw_s, w["ssd_conv_b"], col0=COL_XBC, width=XBC, ct=256,
                                  name="ssd_conv_bwd")
    pieces, offsets = [dxl, dgl, dz, dxbc, ddt], [0, COL_G, COL_Z, COL_XBC, COL_DT]

    g["lru_conv_w"] = dcwb_l[0:4]
    g["lru_conv_b"] = dcwb_l[4:5]
    g["lru_gate_a_b"] = dcwb_l[5:6]
    g["lru_gate_x_b"] = dcwb_l[6:7]
    g["lru_a_param"] = dcwb_l[7:8]
    g["ssd_conv_w"] = dcwb_s[0:4]
    g["ssd_conv_b"] = dcwb_s[4:5]
    g["ssd_dt_bias"] = small[0:1, :N_HEAD]
    g["ssd_a_log"] = small[1:2, :N_HEAD]
    g["ssd_d"] = small[2:3, :N_HEAD]
    rows = jnp.concatenate([g[n] for n in ("ssd_norm_w", "ln1_g", "ln1_b", "ln2_g", "ln2_b", "ln3_g", "ln3_b")]
                           + [jnp.broadcast_to(loss[:, 0:1], (1, D_MODEL))], axis=0)
    late = dict(ssd=dcwb_s, heads=small, rows=rows)
    hooks.small(late)
    raw.update(late)
    dwt = [host(_mm, pc, xb, "tn", tm=512, tn=1024, out_dtype=BF16, name="d_w_in_%d" % q)
           for q, pc in enumerate(pieces)]
    grad("w_in", jnp.concatenate(dwt, axis=0))
    grad_x = host(_mm_pieces, pieces, offsets, w["w_in_t"], tm=256, extra=dt1, epi=lambda acc, e: acc + ALPHA * e,
                  name="d_x")
    return loss[0, 0], grad_x, g, raw


ANY_SPEC = pl.BlockSpec(memory_space=pl.ANY)


def _mesh_pos():
    return lax.axis_index("x"), lax.axis_index("y"), lax.axis_index("c")


def _remote(src, dst, send, recv, k, to):
    return pltpu.make_async_remote_copy(src_ref=src, dst_ref=dst, send_sem=send.at[k], recv_sem=recv.at[k],
                                        device_id=to, device_id_type=MESH_T)


class _Job:
    N_SEM = 7

    def __init__(self, kind, inp):
        self.kind, self.inp = kind, inp
        shape = {"gather": (N_DEV,) + inp.shape, "pair": (4,) + inp.shape[1:], "chip": inp.shape}[kind]
        self.out = jax.ShapeDtypeStruct(shape, inp.dtype)

    def _places(self):
        x, y, c = _mesh_pos()
        return (x, y, c), (x, y, 1 - c), [(1 - x, y), (x, 1 - y), (1 - x, 1 - y)]

    def start(self, inp, out, send, recv, loc):
        me, sibling, chips = self._places()
        x, y, c = me
        if self.kind == "gather":
            mine = out.at[4 * x + 2 * y + c]
            pltpu.make_async_copy(inp, mine, loc.at[0]).start()
            _remote(inp, mine, send, recv, 0, sibling).start()
            for j, chip in enumerate(chips):
                _remote(inp, mine, send, recv, 1 + j, (*chip, c)).start()
        elif self.kind == "pair":
            for k in range(4):
                _remote(inp.at[2 * k + (1 - c)], out.at[k], send, recv, k, sibling).start()
        else:
            kme = 2 * x + y
            pltpu.make_async_copy(inp.at[kme], out.at[kme], loc.at[0]).start()
            for j, (tx, ty) in enumerate(chips):
                _remote(inp.at[2 * tx + ty], out.at[kme], send, recv, j, (tx, ty, c)).start()

    def finish(self, inp, out, send, recv, loc):
        me, sibling, chips = self._places()
        x, y, c = me
        if self.kind == "gather":
            blk = lambda px, py, pc: out.at[4 * px + 2 * py + pc]
            mine = blk(*me)
            for j, chip in enumerate(chips):
                landed = blk(*chip, c)
                _remote(landed, landed, send, recv, 1 + j, me).wait_recv()
                _remote(landed, landed, send, recv, 4 + j, sibling).start()
            _remote(inp, blk(*sibling), send, recv, 0, me).wait_recv()
            for j, chip in enumerate(chips):
                _remote(inp, blk(*chip, 1 - c), send, recv, 4 + j, me).wait_recv()
            for k in range(7):
                _remote(inp, mine, send, recv, k, sibling).wait_send()
            pltpu.make_async_copy(inp, mine, loc.at[0]).wait()
        elif self.kind == "pair":
            for k in range(4):
                _remote(inp.at[2 * k + (1 - c)], out.at[k], send, recv, k, sibling).wait()
        else:
            kme = 2 * x + y
            for j, (tx, ty) in enumerate(chips):
                _remote(inp.at[kme], out.at[2 * tx + ty], send, recv, j, (tx, ty, c)).wait_recv()
            for j, (tx, ty) in enumerate(chips):
                _remote(inp.at[2 * tx + ty], out.at[kme], send, recv, j, (tx, ty, c)).wait_send()
            pltpu.make_async_copy(inp.at[kme], out.at[kme], loc.at[0]).wait()


def _job_scratch(jobs):
    sem = pltpu.SemaphoreType.DMA
    return [s for _ in jobs for s in (sem((_Job.N_SEM,)), sem((_Job.N_SEM,)), sem((1,)))]


def _run_jobs(jobs, method, jins, jouts, jsems):
    for q, job in enumerate(jobs):
        getattr(job, method)(jins[q], jouts[q], *jsems[3 * q:3 * q + 3])


def _exchange(jobs, *, name):
    n = len(jobs)

    def body(*refs):
        jins, jouts, jsems = refs[:n], refs[n:2 * n], refs[2 * n:]
        _run_jobs(jobs, "start", jins, jouts, jsems)
        _run_jobs(jobs, "finish", jins, jouts, jsems)

    return _pcall(body, in_specs=[ANY_SPEC] * n, out_specs=[ANY_SPEC] * n, out_shape=[j.out for j in jobs],
                  scratch_shapes=_job_scratch(jobs), name=name)(*[j.inp for j in jobs])


def _hosted(body, jobs, *, grid, in_specs, out_specs, out_shape, args, name, scratch_shapes=(), aliases=None):
    in_specs, out_specs, out_shape = list(in_specs), list(out_specs), list(out_shape)
    scratch_shapes = list(scratch_shapes)
    n_in, n_out, n_scr, nj = len(in_specs), len(out_specs), len(scratch_shapes), len(jobs)
    sem = ("arbitrary",) * len(grid)
    kw = dict(input_output_aliases=aliases) if aliases else {}
    if not jobs:
        res = _pcall(body, grid=grid, in_specs=in_specs, out_specs=out_specs, out_shape=out_shape,
                     scratch_shapes=scratch_shapes, name=name, compiler_params=_cparams(sem), **kw)(*args)
        return list(res), []

    def full(*refs):
        ins, jins = refs[:n_in], refs[n_in:n_in + nj]
        o0 = n_in + nj
        outs, jouts = refs[o0:o0 + n_out], refs[o0 + n_out:o0 + n_out + nj]
        s0 = o0 + n_out + nj
        scr, jsems = refs[s0:s0 + n_scr], refs[s0 + n_scr:]
        first = pl.program_id(0) == 0
        last = pl.program_id(0) == grid[0] - 1
        for ax in range(1, len(grid)):
            first = jnp.logical_and(first, pl.program_id(ax) == 0)
            last = jnp.logical_and(last, pl.program_id(ax) == grid[ax] - 1)

        @pl.when(first)
        def _():
            _run_jobs(jobs, "start", jins, jouts, jsems)

        body(*ins, *outs, *scr)

        @pl.when(last)
        def _():
            _run_jobs(jobs, "finish", jins, jouts, jsems)

    res = _pcall(full, grid=grid, in_specs=in_specs + [ANY_SPEC] * nj, out_specs=out_specs + [ANY_SPEC] * nj,
                 out_shape=out_shape + [j.out for j in jobs], scratch_shapes=scratch_shapes + _job_scratch(jobs),
                 name=name, compiler_params=_cparams(sem), **kw)(*args, *[j.inp for j in jobs])
    return list(res[:n_out]), list(res[n_out:])


def _pair_add(g8, r4, cidx, *, name):
    _, r, c = g8.shape
    tr = ROW_TILE if r % ROW_TILE == 0 else r

    def body(c_ref, g_ref, r_ref, o_ref):
        o_ref[...] = (g_ref[...].astype(F32) + r_ref[...].astype(F32)).astype(BF16)

    return _pcall(
        body,
        grid_spec=pltpu.PrefetchScalarGridSpec(
            num_scalar_prefetch=1, grid=(4, r // tr),
            in_specs=[pl.BlockSpec((None, tr, c), lambda k, i, cr: (2 * k + cr[0], i, 0)),
                      pl.BlockSpec((None, tr, c), lambda k, i, cr: (k, i, 0))],
            out_specs=pl.BlockSpec((None, tr, c), lambda k, i, cr: (k, i, 0))),
        out_shape=jax.ShapeDtypeStruct((4, r, c), BF16), name=name,
        compiler_params=_cparams(("parallel", "parallel")))(cidx, g8, r4)


def _adam_update(g, w_ref, m_ref, v_ref, g_ref, d_ref, mo_ref, vo_ref):
    c1 = 1.0 - ADAM_B1 ** ADAM_STEP
    c2 = 1.0 - ADAM_B2 ** ADAM_STEP
    m2 = ADAM_B1 * m_ref[...] + (1.0 - ADAM_B1) * g
    v2 = ADAM_B2 * v_ref[...] + (1.0 - ADAM_B2) * (g * g)
    g_ref[...] = g
    mo_ref[...] = m2
    vo_ref[...] = v2
    d_ref[...] = -ADAM_LR * ((m2 / c1) / (jnp.sqrt(v2 / c2) + ADAM_EPS) + ADAM_WD * w_ref[...])


def _adamw_rows(srcs, items, own_cols, me1, *, name):
    ns, ni, no = len(srcs), len(items), len(own_cols)
    full = lambda a: pl.BlockSpec(a.shape, lambda i, me: (0,) * a.ndim)
    in_specs = [full(a) for a in srcs]
    args = list(srcs)
    for (si, _r0, w, _m, _v) in own_cols:
        a = srcs[si]
        in_specs.append(pl.BlockSpec((N_DEV, a.shape[1], w.shape[1]), lambda i, me: (0, 0, me[0])))
        args.append(a)
    out_specs, out_shape = [], []
    for (_si, _r0, w, m, v) in list(items) + list(own_cols):
        in_specs += [full(w)] * 3
        args += [w, m, v]
        out_specs += [full(w)] * 4
        out_shape += [jax.ShapeDtypeStruct(w.shape, F32)] * 4

    def body(me_ref, *refs):
        src_refs, own_refs = refs[:ns], refs[ns:ns + no]
        wmv = refs[ns + no:ns + no + 3 * (ni + no)]
        outs = refs[ns + no + 3 * (ni + no):]
        for q, (si, r0, w, _m, _v) in enumerate(list(items) + list(own_cols)):
            nr, cw = w.shape
            gref = src_refs[si] if q < ni else own_refs[q - ni]
            g = gref[0, r0:r0 + nr, 0:cw]
            for d in range(1, N_DEV):
                g = g + gref[d, r0:r0 + nr, 0:cw]
            _adam_update(g, *wmv[3 * q:3 * q + 3], *outs[4 * q:4 * q + 4])

    res = _pcall(
        body,
        grid_spec=pltpu.PrefetchScalarGridSpec(num_scalar_prefetch=1, grid=(1,), in_specs=in_specs, out_specs=out_specs),
        out_shape=out_shape, name=name, compiler_params=_cparams(("arbitrary",)))(me1, *args)
    return [tuple(res[4 * q:4 * q + 4]) for q in range(ni + no)]


def _adamw(gsrc, w, m, v, *, name):
    k, r, c = gsrc.shape
    tr = ROW_TILE if r % ROW_TILE == 0 else r

    def body(gs_ref, w_ref, m_ref, v_ref, g_ref, d_ref, mo_ref, vo_ref):
        g = gs_ref[0].astype(F32)
        for q in range(1, k):
            g = g + gs_ref[q].astype(F32)
        _adam_update(g, w_ref, m_ref, v_ref, g_ref, d_ref, mo_ref, vo_ref)

    tc = c
    if tr == r and r > ROW_TILE and c % 256 == 0:
        tc = 256
    blk = pl.BlockSpec((tr, tc), lambda i, j: (i, j))
    sd = jax.ShapeDtypeStruct((r, c), F32)
    return _pcall(body, grid=(r // tr, c // tc),
                  in_specs=[pl.BlockSpec((k, tr, tc), lambda i, j: (0, i, j)), blk, blk, blk],
                  out_specs=(blk, blk, blk, blk), out_shape=(sd, sd, sd, sd), name=name,
                  compiler_params=_cparams(("parallel", "parallel")))(gsrc, w, m, v)


WEIGHTS = ['w_in', 'lru_conv_w', 'lru_conv_b', 'lru_gate_a_w', 'lru_gate_a_b', 'lru_gate_x_w', 'lru_gate_x_b',
           'lru_a_param', 'ssd_conv_w', 'ssd_conv_b', 'ssd_dt_bias', 'ssd_a_log', 'ssd_d', 'ssd_norm_w', 'w_out',
           'ln1_g', 'ln1_b', 'w_ff1', 'w_ff2', 'ln2_g', 'ln2_b', 'w_ple_gate', 'w_ple', 'ln3_g', 'ln3_b']
BIG = ['w_in', 'w_out', 'w_ff1', 'w_ff2', 'w_ple_gate', 'w_ple']
COL_SHARDED = ('w_ff1', 'w_ple')
CONV = ['lru_conv_w', 'ssd_conv_w']
REPL = [n for n in WEIGHTS if n not in BIG and n not in CONV]
CONV_CH = {'lru_conv_w': LRU_W, 'ssd_conv_w': XBC}


def _to_dest_major(name, gfull):
    if name == 'w_in':
        gfull = gfull[:D_IN]
    if name in COL_SHARDED:
        r, cfull = gfull.shape
        return gfull.reshape(r, N_DEV, cfull // N_DEV).transpose(1, 0, 2)
    rfull, cdim = gfull.shape
    return gfull.reshape(N_DEV, rfull // N_DEV, cdim)


def _full_weight(name, gathered):
    if name in COL_SHARDED:
        _, r, cs = gathered.shape
        full = gathered.transpose(1, 0, 2).reshape(r, N_DEV * cs)
    else:
        _, rs, cdim = gathered.shape
        full = gathered.reshape(N_DEV * rs, cdim)
    if name == 'w_in':
        full = jnp.concatenate([full, jnp.zeros((D_IN_PAD - D_IN, D_MODEL), full.dtype)], axis=0)
    return full


SMALL_SRC = ("lru", "ssd", "heads", "rows", "gate_a", "gate_x")
AG_HOSTS = {"in_proj": ("w_ff1",), "lru_fwd": ("w_out", "w_ple_gate", "w_ple"), "ssd_fwd": ("w_ff2",)}
PAIR_HOSTS = ("d_x2", "d_pre", "d_x1", "d_ycat", "d_x")
CHIP_HOSTS = {"lru_bwd": ("w_ple_gate", "w_ple", "w_ff2"), "ssd_bwd": ("w_ff1", "w_out")}
SMALL_HOSTS = {"ssd_bwd": ("lru", "gate_a", "gate_x"), "d_w_in_3": ("ssd", "heads", "rows")}


class _Schedule:
    def __init__(self, shards, cidx):
        self.shards, self.cidx = shards, cidx
        self.pair, self.chip, self.small_jobs = [], [], []
        self.dest, self.summed, self.gathered_small = {}, {}, {}
        self.tags = []

    def ride(self, host):
        tags = []
        if host in AG_HOSTS:
            tags = [("weight", n, self.shards[n]) for n in AG_HOSTS[host]]
        elif host in PAIR_HOSTS or host in CHIP_HOSTS or host == "flush":
            tags = [("pair", n, a) for n, a in self.pair]
            self.pair = []
            if host not in PAIR_HOSTS:
                take = [t for t in self.chip if host == "flush" or t[0] in CHIP_HOSTS[host]]
                tags += [("chip", n, a) for n, a in take]
                self.chip = [t for t in self.chip if not any(t is u for u in take)]
        if host in SMALL_HOSTS:
            tags += [("small", n, a) for n, a in self.small_jobs if n in SMALL_HOSTS[host]]
            self.small_jobs = [t for t in self.small_jobs if t[0] not in SMALL_HOSTS[host]]
        self.tags = tags
        return [_Job({"weight": "gather", "small": "gather"}.get(kind, kind), a) for kind, _n, a in tags]

    def done(self, jobs, outs, w):
        for (kind, n, _a), o in zip(self.tags, outs):
            if kind == "weight":
                w[n] = _full_weight(n, o)
            elif kind == "small":
                self.gathered_small[n] = o
            elif kind == "pair":
                self.chip.append((n, _pair_add(self.dest[n], o, self.cidx, name="rs_pair_add_" + n)))
            else:
                self.summed[n] = o

    def grad(self, name, val):
        self.dest[name] = val if val.ndim == 3 else _to_dest_major(name, val)
        self.pair.append((name, self.dest[name]))

    def small(self, raw):
        self.small_jobs += list(raw.items())

    def flush(self):
        step = 0
        while self.pair or self.chip:
            jobs = self.ride("flush")
            self.done(jobs, _exchange(jobs, name="rs_flush_%d" % step), None)
            step += 1


def kernel(x, p, w_in, lru_conv_w, lru_conv_b, lru_gate_a_w, lru_gate_a_b, lru_gate_x_w, lru_gate_x_b, lru_a_param, ssd_conv_w, ssd_conv_b, ssd_dt_bias, ssd_a_log, ssd_d, ssd_norm_w, w_out, ln1_g, ln1_b, w_ff1, w_ff2, ln2_g, ln2_b, w_ple_gate, w_ple, ln3_g, ln3_b, loss_target, m_w_in, m_lru_conv_w, m_lru_conv_b, m_lru_gate_a_w, m_lru_gate_a_b, m_lru_gate_x_w, m_lru_gate_x_b, m_lru_a_param, m_ssd_conv_w, m_ssd_conv_b, m_ssd_dt_bias, m_ssd_a_log, m_ssd_d, m_ssd_norm_w, m_w_out, m_ln1_g, m_ln1_b, m_w_ff1, m_w_ff2, m_ln2_g, m_ln2_b, m_w_ple_gate, m_w_ple, m_ln3_g, m_ln3_b, v_w_in, v_lru_conv_w, v_lru_conv_b, v_lru_gate_a_w, v_lru_gate_a_b, v_lru_gate_x_w, v_lru_gate_x_b, v_lru_a_param, v_ssd_conv_w, v_ssd_conv_b, v_ssd_dt_bias, v_ssd_a_log, v_ssd_d, v_ssd_norm_w, v_w_out, v_ln1_g, v_ln1_b, v_w_ff1, v_w_ff2, v_ln2_g, v_ln2_b, v_w_ple_gate, v_w_ple, v_ln3_g, v_ln3_b):
    given = dict(locals())
    def local(a, n):
        return jnp.swapaxes(a[0], 0, 1) if n == 'w_in' else a[0]

    wsh = {n: local(given[n], n) for n in WEIGHTS}
    msh = {n: local(given["m_" + n], n) for n in WEIGHTS}
    vsh = {n: local(given["v_" + n], n) for n in WEIGHTS}
    xi, yi, ci = _mesh_pos()
    me = 4 * xi + 2 * yi + ci

    shards = {n: wsh[n].astype(BF16) for n in BIG}
    conv_pack = jnp.concatenate([_pad_rows8(wsh[n]) for n in CONV], axis=1)
    g_in, gconv = _exchange([_Job("gather", shards['w_in']), _Job("gather", conv_pack)], name="ag_first")
    full = {'w_in_t': _full_weight('w_in', g_in)}
    c0 = 0
    for n in CONV:
        cw = CONV_CH[n] // N_DEV
        full[n] = gconv[:, :4, c0:c0 + cw].transpose(1, 0, 2).reshape(4, CONV_CH[n])
        c0 += cw
    for n in REPL:
        full[n] = given[n] if given[n].ndim == 2 else wsh[n]

    sched = _Schedule(shards, jnp.reshape(ci, (1,)).astype(jnp.int32))
    loss_local, grad_x, g, raw = _local_step(x[0], p[0, 0], loss_target[0], full, sched)
    sched.flush()
    summed, gat = sched.summed, sched.gathered_small
    loss = gat["rows"][0, 7, 0]
    for d in range(1, N_DEV):
        loss = loss + gat["rows"][d, 7, 0]

    outs = {}
    for n in BIG:
        outs[n] = _adamw(summed[n], wsh[n], msh[n], vsh[n], name="adamw_" + n)
    for n, k in (("lru_gate_a_w", "gate_a"), ("lru_gate_x_w", "gate_x")):
        flat = lambda a: a.reshape(N_HEAD * HEAD_P, HEAD_P)
        res = _adamw(gat[k], flat(wsh[n]), flat(msh[n]), flat(vsh[n]), name="adamw_" + n)
        outs[n] = tuple(r.reshape(N_HEAD, HEAD_P, HEAD_P) for r in res)
    row_items = [("lru_conv_b", 0, 4), ("lru_gate_a_b", 0, 5), ("lru_gate_x_b", 0, 6), ("lru_a_param", 0, 7),
                 ("ssd_conv_b", 1, 4), ("ssd_dt_bias", 2, 0), ("ssd_a_log", 2, 1), ("ssd_d", 2, 2),
                 ("ssd_norm_w", 3, 0), ("ln1_g", 3, 1), ("ln1_b", 3, 2), ("ln2_g", 3, 3), ("ln2_b", 3, 4),
                 ("ln3_g", 3, 5), ("ln3_b", 3, 6)]
    vec = lambda a: a.reshape(1, -1)
    items = [(si, r0, vec(given[n]), vec(given["m_" + n]), vec(given["v_" + n])) for n, si, r0 in row_items]
    own = [(si, 0, wsh[n], msh[n], vsh[n]) for n, si in (("lru_conv_w", 0), ("ssd_conv_w", 1))]
    me1 = jnp.reshape(me, (1,)).astype(jnp.int32)
    res = _adamw_rows([gat[k] for k in SMALL_SRC[:4]], items, own, me1, name="adamw_small")
    for (n, _si, _r0), r4 in zip(row_items, res[:len(row_items)]):
        outs[n] = r4
    for n, r4 in zip(CONV, res[len(row_items):]):
        outs[n] = r4

    def fin(n, k):
        a = jnp.swapaxes(outs[n][k], 0, 1) if n == 'w_in' else outs[n][k]
        return a.reshape(given[n].shape)

    return (loss, grad_x[None],
            *[fin(n, 0) for n in WEIGHTS], *[fin(n, 1) for n in WEIGHTS],
            *[fin(n, 2) for n in WEIGHTS], *[fin(n, 3) for n in WEIGHTS])
```

```python
import math

import jax
import jax.numpy as jnp
from jax import lax
from jax.experimental import pallas as pl
from jax.experimental.pallas import tpu as pltpu

F32 = jnp.float32
BF16 = jnp.bfloat16
HI = lax.Precision.HIGHEST

N_DEV = 8
D_MODEL = 1024
LRU_W = 1024
SSD_W = 1024
XBC = 2048
N_HEAD = 16
HEAD_P = 64
N_GROUP = 4
GROUP_W = 256
N_STATE = 128
CHUNK = 128
D_FF = 4096
PLE_DIM = 256
D_IN = 5136
D_IN_PAD = 5632
COL_G = 1024
COL_Z = 2048
COL_XBC = 3072
COL_DT = 5120
LRU_C = 8.0
ALPHA = 2.0 ** 0.25
LN_EPS = 1e-5
RMS_EPS = 1e-5
ADAM_LR = 0.001
ADAM_B1 = 0.9
ADAM_B2 = 0.999
ADAM_EPS = 1e-08
ADAM_WD = 0.01
ADAM_STEP = 10
GELU_C = math.sqrt(2.0 / math.pi)
LANE = 128
SUBLANE = 8
VMEM_LIMIT = 48 * 1024 * 1024
MESH_T = pl.DeviceIdType.MESH
NEG_BIG = -1e30


def _pcall(body, **kw):
    return pl.pallas_call(body, **kw)


def _cparams(sem):
    return pltpu.CompilerParams(dimension_semantics=sem, vmem_limit_bytes=VMEM_LIMIT)


def _dot(a, b):
    return jnp.dot(a.astype(BF16), b.astype(BF16), preferred_element_type=F32)


def _dot_nt(a, b):
    return lax.dot_general(a.astype(BF16), b.astype(BF16), (((1,), (1,)), ((), ())), preferred_element_type=F32)


def _dot_tn(a, b):
    return lax.dot_general(a.astype(BF16), b.astype(BF16), (((0,), (0,)), ((), ())), preferred_element_type=F32)


def _dotx(a, b):
    return jnp.dot(a, b, precision=HI, preferred_element_type=F32)


def _sigmoid(x):
    return jax.nn.sigmoid(x)


def _softplus(v):
    return jnp.maximum(v, 0.0) + jnp.log1p(jnp.exp(-jnp.abs(v)))


def _gelu(x):
    th = jnp.tanh(GELU_C * (x + 0.044715 * x * x * x))
    return 0.5 * x * (1.0 + th), th


def _gelu_grad(x, th):
    return 0.5 * (1.0 + th) + 0.5 * x * (1.0 - th * th) * GELU_C * (1.0 + 3.0 * 0.044715 * x * x)


def _iota(shape, dim):
    return lax.broadcasted_iota(jnp.int32, shape, dim)


def _mm(a, b, mode, *, tm, tn, name, a_fn=None, extra=None, epi=None, out_dtype=F32, dest_major=False, jobs=()):
    m = a.shape[1] if mode == "tn" else a.shape[0]
    n = b.shape[0] if mode == "nt" else b.shape[1]
    tm, tn = min(tm, m), min(tn, n)
    if dest_major:
        tn = n // N_DEV
    if mode == "nn":
        m, k = a.shape
        _, n = b.shape
        a_spec = pl.BlockSpec((tm, k), lambda i, j: (i, 0))
        b_spec = pl.BlockSpec((k, tn), lambda i, j: (0, j))
        dims = ((1,), (0,))
    elif mode == "nt":
        m, k = a.shape
        n, _ = b.shape
        a_spec = pl.BlockSpec((tm, k), lambda i, j: (i, 0))
        b_spec = pl.BlockSpec((tn, k), lambda i, j: (j, 0))
        dims = ((1,), (1,))
    else:
        k, m = a.shape
        _, n = b.shape
        a_spec = pl.BlockSpec((k, tm), lambda i, j: (0, i))
        b_spec = pl.BlockSpec((k, tn), lambda i, j: (0, j))
        dims = ((0,), (0,))
    assert m % tm == 0 and n % tn == 0, (name, m, n, tm, tn)
    o_spec = pl.BlockSpec((tm, tn), lambda i, j: (i, j))
    in_specs = [a_spec, b_spec]
    args = [a, b]
    if extra is not None:
        in_specs.append(o_spec)
        args.append(extra)

    def body(*refs):
        a_ref, b_ref, o_ref = refs[0], refs[1], refs[-1]
        av = a_ref[...]
        if a_fn is not None:
            av = a_fn(av)
        acc = lax.dot_general(av.astype(BF16), b_ref[...].astype(BF16), (dims, ((), ())), preferred_element_type=F32)
        if epi is not None:
            acc = epi(acc, refs[2][...])
        o_ref[...] = acc.astype(out_dtype)

    out_shape = jax.ShapeDtypeStruct((m, n), out_dtype)
    if dest_major:
        assert extra is None
        o_spec = pl.BlockSpec((None, tm, tn), lambda i, j: (j, i, 0))
        out_shape = jax.ShapeDtypeStruct((N_DEV, m, tn), out_dtype)
    (out,), jouts = _hosted(body, jobs, grid=(m // tm, n // tn), in_specs=in_specs, out_specs=[o_spec],
                            out_shape=[out_shape], args=args, name=name)
    return (out, jouts) if jobs else out


def _mm_pieces(pieces, offsets, b, *, tm, name, extra, epi, jobs=()):
    m = pieces[0].shape[0]
    kb, n = b.shape
    tm = min(tm, m)
    row = lambda wdt: pl.BlockSpec((tm, wdt), lambda i: (i, 0))
    in_specs = [row(pc.shape[1]) for pc in pieces] + [pl.BlockSpec((kb, n), lambda i: (0, 0)), row(n)]
    np_ = len(pieces)

    def body(*refs):
        b_ref, e_ref, o_ref = refs[np_], refs[np_ + 1], refs[np_ + 2]
        acc = jnp.zeros((tm, n), F32)
        for q in range(np_):
            kq = pieces[q].shape[1]
            acc = acc + jnp.dot(refs[q][...].astype(BF16), b_ref[offsets[q]:offsets[q] + kq, :].astype(BF16),
                                preferred_element_type=F32)
        o_ref[...] = epi(acc, e_ref[...])

    (out,), jouts = _hosted(body, jobs, grid=(m // tm,), in_specs=in_specs, out_specs=[row(n)],
                            out_shape=[jax.ShapeDtypeStruct((m, n), F32)], args=list(pieces) + [b, extra], name=name)
    return (out, jouts) if jobs else out


def _relu2(v):
    r = jnp.maximum(v, 0.0)
    return r * r


ROW_TILE = 256


def _ln_stats(t):
    mu = jnp.mean(t, axis=-1, keepdims=True)
    xc = t - mu
    var = jnp.mean(xc * xc, axis=-1, keepdims=True)
    rstd = lax.rsqrt(var + LN_EPS)
    return xc * rstd, rstd


def _ln_bwd_rows(dy, xhat, rstd, g):
    dxh = dy * g
    m1 = jnp.mean(dxh, axis=-1, keepdims=True)
    m2 = jnp.mean(dxh * xhat, axis=-1, keepdims=True)
    return rstd * (dxh - m1 - xhat * m2)


def _ln_fwd(a, b, g, beta, *, name):
    s, d = a.shape
    row = pl.BlockSpec((ROW_TILE, d), lambda i: (i, 0))
    par = pl.BlockSpec((1, d), lambda i: (0, 0))

    def body(a_ref, b_ref, g_ref, be_ref, y_ref, yb_ref):
        xhat, _ = _ln_stats(ALPHA * a_ref[...] + b_ref[...])
        y = xhat * g_ref[...] + be_ref[...]
        y_ref[...] = y
        yb_ref[...] = y.astype(BF16)

    return _pcall(body, grid=(s // ROW_TILE,), in_specs=[row, row, par, par], out_specs=(row, row),
                  out_shape=(jax.ShapeDtypeStruct((s, d), F32), jax.ShapeDtypeStruct((s, d), BF16)), name=name,
                  compiler_params=_cparams(("parallel",)))(a, b, g, beta)


def _ln_bwd(a, b, g, dys, coefs, *, name):
    s, d = a.shape
    row = pl.BlockSpec((ROW_TILE, d), lambda i: (i, 0))
    par = pl.BlockSpec((1, d), lambda i: (0, 0))
    n = len(dys)

    def body(*refs):
        a_ref, b_ref, g_ref = refs[:3]
        dy_refs = refs[3:3 + n]
        dt_ref, dtb_ref, dg_ref, db_ref = refs[3 + n:]
        xhat, rstd = _ln_stats(ALPHA * a_ref[...] + b_ref[...])
        dy = coefs[0] * dy_refs[0][...]
        for q in range(1, n):
            dy = dy + coefs[q] * dy_refs[q][...]
        dt = _ln_bwd_rows(dy, xhat, rstd, g_ref[...])
        dt_ref[...] = dt
        dtb_ref[...] = dt.astype(BF16)

        @pl.when(pl.program_id(0) == 0)
        def _():
            dg_ref[...] = jnp.zeros_like(dg_ref)
            db_ref[...] = jnp.zeros_like(db_ref)

        dg_ref[...] += jnp.sum(dy * xhat, axis=0, keepdims=True)
        db_ref[...] += jnp.sum(dy, axis=0, keepdims=True)

    return _pcall(body, grid=(s // ROW_TILE,), in_specs=[row, row, par] + [row] * n, out_specs=(row, row, par, par),
                  out_shape=(jax.ShapeDtypeStruct((s, d), F32), jax.ShapeDtypeStruct((s, d), BF16),
                             jax.ShapeDtypeStruct((1, d), F32), jax.ShapeDtypeStruct((1, d), F32)),
                  name=name, compiler_params=_cparams(("arbitrary",)))(a, b, g, *dys)


def _head(x2, gpre, ple, g, beta, tgt, *, name):
    s, d = x2.shape
    row = pl.BlockSpec((ROW_TILE, d), lambda i: (i, 0))
    par = pl.BlockSpec((1, d), lambda i: (0, 0))
    lsp = pl.BlockSpec((1, LANE), lambda i: (0, 0))

    def body(x2_ref, gp_ref, ple_ref, g_ref, be_ref, t_ref, loss_ref, dgp_ref, dple_ref, dt_ref, dg_ref, db_ref):
        gate = _sigmoid(gp_ref[...])
        ple_v = ple_ref[...]
        xhat, rstd = _ln_stats(ALPHA * x2_ref[...] + gate * ple_v)
        err = xhat * g_ref[...] + be_ref[...] - t_ref[...]
        dy = err * (1.0 / d)
        dt = _ln_bwd_rows(dy, xhat, rstd, g_ref[...])
        dt_ref[...] = dt
        dgp_ref[...] = (dt * ple_v * gate * (1.0 - gate)).astype(BF16)
        dple_ref[...] = (dt * gate).astype(BF16)

        @pl.when(pl.program_id(0) == 0)
        def _():
            loss_ref[...] = jnp.zeros_like(loss_ref)
            dg_ref[...] = jnp.zeros_like(dg_ref)
            db_ref[...] = jnp.zeros_like(db_ref)

        loss_ref[...] += 0.5 * jnp.sum(jnp.mean(err * err, axis=-1, keepdims=True))
        dg_ref[...] += jnp.sum(dy * xhat, axis=0, keepdims=True)
        db_ref[...] += jnp.sum(dy, axis=0, keepdims=True)

    sd = jax.ShapeDtypeStruct((s, d), F32)
    sb = jax.ShapeDtypeStruct((s, d), BF16)
    pd = jax.ShapeDtypeStruct((1, d), F32)
    return _pcall(body, grid=(s // ROW_TILE,), in_specs=[row, row, row, par, par, row],
                  out_specs=(lsp, row, row, row, par, par),
                  out_shape=(jax.ShapeDtypeStruct((1, LANE), F32), sb, sb, sd, pd, pd),
                  name=name, compiler_params=_cparams(("arbitrary",)))(x2, gpre, ple, g, beta, tgt)


CONV_R = 256
PAD = SUBLANE


def _shift_down(ext, s):
    if s == 0:
        return ext[PAD:, :]
    return pltpu.roll(ext, s, 0)[PAD:, :]


def _shift_up(ext, s):
    r = ext.shape[0] - PAD
    if s == 0:
        return ext[:r, :]
    return pltpu.roll(ext, r + PAD - s, 0)[:r, :]


def _conv_rows(xpad_ref, r0, w_ref):
    ext = xpad_ref[pl.ds(r0, CONV_R + PAD), :]
    acc = _shift_down(ext, 0) * w_ref[3:4, :]
    for k in range(3):
        acc = acc + _shift_down(ext, 3 - k) * w_ref[k:k + 1, :]
    return acc, ext


def _fill_front_padded(dst_ref, src_ref, s):
    dst_ref[0:PAD, :] = jnp.zeros((PAD, dst_ref.shape[1]), F32)

    def cp(q, _):
        r0 = pl.multiple_of(q * CONV_R, CONV_R)
        dst_ref[pl.ds(pl.multiple_of(PAD + r0, PAD), CONV_R), :] = src_ref[pl.ds(r0, CONV_R), :]
        return 0

    lax.fori_loop(0, s // CONV_R, cp, 0)


def _conv_silu_fwd(proj, w8, b, *, col0, width, ct, name):
    s = proj.shape[0]
    nb = col0 // ct

    def body(x_ref, w_ref, b_ref, o_ref, xpad):
        _fill_front_padded(xpad, x_ref, s)

        def step(q, _):
            r0 = pl.multiple_of(q * CONV_R, CONV_R)
            acc, _e = _conv_rows(xpad, r0, w_ref)
            pre = acc + b_ref[...]
            o_ref[pl.ds(r0, CONV_R), :] = pre * _sigmoid(pre)
            return 0

        lax.fori_loop(0, s // CONV_R, step, 0)

    return _pcall(
        body, grid=(width // ct,),
        in_specs=[pl.BlockSpec((s, ct), lambda j: (0, nb + j)), pl.BlockSpec((SUBLANE, ct), lambda j: (0, j)),
                  pl.BlockSpec((1, ct), lambda j: (0, j))],
        out_specs=pl.BlockSpec((s, ct), lambda j: (0, j)),
        out_shape=jax.ShapeDtypeStruct((s, width), F32),
        scratch_shapes=[pltpu.VMEM((s + PAD, ct), F32)], name=name,
        compiler_params=_cparams(("parallel",)))(proj, w8, b)


def _conv_bwd_rows(dpad_ref, r0, w_ref):
    return _conv_bwd_ext(dpad_ref[pl.ds(r0, CONV_R + PAD), :], w_ref)


def _conv_bwd_ext(ext, w_ref):
    acc = _shift_up(ext, 0) * w_ref[3:4, :]
    for k in range(3):
        acc = acc + _shift_up(ext, 3 - k) * w_ref[k:k + 1, :]
    return acc


def _conv_silu_bwd(proj, dact, w8, b, *, col0, width, ct, name, jobs=()):
    s = proj.shape[0]
    nb = col0 // ct

    def body(x_ref, d_ref, w_ref, b_ref, dx_ref, dwb_ref, xpad, dpad):
        _fill_front_padded(xpad, x_ref, s)
        dpad[pl.ds(s, PAD), :] = jnp.zeros((PAD, ct), F32)
        dwb_ref[...] = jnp.zeros_like(dwb_ref)

        def step(q, _):
            r0 = pl.multiple_of(q * CONV_R, CONV_R)
            acc, ext = _conv_rows(xpad, r0, w_ref)
            pre = acc + b_ref[...]
            sg = _sigmoid(pre)
            dpre = d_ref[pl.ds(r0, CONV_R), :] * sg * (1.0 + pre * (1.0 - sg))
            dpad[pl.ds(r0, CONV_R), :] = dpre
            for k in range(4):
                dwb_ref[k:k + 1, :] += jnp.sum(dpre * _shift_down(ext, 3 - k), axis=0, keepdims=True)
            dwb_ref[4:5, :] += jnp.sum(dpre, axis=0, keepdims=True)
            return 0

        lax.fori_loop(0, s // CONV_R, step, 0)

        def step2(q, _):
            r0 = pl.multiple_of(q * CONV_R, CONV_R)
            dx_ref[pl.ds(r0, CONV_R), :] = _conv_bwd_rows(dpad, r0, w_ref).astype(BF16)
            return 0

        lax.fori_loop(0, s // CONV_R, step2, 0)

    colb = pl.BlockSpec((s, ct), lambda j: (0, j))
    outs, jouts = _hosted(
        body, jobs, grid=(width // ct,),
        in_specs=[pl.BlockSpec((s, ct), lambda j: (0, nb + j)), colb, pl.BlockSpec((SUBLANE, ct), lambda j: (0, j)),
                  pl.BlockSpec((1, ct), lambda j: (0, j))],
        out_specs=(colb, pl.BlockSpec((SUBLANE, ct), lambda j: (0, j))),
        out_shape=(jax.ShapeDtypeStruct((s, width), BF16), jax.ShapeDtypeStruct((SUBLANE, width), F32)),
        scratch_shapes=[pltpu.VMEM((s + PAD, ct), F32), pltpu.VMEM((s + PAD, ct), F32)], name=name,
        args=(proj, dact, w8, b))
    return (tuple(outs), jouts) if jobs else tuple(outs)


LRU_CT = 128


def _row_of(v, r):
    return jnp.sum(jnp.where(_iota((v.shape[0], 1), 0) == r, v, 0.0), axis=0, keepdims=True)


def _scan_fwd(a, u):
    r = a.shape[0]
    row = _iota((r, 1), 0)
    d = 1
    while d < r:
        valid = row >= d
        u = jnp.where(valid, a * pltpu.roll(u, d, 0) + u, u)
        a = jnp.where(valid, a * pltpu.roll(a, d, 0), a)
        d *= 2
    return a, u


def _scan_rev(b, u):
    r = b.shape[0]
    row = _iota((r, 1), 0)
    d = 1
    while d < r:
        valid = row < r - d
        u = jnp.where(valid, b * pltpu.roll(u, r - d, 0) + u, u)
        b = jnp.where(valid, b * pltpu.roll(b, r - d, 0), b)
        d *= 2
    return b, u


def _lru_chunk(xpad, r0, cw_ref, cb, wa, ba, wx, bx, sp):
    acc, ext = _conv_rows(xpad, r0, cw_ref)
    xl = acc + cb
    r = _sigmoid(_dot(xl, wa) + ba)
    i = _sigmoid(_dot(xl, wx) + bx)
    la = -LRU_C * r * sp
    a = jnp.exp(la)
    a2 = jnp.exp(2.0 * la)
    mult = jnp.sqrt(-jnp.tanh(la) * (a2 + 1.0))
    first = (r0 + _iota((CONV_R, 1), 0)) == 0
    mult = jnp.where(first, 1.0, mult)
    return ext, xl, r, i, a, a2, mult, first


def _lru_specs(s):
    ct = LRU_CT
    nb_g = COL_G // ct
    return dict(
        x=pl.BlockSpec((s, ct), lambda j: (0, j)),
        g=pl.BlockSpec((s, ct), lambda j: (0, nb_g + j)),
        col=pl.BlockSpec((s, ct), lambda j: (0, j)),
        cw=pl.BlockSpec((SUBLANE, ct), lambda j: (0, j)),
        vec=pl.BlockSpec((1, ct), lambda j: (0, j)),
        gate=pl.BlockSpec((None, ct, ct), lambda j: (j, 0, 0)),
    )


def _lru_fwd(proj, cw8, cb, wa_bd, ba, wx_bd, bx, ap, *, name, jobs=()):
    s = proj.shape[0]
    ct = LRU_CT
    sp_ = _lru_specs(s)

    def body(x_ref, g_ref, cw_ref, cb_ref, wa_ref, ba_ref, wx_ref, bx_ref, ap_ref, y_ref, h_ref, xpad):
        _fill_front_padded(xpad, x_ref, s)
        sp = _softplus(-ap_ref[...])

        def step(q, carry):
            r0 = pl.multiple_of(q * CONV_R, CONV_R)
            _e, xl, _r, i, a, _a2, mult, _f = _lru_chunk(xpad, r0, cw_ref, cb_ref[...], wa_ref[...], ba_ref[...],
                                                       wx_ref[...], bx_ref[...], sp)
            acum, ucum = _scan_fwd(a, xl * i * mult)
            h = acum * carry + ucum
            h_ref[pl.ds(r0, CONV_R), :] = h
            ge, _th = _gelu(g_ref[pl.ds(r0, CONV_R), :])
            y_ref[pl.ds(r0, CONV_R), :] = (ge * h).astype(BF16)
            return _row_of(h, CONV_R - 1)

        lax.fori_loop(0, s // CONV_R, step, jnp.zeros((1, ct), F32))

    (ymix, hs), jouts = _hosted(
        body, jobs, grid=(LRU_W // ct,),
        in_specs=[sp_["x"], sp_["g"], sp_["cw"], sp_["vec"], sp_["gate"], sp_["vec"], sp_["gate"], sp_["vec"], sp_["vec"]],
        out_specs=(sp_["col"], sp_["col"]),
        out_shape=(jax.ShapeDtypeStruct((s, LRU_W + SSD_W), BF16), jax.ShapeDtypeStruct((s, LRU_W), F32)),
        scratch_shapes=[pltpu.VMEM((s + PAD, ct), F32)],
        name=name, args=(proj, proj, cw8, cb, wa_bd, ba, wx_bd, bx, ap))
    return ((ymix, hs), jouts) if jobs else (ymix, hs)


def _lru_bwd(proj, dy, hs, cw8, cb, wa_bd, ba, wx_bd, bx, ap, *, name, jobs=()):
    s = proj.shape[0]
    ct = LRU_CT
    sp_ = _lru_specs(s)

    nq = s // CONV_R

    def body(x_ref, g_ref, dy_ref, h_ref, cw_ref, cb_ref, wa_ref, ba_ref, wx_ref, bx_ref, ap_ref,
             dx_ref, dg_ref, dcwb_ref, dwa_ref, dwx_ref, xpad, hpad):
        _fill_front_padded(xpad, x_ref, s)
        _fill_front_padded(hpad, h_ref, s)
        apv = ap_ref[...]
        sp = _softplus(-apv)
        cb_v, wa, ba_v, wx, bx_v = cb_ref[...], wa_ref[...], ba_ref[...], wx_ref[...], bx_ref[...]
        dcwb_ref[...] = jnp.zeros_like(dcwb_ref)
        dwa_ref[...] = jnp.zeros_like(dwa_ref)
        dwx_ref[...] = jnp.zeros_like(dwx_ref)

        def back(k, carry):
            g_next, a_next, dxl_next = carry
            last_row = _iota((CONV_R, 1), 0) == CONV_R - 1
            r0 = pl.multiple_of((nq - 1 - k) * CONV_R, CONV_R)
            ext, xl, r, i, a, a2, mult, first = _lru_chunk(xpad, r0, cw_ref, cb_v, wa, ba_v, wx, bx_v, sp)
            gv = g_ref[pl.ds(r0, CONV_R), :]
            dyv = dy_ref[pl.ds(r0, CONV_R), :]
            hext = hpad[pl.ds(r0, CONV_R + PAD), :]
            ge, th = _gelu(gv)
            dg_ref[pl.ds(r0, CONV_R), :] = (dyv * _shift_down(hext, 0) * _gelu_grad(gv, th)).astype(BF16)
            b = jnp.where(last_row, a_next, pltpu.roll(a, CONV_R - 1, 0))
            bcum, dcum = _scan_rev(b, dyv * ge)
            gval = dcum + bcum * g_next
            hprev = _shift_down(hext, 1)
            da = gval * hprev
            dxl = gval * i * mult
            di = gval * xl * mult
            dmult = jnp.where(first, 0.0, gval * xl * i)
            dla = da * a - dmult * a2 / mult
            dr = dla * (-LRU_C) * sp
            dcwb_ref[7:8, :] += jnp.sum(dla * (-LRU_C) * r, axis=0, keepdims=True)
            dpr = dr * r * (1.0 - r)
            dpi = di * i * (1.0 - i)
            dxl = dxl + _dot_nt(dpr, wa) + _dot_nt(dpi, wx)
            dwa_ref[...] += _dot_tn(xl, dpr)
            dwx_ref[...] += _dot_tn(xl, dpi)
            dcwb_ref[5:6, :] += jnp.sum(dpr, axis=0, keepdims=True)
            dcwb_ref[6:7, :] += jnp.sum(dpi, axis=0, keepdims=True)
            for tap in range(4):
                dcwb_ref[tap:tap + 1, :] += jnp.sum(dxl * _shift_down(ext, 3 - tap), axis=0, keepdims=True)
            dcwb_ref[4:5, :] += jnp.sum(dxl, axis=0, keepdims=True)
            dx_ref[pl.ds(r0, CONV_R), :] = _conv_bwd_ext(jnp.concatenate([dxl, dxl_next], axis=0), cw_ref).astype(BF16)
            return _row_of(gval, 0), _row_of(a, 0), dxl[:PAD, :]

        zero = jnp.zeros((1, ct), F32)
        lax.fori_loop(0, nq, back, (zero, zero, jnp.zeros((PAD, ct), F32)))
        dcwb_ref[7:8, :] = dcwb_ref[7:8, :] * (-_sigmoid(-apv))

    nt = LRU_W // ct
    outs, jouts = _hosted(
        body, jobs, grid=(nt,),
        in_specs=[sp_["x"], sp_["g"], sp_["col"], sp_["col"], sp_["cw"], sp_["vec"], sp_["gate"], sp_["vec"], sp_["gate"],
                  sp_["vec"], sp_["vec"]],
        out_specs=(sp_["col"], sp_["col"], sp_["cw"], sp_["gate"], sp_["gate"]),
        out_shape=(jax.ShapeDtypeStruct((s, LRU_W), BF16), jax.ShapeDtypeStruct((s, LRU_W), BF16),
                   jax.ShapeDtypeStruct((SUBLANE, LRU_W), F32), jax.ShapeDtypeStruct((nt, ct, ct), F32),
                   jax.ShapeDtypeStruct((nt, ct, ct), F32)),
        scratch_shapes=[pltpu.VMEM((s + PAD, ct), F32), pltpu.VMEM((s + PAD, ct), F32)],
        name=name, args=(proj, proj, dy, hs, cw8, cb, wa_bd, ba, wx_bd, bx, ap))
    return (tuple(outs), jouts) if jobs else tuple(outs)


def _ssd_prep(dtr, bias, alog_pad, alogx):
    l = CHUNK
    lane = _iota((1, LANE), 1)
    a_head = jnp.where(lane < N_HEAD, -jnp.exp(alog_pad), 0.0)
    dt = _softplus(dtr + bias)
    tril = (_iota((l, l), 1) <= _iota((l, l), 0)).astype(F32)
    cs = _dotx(tril, dt * a_head)
    expand = (jnp.right_shift(_iota((LANE, SSD_W), 1), 6) == _iota((LANE, SSD_W), 0)).astype(F32)
    dtx = _dotx(dt, expand)
    ax = -jnp.exp(alogx)
    csx = _dotx(tril, dtx * ax)
    totx = jnp.sum(dtx * ax, axis=0, keepdims=True)
    return dict(a_head=a_head, dt=dt, tril=tril, cs=cs, expand=expand, dtx=dtx, ax=ax, csx=csx, totx=totx)


def _decay_mat(cs, cst_ref, h, causal):
    lane = _iota((CHUNK, LANE), 1)
    col = jnp.sum(jnp.where(lane == h, cs, 0.0), axis=1, keepdims=True)
    row = cst_ref[h:h + 1, :]
    return jnp.exp(jnp.where(causal, col - row, NEG_BIG))


def _head_mask(j):
    lane = _iota((CHUNK, GROUP_W), 1)
    return (lane >= j * HEAD_P) & (lane < (j + 1) * HEAD_P)


def _ssd_group_fwd(q, g, xs_g, bg, cg, ht_g, cst_ref, causal, dx_g):
    sl = slice(g * GROUP_W, (g + 1) * GROUP_W)
    dtx_g, csx_g, totx_g = q["dtx"][:, sl], q["csx"][:, sl], q["totx"][:, sl]
    xdt = xs_g * dtx_g
    ex = jnp.exp(csx_g)
    cb = _dot_nt(cg, bg)
    yoff = _dot(cg, ht_g) * ex
    ydiag = jnp.zeros((CHUNK, GROUP_W), F32)
    for j in range(4):
        sc = cb * _decay_mat(q["cs"], cst_ref, 4 * g + j, causal)
        ydiag = jnp.where(_head_mask(j), _dot(sc, xdt), ydiag)
    y = ydiag + yoff + xs_g * dx_g
    dsx = jnp.exp(totx_g - csx_g)
    return y, dict(xdt=xdt, ex=ex, cb=cb, yoff=yoff, dsx=dsx, dtx=dtx_g, totx=totx_g)


def _gated_norm_fwd(y_g, z_g, w_g):
    sz = _sigmoid(z_g)
    silu = z_g * sz
    yf = y_g * silu
    rs = lax.rsqrt(jnp.mean(yf * yf, axis=1, keepdims=True) + RMS_EPS)
    yn = yf * rs
    return yn * w_g, (sz, silu, rs, yn)


def _ssd_fwd(xact, proj, ymix, bias_pad, alog_pad, alogx, dxp, normw, *, name, jobs=()):
    s = xact.shape[0]
    nc = s // CHUNK

    def body(xa_ref, dt_ref, z_ref, _ymix_ref, bias_ref, alp_ref, alx_ref, dx_ref, nw_ref, y_ref, hp_ref, ht, cst):
        @pl.when(pl.program_id(0) == 0)
        def _():
            ht[...] = jnp.zeros_like(ht)

        hp_ref[...] = ht[...]
        q = _ssd_prep(dt_ref[...], bias_ref[...], alp_ref[...], alx_ref[...])
        cst[...] = q["cs"].T
        causal = q["tril"] > 0.0
        for g in range(N_GROUP):
            sl = slice(g * GROUP_W, (g + 1) * GROUP_W)
            xs_g = xa_ref[:, sl]
            bg = xa_ref[:, SSD_W + g * N_STATE:SSD_W + (g + 1) * N_STATE]
            cg = xa_ref[:, SSD_W + N_GROUP * N_STATE + g * N_STATE:SSD_W + N_GROUP * N_STATE + (g + 1) * N_STATE]
            ht_g = ht[:, sl]
            y, f = _ssd_group_fwd(q, g, xs_g, bg, cg, ht_g, cst, causal, dx_ref[:, sl])
            out, _ = _gated_norm_fwd(y, z_ref[:, sl], nw_ref[:, sl])
            y_ref[:, sl] = out.astype(BF16)
            ht[:, sl] = jnp.exp(f["totx"]) * ht_g + _dot_tn(bg, f["xdt"] * f["dsx"])

    par = lambda w: pl.BlockSpec((1, w), lambda c: (0, 0))
    (ycat, hprev), jouts = _hosted(
        body, jobs, grid=(nc,),
        in_specs=[pl.BlockSpec((CHUNK, XBC), lambda c: (c, 0)),
                  pl.BlockSpec((CHUNK, LANE), lambda c: (c, COL_DT // LANE)),
                  pl.BlockSpec((CHUNK, SSD_W), lambda c: (c, COL_Z // SSD_W)),
                  ANY_SPEC, par(LANE), par(LANE), par(SSD_W), par(SSD_W), par(SSD_W)],
        out_specs=(pl.BlockSpec((CHUNK, SSD_W), lambda c: (c, LRU_W // SSD_W)),
                   pl.BlockSpec((None, N_STATE, SSD_W), lambda c: (c, 0, 0))),
        out_shape=(jax.ShapeDtypeStruct(ymix.shape, ymix.dtype), jax.ShapeDtypeStruct((nc, N_STATE, SSD_W), F32)),
        scratch_shapes=[pltpu.VMEM((N_STATE, SSD_W), F32), pltpu.VMEM((CHUNK, LANE), F32)],
        aliases={3: 0}, name=name, args=(xact, proj, proj, ymix, bias_pad, alog_pad, alogx, dxp, normw))
    return ((ycat, hprev), jouts) if jobs else (ycat, hprev)


def _ssd_bwd(xact, proj, dycat, hprev, bias_pad, alog_pad, alogx, dxp, normw, *, name, jobs=()):
    s = xact.shape[0]
    nc = s // CHUNK
    l = CHUNK

    def body(xa_ref, dt_ref, z_ref, dy_ref, hp_ref, bias_ref, alp_ref, alx_ref, dx_ref, nw_ref,
             dxa_ref, ddt_ref, dz_ref, dnw_ref, small_ref, dht, cst, accx, dcsx_s, ddtx_s):
        step = pl.program_id(0)

        @pl.when(step == 0)
        def _():
            dht[...] = jnp.zeros_like(dht)
            accx[...] = jnp.zeros_like(accx)
            dnw_ref[...] = jnp.zeros_like(dnw_ref)
            small_ref[...] = jnp.zeros_like(small_ref)

        dtr = dt_ref[...]
        q = _ssd_prep(dtr, bias_ref[...], alp_ref[...], alx_ref[...])
        cst[...] = q["cs"].T
        causal = q["tril"] > 0.0
        eye = _iota((l, l), 0) == _iota((l, l), 1)
        lane = _iota((l, LANE), 1)
        dcs_head = jnp.zeros((l, LANE), F32)
        for g in range(N_GROUP):
            sl = slice(g * GROUP_W, (g + 1) * GROUP_W)
            slb = slice(SSD_W + g * N_STATE, SSD_W + (g + 1) * N_STATE)
            slc = slice(SSD_W + N_GROUP * N_STATE + g * N_STATE, SSD_W + N_GROUP * N_STATE + (g + 1) * N_STATE)
            xs_g, bg, cg = xa_ref[:, sl], xa_ref[:, slb], xa_ref[:, slc]
            ht_g = hp_ref[:, sl]
            dxp_g = dx_ref[:, sl]
            y, f = _ssd_group_fwd(q, g, xs_g, bg, cg, ht_g, cst, causal, dxp_g)
            z_g, nw_g = z_ref[:, sl], nw_ref[:, sl]
            _o, (sz, silu, rs, yn) = _gated_norm_fwd(y, z_g, nw_g)
            dout = dy_ref[:, sl]
            dnw_ref[:, sl] += jnp.sum(dout * yn, axis=0, keepdims=True)
            dyn = dout * nw_g
            dyf = rs * (dyn - yn * jnp.mean(dyn * yn, axis=1, keepdims=True))
            dy = dyf * silu
            dz_ref[:, sl] = (dyf * y * sz * (1.0 + z_g * (1.0 - sz))).astype(BF16)
            accx[0:1, sl] += jnp.sum(dy * xs_g, axis=0, keepdims=True)
            dyo = dy * f["ex"]
            dcg = _dot_nt(dyo, ht_g)
            dht_prev = _dot_tn(cg, dyo)
            dcsx = dy * f["yoff"]
            xdt = f["xdt"]
            dxdt = jnp.zeros((l, GROUP_W), F32)
            dcb = jnp.zeros((l, l), F32)
            for j in range(4):
                h = 4 * g + j
                lm = _decay_mat(q["cs"], cst, h, causal)
                sc = f["cb"] * lm
                mask = _head_mask(j)
                ds_ = jnp.where(causal, _dot_nt(jnp.where(mask, dy, 0.0), xdt), 0.0)
                dxdt = jnp.where(mask, _dot_tn(sc, dy), dxdt)
                dcb = dcb + ds_ * lm
                m = ds_ * sc
                rsum = jnp.sum(m, axis=1, keepdims=True)
                csum = jnp.sum(m, axis=0, keepdims=True)
                csum_col = jnp.sum(jnp.where(eye, csum, 0.0), axis=1, keepdims=True)
                dcs_head = dcs_head + jnp.where(lane == h, rsum - csum_col, 0.0)
            dhn = dht[:, sl]
            etot = jnp.exp(f["totx"])
            dxd = _dot(bg, dhn)
            dbg = _dot_nt(xdt * f["dsx"], dhn)
            dxdt = dxdt + dxd * f["dsx"]
            qq = dxd * xdt * f["dsx"]
            dcsx = dcsx - qq
            dtot = jnp.sum(qq, axis=0, keepdims=True) + jnp.sum(dhn * ht_g, axis=0, keepdims=True) * etot
            dht[:, sl] = etot * dhn + dht_prev
            dcg = dcg + _dot(dcb, bg)
            dbg = dbg + _dot_tn(dcb, cg)
            dxa_ref[:, sl] = dxdt * f["dtx"] + dy * dxp_g
            dxa_ref[:, slb] = dbg
            dxa_ref[:, slc] = dcg
            dcsx_s[:, sl] = dcsx
            ddtx_s[:, sl] = dxdt * xs_g
            accx[2:3, sl] = dtot
        triu = (_iota((l, l), 1) >= _iota((l, l), 0)).astype(F32)
        dax = _dotx(triu, dcsx_s[...]) + accx[2:3, :]
        accx[1:2, :] += jnp.sum(dax * q["dtx"], axis=0, keepdims=True)
        reduce = (jnp.right_shift(_iota((SSD_W, LANE), 0), 6) == _iota((SSD_W, LANE), 1)).astype(F32)
        ddt = _dotx(ddtx_s[...] + dax * q["ax"], reduce)
        da_head = _dotx(triu, dcs_head)
        ddt = ddt + da_head * q["a_head"]
        small_ref[1:2, :] += jnp.sum(da_head * q["dt"], axis=0, keepdims=True)
        ddtr = ddt * _sigmoid(dtr + bias_ref[...])
        ddt_ref[...] = ddtr.astype(BF16)
        small_ref[0:1, :] += jnp.sum(ddtr, axis=0, keepdims=True)

        @pl.when(step == nc - 1)
        def _():
            red = _dotx(accx[...], reduce)
            d_a = small_ref[1:2, :] + red[1:2, :]
            small_ref[1:2, :] = d_a * q["a_head"]
            small_ref[2:3, :] = red[0:1, :]

    rev = lambda c: nc - 1 - c
    par = lambda w: pl.BlockSpec((1, w), lambda c: (0, 0))
    outs, jouts = _hosted(
        body, jobs, grid=(nc,),
        in_specs=[pl.BlockSpec((CHUNK, XBC), lambda c: (rev(c), 0)),
                  pl.BlockSpec((CHUNK, LANE), lambda c: (rev(c), COL_DT // LANE)),
                  pl.BlockSpec((CHUNK, SSD_W), lambda c: (rev(c), COL_Z // SSD_W)),
                  pl.BlockSpec((CHUNK, SSD_W), lambda c: (rev(c), 1)),
                  pl.BlockSpec((None, N_STATE, SSD_W), lambda c: (rev(c), 0, 0)),
                  par(LANE), par(LANE), par(SSD_W), par(SSD_W), par(SSD_W)],
        out_specs=(pl.BlockSpec((CHUNK, XBC), lambda c: (rev(c), 0)),
                   pl.BlockSpec((CHUNK, LANE), lambda c: (rev(c), 0)),
                   pl.BlockSpec((CHUNK, SSD_W), lambda c: (rev(c), 0)),
                   par(SSD_W), pl.BlockSpec((SUBLANE, LANE), lambda c: (0, 0))),
        out_shape=(jax.ShapeDtypeStruct((s, XBC), F32), jax.ShapeDtypeStruct((s, LANE), BF16),
                   jax.ShapeDtypeStruct((s, SSD_W), BF16), jax.ShapeDtypeStruct((1, SSD_W), F32),
                   jax.ShapeDtypeStruct((SUBLANE, LANE), F32)),
        scratch_shapes=[pltpu.VMEM((N_STATE, SSD_W), F32), pltpu.VMEM((CHUNK, LANE), F32),
                        pltpu.VMEM((SUBLANE, SSD_W), F32), pltpu.VMEM((CHUNK, SSD_W), F32),
                        pltpu.VMEM((CHUNK, SSD_W), F32)],
        name=name, args=(xact, proj, proj, dycat, hprev, bias_pad, alog_pad, alogx, dxp, normw))
    return (tuple(outs), jouts) if jobs else tuple(outs)


def _blockdiag(w):
    w2 = w.reshape(N_HEAD // 2, 2, HEAD_P, HEAD_P)
    z = jnp.zeros((N_HEAD // 2, HEAD_P, HEAD_P), w.dtype)
    top = jnp.concatenate([w2[:, 0], z], axis=2)
    bot = jnp.concatenate([z, w2[:, 1]], axis=2)
    return jnp.concatenate([top, bot], axis=1)


def _unblockdiag(wbd):
    a = wbd[:, :HEAD_P, :HEAD_P]
    b = wbd[:, HEAD_P:, HEAD_P:]
    return jnp.stack([a, b], axis=1).reshape(N_HEAD, HEAD_P, HEAD_P)


def _pad_rows8(w):
    return jnp.concatenate([w, jnp.zeros((SUBLANE - w.shape[0], w.shape[1]), w.dtype)], axis=0)


def _pad_lane(v):
    return jnp.concatenate([v, jnp.zeros((1, LANE - v.shape[1]), v.dtype)], axis=1)


class _NoExchange:
    def ride(self, host):
        return []

    def done(self, jobs, outs, w):
        pass

    def grad(self, name, val):
        pass

    def small(self, raw):
        pass


def _local_step(x, p, tgt, w, hooks=_NoExchange()):
    cw_l = _pad_rows8(w["lru_conv_w"])
    cw_s = _pad_rows8(w["ssd_conv_w"])
    wa_bd = _blockdiag(w["lru_gate_a_w"])
    wx_bd = _blockdiag(w["lru_gate_x_w"])
    ba = w["lru_gate_a_b"].reshape(1, LRU_W)
    bx = w["lru_gate_x_b"].reshape(1, LRU_W)
    bias_pad = _pad_lane(w["ssd_dt_bias"])
    alog_pad = _pad_lane(w["ssd_a_log"])
    alogx = jnp.repeat(w["ssd_a_log"], HEAD_P, axis=1)
    dxp = jnp.repeat(w["ssd_d"], HEAD_P, axis=1)

    def host(fn, *a, name, **k):
        jobs = hooks.ride(name)
        res = fn(*a, name=name, jobs=jobs, **k)
        if jobs:
            res, jouts = res
            hooks.done(jobs, jouts, w)
        return res

    def grad(n, val):
        g[n] = val
        hooks.grad(n, val)

    xb = x.astype(BF16)
    proj = host(_mm, xb, w["w_in_t"], "nt", tm=1024, tn=512, name="in_proj")
    ymix, h_lru = host(_lru_fwd, proj, cw_l, w["lru_conv_b"], wa_bd, ba, wx_bd, bx, w["lru_a_param"], name="lru_fwd")
    xact = _conv_silu_fwd(proj, cw_s, w["ssd_conv_b"], col0=COL_XBC, width=XBC, ct=256, name="ssd_conv_fwd")
    ycat, hprev = host(_ssd_fwd, xact, proj, ymix, bias_pad, alog_pad, alogx, dxp, w["ssd_norm_w"], name="ssd_fwd")
    mix = _mm(ycat, w["w_out"], "nn", tm=1024, tn=1024, name="out_proj")
    x1, x1b = _ln_fwd(x, mix, w["ln1_g"], w["ln1_b"], name="ln1_fwd")
    pre = _mm(x1b, w["w_ff1"], "nn", tm=1024, tn=512, out_dtype=BF16, name="ff1")
    ff = _mm(pre, w["w_ff2"], "nn", tm=512, tn=1024, a_fn=_relu2, name="ff2")
    x2, x2b = _ln_fwd(x1, ff, w["ln2_g"], w["ln2_b"], name="ln2_fwd")
    gpre = _mm(x2b, w["w_ple_gate"], "nn", tm=1024, tn=1024, name="ple_gate")
    ple = _mm(p, w["w_ple"], "nn", tm=1024, tn=1024, name="ple_proj")
    loss, dgpre, dple, dt3, dg3, db3 = _head(x2, gpre, ple, w["ln3_g"], w["ln3_b"], tgt, name="head")

    g = {}
    g["ln3_g"], g["ln3_b"] = dg3, db3
    grad("w_ple_gate", _mm(x2b, dgpre, "tn", tm=512, tn=1024, out_dtype=BF16, name="d_w_ple_gate"))
    grad("w_ple", _mm(p, dple, "tn", tm=256, tn=512, dest_major=True, out_dtype=BF16, name="d_w_ple"))
    dx2_mm = host(_mm, dgpre, w["w_ple_gate"], "nt", tm=1024, tn=1024, name="d_x2")
    dt2, dt2b, g["ln2_g"], g["ln2_b"] = _ln_bwd(x1, ff, w["ln2_g"], [dt3, dx2_mm], [ALPHA, 1.0], name="ln2_bwd")
    grad("w_ff2", host(_mm, pre, dt2b, "tn", tm=512, tn=1024, a_fn=_relu2, out_dtype=BF16, name="d_w_ff2"))
    dpre = host(_mm, dt2b, w["w_ff2"], "nt", tm=1024, tn=512, extra=pre, out_dtype=BF16,
                epi=lambda acc, pv: acc * 2.0 * jnp.maximum(pv.astype(F32), 0.0), name="d_pre")
    grad("w_ff1", host(_mm, x1b, dpre, "tn", tm=1024, tn=512, dest_major=True, out_dtype=BF16, name="d_w_ff1"))
    dx1_mm = host(_mm, dpre, w["w_ff1"], "nt", tm=512, tn=1024, name="d_x1")
    dt1, dt1b, g["ln1_g"], g["ln1_b"] = _ln_bwd(x, mix, w["ln1_g"], [dt2, dx1_mm], [ALPHA, 1.0], name="ln1_bwd")
    grad("w_out", host(_mm, ycat, dt1b, "tn", tm=512, tn=1024, out_dtype=BF16, name="d_w_out"))
    dycat = host(_mm, dt1b, w["w_out"], "nt", tm=1024, tn=1024, name="d_ycat")
    dxl, dgl, dcwb_l, dwa, dwx = host(_lru_bwd, proj, dycat, h_lru, cw_l, w["lru_conv_b"], wa_bd, ba, wx_bd, bx,
                                      w["lru_a_param"], name="lru_bwd")
    g["lru_gate_a_w"] = _unblockdiag(dwa)
    g["lru_gate_x_w"] = _unblockdiag(dwx)
    raw = dict(lru=dcwb_l, gate_a=g["lru_gate_a_w"].reshape(N_HEAD * HEAD_P, HEAD_P),
               gate_x=g["lru_gate_x_w"].reshape(N_HEAD * HEAD_P, HEAD_P))
    hooks.small(raw)
    dxact, ddt, dz, g["ssd_norm_w"], small = host(_ssd_bwd, xact, proj, dycat, hprev, bias_pad, alog_pad, alogx, dxp,
                                                   w["ssd_norm_w"], name="ssd_bwd")
    dxbc, dcwb_s = host(_conv_silu_bwd, proj, dxact, cw_s, w["ssd_conv_b"], col0=COL_XBC, width=XBC, ct=256,
                        name="ssd_conv_bwd")
    pieces, offsets = [dxl, dgl, dz, dxbc, ddt], [0, COL_G, COL_Z, COL_XBC, COL_DT]

    g["lru_conv_w"] = dcwb_l[0:4]
    g["lru_conv_b"] = dcwb_l[4:5]
    g["lru_gate_a_b"] = dcwb_l[5:6]
    g["lru_gate_x_b"] = dcwb_l[6:7]
    g["lru_a_param"] = dcwb_l[7:8]
    g["ssd_conv_w"] = dcwb_s[0:4]
    g["ssd_conv_b"] = dcwb_s[4:5]
    g["ssd_dt_bias"] = small[0:1, :N_HEAD]
    g["ssd_a_log"] = small[1:2, :N_HEAD]
    g["ssd_d"] = small[2:3, :N_HEAD]
    rows = jnp.concatenate([g[n] for n in ("ssd_norm_w", "ln1_g", "ln1_b", "ln2_g", "ln2_b", "ln3_g", "ln3_b")]
                           + [jnp.broadcast_to(loss[:, 0:1], (1, D_MODEL))], axis=0)
    late = dict(ssd=dcwb_s, heads=small, rows=rows)
    hooks.small(late)
    raw.update(late)
    dwt = [host(_mm, pc, xb, "tn", tm=512, tn=1024, out_dtype=BF16, name="d_w_in_%d" % q)
           for q, pc in enumerate(pieces)]
    grad("w_in", jnp.concatenate(dwt, axis=0))
    grad_x = host(_mm_pieces, pieces, offsets, w["w_in_t"], tm=256, extra=dt1, epi=lambda acc, e: acc + ALPHA * e,
                  name="d_x")
    return loss[0, 0], grad_x, g, raw


ANY_SPEC = pl.BlockSpec(memory_space=pl.ANY)


def _mesh_pos():
    return lax.axis_index("x"), lax.axis_index("y"), lax.axis_index("c")


def _remote(src, dst, send, recv, k, to):
    return pltpu.make_async_remote_copy(src_ref=src, dst_ref=dst, send_sem=send.at[k], recv_sem=recv.at[k],
                                        device_id=to, device_id_type=MESH_T)


class _Job:
    N_SEM = 7

    def __init__(self, kind, inp):
        self.kind, self.inp = kind, inp
        shape = {"gather": (N_DEV,) + inp.shape, "pair": (4,) + inp.shape[1:], "chip": inp.shape}[kind]
        self.out = jax.ShapeDtypeStruct(shape, inp.dtype)

    def _places(self):
        x, y, c = _mesh_pos()
        return (x, y, c), (x, y, 1 - c), [(1 - x, y), (x, 1 - y), (1 - x, 1 - y)]

    def start(self, inp, out, send, recv, loc):
        me, sibling, chips = self._places()
        x, y, c = me
        if self.kind == "gather":
            mine = out.at[4 * x + 2 * y + c]
            pltpu.make_async_copy(inp, mine, loc.at[0]).start()
            _remote(inp, mine, send, recv, 0, sibling).start()
            for j, chip in enumerate(chips):
                _remote(inp, mine, send, recv, 1 + j, (*chip, c)).start()
        elif self.kind == "pair":
            for k in range(4):
                _remote(inp.at[2 * k + (1 - c)], out.at[k], send, recv, k, sibling).start()
        else:
            kme = 2 * x + y
            pltpu.make_async_copy(inp.at[kme], out.at[kme], loc.at[0]).start()
            for j, (tx, ty) in enumerate(chips):
                _remote(inp.at[2 * tx + ty], out.at[kme], send, recv, j, (tx, ty, c)).start()

    def mid(self, inp, out, send, recv, loc):
        if self.kind != "gather":
            return
        me, sibling, chips = self._places()
        c = me[2]
        for j, chip in enumerate(chips):
            landed = out.at[4 * chip[0] + 2 * chip[1] + c]
            _remote(landed, landed, send, recv, 1 + j, me).wait_recv()
            _remote(landed, landed, send, recv, 4 + j, sibling).start()

    def finish(self, inp, out, send, recv, loc):
        me, sibling, chips = self._places()
        x, y, c = me
        if self.kind == "gather":
            blk = lambda px, py, pc: out.at[4 * px + 2 * py + pc]
            mine = blk(*me)
            _remote(inp, blk(*sibling), send, recv, 0, me).wait_recv()
            for j, chip in enumerate(chips):
                _remote(inp, blk(*chip, 1 - c), send, recv, 4 + j, me).wait_recv()
            for k in range(7):
                _remote(inp, mine, send, recv, k, sibling).wait_send()
            pltpu.make_async_copy(inp, mine, loc.at[0]).wait()
        elif self.kind == "pair":
            for k in range(4):
                _remote(inp.at[2 * k + (1 - c)], out.at[k], send, recv, k, sibling).wait()
        else:
            kme = 2 * x + y
            for j, (tx, ty) in enumerate(chips):
                _remote(inp.at[kme], out.at[2 * tx + ty], send, recv, j, (tx, ty, c)).wait_recv()
            for j, (tx, ty) in enumerate(chips):
                _remote(inp.at[2 * tx + ty], out.at[kme], send, recv, j, (tx, ty, c)).wait_send()
            pltpu.make_async_copy(inp.at[kme], out.at[kme], loc.at[0]).wait()


def _job_scratch(jobs):
    sem = pltpu.SemaphoreType.DMA
    return [s for _ in jobs for s in (sem((_Job.N_SEM,)), sem((_Job.N_SEM,)), sem((1,)))]


def _run_jobs(jobs, method, jins, jouts, jsems):
    for q, job in enumerate(jobs):
        getattr(job, method)(jins[q], jouts[q], *jsems[3 * q:3 * q + 3])


def _exchange(jobs, *, name):
    n = len(jobs)

    def body(*refs):
        jins, jouts, jsems = refs[:n], refs[n:2 * n], refs[2 * n:]
        _run_jobs(jobs, "start", jins, jouts, jsems)
        _run_jobs(jobs, "mid", jins, jouts, jsems)
        _run_jobs(jobs, "finish", jins, jouts, jsems)

    return _pcall(body, in_specs=[ANY_SPEC] * n, out_specs=[ANY_SPEC] * n, out_shape=[j.out for j in jobs],
                  scratch_shapes=_job_scratch(jobs), name=name)(*[j.inp for j in jobs])


def _hosted(body, jobs, *, grid, in_specs, out_specs, out_shape, args, name, scratch_shapes=(), aliases=None):
    in_specs, out_specs, out_shape = list(in_specs), list(out_specs), list(out_shape)
    scratch_shapes = list(scratch_shapes)
    n_in, n_out, n_scr, nj = len(in_specs), len(out_specs), len(scratch_shapes), len(jobs)
    sem = ("arbitrary",) * len(grid)
    kw = dict(input_output_aliases=aliases) if aliases else {}
    if not jobs:
        res = _pcall(body, grid=grid, in_specs=in_specs, out_specs=out_specs, out_shape=out_shape,
                     scratch_shapes=scratch_shapes, name=name, compiler_params=_cparams(sem), **kw)(*args)
        return list(res), []

    def full(*refs):
        ins, jins = refs[:n_in], refs[n_in:n_in + nj]
        o0 = n_in + nj
        outs, jouts = refs[o0:o0 + n_out], refs[o0 + n_out:o0 + n_out + nj]
        s0 = o0 + n_out + nj
        scr, jsems = refs[s0:s0 + n_scr], refs[s0 + n_scr:]
        step = pl.program_id(0)
        for ax in range(1, len(grid)):
            step = step * grid[ax] + pl.program_id(ax)
        total = math.prod(grid)
        mid_step = (3 * total) // 4

        @pl.when(step == 0)
        def _():
            _run_jobs(jobs, "start", jins, jouts, jsems)

        if 0 < mid_step < total - 1:
            @pl.when(step == mid_step)
            def _():
                _run_jobs(jobs, "mid", jins, jouts, jsems)

        body(*ins, *outs, *scr)

        @pl.when(step == total - 1)
        def _():
            if not 0 < mid_step < total - 1:
                _run_jobs(jobs, "mid", jins, jouts, jsems)
            _run_jobs(jobs, "finish", jins, jouts, jsems)

    res = _pcall(full, grid=grid, in_specs=in_specs + [ANY_SPEC] * nj, out_specs=out_specs + [ANY_SPEC] * nj,
                 out_shape=out_shape + [j.out for j in jobs], scratch_shapes=scratch_shapes + _job_scratch(jobs),
                 name=name, compiler_params=_cparams(sem), **kw)(*args, *[j.inp for j in jobs])
    return list(res[:n_out]), list(res[n_out:])


def _pair_add(g8, r4, cidx, *, name):
    _, r, c = g8.shape
    tr = ROW_TILE if r % ROW_TILE == 0 else r

    def body(c_ref, g_ref, r_ref, o_ref):
        o_ref[...] = (g_ref[...].astype(F32) + r_ref[...].astype(F32)).astype(BF16)

    return _pcall(
        body,
        grid_spec=pltpu.PrefetchScalarGridSpec(
            num_scalar_prefetch=1, grid=(4, r // tr),
            in_specs=[pl.BlockSpec((None, tr, c), lambda k, i, cr: (2 * k + cr[0], i, 0)),
                      pl.BlockSpec((None, tr, c), lambda k, i, cr: (k, i, 0))],
            out_specs=pl.BlockSpec((None, tr, c), lambda k, i, cr: (k, i, 0))),
        out_shape=jax.ShapeDtypeStruct((4, r, c), BF16), name=name,
        compiler_params=_cparams(("parallel", "parallel")))(cidx, g8, r4)


def _adam_update(g, w_ref, m_ref, v_ref, g_ref, d_ref, mo_ref, vo_ref):
    c1 = 1.0 - ADAM_B1 ** ADAM_STEP
    c2 = 1.0 - ADAM_B2 ** ADAM_STEP
    m2 = ADAM_B1 * m_ref[...] + (1.0 - ADAM_B1) * g
    v2 = ADAM_B2 * v_ref[...] + (1.0 - ADAM_B2) * (g * g)
    g_ref[...] = g
    mo_ref[...] = m2
    vo_ref[...] = v2
    d_ref[...] = -ADAM_LR * ((m2 / c1) / (jnp.sqrt(v2 / c2) + ADAM_EPS) + ADAM_WD * w_ref[...])


def _adamw_rows(srcs, items, own_cols, me1, *, name):
    ns, ni, no = len(srcs), len(items), len(own_cols)
    full = lambda a: pl.BlockSpec(a.shape, lambda i, me: (0,) * a.ndim)
    in_specs = [full(a) for a in srcs]
    args = list(srcs)
    for (si, _r0, w, _m, _v) in own_cols:
        a = srcs[si]
        in_specs.append(pl.BlockSpec((N_DEV, a.shape[1], w.shape[1]), lambda i, me: (0, 0, me[0])))
        args.append(a)
    out_specs, out_shape = [], []
    for (_si, _r0, w, m, v) in list(items) + list(own_cols):
        in_specs += [full(w)] * 3
        args += [w, m, v]
        out_specs += [full(w)] * 4
        out_shape += [jax.ShapeDtypeStruct(w.shape, F32)] * 4

    def body(me_ref, *refs):
        src_refs, own_refs = refs[:ns], refs[ns:ns + no]
        wmv = refs[ns + no:ns + no + 3 * (ni + no)]
        outs = refs[ns + no + 3 * (ni + no):]
        for q, (si, r0, w, _m, _v) in enumerate(list(items) + list(own_cols)):
            nr, cw = w.shape
            gref = src_refs[si] if q < ni else own_refs[q - ni]
            g = gref[0, r0:r0 + nr, 0:cw]
            for d in range(1, N_DEV):
                g = g + gref[d, r0:r0 + nr, 0:cw]
            _adam_update(g, *wmv[3 * q:3 * q + 3], *outs[4 * q:4 * q + 4])

    res = _pcall(
        body,
        grid_spec=pltpu.PrefetchScalarGridSpec(num_scalar_prefetch=1, grid=(1,), in_specs=in_specs, out_specs=out_specs),
        out_shape=out_shape, name=name, compiler_params=_cparams(("arbitrary",)))(me1, *args)
    return [tuple(res[4 * q:4 * q + 4]) for q in range(ni + no)]


def _adamw(gsrc, w, m, v, *, name):
    k, r, c = gsrc.shape
    tr = ROW_TILE if r % ROW_TILE == 0 else r

    def body(gs_ref, w_ref, m_ref, v_ref, g_ref, d_ref, mo_ref, vo_ref):
        g = gs_ref[0].astype(F32)
        for q in range(1, k):
            g = g + gs_ref[q].astype(F32)
        _adam_update(g, w_ref, m_ref, v_ref, g_ref, d_ref, mo_ref, vo_ref)

    tc = c
    if tr == r and r > ROW_TILE and c % 256 == 0:
        tc = 256
    blk = pl.BlockSpec((tr, tc), lambda i, j: (i, j))
    sd = jax.ShapeDtypeStruct((r, c), F32)
    return _pcall(body, grid=(r // tr, c // tc),
                  in_specs=[pl.BlockSpec((k, tr, tc), lambda i, j: (0, i, j)), blk, blk, blk],
                  out_specs=(blk, blk, blk, blk), out_shape=(sd, sd, sd, sd), name=name,
                  compiler_params=_cparams(("parallel", "parallel")))(gsrc, w, m, v)


WEIGHTS = ['w_in', 'lru_conv_w', 'lru_conv_b', 'lru_gate_a_w', 'lru_gate_a_b', 'lru_gate_x_w', 'lru_gate_x_b',
           'lru_a_param', 'ssd_conv_w', 'ssd_conv_b', 'ssd_dt_bias', 'ssd_a_log', 'ssd_d', 'ssd_norm_w', 'w_out',
           'ln1_g', 'ln1_b', 'w_ff1', 'w_ff2', 'ln2_g', 'ln2_b', 'w_ple_gate', 'w_ple', 'ln3_g', 'ln3_b']
BIG = ['w_in', 'w_out', 'w_ff1', 'w_ff2', 'w_ple_gate', 'w_ple']
COL_SHARDED = ('w_ff1', 'w_ple')
CONV = ['lru_conv_w', 'ssd_conv_w']
REPL = [n for n in WEIGHTS if n not in BIG and n not in CONV]
CONV_CH = {'lru_conv_w': LRU_W, 'ssd_conv_w': XBC}


def _to_dest_major(name, gfull):
    if name == 'w_in':
        gfull = gfull[:D_IN]
    if name in COL_SHARDED:
        r, cfull = gfull.shape
        return gfull.reshape(r, N_DEV, cfull // N_DEV).transpose(1, 0, 2)
    rfull, cdim = gfull.shape
    return gfull.reshape(N_DEV, rfull // N_DEV, cdim)


def _full_weight(name, gathered):
    if name in COL_SHARDED:
        _, r, cs = gathered.shape
        full = gathered.transpose(1, 0, 2).reshape(r, N_DEV * cs)
    else:
        _, rs, cdim = gathered.shape
        full = gathered.reshape(N_DEV * rs, cdim)
    if name == 'w_in':
        full = jnp.concatenate([full, jnp.zeros((D_IN_PAD - D_IN, D_MODEL), full.dtype)], axis=0)
    return full


SMALL_SRC = ("lru", "ssd", "heads", "rows", "gate_a", "gate_x")
AG_HOSTS = {"in_proj": ("w_ff1",), "lru_fwd": ("w_out", "w_ple_gate", "w_ple"), "ssd_fwd": ("w_ff2",)}
PAIR_HOSTS = ("d_x2", "d_pre", "d_x1", "d_ycat")
CHIP_HOSTS = {"lru_bwd": ("w_ple_gate", "w_ple", "w_ff2"), "ssd_bwd": ("w_ff1",), "d_x": ("w_out",)}
SMALL_HOSTS = {"ssd_conv_bwd": ("lru", "gate_a", "gate_x"), "d_w_in_3": ("ssd", "heads", "rows")}


class _Schedule:
    def __init__(self, shards, cidx):
        self.shards, self.cidx = shards, cidx
        self.pair, self.chip, self.small_jobs = [], [], []
        self.dest, self.summed, self.gathered_small = {}, {}, {}
        self.tags = []

    def ride(self, host):
        tags = []
        if host in AG_HOSTS:
            tags = [("weight", n, self.shards[n]) for n in AG_HOSTS[host]]
        elif host in PAIR_HOSTS or host in CHIP_HOSTS or host == "flush":
            tags = [("pair", n, a) for n, a in self.pair]
            self.pair = []
            if host not in PAIR_HOSTS:
                take = [t for t in self.chip if host == "flush" or t[0] in CHIP_HOSTS[host]]
                tags += [("chip", n, a) for n, a in take]
                self.chip = [t for t in self.chip if not any(t is u for u in take)]
        if host in SMALL_HOSTS:
            tags += [("small", n, a) for n, a in self.small_jobs if n in SMALL_HOSTS[host]]
            self.small_jobs = [t for t in self.small_jobs if t[0] not in SMALL_HOSTS[host]]
        self.tags = tags
        return [_Job({"weight": "gather", "small": "gather"}.get(kind, kind), a) for kind, _n, a in tags]

    def done(self, jobs, outs, w):
        for (kind, n, _a), o in zip(self.tags, outs):
            if kind == "weight":
                w[n] = _full_weight(n, o)
            elif kind == "small":
                self.gathered_small[n] = o
            elif kind == "pair":
                self.chip.append((n, _pair_add(self.dest[n], o, self.cidx, name="rs_pair_add_" + n)))
            else:
                self.summed[n] = o

    def grad(self, name, val):
        self.dest[name] = val if val.ndim == 3 else _to_dest_major(name, val)
        self.pair.append((name, self.dest[name]))

    def small(self, raw):
        self.small_jobs += list(raw.items())

    def flush(self):
        step = 0
        while self.pair or self.chip:
            jobs = self.ride("flush")
            self.done(jobs, _exchange(jobs, name="rs_flush_%d" % step), None)
            step += 1


def kernel(x, p, w_in, lru_conv_w, lru_conv_b, lru_gate_a_w, lru_gate_a_b, lru_gate_x_w, lru_gate_x_b, lru_a_param, ssd_conv_w, ssd_conv_b, ssd_dt_bias, ssd_a_log, ssd_d, ssd_norm_w, w_out, ln1_g, ln1_b, w_ff1, w_ff2, ln2_g, ln2_b, w_ple_gate, w_ple, ln3_g, ln3_b, loss_target, m_w_in, m_lru_conv_w, m_lru_conv_b, m_lru_gate_a_w, m_lru_gate_a_b, m_lru_gate_x_w, m_lru_gate_x_b, m_lru_a_param, m_ssd_conv_w, m_ssd_conv_b, m_ssd_dt_bias, m_ssd_a_log, m_ssd_d, m_ssd_norm_w, m_w_out, m_ln1_g, m_ln1_b, m_w_ff1, m_w_ff2, m_ln2_g, m_ln2_b, m_w_ple_gate, m_w_ple, m_ln3_g, m_ln3_b, v_w_in, v_lru_conv_w, v_lru_conv_b, v_lru_gate_a_w, v_lru_gate_a_b, v_lru_gate_x_w, v_lru_gate_x_b, v_lru_a_param, v_ssd_conv_w, v_ssd_conv_b, v_ssd_dt_bias, v_ssd_a_log, v_ssd_d, v_ssd_norm_w, v_w_out, v_ln1_g, v_ln1_b, v_w_ff1, v_w_ff2, v_ln2_g, v_ln2_b, v_w_ple_gate, v_w_ple, v_ln3_g, v_ln3_b):
    given = dict(locals())
    def local(a, n):
        return jnp.swapaxes(a[0], 0, 1) if n == 'w_in' else a[0]

    wsh = {n: local(given[n], n) for n in WEIGHTS}
    msh = {n: local(given["m_" + n], n) for n in WEIGHTS}
    vsh = {n: local(given["v_" + n], n) for n in WEIGHTS}
    xi, yi, ci = _mesh_pos()
    me = 4 * xi + 2 * yi + ci

    shards = {n: wsh[n].astype(BF16) for n in BIG}
    conv_pack = jnp.concatenate([_pad_rows8(wsh[n]) for n in CONV], axis=1)
    g_in, gconv = _exchange([_Job("gather", shards['w_in']), _Job("gather", conv_pack)], name="ag_first")
    full = {'w_in_t': _full_weight('w_in', g_in)}
    c0 = 0
    for n in CONV:
        cw = CONV_CH[n] // N_DEV
        full[n] = gconv[:, :4, c0:c0 + cw].transpose(1, 0, 2).reshape(4, CONV_CH[n])
        c0 += cw
    for n in REPL:
        full[n] = given[n] if given[n].ndim == 2 else wsh[n]

    sched = _Schedule(shards, jnp.reshape(ci, (1,)).astype(jnp.int32))
    loss_local, grad_x, g, raw = _local_step(x[0], p[0, 0], loss_target[0], full, sched)
    sched.flush()
    summed, gat = sched.summed, sched.gathered_small
    loss = gat["rows"][0, 7, 0]
    for d in range(1, N_DEV):
        loss = loss + gat["rows"][d, 7, 0]

    outs = {}
    for n in BIG:
        outs[n] = _adamw(summed[n], wsh[n], msh[n], vsh[n], name="adamw_" + n)
    for n, k in (("lru_gate_a_w", "gate_a"), ("lru_gate_x_w", "gate_x")):
        flat = lambda a: a.reshape(N_HEAD * HEAD_P, HEAD_P)
        res = _adamw(gat[k], flat(wsh[n]), flat(msh[n]), flat(vsh[n]), name="adamw_" + n)
        outs[n] = tuple(r.reshape(N_HEAD, HEAD_P, HEAD_P) for r in res)
    row_items = [("lru_conv_b", 0, 4), ("lru_gate_a_b", 0, 5), ("lru_gate_x_b", 0, 6), ("lru_a_param", 0, 7),
                 ("ssd_conv_b", 1, 4), ("ssd_dt_bias", 2, 0), ("ssd_a_log", 2, 1), ("ssd_d", 2, 2),
                 ("ssd_norm_w", 3, 0), ("ln1_g", 3, 1), ("ln1_b", 3, 2), ("ln2_g", 3, 3), ("ln2_b", 3, 4),
                 ("ln3_g", 3, 5), ("ln3_b", 3, 6)]
    vec = lambda a: a.reshape(1, -1)
    items = [(si, r0, vec(given[n]), vec(given["m_" + n]), vec(given["v_" + n])) for n, si, r0 in row_items]
    own = [(si, 0, wsh[n], msh[n], vsh[n]) for n, si in (("lru_conv_w", 0), ("ssd_conv_w", 1))]
    me1 = jnp.reshape(me, (1,)).astype(jnp.int32)
    res = _adamw_rows([gat[k] for k in SMALL_SRC[:4]], items, own, me1, name="adamw_small")
    for (n, _si, _r0), r4 in zip(row_items, res[:len(row_items)]):
        outs[n] = r4
    for n, r4 in zip(CONV, res[len(row_items):]):
        outs[n] = r4

    def fin(n, k):
        a = jnp.swapaxes(outs[n][k], 0, 1) if n == 'w_in' else outs[n][k]
        return a.reshape(given[n].shape)

    return (loss, grad_x[None],
            *[fin(n, 0) for n in WEIGHTS], *[fin(n, 1) for n in WEIGHTS],
            *[fin(n, 2) for n in WEIGHTS], *[fin(n, 3) for n in WEIGHTS])
```

```python
import math

import jax
import jax.numpy as jnp
from jax import lax
from jax.experimental import pallas as pl
from jax.experimental.pallas import tpu as pltpu

F32 = jnp.float32
BF16 = jnp.bfloat16
HI = lax.Precision.HIGHEST

N_DEV = 8
D_MODEL = 1024
LRU_W = 1024
SSD_W = 1024
XBC = 2048
N_HEAD = 16
HEAD_P = 64
N_GROUP = 4
GROUP_W = 256
N_STATE = 128
CHUNK = 128
D_FF = 4096
PLE_DIM = 256
D_IN = 5136
D_IN_PAD = 5632
COL_G = 1024
COL_Z = 2048
COL_XBC = 3072
COL_DT = 5120
LRU_C = 8.0
ALPHA = 2.0 ** 0.25
LN_EPS = 1e-5
RMS_EPS = 1e-5
ADAM_LR = 0.001
ADAM_B1 = 0.9
ADAM_B2 = 0.999
ADAM_EPS = 1e-08
ADAM_WD = 0.01
ADAM_STEP = 10
GELU_C = math.sqrt(2.0 / math.pi)
LANE = 128
SUBLANE = 8
VMEM_LIMIT = 48 * 1024 * 1024
MESH_T = pl.DeviceIdType.MESH
NEG_BIG = -1e30


def _pcall(body, **kw):
    return pl.pallas_call(body, **kw)


def _cparams(sem):
    return pltpu.CompilerParams(dimension_semantics=sem, vmem_limit_bytes=VMEM_LIMIT)


def _dot(a, b):
    return jnp.dot(a.astype(BF16), b.astype(BF16), preferred_element_type=F32)


def _dot_nt(a, b):
    return lax.dot_general(a.astype(BF16), b.astype(BF16), (((1,), (1,)), ((), ())), preferred_element_type=F32)


def _dot_tn(a, b):
    return lax.dot_general(a.astype(BF16), b.astype(BF16), (((0,), (0,)), ((), ())), preferred_element_type=F32)


def _dotx(a, b):
    return jnp.dot(a, b, precision=HI, preferred_element_type=F32)


def _sigmoid(x):
    return jax.nn.sigmoid(x)


def _softplus(v):
    return jnp.maximum(v, 0.0) + jnp.log1p(jnp.exp(-jnp.abs(v)))


def _gelu(x):
    th = jnp.tanh(GELU_C * (x + 0.044715 * x * x * x))
    return 0.5 * x * (1.0 + th), th


def _gelu_grad(x, th):
    return 0.5 * (1.0 + th) + 0.5 * x * (1.0 - th * th) * GELU_C * (1.0 + 3.0 * 0.044715 * x * x)


def _iota(shape, dim):
    return lax.broadcasted_iota(jnp.int32, shape, dim)


def _mm(a, b, mode, *, tm, tn, name, a_fn=None, extra=None, epi=None, out_dtype=F32, dest_major=False, jobs=()):
    m = a.shape[1] if mode == "tn" else a.shape[0]
    n = b.shape[0] if mode == "nt" else b.shape[1]
    tm, tn = min(tm, m), min(tn, n)
    if dest_major:
        tn = n // N_DEV
    if mode == "nn":
        m, k = a.shape
        _, n = b.shape
        a_spec = pl.BlockSpec((tm, k), lambda i, j: (i, 0))
        b_spec = pl.BlockSpec((k, tn), lambda i, j: (0, j))
        dims = ((1,), (0,))
    elif mode == "nt":
        m, k = a.shape
        n, _ = b.shape
        a_spec = pl.BlockSpec((tm, k), lambda i, j: (i, 0))
        b_spec = pl.BlockSpec((tn, k), lambda i, j: (j, 0))
        dims = ((1,), (1,))
    else:
        k, m = a.shape
        _, n = b.shape
        a_spec = pl.BlockSpec((k, tm), lambda i, j: (0, i))
        b_spec = pl.BlockSpec((k, tn), lambda i, j: (0, j))
        dims = ((0,), (0,))
    assert m % tm == 0 and n % tn == 0, (name, m, n, tm, tn)
    o_spec = pl.BlockSpec((tm, tn), lambda i, j: (i, j))
    in_specs = [a_spec, b_spec]
    args = [a, b]
    if extra is not None:
        in_specs.append(o_spec)
        args.append(extra)

    def body(*refs):
        a_ref, b_ref, o_ref = refs[0], refs[1], refs[-1]
        av = a_ref[...]
        if a_fn is not None:
            av = a_fn(av)
        acc = lax.dot_general(av.astype(BF16), b_ref[...].astype(BF16), (dims, ((), ())), preferred_element_type=F32)
        if epi is not None:
            acc = epi(acc, refs[2][...])
        o_ref[...] = acc.astype(out_dtype)

    out_shape = jax.ShapeDtypeStruct((m, n), out_dtype)
    if dest_major:
        assert extra is None
        o_spec = pl.BlockSpec((None, tm, tn), lambda i, j: (j, i, 0))
        out_shape = jax.ShapeDtypeStruct((N_DEV, m, tn), out_dtype)
    (out,), jouts = _hosted(body, jobs, grid=(m // tm, n // tn), in_specs=in_specs, out_specs=[o_spec],
                            out_shape=[out_shape], args=args, name=name)
    return (out, jouts) if jobs else out


def _mm_pieces(pieces, offsets, b, *, tm, name, extra, epi, jobs=()):
    m = pieces[0].shape[0]
    kb, n = b.shape
    tm = min(tm, m)
    row = lambda wdt: pl.BlockSpec((tm, wdt), lambda i: (i, 0))
    in_specs = [row(pc.shape[1]) for pc in pieces] + [pl.BlockSpec((kb, n), lambda i: (0, 0)), row(n)]
    np_ = len(pieces)

    def body(*refs):
        b_ref, e_ref, o_ref = refs[np_], refs[np_ + 1], refs[np_ + 2]
        acc = jnp.zeros((tm, n), F32)
        for q in range(np_):
            kq = pieces[q].shape[1]
            acc = acc + jnp.dot(refs[q][...].astype(BF16), b_ref[offsets[q]:offsets[q] + kq, :].astype(BF16),
                                preferred_element_type=F32)
        o_ref[...] = epi(acc, e_ref[...])

    (out,), jouts = _hosted(body, jobs, grid=(m // tm,), in_specs=in_specs, out_specs=[row(n)],
                            out_shape=[jax.ShapeDtypeStruct((m, n), F32)], args=list(pieces) + [b, extra], name=name)
    return (out, jouts) if jobs else out


def _relu2(v):
    r = jnp.maximum(v, 0.0)
    return r * r


ROW_TILE = 256


def _ln_stats(t):
    mu = jnp.mean(t, axis=-1, keepdims=True)
    xc = t - mu
    var = jnp.mean(xc * xc, axis=-1, keepdims=True)
    rstd = lax.rsqrt(var + LN_EPS)
    return xc * rstd, rstd


def _ln_bwd_rows(dy, xhat, rstd, g):
    dxh = dy * g
    m1 = jnp.mean(dxh, axis=-1, keepdims=True)
    m2 = jnp.mean(dxh * xhat, axis=-1, keepdims=True)
    return rstd * (dxh - m1 - xhat * m2)


def _ln_fwd(a, b, g, beta, *, name):
    s, d = a.shape
    row = pl.BlockSpec((ROW_TILE, d), lambda i: (i, 0))
    par = pl.BlockSpec((1, d), lambda i: (0, 0))

    def body(a_ref, b_ref, g_ref, be_ref, y_ref, yb_ref):
        xhat, _ = _ln_stats(ALPHA * a_ref[...] + b_ref[...])
        y = xhat * g_ref[...] + be_ref[...]
        y_ref[...] = y
        yb_ref[...] = y.astype(BF16)

    return _pcall(body, grid=(s // ROW_TILE,), in_specs=[row, row, par, par], out_specs=(row, row),
                  out_shape=(jax.ShapeDtypeStruct((s, d), F32), jax.ShapeDtypeStruct((s, d), BF16)), name=name,
                  compiler_params=_cparams(("parallel",)))(a, b, g, beta)


def _ln_bwd(a, b, g, dys, coefs, *, name):
    s, d = a.shape
    row = pl.BlockSpec((ROW_TILE, d), lambda i: (i, 0))
    par = pl.BlockSpec((1, d), lambda i: (0, 0))
    n = len(dys)

    def body(*refs):
        a_ref, b_ref, g_ref = refs[:3]
        dy_refs = refs[3:3 + n]
        dt_ref, dtb_ref, dg_ref, db_ref = refs[3 + n:]
        xhat, rstd = _ln_stats(ALPHA * a_ref[...] + b_ref[...])
        dy = coefs[0] * dy_refs[0][...]
        for q in range(1, n):
            dy = dy + coefs[q] * dy_refs[q][...]
        dt = _ln_bwd_rows(dy, xhat, rstd, g_ref[...])
        dt_ref[...] = dt
        dtb_ref[...] = dt.astype(BF16)

        @pl.when(pl.program_id(0) == 0)
        def _():
            dg_ref[...] = jnp.zeros_like(dg_ref)
            db_ref[...] = jnp.zeros_like(db_ref)

        dg_ref[...] += jnp.sum(dy * xhat, axis=0, keepdims=True)
        db_ref[...] += jnp.sum(dy, axis=0, keepdims=True)

    return _pcall(body, grid=(s // ROW_TILE,), in_specs=[row, row, par] + [row] * n, out_specs=(row, row, par, par),
                  out_shape=(jax.ShapeDtypeStruct((s, d), F32), jax.ShapeDtypeStruct((s, d), BF16),
                             jax.ShapeDtypeStruct((1, d), F32), jax.ShapeDtypeStruct((1, d), F32)),
                  name=name, compiler_params=_cparams(("arbitrary",)))(a, b, g, *dys)


def _head(x2, gpre, ple, g, beta, tgt, *, name):
    s, d = x2.shape
    row = pl.BlockSpec((ROW_TILE, d), lambda i: (i, 0))
    par = pl.BlockSpec((1, d), lambda i: (0, 0))
    lsp = pl.BlockSpec((1, LANE), lambda i: (0, 0))

    def body(x2_ref, gp_ref, ple_ref, g_ref, be_ref, t_ref, loss_ref, dgp_ref, dple_ref, dt_ref, dg_ref, db_ref):
        gate = _sigmoid(gp_ref[...])
        ple_v = ple_ref[...]
        xhat, rstd = _ln_stats(ALPHA * x2_ref[...] + gate * ple_v)
        err = xhat * g_ref[...] + be_ref[...] - t_ref[...]
        dy = err * (1.0 / d)
        dt = _ln_bwd_rows(dy, xhat, rstd, g_ref[...])
        dt_ref[...] = dt
        dgp_ref[...] = (dt * ple_v * gate * (1.0 - gate)).astype(BF16)
        dple_ref[...] = (dt * gate).astype(BF16)

        @pl.when(pl.program_id(0) == 0)
        def _():
            loss_ref[...] = jnp.zeros_like(loss_ref)
            dg_ref[...] = jnp.zeros_like(dg_ref)
            db_ref[...] = jnp.zeros_like(db_ref)

        loss_ref[...] += 0.5 * jnp.sum(jnp.mean(err * err, axis=-1, keepdims=True))
        dg_ref[...] += jnp.sum(dy * xhat, axis=0, keepdims=True)
        db_ref[...] += jnp.sum(dy, axis=0, keepdims=True)

    sd = jax.ShapeDtypeStruct((s, d), F32)
    sb = jax.ShapeDtypeStruct((s, d), BF16)
    pd = jax.ShapeDtypeStruct((1, d), F32)
    return _pcall(body, grid=(s // ROW_TILE,), in_specs=[row, row, row, par, par, row],
                  out_specs=(lsp, row, row, row, par, par),
                  out_shape=(jax.ShapeDtypeStruct((1, LANE), F32), sb, sb, sd, pd, pd),
                  name=name, compiler_params=_cparams(("arbitrary",)))(x2, gpre, ple, g, beta, tgt)


CONV_R = 256
PAD = SUBLANE


def _shift_down(ext, s):
    if s == 0:
        return ext[PAD:, :]
    return pltpu.roll(ext, s, 0)[PAD:, :]


def _shift_up(ext, s):
    r = ext.shape[0] - PAD
    if s == 0:
        return ext[:r, :]
    return pltpu.roll(ext, r + PAD - s, 0)[:r, :]


def _conv_rows(xpad_ref, r0, w_ref):
    ext = xpad_ref[pl.ds(r0, CONV_R + PAD), :]
    acc = _shift_down(ext, 0) * w_ref[3:4, :]
    for k in range(3):
        acc = acc + _shift_down(ext, 3 - k) * w_ref[k:k + 1, :]
    return acc, ext


def _fill_front_padded(dst_ref, src_ref, s):
    dst_ref[0:PAD, :] = jnp.zeros((PAD, dst_ref.shape[1]), F32)

    def cp(q, _):
        r0 = pl.multiple_of(q * CONV_R, CONV_R)
        dst_ref[pl.ds(pl.multiple_of(PAD + r0, PAD), CONV_R), :] = src_ref[pl.ds(r0, CONV_R), :]
        return 0

    lax.fori_loop(0, s // CONV_R, cp, 0)


def _conv_silu_fwd(proj, w8, b, *, col0, width, ct, name):
    s = proj.shape[0]
    nb = col0 // ct

    def body(x_ref, w_ref, b_ref, o_ref, xpad):
        _fill_front_padded(xpad, x_ref, s)

        def step(q, _):
            r0 = pl.multiple_of(q * CONV_R, CONV_R)
            acc, _e = _conv_rows(xpad, r0, w_ref)
            pre = acc + b_ref[...]
            o_ref[pl.ds(r0, CONV_R), :] = pre * _sigmoid(pre)
            return 0

        lax.fori_loop(0, s // CONV_R, step, 0)

    return _pcall(
        body, grid=(width // ct,),
        in_specs=[pl.BlockSpec((s, ct), lambda j: (0, nb + j)), pl.BlockSpec((SUBLANE, ct), lambda j: (0, j)),
                  pl.BlockSpec((1, ct), lambda j: (0, j))],
        out_specs=pl.BlockSpec((s, ct), lambda j: (0, j)),
        out_shape=jax.ShapeDtypeStruct((s, width), F32),
        scratch_shapes=[pltpu.VMEM((s + PAD, ct), F32)], name=name,
        compiler_params=_cparams(("parallel",)))(proj, w8, b)


def _conv_bwd_rows(dpad_ref, r0, w_ref):
    return _conv_bwd_ext(dpad_ref[pl.ds(r0, CONV_R + PAD), :], w_ref)


def _conv_bwd_ext(ext, w_ref):
    acc = _shift_up(ext, 0) * w_ref[3:4, :]
    for k in range(3):
        acc = acc + _shift_up(ext, 3 - k) * w_ref[k:k + 1, :]
    return acc


def _conv_silu_bwd(proj, dact, w8, b, *, col0, width, ct, name, jobs=()):
    s = proj.shape[0]
    nb = col0 // ct

    def body(x_ref, d_ref, w_ref, b_ref, dx_ref, dwb_ref, xpad, dpad):
        _fill_front_padded(xpad, x_ref, s)
        dpad[pl.ds(s, PAD), :] = jnp.zeros((PAD, ct), F32)
        dwb_ref[...] = jnp.zeros_like(dwb_ref)

        def step(q, _):
            r0 = pl.multiple_of(q * CONV_R, CONV_R)
            acc, ext = _conv_rows(xpad, r0, w_ref)
            pre = acc + b_ref[...]
            sg = _sigmoid(pre)
            dpre = d_ref[pl.ds(r0, CONV_R), :] * sg * (1.0 + pre * (1.0 - sg))
            dpad[pl.ds(r0, CONV_R), :] = dpre
            for k in range(4):
                dwb_ref[k:k + 1, :] += jnp.sum(dpre * _shift_down(ext, 3 - k), axis=0, keepdims=True)
            dwb_ref[4:5, :] += jnp.sum(dpre, axis=0, keepdims=True)
            return 0

        lax.fori_loop(0, s // CONV_R, step, 0)

        def step2(q, _):
            r0 = pl.multiple_of(q * CONV_R, CONV_R)
            dx_ref[pl.ds(r0, CONV_R), :] = _conv_bwd_rows(dpad, r0, w_ref).astype(BF16)
            return 0

        lax.fori_loop(0, s // CONV_R, step2, 0)

    colb = pl.BlockSpec((s, ct), lambda j: (0, j))
    outs, jouts = _hosted(
        body, jobs, grid=(width // ct,),
        in_specs=[pl.BlockSpec((s, ct), lambda j: (0, nb + j)), colb, pl.BlockSpec((SUBLANE, ct), lambda j: (0, j)),
                  pl.BlockSpec((1, ct), lambda j: (0, j))],
        out_specs=(colb, pl.BlockSpec((SUBLANE, ct), lambda j: (0, j))),
        out_shape=(jax.ShapeDtypeStruct((s, width), BF16), jax.ShapeDtypeStruct((SUBLANE, width), F32)),
        scratch_shapes=[pltpu.VMEM((s + PAD, ct), F32), pltpu.VMEM((s + PAD, ct), F32)], name=name,
        args=(proj, dact, w8, b))
    return (tuple(outs), jouts) if jobs else tuple(outs)


LRU_CT = 128


def _row_of(v, r):
    return jnp.sum(jnp.where(_iota((v.shape[0], 1), 0) == r, v, 0.0), axis=0, keepdims=True)


def _scan_fwd(a, u):
    r = a.shape[0]
    row = _iota((r, 1), 0)
    d = 1
    while d < r:
        valid = row >= d
        u = jnp.where(valid, a * pltpu.roll(u, d, 0) + u, u)
        a = jnp.where(valid, a * pltpu.roll(a, d, 0), a)
        d *= 2
    return a, u


def _scan_rev(b, u):
    r = b.shape[0]
    row = _iota((r, 1), 0)
    d = 1
    while d < r:
        valid = row < r - d
        u = jnp.where(valid, b * pltpu.roll(u, r - d, 0) + u, u)
        b = jnp.where(valid, b * pltpu.roll(b, r - d, 0), b)
        d *= 2
    return b, u


def _lru_chunk(xpad, r0, cw_ref, cb, wa, ba, wx, bx, sp):
    acc, ext = _conv_rows(xpad, r0, cw_ref)
    xl = acc + cb
    r = _sigmoid(_dot(xl, wa) + ba)
    i = _sigmoid(_dot(xl, wx) + bx)
    la = -LRU_C * r * sp
    a = jnp.exp(la)
    a2 = jnp.exp(2.0 * la)
    mult = jnp.sqrt(-jnp.tanh(la) * (a2 + 1.0))
    first = (r0 + _iota((CONV_R, 1), 0)) == 0
    mult = jnp.where(first, 1.0, mult)
    return ext, xl, r, i, a, a2, mult, first


def _lru_specs(s):
    ct = LRU_CT
    nb_g = COL_G // ct
    return dict(
        x=pl.BlockSpec((s, ct), lambda j: (0, j)),
        g=pl.BlockSpec((s, ct), lambda j: (0, nb_g + j)),
        col=pl.BlockSpec((s, ct), lambda j: (0, j)),
        cw=pl.BlockSpec((SUBLANE, ct), lambda j: (0, j)),
        vec=pl.BlockSpec((1, ct), lambda j: (0, j)),
        gate=pl.BlockSpec((None, ct, ct), lambda j: (j, 0, 0)),
    )


def _lru_fwd(proj, cw8, cb, wa_bd, ba, wx_bd, bx, ap, *, name, jobs=()):
    s = proj.shape[0]
    ct = LRU_CT
    sp_ = _lru_specs(s)

    def body(x_ref, g_ref, cw_ref, cb_ref, wa_ref, ba_ref, wx_ref, bx_ref, ap_ref, y_ref, h_ref, xpad):
        _fill_front_padded(xpad, x_ref, s)
        sp = _softplus(-ap_ref[...])

        def step(q, carry):
            r0 = pl.multiple_of(q * CONV_R, CONV_R)
            _e, xl, _r, i, a, _a2, mult, _f = _lru_chunk(xpad, r0, cw_ref, cb_ref[...], wa_ref[...], ba_ref[...],
                                                       wx_ref[...], bx_ref[...], sp)
            acum, ucum = _scan_fwd(a, xl * i * mult)
            h = acum * carry + ucum
            h_ref[pl.ds(r0, CONV_R), :] = h
            ge, _th = _gelu(g_ref[pl.ds(r0, CONV_R), :])
            y_ref[pl.ds(r0, CONV_R), :] = (ge * h).astype(BF16)
            return _row_of(h, CONV_R - 1)

        lax.fori_loop(0, s // CONV_R, step, jnp.zeros((1, ct), F32))

    (ymix, hs), jouts = _hosted(
        body, jobs, grid=(LRU_W // ct,),
        in_specs=[sp_["x"], sp_["g"], sp_["cw"], sp_["vec"], sp_["gate"], sp_["vec"], sp_["gate"], sp_["vec"], sp_["vec"]],
        out_specs=(sp_["col"], sp_["col"]),
        out_shape=(jax.ShapeDtypeStruct((s, LRU_W + SSD_W), BF16), jax.ShapeDtypeStruct((s, LRU_W), F32)),
        scratch_shapes=[pltpu.VMEM((s + PAD, ct), F32)],
        name=name, args=(proj, proj, cw8, cb, wa_bd, ba, wx_bd, bx, ap))
    return ((ymix, hs), jouts) if jobs else (ymix, hs)


def _lru_bwd(proj, dy, hs, cw8, cb, wa_bd, ba, wx_bd, bx, ap, *, name, jobs=()):
    s = proj.shape[0]
    ct = LRU_CT
    sp_ = _lru_specs(s)

    nq = s // CONV_R

    def body(x_ref, g_ref, dy_ref, h_ref, cw_ref, cb_ref, wa_ref, ba_ref, wx_ref, bx_ref, ap_ref,
             dx_ref, dg_ref, dcwb_ref, dwa_ref, dwx_ref, xpad, hpad):
        _fill_front_padded(xpad, x_ref, s)
        _fill_front_padded(hpad, h_ref, s)
        apv = ap_ref[...]
        sp = _softplus(-apv)
        cb_v, wa, ba_v, wx, bx_v = cb_ref[...], wa_ref[...], ba_ref[...], wx_ref[...], bx_ref[...]
        dcwb_ref[...] = jnp.zeros_like(dcwb_ref)
        dwa_ref[...] = jnp.zeros_like(dwa_ref)
        dwx_ref[...] = jnp.zeros_like(dwx_ref)

        def back(k, carry):
            g_next, a_next, dxl_next = carry
            last_row = _iota((CONV_R, 1), 0) == CONV_R - 1
            r0 = pl.multiple_of((nq - 1 - k) * CONV_R, CONV_R)
            ext, xl, r, i, a, a2, mult, first = _lru_chunk(xpad, r0, cw_ref, cb_v, wa, ba_v, wx, bx_v, sp)
            gv = g_ref[pl.ds(r0, CONV_R), :]
            dyv = dy_ref[pl.ds(r0, CONV_R), :]
            hext = hpad[pl.ds(r0, CONV_R + PAD), :]
            ge, th = _gelu(gv)
            dg_ref[pl.ds(r0, CONV_R), :] = (dyv * _shift_down(hext, 0) * _gelu_grad(gv, th)).astype(BF16)
            b = jnp.where(last_row, a_next, pltpu.roll(a, CONV_R - 1, 0))
            bcum, dcum = _scan_rev(b, dyv * ge)
            gval = dcum + bcum * g_next
            hprev = _shift_down(hext, 1)
            da = gval * hprev
            dxl = gval * i * mult
            di = gval * xl * mult
            dmult = jnp.where(first, 0.0, gval * xl * i)
            dla = da * a - dmult * a2 / mult
            dr = dla * (-LRU_C) * sp
            dcwb_ref[7:8, :] += jnp.sum(dla * (-LRU_C) * r, axis=0, keepdims=True)
            dpr = dr * r * (1.0 - r)
            dpi = di * i * (1.0 - i)
            dxl = dxl + _dot_nt(dpr, wa) + _dot_nt(dpi, wx)
            dwa_ref[...] += _dot_tn(xl, dpr)
            dwx_ref[...] += _dot_tn(xl, dpi)
            dcwb_ref[5:6, :] += jnp.sum(dpr, axis=0, keepdims=True)
            dcwb_ref[6:7, :] += jnp.sum(dpi, axis=0, keepdims=True)
            for tap in range(4):
                dcwb_ref[tap:tap + 1, :] += jnp.sum(dxl * _shift_down(ext, 3 - tap), axis=0, keepdims=True)
            dcwb_ref[4:5, :] += jnp.sum(dxl, axis=0, keepdims=True)
            dx_ref[pl.ds(r0, CONV_R), :] = _conv_bwd_ext(jnp.concatenate([dxl, dxl_next], axis=0), cw_ref).astype(BF16)
            return _row_of(gval, 0), _row_of(a, 0), dxl[:PAD, :]

        zero = jnp.zeros((1, ct), F32)
        lax.fori_loop(0, nq, back, (zero, zero, jnp.zeros((PAD, ct), F32)))
        dcwb_ref[7:8, :] = dcwb_ref[7:8, :] * (-_sigmoid(-apv))

    nt = LRU_W // ct
    outs, jouts = _hosted(
        body, jobs, grid=(nt,),
        in_specs=[sp_["x"], sp_["g"], sp_["col"], sp_["col"], sp_["cw"], sp_["vec"], sp_["gate"], sp_["vec"], sp_["gate"],
                  sp_["vec"], sp_["vec"]],
        out_specs=(sp_["col"], sp_["col"], sp_["cw"], sp_["gate"], sp_["gate"]),
        out_shape=(jax.ShapeDtypeStruct((s, LRU_W), BF16), jax.ShapeDtypeStruct((s, LRU_W), BF16),
                   jax.ShapeDtypeStruct((SUBLANE, LRU_W), F32), jax.ShapeDtypeStruct((nt, ct, ct), F32),
                   jax.ShapeDtypeStruct((nt, ct, ct), F32)),
        scratch_shapes=[pltpu.VMEM((s + PAD, ct), F32), pltpu.VMEM((s + PAD, ct), F32)],
        name=name, args=(proj, proj, dy, hs, cw8, cb, wa_bd, ba, wx_bd, bx, ap))
    return (tuple(outs), jouts) if jobs else tuple(outs)


def _split3(v):
    hi = v.astype(BF16)
    r1 = v - hi.astype(F32)
    mid = r1.astype(BF16)
    lo = (r1 - mid.astype(F32)).astype(BF16)
    return hi, mid, lo


def _dot01(m01, v):
    mb = m01.astype(BF16)
    hi, mid, lo = _split3(v)
    f = lambda part: jnp.dot(mb, part, preferred_element_type=F32)
    return f(hi) + f(mid) + f(lo)


def _dot01_r(v, m01):
    mb = m01.astype(BF16)
    hi, mid, lo = _split3(v)
    f = lambda part: jnp.dot(part, mb, preferred_element_type=F32)
    return f(hi) + f(mid) + f(lo)


def _ssd_prep(dtr, bias, alog_pad):
    l = CHUNK
    lane = _iota((1, LANE), 1)
    a_head = jnp.where(lane < N_HEAD, -jnp.exp(alog_pad), 0.0)
    dt = _softplus(dtr + bias)
    tril = (_iota((l, l), 1) <= _iota((l, l), 0)).astype(F32)
    a = dt * a_head
    cs = _dot01(tril, a)
    tot = jnp.sum(a, axis=0, keepdims=True)
    return dict(a_head=a_head, dt=dt, tril=tril, cs=cs, tot=tot)


def _col(v, h):
    lane = _iota(v.shape, 1)
    return jnp.sum(jnp.where(lane == h, v, 0.0), axis=1, keepdims=True)


def _decay_mat(cs, cst_ref, h, causal):
    row = cst_ref[h:h + 1, :]
    return jnp.exp(jnp.where(causal, _col(cs, h) - row, NEG_BIG))


def _head_mask(j, rows=CHUNK):
    lane = _iota((rows, GROUP_W), 1)
    return (lane >= j * HEAD_P) & (lane < (j + 1) * HEAD_P)


def _over_heads(v, g):
    r = v.shape[0]
    out = jnp.zeros((r, GROUP_W), F32)
    for j in range(4):
        out = jnp.where(_head_mask(j, r), _col(v, 4 * g + j), out)
    return out


def _ssd_group_fwd(q, g, xs_g, bg, cg, ht_g, cst_ref, causal, dx_g):
    dtx_g, csx_g, totx_g = _over_heads(q["dt"], g), _over_heads(q["cs"], g), _over_heads(q["tot"], g)
    xdt = xs_g * dtx_g
    ex = jnp.exp(csx_g)
    cb = _dot_nt(cg, bg)
    yoff = _dot(cg, ht_g) * ex
    ydiag = jnp.zeros((CHUNK, GROUP_W), F32)
    for j in range(4):
        sc = cb * _decay_mat(q["cs"], cst_ref, 4 * g + j, causal)
        ydiag = jnp.where(_head_mask(j), _dot(sc, xdt), ydiag)
    y = ydiag + yoff + xs_g * dx_g
    dsx = jnp.exp(totx_g - csx_g)
    return y, dict(xdt=xdt, ex=ex, cb=cb, yoff=yoff, dsx=dsx, dtx=dtx_g, totx=totx_g)


def _gated_norm_fwd(y_g, z_g, w_g):
    sz = _sigmoid(z_g)
    silu = z_g * sz
    yf = y_g * silu
    rs = lax.rsqrt(jnp.mean(yf * yf, axis=1, keepdims=True) + RMS_EPS)
    yn = yf * rs
    return yn * w_g, (sz, silu, rs, yn)


def _ssd_fwd(xact, proj, ymix, bias_pad, alog_pad, dxp, normw, *, name, jobs=()):
    s = xact.shape[0]
    nc = s // CHUNK

    def body(xa_ref, dt_ref, z_ref, _ymix_ref, bias_ref, alp_ref, dx_ref, nw_ref, y_ref, hp_ref, ht, cst):
        @pl.when(pl.program_id(0) == 0)
        def _():
            ht[...] = jnp.zeros_like(ht)

        hp_ref[...] = ht[...]
        q = _ssd_prep(dt_ref[...], bias_ref[...], alp_ref[...])
        cst[...] = q["cs"].T
        causal = q["tril"] > 0.0
        for g in range(N_GROUP):
            sl = slice(g * GROUP_W, (g + 1) * GROUP_W)
            xs_g = xa_ref[:, sl]
            bg = xa_ref[:, SSD_W + g * N_STATE:SSD_W + (g + 1) * N_STATE]
            cg = xa_ref[:, SSD_W + N_GROUP * N_STATE + g * N_STATE:SSD_W + N_GROUP * N_STATE + (g + 1) * N_STATE]
            ht_g = ht[:, sl]
            y, f = _ssd_group_fwd(q, g, xs_g, bg, cg, ht_g, cst, causal, dx_ref[:, sl])
            out, _ = _gated_norm_fwd(y, z_ref[:, sl], nw_ref[:, sl])
            y_ref[:, sl] = out.astype(BF16)
            ht[:, sl] = jnp.exp(f["totx"]) * ht_g + _dot_tn(bg, f["xdt"] * f["dsx"])

    par = lambda w: pl.BlockSpec((1, w), lambda c: (0, 0))
    (ycat, hprev), jouts = _hosted(
        body, jobs, grid=(nc,),
        in_specs=[pl.BlockSpec((CHUNK, XBC), lambda c: (c, 0)),
                  pl.BlockSpec((CHUNK, LANE), lambda c: (c, COL_DT // LANE)),
                  pl.BlockSpec((CHUNK, SSD_W), lambda c: (c, COL_Z // SSD_W)),
                  ANY_SPEC, par(LANE), par(LANE), par(SSD_W), par(SSD_W)],
        out_specs=(pl.BlockSpec((CHUNK, SSD_W), lambda c: (c, LRU_W // SSD_W)),
                   pl.BlockSpec((None, N_STATE, SSD_W), lambda c: (c, 0, 0))),
        out_shape=(jax.ShapeDtypeStruct(ymix.shape, ymix.dtype), jax.ShapeDtypeStruct((nc, N_STATE, SSD_W), F32)),
        scratch_shapes=[pltpu.VMEM((N_STATE, SSD_W), F32), pltpu.VMEM((CHUNK, LANE), F32)],
        aliases={3: 0}, name=name, args=(xact, proj, proj, ymix, bias_pad, alog_pad, dxp, normw))
    return ((ycat, hprev), jouts) if jobs else (ycat, hprev)


def _ssd_bwd(xact, proj, dycat, hprev, bias_pad, alog_pad, dxp, normw, *, name, jobs=()):
    s = xact.shape[0]
    nc = s // CHUNK
    l = CHUNK

    def body(xa_ref, dt_ref, z_ref, dy_ref, hp_ref, bias_ref, alp_ref, dx_ref, nw_ref,
             dxa_ref, ddt_ref, dz_ref, dnw_ref, small_ref, dht, cst, accx, dcsx_s, ddtx_s):
        step = pl.program_id(0)

        @pl.when(step == 0)
        def _():
            dht[...] = jnp.zeros_like(dht)
            accx[...] = jnp.zeros_like(accx)
            dnw_ref[...] = jnp.zeros_like(dnw_ref)
            small_ref[...] = jnp.zeros_like(small_ref)

        dtr = dt_ref[...]
        q = _ssd_prep(dtr, bias_ref[...], alp_ref[...])
        cst[...] = q["cs"].T
        causal = q["tril"] > 0.0
        eye = _iota((l, l), 0) == _iota((l, l), 1)
        lane = _iota((l, LANE), 1)
        dcs_head = jnp.zeros((l, LANE), F32)
        for g in range(N_GROUP):
            sl = slice(g * GROUP_W, (g + 1) * GROUP_W)
            slb = slice(SSD_W + g * N_STATE, SSD_W + (g + 1) * N_STATE)
            slc = slice(SSD_W + N_GROUP * N_STATE + g * N_STATE, SSD_W + N_GROUP * N_STATE + (g + 1) * N_STATE)
            xs_g, bg, cg = xa_ref[:, sl], xa_ref[:, slb], xa_ref[:, slc]
            ht_g = hp_ref[:, sl]
            dxp_g = dx_ref[:, sl]
            y, f = _ssd_group_fwd(q, g, xs_g, bg, cg, ht_g, cst, causal, dxp_g)
            z_g, nw_g = z_ref[:, sl], nw_ref[:, sl]
            _o, (sz, silu, rs, yn) = _gated_norm_fwd(y, z_g, nw_g)
            dout = dy_ref[:, sl]
            dnw_ref[:, sl] += jnp.sum(dout * yn, axis=0, keepdims=True)
            dyn = dout * nw_g
            dyf = rs * (dyn - yn * jnp.mean(dyn * yn, axis=1, keepdims=True))
            dy = dyf * silu
            dz_ref[:, sl] = (dyf * y * sz * (1.0 + z_g * (1.0 - sz))).astype(BF16)
            accx[0:1, sl] += jnp.sum(dy * xs_g, axis=0, keepdims=True)
            dyo = dy * f["ex"]
            dcg = _dot_nt(dyo, ht_g)
            dht_prev = _dot_tn(cg, dyo)
            dcsx = dy * f["yoff"]
            xdt = f["xdt"]
            dxdt = jnp.zeros((l, GROUP_W), F32)
            dcb = jnp.zeros((l, l), F32)
            for j in range(4):
                h = 4 * g + j
                lm = _decay_mat(q["cs"], cst, h, causal)
                sc = f["cb"] * lm
                mask = _head_mask(j)
                ds_ = jnp.where(causal, _dot_nt(jnp.where(mask, dy, 0.0), xdt), 0.0)
                dxdt = jnp.where(mask, _dot_tn(sc, dy), dxdt)
                dcb = dcb + ds_ * lm
                m = ds_ * sc
                rsum = jnp.sum(m, axis=1, keepdims=True)
                csum = jnp.sum(m, axis=0, keepdims=True)
                csum_col = jnp.sum(jnp.where(eye, csum, 0.0), axis=1, keepdims=True)
                dcs_head = dcs_head + jnp.where(lane == h, rsum - csum_col, 0.0)
            dhn = dht[:, sl]
            etot = jnp.exp(f["totx"])
            dxd = _dot(bg, dhn)
            dbg = _dot_nt(xdt * f["dsx"], dhn)
            dxdt = dxdt + dxd * f["dsx"]
            qq = dxd * xdt * f["dsx"]
            dcsx = dcsx - qq
            dtot = jnp.sum(qq, axis=0, keepdims=True) + jnp.sum(dhn * ht_g, axis=0, keepdims=True) * etot
            dht[:, sl] = etot * dhn + dht_prev
            dcg = dcg + _dot(dcb, bg)
            dbg = dbg + _dot_tn(dcb, cg)
            dxa_ref[:, sl] = dxdt * f["dtx"] + dy * dxp_g
            dxa_ref[:, slb] = dbg
            dxa_ref[:, slc] = dcg
            dcsx_s[:, sl] = dcsx
            ddtx_s[:, sl] = dxdt * xs_g
            accx[2:3, sl] = dtot
        reduce = (jnp.right_shift(_iota((SSD_W, LANE), 0), 6) == _iota((SSD_W, LANE), 1)).astype(F32)
        triu = (_iota((l, l), 1) >= _iota((l, l), 0)).astype(F32)
        dtot = _dot01_r(accx[...], reduce)[2:3, :]
        da_head = _dot01(triu, dcs_head + _dot01_r(dcsx_s[...], reduce)) + dtot
        ddt = _dot01_r(ddtx_s[...], reduce) + da_head * q["a_head"]
        small_ref[1:2, :] += jnp.sum(da_head * q["dt"], axis=0, keepdims=True)
        ddtr = ddt * _sigmoid(dtr + bias_ref[...])
        ddt_ref[...] = ddtr.astype(BF16)
        small_ref[0:1, :] += jnp.sum(ddtr, axis=0, keepdims=True)

        @pl.when(step == nc - 1)
        def _():
            small_ref[1:2, :] = small_ref[1:2, :] * q["a_head"]
            small_ref[2:3, :] = _dot01_r(accx[...], reduce)[0:1, :]

    rev = lambda c: nc - 1 - c
    par = lambda w: pl.BlockSpec((1, w), lambda c: (0, 0))
    outs, jouts = _hosted(
        body, jobs, grid=(nc,),
        in_specs=[pl.BlockSpec((CHUNK, XBC), lambda c: (rev(c), 0)),
                  pl.BlockSpec((CHUNK, LANE), lambda c: (rev(c), COL_DT // LANE)),
                  pl.BlockSpec((CHUNK, SSD_W), lambda c: (rev(c), COL_Z // SSD_W)),
                  pl.BlockSpec((CHUNK, SSD_W), lambda c: (rev(c), 1)),
                  pl.BlockSpec((None, N_STATE, SSD_W), lambda c: (rev(c), 0, 0)),
                  par(LANE), par(LANE), par(SSD_W), par(SSD_W)],
        out_specs=(pl.BlockSpec((CHUNK, XBC), lambda c: (rev(c), 0)),
                   pl.BlockSpec((CHUNK, LANE), lambda c: (rev(c), 0)),
                   pl.BlockSpec((CHUNK, SSD_W), lambda c: (rev(c), 0)),
                   par(SSD_W), pl.BlockSpec((SUBLANE, LANE), lambda c: (0, 0))),
        out_shape=(jax.ShapeDtypeStruct((s, XBC), F32), jax.ShapeDtypeStruct((s, LANE), BF16),
                   jax.ShapeDtypeStruct((s, SSD_W), BF16), jax.ShapeDtypeStruct((1, SSD_W), F32),
                   jax.ShapeDtypeStruct((SUBLANE, LANE), F32)),
        scratch_shapes=[pltpu.VMEM((N_STATE, SSD_W), F32), pltpu.VMEM((CHUNK, LANE), F32),
                        pltpu.VMEM((SUBLANE, SSD_W), F32), pltpu.VMEM((CHUNK, SSD_W), F32),
                        pltpu.VMEM((CHUNK, SSD_W), F32)],
        name=name, args=(xact, proj, proj, dycat, hprev, bias_pad, alog_pad, dxp, normw))
    return (tuple(outs), jouts) if jobs else tuple(outs)


def _blockdiag(w):
    w2 = w.reshape(N_HEAD // 2, 2, HEAD_P, HEAD_P)
    z = jnp.zeros((N_HEAD // 2, HEAD_P, HEAD_P), w.dtype)
    top = jnp.concatenate([w2[:, 0], z], axis=2)
    bot = jnp.concatenate([z, w2[:, 1]], axis=2)
    return jnp.concatenate([top, bot], axis=1)


def _unblockdiag(wbd):
    a = wbd[:, :HEAD_P, :HEAD_P]
    b = wbd[:, HEAD_P:, HEAD_P:]
    return jnp.stack([a, b], axis=1).reshape(N_HEAD, HEAD_P, HEAD_P)


def _pad_rows8(w):
    return jnp.concatenate([w, jnp.zeros((SUBLANE - w.shape[0], w.shape[1]), w.dtype)], axis=0)


def _pad_lane(v):
    return jnp.concatenate([v, jnp.zeros((1, LANE - v.shape[1]), v.dtype)], axis=1)


class _NoExchange:
    def ride(self, host):
        return []

    def done(self, jobs, outs, w):
        pass

    def grad(self, name, val):
        pass

    def small(self, raw):
        pass


def _local_step(x, p, tgt, w, hooks=_NoExchange()):
    cw_l = _pad_rows8(w["lru_conv_w"])
    cw_s = _pad_rows8(w["ssd_conv_w"])
    wa_bd = _blockdiag(w["lru_gate_a_w"])
    wx_bd = _blockdiag(w["lru_gate_x_w"])
    ba = w["lru_gate_a_b"].reshape(1, LRU_W)
    bx = w["lru_gate_x_b"].reshape(1, LRU_W)
    bias_pad = _pad_lane(w["ssd_dt_bias"])
    alog_pad = _pad_lane(w["ssd_a_log"])
    dxp = jnp.repeat(w["ssd_d"], HEAD_P, axis=1)

    def host(fn, *a, name, **k):
        jobs = hooks.ride(name)
        res = fn(*a, name=name, jobs=jobs, **k)
        if jobs:
            res, jouts = res
            hooks.done(jobs, jouts, w)
        return res

    def grad(n, val):
        g[n] = val
        hooks.grad(n, val)

    xb = x.astype(BF16)
    proj = host(_mm, xb, w["w_in_t"], "nt", tm=1024, tn=512, name="in_proj")
    ymix, h_lru = host(_lru_fwd, proj, cw_l, w["lru_conv_b"], wa_bd, ba, wx_bd, bx, w["lru_a_param"], name="lru_fwd")
    xact = _conv_silu_fwd(proj, cw_s, w["ssd_conv_b"], col0=COL_XBC, width=XBC, ct=256, name="ssd_conv_fwd")
    ycat, hprev = host(_ssd_fwd, xact, proj, ymix, bias_pad, alog_pad, dxp, w["ssd_norm_w"], name="ssd_fwd")
    mix = _mm(ycat, w["w_out"], "nn", tm=1024, tn=1024, name="out_proj")
    x1, x1b = _ln_fwd(x, mix, w["ln1_g"], w["ln1_b"], name="ln1_fwd")
    pre = _mm(x1b, w["w_ff1"], "nn", tm=1024, tn=512, out_dtype=BF16, name="ff1")
    ff = _mm(pre, w["w_ff2"], "nn", tm=512, tn=1024, a_fn=_relu2, name="ff2")
    x2, x2b = _ln_fwd(x1, ff, w["ln2_g"], w["ln2_b"], name="ln2_fwd")
    gpre = _mm(x2b, w["w_ple_gate"], "nn", tm=1024, tn=1024, name="ple_gate")
    ple = _mm(p, w["w_ple"], "nn", tm=1024, tn=1024, name="ple_proj")
    loss, dgpre, dple, dt3, dg3, db3 = _head(x2, gpre, ple, w["ln3_g"], w["ln3_b"], tgt, name="head")

    g = {}
    g["ln3_g"], g["ln3_b"] = dg3, db3
    grad("w_ple_gate", _mm(x2b, dgpre, "tn", tm=512, tn=1024, out_dtype=BF16, name="d_w_ple_gate"))
    grad("w_ple", _mm(p, dple, "tn", tm=256, tn=512, dest_major=True, out_dtype=BF16, name="d_w_ple"))
    dx2_mm = host(_mm, dgpre, w["w_ple_gate"], "nt", tm=1024, tn=1024, name="d_x2")
    dt2, dt2b, g["ln2_g"], g["ln2_b"] = _ln_bwd(x1, ff, w["ln2_g"], [dt3, dx2_mm], [ALPHA, 1.0], name="ln2_bwd")
    grad("w_ff2", host(_mm, pre, dt2b, "tn", tm=512, tn=1024, a_fn=_relu2, out_dtype=BF16, name="d_w_ff2"))
    dpre = host(_mm, dt2b, w["w_ff2"], "nt", tm=1024, tn=512, extra=pre, out_dtype=BF16,
                epi=lambda acc, pv: acc * 2.0 * jnp.maximum(pv.astype(F32), 0.0), name="d_pre")
    grad("w_ff1", host(_mm, x1b, dpre, "tn", tm=1024, tn=512, dest_major=True, out_dtype=BF16, name="d_w_ff1"))
    dx1_mm = host(_mm, dpre, w["w_ff1"], "nt", tm=512, tn=1024, name="d_x1")
    dt1, dt1b, g["ln1_g"], g["ln1_b"] = _ln_bwd(x, mix, w["ln1_g"], [dt2, dx1_mm], [ALPHA, 1.0], name="ln1_bwd")
    grad("w_out", host(_mm, ycat, dt1b, "tn", tm=512, tn=1024, out_dtype=BF16, name="d_w_out"))
    dycat = host(_mm, dt1b, w["w_out"], "nt", tm=1024, tn=1024, name="d_ycat")
    dxl, dgl, dcwb_l, dwa, dwx = host(_lru_bwd, proj, dycat, h_lru, cw_l, w["lru_conv_b"], wa_bd, ba, wx_bd, bx,
                                      w["lru_a_param"], name="lru_bwd")
    g["lru_gate_a_w"] = _unblockdiag(dwa)
    g["lru_gate_x_w"] = _unblockdiag(dwx)
    raw = dict(lru=dcwb_l, gate_a=g["lru_gate_a_w"].reshape(N_HEAD * HEAD_P, HEAD_P),
               gate_x=g["lru_gate_x_w"].reshape(N_HEAD * HEAD_P, HEAD_P))
    hooks.small(raw)
    dxact, ddt, dz, g["ssd_norm_w"], small = host(_ssd_bwd, xact, proj, dycat, hprev, bias_pad, alog_pad, dxp,
                                                   w["ssd_norm_w"], name="ssd_bwd")
    dxbc, dcwb_s = host(_conv_silu_bwd, proj, dxact, cw_s, w["ssd_conv_b"], col0=COL_XBC, width=XBC, ct=256,
                        name="ssd_conv_bwd")
    pieces, offsets = [dxl, dgl, dz, dxbc, ddt], [0, COL_G, COL_Z, COL_XBC, COL_DT]

    g["lru_conv_w"] = dcwb_l[0:4]
    g["lru_conv_b"] = dcwb_l[4:5]
    g["lru_gate_a_b"] = dcwb_l[5:6]
    g["lru_gate_x_b"] = dcwb_l[6:7]
    g["lru_a_param"] = dcwb_l[7:8]
    g["ssd_conv_w"] = dcwb_s[0:4]
    g["ssd_conv_b"] = dcwb_s[4:5]
    g["ssd_dt_bias"] = small[0:1, :N_HEAD]
    g["ssd_a_log"] = small[1:2, :N_HEAD]
    g["ssd_d"] = small[2:3, :N_HEAD]
    rows = jnp.concatenate([g[n] for n in ("ssd_norm_w", "ln1_g", "ln1_b", "ln2_g", "ln2_b", "ln3_g", "ln3_b")]
                           + [jnp.broadcast_to(loss[:, 0:1], (1, D_MODEL))], axis=0)
    late = dict(ssd=dcwb_s, heads=small, rows=rows)
    hooks.small(late)
    raw.update(late)
    dwt = [host(_mm, pc, xb, "tn", tm=512, tn=1024, out_dtype=BF16, name="d_w_in_%d" % q)
           for q, pc in enumerate(pieces)]
    grad("w_in", jnp.concatenate(dwt, axis=0))
    grad_x = host(_mm_pieces, pieces, offsets, w["w_in_t"], tm=256, extra=dt1, epi=lambda acc, e: acc + ALPHA * e,
                  name="d_x")
    return loss[0, 0], grad_x, g, raw


ANY_SPEC = pl.BlockSpec(memory_space=pl.ANY)


def _mesh_pos():
    return lax.axis_index("x"), lax.axis_index("y"), lax.axis_index("c")


def _remote(src, dst, send, recv, k, to):
    return pltpu.make_async_remote_copy(src_ref=src, dst_ref=dst, send_sem=send.at[k], recv_sem=recv.at[k],
                                        device_id=to, device_id_type=MESH_T)


class _Job:
    N_SEM = 7

    def __init__(self, kind, inp):
        self.kind, self.inp = kind, inp
        shape = {"gather": (N_DEV,) + inp.shape, "pair": (4,) + inp.shape[1:], "chip": inp.shape}[kind]
        self.out = jax.ShapeDtypeStruct(shape, inp.dtype)

    def _places(self):
        x, y, c = _mesh_pos()
        return (x, y, c), (x, y, 1 - c), [(1 - x, y), (x, 1 - y), (1 - x, 1 - y)]

    def start(self, inp, out, send, recv, loc):
        me, sibling, chips = self._places()
        x, y, c = me
        if self.kind == "gather":
            mine = out.at[4 * x + 2 * y + c]
            pltpu.make_async_copy(inp, mine, loc.at[0]).start()
            _remote(inp, mine, send, recv, 0, sibling).start()
            for j, chip in enumerate(chips):
                _remote(inp, mine, send, recv, 1 + j, (*chip, c)).start()
        elif self.kind == "pair":
            for k in range(4):
                _remote(inp.at[2 * k + (1 - c)], out.at[k], send, recv, k, sibling).start()
        else:
            kme = 2 * x + y
            pltpu.make_async_copy(inp.at[kme], out.at[kme], loc.at[0]).start()
            for j, (tx, ty) in enumerate(chips):
                _remote(inp.at[2 * tx + ty], out.at[kme], send, recv, j, (tx, ty, c)).start()

    def mid(self, inp, out, send, recv, loc):
        if self.kind != "gather":
            return
        me, sibling, chips = self._places()
        c = me[2]
        for j, chip in enumerate(chips):
            landed = out.at[4 * chip[0] + 2 * chip[1] + c]
            _remote(landed, landed, send, recv, 1 + j, me).wait_recv()
            _remote(landed, landed, send, recv, 4 + j, sibling).start()

    def finish(self, inp, out, send, recv, loc):
        me, sibling, chips = self._places()
        x, y, c = me
        if self.kind == "gather":
            blk = lambda px, py, pc: out.at[4 * px + 2 * py + pc]
            mine = blk(*me)
            _remote(inp, blk(*sibling), send, recv, 0, me).wait_recv()
            for j, chip in enumerate(chips):
                _remote(inp, blk(*chip, 1 - c), send, recv, 4 + j, me).wait_recv()
            for k in range(7):
                _remote(inp, mine, send, recv, k, sibling).wait_send()
            pltpu.make_async_copy(inp, mine, loc.at[0]).wait()
        elif self.kind == "pair":
            for k in range(4):
                _remote(inp.at[2 * k + (1 - c)], out.at[k], send, recv, k, sibling).wait()
        else:
            kme = 2 * x + y
            for j, (tx, ty) in enumerate(chips):
                _remote(inp.at[kme], out.at[2 * tx + ty], send, recv, j, (tx, ty, c)).wait_recv()
            for j, (tx, ty) in enumerate(chips):
                _remote(inp.at[2 * tx + ty], out.at[kme], send, recv, j, (tx, ty, c)).wait_send()
            pltpu.make_async_copy(inp.at[kme], out.at[kme], loc.at[0]).wait()


def _job_scratch(jobs):
    sem = pltpu.SemaphoreType.DMA
    return [s for _ in jobs for s in (sem((_Job.N_SEM,)), sem((_Job.N_SEM,)), sem((1,)))]


def _run_jobs(jobs, method, jins, jouts, jsems):
    for q, job in enumerate(jobs):
        getattr(job, method)(jins[q], jouts[q], *jsems[3 * q:3 * q + 3])


def _exchange(jobs, *, name):
    n = len(jobs)

    def body(*refs):
        jins, jouts, jsems = refs[:n], refs[n:2 * n], refs[2 * n:]
        _run_jobs(jobs, "start", jins, jouts, jsems)
        _run_jobs(jobs, "mid", jins, jouts, jsems)
        _run_jobs(jobs, "finish", jins, jouts, jsems)

    return _pcall(body, in_specs=[ANY_SPEC] * n, out_specs=[ANY_SPEC] * n, out_shape=[j.out for j in jobs],
                  scratch_shapes=_job_scratch(jobs), name=name)(*[j.inp for j in jobs])


def _hosted(body, jobs, *, grid, in_specs, out_specs, out_shape, args, name, scratch_shapes=(), aliases=None):
    in_specs, out_specs, out_shape = list(in_specs), list(out_specs), list(out_shape)
    scratch_shapes = list(scratch_shapes)
    n_in, n_out, n_scr, nj = len(in_specs), len(out_specs), len(scratch_shapes), len(jobs)
    sem = ("arbitrary",) * len(grid)
    kw = dict(input_output_aliases=aliases) if aliases else {}
    if not jobs:
        res = _pcall(body, grid=grid, in_specs=in_specs, out_specs=out_specs, out_shape=out_shape,
                     scratch_shapes=scratch_shapes, name=name, compiler_params=_cparams(sem), **kw)(*args)
        return list(res), []

    def full(*refs):
        ins, jins = refs[:n_in], refs[n_in:n_in + nj]
        o0 = n_in + nj
        outs, jouts = refs[o0:o0 + n_out], refs[o0 + n_out:o0 + n_out + nj]
        s0 = o0 + n_out + nj
        scr, jsems = refs[s0:s0 + n_scr], refs[s0 + n_scr:]
        step = pl.program_id(0)
        for ax in range(1, len(grid)):
            step = step * grid[ax] + pl.program_id(ax)
        total = math.prod(grid)

        @pl.when(step == 0)
        def _():
            _run_jobs(jobs, "start", jins, jouts, jsems)

        body(*ins, *outs, *scr)

        @pl.when(step == total - 1)
        def _():
            _run_jobs(jobs, "mid", jins, jouts, jsems)
            _run_jobs(jobs, "finish", jins, jouts, jsems)

    res = _pcall(full, grid=grid, in_specs=in_specs + [ANY_SPEC] * nj, out_specs=out_specs + [ANY_SPEC] * nj,
                 out_shape=out_shape + [j.out for j in jobs], scratch_shapes=scratch_shapes + _job_scratch(jobs),
                 name=name, compiler_params=_cparams(sem), **kw)(*args, *[j.inp for j in jobs])
    return list(res[:n_out]), list(res[n_out:])


def _pair_add(g8, r4, cidx, *, name):
    _, r, c = g8.shape
    tr = ROW_TILE if r % ROW_TILE == 0 else r

    def body(c_ref, g_ref, r_ref, o_ref):
        o_ref[...] = (g_ref[...].astype(F32) + r_ref[...].astype(F32)).astype(BF16)

    return _pcall(
        body,
        grid_spec=pltpu.PrefetchScalarGridSpec(
            num_scalar_prefetch=1, grid=(4, r // tr),
            in_specs=[pl.BlockSpec((None, tr, c), lambda k, i, cr: (2 * k + cr[0], i, 0)),
                      pl.BlockSpec((None, tr, c), lambda k, i, cr: (k, i, 0))],
            out_specs=pl.BlockSpec((None, tr, c), lambda k, i, cr: (k, i, 0))),
        out_shape=jax.ShapeDtypeStruct((4, r, c), BF16), name=name,
        compiler_params=_cparams(("parallel", "parallel")))(cidx, g8, r4)


def _adam_update(g, w_ref, m_ref, v_ref, g_ref, d_ref, mo_ref, vo_ref):
    c1 = 1.0 - ADAM_B1 ** ADAM_STEP
    c2 = 1.0 - ADAM_B2 ** ADAM_STEP
    m2 = ADAM_B1 * m_ref[...] + (1.0 - ADAM_B1) * g
    v2 = ADAM_B2 * v_ref[...] + (1.0 - ADAM_B2) * (g * g)
    g_ref[...] = g
    mo_ref[...] = m2
    vo_ref[...] = v2
    d_ref[...] = -ADAM_LR * ((m2 / c1) / (jnp.sqrt(v2 / c2) + ADAM_EPS) + ADAM_WD * w_ref[...])


def _adamw_rows(srcs, items, own_cols, me1, *, name):
    ns, ni, no = len(srcs), len(items), len(own_cols)
    full = lambda a: pl.BlockSpec(a.shape, lambda i, me: (0,) * a.ndim)
    in_specs = [full(a) for a in srcs]
    args = list(srcs)
    for (si, _r0, w, _m, _v) in own_cols:
        a = srcs[si]
        in_specs.append(pl.BlockSpec((N_DEV, a.shape[1], w.shape[1]), lambda i, me: (0, 0, me[0])))
        args.append(a)
    out_specs, out_shape = [], []
    for (_si, _r0, w, m, v) in list(items) + list(own_cols):
        in_specs += [full(w)] * 3
        args += [w, m, v]
        out_specs += [full(w)] * 4
        out_shape += [jax.ShapeDtypeStruct(w.shape, F32)] * 4

    def body(me_ref, *refs):
        src_refs, own_refs = refs[:ns], refs[ns:ns + no]
        wmv = refs[ns + no:ns + no + 3 * (ni + no)]
        outs = refs[ns + no + 3 * (ni + no):]
        for q, (si, r0, w, _m, _v) in enumerate(list(items) + list(own_cols)):
            nr, cw = w.shape
            gref = src_refs[si] if q < ni else own_refs[q - ni]
            g = gref[0, r0:r0 + nr, 0:cw]
            for d in range(1, N_DEV):
                g = g + gref[d, r0:r0 + nr, 0:cw]
            _adam_update(g, *wmv[3 * q:3 * q + 3], *outs[4 * q:4 * q + 4])

    res = _pcall(
        body,
        grid_spec=pltpu.PrefetchScalarGridSpec(num_scalar_prefetch=1, grid=(1,), in_specs=in_specs, out_specs=out_specs),
        out_shape=out_shape, name=name, compiler_params=_cparams(("arbitrary",)))(me1, *args)
    return [tuple(res[4 * q:4 * q + 4]) for q in range(ni + no)]


def _adamw(gsrc, w, m, v, *, name):
    k, r, c = gsrc.shape
    tr = ROW_TILE if r % ROW_TILE == 0 else r

    def body(gs_ref, w_ref, m_ref, v_ref, g_ref, d_ref, mo_ref, vo_ref):
        g = gs_ref[0].astype(F32)
        for q in range(1, k):
            g = g + gs_ref[q].astype(F32)
        _adam_update(g, w_ref, m_ref, v_ref, g_ref, d_ref, mo_ref, vo_ref)

    tc = c
    if tr == r and r > ROW_TILE and c % 256 == 0:
        tc = 256
    blk = pl.BlockSpec((tr, tc), lambda i, j: (i, j))
    sd = jax.ShapeDtypeStruct((r, c), F32)
    return _pcall(body, grid=(r // tr, c // tc),
                  in_specs=[pl.BlockSpec((k, tr, tc), lambda i, j: (0, i, j)), blk, blk, blk],
                  out_specs=(blk, blk, blk, blk), out_shape=(sd, sd, sd, sd), name=name,
                  compiler_params=_cparams(("parallel", "parallel")))(gsrc, w, m, v)


WEIGHTS = ['w_in', 'lru_conv_w', 'lru_conv_b', 'lru_gate_a_w', 'lru_gate_a_b', 'lru_gate_x_w', 'lru_gate_x_b',
           'lru_a_param', 'ssd_conv_w', 'ssd_conv_b', 'ssd_dt_bias', 'ssd_a_log', 'ssd_d', 'ssd_norm_w', 'w_out',
           'ln1_g', 'ln1_b', 'w_ff1', 'w_ff2', 'ln2_g', 'ln2_b', 'w_ple_gate', 'w_ple', 'ln3_g', 'ln3_b']
BIG = ['w_in', 'w_out', 'w_ff1', 'w_ff2', 'w_ple_gate', 'w_ple']
COL_SHARDED = ('w_ff1', 'w_ple')
CONV = ['lru_conv_w', 'ssd_conv_w']
REPL = [n for n in WEIGHTS if n not in BIG and n not in CONV]
CONV_CH = {'lru_conv_w': LRU_W, 'ssd_conv_w': XBC}


def _to_dest_major(name, gfull):
    if name == 'w_in':
        gfull = gfull[:D_IN]
    if name in COL_SHARDED:
        r, cfull = gfull.shape
        return gfull.reshape(r, N_DEV, cfull // N_DEV).transpose(1, 0, 2)
    rfull, cdim = gfull.shape
    return gfull.reshape(N_DEV, rfull // N_DEV, cdim)


def _full_weight(name, gathered):
    if name in COL_SHARDED:
        _, r, cs = gathered.shape
        full = gathered.transpose(1, 0, 2).reshape(r, N_DEV * cs)
    else:
        _, rs, cdim = gathered.shape
        full = gathered.reshape(N_DEV * rs, cdim)
    if name == 'w_in':
        full = jnp.concatenate([full, jnp.zeros((D_IN_PAD - D_IN, D_MODEL), full.dtype)], axis=0)
    return full


SMALL_SRC = ("lru", "ssd", "heads", "rows", "gate_a", "gate_x")
AG_HOSTS = {"in_proj": ("w_ff1",), "lru_fwd": ("w_out", "w_ple_gate", "w_ple"), "ssd_fwd": ("w_ff2",)}
PAIR_HOSTS = ("d_x2", "d_pre", "d_x1", "d_ycat")
CHIP_HOSTS = {"lru_bwd": ("w_ple_gate", "w_ple", "w_ff2"), "ssd_bwd": ("w_ff1",), "d_x": ("w_out",)}
SMALL_HOSTS = {"ssd_bwd": ("lru", "gate_a", "gate_x"), "d_w_in_3": ("ssd", "heads", "rows")}


class _Schedule:
    def __init__(self, shards, cidx):
        self.shards, self.cidx = shards, cidx
        self.pair, self.chip, self.small_jobs = [], [], []
        self.dest, self.summed, self.gathered_small = {}, {}, {}
        self.tags = []

    def ride(self, host):
        tags = []
        if host in AG_HOSTS:
            tags = [("weight", n, self.shards[n]) for n in AG_HOSTS[host]]
        elif host in PAIR_HOSTS or host in CHIP_HOSTS or host == "flush":
            tags = [("pair", n, a) for n, a in self.pair]
            self.pair = []
            if host not in PAIR_HOSTS:
                take = [t for t in self.chip if host == "flush" or t[0] in CHIP_HOSTS[host]]
                tags += [("chip", n, a) for n, a in take]
                self.chip = [t for t in self.chip if not any(t is u for u in take)]
        if host in SMALL_HOSTS:
            tags += [("small", n, a) for n, a in self.small_jobs if n in SMALL_HOSTS[host]]
            self.small_jobs = [t for t in self.small_jobs if t[0] not in SMALL_HOSTS[host]]
        self.tags = tags
        return [_Job({"weight": "gather", "small": "gather"}.get(kind, kind), a) for kind, _n, a in tags]

    def done(self, jobs, outs, w):
        for (kind, n, _a), o in zip(self.tags, outs):
            if kind == "weight":
                w[n] = _full_weight(n, o)
            elif kind == "small":
                self.gathered_small[n] = o
            elif kind == "pair":
                self.chip.append((n, _pair_add(self.dest[n], o, self.cidx, name="rs_pair_add_" + n)))
            else:
                self.summed[n] = o

    def grad(self, name, val):
        self.dest[name] = val if val.ndim == 3 else _to_dest_major(name, val)
        self.pair.append((name, self.dest[name]))

    def small(self, raw):
        self.small_jobs += list(raw.items())

    def flush(self):
        step = 0
        while self.pair or self.chip:
            jobs = self.ride("flush")
            self.done(jobs, _exchange(jobs, name="rs_flush_%d" % step), None)
            step += 1


def kernel(x, p, w_in, lru_conv_w, lru_conv_b, lru_gate_a_w, lru_gate_a_b, lru_gate_x_w, lru_gate_x_b, lru_a_param, ssd_conv_w, ssd_conv_b, ssd_dt_bias, ssd_a_log, ssd_d, ssd_norm_w, w_out, ln1_g, ln1_b, w_ff1, w_ff2, ln2_g, ln2_b, w_ple_gate, w_ple, ln3_g, ln3_b, loss_target, m_w_in, m_lru_conv_w, m_lru_conv_b, m_lru_gate_a_w, m_lru_gate_a_b, m_lru_gate_x_w, m_lru_gate_x_b, m_lru_a_param, m_ssd_conv_w, m_ssd_conv_b, m_ssd_dt_bias, m_ssd_a_log, m_ssd_d, m_ssd_norm_w, m_w_out, m_ln1_g, m_ln1_b, m_w_ff1, m_w_ff2, m_ln2_g, m_ln2_b, m_w_ple_gate, m_w_ple, m_ln3_g, m_ln3_b, v_w_in, v_lru_conv_w, v_lru_conv_b, v_lru_gate_a_w, v_lru_gate_a_b, v_lru_gate_x_w, v_lru_gate_x_b, v_lru_a_param, v_ssd_conv_w, v_ssd_conv_b, v_ssd_dt_bias, v_ssd_a_log, v_ssd_d, v_ssd_norm_w, v_w_out, v_ln1_g, v_ln1_b, v_w_ff1, v_w_ff2, v_ln2_g, v_ln2_b, v_w_ple_gate, v_w_ple, v_ln3_g, v_ln3_b):
    given = dict(locals())
    def local(a, n):
        return jnp.swapaxes(a[0], 0, 1) if n == 'w_in' else a[0]

    wsh = {n: local(given[n], n) for n in WEIGHTS}
    msh = {n: local(given["m_" + n], n) for n in WEIGHTS}
    vsh = {n: local(given["v_" + n], n) for n in WEIGHTS}
    xi, yi, ci = _mesh_pos()
    me = 4 * xi + 2 * yi + ci

    shards = {n: wsh[n].astype(BF16) for n in BIG}
    conv_pack = jnp.concatenate([_pad_rows8(wsh[n]) for n in CONV], axis=1)
    g_in, gconv = _exchange([_Job("gather", shards['w_in']), _Job("gather", conv_pack)], name="ag_first")
    full = {'w_in_t': _full_weight('w_in', g_in)}
    c0 = 0
    for n in CONV:
        cw = CONV_CH[n] // N_DEV
        full[n] = gconv[:, :4, c0:c0 + cw].transpose(1, 0, 2).reshape(4, CONV_CH[n])
        c0 += cw
    for n in REPL:
        full[n] = given[n] if given[n].ndim == 2 else wsh[n]

    sched = _Schedule(shards, jnp.reshape(ci, (1,)).astype(jnp.int32))
    loss_local, grad_x, g, raw = _local_step(x[0], p[0, 0], loss_target[0], full, sched)
    sched.flush()
    summed, gat = sched.summed, sched.gathered_small
    loss = gat["rows"][0, 7, 0]
    for d in range(1, N_DEV):
        loss = loss + gat["rows"][d, 7, 0]

    outs = {}
    for n in BIG:
        outs[n] = _adamw(summed[n], wsh[n], msh[n], vsh[n], name="adamw_" + n)
    for n, k in (("lru_gate_a_w", "gate_a"), ("lru_gate_x_w", "gate_x")):
        flat = lambda a: a.reshape(N_HEAD * HEAD_P, HEAD_P)
        res = _adamw(gat[k], flat(wsh[n]), flat(msh[n]), flat(vsh[n]), name="adamw_" + n)
        outs[n] = tuple(r.reshape(N_HEAD, HEAD_P, HEAD_P) for r in res)
    row_items = [("lru_conv_b", 0, 4), ("lru_gate_a_b", 0, 5), ("lru_gate_x_b", 0, 6), ("lru_a_param", 0, 7),
                 ("ssd_conv_b", 1, 4), ("ssd_dt_bias", 2, 0), ("ssd_a_log", 2, 1), ("ssd_d", 2, 2),
                 ("ssd_norm_w", 3, 0), ("ln1_g", 3, 1), ("ln1_b", 3, 2), ("ln2_g", 3, 3), ("ln2_b", 3, 4),
                 ("ln3_g", 3, 5), ("ln3_b", 3, 6)]
    vec = lambda a: a.reshape(1, -1)
    items = [(si, r0, vec(given[n]), vec(given["m_" + n]), vec(given["v_" + n])) for n, si, r0 in row_items]
    own = [(si, 0, wsh[n], msh[n], vsh[n]) for n, si in (("lru_conv_w", 0), ("ssd_conv_w", 1))]
    me1 = jnp.reshape(me, (1,)).astype(jnp.int32)
    res = _adamw_rows([gat[k] for k in SMALL_SRC[:4]], items, own, me1, name="adamw_small")
    for (n, _si, _r0), r4 in zip(row_items, res[:len(row_items)]):
        outs[n] = r4
    for n, r4 in zip(CONV, res[len(row_items):]):
        outs[n] = r4

    def fin(n, k):
        a = jnp.swapaxes(outs[n][k], 0, 1) if n == 'w_in' else outs[n][k]
        return a.reshape(given[n].shape)

    return (loss, grad_x[None],
            *[fin(n, 0) for n in WEIGHTS], *[fin(n, 1) for n in WEIGHTS],
            *[fin(n, 2) for n in WEIGHTS], *[fin(n, 3) for n in WEIGHTS])
```

```python
import math

import jax
import jax.numpy as jnp
from jax import lax
from jax.experimental import pallas as pl
from jax.experimental.pallas import tpu as pltpu

F32 = jnp.float32
BF16 = jnp.bfloat16
HI = lax.Precision.HIGHEST

N_DEV = 8
D_MODEL = 1024
LRU_W = 1024
SSD_W = 1024
XBC = 2048
N_HEAD = 16
HEAD_P = 64
N_GROUP = 4
GROUP_W = 256
N_STATE = 128
CHUNK = 128
D_FF = 4096
PLE_DIM = 256
D_IN = 5136
D_IN_PAD = 5632
COL_G = 1024
COL_Z = 2048
COL_XBC = 3072
COL_DT = 5120
LRU_C = 8.0
ALPHA = 2.0 ** 0.25
LN_EPS = 1e-5
RMS_EPS = 1e-5
ADAM_LR = 0.001
ADAM_B1 = 0.9
ADAM_B2 = 0.999
ADAM_EPS = 1e-08
ADAM_WD = 0.01
ADAM_STEP = 10
GELU_C = math.sqrt(2.0 / math.pi)
LANE = 128
SUBLANE = 8
VMEM_LIMIT = 48 * 1024 * 1024
MESH_T = pl.DeviceIdType.MESH
NEG_BIG = -1e30


def _pcall(body, **kw):
    return pl.pallas_call(body, **kw)


def _cparams(sem):
    return pltpu.CompilerParams(dimension_semantics=sem, vmem_limit_bytes=VMEM_LIMIT)


def _dot(a, b):
    return jnp.dot(a.astype(BF16), b.astype(BF16), preferred_element_type=F32)


def _dot_nt(a, b):
    return lax.dot_general(a.astype(BF16), b.astype(BF16), (((1,), (1,)), ((), ())), preferred_element_type=F32)


def _dot_tn(a, b):
    return lax.dot_general(a.astype(BF16), b.astype(BF16), (((0,), (0,)), ((), ())), preferred_element_type=F32)


def _dotx(a, b):
    return jnp.dot(a, b, precision=HI, preferred_element_type=F32)


def _sigmoid(x):
    return jax.nn.sigmoid(x)


def _softplus(v):
    return jnp.maximum(v, 0.0) + jnp.log1p(jnp.exp(-jnp.abs(v)))


def _gelu(x):
    th = jnp.tanh(GELU_C * (x + 0.044715 * x * x * x))
    return 0.5 * x * (1.0 + th), th


def _gelu_grad(x, th):
    return 0.5 * (1.0 + th) + 0.5 * x * (1.0 - th * th) * GELU_C * (1.0 + 3.0 * 0.044715 * x * x)


def _iota(shape, dim):
    return lax.broadcasted_iota(jnp.int32, shape, dim)


def _mm(a, b, mode, *, tm, tn, name, a_fn=None, extra=None, epi=None, out_dtype=F32, dest_major=False, jobs=()):
    m = a.shape[1] if mode == "tn" else a.shape[0]
    n = b.shape[0] if mode == "nt" else b.shape[1]
    tm, tn = min(tm, m), min(tn, n)
    if dest_major:
        tn = n // N_DEV
    if mode == "nn":
        m, k = a.shape
        _, n = b.shape
        a_spec = pl.BlockSpec((tm, k), lambda i, j: (i, 0))
        b_spec = pl.BlockSpec((k, tn), lambda i, j: (0, j))
        dims = ((1,), (0,))
    elif mode == "nt":
        m, k = a.shape
        n, _ = b.shape
        a_spec = pl.BlockSpec((tm, k), lambda i, j: (i, 0))
        b_spec = pl.BlockSpec((tn, k), lambda i, j: (j, 0))
        dims = ((1,), (1,))
    else:
        k, m = a.shape
        _, n = b.shape
        a_spec = pl.BlockSpec((k, tm), lambda i, j: (0, i))
        b_spec = pl.BlockSpec((k, tn), lambda i, j: (0, j))
        dims = ((0,), (0,))
    assert m % tm == 0 and n % tn == 0, (name, m, n, tm, tn)
    o_spec = pl.BlockSpec((tm, tn), lambda i, j: (i, j))
    in_specs = [a_spec, b_spec]
    args = [a, b]
    if extra is not None:
        in_specs.append(o_spec)
        args.append(extra)

    def body(*refs):
        a_ref, b_ref, o_ref = refs[0], refs[1], refs[-1]
        av = a_ref[...]
        if a_fn is not None:
            av = a_fn(av)
        acc = lax.dot_general(av.astype(BF16), b_ref[...].astype(BF16), (dims, ((), ())), preferred_element_type=F32)
        if epi is not None:
            acc = epi(acc, refs[2][...])
        o_ref[...] = acc.astype(out_dtype)

    out_shape = jax.ShapeDtypeStruct((m, n), out_dtype)
    if dest_major:
        assert extra is None
        o_spec = pl.BlockSpec((None, tm, tn), lambda i, j: (j, i, 0))
        out_shape = jax.ShapeDtypeStruct((N_DEV, m, tn), out_dtype)
    (out,), jouts = _hosted(body, jobs, grid=(m // tm, n // tn), in_specs=in_specs, out_specs=[o_spec],
                            out_shape=[out_shape], args=args, name=name)
    return (out, jouts) if jobs else out


def _mm_pieces(pieces, offsets, b, *, tm, name, extra, epi, jobs=()):
    m = pieces[0].shape[0]
    kb, n = b.shape
    tm = min(tm, m)
    row = lambda wdt: pl.BlockSpec((tm, wdt), lambda i: (i, 0))
    in_specs = [row(pc.shape[1]) for pc in pieces] + [pl.BlockSpec((kb, n), lambda i: (0, 0)), row(n)]
    np_ = len(pieces)

    def body(*refs):
        b_ref, e_ref, o_ref = refs[np_], refs[np_ + 1], refs[np_ + 2]
        acc = jnp.zeros((tm, n), F32)
        for q in range(np_):
            kq = pieces[q].shape[1]
            acc = acc + jnp.dot(refs[q][...].astype(BF16), b_ref[offsets[q]:offsets[q] + kq, :].astype(BF16),
                                preferred_element_type=F32)
        o_ref[...] = epi(acc, e_ref[...])

    (out,), jouts = _hosted(body, jobs, grid=(m // tm,), in_specs=in_specs, out_specs=[row(n)],
                            out_shape=[jax.ShapeDtypeStruct((m, n), F32)], args=list(pieces) + [b, extra], name=name)
    return (out, jouts) if jobs else out


def _relu2(v):
    r = jnp.maximum(v, 0.0)
    return r * r


ROW_TILE = 256


def _ln_stats(t):
    mu = jnp.mean(t, axis=-1, keepdims=True)
    xc = t - mu
    var = jnp.mean(xc * xc, axis=-1, keepdims=True)
    rstd = lax.rsqrt(var + LN_EPS)
    return xc * rstd, rstd


def _ln_bwd_rows(dy, xhat, rstd, g):
    dxh = dy * g
    m1 = jnp.mean(dxh, axis=-1, keepdims=True)
    m2 = jnp.mean(dxh * xhat, axis=-1, keepdims=True)
    return rstd * (dxh - m1 - xhat * m2)


def _ln_fwd(a, b, g, beta, *, name):
    s, d = a.shape
    row = pl.BlockSpec((ROW_TILE, d), lambda i: (i, 0))
    par = pl.BlockSpec((1, d), lambda i: (0, 0))

    def body(a_ref, b_ref, g_ref, be_ref, y_ref, yb_ref):
        xhat, _ = _ln_stats(ALPHA * a_ref[...] + b_ref[...])
        y = xhat * g_ref[...] + be_ref[...]
        y_ref[...] = y
        yb_ref[...] = y.astype(BF16)

    return _pcall(body, grid=(s // ROW_TILE,), in_specs=[row, row, par, par], out_specs=(row, row),
                  out_shape=(jax.ShapeDtypeStruct((s, d), F32), jax.ShapeDtypeStruct((s, d), BF16)), name=name,
                  compiler_params=_cparams(("parallel",)))(a, b, g, beta)


def _ln_bwd(a, b, g, dys, coefs, *, name):
    s, d = a.shape
    row = pl.BlockSpec((ROW_TILE, d), lambda i: (i, 0))
    par = pl.BlockSpec((1, d), lambda i: (0, 0))
    n = len(dys)

    def body(*refs):
        a_ref, b_ref, g_ref = refs[:3]
        dy_refs = refs[3:3 + n]
        dt_ref, dtb_ref, dg_ref, db_ref = refs[3 + n:]
        xhat, rstd = _ln_stats(ALPHA * a_ref[...] + b_ref[...])
        dy = coefs[0] * dy_refs[0][...]
        for q in range(1, n):
            dy = dy + coefs[q] * dy_refs[q][...]
        dt = _ln_bwd_rows(dy, xhat, rstd, g_ref[...])
        dt_ref[...] = dt
        dtb_ref[...] = dt.astype(BF16)

        @pl.when(pl.program_id(0) == 0)
        def _():
            dg_ref[...] = jnp.zeros_like(dg_ref)
            db_ref[...] = jnp.zeros_like(db_ref)

        dg_ref[...] += jnp.sum(dy * xhat, axis=0, keepdims=True)
        db_ref[...] += jnp.sum(dy, axis=0, keepdims=True)

    return _pcall(body, grid=(s // ROW_TILE,), in_specs=[row, row, par] + [row] * n, out_specs=(row, row, par, par),
                  out_shape=(jax.ShapeDtypeStruct((s, d), F32), jax.ShapeDtypeStruct((s, d), BF16),
                             jax.ShapeDtypeStruct((1, d), F32), jax.ShapeDtypeStruct((1, d), F32)),
                  name=name, compiler_params=_cparams(("arbitrary",)))(a, b, g, *dys)


def _head(x2, gpre, ple, g, beta, tgt, *, name):
    s, d = x2.shape
    row = pl.BlockSpec((ROW_TILE, d), lambda i: (i, 0))
    par = pl.BlockSpec((1, d), lambda i: (0, 0))
    lsp = pl.BlockSpec((1, LANE), lambda i: (0, 0))

    def body(x2_ref, gp_ref, ple_ref, g_ref, be_ref, t_ref, loss_ref, dgp_ref, dple_ref, dt_ref, dg_ref, db_ref):
        gate = _sigmoid(gp_ref[...])
        ple_v = ple_ref[...]
        xhat, rstd = _ln_stats(ALPHA * x2_ref[...] + gate * ple_v)
        err = xhat * g_ref[...] + be_ref[...] - t_ref[...]
        dy = err * (1.0 / d)
        dt = _ln_bwd_rows(dy, xhat, rstd, g_ref[...])
        dt_ref[...] = dt
        dgp_ref[...] = (dt * ple_v * gate * (1.0 - gate)).astype(BF16)
        dple_ref[...] = (dt * gate).astype(BF16)

        @pl.when(pl.program_id(0) == 0)
        def _():
            loss_ref[...] = jnp.zeros_like(loss_ref)
            dg_ref[...] = jnp.zeros_like(dg_ref)
            db_ref[...] = jnp.zeros_like(db_ref)

        loss_ref[...] += 0.5 * jnp.sum(jnp.mean(err * err, axis=-1, keepdims=True))
        dg_ref[...] += jnp.sum(dy * xhat, axis=0, keepdims=True)
        db_ref[...] += jnp.sum(dy, axis=0, keepdims=True)

    sd = jax.ShapeDtypeStruct((s, d), F32)
    sb = jax.ShapeDtypeStruct((s, d), BF16)
    pd = jax.ShapeDtypeStruct((1, d), F32)
    return _pcall(body, grid=(s // ROW_TILE,), in_specs=[row, row, row, par, par, row],
                  out_specs=(lsp, row, row, row, par, par),
                  out_shape=(jax.ShapeDtypeStruct((1, LANE), F32), sb, sb, sd, pd, pd),
                  name=name, compiler_params=_cparams(("arbitrary",)))(x2, gpre, ple, g, beta, tgt)


CONV_R = 256
PAD = SUBLANE


def _shift_down(ext, s):
    if s == 0:
        return ext[PAD:, :]
    return pltpu.roll(ext, s, 0)[PAD:, :]


def _shift_up(ext, s):
    r = ext.shape[0] - PAD
    if s == 0:
        return ext[:r, :]
    return pltpu.roll(ext, r + PAD - s, 0)[:r, :]


def _conv_rows(xpad_ref, r0, w_ref):
    ext = xpad_ref[pl.ds(r0, CONV_R + PAD), :]
    acc = _shift_down(ext, 0) * w_ref[3:4, :]
    for k in range(3):
        acc = acc + _shift_down(ext, 3 - k) * w_ref[k:k + 1, :]
    return acc, ext


def _fill_front_padded(dst_ref, src_ref, s):
    dst_ref[0:PAD, :] = jnp.zeros((PAD, dst_ref.shape[1]), F32)

    def cp(q, _):
        r0 = pl.multiple_of(q * CONV_R, CONV_R)
        dst_ref[pl.ds(pl.multiple_of(PAD + r0, PAD), CONV_R), :] = src_ref[pl.ds(r0, CONV_R), :]
        return 0

    lax.fori_loop(0, s // CONV_R, cp, 0)


def _conv_silu_fwd(proj, w8, b, *, col0, width, ct, name, jobs=()):
    s = proj.shape[0]
    nb = col0 // ct

    def body(x_ref, w_ref, b_ref, o_ref, xpad):
        _fill_front_padded(xpad, x_ref, s)

        def step(q, _):
            r0 = pl.multiple_of(q * CONV_R, CONV_R)
            acc, _e = _conv_rows(xpad, r0, w_ref)
            pre = acc + b_ref[...]
            o_ref[pl.ds(r0, CONV_R), :] = pre * _sigmoid(pre)
            return 0

        lax.fori_loop(0, s // CONV_R, step, 0)

    (out,), jouts = _hosted(
        body, jobs, grid=(width // ct,),
        in_specs=[pl.BlockSpec((s, ct), lambda j: (0, nb + j)), pl.BlockSpec((SUBLANE, ct), lambda j: (0, j)),
                  pl.BlockSpec((1, ct), lambda j: (0, j))],
        out_specs=[pl.BlockSpec((s, ct), lambda j: (0, j))],
        out_shape=[jax.ShapeDtypeStruct((s, width), F32)],
        scratch_shapes=[pltpu.VMEM((s + PAD, ct), F32)], name=name, args=(proj, w8, b))
    return (out, jouts) if jobs else out


def _conv_bwd_rows(dpad_ref, r0, w_ref):
    return _conv_bwd_ext(dpad_ref[pl.ds(r0, CONV_R + PAD), :], w_ref)


def _conv_bwd_ext(ext, w_ref):
    acc = _shift_up(ext, 0) * w_ref[3:4, :]
    for k in range(3):
        acc = acc + _shift_up(ext, 3 - k) * w_ref[k:k + 1, :]
    return acc


def _conv_silu_bwd(proj, dact, w8, b, *, col0, width, ct, name, jobs=()):
    s = proj.shape[0]
    nb = col0 // ct

    def body(x_ref, d_ref, w_ref, b_ref, dx_ref, dwb_ref, xpad, dpad):
        _fill_front_padded(xpad, x_ref, s)
        dpad[pl.ds(s, PAD), :] = jnp.zeros((PAD, ct), F32)
        dwb_ref[...] = jnp.zeros_like(dwb_ref)

        def step(q, _):
            r0 = pl.multiple_of(q * CONV_R, CONV_R)
            acc, ext = _conv_rows(xpad, r0, w_ref)
            pre = acc + b_ref[...]
            sg = _sigmoid(pre)
            dpre = d_ref[pl.ds(r0, CONV_R), :] * sg * (1.0 + pre * (1.0 - sg))
            dpad[pl.ds(r0, CONV_R), :] = dpre
            for k in range(4):
                dwb_ref[k:k + 1, :] += jnp.sum(dpre * _shift_down(ext, 3 - k), axis=0, keepdims=True)
            dwb_ref[4:5, :] += jnp.sum(dpre, axis=0, keepdims=True)
            return 0

        lax.fori_loop(0, s // CONV_R, step, 0)

        def step2(q, _):
            r0 = pl.multiple_of(q * CONV_R, CONV_R)
            dx_ref[pl.ds(r0, CONV_R), :] = _conv_bwd_rows(dpad, r0, w_ref).astype(BF16)
            return 0

        lax.fori_loop(0, s // CONV_R, step2, 0)

    colb = pl.BlockSpec((s, ct), lambda j: (0, j))
    outs, jouts = _hosted(
        body, jobs, grid=(width // ct,),
        in_specs=[pl.BlockSpec((s, ct), lambda j: (0, nb + j)), colb, pl.BlockSpec((SUBLANE, ct), lambda j: (0, j)),
                  pl.BlockSpec((1, ct), lambda j: (0, j))],
        out_specs=(colb, pl.BlockSpec((SUBLANE, ct), lambda j: (0, j))),
        out_shape=(jax.ShapeDtypeStruct((s, width), BF16), jax.ShapeDtypeStruct((SUBLANE, width), F32)),
        scratch_shapes=[pltpu.VMEM((s + PAD, ct), F32), pltpu.VMEM((s + PAD, ct), F32)], name=name,
        args=(proj, dact, w8, b))
    return (tuple(outs), jouts) if jobs else tuple(outs)


LRU_CT = 128


def _row_of(v, r):
    return jnp.sum(jnp.where(_iota((v.shape[0], 1), 0) == r, v, 0.0), axis=0, keepdims=True)


def _scan_fwd(a, u):
    r = a.shape[0]
    row = _iota((r, 1), 0)
    d = 1
    while d < r:
        valid = row >= d
        u = jnp.where(valid, a * pltpu.roll(u, d, 0) + u, u)
        a = jnp.where(valid, a * pltpu.roll(a, d, 0), a)
        d *= 2
    return a, u


def _scan_rev(b, u):
    r = b.shape[0]
    row = _iota((r, 1), 0)
    d = 1
    while d < r:
        valid = row < r - d
        u = jnp.where(valid, b * pltpu.roll(u, r - d, 0) + u, u)
        b = jnp.where(valid, b * pltpu.roll(b, r - d, 0), b)
        d *= 2
    return b, u


def _lru_chunk(xpad, r0, cw_ref, cb, wa, ba, wx, bx, sp):
    acc, ext = _conv_rows(xpad, r0, cw_ref)
    xl = acc + cb
    r = _sigmoid(_dot(xl, wa) + ba)
    i = _sigmoid(_dot(xl, wx) + bx)
    la = -LRU_C * r * sp
    a = jnp.exp(la)
    a2 = jnp.exp(2.0 * la)
    mult = jnp.sqrt(-jnp.tanh(la) * (a2 + 1.0))
    first = (r0 + _iota((CONV_R, 1), 0)) == 0
    mult = jnp.where(first, 1.0, mult)
    return ext, xl, r, i, a, a2, mult, first


def _lru_specs(s):
    ct = LRU_CT
    nb_g = COL_G // ct
    return dict(
        x=pl.BlockSpec((s, ct), lambda j: (0, j)),
        g=pl.BlockSpec((s, ct), lambda j: (0, nb_g + j)),
        col=pl.BlockSpec((s, ct), lambda j: (0, j)),
        cw=pl.BlockSpec((SUBLANE, ct), lambda j: (0, j)),
        vec=pl.BlockSpec((1, ct), lambda j: (0, j)),
        gate=pl.BlockSpec((None, ct, ct), lambda j: (j, 0, 0)),
    )


def _lru_fwd(proj, cw8, cb, wa_bd, ba, wx_bd, bx, ap, *, name, jobs=()):
    s = proj.shape[0]
    ct = LRU_CT
    sp_ = _lru_specs(s)

    def body(x_ref, g_ref, cw_ref, cb_ref, wa_ref, ba_ref, wx_ref, bx_ref, ap_ref, y_ref, h_ref, xpad):
        _fill_front_padded(xpad, x_ref, s)
        sp = _softplus(-ap_ref[...])

        def step(q, carry):
            r0 = pl.multiple_of(q * CONV_R, CONV_R)
            _e, xl, _r, i, a, _a2, mult, _f = _lru_chunk(xpad, r0, cw_ref, cb_ref[...], wa_ref[...], ba_ref[...],
                                                       wx_ref[...], bx_ref[...], sp)
            acum, ucum = _scan_fwd(a, xl * i * mult)
            h = acum * carry + ucum
            h_ref[pl.ds(r0, CONV_R), :] = h
            ge, _th = _gelu(g_ref[pl.ds(r0, CONV_R), :])
            y_ref[pl.ds(r0, CONV_R), :] = (ge * h).astype(BF16)
            return _row_of(h, CONV_R - 1)

        lax.fori_loop(0, s // CONV_R, step, jnp.zeros((1, ct), F32))

    (ymix, hs), jouts = _hosted(
        body, jobs, grid=(LRU_W // ct,),
        in_specs=[sp_["x"], sp_["g"], sp_["cw"], sp_["vec"], sp_["gate"], sp_["vec"], sp_["gate"], sp_["vec"], sp_["vec"]],
        out_specs=(sp_["col"], sp_["col"]),
        out_shape=(jax.ShapeDtypeStruct((s, LRU_W + SSD_W), BF16), jax.ShapeDtypeStruct((s, LRU_W), F32)),
        scratch_shapes=[pltpu.VMEM((s + PAD, ct), F32)],
        name=name, args=(proj, proj, cw8, cb, wa_bd, ba, wx_bd, bx, ap))
    return ((ymix, hs), jouts) if jobs else (ymix, hs)


def _lru_bwd(proj, dy, hs, cw8, cb, wa_bd, ba, wx_bd, bx, ap, *, name, jobs=()):
    s = proj.shape[0]
    ct = LRU_CT
    sp_ = _lru_specs(s)

    nq = s // CONV_R

    def body(x_ref, g_ref, dy_ref, h_ref, cw_ref, cb_ref, wa_ref, ba_ref, wx_ref, bx_ref, ap_ref,
             dx_ref, dg_ref, dcwb_ref, dwa_ref, dwx_ref, xpad, hpad):
        _fill_front_padded(xpad, x_ref, s)
        _fill_front_padded(hpad, h_ref, s)
        apv = ap_ref[...]
        sp = _softplus(-apv)
        cb_v, wa, ba_v, wx, bx_v = cb_ref[...], wa_ref[...], ba_ref[...], wx_ref[...], bx_ref[...]
        dcwb_ref[...] = jnp.zeros_like(dcwb_ref)
        dwa_ref[...] = jnp.zeros_like(dwa_ref)
        dwx_ref[...] = jnp.zeros_like(dwx_ref)

        def back(k, carry):
            g_next, a_next, dxl_next = carry
            last_row = _iota((CONV_R, 1), 0) == CONV_R - 1
            r0 = pl.multiple_of((nq - 1 - k) * CONV_R, CONV_R)
            ext, xl, r, i, a, a2, mult, first = _lru_chunk(xpad, r0, cw_ref, cb_v, wa, ba_v, wx, bx_v, sp)
            gv = g_ref[pl.ds(r0, CONV_R), :]
            dyv = dy_ref[pl.ds(r0, CONV_R), :]
            hext = hpad[pl.ds(r0, CONV_R + PAD), :]
            ge, th = _gelu(gv)
            dg_ref[pl.ds(r0, CONV_R), :] = (dyv * _shift_down(hext, 0) * _gelu_grad(gv, th)).astype(BF16)
            b = jnp.where(last_row, a_next, pltpu.roll(a, CONV_R - 1, 0))
            bcum, dcum = _scan_rev(b, dyv * ge)
            gval = dcum + bcum * g_next
            hprev = _shift_down(hext, 1)
            da = gval * hprev
            dxl = gval * i * mult
            di = gval * xl * mult
            dmult = jnp.where(first, 0.0, gval * xl * i)
            dla = da * a - dmult * a2 / mult
            dr = dla * (-LRU_C) * sp
            dcwb_ref[7:8, :] += jnp.sum(dla * (-LRU_C) * r, axis=0, keepdims=True)
            dpr = dr * r * (1.0 - r)
            dpi = di * i * (1.0 - i)
            dxl = dxl + _dot_nt(dpr, wa) + _dot_nt(dpi, wx)
            dwa_ref[...] += _dot_tn(xl, dpr)
            dwx_ref[...] += _dot_tn(xl, dpi)
            dcwb_ref[5:6, :] += jnp.sum(dpr, axis=0, keepdims=True)
            dcwb_ref[6:7, :] += jnp.sum(dpi, axis=0, keepdims=True)
            for tap in range(4):
                dcwb_ref[tap:tap + 1, :] += jnp.sum(dxl * _shift_down(ext, 3 - tap), axis=0, keepdims=True)
            dcwb_ref[4:5, :] += jnp.sum(dxl, axis=0, keepdims=True)
            dx_ref[pl.ds(r0, CONV_R), :] = _conv_bwd_ext(jnp.concatenate([dxl, dxl_next], axis=0), cw_ref).astype(BF16)
            return _row_of(gval, 0), _row_of(a, 0), dxl[:PAD, :]

        zero = jnp.zeros((1, ct), F32)
        lax.fori_loop(0, nq, back, (zero, zero, jnp.zeros((PAD, ct), F32)))
        dcwb_ref[7:8, :] = dcwb_ref[7:8, :] * (-_sigmoid(-apv))

    nt = LRU_W // ct
    outs, jouts = _hosted(
        body, jobs, grid=(nt,),
        in_specs=[sp_["x"], sp_["g"], sp_["col"], sp_["col"], sp_["cw"], sp_["vec"], sp_["gate"], sp_["vec"], sp_["gate"],
                  sp_["vec"], sp_["vec"]],
        out_specs=(sp_["col"], sp_["col"], sp_["cw"], sp_["gate"], sp_["gate"]),
        out_shape=(jax.ShapeDtypeStruct((s, LRU_W), BF16), jax.ShapeDtypeStruct((s, LRU_W), BF16),
                   jax.ShapeDtypeStruct((SUBLANE, LRU_W), F32), jax.ShapeDtypeStruct((nt, ct, ct), F32),
                   jax.ShapeDtypeStruct((nt, ct, ct), F32)),
        scratch_shapes=[pltpu.VMEM((s + PAD, ct), F32), pltpu.VMEM((s + PAD, ct), F32)],
        name=name, args=(proj, proj, dy, hs, cw8, cb, wa_bd, ba, wx_bd, bx, ap))
    return (tuple(outs), jouts) if jobs else tuple(outs)


def _split3(v):
    hi = v.astype(BF16)
    r1 = v - hi.astype(F32)
    mid = r1.astype(BF16)
    lo = (r1 - mid.astype(F32)).astype(BF16)
    return hi, mid, lo


def _dot01(m01, v):
    mb = m01.astype(BF16)
    hi, mid, lo = _split3(v)
    f = lambda part: jnp.dot(mb, part, preferred_element_type=F32)
    return f(hi) + f(mid) + f(lo)


def _dot01_r(v, m01):
    mb = m01.astype(BF16)
    hi, mid, lo = _split3(v)
    f = lambda part: jnp.dot(part, mb, preferred_element_type=F32)
    return f(hi) + f(mid) + f(lo)


def _ssd_prep(dtr, bias, alog_pad):
    l = CHUNK
    lane = _iota((1, LANE), 1)
    a_head = jnp.where(lane < N_HEAD, -jnp.exp(alog_pad), 0.0)
    dt = _softplus(dtr + bias)
    tril = (_iota((l, l), 1) <= _iota((l, l), 0)).astype(F32)
    a = dt * a_head
    cs = _dot01(tril, a)
    tot = jnp.sum(a, axis=0, keepdims=True)
    return dict(a_head=a_head, dt=dt, tril=tril, cs=cs, tot=tot)


def _col(v, h):
    lane = _iota(v.shape, 1)
    return jnp.sum(jnp.where(lane == h, v, 0.0), axis=1, keepdims=True)


def _decay_mat(cs, cst_ref, h, causal):
    row = cst_ref[h:h + 1, :]
    return jnp.exp(jnp.where(causal, _col(cs, h) - row, NEG_BIG))


def _head_mask(j, rows=CHUNK):
    lane = _iota((rows, GROUP_W), 1)
    return (lane >= j * HEAD_P) & (lane < (j + 1) * HEAD_P)


def _over_heads(v, g):
    r = v.shape[0]
    out = jnp.zeros((r, GROUP_W), F32)
    for j in range(4):
        out = jnp.where(_head_mask(j, r), _col(v, 4 * g + j), out)
    return out


def _ssd_group_fwd(q, g, xs_g, bg, cg, ht_g, cst_ref, causal, dx_g):
    dtx_g, csx_g, totx_g = _over_heads(q["dt"], g), _over_heads(q["cs"], g), _over_heads(q["tot"], g)
    xdt = xs_g * dtx_g
    ex = jnp.exp(csx_g)
    cb = _dot_nt(cg, bg)
    yoff = _dot(cg, ht_g) * ex
    ydiag = jnp.zeros((CHUNK, GROUP_W), F32)
    for j in range(4):
        sc = cb * _decay_mat(q["cs"], cst_ref, 4 * g + j, causal)
        ydiag = jnp.where(_head_mask(j), _dot(sc, xdt), ydiag)
    y = ydiag + yoff + xs_g * dx_g
    dsx = jnp.exp(totx_g - csx_g)
    return y, dict(xdt=xdt, ex=ex, cb=cb, yoff=yoff, dsx=dsx, dtx=dtx_g, totx=totx_g)


def _gated_norm_fwd(y_g, z_g, w_g):
    sz = _sigmoid(z_g)
    silu = z_g * sz
    yf = y_g * silu
    rs = lax.rsqrt(jnp.mean(yf * yf, axis=1, keepdims=True) + RMS_EPS)
    yn = yf * rs
    return yn * w_g, (sz, silu, rs, yn)


def _ssd_fwd(xact, proj, ymix, bias_pad, alog_pad, dxp, normw, *, name, jobs=()):
    s = xact.shape[0]
    nc = s // CHUNK

    def body(xa_ref, dt_ref, z_ref, _ymix_ref, bias_ref, alp_ref, dx_ref, nw_ref, y_ref, hp_ref, ht, cst):
        @pl.when(pl.program_id(0) == 0)
        def _():
            ht[...] = jnp.zeros_like(ht)

        hp_ref[...] = ht[...]
        q = _ssd_prep(dt_ref[...], bias_ref[...], alp_ref[...])
        cst[...] = q["cs"].T
        causal = q["tril"] > 0.0
        for g in range(N_GROUP):
            sl = slice(g * GROUP_W, (g + 1) * GROUP_W)
            xs_g = xa_ref[:, sl]
            bg = xa_ref[:, SSD_W + g * N_STATE:SSD_W + (g + 1) * N_STATE]
            cg = xa_ref[:, SSD_W + N_GROUP * N_STATE + g * N_STATE:SSD_W + N_GROUP * N_STATE + (g + 1) * N_STATE]
            ht_g = ht[:, sl]
            y, f = _ssd_group_fwd(q, g, xs_g, bg, cg, ht_g, cst, causal, dx_ref[:, sl])
            out, _ = _gated_norm_fwd(y, z_ref[:, sl], nw_ref[:, sl])
            y_ref[:, sl] = out.astype(BF16)
            ht[:, sl] = jnp.exp(f["totx"]) * ht_g + _dot_tn(bg, f["xdt"] * f["dsx"])

    par = lambda w: pl.BlockSpec((1, w), lambda c: (0, 0))
    (ycat, hprev), jouts = _hosted(
        body, jobs, grid=(nc,),
        in_specs=[pl.BlockSpec((CHUNK, XBC), lambda c: (c, 0)),
                  pl.BlockSpec((CHUNK, LANE), lambda c: (c, COL_DT // LANE)),
                  pl.BlockSpec((CHUNK, SSD_W), lambda c: (c, COL_Z // SSD_W)),
                  ANY_SPEC, par(LANE), par(LANE), par(SSD_W), par(SSD_W)],
        out_specs=(pl.BlockSpec((CHUNK, SSD_W), lambda c: (c, LRU_W // SSD_W)),
                   pl.BlockSpec((None, N_STATE, SSD_W), lambda c: (c, 0, 0))),
        out_shape=(jax.ShapeDtypeStruct(ymix.shape, ymix.dtype), jax.ShapeDtypeStruct((nc, N_STATE, SSD_W), F32)),
        scratch_shapes=[pltpu.VMEM((N_STATE, SSD_W), F32), pltpu.VMEM((CHUNK, LANE), F32)],
        aliases={3: 0}, name=name, args=(xact, proj, proj, ymix, bias_pad, alog_pad, dxp, normw))
    return ((ycat, hprev), jouts) if jobs else (ycat, hprev)


def _ssd_bwd(xact, proj, dycat, hprev, bias_pad, alog_pad, dxp, normw, *, name, jobs=()):
    s = xact.shape[0]
    nc = s // CHUNK
    l = CHUNK

    def body(xa_ref, dt_ref, z_ref, dy_ref, hp_ref, bias_ref, alp_ref, dx_ref, nw_ref,
             dxa_ref, ddt_ref, dz_ref, dnw_ref, small_ref, dht, cst, accx, dcsx_s, ddtx_s):
        step = pl.program_id(0)

        @pl.when(step == 0)
        def _():
            dht[...] = jnp.zeros_like(dht)
            accx[...] = jnp.zeros_like(accx)
            dnw_ref[...] = jnp.zeros_like(dnw_ref)
            small_ref[...] = jnp.zeros_like(small_ref)

        dtr = dt_ref[...]
        q = _ssd_prep(dtr, bias_ref[...], alp_ref[...])
        cst[...] = q["cs"].T
        causal = q["tril"] > 0.0
        eye = _iota((l, l), 0) == _iota((l, l), 1)
        lane = _iota((l, LANE), 1)
        dcs_head = jnp.zeros((l, LANE), F32)
        for g in range(N_GROUP):
            sl = slice(g * GROUP_W, (g + 1) * GROUP_W)
            slb = slice(SSD_W + g * N_STATE, SSD_W + (g + 1) * N_STATE)
            slc = slice(SSD_W + N_GROUP * N_STATE + g * N_STATE, SSD_W + N_GROUP * N_STATE + (g + 1) * N_STATE)
            xs_g, bg, cg = xa_ref[:, sl], xa_ref[:, slb], xa_ref[:, slc]
            ht_g = hp_ref[:, sl]
            dxp_g = dx_ref[:, sl]
            y, f = _ssd_group_fwd(q, g, xs_g, bg, cg, ht_g, cst, causal, dxp_g)
            z_g, nw_g = z_ref[:, sl], nw_ref[:, sl]
            _o, (sz, silu, rs, yn) = _gated_norm_fwd(y, z_g, nw_g)
            dout = dy_ref[:, sl]
            dnw_ref[:, sl] += jnp.sum(dout * yn, axis=0, keepdims=True)
            dyn = dout * nw_g
            dyf = rs * (dyn - yn * jnp.mean(dyn * yn, axis=1, keepdims=True))
            dy = dyf * silu
            dz_ref[:, sl] = (dyf * y * sz * (1.0 + z_g * (1.0 - sz))).astype(BF16)
            accx[0:1, sl] += jnp.sum(dy * xs_g, axis=0, keepdims=True)
            dyo = dy * f["ex"]
            dcg = _dot_nt(dyo, ht_g)
            dht_prev = _dot_tn(cg, dyo)
            dcsx = dy * f["yoff"]
            xdt = f["xdt"]
            dxdt = jnp.zeros((l, GROUP_W), F32)
            dcb = jnp.zeros((l, l), F32)
            for j in range(4):
                h = 4 * g + j
                lm = _decay_mat(q["cs"], cst, h, causal)
                sc = f["cb"] * lm
                mask = _head_mask(j)
                ds_ = jnp.where(causal, _dot_nt(jnp.where(mask, dy, 0.0), xdt), 0.0)
                dxdt = jnp.where(mask, _dot_tn(sc, dy), dxdt)
                dcb = dcb + ds_ * lm
                m = ds_ * sc
                rsum = jnp.sum(m, axis=1, keepdims=True)
                csum = jnp.sum(m, axis=0, keepdims=True)
                csum_col = jnp.sum(jnp.where(eye, csum, 0.0), axis=1, keepdims=True)
                dcs_head = dcs_head + jnp.where(lane == h, rsum - csum_col, 0.0)
            dhn = dht[:, sl]
            etot = jnp.exp(f["totx"])
            dxd = _dot(bg, dhn)
            dbg = _dot_nt(xdt * f["dsx"], dhn)
            dxdt = dxdt + dxd * f["dsx"]
            qq = dxd * xdt * f["dsx"]
            dcsx = dcsx - qq
            dtot = jnp.sum(qq, axis=0, keepdims=True) + jnp.sum(dhn * ht_g, axis=0, keepdims=True) * etot
            dht[:, sl] = etot * dhn + dht_prev
            dcg = dcg + _dot(dcb, bg)
            dbg = dbg + _dot_tn(dcb, cg)
            dxa_ref[:, sl] = dxdt * f["dtx"] + dy * dxp_g
            dxa_ref[:, slb] = dbg
            dxa_ref[:, slc] = dcg
            dcsx_s[:, sl] = dcsx
            ddtx_s[:, sl] = dxdt * xs_g
            accx[2:3, sl] = dtot
        reduce = (jnp.right_shift(_iota((SSD_W, LANE), 0), 6) == _iota((SSD_W, LANE), 1)).astype(F32)
        triu = (_iota((l, l), 1) >= _iota((l, l), 0)).astype(F32)
        dtot = _dot01_r(accx[...], reduce)[2:3, :]
        da_head = _dot01(triu, dcs_head + _dot01_r(dcsx_s[...], reduce)) + dtot
        ddt = _dot01_r(ddtx_s[...], reduce) + da_head * q["a_head"]
        small_ref[1:2, :] += jnp.sum(da_head * q["dt"], axis=0, keepdims=True)
        ddtr = ddt * _sigmoid(dtr + bias_ref[...])
        ddt_ref[...] = ddtr.astype(BF16)
        small_ref[0:1, :] += jnp.sum(ddtr, axis=0, keepdims=True)

        @pl.when(step == nc - 1)
        def _():
            small_ref[1:2, :] = small_ref[1:2, :] * q["a_head"]
            small_ref[2:3, :] = _dot01_r(accx[...], reduce)[0:1, :]

    rev = lambda c: nc - 1 - c
    par = lambda w: pl.BlockSpec((1, w), lambda c: (0, 0))
    outs, jouts = _hosted(
        body, jobs, grid=(nc,),
        in_specs=[pl.BlockSpec((CHUNK, XBC), lambda c: (rev(c), 0)),
                  pl.BlockSpec((CHUNK, LANE), lambda c: (rev(c), COL_DT // LANE)),
                  pl.BlockSpec((CHUNK, SSD_W), lambda c: (rev(c), COL_Z // SSD_W)),
                  pl.BlockSpec((CHUNK, SSD_W), lambda c: (rev(c), 1)),
                  pl.BlockSpec((None, N_STATE, SSD_W), lambda c: (rev(c), 0, 0)),
                  par(LANE), par(LANE), par(SSD_W), par(SSD_W)],
        out_specs=(pl.BlockSpec((CHUNK, XBC), lambda c: (rev(c), 0)),
                   pl.BlockSpec((CHUNK, LANE), lambda c: (rev(c), 0)),
                   pl.BlockSpec((CHUNK, SSD_W), lambda c: (rev(c), 0)),
                   par(SSD_W), pl.BlockSpec((SUBLANE, LANE), lambda c: (0, 0))),
        out_shape=(jax.ShapeDtypeStruct((s, XBC), F32), jax.ShapeDtypeStruct((s, LANE), BF16),
                   jax.ShapeDtypeStruct((s, SSD_W), BF16), jax.ShapeDtypeStruct((1, SSD_W), F32),
                   jax.ShapeDtypeStruct((SUBLANE, LANE), F32)),
        scratch_shapes=[pltpu.VMEM((N_STATE, SSD_W), F32), pltpu.VMEM((CHUNK, LANE), F32),
                        pltpu.VMEM((SUBLANE, SSD_W), F32), pltpu.VMEM((CHUNK, SSD_W), F32),
                        pltpu.VMEM((CHUNK, SSD_W), F32)],
        name=name, args=(xact, proj, proj, dycat, hprev, bias_pad, alog_pad, dxp, normw))
    return (tuple(outs), jouts) if jobs else tuple(outs)


def _blockdiag(w):
    w2 = w.reshape(N_HEAD // 2, 2, HEAD_P, HEAD_P)
    z = jnp.zeros((N_HEAD // 2, HEAD_P, HEAD_P), w.dtype)
    top = jnp.concatenate([w2[:, 0], z], axis=2)
    bot = jnp.concatenate([z, w2[:, 1]], axis=2)
    return jnp.concatenate([top, bot], axis=1)


def _unblockdiag(wbd):
    a = wbd[:, :HEAD_P, :HEAD_P]
    b = wbd[:, HEAD_P:, HEAD_P:]
    return jnp.stack([a, b], axis=1).reshape(N_HEAD, HEAD_P, HEAD_P)


def _pad_rows8(w):
    return jnp.concatenate([w, jnp.zeros((SUBLANE - w.shape[0], w.shape[1]), w.dtype)], axis=0)


def _pad_lane(v):
    return jnp.concatenate([v, jnp.zeros((1, LANE - v.shape[1]), v.dtype)], axis=1)


class _NoExchange:
    def ride(self, host):
        return []

    def done(self, jobs, outs, w):
        pass

    def grad(self, name, val):
        pass

    def small(self, raw):
        pass


def _local_step(x, p, tgt, w, hooks=_NoExchange()):
    cw_l = _pad_rows8(w["lru_conv_w"])
    cw_s = _pad_rows8(w["ssd_conv_w"])
    wa_bd = _blockdiag(w["lru_gate_a_w"])
    wx_bd = _blockdiag(w["lru_gate_x_w"])
    ba = w["lru_gate_a_b"].reshape(1, LRU_W)
    bx = w["lru_gate_x_b"].reshape(1, LRU_W)
    bias_pad = _pad_lane(w["ssd_dt_bias"])
    alog_pad = _pad_lane(w["ssd_a_log"])
    dxp = jnp.repeat(w["ssd_d"], HEAD_P, axis=1)

    def host(fn, *a, name, **k):
        jobs = hooks.ride(name)
        res = fn(*a, name=name, jobs=jobs, **k)
        if jobs:
            res, jouts = res
            hooks.done(jobs, jouts, w)
        return res

    def grad(n, val):
        g[n] = val
        hooks.grad(n, val)

    xb = x.astype(BF16)
    proj = host(_mm, xb, w["w_in_t"], "nt", tm=1024, tn=512, name="in_proj")
    ymix, h_lru = host(_lru_fwd, proj, cw_l, w["lru_conv_b"], wa_bd, ba, wx_bd, bx, w["lru_a_param"], name="lru_fwd")
    xact = host(_conv_silu_fwd, proj, cw_s, w["ssd_conv_b"], col0=COL_XBC, width=XBC, ct=256, name="ssd_conv_fwd")
    ycat, hprev = host(_ssd_fwd, xact, proj, ymix, bias_pad, alog_pad, dxp, w["ssd_norm_w"], name="ssd_fwd")
    mix = _mm(ycat, w["w_out"], "nn", tm=1024, tn=1024, name="out_proj")
    x1, x1b = _ln_fwd(x, mix, w["ln1_g"], w["ln1_b"], name="ln1_fwd")
    pre = _mm(x1b, w["w_ff1"], "nn", tm=1024, tn=512, out_dtype=BF16, name="ff1")
    ff = _mm(pre, w["w_ff2"], "nn", tm=512, tn=1024, a_fn=_relu2, name="ff2")
    x2, x2b = _ln_fwd(x1, ff, w["ln2_g"], w["ln2_b"], name="ln2_fwd")
    gpre = _mm(x2b, w["w_ple_gate"], "nn", tm=1024, tn=1024, name="ple_gate")
    ple = _mm(p, w["w_ple"], "nn", tm=1024, tn=1024, name="ple_proj")
    loss, dgpre, dple, dt3, dg3, db3 = _head(x2, gpre, ple, w["ln3_g"], w["ln3_b"], tgt, name="head")

    g = {}
    g["ln3_g"], g["ln3_b"] = dg3, db3
    grad("w_ple_gate", _mm(x2b, dgpre, "tn", tm=512, tn=1024, out_dtype=BF16, name="d_w_ple_gate"))
    grad("w_ple", _mm(p, dple, "tn", tm=256, tn=512, dest_major=True, out_dtype=BF16, name="d_w_ple"))
    dx2_mm = host(_mm, dgpre, w["w_ple_gate"], "nt", tm=1024, tn=1024, name="d_x2")
    dt2, dt2b, g["ln2_g"], g["ln2_b"] = _ln_bwd(x1, ff, w["ln2_g"], [dt3, dx2_mm], [ALPHA, 1.0], name="ln2_bwd")
    grad("w_ff2", host(_mm, pre, dt2b, "tn", tm=512, tn=1024, a_fn=_relu2, out_dtype=BF16, name="d_w_ff2"))
    dpre = host(_mm, dt2b, w["w_ff2"], "nt", tm=1024, tn=512, extra=pre, out_dtype=BF16,
                epi=lambda acc, pv: acc * 2.0 * jnp.maximum(pv.astype(F32), 0.0), name="d_pre")
    grad("w_ff1", host(_mm, x1b, dpre, "tn", tm=1024, tn=512, dest_major=True, out_dtype=BF16, name="d_w_ff1"))
    dx1_mm = host(_mm, dpre, w["w_ff1"], "nt", tm=512, tn=1024, name="d_x1")
    dt1, dt1b, g["ln1_g"], g["ln1_b"] = _ln_bwd(x, mix, w["ln1_g"], [dt2, dx1_mm], [ALPHA, 1.0], name="ln1_bwd")
    grad("w_out", host(_mm, ycat, dt1b, "tn", tm=512, tn=1024, out_dtype=BF16, name="d_w_out"))
    dycat = host(_mm, dt1b, w["w_out"], "nt", tm=1024, tn=1024, name="d_ycat")
    dxl, dgl, dcwb_l, dwa, dwx = host(_lru_bwd, proj, dycat, h_lru, cw_l, w["lru_conv_b"], wa_bd, ba, wx_bd, bx,
                                      w["lru_a_param"], name="lru_bwd")
    g["lru_gate_a_w"] = _unblockdiag(dwa)
    g["lru_gate_x_w"] = _unblockdiag(dwx)
    raw = dict(lru=dcwb_l, gate_a=g["lru_gate_a_w"].reshape(N_HEAD * HEAD_P, HEAD_P).astype(BF16),
               gate_x=g["lru_gate_x_w"].reshape(N_HEAD * HEAD_P, HEAD_P).astype(BF16))
    hooks.small(raw)
    dxact, ddt, dz, g["ssd_norm_w"], small = host(_ssd_bwd, xact, proj, dycat, hprev, bias_pad, alog_pad, dxp,
                                                   w["ssd_norm_w"], name="ssd_bwd")
    dxbc, dcwb_s = host(_conv_silu_bwd, proj, dxact, cw_s, w["ssd_conv_b"], col0=COL_XBC, width=XBC, ct=256,
                        name="ssd_conv_bwd")
    pieces, offsets = [dxl, dgl, dz, dxbc, ddt], [0, COL_G, COL_Z, COL_XBC, COL_DT]

    g["lru_conv_w"] = dcwb_l[0:4]
    g["lru_conv_b"] = dcwb_l[4:5]
    g["lru_gate_a_b"] = dcwb_l[5:6]
    g["lru_gate_x_b"] = dcwb_l[6:7]
    g["lru_a_param"] = dcwb_l[7:8]
    g["ssd_conv_w"] = dcwb_s[0:4]
    g["ssd_conv_b"] = dcwb_s[4:5]
    g["ssd_dt_bias"] = small[0:1, :N_HEAD]
    g["ssd_a_log"] = small[1:2, :N_HEAD]
    g["ssd_d"] = small[2:3, :N_HEAD]
    rows = jnp.concatenate([g[n] for n in ("ssd_norm_w", "ln1_g", "ln1_b", "ln2_g", "ln2_b", "ln3_g", "ln3_b")]
                           + [jnp.broadcast_to(loss[:, 0:1], (1, D_MODEL))], axis=0)
    late = dict(ssd=dcwb_s, heads=small, rows=rows)
    hooks.small(late)
    raw.update(late)
    dwt = [host(_mm, pc, xb, "tn", tm=512, tn=1024, out_dtype=BF16, name="d_w_in_%d" % q)
           for q, pc in enumerate(pieces)]
    grad("w_in", jnp.concatenate(dwt, axis=0))
    grad_x = host(_mm_pieces, pieces, offsets, w["w_in_t"], tm=256, extra=dt1, epi=lambda acc, e: acc + ALPHA * e,
                  name="d_x")
    return loss[0, 0], grad_x, g, raw


ANY_SPEC = pl.BlockSpec(memory_space=pl.ANY)


def _mesh_pos():
    return lax.axis_index("x"), lax.axis_index("y"), lax.axis_index("c")


def _remote(src, dst, send, recv, k, to):
    return pltpu.make_async_remote_copy(src_ref=src, dst_ref=dst, send_sem=send.at[k], recv_sem=recv.at[k],
                                        device_id=to, device_id_type=MESH_T)


class _Job:
    N_SEM = 7

    def __init__(self, kind, inp):
        self.kind, self.inp = kind, inp
        shape = {"gather": (N_DEV,) + inp.shape, "pair": (4,) + inp.shape[1:], "chip": inp.shape}[kind]
        self.out = jax.ShapeDtypeStruct(shape, inp.dtype)

    def _places(self):
        x, y, c = _mesh_pos()
        return (x, y, c), (x, y, 1 - c), [(1 - x, y), (x, 1 - y), (1 - x, 1 - y)]

    def start(self, inp, out, send, recv, loc):
        me, sibling, chips = self._places()
        x, y, c = me
        if self.kind == "gather":
            mine = out.at[4 * x + 2 * y + c]
            pltpu.make_async_copy(inp, mine, loc.at[0]).start()
            _remote(inp, mine, send, recv, 0, sibling).start()
            for j, chip in enumerate(chips):
                _remote(inp, mine, send, recv, 1 + j, (*chip, c)).start()
        elif self.kind == "pair":
            for k in range(4):
                _remote(inp.at[2 * k + (1 - c)], out.at[k], send, recv, k, sibling).start()
        else:
            kme = 2 * x + y
            pltpu.make_async_copy(inp.at[kme], out.at[kme], loc.at[0]).start()
            for j, (tx, ty) in enumerate(chips):
                _remote(inp.at[2 * tx + ty], out.at[kme], send, recv, j, (tx, ty, c)).start()

    def mid(self, inp, out, send, recv, loc):
        if self.kind != "gather":
            return
        me, sibling, chips = self._places()
        c = me[2]
        for j, chip in enumerate(chips):
            landed = out.at[4 * chip[0] + 2 * chip[1] + c]
            _remote(landed, landed, send, recv, 1 + j, me).wait_recv()
            _remote(landed, landed, send, recv, 4 + j, sibling).start()

    def finish(self, inp, out, send, recv, loc):
        me, sibling, chips = self._places()
        x, y, c = me
        if self.kind == "gather":
            blk = lambda px, py, pc: out.at[4 * px + 2 * py + pc]
            mine = blk(*me)
            _remote(inp, blk(*sibling), send, recv, 0, me).wait_recv()
            for j, chip in enumerate(chips):
                _remote(inp, blk(*chip, 1 - c), send, recv, 4 + j, me).wait_recv()
            for k in range(7):
                _remote(inp, mine, send, recv, k, sibling).wait_send()
            pltpu.make_async_copy(inp, mine, loc.at[0]).wait()
        elif self.kind == "pair":
            for k in range(4):
                _remote(inp.at[2 * k + (1 - c)], out.at[k], send, recv, k, sibling).wait()
        else:
            kme = 2 * x + y
            for j, (tx, ty) in enumerate(chips):
                _remote(inp.at[kme], out.at[2 * tx + ty], send, recv, j, (tx, ty, c)).wait_recv()
            for j, (tx, ty) in enumerate(chips):
                _remote(inp.at[2 * tx + ty], out.at[kme], send, recv, j, (tx, ty, c)).wait_send()
            pltpu.make_async_copy(inp.at[kme], out.at[kme], loc.at[0]).wait()


def _job_scratch(jobs):
    sem = pltpu.SemaphoreType.DMA
    return [s for _ in jobs for s in (sem((_Job.N_SEM,)), sem((_Job.N_SEM,)), sem((1,)))]


def _run_jobs(jobs, method, jins, jouts, jsems):
    for q, job in enumerate(jobs):
        getattr(job, method)(jins[q], jouts[q], *jsems[3 * q:3 * q + 3])


def _exchange(jobs, *, name):
    n = len(jobs)

    def body(*refs):
        jins, jouts, jsems = refs[:n], refs[n:2 * n], refs[2 * n:]
        _run_jobs(jobs, "start", jins, jouts, jsems)
        _run_jobs(jobs, "mid", jins, jouts, jsems)
        _run_jobs(jobs, "finish", jins, jouts, jsems)

    return _pcall(body, in_specs=[ANY_SPEC] * n, out_specs=[ANY_SPEC] * n, out_shape=[j.out for j in jobs],
                  scratch_shapes=_job_scratch(jobs), name=name)(*[j.inp for j in jobs])


def _hosted(body, jobs, *, grid, in_specs, out_specs, out_shape, args, name, scratch_shapes=(), aliases=None):
    in_specs, out_specs, out_shape = list(in_specs), list(out_specs), list(out_shape)
    scratch_shapes = list(scratch_shapes)
    n_in, n_out, n_scr, nj = len(in_specs), len(out_specs), len(scratch_shapes), len(jobs)
    sem = ("arbitrary",) * len(grid)
    kw = dict(input_output_aliases=aliases) if aliases else {}
    if not jobs:
        res = _pcall(body, grid=grid, in_specs=in_specs, out_specs=out_specs, out_shape=out_shape,
                     scratch_shapes=scratch_shapes, name=name, compiler_params=_cparams(sem), **kw)(*args)
        return list(res), []

    def full(*refs):
        ins, jins = refs[:n_in], refs[n_in:n_in + nj]
        o0 = n_in + nj
        outs, jouts = refs[o0:o0 + n_out], refs[o0 + n_out:o0 + n_out + nj]
        s0 = o0 + n_out + nj
        scr, jsems = refs[s0:s0 + n_scr], refs[s0 + n_scr:]
        step = pl.program_id(0)
        for ax in range(1, len(grid)):
            step = step * grid[ax] + pl.program_id(ax)
        total = math.prod(grid)

        @pl.when(step == 0)
        def _():
            _run_jobs(jobs, "start", jins, jouts, jsems)

        body(*ins, *outs, *scr)

        @pl.when(step == total - 1)
        def _():
            _run_jobs(jobs, "mid", jins, jouts, jsems)
            _run_jobs(jobs, "finish", jins, jouts, jsems)

    res = _pcall(full, grid=grid, in_specs=in_specs + [ANY_SPEC] * nj, out_specs=out_specs + [ANY_SPEC] * nj,
                 out_shape=out_shape + [j.out for j in jobs], scratch_shapes=scratch_shapes + _job_scratch(jobs),
                 name=name, compiler_params=_cparams(sem), **kw)(*args, *[j.inp for j in jobs])
    return list(res[:n_out]), list(res[n_out:])


def _pair_add(g8, r4, cidx, *, name):
    _, r, c = g8.shape
    tr = ROW_TILE if r % ROW_TILE == 0 else r

    def body(c_ref, g_ref, r_ref, o_ref):
        o_ref[...] = (g_ref[...].astype(F32) + r_ref[...].astype(F32)).astype(BF16)

    return _pcall(
        body,
        grid_spec=pltpu.PrefetchScalarGridSpec(
            num_scalar_prefetch=1, grid=(4, r // tr),
            in_specs=[pl.BlockSpec((None, tr, c), lambda k, i, cr: (2 * k + cr[0], i, 0)),
                      pl.BlockSpec((None, tr, c), lambda k, i, cr: (k, i, 0))],
            out_specs=pl.BlockSpec((None, tr, c), lambda k, i, cr: (k, i, 0))),
        out_shape=jax.ShapeDtypeStruct((4, r, c), BF16), name=name,
        compiler_params=_cparams(("parallel", "parallel")))(cidx, g8, r4)


def _adam_update(g, w_ref, m_ref, v_ref, g_ref, d_ref, mo_ref, vo_ref):
    c1 = 1.0 - ADAM_B1 ** ADAM_STEP
    c2 = 1.0 - ADAM_B2 ** ADAM_STEP
    m2 = ADAM_B1 * m_ref[...] + (1.0 - ADAM_B1) * g
    v2 = ADAM_B2 * v_ref[...] + (1.0 - ADAM_B2) * (g * g)
    g_ref[...] = g
    mo_ref[...] = m2
    vo_ref[...] = v2
    d_ref[...] = -ADAM_LR * ((m2 / c1) / (jnp.sqrt(v2 / c2) + ADAM_EPS) + ADAM_WD * w_ref[...])


def _adamw_rows(srcs, items, own_cols, me1, *, name):
    ns, ni, no = len(srcs), len(items), len(own_cols)
    full = lambda a: pl.BlockSpec(a.shape, lambda i, me: (0,) * a.ndim)
    in_specs = [full(a) for a in srcs]
    args = list(srcs)
    for (si, _r0, w, _m, _v) in own_cols:
        a = srcs[si]
        in_specs.append(pl.BlockSpec((N_DEV, a.shape[1], w.shape[1]), lambda i, me: (0, 0, me[0])))
        args.append(a)
    out_specs, out_shape = [], []
    for (_si, _r0, w, m, v) in list(items) + list(own_cols):
        in_specs += [full(w)] * 3
        args += [w, m, v]
        out_specs += [full(w)] * 4
        out_shape += [jax.ShapeDtypeStruct(w.shape, F32)] * 4

    def body(me_ref, *refs):
        src_refs, own_refs = refs[:ns], refs[ns:ns + no]
        wmv = refs[ns + no:ns + no + 3 * (ni + no)]
        outs = refs[ns + no + 3 * (ni + no):]
        for q, (si, r0, w, _m, _v) in enumerate(list(items) + list(own_cols)):
            nr, cw = w.shape
            gref = src_refs[si] if q < ni else own_refs[q - ni]
            g = gref[0, r0:r0 + nr, 0:cw]
            for d in range(1, N_DEV):
                g = g + gref[d, r0:r0 + nr, 0:cw]
            _adam_update(g, *wmv[3 * q:3 * q + 3], *outs[4 * q:4 * q + 4])

    res = _pcall(
        body,
        grid_spec=pltpu.PrefetchScalarGridSpec(num_scalar_prefetch=1, grid=(1,), in_specs=in_specs, out_specs=out_specs),
        out_shape=out_shape, name=name, compiler_params=_cparams(("arbitrary",)))(me1, *args)
    return [tuple(res[4 * q:4 * q + 4]) for q in range(ni + no)]


def _adamw(gsrc, w, m, v, *, name):
    k, r, c = gsrc.shape
    tr = ROW_TILE if r % ROW_TILE == 0 else r

    def body(gs_ref, w_ref, m_ref, v_ref, g_ref, d_ref, mo_ref, vo_ref):
        g = gs_ref[0].astype(F32)
        for q in range(1, k):
            g = g + gs_ref[q].astype(F32)
        _adam_update(g, w_ref, m_ref, v_ref, g_ref, d_ref, mo_ref, vo_ref)

    tc = c
    if tr == r and r > ROW_TILE and c % 256 == 0:
        tc = 256
    blk = pl.BlockSpec((tr, tc), lambda i, j: (i, j))
    sd = jax.ShapeDtypeStruct((r, c), F32)
    return _pcall(body, grid=(r // tr, c // tc),
                  in_specs=[pl.BlockSpec((k, tr, tc), lambda i, j: (0, i, j)), blk, blk, blk],
                  out_specs=(blk, blk, blk, blk), out_shape=(sd, sd, sd, sd), name=name,
                  compiler_params=_cparams(("parallel", "parallel")))(gsrc, w, m, v)


WEIGHTS = ['w_in', 'lru_conv_w', 'lru_conv_b', 'lru_gate_a_w', 'lru_gate_a_b', 'lru_gate_x_w', 'lru_gate_x_b',
           'lru_a_param', 'ssd_conv_w', 'ssd_conv_b', 'ssd_dt_bias', 'ssd_a_log', 'ssd_d', 'ssd_norm_w', 'w_out',
           'ln1_g', 'ln1_b', 'w_ff1', 'w_ff2', 'ln2_g', 'ln2_b', 'w_ple_gate', 'w_ple', 'ln3_g', 'ln3_b']
BIG = ['w_in', 'w_out', 'w_ff1', 'w_ff2', 'w_ple_gate', 'w_ple']
COL_SHARDED = ('w_ff1', 'w_ple')
CONV = ['lru_conv_w', 'ssd_conv_w']
REPL = [n for n in WEIGHTS if n not in BIG and n not in CONV]
CONV_CH = {'lru_conv_w': LRU_W, 'ssd_conv_w': XBC}


def _to_dest_major(name, gfull):
    if name == 'w_in':
        gfull = gfull[:D_IN]
    if name in COL_SHARDED:
        r, cfull = gfull.shape
        return gfull.reshape(r, N_DEV, cfull // N_DEV).transpose(1, 0, 2)
    rfull, cdim = gfull.shape
    return gfull.reshape(N_DEV, rfull // N_DEV, cdim)


def _full_weight(name, gathered):
    if name in COL_SHARDED:
        _, r, cs = gathered.shape
        full = gathered.transpose(1, 0, 2).reshape(r, N_DEV * cs)
    else:
        _, rs, cdim = gathered.shape
        full = gathered.reshape(N_DEV * rs, cdim)
    if name == 'w_in':
        full = jnp.concatenate([full, jnp.zeros((D_IN_PAD - D_IN, D_MODEL), full.dtype)], axis=0)
    return full


SMALL_SRC = ("lru", "ssd", "heads", "rows", "gate_a", "gate_x")
AG_HOSTS = {"in_proj": ("w_ff1",), "lru_fwd": ("w_ff2",), "ssd_conv_fwd": ("w_ple_gate", "w_ple"), "ssd_fwd": ("w_out",)}
PAIR_HOSTS = ("d_x2", "d_pre", "d_x1", "d_ycat")
CHIP_HOSTS = {"lru_bwd": ("w_ple_gate", "w_ple", "w_ff2"), "ssd_bwd": ("w_ff1",), "d_x": ("w_out",)}
SMALL_HOSTS = {"ssd_bwd": ("lru", "gate_a", "gate_x"), "d_w_in_3": ("ssd", "heads", "rows")}


class _Schedule:
    def __init__(self, shards, cidx):
        self.shards, self.cidx = shards, cidx
        self.pair, self.chip, self.small_jobs = [], [], []
        self.dest, self.summed, self.gathered_small = {}, {}, {}
        self.tags = []

    def ride(self, host):
        tags = []
        if host in AG_HOSTS:
            tags = [("weight", n, self.shards[n]) for n in AG_HOSTS[host]]
        elif host in PAIR_HOSTS or host in CHIP_HOSTS or host == "flush":
            tags = [("pair", n, a) for n, a in self.pair]
            self.pair = []
            if host not in PAIR_HOSTS:
                take = [t for t in self.chip if host == "flush" or t[0] in CHIP_HOSTS[host]]
                tags += [("chip", n, a) for n, a in take]
                self.chip = [t for t in self.chip if not any(t is u for u in take)]
        if host in SMALL_HOSTS:
            tags += [("small", n, a) for n, a in self.small_jobs if n in SMALL_HOSTS[host]]
            self.small_jobs = [t for t in self.small_jobs if t[0] not in SMALL_HOSTS[host]]
        self.tags = tags
        return [_Job({"weight": "gather", "small": "gather"}.get(kind, kind), a) for kind, _n, a in tags]

    def done(self, jobs, outs, w):
        for (kind, n, _a), o in zip(self.tags, outs):
            if kind == "weight":
                w[n] = _full_weight(n, o)
            elif kind == "small":
                self.gathered_small[n] = o
            elif kind == "pair":
                self.chip.append((n, _pair_add(self.dest[n], o, self.cidx, name="rs_pair_add_" + n)))
            else:
                self.summed[n] = o

    def grad(self, name, val):
        self.dest[name] = val if val.ndim == 3 else _to_dest_major(name, val)
        self.pair.append((name, self.dest[name]))

    def small(self, raw):
        self.small_jobs += list(raw.items())

    def flush(self):
        step = 0
        while self.pair or self.chip:
            jobs = self.ride("flush")
            self.done(jobs, _exchange(jobs, name="rs_flush_%d" % step), None)
            step += 1


def kernel(x, p, w_in, lru_conv_w, lru_conv_b, lru_gate_a_w, lru_gate_a_b, lru_gate_x_w, lru_gate_x_b, lru_a_param, ssd_conv_w, ssd_conv_b, ssd_dt_bias, ssd_a_log, ssd_d, ssd_norm_w, w_out, ln1_g, ln1_b, w_ff1, w_ff2, ln2_g, ln2_b, w_ple_gate, w_ple, ln3_g, ln3_b, loss_target, m_w_in, m_lru_conv_w, m_lru_conv_b, m_lru_gate_a_w, m_lru_gate_a_b, m_lru_gate_x_w, m_lru_gate_x_b, m_lru_a_param, m_ssd_conv_w, m_ssd_conv_b, m_ssd_dt_bias, m_ssd_a_log, m_ssd_d, m_ssd_norm_w, m_w_out, m_ln1_g, m_ln1_b, m_w_ff1, m_w_ff2, m_ln2_g, m_ln2_b, m_w_ple_gate, m_w_ple, m_ln3_g, m_ln3_b, v_w_in, v_lru_conv_w, v_lru_conv_b, v_lru_gate_a_w, v_lru_gate_a_b, v_lru_gate_x_w, v_lru_gate_x_b, v_lru_a_param, v_ssd_conv_w, v_ssd_conv_b, v_ssd_dt_bias, v_ssd_a_log, v_ssd_d, v_ssd_norm_w, v_w_out, v_ln1_g, v_ln1_b, v_w_ff1, v_w_ff2, v_ln2_g, v_ln2_b, v_w_ple_gate, v_w_ple, v_ln3_g, v_ln3_b):
    given = dict(locals())
    def local(a, n):
        return jnp.swapaxes(a[0], 0, 1) if n == 'w_in' else a[0]

    wsh = {n: local(given[n], n) for n in WEIGHTS}
    msh = {n: local(given["m_" + n], n) for n in WEIGHTS}
    vsh = {n: local(given["v_" + n], n) for n in WEIGHTS}
    xi, yi, ci = _mesh_pos()
    me = 4 * xi + 2 * yi + ci

    shards = {n: wsh[n].astype(BF16) for n in BIG}
    conv_pack = jnp.concatenate([_pad_rows8(wsh[n]) for n in CONV], axis=1)
    g_in, gconv = _exchange([_Job("gather", shards['w_in']), _Job("gather", conv_pack)], name="ag_first")
    full = {'w_in_t': _full_weight('w_in', g_in)}
    c0 = 0
    for n in CONV:
        cw = CONV_CH[n] // N_DEV
        full[n] = gconv[:, :4, c0:c0 + cw].transpose(1, 0, 2).reshape(4, CONV_CH[n])
        c0 += cw
    for n in REPL:
        full[n] = given[n] if given[n].ndim == 2 else wsh[n]

    sched = _Schedule(shards, jnp.reshape(ci, (1,)).astype(jnp.int32))
    loss_local, grad_x, g, raw = _local_step(x[0], p[0, 0], loss_target[0], full, sched)
    sched.flush()
    summed, gat = sched.summed, sched.gathered_small
    loss = gat["rows"][0, 7, 0]
    for d in range(1, N_DEV):
        loss = loss + gat["rows"][d, 7, 0]

    outs = {}
    for n in BIG:
        outs[n] = _adamw(summed[n], wsh[n], msh[n], vsh[n], name="adamw_" + n)
    for n, k in (("lru_gate_a_w", "gate_a"), ("lru_gate_x_w", "gate_x")):
        flat = lambda a: a.reshape(N_HEAD * HEAD_P, HEAD_P)
        res = _adamw(gat[k], flat(wsh[n]), flat(msh[n]), flat(vsh[n]), name="adamw_" + n)
        outs[n] = tuple(r.reshape(N_HEAD, HEAD_P, HEAD_P) for r in res)
    row_items = [("lru_conv_b", 0, 4), ("lru_gate_a_b", 0, 5), ("lru_gate_x_b", 0, 6), ("lru_a_param", 0, 7),
                 ("ssd_conv_b", 1, 4), ("ssd_dt_bias", 2, 0), ("ssd_a_log", 2, 1), ("ssd_d", 2, 2),
                 ("ssd_norm_w", 3, 0), ("ln1_g", 3, 1), ("ln1_b", 3, 2), ("ln2_g", 3, 3), ("ln2_b", 3, 4),
                 ("ln3_g", 3, 5), ("ln3_b", 3, 6)]
    vec = lambda a: a.reshape(1, -1)
    items = [(si, r0, vec(given[n]), vec(given["m_" + n]), vec(given["v_" + n])) for n, si, r0 in row_items]
    own = [(si, 0, wsh[n], msh[n], vsh[n]) for n, si in (("lru_conv_w", 0), ("ssd_conv_w", 1))]
    me1 = jnp.reshape(me, (1,)).astype(jnp.int32)
    res = _adamw_rows([gat[k] for k in SMALL_SRC[:4]], items, own, me1, name="adamw_small")
    for (n, _si, _r0), r4 in zip(row_items, res[:len(row_items)]):
        outs[n] = r4
    for n, r4 in zip(CONV, res[len(row_items):]):
        outs[n] = r4

    def fin(n, k):
        a = jnp.swapaxes(outs[n][k], 0, 1) if n == 'w_in' else outs[n][k]
        return a.reshape(given[n].shape)

    return (loss, grad_x[None],
            *[fin(n, 0) for n in WEIGHTS], *[fin(n, 1) for n in WEIGHTS],
            *[fin(n, 2) for n in WEIGHTS], *[fin(n, 3) for n in WEIGHTS])
```

```python
import math

import jax
import jax.numpy as jnp
from jax import lax
from jax.experimental import pallas as pl
from jax.experimental.pallas import tpu as pltpu

F32 = jnp.float32
BF16 = jnp.bfloat16
HI = lax.Precision.HIGHEST

N_DEV = 8
D_MODEL = 1024
LRU_W = 1024
SSD_W = 1024
XBC = 2048
N_HEAD = 16
HEAD_P = 64
N_GROUP = 4
GROUP_W = 256
N_STATE = 128
CHUNK = 128
D_FF = 4096
PLE_DIM = 256
D_IN = 5136
D_IN_PAD = 5632
COL_G = 1024
COL_Z = 2048
COL_XBC = 3072
COL_DT = 5120
LRU_C = 8.0
ALPHA = 2.0 ** 0.25
LN_EPS = 1e-5
RMS_EPS = 1e-5
ADAM_LR = 0.001
ADAM_B1 = 0.9
ADAM_B2 = 0.999
ADAM_EPS = 1e-08
ADAM_WD = 0.01
ADAM_STEP = 10
GELU_C = math.sqrt(2.0 / math.pi)
LANE = 128
SUBLANE = 8
VMEM_LIMIT = 48 * 1024 * 1024
MESH_T = pl.DeviceIdType.MESH
NEG_BIG = -1e30


def _pcall(body, **kw):
    return pl.pallas_call(body, **kw)


def _cparams(sem):
    return pltpu.CompilerParams(dimension_semantics=sem, vmem_limit_bytes=VMEM_LIMIT)


def _dot(a, b):
    return jnp.dot(a.astype(BF16), b.astype(BF16), preferred_element_type=F32)


def _dot_nt(a, b):
    return lax.dot_general(a.astype(BF16), b.astype(BF16), (((1,), (1,)), ((), ())), preferred_element_type=F32)


def _dot_tn(a, b):
    return lax.dot_general(a.astype(BF16), b.astype(BF16), (((0,), (0,)), ((), ())), preferred_element_type=F32)


def _dotx(a, b):
    return jnp.dot(a, b, precision=HI, preferred_element_type=F32)


def _sigmoid(x):
    return jax.nn.sigmoid(x)


def _softplus(v):
    return jnp.maximum(v, 0.0) + jnp.log1p(jnp.exp(-jnp.abs(v)))


def _gelu(x):
    th = jnp.tanh(GELU_C * (x + 0.044715 * x * x * x))
    return 0.5 * x * (1.0 + th), th


def _gelu_grad(x, th):
    return 0.5 * (1.0 + th) + 0.5 * x * (1.0 - th * th) * GELU_C * (1.0 + 3.0 * 0.044715 * x * x)


def _iota(shape, dim):
    return lax.broadcasted_iota(jnp.int32, shape, dim)


def _mm(a, b, mode, *, tm, tn, name, a_fn=None, extra=None, epi=None, out_dtype=F32, dest_major=False, into=None,
        jobs=()):
    m = a.shape[1] if mode == "tn" else a.shape[0]
    n = b.shape[0] if mode == "nt" else b.shape[1]
    tm, tn = min(tm, m), min(tn, n)
    if dest_major:
        tn = n // N_DEV
    if mode == "nn":
        m, k = a.shape
        _, n = b.shape
        a_spec = pl.BlockSpec((tm, k), lambda i, j: (i, 0))
        b_spec = pl.BlockSpec((k, tn), lambda i, j: (0, j))
        dims = ((1,), (0,))
    elif mode == "nt":
        m, k = a.shape
        n, _ = b.shape
        a_spec = pl.BlockSpec((tm, k), lambda i, j: (i, 0))
        b_spec = pl.BlockSpec((tn, k), lambda i, j: (j, 0))
        dims = ((1,), (1,))
    else:
        k, m = a.shape
        _, n = b.shape
        a_spec = pl.BlockSpec((k, tm), lambda i, j: (0, i))
        b_spec = pl.BlockSpec((k, tn), lambda i, j: (0, j))
        dims = ((0,), (0,))
    assert m % tm == 0 and n % tn == 0, (name, m, n, tm, tn)
    o_spec = pl.BlockSpec((tm, tn), lambda i, j: (i, j))
    in_specs = [a_spec, b_spec]
    args = [a, b]
    if extra is not None:
        in_specs.append(o_spec)
        args.append(extra)

    def body(*refs):
        a_ref, b_ref, o_ref = refs[0], refs[1], refs[-1]
        av = a_ref[...]
        if a_fn is not None:
            av = a_fn(av)
        acc = lax.dot_general(av.astype(BF16), b_ref[...].astype(BF16), (dims, ((), ())), preferred_element_type=F32)
        if epi is not None:
            acc = epi(acc, refs[2][...])
        o_ref[...] = acc.astype(out_dtype)

    out_shape = jax.ShapeDtypeStruct((m, n), out_dtype)
    aliases = None
    if dest_major:
        assert extra is None
        o_spec = pl.BlockSpec((None, tm, tn), lambda i, j: (j, i, 0))
        out_shape = jax.ShapeDtypeStruct((N_DEV, m, tn), out_dtype)
    if into is not None:
        buf, row0, total = into
        assert extra is None and row0 % tm == 0
        o_spec = pl.BlockSpec((tm, tn), lambda i, j: (row0 // tm + i, j))
        out_shape = jax.ShapeDtypeStruct((total, n), out_dtype)
        if buf is not None:
            in_specs.append(ANY_SPEC)
            args.append(buf)
            aliases = {len(args) - 1: 0}
    (out,), jouts = _hosted(body, jobs, grid=(m // tm, n // tn), in_specs=in_specs, out_specs=[o_spec],
                            out_shape=[out_shape], args=args, name=name, aliases=aliases)
    return (out, jouts) if jobs else out


def _mm_pieces(pieces, offsets, b, *, tm, name, extra, epi, jobs=()):
    m = pieces[0].shape[0]
    kb, n = b.shape
    tm = min(tm, m)
    row = lambda wdt: pl.BlockSpec((tm, wdt), lambda i: (i, 0))
    in_specs = [row(pc.shape[1]) for pc in pieces] + [pl.BlockSpec((kb, n), lambda i: (0, 0)), row(n)]
    np_ = len(pieces)

    def body(*refs):
        b_ref, e_ref, o_ref = refs[np_], refs[np_ + 1], refs[np_ + 2]
        acc = jnp.zeros((tm, n), F32)
        for q in range(np_):
            kq = pieces[q].shape[1]
            acc = acc + jnp.dot(refs[q][...].astype(BF16), b_ref[offsets[q]:offsets[q] + kq, :].astype(BF16),
                                preferred_element_type=F32)
        o_ref[...] = epi(acc, e_ref[...])

    (out,), jouts = _hosted(body, jobs, grid=(m // tm,), in_specs=in_specs, out_specs=[row(n)],
                            out_shape=[jax.ShapeDtypeStruct((m, n), F32)], args=list(pieces) + [b, extra], name=name)
    return (out, jouts) if jobs else out


def _relu2(v):
    r = jnp.maximum(v, 0.0)
    return r * r


ROW_TILE = 256


def _ln_stats(t):
    mu = jnp.mean(t, axis=-1, keepdims=True)
    xc = t - mu
    var = jnp.mean(xc * xc, axis=-1, keepdims=True)
    rstd = lax.rsqrt(var + LN_EPS)
    return xc * rstd, rstd


def _ln_bwd_rows(dy, xhat, rstd, g):
    dxh = dy * g
    m1 = jnp.mean(dxh, axis=-1, keepdims=True)
    m2 = jnp.mean(dxh * xhat, axis=-1, keepdims=True)
    return rstd * (dxh - m1 - xhat * m2)


def _ln_fwd(a, b, g, beta, *, name):
    s, d = a.shape
    row = pl.BlockSpec((ROW_TILE, d), lambda i: (i, 0))
    par = pl.BlockSpec((1, d), lambda i: (0, 0))

    def body(a_ref, b_ref, g_ref, be_ref, y_ref, yb_ref):
        xhat, _ = _ln_stats(ALPHA * a_ref[...] + b_ref[...])
        y = xhat * g_ref[...] + be_ref[...]
        y_ref[...] = y
        yb_ref[...] = y.astype(BF16)

    return _pcall(body, grid=(s // ROW_TILE,), in_specs=[row, row, par, par], out_specs=(row, row),
                  out_shape=(jax.ShapeDtypeStruct((s, d), F32), jax.ShapeDtypeStruct((s, d), BF16)), name=name,
                  compiler_params=_cparams(("parallel",)))(a, b, g, beta)


def _ln_bwd(a, b, g, dys, coefs, *, name):
    s, d = a.shape
    row = pl.BlockSpec((ROW_TILE, d), lambda i: (i, 0))
    par = pl.BlockSpec((1, d), lambda i: (0, 0))
    n = len(dys)

    def body(*refs):
        a_ref, b_ref, g_ref = refs[:3]
        dy_refs = refs[3:3 + n]
        dt_ref, dtb_ref, dg_ref, db_ref = refs[3 + n:]
        xhat, rstd = _ln_stats(ALPHA * a_ref[...] + b_ref[...])
        dy = coefs[0] * dy_refs[0][...]
        for q in range(1, n):
            dy = dy + coefs[q] * dy_refs[q][...]
        dt = _ln_bwd_rows(dy, xhat, rstd, g_ref[...])
        dt_ref[...] = dt
        dtb_ref[...] = dt.astype(BF16)

        @pl.when(pl.program_id(0) == 0)
        def _():
            dg_ref[...] = jnp.zeros_like(dg_ref)
            db_ref[...] = jnp.zeros_like(db_ref)

        dg_ref[...] += jnp.sum(dy * xhat, axis=0, keepdims=True)
        db_ref[...] += jnp.sum(dy, axis=0, keepdims=True)

    return _pcall(body, grid=(s // ROW_TILE,), in_specs=[row, row, par] + [row] * n, out_specs=(row, row, par, par),
                  out_shape=(jax.ShapeDtypeStruct((s, d), F32), jax.ShapeDtypeStruct((s, d), BF16),
                             jax.ShapeDtypeStruct((1, d), F32), jax.ShapeDtypeStruct((1, d), F32)),
                  name=name, compiler_params=_cparams(("arbitrary",)))(a, b, g, *dys)


def _head(x2, gpre, ple, g, beta, tgt, *, name):
    s, d = x2.shape
    row = pl.BlockSpec((ROW_TILE, d), lambda i: (i, 0))
    par = pl.BlockSpec((1, d), lambda i: (0, 0))
    lsp = pl.BlockSpec((1, LANE), lambda i: (0, 0))

    def body(x2_ref, gp_ref, ple_ref, g_ref, be_ref, t_ref, loss_ref, dgp_ref, dple_ref, dt_ref, dg_ref, db_ref):
        gate = _sigmoid(gp_ref[...])
        ple_v = ple_ref[...]
        xhat, rstd = _ln_stats(ALPHA * x2_ref[...] + gate * ple_v)
        err = xhat * g_ref[...] + be_ref[...] - t_ref[...]
        dy = err * (1.0 / d)
        dt = _ln_bwd_rows(dy, xhat, rstd, g_ref[...])
        dt_ref[...] = dt
        dgp_ref[...] = (dt * ple_v * gate * (1.0 - gate)).astype(BF16)
        dple_ref[...] = (dt * gate).astype(BF16)

        @pl.when(pl.program_id(0) == 0)
        def _():
            loss_ref[...] = jnp.zeros_like(loss_ref)
            dg_ref[...] = jnp.zeros_like(dg_ref)
            db_ref[...] = jnp.zeros_like(db_ref)

        loss_ref[...] += 0.5 * jnp.sum(jnp.mean(err * err, axis=-1, keepdims=True))
        dg_ref[...] += jnp.sum(dy * xhat, axis=0, keepdims=True)
        db_ref[...] += jnp.sum(dy, axis=0, keepdims=True)

    sd = jax.ShapeDtypeStruct((s, d), F32)
    sb = jax.ShapeDtypeStruct((s, d), BF16)
    pd = jax.ShapeDtypeStruct((1, d), F32)
    return _pcall(body, grid=(s // ROW_TILE,), in_specs=[row, row, row, par, par, row],
                  out_specs=(lsp, row, row, row, par, par),
                  out_shape=(jax.ShapeDtypeStruct((1, LANE), F32), sb, sb, sd, pd, pd),
                  name=name, compiler_params=_cparams(("arbitrary",)))(x2, gpre, ple, g, beta, tgt)


CONV_R = 256
PAD = SUBLANE


def _shift_down(ext, s):
    if s == 0:
        return ext[PAD:, :]
    return pltpu.roll(ext, s, 0)[PAD:, :]


def _shift_up(ext, s):
    r = ext.shape[0] - PAD
    if s == 0:
        return ext[:r, :]
    return pltpu.roll(ext, r + PAD - s, 0)[:r, :]


def _conv_rows(xpad_ref, r0, w_ref):
    ext = xpad_ref[pl.ds(r0, CONV_R + PAD), :]
    acc = _shift_down(ext, 0) * w_ref[3:4, :]
    for k in range(3):
        acc = acc + _shift_down(ext, 3 - k) * w_ref[k:k + 1, :]
    return acc, ext


def _fill_front_padded(dst_ref, src_ref, s):
    dst_ref[0:PAD, :] = jnp.zeros((PAD, dst_ref.shape[1]), F32)

    def cp(q, _):
        r0 = pl.multiple_of(q * CONV_R, CONV_R)
        dst_ref[pl.ds(pl.multiple_of(PAD + r0, PAD), CONV_R), :] = src_ref[pl.ds(r0, CONV_R), :]
        return 0

    lax.fori_loop(0, s // CONV_R, cp, 0)


def _conv_silu_fwd(proj, w8, b, *, col0, width, ct, name, jobs=()):
    s = proj.shape[0]
    nb = col0 // ct

    def body(x_ref, w_ref, b_ref, o_ref, xpad):
        _fill_front_padded(xpad, x_ref, s)

        def step(q, _):
            r0 = pl.multiple_of(q * CONV_R, CONV_R)
            acc, _e = _conv_rows(xpad, r0, w_ref)
            pre = acc + b_ref[...]
            o_ref[pl.ds(r0, CONV_R), :] = pre * _sigmoid(pre)
            return 0

        lax.fori_loop(0, s // CONV_R, step, 0)

    (out,), jouts = _hosted(
        body, jobs, grid=(width // ct,),
        in_specs=[pl.BlockSpec((s, ct), lambda j: (0, nb + j)), pl.BlockSpec((SUBLANE, ct), lambda j: (0, j)),
                  pl.BlockSpec((1, ct), lambda j: (0, j))],
        out_specs=[pl.BlockSpec((s, ct), lambda j: (0, j))],
        out_shape=[jax.ShapeDtypeStruct((s, width), F32)],
        scratch_shapes=[pltpu.VMEM((s + PAD, ct), F32)], name=name, args=(proj, w8, b))
    return (out, jouts) if jobs else out


def _conv_bwd_rows(dpad_ref, r0, w_ref):
    return _conv_bwd_ext(dpad_ref[pl.ds(r0, CONV_R + PAD), :], w_ref)


def _conv_bwd_ext(ext, w_ref):
    acc = _shift_up(ext, 0) * w_ref[3:4, :]
    for k in range(3):
        acc = acc + _shift_up(ext, 3 - k) * w_ref[k:k + 1, :]
    return acc


def _conv_silu_bwd(proj, dact, w8, b, *, col0, width, ct, name, jobs=()):
    s = proj.shape[0]
    nb = col0 // ct

    def body(x_ref, d_ref, w_ref, b_ref, dx_ref, dwb_ref, xpad, dpad):
        _fill_front_padded(xpad, x_ref, s)
        dpad[pl.ds(s, PAD), :] = jnp.zeros((PAD, ct), F32)
        dwb_ref[...] = jnp.zeros_like(dwb_ref)

        def step(q, _):
            r0 = pl.multiple_of(q * CONV_R, CONV_R)
            acc, ext = _conv_rows(xpad, r0, w_ref)
            pre = acc + b_ref[...]
            sg = _sigmoid(pre)
            dpre = d_ref[pl.ds(r0, CONV_R), :] * sg * (1.0 + pre * (1.0 - sg))
            dpad[pl.ds(r0, CONV_R), :] = dpre
            for k in range(4):
                dwb_ref[k:k + 1, :] += jnp.sum(dpre * _shift_down(ext, 3 - k), axis=0, keepdims=True)
            dwb_ref[4:5, :] += jnp.sum(dpre, axis=0, keepdims=True)
            return 0

        lax.fori_loop(0, s // CONV_R, step, 0)

        def step2(q, _):
            r0 = pl.multiple_of(q * CONV_R, CONV_R)
            dx_ref[pl.ds(r0, CONV_R), :] = _conv_bwd_rows(dpad, r0, w_ref).astype(BF16)
            return 0

        lax.fori_loop(0, s // CONV_R, step2, 0)

    colb = pl.BlockSpec((s, ct), lambda j: (0, j))
    outs, jouts = _hosted(
        body, jobs, grid=(width // ct,),
        in_specs=[pl.BlockSpec((s, ct), lambda j: (0, nb + j)), colb, pl.BlockSpec((SUBLANE, ct), lambda j: (0, j)),
                  pl.BlockSpec((1, ct), lambda j: (0, j))],
        out_specs=(colb, pl.BlockSpec((SUBLANE, ct), lambda j: (0, j))),
        out_shape=(jax.ShapeDtypeStruct((s, width), BF16), jax.ShapeDtypeStruct((SUBLANE, width), F32)),
        scratch_shapes=[pltpu.VMEM((s + PAD, ct), F32), pltpu.VMEM((s + PAD, ct), F32)], name=name,
        args=(proj, dact, w8, b))
    return (tuple(outs), jouts) if jobs else tuple(outs)


LRU_CT = 128


def _row_of(v, r):
    return jnp.sum(jnp.where(_iota((v.shape[0], 1), 0) == r, v, 0.0), axis=0, keepdims=True)


def _scan_fwd(a, u):
    r = a.shape[0]
    row = _iota((r, 1), 0)
    d = 1
    while d < r:
        valid = row >= d
        u = jnp.where(valid, a * pltpu.roll(u, d, 0) + u, u)
        a = jnp.where(valid, a * pltpu.roll(a, d, 0), a)
        d *= 2
    return a, u


def _scan_rev(b, u):
    r = b.shape[0]
    row = _iota((r, 1), 0)
    d = 1
    while d < r:
        valid = row < r - d
        u = jnp.where(valid, b * pltpu.roll(u, r - d, 0) + u, u)
        b = jnp.where(valid, b * pltpu.roll(b, r - d, 0), b)
        d *= 2
    return b, u


def _lru_chunk(xpad, r0, cw_ref, cb, wa, ba, wx, bx, sp):
    acc, ext = _conv_rows(xpad, r0, cw_ref)
    xl = acc + cb
    r = _sigmoid(_dot(xl, wa) + ba)
    i = _sigmoid(_dot(xl, wx) + bx)
    la = -LRU_C * r * sp
    a = jnp.exp(la)
    a2 = jnp.exp(2.0 * la)
    mult = jnp.sqrt(-jnp.tanh(la) * (a2 + 1.0))
    first = (r0 + _iota((CONV_R, 1), 0)) == 0
    mult = jnp.where(first, 1.0, mult)
    return ext, xl, r, i, a, a2, mult, first


def _lru_specs(s):
    ct = LRU_CT
    nb_g = COL_G // ct
    return dict(
        x=pl.BlockSpec((s, ct), lambda j: (0, j)),
        g=pl.BlockSpec((s, ct), lambda j: (0, nb_g + j)),
        col=pl.BlockSpec((s, ct), lambda j: (0, j)),
        cw=pl.BlockSpec((SUBLANE, ct), lambda j: (0, j)),
        vec=pl.BlockSpec((1, ct), lambda j: (0, j)),
        gate=pl.BlockSpec((None, ct, ct), lambda j: (j, 0, 0)),
    )


def _lru_fwd(proj, cw8, cb, wa_bd, ba, wx_bd, bx, ap, *, name, jobs=()):
    s = proj.shape[0]
    ct = LRU_CT
    sp_ = _lru_specs(s)

    def body(x_ref, g_ref, cw_ref, cb_ref, wa_ref, ba_ref, wx_ref, bx_ref, ap_ref, y_ref, h_ref, xpad):
        _fill_front_padded(xpad, x_ref, s)
        sp = _softplus(-ap_ref[...])

        def step(q, carry):
            r0 = pl.multiple_of(q * CONV_R, CONV_R)
            _e, xl, _r, i, a, _a2, mult, _f = _lru_chunk(xpad, r0, cw_ref, cb_ref[...], wa_ref[...], ba_ref[...],
                                                       wx_ref[...], bx_ref[...], sp)
            acum, ucum = _scan_fwd(a, xl * i * mult)
            h = acum * carry + ucum
            h_ref[pl.ds(r0, CONV_R), :] = h
            ge, _th = _gelu(g_ref[pl.ds(r0, CONV_R), :])
            y_ref[pl.ds(r0, CONV_R), :] = (ge * h).astype(BF16)
            return _row_of(h, CONV_R - 1)

        lax.fori_loop(0, s // CONV_R, step, jnp.zeros((1, ct), F32))

    (ymix, hs), jouts = _hosted(
        body, jobs, grid=(LRU_W // ct,),
        in_specs=[sp_["x"], sp_["g"], sp_["cw"], sp_["vec"], sp_["gate"], sp_["vec"], sp_["gate"], sp_["vec"], sp_["vec"]],
        out_specs=(sp_["col"], sp_["col"]),
        out_shape=(jax.ShapeDtypeStruct((s, LRU_W + SSD_W), BF16), jax.ShapeDtypeStruct((s, LRU_W), F32)),
        scratch_shapes=[pltpu.VMEM((s + PAD, ct), F32)],
        name=name, args=(proj, proj, cw8, cb, wa_bd, ba, wx_bd, bx, ap))
    return ((ymix, hs), jouts) if jobs else (ymix, hs)


def _lru_bwd(proj, dy, hs, cw8, cb, wa_bd, ba, wx_bd, bx, ap, *, name, jobs=()):
    s = proj.shape[0]
    ct = LRU_CT
    sp_ = _lru_specs(s)

    nq = s // CONV_R

    def body(x_ref, g_ref, dy_ref, h_ref, cw_ref, cb_ref, wa_ref, ba_ref, wx_ref, bx_ref, ap_ref,
             dx_ref, dg_ref, dcwb_ref, dwa_ref, dwx_ref, xpad, hpad):
        _fill_front_padded(xpad, x_ref, s)
        _fill_front_padded(hpad, h_ref, s)
        apv = ap_ref[...]
        sp = _softplus(-apv)
        cb_v, wa, ba_v, wx, bx_v = cb_ref[...], wa_ref[...], ba_ref[...], wx_ref[...], bx_ref[...]
        dcwb_ref[...] = jnp.zeros_like(dcwb_ref)
        dwa_ref[...] = jnp.zeros_like(dwa_ref)
        dwx_ref[...] = jnp.zeros_like(dwx_ref)

        def back(k, carry):
            g_next, a_next, dxl_next = carry
            last_row = _iota((CONV_R, 1), 0) == CONV_R - 1
            r0 = pl.multiple_of((nq - 1 - k) * CONV_R, CONV_R)
            ext, xl, r, i, a, a2, mult, first = _lru_chunk(xpad, r0, cw_ref, cb_v, wa, ba_v, wx, bx_v, sp)
            gv = g_ref[pl.ds(r0, CONV_R), :]
            dyv = dy_ref[pl.ds(r0, CONV_R), :]
            hext = hpad[pl.ds(r0, CONV_R + PAD), :]
            ge, th = _gelu(gv)
            dg_ref[pl.ds(r0, CONV_R), :] = (dyv * _shift_down(hext, 0) * _gelu_grad(gv, th)).astype(BF16)
            b = jnp.where(last_row, a_next, pltpu.roll(a, CONV_R - 1, 0))
            bcum, dcum = _scan_rev(b, dyv * ge)
            gval = dcum + bcum * g_next
            hprev = _shift_down(hext, 1)
            da = gval * hprev
            dxl = gval * i * mult
            di = gval * xl * mult
            dmult = jnp.where(first, 0.0, gval * xl * i)
            dla = da * a - dmult * a2 / mult
            dr = dla * (-LRU_C) * sp
            dcwb_ref[7:8, :] += jnp.sum(dla * (-LRU_C) * r, axis=0, keepdims=True)
            dpr = dr * r * (1.0 - r)
            dpi = di * i * (1.0 - i)
            dxl = dxl + _dot_nt(dpr, wa) + _dot_nt(dpi, wx)
            dwa_ref[...] += _dot_tn(xl, dpr)
            dwx_ref[...] += _dot_tn(xl, dpi)
            dcwb_ref[5:6, :] += jnp.sum(dpr, axis=0, keepdims=True)
            dcwb_ref[6:7, :] += jnp.sum(dpi, axis=0, keepdims=True)
            for tap in range(4):
                dcwb_ref[tap:tap + 1, :] += jnp.sum(dxl * _shift_down(ext, 3 - tap), axis=0, keepdims=True)
            dcwb_ref[4:5, :] += jnp.sum(dxl, axis=0, keepdims=True)
            dx_ref[pl.ds(r0, CONV_R), :] = _conv_bwd_ext(jnp.concatenate([dxl, dxl_next], axis=0), cw_ref).astype(BF16)
            return _row_of(gval, 0), _row_of(a, 0), dxl[:PAD, :]

        zero = jnp.zeros((1, ct), F32)
        lax.fori_loop(0, nq, back, (zero, zero, jnp.zeros((PAD, ct), F32)))
        dcwb_ref[7:8, :] = dcwb_ref[7:8, :] * (-_sigmoid(-apv))

    nt = LRU_W // ct
    outs, jouts = _hosted(
        body, jobs, grid=(nt,),
        in_specs=[sp_["x"], sp_["g"], sp_["col"], sp_["col"], sp_["cw"], sp_["vec"], sp_["gate"], sp_["vec"], sp_["gate"],
                  sp_["vec"], sp_["vec"]],
        out_specs=(sp_["col"], sp_["col"], sp_["cw"], sp_["gate"], sp_["gate"]),
        out_shape=(jax.ShapeDtypeStruct((s, LRU_W), BF16), jax.ShapeDtypeStruct((s, LRU_W), BF16),
                   jax.ShapeDtypeStruct((SUBLANE, LRU_W), F32), jax.ShapeDtypeStruct((nt, ct, ct), F32),
                   jax.ShapeDtypeStruct((nt, ct, ct), F32)),
        scratch_shapes=[pltpu.VMEM((s + PAD, ct), F32), pltpu.VMEM((s + PAD, ct), F32)],
        name=name, args=(proj, proj, dy, hs, cw8, cb, wa_bd, ba, wx_bd, bx, ap))
    return (tuple(outs), jouts) if jobs else tuple(outs)


def _split3(v):
    hi = v.astype(BF16)
    r1 = v - hi.astype(F32)
    mid = r1.astype(BF16)
    lo = (r1 - mid.astype(F32)).astype(BF16)
    return hi, mid, lo


def _dot01(m01, v):
    mb = m01.astype(BF16)
    hi, mid, lo = _split3(v)
    f = lambda part: jnp.dot(mb, part, preferred_element_type=F32)
    return f(hi) + f(mid) + f(lo)


def _dot01_r(v, m01):
    mb = m01.astype(BF16)
    hi, mid, lo = _split3(v)
    f = lambda part: jnp.dot(part, mb, preferred_element_type=F32)
    return f(hi) + f(mid) + f(lo)


def _ssd_prep(dtr, bias, alog_pad):
    l = CHUNK
    lane = _iota((1, LANE), 1)
    a_head = jnp.where(lane < N_HEAD, -jnp.exp(alog_pad), 0.0)
    dt = _softplus(dtr + bias)
    tril = (_iota((l, l), 1) <= _iota((l, l), 0)).astype(F32)
    a = dt * a_head
    cs = _dot01(tril, a)
    tot = jnp.sum(a, axis=0, keepdims=True)
    return dict(a_head=a_head, dt=dt, tril=tril, cs=cs, tot=tot)


def _col(v, h):
    lane = _iota(v.shape, 1)
    return jnp.sum(jnp.where(lane == h, v, 0.0), axis=1, keepdims=True)


def _decay_mat(cs, cst_ref, h, causal):
    row = cst_ref[h:h + 1, :]
    return jnp.exp(jnp.where(causal, _col(cs, h) - row, NEG_BIG))


def _head_mask(j, rows=CHUNK):
    lane = _iota((rows, GROUP_W), 1)
    return (lane >= j * HEAD_P) & (lane < (j + 1) * HEAD_P)


def _over_heads(v, g):
    r = v.shape[0]
    out = jnp.zeros((r, GROUP_W), F32)
    for j in range(4):
        out = jnp.where(_head_mask(j, r), _col(v, 4 * g + j), out)
    return out


def _ssd_group_fwd(q, g, xs_g, bg, cg, ht_g, cst_ref, causal, dx_g):
    dtx_g, csx_g, totx_g = _over_heads(q["dt"], g), _over_heads(q["cs"], g), _over_heads(q["tot"], g)
    xdt = xs_g * dtx_g
    ex = jnp.exp(csx_g)
    cb = _dot_nt(cg, bg)
    yoff = _dot(cg, ht_g) * ex
    ydiag = jnp.zeros((CHUNK, GROUP_W), F32)
    for j in range(4):
        sc = cb * _decay_mat(q["cs"], cst_ref, 4 * g + j, causal)
        ydiag = jnp.where(_head_mask(j), _dot(sc, xdt), ydiag)
    y = ydiag + yoff + xs_g * dx_g
    dsx = jnp.exp(totx_g - csx_g)
    return y, dict(xdt=xdt, ex=ex, cb=cb, yoff=yoff, dsx=dsx, dtx=dtx_g, totx=totx_g)


def _gated_norm_fwd(y_g, z_g, w_g):
    sz = _sigmoid(z_g)
    silu = z_g * sz
    yf = y_g * silu
    rs = lax.rsqrt(jnp.mean(yf * yf, axis=1, keepdims=True) + RMS_EPS)
    yn = yf * rs
    return yn * w_g, (sz, silu, rs, yn)


def _ssd_fwd(xact, proj, ymix, bias_pad, alog_pad, dxp, normw, *, name, jobs=()):
    s = xact.shape[0]
    nc = s // CHUNK

    def body(xa_ref, dt_ref, z_ref, _ymix_ref, bias_ref, alp_ref, dx_ref, nw_ref, y_ref, hp_ref, ht, cst):
        @pl.when(pl.program_id(0) == 0)
        def _():
            ht[...] = jnp.zeros_like(ht)

        hp_ref[...] = ht[...]
        q = _ssd_prep(dt_ref[...], bias_ref[...], alp_ref[...])
        cst[...] = q["cs"].T
        causal = q["tril"] > 0.0
        for g in range(N_GROUP):
            sl = slice(g * GROUP_W, (g + 1) * GROUP_W)
            xs_g = xa_ref[:, sl]
            bg = xa_ref[:, SSD_W + g * N_STATE:SSD_W + (g + 1) * N_STATE]
            cg = xa_ref[:, SSD_W + N_GROUP * N_STATE + g * N_STATE:SSD_W + N_GROUP * N_STATE + (g + 1) * N_STATE]
            ht_g = ht[:, sl]
            y, f = _ssd_group_fwd(q, g, xs_g, bg, cg, ht_g, cst, causal, dx_ref[:, sl])
            out, _ = _gated_norm_fwd(y, z_ref[:, sl], nw_ref[:, sl])
            y_ref[:, sl] = out.astype(BF16)
            ht[:, sl] = jnp.exp(f["totx"]) * ht_g + _dot_tn(bg, f["xdt"] * f["dsx"])

    par = lambda w: pl.BlockSpec((1, w), lambda c: (0, 0))
    (ycat, hprev), jouts = _hosted(
        body, jobs, grid=(nc,),
        in_specs=[pl.BlockSpec((CHUNK, XBC), lambda c: (c, 0)),
                  pl.BlockSpec((CHUNK, LANE), lambda c: (c, COL_DT // LANE)),
                  pl.BlockSpec((CHUNK, SSD_W), lambda c: (c, COL_Z // SSD_W)),
                  ANY_SPEC, par(LANE), par(LANE), par(SSD_W), par(SSD_W)],
        out_specs=(pl.BlockSpec((CHUNK, SSD_W), lambda c: (c, LRU_W // SSD_W)),
                   pl.BlockSpec((None, N_STATE, SSD_W), lambda c: (c, 0, 0))),
        out_shape=(jax.ShapeDtypeStruct(ymix.shape, ymix.dtype), jax.ShapeDtypeStruct((nc, N_STATE, SSD_W), F32)),
        scratch_shapes=[pltpu.VMEM((N_STATE, SSD_W), F32), pltpu.VMEM((CHUNK, LANE), F32)],
        aliases={3: 0}, name=name, args=(xact, proj, proj, ymix, bias_pad, alog_pad, dxp, normw))
    return ((ycat, hprev), jouts) if jobs else (ycat, hprev)


def _ssd_bwd(xact, proj, dycat, hprev, bias_pad, alog_pad, dxp, normw, *, name, jobs=()):
    s = xact.shape[0]
    nc = s // CHUNK
    l = CHUNK

    def body(xa_ref, dt_ref, z_ref, dy_ref, hp_ref, bias_ref, alp_ref, dx_ref, nw_ref,
             dxa_ref, ddt_ref, dz_ref, dnw_ref, small_ref, dht, cst, accx, dcsx_s, ddtx_s):
        step = pl.program_id(0)

        @pl.when(step == 0)
        def _():
            dht[...] = jnp.zeros_like(dht)
            accx[...] = jnp.zeros_like(accx)
            dnw_ref[...] = jnp.zeros_like(dnw_ref)
            small_ref[...] = jnp.zeros_like(small_ref)

        dtr = dt_ref[...]
        q = _ssd_prep(dtr, bias_ref[...], alp_ref[...])
        cst[...] = q["cs"].T
        causal = q["tril"] > 0.0
        eye = _iota((l, l), 0) == _iota((l, l), 1)
        lane = _iota((l, LANE), 1)
        dcs_head = jnp.zeros((l, LANE), F32)
        for g in range(N_GROUP):
            sl = slice(g * GROUP_W, (g + 1) * GROUP_W)
            slb = slice(SSD_W + g * N_STATE, SSD_W + (g + 1) * N_STATE)
            slc = slice(SSD_W + N_GROUP * N_STATE + g * N_STATE, SSD_W + N_GROUP * N_STATE + (g + 1) * N_STATE)
            xs_g, bg, cg = xa_ref[:, sl], xa_ref[:, slb], xa_ref[:, slc]
            ht_g = hp_ref[:, sl]
            dxp_g = dx_ref[:, sl]
            y, f = _ssd_group_fwd(q, g, xs_g, bg, cg, ht_g, cst, causal, dxp_g)
            z_g, nw_g = z_ref[:, sl], nw_ref[:, sl]
            _o, (sz, silu, rs, yn) = _gated_norm_fwd(y, z_g, nw_g)
            dout = dy_ref[:, sl]
            dnw_ref[:, sl] += jnp.sum(dout * yn, axis=0, keepdims=True)
            dyn = dout * nw_g
            dyf = rs * (dyn - yn * jnp.mean(dyn * yn, axis=1, keepdims=True))
            dy = dyf * silu
            dz_ref[:, sl] = (dyf * y * sz * (1.0 + z_g * (1.0 - sz))).astype(BF16)
            accx[0:1, sl] += jnp.sum(dy * xs_g, axis=0, keepdims=True)
            dyo = dy * f["ex"]
            dcg = _dot_nt(dyo, ht_g)
            dht_prev = _dot_tn(cg, dyo)
            dcsx = dy * f["yoff"]
            xdt = f["xdt"]
            dxdt = jnp.zeros((l, GROUP_W), F32)
            dcb = jnp.zeros((l, l), F32)
            for j in range(4):
                h = 4 * g + j
                lm = _decay_mat(q["cs"], cst, h, causal)
                sc = f["cb"] * lm
                mask = _head_mask(j)
                ds_ = jnp.where(causal, _dot_nt(jnp.where(mask, dy, 0.0), xdt), 0.0)
                dxdt = jnp.where(mask, _dot_tn(sc, dy), dxdt)
                dcb = dcb + ds_ * lm
                m = ds_ * sc
                rsum = jnp.sum(m, axis=1, keepdims=True)
                csum = jnp.sum(m, axis=0, keepdims=True)
                csum_col = jnp.sum(jnp.where(eye, csum, 0.0), axis=1, keepdims=True)
                dcs_head = dcs_head + jnp.where(lane == h, rsum - csum_col, 0.0)
            dhn = dht[:, sl]
            etot = jnp.exp(f["totx"])
            dxd = _dot(bg, dhn)
            dbg = _dot_nt(xdt * f["dsx"], dhn)
            dxdt = dxdt + dxd * f["dsx"]
            qq = dxd * xdt * f["dsx"]
            dcsx = dcsx - qq
            dtot = jnp.sum(qq, axis=0, keepdims=True) + jnp.sum(dhn * ht_g, axis=0, keepdims=True) * etot
            dht[:, sl] = etot * dhn + dht_prev
            dcg = dcg + _dot(dcb, bg)
            dbg = dbg + _dot_tn(dcb, cg)
            dxa_ref[:, sl] = dxdt * f["dtx"] + dy * dxp_g
            dxa_ref[:, slb] = dbg
            dxa_ref[:, slc] = dcg
            dcsx_s[:, sl] = dcsx
            ddtx_s[:, sl] = dxdt * xs_g
            accx[2:3, sl] = dtot
        reduce = (jnp.right_shift(_iota((SSD_W, LANE), 0), 6) == _iota((SSD_W, LANE), 1)).astype(F32)
        triu = (_iota((l, l), 1) >= _iota((l, l), 0)).astype(F32)
        dtot = _dot01_r(accx[...], reduce)[2:3, :]
        da_head = _dot01(triu, dcs_head + _dot01_r(dcsx_s[...], reduce)) + dtot
        ddt = _dot01_r(ddtx_s[...], reduce) + da_head * q["a_head"]
        small_ref[1:2, :] += jnp.sum(da_head * q["dt"], axis=0, keepdims=True)
        ddtr = ddt * _sigmoid(dtr + bias_ref[...])
        ddt_ref[...] = ddtr.astype(BF16)
        small_ref[0:1, :] += jnp.sum(ddtr, axis=0, keepdims=True)

        @pl.when(step == nc - 1)
        def _():
            small_ref[1:2, :] = small_ref[1:2, :] * q["a_head"]
            small_ref[2:3, :] = _dot01_r(accx[...], reduce)[0:1, :]

    rev = lambda c: nc - 1 - c
    par = lambda w: pl.BlockSpec((1, w), lambda c: (0, 0))
    outs, jouts = _hosted(
        body, jobs, grid=(nc,),
        in_specs=[pl.BlockSpec((CHUNK, XBC), lambda c: (rev(c), 0)),
                  pl.BlockSpec((CHUNK, LANE), lambda c: (rev(c), COL_DT // LANE)),
                  pl.BlockSpec((CHUNK, SSD_W), lambda c: (rev(c), COL_Z // SSD_W)),
                  pl.BlockSpec((CHUNK, SSD_W), lambda c: (rev(c), 1)),
                  pl.BlockSpec((None, N_STATE, SSD_W), lambda c: (rev(c), 0, 0)),
                  par(LANE), par(LANE), par(SSD_W), par(SSD_W)],
        out_specs=(pl.BlockSpec((CHUNK, XBC), lambda c: (rev(c), 0)),
                   pl.BlockSpec((CHUNK, LANE), lambda c: (rev(c), 0)),
                   pl.BlockSpec((CHUNK, SSD_W), lambda c: (rev(c), 0)),
                   par(SSD_W), pl.BlockSpec((SUBLANE, LANE), lambda c: (0, 0))),
        out_shape=(jax.ShapeDtypeStruct((s, XBC), F32), jax.ShapeDtypeStruct((s, LANE), BF16),
                   jax.ShapeDtypeStruct((s, SSD_W), BF16), jax.ShapeDtypeStruct((1, SSD_W), F32),
                   jax.ShapeDtypeStruct((SUBLANE, LANE), F32)),
        scratch_shapes=[pltpu.VMEM((N_STATE, SSD_W), F32), pltpu.VMEM((CHUNK, LANE), F32),
                        pltpu.VMEM((SUBLANE, SSD_W), F32), pltpu.VMEM((CHUNK, SSD_W), F32),
                        pltpu.VMEM((CHUNK, SSD_W), F32)],
        name=name, args=(xact, proj, proj, dycat, hprev, bias_pad, alog_pad, dxp, normw))
    return (tuple(outs), jouts) if jobs else tuple(outs)


def _blockdiag(w):
    w2 = w.reshape(N_HEAD // 2, 2, HEAD_P, HEAD_P)
    z = jnp.zeros((N_HEAD // 2, HEAD_P, HEAD_P), w.dtype)
    top = jnp.concatenate([w2[:, 0], z], axis=2)
    bot = jnp.concatenate([z, w2[:, 1]], axis=2)
    return jnp.concatenate([top, bot], axis=1)


def _unblockdiag(wbd):
    a = wbd[:, :HEAD_P, :HEAD_P]
    b = wbd[:, HEAD_P:, HEAD_P:]
    return jnp.stack([a, b], axis=1).reshape(N_HEAD, HEAD_P, HEAD_P)


def _pad_rows8(w):
    return jnp.concatenate([w, jnp.zeros((SUBLANE - w.shape[0], w.shape[1]), w.dtype)], axis=0)


def _pad_lane(v):
    return jnp.concatenate([v, jnp.zeros((1, LANE - v.shape[1]), v.dtype)], axis=1)


class _NoExchange:
    def ride(self, host):
        return []

    def done(self, jobs, outs, w):
        pass

    def grad(self, name, val):
        pass

    def small(self, raw):
        pass

    def pairs_now(self):
        pass


def _local_step(x, p, tgt, w, hooks=_NoExchange()):
    cw_l = _pad_rows8(w["lru_conv_w"])
    cw_s = _pad_rows8(w["ssd_conv_w"])
    wa_bd = _blockdiag(w["lru_gate_a_w"])
    wx_bd = _blockdiag(w["lru_gate_x_w"])
    ba = w["lru_gate_a_b"].reshape(1, LRU_W)
    bx = w["lru_gate_x_b"].reshape(1, LRU_W)
    bias_pad = _pad_lane(w["ssd_dt_bias"])
    alog_pad = _pad_lane(w["ssd_a_log"])
    dxp = jnp.repeat(w["ssd_d"], HEAD_P, axis=1)

    def host(fn, *a, name, **k):
        jobs = hooks.ride(name)
        res = fn(*a, name=name, jobs=jobs, **k)
        if jobs:
            res, jouts = res
            hooks.done(jobs, jouts, w)
        return res

    def grad(n, val):
        g[n] = val
        hooks.grad(n, val)

    xb = x.astype(BF16)
    proj = host(_mm, xb, w["w_in_t"], "nt", tm=1024, tn=512, name="in_proj")
    ymix, h_lru = host(_lru_fwd, proj, cw_l, w["lru_conv_b"], wa_bd, ba, wx_bd, bx, w["lru_a_param"], name="lru_fwd")
    xact = host(_conv_silu_fwd, proj, cw_s, w["ssd_conv_b"], col0=COL_XBC, width=XBC, ct=256, name="ssd_conv_fwd")
    ycat, hprev = host(_ssd_fwd, xact, proj, ymix, bias_pad, alog_pad, dxp, w["ssd_norm_w"], name="ssd_fwd")
    mix = _mm(ycat, w["w_out"], "nn", tm=1024, tn=1024, name="out_proj")
    x1, x1b = _ln_fwd(x, mix, w["ln1_g"], w["ln1_b"], name="ln1_fwd")
    pre = _mm(x1b, w["w_ff1"], "nn", tm=1024, tn=512, out_dtype=BF16, name="ff1")
    ff = _mm(pre, w["w_ff2"], "nn", tm=512, tn=1024, a_fn=_relu2, name="ff2")
    x2, x2b = _ln_fwd(x1, ff, w["ln2_g"], w["ln2_b"], name="ln2_fwd")
    gpre = _mm(x2b, w["w_ple_gate"], "nn", tm=1024, tn=1024, name="ple_gate")
    ple = _mm(p, w["w_ple"], "nn", tm=1024, tn=1024, name="ple_proj")
    loss, dgpre, dple, dt3, dg3, db3 = _head(x2, gpre, ple, w["ln3_g"], w["ln3_b"], tgt, name="head")

    g = {}
    g["ln3_g"], g["ln3_b"] = dg3, db3
    grad("w_ple_gate", _mm(x2b, dgpre, "tn", tm=512, tn=1024, out_dtype=BF16, name="d_w_ple_gate"))
    grad("w_ple", _mm(p, dple, "tn", tm=256, tn=512, dest_major=True, out_dtype=BF16, name="d_w_ple"))
    dx2_mm = host(_mm, dgpre, w["w_ple_gate"], "nt", tm=1024, tn=1024, name="d_x2")
    dt2, dt2b, g["ln2_g"], g["ln2_b"] = _ln_bwd(x1, ff, w["ln2_g"], [dt3, dx2_mm], [ALPHA, 1.0], name="ln2_bwd")
    grad("w_ff2", host(_mm, pre, dt2b, "tn", tm=512, tn=1024, a_fn=_relu2, out_dtype=BF16, name="d_w_ff2"))
    dpre = host(_mm, dt2b, w["w_ff2"], "nt", tm=1024, tn=512, extra=pre, out_dtype=BF16,
                epi=lambda acc, pv: acc * 2.0 * jnp.maximum(pv.astype(F32), 0.0), name="d_pre")
    grad("w_ff1", host(_mm, x1b, dpre, "tn", tm=1024, tn=512, dest_major=True, out_dtype=BF16, name="d_w_ff1"))
    dx1_mm = host(_mm, dpre, w["w_ff1"], "nt", tm=512, tn=1024, name="d_x1")
    dt1, dt1b, g["ln1_g"], g["ln1_b"] = _ln_bwd(x, mix, w["ln1_g"], [dt2, dx1_mm], [ALPHA, 1.0], name="ln1_bwd")
    grad("w_out", host(_mm, ycat, dt1b, "tn", tm=512, tn=1024, out_dtype=BF16, name="d_w_out"))
    dycat = host(_mm, dt1b, w["w_out"], "nt", tm=1024, tn=1024, name="d_ycat")
    dxl, dgl, dcwb_l, dwa, dwx = host(_lru_bwd, proj, dycat, h_lru, cw_l, w["lru_conv_b"], wa_bd, ba, wx_bd, bx,
                                      w["lru_a_param"], name="lru_bwd")
    g["lru_gate_a_w"] = _unblockdiag(dwa)
    g["lru_gate_x_w"] = _unblockdiag(dwx)
    raw = dict(lru=dcwb_l, gate_a=g["lru_gate_a_w"].reshape(N_HEAD * HEAD_P, HEAD_P).astype(BF16),
               gate_x=g["lru_gate_x_w"].reshape(N_HEAD * HEAD_P, HEAD_P).astype(BF16))
    hooks.small(raw)
    dxact, ddt, dz, g["ssd_norm_w"], small = host(_ssd_bwd, xact, proj, dycat, hprev, bias_pad, alog_pad, dxp,
                                                   w["ssd_norm_w"], name="ssd_bwd")
    dxbc, dcwb_s = host(_conv_silu_bwd, proj, dxact, cw_s, w["ssd_conv_b"], col0=COL_XBC, width=XBC, ct=256,
                        name="ssd_conv_bwd")
    pieces, offsets = [dxl, dgl, dz, dxbc, ddt], [0, COL_G, COL_Z, COL_XBC, COL_DT]

    g["lru_conv_w"] = dcwb_l[0:4]
    g["lru_conv_b"] = dcwb_l[4:5]
    g["lru_gate_a_b"] = dcwb_l[5:6]
    g["lru_gate_x_b"] = dcwb_l[6:7]
    g["lru_a_param"] = dcwb_l[7:8]
    g["ssd_conv_w"] = dcwb_s[0:4]
    g["ssd_conv_b"] = dcwb_s[4:5]
    g["ssd_dt_bias"] = small[0:1, :N_HEAD]
    g["ssd_a_log"] = small[1:2, :N_HEAD]
    g["ssd_d"] = small[2:3, :N_HEAD]
    rows = jnp.concatenate([g[n] for n in ("ssd_norm_w", "ln1_g", "ln1_b", "ln2_g", "ln2_b", "ln3_g", "ln3_b")]
                           + [jnp.broadcast_to(loss[:, 0:1], (1, D_MODEL))], axis=0)
    late = dict(ssd=dcwb_s, heads=small, rows=rows)
    hooks.small(late)
    raw.update(late)
    dwt = None
    for q, (pc, off) in enumerate(zip(pieces, offsets)):
        dwt = host(_mm, pc, xb, "tn", tm=512, tn=1024, out_dtype=BF16, into=(dwt, off, COL_DT + LANE),
                   name="d_w_in_%d" % q)
    grad("w_in", dwt)
    hooks.pairs_now()
    grad_x = host(_mm_pieces, pieces, offsets, w["w_in_t"], tm=256, extra=dt1, epi=lambda acc, e: acc + ALPHA * e,
                  name="d_x")
    return loss[0, 0], grad_x, g, raw


ANY_SPEC = pl.BlockSpec(memory_space=pl.ANY)


def _mesh_pos():
    return lax.axis_index("x"), lax.axis_index("y"), lax.axis_index("c")


def _remote(src, dst, send, recv, k, to):
    return pltpu.make_async_remote_copy(src_ref=src, dst_ref=dst, send_sem=send.at[k], recv_sem=recv.at[k],
                                        device_id=to, device_id_type=MESH_T)


class _Job:
    N_SEM = 7

    def __init__(self, kind, inp):
        self.kind, self.inp = kind, inp
        shape = {"gather": (N_DEV,) + inp.shape, "pair": (4,) + inp.shape[1:], "chip": inp.shape}[kind]
        self.out = jax.ShapeDtypeStruct(shape, inp.dtype)

    def _places(self):
        x, y, c = _mesh_pos()
        return (x, y, c), (x, y, 1 - c), [(1 - x, y), (x, 1 - y), (1 - x, 1 - y)]

    def start(self, inp, out, send, recv, loc):
        me, sibling, chips = self._places()
        x, y, c = me
        if self.kind == "gather":
            mine = out.at[4 * x + 2 * y + c]
            pltpu.make_async_copy(inp, mine, loc.at[0]).start()
            _remote(inp, mine, send, recv, 0, sibling).start()
            for j, chip in enumerate(chips):
                _remote(inp, mine, send, recv, 1 + j, (*chip, c)).start()
        elif self.kind == "pair":
            for k in range(4):
                _remote(inp.at[2 * k + (1 - c)], out.at[k], send, recv, k, sibling).start()
        else:
            kme = 2 * x + y
            pltpu.make_async_copy(inp.at[kme], out.at[kme], loc.at[0]).start()
            for j, (tx, ty) in enumerate(chips):
                _remote(inp.at[2 * tx + ty], out.at[kme], send, recv, j, (tx, ty, c)).start()

    def mid(self, inp, out, send, recv, loc):
        if self.kind != "gather":
            return
        me, sibling, chips = self._places()
        c = me[2]
        for j, chip in enumerate(chips):
            landed = out.at[4 * chip[0] + 2 * chip[1] + c]
            _remote(landed, landed, send, recv, 1 + j, me).wait_recv()
            _remote(landed, landed, send, recv, 4 + j, sibling).start()

    def finish(self, inp, out, send, recv, loc):
        me, sibling, chips = self._places()
        x, y, c = me
        if self.kind == "gather":
            blk = lambda px, py, pc: out.at[4 * px + 2 * py + pc]
            mine = blk(*me)
            _remote(inp, blk(*sibling), send, recv, 0, me).wait_recv()
            for j, chip in enumerate(chips):
                _remote(inp, blk(*chip, 1 - c), send, recv, 4 + j, me).wait_recv()
            for k in range(7):
                _remote(inp, mine, send, recv, k, sibling).wait_send()
            pltpu.make_async_copy(inp, mine, loc.at[0]).wait()
        elif self.kind == "pair":
            for k in range(4):
                _remote(inp.at[2 * k + (1 - c)], out.at[k], send, recv, k, sibling).wait()
        else:
            kme = 2 * x + y
            for j, (tx, ty) in enumerate(chips):
                _remote(inp.at[kme], out.at[2 * tx + ty], send, recv, j, (tx, ty, c)).wait_recv()
            for j, (tx, ty) in enumerate(chips):
                _remote(inp.at[2 * tx + ty], out.at[kme], send, recv, j, (tx, ty, c)).wait_send()
            pltpu.make_async_copy(inp.at[kme], out.at[kme], loc.at[0]).wait()


def _job_scratch(jobs):
    sem = pltpu.SemaphoreType.DMA
    return [s for _ in jobs for s in (sem((_Job.N_SEM,)), sem((_Job.N_SEM,)), sem((1,)))]


def _run_jobs(jobs, method, jins, jouts, jsems):
    for q, job in enumerate(jobs):
        getattr(job, method)(jins[q], jouts[q], *jsems[3 * q:3 * q + 3])


def _exchange(jobs, *, name):
    n = len(jobs)

    def body(*refs):
        jins, jouts, jsems = refs[:n], refs[n:2 * n], refs[2 * n:]
        _run_jobs(jobs, "start", jins, jouts, jsems)
        _run_jobs(jobs, "mid", jins, jouts, jsems)
        _run_jobs(jobs, "finish", jins, jouts, jsems)

    return _pcall(body, in_specs=[ANY_SPEC] * n, out_specs=[ANY_SPEC] * n, out_shape=[j.out for j in jobs],
                  scratch_shapes=_job_scratch(jobs), name=name)(*[j.inp for j in jobs])


def _hosted(body, jobs, *, grid, in_specs, out_specs, out_shape, args, name, scratch_shapes=(), aliases=None):
    in_specs, out_specs, out_shape = list(in_specs), list(out_specs), list(out_shape)
    scratch_shapes = list(scratch_shapes)
    n_in, n_out, n_scr, nj = len(in_specs), len(out_specs), len(scratch_shapes), len(jobs)
    sem = ("arbitrary",) * len(grid)
    kw = dict(input_output_aliases=aliases) if aliases else {}
    if not jobs:
        res = _pcall(body, grid=grid, in_specs=in_specs, out_specs=out_specs, out_shape=out_shape,
                     scratch_shapes=scratch_shapes, name=name, compiler_params=_cparams(sem), **kw)(*args)
        return list(res), []

    def full(*refs):
        ins, jins = refs[:n_in], refs[n_in:n_in + nj]
        o0 = n_in + nj
        outs, jouts = refs[o0:o0 + n_out], refs[o0 + n_out:o0 + n_out + nj]
        s0 = o0 + n_out + nj
        scr, jsems = refs[s0:s0 + n_scr], refs[s0 + n_scr:]
        step = pl.program_id(0)
        for ax in range(1, len(grid)):
            step = step * grid[ax] + pl.program_id(ax)
        total = math.prod(grid)

        @pl.when(step == 0)
        def _():
            _run_jobs(jobs, "start", jins, jouts, jsems)

        body(*ins, *outs, *scr)

        @pl.when(step == total - 1)
        def _():
            _run_jobs(jobs, "mid", jins, jouts, jsems)
            _run_jobs(jobs, "finish", jins, jouts, jsems)

    res = _pcall(full, grid=grid, in_specs=in_specs + [ANY_SPEC] * nj, out_specs=out_specs + [ANY_SPEC] * nj,
                 out_shape=out_shape + [j.out for j in jobs], scratch_shapes=scratch_shapes + _job_scratch(jobs),
                 name=name, compiler_params=_cparams(sem), **kw)(*args, *[j.inp for j in jobs])
    return list(res[:n_out]), list(res[n_out:])


def _pair_add(g8, r4, cidx, *, name):
    _, r, c = g8.shape
    tr = ROW_TILE if r % ROW_TILE == 0 else r

    def body(c_ref, g_ref, r_ref, o_ref):
        o_ref[...] = (g_ref[...].astype(F32) + r_ref[...].astype(F32)).astype(BF16)

    return _pcall(
        body,
        grid_spec=pltpu.PrefetchScalarGridSpec(
            num_scalar_prefetch=1, grid=(4, r // tr),
            in_specs=[pl.BlockSpec((None, tr, c), lambda k, i, cr: (2 * k + cr[0], i, 0)),
                      pl.BlockSpec((None, tr, c), lambda k, i, cr: (k, i, 0))],
            out_specs=pl.BlockSpec((None, tr, c), lambda k, i, cr: (k, i, 0))),
        out_shape=jax.ShapeDtypeStruct((4, r, c), BF16), name=name,
        compiler_params=_cparams(("parallel", "parallel")))(cidx, g8, r4)


def _adam_update(g, w_ref, m_ref, v_ref, g_ref, d_ref, mo_ref, vo_ref):
    c1 = 1.0 - ADAM_B1 ** ADAM_STEP
    c2 = 1.0 - ADAM_B2 ** ADAM_STEP
    m2 = ADAM_B1 * m_ref[...] + (1.0 - ADAM_B1) * g
    v2 = ADAM_B2 * v_ref[...] + (1.0 - ADAM_B2) * (g * g)
    g_ref[...] = g
    mo_ref[...] = m2
    vo_ref[...] = v2
    d_ref[...] = -ADAM_LR * ((m2 / c1) / (jnp.sqrt(v2 / c2) + ADAM_EPS) + ADAM_WD * w_ref[...])


def _adamw_rows(srcs, items, own_cols, me1, *, name):
    ns, ni, no = len(srcs), len(items), len(own_cols)
    full = lambda a: pl.BlockSpec(a.shape, lambda i, me: (0,) * a.ndim)
    in_specs = [full(a) for a in srcs]
    args = list(srcs)
    for (si, _r0, w, _m, _v) in own_cols:
        a = srcs[si]
        in_specs.append(pl.BlockSpec((N_DEV, a.shape[1], w.shape[1]), lambda i, me: (0, 0, me[0])))
        args.append(a)
    out_specs, out_shape = [], []
    for (_si, _r0, w, m, v) in list(items) + list(own_cols):
        in_specs += [full(w)] * 3
        args += [w, m, v]
        out_specs += [full(w)] * 4
        out_shape += [jax.ShapeDtypeStruct(w.shape, F32)] * 4

    def body(me_ref, *refs):
        src_refs, own_refs = refs[:ns], refs[ns:ns + no]
        wmv = refs[ns + no:ns + no + 3 * (ni + no)]
        outs = refs[ns + no + 3 * (ni + no):]
        for q, (si, r0, w, _m, _v) in enumerate(list(items) + list(own_cols)):
            nr, cw = w.shape
            gref = src_refs[si] if q < ni else own_refs[q - ni]
            g = gref[0, r0:r0 + nr, 0:cw]
            for d in range(1, N_DEV):
                g = g + gref[d, r0:r0 + nr, 0:cw]
            _adam_update(g, *wmv[3 * q:3 * q + 3], *outs[4 * q:4 * q + 4])

    res = _pcall(
        body,
        grid_spec=pltpu.PrefetchScalarGridSpec(num_scalar_prefetch=1, grid=(1,), in_specs=in_specs, out_specs=out_specs),
        out_shape=out_shape, name=name, compiler_params=_cparams(("arbitrary",)))(me1, *args)
    return [tuple(res[4 * q:4 * q + 4]) for q in range(ni + no)]


def _adamw(gsrc, w, m, v, *, name):
    k, r, c = gsrc.shape
    tr = ROW_TILE if r % ROW_TILE == 0 else r

    def body(gs_ref, w_ref, m_ref, v_ref, g_ref, d_ref, mo_ref, vo_ref):
        g = gs_ref[0].astype(F32)
        for q in range(1, k):
            g = g + gs_ref[q].astype(F32)
        _adam_update(g, w_ref, m_ref, v_ref, g_ref, d_ref, mo_ref, vo_ref)

    tc = c
    if tr == r and r > ROW_TILE and c % 256 == 0:
        tc = 256
    blk = pl.BlockSpec((tr, tc), lambda i, j: (i, j))
    sd = jax.ShapeDtypeStruct((r, c), F32)
    return _pcall(body, grid=(r // tr, c // tc),
                  in_specs=[pl.BlockSpec((k, tr, tc), lambda i, j: (0, i, j)), blk, blk, blk],
                  out_specs=(blk, blk, blk, blk), out_shape=(sd, sd, sd, sd), name=name,
                  compiler_params=_cparams(("parallel", "parallel")))(gsrc, w, m, v)


WEIGHTS = ['w_in', 'lru_conv_w', 'lru_conv_b', 'lru_gate_a_w', 'lru_gate_a_b', 'lru_gate_x_w', 'lru_gate_x_b',
           'lru_a_param', 'ssd_conv_w', 'ssd_conv_b', 'ssd_dt_bias', 'ssd_a_log', 'ssd_d', 'ssd_norm_w', 'w_out',
           'ln1_g', 'ln1_b', 'w_ff1', 'w_ff2', 'ln2_g', 'ln2_b', 'w_ple_gate', 'w_ple', 'ln3_g', 'ln3_b']
BIG = ['w_in', 'w_out', 'w_ff1', 'w_ff2', 'w_ple_gate', 'w_ple']
COL_SHARDED = ('w_ff1', 'w_ple')
CONV = ['lru_conv_w', 'ssd_conv_w']
REPL = [n for n in WEIGHTS if n not in BIG and n not in CONV]
CONV_CH = {'lru_conv_w': LRU_W, 'ssd_conv_w': XBC}


def _to_dest_major(name, gfull):
    if name == 'w_in':
        gfull = gfull[:D_IN]
    if name in COL_SHARDED:
        r, cfull = gfull.shape
        return gfull.reshape(r, N_DEV, cfull // N_DEV).transpose(1, 0, 2)
    rfull, cdim = gfull.shape
    return gfull.reshape(N_DEV, rfull // N_DEV, cdim)


def _full_weight(name, gathered):
    if name in COL_SHARDED:
        _, r, cs = gathered.shape
        full = gathered.transpose(1, 0, 2).reshape(r, N_DEV * cs)
    else:
        _, rs, cdim = gathered.shape
        full = gathered.reshape(N_DEV * rs, cdim)
    if name == 'w_in':
        full = jnp.concatenate([full, jnp.zeros((D_IN_PAD - D_IN, D_MODEL), full.dtype)], axis=0)
    return full


SMALL_SRC = ("lru", "ssd", "heads", "rows", "gate_a", "gate_x")
AG_HOSTS = {"in_proj": ("w_ff1",), "lru_fwd": ("w_ff2",), "ssd_conv_fwd": ("w_ple_gate", "w_ple"), "ssd_fwd": ("w_out",)}
PAIR_HOSTS = ("d_x2", "d_pre", "d_x1", "d_ycat")
CHIP_HOSTS = {"lru_bwd": ("w_ple_gate", "w_ple", "w_ff2"), "ssd_bwd": ("w_ff1",), "ssd_conv_bwd": ("w_out",),
              "d_x": ("w_in",)}
SMALL_HOSTS = {"ssd_bwd": ("lru", "gate_a", "gate_x"), "d_w_in_3": ("ssd", "heads", "rows")}


class _Schedule:
    def __init__(self, shards, cidx):
        self.shards, self.cidx = shards, cidx
        self.pair, self.chip, self.small_jobs = [], [], []
        self.dest, self.summed, self.gathered_small = {}, {}, {}
        self.tags = []

    def ride(self, host):
        tags = []
        if host in AG_HOSTS:
            tags = [("weight", n, self.shards[n]) for n in AG_HOSTS[host]]
        elif host in PAIR_HOSTS or host in CHIP_HOSTS or host == "flush":
            tags = [("pair", n, a) for n, a in self.pair]
            self.pair = []
            if host not in PAIR_HOSTS:
                take = [t for t in self.chip if host == "flush" or t[0] in CHIP_HOSTS[host]]
                tags += [("chip", n, a) for n, a in take]
                self.chip = [t for t in self.chip if not any(t is u for u in take)]
        if host in SMALL_HOSTS:
            tags += [("small", n, a) for n, a in self.small_jobs if n in SMALL_HOSTS[host]]
            self.small_jobs = [t for t in self.small_jobs if t[0] not in SMALL_HOSTS[host]]
        self.tags = tags
        return [_Job({"weight": "gather", "small": "gather"}.get(kind, kind), a) for kind, _n, a in tags]

    def done(self, jobs, outs, w):
        for (kind, n, _a), o in zip(self.tags, outs):
            if kind == "weight":
                w[n] = _full_weight(n, o)
            elif kind == "small":
                self.gathered_small[n] = o
            elif kind == "pair":
                self.chip.append((n, _pair_add(self.dest[n], o, self.cidx, name="rs_pair_add_" + n)))
            else:
                self.summed[n] = o

    def grad(self, name, val):
        self.dest[name] = val if val.ndim == 3 else _to_dest_major(name, val)
        self.pair.append((name, self.dest[name]))

    def small(self, raw):
        self.small_jobs += list(raw.items())

    def pairs_now(self):
        tags = [("pair", n, a) for n, a in self.pair]
        self.pair, self.tags = [], tags
        jobs = [_Job("pair", a) for _k, _n, a in tags]
        self.done(jobs, _exchange(jobs, name="rs_pairs_now"), None)

    def flush(self):
        step = 0
        while self.pair or self.chip:
            jobs = self.ride("flush")
            self.done(jobs, _exchange(jobs, name="rs_flush_%d" % step), None)
            step += 1


def kernel(x, p, w_in, lru_conv_w, lru_conv_b, lru_gate_a_w, lru_gate_a_b, lru_gate_x_w, lru_gate_x_b, lru_a_param, ssd_conv_w, ssd_conv_b, ssd_dt_bias, ssd_a_log, ssd_d, ssd_norm_w, w_out, ln1_g, ln1_b, w_ff1, w_ff2, ln2_g, ln2_b, w_ple_gate, w_ple, ln3_g, ln3_b, loss_target, m_w_in, m_lru_conv_w, m_lru_conv_b, m_lru_gate_a_w, m_lru_gate_a_b, m_lru_gate_x_w, m_lru_gate_x_b, m_lru_a_param, m_ssd_conv_w, m_ssd_conv_b, m_ssd_dt_bias, m_ssd_a_log, m_ssd_d, m_ssd_norm_w, m_w_out, m_ln1_g, m_ln1_b, m_w_ff1, m_w_ff2, m_ln2_g, m_ln2_b, m_w_ple_gate, m_w_ple, m_ln3_g, m_ln3_b, v_w_in, v_lru_conv_w, v_lru_conv_b, v_lru_gate_a_w, v_lru_gate_a_b, v_lru_gate_x_w, v_lru_gate_x_b, v_lru_a_param, v_ssd_conv_w, v_ssd_conv_b, v_ssd_dt_bias, v_ssd_a_log, v_ssd_d, v_ssd_norm_w, v_w_out, v_ln1_g, v_ln1_b, v_w_ff1, v_w_ff2, v_ln2_g, v_ln2_b, v_w_ple_gate, v_w_ple, v_ln3_g, v_ln3_b):
    given = dict(locals())
    def local(a, n):
        return jnp.swapaxes(a[0], 0, 1) if n == 'w_in' else a[0]

    wsh = {n: local(given[n], n) for n in WEIGHTS}
    msh = {n: local(given["m_" + n], n) for n in WEIGHTS}
    vsh = {n: local(given["v_" + n], n) for n in WEIGHTS}
    xi, yi, ci = _mesh_pos()
    me = 4 * xi + 2 * yi + ci

    shards = {n: wsh[n].astype(BF16) for n in BIG}
    conv_pack = jnp.concatenate([_pad_rows8(wsh[n]) for n in CONV], axis=1)
    g_in, gconv = _exchange([_Job("gather", shards['w_in']), _Job("gather", conv_pack)], name="ag_first")
    full = {'w_in_t': _full_weight('w_in', g_in)}
    c0 = 0
    for n in CONV:
        cw = CONV_CH[n] // N_DEV
        full[n] = gconv[:, :4, c0:c0 + cw].transpose(1, 0, 2).reshape(4, CONV_CH[n])
        c0 += cw
    for n in REPL:
        full[n] = given[n] if given[n].ndim == 2 else wsh[n]

    sched = _Schedule(shards, jnp.reshape(ci, (1,)).astype(jnp.int32))
    loss_local, grad_x, g, raw = _local_step(x[0], p[0, 0], loss_target[0], full, sched)
    sched.flush()
    summed, gat = sched.summed, sched.gathered_small
    loss = gat["rows"][0, 7, 0]
    for d in range(1, N_DEV):
        loss = loss + gat["rows"][d, 7, 0]

    outs = {}
    for n in BIG:
        outs[n] = _adamw(summed[n], wsh[n], msh[n], vsh[n], name="adamw_" + n)
    for n, k in (("lru_gate_a_w", "gate_a"), ("lru_gate_x_w", "gate_x")):
        flat = lambda a: a.reshape(N_HEAD * HEAD_P, HEAD_P)
        res = _adamw(gat[k], flat(wsh[n]), flat(msh[n]), flat(vsh[n]), name="adamw_" + n)
        outs[n] = tuple(r.reshape(N_HEAD, HEAD_P, HEAD_P) for r in res)
    row_items = [("lru_conv_b", 0, 4), ("lru_gate_a_b", 0, 5), ("lru_gate_x_b", 0, 6), ("lru_a_param", 0, 7),
                 ("ssd_conv_b", 1, 4), ("ssd_dt_bias", 2, 0), ("ssd_a_log", 2, 1), ("ssd_d", 2, 2),
                 ("ssd_norm_w", 3, 0), ("ln1_g", 3, 1), ("ln1_b", 3, 2), ("ln2_g", 3, 3), ("ln2_b", 3, 4),
                 ("ln3_g", 3, 5), ("ln3_b", 3, 6)]
    vec = lambda a: a.reshape(1, -1)
    items = [(si, r0, vec(given[n]), vec(given["m_" + n]), vec(given["v_" + n])) for n, si, r0 in row_items]
    own = [(si, 0, wsh[n], msh[n], vsh[n]) for n, si in (("lru_conv_w", 0), ("ssd_conv_w", 1))]
    me1 = jnp.reshape(me, (1,)).astype(jnp.int32)
    res = _adamw_rows([gat[k] for k in SMALL_SRC[:4]], items, own, me1, name="adamw_small")
    for (n, _si, _r0), r4 in zip(row_items, res[:len(row_items)]):
        outs[n] = r4
    for n, r4 in zip(CONV, res[len(row_items):]):
        outs[n] = r4

    def fin(n, k):
        a = jnp.swapaxes(outs[n][k], 0, 1) if n == 'w_in' else outs[n][k]
        return a.reshape(given[n].shape)

    return (loss, grad_x[None],
            *[fin(n, 0) for n in WEIGHTS], *[fin(n, 1) for n in WEIGHTS],
            *[fin(n, 2) for n in WEIGHTS], *[fin(n, 3) for n in WEIGHTS])
```

```python
import math

import jax
import jax.numpy as jnp
from jax import lax
from jax.experimental import pallas as pl
from jax.experimental.pallas import tpu as pltpu

F32 = jnp.float32
BF16 = jnp.bfloat16
HI = lax.Precision.HIGHEST

N_DEV = 8
D_MODEL = 1024
LRU_W = 1024
SSD_W = 1024
XBC = 2048
N_HEAD = 16
HEAD_P = 64
N_GROUP = 4
GROUP_W = 256
N_STATE = 128
CHUNK = 128
D_FF = 4096
PLE_DIM = 256
D_IN = 5136
D_IN_PAD = 5632
COL_G = 1024
COL_Z = 2048
COL_XBC = 3072
COL_DT = 5120
LRU_C = 8.0
ALPHA = 2.0 ** 0.25
LN_EPS = 1e-5
RMS_EPS = 1e-5
ADAM_LR = 0.001
ADAM_B1 = 0.9
ADAM_B2 = 0.999
ADAM_EPS = 1e-08
ADAM_WD = 0.01
ADAM_STEP = 10
GELU_C = math.sqrt(2.0 / math.pi)
LANE = 128
SUBLANE = 8
VMEM_LIMIT = 48 * 1024 * 1024
MESH_T = pl.DeviceIdType.MESH
NEG_BIG = -1e30


def _pcall(body, **kw):
    return pl.pallas_call(body, **kw)


def _cparams(sem):
    return pltpu.CompilerParams(dimension_semantics=sem, vmem_limit_bytes=VMEM_LIMIT)


def _dot(a, b):
    return jnp.dot(a.astype(BF16), b.astype(BF16), preferred_element_type=F32)


def _dot_nt(a, b):
    return lax.dot_general(a.astype(BF16), b.astype(BF16), (((1,), (1,)), ((), ())), preferred_element_type=F32)


def _dot_tn(a, b):
    return lax.dot_general(a.astype(BF16), b.astype(BF16), (((0,), (0,)), ((), ())), preferred_element_type=F32)


def _dotx(a, b):
    return jnp.dot(a, b, precision=HI, preferred_element_type=F32)


def _sigmoid(x):
    return jax.nn.sigmoid(x)


def _softplus(v):
    return jnp.maximum(v, 0.0) + jnp.log1p(jnp.exp(-jnp.abs(v)))


def _gelu(x):
    th = jnp.tanh(GELU_C * (x + 0.044715 * x * x * x))
    return 0.5 * x * (1.0 + th), th


def _gelu_grad(x, th):
    return 0.5 * (1.0 + th) + 0.5 * x * (1.0 - th * th) * GELU_C * (1.0 + 3.0 * 0.044715 * x * x)


def _iota(shape, dim):
    return lax.broadcasted_iota(jnp.int32, shape, dim)


def _mm(a, b, mode, *, tm, tn, name, a_fn=None, extra=None, epi=None, out_dtype=F32, dest_major=False, into=None,
        jobs=()):
    m = a.shape[1] if mode == "tn" else a.shape[0]
    n = b.shape[0] if mode == "nt" else b.shape[1]
    tm, tn = min(tm, m), min(tn, n)
    if dest_major:
        tn = n // N_DEV
    if mode == "nn":
        m, k = a.shape
        _, n = b.shape
        a_spec = pl.BlockSpec((tm, k), lambda i, j: (i, 0))
        b_spec = pl.BlockSpec((k, tn), lambda i, j: (0, j))
        dims = ((1,), (0,))
    elif mode == "nt":
        m, k = a.shape
        n, _ = b.shape
        a_spec = pl.BlockSpec((tm, k), lambda i, j: (i, 0))
        b_spec = pl.BlockSpec((tn, k), lambda i, j: (j, 0))
        dims = ((1,), (1,))
    else:
        k, m = a.shape
        _, n = b.shape
        a_spec = pl.BlockSpec((k, tm), lambda i, j: (0, i))
        b_spec = pl.BlockSpec((k, tn), lambda i, j: (0, j))
        dims = ((0,), (0,))
    assert m % tm == 0 and n % tn == 0, (name, m, n, tm, tn)
    o_spec = pl.BlockSpec((tm, tn), lambda i, j: (i, j))
    in_specs = [a_spec, b_spec]
    args = [a, b]
    if extra is not None:
        in_specs.append(o_spec)
        args.append(extra)

    def body(*refs):
        a_ref, b_ref, o_ref = refs[0], refs[1], refs[-1]
        av = a_ref[...]
        if a_fn is not None:
            av = a_fn(av)
        acc = lax.dot_general(av.astype(BF16), b_ref[...].astype(BF16), (dims, ((), ())), preferred_element_type=F32)
        if epi is not None:
            acc = epi(acc, refs[2][...])
        o_ref[...] = acc.astype(out_dtype)

    out_shape = jax.ShapeDtypeStruct((m, n), out_dtype)
    aliases = None
    if dest_major:
        assert extra is None
        o_spec = pl.BlockSpec((None, tm, tn), lambda i, j: (j, i, 0))
        out_shape = jax.ShapeDtypeStruct((N_DEV, m, tn), out_dtype)
    if into is not None:
        buf, row0, total = into
        assert extra is None and row0 % tm == 0
        o_spec = pl.BlockSpec((tm, tn), lambda i, j: (row0 // tm + i, j))
        out_shape = jax.ShapeDtypeStruct((total, n), out_dtype)
        if buf is not None:
            in_specs.append(ANY_SPEC)
            args.append(buf)
            aliases = {len(args) - 1: 0}
    (out,), jouts = _hosted(body, jobs, grid=(m // tm, n // tn), in_specs=in_specs, out_specs=[o_spec],
                            out_shape=[out_shape], args=args, name=name, aliases=aliases)
    return (out, jouts) if jobs else out


def _mm_pieces(pieces, offsets, b, *, tm, name, extra, epi, jobs=()):
    m = pieces[0].shape[0]
    kb, n = b.shape
    tm = min(tm, m)
    row = lambda wdt: pl.BlockSpec((tm, wdt), lambda i: (i, 0))
    in_specs = [row(pc.shape[1]) for pc in pieces] + [pl.BlockSpec((kb, n), lambda i: (0, 0)), row(n)]
    np_ = len(pieces)

    def body(*refs):
        b_ref, e_ref, o_ref = refs[np_], refs[np_ + 1], refs[np_ + 2]
        acc = jnp.zeros((tm, n), F32)
        for q in range(np_):
            kq = pieces[q].shape[1]
            acc = acc + jnp.dot(refs[q][...].astype(BF16), b_ref[offsets[q]:offsets[q] + kq, :].astype(BF16),
                                preferred_element_type=F32)
        o_ref[...] = epi(acc, e_ref[...])

    (out,), jouts = _hosted(body, jobs, grid=(m // tm,), in_specs=in_specs, out_specs=[row(n)],
                            out_shape=[jax.ShapeDtypeStruct((m, n), F32)], args=list(pieces) + [b, extra], name=name)
    return (out, jouts) if jobs else out


def _relu2(v):
    r = jnp.maximum(v, 0.0)
    return r * r


ROW_TILE = 256


def _ln_stats(t):
    mu = jnp.mean(t, axis=-1, keepdims=True)
    xc = t - mu
    var = jnp.mean(xc * xc, axis=-1, keepdims=True)
    rstd = lax.rsqrt(var + LN_EPS)
    return xc * rstd, rstd


def _ln_bwd_rows(dy, xhat, rstd, g):
    dxh = dy * g
    m1 = jnp.mean(dxh, axis=-1, keepdims=True)
    m2 = jnp.mean(dxh * xhat, axis=-1, keepdims=True)
    return rstd * (dxh - m1 - xhat * m2)


def _mm_ln(a, b, res, g, beta, *, tm, name, a_fn=None):
    m, k = a.shape
    d = b.shape[1]
    tm = min(tm, m)
    row = pl.BlockSpec((tm, d), lambda i: (i, 0))
    par = pl.BlockSpec((1, d), lambda i: (0, 0))

    def body(a_ref, b_ref, r_ref, g_ref, be_ref, br_ref, y_ref, yb_ref):
        av = a_ref[...]
        if a_fn is not None:
            av = a_fn(av)
        acc = jnp.dot(av.astype(BF16), b_ref[...].astype(BF16), preferred_element_type=F32)
        br_ref[...] = acc
        xhat, _ = _ln_stats(ALPHA * r_ref[...] + acc)
        y = xhat * g_ref[...] + be_ref[...]
        y_ref[...] = y
        yb_ref[...] = y.astype(BF16)

    sd = jax.ShapeDtypeStruct((m, d), F32)
    return _pcall(body, grid=(m // tm,),
                  in_specs=[pl.BlockSpec((tm, k), lambda i: (i, 0)), pl.BlockSpec((k, d), lambda i: (0, 0)), row, par, par],
                  out_specs=(row, row, row), out_shape=(sd, sd, jax.ShapeDtypeStruct((m, d), BF16)), name=name,
                  compiler_params=_cparams(("parallel",)))(a, b, res, g, beta)


def _mm_ln_bwd(a, b, res, branch, g, dy0, coef0, *, tm, name, jobs=()):
    m, k = a.shape
    d = b.shape[0]
    tm = min(tm, m)
    row = pl.BlockSpec((tm, d), lambda i: (i, 0))
    par = pl.BlockSpec((1, d), lambda i: (0, 0))

    def body(a_ref, b_ref, r_ref, br_ref, g_ref, dy0_ref, dt_ref, dtb_ref, dg_ref, db_ref):
        acc = lax.dot_general(a_ref[...].astype(BF16), b_ref[...].astype(BF16), (((1,), (1,)), ((), ())),
                              preferred_element_type=F32)
        dy = coef0 * dy0_ref[...] + acc
        xhat, rstd = _ln_stats(ALPHA * r_ref[...] + br_ref[...])
        dt = _ln_bwd_rows(dy, xhat, rstd, g_ref[...])
        dt_ref[...] = dt
        dtb_ref[...] = dt.astype(BF16)

        @pl.when(pl.program_id(0) == 0)
        def _():
            dg_ref[...] = jnp.zeros_like(dg_ref)
            db_ref[...] = jnp.zeros_like(db_ref)

        dg_ref[...] += jnp.sum(dy * xhat, axis=0, keepdims=True)
        db_ref[...] += jnp.sum(dy, axis=0, keepdims=True)

    pd = jax.ShapeDtypeStruct((1, d), F32)
    outs, jouts = _hosted(
        body, jobs, grid=(m // tm,),
        in_specs=[pl.BlockSpec((tm, k), lambda i: (i, 0)), pl.BlockSpec((d, k), lambda i: (0, 0)), row, row, par, row],
        out_specs=(row, row, par, par),
        out_shape=(jax.ShapeDtypeStruct((m, d), F32), jax.ShapeDtypeStruct((m, d), BF16), pd, pd),
        args=(a, b, res, branch, g, dy0), name=name)
    return (tuple(outs), jouts) if jobs else tuple(outs)


def _head(x2, gpre, ple, g, beta, tgt, *, name):
    s, d = x2.shape
    row = pl.BlockSpec((ROW_TILE, d), lambda i: (i, 0))
    par = pl.BlockSpec((1, d), lambda i: (0, 0))
    lsp = pl.BlockSpec((1, LANE), lambda i: (0, 0))

    def body(x2_ref, gp_ref, ple_ref, g_ref, be_ref, t_ref, loss_ref, dgp_ref, dple_ref, dt_ref, dg_ref, db_ref):
        gate = _sigmoid(gp_ref[...])
        ple_v = ple_ref[...]
        xhat, rstd = _ln_stats(ALPHA * x2_ref[...] + gate * ple_v)
        err = xhat * g_ref[...] + be_ref[...] - t_ref[...]
        dy = err * (1.0 / d)
        dt = _ln_bwd_rows(dy, xhat, rstd, g_ref[...])
        dt_ref[...] = dt
        dgp_ref[...] = (dt * ple_v * gate * (1.0 - gate)).astype(BF16)
        dple_ref[...] = (dt * gate).astype(BF16)

        @pl.when(pl.program_id(0) == 0)
        def _():
            loss_ref[...] = jnp.zeros_like(loss_ref)
            dg_ref[...] = jnp.zeros_like(dg_ref)
            db_ref[...] = jnp.zeros_like(db_ref)

        loss_ref[...] += 0.5 * jnp.sum(jnp.mean(err * err, axis=-1, keepdims=True))
        dg_ref[...] += jnp.sum(dy * xhat, axis=0, keepdims=True)
        db_ref[...] += jnp.sum(dy, axis=0, keepdims=True)

    sd = jax.ShapeDtypeStruct((s, d), F32)
    sb = jax.ShapeDtypeStruct((s, d), BF16)
    pd = jax.ShapeDtypeStruct((1, d), F32)
    return _pcall(body, grid=(s // ROW_TILE,), in_specs=[row, row, row, par, par, row],
                  out_specs=(lsp, row, row, row, par, par),
                  out_shape=(jax.ShapeDtypeStruct((1, LANE), F32), sb, sb, sd, pd, pd),
                  name=name, compiler_params=_cparams(("arbitrary",)))(x2, gpre, ple, g, beta, tgt)


CONV_R = 256
PAD = SUBLANE


def _shift_down(ext, s):
    if s == 0:
        return ext[PAD:, :]
    return pltpu.roll(ext, s, 0)[PAD:, :]


def _shift_up(ext, s):
    r = ext.shape[0] - PAD
    if s == 0:
        return ext[:r, :]
    return pltpu.roll(ext, r + PAD - s, 0)[:r, :]


def _conv_rows(xpad_ref, r0, w_ref):
    ext = xpad_ref[pl.ds(r0, CONV_R + PAD), :]
    acc = _shift_down(ext, 0) * w_ref[3:4, :]
    for k in range(3):
        acc = acc + _shift_down(ext, 3 - k) * w_ref[k:k + 1, :]
    return acc, ext


def _fill_front_padded(dst_ref, src_ref, s):
    dst_ref[0:PAD, :] = jnp.zeros((PAD, dst_ref.shape[1]), F32)

    def cp(q, _):
        r0 = pl.multiple_of(q * CONV_R, CONV_R)
        dst_ref[pl.ds(pl.multiple_of(PAD + r0, PAD), CONV_R), :] = src_ref[pl.ds(r0, CONV_R), :]
        return 0

    lax.fori_loop(0, s // CONV_R, cp, 0)


def _conv_silu_fwd(proj, w8, b, *, col0, width, ct, name, jobs=()):
    s = proj.shape[0]
    nb = col0 // ct

    def body(x_ref, w_ref, b_ref, o_ref, xpad):
        _fill_front_padded(xpad, x_ref, s)

        def step(q, _):
            r0 = pl.multiple_of(q * CONV_R, CONV_R)
            acc, _e = _conv_rows(xpad, r0, w_ref)
            pre = acc + b_ref[...]
            o_ref[pl.ds(r0, CONV_R), :] = pre * _sigmoid(pre)
            return 0

        lax.fori_loop(0, s // CONV_R, step, 0)

    (out,), jouts = _hosted(
        body, jobs, grid=(width // ct,),
        in_specs=[pl.BlockSpec((s, ct), lambda j: (0, nb + j)), pl.BlockSpec((SUBLANE, ct), lambda j: (0, j)),
                  pl.BlockSpec((1, ct), lambda j: (0, j))],
        out_specs=[pl.BlockSpec((s, ct), lambda j: (0, j))],
        out_shape=[jax.ShapeDtypeStruct((s, width), F32)],
        scratch_shapes=[pltpu.VMEM((s + PAD, ct), F32)], name=name, args=(proj, w8, b))
    return (out, jouts) if jobs else out


def _conv_bwd_rows(dpad_ref, r0, w_ref):
    return _conv_bwd_ext(dpad_ref[pl.ds(r0, CONV_R + PAD), :], w_ref)


def _conv_bwd_ext(ext, w_ref):
    acc = _shift_up(ext, 0) * w_ref[3:4, :]
    for k in range(3):
        acc = acc + _shift_up(ext, 3 - k) * w_ref[k:k + 1, :]
    return acc


def _conv_silu_bwd(proj, dact, w8, b, *, col0, width, ct, name, jobs=()):
    s = proj.shape[0]
    nb = col0 // ct

    def body(x_ref, d_ref, w_ref, b_ref, dx_ref, dwb_ref, xpad, dpad):
        _fill_front_padded(xpad, x_ref, s)
        dpad[pl.ds(s, PAD), :] = jnp.zeros((PAD, ct), F32)
        dwb_ref[...] = jnp.zeros_like(dwb_ref)

        def step(q, _):
            r0 = pl.multiple_of(q * CONV_R, CONV_R)
            acc, ext = _conv_rows(xpad, r0, w_ref)
            pre = acc + b_ref[...]
            sg = _sigmoid(pre)
            dpre = d_ref[pl.ds(r0, CONV_R), :] * sg * (1.0 + pre * (1.0 - sg))
            dpad[pl.ds(r0, CONV_R), :] = dpre
            for k in range(4):
                dwb_ref[k:k + 1, :] += jnp.sum(dpre * _shift_down(ext, 3 - k), axis=0, keepdims=True)
            dwb_ref[4:5, :] += jnp.sum(dpre, axis=0, keepdims=True)
            return 0

        lax.fori_loop(0, s // CONV_R, step, 0)

        def step2(q, _):
            r0 = pl.multiple_of(q * CONV_R, CONV_R)
            dx_ref[pl.ds(r0, CONV_R), :] = _conv_bwd_rows(dpad, r0, w_ref).astype(BF16)
            return 0

        lax.fori_loop(0, s // CONV_R, step2, 0)

    colb = pl.BlockSpec((s, ct), lambda j: (0, j))
    outs, jouts = _hosted(
        body, jobs, grid=(width // ct,),
        in_specs=[pl.BlockSpec((s, ct), lambda j: (0, nb + j)), colb, pl.BlockSpec((SUBLANE, ct), lambda j: (0, j)),
                  pl.BlockSpec((1, ct), lambda j: (0, j))],
        out_specs=(colb, pl.BlockSpec((SUBLANE, ct), lambda j: (0, j))),
        out_shape=(jax.ShapeDtypeStruct((s, width), BF16), jax.ShapeDtypeStruct((SUBLANE, width), F32)),
        scratch_shapes=[pltpu.VMEM((s + PAD, ct), F32), pltpu.VMEM((s + PAD, ct), F32)], name=name,
        args=(proj, dact, w8, b))
    return (tuple(outs), jouts) if jobs else tuple(outs)


LRU_CT = 128


def _row_of(v, r):
    return jnp.sum(jnp.where(_iota((v.shape[0], 1), 0) == r, v, 0.0), axis=0, keepdims=True)


def _scan_fwd(a, u):
    r = a.shape[0]
    row = _iota((r, 1), 0)
    d = 1
    while d < r:
        valid = row >= d
        u = jnp.where(valid, a * pltpu.roll(u, d, 0) + u, u)
        a = jnp.where(valid, a * pltpu.roll(a, d, 0), a)
        d *= 2
    return a, u


def _scan_rev(b, u):
    r = b.shape[0]
    row = _iota((r, 1), 0)
    d = 1
    while d < r:
        valid = row < r - d
        u = jnp.where(valid, b * pltpu.roll(u, r - d, 0) + u, u)
        b = jnp.where(valid, b * pltpu.roll(b, r - d, 0), b)
        d *= 2
    return b, u


def _lru_chunk(xpad, r0, cw_ref, cb, wa, ba, wx, bx, sp):
    acc, ext = _conv_rows(xpad, r0, cw_ref)
    xl = acc + cb
    r = _sigmoid(_dot(xl, wa) + ba)
    i = _sigmoid(_dot(xl, wx) + bx)
    la = -LRU_C * r * sp
    a = jnp.exp(la)
    a2 = jnp.exp(2.0 * la)
    mult = jnp.sqrt(-jnp.tanh(la) * (a2 + 1.0))
    first = (r0 + _iota((CONV_R, 1), 0)) == 0
    mult = jnp.where(first, 1.0, mult)
    return ext, xl, r, i, a, a2, mult, first


def _lru_specs(s):
    ct = LRU_CT
    nb_g = COL_G // ct
    return dict(
        x=pl.BlockSpec((s, ct), lambda j: (0, j)),
        g=pl.BlockSpec((s, ct), lambda j: (0, nb_g + j)),
        col=pl.BlockSpec((s, ct), lambda j: (0, j)),
        cw=pl.BlockSpec((SUBLANE, ct), lambda j: (0, j)),
        vec=pl.BlockSpec((1, ct), lambda j: (0, j)),
        gate=pl.BlockSpec((None, ct, ct), lambda j: (j, 0, 0)),
    )


def _lru_fwd(proj, cw8, cb, wa_bd, ba, wx_bd, bx, ap, *, name, jobs=()):
    s = proj.shape[0]
    ct = LRU_CT
    sp_ = _lru_specs(s)

    def body(x_ref, g_ref, cw_ref, cb_ref, wa_ref, ba_ref, wx_ref, bx_ref, ap_ref, y_ref, h_ref, xpad):
        _fill_front_padded(xpad, x_ref, s)
        sp = _softplus(-ap_ref[...])

        def step(q, carry):
            r0 = pl.multiple_of(q * CONV_R, CONV_R)
            _e, xl, _r, i, a, _a2, mult, _f = _lru_chunk(xpad, r0, cw_ref, cb_ref[...], wa_ref[...], ba_ref[...],
                                                       wx_ref[...], bx_ref[...], sp)
            acum, ucum = _scan_fwd(a, xl * i * mult)
            h = acum * carry + ucum
            h_ref[pl.ds(r0, CONV_R), :] = h
            ge, _th = _gelu(g_ref[pl.ds(r0, CONV_R), :])
            y_ref[pl.ds(r0, CONV_R), :] = (ge * h).astype(BF16)
            return _row_of(h, CONV_R - 1)

        lax.fori_loop(0, s // CONV_R, step, jnp.zeros((1, ct), F32))

    (ymix, hs), jouts = _hosted(
        body, jobs, grid=(LRU_W // ct,),
        in_specs=[sp_["x"], sp_["g"], sp_["cw"], sp_["vec"], sp_["gate"], sp_["vec"], sp_["gate"], sp_["vec"], sp_["vec"]],
        out_specs=(sp_["col"], sp_["col"]),
        out_shape=(jax.ShapeDtypeStruct((s, LRU_W + SSD_W), BF16), jax.ShapeDtypeStruct((s, LRU_W), F32)),
        scratch_shapes=[pltpu.VMEM((s + PAD, ct), F32)],
        name=name, args=(proj, proj, cw8, cb, wa_bd, ba, wx_bd, bx, ap))
    return ((ymix, hs), jouts) if jobs else (ymix, hs)


def _lru_bwd(proj, dy, hs, cw8, cb, wa_bd, ba, wx_bd, bx, ap, *, name, jobs=()):
    s = proj.shape[0]
    ct = LRU_CT
    sp_ = _lru_specs(s)

    nq = s // CONV_R

    def body(x_ref, g_ref, dy_ref, h_ref, cw_ref, cb_ref, wa_ref, ba_ref, wx_ref, bx_ref, ap_ref,
             dx_ref, dg_ref, dcwb_ref, dwa_ref, dwx_ref, xpad, hpad):
        _fill_front_padded(xpad, x_ref, s)
        _fill_front_padded(hpad, h_ref, s)
        apv = ap_ref[...]
        sp = _softplus(-apv)
        cb_v, wa, ba_v, wx, bx_v = cb_ref[...], wa_ref[...], ba_ref[...], wx_ref[...], bx_ref[...]
        dcwb_ref[...] = jnp.zeros_like(dcwb_ref)
        dwa_ref[...] = jnp.zeros_like(dwa_ref)
        dwx_ref[...] = jnp.zeros_like(dwx_ref)

        def back(k, carry):
            g_next, a_next, dxl_next = carry
            last_row = _iota((CONV_R, 1), 0) == CONV_R - 1
            r0 = pl.multiple_of((nq - 1 - k) * CONV_R, CONV_R)
            ext, xl, r, i, a, a2, mult, first = _lru_chunk(xpad, r0, cw_ref, cb_v, wa, ba_v, wx, bx_v, sp)
            gv = g_ref[pl.ds(r0, CONV_R), :]
            dyv = dy_ref[pl.ds(r0, CONV_R), :]
            hext = hpad[pl.ds(r0, CONV_R + PAD), :]
            ge, th = _gelu(gv)
            dg_ref[pl.ds(r0, CONV_R), :] = (dyv * _shift_down(hext, 0) * _gelu_grad(gv, th)).astype(BF16)
            b = jnp.where(last_row, a_next, pltpu.roll(a, CONV_R - 1, 0))
            bcum, dcum = _scan_rev(b, dyv * ge)
            gval = dcum + bcum * g_next
            hprev = _shift_down(hext, 1)
            da = gval * hprev
            dxl = gval * i * mult
            di = gval * xl * mult
            dmult = jnp.where(first, 0.0, gval * xl * i)
            dla = da * a - dmult * a2 / mult
            dr = dla * (-LRU_C) * sp
            dcwb_ref[7:8, :] += jnp.sum(dla * (-LRU_C) * r, axis=0, keepdims=True)
            dpr = dr * r * (1.0 - r)
            dpi = di * i * (1.0 - i)
            dxl = dxl + _dot_nt(dpr, wa) + _dot_nt(dpi, wx)
            dwa_ref[...] += _dot_tn(xl, dpr)
            dwx_ref[...] += _dot_tn(xl, dpi)
            dcwb_ref[5:6, :] += jnp.sum(dpr, axis=0, keepdims=True)
            dcwb_ref[6:7, :] += jnp.sum(dpi, axis=0, keepdims=True)
            for tap in range(4):
                dcwb_ref[tap:tap + 1, :] += jnp.sum(dxl * _shift_down(ext, 3 - tap), axis=0, keepdims=True)
            dcwb_ref[4:5, :] += jnp.sum(dxl, axis=0, keepdims=True)
            dx_ref[pl.ds(r0, CONV_R), :] = _conv_bwd_ext(jnp.concatenate([dxl, dxl_next], axis=0), cw_ref).astype(BF16)
            return _row_of(gval, 0), _row_of(a, 0), dxl[:PAD, :]

        zero = jnp.zeros((1, ct), F32)
        lax.fori_loop(0, nq, back, (zero, zero, jnp.zeros((PAD, ct), F32)))
        dcwb_ref[7:8, :] = dcwb_ref[7:8, :] * (-_sigmoid(-apv))

    nt = LRU_W // ct
    outs, jouts = _hosted(
        body, jobs, grid=(nt,),
        in_specs=[sp_["x"], sp_["g"], sp_["col"], sp_["col"], sp_["cw"], sp_["vec"], sp_["gate"], sp_["vec"], sp_["gate"],
                  sp_["vec"], sp_["vec"]],
        out_specs=(sp_["col"], sp_["col"], sp_["cw"], sp_["gate"], sp_["gate"]),
        out_shape=(jax.ShapeDtypeStruct((s, LRU_W), BF16), jax.ShapeDtypeStruct((s, LRU_W), BF16),
                   jax.ShapeDtypeStruct((SUBLANE, LRU_W), F32), jax.ShapeDtypeStruct((nt, ct, ct), F32),
                   jax.ShapeDtypeStruct((nt, ct, ct), F32)),
        scratch_shapes=[pltpu.VMEM((s + PAD, ct), F32), pltpu.VMEM((s + PAD, ct), F32)],
        name=name, args=(proj, proj, dy, hs, cw8, cb, wa_bd, ba, wx_bd, bx, ap))
    return (tuple(outs), jouts) if jobs else tuple(outs)


def _split3(v):
    hi = v.astype(BF16)
    r1 = v - hi.astype(F32)
    mid = r1.astype(BF16)
    lo = (r1 - mid.astype(F32)).astype(BF16)
    return hi, mid, lo


def _dot01(m01, v):
    mb = m01.astype(BF16)
    hi, mid, lo = _split3(v)
    f = lambda part: jnp.dot(mb, part, preferred_element_type=F32)
    return f(hi) + f(mid) + f(lo)


def _dot01_r(v, m01):
    mb = m01.astype(BF16)
    hi, mid, lo = _split3(v)
    f = lambda part: jnp.dot(part, mb, preferred_element_type=F32)
    return f(hi) + f(mid) + f(lo)


def _ssd_prep(dtr, bias, alog_pad):
    l = CHUNK
    lane = _iota((1, LANE), 1)
    a_head = jnp.where(lane < N_HEAD, -jnp.exp(alog_pad), 0.0)
    dt = _softplus(dtr + bias)
    tril = (_iota((l, l), 1) <= _iota((l, l), 0)).astype(F32)
    a = dt * a_head
    cs = _dot01(tril, a)
    tot = jnp.sum(a, axis=0, keepdims=True)
    return dict(a_head=a_head, dt=dt, tril=tril, cs=cs, tot=tot)


def _col(v, h):
    lane = _iota(v.shape, 1)
    return jnp.sum(jnp.where(lane == h, v, 0.0), axis=1, keepdims=True)


def _decay_mat(cs, cst_ref, h, causal):
    row = cst_ref[h:h + 1, :]
    return jnp.exp(jnp.where(causal, _col(cs, h) - row, NEG_BIG))


def _head_mask(j, rows=CHUNK):
    lane = _iota((rows, GROUP_W), 1)
    return (lane >= j * HEAD_P) & (lane < (j + 1) * HEAD_P)


def _over_heads(v, g):
    r = v.shape[0]
    out = jnp.zeros((r, GROUP_W), F32)
    for j in range(4):
        out = jnp.where(_head_mask(j, r), _col(v, 4 * g + j), out)
    return out


def _ssd_group_fwd(q, g, xs_g, bg, cg, ht_g, cst_ref, causal, dx_g):
    dtx_g, csx_g, totx_g = _over_heads(q["dt"], g), _over_heads(q["cs"], g), _over_heads(q["tot"], g)
    xdt = xs_g * dtx_g
    ex = jnp.exp(csx_g)
    cb = _dot_nt(cg, bg)
    yoff = _dot(cg, ht_g) * ex
    ydiag = jnp.zeros((CHUNK, GROUP_W), F32)
    for j in range(4):
        sc = cb * _decay_mat(q["cs"], cst_ref, 4 * g + j, causal)
        ydiag = jnp.where(_head_mask(j), _dot(sc, xdt), ydiag)
    y = ydiag + yoff + xs_g * dx_g
    dsx = jnp.exp(totx_g - csx_g)
    return y, dict(xdt=xdt, ex=ex, cb=cb, yoff=yoff, dsx=dsx, dtx=dtx_g, totx=totx_g)


def _gated_norm_fwd(y_g, z_g, w_g):
    sz = _sigmoid(z_g)
    silu = z_g * sz
    yf = y_g * silu
    rs = lax.rsqrt(jnp.mean(yf * yf, axis=1, keepdims=True) + RMS_EPS)
    yn = yf * rs
    return yn * w_g, (sz, silu, rs, yn)


def _ssd_fwd(xact, proj, ymix, bias_pad, alog_pad, dxp, normw, *, name, jobs=()):
    s = xact.shape[0]
    nc = s // CHUNK

    def body(xa_ref, dt_ref, z_ref, _ymix_ref, bias_ref, alp_ref, dx_ref, nw_ref, y_ref, hp_ref, ht, cst):
        @pl.when(pl.program_id(0) == 0)
        def _():
            ht[...] = jnp.zeros_like(ht)

        hp_ref[...] = ht[...]
        q = _ssd_prep(dt_ref[...], bias_ref[...], alp_ref[...])
        cst[...] = q["cs"].T
        causal = q["tril"] > 0.0
        for g in range(N_GROUP):
            sl = slice(g * GROUP_W, (g + 1) * GROUP_W)
            xs_g = xa_ref[:, sl]
            bg = xa_ref[:, SSD_W + g * N_STATE:SSD_W + (g + 1) * N_STATE]
            cg = xa_ref[:, SSD_W + N_GROUP * N_STATE + g * N_STATE:SSD_W + N_GROUP * N_STATE + (g + 1) * N_STATE]
            ht_g = ht[:, sl]
            y, f = _ssd_group_fwd(q, g, xs_g, bg, cg, ht_g, cst, causal, dx_ref[:, sl])
            out, _ = _gated_norm_fwd(y, z_ref[:, sl], nw_ref[:, sl])
            y_ref[:, sl] = out.astype(BF16)
            ht[:, sl] = jnp.exp(f["totx"]) * ht_g + _dot_tn(bg, f["xdt"] * f["dsx"])

    par = lambda w: pl.BlockSpec((1, w), lambda c: (0, 0))
    (ycat, hprev), jouts = _hosted(
        body, jobs, grid=(nc,),
        in_specs=[pl.BlockSpec((CHUNK, XBC), lambda c: (c, 0)),
                  pl.BlockSpec((CHUNK, LANE), lambda c: (c, COL_DT // LANE)),
                  pl.BlockSpec((CHUNK, SSD_W), lambda c: (c, COL_Z // SSD_W)),
                  ANY_SPEC, par(LANE), par(LANE), par(SSD_W), par(SSD_W)],
        out_specs=(pl.BlockSpec((CHUNK, SSD_W), lambda c: (c, LRU_W // SSD_W)),
                   pl.BlockSpec((None, N_STATE, SSD_W), lambda c: (c, 0, 0))),
        out_shape=(jax.ShapeDtypeStruct(ymix.shape, ymix.dtype), jax.ShapeDtypeStruct((nc, N_STATE, SSD_W), F32)),
        scratch_shapes=[pltpu.VMEM((N_STATE, SSD_W), F32), pltpu.VMEM((CHUNK, LANE), F32)],
        aliases={3: 0}, name=name, args=(xact, proj, proj, ymix, bias_pad, alog_pad, dxp, normw))
    return ((ycat, hprev), jouts) if jobs else (ycat, hprev)


def _ssd_bwd(xact, proj, dycat, hprev, bias_pad, alog_pad, dxp, normw, *, name, jobs=()):
    s = xact.shape[0]
    nc = s // CHUNK
    l = CHUNK

    def body(xa_ref, dt_ref, z_ref, dy_ref, hp_ref, bias_ref, alp_ref, dx_ref, nw_ref,
             dxa_ref, ddt_ref, dz_ref, dnw_ref, small_ref, dht, cst, accx, dcsx_s, ddtx_s):
        step = pl.program_id(0)

        @pl.when(step == 0)
        def _():
            dht[...] = jnp.zeros_like(dht)
            accx[...] = jnp.zeros_like(accx)
            dnw_ref[...] = jnp.zeros_like(dnw_ref)
            small_ref[...] = jnp.zeros_like(small_ref)

        dtr = dt_ref[...]
        q = _ssd_prep(dtr, bias_ref[...], alp_ref[...])
        cst[...] = q["cs"].T
        causal = q["tril"] > 0.0
        eye = _iota((l, l), 0) == _iota((l, l), 1)
        lane = _iota((l, LANE), 1)
        dcs_head = jnp.zeros((l, LANE), F32)
        for g in range(N_GROUP):
            sl = slice(g * GROUP_W, (g + 1) * GROUP_W)
            slb = slice(SSD_W + g * N_STATE, SSD_W + (g + 1) * N_STATE)
            slc = slice(SSD_W + N_GROUP * N_STATE + g * N_STATE, SSD_W + N_GROUP * N_STATE + (g + 1) * N_STATE)
            xs_g, bg, cg = xa_ref[:, sl], xa_ref[:, slb], xa_ref[:, slc]
            ht_g = hp_ref[:, sl]
            dxp_g = dx_ref[:, sl]
            y, f = _ssd_group_fwd(q, g, xs_g, bg, cg, ht_g, cst, causal, dxp_g)
            z_g, nw_g = z_ref[:, sl], nw_ref[:, sl]
            _o, (sz, silu, rs, yn) = _gated_norm_fwd(y, z_g, nw_g)
            dout = dy_ref[:, sl]
            dnw_ref[:, sl] += jnp.sum(dout * yn, axis=0, keepdims=True)
            dyn = dout * nw_g
            dyf = rs * (dyn - yn * jnp.mean(dyn * yn, axis=1, keepdims=True))
            dy = dyf * silu
            dz_ref[:, sl] = (dyf * y * sz * (1.0 + z_g * (1.0 - sz))).astype(BF16)
            accx[0:1, sl] += jnp.sum(dy * xs_g, axis=0, keepdims=True)
            dyo = dy * f["ex"]
            dcg = _dot_nt(dyo, ht_g)
            dht_prev = _dot_tn(cg, dyo)
            dcsx = dy * f["yoff"]
            xdt = f["xdt"]
            dxdt = jnp.zeros((l, GROUP_W), F32)
            dcb = jnp.zeros((l, l), F32)
            for j in range(4):
                h = 4 * g + j
                lm = _decay_mat(q["cs"], cst, h, causal)
                sc = f["cb"] * lm
                mask = _head_mask(j)
                ds_ = jnp.where(causal, _dot_nt(jnp.where(mask, dy, 0.0), xdt), 0.0)
                dxdt = jnp.where(mask, _dot_tn(sc, dy), dxdt)
                dcb = dcb + ds_ * lm
                m = ds_ * sc
                rsum = jnp.sum(m, axis=1, keepdims=True)
                csum = jnp.sum(m, axis=0, keepdims=True)
                csum_col = jnp.sum(jnp.where(eye, csum, 0.0), axis=1, keepdims=True)
                dcs_head = dcs_head + jnp.where(lane == h, rsum - csum_col, 0.0)
            dhn = dht[:, sl]
            etot = jnp.exp(f["totx"])
            dxd = _dot(bg, dhn)
            dbg = _dot_nt(xdt * f["dsx"], dhn)
            dxdt = dxdt + dxd * f["dsx"]
            qq = dxd * xdt * f["dsx"]
            dcsx = dcsx - qq
            dtot = jnp.sum(qq, axis=0, keepdims=True) + jnp.sum(dhn * ht_g, axis=0, keepdims=True) * etot
            dht[:, sl] = etot * dhn + dht_prev
            dcg = dcg + _dot(dcb, bg)
            dbg = dbg + _dot_tn(dcb, cg)
            dxa_ref[:, sl] = dxdt * f["dtx"] + dy * dxp_g
            dxa_ref[:, slb] = dbg
            dxa_ref[:, slc] = dcg
            dcsx_s[:, sl] = dcsx
            ddtx_s[:, sl] = dxdt * xs_g
            accx[2:3, sl] = dtot
        reduce = (jnp.right_shift(_iota((SSD_W, LANE), 0), 6) == _iota((SSD_W, LANE), 1)).astype(F32)
        triu = (_iota((l, l), 1) >= _iota((l, l), 0)).astype(F32)
        dtot = _dot01_r(accx[...], reduce)[2:3, :]
        da_head = _dot01(triu, dcs_head + _dot01_r(dcsx_s[...], reduce)) + dtot
        ddt = _dot01_r(ddtx_s[...], reduce) + da_head * q["a_head"]
        small_ref[1:2, :] += jnp.sum(da_head * q["dt"], axis=0, keepdims=True)
        ddtr = ddt * _sigmoid(dtr + bias_ref[...])
        ddt_ref[...] = ddtr.astype(BF16)
        small_ref[0:1, :] += jnp.sum(ddtr, axis=0, keepdims=True)

        @pl.when(step == nc - 1)
        def _():
            small_ref[1:2, :] = small_ref[1:2, :] * q["a_head"]
            small_ref[2:3, :] = _dot01_r(accx[...], reduce)[0:1, :]

    rev = lambda c: nc - 1 - c
    par = lambda w: pl.BlockSpec((1, w), lambda c: (0, 0))
    outs, jouts = _hosted(
        body, jobs, grid=(nc,),
        in_specs=[pl.BlockSpec((CHUNK, XBC), lambda c: (rev(c), 0)),
                  pl.BlockSpec((CHUNK, LANE), lambda c: (rev(c), COL_DT // LANE)),
                  pl.BlockSpec((CHUNK, SSD_W), lambda c: (rev(c), COL_Z // SSD_W)),
                  pl.BlockSpec((CHUNK, SSD_W), lambda c: (rev(c), 1)),
                  pl.BlockSpec((None, N_STATE, SSD_W), lambda c: (rev(c), 0, 0)),
                  par(LANE), par(LANE), par(SSD_W), par(SSD_W)],
        out_specs=(pl.BlockSpec((CHUNK, XBC), lambda c: (rev(c), 0)),
                   pl.BlockSpec((CHUNK, LANE), lambda c: (rev(c), 0)),
                   pl.BlockSpec((CHUNK, SSD_W), lambda c: (rev(c), 0)),
                   par(SSD_W), pl.BlockSpec((SUBLANE, LANE), lambda c: (0, 0))),
        out_shape=(jax.ShapeDtypeStruct((s, XBC), F32), jax.ShapeDtypeStruct((s, LANE), BF16),
                   jax.ShapeDtypeStruct((s, SSD_W), BF16), jax.ShapeDtypeStruct((1, SSD_W), F32),
                   jax.ShapeDtypeStruct((SUBLANE, LANE), F32)),
        scratch_shapes=[pltpu.VMEM((N_STATE, SSD_W), F32), pltpu.VMEM((CHUNK, LANE), F32),
                        pltpu.VMEM((SUBLANE, SSD_W), F32), pltpu.VMEM((CHUNK, SSD_W), F32),
                        pltpu.VMEM((CHUNK, SSD_W), F32)],
        name=name, args=(xact, proj, proj, dycat, hprev, bias_pad, alog_pad, dxp, normw))
    return (tuple(outs), jouts) if jobs else tuple(outs)


def _blockdiag(w):
    w2 = w.reshape(N_HEAD // 2, 2, HEAD_P, HEAD_P)
    z = jnp.zeros((N_HEAD // 2, HEAD_P, HEAD_P), w.dtype)
    top = jnp.concatenate([w2[:, 0], z], axis=2)
    bot = jnp.concatenate([z, w2[:, 1]], axis=2)
    return jnp.concatenate([top, bot], axis=1)


def _unblockdiag(wbd):
    a = wbd[:, :HEAD_P, :HEAD_P]
    b = wbd[:, HEAD_P:, HEAD_P:]
    return jnp.stack([a, b], axis=1).reshape(N_HEAD, HEAD_P, HEAD_P)


def _pad_rows8(w):
    return jnp.concatenate([w, jnp.zeros((SUBLANE - w.shape[0], w.shape[1]), w.dtype)], axis=0)


def _pad_lane(v):
    return jnp.concatenate([v, jnp.zeros((1, LANE - v.shape[1]), v.dtype)], axis=1)


class _NoExchange:
    def ride(self, host):
        return []

    def done(self, jobs, outs, w):
        pass

    def grad(self, name, val):
        pass

    def small(self, raw):
        pass

    def pairs_now(self):
        pass


def _local_step(x, p, tgt, w, hooks=_NoExchange()):
    cw_l = _pad_rows8(w["lru_conv_w"])
    cw_s = _pad_rows8(w["ssd_conv_w"])
    wa_bd = _blockdiag(w["lru_gate_a_w"])
    wx_bd = _blockdiag(w["lru_gate_x_w"])
    ba = w["lru_gate_a_b"].reshape(1, LRU_W)
    bx = w["lru_gate_x_b"].reshape(1, LRU_W)
    bias_pad = _pad_lane(w["ssd_dt_bias"])
    alog_pad = _pad_lane(w["ssd_a_log"])
    dxp = jnp.repeat(w["ssd_d"], HEAD_P, axis=1)

    def host(fn, *a, name, **k):
        jobs = hooks.ride(name)
        res = fn(*a, name=name, jobs=jobs, **k)
        if jobs:
            res, jouts = res
            hooks.done(jobs, jouts, w)
        return res

    def grad(n, val):
        g[n] = val
        hooks.grad(n, val)

    xb = x.astype(BF16)
    proj = host(_mm, xb, w["w_in_t"], "nt", tm=1024, tn=512, name="in_proj")
    ymix, h_lru = host(_lru_fwd, proj, cw_l, w["lru_conv_b"], wa_bd, ba, wx_bd, bx, w["lru_a_param"], name="lru_fwd")
    xact = host(_conv_silu_fwd, proj, cw_s, w["ssd_conv_b"], col0=COL_XBC, width=XBC, ct=256, name="ssd_conv_fwd")
    ycat, hprev = host(_ssd_fwd, xact, proj, ymix, bias_pad, alog_pad, dxp, w["ssd_norm_w"], name="ssd_fwd")
    mix, x1, x1b = _mm_ln(ycat, w["w_out"], x, w["ln1_g"], w["ln1_b"], tm=512, name="out_proj")
    pre = _mm(x1b, w["w_ff1"], "nn", tm=1024, tn=512, out_dtype=BF16, name="ff1")
    ff, x2, x2b = _mm_ln(pre, w["w_ff2"], x1, w["ln2_g"], w["ln2_b"], tm=512, a_fn=_relu2, name="ff2")
    gpre = _mm(x2b, w["w_ple_gate"], "nn", tm=1024, tn=1024, name="ple_gate")
    ple = _mm(p, w["w_ple"], "nn", tm=1024, tn=1024, name="ple_proj")
    loss, dgpre, dple, dt3, dg3, db3 = _head(x2, gpre, ple, w["ln3_g"], w["ln3_b"], tgt, name="head")

    g = {}
    g["ln3_g"], g["ln3_b"] = dg3, db3
    grad("w_ple_gate", _mm(x2b, dgpre, "tn", tm=512, tn=1024, out_dtype=BF16, name="d_w_ple_gate"))
    grad("w_ple", _mm(p, dple, "tn", tm=256, tn=512, dest_major=True, out_dtype=BF16, name="d_w_ple"))
    dt2, dt2b, g["ln2_g"], g["ln2_b"] = host(_mm_ln_bwd, dgpre, w["w_ple_gate"], x1, ff, w["ln2_g"], dt3, ALPHA,
                                             tm=512, name="d_x2")
    grad("w_ff2", host(_mm, pre, dt2b, "tn", tm=512, tn=1024, a_fn=_relu2, out_dtype=BF16, name="d_w_ff2"))
    dpre = host(_mm, dt2b, w["w_ff2"], "nt", tm=1024, tn=512, extra=pre, out_dtype=BF16,
                epi=lambda acc, pv: acc * 2.0 * jnp.maximum(pv.astype(F32), 0.0), name="d_pre")
    grad("w_ff1", host(_mm, x1b, dpre, "tn", tm=1024, tn=512, dest_major=True, out_dtype=BF16, name="d_w_ff1"))
    dt1, dt1b, g["ln1_g"], g["ln1_b"] = host(_mm_ln_bwd, dpre, w["w_ff1"], x, mix, w["ln1_g"], dt2, ALPHA,
                                             tm=256, name="d_x1")
    grad("w_out", host(_mm, ycat, dt1b, "tn", tm=512, tn=1024, out_dtype=BF16, name="d_w_out"))
    dycat = host(_mm, dt1b, w["w_out"], "nt", tm=1024, tn=1024, name="d_ycat")
    dxl, dgl, dcwb_l, dwa, dwx = host(_lru_bwd, proj, dycat, h_lru, cw_l, w["lru_conv_b"], wa_bd, ba, wx_bd, bx,
                                      w["lru_a_param"], name="lru_bwd")
    g["lru_gate_a_w"] = _unblockdiag(dwa)
    g["lru_gate_x_w"] = _unblockdiag(dwx)
    raw = dict(lru=dcwb_l, gate_a=g["lru_gate_a_w"].reshape(N_HEAD * HEAD_P, HEAD_P).astype(BF16),
               gate_x=g["lru_gate_x_w"].reshape(N_HEAD * HEAD_P, HEAD_P).astype(BF16))
    hooks.small(raw)
    dxact, ddt, dz, g["ssd_norm_w"], small = host(_ssd_bwd, xact, proj, dycat, hprev, bias_pad, alog_pad, dxp,
                                                   w["ssd_norm_w"], name="ssd_bwd")
    dxbc, dcwb_s = host(_conv_silu_bwd, proj, dxact, cw_s, w["ssd_conv_b"], col0=COL_XBC, width=XBC, ct=256,
                        name="ssd_conv_bwd")
    pieces, offsets = [dxl, dgl, dz, dxbc, ddt], [0, COL_G, COL_Z, COL_XBC, COL_DT]

    g["lru_conv_w"] = dcwb_l[0:4]
    g["lru_conv_b"] = dcwb_l[4:5]
    g["lru_gate_a_b"] = dcwb_l[5:6]
    g["lru_gate_x_b"] = dcwb_l[6:7]
    g["lru_a_param"] = dcwb_l[7:8]
    g["ssd_conv_w"] = dcwb_s[0:4]
    g["ssd_conv_b"] = dcwb_s[4:5]
    g["ssd_dt_bias"] = small[0:1, :N_HEAD]
    g["ssd_a_log"] = small[1:2, :N_HEAD]
    g["ssd_d"] = small[2:3, :N_HEAD]
    rows = jnp.concatenate([g[n] for n in ("ssd_norm_w", "ln1_g", "ln1_b", "ln2_g", "ln2_b", "ln3_g", "ln3_b")]
                           + [jnp.broadcast_to(loss[:, 0:1], (1, D_MODEL))], axis=0)
    late = dict(ssd=dcwb_s, heads=small, rows=rows)
    hooks.small(late)
    raw.update(late)
    dwt = None
    for q, (pc, off) in enumerate(zip(pieces, offsets)):
        dwt = host(_mm, pc, xb, "tn", tm=512, tn=1024, out_dtype=BF16, into=(dwt, off, COL_DT + LANE),
                   name="d_w_in_%d" % q)
    grad("w_in", dwt)
    hooks.pairs_now()
    grad_x = host(_mm_pieces, pieces, offsets, w["w_in_t"], tm=256, extra=dt1, epi=lambda acc, e: acc + ALPHA * e,
                  name="d_x")
    return loss[0, 0], grad_x, g, raw


ANY_SPEC = pl.BlockSpec(memory_space=pl.ANY)


def _mesh_pos():
    return lax.axis_index("x"), lax.axis_index("y"), lax.axis_index("c")


def _remote(src, dst, send, recv, k, to):
    return pltpu.make_async_remote_copy(src_ref=src, dst_ref=dst, send_sem=send.at[k], recv_sem=recv.at[k],
                                        device_id=to, device_id_type=MESH_T)


class _Job:
    N_SEM = 7

    def __init__(self, kind, inp):
        self.kind, self.inp = kind, inp
        shape = {"gather": (N_DEV,) + inp.shape, "pair": (4,) + inp.shape[1:], "chip": inp.shape}[kind]
        self.out = jax.ShapeDtypeStruct(shape, inp.dtype)

    def _places(self):
        x, y, c = _mesh_pos()
        return (x, y, c), (x, y, 1 - c), [(1 - x, y), (x, 1 - y), (1 - x, 1 - y)]

    def start(self, inp, out, send, recv, loc):
        me, sibling, chips = self._places()
        x, y, c = me
        if self.kind == "gather":
            mine = out.at[4 * x + 2 * y + c]
            pltpu.make_async_copy(inp, mine, loc.at[0]).start()
            _remote(inp, mine, send, recv, 0, sibling).start()
            for j, chip in enumerate(chips):
                _remote(inp, mine, send, recv, 1 + j, (*chip, c)).start()
        elif self.kind == "pair":
            for k in range(4):
                _remote(inp.at[2 * k + (1 - c)], out.at[k], send, recv, k, sibling).start()
        else:
            kme = 2 * x + y
            pltpu.make_async_copy(inp.at[kme], out.at[kme], loc.at[0]).start()
            for j, (tx, ty) in enumerate(chips):
                _remote(inp.at[2 * tx + ty], out.at[kme], send, recv, j, (tx, ty, c)).start()

    def mid(self, inp, out, send, recv, loc):
        if self.kind != "gather":
            return
        me, sibling, chips = self._places()
        c = me[2]
        for j, chip in enumerate(chips):
            landed = out.at[4 * chip[0] + 2 * chip[1] + c]
            _remote(landed, landed, send, recv, 1 + j, me).wait_recv()
            _remote(landed, landed, send, recv, 4 + j, sibling).start()

    def finish(self, inp, out, send, recv, loc):
        me, sibling, chips = self._places()
        x, y, c = me
        if self.kind == "gather":
            blk = lambda px, py, pc: out.at[4 * px + 2 * py + pc]
            mine = blk(*me)
            _remote(inp, blk(*sibling), send, recv, 0, me).wait_recv()
            for j, chip in enumerate(chips):
                _remote(inp, blk(*chip, 1 - c), send, recv, 4 + j, me).wait_recv()
            for k in range(7):
                _remote(inp, mine, send, recv, k, sibling).wait_send()
            pltpu.make_async_copy(inp, mine, loc.at[0]).wait()
        elif self.kind == "pair":
            for k in range(4):
                _remote(inp.at[2 * k + (1 - c)], out.at[k], send, recv, k, sibling).wait()
        else:
            kme = 2 * x + y
            for j, (tx, ty) in enumerate(chips):
                _remote(inp.at[kme], out.at[2 * tx + ty], send, recv, j, (tx, ty, c)).wait_recv()
            for j, (tx, ty) in enumerate(chips):
                _remote(inp.at[2 * tx + ty], out.at[kme], send, recv, j, (tx, ty, c)).wait_send()
            pltpu.make_async_copy(inp.at[kme], out.at[kme], loc.at[0]).wait()


def _job_scratch(jobs):
    sem = pltpu.SemaphoreType.DMA
    return [s for _ in jobs for s in (sem((_Job.N_SEM,)), sem((_Job.N_SEM,)), sem((1,)))]


def _run_jobs(jobs, method, jins, jouts, jsems):
    for q, job in enumerate(jobs):
        getattr(job, method)(jins[q], jouts[q], *jsems[3 * q:3 * q + 3])


def _exchange(jobs, *, name):
    n = len(jobs)

    def body(*refs):
        jins, jouts, jsems = refs[:n], refs[n:2 * n], refs[2 * n:]
        _run_jobs(jobs, "start", jins, jouts, jsems)
        _run_jobs(jobs, "mid", jins, jouts, jsems)
        _run_jobs(jobs, "finish", jins, jouts, jsems)

    return _pcall(body, in_specs=[ANY_SPEC] * n, out_specs=[ANY_SPEC] * n, out_shape=[j.out for j in jobs],
                  scratch_shapes=_job_scratch(jobs), name=name)(*[j.inp for j in jobs])


def _hosted(body, jobs, *, grid, in_specs, out_specs, out_shape, args, name, scratch_shapes=(), aliases=None):
    in_specs, out_specs, out_shape = list(in_specs), list(out_specs), list(out_shape)
    scratch_shapes = list(scratch_shapes)
    n_in, n_out, n_scr, nj = len(in_specs), len(out_specs), len(scratch_shapes), len(jobs)
    sem = ("arbitrary",) * len(grid)
    kw = dict(input_output_aliases=aliases) if aliases else {}
    if not jobs:
        res = _pcall(body, grid=grid, in_specs=in_specs, out_specs=out_specs, out_shape=out_shape,
                     scratch_shapes=scratch_shapes, name=name, compiler_params=_cparams(sem), **kw)(*args)
        return list(res), []

    def full(*refs):
        ins, jins = refs[:n_in], refs[n_in:n_in + nj]
        o0 = n_in + nj
        outs, jouts = refs[o0:o0 + n_out], refs[o0 + n_out:o0 + n_out + nj]
        s0 = o0 + n_out + nj
        scr, jsems = refs[s0:s0 + n_scr], refs[s0 + n_scr:]
        step = pl.program_id(0)
        for ax in range(1, len(grid)):
            step = step * grid[ax] + pl.program_id(ax)
        total = math.prod(grid)

        @pl.when(step == 0)
        def _():
            _run_jobs(jobs, "start", jins, jouts, jsems)

        body(*ins, *outs, *scr)

        @pl.when(step == total - 1)
        def _():
            _run_jobs(jobs, "mid", jins, jouts, jsems)
            _run_jobs(jobs, "finish", jins, jouts, jsems)

    res = _pcall(full, grid=grid, in_specs=in_specs + [ANY_SPEC] * nj, out_specs=out_specs + [ANY_SPEC] * nj,
                 out_shape=out_shape + [j.out for j in jobs], scratch_shapes=scratch_shapes + _job_scratch(jobs),
                 name=name, compiler_params=_cparams(sem), **kw)(*args, *[j.inp for j in jobs])
    return list(res[:n_out]), list(res[n_out:])


def _pair_add(g8, r4, cidx, *, name):
    _, r, c = g8.shape
    tr = ROW_TILE if r % ROW_TILE == 0 else r

    def body(c_ref, g_ref, r_ref, o_ref):
        o_ref[...] = (g_ref[...].astype(F32) + r_ref[...].astype(F32)).astype(BF16)

    return _pcall(
        body,
        grid_spec=pltpu.PrefetchScalarGridSpec(
            num_scalar_prefetch=1, grid=(4, r // tr),
            in_specs=[pl.BlockSpec((None, tr, c), lambda k, i, cr: (2 * k + cr[0], i, 0)),
                      pl.BlockSpec((None, tr, c), lambda k, i, cr: (k, i, 0))],
            out_specs=pl.BlockSpec((None, tr, c), lambda k, i, cr: (k, i, 0))),
        out_shape=jax.ShapeDtypeStruct((4, r, c), BF16), name=name,
        compiler_params=_cparams(("parallel", "parallel")))(cidx, g8, r4)


def _adam_update(g, w_ref, m_ref, v_ref, g_ref, d_ref, mo_ref, vo_ref):
    c1 = 1.0 - ADAM_B1 ** ADAM_STEP
    c2 = 1.0 - ADAM_B2 ** ADAM_STEP
    m2 = ADAM_B1 * m_ref[...] + (1.0 - ADAM_B1) * g
    v2 = ADAM_B2 * v_ref[...] + (1.0 - ADAM_B2) * (g * g)
    g_ref[...] = g
    mo_ref[...] = m2
    vo_ref[...] = v2
    d_ref[...] = -ADAM_LR * ((m2 / c1) / (jnp.sqrt(v2 / c2) + ADAM_EPS) + ADAM_WD * w_ref[...])


def _adamw_rows(srcs, items, own_cols, me1, *, name):
    ns, ni, no = len(srcs), len(items), len(own_cols)
    full = lambda a: pl.BlockSpec(a.shape, lambda i, me: (0,) * a.ndim)
    in_specs = [full(a) for a in srcs]
    args = list(srcs)
    for (si, _r0, w, _m, _v) in own_cols:
        a = srcs[si]
        in_specs.append(pl.BlockSpec((N_DEV, a.shape[1], w.shape[1]), lambda i, me: (0, 0, me[0])))
        args.append(a)
    out_specs, out_shape = [], []
    for (_si, _r0, w, m, v) in list(items) + list(own_cols):
        in_specs += [full(w)] * 3
        args += [w, m, v]
        out_specs += [full(w)] * 4
        out_shape += [jax.ShapeDtypeStruct(w.shape, F32)] * 4

    def body(me_ref, *refs):
        src_refs, own_refs = refs[:ns], refs[ns:ns + no]
        wmv = refs[ns + no:ns + no + 3 * (ni + no)]
        outs = refs[ns + no + 3 * (ni + no):]
        for q, (si, r0, w, _m, _v) in enumerate(list(items) + list(own_cols)):
            nr, cw = w.shape
            gref = src_refs[si] if q < ni else own_refs[q - ni]
            g = gref[0, r0:r0 + nr, 0:cw]
            for d in range(1, N_DEV):
                g = g + gref[d, r0:r0 + nr, 0:cw]
            _adam_update(g, *wmv[3 * q:3 * q + 3], *outs[4 * q:4 * q + 4])

    res = _pcall(
        body,
        grid_spec=pltpu.PrefetchScalarGridSpec(num_scalar_prefetch=1, grid=(1,), in_specs=in_specs, out_specs=out_specs),
        out_shape=out_shape, name=name, compiler_params=_cparams(("arbitrary",)))(me1, *args)
    return [tuple(res[4 * q:4 * q + 4]) for q in range(ni + no)]


def _adamw(gsrc, w, m, v, *, name):
    k, r, c = gsrc.shape
    tr = ROW_TILE if r % ROW_TILE == 0 else r

    def body(gs_ref, w_ref, m_ref, v_ref, g_ref, d_ref, mo_ref, vo_ref):
        g = gs_ref[0].astype(F32)
        for q in range(1, k):
            g = g + gs_ref[q].astype(F32)
        _adam_update(g, w_ref, m_ref, v_ref, g_ref, d_ref, mo_ref, vo_ref)

    tc = c
    if tr == r and r > ROW_TILE and c % 256 == 0:
        tc = 256
    blk = pl.BlockSpec((tr, tc), lambda i, j: (i, j))
    sd = jax.ShapeDtypeStruct((r, c), F32)
    return _pcall(body, grid=(r // tr, c // tc),
                  in_specs=[pl.BlockSpec((k, tr, tc), lambda i, j: (0, i, j)), blk, blk, blk],
                  out_specs=(blk, blk, blk, blk), out_shape=(sd, sd, sd, sd), name=name,
                  compiler_params=_cparams(("parallel", "parallel")))(gsrc, w, m, v)


WEIGHTS = ['w_in', 'lru_conv_w', 'lru_conv_b', 'lru_gate_a_w', 'lru_gate_a_b', 'lru_gate_x_w', 'lru_gate_x_b',
           'lru_a_param', 'ssd_conv_w', 'ssd_conv_b', 'ssd_dt_bias', 'ssd_a_log', 'ssd_d', 'ssd_norm_w', 'w_out',
           'ln1_g', 'ln1_b', 'w_ff1', 'w_ff2', 'ln2_g', 'ln2_b', 'w_ple_gate', 'w_ple', 'ln3_g', 'ln3_b']
BIG = ['w_in', 'w_out', 'w_ff1', 'w_ff2', 'w_ple_gate', 'w_ple']
COL_SHARDED = ('w_ff1', 'w_ple')
CONV = ['lru_conv_w', 'ssd_conv_w']
REPL = [n for n in WEIGHTS if n not in BIG and n not in CONV]
CONV_CH = {'lru_conv_w': LRU_W, 'ssd_conv_w': XBC}


def _to_dest_major(name, gfull):
    if name == 'w_in':
        gfull = gfull[:D_IN]
    if name in COL_SHARDED:
        r, cfull = gfull.shape
        return gfull.reshape(r, N_DEV, cfull // N_DEV).transpose(1, 0, 2)
    rfull, cdim = gfull.shape
    return gfull.reshape(N_DEV, rfull // N_DEV, cdim)


def _full_weight(name, gathered):
    if name in COL_SHARDED:
        _, r, cs = gathered.shape
        full = gathered.transpose(1, 0, 2).reshape(r, N_DEV * cs)
    else:
        _, rs, cdim = gathered.shape
        full = gathered.reshape(N_DEV * rs, cdim)
    if name == 'w_in':
        full = jnp.concatenate([full, jnp.zeros((D_IN_PAD - D_IN, D_MODEL), full.dtype)], axis=0)
    return full


SMALL_SRC = ("lru", "ssd", "heads", "rows", "gate_a", "gate_x")
AG_HOSTS = {"in_proj": ("w_ff1",), "lru_fwd": ("w_ff2",), "ssd_conv_fwd": ("w_ple_gate", "w_ple"), "ssd_fwd": ("w_out",)}
PAIR_HOSTS = ("d_x2", "d_pre", "d_x1", "d_ycat")
CHIP_HOSTS = {"lru_bwd": ("w_ple_gate", "w_ple", "w_ff2"), "ssd_bwd": ("w_ff1",), "ssd_conv_bwd": ("w_out",),
              "d_x": ("w_in",)}
SMALL_HOSTS = {"ssd_bwd": ("lru", "gate_a", "gate_x"), "d_w_in_3": ("ssd", "heads", "rows")}


class _Schedule:
    def __init__(self, shards, cidx):
        self.shards, self.cidx = shards, cidx
        self.pair, self.chip, self.small_jobs = [], [], []
        self.dest, self.summed, self.gathered_small = {}, {}, {}
        self.tags = []

    def ride(self, host):
        tags = []
        if host in AG_HOSTS:
            tags = [("weight", n, self.shards[n]) for n in AG_HOSTS[host]]
        elif host in PAIR_HOSTS or host in CHIP_HOSTS or host == "flush":
            tags = [("pair", n, a) for n, a in self.pair]
            self.pair = []
            if host not in PAIR_HOSTS:
                take = [t for t in self.chip if host == "flush" or t[0] in CHIP_HOSTS[host]]
                tags += [("chip", n, a) for n, a in take]
                self.chip = [t for t in self.chip if not any(t is u for u in take)]
        if host in SMALL_HOSTS:
            tags += [("small", n, a) for n, a in self.small_jobs if n in SMALL_HOSTS[host]]
            self.small_jobs = [t for t in self.small_jobs if t[0] not in SMALL_HOSTS[host]]
        self.tags = tags
        return [_Job({"weight": "gather", "small": "gather"}.get(kind, kind), a) for kind, _n, a in tags]

    def done(self, jobs, outs, w):
        for (kind, n, _a), o in zip(self.tags, outs):
            if kind == "weight":
                w[n] = _full_weight(n, o)
            elif kind == "small":
                self.gathered_small[n] = o
            elif kind == "pair":
                self.chip.append((n, _pair_add(self.dest[n], o, self.cidx, name="rs_pair_add_" + n)))
            else:
                self.summed[n] = o

    def grad(self, name, val):
        self.dest[name] = val if val.ndim == 3 else _to_dest_major(name, val)
        self.pair.append((name, self.dest[name]))

    def small(self, raw):
        self.small_jobs += list(raw.items())

    def pairs_now(self):
        tags = [("pair", n, a) for n, a in self.pair]
        self.pair, self.tags = [], tags
        jobs = [_Job("pair", a) for _k, _n, a in tags]
        self.done(jobs, _exchange(jobs, name="rs_pairs_now"), None)

    def flush(self):
        step = 0
        while self.pair or self.chip:
            jobs = self.ride("flush")
            self.done(jobs, _exchange(jobs, name="rs_flush_%d" % step), None)
            step += 1


def kernel(x, p, w_in, lru_conv_w, lru_conv_b, lru_gate_a_w, lru_gate_a_b, lru_gate_x_w, lru_gate_x_b, lru_a_param, ssd_conv_w, ssd_conv_b, ssd_dt_bias, ssd_a_log, ssd_d, ssd_norm_w, w_out, ln1_g, ln1_b, w_ff1, w_ff2, ln2_g, ln2_b, w_ple_gate, w_ple, ln3_g, ln3_b, loss_target, m_w_in, m_lru_conv_w, m_lru_conv_b, m_lru_gate_a_w, m_lru_gate_a_b, m_lru_gate_x_w, m_lru_gate_x_b, m_lru_a_param, m_ssd_conv_w, m_ssd_conv_b, m_ssd_dt_bias, m_ssd_a_log, m_ssd_d, m_ssd_norm_w, m_w_out, m_ln1_g, m_ln1_b, m_w_ff1, m_w_ff2, m_ln2_g, m_ln2_b, m_w_ple_gate, m_w_ple, m_ln3_g, m_ln3_b, v_w_in, v_lru_conv_w, v_lru_conv_b, v_lru_gate_a_w, v_lru_gate_a_b, v_lru_gate_x_w, v_lru_gate_x_b, v_lru_a_param, v_ssd_conv_w, v_ssd_conv_b, v_ssd_dt_bias, v_ssd_a_log, v_ssd_d, v_ssd_norm_w, v_w_out, v_ln1_g, v_ln1_b, v_w_ff1, v_w_ff2, v_ln2_g, v_ln2_b, v_w_ple_gate, v_w_ple, v_ln3_g, v_ln3_b):
    given = dict(locals())
    def local(a, n):
        return jnp.swapaxes(a[0], 0, 1) if n == 'w_in' else a[0]

    wsh = {n: local(given[n], n) for n in WEIGHTS}
    msh = {n: local(given["m_" + n], n) for n in WEIGHTS}
    vsh = {n: local(given["v_" + n], n) for n in WEIGHTS}
    xi, yi, ci = _mesh_pos()
    me = 4 * xi + 2 * yi + ci

    shards = {n: wsh[n].astype(BF16) for n in BIG}
    conv_pack = jnp.concatenate([_pad_rows8(wsh[n]) for n in CONV], axis=1)
    g_in, gconv = _exchange([_Job("gather", shards['w_in']), _Job("gather", conv_pack)], name="ag_first")
    full = {'w_in_t': _full_weight('w_in', g_in)}
    c0 = 0
    for n in CONV:
        cw = CONV_CH[n] // N_DEV
        full[n] = gconv[:, :4, c0:c0 + cw].transpose(1, 0, 2).reshape(4, CONV_CH[n])
        c0 += cw
    for n in REPL:
        full[n] = given[n] if given[n].ndim == 2 else wsh[n]

    sched = _Schedule(shards, jnp.reshape(ci, (1,)).astype(jnp.int32))
    loss_local, grad_x, g, raw = _local_step(x[0], p[0, 0], loss_target[0], full, sched)
    sched.flush()
    summed, gat = sched.summed, sched.gathered_small
    loss = gat["rows"][0, 7, 0]
    for d in range(1, N_DEV):
        loss = loss + gat["rows"][d, 7, 0]

    outs = {}
    for n in BIG:
        outs[n] = _adamw(summed[n], wsh[n], msh[n], vsh[n], name="adamw_" + n)
    for n, k in (("lru_gate_a_w", "gate_a"), ("lru_gate_x_w", "gate_x")):
        flat = lambda a: a.reshape(N_HEAD * HEAD_P, HEAD_P)
        res = _adamw(gat[k], flat(wsh[n]), flat(msh[n]), flat(vsh[n]), name="adamw_" + n)
        outs[n] = tuple(r.reshape(N_HEAD, HEAD_P, HEAD_P) for r in res)
    row_items = [("lru_conv_b", 0, 4), ("lru_gate_a_b", 0, 5), ("lru_gate_x_b", 0, 6), ("lru_a_param", 0, 7),
                 ("ssd_conv_b", 1, 4), ("ssd_dt_bias", 2, 0), ("ssd_a_log", 2, 1), ("ssd_d", 2, 2),
                 ("ssd_norm_w", 3, 0), ("ln1_g", 3, 1), ("ln1_b", 3, 2), ("ln2_g", 3, 3), ("ln2_b", 3, 4),
                 ("ln3_g", 3, 5), ("ln3_b", 3, 6)]
    vec = lambda a: a.reshape(1, -1)
    items = [(si, r0, vec(given[n]), vec(given["m_" + n]), vec(given["v_" + n])) for n, si, r0 in row_items]
    own = [(si, 0, wsh[n], msh[n], vsh[n]) for n, si in (("lru_conv_w", 0), ("ssd_conv_w", 1))]
    me1 = jnp.reshape(me, (1,)).astype(jnp.int32)
    res = _adamw_rows([gat[k] for k in SMALL_SRC[:4]], items, own, me1, name="adamw_small")
    for (n, _si, _r0), r4 in zip(row_items, res[:len(row_items)]):
        outs[n] = r4
    for n, r4 in zip(CONV, res[len(row_items):]):
        outs[n] = r4

    def fin(n, k):
        a = jnp.swapaxes(outs[n][k], 0, 1) if n == 'w_in' else outs[n][k]
        return a.reshape(given[n].shape)

    return (loss, grad_x[None],
            *[fin(n, 0) for n in WEIGHTS], *[fin(n, 1) for n in WEIGHTS],
            *[fin(n, 2) for n in WEIGHTS], *[fin(n, 3) for n in WEIGHTS])
```

```python
import math

import jax
import jax.numpy as jnp
from jax import lax
from jax.experimental import pallas as pl
from jax.experimental.pallas import tpu as pltpu

F32 = jnp.float32
BF16 = jnp.bfloat16
HI = lax.Precision.HIGHEST

N_DEV = 8
D_MODEL = 1024
LRU_W = 1024
SSD_W = 1024
XBC = 2048
N_HEAD = 16
HEAD_P = 64
N_GROUP = 4
GROUP_W = 256
N_STATE = 128
CHUNK = 128
D_FF = 4096
PLE_DIM = 256
D_IN = 5136
D_IN_PAD = 5632
COL_G = 1024
COL_Z = 2048
COL_XBC = 3072
COL_DT = 5120
LRU_C = 8.0
ALPHA = 2.0 ** 0.25
LN_EPS = 1e-5
RMS_EPS = 1e-5
ADAM_LR = 0.001
ADAM_B1 = 0.9
ADAM_B2 = 0.999
ADAM_EPS = 1e-08
ADAM_WD = 0.01
ADAM_STEP = 10
GELU_C = math.sqrt(2.0 / math.pi)
LANE = 128
SUBLANE = 8
VMEM_LIMIT = 48 * 1024 * 1024
MESH_T = pl.DeviceIdType.MESH
NEG_BIG = -1e30


def _pcall(body, **kw):
    return pl.pallas_call(body, **kw)


def _cparams(sem):
    return pltpu.CompilerParams(dimension_semantics=sem, vmem_limit_bytes=VMEM_LIMIT)


def _dot(a, b):
    return jnp.dot(a.astype(BF16), b.astype(BF16), preferred_element_type=F32)


def _dot_nt(a, b):
    return lax.dot_general(a.astype(BF16), b.astype(BF16), (((1,), (1,)), ((), ())), preferred_element_type=F32)


def _dot_tn(a, b):
    return lax.dot_general(a.astype(BF16), b.astype(BF16), (((0,), (0,)), ((), ())), preferred_element_type=F32)


def _dotx(a, b):
    return jnp.dot(a, b, precision=HI, preferred_element_type=F32)


def _sigmoid(x):
    return jax.nn.sigmoid(x)


def _softplus(v):
    return jnp.maximum(v, 0.0) + jnp.log1p(jnp.exp(-jnp.abs(v)))


def _gelu(x):
    th = jnp.tanh(GELU_C * (x + 0.044715 * x * x * x))
    return 0.5 * x * (1.0 + th), th


def _gelu_grad(x, th):
    return 0.5 * (1.0 + th) + 0.5 * x * (1.0 - th * th) * GELU_C * (1.0 + 3.0 * 0.044715 * x * x)


def _iota(shape, dim):
    return lax.broadcasted_iota(jnp.int32, shape, dim)


def _mm(a, b, mode, *, tm, tn, name, a_fn=None, extra=None, epi=None, out_dtype=F32, dest_major=False, into=None,
        jobs=()):
    m = a.shape[1] if mode == "tn" else a.shape[0]
    n = b.shape[0] if mode == "nt" else b.shape[1]
    tm, tn = min(tm, m), min(tn, n)
    if dest_major:
        tn = n // N_DEV
    if mode == "nn":
        m, k = a.shape
        _, n = b.shape
        a_spec = pl.BlockSpec((tm, k), lambda i, j: (i, 0))
        b_spec = pl.BlockSpec((k, tn), lambda i, j: (0, j))
        dims = ((1,), (0,))
    elif mode == "nt":
        m, k = a.shape
        n, _ = b.shape
        a_spec = pl.BlockSpec((tm, k), lambda i, j: (i, 0))
        b_spec = pl.BlockSpec((tn, k), lambda i, j: (j, 0))
        dims = ((1,), (1,))
    else:
        k, m = a.shape
        _, n = b.shape
        a_spec = pl.BlockSpec((k, tm), lambda i, j: (0, i))
        b_spec = pl.BlockSpec((k, tn), lambda i, j: (0, j))
        dims = ((0,), (0,))
    assert m % tm == 0 and n % tn == 0, (name, m, n, tm, tn)
    o_spec = pl.BlockSpec((tm, tn), lambda i, j: (i, j))
    in_specs = [a_spec, b_spec]
    args = [a, b]
    if extra is not None:
        in_specs.append(o_spec)
        args.append(extra)

    def body(*refs):
        a_ref, b_ref, o_ref = refs[0], refs[1], refs[-1]
        av = a_ref[...]
        if a_fn is not None:
            av = a_fn(av)
        acc = lax.dot_general(av.astype(BF16), b_ref[...].astype(BF16), (dims, ((), ())), preferred_element_type=F32)
        if epi is not None:
            acc = epi(acc, refs[2][...])
        o_ref[...] = acc.astype(out_dtype)

    out_shape = jax.ShapeDtypeStruct((m, n), out_dtype)
    aliases = None
    if dest_major:
        assert extra is None
        o_spec = pl.BlockSpec((None, tm, tn), lambda i, j: (j, i, 0))
        out_shape = jax.ShapeDtypeStruct((N_DEV, m, tn), out_dtype)
    if into is not None:
        buf, row0, total = into
        assert extra is None and row0 % tm == 0
        o_spec = pl.BlockSpec((tm, tn), lambda i, j: (row0 // tm + i, j))
        out_shape = jax.ShapeDtypeStruct((total, n), out_dtype)
        if buf is not None:
            in_specs.append(ANY_SPEC)
            args.append(buf)
            aliases = {len(args) - 1: 0}
    (out,), jouts = _hosted(body, jobs, grid=(m // tm, n // tn), in_specs=in_specs, out_specs=[o_spec],
                            out_shape=[out_shape], args=args, name=name, aliases=aliases)
    return (out, jouts) if jobs else out


def _mm_pieces(pieces, offsets, b, *, tm, name, extra, epi, jobs=()):
    m = pieces[0].shape[0]
    kb, n = b.shape
    tm = min(tm, m)
    row = lambda wdt: pl.BlockSpec((tm, wdt), lambda i: (i, 0))
    in_specs = [row(pc.shape[1]) for pc in pieces] + [pl.BlockSpec((kb, n), lambda i: (0, 0)), row(n)]
    np_ = len(pieces)

    def body(*refs):
        b_ref, e_ref, o_ref = refs[np_], refs[np_ + 1], refs[np_ + 2]
        acc = jnp.zeros((tm, n), F32)
        for q in range(np_):
            kq = pieces[q].shape[1]
            acc = acc + jnp.dot(refs[q][...].astype(BF16), b_ref[offsets[q]:offsets[q] + kq, :].astype(BF16),
                                preferred_element_type=F32)
        o_ref[...] = epi(acc, e_ref[...])

    (out,), jouts = _hosted(body, jobs, grid=(m // tm,), in_specs=in_specs, out_specs=[row(n)],
                            out_shape=[jax.ShapeDtypeStruct((m, n), F32)], args=list(pieces) + [b, extra], name=name)
    return (out, jouts) if jobs else out


def _relu2(v):
    r = jnp.maximum(v, 0.0)
    return r * r


ROW_TILE = 256


def _ln_stats(t):
    mu = jnp.mean(t, axis=-1, keepdims=True)
    xc = t - mu
    var = jnp.mean(xc * xc, axis=-1, keepdims=True)
    rstd = lax.rsqrt(var + LN_EPS)
    return xc * rstd, rstd


def _ln_bwd_rows(dy, xhat, rstd, g):
    dxh = dy * g
    m1 = jnp.mean(dxh, axis=-1, keepdims=True)
    m2 = jnp.mean(dxh * xhat, axis=-1, keepdims=True)
    return rstd * (dxh - m1 - xhat * m2)


def _mm_ln(a, b, res, g, beta, *, tm, name, a_fn=None):
    m, k = a.shape
    d = b.shape[1]
    tm = min(tm, m)
    row = pl.BlockSpec((tm, d), lambda i: (i, 0))
    par = pl.BlockSpec((1, d), lambda i: (0, 0))

    def body(a_ref, b_ref, r_ref, g_ref, be_ref, br_ref, y_ref, yb_ref):
        av = a_ref[...]
        if a_fn is not None:
            av = a_fn(av)
        acc = jnp.dot(av.astype(BF16), b_ref[...].astype(BF16), preferred_element_type=F32)
        br_ref[...] = acc
        xhat, _ = _ln_stats(ALPHA * r_ref[...] + acc)
        y = xhat * g_ref[...] + be_ref[...]
        y_ref[...] = y
        yb_ref[...] = y.astype(BF16)

    sd = jax.ShapeDtypeStruct((m, d), F32)
    return _pcall(body, grid=(m // tm,),
                  in_specs=[pl.BlockSpec((tm, k), lambda i: (i, 0)), pl.BlockSpec((k, d), lambda i: (0, 0)), row, par, par],
                  out_specs=(row, row, row), out_shape=(sd, sd, jax.ShapeDtypeStruct((m, d), BF16)), name=name,
                  compiler_params=_cparams(("parallel",)))(a, b, res, g, beta)


def _mm_ln_bwd(a, b, res, branch, g, dy0, coef0, *, tm, name, jobs=()):
    m, k = a.shape
    d = b.shape[0]
    tm = min(tm, m)
    row = pl.BlockSpec((tm, d), lambda i: (i, 0))
    par = pl.BlockSpec((1, d), lambda i: (0, 0))

    def body(a_ref, b_ref, r_ref, br_ref, g_ref, dy0_ref, dt_ref, dtb_ref, dg_ref, db_ref):
        acc = lax.dot_general(a_ref[...].astype(BF16), b_ref[...].astype(BF16), (((1,), (1,)), ((), ())),
                              preferred_element_type=F32)
        dy = coef0 * dy0_ref[...] + acc
        xhat, rstd = _ln_stats(ALPHA * r_ref[...] + br_ref[...])
        dt = _ln_bwd_rows(dy, xhat, rstd, g_ref[...])
        dt_ref[...] = dt
        dtb_ref[...] = dt.astype(BF16)

        @pl.when(pl.program_id(0) == 0)
        def _():
            dg_ref[...] = jnp.zeros_like(dg_ref)
            db_ref[...] = jnp.zeros_like(db_ref)

        dg_ref[...] += jnp.sum(dy * xhat, axis=0, keepdims=True)
        db_ref[...] += jnp.sum(dy, axis=0, keepdims=True)

    pd = jax.ShapeDtypeStruct((1, d), F32)
    outs, jouts = _hosted(
        body, jobs, grid=(m // tm,),
        in_specs=[pl.BlockSpec((tm, k), lambda i: (i, 0)), pl.BlockSpec((d, k), lambda i: (0, 0)), row, row, par, row],
        out_specs=(row, row, par, par),
        out_shape=(jax.ShapeDtypeStruct((m, d), F32), jax.ShapeDtypeStruct((m, d), BF16), pd, pd),
        args=(a, b, res, branch, g, dy0), name=name)
    return (tuple(outs), jouts) if jobs else tuple(outs)


def _head(x2, gpre, ple, g, beta, tgt, *, name):
    s, d = x2.shape
    row = pl.BlockSpec((ROW_TILE, d), lambda i: (i, 0))
    par = pl.BlockSpec((1, d), lambda i: (0, 0))
    lsp = pl.BlockSpec((1, LANE), lambda i: (0, 0))

    def body(x2_ref, gp_ref, ple_ref, g_ref, be_ref, t_ref, loss_ref, dgp_ref, dple_ref, dt_ref, dg_ref, db_ref):
        gate = _sigmoid(gp_ref[...])
        ple_v = ple_ref[...]
        xhat, rstd = _ln_stats(ALPHA * x2_ref[...] + gate * ple_v)
        err = xhat * g_ref[...] + be_ref[...] - t_ref[...]
        dy = err * (1.0 / d)
        dt = _ln_bwd_rows(dy, xhat, rstd, g_ref[...])
        dt_ref[...] = dt
        dgp_ref[...] = (dt * ple_v * gate * (1.0 - gate)).astype(BF16)
        dple_ref[...] = (dt * gate).astype(BF16)

        @pl.when(pl.program_id(0) == 0)
        def _():
            loss_ref[...] = jnp.zeros_like(loss_ref)
            dg_ref[...] = jnp.zeros_like(dg_ref)
            db_ref[...] = jnp.zeros_like(db_ref)

        loss_ref[...] += 0.5 * jnp.sum(jnp.mean(err * err, axis=-1, keepdims=True))
        dg_ref[...] += jnp.sum(dy * xhat, axis=0, keepdims=True)
        db_ref[...] += jnp.sum(dy, axis=0, keepdims=True)

    sd = jax.ShapeDtypeStruct((s, d), F32)
    sb = jax.ShapeDtypeStruct((s, d), BF16)
    pd = jax.ShapeDtypeStruct((1, d), F32)
    return _pcall(body, grid=(s // ROW_TILE,), in_specs=[row, row, row, par, par, row],
                  out_specs=(lsp, row, row, row, par, par),
                  out_shape=(jax.ShapeDtypeStruct((1, LANE), F32), sb, sb, sd, pd, pd),
                  name=name, compiler_params=_cparams(("arbitrary",)))(x2, gpre, ple, g, beta, tgt)


CONV_R = 256
PAD = SUBLANE


def _shift_down(ext, s):
    if s == 0:
        return ext[PAD:, :]
    return pltpu.roll(ext, s, 0)[PAD:, :]


def _shift_up(ext, s):
    r = ext.shape[0] - PAD
    if s == 0:
        return ext[:r, :]
    return pltpu.roll(ext, r + PAD - s, 0)[:r, :]


def _conv_rows(xpad_ref, r0, w_ref):
    ext = xpad_ref[pl.ds(r0, CONV_R + PAD), :]
    acc = _shift_down(ext, 0) * w_ref[3:4, :]
    for k in range(3):
        acc = acc + _shift_down(ext, 3 - k) * w_ref[k:k + 1, :]
    return acc, ext


def _fill_front_padded(dst_ref, src_ref, s):
    dst_ref[0:PAD, :] = jnp.zeros((PAD, dst_ref.shape[1]), F32)

    def cp(q, _):
        r0 = pl.multiple_of(q * CONV_R, CONV_R)
        dst_ref[pl.ds(pl.multiple_of(PAD + r0, PAD), CONV_R), :] = src_ref[pl.ds(r0, CONV_R), :]
        return 0

    lax.fori_loop(0, s // CONV_R, cp, 0)


def _conv_silu_fwd(proj, w8, b, *, col0, width, ct, name, jobs=()):
    s = proj.shape[0]
    nb = col0 // ct

    def body(x_ref, w_ref, b_ref, o_ref, xpad):
        _fill_front_padded(xpad, x_ref, s)

        def step(q, _):
            r0 = pl.multiple_of(q * CONV_R, CONV_R)
            acc, _e = _conv_rows(xpad, r0, w_ref)
            pre = acc + b_ref[...]
            o_ref[pl.ds(r0, CONV_R), :] = pre * _sigmoid(pre)
            return 0

        lax.fori_loop(0, s // CONV_R, step, 0)

    (out,), jouts = _hosted(
        body, jobs, grid=(width // ct,),
        in_specs=[pl.BlockSpec((s, ct), lambda j: (0, nb + j)), pl.BlockSpec((SUBLANE, ct), lambda j: (0, j)),
                  pl.BlockSpec((1, ct), lambda j: (0, j))],
        out_specs=[pl.BlockSpec((s, ct), lambda j: (0, j))],
        out_shape=[jax.ShapeDtypeStruct((s, width), F32)],
        scratch_shapes=[pltpu.VMEM((s + PAD, ct), F32)], name=name, args=(proj, w8, b))
    return (out, jouts) if jobs else out


def _conv_bwd_rows(dpad_ref, r0, w_ref):
    return _conv_bwd_ext(dpad_ref[pl.ds(r0, CONV_R + PAD), :], w_ref)


def _conv_bwd_ext(ext, w_ref):
    acc = _shift_up(ext, 0) * w_ref[3:4, :]
    for k in range(3):
        acc = acc + _shift_up(ext, 3 - k) * w_ref[k:k + 1, :]
    return acc


def _conv_silu_bwd(proj, dact, w8, b, *, col0, width, ct, name, jobs=()):
    s = proj.shape[0]
    nb = col0 // ct

    def body(x_ref, d_ref, w_ref, b_ref, dx_ref, dwb_ref, xpad, dpad):
        _fill_front_padded(xpad, x_ref, s)
        dpad[pl.ds(s, PAD), :] = jnp.zeros((PAD, ct), F32)
        dwb_ref[...] = jnp.zeros_like(dwb_ref)

        def step(q, _):
            r0 = pl.multiple_of(q * CONV_R, CONV_R)
            acc, ext = _conv_rows(xpad, r0, w_ref)
            pre = acc + b_ref[...]
            sg = _sigmoid(pre)
            dpre = d_ref[pl.ds(r0, CONV_R), :] * sg * (1.0 + pre * (1.0 - sg))
            dpad[pl.ds(r0, CONV_R), :] = dpre
            for k in range(4):
                dwb_ref[k:k + 1, :] += jnp.sum(dpre * _shift_down(ext, 3 - k), axis=0, keepdims=True)
            dwb_ref[4:5, :] += jnp.sum(dpre, axis=0, keepdims=True)
            return 0

        lax.fori_loop(0, s // CONV_R, step, 0)

        def step2(q, _):
            r0 = pl.multiple_of(q * CONV_R, CONV_R)
            dx_ref[pl.ds(r0, CONV_R), :] = _conv_bwd_rows(dpad, r0, w_ref).astype(BF16)
            return 0

        lax.fori_loop(0, s // CONV_R, step2, 0)

    colb = pl.BlockSpec((s, ct), lambda j: (0, j))
    outs, jouts = _hosted(
        body, jobs, grid=(width // ct,),
        in_specs=[pl.BlockSpec((s, ct), lambda j: (0, nb + j)), colb, pl.BlockSpec((SUBLANE, ct), lambda j: (0, j)),
                  pl.BlockSpec((1, ct), lambda j: (0, j))],
        out_specs=(colb, pl.BlockSpec((SUBLANE, ct), lambda j: (0, j))),
        out_shape=(jax.ShapeDtypeStruct((s, width), BF16), jax.ShapeDtypeStruct((SUBLANE, width), F32)),
        scratch_shapes=[pltpu.VMEM((s + PAD, ct), F32), pltpu.VMEM((s + PAD, ct), F32)], name=name,
        args=(proj, dact, w8, b))
    return (tuple(outs), jouts) if jobs else tuple(outs)


LRU_CT = 128


def _row_of(v, r):
    return jnp.sum(jnp.where(_iota((v.shape[0], 1), 0) == r, v, 0.0), axis=0, keepdims=True)


def _scan_fwd(a, u):
    r = a.shape[0]
    row = _iota((r, 1), 0)
    d = 1
    while d < r:
        valid = row >= d
        u = jnp.where(valid, a * pltpu.roll(u, d, 0) + u, u)
        a = jnp.where(valid, a * pltpu.roll(a, d, 0), a)
        d *= 2
    return a, u


def _scan_rev(b, u):
    r = b.shape[0]
    row = _iota((r, 1), 0)
    d = 1
    while d < r:
        valid = row < r - d
        u = jnp.where(valid, b * pltpu.roll(u, r - d, 0) + u, u)
        b = jnp.where(valid, b * pltpu.roll(b, r - d, 0), b)
        d *= 2
    return b, u


def _lru_chunk(xpad, r0, cw_ref, cb, wa, ba, wx, bx, sp):
    acc, ext = _conv_rows(xpad, r0, cw_ref)
    xl = acc + cb
    r = _sigmoid(_dot(xl, wa) + ba)
    i = _sigmoid(_dot(xl, wx) + bx)
    la = -LRU_C * r * sp
    a = jnp.exp(la)
    a2 = jnp.exp(2.0 * la)
    mult = jnp.sqrt(-jnp.tanh(la) * (a2 + 1.0))
    first = (r0 + _iota((CONV_R, 1), 0)) == 0
    mult = jnp.where(first, 1.0, mult)
    return ext, xl, r, i, a, a2, mult, first


def _lru_specs(s):
    ct = LRU_CT
    nb_g = COL_G // ct
    return dict(
        x=pl.BlockSpec((s, ct), lambda j: (0, j)),
        g=pl.BlockSpec((s, ct), lambda j: (0, nb_g + j)),
        col=pl.BlockSpec((s, ct), lambda j: (0, j)),
        cw=pl.BlockSpec((SUBLANE, ct), lambda j: (0, j)),
        vec=pl.BlockSpec((1, ct), lambda j: (0, j)),
        gate=pl.BlockSpec((None, ct, ct), lambda j: (j, 0, 0)),
    )


def _lru_fwd(proj, cw8, cb, wa_bd, ba, wx_bd, bx, ap, *, name, jobs=()):
    s = proj.shape[0]
    ct = LRU_CT
    sp_ = _lru_specs(s)

    def body(x_ref, g_ref, cw_ref, cb_ref, wa_ref, ba_ref, wx_ref, bx_ref, ap_ref, y_ref, h_ref, xpad):
        _fill_front_padded(xpad, x_ref, s)
        sp = _softplus(-ap_ref[...])

        def step(q, carry):
            r0 = pl.multiple_of(q * CONV_R, CONV_R)
            _e, xl, _r, i, a, _a2, mult, _f = _lru_chunk(xpad, r0, cw_ref, cb_ref[...], wa_ref[...], ba_ref[...],
                                                       wx_ref[...], bx_ref[...], sp)
            acum, ucum = _scan_fwd(a, xl * i * mult)
            h = acum * carry + ucum
            h_ref[pl.ds(r0, CONV_R), :] = h
            ge, _th = _gelu(g_ref[pl.ds(r0, CONV_R), :])
            y_ref[pl.ds(r0, CONV_R), :] = (ge * h).astype(BF16)
            return _row_of(h, CONV_R - 1)

        lax.fori_loop(0, s // CONV_R, step, jnp.zeros((1, ct), F32))

    (ymix, hs), jouts = _hosted(
        body, jobs, grid=(LRU_W // ct,),
        in_specs=[sp_["x"], sp_["g"], sp_["cw"], sp_["vec"], sp_["gate"], sp_["vec"], sp_["gate"], sp_["vec"], sp_["vec"]],
        out_specs=(sp_["col"], sp_["col"]),
        out_shape=(jax.ShapeDtypeStruct((s, LRU_W + SSD_W), BF16), jax.ShapeDtypeStruct((s, LRU_W), F32)),
        scratch_shapes=[pltpu.VMEM((s + PAD, ct), F32)],
        name=name, args=(proj, proj, cw8, cb, wa_bd, ba, wx_bd, bx, ap))
    return ((ymix, hs), jouts) if jobs else (ymix, hs)


def _lru_bwd(proj, dy, hs, cw8, cb, wa_bd, ba, wx_bd, bx, ap, *, name, jobs=()):
    s = proj.shape[0]
    ct = LRU_CT
    sp_ = _lru_specs(s)

    nq = s // CONV_R

    def body(x_ref, g_ref, dy_ref, h_ref, cw_ref, cb_ref, wa_ref, ba_ref, wx_ref, bx_ref, ap_ref,
             dx_ref, dg_ref, dcwb_ref, dwa_ref, dwx_ref, xpad, hpad):
        _fill_front_padded(xpad, x_ref, s)
        _fill_front_padded(hpad, h_ref, s)
        apv = ap_ref[...]
        sp = _softplus(-apv)
        cb_v, wa, ba_v, wx, bx_v = cb_ref[...], wa_ref[...], ba_ref[...], wx_ref[...], bx_ref[...]
        dcwb_ref[...] = jnp.zeros_like(dcwb_ref)
        dwa_ref[...] = jnp.zeros_like(dwa_ref)
        dwx_ref[...] = jnp.zeros_like(dwx_ref)

        def back(k, carry):
            g_next, a_next, dxl_next = carry
            last_row = _iota((CONV_R, 1), 0) == CONV_R - 1
            r0 = pl.multiple_of((nq - 1 - k) * CONV_R, CONV_R)
            ext, xl, r, i, a, a2, mult, first = _lru_chunk(xpad, r0, cw_ref, cb_v, wa, ba_v, wx, bx_v, sp)
            gv = g_ref[pl.ds(r0, CONV_R), :]
            dyv = dy_ref[pl.ds(r0, CONV_R), :]
            hext = hpad[pl.ds(r0, CONV_R + PAD), :]
            ge, th = _gelu(gv)
            dg_ref[pl.ds(r0, CONV_R), :] = (dyv * _shift_down(hext, 0) * _gelu_grad(gv, th)).astype(BF16)
            b = jnp.where(last_row, a_next, pltpu.roll(a, CONV_R - 1, 0))
            bcum, dcum = _scan_rev(b, dyv * ge)
            gval = dcum + bcum * g_next
            hprev = _shift_down(hext, 1)
            da = gval * hprev
            dxl = gval * i * mult
            di = gval * xl * mult
            dmult = jnp.where(first, 0.0, gval * xl * i)
            dla = da * a - dmult * a2 / mult
            dr = dla * (-LRU_C) * sp
            dcwb_ref[7:8, :] += jnp.sum(dla * (-LRU_C) * r, axis=0, keepdims=True)
            dpr = dr * r * (1.0 - r)
            dpi = di * i * (1.0 - i)
            dxl = dxl + _dot_nt(dpr, wa) + _dot_nt(dpi, wx)
            dwa_ref[...] += _dot_tn(xl, dpr)
            dwx_ref[...] += _dot_tn(xl, dpi)
            dcwb_ref[5:6, :] += jnp.sum(dpr, axis=0, keepdims=True)
            dcwb_ref[6:7, :] += jnp.sum(dpi, axis=0, keepdims=True)
            for tap in range(4):
                dcwb_ref[tap:tap + 1, :] += jnp.sum(dxl * _shift_down(ext, 3 - tap), axis=0, keepdims=True)
            dcwb_ref[4:5, :] += jnp.sum(dxl, axis=0, keepdims=True)
            dx_ref[pl.ds(r0, CONV_R), :] = _conv_bwd_ext(jnp.concatenate([dxl, dxl_next], axis=0), cw_ref).astype(BF16)
            return _row_of(gval, 0), _row_of(a, 0), dxl[:PAD, :]

        zero = jnp.zeros((1, ct), F32)
        lax.fori_loop(0, nq, back, (zero, zero, jnp.zeros((PAD, ct), F32)))
        dcwb_ref[7:8, :] = dcwb_ref[7:8, :] * (-_sigmoid(-apv))

    nt = LRU_W // ct
    outs, jouts = _hosted(
        body, jobs, grid=(nt,),
        in_specs=[sp_["x"], sp_["g"], sp_["col"], sp_["col"], sp_["cw"], sp_["vec"], sp_["gate"], sp_["vec"], sp_["gate"],
                  sp_["vec"], sp_["vec"]],
        out_specs=(sp_["col"], sp_["col"], sp_["cw"], sp_["gate"], sp_["gate"]),
        out_shape=(jax.ShapeDtypeStruct((s, LRU_W), BF16), jax.ShapeDtypeStruct((s, LRU_W), BF16),
                   jax.ShapeDtypeStruct((SUBLANE, LRU_W), F32), jax.ShapeDtypeStruct((nt, ct, ct), F32),
                   jax.ShapeDtypeStruct((nt, ct, ct), F32)),
        scratch_shapes=[pltpu.VMEM((s + PAD, ct), F32), pltpu.VMEM((s + PAD, ct), F32)],
        name=name, args=(proj, proj, dy, hs, cw8, cb, wa_bd, ba, wx_bd, bx, ap))
    return (tuple(outs), jouts) if jobs else tuple(outs)


def _split3(v):
    hi = v.astype(BF16)
    r1 = v - hi.astype(F32)
    mid = r1.astype(BF16)
    lo = (r1 - mid.astype(F32)).astype(BF16)
    return hi, mid, lo


def _dot01(m01, v):
    mb = m01.astype(BF16)
    hi, mid, lo = _split3(v)
    f = lambda part: jnp.dot(mb, part, preferred_element_type=F32)
    return f(hi) + f(mid) + f(lo)


def _dot01_r(v, m01):
    mb = m01.astype(BF16)
    hi, mid, lo = _split3(v)
    f = lambda part: jnp.dot(part, mb, preferred_element_type=F32)
    return f(hi) + f(mid) + f(lo)


def _ssd_prep(dtr, bias, alog_pad):
    l = CHUNK
    lane = _iota((1, LANE), 1)
    a_head = jnp.where(lane < N_HEAD, -jnp.exp(alog_pad), 0.0)
    dt = _softplus(dtr + bias)
    tril = (_iota((l, l), 1) <= _iota((l, l), 0)).astype(F32)
    a = dt * a_head
    cs = _dot01(tril, a)
    tot = jnp.sum(a, axis=0, keepdims=True)
    return dict(a_head=a_head, dt=dt, tril=tril, cs=cs, tot=tot)


def _col(v, h):
    lane = _iota(v.shape, 1)
    return jnp.sum(jnp.where(lane == h, v, 0.0), axis=1, keepdims=True)


def _decay_mat(cs, cst_ref, h, causal):
    row = cst_ref[h:h + 1, :]
    return jnp.exp(jnp.where(causal, _col(cs, h) - row, NEG_BIG))


def _head_mask(j, rows=CHUNK):
    lane = _iota((rows, GROUP_W), 1)
    return (lane >= j * HEAD_P) & (lane < (j + 1) * HEAD_P)


def _over_heads(v, g):
    r = v.shape[0]
    out = jnp.zeros((r, GROUP_W), F32)
    for j in range(4):
        out = jnp.where(_head_mask(j, r), _col(v, 4 * g + j), out)
    return out


def _ssd_group_fwd(q, g, xs_g, bg, cg, ht_g, cst_ref, causal, dx_g):
    dtx_g, csx_g, totx_g = _over_heads(q["dt"], g), _over_heads(q["cs"], g), _over_heads(q["tot"], g)
    xdt = xs_g * dtx_g
    ex = jnp.exp(csx_g)
    cb = _dot_nt(cg, bg)
    yoff = _dot(cg, ht_g) * ex
    ydiag = jnp.zeros((CHUNK, GROUP_W), F32)
    for j in range(4):
        sc = cb * _decay_mat(q["cs"], cst_ref, 4 * g + j, causal)
        ydiag = jnp.where(_head_mask(j), _dot(sc, xdt), ydiag)
    y = ydiag + yoff + xs_g * dx_g
    dsx = jnp.exp(totx_g - csx_g)
    return y, dict(xdt=xdt, ex=ex, cb=cb, yoff=yoff, dsx=dsx, dtx=dtx_g, totx=totx_g)


def _gated_norm_fwd(y_g, z_g, w_g):
    sz = _sigmoid(z_g)
    silu = z_g * sz
    yf = y_g * silu
    rs = lax.rsqrt(jnp.mean(yf * yf, axis=1, keepdims=True) + RMS_EPS)
    yn = yf * rs
    return yn * w_g, (sz, silu, rs, yn)


def _ssd_fwd(xact, proj, ymix, bias_pad, alog_pad, dxp, normw, *, name, jobs=()):
    s = xact.shape[0]
    nc = s // CHUNK

    def body(xa_ref, dt_ref, z_ref, _ymix_ref, bias_ref, alp_ref, dx_ref, nw_ref, y_ref, hp_ref, ht, cst):
        @pl.when(pl.program_id(0) == 0)
        def _():
            ht[...] = jnp.zeros_like(ht)

        hp_ref[...] = ht[...]
        q = _ssd_prep(dt_ref[...], bias_ref[...], alp_ref[...])
        cst[...] = q["cs"].T
        causal = q["tril"] > 0.0
        for g in range(N_GROUP):
            sl = slice(g * GROUP_W, (g + 1) * GROUP_W)
            xs_g = xa_ref[:, sl]
            bg = xa_ref[:, SSD_W + g * N_STATE:SSD_W + (g + 1) * N_STATE]
            cg = xa_ref[:, SSD_W + N_GROUP * N_STATE + g * N_STATE:SSD_W + N_GROUP * N_STATE + (g + 1) * N_STATE]
            ht_g = ht[:, sl]
            y, f = _ssd_group_fwd(q, g, xs_g, bg, cg, ht_g, cst, causal, dx_ref[:, sl])
            out, _ = _gated_norm_fwd(y, z_ref[:, sl], nw_ref[:, sl])
            y_ref[:, sl] = out.astype(BF16)
            ht[:, sl] = jnp.exp(f["totx"]) * ht_g + _dot_tn(bg, f["xdt"] * f["dsx"])

    par = lambda w: pl.BlockSpec((1, w), lambda c: (0, 0))
    (ycat, hprev), jouts = _hosted(
        body, jobs, grid=(nc,),
        in_specs=[pl.BlockSpec((CHUNK, XBC), lambda c: (c, 0)),
                  pl.BlockSpec((CHUNK, LANE), lambda c: (c, COL_DT // LANE)),
                  pl.BlockSpec((CHUNK, SSD_W), lambda c: (c, COL_Z // SSD_W)),
                  ANY_SPEC, par(LANE), par(LANE), par(SSD_W), par(SSD_W)],
        out_specs=(pl.BlockSpec((CHUNK, SSD_W), lambda c: (c, LRU_W // SSD_W)),
                   pl.BlockSpec((None, N_STATE, SSD_W), lambda c: (c, 0, 0))),
        out_shape=(jax.ShapeDtypeStruct(ymix.shape, ymix.dtype), jax.ShapeDtypeStruct((nc, N_STATE, SSD_W), F32)),
        scratch_shapes=[pltpu.VMEM((N_STATE, SSD_W), F32), pltpu.VMEM((CHUNK, LANE), F32)],
        aliases={3: 0}, name=name, args=(xact, proj, proj, ymix, bias_pad, alog_pad, dxp, normw))
    return ((ycat, hprev), jouts) if jobs else (ycat, hprev)


def _ssd_bwd(xact, proj, dycat, hprev, bias_pad, alog_pad, dxp, normw, *, name, jobs=()):
    s = xact.shape[0]
    nc = s // CHUNK
    l = CHUNK

    def body(xa_ref, dt_ref, z_ref, dy_ref, hp_ref, bias_ref, alp_ref, dx_ref, nw_ref,
             dxa_ref, ddt_ref, dz_ref, dnw_ref, small_ref, dht, cst, accx, dcsx_s, ddtx_s):
        step = pl.program_id(0)

        @pl.when(step == 0)
        def _():
            dht[...] = jnp.zeros_like(dht)
            accx[...] = jnp.zeros_like(accx)
            dnw_ref[...] = jnp.zeros_like(dnw_ref)
            small_ref[...] = jnp.zeros_like(small_ref)

        dtr = dt_ref[...]
        q = _ssd_prep(dtr, bias_ref[...], alp_ref[...])
        cst[...] = q["cs"].T
        causal = q["tril"] > 0.0
        eye = _iota((l, l), 0) == _iota((l, l), 1)
        lane = _iota((l, LANE), 1)
        dcs_head = jnp.zeros((l, LANE), F32)
        for g in range(N_GROUP):
            sl = slice(g * GROUP_W, (g + 1) * GROUP_W)
            slb = slice(SSD_W + g * N_STATE, SSD_W + (g + 1) * N_STATE)
            slc = slice(SSD_W + N_GROUP * N_STATE + g * N_STATE, SSD_W + N_GROUP * N_STATE + (g + 1) * N_STATE)
            xs_g, bg, cg = xa_ref[:, sl], xa_ref[:, slb], xa_ref[:, slc]
            ht_g = hp_ref[:, sl]
            dxp_g = dx_ref[:, sl]
            y, f = _ssd_group_fwd(q, g, xs_g, bg, cg, ht_g, cst, causal, dxp_g)
            z_g, nw_g = z_ref[:, sl], nw_ref[:, sl]
            _o, (sz, silu, rs, yn) = _gated_norm_fwd(y, z_g, nw_g)
            dout = dy_ref[:, sl]
            dnw_ref[:, sl] += jnp.sum(dout * yn, axis=0, keepdims=True)
            dyn = dout * nw_g
            dyf = rs * (dyn - yn * jnp.mean(dyn * yn, axis=1, keepdims=True))
            dy = dyf * silu
            dz_ref[:, sl] = (dyf * y * sz * (1.0 + z_g * (1.0 - sz))).astype(BF16)
            accx[0:1, sl] += jnp.sum(dy * xs_g, axis=0, keepdims=True)
            dyo = dy * f["ex"]
            dcg = _dot_nt(dyo, ht_g)
            dht_prev = _dot_tn(cg, dyo)
            dcsx = dy * f["yoff"]
            xdt = f["xdt"]
            dxdt = jnp.zeros((l, GROUP_W), F32)
            dcb = jnp.zeros((l, l), F32)
            for j in range(4):
                h = 4 * g + j
                lm = _decay_mat(q["cs"], cst, h, causal)
                sc = f["cb"] * lm
                mask = _head_mask(j)
                ds_ = jnp.where(causal, _dot_nt(jnp.where(mask, dy, 0.0), xdt), 0.0)
                dxdt = jnp.where(mask, _dot_tn(sc, dy), dxdt)
                dcb = dcb + ds_ * lm
                m = ds_ * sc
                rsum = jnp.sum(m, axis=1, keepdims=True)
                csum = jnp.sum(m, axis=0, keepdims=True)
                csum_col = jnp.sum(jnp.where(eye, csum, 0.0), axis=1, keepdims=True)
                dcs_head = dcs_head + jnp.where(lane == h, rsum - csum_col, 0.0)
            dhn = dht[:, sl]
            etot = jnp.exp(f["totx"])
            dxd = _dot(bg, dhn)
            dbg = _dot_nt(xdt * f["dsx"], dhn)
            dxdt = dxdt + dxd * f["dsx"]
            qq = dxd * xdt * f["dsx"]
            dcsx = dcsx - qq
            dtot = jnp.sum(qq, axis=0, keepdims=True) + jnp.sum(dhn * ht_g, axis=0, keepdims=True) * etot
            dht[:, sl] = etot * dhn + dht_prev
            dcg = dcg + _dot(dcb, bg)
            dbg = dbg + _dot_tn(dcb, cg)
            dxa_ref[:, sl] = dxdt * f["dtx"] + dy * dxp_g
            dxa_ref[:, slb] = dbg
            dxa_ref[:, slc] = dcg
            dcsx_s[:, sl] = dcsx
            ddtx_s[:, sl] = dxdt * xs_g
            accx[2:3, sl] = dtot
        reduce = (jnp.right_shift(_iota((SSD_W, LANE), 0), 6) == _iota((SSD_W, LANE), 1)).astype(F32)
        triu = (_iota((l, l), 1) >= _iota((l, l), 0)).astype(F32)
        dtot = _dot01_r(accx[...], reduce)[2:3, :]
        da_head = _dot01(triu, dcs_head + _dot01_r(dcsx_s[...], reduce)) + dtot
        ddt = _dot01_r(ddtx_s[...], reduce) + da_head * q["a_head"]
        small_ref[1:2, :] += jnp.sum(da_head * q["dt"], axis=0, keepdims=True)
        ddtr = ddt * _sigmoid(dtr + bias_ref[...])
        ddt_ref[...] = ddtr.astype(BF16)
        small_ref[0:1, :] += jnp.sum(ddtr, axis=0, keepdims=True)

        @pl.when(step == nc - 1)
        def _():
            small_ref[1:2, :] = small_ref[1:2, :] * q["a_head"]
            small_ref[2:3, :] = _dot01_r(accx[...], reduce)[0:1, :]

    rev = lambda c: nc - 1 - c
    par = lambda w: pl.BlockSpec((1, w), lambda c: (0, 0))
    outs, jouts = _hosted(
        body, jobs, grid=(nc,),
        in_specs=[pl.BlockSpec((CHUNK, XBC), lambda c: (rev(c), 0)),
                  pl.BlockSpec((CHUNK, LANE), lambda c: (rev(c), COL_DT // LANE)),
                  pl.BlockSpec((CHUNK, SSD_W), lambda c: (rev(c), COL_Z // SSD_W)),
                  pl.BlockSpec((CHUNK, SSD_W), lambda c: (rev(c), 1)),
                  pl.BlockSpec((None, N_STATE, SSD_W), lambda c: (rev(c), 0, 0)),
                  par(LANE), par(LANE), par(SSD_W), par(SSD_W)],
        out_specs=(pl.BlockSpec((CHUNK, XBC), lambda c: (rev(c), 0)),
                   pl.BlockSpec((CHUNK, LANE), lambda c: (rev(c), 0)),
                   pl.BlockSpec((CHUNK, SSD_W), lambda c: (rev(c), 0)),
                   par(SSD_W), pl.BlockSpec((SUBLANE, LANE), lambda c: (0, 0))),
        out_shape=(jax.ShapeDtypeStruct((s, XBC), F32), jax.ShapeDtypeStruct((s, LANE), BF16),
                   jax.ShapeDtypeStruct((s, SSD_W), BF16), jax.ShapeDtypeStruct((1, SSD_W), F32),
                   jax.ShapeDtypeStruct((SUBLANE, LANE), F32)),
        scratch_shapes=[pltpu.VMEM((N_STATE, SSD_W), F32), pltpu.VMEM((CHUNK, LANE), F32),
                        pltpu.VMEM((SUBLANE, SSD_W), F32), pltpu.VMEM((CHUNK, SSD_W), F32),
                        pltpu.VMEM((CHUNK, SSD_W), F32)],
        name=name, args=(xact, proj, proj, dycat, hprev, bias_pad, alog_pad, dxp, normw))
    return (tuple(outs), jouts) if jobs else tuple(outs)


def _blockdiag(w):
    w2 = w.reshape(N_HEAD // 2, 2, HEAD_P, HEAD_P)
    z = jnp.zeros((N_HEAD // 2, HEAD_P, HEAD_P), w.dtype)
    top = jnp.concatenate([w2[:, 0], z], axis=2)
    bot = jnp.concatenate([z, w2[:, 1]], axis=2)
    return jnp.concatenate([top, bot], axis=1)


def _unblockdiag(wbd):
    a = wbd[:, :HEAD_P, :HEAD_P]
    b = wbd[:, HEAD_P:, HEAD_P:]
    return jnp.stack([a, b], axis=1).reshape(N_HEAD, HEAD_P, HEAD_P)


def _pad_rows8(w):
    return jnp.concatenate([w, jnp.zeros((SUBLANE - w.shape[0], w.shape[1]), w.dtype)], axis=0)


def _pad_lane(v):
    return jnp.concatenate([v, jnp.zeros((1, LANE - v.shape[1]), v.dtype)], axis=1)


class _NoExchange:
    def ride(self, host):
        return []

    def done(self, jobs, outs, w):
        pass

    def grad(self, name, val):
        pass

    def small(self, raw):
        pass

    def pairs_now(self):
        pass


def _local_step(x, p, tgt, w, hooks=_NoExchange()):
    cw_l = _pad_rows8(w["lru_conv_w"])
    cw_s = _pad_rows8(w["ssd_conv_w"])
    wa_bd = _blockdiag(w["lru_gate_a_w"])
    wx_bd = _blockdiag(w["lru_gate_x_w"])
    ba = w["lru_gate_a_b"].reshape(1, LRU_W)
    bx = w["lru_gate_x_b"].reshape(1, LRU_W)
    bias_pad = _pad_lane(w["ssd_dt_bias"])
    alog_pad = _pad_lane(w["ssd_a_log"])
    dxp = jnp.repeat(w["ssd_d"], HEAD_P, axis=1)

    def host(fn, *a, name, **k):
        jobs = hooks.ride(name)
        res = fn(*a, name=name, jobs=jobs, **k)
        if jobs:
            res, jouts = res
            hooks.done(jobs, jouts, w)
        return res

    def grad(n, val):
        g[n] = val
        hooks.grad(n, val)

    xb = x.astype(BF16)
    proj = host(_mm, xb, w["w_in_t"], "nt", tm=1024, tn=512, name="in_proj")
    ymix, h_lru = host(_lru_fwd, proj, cw_l, w["lru_conv_b"], wa_bd, ba, wx_bd, bx, w["lru_a_param"], name="lru_fwd")
    xact = host(_conv_silu_fwd, proj, cw_s, w["ssd_conv_b"], col0=COL_XBC, width=XBC, ct=256, name="ssd_conv_fwd")
    ycat, hprev = host(_ssd_fwd, xact, proj, ymix, bias_pad, alog_pad, dxp, w["ssd_norm_w"], name="ssd_fwd")
    mix, x1, x1b = _mm_ln(ycat, w["w_out"], x, w["ln1_g"], w["ln1_b"], tm=512, name="out_proj")
    pre = _mm(x1b, w["w_ff1"], "nn", tm=1024, tn=512, out_dtype=BF16, name="ff1")
    ff, x2, x2b = _mm_ln(pre, w["w_ff2"], x1, w["ln2_g"], w["ln2_b"], tm=512, a_fn=_relu2, name="ff2")
    gpre = _mm(x2b, w["w_ple_gate"], "nn", tm=1024, tn=1024, name="ple_gate")
    ple = _mm(p, w["w_ple"], "nn", tm=1024, tn=1024, name="ple_proj")
    loss, dgpre, dple, dt3, dg3, db3 = _head(x2, gpre, ple, w["ln3_g"], w["ln3_b"], tgt, name="head")

    g = {}
    g["ln3_g"], g["ln3_b"] = dg3, db3
    grad("w_ple_gate", _mm(x2b, dgpre, "tn", tm=512, tn=1024, out_dtype=BF16, name="d_w_ple_gate"))
    grad("w_ple", _mm(p, dple, "tn", tm=256, tn=512, dest_major=True, out_dtype=BF16, name="d_w_ple"))
    dt2, dt2b, g["ln2_g"], g["ln2_b"] = host(_mm_ln_bwd, dgpre, w["w_ple_gate"], x1, ff, w["ln2_g"], dt3, ALPHA,
                                             tm=512, name="d_x2")
    grad("w_ff2", host(_mm, pre, dt2b, "tn", tm=512, tn=1024, a_fn=_relu2, out_dtype=BF16, name="d_w_ff2"))
    dpre = host(_mm, dt2b, w["w_ff2"], "nt", tm=1024, tn=512, extra=pre, out_dtype=BF16,
                epi=lambda acc, pv: acc * 2.0 * jnp.maximum(pv.astype(F32), 0.0), name="d_pre")
    grad("w_ff1", host(_mm, x1b, dpre, "tn", tm=1024, tn=512, dest_major=True, out_dtype=BF16, name="d_w_ff1"))
    dt1, dt1b, g["ln1_g"], g["ln1_b"] = host(_mm_ln_bwd, dpre, w["w_ff1"], x, mix, w["ln1_g"], dt2, ALPHA,
                                             tm=256, name="d_x1")
    grad("w_out", host(_mm, ycat, dt1b, "tn", tm=512, tn=1024, out_dtype=BF16, name="d_w_out"))
    dycat = host(_mm, dt1b, w["w_out"], "nt", tm=1024, tn=1024, name="d_ycat")
    dxl, dgl, dcwb_l, dwa, dwx = host(_lru_bwd, proj, dycat, h_lru, cw_l, w["lru_conv_b"], wa_bd, ba, wx_bd, bx,
                                      w["lru_a_param"], name="lru_bwd")
    g["lru_gate_a_w"] = _unblockdiag(dwa)
    g["lru_gate_x_w"] = _unblockdiag(dwx)
    raw = dict(lru=dcwb_l, gate_a=g["lru_gate_a_w"].reshape(N_HEAD * HEAD_P, HEAD_P).astype(BF16),
               gate_x=g["lru_gate_x_w"].reshape(N_HEAD * HEAD_P, HEAD_P).astype(BF16))
    hooks.small(raw)
    dxact, ddt, dz, g["ssd_norm_w"], small = host(_ssd_bwd, xact, proj, dycat, hprev, bias_pad, alog_pad, dxp,
                                                   w["ssd_norm_w"], name="ssd_bwd")
    dxbc, dcwb_s = host(_conv_silu_bwd, proj, dxact, cw_s, w["ssd_conv_b"], col0=COL_XBC, width=XBC, ct=256,
                        name="ssd_conv_bwd")
    pieces, offsets = [dxl, dgl, dz, dxbc, ddt], [0, COL_G, COL_Z, COL_XBC, COL_DT]

    g["lru_conv_w"] = dcwb_l[0:4]
    g["lru_conv_b"] = dcwb_l[4:5]
    g["lru_gate_a_b"] = dcwb_l[5:6]
    g["lru_gate_x_b"] = dcwb_l[6:7]
    g["lru_a_param"] = dcwb_l[7:8]
    g["ssd_conv_w"] = dcwb_s[0:4]
    g["ssd_conv_b"] = dcwb_s[4:5]
    g["ssd_dt_bias"] = small[0:1, :N_HEAD]
    g["ssd_a_log"] = small[1:2, :N_HEAD]
    g["ssd_d"] = small[2:3, :N_HEAD]
    rows = jnp.concatenate([g[n] for n in ("ssd_norm_w", "ln1_g", "ln1_b", "ln2_g", "ln2_b", "ln3_g", "ln3_b")]
                           + [jnp.broadcast_to(loss[:, 0:1], (1, D_MODEL))], axis=0)
    late = dict(ssd=dcwb_s, heads=small, rows=rows)
    hooks.small(late)
    raw.update(late)
    dwt = None
    for q, (pc, off) in enumerate(zip(pieces, offsets)):
        dwt = host(_mm, pc, xb, "tn", tm=512, tn=1024, out_dtype=BF16, into=(dwt, off, D_IN),
                   name="d_w_in_%d" % q)
    grad("w_in", dwt)
    hooks.pairs_now()
    grad_x = host(_mm_pieces, pieces, offsets, w["w_in_t"], tm=256, extra=dt1, epi=lambda acc, e: acc + ALPHA * e,
                  name="d_x")
    return loss[0, 0], grad_x, g, raw


ANY_SPEC = pl.BlockSpec(memory_space=pl.ANY)


def _mesh_pos():
    return lax.axis_index("x"), lax.axis_index("y"), lax.axis_index("c")


def _remote(src, dst, send, recv, k, to):
    return pltpu.make_async_remote_copy(src_ref=src, dst_ref=dst, send_sem=send.at[k], recv_sem=recv.at[k],
                                        device_id=to, device_id_type=MESH_T)


class _Job:
    N_SEM = 7

    def __init__(self, kind, inp):
        self.kind, self.inp = kind, inp
        shape = {"gather": (N_DEV,) + inp.shape, "pair": (4,) + inp.shape[1:], "chip": inp.shape}[kind]
        self.out = jax.ShapeDtypeStruct(shape, inp.dtype)

    def _places(self):
        x, y, c = _mesh_pos()
        return (x, y, c), (x, y, 1 - c), [(1 - x, y), (x, 1 - y), (1 - x, 1 - y)]

    def start(self, inp, out, send, recv, loc):
        me, sibling, chips = self._places()
        x, y, c = me
        if self.kind == "gather":
            mine = out.at[4 * x + 2 * y + c]
            pltpu.make_async_copy(inp, mine, loc.at[0]).start()
            _remote(inp, mine, send, recv, 0, sibling).start()
            for j, chip in enumerate(chips):
                _remote(inp, mine, send, recv, 1 + j, (*chip, c)).start()
        elif self.kind == "pair":
            for k in range(4):
                _remote(inp.at[2 * k + (1 - c)], out.at[k], send, recv, k, sibling).start()
        else:
            kme = 2 * x + y
            pltpu.make_async_copy(inp.at[kme], out.at[kme], loc.at[0]).start()
            for j, (tx, ty) in enumerate(chips):
                _remote(inp.at[2 * tx + ty], out.at[kme], send, recv, j, (tx, ty, c)).start()

    def mid(self, inp, out, send, recv, loc):
        if self.kind != "gather":
            return
        me, sibling, chips = self._places()
        c = me[2]
        for j, chip in enumerate(chips):
            landed = out.at[4 * chip[0] + 2 * chip[1] + c]
            _remote(landed, landed, send, recv, 1 + j, me).wait_recv()
            _remote(landed, landed, send, recv, 4 + j, sibling).start()

    def finish(self, inp, out, send, recv, loc):
        me, sibling, chips = self._places()
        x, y, c = me
        if self.kind == "gather":
            blk = lambda px, py, pc: out.at[4 * px + 2 * py + pc]
            mine = blk(*me)
            _remote(inp, blk(*sibling), send, recv, 0, me).wait_recv()
            for j, chip in enumerate(chips):
                _remote(inp, blk(*chip, 1 - c), send, recv, 4 + j, me).wait_recv()
            for k in range(7):
                _remote(inp, mine, send, recv, k, sibling).wait_send()
            pltpu.make_async_copy(inp, mine, loc.at[0]).wait()
        elif self.kind == "pair":
            for k in range(4):
                _remote(inp.at[2 * k + (1 - c)], out.at[k], send, recv, k, sibling).wait()
        else:
            kme = 2 * x + y
            for j, (tx, ty) in enumerate(chips):
                _remote(inp.at[kme], out.at[2 * tx + ty], send, recv, j, (tx, ty, c)).wait_recv()
            for j, (tx, ty) in enumerate(chips):
                _remote(inp.at[2 * tx + ty], out.at[kme], send, recv, j, (tx, ty, c)).wait_send()
            pltpu.make_async_copy(inp.at[kme], out.at[kme], loc.at[0]).wait()


def _job_scratch(jobs):
    sem = pltpu.SemaphoreType.DMA
    return [s for _ in jobs for s in (sem((_Job.N_SEM,)), sem((_Job.N_SEM,)), sem((1,)))]


def _run_jobs(jobs, method, jins, jouts, jsems):
    for q, job in enumerate(jobs):
        getattr(job, method)(jins[q], jouts[q], *jsems[3 * q:3 * q + 3])


def _exchange(jobs, *, name):
    n = len(jobs)

    def body(*refs):
        jins, jouts, jsems = refs[:n], refs[n:2 * n], refs[2 * n:]
        _run_jobs(jobs, "start", jins, jouts, jsems)
        _run_jobs(jobs, "mid", jins, jouts, jsems)
        _run_jobs(jobs, "finish", jins, jouts, jsems)

    return _pcall(body, in_specs=[ANY_SPEC] * n, out_specs=[ANY_SPEC] * n, out_shape=[j.out for j in jobs],
                  scratch_shapes=_job_scratch(jobs), name=name)(*[j.inp for j in jobs])


def _hosted(body, jobs, *, grid, in_specs, out_specs, out_shape, args, name, scratch_shapes=(), aliases=None):
    in_specs, out_specs, out_shape = list(in_specs), list(out_specs), list(out_shape)
    scratch_shapes = list(scratch_shapes)
    n_in, n_out, n_scr, nj = len(in_specs), len(out_specs), len(scratch_shapes), len(jobs)
    sem = ("arbitrary",) * len(grid)
    kw = dict(input_output_aliases=aliases) if aliases else {}
    if not jobs:
        res = _pcall(body, grid=grid, in_specs=in_specs, out_specs=out_specs, out_shape=out_shape,
                     scratch_shapes=scratch_shapes, name=name, compiler_params=_cparams(sem), **kw)(*args)
        return list(res), []

    def full(*refs):
        ins, jins = refs[:n_in], refs[n_in:n_in + nj]
        o0 = n_in + nj
        outs, jouts = refs[o0:o0 + n_out], refs[o0 + n_out:o0 + n_out + nj]
        s0 = o0 + n_out + nj
        scr, jsems = refs[s0:s0 + n_scr], refs[s0 + n_scr:]
        step = pl.program_id(0)
        for ax in range(1, len(grid)):
            step = step * grid[ax] + pl.program_id(ax)
        total = math.prod(grid)

        @pl.when(step == 0)
        def _():
            _run_jobs(jobs, "start", jins, jouts, jsems)

        body(*ins, *outs, *scr)

        @pl.when(step == total - 1)
        def _():
            _run_jobs(jobs, "mid", jins, jouts, jsems)
            _run_jobs(jobs, "finish", jins, jouts, jsems)

    res = _pcall(full, grid=grid, in_specs=in_specs + [ANY_SPEC] * nj, out_specs=out_specs + [ANY_SPEC] * nj,
                 out_shape=out_shape + [j.out for j in jobs], scratch_shapes=scratch_shapes + _job_scratch(jobs),
                 name=name, compiler_params=_cparams(sem), **kw)(*args, *[j.inp for j in jobs])
    return list(res[:n_out]), list(res[n_out:])


def _pair_add(g8, r4, cidx, *, name):
    _, r, c = g8.shape
    tr = ROW_TILE if r % ROW_TILE == 0 else r

    def body(c_ref, g_ref, r_ref, o_ref):
        o_ref[...] = (g_ref[...].astype(F32) + r_ref[...].astype(F32)).astype(BF16)

    return _pcall(
        body,
        grid_spec=pltpu.PrefetchScalarGridSpec(
            num_scalar_prefetch=1, grid=(4, r // tr),
            in_specs=[pl.BlockSpec((None, tr, c), lambda k, i, cr: (2 * k + cr[0], i, 0)),
                      pl.BlockSpec((None, tr, c), lambda k, i, cr: (k, i, 0))],
            out_specs=pl.BlockSpec((None, tr, c), lambda k, i, cr: (k, i, 0))),
        out_shape=jax.ShapeDtypeStruct((4, r, c), BF16), name=name,
        compiler_params=_cparams(("parallel", "parallel")))(cidx, g8, r4)


def _adam_update(g, w_ref, m_ref, v_ref, g_ref, d_ref, mo_ref, vo_ref):
    c1 = 1.0 - ADAM_B1 ** ADAM_STEP
    c2 = 1.0 - ADAM_B2 ** ADAM_STEP
    m2 = ADAM_B1 * m_ref[...] + (1.0 - ADAM_B1) * g
    v2 = ADAM_B2 * v_ref[...] + (1.0 - ADAM_B2) * (g * g)
    g_ref[...] = g
    mo_ref[...] = m2
    vo_ref[...] = v2
    d_ref[...] = -ADAM_LR * ((m2 / c1) / (jnp.sqrt(v2 / c2) + ADAM_EPS) + ADAM_WD * w_ref[...])


def _adamw_rows(srcs, items, own_cols, me1, *, name):
    ns, ni, no = len(srcs), len(items), len(own_cols)
    full = lambda a: pl.BlockSpec(a.shape, lambda i, me: (0,) * a.ndim)
    in_specs = [full(a) for a in srcs]
    args = list(srcs)
    for (si, _r0, w, _m, _v) in own_cols:
        a = srcs[si]
        in_specs.append(pl.BlockSpec((N_DEV, a.shape[1], w.shape[1]), lambda i, me: (0, 0, me[0])))
        args.append(a)
    out_specs, out_shape = [], []
    for (_si, _r0, w, m, v) in list(items) + list(own_cols):
        in_specs += [full(w)] * 3
        args += [w, m, v]
        out_specs += [full(w)] * 4
        out_shape += [jax.ShapeDtypeStruct(w.shape, F32)] * 4

    def body(me_ref, *refs):
        src_refs, own_refs = refs[:ns], refs[ns:ns + no]
        wmv = refs[ns + no:ns + no + 3 * (ni + no)]
        outs = refs[ns + no + 3 * (ni + no):]
        for q, (si, r0, w, _m, _v) in enumerate(list(items) + list(own_cols)):
            nr, cw = w.shape
            gref = src_refs[si] if q < ni else own_refs[q - ni]
            g = gref[0, r0:r0 + nr, 0:cw]
            for d in range(1, N_DEV):
                g = g + gref[d, r0:r0 + nr, 0:cw]
            _adam_update(g, *wmv[3 * q:3 * q + 3], *outs[4 * q:4 * q + 4])

    res = _pcall(
        body,
        grid_spec=pltpu.PrefetchScalarGridSpec(num_scalar_prefetch=1, grid=(1,), in_specs=in_specs, out_specs=out_specs),
        out_shape=out_shape, name=name, compiler_params=_cparams(("arbitrary",)))(me1, *args)
    return [tuple(res[4 * q:4 * q + 4]) for q in range(ni + no)]


def _adamw(gsrc, w, m, v, *, name):
    k, r, c = gsrc.shape
    tr = ROW_TILE if r % ROW_TILE == 0 else r

    def body(gs_ref, w_ref, m_ref, v_ref, g_ref, d_ref, mo_ref, vo_ref):
        g = gs_ref[0].astype(F32)
        for q in range(1, k):
            g = g + gs_ref[q].astype(F32)
        _adam_update(g, w_ref, m_ref, v_ref, g_ref, d_ref, mo_ref, vo_ref)

    tc = c
    if tr == r and r > ROW_TILE and c % 256 == 0:
        tc = 256
    blk = pl.BlockSpec((tr, tc), lambda i, j: (i, j))
    sd = jax.ShapeDtypeStruct((r, c), F32)
    return _pcall(body, grid=(r // tr, c // tc),
                  in_specs=[pl.BlockSpec((k, tr, tc), lambda i, j: (0, i, j)), blk, blk, blk],
                  out_specs=(blk, blk, blk, blk), out_shape=(sd, sd, sd, sd), name=name,
                  compiler_params=_cparams(("parallel", "parallel")))(gsrc, w, m, v)


WEIGHTS = ['w_in', 'lru_conv_w', 'lru_conv_b', 'lru_gate_a_w', 'lru_gate_a_b', 'lru_gate_x_w', 'lru_gate_x_b',
           'lru_a_param', 'ssd_conv_w', 'ssd_conv_b', 'ssd_dt_bias', 'ssd_a_log', 'ssd_d', 'ssd_norm_w', 'w_out',
           'ln1_g', 'ln1_b', 'w_ff1', 'w_ff2', 'ln2_g', 'ln2_b', 'w_ple_gate', 'w_ple', 'ln3_g', 'ln3_b']
BIG = ['w_in', 'w_out', 'w_ff1', 'w_ff2', 'w_ple_gate', 'w_ple']
COL_SHARDED = ('w_ff1', 'w_ple')
CONV = ['lru_conv_w', 'ssd_conv_w']
REPL = [n for n in WEIGHTS if n not in BIG and n not in CONV]
CONV_CH = {'lru_conv_w': LRU_W, 'ssd_conv_w': XBC}


def _to_dest_major(name, gfull):
    if name in COL_SHARDED:
        r, cfull = gfull.shape
        return gfull.reshape(r, N_DEV, cfull // N_DEV).transpose(1, 0, 2)
    rfull, cdim = gfull.shape
    return gfull.reshape(N_DEV, rfull // N_DEV, cdim)


def _full_weight(name, gathered):
    if name in COL_SHARDED:
        _, r, cs = gathered.shape
        full = gathered.transpose(1, 0, 2).reshape(r, N_DEV * cs)
    else:
        _, rs, cdim = gathered.shape
        full = gathered.reshape(N_DEV * rs, cdim)
    if name == 'w_in':
        full = lax.dynamic_update_slice(jnp.zeros((D_IN_PAD, D_MODEL), full.dtype), full, (0, 0))
    return full


SMALL_SRC = ("lru", "ssd", "heads", "rows", "gate_a", "gate_x")
AG_HOSTS = {"in_proj": ("w_ff1",), "lru_fwd": ("w_ff2",), "ssd_conv_fwd": ("w_ple_gate", "w_ple"), "ssd_fwd": ("w_out",)}
PAIR_HOSTS = ("d_x2", "d_pre", "d_x1", "d_ycat")
CHIP_HOSTS = {"lru_bwd": ("w_ple_gate", "w_ple", "w_ff2"), "ssd_bwd": ("w_ff1",), "ssd_conv_bwd": ("w_out",),
              "d_x": ("w_in",)}
SMALL_HOSTS = {"ssd_bwd": ("lru", "gate_a", "gate_x"), "d_w_in_3": ("ssd", "heads", "rows")}


class _Schedule:
    def __init__(self, shards, cidx):
        self.shards, self.cidx = shards, cidx
        self.pair, self.chip, self.small_jobs = [], [], []
        self.dest, self.summed, self.gathered_small = {}, {}, {}
        self.tags = []

    def ride(self, host):
        tags = []
        if host in AG_HOSTS:
            tags = [("weight", n, self.shards[n]) for n in AG_HOSTS[host]]
        elif host in PAIR_HOSTS or host in CHIP_HOSTS or host == "flush":
            tags = [("pair", n, a) for n, a in self.pair]
            self.pair = []
            if host not in PAIR_HOSTS:
                take = [t for t in self.chip if host == "flush" or t[0] in CHIP_HOSTS[host]]
                tags += [("chip", n, a) for n, a in take]
                self.chip = [t for t in self.chip if not any(t is u for u in take)]
        if host in SMALL_HOSTS:
            tags += [("small", n, a) for n, a in self.small_jobs if n in SMALL_HOSTS[host]]
            self.small_jobs = [t for t in self.small_jobs if t[0] not in SMALL_HOSTS[host]]
        self.tags = tags
        return [_Job({"weight": "gather", "small": "gather"}.get(kind, kind), a) for kind, _n, a in tags]

    def done(self, jobs, outs, w):
        for (kind, n, _a), o in zip(self.tags, outs):
            if kind == "weight":
                w[n] = _full_weight(n, o)
            elif kind == "small":
                self.gathered_small[n] = o
            elif kind == "pair":
                self.chip.append((n, _pair_add(self.dest[n], o, self.cidx, name="rs_pair_add_" + n)))
            else:
                self.summed[n] = o

    def grad(self, name, val):
        self.dest[name] = val if val.ndim == 3 else _to_dest_major(name, val)
        self.pair.append((name, self.dest[name]))

    def small(self, raw):
        self.small_jobs += list(raw.items())

    def pairs_now(self):
        tags = [("pair", n, a) for n, a in self.pair]
        self.pair, self.tags = [], tags
        jobs = [_Job("pair", a) for _k, _n, a in tags]
        self.done(jobs, _exchange(jobs, name="rs_pairs_now"), None)

    def flush(self):
        step = 0
        while self.pair or self.chip:
            jobs = self.ride("flush")
            self.done(jobs, _exchange(jobs, name="rs_flush_%d" % step), None)
            step += 1


def kernel(x, p, w_in, lru_conv_w, lru_conv_b, lru_gate_a_w, lru_gate_a_b, lru_gate_x_w, lru_gate_x_b, lru_a_param, ssd_conv_w, ssd_conv_b, ssd_dt_bias, ssd_a_log, ssd_d, ssd_norm_w, w_out, ln1_g, ln1_b, w_ff1, w_ff2, ln2_g, ln2_b, w_ple_gate, w_ple, ln3_g, ln3_b, loss_target, m_w_in, m_lru_conv_w, m_lru_conv_b, m_lru_gate_a_w, m_lru_gate_a_b, m_lru_gate_x_w, m_lru_gate_x_b, m_lru_a_param, m_ssd_conv_w, m_ssd_conv_b, m_ssd_dt_bias, m_ssd_a_log, m_ssd_d, m_ssd_norm_w, m_w_out, m_ln1_g, m_ln1_b, m_w_ff1, m_w_ff2, m_ln2_g, m_ln2_b, m_w_ple_gate, m_w_ple, m_ln3_g, m_ln3_b, v_w_in, v_lru_conv_w, v_lru_conv_b, v_lru_gate_a_w, v_lru_gate_a_b, v_lru_gate_x_w, v_lru_gate_x_b, v_lru_a_param, v_ssd_conv_w, v_ssd_conv_b, v_ssd_dt_bias, v_ssd_a_log, v_ssd_d, v_ssd_norm_w, v_w_out, v_ln1_g, v_ln1_b, v_w_ff1, v_w_ff2, v_ln2_g, v_ln2_b, v_w_ple_gate, v_w_ple, v_ln3_g, v_ln3_b):
    given = dict(locals())
    def local(a, n):
        return jnp.swapaxes(a[0], 0, 1) if n == 'w_in' else a[0]

    wsh = {n: local(given[n], n) for n in WEIGHTS}
    msh = {n: local(given["m_" + n], n) for n in WEIGHTS}
    vsh = {n: local(given["v_" + n], n) for n in WEIGHTS}
    xi, yi, ci = _mesh_pos()
    me = 4 * xi + 2 * yi + ci

    shards = {n: wsh[n].astype(BF16) for n in BIG}
    conv_pack = jnp.concatenate([_pad_rows8(wsh[n]) for n in CONV], axis=1)
    g_in, gconv = _exchange([_Job("gather", shards['w_in']), _Job("gather", conv_pack)], name="ag_first")
    full = {'w_in_t': _full_weight('w_in', g_in)}
    c0 = 0
    for n in CONV:
        cw = CONV_CH[n] // N_DEV
        full[n] = gconv[:, :4, c0:c0 + cw].transpose(1, 0, 2).reshape(4, CONV_CH[n])
        c0 += cw
    for n in REPL:
        full[n] = given[n] if given[n].ndim == 2 else wsh[n]

    sched = _Schedule(shards, jnp.reshape(ci, (1,)).astype(jnp.int32))
    loss_local, grad_x, g, raw = _local_step(x[0], p[0, 0], loss_target[0], full, sched)
    sched.flush()
    summed, gat = sched.summed, sched.gathered_small
    loss = gat["rows"][0, 7, 0]
    for d in range(1, N_DEV):
        loss = loss + gat["rows"][d, 7, 0]

    outs = {}
    for n in BIG:
        outs[n] = _adamw(summed[n], wsh[n], msh[n], vsh[n], name="adamw_" + n)
    for n, k in (("lru_gate_a_w", "gate_a"), ("lru_gate_x_w", "gate_x")):
        flat = lambda a: a.reshape(N_HEAD * HEAD_P, HEAD_P)
        res = _adamw(gat[k], flat(wsh[n]), flat(msh[n]), flat(vsh[n]), name="adamw_" + n)
        outs[n] = tuple(r.reshape(N_HEAD, HEAD_P, HEAD_P) for r in res)
    row_items = [("lru_conv_b", 0, 4), ("lru_gate_a_b", 0, 5), ("lru_gate_x_b", 0, 6), ("lru_a_param", 0, 7),
                 ("ssd_conv_b", 1, 4), ("ssd_dt_bias", 2, 0), ("ssd_a_log", 2, 1), ("ssd_d", 2, 2),
                 ("ssd_norm_w", 3, 0), ("ln1_g", 3, 1), ("ln1_b", 3, 2), ("ln2_g", 3, 3), ("ln2_b", 3, 4),
                 ("ln3_g", 3, 5), ("ln3_b", 3, 6)]
    vec = lambda a: a.reshape(1, -1)
    items = [(si, r0, vec(given[n]), vec(given["m_" + n]), vec(given["v_" + n])) for n, si, r0 in row_items]
    own = [(si, 0, wsh[n], msh[n], vsh[n]) for n, si in (("lru_conv_w", 0), ("ssd_conv_w", 1))]
    me1 = jnp.reshape(me, (1,)).astype(jnp.int32)
    res = _adamw_rows([gat[k] for k in SMALL_SRC[:4]], items, own, me1, name="adamw_small")
    for (n, _si, _r0), r4 in zip(row_items, res[:len(row_items)]):
        outs[n] = r4
    for n, r4 in zip(CONV, res[len(row_items):]):
        outs[n] = r4

    def fin(n, k):
        a = jnp.swapaxes(outs[n][k], 0, 1) if n == 'w_in' else outs[n][k]
        return a.reshape(given[n].shape)

    return (loss, grad_x[None],
            *[fin(n, 0) for n in WEIGHTS], *[fin(n, 1) for n in WEIGHTS],
            *[fin(n, 2) for n in WEIGHTS], *[fin(n, 3) for n in WEIGHTS])
```

```python
import math

import jax
import jax.numpy as jnp
from jax import lax
from jax.experimental import pallas as pl
from jax.experimental.pallas import tpu as pltpu

F32 = jnp.float32
BF16 = jnp.bfloat16
HI = lax.Precision.HIGHEST

N_DEV = 8
D_MODEL = 1024
LRU_W = 1024
SSD_W = 1024
XBC = 2048
N_HEAD = 16
HEAD_P = 64
N_GROUP = 4
GROUP_W = 256
N_STATE = 128
CHUNK = 128
D_FF = 4096
PLE_DIM = 256
D_IN = 5136
D_IN_PAD = 5632
COL_G = 1024
COL_Z = 2048
COL_XBC = 3072
COL_DT = 5120
LRU_C = 8.0
ALPHA = 2.0 ** 0.25
LN_EPS = 1e-5
RMS_EPS = 1e-5
ADAM_LR = 0.001
ADAM_B1 = 0.9
ADAM_B2 = 0.999
ADAM_EPS = 1e-08
ADAM_WD = 0.01
ADAM_STEP = 10
GELU_C = math.sqrt(2.0 / math.pi)
LANE = 128
SUBLANE = 8
VMEM_LIMIT = 48 * 1024 * 1024
MESH_T = pl.DeviceIdType.MESH
NEG_BIG = -1e30


def _pcall(body, **kw):
    return pl.pallas_call(body, **kw)


def _cparams(sem):
    return pltpu.CompilerParams(dimension_semantics=sem, vmem_limit_bytes=VMEM_LIMIT)


def _dot(a, b):
    return jnp.dot(a.astype(BF16), b.astype(BF16), preferred_element_type=F32)


def _dot_nt(a, b):
    return lax.dot_general(a.astype(BF16), b.astype(BF16), (((1,), (1,)), ((), ())), preferred_element_type=F32)


def _dot_tn(a, b):
    return lax.dot_general(a.astype(BF16), b.astype(BF16), (((0,), (0,)), ((), ())), preferred_element_type=F32)


def _dotx(a, b):
    return jnp.dot(a, b, precision=HI, preferred_element_type=F32)


def _sigmoid(x):
    return jax.nn.sigmoid(x)


def _softplus(v):
    return jnp.maximum(v, 0.0) + jnp.log1p(jnp.exp(-jnp.abs(v)))


def _gelu(x):
    th = jnp.tanh(GELU_C * (x + 0.044715 * x * x * x))
    return 0.5 * x * (1.0 + th), th


def _gelu_grad(x, th):
    return 0.5 * (1.0 + th) + 0.5 * x * (1.0 - th * th) * GELU_C * (1.0 + 3.0 * 0.044715 * x * x)


def _iota(shape, dim):
    return lax.broadcasted_iota(jnp.int32, shape, dim)


def _mm(a, b, mode, *, tm, tn, name, a_fn=None, extra=None, epi=None, out_dtype=F32, dest_major=False, into=None,
        jobs=()):
    m = a.shape[1] if mode == "tn" else a.shape[0]
    n = b.shape[0] if mode == "nt" else b.shape[1]
    tm, tn = min(tm, m), min(tn, n)
    if dest_major:
        tn = n // N_DEV
    if mode == "nn":
        m, k = a.shape
        _, n = b.shape
        a_spec = pl.BlockSpec((tm, k), lambda i, j: (i, 0))
        b_spec = pl.BlockSpec((k, tn), lambda i, j: (0, j))
        dims = ((1,), (0,))
    elif mode == "nt":
        m, k = a.shape
        n, _ = b.shape
        a_spec = pl.BlockSpec((tm, k), lambda i, j: (i, 0))
        b_spec = pl.BlockSpec((tn, k), lambda i, j: (j, 0))
        dims = ((1,), (1,))
    else:
        k, m = a.shape
        _, n = b.shape
        a_spec = pl.BlockSpec((k, tm), lambda i, j: (0, i))
        b_spec = pl.BlockSpec((k, tn), lambda i, j: (0, j))
        dims = ((0,), (0,))
    assert m % tm == 0 and n % tn == 0, (name, m, n, tm, tn)
    o_spec = pl.BlockSpec((tm, tn), lambda i, j: (i, j))
    in_specs = [a_spec, b_spec]
    args = [a, b]
    if extra is not None:
        in_specs.append(o_spec)
        args.append(extra)

    def body(*refs):
        a_ref, b_ref, o_ref = refs[0], refs[1], refs[-1]
        av = a_ref[...]
        if a_fn is not None:
            av = a_fn(av)
        acc = lax.dot_general(av.astype(BF16), b_ref[...].astype(BF16), (dims, ((), ())), preferred_element_type=F32)
        if epi is not None:
            acc = epi(acc, refs[2][...])
        o_ref[...] = acc.astype(out_dtype)

    out_shape = jax.ShapeDtypeStruct((m, n), out_dtype)
    aliases = None
    if dest_major:
        assert extra is None
        o_spec = pl.BlockSpec((None, tm, tn), lambda i, j: (j, i, 0))
        out_shape = jax.ShapeDtypeStruct((N_DEV, m, tn), out_dtype)
    if into is not None:
        buf, row0, total = into
        assert extra is None and row0 % tm == 0
        o_spec = pl.BlockSpec((tm, tn), lambda i, j: (row0 // tm + i, j))
        out_shape = jax.ShapeDtypeStruct((total, n), out_dtype)
        if buf is not None:
            in_specs.append(ANY_SPEC)
            args.append(buf)
            aliases = {len(args) - 1: 0}
    (out,), jouts = _hosted(body, jobs, grid=(m // tm, n // tn), in_specs=in_specs, out_specs=[o_spec],
                            out_shape=[out_shape], args=args, name=name, aliases=aliases)
    return (out, jouts) if jobs else out


def _mm_pieces(pieces, offsets, b, *, tm, name, extra, epi, jobs=()):
    m = pieces[0].shape[0]
    kb, n = b.shape
    tm = min(tm, m)
    row = lambda wdt: pl.BlockSpec((tm, wdt), lambda i: (i, 0))
    in_specs = [row(pc.shape[1]) for pc in pieces] + [pl.BlockSpec((kb, n), lambda i: (0, 0)), row(n)]
    np_ = len(pieces)

    def body(*refs):
        b_ref, e_ref, o_ref = refs[np_], refs[np_ + 1], refs[np_ + 2]
        acc = jnp.zeros((tm, n), F32)
        for q in range(np_):
            kq = pieces[q].shape[1]
            acc = acc + jnp.dot(refs[q][...].astype(BF16), b_ref[offsets[q]:offsets[q] + kq, :].astype(BF16),
                                preferred_element_type=F32)
        o_ref[...] = epi(acc, e_ref[...])

    (out,), jouts = _hosted(body, jobs, grid=(m // tm,), in_specs=in_specs, out_specs=[row(n)],
                            out_shape=[jax.ShapeDtypeStruct((m, n), F32)], args=list(pieces) + [b, extra], name=name)
    return (out, jouts) if jobs else out


def _relu2(v):
    r = jnp.maximum(v, 0.0)
    return r * r


ROW_TILE = 256


def _ln_stats(t):
    mu = jnp.mean(t, axis=-1, keepdims=True)
    xc = t - mu
    var = jnp.mean(xc * xc, axis=-1, keepdims=True)
    rstd = lax.rsqrt(var + LN_EPS)
    return xc * rstd, rstd


def _ln_bwd_rows(dy, xhat, rstd, g):
    dxh = dy * g
    m1 = jnp.mean(dxh, axis=-1, keepdims=True)
    m2 = jnp.mean(dxh * xhat, axis=-1, keepdims=True)
    return rstd * (dxh - m1 - xhat * m2)


def _mm_ln(a, b, res, g, beta, *, tm, name, a_fn=None):
    m, k = a.shape
    d = b.shape[1]
    tm = min(tm, m)
    row = pl.BlockSpec((tm, d), lambda i: (i, 0))
    par = pl.BlockSpec((1, d), lambda i: (0, 0))

    def body(a_ref, b_ref, r_ref, g_ref, be_ref, br_ref, y_ref, yb_ref):
        av = a_ref[...]
        if a_fn is not None:
            av = a_fn(av)
        acc = jnp.dot(av.astype(BF16), b_ref[...].astype(BF16), preferred_element_type=F32)
        br_ref[...] = acc
        xhat, _ = _ln_stats(ALPHA * r_ref[...] + acc)
        y = xhat * g_ref[...] + be_ref[...]
        y_ref[...] = y
        yb_ref[...] = y.astype(BF16)

    sd = jax.ShapeDtypeStruct((m, d), F32)
    return _pcall(body, grid=(m // tm,),
                  in_specs=[pl.BlockSpec((tm, k), lambda i: (i, 0)), pl.BlockSpec((k, d), lambda i: (0, 0)), row, par, par],
                  out_specs=(row, row, row), out_shape=(sd, sd, jax.ShapeDtypeStruct((m, d), BF16)), name=name,
                  compiler_params=_cparams(("parallel",)))(a, b, res, g, beta)


def _mm_ln_bwd(a, b, res, branch, g, dy0, coef0, *, tm, name, jobs=()):
    m, k = a.shape
    d = b.shape[0]
    tm = min(tm, m)
    row = pl.BlockSpec((tm, d), lambda i: (i, 0))
    par = pl.BlockSpec((1, d), lambda i: (0, 0))

    def body(a_ref, b_ref, r_ref, br_ref, g_ref, dy0_ref, dt_ref, dtb_ref, dg_ref, db_ref):
        acc = lax.dot_general(a_ref[...].astype(BF16), b_ref[...].astype(BF16), (((1,), (1,)), ((), ())),
                              preferred_element_type=F32)
        dy = coef0 * dy0_ref[...] + acc
        xhat, rstd = _ln_stats(ALPHA * r_ref[...] + br_ref[...])
        dt = _ln_bwd_rows(dy, xhat, rstd, g_ref[...])
        dt_ref[...] = dt
        dtb_ref[...] = dt.astype(BF16)

        @pl.when(pl.program_id(0) == 0)
        def _():
            dg_ref[...] = jnp.zeros_like(dg_ref)
            db_ref[...] = jnp.zeros_like(db_ref)

        dg_ref[...] += jnp.sum(dy * xhat, axis=0, keepdims=True)
        db_ref[...] += jnp.sum(dy, axis=0, keepdims=True)

    pd = jax.ShapeDtypeStruct((1, d), F32)
    outs, jouts = _hosted(
        body, jobs, grid=(m // tm,),
        in_specs=[pl.BlockSpec((tm, k), lambda i: (i, 0)), pl.BlockSpec((d, k), lambda i: (0, 0)), row, row, par, row],
        out_specs=(row, row, par, par),
        out_shape=(jax.ShapeDtypeStruct((m, d), F32), jax.ShapeDtypeStruct((m, d), BF16), pd, pd),
        args=(a, b, res, branch, g, dy0), name=name)
    return (tuple(outs), jouts) if jobs else tuple(outs)


def _head(x2, x2b, p, wg, wp, g, beta, tgt, *, name):
    s, d = x2.shape
    tile = 2 * ROW_TILE
    row = pl.BlockSpec((tile, d), lambda i: (i, 0))
    par = pl.BlockSpec((1, d), lambda i: (0, 0))
    lsp = pl.BlockSpec((1, LANE), lambda i: (0, 0))
    whole = lambda a: pl.BlockSpec(a.shape, lambda i: (0, 0))

    def body(x2_ref, x2b_ref, p_ref, wg_ref, wp_ref, g_ref, be_ref, t_ref,
             loss_ref, dgp_ref, dple_ref, dt_ref, dg_ref, db_ref):
        gate = _sigmoid(_dot(x2b_ref[...], wg_ref[...]))
        ple_v = _dot(p_ref[...], wp_ref[...])
        xhat, rstd = _ln_stats(ALPHA * x2_ref[...] + gate * ple_v)
        err = xhat * g_ref[...] + be_ref[...] - t_ref[...]
        dy = err * (1.0 / d)
        dt = _ln_bwd_rows(dy, xhat, rstd, g_ref[...])
        dt_ref[...] = dt
        dgp_ref[...] = (dt * ple_v * gate * (1.0 - gate)).astype(BF16)
        dple_ref[...] = (dt * gate).astype(BF16)

        @pl.when(pl.program_id(0) == 0)
        def _():
            loss_ref[...] = jnp.zeros_like(loss_ref)
            dg_ref[...] = jnp.zeros_like(dg_ref)
            db_ref[...] = jnp.zeros_like(db_ref)

        loss_ref[...] += 0.5 * jnp.sum(jnp.mean(err * err, axis=-1, keepdims=True))
        dg_ref[...] += jnp.sum(dy * xhat, axis=0, keepdims=True)
        db_ref[...] += jnp.sum(dy, axis=0, keepdims=True)

    sd = jax.ShapeDtypeStruct((s, d), F32)
    sb = jax.ShapeDtypeStruct((s, d), BF16)
    pd = jax.ShapeDtypeStruct((1, d), F32)
    return _pcall(body, grid=(s // tile,),
                  in_specs=[row, row, pl.BlockSpec((tile, p.shape[1]), lambda i: (i, 0)), whole(wg), whole(wp), par, par,
                            row],
                  out_specs=(lsp, row, row, row, par, par),
                  out_shape=(jax.ShapeDtypeStruct((1, LANE), F32), sb, sb, sd, pd, pd),
                  name=name, compiler_params=_cparams(("arbitrary",)))(x2, x2b, p, wg, wp, g, beta, tgt)


CONV_R = 256
PAD = SUBLANE


def _shift_down(ext, s):
    if s == 0:
        return ext[PAD:, :]
    return pltpu.roll(ext, s, 0)[PAD:, :]


def _shift_up(ext, s):
    r = ext.shape[0] - PAD
    if s == 0:
        return ext[:r, :]
    return pltpu.roll(ext, r + PAD - s, 0)[:r, :]


def _conv_rows(xpad_ref, r0, w_ref):
    ext = xpad_ref[pl.ds(r0, CONV_R + PAD), :]
    acc = _shift_down(ext, 0) * w_ref[3:4, :]
    for k in range(3):
        acc = acc + _shift_down(ext, 3 - k) * w_ref[k:k + 1, :]
    return acc, ext


def _fill_front_padded(dst_ref, src_ref, s):
    dst_ref[0:PAD, :] = jnp.zeros((PAD, dst_ref.shape[1]), F32)

    def cp(q, _):
        r0 = pl.multiple_of(q * CONV_R, CONV_R)
        dst_ref[pl.ds(pl.multiple_of(PAD + r0, PAD), CONV_R), :] = src_ref[pl.ds(r0, CONV_R), :]
        return 0

    lax.fori_loop(0, s // CONV_R, cp, 0)


def _conv_silu_fwd(proj, w8, b, *, col0, width, ct, name, jobs=()):
    s = proj.shape[0]
    nb = col0 // ct

    def body(x_ref, w_ref, b_ref, o_ref, xpad):
        _fill_front_padded(xpad, x_ref, s)

        def step(q, _):
            r0 = pl.multiple_of(q * CONV_R, CONV_R)
            acc, _e = _conv_rows(xpad, r0, w_ref)
            pre = acc + b_ref[...]
            o_ref[pl.ds(r0, CONV_R), :] = pre * _sigmoid(pre)
            return 0

        lax.fori_loop(0, s // CONV_R, step, 0)

    (out,), jouts = _hosted(
        body, jobs, grid=(width // ct,),
        in_specs=[pl.BlockSpec((s, ct), lambda j: (0, nb + j)), pl.BlockSpec((SUBLANE, ct), lambda j: (0, j)),
                  pl.BlockSpec((1, ct), lambda j: (0, j))],
        out_specs=[pl.BlockSpec((s, ct), lambda j: (0, j))],
        out_shape=[jax.ShapeDtypeStruct((s, width), F32)],
        scratch_shapes=[pltpu.VMEM((s + PAD, ct), F32)], name=name, args=(proj, w8, b))
    return (out, jouts) if jobs else out


def _conv_bwd_rows(dpad_ref, r0, w_ref):
    return _conv_bwd_ext(dpad_ref[pl.ds(r0, CONV_R + PAD), :], w_ref)


def _conv_bwd_ext(ext, w_ref):
    acc = _shift_up(ext, 0) * w_ref[3:4, :]
    for k in range(3):
        acc = acc + _shift_up(ext, 3 - k) * w_ref[k:k + 1, :]
    return acc


def _conv_silu_bwd(proj, dact, w8, b, *, col0, width, ct, name, jobs=()):
    s = proj.shape[0]
    nb = col0 // ct

    def body(x_ref, d_ref, w_ref, b_ref, dx_ref, dwb_ref, xpad, dpad):
        _fill_front_padded(xpad, x_ref, s)
        dpad[pl.ds(s, PAD), :] = jnp.zeros((PAD, ct), F32)
        dwb_ref[...] = jnp.zeros_like(dwb_ref)

        def step(q, _):
            r0 = pl.multiple_of(q * CONV_R, CONV_R)
            acc, ext = _conv_rows(xpad, r0, w_ref)
            pre = acc + b_ref[...]
            sg = _sigmoid(pre)
            dpre = d_ref[pl.ds(r0, CONV_R), :] * sg * (1.0 + pre * (1.0 - sg))
            dpad[pl.ds(r0, CONV_R), :] = dpre
            for k in range(4):
                dwb_ref[k:k + 1, :] += jnp.sum(dpre * _shift_down(ext, 3 - k), axis=0, keepdims=True)
            dwb_ref[4:5, :] += jnp.sum(dpre, axis=0, keepdims=True)
            return 0

        lax.fori_loop(0, s // CONV_R, step, 0)

        def step2(q, _):
            r0 = pl.multiple_of(q * CONV_R, CONV_R)
            dx_ref[pl.ds(r0, CONV_R), :] = _conv_bwd_rows(dpad, r0, w_ref).astype(BF16)
            return 0

        lax.fori_loop(0, s // CONV_R, step2, 0)

    colb = pl.BlockSpec((s, ct), lambda j: (0, j))
    outs, jouts = _hosted(
        body, jobs, grid=(width // ct,),
        in_specs=[pl.BlockSpec((s, ct), lambda j: (0, nb + j)), colb, pl.BlockSpec((SUBLANE, ct), lambda j: (0, j)),
                  pl.BlockSpec((1, ct), lambda j: (0, j))],
        out_specs=(colb, pl.BlockSpec((SUBLANE, ct), lambda j: (0, j))),
        out_shape=(jax.ShapeDtypeStruct((s, width), BF16), jax.ShapeDtypeStruct((SUBLANE, width), F32)),
        scratch_shapes=[pltpu.VMEM((s + PAD, ct), F32), pltpu.VMEM((s + PAD, ct), F32)], name=name,
        args=(proj, dact, w8, b))
    return (tuple(outs), jouts) if jobs else tuple(outs)


LRU_CT = 128


def _row_of(v, r):
    return jnp.sum(jnp.where(_iota((v.shape[0], 1), 0) == r, v, 0.0), axis=0, keepdims=True)


def _scan_fwd(a, u):
    r = a.shape[0]
    row = _iota((r, 1), 0)
    d = 1
    while d < r:
        valid = row >= d
        u = jnp.where(valid, a * pltpu.roll(u, d, 0) + u, u)
        a = jnp.where(valid, a * pltpu.roll(a, d, 0), a)
        d *= 2
    return a, u


def _scan_rev(b, u):
    r = b.shape[0]
    row = _iota((r, 1), 0)
    d = 1
    while d < r:
        valid = row < r - d
        u = jnp.where(valid, b * pltpu.roll(u, r - d, 0) + u, u)
        b = jnp.where(valid, b * pltpu.roll(b, r - d, 0), b)
        d *= 2
    return b, u


def _lru_chunk(xpad, r0, cw_ref, cb, wa, ba, wx, bx, sp):
    acc, ext = _conv_rows(xpad, r0, cw_ref)
    xl = acc + cb
    r = _sigmoid(_dot(xl, wa) + ba)
    i = _sigmoid(_dot(xl, wx) + bx)
    la = -LRU_C * r * sp
    a = jnp.exp(la)
    a2 = jnp.exp(2.0 * la)
    mult = jnp.sqrt(-jnp.tanh(la) * (a2 + 1.0))
    first = (r0 + _iota((CONV_R, 1), 0)) == 0
    mult = jnp.where(first, 1.0, mult)
    return ext, xl, r, i, a, a2, mult, first


def _lru_specs(s):
    ct = LRU_CT
    nb_g = COL_G // ct
    return dict(
        x=pl.BlockSpec((s, ct), lambda j: (0, j)),
        g=pl.BlockSpec((s, ct), lambda j: (0, nb_g + j)),
        col=pl.BlockSpec((s, ct), lambda j: (0, j)),
        cw=pl.BlockSpec((SUBLANE, ct), lambda j: (0, j)),
        vec=pl.BlockSpec((1, ct), lambda j: (0, j)),
        gate=pl.BlockSpec((None, ct, ct), lambda j: (j, 0, 0)),
    )


def _lru_fwd(proj, cw8, cb, wa_bd, ba, wx_bd, bx, ap, *, name, jobs=()):
    s = proj.shape[0]
    ct = LRU_CT
    sp_ = _lru_specs(s)

    def body(x_ref, g_ref, cw_ref, cb_ref, wa_ref, ba_ref, wx_ref, bx_ref, ap_ref, y_ref, h_ref, xpad):
        _fill_front_padded(xpad, x_ref, s)
        sp = _softplus(-ap_ref[...])

        def step(q, carry):
            r0 = pl.multiple_of(q * CONV_R, CONV_R)
            _e, xl, _r, i, a, _a2, mult, _f = _lru_chunk(xpad, r0, cw_ref, cb_ref[...], wa_ref[...], ba_ref[...],
                                                       wx_ref[...], bx_ref[...], sp)
            acum, ucum = _scan_fwd(a, xl * i * mult)
            h = acum * carry + ucum
            h_ref[pl.ds(r0, CONV_R), :] = h
            ge, _th = _gelu(g_ref[pl.ds(r0, CONV_R), :])
            y_ref[pl.ds(r0, CONV_R), :] = (ge * h).astype(BF16)
            return _row_of(h, CONV_R - 1)

        lax.fori_loop(0, s // CONV_R, step, jnp.zeros((1, ct), F32))

    (ymix, hs), jouts = _hosted(
        body, jobs, grid=(LRU_W // ct,),
        in_specs=[sp_["x"], sp_["g"], sp_["cw"], sp_["vec"], sp_["gate"], sp_["vec"], sp_["gate"], sp_["vec"], sp_["vec"]],
        out_specs=(sp_["col"], sp_["col"]),
        out_shape=(jax.ShapeDtypeStruct((s, LRU_W + SSD_W), BF16), jax.ShapeDtypeStruct((s, LRU_W), F32)),
        scratch_shapes=[pltpu.VMEM((s + PAD, ct), F32)],
        name=name, args=(proj, proj, cw8, cb, wa_bd, ba, wx_bd, bx, ap))
    return ((ymix, hs), jouts) if jobs else (ymix, hs)


def _lru_bwd(proj, dy, hs, cw8, cb, wa_bd, ba, wx_bd, bx, ap, *, name, jobs=()):
    s = proj.shape[0]
    ct = LRU_CT
    sp_ = _lru_specs(s)

    nq = s // CONV_R

    def body(x_ref, g_ref, dy_ref, h_ref, cw_ref, cb_ref, wa_ref, ba_ref, wx_ref, bx_ref, ap_ref,
             dx_ref, dg_ref, dcwb_ref, dwa_ref, dwx_ref, xpad, hpad):
        _fill_front_padded(xpad, x_ref, s)
        _fill_front_padded(hpad, h_ref, s)
        apv = ap_ref[...]
        sp = _softplus(-apv)
        cb_v, wa, ba_v, wx, bx_v = cb_ref[...], wa_ref[...], ba_ref[...], wx_ref[...], bx_ref[...]
        dcwb_ref[...] = jnp.zeros_like(dcwb_ref)
        dwa_ref[...] = jnp.zeros_like(dwa_ref)
        dwx_ref[...] = jnp.zeros_like(dwx_ref)

        def back(k, carry):
            g_next, a_next, dxl_next = carry
            last_row = _iota((CONV_R, 1), 0) == CONV_R - 1
            r0 = pl.multiple_of((nq - 1 - k) * CONV_R, CONV_R)
            ext, xl, r, i, a, a2, mult, first = _lru_chunk(xpad, r0, cw_ref, cb_v, wa, ba_v, wx, bx_v, sp)
            gv = g_ref[pl.ds(r0, CONV_R), :]
            dyv = dy_ref[pl.ds(r0, CONV_R), :]
            hext = hpad[pl.ds(r0, CONV_R + PAD), :]
            ge, th = _gelu(gv)
            dg_ref[pl.ds(r0, CONV_R), :] = (dyv * _shift_down(hext, 0) * _gelu_grad(gv, th)).astype(BF16)
            b = jnp.where(last_row, a_next, pltpu.roll(a, CONV_R - 1, 0))
            bcum, dcum = _scan_rev(b, dyv * ge)
            gval = dcum + bcum * g_next
            hprev = _shift_down(hext, 1)
            da = gval * hprev
            dxl = gval * i * mult
            di = gval * xl * mult
            dmult = jnp.where(first, 0.0, gval * xl * i)
            dla = da * a - dmult * a2 / mult
            dr = dla * (-LRU_C) * sp
            dcwb_ref[7:8, :] += jnp.sum(dla * (-LRU_C) * r, axis=0, keepdims=True)
            dpr = dr * r * (1.0 - r)
            dpi = di * i * (1.0 - i)
            dxl = dxl + _dot_nt(dpr, wa) + _dot_nt(dpi, wx)
            dwa_ref[...] += _dot_tn(xl, dpr)
            dwx_ref[...] += _dot_tn(xl, dpi)
            dcwb_ref[5:6, :] += jnp.sum(dpr, axis=0, keepdims=True)
            dcwb_ref[6:7, :] += jnp.sum(dpi, axis=0, keepdims=True)
            for tap in range(4):
                dcwb_ref[tap:tap + 1, :] += jnp.sum(dxl * _shift_down(ext, 3 - tap), axis=0, keepdims=True)
            dcwb_ref[4:5, :] += jnp.sum(dxl, axis=0, keepdims=True)
            dx_ref[pl.ds(r0, CONV_R), :] = _conv_bwd_ext(jnp.concatenate([dxl, dxl_next], axis=0), cw_ref).astype(BF16)
            return _row_of(gval, 0), _row_of(a, 0), dxl[:PAD, :]

        zero = jnp.zeros((1, ct), F32)
        lax.fori_loop(0, nq, back, (zero, zero, jnp.zeros((PAD, ct), F32)))
        dcwb_ref[7:8, :] = dcwb_ref[7:8, :] * (-_sigmoid(-apv))

    nt = LRU_W // ct
    outs, jouts = _hosted(
        body, jobs, grid=(nt,),
        in_specs=[sp_["x"], sp_["g"], sp_["col"], sp_["col"], sp_["cw"], sp_["vec"], sp_["gate"], sp_["vec"], sp_["gate"],
                  sp_["vec"], sp_["vec"]],
        out_specs=(sp_["col"], sp_["col"], sp_["cw"], sp_["gate"], sp_["gate"]),
        out_shape=(jax.ShapeDtypeStruct((s, LRU_W), BF16), jax.ShapeDtypeStruct((s, LRU_W), BF16),
                   jax.ShapeDtypeStruct((SUBLANE, LRU_W), F32), jax.ShapeDtypeStruct((nt, ct, ct), F32),
                   jax.ShapeDtypeStruct((nt, ct, ct), F32)),
        scratch_shapes=[pltpu.VMEM((s + PAD, ct), F32), pltpu.VMEM((s + PAD, ct), F32)],
        name=name, args=(proj, proj, dy, hs, cw8, cb, wa_bd, ba, wx_bd, bx, ap))
    return (tuple(outs), jouts) if jobs else tuple(outs)


def _split3(v):
    hi = v.astype(BF16)
    r1 = v - hi.astype(F32)
    mid = r1.astype(BF16)
    lo = (r1 - mid.astype(F32)).astype(BF16)
    return hi, mid, lo


def _dot01(m01, v):
    mb = m01.astype(BF16)
    hi, mid, lo = _split3(v)
    f = lambda part: jnp.dot(mb, part, preferred_element_type=F32)
    return f(hi) + f(mid) + f(lo)


def _dot01_r(v, m01):
    mb = m01.astype(BF16)
    hi, mid, lo = _split3(v)
    f = lambda part: jnp.dot(part, mb, preferred_element_type=F32)
    return f(hi) + f(mid) + f(lo)


def _ssd_prep(dtr, bias, alog_pad):
    l = CHUNK
    lane = _iota((1, LANE), 1)
    a_head = jnp.where(lane < N_HEAD, -jnp.exp(alog_pad), 0.0)
    dt = _softplus(dtr + bias)
    tril = (_iota((l, l), 1) <= _iota((l, l), 0)).astype(F32)
    a = dt * a_head
    cs = _dot01(tril, a)
    tot = jnp.sum(a, axis=0, keepdims=True)
    return dict(a_head=a_head, dt=dt, tril=tril, cs=cs, tot=tot)


def _col(v, h):
    lane = _iota(v.shape, 1)
    return jnp.sum(jnp.where(lane == h, v, 0.0), axis=1, keepdims=True)


def _decay_mat(cs, cst_ref, h, causal):
    row = cst_ref[h:h + 1, :]
    return jnp.exp(jnp.where(causal, _col(cs, h) - row, NEG_BIG))


def _head_mask(j, rows=CHUNK):
    lane = _iota((rows, GROUP_W), 1)
    return (lane >= j * HEAD_P) & (lane < (j + 1) * HEAD_P)


def _over_heads(v, g):
    r = v.shape[0]
    out = jnp.zeros((r, GROUP_W), F32)
    for j in range(4):
        out = jnp.where(_head_mask(j, r), _col(v, 4 * g + j), out)
    return out


def _ssd_group_fwd(q, g, xs_g, bg, cg, ht_g, cst_ref, causal, dx_g):
    dtx_g, csx_g, totx_g = _over_heads(q["dt"], g), _over_heads(q["cs"], g), _over_heads(q["tot"], g)
    xdt = xs_g * dtx_g
    ex = jnp.exp(csx_g)
    cb = _dot_nt(cg, bg)
    yoff = _dot(cg, ht_g) * ex
    ydiag = jnp.zeros((CHUNK, GROUP_W), F32)
    for j in range(4):
        sc = cb * _decay_mat(q["cs"], cst_ref, 4 * g + j, causal)
        ydiag = jnp.where(_head_mask(j), _dot(sc, xdt), ydiag)
    y = ydiag + yoff + xs_g * dx_g
    dsx = jnp.exp(totx_g - csx_g)
    return y, dict(xdt=xdt, ex=ex, cb=cb, yoff=yoff, dsx=dsx, dtx=dtx_g, totx=totx_g)


def _gated_norm_fwd(y_g, z_g, w_g):
    sz = _sigmoid(z_g)
    silu = z_g * sz
    yf = y_g * silu
    rs = lax.rsqrt(jnp.mean(yf * yf, axis=1, keepdims=True) + RMS_EPS)
    yn = yf * rs
    return yn * w_g, (sz, silu, rs, yn)


def _ssd_fwd(xact, proj, ymix, bias_pad, alog_pad, dxp, normw, *, name, jobs=()):
    s = xact.shape[0]
    nc = s // CHUNK

    def body(xa_ref, dt_ref, z_ref, _ymix_ref, bias_ref, alp_ref, dx_ref, nw_ref, y_ref, hp_ref, ht, cst):
        @pl.when(pl.program_id(0) == 0)
        def _():
            ht[...] = jnp.zeros_like(ht)

        hp_ref[...] = ht[...]
        q = _ssd_prep(dt_ref[...], bias_ref[...], alp_ref[...])
        cst[...] = q["cs"].T
        causal = q["tril"] > 0.0
        for g in range(N_GROUP):
            sl = slice(g * GROUP_W, (g + 1) * GROUP_W)
            xs_g = xa_ref[:, sl]
            bg = xa_ref[:, SSD_W + g * N_STATE:SSD_W + (g + 1) * N_STATE]
            cg = xa_ref[:, SSD_W + N_GROUP * N_STATE + g * N_STATE:SSD_W + N_GROUP * N_STATE + (g + 1) * N_STATE]
            ht_g = ht[:, sl]
            y, f = _ssd_group_fwd(q, g, xs_g, bg, cg, ht_g, cst, causal, dx_ref[:, sl])
            out, _ = _gated_norm_fwd(y, z_ref[:, sl], nw_ref[:, sl])
            y_ref[:, sl] = out.astype(BF16)
            ht[:, sl] = jnp.exp(f["totx"]) * ht_g + _dot_tn(bg, f["xdt"] * f["dsx"])

    par = lambda w: pl.BlockSpec((1, w), lambda c: (0, 0))
    (ycat, hprev), jouts = _hosted(
        body, jobs, grid=(nc,),
        in_specs=[pl.BlockSpec((CHUNK, XBC), lambda c: (c, 0)),
                  pl.BlockSpec((CHUNK, LANE), lambda c: (c, COL_DT // LANE)),
                  pl.BlockSpec((CHUNK, SSD_W), lambda c: (c, COL_Z // SSD_W)),
                  ANY_SPEC, par(LANE), par(LANE), par(SSD_W), par(SSD_W)],
        out_specs=(pl.BlockSpec((CHUNK, SSD_W), lambda c: (c, LRU_W // SSD_W)),
                   pl.BlockSpec((None, N_STATE, SSD_W), lambda c: (c, 0, 0))),
        out_shape=(jax.ShapeDtypeStruct(ymix.shape, ymix.dtype), jax.ShapeDtypeStruct((nc, N_STATE, SSD_W), F32)),
        scratch_shapes=[pltpu.VMEM((N_STATE, SSD_W), F32), pltpu.VMEM((CHUNK, LANE), F32)],
        aliases={3: 0}, name=name, args=(xact, proj, proj, ymix, bias_pad, alog_pad, dxp, normw))
    return ((ycat, hprev), jouts) if jobs else (ycat, hprev)


def _ssd_bwd(xact, proj, dycat, hprev, bias_pad, alog_pad, dxp, normw, *, name, jobs=()):
    s = xact.shape[0]
    nc = s // CHUNK
    l = CHUNK

    def body(xa_ref, dt_ref, z_ref, dy_ref, hp_ref, bias_ref, alp_ref, dx_ref, nw_ref,
             dxa_ref, ddt_ref, dz_ref, dnw_ref, small_ref, dht, cst, accx, dcsx_s, ddtx_s):
        step = pl.program_id(0)

        @pl.when(step == 0)
        def _():
            dht[...] = jnp.zeros_like(dht)
            accx[...] = jnp.zeros_like(accx)
            dnw_ref[...] = jnp.zeros_like(dnw_ref)
            small_ref[...] = jnp.zeros_like(small_ref)

        dtr = dt_ref[...]
        q = _ssd_prep(dtr, bias_ref[...], alp_ref[...])
        cst[...] = q["cs"].T
        causal = q["tril"] > 0.0
        eye = _iota((l, l), 0) == _iota((l, l), 1)
        lane = _iota((l, LANE), 1)
        dcs_head = jnp.zeros((l, LANE), F32)
        for g in range(N_GROUP):
            sl = slice(g * GROUP_W, (g + 1) * GROUP_W)
            slb = slice(SSD_W + g * N_STATE, SSD_W + (g + 1) * N_STATE)
            slc = slice(SSD_W + N_GROUP * N_STATE + g * N_STATE, SSD_W + N_GROUP * N_STATE + (g + 1) * N_STATE)
            xs_g, bg, cg = xa_ref[:, sl], xa_ref[:, slb], xa_ref[:, slc]
            ht_g = hp_ref[:, sl]
            dxp_g = dx_ref[:, sl]
            y, f = _ssd_group_fwd(q, g, xs_g, bg, cg, ht_g, cst, causal, dxp_g)
            z_g, nw_g = z_ref[:, sl], nw_ref[:, sl]
            _o, (sz, silu, rs, yn) = _gated_norm_fwd(y, z_g, nw_g)
            dout = dy_ref[:, sl]
            dnw_ref[:, sl] += jnp.sum(dout * yn, axis=0, keepdims=True)
            dyn = dout * nw_g
            dyf = rs * (dyn - yn * jnp.mean(dyn * yn, axis=1, keepdims=True))
            dy = dyf * silu
            dz_ref[:, sl] = (dyf * y * sz * (1.0 + z_g * (1.0 - sz))).astype(BF16)
            accx[0:1, sl] += jnp.sum(dy * xs_g, axis=0, keepdims=True)
            dyo = dy * f["ex"]
            dcg = _dot_nt(dyo, ht_g)
            dht_prev = _dot_tn(cg, dyo)
            dcsx = dy * f["yoff"]
            xdt = f["xdt"]
            dxdt = jnp.zeros((l, GROUP_W), F32)
            dcb = jnp.zeros((l, l), F32)
            for j in range(4):
                h = 4 * g + j
                lm = _decay_mat(q["cs"], cst, h, causal)
                sc = f["cb"] * lm
                mask = _head_mask(j)
                ds_ = jnp.where(causal, _dot_nt(jnp.where(mask, dy, 0.0), xdt), 0.0)
                dxdt = jnp.where(mask, _dot_tn(sc, dy), dxdt)
                dcb = dcb + ds_ * lm
                m = ds_ * sc
                rsum = jnp.sum(m, axis=1, keepdims=True)
                csum = jnp.sum(m, axis=0, keepdims=True)
                csum_col = jnp.sum(jnp.where(eye, csum, 0.0), axis=1, keepdims=True)
                dcs_head = dcs_head + jnp.where(lane == h, rsum - csum_col, 0.0)
            dhn = dht[:, sl]
            etot = jnp.exp(f["totx"])
            dxd = _dot(bg, dhn)
            dbg = _dot_nt(xdt * f["dsx"], dhn)
            dxdt = dxdt + dxd * f["dsx"]
            qq = dxd * xdt * f["dsx"]
            dcsx = dcsx - qq
            dtot = jnp.sum(qq, axis=0, keepdims=True) + jnp.sum(dhn * ht_g, axis=0, keepdims=True) * etot
            dht[:, sl] = etot * dhn + dht_prev
            dcg = dcg + _dot(dcb, bg)
            dbg = dbg + _dot_tn(dcb, cg)
            dxa_ref[:, sl] = dxdt * f["dtx"] + dy * dxp_g
            dxa_ref[:, slb] = dbg
            dxa_ref[:, slc] = dcg
            dcsx_s[:, sl] = dcsx
            ddtx_s[:, sl] = dxdt * xs_g
            accx[2:3, sl] = dtot
        reduce = (jnp.right_shift(_iota((SSD_W, LANE), 0), 6) == _iota((SSD_W, LANE), 1)).astype(F32)
        triu = (_iota((l, l), 1) >= _iota((l, l), 0)).astype(F32)
        dtot = _dot01_r(accx[...], reduce)[2:3, :]
        da_head = _dot01(triu, dcs_head + _dot01_r(dcsx_s[...], reduce)) + dtot
        ddt = _dot01_r(ddtx_s[...], reduce) + da_head * q["a_head"]
        small_ref[1:2, :] += jnp.sum(da_head * q["dt"], axis=0, keepdims=True)
        ddtr = ddt * _sigmoid(dtr + bias_ref[...])
        ddt_ref[...] = ddtr.astype(BF16)
        small_ref[0:1, :] += jnp.sum(ddtr, axis=0, keepdims=True)

        @pl.when(step == nc - 1)
        def _():
            small_ref[1:2, :] = small_ref[1:2, :] * q["a_head"]
            small_ref[2:3, :] = _dot01_r(accx[...], reduce)[0:1, :]

    rev = lambda c: nc - 1 - c
    par = lambda w: pl.BlockSpec((1, w), lambda c: (0, 0))
    outs, jouts = _hosted(
        body, jobs, grid=(nc,),
        in_specs=[pl.BlockSpec((CHUNK, XBC), lambda c: (rev(c), 0)),
                  pl.BlockSpec((CHUNK, LANE), lambda c: (rev(c), COL_DT // LANE)),
                  pl.BlockSpec((CHUNK, SSD_W), lambda c: (rev(c), COL_Z // SSD_W)),
                  pl.BlockSpec((CHUNK, SSD_W), lambda c: (rev(c), 1)),
                  pl.BlockSpec((None, N_STATE, SSD_W), lambda c: (rev(c), 0, 0)),
                  par(LANE), par(LANE), par(SSD_W), par(SSD_W)],
        out_specs=(pl.BlockSpec((CHUNK, XBC), lambda c: (rev(c), 0)),
                   pl.BlockSpec((CHUNK, LANE), lambda c: (rev(c), 0)),
                   pl.BlockSpec((CHUNK, SSD_W), lambda c: (rev(c), 0)),
                   par(SSD_W), pl.BlockSpec((SUBLANE, LANE), lambda c: (0, 0))),
        out_shape=(jax.ShapeDtypeStruct((s, XBC), F32), jax.ShapeDtypeStruct((s, LANE), BF16),
                   jax.ShapeDtypeStruct((s, SSD_W), BF16), jax.ShapeDtypeStruct((1, SSD_W), F32),
                   jax.ShapeDtypeStruct((SUBLANE, LANE), F32)),
        scratch_shapes=[pltpu.VMEM((N_STATE, SSD_W), F32), pltpu.VMEM((CHUNK, LANE), F32),
                        pltpu.VMEM((SUBLANE, SSD_W), F32), pltpu.VMEM((CHUNK, SSD_W), F32),
                        pltpu.VMEM((CHUNK, SSD_W), F32)],
        name=name, args=(xact, proj, proj, dycat, hprev, bias_pad, alog_pad, dxp, normw))
    return (tuple(outs), jouts) if jobs else tuple(outs)


def _blockdiag(w):
    w2 = w.reshape(N_HEAD // 2, 2, HEAD_P, HEAD_P)
    z = jnp.zeros((N_HEAD // 2, HEAD_P, HEAD_P), w.dtype)
    top = jnp.concatenate([w2[:, 0], z], axis=2)
    bot = jnp.concatenate([z, w2[:, 1]], axis=2)
    return jnp.concatenate([top, bot], axis=1)


def _unblockdiag(wbd):
    a = wbd[:, :HEAD_P, :HEAD_P]
    b = wbd[:, HEAD_P:, HEAD_P:]
    return jnp.stack([a, b], axis=1).reshape(N_HEAD, HEAD_P, HEAD_P)


def _pad_rows8(w):
    return jnp.concatenate([w, jnp.zeros((SUBLANE - w.shape[0], w.shape[1]), w.dtype)], axis=0)


def _pad_lane(v):
    return jnp.concatenate([v, jnp.zeros((1, LANE - v.shape[1]), v.dtype)], axis=1)


class _NoExchange:
    def ride(self, host):
        return []

    def done(self, jobs, outs, w):
        pass

    def grad(self, name, val):
        pass

    def small(self, raw):
        pass

    def pairs_now(self):
        pass


def _local_step(x, p, tgt, w, hooks=_NoExchange()):
    cw_l = _pad_rows8(w["lru_conv_w"])
    cw_s = _pad_rows8(w["ssd_conv_w"])
    wa_bd = _blockdiag(w["lru_gate_a_w"])
    wx_bd = _blockdiag(w["lru_gate_x_w"])
    ba = w["lru_gate_a_b"].reshape(1, LRU_W)
    bx = w["lru_gate_x_b"].reshape(1, LRU_W)
    bias_pad = _pad_lane(w["ssd_dt_bias"])
    alog_pad = _pad_lane(w["ssd_a_log"])
    dxp = jnp.repeat(w["ssd_d"], HEAD_P, axis=1)

    def host(fn, *a, name, **k):
        jobs = hooks.ride(name)
        res = fn(*a, name=name, jobs=jobs, **k)
        if jobs:
            res, jouts = res
            hooks.done(jobs, jouts, w)
        return res

    def grad(n, val):
        g[n] = val
        hooks.grad(n, val)

    xb = x.astype(BF16)
    proj = host(_mm, xb, w["w_in_t"], "nt", tm=1024, tn=512, name="in_proj")
    ymix, h_lru = host(_lru_fwd, proj, cw_l, w["lru_conv_b"], wa_bd, ba, wx_bd, bx, w["lru_a_param"], name="lru_fwd")
    xact = host(_conv_silu_fwd, proj, cw_s, w["ssd_conv_b"], col0=COL_XBC, width=XBC, ct=256, name="ssd_conv_fwd")
    ycat, hprev = host(_ssd_fwd, xact, proj, ymix, bias_pad, alog_pad, dxp, w["ssd_norm_w"], name="ssd_fwd")
    mix, x1, x1b = _mm_ln(ycat, w["w_out"], x, w["ln1_g"], w["ln1_b"], tm=512, name="out_proj")
    pre = _mm(x1b, w["w_ff1"], "nn", tm=1024, tn=512, out_dtype=BF16, name="ff1")
    ff, x2, x2b = _mm_ln(pre, w["w_ff2"], x1, w["ln2_g"], w["ln2_b"], tm=512, a_fn=_relu2, name="ff2")
    loss, dgpre, dple, dt3, dg3, db3 = _head(x2, x2b, p, w["w_ple_gate"], w["w_ple"], w["ln3_g"], w["ln3_b"], tgt,
                                             name="head")

    g = {}
    g["ln3_g"], g["ln3_b"] = dg3, db3
    grad("w_ple_gate", _mm(x2b, dgpre, "tn", tm=512, tn=1024, out_dtype=BF16, name="d_w_ple_gate"))
    grad("w_ple", _mm(p, dple, "tn", tm=256, tn=512, dest_major=True, out_dtype=BF16, name="d_w_ple"))
    dt2, dt2b, g["ln2_g"], g["ln2_b"] = host(_mm_ln_bwd, dgpre, w["w_ple_gate"], x1, ff, w["ln2_g"], dt3, ALPHA,
                                             tm=512, name="d_x2")
    grad("w_ff2", host(_mm, pre, dt2b, "tn", tm=512, tn=1024, a_fn=_relu2, out_dtype=BF16, name="d_w_ff2"))
    dpre = host(_mm, dt2b, w["w_ff2"], "nt", tm=1024, tn=512, extra=pre, out_dtype=BF16,
                epi=lambda acc, pv: acc * 2.0 * jnp.maximum(pv.astype(F32), 0.0), name="d_pre")
    grad("w_ff1", host(_mm, x1b, dpre, "tn", tm=1024, tn=512, dest_major=True, out_dtype=BF16, name="d_w_ff1"))
    dt1, dt1b, g["ln1_g"], g["ln1_b"] = host(_mm_ln_bwd, dpre, w["w_ff1"], x, mix, w["ln1_g"], dt2, ALPHA,
                                             tm=256, name="d_x1")
    grad("w_out", host(_mm, ycat, dt1b, "tn", tm=512, tn=1024, out_dtype=BF16, name="d_w_out"))
    dycat = host(_mm, dt1b, w["w_out"], "nt", tm=1024, tn=1024, name="d_ycat")
    dxl, dgl, dcwb_l, dwa, dwx = host(_lru_bwd, proj, dycat, h_lru, cw_l, w["lru_conv_b"], wa_bd, ba, wx_bd, bx,
                                      w["lru_a_param"], name="lru_bwd")
    g["lru_gate_a_w"] = _unblockdiag(dwa)
    g["lru_gate_x_w"] = _unblockdiag(dwx)
    raw = dict(lru=dcwb_l, gate_a=g["lru_gate_a_w"].reshape(N_HEAD * HEAD_P, HEAD_P).astype(BF16),
               gate_x=g["lru_gate_x_w"].reshape(N_HEAD * HEAD_P, HEAD_P).astype(BF16))
    hooks.small(raw)
    dxact, ddt, dz, g["ssd_norm_w"], small = host(_ssd_bwd, xact, proj, dycat, hprev, bias_pad, alog_pad, dxp,
                                                   w["ssd_norm_w"], name="ssd_bwd")
    dxbc, dcwb_s = host(_conv_silu_bwd, proj, dxact, cw_s, w["ssd_conv_b"], col0=COL_XBC, width=XBC, ct=256,
                        name="ssd_conv_bwd")
    pieces, offsets = [dxl, dgl, dz, dxbc, ddt], [0, COL_G, COL_Z, COL_XBC, COL_DT]

    g["lru_conv_w"] = dcwb_l[0:4]
    g["lru_conv_b"] = dcwb_l[4:5]
    g["lru_gate_a_b"] = dcwb_l[5:6]
    g["lru_gate_x_b"] = dcwb_l[6:7]
    g["lru_a_param"] = dcwb_l[7:8]
    g["ssd_conv_w"] = dcwb_s[0:4]
    g["ssd_conv_b"] = dcwb_s[4:5]
    g["ssd_dt_bias"] = small[0:1, :N_HEAD]
    g["ssd_a_log"] = small[1:2, :N_HEAD]
    g["ssd_d"] = small[2:3, :N_HEAD]
    rows = jnp.concatenate([g[n] for n in ("ssd_norm_w", "ln1_g", "ln1_b", "ln2_g", "ln2_b", "ln3_g", "ln3_b")]
                           + [jnp.broadcast_to(loss[:, 0:1], (1, D_MODEL))], axis=0)
    late = dict(ssd=dcwb_s, heads=small, rows=rows)
    hooks.small(late)
    raw.update(late)
    dwt = None
    for q, (pc, off) in enumerate(zip(pieces, offsets)):
        dwt = host(_mm, pc, xb, "tn", tm=512, tn=1024, out_dtype=BF16, into=(dwt, off, D_IN),
                   name="d_w_in_%d" % q)
    grad("w_in", dwt)
    hooks.pairs_now()
    grad_x = host(_mm_pieces, pieces, offsets, w["w_in_t"], tm=256, extra=dt1, epi=lambda acc, e: acc + ALPHA * e,
                  name="d_x")
    return loss[0, 0], grad_x, g, raw


ANY_SPEC = pl.BlockSpec(memory_space=pl.ANY)


def _mesh_pos():
    return lax.axis_index("x"), lax.axis_index("y"), lax.axis_index("c")


def _remote(src, dst, send, recv, k, to):
    return pltpu.make_async_remote_copy(src_ref=src, dst_ref=dst, send_sem=send.at[k], recv_sem=recv.at[k],
                                        device_id=to, device_id_type=MESH_T)


class _Job:
    N_SEM = 7

    def __init__(self, kind, inp):
        self.kind, self.inp = kind, inp
        shape = {"gather": (N_DEV,) + inp.shape, "pair": (4,) + inp.shape[1:], "chip": inp.shape}[kind]
        self.out = jax.ShapeDtypeStruct(shape, inp.dtype)

    def _places(self):
        x, y, c = _mesh_pos()
        return (x, y, c), (x, y, 1 - c), [(1 - x, y), (x, 1 - y), (1 - x, 1 - y)]

    def start(self, inp, out, send, recv, loc):
        me, sibling, chips = self._places()
        x, y, c = me
        if self.kind == "gather":
            mine = out.at[4 * x + 2 * y + c]
            pltpu.make_async_copy(inp, mine, loc.at[0]).start()
            _remote(inp, mine, send, recv, 0, sibling).start()
            for j, chip in enumerate(chips):
                _remote(inp, mine, send, recv, 1 + j, (*chip, c)).start()
        elif self.kind == "pair":
            for k in range(4):
                _remote(inp.at[2 * k + (1 - c)], out.at[k], send, recv, k, sibling).start()
        else:
            kme = 2 * x + y
            pltpu.make_async_copy(inp.at[kme], out.at[kme], loc.at[0]).start()
            for j, (tx, ty) in enumerate(chips):
                _remote(inp.at[2 * tx + ty], out.at[kme], send, recv, j, (tx, ty, c)).start()

    def mid(self, inp, out, send, recv, loc):
        if self.kind != "gather":
            return
        me, sibling, chips = self._places()
        c = me[2]
        for j, chip in enumerate(chips):
            landed = out.at[4 * chip[0] + 2 * chip[1] + c]
            _remote(landed, landed, send, recv, 1 + j, me).wait_recv()
            _remote(landed, landed, send, recv, 4 + j, sibling).start()

    def finish(self, inp, out, send, recv, loc):
        me, sibling, chips = self._places()
        x, y, c = me
        if self.kind == "gather":
            blk = lambda px, py, pc: out.at[4 * px + 2 * py + pc]
            mine = blk(*me)
            _remote(inp, blk(*sibling), send, recv, 0, me).wait_recv()
            for j, chip in enumerate(chips):
                _remote(inp, blk(*chip, 1 - c), send, recv, 4 + j, me).wait_recv()
            for k in range(7):
                _remote(inp, mine, send, recv, k, sibling).wait_send()
            pltpu.make_async_copy(inp, mine, loc.at[0]).wait()
        elif self.kind == "pair":
            for k in range(4):
                _remote(inp.at[2 * k + (1 - c)], out.at[k], send, recv, k, sibling).wait()
        else:
            kme = 2 * x + y
            for j, (tx, ty) in enumerate(chips):
                _remote(inp.at[kme], out.at[2 * tx + ty], send, recv, j, (tx, ty, c)).wait_recv()
            for j, (tx, ty) in enumerate(chips):
                _remote(inp.at[2 * tx + ty], out.at[kme], send, recv, j, (tx, ty, c)).wait_send()
            pltpu.make_async_copy(inp.at[kme], out.at[kme], loc.at[0]).wait()


def _job_scratch(jobs):
    sem = pltpu.SemaphoreType.DMA
    return [s for _ in jobs for s in (sem((_Job.N_SEM,)), sem((_Job.N_SEM,)), sem((1,)))]


def _run_jobs(jobs, method, jins, jouts, jsems):
    for q, job in enumerate(jobs):
        getattr(job, method)(jins[q], jouts[q], *jsems[3 * q:3 * q + 3])


def _exchange(jobs, *, name):
    n = len(jobs)

    def body(*refs):
        jins, jouts, jsems = refs[:n], refs[n:2 * n], refs[2 * n:]
        _run_jobs(jobs, "start", jins, jouts, jsems)
        _run_jobs(jobs, "mid", jins, jouts, jsems)
        _run_jobs(jobs, "finish", jins, jouts, jsems)

    return _pcall(body, in_specs=[ANY_SPEC] * n, out_specs=[ANY_SPEC] * n, out_shape=[j.out for j in jobs],
                  scratch_shapes=_job_scratch(jobs), name=name)(*[j.inp for j in jobs])


def _hosted(body, jobs, *, grid, in_specs, out_specs, out_shape, args, name, scratch_shapes=(), aliases=None):
    in_specs, out_specs, out_shape = list(in_specs), list(out_specs), list(out_shape)
    scratch_shapes = list(scratch_shapes)
    n_in, n_out, n_scr, nj = len(in_specs), len(out_specs), len(scratch_shapes), len(jobs)
    sem = ("arbitrary",) * len(grid)
    kw = dict(input_output_aliases=aliases) if aliases else {}
    if not jobs:
        res = _pcall(body, grid=grid, in_specs=in_specs, out_specs=out_specs, out_shape=out_shape,
                     scratch_shapes=scratch_shapes, name=name, compiler_params=_cparams(sem), **kw)(*args)
        return list(res), []

    def full(*refs):
        ins, jins = refs[:n_in], refs[n_in:n_in + nj]
        o0 = n_in + nj
        outs, jouts = refs[o0:o0 + n_out], refs[o0 + n_out:o0 + n_out + nj]
        s0 = o0 + n_out + nj
        scr, jsems = refs[s0:s0 + n_scr], refs[s0 + n_scr:]
        step = pl.program_id(0)
        for ax in range(1, len(grid)):
            step = step * grid[ax] + pl.program_id(ax)
        total = math.prod(grid)

        @pl.when(step == 0)
        def _():
            _run_jobs(jobs, "start", jins, jouts, jsems)

        body(*ins, *outs, *scr)

        @pl.when(step == total - 1)
        def _():
            _run_jobs(jobs, "mid", jins, jouts, jsems)
            _run_jobs(jobs, "finish", jins, jouts, jsems)

    res = _pcall(full, grid=grid, in_specs=in_specs + [ANY_SPEC] * nj, out_specs=out_specs + [ANY_SPEC] * nj,
                 out_shape=out_shape + [j.out for j in jobs], scratch_shapes=scratch_shapes + _job_scratch(jobs),
                 name=name, compiler_params=_cparams(sem), **kw)(*args, *[j.inp for j in jobs])
    return list(res[:n_out]), list(res[n_out:])


def _pair_add(g8, r4, cidx, *, name):
    _, r, c = g8.shape
    tr = ROW_TILE if r % ROW_TILE == 0 else r

    def body(c_ref, g_ref, r_ref, o_ref):
        o_ref[...] = (g_ref[...].astype(F32) + r_ref[...].astype(F32)).astype(BF16)

    return _pcall(
        body,
        grid_spec=pltpu.PrefetchScalarGridSpec(
            num_scalar_prefetch=1, grid=(4, r // tr),
            in_specs=[pl.BlockSpec((None, tr, c), lambda k, i, cr: (2 * k + cr[0], i, 0)),
                      pl.BlockSpec((None, tr, c), lambda k, i, cr: (k, i, 0))],
            out_specs=pl.BlockSpec((None, tr, c), lambda k, i, cr: (k, i, 0))),
        out_shape=jax.ShapeDtypeStruct((4, r, c), BF16), name=name,
        compiler_params=_cparams(("parallel", "parallel")))(cidx, g8, r4)


def _adam_update(g, w_ref, m_ref, v_ref, g_ref, d_ref, mo_ref, vo_ref):
    c1 = 1.0 - ADAM_B1 ** ADAM_STEP
    c2 = 1.0 - ADAM_B2 ** ADAM_STEP
    m2 = ADAM_B1 * m_ref[...] + (1.0 - ADAM_B1) * g
    v2 = ADAM_B2 * v_ref[...] + (1.0 - ADAM_B2) * (g * g)
    g_ref[...] = g
    mo_ref[...] = m2
    vo_ref[...] = v2
    d_ref[...] = -ADAM_LR * ((m2 / c1) / (jnp.sqrt(v2 / c2) + ADAM_EPS) + ADAM_WD * w_ref[...])


def _adamw_rows(srcs, items, own_cols, me1, *, name):
    ns, ni, no = len(srcs), len(items), len(own_cols)
    full = lambda a: pl.BlockSpec(a.shape, lambda i, me: (0,) * a.ndim)
    in_specs = [full(a) for a in srcs]
    args = list(srcs)
    for (si, _r0, w, _m, _v) in own_cols:
        a = srcs[si]
        in_specs.append(pl.BlockSpec((N_DEV, a.shape[1], w.shape[1]), lambda i, me: (0, 0, me[0])))
        args.append(a)
    out_specs, out_shape = [], []
    for (_si, _r0, w, m, v) in list(items) + list(own_cols):
        in_specs += [full(w)] * 3
        args += [w, m, v]
        out_specs += [full(w)] * 4
        out_shape += [jax.ShapeDtypeStruct(w.shape, F32)] * 4

    def body(me_ref, *refs):
        src_refs, own_refs = refs[:ns], refs[ns:ns + no]
        wmv = refs[ns + no:ns + no + 3 * (ni + no)]
        outs = refs[ns + no + 3 * (ni + no):]
        for q, (si, r0, w, _m, _v) in enumerate(list(items) + list(own_cols)):
            nr, cw = w.shape
            gref = src_refs[si] if q < ni else own_refs[q - ni]
            g = gref[0, r0:r0 + nr, 0:cw]
            for d in range(1, N_DEV):
                g = g + gref[d, r0:r0 + nr, 0:cw]
            _adam_update(g, *wmv[3 * q:3 * q + 3], *outs[4 * q:4 * q + 4])

    res = _pcall(
        body,
        grid_spec=pltpu.PrefetchScalarGridSpec(num_scalar_prefetch=1, grid=(1,), in_specs=in_specs, out_specs=out_specs),
        out_shape=out_shape, name=name, compiler_params=_cparams(("arbitrary",)))(me1, *args)
    return [tuple(res[4 * q:4 * q + 4]) for q in range(ni + no)]


def _adamw(gsrc, w, m, v, *, name):
    k, r, c = gsrc.shape
    tr = ROW_TILE if r % ROW_TILE == 0 else r

    def body(gs_ref, w_ref, m_ref, v_ref, g_ref, d_ref, mo_ref, vo_ref):
        g = gs_ref[0].astype(F32)
        for q in range(1, k):
            g = g + gs_ref[q].astype(F32)
        _adam_update(g, w_ref, m_ref, v_ref, g_ref, d_ref, mo_ref, vo_ref)

    tc = c
    if tr == r and r > ROW_TILE and c % 256 == 0:
        tc = 256
    blk = pl.BlockSpec((tr, tc), lambda i, j: (i, j))
    sd = jax.ShapeDtypeStruct((r, c), F32)
    return _pcall(body, grid=(r // tr, c // tc),
                  in_specs=[pl.BlockSpec((k, tr, tc), lambda i, j: (0, i, j)), blk, blk, blk],
                  out_specs=(blk, blk, blk, blk), out_shape=(sd, sd, sd, sd), name=name,
                  compiler_params=_cparams(("parallel", "parallel")))(gsrc, w, m, v)


WEIGHTS = ['w_in', 'lru_conv_w', 'lru_conv_b', 'lru_gate_a_w', 'lru_gate_a_b', 'lru_gate_x_w', 'lru_gate_x_b',
           'lru_a_param', 'ssd_conv_w', 'ssd_conv_b', 'ssd_dt_bias', 'ssd_a_log', 'ssd_d', 'ssd_norm_w', 'w_out',
           'ln1_g', 'ln1_b', 'w_ff1', 'w_ff2', 'ln2_g', 'ln2_b', 'w_ple_gate', 'w_ple', 'ln3_g', 'ln3_b']
BIG = ['w_in', 'w_out', 'w_ff1', 'w_ff2', 'w_ple_gate', 'w_ple']
COL_SHARDED = ('w_ff1', 'w_ple')
CONV = ['lru_conv_w', 'ssd_conv_w']
REPL = [n for n in WEIGHTS if n not in BIG and n not in CONV]
CONV_CH = {'lru_conv_w': LRU_W, 'ssd_conv_w': XBC}


def _to_dest_major(name, gfull):
    if name in COL_SHARDED:
        r, cfull = gfull.shape
        return gfull.reshape(r, N_DEV, cfull // N_DEV).transpose(1, 0, 2)
    rfull, cdim = gfull.shape
    return gfull.reshape(N_DEV, rfull // N_DEV, cdim)


def _full_weight(name, gathered):
    if name in COL_SHARDED:
        _, r, cs = gathered.shape
        full = gathered.transpose(1, 0, 2).reshape(r, N_DEV * cs)
    else:
        _, rs, cdim = gathered.shape
        full = gathered.reshape(N_DEV * rs, cdim)
    if name == 'w_in':
        full = lax.dynamic_update_slice(jnp.zeros((D_IN_PAD, D_MODEL), full.dtype), full, (0, 0))
    return full


SMALL_SRC = ("lru", "ssd", "heads", "rows", "gate_a", "gate_x")
AG_HOSTS = {"in_proj": ("w_ff1",), "lru_fwd": ("w_ff2",), "ssd_conv_fwd": ("w_ple_gate", "w_ple"), "ssd_fwd": ("w_out",)}
PAIR_HOSTS = ("d_x2", "d_pre", "d_x1", "d_ycat")
CHIP_HOSTS = {"lru_bwd": ("w_ple_gate", "w_ple", "w_ff2"), "ssd_bwd": ("w_ff1",), "ssd_conv_bwd": ("w_out",),
              "d_x": ("w_in",)}
SMALL_HOSTS = {"ssd_bwd": ("lru", "gate_a", "gate_x"), "d_w_in_3": ("ssd", "heads", "rows")}


class _Schedule:
    def __init__(self, shards, cidx):
        self.shards, self.cidx = shards, cidx
        self.pair, self.chip, self.small_jobs = [], [], []
        self.dest, self.summed, self.gathered_small = {}, {}, {}
        self.tags = []

    def ride(self, host):
        tags = []
        if host in AG_HOSTS:
            tags = [("weight", n, self.shards[n]) for n in AG_HOSTS[host]]
        elif host in PAIR_HOSTS or host in CHIP_HOSTS or host == "flush":
            tags = [("pair", n, a) for n, a in self.pair]
            self.pair = []
            if host not in PAIR_HOSTS:
                take = [t for t in self.chip if host == "flush" or t[0] in CHIP_HOSTS[host]]
                tags += [("chip", n, a) for n, a in take]
                self.chip = [t for t in self.chip if not any(t is u for u in take)]
        if host in SMALL_HOSTS:
            tags += [("small", n, a) for n, a in self.small_jobs if n in SMALL_HOSTS[host]]
            self.small_jobs = [t for t in self.small_jobs if t[0] not in SMALL_HOSTS[host]]
        self.tags = tags
        return [_Job({"weight": "gather", "small": "gather"}.get(kind, kind), a) for kind, _n, a in tags]

    def done(self, jobs, outs, w):
        for (kind, n, _a), o in zip(self.tags, outs):
            if kind == "weight":
                w[n] = _full_weight(n, o)
            elif kind == "small":
                self.gathered_small[n] = o
            elif kind == "pair":
                self.chip.append((n, _pair_add(self.dest[n], o, self.cidx, name="rs_pair_add_" + n)))
            else:
                self.summed[n] = o

    def grad(self, name, val):
        self.dest[name] = val if val.ndim == 3 else _to_dest_major(name, val)
        self.pair.append((name, self.dest[name]))

    def small(self, raw):
        self.small_jobs += list(raw.items())

    def pairs_now(self):
        tags = [("pair", n, a) for n, a in self.pair]
        self.pair, self.tags = [], tags
        jobs = [_Job("pair", a) for _k, _n, a in tags]
        self.done(jobs, _exchange(jobs, name="rs_pairs_now"), None)

    def flush(self):
        step = 0
        while self.pair or self.chip:
            jobs = self.ride("flush")
            self.done(jobs, _exchange(jobs, name="rs_flush_%d" % step), None)
            step += 1


def kernel(x, p, w_in, lru_conv_w, lru_conv_b, lru_gate_a_w, lru_gate_a_b, lru_gate_x_w, lru_gate_x_b, lru_a_param, ssd_conv_w, ssd_conv_b, ssd_dt_bias, ssd_a_log, ssd_d, ssd_norm_w, w_out, ln1_g, ln1_b, w_ff1, w_ff2, ln2_g, ln2_b, w_ple_gate, w_ple, ln3_g, ln3_b, loss_target, m_w_in, m_lru_conv_w, m_lru_conv_b, m_lru_gate_a_w, m_lru_gate_a_b, m_lru_gate_x_w, m_lru_gate_x_b, m_lru_a_param, m_ssd_conv_w, m_ssd_conv_b, m_ssd_dt_bias, m_ssd_a_log, m_ssd_d, m_ssd_norm_w, m_w_out, m_ln1_g, m_ln1_b, m_w_ff1, m_w_ff2, m_ln2_g, m_ln2_b, m_w_ple_gate, m_w_ple, m_ln3_g, m_ln3_b, v_w_in, v_lru_conv_w, v_lru_conv_b, v_lru_gate_a_w, v_lru_gate_a_b, v_lru_gate_x_w, v_lru_gate_x_b, v_lru_a_param, v_ssd_conv_w, v_ssd_conv_b, v_ssd_dt_bias, v_ssd_a_log, v_ssd_d, v_ssd_norm_w, v_w_out, v_ln1_g, v_ln1_b, v_w_ff1, v_w_ff2, v_ln2_g, v_ln2_b, v_w_ple_gate, v_w_ple, v_ln3_g, v_ln3_b):
    given = dict(locals())
    def local(a, n):
        return jnp.swapaxes(a[0], 0, 1) if n == 'w_in' else a[0]

    wsh = {n: local(given[n], n) for n in WEIGHTS}
    msh = {n: local(given["m_" + n], n) for n in WEIGHTS}
    vsh = {n: local(given["v_" + n], n) for n in WEIGHTS}
    xi, yi, ci = _mesh_pos()
    me = 4 * xi + 2 * yi + ci

    shards = {n: wsh[n].astype(BF16) for n in BIG}
    conv_pack = jnp.concatenate([_pad_rows8(wsh[n]) for n in CONV], axis=1)
    g_in, gconv = _exchange([_Job("gather", shards['w_in']), _Job("gather", conv_pack)], name="ag_first")
    full = {'w_in_t': _full_weight('w_in', g_in)}
    c0 = 0
    for n in CONV:
        cw = CONV_CH[n] // N_DEV
        full[n] = gconv[:, :4, c0:c0 + cw].transpose(1, 0, 2).reshape(4, CONV_CH[n])
        c0 += cw
    for n in REPL:
        full[n] = given[n] if given[n].ndim == 2 else wsh[n]

    sched = _Schedule(shards, jnp.reshape(ci, (1,)).astype(jnp.int32))
    loss_local, grad_x, g, raw = _local_step(x[0], p[0, 0], loss_target[0], full, sched)
    sched.flush()
    summed, gat = sched.summed, sched.gathered_small
    loss = gat["rows"][0, 7, 0]
    for d in range(1, N_DEV):
        loss = loss + gat["rows"][d, 7, 0]

    outs = {}
    for n in BIG:
        outs[n] = _adamw(summed[n], wsh[n], msh[n], vsh[n], name="adamw_" + n)
    for n, k in (("lru_gate_a_w", "gate_a"), ("lru_gate_x_w", "gate_x")):
        flat = lambda a: a.reshape(N_HEAD * HEAD_P, HEAD_P)
        res = _adamw(gat[k], flat(wsh[n]), flat(msh[n]), flat(vsh[n]), name="adamw_" + n)
        outs[n] = tuple(r.reshape(N_HEAD, HEAD_P, HEAD_P) for r in res)
    row_items = [("lru_conv_b", 0, 4), ("lru_gate_a_b", 0, 5), ("lru_gate_x_b", 0, 6), ("lru_a_param", 0, 7),
                 ("ssd_conv_b", 1, 4), ("ssd_dt_bias", 2, 0), ("ssd_a_log", 2, 1), ("ssd_d", 2, 2),
                 ("ssd_norm_w", 3, 0), ("ln1_g", 3, 1), ("ln1_b", 3, 2), ("ln2_g", 3, 3), ("ln2_b", 3, 4),
                 ("ln3_g", 3, 5), ("ln3_b", 3, 6)]
    vec = lambda a: a.reshape(1, -1)
    items = [(si, r0, vec(given[n]), vec(given["m_" + n]), vec(given["v_" + n])) for n, si, r0 in row_items]
    own = [(si, 0, wsh[n], msh[n], vsh[n]) for n, si in (("lru_conv_w", 0), ("ssd_conv_w", 1))]
    me1 = jnp.reshape(me, (1,)).astype(jnp.int32)
    res = _adamw_rows([gat[k] for k in SMALL_SRC[:4]], items, own, me1, name="adamw_small")
    for (n, _si, _r0), r4 in zip(row_items, res[:len(row_items)]):
        outs[n] = r4
    for n, r4 in zip(CONV, res[len(row_items):]):
        outs[n] = r4

    def fin(n, k):
        a = jnp.swapaxes(outs[n][k], 0, 1) if n == 'w_in' else outs[n][k]
        return a.reshape(given[n].shape)

    return (loss, grad_x[None],
            *[fin(n, 0) for n in WEIGHTS], *[fin(n, 1) for n in WEIGHTS],
            *[fin(n, 2) for n in WEIGHTS], *[fin(n, 3) for n in WEIGHTS])
```

```python
import math

import jax
import jax.numpy as jnp
from jax import lax
from jax.experimental import pallas as pl
from jax.experimental.pallas import tpu as pltpu

F32 = jnp.float32
BF16 = jnp.bfloat16
HI = lax.Precision.HIGHEST

N_DEV = 8
D_MODEL = 1024
LRU_W = 1024
SSD_W = 1024
XBC = 2048
N_HEAD = 16
HEAD_P = 64
N_GROUP = 4
GROUP_W = 256
N_STATE = 128
CHUNK = 128
D_FF = 4096
PLE_DIM = 256
D_IN = 5136
D_IN_PAD = 5632
COL_G = 1024
COL_Z = 2048
COL_XBC = 3072
COL_DT = 5120
LRU_C = 8.0
ALPHA = 2.0 ** 0.25
LN_EPS = 1e-5
RMS_EPS = 1e-5
ADAM_LR = 0.001
ADAM_B1 = 0.9
ADAM_B2 = 0.999
ADAM_EPS = 1e-08
ADAM_WD = 0.01
ADAM_STEP = 10
GELU_C = math.sqrt(2.0 / math.pi)
LANE = 128
SUBLANE = 8
VMEM_LIMIT = 48 * 1024 * 1024
MESH_T = pl.DeviceIdType.MESH
NEG_BIG = -1e30


def _pcall(body, **kw):
    return pl.pallas_call(body, **kw)


def _cparams(sem):
    return pltpu.CompilerParams(dimension_semantics=sem, vmem_limit_bytes=VMEM_LIMIT)


def _dot(a, b):
    return jnp.dot(a.astype(BF16), b.astype(BF16), preferred_element_type=F32)


def _dot_nt(a, b):
    return lax.dot_general(a.astype(BF16), b.astype(BF16), (((1,), (1,)), ((), ())), preferred_element_type=F32)


def _dot_tn(a, b):
    return lax.dot_general(a.astype(BF16), b.astype(BF16), (((0,), (0,)), ((), ())), preferred_element_type=F32)


def _dotx(a, b):
    return jnp.dot(a, b, precision=HI, preferred_element_type=F32)


def _sigmoid(x):
    return jax.nn.sigmoid(x)


def _softplus(v):
    return jnp.maximum(v, 0.0) + jnp.log1p(jnp.exp(-jnp.abs(v)))


def _gelu(x):
    th = jnp.tanh(GELU_C * (x + 0.044715 * x * x * x))
    return 0.5 * x * (1.0 + th), th


def _gelu_grad(x, th):
    return 0.5 * (1.0 + th) + 0.5 * x * (1.0 - th * th) * GELU_C * (1.0 + 3.0 * 0.044715 * x * x)


def _iota(shape, dim):
    return lax.broadcasted_iota(jnp.int32, shape, dim)


def _mm(a, b, mode, *, tm, tn, name, a_fn=None, extra=None, epi=None, out_dtype=F32, dest_major=False, into=None,
        jobs=()):
    m = a.shape[1] if mode == "tn" else a.shape[0]
    n = b.shape[0] if mode == "nt" else b.shape[1]
    tm, tn = min(tm, m), min(tn, n)
    if dest_major:
        tn = n // N_DEV
    if mode == "nn":
        m, k = a.shape
        _, n = b.shape
        a_spec = pl.BlockSpec((tm, k), lambda i, j: (i, 0))
        b_spec = pl.BlockSpec((k, tn), lambda i, j: (0, j))
        dims = ((1,), (0,))
    elif mode == "nt":
        m, k = a.shape
        n, _ = b.shape
        a_spec = pl.BlockSpec((tm, k), lambda i, j: (i, 0))
        b_spec = pl.BlockSpec((tn, k), lambda i, j: (j, 0))
        dims = ((1,), (1,))
    else:
        k, m = a.shape
        _, n = b.shape
        a_spec = pl.BlockSpec((k, tm), lambda i, j: (0, i))
        b_spec = pl.BlockSpec((k, tn), lambda i, j: (0, j))
        dims = ((0,), (0,))
    assert m % tm == 0 and n % tn == 0, (name, m, n, tm, tn)
    o_spec = pl.BlockSpec((tm, tn), lambda i, j: (i, j))
    in_specs = [a_spec, b_spec]
    args = [a, b]
    if extra is not None:
        in_specs.append(o_spec)
        args.append(extra)

    def body(*refs):
        a_ref, b_ref, o_ref = refs[0], refs[1], refs[-1]
        av = a_ref[...]
        if a_fn is not None:
            av = a_fn(av)
        acc = lax.dot_general(av.astype(BF16), b_ref[...].astype(BF16), (dims, ((), ())), preferred_element_type=F32)
        if epi is not None:
            acc = epi(acc, refs[2][...])
        o_ref[...] = acc.astype(out_dtype)

    out_shape = jax.ShapeDtypeStruct((m, n), out_dtype)
    aliases = None
    if dest_major:
        assert extra is None
        o_spec = pl.BlockSpec((None, tm, tn), lambda i, j: (j, i, 0))
        out_shape = jax.ShapeDtypeStruct((N_DEV, m, tn), out_dtype)
    if into is not None:
        buf, row0, total = into
        assert extra is None and row0 % tm == 0
        o_spec = pl.BlockSpec((tm, tn), lambda i, j: (row0 // tm + i, j))
        out_shape = jax.ShapeDtypeStruct((total, n), out_dtype)
        if buf is not None:
            in_specs.append(ANY_SPEC)
            args.append(buf)
            aliases = {len(args) - 1: 0}
    (out,), jouts = _hosted(body, jobs, grid=(m // tm, n // tn), in_specs=in_specs, out_specs=[o_spec],
                            out_shape=[out_shape], args=args, name=name, aliases=aliases)
    return (out, jouts) if jobs else out


def _mm_pieces(pieces, offsets, b, *, tm, name, extra, epi, jobs=()):
    m = pieces[0].shape[0]
    kb, n = b.shape
    tm = min(tm, m)
    row = lambda wdt: pl.BlockSpec((tm, wdt), lambda i: (i, 0))
    in_specs = [row(pc.shape[1]) for pc in pieces] + [pl.BlockSpec((kb, n), lambda i: (0, 0)), row(n)]
    np_ = len(pieces)

    def body(*refs):
        b_ref, e_ref, o_ref = refs[np_], refs[np_ + 1], refs[np_ + 2]
        acc = jnp.zeros((tm, n), F32)
        for q in range(np_):
            kq = pieces[q].shape[1]
            acc = acc + jnp.dot(refs[q][...].astype(BF16), b_ref[offsets[q]:offsets[q] + kq, :].astype(BF16),
                                preferred_element_type=F32)
        o_ref[...] = epi(acc, e_ref[...])

    (out,), jouts = _hosted(body, jobs, grid=(m // tm,), in_specs=in_specs, out_specs=[row(n)],
                            out_shape=[jax.ShapeDtypeStruct((m, n), F32)], args=list(pieces) + [b, extra], name=name)
    return (out, jouts) if jobs else out


def _relu2(v):
    r = jnp.maximum(v, 0.0)
    return r * r


ROW_TILE = 256
MULTI_TILE = 128


def _ln_stats(t):
    mu = jnp.mean(t, axis=-1, keepdims=True)
    xc = t - mu
    var = jnp.mean(xc * xc, axis=-1, keepdims=True)
    rstd = lax.rsqrt(var + LN_EPS)
    return xc * rstd, rstd


def _ln_bwd_rows(dy, xhat, rstd, g):
    dxh = dy * g
    m1 = jnp.mean(dxh, axis=-1, keepdims=True)
    m2 = jnp.mean(dxh * xhat, axis=-1, keepdims=True)
    return rstd * (dxh - m1 - xhat * m2)


def _mm_ln(a, b, res, g, beta, *, tm, name, a_fn=None):
    m, k = a.shape
    d = b.shape[1]
    tm = min(tm, m)
    row = pl.BlockSpec((tm, d), lambda i: (i, 0))
    par = pl.BlockSpec((1, d), lambda i: (0, 0))

    def body(a_ref, b_ref, r_ref, g_ref, be_ref, br_ref, y_ref, yb_ref):
        av = a_ref[...]
        if a_fn is not None:
            av = a_fn(av)
        acc = jnp.dot(av.astype(BF16), b_ref[...].astype(BF16), preferred_element_type=F32)
        br_ref[...] = acc
        xhat, _ = _ln_stats(ALPHA * r_ref[...] + acc)
        y = xhat * g_ref[...] + be_ref[...]
        y_ref[...] = y
        yb_ref[...] = y.astype(BF16)

    sd = jax.ShapeDtypeStruct((m, d), F32)
    return _pcall(body, grid=(m // tm,),
                  in_specs=[pl.BlockSpec((tm, k), lambda i: (i, 0)), pl.BlockSpec((k, d), lambda i: (0, 0)), row, par, par],
                  out_specs=(row, row, row), out_shape=(sd, sd, jax.ShapeDtypeStruct((m, d), BF16)), name=name,
                  compiler_params=_cparams(("parallel",)))(a, b, res, g, beta)


def _mm_ln_bwd(a, b, res, branch, g, dy0, coef0, *, tm, name, jobs=()):
    m, k = a.shape
    d = b.shape[0]
    tm = min(tm, m)
    row = pl.BlockSpec((tm, d), lambda i: (i, 0))
    par = pl.BlockSpec((1, d), lambda i: (0, 0))

    def body(a_ref, b_ref, r_ref, br_ref, g_ref, dy0_ref, dt_ref, dtb_ref, dg_ref, db_ref):
        acc = lax.dot_general(a_ref[...].astype(BF16), b_ref[...].astype(BF16), (((1,), (1,)), ((), ())),
                              preferred_element_type=F32)
        dy = coef0 * dy0_ref[...] + acc
        xhat, rstd = _ln_stats(ALPHA * r_ref[...] + br_ref[...])
        dt = _ln_bwd_rows(dy, xhat, rstd, g_ref[...])
        dt_ref[...] = dt
        dtb_ref[...] = dt.astype(BF16)

        @pl.when(pl.program_id(0) == 0)
        def _():
            dg_ref[...] = jnp.zeros_like(dg_ref)
            db_ref[...] = jnp.zeros_like(db_ref)

        dg_ref[...] += jnp.sum(dy * xhat, axis=0, keepdims=True)
        db_ref[...] += jnp.sum(dy, axis=0, keepdims=True)

    pd = jax.ShapeDtypeStruct((1, d), F32)
    outs, jouts = _hosted(
        body, jobs, grid=(m // tm,),
        in_specs=[pl.BlockSpec((tm, k), lambda i: (i, 0)), pl.BlockSpec((d, k), lambda i: (0, 0)), row, row, par, row],
        out_specs=(row, row, par, par),
        out_shape=(jax.ShapeDtypeStruct((m, d), F32), jax.ShapeDtypeStruct((m, d), BF16), pd, pd),
        args=(a, b, res, branch, g, dy0), name=name)
    return (tuple(outs), jouts) if jobs else tuple(outs)


def _head(x2, x2b, p, wg, wp, g, beta, tgt, *, name):
    s, d = x2.shape
    tile = 2 * ROW_TILE
    row = pl.BlockSpec((tile, d), lambda i: (i, 0))
    par = pl.BlockSpec((1, d), lambda i: (0, 0))
    lsp = pl.BlockSpec((1, LANE), lambda i: (0, 0))
    whole = lambda a: pl.BlockSpec(a.shape, lambda i: (0, 0))

    def body(x2_ref, x2b_ref, p_ref, wg_ref, wp_ref, g_ref, be_ref, t_ref,
             loss_ref, dgp_ref, dple_ref, dt_ref, dg_ref, db_ref):
        gate = _sigmoid(_dot(x2b_ref[...], wg_ref[...]))
        ple_v = _dot(p_ref[...], wp_ref[...])
        xhat, rstd = _ln_stats(ALPHA * x2_ref[...] + gate * ple_v)
        err = xhat * g_ref[...] + be_ref[...] - t_ref[...]
        dy = err * (1.0 / d)
        dt = _ln_bwd_rows(dy, xhat, rstd, g_ref[...])
        dt_ref[...] = dt
        dgp_ref[...] = (dt * ple_v * gate * (1.0 - gate)).astype(BF16)
        dple_ref[...] = (dt * gate).astype(BF16)

        @pl.when(pl.program_id(0) == 0)
        def _():
            loss_ref[...] = jnp.zeros_like(loss_ref)
            dg_ref[...] = jnp.zeros_like(dg_ref)
            db_ref[...] = jnp.zeros_like(db_ref)

        loss_ref[...] += 0.5 * jnp.sum(jnp.mean(err * err, axis=-1, keepdims=True))
        dg_ref[...] += jnp.sum(dy * xhat, axis=0, keepdims=True)
        db_ref[...] += jnp.sum(dy, axis=0, keepdims=True)

    sd = jax.ShapeDtypeStruct((s, d), F32)
    sb = jax.ShapeDtypeStruct((s, d), BF16)
    pd = jax.ShapeDtypeStruct((1, d), F32)
    return _pcall(body, grid=(s // tile,),
                  in_specs=[row, row, pl.BlockSpec((tile, p.shape[1]), lambda i: (i, 0)), whole(wg), whole(wp), par, par,
                            row],
                  out_specs=(lsp, row, row, row, par, par),
                  out_shape=(jax.ShapeDtypeStruct((1, LANE), F32), sb, sb, sd, pd, pd),
                  name=name, compiler_params=_cparams(("arbitrary",)))(x2, x2b, p, wg, wp, g, beta, tgt)


CONV_R = 256
PAD = SUBLANE


def _shift_down(ext, s):
    if s == 0:
        return ext[PAD:, :]
    return pltpu.roll(ext, s, 0)[PAD:, :]


def _shift_up(ext, s):
    r = ext.shape[0] - PAD
    if s == 0:
        return ext[:r, :]
    return pltpu.roll(ext, r + PAD - s, 0)[:r, :]


def _conv_rows(xpad_ref, r0, w_ref):
    ext = xpad_ref[pl.ds(r0, CONV_R + PAD), :]
    acc = _shift_down(ext, 0) * w_ref[3:4, :]
    for k in range(3):
        acc = acc + _shift_down(ext, 3 - k) * w_ref[k:k + 1, :]
    return acc, ext


def _fill_front_padded(dst_ref, src_ref, s):
    dst_ref[0:PAD, :] = jnp.zeros((PAD, dst_ref.shape[1]), F32)

    def cp(q, _):
        r0 = pl.multiple_of(q * CONV_R, CONV_R)
        dst_ref[pl.ds(pl.multiple_of(PAD + r0, PAD), CONV_R), :] = src_ref[pl.ds(r0, CONV_R), :]
        return 0

    lax.fori_loop(0, s // CONV_R, cp, 0)


def _conv_silu_fwd(proj, w8, b, *, col0, width, ct, name, jobs=()):
    s = proj.shape[0]
    nb = col0 // ct

    def body(x_ref, w_ref, b_ref, o_ref, xpad):
        _fill_front_padded(xpad, x_ref, s)

        def step(q, _):
            r0 = pl.multiple_of(q * CONV_R, CONV_R)
            acc, _e = _conv_rows(xpad, r0, w_ref)
            pre = acc + b_ref[...]
            o_ref[pl.ds(r0, CONV_R), :] = pre * _sigmoid(pre)
            return 0

        lax.fori_loop(0, s // CONV_R, step, 0)

    (out,), jouts = _hosted(
        body, jobs, grid=(width // ct,),
        in_specs=[pl.BlockSpec((s, ct), lambda j: (0, nb + j)), pl.BlockSpec((SUBLANE, ct), lambda j: (0, j)),
                  pl.BlockSpec((1, ct), lambda j: (0, j))],
        out_specs=[pl.BlockSpec((s, ct), lambda j: (0, j))],
        out_shape=[jax.ShapeDtypeStruct((s, width), F32)],
        scratch_shapes=[pltpu.VMEM((s + PAD, ct), F32)], name=name, args=(proj, w8, b))
    return (out, jouts) if jobs else out


def _conv_bwd_rows(dpad_ref, r0, w_ref):
    return _conv_bwd_ext(dpad_ref[pl.ds(r0, CONV_R + PAD), :], w_ref)


def _conv_bwd_ext(ext, w_ref):
    acc = _shift_up(ext, 0) * w_ref[3:4, :]
    for k in range(3):
        acc = acc + _shift_up(ext, 3 - k) * w_ref[k:k + 1, :]
    return acc


def _conv_silu_bwd(proj, dact, w8, b, *, col0, width, ct, name, jobs=()):
    s = proj.shape[0]
    nb = col0 // ct

    def body(x_ref, d_ref, w_ref, b_ref, dx_ref, dwb_ref, xpad, dpad):
        _fill_front_padded(xpad, x_ref, s)
        dpad[pl.ds(s, PAD), :] = jnp.zeros((PAD, ct), F32)
        dwb_ref[...] = jnp.zeros_like(dwb_ref)

        def step(q, _):
            r0 = pl.multiple_of(q * CONV_R, CONV_R)
            acc, ext = _conv_rows(xpad, r0, w_ref)
            pre = acc + b_ref[...]
            sg = _sigmoid(pre)
            dpre = d_ref[pl.ds(r0, CONV_R), :] * sg * (1.0 + pre * (1.0 - sg))
            dpad[pl.ds(r0, CONV_R), :] = dpre
            for k in range(4):
                dwb_ref[k:k + 1, :] += jnp.sum(dpre * _shift_down(ext, 3 - k), axis=0, keepdims=True)
            dwb_ref[4:5, :] += jnp.sum(dpre, axis=0, keepdims=True)
            return 0

        lax.fori_loop(0, s // CONV_R, step, 0)

        def step2(q, _):
            r0 = pl.multiple_of(q * CONV_R, CONV_R)
            dx_ref[pl.ds(r0, CONV_R), :] = _conv_bwd_rows(dpad, r0, w_ref).astype(BF16)
            return 0

        lax.fori_loop(0, s // CONV_R, step2, 0)

    colb = pl.BlockSpec((s, ct), lambda j: (0, j))
    outs, jouts = _hosted(
        body, jobs, grid=(width // ct,),
        in_specs=[pl.BlockSpec((s, ct), lambda j: (0, nb + j)), colb, pl.BlockSpec((SUBLANE, ct), lambda j: (0, j)),
                  pl.BlockSpec((1, ct), lambda j: (0, j))],
        out_specs=(colb, pl.BlockSpec((SUBLANE, ct), lambda j: (0, j))),
        out_shape=(jax.ShapeDtypeStruct((s, width), BF16), jax.ShapeDtypeStruct((SUBLANE, width), F32)),
        scratch_shapes=[pltpu.VMEM((s + PAD, ct), F32), pltpu.VMEM((s + PAD, ct), F32)], name=name,
        args=(proj, dact, w8, b))
    return (tuple(outs), jouts) if jobs else tuple(outs)


LRU_CT = 128


def _row_of(v, r):
    return jnp.sum(jnp.where(_iota((v.shape[0], 1), 0) == r, v, 0.0), axis=0, keepdims=True)


def _scan_fwd(a, u):
    r = a.shape[0]
    row = _iota((r, 1), 0)
    d = 1
    while d < r:
        valid = row >= d
        u = jnp.where(valid, a * pltpu.roll(u, d, 0) + u, u)
        a = jnp.where(valid, a * pltpu.roll(a, d, 0), a)
        d *= 2
    return a, u


def _scan_rev(b, u):
    r = b.shape[0]
    row = _iota((r, 1), 0)
    d = 1
    while d < r:
        valid = row < r - d
        u = jnp.where(valid, b * pltpu.roll(u, r - d, 0) + u, u)
        b = jnp.where(valid, b * pltpu.roll(b, r - d, 0), b)
        d *= 2
    return b, u


def _lru_chunk(xpad, r0, cw_ref, cb, wa, ba, wx, bx, sp):
    acc, ext = _conv_rows(xpad, r0, cw_ref)
    xl = acc + cb
    r = _sigmoid(_dot(xl, wa) + ba)
    i = _sigmoid(_dot(xl, wx) + bx)
    la = -LRU_C * r * sp
    a = jnp.exp(la)
    a2 = jnp.exp(2.0 * la)
    mult = jnp.sqrt(-jnp.tanh(la) * (a2 + 1.0))
    first = (r0 + _iota((CONV_R, 1), 0)) == 0
    mult = jnp.where(first, 1.0, mult)
    return ext, xl, r, i, a, a2, mult, first


def _lru_specs(s):
    ct = LRU_CT
    nb_g = COL_G // ct
    return dict(
        x=pl.BlockSpec((s, ct), lambda j: (0, j)),
        g=pl.BlockSpec((s, ct), lambda j: (0, nb_g + j)),
        col=pl.BlockSpec((s, ct), lambda j: (0, j)),
        cw=pl.BlockSpec((SUBLANE, ct), lambda j: (0, j)),
        vec=pl.BlockSpec((1, ct), lambda j: (0, j)),
        gate=pl.BlockSpec((None, ct, ct), lambda j: (j, 0, 0)),
    )


def _lru_fwd(proj, cw8, cb, wa_bd, ba, wx_bd, bx, ap, *, name, jobs=()):
    s = proj.shape[0]
    ct = LRU_CT
    sp_ = _lru_specs(s)

    def body(x_ref, g_ref, cw_ref, cb_ref, wa_ref, ba_ref, wx_ref, bx_ref, ap_ref, y_ref, h_ref, xpad):
        _fill_front_padded(xpad, x_ref, s)
        sp = _softplus(-ap_ref[...])

        def step(q, carry):
            r0 = pl.multiple_of(q * CONV_R, CONV_R)
            _e, xl, _r, i, a, _a2, mult, _f = _lru_chunk(xpad, r0, cw_ref, cb_ref[...], wa_ref[...], ba_ref[...],
                                                       wx_ref[...], bx_ref[...], sp)
            acum, ucum = _scan_fwd(a, xl * i * mult)
            h = acum * carry + ucum
            h_ref[pl.ds(r0, CONV_R), :] = h
            ge, _th = _gelu(g_ref[pl.ds(r0, CONV_R), :])
            y_ref[pl.ds(r0, CONV_R), :] = (ge * h).astype(BF16)
            return _row_of(h, CONV_R - 1)

        lax.fori_loop(0, s // CONV_R, step, jnp.zeros((1, ct), F32))

    (ymix, hs), jouts = _hosted(
        body, jobs, grid=(LRU_W // ct,),
        in_specs=[sp_["x"], sp_["g"], sp_["cw"], sp_["vec"], sp_["gate"], sp_["vec"], sp_["gate"], sp_["vec"], sp_["vec"]],
        out_specs=(sp_["col"], sp_["col"]),
        out_shape=(jax.ShapeDtypeStruct((s, LRU_W + SSD_W), BF16), jax.ShapeDtypeStruct((s, LRU_W), F32)),
        scratch_shapes=[pltpu.VMEM((s + PAD, ct), F32)],
        name=name, args=(proj, proj, cw8, cb, wa_bd, ba, wx_bd, bx, ap))
    return ((ymix, hs), jouts) if jobs else (ymix, hs)


def _lru_bwd(proj, dy, hs, cw8, cb, wa_bd, ba, wx_bd, bx, ap, *, name, jobs=()):
    s = proj.shape[0]
    ct = LRU_CT
    sp_ = _lru_specs(s)

    nq = s // CONV_R

    def body(x_ref, g_ref, dy_ref, h_ref, cw_ref, cb_ref, wa_ref, ba_ref, wx_ref, bx_ref, ap_ref,
             dx_ref, dg_ref, dcwb_ref, dwa_ref, dwx_ref, xpad, hpad):
        _fill_front_padded(xpad, x_ref, s)
        _fill_front_padded(hpad, h_ref, s)
        apv = ap_ref[...]
        sp = _softplus(-apv)
        cb_v, wa, ba_v, wx, bx_v = cb_ref[...], wa_ref[...], ba_ref[...], wx_ref[...], bx_ref[...]
        dcwb_ref[...] = jnp.zeros_like(dcwb_ref)
        dwa_ref[...] = jnp.zeros_like(dwa_ref)
        dwx_ref[...] = jnp.zeros_like(dwx_ref)

        def back(k, carry):
            g_next, a_next, dxl_next = carry
            last_row = _iota((CONV_R, 1), 0) == CONV_R - 1
            r0 = pl.multiple_of((nq - 1 - k) * CONV_R, CONV_R)
            ext, xl, r, i, a, a2, mult, first = _lru_chunk(xpad, r0, cw_ref, cb_v, wa, ba_v, wx, bx_v, sp)
            gv = g_ref[pl.ds(r0, CONV_R), :]
            dyv = dy_ref[pl.ds(r0, CONV_R), :]
            hext = hpad[pl.ds(r0, CONV_R + PAD), :]
            ge, th = _gelu(gv)
            dg_ref[pl.ds(r0, CONV_R), :] = (dyv * _shift_down(hext, 0) * _gelu_grad(gv, th)).astype(BF16)
            b = jnp.where(last_row, a_next, pltpu.roll(a, CONV_R - 1, 0))
            bcum, dcum = _scan_rev(b, dyv * ge)
            gval = dcum + bcum * g_next
            hprev = _shift_down(hext, 1)
            da = gval * hprev
            dxl = gval * i * mult
            di = gval * xl * mult
            dmult = jnp.where(first, 0.0, gval * xl * i)
            dla = da * a - dmult * a2 / mult
            dr = dla * (-LRU_C) * sp
            dcwb_ref[7:8, :] += jnp.sum(dla * (-LRU_C) * r, axis=0, keepdims=True)
            dpr = dr * r * (1.0 - r)
            dpi = di * i * (1.0 - i)
            dxl = dxl + _dot_nt(dpr, wa) + _dot_nt(dpi, wx)
            dwa_ref[...] += _dot_tn(xl, dpr)
            dwx_ref[...] += _dot_tn(xl, dpi)
            dcwb_ref[5:6, :] += jnp.sum(dpr, axis=0, keepdims=True)
            dcwb_ref[6:7, :] += jnp.sum(dpi, axis=0, keepdims=True)
            for tap in range(4):
                dcwb_ref[tap:tap + 1, :] += jnp.sum(dxl * _shift_down(ext, 3 - tap), axis=0, keepdims=True)
            dcwb_ref[4:5, :] += jnp.sum(dxl, axis=0, keepdims=True)
            dx_ref[pl.ds(r0, CONV_R), :] = _conv_bwd_ext(jnp.concatenate([dxl, dxl_next], axis=0), cw_ref).astype(BF16)
            return _row_of(gval, 0), _row_of(a, 0), dxl[:PAD, :]

        zero = jnp.zeros((1, ct), F32)
        lax.fori_loop(0, nq, back, (zero, zero, jnp.zeros((PAD, ct), F32)))
        dcwb_ref[7:8, :] = dcwb_ref[7:8, :] * (-_sigmoid(-apv))

    nt = LRU_W // ct
    outs, jouts = _hosted(
        body, jobs, grid=(nt,),
        in_specs=[sp_["x"], sp_["g"], sp_["col"], sp_["col"], sp_["cw"], sp_["vec"], sp_["gate"], sp_["vec"], sp_["gate"],
                  sp_["vec"], sp_["vec"]],
        out_specs=(sp_["col"], sp_["col"], sp_["cw"], sp_["gate"], sp_["gate"]),
        out_shape=(jax.ShapeDtypeStruct((s, LRU_W), BF16), jax.ShapeDtypeStruct((s, LRU_W), BF16),
                   jax.ShapeDtypeStruct((SUBLANE, LRU_W), F32), jax.ShapeDtypeStruct((nt, ct, ct), F32),
                   jax.ShapeDtypeStruct((nt, ct, ct), F32)),
        scratch_shapes=[pltpu.VMEM((s + PAD, ct), F32), pltpu.VMEM((s + PAD, ct), F32)],
        name=name, args=(proj, proj, dy, hs, cw8, cb, wa_bd, ba, wx_bd, bx, ap))
    return (tuple(outs), jouts) if jobs else tuple(outs)


def _split3(v):
    hi = v.astype(BF16)
    r1 = v - hi.astype(F32)
    mid = r1.astype(BF16)
    lo = (r1 - mid.astype(F32)).astype(BF16)
    return hi, mid, lo


def _dot01(m01, v):
    mb = m01.astype(BF16)
    hi, mid, lo = _split3(v)
    f = lambda part: jnp.dot(mb, part, preferred_element_type=F32)
    return f(hi) + f(mid) + f(lo)


def _dot01_r(v, m01):
    mb = m01.astype(BF16)
    hi, mid, lo = _split3(v)
    f = lambda part: jnp.dot(part, mb, preferred_element_type=F32)
    return f(hi) + f(mid) + f(lo)


def _ssd_prep(dtr, bias, alog_pad):
    l = CHUNK
    lane = _iota((1, LANE), 1)
    a_head = jnp.where(lane < N_HEAD, -jnp.exp(alog_pad), 0.0)
    dt = _softplus(dtr + bias)
    tril = (_iota((l, l), 1) <= _iota((l, l), 0)).astype(F32)
    a = dt * a_head
    cs = _dot01(tril, a)
    tot = jnp.sum(a, axis=0, keepdims=True)
    return dict(a_head=a_head, dt=dt, tril=tril, cs=cs, tot=tot)


def _col(v, h):
    lane = _iota(v.shape, 1)
    return jnp.sum(jnp.where(lane == h, v, 0.0), axis=1, keepdims=True)


def _decay_mat(cs, cst_ref, h, causal):
    row = cst_ref[h:h + 1, :]
    return jnp.exp(jnp.where(causal, _col(cs, h) - row, NEG_BIG))


def _head_mask(j, rows=CHUNK):
    lane = _iota((rows, GROUP_W), 1)
    return (lane >= j * HEAD_P) & (lane < (j + 1) * HEAD_P)


def _over_heads(v, g):
    r = v.shape[0]
    out = jnp.zeros((r, GROUP_W), F32)
    for j in range(4):
        out = jnp.where(_head_mask(j, r), _col(v, 4 * g + j), out)
    return out


def _ssd_group_fwd(q, g, xs_g, bg, cg, ht_g, cst_ref, causal, dx_g):
    dtx_g, csx_g, totx_g = _over_heads(q["dt"], g), _over_heads(q["cs"], g), _over_heads(q["tot"], g)
    xdt = xs_g * dtx_g
    ex = jnp.exp(csx_g)
    cb = _dot_nt(cg, bg)
    yoff = _dot(cg, ht_g) * ex
    ydiag = jnp.zeros((CHUNK, GROUP_W), F32)
    for j in range(4):
        sc = cb * _decay_mat(q["cs"], cst_ref, 4 * g + j, causal)
        ydiag = jnp.where(_head_mask(j), _dot(sc, xdt), ydiag)
    y = ydiag + yoff + xs_g * dx_g
    dsx = jnp.exp(totx_g - csx_g)
    return y, dict(xdt=xdt, ex=ex, cb=cb, yoff=yoff, dsx=dsx, dtx=dtx_g, totx=totx_g)


def _gated_norm_fwd(y_g, z_g, w_g):
    sz = _sigmoid(z_g)
    silu = z_g * sz
    yf = y_g * silu
    rs = lax.rsqrt(jnp.mean(yf * yf, axis=1, keepdims=True) + RMS_EPS)
    yn = yf * rs
    return yn * w_g, (sz, silu, rs, yn)


def _ssd_fwd(xact, proj, ymix, bias_pad, alog_pad, dxp, normw, *, name, jobs=()):
    s = xact.shape[0]
    nc = s // CHUNK

    def body(xa_ref, dt_ref, z_ref, _ymix_ref, bias_ref, alp_ref, dx_ref, nw_ref, y_ref, hp_ref, ht, cst):
        @pl.when(pl.program_id(0) == 0)
        def _():
            ht[...] = jnp.zeros_like(ht)

        hp_ref[...] = ht[...]
        q = _ssd_prep(dt_ref[...], bias_ref[...], alp_ref[...])
        cst[...] = q["cs"].T
        causal = q["tril"] > 0.0
        for g in range(N_GROUP):
            sl = slice(g * GROUP_W, (g + 1) * GROUP_W)
            xs_g = xa_ref[:, sl]
            bg = xa_ref[:, SSD_W + g * N_STATE:SSD_W + (g + 1) * N_STATE]
            cg = xa_ref[:, SSD_W + N_GROUP * N_STATE + g * N_STATE:SSD_W + N_GROUP * N_STATE + (g + 1) * N_STATE]
            ht_g = ht[:, sl]
            y, f = _ssd_group_fwd(q, g, xs_g, bg, cg, ht_g, cst, causal, dx_ref[:, sl])
            out, _ = _gated_norm_fwd(y, z_ref[:, sl], nw_ref[:, sl])
            y_ref[:, sl] = out.astype(BF16)
            ht[:, sl] = jnp.exp(f["totx"]) * ht_g + _dot_tn(bg, f["xdt"] * f["dsx"])

    par = lambda w: pl.BlockSpec((1, w), lambda c: (0, 0))
    (ycat, hprev), jouts = _hosted(
        body, jobs, grid=(nc,),
        in_specs=[pl.BlockSpec((CHUNK, XBC), lambda c: (c, 0)),
                  pl.BlockSpec((CHUNK, LANE), lambda c: (c, COL_DT // LANE)),
                  pl.BlockSpec((CHUNK, SSD_W), lambda c: (c, COL_Z // SSD_W)),
                  ANY_SPEC, par(LANE), par(LANE), par(SSD_W), par(SSD_W)],
        out_specs=(pl.BlockSpec((CHUNK, SSD_W), lambda c: (c, LRU_W // SSD_W)),
                   pl.BlockSpec((None, N_STATE, SSD_W), lambda c: (c, 0, 0))),
        out_shape=(jax.ShapeDtypeStruct(ymix.shape, ymix.dtype), jax.ShapeDtypeStruct((nc, N_STATE, SSD_W), F32)),
        scratch_shapes=[pltpu.VMEM((N_STATE, SSD_W), F32), pltpu.VMEM((CHUNK, LANE), F32)],
        aliases={3: 0}, name=name, args=(xact, proj, proj, ymix, bias_pad, alog_pad, dxp, normw))
    return ((ycat, hprev), jouts) if jobs else (ycat, hprev)


def _ssd_bwd(xact, proj, dycat, hprev, bias_pad, alog_pad, dxp, normw, *, name, jobs=()):
    s = xact.shape[0]
    nc = s // CHUNK
    l = CHUNK

    def body(xa_ref, dt_ref, z_ref, dy_ref, hp_ref, bias_ref, alp_ref, dx_ref, nw_ref,
             dxa_ref, ddt_ref, dz_ref, dnw_ref, small_ref, dht, cst, accx, dcsx_s, ddtx_s):
        step = pl.program_id(0)

        @pl.when(step == 0)
        def _():
            dht[...] = jnp.zeros_like(dht)
            accx[...] = jnp.zeros_like(accx)
            dnw_ref[...] = jnp.zeros_like(dnw_ref)
            small_ref[...] = jnp.zeros_like(small_ref)

        dtr = dt_ref[...]
        q = _ssd_prep(dtr, bias_ref[...], alp_ref[...])
        cst[...] = q["cs"].T
        causal = q["tril"] > 0.0
        eye = _iota((l, l), 0) == _iota((l, l), 1)
        lane = _iota((l, LANE), 1)
        dcs_head = jnp.zeros((l, LANE), F32)
        for g in range(N_GROUP):
            sl = slice(g * GROUP_W, (g + 1) * GROUP_W)
            slb = slice(SSD_W + g * N_STATE, SSD_W + (g + 1) * N_STATE)
            slc = slice(SSD_W + N_GROUP * N_STATE + g * N_STATE, SSD_W + N_GROUP * N_STATE + (g + 1) * N_STATE)
            xs_g, bg, cg = xa_ref[:, sl], xa_ref[:, slb], xa_ref[:, slc]
            ht_g = hp_ref[:, sl]
            dxp_g = dx_ref[:, sl]
            y, f = _ssd_group_fwd(q, g, xs_g, bg, cg, ht_g, cst, causal, dxp_g)
            z_g, nw_g = z_ref[:, sl], nw_ref[:, sl]
            _o, (sz, silu, rs, yn) = _gated_norm_fwd(y, z_g, nw_g)
            dout = dy_ref[:, sl]
            dnw_ref[:, sl] += jnp.sum(dout * yn, axis=0, keepdims=True)
            dyn = dout * nw_g
            dyf = rs * (dyn - yn * jnp.mean(dyn * yn, axis=1, keepdims=True))
            dy = dyf * silu
            dz_ref[:, sl] = (dyf * y * sz * (1.0 + z_g * (1.0 - sz))).astype(BF16)
            accx[0:1, sl] += jnp.sum(dy * xs_g, axis=0, keepdims=True)
            dyo = dy * f["ex"]
            dcg = _dot_nt(dyo, ht_g)
            dht_prev = _dot_tn(cg, dyo)
            dcsx = dy * f["yoff"]
            xdt = f["xdt"]
            dxdt = jnp.zeros((l, GROUP_W), F32)
            dcb = jnp.zeros((l, l), F32)
            for j in range(4):
                h = 4 * g + j
                lm = _decay_mat(q["cs"], cst, h, causal)
                sc = f["cb"] * lm
                mask = _head_mask(j)
                ds_ = jnp.where(causal, _dot_nt(jnp.where(mask, dy, 0.0), xdt), 0.0)
                dxdt = jnp.where(mask, _dot_tn(sc, dy), dxdt)
                dcb = dcb + ds_ * lm
                m = ds_ * sc
                rsum = jnp.sum(m, axis=1, keepdims=True)
                csum = jnp.sum(m, axis=0, keepdims=True)
                csum_col = jnp.sum(jnp.where(eye, csum, 0.0), axis=1, keepdims=True)
                dcs_head = dcs_head + jnp.where(lane == h, rsum - csum_col, 0.0)
            dhn = dht[:, sl]
            etot = jnp.exp(f["totx"])
            dxd = _dot(bg, dhn)
            dbg = _dot_nt(xdt * f["dsx"], dhn)
            dxdt = dxdt + dxd * f["dsx"]
            qq = dxd * xdt * f["dsx"]
            dcsx = dcsx - qq
            dtot = jnp.sum(qq, axis=0, keepdims=True) + jnp.sum(dhn * ht_g, axis=0, keepdims=True) * etot
            dht[:, sl] = etot * dhn + dht_prev
            dcg = dcg + _dot(dcb, bg)
            dbg = dbg + _dot_tn(dcb, cg)
            dxa_ref[:, sl] = dxdt * f["dtx"] + dy * dxp_g
            dxa_ref[:, slb] = dbg
            dxa_ref[:, slc] = dcg
            dcsx_s[:, sl] = dcsx
            ddtx_s[:, sl] = dxdt * xs_g
            accx[2:3, sl] = dtot
        reduce = (jnp.right_shift(_iota((SSD_W, LANE), 0), 6) == _iota((SSD_W, LANE), 1)).astype(F32)
        triu = (_iota((l, l), 1) >= _iota((l, l), 0)).astype(F32)
        dtot = _dot01_r(accx[...], reduce)[2:3, :]
        da_head = _dot01(triu, dcs_head + _dot01_r(dcsx_s[...], reduce)) + dtot
        ddt = _dot01_r(ddtx_s[...], reduce) + da_head * q["a_head"]
        small_ref[1:2, :] += jnp.sum(da_head * q["dt"], axis=0, keepdims=True)
        ddtr = ddt * _sigmoid(dtr + bias_ref[...])
        ddt_ref[...] = ddtr.astype(BF16)
        small_ref[0:1, :] += jnp.sum(ddtr, axis=0, keepdims=True)

        @pl.when(step == nc - 1)
        def _():
            small_ref[1:2, :] = small_ref[1:2, :] * q["a_head"]
            small_ref[2:3, :] = _dot01_r(accx[...], reduce)[0:1, :]

    rev = lambda c: nc - 1 - c
    par = lambda w: pl.BlockSpec((1, w), lambda c: (0, 0))
    outs, jouts = _hosted(
        body, jobs, grid=(nc,),
        in_specs=[pl.BlockSpec((CHUNK, XBC), lambda c: (rev(c), 0)),
                  pl.BlockSpec((CHUNK, LANE), lambda c: (rev(c), COL_DT // LANE)),
                  pl.BlockSpec((CHUNK, SSD_W), lambda c: (rev(c), COL_Z // SSD_W)),
                  pl.BlockSpec((CHUNK, SSD_W), lambda c: (rev(c), 1)),
                  pl.BlockSpec((None, N_STATE, SSD_W), lambda c: (rev(c), 0, 0)),
                  par(LANE), par(LANE), par(SSD_W), par(SSD_W)],
        out_specs=(pl.BlockSpec((CHUNK, XBC), lambda c: (rev(c), 0)),
                   pl.BlockSpec((CHUNK, LANE), lambda c: (rev(c), 0)),
                   pl.BlockSpec((CHUNK, SSD_W), lambda c: (rev(c), 0)),
                   par(SSD_W), pl.BlockSpec((SUBLANE, LANE), lambda c: (0, 0))),
        out_shape=(jax.ShapeDtypeStruct((s, XBC), F32), jax.ShapeDtypeStruct((s, LANE), BF16),
                   jax.ShapeDtypeStruct((s, SSD_W), BF16), jax.ShapeDtypeStruct((1, SSD_W), F32),
                   jax.ShapeDtypeStruct((SUBLANE, LANE), F32)),
        scratch_shapes=[pltpu.VMEM((N_STATE, SSD_W), F32), pltpu.VMEM((CHUNK, LANE), F32),
                        pltpu.VMEM((SUBLANE, SSD_W), F32), pltpu.VMEM((CHUNK, SSD_W), F32),
                        pltpu.VMEM((CHUNK, SSD_W), F32)],
        name=name, args=(xact, proj, proj, dycat, hprev, bias_pad, alog_pad, dxp, normw))
    return (tuple(outs), jouts) if jobs else tuple(outs)


def _blockdiag(w):
    w2 = w.reshape(N_HEAD // 2, 2, HEAD_P, HEAD_P)
    z = jnp.zeros((N_HEAD // 2, HEAD_P, HEAD_P), w.dtype)
    top = jnp.concatenate([w2[:, 0], z], axis=2)
    bot = jnp.concatenate([z, w2[:, 1]], axis=2)
    return jnp.concatenate([top, bot], axis=1)


def _unblockdiag(wbd):
    a = wbd[:, :HEAD_P, :HEAD_P]
    b = wbd[:, HEAD_P:, HEAD_P:]
    return jnp.stack([a, b], axis=1).reshape(N_HEAD, HEAD_P, HEAD_P)


def _pad_rows8(w):
    return jnp.concatenate([w, jnp.zeros((SUBLANE - w.shape[0], w.shape[1]), w.dtype)], axis=0)


def _pad_lane(v):
    return jnp.concatenate([v, jnp.zeros((1, LANE - v.shape[1]), v.dtype)], axis=1)


class _NoExchange:
    def ride(self, host):
        return []

    def done(self, jobs, outs, w):
        pass

    def grad(self, name, val):
        pass

    def small(self, raw):
        pass

    def pairs_now(self):
        pass


def _local_step(x, p, tgt, w, hooks=_NoExchange()):
    cw_l = _pad_rows8(w["lru_conv_w"])
    cw_s = _pad_rows8(w["ssd_conv_w"])
    wa_bd = _blockdiag(w["lru_gate_a_w"])
    wx_bd = _blockdiag(w["lru_gate_x_w"])
    ba = w["lru_gate_a_b"].reshape(1, LRU_W)
    bx = w["lru_gate_x_b"].reshape(1, LRU_W)
    bias_pad = _pad_lane(w["ssd_dt_bias"])
    alog_pad = _pad_lane(w["ssd_a_log"])
    dxp = jnp.repeat(w["ssd_d"], HEAD_P, axis=1)

    def host(fn, *a, name, **k):
        jobs = hooks.ride(name)
        res = fn(*a, name=name, jobs=jobs, **k)
        if jobs:
            res, jouts = res
            hooks.done(jobs, jouts, w)
        return res

    def grad(n, val):
        g[n] = val
        hooks.grad(n, val)

    xb = x.astype(BF16)
    proj = host(_mm, xb, w["w_in_t"], "nt", tm=1024, tn=512, name="in_proj")
    ymix, h_lru = host(_lru_fwd, proj, cw_l, w["lru_conv_b"], wa_bd, ba, wx_bd, bx, w["lru_a_param"], name="lru_fwd")
    xact = host(_conv_silu_fwd, proj, cw_s, w["ssd_conv_b"], col0=COL_XBC, width=XBC, ct=256, name="ssd_conv_fwd")
    ycat, hprev = host(_ssd_fwd, xact, proj, ymix, bias_pad, alog_pad, dxp, w["ssd_norm_w"], name="ssd_fwd")
    mix, x1, x1b = _mm_ln(ycat, w["w_out"], x, w["ln1_g"], w["ln1_b"], tm=512, name="out_proj")
    pre = _mm(x1b, w["w_ff1"], "nn", tm=1024, tn=512, out_dtype=BF16, name="ff1")
    ff, x2, x2b = _mm_ln(pre, w["w_ff2"], x1, w["ln2_g"], w["ln2_b"], tm=512, a_fn=_relu2, name="ff2")
    loss, dgpre, dple, dt3, dg3, db3 = _head(x2, x2b, p, w["w_ple_gate"], w["w_ple"], w["ln3_g"], w["ln3_b"], tgt,
                                             name="head")

    g = {}
    g["ln3_g"], g["ln3_b"] = dg3, db3
    grad("w_ple_gate", _mm(x2b, dgpre, "tn", tm=512, tn=1024, out_dtype=BF16, name="d_w_ple_gate"))
    grad("w_ple", _mm(p, dple, "tn", tm=256, tn=512, dest_major=True, out_dtype=BF16, name="d_w_ple"))
    dt2, dt2b, g["ln2_g"], g["ln2_b"] = host(_mm_ln_bwd, dgpre, w["w_ple_gate"], x1, ff, w["ln2_g"], dt3, ALPHA,
                                             tm=512, name="d_x2")
    grad("w_ff2", host(_mm, pre, dt2b, "tn", tm=512, tn=1024, a_fn=_relu2, out_dtype=BF16, name="d_w_ff2"))
    dpre = host(_mm, dt2b, w["w_ff2"], "nt", tm=1024, tn=512, extra=pre, out_dtype=BF16,
                epi=lambda acc, pv: acc * 2.0 * jnp.maximum(pv.astype(F32), 0.0), name="d_pre")
    grad("w_ff1", host(_mm, x1b, dpre, "tn", tm=1024, tn=512, dest_major=True, out_dtype=BF16, name="d_w_ff1"))
    dt1, dt1b, g["ln1_g"], g["ln1_b"] = host(_mm_ln_bwd, dpre, w["w_ff1"], x, mix, w["ln1_g"], dt2, ALPHA,
                                             tm=256, name="d_x1")
    grad("w_out", host(_mm, ycat, dt1b, "tn", tm=512, tn=1024, out_dtype=BF16, name="d_w_out"))
    dycat = host(_mm, dt1b, w["w_out"], "nt", tm=1024, tn=1024, name="d_ycat")
    dxl, dgl, dcwb_l, dwa, dwx = host(_lru_bwd, proj, dycat, h_lru, cw_l, w["lru_conv_b"], wa_bd, ba, wx_bd, bx,
                                      w["lru_a_param"], name="lru_bwd")
    g["lru_gate_a_w"] = _unblockdiag(dwa)
    g["lru_gate_x_w"] = _unblockdiag(dwx)
    raw = dict(lru=dcwb_l, gate_a=g["lru_gate_a_w"].reshape(N_HEAD * HEAD_P, HEAD_P).astype(BF16),
               gate_x=g["lru_gate_x_w"].reshape(N_HEAD * HEAD_P, HEAD_P).astype(BF16))
    hooks.small(raw)
    dxact, ddt, dz, g["ssd_norm_w"], small = host(_ssd_bwd, xact, proj, dycat, hprev, bias_pad, alog_pad, dxp,
                                                   w["ssd_norm_w"], name="ssd_bwd")
    dxbc, dcwb_s = host(_conv_silu_bwd, proj, dxact, cw_s, w["ssd_conv_b"], col0=COL_XBC, width=XBC, ct=256,
                        name="ssd_conv_bwd")
    pieces, offsets = [dxl, dgl, dz, dxbc, ddt], [0, COL_G, COL_Z, COL_XBC, COL_DT]

    g["lru_conv_w"] = dcwb_l[0:4]
    g["lru_conv_b"] = dcwb_l[4:5]
    g["lru_gate_a_b"] = dcwb_l[5:6]
    g["lru_gate_x_b"] = dcwb_l[6:7]
    g["lru_a_param"] = dcwb_l[7:8]
    g["ssd_conv_w"] = dcwb_s[0:4]
    g["ssd_conv_b"] = dcwb_s[4:5]
    g["ssd_dt_bias"] = small[0:1, :N_HEAD]
    g["ssd_a_log"] = small[1:2, :N_HEAD]
    g["ssd_d"] = small[2:3, :N_HEAD]
    rows = jnp.concatenate([g[n] for n in ("ssd_norm_w", "ln1_g", "ln1_b", "ln2_g", "ln2_b", "ln3_g", "ln3_b")]
                           + [jnp.broadcast_to(loss[:, 0:1], (1, D_MODEL))], axis=0)
    late = dict(ssd=dcwb_s, heads=small, rows=rows)
    hooks.small(late)
    raw.update(late)
    dwt = None
    for q, (pc, off) in enumerate(zip(pieces, offsets)):
        dwt = host(_mm, pc, xb, "tn", tm=512, tn=1024, out_dtype=BF16, into=(dwt, off, D_IN),
                   name="d_w_in_%d" % q)
    grad("w_in", dwt)
    hooks.pairs_now()
    grad_x = host(_mm_pieces, pieces, offsets, w["w_in_t"], tm=256, extra=dt1, epi=lambda acc, e: acc + ALPHA * e,
                  name="d_x")
    return loss[0, 0], grad_x, g, raw


ANY_SPEC = pl.BlockSpec(memory_space=pl.ANY)


def _mesh_pos():
    return lax.axis_index("x"), lax.axis_index("y"), lax.axis_index("c")


def _remote(src, dst, send, recv, k, to):
    return pltpu.make_async_remote_copy(src_ref=src, dst_ref=dst, send_sem=send.at[k], recv_sem=recv.at[k],
                                        device_id=to, device_id_type=MESH_T)


class _Job:
    N_SEM = 7

    def __init__(self, kind, inp, rows=None):
        self.kind, self.inp, self.rows = kind, inp, rows
        shape = {"gather": (N_DEV,) + inp.shape, "pair": (4,) + inp.shape[1:], "chip": inp.shape}[kind]
        self.out = jax.ShapeDtypeStruct(shape, inp.dtype)

    def _blk(self, ref, k):
        return ref.at[k] if self.rows is None else ref.at[k, pl.ds(self.rows[0], self.rows[1])]

    def _places(self):
        x, y, c = _mesh_pos()
        return (x, y, c), (x, y, 1 - c), [(1 - x, y), (x, 1 - y), (1 - x, 1 - y)]

    def start(self, inp, out, send, recv, loc):
        me, sibling, chips = self._places()
        x, y, c = me
        if self.kind == "gather":
            mine = out.at[4 * x + 2 * y + c]
            pltpu.make_async_copy(inp, mine, loc.at[0]).start()
            _remote(inp, mine, send, recv, 0, sibling).start()
            for j, chip in enumerate(chips):
                _remote(inp, mine, send, recv, 1 + j, (*chip, c)).start()
        elif self.kind == "pair":
            for k in range(4):
                _remote(inp.at[2 * k + (1 - c)], out.at[k], send, recv, k, sibling).start()
        else:
            kme = 2 * x + y
            pltpu.make_async_copy(self._blk(inp, kme), self._blk(out, kme), loc.at[0]).start()
            for j, (tx, ty) in enumerate(chips):
                _remote(self._blk(inp, 2 * tx + ty), self._blk(out, kme), send, recv, j, (tx, ty, c)).start()

    def mid(self, inp, out, send, recv, loc):
        if self.kind != "gather":
            return
        me, sibling, chips = self._places()
        c = me[2]
        for j, chip in enumerate(chips):
            landed = out.at[4 * chip[0] + 2 * chip[1] + c]
            _remote(landed, landed, send, recv, 1 + j, me).wait_recv()
            _remote(landed, landed, send, recv, 4 + j, sibling).start()

    def finish(self, inp, out, send, recv, loc):
        me, sibling, chips = self._places()
        x, y, c = me
        if self.kind == "gather":
            blk = lambda px, py, pc: out.at[4 * px + 2 * py + pc]
            mine = blk(*me)
            _remote(inp, blk(*sibling), send, recv, 0, me).wait_recv()
            for j, chip in enumerate(chips):
                _remote(inp, blk(*chip, 1 - c), send, recv, 4 + j, me).wait_recv()
            for k in range(7):
                _remote(inp, mine, send, recv, k, sibling).wait_send()
            pltpu.make_async_copy(inp, mine, loc.at[0]).wait()
        elif self.kind == "pair":
            for k in range(4):
                _remote(inp.at[2 * k + (1 - c)], out.at[k], send, recv, k, sibling).wait()
        else:
            kme = 2 * x + y
            for j, (tx, ty) in enumerate(chips):
                _remote(self._blk(inp, kme), self._blk(out, 2 * tx + ty), send, recv, j, (tx, ty, c)).wait_recv()
            for j, (tx, ty) in enumerate(chips):
                _remote(self._blk(inp, 2 * tx + ty), self._blk(out, kme), send, recv, j, (tx, ty, c)).wait_send()
            pltpu.make_async_copy(self._blk(inp, kme), self._blk(out, kme), loc.at[0]).wait()


def _job_scratch(jobs):
    sem = pltpu.SemaphoreType.DMA
    return [s for _ in jobs for s in (sem((_Job.N_SEM,)), sem((_Job.N_SEM,)), sem((1,)))]


def _run_jobs(jobs, method, jins, jouts, jsems):
    for q, job in enumerate(jobs):
        getattr(job, method)(jins[q], jouts[q], *jsems[3 * q:3 * q + 3])


def _exchange(jobs, *, name):
    n = len(jobs)

    def body(*refs):
        jins, jouts, jsems = refs[:n], refs[n:2 * n], refs[2 * n:]
        _run_jobs(jobs, "start", jins, jouts, jsems)
        _run_jobs(jobs, "mid", jins, jouts, jsems)
        _run_jobs(jobs, "finish", jins, jouts, jsems)

    return _pcall(body, in_specs=[ANY_SPEC] * n, out_specs=[ANY_SPEC] * n, out_shape=[j.out for j in jobs],
                  scratch_shapes=_job_scratch(jobs), name=name)(*[j.inp for j in jobs])


def _hosted(body, jobs, *, grid, in_specs, out_specs, out_shape, args, name, scratch_shapes=(), aliases=None):
    in_specs, out_specs, out_shape = list(in_specs), list(out_specs), list(out_shape)
    scratch_shapes = list(scratch_shapes)
    n_in, n_out, n_scr, nj = len(in_specs), len(out_specs), len(scratch_shapes), len(jobs)
    sem = ("arbitrary",) * len(grid)
    kw = dict(input_output_aliases=aliases) if aliases else {}
    if not jobs:
        res = _pcall(body, grid=grid, in_specs=in_specs, out_specs=out_specs, out_shape=out_shape,
                     scratch_shapes=scratch_shapes, name=name, compiler_params=_cparams(sem), **kw)(*args)
        return list(res), []

    def full(*refs):
        ins, jins = refs[:n_in], refs[n_in:n_in + nj]
        o0 = n_in + nj
        outs, jouts = refs[o0:o0 + n_out], refs[o0 + n_out:o0 + n_out + nj]
        s0 = o0 + n_out + nj
        scr, jsems = refs[s0:s0 + n_scr], refs[s0 + n_scr:]
        step = pl.program_id(0)
        for ax in range(1, len(grid)):
            step = step * grid[ax] + pl.program_id(ax)
        total = math.prod(grid)

        @pl.when(step == 0)
        def _():
            _run_jobs(jobs, "start", jins, jouts, jsems)

        body(*ins, *outs, *scr)

        @pl.when(step == total - 1)
        def _():
            _run_jobs(jobs, "mid", jins, jouts, jsems)
            _run_jobs(jobs, "finish", jins, jouts, jsems)

    res = _pcall(full, grid=grid, in_specs=in_specs + [ANY_SPEC] * nj, out_specs=out_specs + [ANY_SPEC] * nj,
                 out_shape=out_shape + [j.out for j in jobs], scratch_shapes=scratch_shapes + _job_scratch(jobs),
                 name=name, compiler_params=_cparams(sem), **kw)(*args, *[j.inp for j in jobs])
    return list(res[:n_out]), list(res[n_out:])


def _pair_add(g8, r4, cidx, *, name):
    _, r, c = g8.shape
    tr = ROW_TILE if r % ROW_TILE == 0 else r

    def body(c_ref, g_ref, r_ref, o_ref):
        o_ref[...] = (g_ref[...].astype(F32) + r_ref[...].astype(F32)).astype(BF16)

    return _pcall(
        body,
        grid_spec=pltpu.PrefetchScalarGridSpec(
            num_scalar_prefetch=1, grid=(4, r // tr),
            in_specs=[pl.BlockSpec((None, tr, c), lambda k, i, cr: (2 * k + cr[0], i, 0)),
                      pl.BlockSpec((None, tr, c), lambda k, i, cr: (k, i, 0))],
            out_specs=pl.BlockSpec((None, tr, c), lambda k, i, cr: (k, i, 0))),
        out_shape=jax.ShapeDtypeStruct((4, r, c), BF16), name=name,
        compiler_params=_cparams(("parallel", "parallel")))(cidx, g8, r4)


def _adam_update(g, w_ref, m_ref, v_ref, g_ref, d_ref, mo_ref, vo_ref):
    c1 = 1.0 - ADAM_B1 ** ADAM_STEP
    c2 = 1.0 - ADAM_B2 ** ADAM_STEP
    m2 = ADAM_B1 * m_ref[...] + (1.0 - ADAM_B1) * g
    v2 = ADAM_B2 * v_ref[...] + (1.0 - ADAM_B2) * (g * g)
    g_ref[...] = g
    mo_ref[...] = m2
    vo_ref[...] = v2
    d_ref[...] = -ADAM_LR * ((m2 / c1) / (jnp.sqrt(v2 / c2) + ADAM_EPS) + ADAM_WD * w_ref[...])


def _adamw_rows(srcs, items, own_cols, me1, *, name):
    ns, ni, no = len(srcs), len(items), len(own_cols)
    full = lambda a: pl.BlockSpec(a.shape, lambda i, me: (0,) * a.ndim)
    in_specs = [full(a) for a in srcs]
    args = list(srcs)
    for (si, _r0, w, _m, _v) in own_cols:
        a = srcs[si]
        in_specs.append(pl.BlockSpec((N_DEV, a.shape[1], w.shape[1]), lambda i, me: (0, 0, me[0])))
        args.append(a)
    out_specs, out_shape = [], []
    for (_si, _r0, w, m, v) in list(items) + list(own_cols):
        in_specs += [full(w)] * 3
        args += [w, m, v]
        out_specs += [full(w)] * 4
        out_shape += [jax.ShapeDtypeStruct(w.shape, F32)] * 4

    def body(me_ref, *refs):
        src_refs, own_refs = refs[:ns], refs[ns:ns + no]
        wmv = refs[ns + no:ns + no + 3 * (ni + no)]
        outs = refs[ns + no + 3 * (ni + no):]
        for q, (si, r0, w, _m, _v) in enumerate(list(items) + list(own_cols)):
            nr, cw = w.shape
            gref = src_refs[si] if q < ni else own_refs[q - ni]
            g = gref[0, r0:r0 + nr, 0:cw]
            for d in range(1, N_DEV):
                g = g + gref[d, r0:r0 + nr, 0:cw]
            _adam_update(g, *wmv[3 * q:3 * q + 3], *outs[4 * q:4 * q + 4])

    res = _pcall(
        body,
        grid_spec=pltpu.PrefetchScalarGridSpec(num_scalar_prefetch=1, grid=(1,), in_specs=in_specs, out_specs=out_specs),
        out_shape=out_shape, name=name, compiler_params=_cparams(("arbitrary",)))(me1, *args)
    return [tuple(res[4 * q:4 * q + 4]) for q in range(ni + no)]


def _adamw(gsrc, w, m, v, *, name, tail=None):
    k, r, c = gsrc.shape
    tr = ROW_TILE if r % ROW_TILE == 0 else r
    nsrc = 1 if tail is None else 2

    def body(*refs):
        gs_ref = refs[0]
        g = gs_ref[0].astype(F32)
        for q in range(1, k):
            g = g + gs_ref[q].astype(F32)
        if tail is not None:
            g2 = refs[1][0].astype(F32)
            for q in range(1, k):
                g2 = g2 + refs[1][q].astype(F32)
            g = jnp.where(_iota(g.shape, 0) >= tail[1], g2, g)
        _adam_update(g, *refs[nsrc:])

    tc = c
    if tr == r and r > ROW_TILE and c % 256 == 0:
        tc = 256
    assert tail is None or tr == r
    blk = pl.BlockSpec((tr, tc), lambda i, j: (i, j))
    gblk = pl.BlockSpec((k, tr, tc), lambda i, j: (0, i, j))
    sd = jax.ShapeDtypeStruct((r, c), F32)
    srcs = [gsrc] if tail is None else [gsrc, tail[0]]
    return _pcall(body, grid=(r // tr, c // tc), in_specs=[gblk] * nsrc + [blk, blk, blk],
                  out_specs=(blk, blk, blk, blk), out_shape=(sd, sd, sd, sd), name=name,
                  compiler_params=_cparams(("parallel", "parallel")))(*srcs, w, m, v)


def _adamw_multi(sets, *, name, jobs=()):
    plan, off = [], 0
    for gs, w, _m, _v in sets:
        r = w.shape[0]
        tr = MULTI_TILE if r % MULTI_TILE == 0 else r
        plan.append((off, r // tr, tr))
        off += r // tr
    in_specs, out_specs, out_shape, args = [], [], [], []
    for (gs, w, m, v), (o, n, tr) in zip(sets, plan):
        idx = lambda i, o=o, n=n: jnp.clip(i - o, 0, n - 1)
        c = w.shape[1]
        row = pl.BlockSpec((tr, c), lambda i, idx=idx: (idx(i), 0))
        in_specs += [pl.BlockSpec((gs.shape[0], tr, c), lambda i, idx=idx: (0, idx(i), 0)), row, row, row]
        out_specs += [row] * 4
        out_shape += [jax.ShapeDtypeStruct(w.shape, F32)] * 4
        args += [gs, w, m, v]
    ns = len(sets)

    def body(*refs):
        i = pl.program_id(0)
        for q, (o, n, _tr) in enumerate(plan):
            @pl.when(jnp.logical_and(i >= o, i < o + n))
            def _(q=q):
                gs_ref = refs[4 * q]
                g = gs_ref[0].astype(F32)
                for d in range(1, gs_ref.shape[0]):
                    g = g + gs_ref[d].astype(F32)
                _adam_update(g, *refs[4 * q + 1:4 * q + 4], *refs[4 * ns + 4 * q:4 * ns + 4 * q + 4])

    outs, jouts = _hosted(body, jobs, grid=(off,), in_specs=in_specs, out_specs=out_specs, out_shape=out_shape,
                          args=args, name=name)
    res = [tuple(outs[4 * q:4 * q + 4]) for q in range(ns)]
    return (res, jouts) if jobs else res


WEIGHTS = ['w_in', 'lru_conv_w', 'lru_conv_b', 'lru_gate_a_w', 'lru_gate_a_b', 'lru_gate_x_w', 'lru_gate_x_b',
           'lru_a_param', 'ssd_conv_w', 'ssd_conv_b', 'ssd_dt_bias', 'ssd_a_log', 'ssd_d', 'ssd_norm_w', 'w_out',
           'ln1_g', 'ln1_b', 'w_ff1', 'w_ff2', 'ln2_g', 'ln2_b', 'w_ple_gate', 'w_ple', 'ln3_g', 'ln3_b']
BIG = ['w_in', 'w_out', 'w_ff1', 'w_ff2', 'w_ple_gate', 'w_ple']
COL_SHARDED = ('w_ff1', 'w_ple')
CONV = ['lru_conv_w', 'ssd_conv_w']
REPL = [n for n in WEIGHTS if n not in BIG and n not in CONV]
CONV_CH = {'lru_conv_w': LRU_W, 'ssd_conv_w': XBC}


def _to_dest_major(name, gfull):
    if name in COL_SHARDED:
        r, cfull = gfull.shape
        return gfull.reshape(r, N_DEV, cfull // N_DEV).transpose(1, 0, 2)
    rfull, cdim = gfull.shape
    return gfull.reshape(N_DEV, rfull // N_DEV, cdim)


def _full_weight(name, gathered):
    if name in COL_SHARDED:
        _, r, cs = gathered.shape
        full = gathered.transpose(1, 0, 2).reshape(r, N_DEV * cs)
    else:
        _, rs, cdim = gathered.shape
        full = gathered.reshape(N_DEV * rs, cdim)
    if name == 'w_in':
        full = lax.dynamic_update_slice(jnp.zeros((D_IN_PAD, D_MODEL), full.dtype), full, (0, 0))
    return full


SMALL_SRC = ("lru", "ssd", "heads", "rows", "gate_a", "gate_x")
AG_HOSTS = {"in_proj": ("w_ff1",), "lru_fwd": ("w_ff2",), "ssd_conv_fwd": ("w_ple_gate", "w_ple"), "ssd_fwd": ("w_out",)}
PAIR_HOSTS = ("d_x2", "d_pre", "d_x1", "d_ycat")
CHIP_HOSTS = {"lru_bwd": ("w_ple_gate", "w_ple", "w_ff2"), "ssd_bwd": ("w_ff1",), "ssd_conv_bwd": ("w_out",),
              "d_x": ("w_in#a",), "adamw_rest": ("w_in#b",)}
W_IN_SPLIT = 320
SMALL_HOSTS = {"ssd_bwd": ("lru", "gate_a", "gate_x"), "d_w_in_3": ("ssd", "heads", "rows")}


class _Schedule:
    def __init__(self, shards, cidx):
        self.shards, self.cidx = shards, cidx
        self.pair, self.chip, self.small_jobs = [], [], []
        self.dest, self.summed, self.gathered_small = {}, {}, {}
        self.tags = []

    def ride(self, host):
        tags = []
        if host in AG_HOSTS:
            tags = [("weight", n, self.shards[n]) for n in AG_HOSTS[host]]
        elif host in PAIR_HOSTS or host in CHIP_HOSTS or host == "flush":
            tags = [("pair", n, a) for n, a in self.pair]
            self.pair = []
            if host not in PAIR_HOSTS:
                take = [t for t in self.chip if host == "flush" or t[0] in CHIP_HOSTS[host]]
                tags += [("chip", n, a, rows) for n, a, rows in take]
                self.chip = [t for t in self.chip if not any(t is u for u in take)]
        if host in SMALL_HOSTS:
            tags += [("small", n, a) for n, a in self.small_jobs if n in SMALL_HOSTS[host]]
            self.small_jobs = [t for t in self.small_jobs if t[0] not in SMALL_HOSTS[host]]
        self.tags = tags
        return [_Job({"weight": "gather", "small": "gather"}.get(t[0], t[0]), t[2], *t[3:]) for t in tags]

    def done(self, jobs, outs, w):
        for (kind, n, *_rest), o in zip(self.tags, outs):
            if kind == "weight":
                w[n] = _full_weight(n, o)
            elif kind == "small":
                self.gathered_small[n] = o
            elif kind == "pair":
                p4 = _pair_add(self.dest[n], o, self.cidx, name="rs_pair_add_" + n)
                if n == "w_in":
                    rows = p4.shape[1]
                    self.chip += [("w_in#a", p4, (0, W_IN_SPLIT)), ("w_in#b", p4, (W_IN_SPLIT, rows - W_IN_SPLIT))]
                else:
                    self.chip.append((n, p4, None))
            else:
                self.summed[n] = o

    def grad(self, name, val):
        self.dest[name] = val if val.ndim == 3 else _to_dest_major(name, val)
        self.pair.append((name, self.dest[name]))

    def small(self, raw):
        self.small_jobs += list(raw.items())

    def pairs_now(self):
        tags = [("pair", n, a) for n, a in self.pair]
        self.pair, self.tags = [], tags
        jobs = [_Job("pair", a) for _k, _n, a in tags]
        self.done(jobs, _exchange(jobs, name="rs_pairs_now"), None)

    def flush(self):
        step = 0
        while self.pair or self.chip:
            jobs = self.ride("flush")
            self.done(jobs, _exchange(jobs, name="rs_flush_%d" % step), None)
            step += 1


def kernel(x, p, w_in, lru_conv_w, lru_conv_b, lru_gate_a_w, lru_gate_a_b, lru_gate_x_w, lru_gate_x_b, lru_a_param, ssd_conv_w, ssd_conv_b, ssd_dt_bias, ssd_a_log, ssd_d, ssd_norm_w, w_out, ln1_g, ln1_b, w_ff1, w_ff2, ln2_g, ln2_b, w_ple_gate, w_ple, ln3_g, ln3_b, loss_target, m_w_in, m_lru_conv_w, m_lru_conv_b, m_lru_gate_a_w, m_lru_gate_a_b, m_lru_gate_x_w, m_lru_gate_x_b, m_lru_a_param, m_ssd_conv_w, m_ssd_conv_b, m_ssd_dt_bias, m_ssd_a_log, m_ssd_d, m_ssd_norm_w, m_w_out, m_ln1_g, m_ln1_b, m_w_ff1, m_w_ff2, m_ln2_g, m_ln2_b, m_w_ple_gate, m_w_ple, m_ln3_g, m_ln3_b, v_w_in, v_lru_conv_w, v_lru_conv_b, v_lru_gate_a_w, v_lru_gate_a_b, v_lru_gate_x_w, v_lru_gate_x_b, v_lru_a_param, v_ssd_conv_w, v_ssd_conv_b, v_ssd_dt_bias, v_ssd_a_log, v_ssd_d, v_ssd_norm_w, v_w_out, v_ln1_g, v_ln1_b, v_w_ff1, v_w_ff2, v_ln2_g, v_ln2_b, v_w_ple_gate, v_w_ple, v_ln3_g, v_ln3_b):
    given = dict(locals())
    def local(a, n):
        return jnp.swapaxes(a[0], 0, 1) if n == 'w_in' else a[0]

    wsh = {n: local(given[n], n) for n in WEIGHTS}
    msh = {n: local(given["m_" + n], n) for n in WEIGHTS}
    vsh = {n: local(given["v_" + n], n) for n in WEIGHTS}
    xi, yi, ci = _mesh_pos()
    me = 4 * xi + 2 * yi + ci

    shards = {n: wsh[n].astype(BF16) for n in BIG}
    conv_pack = jnp.concatenate([_pad_rows8(wsh[n]) for n in CONV], axis=1)
    g_in, gconv = _exchange([_Job("gather", shards['w_in']), _Job("gather", conv_pack)], name="ag_first")
    full = {'w_in_t': _full_weight('w_in', g_in)}
    c0 = 0
    for n in CONV:
        cw = CONV_CH[n] // N_DEV
        full[n] = gconv[:, :4, c0:c0 + cw].transpose(1, 0, 2).reshape(4, CONV_CH[n])
        c0 += cw
    for n in REPL:
        full[n] = given[n] if given[n].ndim == 2 else wsh[n]

    sched = _Schedule(shards, jnp.reshape(ci, (1,)).astype(jnp.int32))
    loss_local, grad_x, g, raw = _local_step(x[0], p[0, 0], loss_target[0], full, sched)
    summed, gat = sched.summed, sched.gathered_small
    loss = gat["rows"][0, 7, 0]
    for d in range(1, N_DEV):
        loss = loss + gat["rows"][d, 7, 0]

    outs = {}
    flat = lambda a: a.reshape(N_HEAD * HEAD_P, HEAD_P)
    gates = (("lru_gate_a_w", "gate_a"), ("lru_gate_x_w", "gate_x"))
    sets = [(summed[n], wsh[n], msh[n], vsh[n]) for n in BIG[1:]]
    sets += [(gat[k], flat(wsh[n]), flat(msh[n]), flat(vsh[n])) for n, k in gates]
    jobs = sched.ride("adamw_rest")
    res, jouts = _adamw_multi(sets, name="adamw_rest", jobs=jobs)
    sched.done(jobs, jouts, None)
    sched.flush()
    for n, r4 in zip(BIG[1:], res):
        outs[n] = r4
    for (n, _k), r4 in zip(gates, res[len(BIG) - 1:]):
        outs[n] = tuple(r.reshape(N_HEAD, HEAD_P, HEAD_P) for r in r4)
    outs['w_in'] = _adamw(summed["w_in#a"], wsh['w_in'], msh['w_in'], vsh['w_in'], name="adamw_w_in",
                          tail=(summed["w_in#b"], W_IN_SPLIT))
    row_items = [("lru_conv_b", 0, 4), ("lru_gate_a_b", 0, 5), ("lru_gate_x_b", 0, 6), ("lru_a_param", 0, 7),
                 ("ssd_conv_b", 1, 4), ("ssd_dt_bias", 2, 0), ("ssd_a_log", 2, 1), ("ssd_d", 2, 2),
                 ("ssd_norm_w", 3, 0), ("ln1_g", 3, 1), ("ln1_b", 3, 2), ("ln2_g", 3, 3), ("ln2_b", 3, 4),
                 ("ln3_g", 3, 5), ("ln3_b", 3, 6)]
    vec = lambda a: a.reshape(1, -1)
    items = [(si, r0, vec(given[n]), vec(given["m_" + n]), vec(given["v_" + n])) for n, si, r0 in row_items]
    own = [(si, 0, wsh[n], msh[n], vsh[n]) for n, si in (("lru_conv_w", 0), ("ssd_conv_w", 1))]
    me1 = jnp.reshape(me, (1,)).astype(jnp.int32)
    res = _adamw_rows([gat[k] for k in SMALL_SRC[:4]], items, own, me1, name="adamw_small")
    for (n, _si, _r0), r4 in zip(row_items, res[:len(row_items)]):
        outs[n] = r4
    for n, r4 in zip(CONV, res[len(row_items):]):
        outs[n] = r4

    def fin(n, k):
        a = jnp.swapaxes(outs[n][k], 0, 1) if n == 'w_in' else outs[n][k]
        return a.reshape(given[n].shape)

    return (loss, grad_x[None],
            *[fin(n, 0) for n in WEIGHTS], *[fin(n, 1) for n in WEIGHTS],
            *[fin(n, 2) for n in WEIGHTS], *[fin(n, 3) for n in WEIGHTS])
```

```python
import math

import jax
import jax.numpy as jnp
from jax import lax
from jax.experimental import pallas as pl
from jax.experimental.pallas import tpu as pltpu

F32 = jnp.float32
BF16 = jnp.bfloat16
HI = lax.Precision.HIGHEST

N_DEV = 8
D_MODEL = 1024
LRU_W = 1024
SSD_W = 1024
XBC = 2048
N_HEAD = 16
HEAD_P = 64
N_GROUP = 4
GROUP_W = 256
N_STATE = 128
CHUNK = 128
D_FF = 4096
PLE_DIM = 256
D_IN = 5136
D_IN_PAD = 5632
COL_G = 1024
COL_Z = 2048
COL_XBC = 3072
COL_DT = 5120
LRU_C = 8.0
ALPHA = 2.0 ** 0.25
LN_EPS = 1e-5
RMS_EPS = 1e-5
ADAM_LR = 0.001
ADAM_B1 = 0.9
ADAM_B2 = 0.999
ADAM_EPS = 1e-08
ADAM_WD = 0.01
ADAM_STEP = 10
GELU_C = math.sqrt(2.0 / math.pi)
LANE = 128
SUBLANE = 8
VMEM_LIMIT = 48 * 1024 * 1024
MESH_T = pl.DeviceIdType.MESH
NEG_BIG = -1e30


def _pcall(body, **kw):
    return pl.pallas_call(body, **kw)


def _cparams(sem):
    return pltpu.CompilerParams(dimension_semantics=sem, vmem_limit_bytes=VMEM_LIMIT)


def _dot(a, b):
    return jnp.dot(a.astype(BF16), b.astype(BF16), preferred_element_type=F32)


def _dot_nt(a, b):
    return lax.dot_general(a.astype(BF16), b.astype(BF16), (((1,), (1,)), ((), ())), preferred_element_type=F32)


def _dot_tn(a, b):
    return lax.dot_general(a.astype(BF16), b.astype(BF16), (((0,), (0,)), ((), ())), preferred_element_type=F32)


def _dotx(a, b):
    return jnp.dot(a, b, precision=HI, preferred_element_type=F32)


def _sigmoid(x):
    return jax.nn.sigmoid(x)


def _softplus(v):
    return jnp.maximum(v, 0.0) + jnp.log1p(jnp.exp(-jnp.abs(v)))


def _gelu(x):
    th = jnp.tanh(GELU_C * (x + 0.044715 * x * x * x))
    return 0.5 * x * (1.0 + th), th


def _gelu_grad(x, th):
    return 0.5 * (1.0 + th) + 0.5 * x * (1.0 - th * th) * GELU_C * (1.0 + 3.0 * 0.044715 * x * x)


def _iota(shape, dim):
    return lax.broadcasted_iota(jnp.int32, shape, dim)


def _mm(a, b, mode, *, tm, tn, name, a_fn=None, extra=None, epi=None, out_dtype=F32, dest_major=False, into=None,
        jobs=()):
    m = a.shape[1] if mode == "tn" else a.shape[0]
    n = b.shape[0] if mode == "nt" else b.shape[1]
    tm, tn = min(tm, m), min(tn, n)
    if dest_major:
        tn = n // N_DEV
    if mode == "nn":
        m, k = a.shape
        _, n = b.shape
        a_spec = pl.BlockSpec((tm, k), lambda i, j: (i, 0))
        b_spec = pl.BlockSpec((k, tn), lambda i, j: (0, j))
        dims = ((1,), (0,))
    elif mode == "nt":
        m, k = a.shape
        n, _ = b.shape
        a_spec = pl.BlockSpec((tm, k), lambda i, j: (i, 0))
        b_spec = pl.BlockSpec((tn, k), lambda i, j: (j, 0))
        dims = ((1,), (1,))
    else:
        k, m = a.shape
        _, n = b.shape
        a_spec = pl.BlockSpec((k, tm), lambda i, j: (0, i))
        b_spec = pl.BlockSpec((k, tn), lambda i, j: (0, j))
        dims = ((0,), (0,))
    assert m % tm == 0 and n % tn == 0, (name, m, n, tm, tn)
    o_spec = pl.BlockSpec((tm, tn), lambda i, j: (i, j))
    in_specs = [a_spec, b_spec]
    args = [a, b]
    if extra is not None:
        in_specs.append(o_spec)
        args.append(extra)

    def body(*refs):
        a_ref, b_ref, o_ref = refs[0], refs[1], refs[-1]
        av = a_ref[...]
        if a_fn is not None:
            av = a_fn(av)
        acc = lax.dot_general(av.astype(BF16), b_ref[...].astype(BF16), (dims, ((), ())), preferred_element_type=F32)
        if epi is not None:
            acc = epi(acc, refs[2][...])
        o_ref[...] = acc.astype(out_dtype)

    out_shape = jax.ShapeDtypeStruct((m, n), out_dtype)
    aliases = None
    if dest_major:
        assert extra is None
        o_spec = pl.BlockSpec((None, tm, tn), lambda i, j: (j, i, 0))
        out_shape = jax.ShapeDtypeStruct((N_DEV, m, tn), out_dtype)
    if into is not None:
        buf, row0, total = into
        assert extra is None and row0 % tm == 0
        o_spec = pl.BlockSpec((tm, tn), lambda i, j: (row0 // tm + i, j))
        out_shape = jax.ShapeDtypeStruct((total, n), out_dtype)
        if buf is not None:
            in_specs.append(ANY_SPEC)
            args.append(buf)
            aliases = {len(args) - 1: 0}
    (out,), jouts = _hosted(body, jobs, grid=(m // tm, n // tn), in_specs=in_specs, out_specs=[o_spec],
                            out_shape=[out_shape], args=args, name=name, aliases=aliases)
    return (out, jouts) if jobs else out


def _mm_pieces(pieces, offsets, b, *, tm, name, extra, epi, jobs=()):
    m = pieces[0].shape[0]
    kb, n = b.shape
    tm = min(tm, m)
    row = lambda wdt: pl.BlockSpec((tm, wdt), lambda i: (i, 0))
    in_specs = [row(pc.shape[1]) for pc in pieces] + [pl.BlockSpec((kb, n), lambda i: (0, 0)), row(n)]
    np_ = len(pieces)

    def body(*refs):
        b_ref, e_ref, o_ref = refs[np_], refs[np_ + 1], refs[np_ + 2]
        acc = jnp.zeros((tm, n), F32)
        for q in range(np_):
            kq = pieces[q].shape[1]
            acc = acc + jnp.dot(refs[q][...].astype(BF16), b_ref[offsets[q]:offsets[q] + kq, :].astype(BF16),
                                preferred_element_type=F32)
        o_ref[...] = epi(acc, e_ref[...])

    (out,), jouts = _hosted(body, jobs, grid=(m // tm,), in_specs=in_specs, out_specs=[row(n)],
                            out_shape=[jax.ShapeDtypeStruct((m, n), F32)], args=list(pieces) + [b, extra], name=name)
    return (out, jouts) if jobs else out


def _relu2(v):
    r = jnp.maximum(v, 0.0)
    return r * r


ROW_TILE = 256


def _ln_stats(t):
    mu = jnp.mean(t, axis=-1, keepdims=True)
    xc = t - mu
    var = jnp.mean(xc * xc, axis=-1, keepdims=True)
    rstd = lax.rsqrt(var + LN_EPS)
    return xc * rstd, rstd


def _ln_bwd_rows(dy, xhat, rstd, g):
    dxh = dy * g
    m1 = jnp.mean(dxh, axis=-1, keepdims=True)
    m2 = jnp.mean(dxh * xhat, axis=-1, keepdims=True)
    return rstd * (dxh - m1 - xhat * m2)


def _mm_ln(a, b, res, g, beta, *, tm, name, a_fn=None):
    m, k = a.shape
    d = b.shape[1]
    tm = min(tm, m)
    row = pl.BlockSpec((tm, d), lambda i: (i, 0))
    par = pl.BlockSpec((1, d), lambda i: (0, 0))

    def body(a_ref, b_ref, r_ref, g_ref, be_ref, br_ref, y_ref, yb_ref):
        av = a_ref[...]
        if a_fn is not None:
            av = a_fn(av)
        acc = jnp.dot(av.astype(BF16), b_ref[...].astype(BF16), preferred_element_type=F32)
        br_ref[...] = acc
        xhat, _ = _ln_stats(ALPHA * r_ref[...] + acc)
        y = xhat * g_ref[...] + be_ref[...]
        y_ref[...] = y
        yb_ref[...] = y.astype(BF16)

    sd = jax.ShapeDtypeStruct((m, d), F32)
    return _pcall(body, grid=(m // tm,),
                  in_specs=[pl.BlockSpec((tm, k), lambda i: (i, 0)), pl.BlockSpec((k, d), lambda i: (0, 0)), row, par, par],
                  out_specs=(row, row, row), out_shape=(sd, sd, jax.ShapeDtypeStruct((m, d), BF16)), name=name,
                  compiler_params=_cparams(("parallel",)))(a, b, res, g, beta)


def _mm_ln_bwd(a, b, res, branch, g, dy0, coef0, *, tm, name, jobs=()):
    m, k = a.shape
    d = b.shape[0]
    tm = min(tm, m)
    row = pl.BlockSpec((tm, d), lambda i: (i, 0))
    par = pl.BlockSpec((1, d), lambda i: (0, 0))

    def body(a_ref, b_ref, r_ref, br_ref, g_ref, dy0_ref, dt_ref, dtb_ref, dg_ref, db_ref):
        acc = lax.dot_general(a_ref[...].astype(BF16), b_ref[...].astype(BF16), (((1,), (1,)), ((), ())),
                              preferred_element_type=F32)
        dy = coef0 * dy0_ref[...] + acc
        xhat, rstd = _ln_stats(ALPHA * r_ref[...] + br_ref[...])
        dt = _ln_bwd_rows(dy, xhat, rstd, g_ref[...])
        dt_ref[...] = dt
        dtb_ref[...] = dt.astype(BF16)

        @pl.when(pl.program_id(0) == 0)
        def _():
            dg_ref[...] = jnp.zeros_like(dg_ref)
            db_ref[...] = jnp.zeros_like(db_ref)

        dg_ref[...] += jnp.sum(dy * xhat, axis=0, keepdims=True)
        db_ref[...] += jnp.sum(dy, axis=0, keepdims=True)

    pd = jax.ShapeDtypeStruct((1, d), F32)
    outs, jouts = _hosted(
        body, jobs, grid=(m // tm,),
        in_specs=[pl.BlockSpec((tm, k), lambda i: (i, 0)), pl.BlockSpec((d, k), lambda i: (0, 0)), row, row, par, row],
        out_specs=(row, row, par, par),
        out_shape=(jax.ShapeDtypeStruct((m, d), F32), jax.ShapeDtypeStruct((m, d), BF16), pd, pd),
        args=(a, b, res, branch, g, dy0), name=name)
    return (tuple(outs), jouts) if jobs else tuple(outs)


def _head(x2, x2b, p, wg, wp, g, beta, tgt, *, name):
    s, d = x2.shape
    tile = 2 * ROW_TILE
    row = pl.BlockSpec((tile, d), lambda i: (i, 0))
    par = pl.BlockSpec((1, d), lambda i: (0, 0))
    lsp = pl.BlockSpec((1, LANE), lambda i: (0, 0))
    whole = lambda a: pl.BlockSpec(a.shape, lambda i: (0, 0))

    def body(x2_ref, x2b_ref, p_ref, wg_ref, wp_ref, g_ref, be_ref, t_ref,
             loss_ref, dgp_ref, dple_ref, dt_ref, dg_ref, db_ref):
        gate = _sigmoid(_dot(x2b_ref[...], wg_ref[...]))
        ple_v = _dot(p_ref[...], wp_ref[...])
        xhat, rstd = _ln_stats(ALPHA * x2_ref[...] + gate * ple_v)
        err = xhat * g_ref[...] + be_ref[...] - t_ref[...]
        dy = err * (1.0 / d)
        dt = _ln_bwd_rows(dy, xhat, rstd, g_ref[...])
        dt_ref[...] = dt
        dgp_ref[...] = (dt * ple_v * gate * (1.0 - gate)).astype(BF16)
        dple_ref[...] = (dt * gate).astype(BF16)

        @pl.when(pl.program_id(0) == 0)
        def _():
            loss_ref[...] = jnp.zeros_like(loss_ref)
            dg_ref[...] = jnp.zeros_like(dg_ref)
            db_ref[...] = jnp.zeros_like(db_ref)

        loss_ref[...] += 0.5 * jnp.sum(jnp.mean(err * err, axis=-1, keepdims=True))
        dg_ref[...] += jnp.sum(dy * xhat, axis=0, keepdims=True)
        db_ref[...] += jnp.sum(dy, axis=0, keepdims=True)

    sd = jax.ShapeDtypeStruct((s, d), F32)
    sb = jax.ShapeDtypeStruct((s, d), BF16)
    pd = jax.ShapeDtypeStruct((1, d), F32)
    return _pcall(body, grid=(s // tile,),
                  in_specs=[row, row, pl.BlockSpec((tile, p.shape[1]), lambda i: (i, 0)), whole(wg), whole(wp), par, par,
                            row],
                  out_specs=(lsp, row, row, row, par, par),
                  out_shape=(jax.ShapeDtypeStruct((1, LANE), F32), sb, sb, sd, pd, pd),
                  name=name, compiler_params=_cparams(("arbitrary",)))(x2, x2b, p, wg, wp, g, beta, tgt)


CONV_R = 256
PAD = SUBLANE


def _shift_down(ext, s):
    if s == 0:
        return ext[PAD:, :]
    return pltpu.roll(ext, s, 0)[PAD:, :]


def _shift_up(ext, s):
    r = ext.shape[0] - PAD
    if s == 0:
        return ext[:r, :]
    return pltpu.roll(ext, r + PAD - s, 0)[:r, :]


def _conv_rows(xpad_ref, r0, w_ref):
    ext = xpad_ref[pl.ds(r0, CONV_R + PAD), :]
    acc = _shift_down(ext, 0) * w_ref[3:4, :]
    for k in range(3):
        acc = acc + _shift_down(ext, 3 - k) * w_ref[k:k + 1, :]
    return acc, ext


def _fill_front_padded(dst_ref, src_ref, s):
    dst_ref[0:PAD, :] = jnp.zeros((PAD, dst_ref.shape[1]), F32)

    def cp(q, _):
        r0 = pl.multiple_of(q * CONV_R, CONV_R)
        dst_ref[pl.ds(pl.multiple_of(PAD + r0, PAD), CONV_R), :] = src_ref[pl.ds(r0, CONV_R), :]
        return 0

    lax.fori_loop(0, s // CONV_R, cp, 0)


def _conv_silu_fwd(proj, w8, b, *, col0, width, ct, name, jobs=()):
    s = proj.shape[0]
    nb = col0 // ct

    def body(x_ref, w_ref, b_ref, o_ref, xpad):
        _fill_front_padded(xpad, x_ref, s)

        def step(q, _):
            r0 = pl.multiple_of(q * CONV_R, CONV_R)
            acc, _e = _conv_rows(xpad, r0, w_ref)
            pre = acc + b_ref[...]
            o_ref[pl.ds(r0, CONV_R), :] = pre * _sigmoid(pre)
            return 0

        lax.fori_loop(0, s // CONV_R, step, 0)

    (out,), jouts = _hosted(
        body, jobs, grid=(width // ct,),
        in_specs=[pl.BlockSpec((s, ct), lambda j: (0, nb + j)), pl.BlockSpec((SUBLANE, ct), lambda j: (0, j)),
                  pl.BlockSpec((1, ct), lambda j: (0, j))],
        out_specs=[pl.BlockSpec((s, ct), lambda j: (0, j))],
        out_shape=[jax.ShapeDtypeStruct((s, width), F32)],
        scratch_shapes=[pltpu.VMEM((s + PAD, ct), F32)], name=name, args=(proj, w8, b))
    return (out, jouts) if jobs else out


def _conv_bwd_rows(dpad_ref, r0, w_ref):
    return _conv_bwd_ext(dpad_ref[pl.ds(r0, CONV_R + PAD), :], w_ref)


def _conv_bwd_ext(ext, w_ref):
    acc = _shift_up(ext, 0) * w_ref[3:4, :]
    for k in range(3):
        acc = acc + _shift_up(ext, 3 - k) * w_ref[k:k + 1, :]
    return acc


def _conv_silu_bwd(proj, dact, w8, b, *, col0, width, ct, name, jobs=()):
    s = proj.shape[0]
    nb = col0 // ct

    def body(x_ref, d_ref, w_ref, b_ref, dx_ref, dwb_ref, xpad, dpad):
        _fill_front_padded(xpad, x_ref, s)
        dpad[pl.ds(s, PAD), :] = jnp.zeros((PAD, ct), F32)
        dwb_ref[...] = jnp.zeros_like(dwb_ref)

        def step(q, _):
            r0 = pl.multiple_of(q * CONV_R, CONV_R)
            acc, ext = _conv_rows(xpad, r0, w_ref)
            pre = acc + b_ref[...]
            sg = _sigmoid(pre)
            dpre = d_ref[pl.ds(r0, CONV_R), :] * sg * (1.0 + pre * (1.0 - sg))
            dpad[pl.ds(r0, CONV_R), :] = dpre
            for k in range(4):
                dwb_ref[k:k + 1, :] += jnp.sum(dpre * _shift_down(ext, 3 - k), axis=0, keepdims=True)
            dwb_ref[4:5, :] += jnp.sum(dpre, axis=0, keepdims=True)
            return 0

        lax.fori_loop(0, s // CONV_R, step, 0)

        def step2(q, _):
            r0 = pl.multiple_of(q * CONV_R, CONV_R)
            dx_ref[pl.ds(r0, CONV_R), :] = _conv_bwd_rows(dpad, r0, w_ref).astype(BF16)
            return 0

        lax.fori_loop(0, s // CONV_R, step2, 0)

    colb = pl.BlockSpec((s, ct), lambda j: (0, j))
    outs, jouts = _hosted(
        body, jobs, grid=(width // ct,),
        in_specs=[pl.BlockSpec((s, ct), lambda j: (0, nb + j)), colb, pl.BlockSpec((SUBLANE, ct), lambda j: (0, j)),
                  pl.BlockSpec((1, ct), lambda j: (0, j))],
        out_specs=(colb, pl.BlockSpec((SUBLANE, ct), lambda j: (0, j))),
        out_shape=(jax.ShapeDtypeStruct((s, width), BF16), jax.ShapeDtypeStruct((SUBLANE, width), F32)),
        scratch_shapes=[pltpu.VMEM((s + PAD, ct), F32), pltpu.VMEM((s + PAD, ct), F32)], name=name,
        args=(proj, dact, w8, b))
    return (tuple(outs), jouts) if jobs else tuple(outs)


LRU_CT = 128


def _row_of(v, r):
    return jnp.sum(jnp.where(_iota((v.shape[0], 1), 0) == r, v, 0.0), axis=0, keepdims=True)


def _scan_fwd(a, u):
    r = a.shape[0]
    row = _iota((r, 1), 0)
    d = 1
    while d < r:
        valid = row >= d
        u = jnp.where(valid, a * pltpu.roll(u, d, 0) + u, u)
        a = jnp.where(valid, a * pltpu.roll(a, d, 0), a)
        d *= 2
    return a, u


def _scan_rev(b, u):
    r = b.shape[0]
    row = _iota((r, 1), 0)
    d = 1
    while d < r:
        valid = row < r - d
        u = jnp.where(valid, b * pltpu.roll(u, r - d, 0) + u, u)
        b = jnp.where(valid, b * pltpu.roll(b, r - d, 0), b)
        d *= 2
    return b, u


def _lru_chunk(xpad, r0, cw_ref, cb, wa, ba, wx, bx, sp):
    acc, ext = _conv_rows(xpad, r0, cw_ref)
    xl = acc + cb
    r = _sigmoid(_dot(xl, wa) + ba)
    i = _sigmoid(_dot(xl, wx) + bx)
    la = -LRU_C * r * sp
    a = jnp.exp(la)
    a2 = jnp.exp(2.0 * la)
    mult = jnp.sqrt(-jnp.tanh(la) * (a2 + 1.0))
    first = (r0 + _iota((CONV_R, 1), 0)) == 0
    mult = jnp.where(first, 1.0, mult)
    return ext, xl, r, i, a, a2, mult, first


def _lru_specs(s):
    ct = LRU_CT
    nb_g = COL_G // ct
    return dict(
        x=pl.BlockSpec((s, ct), lambda j: (0, j)),
        g=pl.BlockSpec((s, ct), lambda j: (0, nb_g + j)),
        col=pl.BlockSpec((s, ct), lambda j: (0, j)),
        cw=pl.BlockSpec((SUBLANE, ct), lambda j: (0, j)),
        vec=pl.BlockSpec((1, ct), lambda j: (0, j)),
        gate=pl.BlockSpec((None, ct, ct), lambda j: (j, 0, 0)),
    )


def _lru_fwd(proj, cw8, cb, wa_bd, ba, wx_bd, bx, ap, *, name, jobs=()):
    s = proj.shape[0]
    ct = LRU_CT
    sp_ = _lru_specs(s)

    def body(x_ref, g_ref, cw_ref, cb_ref, wa_ref, ba_ref, wx_ref, bx_ref, ap_ref, y_ref, h_ref, xpad):
        _fill_front_padded(xpad, x_ref, s)
        sp = _softplus(-ap_ref[...])

        def step(q, carry):
            r0 = pl.multiple_of(q * CONV_R, CONV_R)
            _e, xl, _r, i, a, _a2, mult, _f = _lru_chunk(xpad, r0, cw_ref, cb_ref[...], wa_ref[...], ba_ref[...],
                                                       wx_ref[...], bx_ref[...], sp)
            acum, ucum = _scan_fwd(a, xl * i * mult)
            h = acum * carry + ucum
            h_ref[pl.ds(r0, CONV_R), :] = h
            ge, _th = _gelu(g_ref[pl.ds(r0, CONV_R), :])
            y_ref[pl.ds(r0, CONV_R), :] = (ge * h).astype(BF16)
            return _row_of(h, CONV_R - 1)

        lax.fori_loop(0, s // CONV_R, step, jnp.zeros((1, ct), F32))

    (ymix, hs), jouts = _hosted(
        body, jobs, grid=(LRU_W // ct,),
        in_specs=[sp_["x"], sp_["g"], sp_["cw"], sp_["vec"], sp_["gate"], sp_["vec"], sp_["gate"], sp_["vec"], sp_["vec"]],
        out_specs=(sp_["col"], sp_["col"]),
        out_shape=(jax.ShapeDtypeStruct((s, LRU_W + SSD_W), BF16), jax.ShapeDtypeStruct((s, LRU_W), F32)),
        scratch_shapes=[pltpu.VMEM((s + PAD, ct), F32)],
        name=name, args=(proj, proj, cw8, cb, wa_bd, ba, wx_bd, bx, ap))
    return ((ymix, hs), jouts) if jobs else (ymix, hs)


def _lru_bwd(proj, dy, hs, cw8, cb, wa_bd, ba, wx_bd, bx, ap, *, name, jobs=()):
    s = proj.shape[0]
    ct = LRU_CT
    sp_ = _lru_specs(s)

    nq = s // CONV_R

    def body(x_ref, g_ref, dy_ref, h_ref, cw_ref, cb_ref, wa_ref, ba_ref, wx_ref, bx_ref, ap_ref,
             dx_ref, dg_ref, dcwb_ref, dwa_ref, dwx_ref, xpad, hpad):
        _fill_front_padded(xpad, x_ref, s)
        _fill_front_padded(hpad, h_ref, s)
        apv = ap_ref[...]
        sp = _softplus(-apv)
        cb_v, wa, ba_v, wx, bx_v = cb_ref[...], wa_ref[...], ba_ref[...], wx_ref[...], bx_ref[...]
        dcwb_ref[...] = jnp.zeros_like(dcwb_ref)
        dwa_ref[...] = jnp.zeros_like(dwa_ref)
        dwx_ref[...] = jnp.zeros_like(dwx_ref)

        def back(k, carry):
            g_next, a_next, dxl_next = carry
            last_row = _iota((CONV_R, 1), 0) == CONV_R - 1
            r0 = pl.multiple_of((nq - 1 - k) * CONV_R, CONV_R)
            ext, xl, r, i, a, a2, mult, first = _lru_chunk(xpad, r0, cw_ref, cb_v, wa, ba_v, wx, bx_v, sp)
            gv = g_ref[pl.ds(r0, CONV_R), :]
            dyv = dy_ref[pl.ds(r0, CONV_R), :]
            hext = hpad[pl.ds(r0, CONV_R + PAD), :]
            ge, th = _gelu(gv)
            dg_ref[pl.ds(r0, CONV_R), :] = (dyv * _shift_down(hext, 0) * _gelu_grad(gv, th)).astype(BF16)
            b = jnp.where(last_row, a_next, pltpu.roll(a, CONV_R - 1, 0))
            bcum, dcum = _scan_rev(b, dyv * ge)
            gval = dcum + bcum * g_next
            hprev = _shift_down(hext, 1)
            da = gval * hprev
            dxl = gval * i * mult
            di = gval * xl * mult
            dmult = jnp.where(first, 0.0, gval * xl * i)
            dla = da * a - dmult * a2 / mult
            dr = dla * (-LRU_C) * sp
            dcwb_ref[7:8, :] += jnp.sum(dla * (-LRU_C) * r, axis=0, keepdims=True)
            dpr = dr * r * (1.0 - r)
            dpi = di * i * (1.0 - i)
            dxl = dxl + _dot_nt(dpr, wa) + _dot_nt(dpi, wx)
            dwa_ref[...] += _dot_tn(xl, dpr)
            dwx_ref[...] += _dot_tn(xl, dpi)
            dcwb_ref[5:6, :] += jnp.sum(dpr, axis=0, keepdims=True)
            dcwb_ref[6:7, :] += jnp.sum(dpi, axis=0, keepdims=True)
            for tap in range(4):
                dcwb_ref[tap:tap + 1, :] += jnp.sum(dxl * _shift_down(ext, 3 - tap), axis=0, keepdims=True)
            dcwb_ref[4:5, :] += jnp.sum(dxl, axis=0, keepdims=True)
            dx_ref[pl.ds(r0, CONV_R), :] = _conv_bwd_ext(jnp.concatenate([dxl, dxl_next], axis=0), cw_ref).astype(BF16)
            return _row_of(gval, 0), _row_of(a, 0), dxl[:PAD, :]

        zero = jnp.zeros((1, ct), F32)
        lax.fori_loop(0, nq, back, (zero, zero, jnp.zeros((PAD, ct), F32)))
        dcwb_ref[7:8, :] = dcwb_ref[7:8, :] * (-_sigmoid(-apv))

    nt = LRU_W // ct
    outs, jouts = _hosted(
        body, jobs, grid=(nt,),
        in_specs=[sp_["x"], sp_["g"], sp_["col"], sp_["col"], sp_["cw"], sp_["vec"], sp_["gate"], sp_["vec"], sp_["gate"],
                  sp_["vec"], sp_["vec"]],
        out_specs=(sp_["col"], sp_["col"], sp_["cw"], sp_["gate"], sp_["gate"]),
        out_shape=(jax.ShapeDtypeStruct((s, LRU_W), BF16), jax.ShapeDtypeStruct((s, LRU_W), BF16),
                   jax.ShapeDtypeStruct((SUBLANE, LRU_W), F32), jax.ShapeDtypeStruct((nt, ct, ct), F32),
                   jax.ShapeDtypeStruct((nt, ct, ct), F32)),
        scratch_shapes=[pltpu.VMEM((s + PAD, ct), F32), pltpu.VMEM((s + PAD, ct), F32)],
        name=name, args=(proj, proj, dy, hs, cw8, cb, wa_bd, ba, wx_bd, bx, ap))
    return (tuple(outs), jouts) if jobs else tuple(outs)


def _split3(v):
    hi = v.astype(BF16)
    r1 = v - hi.astype(F32)
    mid = r1.astype(BF16)
    lo = (r1 - mid.astype(F32)).astype(BF16)
    return hi, mid, lo


def _dot01(m01, v):
    mb = m01.astype(BF16)
    hi, mid, lo = _split3(v)
    f = lambda part: jnp.dot(mb, part, preferred_element_type=F32)
    return f(hi) + f(mid) + f(lo)


def _dot01_r(v, m01, parts=3):
    mb = m01.astype(BF16)
    acc = None
    for part in _split3(v)[:parts]:
        t = jnp.dot(part, mb, preferred_element_type=F32)
        acc = t if acc is None else acc + t
    return acc


def _ssd_prep(dtr, bias, alog_pad):
    l = CHUNK
    lane = _iota((1, LANE), 1)
    a_head = jnp.where(lane < N_HEAD, -jnp.exp(alog_pad), 0.0)
    dt = _softplus(dtr + bias)
    tril = (_iota((l, l), 1) <= _iota((l, l), 0)).astype(F32)
    a = dt * a_head
    cs = _dot01(tril, a)
    tot = jnp.sum(a, axis=0, keepdims=True)
    return dict(a_head=a_head, dt=dt, tril=tril, cs=cs, tot=tot)


def _col(v, h):
    lane = _iota(v.shape, 1)
    return jnp.sum(jnp.where(lane == h, v, 0.0), axis=1, keepdims=True)


def _decay_mat(cs, cst_ref, h, causal):
    row = cst_ref[h:h + 1, :]
    return jnp.exp(jnp.where(causal, _col(cs, h) - row, NEG_BIG))


def _head_mask(j, rows=CHUNK):
    lane = _iota((rows, GROUP_W), 1)
    return (lane >= j * HEAD_P) & (lane < (j + 1) * HEAD_P)


def _over_heads(v, g):
    r = v.shape[0]
    out = jnp.zeros((r, GROUP_W), F32)
    for j in range(4):
        out = jnp.where(_head_mask(j, r), _col(v, 4 * g + j), out)
    return out


def _ssd_group_fwd(q, g, xs_g, bg, cg, ht_g, cst_ref, causal, dx_g):
    dtx_g, csx_g, totx_g = _over_heads(q["dt"], g), _over_heads(q["cs"], g), _over_heads(q["tot"], g)
    xdt = xs_g * dtx_g
    ex = jnp.exp(csx_g)
    cb = _dot_nt(cg, bg)
    yoff = _dot(cg, ht_g) * ex
    ydiag = jnp.zeros((CHUNK, GROUP_W), F32)
    lms = []
    for j in range(4):
        lms.append(_decay_mat(q["cs"], cst_ref, 4 * g + j, causal))
        ydiag = jnp.where(_head_mask(j), _dot(cb * lms[j], xdt), ydiag)
    y = ydiag + yoff + xs_g * dx_g
    dsx = jnp.exp(totx_g - csx_g)
    return y, dict(xdt=xdt, ex=ex, cb=cb, yoff=yoff, dsx=dsx, dtx=dtx_g, totx=totx_g, lms=lms)


def _gated_norm_fwd(y_g, z_g, w_g):
    sz = _sigmoid(z_g)
    silu = z_g * sz
    yf = y_g * silu
    rs = lax.rsqrt(jnp.mean(yf * yf, axis=1, keepdims=True) + RMS_EPS)
    yn = yf * rs
    return yn * w_g, (sz, silu, rs, yn)


def _ssd_fwd(xact, proj, ymix, bias_pad, alog_pad, dxp, normw, *, name, jobs=()):
    s = xact.shape[0]
    nc = s // CHUNK

    def body(xa_ref, dt_ref, z_ref, _ymix_ref, bias_ref, alp_ref, dx_ref, nw_ref, y_ref, hp_ref, ht, cst):
        @pl.when(pl.program_id(0) == 0)
        def _():
            ht[...] = jnp.zeros_like(ht)

        hp_ref[...] = ht[...]
        q = _ssd_prep(dt_ref[...], bias_ref[...], alp_ref[...])
        cst[...] = q["cs"].T
        causal = q["tril"] > 0.0
        for g in range(N_GROUP):
            sl = slice(g * GROUP_W, (g + 1) * GROUP_W)
            xs_g = xa_ref[:, sl]
            bg = xa_ref[:, SSD_W + g * N_STATE:SSD_W + (g + 1) * N_STATE]
            cg = xa_ref[:, SSD_W + N_GROUP * N_STATE + g * N_STATE:SSD_W + N_GROUP * N_STATE + (g + 1) * N_STATE]
            ht_g = ht[:, sl]
            y, f = _ssd_group_fwd(q, g, xs_g, bg, cg, ht_g, cst, causal, dx_ref[:, sl])
            out, _ = _gated_norm_fwd(y, z_ref[:, sl], nw_ref[:, sl])
            y_ref[:, sl] = out.astype(BF16)
            ht[:, sl] = jnp.exp(f["totx"]) * ht_g + _dot_tn(bg, f["xdt"] * f["dsx"])

    par = lambda w: pl.BlockSpec((1, w), lambda c: (0, 0))
    (ycat, hprev), jouts = _hosted(
        body, jobs, grid=(nc,),
        in_specs=[pl.BlockSpec((CHUNK, XBC), lambda c: (c, 0)),
                  pl.BlockSpec((CHUNK, LANE), lambda c: (c, COL_DT // LANE)),
                  pl.BlockSpec((CHUNK, SSD_W), lambda c: (c, COL_Z // SSD_W)),
                  ANY_SPEC, par(LANE), par(LANE), par(SSD_W), par(SSD_W)],
        out_specs=(pl.BlockSpec((CHUNK, SSD_W), lambda c: (c, LRU_W // SSD_W)),
                   pl.BlockSpec((None, N_STATE, SSD_W), lambda c: (c, 0, 0))),
        out_shape=(jax.ShapeDtypeStruct(ymix.shape, ymix.dtype), jax.ShapeDtypeStruct((nc, N_STATE, SSD_W), F32)),
        scratch_shapes=[pltpu.VMEM((N_STATE, SSD_W), F32), pltpu.VMEM((CHUNK, LANE), F32)],
        aliases={3: 0}, name=name, args=(xact, proj, proj, ymix, bias_pad, alog_pad, dxp, normw))
    return ((ycat, hprev), jouts) if jobs else (ycat, hprev)


def _ssd_bwd(xact, proj, dycat, hprev, bias_pad, alog_pad, dxp, normw, *, name, jobs=()):
    s = xact.shape[0]
    nc = s // CHUNK
    l = CHUNK

    def body(xa_ref, dt_ref, z_ref, dy_ref, hp_ref, bias_ref, alp_ref, dx_ref, nw_ref,
             dxa_ref, ddt_ref, dz_ref, dnw_ref, small_ref, dht, cst, accx, dcsx_s, ddtx_s):
        step = pl.program_id(0)

        @pl.when(step == 0)
        def _():
            dht[...] = jnp.zeros_like(dht)
            accx[...] = jnp.zeros_like(accx)
            dnw_ref[...] = jnp.zeros_like(dnw_ref)
            small_ref[...] = jnp.zeros_like(small_ref)

        dtr = dt_ref[...]
        q = _ssd_prep(dtr, bias_ref[...], alp_ref[...])
        cst[...] = q["cs"].T
        causal = q["tril"] > 0.0
        lane = _iota((l, LANE), 1)
        head_row = _iota((LANE, l), 0)
        dcs_head = jnp.zeros((l, LANE), F32)
        dcs_rows = jnp.zeros((LANE, l), F32)
        for g in range(N_GROUP):
            sl = slice(g * GROUP_W, (g + 1) * GROUP_W)
            slb = slice(SSD_W + g * N_STATE, SSD_W + (g + 1) * N_STATE)
            slc = slice(SSD_W + N_GROUP * N_STATE + g * N_STATE, SSD_W + N_GROUP * N_STATE + (g + 1) * N_STATE)
            xs_g, bg, cg = xa_ref[:, sl], xa_ref[:, slb], xa_ref[:, slc]
            ht_g = hp_ref[:, sl]
            dxp_g = dx_ref[:, sl]
            y, f = _ssd_group_fwd(q, g, xs_g, bg, cg, ht_g, cst, causal, dxp_g)
            z_g, nw_g = z_ref[:, sl], nw_ref[:, sl]
            _o, (sz, silu, rs, yn) = _gated_norm_fwd(y, z_g, nw_g)
            dout = dy_ref[:, sl]
            dnw_ref[:, sl] += jnp.sum(dout * yn, axis=0, keepdims=True)
            dyn = dout * nw_g
            dyf = rs * (dyn - yn * jnp.mean(dyn * yn, axis=1, keepdims=True))
            dy = dyf * silu
            dz_ref[:, sl] = (dyf * y * sz * (1.0 + z_g * (1.0 - sz))).astype(BF16)
            accx[0:1, sl] += jnp.sum(dy * xs_g, axis=0, keepdims=True)
            dyo = dy * f["ex"]
            dcg = _dot_nt(dyo, ht_g)
            dht_prev = _dot_tn(cg, dyo)
            dcsx = dy * f["yoff"]
            xdt = f["xdt"]
            dxdt = jnp.zeros((l, GROUP_W), F32)
            dcb = jnp.zeros((l, l), F32)
            for j in range(4):
                h = 4 * g + j
                lm = f["lms"][j]
                sc = f["cb"] * lm
                mask = _head_mask(j)
                ds_ = jnp.where(causal, _dot_nt(jnp.where(mask, dy, 0.0), xdt), 0.0)
                dxdt = jnp.where(mask, _dot_tn(sc, dy), dxdt)
                dcb = dcb + ds_ * lm
                m = ds_ * sc
                dcs_head = dcs_head + jnp.where(lane == h, jnp.sum(m, axis=1, keepdims=True), 0.0)
                dcs_rows = dcs_rows + jnp.where(head_row == h, jnp.sum(m, axis=0, keepdims=True), 0.0)
            dhn = dht[:, sl]
            etot = jnp.exp(f["totx"])
            dxd = _dot(bg, dhn)
            dbg = _dot_nt(xdt * f["dsx"], dhn)
            dxdt = dxdt + dxd * f["dsx"]
            qq = dxd * xdt * f["dsx"]
            dcsx = dcsx - qq
            dtot = jnp.sum(qq, axis=0, keepdims=True) + jnp.sum(dhn * ht_g, axis=0, keepdims=True) * etot
            dht[:, sl] = etot * dhn + dht_prev
            dcg = dcg + _dot(dcb, bg)
            dbg = dbg + _dot_tn(dcb, cg)
            dxa_ref[:, sl] = dxdt * f["dtx"] + dy * dxp_g
            dxa_ref[:, slb] = dbg
            dxa_ref[:, slc] = dcg
            dcsx_s[:, sl] = dcsx
            ddtx_s[:, sl] = dxdt * xs_g
            accx[2:3, sl] = dtot
        reduce = (jnp.right_shift(_iota((SSD_W, LANE), 0), 6) == _iota((SSD_W, LANE), 1)).astype(F32)
        triu = (_iota((l, l), 1) >= _iota((l, l), 0)).astype(F32)
        dtot = _dot01_r(accx[...], reduce)[2:3, :]
        dcs_head = dcs_head - dcs_rows.T
        da_head = _dot01(triu, dcs_head + _dot01_r(dcsx_s[...], reduce, parts=2)) + dtot
        ddt = _dot01_r(ddtx_s[...], reduce, parts=2) + da_head * q["a_head"]
        small_ref[1:2, :] += jnp.sum(da_head * q["dt"], axis=0, keepdims=True)
        ddtr = ddt * _sigmoid(dtr + bias_ref[...])
        ddt_ref[...] = ddtr.astype(BF16)
        small_ref[0:1, :] += jnp.sum(ddtr, axis=0, keepdims=True)

        @pl.when(step == nc - 1)
        def _():
            small_ref[1:2, :] = small_ref[1:2, :] * q["a_head"]
            small_ref[2:3, :] = _dot01_r(accx[...], reduce)[0:1, :]

    rev = lambda c: nc - 1 - c
    par = lambda w: pl.BlockSpec((1, w), lambda c: (0, 0))
    outs, jouts = _hosted(
        body, jobs, grid=(nc,),
        in_specs=[pl.BlockSpec((CHUNK, XBC), lambda c: (rev(c), 0)),
                  pl.BlockSpec((CHUNK, LANE), lambda c: (rev(c), COL_DT // LANE)),
                  pl.BlockSpec((CHUNK, SSD_W), lambda c: (rev(c), COL_Z // SSD_W)),
                  pl.BlockSpec((CHUNK, SSD_W), lambda c: (rev(c), 1)),
                  pl.BlockSpec((None, N_STATE, SSD_W), lambda c: (rev(c), 0, 0)),
                  par(LANE), par(LANE), par(SSD_W), par(SSD_W)],
        out_specs=(pl.BlockSpec((CHUNK, XBC), lambda c: (rev(c), 0)),
                   pl.BlockSpec((CHUNK, LANE), lambda c: (rev(c), 0)),
                   pl.BlockSpec((CHUNK, SSD_W), lambda c: (rev(c), 0)),
                   par(SSD_W), pl.BlockSpec((SUBLANE, LANE), lambda c: (0, 0))),
        out_shape=(jax.ShapeDtypeStruct((s, XBC), F32), jax.ShapeDtypeStruct((s, LANE), BF16),
                   jax.ShapeDtypeStruct((s, SSD_W), BF16), jax.ShapeDtypeStruct((1, SSD_W), F32),
                   jax.ShapeDtypeStruct((SUBLANE, LANE), F32)),
        scratch_shapes=[pltpu.VMEM((N_STATE, SSD_W), F32), pltpu.VMEM((CHUNK, LANE), F32),
                        pltpu.VMEM((SUBLANE, SSD_W), F32), pltpu.VMEM((CHUNK, SSD_W), F32),
                        pltpu.VMEM((CHUNK, SSD_W), F32)],
        name=name, args=(xact, proj, proj, dycat, hprev, bias_pad, alog_pad, dxp, normw))
    return (tuple(outs), jouts) if jobs else tuple(outs)


def _blockdiag(w):
    w2 = w.reshape(N_HEAD // 2, 2, HEAD_P, HEAD_P)
    z = jnp.zeros((N_HEAD // 2, HEAD_P, HEAD_P), w.dtype)
    top = jnp.concatenate([w2[:, 0], z], axis=2)
    bot = jnp.concatenate([z, w2[:, 1]], axis=2)
    return jnp.concatenate([top, bot], axis=1)


def _unblockdiag(wbd):
    a = wbd[:, :HEAD_P, :HEAD_P]
    b = wbd[:, HEAD_P:, HEAD_P:]
    return jnp.stack([a, b], axis=1).reshape(N_HEAD, HEAD_P, HEAD_P)


def _pad_rows8(w):
    return jnp.concatenate([w, jnp.zeros((SUBLANE - w.shape[0], w.shape[1]), w.dtype)], axis=0)


def _pad_lane(v):
    return jnp.concatenate([v, jnp.zeros((1, LANE - v.shape[1]), v.dtype)], axis=1)


class _NoExchange:
    def ride(self, host):
        return []

    def done(self, jobs, outs, w):
        pass

    def grad(self, name, val):
        pass

    def small(self, raw):
        pass

    def pairs_now(self):
        pass


def _local_step(x, p, tgt, w, hooks=_NoExchange()):
    cw_l = _pad_rows8(w["lru_conv_w"])
    cw_s = _pad_rows8(w["ssd_conv_w"])
    wa_bd = _blockdiag(w["lru_gate_a_w"])
    wx_bd = _blockdiag(w["lru_gate_x_w"])
    ba = w["lru_gate_a_b"].reshape(1, LRU_W)
    bx = w["lru_gate_x_b"].reshape(1, LRU_W)
    bias_pad = _pad_lane(w["ssd_dt_bias"])
    alog_pad = _pad_lane(w["ssd_a_log"])
    dxp = jnp.repeat(w["ssd_d"], HEAD_P, axis=1)

    def host(fn, *a, name, **k):
        jobs = hooks.ride(name)
        res = fn(*a, name=name, jobs=jobs, **k)
        if jobs:
            res, jouts = res
            hooks.done(jobs, jouts, w)
        return res

    def grad(n, val):
        g[n] = val
        hooks.grad(n, val)

    xb = x.astype(BF16)
    proj = host(_mm, xb, w["w_in_t"], "nt", tm=2048, tn=512, name="in_proj")
    ymix, h_lru = host(_lru_fwd, proj, cw_l, w["lru_conv_b"], wa_bd, ba, wx_bd, bx, w["lru_a_param"], name="lru_fwd")
    xact = host(_conv_silu_fwd, proj, cw_s, w["ssd_conv_b"], col0=COL_XBC, width=XBC, ct=256, name="ssd_conv_fwd")
    ycat, hprev = host(_ssd_fwd, xact, proj, ymix, bias_pad, alog_pad, dxp, w["ssd_norm_w"], name="ssd_fwd")
    mix, x1, x1b = _mm_ln(ycat, w["w_out"], x, w["ln1_g"], w["ln1_b"], tm=512, name="out_proj")
    pre = _mm(x1b, w["w_ff1"], "nn", tm=2048, tn=512, out_dtype=BF16, name="ff1")
    ff, x2, x2b = _mm_ln(pre, w["w_ff2"], x1, w["ln2_g"], w["ln2_b"], tm=512, a_fn=_relu2, name="ff2")
    loss, dgpre, dple, dt3, dg3, db3 = _head(x2, x2b, p, w["w_ple_gate"], w["w_ple"], w["ln3_g"], w["ln3_b"], tgt,
                                             name="head")

    g = {}
    g["ln3_g"], g["ln3_b"] = dg3, db3
    grad("w_ple_gate", _mm(x2b, dgpre, "tn", tm=512, tn=1024, out_dtype=BF16, name="d_w_ple_gate"))
    grad("w_ple", _mm(p, dple, "tn", tm=256, tn=512, dest_major=True, out_dtype=BF16, name="d_w_ple"))
    dt2, dt2b, g["ln2_g"], g["ln2_b"] = host(_mm_ln_bwd, dgpre, w["w_ple_gate"], x1, ff, w["ln2_g"], dt3, ALPHA,
                                             tm=512, name="d_x2")
    grad("w_ff2", host(_mm, pre, dt2b, "tn", tm=512, tn=1024, a_fn=_relu2, out_dtype=BF16, name="d_w_ff2"))
    dpre = host(_mm, dt2b, w["w_ff2"], "nt", tm=2048, tn=512, extra=pre, out_dtype=BF16,
                epi=lambda acc, pv: acc * 2.0 * jnp.maximum(pv.astype(F32), 0.0), name="d_pre")
    grad("w_ff1", host(_mm, x1b, dpre, "tn", tm=1024, tn=512, dest_major=True, out_dtype=BF16, name="d_w_ff1"))
    dt1, dt1b, g["ln1_g"], g["ln1_b"] = host(_mm_ln_bwd, dpre, w["w_ff1"], x, mix, w["ln1_g"], dt2, ALPHA,
                                             tm=256, name="d_x1")
    grad("w_out", host(_mm, ycat, dt1b, "tn", tm=512, tn=1024, out_dtype=BF16, name="d_w_out"))
    dycat = host(_mm, dt1b, w["w_out"], "nt", tm=2048, tn=512, name="d_ycat")
    dxl, dgl, dcwb_l, dwa, dwx = host(_lru_bwd, proj, dycat, h_lru, cw_l, w["lru_conv_b"], wa_bd, ba, wx_bd, bx,
                                      w["lru_a_param"], name="lru_bwd")
    g["lru_gate_a_w"] = _unblockdiag(dwa)
    g["lru_gate_x_w"] = _unblockdiag(dwx)
    raw = dict(lru=dcwb_l, gate_a=g["lru_gate_a_w"].reshape(N_HEAD * HEAD_P, HEAD_P).astype(BF16),
               gate_x=g["lru_gate_x_w"].reshape(N_HEAD * HEAD_P, HEAD_P).astype(BF16))
    hooks.small(raw)
    dxact, ddt, dz, g["ssd_norm_w"], small = host(_ssd_bwd, xact, proj, dycat, hprev, bias_pad, alog_pad, dxp,
                                                   w["ssd_norm_w"], name="ssd_bwd")
    dxbc, dcwb_s = host(_conv_silu_bwd, proj, dxact, cw_s, w["ssd_conv_b"], col0=COL_XBC, width=XBC, ct=256,
                        name="ssd_conv_bwd")
    pieces, offsets = [dxl, dgl, dz, dxbc, ddt], [0, COL_G, COL_Z, COL_XBC, COL_DT]

    g["lru_conv_w"] = dcwb_l[0:4]
    g["lru_conv_b"] = dcwb_l[4:5]
    g["lru_gate_a_b"] = dcwb_l[5:6]
    g["lru_gate_x_b"] = dcwb_l[6:7]
    g["lru_a_param"] = dcwb_l[7:8]
    g["ssd_conv_w"] = dcwb_s[0:4]
    g["ssd_conv_b"] = dcwb_s[4:5]
    g["ssd_dt_bias"] = small[0:1, :N_HEAD]
    g["ssd_a_log"] = small[1:2, :N_HEAD]
    g["ssd_d"] = small[2:3, :N_HEAD]
    rows = jnp.concatenate([g[n] for n in ("ssd_norm_w", "ln1_g", "ln1_b", "ln2_g", "ln2_b", "ln3_g", "ln3_b")]
                           + [jnp.broadcast_to(loss[:, 0:1], (1, D_MODEL))], axis=0)
    late = dict(ssd=dcwb_s, heads=small, rows=rows)
    hooks.small(late)
    raw.update(late)
    dwt = None
    for q, (pc, off) in enumerate(zip(pieces, offsets)):
        dwt = host(_mm, pc, xb, "tn", tm=512, tn=1024, out_dtype=BF16, into=(dwt, off, D_IN),
                   name="d_w_in_%d" % q)
    grad("w_in", dwt)
    hooks.pairs_now()
    grad_x = host(_mm_pieces, pieces, offsets, w["w_in_t"], tm=256, extra=dt1, epi=lambda acc, e: acc + ALPHA * e,
                  name="d_x")
    return loss[0, 0], grad_x, g, raw


ANY_SPEC = pl.BlockSpec(memory_space=pl.ANY)


def _mesh_pos():
    return lax.axis_index("x"), lax.axis_index("y"), lax.axis_index("c")


def _remote(src, dst, send, recv, k, to):
    return pltpu.make_async_remote_copy(src_ref=src, dst_ref=dst, send_sem=send.at[k], recv_sem=recv.at[k],
                                        device_id=to, device_id_type=MESH_T)


class _Job:
    N_SEM = 7

    def __init__(self, kind, inp):
        self.kind, self.inp = kind, inp
        shape = {"gather": (N_DEV,) + inp.shape, "pair": (4,) + inp.shape[1:], "chip": inp.shape}[kind]
        self.out = jax.ShapeDtypeStruct(shape, inp.dtype)

    def _blk(self, ref, k):
        return ref.at[k]

    def _places(self):
        x, y, c = _mesh_pos()
        return (x, y, c), (x, y, 1 - c), [(1 - x, y), (x, 1 - y), (1 - x, 1 - y)]

    def start(self, inp, out, send, recv, loc):
        me, sibling, chips = self._places()
        x, y, c = me
        if self.kind == "gather":
            mine = out.at[4 * x + 2 * y + c]
            pltpu.make_async_copy(inp, mine, loc.at[0]).start()
            _remote(inp, mine, send, recv, 0, sibling).start()
            for j, chip in enumerate(chips):
                _remote(inp, mine, send, recv, 1 + j, (*chip, c)).start()
        elif self.kind == "pair":
            for k in range(4):
                _remote(inp.at[2 * k + (1 - c)], out.at[k], send, recv, k, sibling).start()
        else:
            kme = 2 * x + y
            pltpu.make_async_copy(self._blk(inp, kme), self._blk(out, kme), loc.at[0]).start()
            for j, (tx, ty) in enumerate(chips):
                _remote(self._blk(inp, 2 * tx + ty), self._blk(out, kme), send, recv, j, (tx, ty, c)).start()

    def mid(self, inp, out, send, recv, loc):
        if self.kind != "gather":
            return
        me, sibling, chips = self._places()
        c = me[2]
        for j, chip in enumerate(chips):
            landed = out.at[4 * chip[0] + 2 * chip[1] + c]
            _remote(landed, landed, send, recv, 1 + j, me).wait_recv()
            _remote(landed, landed, send, recv, 4 + j, sibling).start()

    def finish(self, inp, out, send, recv, loc):
        me, sibling, chips = self._places()
        x, y, c = me
        if self.kind == "gather":
            blk = lambda px, py, pc: out.at[4 * px + 2 * py + pc]
            mine = blk(*me)
            _remote(inp, blk(*sibling), send, recv, 0, me).wait_recv()
            for j, chip in enumerate(chips):
                _remote(inp, blk(*chip, 1 - c), send, recv, 4 + j, me).wait_recv()
            for k in range(7):
                _remote(inp, mine, send, recv, k, sibling).wait_send()
            pltpu.make_async_copy(inp, mine, loc.at[0]).wait()
        elif self.kind == "pair":
            for k in range(4):
                _remote(inp.at[2 * k + (1 - c)], out.at[k], send, recv, k, sibling).wait()
        else:
            kme = 2 * x + y
            for j, (tx, ty) in enumerate(chips):
                _remote(self._blk(inp, kme), self._blk(out, 2 * tx + ty), send, recv, j, (tx, ty, c)).wait_recv()
            for j, (tx, ty) in enumerate(chips):
                _remote(self._blk(inp, 2 * tx + ty), self._blk(out, kme), send, recv, j, (tx, ty, c)).wait_send()
            pltpu.make_async_copy(self._blk(inp, kme), self._blk(out, kme), loc.at[0]).wait()


def _job_scratch(jobs):
    sem = pltpu.SemaphoreType.DMA
    return [s for _ in jobs for s in (sem((_Job.N_SEM,)), sem((_Job.N_SEM,)), sem((1,)))]


def _run_jobs(jobs, method, jins, jouts, jsems):
    for q, job in enumerate(jobs):
        getattr(job, method)(jins[q], jouts[q], *jsems[3 * q:3 * q + 3])


def _exchange(jobs, *, name):
    n = len(jobs)

    def body(*refs):
        jins, jouts, jsems = refs[:n], refs[n:2 * n], refs[2 * n:]
        _run_jobs(jobs, "start", jins, jouts, jsems)
        _run_jobs(jobs, "mid", jins, jouts, jsems)
        _run_jobs(jobs, "finish", jins, jouts, jsems)

    return _pcall(body, in_specs=[ANY_SPEC] * n, out_specs=[ANY_SPEC] * n, out_shape=[j.out for j in jobs],
                  scratch_shapes=_job_scratch(jobs), name=name)(*[j.inp for j in jobs])


def _hosted(body, jobs, *, grid, in_specs, out_specs, out_shape, args, name, scratch_shapes=(), aliases=None):
    in_specs, out_specs, out_shape = list(in_specs), list(out_specs), list(out_shape)
    scratch_shapes = list(scratch_shapes)
    n_in, n_out, n_scr, nj = len(in_specs), len(out_specs), len(scratch_shapes), len(jobs)
    sem = ("arbitrary",) * len(grid)
    kw = dict(input_output_aliases=aliases) if aliases else {}
    if not jobs:
        res = _pcall(body, grid=grid, in_specs=in_specs, out_specs=out_specs, out_shape=out_shape,
                     scratch_shapes=scratch_shapes, name=name, compiler_params=_cparams(sem), **kw)(*args)
        return list(res), []

    def full(*refs):
        ins, jins = refs[:n_in], refs[n_in:n_in + nj]
        o0 = n_in + nj
        outs, jouts = refs[o0:o0 + n_out], refs[o0 + n_out:o0 + n_out + nj]
        s0 = o0 + n_out + nj
        scr, jsems = refs[s0:s0 + n_scr], refs[s0 + n_scr:]
        step = pl.program_id(0)
        for ax in range(1, len(grid)):
            step = step * grid[ax] + pl.program_id(ax)
        total = math.prod(grid)

        @pl.when(step == 0)
        def _():
            _run_jobs(jobs, "start", jins, jouts, jsems)

        body(*ins, *outs, *scr)

        @pl.when(step == total - 1)
        def _():
            _run_jobs(jobs, "mid", jins, jouts, jsems)
            _run_jobs(jobs, "finish", jins, jouts, jsems)

    res = _pcall(full, grid=grid, in_specs=in_specs + [ANY_SPEC] * nj, out_specs=out_specs + [ANY_SPEC] * nj,
                 out_shape=out_shape + [j.out for j in jobs], scratch_shapes=scratch_shapes + _job_scratch(jobs),
                 name=name, compiler_params=_cparams(sem), **kw)(*args, *[j.inp for j in jobs])
    return list(res[:n_out]), list(res[n_out:])


def _pair_add(g8, r4, cidx, *, name):
    _, r, c = g8.shape
    tr = ROW_TILE if r % ROW_TILE == 0 else r

    def body(c_ref, g_ref, r_ref, o_ref):
        o_ref[...] = (g_ref[...].astype(F32) + r_ref[...].astype(F32)).astype(BF16)

    return _pcall(
        body,
        grid_spec=pltpu.PrefetchScalarGridSpec(
            num_scalar_prefetch=1, grid=(4, r // tr),
            in_specs=[pl.BlockSpec((None, tr, c), lambda k, i, cr: (2 * k + cr[0], i, 0)),
                      pl.BlockSpec((None, tr, c), lambda k, i, cr: (k, i, 0))],
            out_specs=pl.BlockSpec((None, tr, c), lambda k, i, cr: (k, i, 0))),
        out_shape=jax.ShapeDtypeStruct((4, r, c), BF16), name=name,
        compiler_params=_cparams(("parallel", "parallel")))(cidx, g8, r4)


def _adam_update(g, w_ref, m_ref, v_ref, g_ref, d_ref, mo_ref, vo_ref):
    c1 = 1.0 - ADAM_B1 ** ADAM_STEP
    c2 = 1.0 - ADAM_B2 ** ADAM_STEP
    m2 = ADAM_B1 * m_ref[...] + (1.0 - ADAM_B1) * g
    v2 = ADAM_B2 * v_ref[...] + (1.0 - ADAM_B2) * (g * g)
    g_ref[...] = g
    mo_ref[...] = m2
    vo_ref[...] = v2
    d_ref[...] = -ADAM_LR * ((m2 / c1) / (jnp.sqrt(v2 / c2) + ADAM_EPS) + ADAM_WD * w_ref[...])


def _adamw_rows(srcs, items, own_cols, me1, *, name):
    ns, ni, no = len(srcs), len(items), len(own_cols)
    full = lambda a: pl.BlockSpec(a.shape, lambda i, me: (0,) * a.ndim)
    in_specs = [full(a) for a in srcs]
    args = list(srcs)
    for (si, _r0, w, _m, _v) in own_cols:
        a = srcs[si]
        in_specs.append(pl.BlockSpec((N_DEV, a.shape[1], w.shape[1]), lambda i, me: (0, 0, me[0])))
        args.append(a)
    out_specs, out_shape = [], []
    for (_si, _r0, w, m, v) in list(items) + list(own_cols):
        in_specs += [full(w)] * 3
        args += [w, m, v]
        out_specs += [full(w)] * 4
        out_shape += [jax.ShapeDtypeStruct(w.shape, F32)] * 4

    def body(me_ref, *refs):
        src_refs, own_refs = refs[:ns], refs[ns:ns + no]
        wmv = refs[ns + no:ns + no + 3 * (ni + no)]
        outs = refs[ns + no + 3 * (ni + no):]
        for q, (si, r0, w, _m, _v) in enumerate(list(items) + list(own_cols)):
            nr, cw = w.shape
            gref = src_refs[si] if q < ni else own_refs[q - ni]
            g = gref[0, r0:r0 + nr, 0:cw]
            for d in range(1, N_DEV):
                g = g + gref[d, r0:r0 + nr, 0:cw]
            _adam_update(g, *wmv[3 * q:3 * q + 3], *outs[4 * q:4 * q + 4])

    res = _pcall(
        body,
        grid_spec=pltpu.PrefetchScalarGridSpec(num_scalar_prefetch=1, grid=(1,), in_specs=in_specs, out_specs=out_specs),
        out_shape=out_shape, name=name, compiler_params=_cparams(("arbitrary",)))(me1, *args)
    return [tuple(res[4 * q:4 * q + 4]) for q in range(ni + no)]


def _adamw(gsrc, w, m, v, *, name):
    k, r, c = gsrc.shape
    tr = ROW_TILE if r % ROW_TILE == 0 else r

    def body(gs_ref, w_ref, m_ref, v_ref, g_ref, d_ref, mo_ref, vo_ref):
        g = gs_ref[0].astype(F32)
        for q in range(1, k):
            g = g + gs_ref[q].astype(F32)
        _adam_update(g, w_ref, m_ref, v_ref, g_ref, d_ref, mo_ref, vo_ref)

    tc = c
    if tr == r and r > ROW_TILE and c % 256 == 0:
        tc = 256
    blk = pl.BlockSpec((tr, tc), lambda i, j: (i, j))
    sd = jax.ShapeDtypeStruct((r, c), F32)
    return _pcall(body, grid=(r // tr, c // tc),
                  in_specs=[pl.BlockSpec((k, tr, tc), lambda i, j: (0, i, j)), blk, blk, blk],
                  out_specs=(blk, blk, blk, blk), out_shape=(sd, sd, sd, sd), name=name,
                  compiler_params=_cparams(("parallel", "parallel")))(gsrc, w, m, v)


WEIGHTS = ['w_in', 'lru_conv_w', 'lru_conv_b', 'lru_gate_a_w', 'lru_gate_a_b', 'lru_gate_x_w', 'lru_gate_x_b',
           'lru_a_param', 'ssd_conv_w', 'ssd_conv_b', 'ssd_dt_bias', 'ssd_a_log', 'ssd_d', 'ssd_norm_w', 'w_out',
           'ln1_g', 'ln1_b', 'w_ff1', 'w_ff2', 'ln2_g', 'ln2_b', 'w_ple_gate', 'w_ple', 'ln3_g', 'ln3_b']
BIG = ['w_in', 'w_out', 'w_ff1', 'w_ff2', 'w_ple_gate', 'w_ple']
COL_SHARDED = ('w_ff1', 'w_ple')
CONV = ['lru_conv_w', 'ssd_conv_w']
REPL = [n for n in WEIGHTS if n not in BIG and n not in CONV]
CONV_CH = {'lru_conv_w': LRU_W, 'ssd_conv_w': XBC}


def _to_dest_major(name, gfull):
    if name in COL_SHARDED:
        r, cfull = gfull.shape
        return gfull.reshape(r, N_DEV, cfull // N_DEV).transpose(1, 0, 2)
    rfull, cdim = gfull.shape
    return gfull.reshape(N_DEV, rfull // N_DEV, cdim)


def _full_weight(name, gathered):
    if name in COL_SHARDED:
        _, r, cs = gathered.shape
        full = gathered.transpose(1, 0, 2).reshape(r, N_DEV * cs)
    else:
        _, rs, cdim = gathered.shape
        full = gathered.reshape(N_DEV * rs, cdim)
    if name == 'w_in':
        full = lax.dynamic_update_slice(jnp.zeros((D_IN_PAD, D_MODEL), full.dtype), full, (0, 0))
    return full


SMALL_SRC = ("lru", "ssd", "heads", "rows", "gate_a", "gate_x")
AG_HOSTS = {"in_proj": ("w_ff1",), "lru_fwd": ("w_ff2",), "ssd_conv_fwd": ("w_ple_gate", "w_ple"), "ssd_fwd": ("w_out",)}
PAIR_HOSTS = ("d_x2", "d_pre", "d_x1", "d_ycat")
CHIP_HOSTS = {"lru_bwd": ("w_ple_gate", "w_ple", "w_ff2"), "ssd_bwd": ("w_ff1",), "ssd_conv_bwd": ("w_out",),
              "d_x": ("w_in",)}
SMALL_HOSTS = {"ssd_bwd": ("lru", "gate_a", "gate_x"), "d_w_in_3": ("ssd", "heads", "rows")}


class _Schedule:
    def __init__(self, shards, cidx):
        self.shards, self.cidx = shards, cidx
        self.pair, self.chip, self.small_jobs = [], [], []
        self.dest, self.summed, self.gathered_small = {}, {}, {}
        self.tags = []

    def ride(self, host):
        tags = []
        if host in AG_HOSTS:
            tags = [("weight", n, self.shards[n]) for n in AG_HOSTS[host]]
        elif host in PAIR_HOSTS or host in CHIP_HOSTS or host == "flush":
            tags = [("pair", n, a) for n, a in self.pair]
            self.pair = []
            if host not in PAIR_HOSTS:
                take = [t for t in self.chip if host == "flush" or t[0] in CHIP_HOSTS[host]]
                tags += [("chip", n, a) for n, a in take]
                self.chip = [t for t in self.chip if not any(t is u for u in take)]
        if host in SMALL_HOSTS:
            tags += [("small", n, a) for n, a in self.small_jobs if n in SMALL_HOSTS[host]]
            self.small_jobs = [t for t in self.small_jobs if t[0] not in SMALL_HOSTS[host]]
        self.tags = tags
        return [_Job({"weight": "gather", "small": "gather"}.get(kind, kind), a) for kind, _n, a in tags]

    def done(self, jobs, outs, w):
        for (kind, n, _a), o in zip(self.tags, outs):
            if kind == "weight":
                w[n] = _full_weight(n, o)
            elif kind == "small":
                self.gathered_small[n] = o
            elif kind == "pair":
                self.chip.append((n, _pair_add(self.dest[n], o, self.cidx, name="rs_pair_add_" + n)))
            else:
                self.summed[n] = o

    def grad(self, name, val):
        self.dest[name] = val if val.ndim == 3 else _to_dest_major(name, val)
        self.pair.append((name, self.dest[name]))

    def small(self, raw):
        self.small_jobs += list(raw.items())

    def pairs_now(self):
        tags = [("pair", n, a) for n, a in self.pair]
        self.pair, self.tags = [], tags
        jobs = [_Job("pair", a) for _k, _n, a in tags]
        self.done(jobs, _exchange(jobs, name="rs_pairs_now"), None)

    def flush(self):
        step = 0
        while self.pair or self.chip:
            jobs = self.ride("flush")
            self.done(jobs, _exchange(jobs, name="rs_flush_%d" % step), None)
            step += 1


def kernel(x, p, w_in, lru_conv_w, lru_conv_b, lru_gate_a_w, lru_gate_a_b, lru_gate_x_w, lru_gate_x_b, lru_a_param, ssd_conv_w, ssd_conv_b, ssd_dt_bias, ssd_a_log, ssd_d, ssd_norm_w, w_out, ln1_g, ln1_b, w_ff1, w_ff2, ln2_g, ln2_b, w_ple_gate, w_ple, ln3_g, ln3_b, loss_target, m_w_in, m_lru_conv_w, m_lru_conv_b, m_lru_gate_a_w, m_lru_gate_a_b, m_lru_gate_x_w, m_lru_gate_x_b, m_lru_a_param, m_ssd_conv_w, m_ssd_conv_b, m_ssd_dt_bias, m_ssd_a_log, m_ssd_d, m_ssd_norm_w, m_w_out, m_ln1_g, m_ln1_b, m_w_ff1, m_w_ff2, m_ln2_g, m_ln2_b, m_w_ple_gate, m_w_ple, m_ln3_g, m_ln3_b, v_w_in, v_lru_conv_w, v_lru_conv_b, v_lru_gate_a_w, v_lru_gate_a_b, v_lru_gate_x_w, v_lru_gate_x_b, v_lru_a_param, v_ssd_conv_w, v_ssd_conv_b, v_ssd_dt_bias, v_ssd_a_log, v_ssd_d, v_ssd_norm_w, v_w_out, v_ln1_g, v_ln1_b, v_w_ff1, v_w_ff2, v_ln2_g, v_ln2_b, v_w_ple_gate, v_w_ple, v_ln3_g, v_ln3_b):
    given = dict(locals())
    def local(a, n):
        return jnp.swapaxes(a[0], 0, 1) if n == 'w_in' else a[0]

    wsh = {n: local(given[n], n) for n in WEIGHTS}
    msh = {n: local(given["m_" + n], n) for n in WEIGHTS}
    vsh = {n: local(given["v_" + n], n) for n in WEIGHTS}
    xi, yi, ci = _mesh_pos()
    me = 4 * xi + 2 * yi + ci

    shards = {n: wsh[n].astype(BF16) for n in BIG}
    conv_pack = jnp.concatenate([_pad_rows8(wsh[n]) for n in CONV], axis=1)
    g_in, gconv = _exchange([_Job("gather", shards['w_in']), _Job("gather", conv_pack)], name="ag_first")
    full = {'w_in_t': _full_weight('w_in', g_in)}
    c0 = 0
    for n in CONV:
        cw = CONV_CH[n] // N_DEV
        full[n] = gconv[:, :4, c0:c0 + cw].transpose(1, 0, 2).reshape(4, CONV_CH[n])
        c0 += cw
    for n in REPL:
        full[n] = given[n] if given[n].ndim == 2 else wsh[n]

    sched = _Schedule(shards, jnp.reshape(ci, (1,)).astype(jnp.int32))
    loss_local, grad_x, g, raw = _local_step(x[0], p[0, 0], loss_target[0], full, sched)
    sched.flush()
    summed, gat = sched.summed, sched.gathered_small
    loss = gat["rows"][0, 7, 0]
    for d in range(1, N_DEV):
        loss = loss + gat["rows"][d, 7, 0]

    outs = {}
    for n in BIG:
        outs[n] = _adamw(summed[n], wsh[n], msh[n], vsh[n], name="adamw_" + n)
    for n, k in (("lru_gate_a_w", "gate_a"), ("lru_gate_x_w", "gate_x")):
        flat = lambda a: a.reshape(N_HEAD * HEAD_P, HEAD_P)
        res = _adamw(gat[k], flat(wsh[n]), flat(msh[n]), flat(vsh[n]), name="adamw_" + n)
        outs[n] = tuple(r.reshape(N_HEAD, HEAD_P, HEAD_P) for r in res)
    row_items = [("lru_conv_b", 0, 4), ("lru_gate_a_b", 0, 5), ("lru_gate_x_b", 0, 6), ("lru_a_param", 0, 7),
                 ("ssd_conv_b", 1, 4), ("ssd_dt_bias", 2, 0), ("ssd_a_log", 2, 1), ("ssd_d", 2, 2),
                 ("ssd_norm_w", 3, 0), ("ln1_g", 3, 1), ("ln1_b", 3, 2), ("ln2_g", 3, 3), ("ln2_b", 3, 4),
                 ("ln3_g", 3, 5), ("ln3_b", 3, 6)]
    vec = lambda a: a.reshape(1, -1)
    items = [(si, r0, vec(given[n]), vec(given["m_" + n]), vec(given["v_" + n])) for n, si, r0 in row_items]
    own = [(si, 0, wsh[n], msh[n], vsh[n]) for n, si in (("lru_conv_w", 0), ("ssd_conv_w", 1))]
    me1 = jnp.reshape(me, (1,)).astype(jnp.int32)
    res = _adamw_rows([gat[k] for k in SMALL_SRC[:4]], items, own, me1, name="adamw_small")
    for (n, _si, _r0), r4 in zip(row_items, res[:len(row_items)]):
        outs[n] = r4
    for n, r4 in zip(CONV, res[len(row_items):]):
        outs[n] = r4

    def fin(n, k):
        a = jnp.swapaxes(outs[n][k], 0, 1) if n == 'w_in' else outs[n][k]
        return a.reshape(given[n].shape)

    return (loss, grad_x[None],
            *[fin(n, 0) for n in WEIGHTS], *[fin(n, 1) for n in WEIGHTS],
            *[fin(n, 2) for n in WEIGHTS], *[fin(n, 3) for n in WEIGHTS])
```

```python
import math

import jax
import jax.numpy as jnp
from jax import lax
from jax.experimental import pallas as pl
from jax.experimental.pallas import tpu as pltpu

F32 = jnp.float32
BF16 = jnp.bfloat16
HI = lax.Precision.HIGHEST

N_DEV = 8
D_MODEL = 1024
LRU_W = 1024
SSD_W = 1024
XBC = 2048
N_HEAD = 16
HEAD_P = 64
N_GROUP = 4
GROUP_W = 256
N_STATE = 128
CHUNK = 128
D_FF = 4096
PLE_DIM = 256
D_IN = 5136
D_IN_PAD = 5632
COL_G = 1024
COL_Z = 2048
COL_XBC = 3072
COL_DT = 5120
LRU_C = 8.0
ALPHA = 2.0 ** 0.25
LN_EPS = 1e-5
RMS_EPS = 1e-5
ADAM_LR = 0.001
ADAM_B1 = 0.9
ADAM_B2 = 0.999
ADAM_EPS = 1e-08
ADAM_WD = 0.01
ADAM_STEP = 10
GELU_C = math.sqrt(2.0 / math.pi)
LANE = 128
SUBLANE = 8
VMEM_LIMIT = 48 * 1024 * 1024
MESH_T = pl.DeviceIdType.MESH
NEG_BIG = -1e30


def _pcall(body, **kw):
    return pl.pallas_call(body, **kw)


def _cparams(sem):
    return pltpu.CompilerParams(dimension_semantics=sem, vmem_limit_bytes=VMEM_LIMIT)


def _dot(a, b):
    return jnp.dot(a.astype(BF16), b.astype(BF16), preferred_element_type=F32)


def _dot_nt(a, b):
    return lax.dot_general(a.astype(BF16), b.astype(BF16), (((1,), (1,)), ((), ())), preferred_element_type=F32)


def _dot_tn(a, b):
    return lax.dot_general(a.astype(BF16), b.astype(BF16), (((0,), (0,)), ((), ())), preferred_element_type=F32)


def _dotx(a, b):
    return jnp.dot(a, b, precision=HI, preferred_element_type=F32)


def _sigmoid(x):
    return jax.nn.sigmoid(x)


def _softplus(v):
    return jnp.maximum(v, 0.0) + jnp.log1p(jnp.exp(-jnp.abs(v)))


def _gelu(x):
    th = jnp.tanh(GELU_C * (x + 0.044715 * x * x * x))
    return 0.5 * x * (1.0 + th), th


def _gelu_grad(x, th):
    return 0.5 * (1.0 + th) + 0.5 * x * (1.0 - th * th) * GELU_C * (1.0 + 3.0 * 0.044715 * x * x)


def _iota(shape, dim):
    return lax.broadcasted_iota(jnp.int32, shape, dim)


def _mm(a, b, mode, *, tm, tn, name, a_fn=None, extra=None, epi=None, out_dtype=F32, dest_major=False, into=None,
        jobs=()):
    m = a.shape[1] if mode == "tn" else a.shape[0]
    n = b.shape[0] if mode == "nt" else b.shape[1]
    tm, tn = min(tm, m), min(tn, n)
    if dest_major:
        tn = n // N_DEV
    if mode == "nn":
        m, k = a.shape
        _, n = b.shape
        a_spec = pl.BlockSpec((tm, k), lambda i, j: (i, 0))
        b_spec = pl.BlockSpec((k, tn), lambda i, j: (0, j))
        dims = ((1,), (0,))
    elif mode == "nt":
        m, k = a.shape
        n, _ = b.shape
        a_spec = pl.BlockSpec((tm, k), lambda i, j: (i, 0))
        b_spec = pl.BlockSpec((tn, k), lambda i, j: (j, 0))
        dims = ((1,), (1,))
    else:
        k, m = a.shape
        _, n = b.shape
        a_spec = pl.BlockSpec((k, tm), lambda i, j: (0, i))
        b_spec = pl.BlockSpec((k, tn), lambda i, j: (0, j))
        dims = ((0,), (0,))
    assert m % tm == 0 and n % tn == 0, (name, m, n, tm, tn)
    o_spec = pl.BlockSpec((tm, tn), lambda i, j: (i, j))
    in_specs = [a_spec, b_spec]
    args = [a, b]
    if extra is not None:
        in_specs.append(o_spec)
        args.append(extra)

    def body(*refs):
        a_ref, b_ref, o_ref = refs[0], refs[1], refs[-1]
        av = a_ref[...]
        if a_fn is not None:
            av = a_fn(av)
        acc = lax.dot_general(av.astype(BF16), b_ref[...].astype(BF16), (dims, ((), ())), preferred_element_type=F32)
        if epi is not None:
            acc = epi(acc, refs[2][...])
        o_ref[...] = acc.astype(out_dtype)

    out_shape = jax.ShapeDtypeStruct((m, n), out_dtype)
    aliases = None
    if dest_major:
        assert extra is None
        o_spec = pl.BlockSpec((None, tm, tn), lambda i, j: (j, i, 0))
        out_shape = jax.ShapeDtypeStruct((N_DEV, m, tn), out_dtype)
    if into is not None:
        buf, row0, total = into
        assert extra is None and row0 % tm == 0
        o_spec = pl.BlockSpec((tm, tn), lambda i, j: (row0 // tm + i, j))
        out_shape = jax.ShapeDtypeStruct((total, n), out_dtype)
        if buf is not None:
            in_specs.append(ANY_SPEC)
            args.append(buf)
            aliases = {len(args) - 1: 0}
    (out,), jouts = _hosted(body, jobs, grid=(m // tm, n // tn), in_specs=in_specs, out_specs=[o_spec],
                            out_shape=[out_shape], args=args, name=name, aliases=aliases)
    return (out, jouts) if jobs else out


def _mm_pieces(pieces, offsets, b, *, tm, name, extra, epi, jobs=()):
    m = pieces[0].shape[0]
    kb, n = b.shape
    tm = min(tm, m)
    row = lambda wdt: pl.BlockSpec((tm, wdt), lambda i: (i, 0))
    in_specs = [row(pc.shape[1]) for pc in pieces] + [pl.BlockSpec((kb, n), lambda i: (0, 0)), row(n)]
    np_ = len(pieces)

    def body(*refs):
        b_ref, e_ref, o_ref = refs[np_], refs[np_ + 1], refs[np_ + 2]
        acc = jnp.zeros((tm, n), F32)
        for q in range(np_):
            kq = pieces[q].shape[1]
            acc = acc + jnp.dot(refs[q][...].astype(BF16), b_ref[offsets[q]:offsets[q] + kq, :].astype(BF16),
                                preferred_element_type=F32)
        o_ref[...] = epi(acc, e_ref[...])

    (out,), jouts = _hosted(body, jobs, grid=(m // tm,), in_specs=in_specs, out_specs=[row(n)],
                            out_shape=[jax.ShapeDtypeStruct((m, n), F32)], args=list(pieces) + [b, extra], name=name)
    return (out, jouts) if jobs else out


def _relu2(v):
    r = jnp.maximum(v, 0.0)
    return r * r


ROW_TILE = 256


def _ln_stats(t):
    mu = jnp.mean(t, axis=-1, keepdims=True)
    xc = t - mu
    var = jnp.mean(xc * xc, axis=-1, keepdims=True)
    rstd = lax.rsqrt(var + LN_EPS)
    return xc * rstd, rstd


def _ln_bwd_rows(dy, xhat, rstd, g):
    dxh = dy * g
    m1 = jnp.mean(dxh, axis=-1, keepdims=True)
    m2 = jnp.mean(dxh * xhat, axis=-1, keepdims=True)
    return rstd * (dxh - m1 - xhat * m2)


def _mm_ln(a, b, res, g, beta, *, tm, name, a_fn=None):
    m, k = a.shape
    d = b.shape[1]
    tm = min(tm, m)
    row = pl.BlockSpec((tm, d), lambda i: (i, 0))
    par = pl.BlockSpec((1, d), lambda i: (0, 0))

    def body(a_ref, b_ref, r_ref, g_ref, be_ref, br_ref, y_ref, yb_ref):
        av = a_ref[...]
        if a_fn is not None:
            av = a_fn(av)
        acc = jnp.dot(av.astype(BF16), b_ref[...].astype(BF16), preferred_element_type=F32)
        br_ref[...] = acc
        xhat, _ = _ln_stats(ALPHA * r_ref[...] + acc)
        y = xhat * g_ref[...] + be_ref[...]
        y_ref[...] = y
        yb_ref[...] = y.astype(BF16)

    sd = jax.ShapeDtypeStruct((m, d), F32)
    return _pcall(body, grid=(m // tm,),
                  in_specs=[pl.BlockSpec((tm, k), lambda i: (i, 0)), pl.BlockSpec((k, d), lambda i: (0, 0)), row, par, par],
                  out_specs=(row, row, row), out_shape=(sd, sd, jax.ShapeDtypeStruct((m, d), BF16)), name=name,
                  compiler_params=_cparams(("parallel",)))(a, b, res, g, beta)


def _mm_ln_bwd(a, b, res, branch, g, dy0, coef0, *, tm, name, jobs=()):
    m, k = a.shape
    d = b.shape[0]
    tm = min(tm, m)
    row = pl.BlockSpec((tm, d), lambda i: (i, 0))
    par = pl.BlockSpec((1, d), lambda i: (0, 0))

    def body(a_ref, b_ref, r_ref, br_ref, g_ref, dy0_ref, dt_ref, dtb_ref, dg_ref, db_ref):
        acc = lax.dot_general(a_ref[...].astype(BF16), b_ref[...].astype(BF16), (((1,), (1,)), ((), ())),
                              preferred_element_type=F32)
        dy = coef0 * dy0_ref[...] + acc
        xhat, rstd = _ln_stats(ALPHA * r_ref[...] + br_ref[...])
        dt = _ln_bwd_rows(dy, xhat, rstd, g_ref[...])
        dt_ref[...] = dt
        dtb_ref[...] = dt.astype(BF16)

        @pl.when(pl.program_id(0) == 0)
        def _():
            dg_ref[...] = jnp.zeros_like(dg_ref)
            db_ref[...] = jnp.zeros_like(db_ref)

        dg_ref[...] += jnp.sum(dy * xhat, axis=0, keepdims=True)
        db_ref[...] += jnp.sum(dy, axis=0, keepdims=True)

    pd = jax.ShapeDtypeStruct((1, d), F32)
    outs, jouts = _hosted(
        body, jobs, grid=(m // tm,),
        in_specs=[pl.BlockSpec((tm, k), lambda i: (i, 0)), pl.BlockSpec((d, k), lambda i: (0, 0)), row, row, par, row],
        out_specs=(row, row, par, par),
        out_shape=(jax.ShapeDtypeStruct((m, d), F32), jax.ShapeDtypeStruct((m, d), BF16), pd, pd),
        args=(a, b, res, branch, g, dy0), name=name)
    return (tuple(outs), jouts) if jobs else tuple(outs)


def _head(x2, x2b, p, wg, wp, g, beta, tgt, *, name):
    s, d = x2.shape
    tile = 2 * ROW_TILE
    row = pl.BlockSpec((tile, d), lambda i: (i, 0))
    par = pl.BlockSpec((1, d), lambda i: (0, 0))
    lsp = pl.BlockSpec((1, LANE), lambda i: (0, 0))
    whole = lambda a: pl.BlockSpec(a.shape, lambda i: (0, 0))

    def body(x2_ref, x2b_ref, p_ref, wg_ref, wp_ref, g_ref, be_ref, t_ref,
             loss_ref, dgp_ref, dple_ref, dt_ref, dg_ref, db_ref):
        gate = _sigmoid(_dot(x2b_ref[...], wg_ref[...]))
        ple_v = _dot(p_ref[...], wp_ref[...])
        xhat, rstd = _ln_stats(ALPHA * x2_ref[...] + gate * ple_v)
        err = xhat * g_ref[...] + be_ref[...] - t_ref[...]
        dy = err * (1.0 / d)
        dt = _ln_bwd_rows(dy, xhat, rstd, g_ref[...])
        dt_ref[...] = dt
        dgp_ref[...] = (dt * ple_v * gate * (1.0 - gate)).astype(BF16)
        dple_ref[...] = (dt * gate).astype(BF16)

        @pl.when(pl.program_id(0) == 0)
        def _():
            loss_ref[...] = jnp.zeros_like(loss_ref)
            dg_ref[...] = jnp.zeros_like(dg_ref)
            db_ref[...] = jnp.zeros_like(db_ref)

        loss_ref[...] += 0.5 * jnp.sum(jnp.mean(err * err, axis=-1, keepdims=True))
        dg_ref[...] += jnp.sum(dy * xhat, axis=0, keepdims=True)
        db_ref[...] += jnp.sum(dy, axis=0, keepdims=True)

    sd = jax.ShapeDtypeStruct((s, d), F32)
    sb = jax.ShapeDtypeStruct((s, d), BF16)
    pd = jax.ShapeDtypeStruct((1, d), F32)
    return _pcall(body, grid=(s // tile,),
                  in_specs=[row, row, pl.BlockSpec((tile, p.shape[1]), lambda i: (i, 0)), whole(wg), whole(wp), par, par,
                            row],
                  out_specs=(lsp, row, row, row, par, par),
                  out_shape=(jax.ShapeDtypeStruct((1, LANE), F32), sb, sb, sd, pd, pd),
                  name=name, compiler_params=_cparams(("arbitrary",)))(x2, x2b, p, wg, wp, g, beta, tgt)


CONV_R = 256
PAD = SUBLANE


def _shift_down(ext, s):
    if s == 0:
        return ext[PAD:, :]
    return pltpu.roll(ext, s, 0)[PAD:, :]


def _shift_up(ext, s):
    r = ext.shape[0] - PAD
    if s == 0:
        return ext[:r, :]
    return pltpu.roll(ext, r + PAD - s, 0)[:r, :]


def _conv_rows(xpad_ref, r0, w_ref):
    ext = xpad_ref[pl.ds(r0, CONV_R + PAD), :]
    acc = _shift_down(ext, 0) * w_ref[3:4, :]
    for k in range(3):
        acc = acc + _shift_down(ext, 3 - k) * w_ref[k:k + 1, :]
    return acc, ext


def _fill_front_padded(dst_ref, src_ref, s):
    dst_ref[0:PAD, :] = jnp.zeros((PAD, dst_ref.shape[1]), F32)

    def cp(q, _):
        r0 = pl.multiple_of(q * CONV_R, CONV_R)
        dst_ref[pl.ds(pl.multiple_of(PAD + r0, PAD), CONV_R), :] = src_ref[pl.ds(r0, CONV_R), :]
        return 0

    lax.fori_loop(0, s // CONV_R, cp, 0)


def _conv_silu_fwd(proj, w8, b, *, col0, width, ct, name, jobs=()):
    s = proj.shape[0]
    nb = col0 // ct

    def body(x_ref, w_ref, b_ref, o_ref, xpad):
        _fill_front_padded(xpad, x_ref, s)

        def step(q, _):
            r0 = pl.multiple_of(q * CONV_R, CONV_R)
            acc, _e = _conv_rows(xpad, r0, w_ref)
            pre = acc + b_ref[...]
            o_ref[pl.ds(r0, CONV_R), :] = pre * _sigmoid(pre)
            return 0

        lax.fori_loop(0, s // CONV_R, step, 0)

    (out,), jouts = _hosted(
        body, jobs, grid=(width // ct,),
        in_specs=[pl.BlockSpec((s, ct), lambda j: (0, nb + j)), pl.BlockSpec((SUBLANE, ct), lambda j: (0, j)),
                  pl.BlockSpec((1, ct), lambda j: (0, j))],
        out_specs=[pl.BlockSpec((s, ct), lambda j: (0, j))],
        out_shape=[jax.ShapeDtypeStruct((s, width), F32)],
        scratch_shapes=[pltpu.VMEM((s + PAD, ct), F32)], name=name, args=(proj, w8, b))
    return (out, jouts) if jobs else out


def _conv_bwd_rows(dpad_ref, r0, w_ref):
    return _conv_bwd_ext(dpad_ref[pl.ds(r0, CONV_R + PAD), :], w_ref)


def _conv_bwd_ext(ext, w_ref):
    acc = _shift_up(ext, 0) * w_ref[3:4, :]
    for k in range(3):
        acc = acc + _shift_up(ext, 3 - k) * w_ref[k:k + 1, :]
    return acc


def _conv_silu_bwd(proj, dact, w8, b, *, col0, width, ct, name, jobs=()):
    s = proj.shape[0]
    nb = col0 // ct

    def body(x_ref, d_ref, w_ref, b_ref, dx_ref, dwb_ref, xpad, dpad):
        _fill_front_padded(xpad, x_ref, s)
        dpad[pl.ds(s, PAD), :] = jnp.zeros((PAD, ct), F32)
        dwb_ref[...] = jnp.zeros_like(dwb_ref)

        def step(q, _):
            r0 = pl.multiple_of(q * CONV_R, CONV_R)
            acc, ext = _conv_rows(xpad, r0, w_ref)
            pre = acc + b_ref[...]
            sg = _sigmoid(pre)
            dpre = d_ref[pl.ds(r0, CONV_R), :] * sg * (1.0 + pre * (1.0 - sg))
            dpad[pl.ds(r0, CONV_R), :] = dpre
            for k in range(4):
                dwb_ref[k:k + 1, :] += jnp.sum(dpre * _shift_down(ext, 3 - k), axis=0, keepdims=True)
            dwb_ref[4:5, :] += jnp.sum(dpre, axis=0, keepdims=True)
            return 0

        lax.fori_loop(0, s // CONV_R, step, 0)

        def step2(q, _):
            r0 = pl.multiple_of(q * CONV_R, CONV_R)
            dx_ref[pl.ds(r0, CONV_R), :] = _conv_bwd_rows(dpad, r0, w_ref).astype(BF16)
            return 0

        lax.fori_loop(0, s // CONV_R, step2, 0)

    colb = pl.BlockSpec((s, ct), lambda j: (0, j))
    outs, jouts = _hosted(
        body, jobs, grid=(width // ct,),
        in_specs=[pl.BlockSpec((s, ct), lambda j: (0, nb + j)), colb, pl.BlockSpec((SUBLANE, ct), lambda j: (0, j)),
                  pl.BlockSpec((1, ct), lambda j: (0, j))],
        out_specs=(colb, pl.BlockSpec((SUBLANE, ct), lambda j: (0, j))),
        out_shape=(jax.ShapeDtypeStruct((s, width), BF16), jax.ShapeDtypeStruct((SUBLANE, width), F32)),
        scratch_shapes=[pltpu.VMEM((s + PAD, ct), F32), pltpu.VMEM((s + PAD, ct), F32)], name=name,
        args=(proj, dact, w8, b))
    return (tuple(outs), jouts) if jobs else tuple(outs)


LRU_CT = 128


def _row_of(v, r):
    return jnp.sum(jnp.where(_iota((v.shape[0], 1), 0) == r, v, 0.0), axis=0, keepdims=True)


def _scan_fwd(a, u):
    r = a.shape[0]
    row = _iota((r, 1), 0)
    d = 1
    while d < r:
        valid = row >= d
        u = jnp.where(valid, a * pltpu.roll(u, d, 0) + u, u)
        a = jnp.where(valid, a * pltpu.roll(a, d, 0), a)
        d *= 2
    return a, u


def _scan_rev(b, u):
    r = b.shape[0]
    row = _iota((r, 1), 0)
    d = 1
    while d < r:
        valid = row < r - d
        u = jnp.where(valid, b * pltpu.roll(u, r - d, 0) + u, u)
        b = jnp.where(valid, b * pltpu.roll(b, r - d, 0), b)
        d *= 2
    return b, u


def _lru_chunk(xpad, r0, cw_ref, cb, wa, ba, wx, bx, sp):
    acc, ext = _conv_rows(xpad, r0, cw_ref)
    xl = acc + cb
    r = _sigmoid(_dot(xl, wa) + ba)
    i = _sigmoid(_dot(xl, wx) + bx)
    la = -LRU_C * r * sp
    a = jnp.exp(la)
    a2 = jnp.exp(2.0 * la)
    mult = jnp.sqrt(-jnp.tanh(la) * (a2 + 1.0))
    first = (r0 + _iota((CONV_R, 1), 0)) == 0
    mult = jnp.where(first, 1.0, mult)
    return ext, xl, r, i, a, a2, mult, first


def _lru_specs(s):
    ct = LRU_CT
    nb_g = COL_G // ct
    return dict(
        x=pl.BlockSpec((s, ct), lambda j: (0, j)),
        g=pl.BlockSpec((s, ct), lambda j: (0, nb_g + j)),
        col=pl.BlockSpec((s, ct), lambda j: (0, j)),
        cw=pl.BlockSpec((SUBLANE, ct), lambda j: (0, j)),
        vec=pl.BlockSpec((1, ct), lambda j: (0, j)),
        gate=pl.BlockSpec((None, ct, ct), lambda j: (j, 0, 0)),
    )


def _lru_fwd(proj, cw8, cb, wa_bd, ba, wx_bd, bx, ap, *, name, jobs=()):
    s = proj.shape[0]
    ct = LRU_CT
    sp_ = _lru_specs(s)

    def body(x_ref, g_ref, cw_ref, cb_ref, wa_ref, ba_ref, wx_ref, bx_ref, ap_ref, y_ref, h_ref, xpad):
        _fill_front_padded(xpad, x_ref, s)
        sp = _softplus(-ap_ref[...])

        def step(q, carry):
            r0 = pl.multiple_of(q * CONV_R, CONV_R)
            _e, xl, _r, i, a, _a2, mult, _f = _lru_chunk(xpad, r0, cw_ref, cb_ref[...], wa_ref[...], ba_ref[...],
                                                       wx_ref[...], bx_ref[...], sp)
            acum, ucum = _scan_fwd(a, xl * i * mult)
            h = acum * carry + ucum
            h_ref[pl.ds(r0, CONV_R), :] = h
            ge, _th = _gelu(g_ref[pl.ds(r0, CONV_R), :])
            y_ref[pl.ds(r0, CONV_R), :] = (ge * h).astype(BF16)
            return _row_of(h, CONV_R - 1)

        lax.fori_loop(0, s // CONV_R, step, jnp.zeros((1, ct), F32))

    (ymix, hs), jouts = _hosted(
        body, jobs, grid=(LRU_W // ct,),
        in_specs=[sp_["x"], sp_["g"], sp_["cw"], sp_["vec"], sp_["gate"], sp_["vec"], sp_["gate"], sp_["vec"], sp_["vec"]],
        out_specs=(sp_["col"], sp_["col"]),
        out_shape=(jax.ShapeDtypeStruct((s, LRU_W + SSD_W), BF16), jax.ShapeDtypeStruct((s, LRU_W), F32)),
        scratch_shapes=[pltpu.VMEM((s + PAD, ct), F32)],
        name=name, args=(proj, proj, cw8, cb, wa_bd, ba, wx_bd, bx, ap))
    return ((ymix, hs), jouts) if jobs else (ymix, hs)


def _lru_bwd(proj, dy, hs, cw8, cb, wa_bd, ba, wx_bd, bx, ap, *, name, jobs=()):
    s = proj.shape[0]
    ct = LRU_CT
    sp_ = _lru_specs(s)

    nq = s // CONV_R

    def body(x_ref, g_ref, dy_ref, h_ref, cw_ref, cb_ref, wa_ref, ba_ref, wx_ref, bx_ref, ap_ref,
             dx_ref, dg_ref, dcwb_ref, dwa_ref, dwx_ref, xpad, hpad):
        _fill_front_padded(xpad, x_ref, s)
        _fill_front_padded(hpad, h_ref, s)
        apv = ap_ref[...]
        sp = _softplus(-apv)
        cb_v, wa, ba_v, wx, bx_v = cb_ref[...], wa_ref[...], ba_ref[...], wx_ref[...], bx_ref[...]
        dcwb_ref[...] = jnp.zeros_like(dcwb_ref)
        dwa_ref[...] = jnp.zeros_like(dwa_ref)
        dwx_ref[...] = jnp.zeros_like(dwx_ref)

        def back(k, carry):
            g_next, a_next, dxl_next = carry
            last_row = _iota((CONV_R, 1), 0) == CONV_R - 1
            r0 = pl.multiple_of((nq - 1 - k) * CONV_R, CONV_R)
            ext, xl, r, i, a, a2, mult, first = _lru_chunk(xpad, r0, cw_ref, cb_v, wa, ba_v, wx, bx_v, sp)
            gv = g_ref[pl.ds(r0, CONV_R), :]
            dyv = dy_ref[pl.ds(r0, CONV_R), :]
            hext = hpad[pl.ds(r0, CONV_R + PAD), :]
            ge, th = _gelu(gv)
            dg_ref[pl.ds(r0, CONV_R), :] = (dyv * _shift_down(hext, 0) * _gelu_grad(gv, th)).astype(BF16)
            b = jnp.where(last_row, a_next, pltpu.roll(a, CONV_R - 1, 0))
            bcum, dcum = _scan_rev(b, dyv * ge)
            gval = dcum + bcum * g_next
            hprev = _shift_down(hext, 1)
            da = gval * hprev
            dxl = gval * i * mult
            di = gval * xl * mult
            dmult = jnp.where(first, 0.0, gval * xl * i)
            dla = da * a - dmult * a2 / mult
            dr = dla * (-LRU_C) * sp
            dcwb_ref[7:8, :] += jnp.sum(dla * (-LRU_C) * r, axis=0, keepdims=True)
            dpr = dr * r * (1.0 - r)
            dpi = di * i * (1.0 - i)
            dxl = dxl + _dot_nt(dpr, wa) + _dot_nt(dpi, wx)
            dwa_ref[...] += _dot_tn(xl, dpr)
            dwx_ref[...] += _dot_tn(xl, dpi)
            dcwb_ref[5:6, :] += jnp.sum(dpr, axis=0, keepdims=True)
            dcwb_ref[6:7, :] += jnp.sum(dpi, axis=0, keepdims=True)
            for tap in range(4):
                dcwb_ref[tap:tap + 1, :] += jnp.sum(dxl * _shift_down(ext, 3 - tap), axis=0, keepdims=True)
            dcwb_ref[4:5, :] += jnp.sum(dxl, axis=0, keepdims=True)
            dx_ref[pl.ds(r0, CONV_R), :] = _conv_bwd_ext(jnp.concatenate([dxl, dxl_next], axis=0), cw_ref).astype(BF16)
            return _row_of(gval, 0), _row_of(a, 0), dxl[:PAD, :]

        zero = jnp.zeros((1, ct), F32)
        lax.fori_loop(0, nq, back, (zero, zero, jnp.zeros((PAD, ct), F32)))
        dcwb_ref[7:8, :] = dcwb_ref[7:8, :] * (-_sigmoid(-apv))

    nt = LRU_W // ct
    outs, jouts = _hosted(
        body, jobs, grid=(nt,),
        in_specs=[sp_["x"], sp_["g"], sp_["col"], sp_["col"], sp_["cw"], sp_["vec"], sp_["gate"], sp_["vec"], sp_["gate"],
                  sp_["vec"], sp_["vec"]],
        out_specs=(sp_["col"], sp_["col"], sp_["cw"], sp_["gate"], sp_["gate"]),
        out_shape=(jax.ShapeDtypeStruct((s, LRU_W), BF16), jax.ShapeDtypeStruct((s, LRU_W), BF16),
                   jax.ShapeDtypeStruct((SUBLANE, LRU_W), F32), jax.ShapeDtypeStruct((nt, ct, ct), F32),
                   jax.ShapeDtypeStruct((nt, ct, ct), F32)),
        scratch_shapes=[pltpu.VMEM((s + PAD, ct), F32), pltpu.VMEM((s + PAD, ct), F32)],
        name=name, args=(proj, proj, dy, hs, cw8, cb, wa_bd, ba, wx_bd, bx, ap))
    return (tuple(outs), jouts) if jobs else tuple(outs)


def _split3(v):
    hi = v.astype(BF16)
    r1 = v - hi.astype(F32)
    mid = r1.astype(BF16)
    lo = (r1 - mid.astype(F32)).astype(BF16)
    return hi, mid, lo


def _dot01(m01, v):
    mb = m01.astype(BF16)
    hi, mid, lo = _split3(v)
    f = lambda part: jnp.dot(mb, part, preferred_element_type=F32)
    return f(hi) + f(mid) + f(lo)


def _dot01_r(v, m01, parts=3):
    mb = m01.astype(BF16)
    acc = None
    for part in _split3(v)[:parts]:
        t = jnp.dot(part, mb, preferred_element_type=F32)
        acc = t if acc is None else acc + t
    return acc


def _ssd_prep(dtr, bias, alog_pad):
    l = CHUNK
    lane = _iota((1, LANE), 1)
    a_head = jnp.where(lane < N_HEAD, -jnp.exp(alog_pad), 0.0)
    dt = _softplus(dtr + bias)
    tril = (_iota((l, l), 1) <= _iota((l, l), 0)).astype(F32)
    a = dt * a_head
    cs = _dot01(tril, a)
    tot = jnp.sum(a, axis=0, keepdims=True)
    return dict(a_head=a_head, dt=dt, tril=tril, cs=cs, tot=tot)


def _col(v, h):
    lane = _iota(v.shape, 1)
    return jnp.sum(jnp.where(lane == h, v, 0.0), axis=1, keepdims=True)


def _decay_mat(cs, cst_ref, h, causal):
    row = cst_ref[h:h + 1, :]
    return jnp.exp(jnp.where(causal, _col(cs, h) - row, NEG_BIG))


def _head_mask(j, rows=CHUNK):
    lane = _iota((rows, GROUP_W), 1)
    return (lane >= j * HEAD_P) & (lane < (j + 1) * HEAD_P)


def _over_heads(v, g):
    r = v.shape[0]
    out = jnp.zeros((r, GROUP_W), F32)
    for j in range(4):
        out = jnp.where(_head_mask(j, r), _col(v, 4 * g + j), out)
    return out


def _ssd_group_fwd(q, g, xs_g, bg, cg, ht_g, cst_ref, causal, dx_g):
    dtx_g, csx_g, totx_g = _over_heads(q["dt"], g), _over_heads(q["cs"], g), _over_heads(q["tot"], g)
    xdt = xs_g * dtx_g
    ex = jnp.exp(csx_g)
    cb = _dot_nt(cg, bg)
    yoff = _dot(cg, ht_g) * ex
    ydiag = jnp.zeros((CHUNK, GROUP_W), F32)
    lms = []
    for j in range(4):
        lms.append(_decay_mat(q["cs"], cst_ref, 4 * g + j, causal))
        ydiag = jnp.where(_head_mask(j), _dot(cb * lms[j], xdt), ydiag)
    y = ydiag + yoff + xs_g * dx_g
    dsx = jnp.exp(totx_g - csx_g)
    return y, dict(xdt=xdt, ex=ex, cb=cb, yoff=yoff, dsx=dsx, dtx=dtx_g, totx=totx_g, lms=lms)


def _gated_norm_fwd(y_g, z_g, w_g):
    sz = _sigmoid(z_g)
    silu = z_g * sz
    yf = y_g * silu
    rs = lax.rsqrt(jnp.mean(yf * yf, axis=1, keepdims=True) + RMS_EPS)
    yn = yf * rs
    return yn * w_g, (sz, silu, rs, yn)


def _ssd_fwd(xact, proj, ymix, bias_pad, alog_pad, dxp, normw, *, name, jobs=()):
    s = xact.shape[0]
    nc = s // CHUNK

    def body(xa_ref, dt_ref, z_ref, _ymix_ref, bias_ref, alp_ref, dx_ref, nw_ref, y_ref, hp_ref, ht, cst):
        @pl.when(pl.program_id(0) == 0)
        def _():
            ht[...] = jnp.zeros_like(ht)

        hp_ref[...] = ht[...]
        q = _ssd_prep(dt_ref[...], bias_ref[...], alp_ref[...])
        cst[...] = q["cs"].T
        causal = q["tril"] > 0.0
        for g in range(N_GROUP):
            sl = slice(g * GROUP_W, (g + 1) * GROUP_W)
            xs_g = xa_ref[:, sl]
            bg = xa_ref[:, SSD_W + g * N_STATE:SSD_W + (g + 1) * N_STATE]
            cg = xa_ref[:, SSD_W + N_GROUP * N_STATE + g * N_STATE:SSD_W + N_GROUP * N_STATE + (g + 1) * N_STATE]
            ht_g = ht[:, sl]
            y, f = _ssd_group_fwd(q, g, xs_g, bg, cg, ht_g, cst, causal, dx_ref[:, sl])
            out, _ = _gated_norm_fwd(y, z_ref[:, sl], nw_ref[:, sl])
            y_ref[:, sl] = out.astype(BF16)
            ht[:, sl] = jnp.exp(f["totx"]) * ht_g + _dot_tn(bg, f["xdt"] * f["dsx"])

    par = lambda w: pl.BlockSpec((1, w), lambda c: (0, 0))
    (ycat, hprev), jouts = _hosted(
        body, jobs, grid=(nc,),
        in_specs=[pl.BlockSpec((CHUNK, XBC), lambda c: (c, 0)),
                  pl.BlockSpec((CHUNK, LANE), lambda c: (c, COL_DT // LANE)),
                  pl.BlockSpec((CHUNK, SSD_W), lambda c: (c, COL_Z // SSD_W)),
                  ANY_SPEC, par(LANE), par(LANE), par(SSD_W), par(SSD_W)],
        out_specs=(pl.BlockSpec((CHUNK, SSD_W), lambda c: (c, LRU_W // SSD_W)),
                   pl.BlockSpec((None, N_STATE, SSD_W), lambda c: (c, 0, 0))),
        out_shape=(jax.ShapeDtypeStruct(ymix.shape, ymix.dtype), jax.ShapeDtypeStruct((nc, N_STATE, SSD_W), F32)),
        scratch_shapes=[pltpu.VMEM((N_STATE, SSD_W), F32), pltpu.VMEM((CHUNK, LANE), F32)],
        aliases={3: 0}, name=name, args=(xact, proj, proj, ymix, bias_pad, alog_pad, dxp, normw))
    return ((ycat, hprev), jouts) if jobs else (ycat, hprev)


def _ssd_bwd(xact, proj, dycat, hprev, bias_pad, alog_pad, dxp, normw, *, name, jobs=()):
    s = xact.shape[0]
    nc = s // CHUNK
    l = CHUNK

    def body(xa_ref, dt_ref, z_ref, dy_ref, hp_ref, bias_ref, alp_ref, dx_ref, nw_ref,
             dxa_ref, ddt_ref, dz_ref, dnw_ref, small_ref, dht, cst, accx, dcsx_s, ddtx_s):
        step = pl.program_id(0)

        @pl.when(step == 0)
        def _():
            dht[...] = jnp.zeros_like(dht)
            accx[...] = jnp.zeros_like(accx)
            dnw_ref[...] = jnp.zeros_like(dnw_ref)
            small_ref[...] = jnp.zeros_like(small_ref)

        dtr = dt_ref[...]
        q = _ssd_prep(dtr, bias_ref[...], alp_ref[...])
        cst[...] = q["cs"].T
        causal = q["tril"] > 0.0
        lane = _iota((l, LANE), 1)
        head_row = _iota((LANE, l), 0)
        dcs_head = jnp.zeros((l, LANE), F32)
        dcs_rows = jnp.zeros((LANE, l), F32)
        for g in range(N_GROUP):
            sl = slice(g * GROUP_W, (g + 1) * GROUP_W)
            slb = slice(SSD_W + g * N_STATE, SSD_W + (g + 1) * N_STATE)
            slc = slice(SSD_W + N_GROUP * N_STATE + g * N_STATE, SSD_W + N_GROUP * N_STATE + (g + 1) * N_STATE)
            xs_g, bg, cg = xa_ref[:, sl], xa_ref[:, slb], xa_ref[:, slc]
            ht_g = hp_ref[:, sl]
            dxp_g = dx_ref[:, sl]
            y, f = _ssd_group_fwd(q, g, xs_g, bg, cg, ht_g, cst, causal, dxp_g)
            z_g, nw_g = z_ref[:, sl], nw_ref[:, sl]
            _o, (sz, silu, rs, yn) = _gated_norm_fwd(y, z_g, nw_g)
            dout = dy_ref[:, sl]
            dnw_ref[:, sl] += jnp.sum(dout * yn, axis=0, keepdims=True)
            dyn = dout * nw_g
            dyf = rs * (dyn - yn * jnp.mean(dyn * yn, axis=1, keepdims=True))
            dy = dyf * silu
            dz_ref[:, sl] = (dyf * y * sz * (1.0 + z_g * (1.0 - sz))).astype(BF16)
            accx[0:1, sl] += jnp.sum(dy * xs_g, axis=0, keepdims=True)
            dyo = dy * f["ex"]
            dcg = _dot_nt(dyo, ht_g)
            dht_prev = _dot_tn(cg, dyo)
            dcsx = dy * f["yoff"]
            xdt = f["xdt"]
            dxdt = jnp.zeros((l, GROUP_W), F32)
            dcb = jnp.zeros((l, l), F32)
            for j in range(4):
                h = 4 * g + j
                lm = f["lms"][j]
                sc = f["cb"] * lm
                mask = _head_mask(j)
                ds_ = jnp.where(causal, _dot_nt(jnp.where(mask, dy, 0.0), xdt), 0.0)
                dxdt = jnp.where(mask, _dot_tn(sc, dy), dxdt)
                dcb = dcb + ds_ * lm
                m = ds_ * sc
                dcs_head = dcs_head + jnp.where(lane == h, jnp.sum(m, axis=1, keepdims=True), 0.0)
                dcs_rows = dcs_rows + jnp.where(head_row == h, jnp.sum(m, axis=0, keepdims=True), 0.0)
            dhn = dht[:, sl]
            etot = jnp.exp(f["totx"])
            dxd = _dot(bg, dhn)
            dbg = _dot_nt(xdt * f["dsx"], dhn)
            dxdt = dxdt + dxd * f["dsx"]
            qq = dxd * xdt * f["dsx"]
            dcsx = dcsx - qq
            dtot = jnp.sum(qq, axis=0, keepdims=True) + jnp.sum(dhn * ht_g, axis=0, keepdims=True) * etot
            dht[:, sl] = etot * dhn + dht_prev
            dcg = dcg + _dot(dcb, bg)
            dbg = dbg + _dot_tn(dcb, cg)
            dxa_ref[:, sl] = dxdt * f["dtx"] + dy * dxp_g
            dxa_ref[:, slb] = dbg
            dxa_ref[:, slc] = dcg
            dcsx_s[:, sl] = dcsx
            ddtx_s[:, sl] = dxdt * xs_g
            accx[2:3, sl] = dtot
        reduce = (jnp.right_shift(_iota((SSD_W, LANE), 0), 6) == _iota((SSD_W, LANE), 1)).astype(F32)
        triu = (_iota((l, l), 1) >= _iota((l, l), 0)).astype(F32)
        dtot = _dot01_r(accx[...], reduce)[2:3, :]
        dcs_head = dcs_head - dcs_rows.T
        da_head = _dot01(triu, dcs_head + _dot01_r(dcsx_s[...], reduce, parts=2)) + dtot
        ddt = _dot01_r(ddtx_s[...], reduce, parts=2) + da_head * q["a_head"]
        small_ref[1:2, :] += jnp.sum(da_head * q["dt"], axis=0, keepdims=True)
        ddtr = ddt * _sigmoid(dtr + bias_ref[...])
        ddt_ref[...] = ddtr.astype(BF16)
        small_ref[0:1, :] += jnp.sum(ddtr, axis=0, keepdims=True)

        @pl.when(step == nc - 1)
        def _():
            small_ref[1:2, :] = small_ref[1:2, :] * q["a_head"]
            small_ref[2:3, :] = _dot01_r(accx[...], reduce)[0:1, :]

    rev = lambda c: nc - 1 - c
    par = lambda w: pl.BlockSpec((1, w), lambda c: (0, 0))
    outs, jouts = _hosted(
        body, jobs, grid=(nc,),
        in_specs=[pl.BlockSpec((CHUNK, XBC), lambda c: (rev(c), 0)),
                  pl.BlockSpec((CHUNK, LANE), lambda c: (rev(c), COL_DT // LANE)),
                  pl.BlockSpec((CHUNK, SSD_W), lambda c: (rev(c), COL_Z // SSD_W)),
                  pl.BlockSpec((CHUNK, SSD_W), lambda c: (rev(c), 1)),
                  pl.BlockSpec((None, N_STATE, SSD_W), lambda c: (rev(c), 0, 0)),
                  par(LANE), par(LANE), par(SSD_W), par(SSD_W)],
        out_specs=(pl.BlockSpec((CHUNK, XBC), lambda c: (rev(c), 0)),
                   pl.BlockSpec((CHUNK, LANE), lambda c: (rev(c), 0)),
                   pl.BlockSpec((CHUNK, SSD_W), lambda c: (rev(c), 0)),
                   par(SSD_W), pl.BlockSpec((SUBLANE, LANE), lambda c: (0, 0))),
        out_shape=(jax.ShapeDtypeStruct((s, XBC), F32), jax.ShapeDtypeStruct((s, LANE), BF16),
                   jax.ShapeDtypeStruct((s, SSD_W), BF16), jax.ShapeDtypeStruct((1, SSD_W), F32),
                   jax.ShapeDtypeStruct((SUBLANE, LANE), F32)),
        scratch_shapes=[pltpu.VMEM((N_STATE, SSD_W), F32), pltpu.VMEM((CHUNK, LANE), F32),
                        pltpu.VMEM((SUBLANE, SSD_W), F32), pltpu.VMEM((CHUNK, SSD_W), F32),
                        pltpu.VMEM((CHUNK, SSD_W), F32)],
        name=name, args=(xact, proj, proj, dycat, hprev, bias_pad, alog_pad, dxp, normw))
    return (tuple(outs), jouts) if jobs else tuple(outs)


def _blockdiag(w):
    w2 = w.reshape(N_HEAD // 2, 2, HEAD_P, HEAD_P)
    z = jnp.zeros((N_HEAD // 2, HEAD_P, HEAD_P), w.dtype)
    top = jnp.concatenate([w2[:, 0], z], axis=2)
    bot = jnp.concatenate([z, w2[:, 1]], axis=2)
    return jnp.concatenate([top, bot], axis=1)


def _unblockdiag(wbd):
    a = wbd[:, :HEAD_P, :HEAD_P]
    b = wbd[:, HEAD_P:, HEAD_P:]
    return jnp.stack([a, b], axis=1).reshape(N_HEAD, HEAD_P, HEAD_P)


def _pad_rows8(w):
    return jnp.concatenate([w, jnp.zeros((SUBLANE - w.shape[0], w.shape[1]), w.dtype)], axis=0)


def _pad_lane(v):
    return jnp.concatenate([v, jnp.zeros((1, LANE - v.shape[1]), v.dtype)], axis=1)


class _NoExchange:
    def ride(self, host):
        return []

    def done(self, jobs, outs, w):
        pass

    def grad(self, name, val):
        pass

    def small(self, raw):
        pass

    def pairs_now(self):
        pass


def _local_step(x, p, tgt, w, hooks=_NoExchange()):
    cw_l = _pad_rows8(w["lru_conv_w"])
    cw_s = _pad_rows8(w["ssd_conv_w"])
    wa_bd = _blockdiag(w["lru_gate_a_w"])
    wx_bd = _blockdiag(w["lru_gate_x_w"])
    ba = w["lru_gate_a_b"].reshape(1, LRU_W)
    bx = w["lru_gate_x_b"].reshape(1, LRU_W)
    bias_pad = _pad_lane(w["ssd_dt_bias"])
    alog_pad = _pad_lane(w["ssd_a_log"])
    dxp = jnp.repeat(w["ssd_d"], HEAD_P, axis=1)

    def host(fn, *a, name, **k):
        jobs = hooks.ride(name)
        res = fn(*a, name=name, jobs=jobs, **k)
        if jobs:
            res, jouts = res
            hooks.done(jobs, jouts, w)
        return res

    def grad(n, val):
        g[n] = val
        hooks.grad(n, val)

    xb = x.astype(BF16)
    proj = host(_mm, xb, w["w_in_t"], "nt", tm=2048, tn=512, name="in_proj")
    ymix, h_lru = host(_lru_fwd, proj, cw_l, w["lru_conv_b"], wa_bd, ba, wx_bd, bx, w["lru_a_param"], name="lru_fwd")
    xact = host(_conv_silu_fwd, proj, cw_s, w["ssd_conv_b"], col0=COL_XBC, width=XBC, ct=256, name="ssd_conv_fwd")
    ycat, hprev = host(_ssd_fwd, xact, proj, ymix, bias_pad, alog_pad, dxp, w["ssd_norm_w"], name="ssd_fwd")
    mix, x1, x1b = _mm_ln(ycat, w["w_out"], x, w["ln1_g"], w["ln1_b"], tm=512, name="out_proj")
    pre = _mm(x1b, w["w_ff1"], "nn", tm=2048, tn=512, out_dtype=BF16, name="ff1")
    ff, x2, x2b = _mm_ln(pre, w["w_ff2"], x1, w["ln2_g"], w["ln2_b"], tm=512, a_fn=_relu2, name="ff2")
    loss, dgpre, dple, dt3, dg3, db3 = _head(x2, x2b, p, w["w_ple_gate"], w["w_ple"], w["ln3_g"], w["ln3_b"], tgt,
                                             name="head")

    g = {}
    g["ln3_g"], g["ln3_b"] = dg3, db3
    grad("w_ple_gate", _mm(x2b, dgpre, "tn", tm=512, tn=1024, out_dtype=BF16, name="d_w_ple_gate"))
    grad("w_ple", _mm(p, dple, "tn", tm=256, tn=512, dest_major=True, out_dtype=BF16, name="d_w_ple"))
    dt2, dt2b, g["ln2_g"], g["ln2_b"] = host(_mm_ln_bwd, dgpre, w["w_ple_gate"], x1, ff, w["ln2_g"], dt3, ALPHA,
                                             tm=512, name="d_x2")
    grad("w_ff2", host(_mm, pre, dt2b, "tn", tm=512, tn=1024, a_fn=_relu2, out_dtype=BF16, name="d_w_ff2"))
    dpre = host(_mm, dt2b, w["w_ff2"], "nt", tm=2048, tn=512, extra=pre, out_dtype=BF16,
                epi=lambda acc, pv: acc * 2.0 * jnp.maximum(pv.astype(F32), 0.0), name="d_pre")
    grad("w_ff1", host(_mm, x1b, dpre, "tn", tm=1024, tn=512, dest_major=True, out_dtype=BF16, name="d_w_ff1"))
    dt1, dt1b, g["ln1_g"], g["ln1_b"] = host(_mm_ln_bwd, dpre, w["w_ff1"], x, mix, w["ln1_g"], dt2, ALPHA,
                                             tm=256, name="d_x1")
    grad("w_out", host(_mm, ycat, dt1b, "tn", tm=512, tn=1024, out_dtype=BF16, name="d_w_out"))
    dycat = host(_mm, dt1b, w["w_out"], "nt", tm=2048, tn=512, name="d_ycat")
    dxl, dgl, dcwb_l, dwa, dwx = host(_lru_bwd, proj, dycat, h_lru, cw_l, w["lru_conv_b"], wa_bd, ba, wx_bd, bx,
                                      w["lru_a_param"], name="lru_bwd")
    g["lru_gate_a_w"] = _unblockdiag(dwa)
    g["lru_gate_x_w"] = _unblockdiag(dwx)
    raw = dict(lru=dcwb_l, gate_a=g["lru_gate_a_w"].reshape(N_HEAD * HEAD_P, HEAD_P).astype(BF16),
               gate_x=g["lru_gate_x_w"].reshape(N_HEAD * HEAD_P, HEAD_P).astype(BF16))
    hooks.small(raw)
    dxact, ddt, dz, g["ssd_norm_w"], small = host(_ssd_bwd, xact, proj, dycat, hprev, bias_pad, alog_pad, dxp,
                                                   w["ssd_norm_w"], name="ssd_bwd")
    dxbc, dcwb_s = host(_conv_silu_bwd, proj, dxact, cw_s, w["ssd_conv_b"], col0=COL_XBC, width=XBC, ct=256,
                        name="ssd_conv_bwd")
    pieces, offsets = [dxl, dgl, dz, dxbc, ddt], [0, COL_G, COL_Z, COL_XBC, COL_DT]

    g["lru_conv_w"] = dcwb_l[0:4]
    g["lru_conv_b"] = dcwb_l[4:5]
    g["lru_gate_a_b"] = dcwb_l[5:6]
    g["lru_gate_x_b"] = dcwb_l[6:7]
    g["lru_a_param"] = dcwb_l[7:8]
    g["ssd_conv_w"] = dcwb_s[0:4]
    g["ssd_conv_b"] = dcwb_s[4:5]
    g["ssd_dt_bias"] = small[0:1, :N_HEAD]
    g["ssd_a_log"] = small[1:2, :N_HEAD]
    g["ssd_d"] = small[2:3, :N_HEAD]
    rows = jnp.concatenate([g[n] for n in ("ssd_norm_w", "ln1_g", "ln1_b", "ln2_g", "ln2_b", "ln3_g", "ln3_b")]
                           + [jnp.broadcast_to(loss[:, 0:1], (1, D_MODEL))], axis=0)
    late = dict(ssd=dcwb_s, heads=small, rows=rows)
    hooks.small(late)
    raw.update(late)
    dwt = None
    for q, (pc, off) in enumerate(zip(pieces, offsets)):
        dwt = host(_mm, pc, xb, "tn", tm=512, tn=1024, out_dtype=BF16, into=(dwt, off, D_IN),
                   name="d_w_in_%d" % q)
    grad("w_in", dwt)
    hooks.pairs_now()
    grad_x = host(_mm_pieces, pieces, offsets, w["w_in_t"], tm=256, extra=dt1, epi=lambda acc, e: acc + ALPHA * e,
                  name="d_x")
    return loss[0, 0], grad_x, g, raw


ANY_SPEC = pl.BlockSpec(memory_space=pl.ANY)


def _mesh_pos():
    return lax.axis_index("x"), lax.axis_index("y"), lax.axis_index("c")


def _remote(src, dst, send, recv, k, to):
    return pltpu.make_async_remote_copy(src_ref=src, dst_ref=dst, send_sem=send.at[k], recv_sem=recv.at[k],
                                        device_id=to, device_id_type=MESH_T)


class _Job:
    N_SEM = 9

    def __init__(self, kind, inp):
        self.kind, self.inp = kind, inp
        shape = {"gather": (N_DEV,) + inp.shape, "relay": (N_DEV,) + inp.shape, "pair": (4,) + inp.shape[1:],
                 "chip": inp.shape}[kind]
        self.out = jax.ShapeDtypeStruct(shape, inp.dtype)
        self.top = (inp.shape[0] // 2) // 16 * 16

    def _blk(self, ref, k):
        return ref.at[k]

    def _relay_copies(self, inp, out, send, recv):
        x, y, c = _mesh_pos()
        sib, xn, yn, dg = (x, y, 1 - c), (1 - x, y, c), (x, 1 - y, c), (1 - x, 1 - y, c)
        blk = lambda p, cc=None: out.at[4 * p[0] + 2 * p[1] + (p[2] if cc is None else cc)]
        top = lambda r: r.at[pl.ds(0, self.top)]
        bot = lambda r: r.at[pl.ds(self.top, self.inp.shape[0] - self.top)]
        mine = blk((x, y, c))
        plan = [
            (inp, mine, sib, blk(sib)),
            (inp, mine, xn, blk(xn)),
            (inp, mine, yn, blk(yn)),
            (top(blk(xn)), top(blk(xn)), yn, top(blk(dg))),
            (bot(blk(yn)), bot(blk(yn)), xn, bot(blk(dg))),
            (blk(xn), blk(xn), sib, blk(xn, 1 - c)),
            (blk(yn), blk(yn), sib, blk(yn, 1 - c)),
            (top(blk(dg)), top(blk(dg)), sib, top(blk(dg, 1 - c))),
            (bot(blk(dg)), bot(blk(dg)), sib, bot(blk(dg, 1 - c))),
        ]
        me = (x, y, c)
        return [(_remote(s, d, send, recv, k, to), _remote(s, land, send, recv, k, me))
                for k, (s, d, to, land) in enumerate(plan)]

    def _places(self):
        x, y, c = _mesh_pos()
        return (x, y, c), (x, y, 1 - c), [(1 - x, y), (x, 1 - y), (1 - x, 1 - y)]

    def start(self, inp, out, send, recv, loc):
        me, sibling, chips = self._places()
        x, y, c = me
        if self.kind == "relay":
            pltpu.make_async_copy(inp, out.at[4 * x + 2 * y + c], loc.at[0]).start()
            cps = self._relay_copies(inp, out, send, recv)
            for k in (0, 1, 2):
                cps[k][0].start()
        elif self.kind == "gather":
            mine = out.at[4 * x + 2 * y + c]
            pltpu.make_async_copy(inp, mine, loc.at[0]).start()
            _remote(inp, mine, send, recv, 0, sibling).start()
            for j, chip in enumerate(chips):
                _remote(inp, mine, send, recv, 1 + j, (*chip, c)).start()
        elif self.kind == "pair":
            for k in range(4):
                _remote(inp.at[2 * k + (1 - c)], out.at[k], send, recv, k, sibling).start()
        else:
            kme = 2 * x + y
            pltpu.make_async_copy(self._blk(inp, kme), self._blk(out, kme), loc.at[0]).start()
            for j, (tx, ty) in enumerate(chips):
                _remote(self._blk(inp, 2 * tx + ty), self._blk(out, kme), send, recv, j, (tx, ty, c)).start()

    def mid(self, inp, out, send, recv, loc):
        if self.kind == "relay":
            cps = self._relay_copies(inp, out, send, recv)
            for k, onward in ((1, (3, 5)), (2, (4, 6))):
                cps[k][1].wait_recv()
                for q in onward:
                    cps[q][0].start()
            return
        if self.kind != "gather":
            return
        me, sibling, chips = self._places()
        c = me[2]
        for j, chip in enumerate(chips):
            landed = out.at[4 * chip[0] + 2 * chip[1] + c]
            _remote(landed, landed, send, recv, 1 + j, me).wait_recv()
            _remote(landed, landed, send, recv, 4 + j, sibling).start()

    def finish(self, inp, out, send, recv, loc):
        me, sibling, chips = self._places()
        x, y, c = me
        if self.kind == "relay":
            cps = self._relay_copies(inp, out, send, recv)
            for k, onward in ((3, 7), (4, 8)):
                cps[k][1].wait_recv()
                cps[onward][0].start()
            for k in (0, 5, 6, 7, 8):
                cps[k][1].wait_recv()
            for k in range(9):
                cps[k][0].wait_send()
            pltpu.make_async_copy(inp, out.at[4 * x + 2 * y + c], loc.at[0]).wait()
        elif self.kind == "gather":
            blk = lambda px, py, pc: out.at[4 * px + 2 * py + pc]
            mine = blk(*me)
            _remote(inp, blk(*sibling), send, recv, 0, me).wait_recv()
            for j, chip in enumerate(chips):
                _remote(inp, blk(*chip, 1 - c), send, recv, 4 + j, me).wait_recv()
            for k in range(7):
                _remote(inp, mine, send, recv, k, sibling).wait_send()
            pltpu.make_async_copy(inp, mine, loc.at[0]).wait()
        elif self.kind == "pair":
            for k in range(4):
                _remote(inp.at[2 * k + (1 - c)], out.at[k], send, recv, k, sibling).wait()
        else:
            kme = 2 * x + y
            for j, (tx, ty) in enumerate(chips):
                _remote(self._blk(inp, kme), self._blk(out, 2 * tx + ty), send, recv, j, (tx, ty, c)).wait_recv()
            for j, (tx, ty) in enumerate(chips):
                _remote(self._blk(inp, 2 * tx + ty), self._blk(out, kme), send, recv, j, (tx, ty, c)).wait_send()
            pltpu.make_async_copy(self._blk(inp, kme), self._blk(out, kme), loc.at[0]).wait()


def _job_scratch(jobs):
    sem = pltpu.SemaphoreType.DMA
    return [s for _ in jobs for s in (sem((_Job.N_SEM,)), sem((_Job.N_SEM,)), sem((1,)))]


def _run_jobs(jobs, method, jins, jouts, jsems, only=None):
    for q, job in enumerate(jobs):
        if only is None or only[q]:
            getattr(job, method)(jins[q], jouts[q], *jsems[3 * q:3 * q + 3])


def _exchange(jobs, *, name):
    n = len(jobs)

    def body(*refs):
        jins, jouts, jsems = refs[:n], refs[n:2 * n], refs[2 * n:]
        _run_jobs(jobs, "start", jins, jouts, jsems)
        _run_jobs(jobs, "mid", jins, jouts, jsems)
        _run_jobs(jobs, "finish", jins, jouts, jsems)

    return _pcall(body, in_specs=[ANY_SPEC] * n, out_specs=[ANY_SPEC] * n, out_shape=[j.out for j in jobs],
                  scratch_shapes=_job_scratch(jobs), name=name)(*[j.inp for j in jobs])


def _hosted(body, jobs, *, grid, in_specs, out_specs, out_shape, args, name, scratch_shapes=(), aliases=None):
    in_specs, out_specs, out_shape = list(in_specs), list(out_specs), list(out_shape)
    scratch_shapes = list(scratch_shapes)
    n_in, n_out, n_scr, nj = len(in_specs), len(out_specs), len(scratch_shapes), len(jobs)
    sem = ("arbitrary",) * len(grid)
    kw = dict(input_output_aliases=aliases) if aliases else {}
    if not jobs:
        res = _pcall(body, grid=grid, in_specs=in_specs, out_specs=out_specs, out_shape=out_shape,
                     scratch_shapes=scratch_shapes, name=name, compiler_params=_cparams(sem), **kw)(*args)
        return list(res), []

    def full(*refs):
        ins, jins = refs[:n_in], refs[n_in:n_in + nj]
        o0 = n_in + nj
        outs, jouts = refs[o0:o0 + n_out], refs[o0 + n_out:o0 + n_out + nj]
        s0 = o0 + n_out + nj
        scr, jsems = refs[s0:s0 + n_scr], refs[s0 + n_scr:]
        step = pl.program_id(0)
        for ax in range(1, len(grid)):
            step = step * grid[ax] + pl.program_id(ax)
        total = math.prod(grid)
        early = [job.kind == "relay" for job in jobs]
        mid_step = (3 * total) // 5
        split = any(early) and 0 < mid_step < total - 1

        @pl.when(step == 0)
        def _():
            _run_jobs(jobs, "start", jins, jouts, jsems)

        if split:
            @pl.when(step == mid_step)
            def _():
                _run_jobs(jobs, "mid", jins, jouts, jsems, only=early)

        body(*ins, *outs, *scr)

        @pl.when(step == total - 1)
        def _():
            _run_jobs(jobs, "mid", jins, jouts, jsems, only=[not e for e in early] if split else None)
            _run_jobs(jobs, "finish", jins, jouts, jsems)

    res = _pcall(full, grid=grid, in_specs=in_specs + [ANY_SPEC] * nj, out_specs=out_specs + [ANY_SPEC] * nj,
                 out_shape=out_shape + [j.out for j in jobs], scratch_shapes=scratch_shapes + _job_scratch(jobs),
                 name=name, compiler_params=_cparams(sem), **kw)(*args, *[j.inp for j in jobs])
    return list(res[:n_out]), list(res[n_out:])


def _pair_add(g8, r4, cidx, *, name):
    _, r, c = g8.shape
    tr = ROW_TILE if r % ROW_TILE == 0 else r

    def body(c_ref, g_ref, r_ref, o_ref):
        o_ref[...] = (g_ref[...].astype(F32) + r_ref[...].astype(F32)).astype(BF16)

    return _pcall(
        body,
        grid_spec=pltpu.PrefetchScalarGridSpec(
            num_scalar_prefetch=1, grid=(4, r // tr),
            in_specs=[pl.BlockSpec((None, tr, c), lambda k, i, cr: (2 * k + cr[0], i, 0)),
                      pl.BlockSpec((None, tr, c), lambda k, i, cr: (k, i, 0))],
            out_specs=pl.BlockSpec((None, tr, c), lambda k, i, cr: (k, i, 0))),
        out_shape=jax.ShapeDtypeStruct((4, r, c), BF16), name=name,
        compiler_params=_cparams(("parallel", "parallel")))(cidx, g8, r4)


def _adam_update(g, w_ref, m_ref, v_ref, g_ref, d_ref, mo_ref, vo_ref):
    c1 = 1.0 - ADAM_B1 ** ADAM_STEP
    c2 = 1.0 - ADAM_B2 ** ADAM_STEP
    m2 = ADAM_B1 * m_ref[...] + (1.0 - ADAM_B1) * g
    v2 = ADAM_B2 * v_ref[...] + (1.0 - ADAM_B2) * (g * g)
    g_ref[...] = g
    mo_ref[...] = m2
    vo_ref[...] = v2
    d_ref[...] = -ADAM_LR * ((m2 / c1) / (jnp.sqrt(v2 / c2) + ADAM_EPS) + ADAM_WD * w_ref[...])


def _adamw_rows(srcs, items, own_cols, me1, *, name):
    ns, ni, no = len(srcs), len(items), len(own_cols)
    full = lambda a: pl.BlockSpec(a.shape, lambda i, me: (0,) * a.ndim)
    in_specs = [full(a) for a in srcs]
    args = list(srcs)
    for (si, _r0, w, _m, _v) in own_cols:
        a = srcs[si]
        in_specs.append(pl.BlockSpec((N_DEV, a.shape[1], w.shape[1]), lambda i, me: (0, 0, me[0])))
        args.append(a)
    out_specs, out_shape = [], []
    for (_si, _r0, w, m, v) in list(items) + list(own_cols):
        in_specs += [full(w)] * 3
        args += [w, m, v]
        out_specs += [full(w)] * 4
        out_shape += [jax.ShapeDtypeStruct(w.shape, F32)] * 4

    def body(me_ref, *refs):
        src_refs, own_refs = refs[:ns], refs[ns:ns + no]
        wmv = refs[ns + no:ns + no + 3 * (ni + no)]
        outs = refs[ns + no + 3 * (ni + no):]
        for q, (si, r0, w, _m, _v) in enumerate(list(items) + list(own_cols)):
            nr, cw = w.shape
            gref = src_refs[si] if q < ni else own_refs[q - ni]
            g = gref[0, r0:r0 + nr, 0:cw]
            for d in range(1, N_DEV):
                g = g + gref[d, r0:r0 + nr, 0:cw]
            _adam_update(g, *wmv[3 * q:3 * q + 3], *outs[4 * q:4 * q + 4])

    res = _pcall(
        body,
        grid_spec=pltpu.PrefetchScalarGridSpec(num_scalar_prefetch=1, grid=(1,), in_specs=in_specs, out_specs=out_specs),
        out_shape=out_shape, name=name, compiler_params=_cparams(("arbitrary",)))(me1, *args)
    return [tuple(res[4 * q:4 * q + 4]) for q in range(ni + no)]


def _adamw(gsrc, w, m, v, *, name):
    k, r, c = gsrc.shape
    tr = ROW_TILE if r % ROW_TILE == 0 else r

    def body(gs_ref, w_ref, m_ref, v_ref, g_ref, d_ref, mo_ref, vo_ref):
        g = gs_ref[0].astype(F32)
        for q in range(1, k):
            g = g + gs_ref[q].astype(F32)
        _adam_update(g, w_ref, m_ref, v_ref, g_ref, d_ref, mo_ref, vo_ref)

    tc = c
    if tr == r and r > ROW_TILE and c % 256 == 0:
        tc = 256
    blk = pl.BlockSpec((tr, tc), lambda i, j: (i, j))
    sd = jax.ShapeDtypeStruct((r, c), F32)
    return _pcall(body, grid=(r // tr, c // tc),
                  in_specs=[pl.BlockSpec((k, tr, tc), lambda i, j: (0, i, j)), blk, blk, blk],
                  out_specs=(blk, blk, blk, blk), out_shape=(sd, sd, sd, sd), name=name,
                  compiler_params=_cparams(("parallel", "parallel")))(gsrc, w, m, v)


WEIGHTS = ['w_in', 'lru_conv_w', 'lru_conv_b', 'lru_gate_a_w', 'lru_gate_a_b', 'lru_gate_x_w', 'lru_gate_x_b',
           'lru_a_param', 'ssd_conv_w', 'ssd_conv_b', 'ssd_dt_bias', 'ssd_a_log', 'ssd_d', 'ssd_norm_w', 'w_out',
           'ln1_g', 'ln1_b', 'w_ff1', 'w_ff2', 'ln2_g', 'ln2_b', 'w_ple_gate', 'w_ple', 'ln3_g', 'ln3_b']
BIG = ['w_in', 'w_out', 'w_ff1', 'w_ff2', 'w_ple_gate', 'w_ple']
COL_SHARDED = ('w_ff1', 'w_ple')
CONV = ['lru_conv_w', 'ssd_conv_w']
REPL = [n for n in WEIGHTS if n not in BIG and n not in CONV]
CONV_CH = {'lru_conv_w': LRU_W, 'ssd_conv_w': XBC}


def _to_dest_major(name, gfull):
    if name in COL_SHARDED:
        r, cfull = gfull.shape
        return gfull.reshape(r, N_DEV, cfull // N_DEV).transpose(1, 0, 2)
    rfull, cdim = gfull.shape
    return gfull.reshape(N_DEV, rfull // N_DEV, cdim)


def _full_weight(name, gathered):
    if name in COL_SHARDED:
        _, r, cs = gathered.shape
        full = gathered.transpose(1, 0, 2).reshape(r, N_DEV * cs)
    else:
        _, rs, cdim = gathered.shape
        full = gathered.reshape(N_DEV * rs, cdim)
    if name == 'w_in':
        full = lax.dynamic_update_slice(jnp.zeros((D_IN_PAD, D_MODEL), full.dtype), full, (0, 0))
    return full


SMALL_SRC = ("lru", "ssd", "heads", "rows", "gate_a", "gate_x")
AG_HOSTS = {"in_proj": ("w_ff1",), "lru_fwd": ("w_ff2",), "ssd_conv_fwd": ("w_ple_gate", "w_ple"), "ssd_fwd": ("w_out",)}
PAIR_HOSTS = ("d_x2", "d_pre", "d_x1", "d_ycat")
CHIP_HOSTS = {"lru_bwd": ("w_ple_gate", "w_ple", "w_ff2"), "ssd_bwd": ("w_ff1",), "ssd_conv_bwd": ("w_out",),
              "d_x": ("w_in",)}
SMALL_HOSTS = {"ssd_bwd": ("lru", "gate_a", "gate_x"), "d_w_in_3": ("ssd", "heads", "rows")}


class _Schedule:
    def __init__(self, shards, cidx):
        self.shards, self.cidx = shards, cidx
        self.pair, self.chip, self.small_jobs = [], [], []
        self.dest, self.summed, self.gathered_small = {}, {}, {}
        self.tags = []

    def ride(self, host):
        tags = []
        if host in AG_HOSTS:
            tags = [("weight", n, self.shards[n]) for n in AG_HOSTS[host]]
        elif host in PAIR_HOSTS or host in CHIP_HOSTS or host == "flush":
            tags = [("pair", n, a) for n, a in self.pair]
            self.pair = []
            if host not in PAIR_HOSTS:
                take = [t for t in self.chip if host == "flush" or t[0] in CHIP_HOSTS[host]]
                tags += [("chip", n, a) for n, a in take]
                self.chip = [t for t in self.chip if not any(t is u for u in take)]
        if host in SMALL_HOSTS:
            tags += [("small", n, a) for n, a in self.small_jobs if n in SMALL_HOSTS[host]]
            self.small_jobs = [t for t in self.small_jobs if t[0] not in SMALL_HOSTS[host]]
        self.tags = tags
        return [_Job({"weight": "relay", "small": "gather"}.get(kind, kind), a) for kind, _n, a in tags]

    def done(self, jobs, outs, w):
        for (kind, n, _a), o in zip(self.tags, outs):
            if kind == "weight":
                w[n] = _full_weight(n, o)
            elif kind == "small":
                self.gathered_small[n] = o
            elif kind == "pair":
                self.chip.append((n, _pair_add(self.dest[n], o, self.cidx, name="rs_pair_add_" + n)))
            else:
                self.summed[n] = o

    def grad(self, name, val):
        self.dest[name] = val if val.ndim == 3 else _to_dest_major(name, val)
        self.pair.append((name, self.dest[name]))

    def small(self, raw):
        self.small_jobs += list(raw.items())

    def pairs_now(self):
        tags = [("pair", n, a) for n, a in self.pair]
        self.pair, self.tags = [], tags
        jobs = [_Job("pair", a) for _k, _n, a in tags]
        self.done(jobs, _exchange(jobs, name="rs_pairs_now"), None)

    def flush(self):
        step = 0
        while self.pair or self.chip:
            jobs = self.ride("flush")
            self.done(jobs, _exchange(jobs, name="rs_flush_%d" % step), None)
            step += 1


def kernel(x, p, w_in, lru_conv_w, lru_conv_b, lru_gate_a_w, lru_gate_a_b, lru_gate_x_w, lru_gate_x_b, lru_a_param, ssd_conv_w, ssd_conv_b, ssd_dt_bias, ssd_a_log, ssd_d, ssd_norm_w, w_out, ln1_g, ln1_b, w_ff1, w_ff2, ln2_g, ln2_b, w_ple_gate, w_ple, ln3_g, ln3_b, loss_target, m_w_in, m_lru_conv_w, m_lru_conv_b, m_lru_gate_a_w, m_lru_gate_a_b, m_lru_gate_x_w, m_lru_gate_x_b, m_lru_a_param, m_ssd_conv_w, m_ssd_conv_b, m_ssd_dt_bias, m_ssd_a_log, m_ssd_d, m_ssd_norm_w, m_w_out, m_ln1_g, m_ln1_b, m_w_ff1, m_w_ff2, m_ln2_g, m_ln2_b, m_w_ple_gate, m_w_ple, m_ln3_g, m_ln3_b, v_w_in, v_lru_conv_w, v_lru_conv_b, v_lru_gate_a_w, v_lru_gate_a_b, v_lru_gate_x_w, v_lru_gate_x_b, v_lru_a_param, v_ssd_conv_w, v_ssd_conv_b, v_ssd_dt_bias, v_ssd_a_log, v_ssd_d, v_ssd_norm_w, v_w_out, v_ln1_g, v_ln1_b, v_w_ff1, v_w_ff2, v_ln2_g, v_ln2_b, v_w_ple_gate, v_w_ple, v_ln3_g, v_ln3_b):
    given = dict(locals())
    def local(a, n):
        return jnp.swapaxes(a[0], 0, 1) if n == 'w_in' else a[0]

    wsh = {n: local(given[n], n) for n in WEIGHTS}
    msh = {n: local(given["m_" + n], n) for n in WEIGHTS}
    vsh = {n: local(given["v_" + n], n) for n in WEIGHTS}
    xi, yi, ci = _mesh_pos()
    me = 4 * xi + 2 * yi + ci

    shards = {n: wsh[n].astype(BF16) for n in BIG}
    conv_pack = jnp.concatenate([_pad_rows8(wsh[n]) for n in CONV], axis=1)
    g_in, gconv = _exchange([_Job("relay", shards['w_in']), _Job("gather", conv_pack)], name="ag_first")
    full = {'w_in_t': _full_weight('w_in', g_in)}
    c0 = 0
    for n in CONV:
        cw = CONV_CH[n] // N_DEV
        full[n] = gconv[:, :4, c0:c0 + cw].transpose(1, 0, 2).reshape(4, CONV_CH[n])
        c0 += cw
    for n in REPL:
        full[n] = given[n] if given[n].ndim == 2 else wsh[n]

    sched = _Schedule(shards, jnp.reshape(ci, (1,)).astype(jnp.int32))
    loss_local, grad_x, g, raw = _local_step(x[0], p[0, 0], loss_target[0], full, sched)
    sched.flush()
    summed, gat = sched.summed, sched.gathered_small
    loss = gat["rows"][0, 7, 0]
    for d in range(1, N_DEV):
        loss = loss + gat["rows"][d, 7, 0]

    outs = {}
    for n in BIG:
        outs[n] = _adamw(summed[n], wsh[n], msh[n], vsh[n], name="adamw_" + n)
    for n, k in (("lru_gate_a_w", "gate_a"), ("lru_gate_x_w", "gate_x")):
        flat = lambda a: a.reshape(N_HEAD * HEAD_P, HEAD_P)
        res = _adamw(gat[k], flat(wsh[n]), flat(msh[n]), flat(vsh[n]), name="adamw_" + n)
        outs[n] = tuple(r.reshape(N_HEAD, HEAD_P, HEAD_P) for r in res)
    row_items = [("lru_conv_b", 0, 4), ("lru_gate_a_b", 0, 5), ("lru_gate_x_b", 0, 6), ("lru_a_param", 0, 7),
                 ("ssd_conv_b", 1, 4), ("ssd_dt_bias", 2, 0), ("ssd_a_log", 2, 1), ("ssd_d", 2, 2),
                 ("ssd_norm_w", 3, 0), ("ln1_g", 3, 1), ("ln1_b", 3, 2), ("ln2_g", 3, 3), ("ln2_b", 3, 4),
                 ("ln3_g", 3, 5), ("ln3_b", 3, 6)]
    vec = lambda a: a.reshape(1, -1)
    items = [(si, r0, vec(given[n]), vec(given["m_" + n]), vec(given["v_" + n])) for n, si, r0 in row_items]
    own = [(si, 0, wsh[n], msh[n], vsh[n]) for n, si in (("lru_conv_w", 0), ("ssd_conv_w", 1))]
    me1 = jnp.reshape(me, (1,)).astype(jnp.int32)
    res = _adamw_rows([gat[k] for k in SMALL_SRC[:4]], items, own, me1, name="adamw_small")
    for (n, _si, _r0), r4 in zip(row_items, res[:len(row_items)]):
        outs[n] = r4
    for n, r4 in zip(CONV, res[len(row_items):]):
        outs[n] = r4

    def fin(n, k):
        a = jnp.swapaxes(outs[n][k], 0, 1) if n == 'w_in' else outs[n][k]
        return a.reshape(given[n].shape)

    return (loss, grad_x[None],
            *[fin(n, 0) for n in WEIGHTS], *[fin(n, 1) for n in WEIGHTS],
            *[fin(n, 2) for n in WEIGHTS], *[fin(n, 3) for n in WEIGHTS])
```

```python
import math

import jax
import jax.numpy as jnp
from jax import lax
from jax.experimental import pallas as pl
from jax.experimental.pallas import tpu as pltpu

F32 = jnp.float32
BF16 = jnp.bfloat16
HI = lax.Precision.HIGHEST

N_DEV = 8
D_MODEL = 1024
LRU_W = 1024
SSD_W = 1024
XBC = 2048
N_HEAD = 16
HEAD_P = 64
N_GROUP = 4
GROUP_W = 256
N_STATE = 128
CHUNK = 128
D_FF = 4096
PLE_DIM = 256
D_IN = 5136
D_IN_PAD = 5632
COL_G = 1024
COL_Z = 2048
COL_XBC = 3072
COL_DT = 5120
LRU_C = 8.0
ALPHA = 2.0 ** 0.25
LN_EPS = 1e-5
RMS_EPS = 1e-5
ADAM_LR = 0.001
ADAM_B1 = 0.9
ADAM_B2 = 0.999
ADAM_EPS = 1e-08
ADAM_WD = 0.01
ADAM_STEP = 10
GELU_C = math.sqrt(2.0 / math.pi)
LANE = 128
SUBLANE = 8
VMEM_LIMIT = 48 * 1024 * 1024
MESH_T = pl.DeviceIdType.MESH
NEG_BIG = -1e30


def _pcall(body, **kw):
    return pl.pallas_call(body, **kw)


def _cparams(sem):
    return pltpu.CompilerParams(dimension_semantics=sem, vmem_limit_bytes=VMEM_LIMIT)


def _dot(a, b):
    return jnp.dot(a.astype(BF16), b.astype(BF16), preferred_element_type=F32)


def _dot_nt(a, b):
    return lax.dot_general(a.astype(BF16), b.astype(BF16), (((1,), (1,)), ((), ())), preferred_element_type=F32)


def _dot_tn(a, b):
    return lax.dot_general(a.astype(BF16), b.astype(BF16), (((0,), (0,)), ((), ())), preferred_element_type=F32)


def _dotx(a, b):
    return jnp.dot(a, b, precision=HI, preferred_element_type=F32)


def _sigmoid(x):
    return jax.nn.sigmoid(x)


def _softplus(v):
    return jnp.maximum(v, 0.0) + jnp.log1p(jnp.exp(-jnp.abs(v)))


def _gelu(x):
    th = jnp.tanh(GELU_C * (x + 0.044715 * x * x * x))
    return 0.5 * x * (1.0 + th), th


def _gelu_grad(x, th):
    return 0.5 * (1.0 + th) + 0.5 * x * (1.0 - th * th) * GELU_C * (1.0 + 3.0 * 0.044715 * x * x)


def _iota(shape, dim):
    return lax.broadcasted_iota(jnp.int32, shape, dim)


def _mm(a, b, mode, *, tm, tn, name, a_fn=None, extra=None, epi=None, out_dtype=F32, dest_major=False, into=None,
        jobs=()):
    m = a.shape[1] if mode == "tn" else a.shape[0]
    n = b.shape[0] if mode == "nt" else b.shape[1]
    tm, tn = min(tm, m), min(tn, n)
    if dest_major:
        tn = n // N_DEV
    if mode == "nn":
        m, k = a.shape
        _, n = b.shape
        a_spec = pl.BlockSpec((tm, k), lambda i, j: (i, 0))
        b_spec = pl.BlockSpec((k, tn), lambda i, j: (0, j))
        dims = ((1,), (0,))
    elif mode == "nt":
        m, k = a.shape
        n, _ = b.shape
        a_spec = pl.BlockSpec((tm, k), lambda i, j: (i, 0))
        b_spec = pl.BlockSpec((tn, k), lambda i, j: (j, 0))
        dims = ((1,), (1,))
    else:
        k, m = a.shape
        _, n = b.shape
        a_spec = pl.BlockSpec((k, tm), lambda i, j: (0, i))
        b_spec = pl.BlockSpec((k, tn), lambda i, j: (0, j))
        dims = ((0,), (0,))
    assert m % tm == 0 and n % tn == 0, (name, m, n, tm, tn)
    o_spec = pl.BlockSpec((tm, tn), lambda i, j: (i, j))
    in_specs = [a_spec, b_spec]
    args = [a, b]
    if extra is not None:
        in_specs.append(o_spec)
        args.append(extra)

    def body(*refs):
        a_ref, b_ref, o_ref = refs[0], refs[1], refs[-1]
        av = a_ref[...]
        if a_fn is not None:
            av = a_fn(av)
        acc = lax.dot_general(av.astype(BF16), b_ref[...].astype(BF16), (dims, ((), ())), preferred_element_type=F32)
        if epi is not None:
            acc = epi(acc, refs[2][...])
        o_ref[...] = acc.astype(out_dtype)

    out_shape = jax.ShapeDtypeStruct((m, n), out_dtype)
    aliases = None
    if dest_major:
        assert extra is None
        o_spec = pl.BlockSpec((None, tm, tn), lambda i, j: (j, i, 0))
        out_shape = jax.ShapeDtypeStruct((N_DEV, m, tn), out_dtype)
    if into is not None:
        buf, row0, total = into
        assert extra is None and row0 % tm == 0
        o_spec = pl.BlockSpec((tm, tn), lambda i, j: (row0 // tm + i, j))
        out_shape = jax.ShapeDtypeStruct((total, n), out_dtype)
        if buf is not None:
            in_specs.append(ANY_SPEC)
            args.append(buf)
            aliases = {len(args) - 1: 0}
    (out,), jouts = _hosted(body, jobs, grid=(m // tm, n // tn), in_specs=in_specs, out_specs=[o_spec],
                            out_shape=[out_shape], args=args, name=name, aliases=aliases)
    return (out, jouts) if jobs else out


def _mm_pieces(pieces, offsets, b, *, tm, name, extra, epi, jobs=()):
    m = pieces[0].shape[0]
    kb, n = b.shape
    tm = min(tm, m)
    row = lambda wdt: pl.BlockSpec((tm, wdt), lambda i: (i, 0))
    in_specs = [row(pc.shape[1]) for pc in pieces] + [pl.BlockSpec((kb, n), lambda i: (0, 0)), row(n)]
    np_ = len(pieces)

    def body(*refs):
        b_ref, e_ref, o_ref = refs[np_], refs[np_ + 1], refs[np_ + 2]
        acc = jnp.zeros((tm, n), F32)
        for q in range(np_):
            kq = pieces[q].shape[1]
            acc = acc + jnp.dot(refs[q][...].astype(BF16), b_ref[offsets[q]:offsets[q] + kq, :].astype(BF16),
                                preferred_element_type=F32)
        o_ref[...] = epi(acc, e_ref[...])

    (out,), jouts = _hosted(body, jobs, grid=(m // tm,), in_specs=in_specs, out_specs=[row(n)],
                            out_shape=[jax.ShapeDtypeStruct((m, n), F32)], args=list(pieces) + [b, extra], name=name)
    return (out, jouts) if jobs else out


def _relu2(v):
    r = jnp.maximum(v, 0.0)
    return r * r


ROW_TILE = 256


def _ln_stats(t):
    mu = jnp.mean(t, axis=-1, keepdims=True)
    xc = t - mu
    var = jnp.mean(xc * xc, axis=-1, keepdims=True)
    rstd = lax.rsqrt(var + LN_EPS)
    return xc * rstd, rstd


def _ln_bwd_rows(dy, xhat, rstd, g):
    dxh = dy * g
    m1 = jnp.mean(dxh, axis=-1, keepdims=True)
    m2 = jnp.mean(dxh * xhat, axis=-1, keepdims=True)
    return rstd * (dxh - m1 - xhat * m2)


def _mm_ln(a, b, res, g, beta, *, tm, name, a_fn=None):
    m, k = a.shape
    d = b.shape[1]
    tm = min(tm, m)
    row = pl.BlockSpec((tm, d), lambda i: (i, 0))
    par = pl.BlockSpec((1, d), lambda i: (0, 0))

    def body(a_ref, b_ref, r_ref, g_ref, be_ref, br_ref, y_ref, yb_ref):
        av = a_ref[...]
        if a_fn is not None:
            av = a_fn(av)
        acc = jnp.dot(av.astype(BF16), b_ref[...].astype(BF16), preferred_element_type=F32)
        br_ref[...] = acc
        xhat, _ = _ln_stats(ALPHA * r_ref[...] + acc)
        y = xhat * g_ref[...] + be_ref[...]
        y_ref[...] = y
        yb_ref[...] = y.astype(BF16)

    sd = jax.ShapeDtypeStruct((m, d), F32)
    return _pcall(body, grid=(m // tm,),
                  in_specs=[pl.BlockSpec((tm, k), lambda i: (i, 0)), pl.BlockSpec((k, d), lambda i: (0, 0)), row, par, par],
                  out_specs=(row, row, row), out_shape=(sd, sd, jax.ShapeDtypeStruct((m, d), BF16)), name=name,
                  compiler_params=_cparams(("parallel",)))(a, b, res, g, beta)


def _mm_ln_bwd(a, b, res, branch, g, dy0, coef0, *, tm, name, jobs=()):
    m, k = a.shape
    d = b.shape[0]
    tm = min(tm, m)
    row = pl.BlockSpec((tm, d), lambda i: (i, 0))
    par = pl.BlockSpec((1, d), lambda i: (0, 0))

    def body(a_ref, b_ref, r_ref, br_ref, g_ref, dy0_ref, dt_ref, dtb_ref, dg_ref, db_ref):
        acc = lax.dot_general(a_ref[...].astype(BF16), b_ref[...].astype(BF16), (((1,), (1,)), ((), ())),
                              preferred_element_type=F32)
        dy = coef0 * dy0_ref[...] + acc
        xhat, rstd = _ln_stats(ALPHA * r_ref[...] + br_ref[...])
        dt = _ln_bwd_rows(dy, xhat, rstd, g_ref[...])
        dt_ref[...] = dt
        dtb_ref[...] = dt.astype(BF16)

        @pl.when(pl.program_id(0) == 0)
        def _():
            dg_ref[...] = jnp.zeros_like(dg_ref)
            db_ref[...] = jnp.zeros_like(db_ref)

        dg_ref[...] += jnp.sum(dy * xhat, axis=0, keepdims=True)
        db_ref[...] += jnp.sum(dy, axis=0, keepdims=True)

    pd = jax.ShapeDtypeStruct((1, d), F32)
    outs, jouts = _hosted(
        body, jobs, grid=(m // tm,),
        in_specs=[pl.BlockSpec((tm, k), lambda i: (i, 0)), pl.BlockSpec((d, k), lambda i: (0, 0)), row, row, par, row],
        out_specs=(row, row, par, par),
        out_shape=(jax.ShapeDtypeStruct((m, d), F32), jax.ShapeDtypeStruct((m, d), BF16), pd, pd),
        args=(a, b, res, branch, g, dy0), name=name)
    return (tuple(outs), jouts) if jobs else tuple(outs)


def _head(x2, x2b, p, wg, wp, g, beta, tgt, *, name):
    s, d = x2.shape
    tile = 2 * ROW_TILE
    row = pl.BlockSpec((tile, d), lambda i: (i, 0))
    par = pl.BlockSpec((1, d), lambda i: (0, 0))
    lsp = pl.BlockSpec((1, LANE), lambda i: (0, 0))
    whole = lambda a: pl.BlockSpec(a.shape, lambda i: (0, 0))

    def body(x2_ref, x2b_ref, p_ref, wg_ref, wp_ref, g_ref, be_ref, t_ref,
             loss_ref, dgp_ref, dple_ref, dt_ref, dg_ref, db_ref):
        gate = _sigmoid(_dot(x2b_ref[...], wg_ref[...]))
        ple_v = _dot(p_ref[...], wp_ref[...])
        xhat, rstd = _ln_stats(ALPHA * x2_ref[...] + gate * ple_v)
        err = xhat * g_ref[...] + be_ref[...] - t_ref[...]
        dy = err * (1.0 / d)
        dt = _ln_bwd_rows(dy, xhat, rstd, g_ref[...])
        dt_ref[...] = dt
        dgp_ref[...] = (dt * ple_v * gate * (1.0 - gate)).astype(BF16)
        dple_ref[...] = (dt * gate).astype(BF16)

        @pl.when(pl.program_id(0) == 0)
        def _():
            loss_ref[...] = jnp.zeros_like(loss_ref)
            dg_ref[...] = jnp.zeros_like(dg_ref)
            db_ref[...] = jnp.zeros_like(db_ref)

        loss_ref[...] += 0.5 * jnp.sum(jnp.mean(err * err, axis=-1, keepdims=True))
        dg_ref[...] += jnp.sum(dy * xhat, axis=0, keepdims=True)
        db_ref[...] += jnp.sum(dy, axis=0, keepdims=True)

    sd = jax.ShapeDtypeStruct((s, d), F32)
    sb = jax.ShapeDtypeStruct((s, d), BF16)
    pd = jax.ShapeDtypeStruct((1, d), F32)
    return _pcall(body, grid=(s // tile,),
                  in_specs=[row, row, pl.BlockSpec((tile, p.shape[1]), lambda i: (i, 0)), whole(wg), whole(wp), par, par,
                            row],
                  out_specs=(lsp, row, row, row, par, par),
                  out_shape=(jax.ShapeDtypeStruct((1, LANE), F32), sb, sb, sd, pd, pd),
                  name=name, compiler_params=_cparams(("arbitrary",)))(x2, x2b, p, wg, wp, g, beta, tgt)


CONV_R = 256
PAD = SUBLANE


def _shift_down(ext, s):
    if s == 0:
        return ext[PAD:, :]
    return pltpu.roll(ext, s, 0)[PAD:, :]


def _shift_up(ext, s):
    r = ext.shape[0] - PAD
    if s == 0:
        return ext[:r, :]
    return pltpu.roll(ext, r + PAD - s, 0)[:r, :]


def _conv_rows(xpad_ref, r0, w_ref):
    ext = xpad_ref[pl.ds(r0, CONV_R + PAD), :]
    acc = _shift_down(ext, 0) * w_ref[3:4, :]
    for k in range(3):
        acc = acc + _shift_down(ext, 3 - k) * w_ref[k:k + 1, :]
    return acc, ext


def _fill_front_padded(dst_ref, src_ref, s):
    dst_ref[0:PAD, :] = jnp.zeros((PAD, dst_ref.shape[1]), F32)

    def cp(q, _):
        r0 = pl.multiple_of(q * CONV_R, CONV_R)
        dst_ref[pl.ds(pl.multiple_of(PAD + r0, PAD), CONV_R), :] = src_ref[pl.ds(r0, CONV_R), :]
        return 0

    lax.fori_loop(0, s // CONV_R, cp, 0)


def _conv_silu_fwd(proj, w8, b, *, col0, width, ct, name, jobs=()):
    s = proj.shape[0]
    nb = col0 // ct

    def body(x_ref, w_ref, b_ref, o_ref, xpad):
        _fill_front_padded(xpad, x_ref, s)

        def step(q, _):
            r0 = pl.multiple_of(q * CONV_R, CONV_R)
            acc, _e = _conv_rows(xpad, r0, w_ref)
            pre = acc + b_ref[...]
            o_ref[pl.ds(r0, CONV_R), :] = pre * _sigmoid(pre)
            return 0

        lax.fori_loop(0, s // CONV_R, step, 0)

    (out,), jouts = _hosted(
        body, jobs, grid=(width // ct,),
        in_specs=[pl.BlockSpec((s, ct), lambda j: (0, nb + j)), pl.BlockSpec((SUBLANE, ct), lambda j: (0, j)),
                  pl.BlockSpec((1, ct), lambda j: (0, j))],
        out_specs=[pl.BlockSpec((s, ct), lambda j: (0, j))],
        out_shape=[jax.ShapeDtypeStruct((s, width), F32)],
        scratch_shapes=[pltpu.VMEM((s + PAD, ct), F32)], name=name, args=(proj, w8, b))
    return (out, jouts) if jobs else out


def _conv_bwd_rows(dpad_ref, r0, w_ref):
    return _conv_bwd_ext(dpad_ref[pl.ds(r0, CONV_R + PAD), :], w_ref)


def _conv_bwd_ext(ext, w_ref):
    acc = _shift_up(ext, 0) * w_ref[3:4, :]
    for k in range(3):
        acc = acc + _shift_up(ext, 3 - k) * w_ref[k:k + 1, :]
    return acc


def _conv_silu_bwd(proj, dact, w8, b, *, col0, width, ct, name, jobs=()):
    s = proj.shape[0]
    nb = col0 // ct

    def body(x_ref, d_ref, w_ref, b_ref, dx_ref, dwb_ref, xpad, dpad):
        _fill_front_padded(xpad, x_ref, s)
        dpad[pl.ds(s, PAD), :] = jnp.zeros((PAD, ct), F32)
        dwb_ref[...] = jnp.zeros_like(dwb_ref)

        def step(q, _):
            r0 = pl.multiple_of(q * CONV_R, CONV_R)
            acc, ext = _conv_rows(xpad, r0, w_ref)
            pre = acc + b_ref[...]
            sg = _sigmoid(pre)
            dpre = d_ref[pl.ds(r0, CONV_R), :] * sg * (1.0 + pre * (1.0 - sg))
            dpad[pl.ds(r0, CONV_R), :] = dpre
            for k in range(4):
                dwb_ref[k:k + 1, :] += jnp.sum(dpre * _shift_down(ext, 3 - k), axis=0, keepdims=True)
            dwb_ref[4:5, :] += jnp.sum(dpre, axis=0, keepdims=True)
            return 0

        lax.fori_loop(0, s // CONV_R, step, 0)

        def step2(q, _):
            r0 = pl.multiple_of(q * CONV_R, CONV_R)
            dx_ref[pl.ds(r0, CONV_R), :] = _conv_bwd_rows(dpad, r0, w_ref).astype(BF16)
            return 0

        lax.fori_loop(0, s // CONV_R, step2, 0)

    colb = pl.BlockSpec((s, ct), lambda j: (0, j))
    outs, jouts = _hosted(
        body, jobs, grid=(width // ct,),
        in_specs=[pl.BlockSpec((s, ct), lambda j: (0, nb + j)), colb, pl.BlockSpec((SUBLANE, ct), lambda j: (0, j)),
                  pl.BlockSpec((1, ct), lambda j: (0, j))],
        out_specs=(colb, pl.BlockSpec((SUBLANE, ct), lambda j: (0, j))),
        out_shape=(jax.ShapeDtypeStruct((s, width), BF16), jax.ShapeDtypeStruct((SUBLANE, width), F32)),
        scratch_shapes=[pltpu.VMEM((s + PAD, ct), F32), pltpu.VMEM((s + PAD, ct), F32)], name=name,
        args=(proj, dact, w8, b))
    return (tuple(outs), jouts) if jobs else tuple(outs)


LRU_CT = 128


def _row_of(v, r):
    return jnp.sum(jnp.where(_iota((v.shape[0], 1), 0) == r, v, 0.0), axis=0, keepdims=True)


def _scan_fwd(a, u):
    r = a.shape[0]
    row = _iota((r, 1), 0)
    d = 1
    while d < r:
        valid = row >= d
        u = jnp.where(valid, a * pltpu.roll(u, d, 0) + u, u)
        a = jnp.where(valid, a * pltpu.roll(a, d, 0), a)
        d *= 2
    return a, u


def _scan_rev(b, u):
    r = b.shape[0]
    row = _iota((r, 1), 0)
    d = 1
    while d < r:
        valid = row < r - d
        u = jnp.where(valid, b * pltpu.roll(u, r - d, 0) + u, u)
        b = jnp.where(valid, b * pltpu.roll(b, r - d, 0), b)
        d *= 2
    return b, u


def _lru_chunk(xpad, r0, cw_ref, cb, wa, ba, wx, bx, sp):
    acc, ext = _conv_rows(xpad, r0, cw_ref)
    xl = acc + cb
    r = _sigmoid(_dot(xl, wa) + ba)
    i = _sigmoid(_dot(xl, wx) + bx)
    la = -LRU_C * r * sp
    a = jnp.exp(la)
    a2 = jnp.exp(2.0 * la)
    mult = jnp.sqrt(-jnp.tanh(la) * (a2 + 1.0))
    first = (r0 + _iota((CONV_R, 1), 0)) == 0
    mult = jnp.where(first, 1.0, mult)
    return ext, xl, r, i, a, a2, mult, first


def _lru_specs(s):
    ct = LRU_CT
    nb_g = COL_G // ct
    return dict(
        x=pl.BlockSpec((s, ct), lambda j: (0, j)),
        g=pl.BlockSpec((s, ct), lambda j: (0, nb_g + j)),
        col=pl.BlockSpec((s, ct), lambda j: (0, j)),
        cw=pl.BlockSpec((SUBLANE, ct), lambda j: (0, j)),
        vec=pl.BlockSpec((1, ct), lambda j: (0, j)),
        gate=pl.BlockSpec((None, ct, ct), lambda j: (j, 0, 0)),
    )


def _lru_fwd(proj, cw8, cb, wa_bd, ba, wx_bd, bx, ap, *, name, jobs=()):
    s = proj.shape[0]
    ct = LRU_CT
    sp_ = _lru_specs(s)

    def body(x_ref, g_ref, cw_ref, cb_ref, wa_ref, ba_ref, wx_ref, bx_ref, ap_ref, y_ref, h_ref, xpad):
        _fill_front_padded(xpad, x_ref, s)
        sp = _softplus(-ap_ref[...])

        def step(q, carry):
            r0 = pl.multiple_of(q * CONV_R, CONV_R)
            _e, xl, _r, i, a, _a2, mult, _f = _lru_chunk(xpad, r0, cw_ref, cb_ref[...], wa_ref[...], ba_ref[...],
                                                       wx_ref[...], bx_ref[...], sp)
            acum, ucum = _scan_fwd(a, xl * i * mult)
            h = acum * carry + ucum
            h_ref[pl.ds(r0, CONV_R), :] = h
            ge, _th = _gelu(g_ref[pl.ds(r0, CONV_R), :])
            y_ref[pl.ds(r0, CONV_R), :] = (ge * h).astype(BF16)
            return _row_of(h, CONV_R - 1)

        lax.fori_loop(0, s // CONV_R, step, jnp.zeros((1, ct), F32))

    (ymix, hs), jouts = _hosted(
        body, jobs, grid=(LRU_W // ct,),
        in_specs=[sp_["x"], sp_["g"], sp_["cw"], sp_["vec"], sp_["gate"], sp_["vec"], sp_["gate"], sp_["vec"], sp_["vec"]],
        out_specs=(sp_["col"], sp_["col"]),
        out_shape=(jax.ShapeDtypeStruct((s, LRU_W + SSD_W), BF16), jax.ShapeDtypeStruct((s, LRU_W), F32)),
        scratch_shapes=[pltpu.VMEM((s + PAD, ct), F32)],
        name=name, args=(proj, proj, cw8, cb, wa_bd, ba, wx_bd, bx, ap))
    return ((ymix, hs), jouts) if jobs else (ymix, hs)


def _lru_bwd(proj, dy, hs, cw8, cb, wa_bd, ba, wx_bd, bx, ap, *, name, jobs=()):
    s = proj.shape[0]
    ct = LRU_CT
    sp_ = _lru_specs(s)

    nq = s // CONV_R

    def body(x_ref, g_ref, dy_ref, h_ref, cw_ref, cb_ref, wa_ref, ba_ref, wx_ref, bx_ref, ap_ref,
             dx_ref, dg_ref, dcwb_ref, dwa_ref, dwx_ref, xpad, hpad):
        _fill_front_padded(xpad, x_ref, s)
        _fill_front_padded(hpad, h_ref, s)
        apv = ap_ref[...]
        sp = _softplus(-apv)
        cb_v, wa, ba_v, wx, bx_v = cb_ref[...], wa_ref[...], ba_ref[...], wx_ref[...], bx_ref[...]
        dcwb_ref[...] = jnp.zeros_like(dcwb_ref)
        dwa_ref[...] = jnp.zeros_like(dwa_ref)
        dwx_ref[...] = jnp.zeros_like(dwx_ref)

        def back(k, carry):
            g_next, a_next, dxl_next = carry
            last_row = _iota((CONV_R, 1), 0) == CONV_R - 1
            r0 = pl.multiple_of((nq - 1 - k) * CONV_R, CONV_R)
            ext, xl, r, i, a, a2, mult, first = _lru_chunk(xpad, r0, cw_ref, cb_v, wa, ba_v, wx, bx_v, sp)
            gv = g_ref[pl.ds(r0, CONV_R), :]
            dyv = dy_ref[pl.ds(r0, CONV_R), :]
            hext = hpad[pl.ds(r0, CONV_R + PAD), :]
            ge, th = _gelu(gv)
            dg_ref[pl.ds(r0, CONV_R), :] = (dyv * _shift_down(hext, 0) * _gelu_grad(gv, th)).astype(BF16)
            b = jnp.where(last_row, a_next, pltpu.roll(a, CONV_R - 1, 0))
            bcum, dcum = _scan_rev(b, dyv * ge)
            gval = dcum + bcum * g_next
            hprev = _shift_down(hext, 1)
            da = gval * hprev
            dxl = gval * i * mult
            di = gval * xl * mult
            dmult = jnp.where(first, 0.0, gval * xl * i)
            dla = da * a - dmult * a2 / mult
            dr = dla * (-LRU_C) * sp
            dcwb_ref[7:8, :] += jnp.sum(dla * (-LRU_C) * r, axis=0, keepdims=True)
            dpr = dr * r * (1.0 - r)
            dpi = di * i * (1.0 - i)
            dxl = dxl + _dot_nt(dpr, wa) + _dot_nt(dpi, wx)
            dwa_ref[...] += _dot_tn(xl, dpr)
            dwx_ref[...] += _dot_tn(xl, dpi)
            dcwb_ref[5:6, :] += jnp.sum(dpr, axis=0, keepdims=True)
            dcwb_ref[6:7, :] += jnp.sum(dpi, axis=0, keepdims=True)
            for tap in range(4):
                dcwb_ref[tap:tap + 1, :] += jnp.sum(dxl * _shift_down(ext, 3 - tap), axis=0, keepdims=True)
            dcwb_ref[4:5, :] += jnp.sum(dxl, axis=0, keepdims=True)
            dx_ref[pl.ds(r0, CONV_R), :] = _conv_bwd_ext(jnp.concatenate([dxl, dxl_next], axis=0), cw_ref).astype(BF16)
            return _row_of(gval, 0), _row_of(a, 0), dxl[:PAD, :]

        zero = jnp.zeros((1, ct), F32)
        lax.fori_loop(0, nq, back, (zero, zero, jnp.zeros((PAD, ct), F32)))
        dcwb_ref[7:8, :] = dcwb_ref[7:8, :] * (-_sigmoid(-apv))

    nt = LRU_W // ct
    outs, jouts = _hosted(
        body, jobs, grid=(nt,),
        in_specs=[sp_["x"], sp_["g"], sp_["col"], sp_["col"], sp_["cw"], sp_["vec"], sp_["gate"], sp_["vec"], sp_["gate"],
                  sp_["vec"], sp_["vec"]],
        out_specs=(sp_["col"], sp_["col"], sp_["cw"], sp_["gate"], sp_["gate"]),
        out_shape=(jax.ShapeDtypeStruct((s, LRU_W), BF16), jax.ShapeDtypeStruct((s, LRU_W), BF16),
                   jax.ShapeDtypeStruct((SUBLANE, LRU_W), F32), jax.ShapeDtypeStruct((nt, ct, ct), F32),
                   jax.ShapeDtypeStruct((nt, ct, ct), F32)),
        scratch_shapes=[pltpu.VMEM((s + PAD, ct), F32), pltpu.VMEM((s + PAD, ct), F32)],
        name=name, args=(proj, proj, dy, hs, cw8, cb, wa_bd, ba, wx_bd, bx, ap))
    return (tuple(outs), jouts) if jobs else tuple(outs)


def _split3(v):
    hi = v.astype(BF16)
    r1 = v - hi.astype(F32)
    mid = r1.astype(BF16)
    lo = (r1 - mid.astype(F32)).astype(BF16)
    return hi, mid, lo


def _dot01(m01, v):
    mb = m01.astype(BF16)
    hi, mid, lo = _split3(v)
    f = lambda part: jnp.dot(mb, part, preferred_element_type=F32)
    return f(hi) + f(mid) + f(lo)


def _dot01_r(v, m01, parts=3):
    mb = m01.astype(BF16)
    acc = None
    for part in _split3(v)[:parts]:
        t = jnp.dot(part, mb, preferred_element_type=F32)
        acc = t if acc is None else acc + t
    return acc


def _ssd_prep(dtr, bias, alog_pad):
    l = CHUNK
    lane = _iota((1, LANE), 1)
    a_head = jnp.where(lane < N_HEAD, -jnp.exp(alog_pad), 0.0)
    dt = _softplus(dtr + bias)
    tril = (_iota((l, l), 1) <= _iota((l, l), 0)).astype(F32)
    a = dt * a_head
    cs = _dot01(tril, a)
    tot = jnp.sum(a, axis=0, keepdims=True)
    return dict(a_head=a_head, dt=dt, tril=tril, cs=cs, tot=tot)


def _col(v, h):
    lane = _iota(v.shape, 1)
    return jnp.sum(jnp.where(lane == h, v, 0.0), axis=1, keepdims=True)


def _decay_mat(cs, cst_ref, h, causal):
    row = cst_ref[h:h + 1, :]
    return jnp.exp(jnp.where(causal, _col(cs, h) - row, NEG_BIG))


def _head_mask(j, rows=CHUNK):
    lane = _iota((rows, GROUP_W), 1)
    return (lane >= j * HEAD_P) & (lane < (j + 1) * HEAD_P)


def _over_heads(v, g):
    r = v.shape[0]
    out = jnp.zeros((r, GROUP_W), F32)
    for j in range(4):
        out = jnp.where(_head_mask(j, r), _col(v, 4 * g + j), out)
    return out


def _ssd_group_fwd(q, g, xs_g, bg, cg, ht_g, cst_ref, causal, dx_g):
    dtx_g, csx_g, totx_g = _over_heads(q["dt"], g), _over_heads(q["cs"], g), _over_heads(q["tot"], g)
    xdt = xs_g * dtx_g
    ex = jnp.exp(csx_g)
    cb = _dot_nt(cg, bg)
    yoff = _dot(cg, ht_g) * ex
    ydiag = jnp.zeros((CHUNK, GROUP_W), F32)
    lms = []
    for j in range(4):
        lms.append(_decay_mat(q["cs"], cst_ref, 4 * g + j, causal))
        ydiag = jnp.where(_head_mask(j), _dot(cb * lms[j], xdt), ydiag)
    y = ydiag + yoff + xs_g * dx_g
    dsx = jnp.exp(totx_g - csx_g)
    return y, dict(xdt=xdt, ex=ex, cb=cb, yoff=yoff, dsx=dsx, dtx=dtx_g, totx=totx_g, lms=lms)


def _gated_norm_fwd(y_g, z_g, w_g):
    sz = _sigmoid(z_g)
    silu = z_g * sz
    yf = y_g * silu
    rs = lax.rsqrt(jnp.mean(yf * yf, axis=1, keepdims=True) + RMS_EPS)
    yn = yf * rs
    return yn * w_g, (sz, silu, rs, yn)


def _ssd_fwd(xact, proj, ymix, bias_pad, alog_pad, dxp, normw, *, name, jobs=()):
    s = xact.shape[0]
    nc = s // CHUNK

    def body(xa_ref, dt_ref, z_ref, _ymix_ref, bias_ref, alp_ref, dx_ref, nw_ref, y_ref, hp_ref, ht, cst):
        @pl.when(pl.program_id(0) == 0)
        def _():
            ht[...] = jnp.zeros_like(ht)

        hp_ref[...] = ht[...]
        q = _ssd_prep(dt_ref[...], bias_ref[...], alp_ref[...])
        cst[...] = q["cs"].T
        causal = q["tril"] > 0.0
        for g in range(N_GROUP):
            sl = slice(g * GROUP_W, (g + 1) * GROUP_W)
            xs_g = xa_ref[:, sl]
            bg = xa_ref[:, SSD_W + g * N_STATE:SSD_W + (g + 1) * N_STATE]
            cg = xa_ref[:, SSD_W + N_GROUP * N_STATE + g * N_STATE:SSD_W + N_GROUP * N_STATE + (g + 1) * N_STATE]
            ht_g = ht[:, sl]
            y, f = _ssd_group_fwd(q, g, xs_g, bg, cg, ht_g, cst, causal, dx_ref[:, sl])
            out, _ = _gated_norm_fwd(y, z_ref[:, sl], nw_ref[:, sl])
            y_ref[:, sl] = out.astype(BF16)
            ht[:, sl] = jnp.exp(f["totx"]) * ht_g + _dot_tn(bg, f["xdt"] * f["dsx"])

    par = lambda w: pl.BlockSpec((1, w), lambda c: (0, 0))
    (ycat, hprev), jouts = _hosted(
        body, jobs, grid=(nc,),
        in_specs=[pl.BlockSpec((CHUNK, XBC), lambda c: (c, 0)),
                  pl.BlockSpec((CHUNK, LANE), lambda c: (c, COL_DT // LANE)),
                  pl.BlockSpec((CHUNK, SSD_W), lambda c: (c, COL_Z // SSD_W)),
                  ANY_SPEC, par(LANE), par(LANE), par(SSD_W), par(SSD_W)],
        out_specs=(pl.BlockSpec((CHUNK, SSD_W), lambda c: (c, LRU_W // SSD_W)),
                   pl.BlockSpec((None, N_STATE, SSD_W), lambda c: (c, 0, 0))),
        out_shape=(jax.ShapeDtypeStruct(ymix.shape, ymix.dtype), jax.ShapeDtypeStruct((nc, N_STATE, SSD_W), F32)),
        scratch_shapes=[pltpu.VMEM((N_STATE, SSD_W), F32), pltpu.VMEM((CHUNK, LANE), F32)],
        aliases={3: 0}, name=name, args=(xact, proj, proj, ymix, bias_pad, alog_pad, dxp, normw))
    return ((ycat, hprev), jouts) if jobs else (ycat, hprev)


def _ssd_bwd(xact, proj, dycat, hprev, bias_pad, alog_pad, dxp, normw, *, name, jobs=()):
    s = xact.shape[0]
    nc = s // CHUNK
    l = CHUNK

    def body(xa_ref, dt_ref, z_ref, dy_ref, hp_ref, bias_ref, alp_ref, dx_ref, nw_ref,
             dxa_ref, ddt_ref, dz_ref, dnw_ref, small_ref, dht, cst, accx, dcsx_s, ddtx_s):
        step = pl.program_id(0)

        @pl.when(step == 0)
        def _():
            dht[...] = jnp.zeros_like(dht)
            accx[...] = jnp.zeros_like(accx)
            dnw_ref[...] = jnp.zeros_like(dnw_ref)
            small_ref[...] = jnp.zeros_like(small_ref)

        dtr = dt_ref[...]
        q = _ssd_prep(dtr, bias_ref[...], alp_ref[...])
        cst[...] = q["cs"].T
        causal = q["tril"] > 0.0
        lane = _iota((l, LANE), 1)
        head_row = _iota((LANE, l), 0)
        dcs_head = jnp.zeros((l, LANE), F32)
        dcs_rows = jnp.zeros((LANE, l), F32)
        for g in range(N_GROUP):
            sl = slice(g * GROUP_W, (g + 1) * GROUP_W)
            slb = slice(SSD_W + g * N_STATE, SSD_W + (g + 1) * N_STATE)
            slc = slice(SSD_W + N_GROUP * N_STATE + g * N_STATE, SSD_W + N_GROUP * N_STATE + (g + 1) * N_STATE)
            xs_g, bg, cg = xa_ref[:, sl], xa_ref[:, slb], xa_ref[:, slc]
            ht_g = hp_ref[:, sl]
            dxp_g = dx_ref[:, sl]
            y, f = _ssd_group_fwd(q, g, xs_g, bg, cg, ht_g, cst, causal, dxp_g)
            z_g, nw_g = z_ref[:, sl], nw_ref[:, sl]
            _o, (sz, silu, rs, yn) = _gated_norm_fwd(y, z_g, nw_g)
            dout = dy_ref[:, sl]
            dnw_ref[:, sl] += jnp.sum(dout * yn, axis=0, keepdims=True)
            dyn = dout * nw_g
            dyf = rs * (dyn - yn * jnp.mean(dyn * yn, axis=1, keepdims=True))
            dy = dyf * silu
            dz_ref[:, sl] = (dyf * y * sz * (1.0 + z_g * (1.0 - sz))).astype(BF16)
            accx[0:1, sl] += jnp.sum(dy * xs_g, axis=0, keepdims=True)
            dyo = dy * f["ex"]
            dcg = _dot_nt(dyo, ht_g)
            dht_prev = _dot_tn(cg, dyo)
            dcsx = dy * f["yoff"]
            xdt = f["xdt"]
            dxdt = jnp.zeros((l, GROUP_W), F32)
            dcb = jnp.zeros((l, l), F32)
            for j in range(4):
                h = 4 * g + j
                lm = f["lms"][j]
                sc = f["cb"] * lm
                mask = _head_mask(j)
                ds_ = jnp.where(causal, _dot_nt(jnp.where(mask, dy, 0.0), xdt), 0.0)
                dxdt = jnp.where(mask, _dot_tn(sc, dy), dxdt)
                dcb = dcb + ds_ * lm
                m = ds_ * sc
                dcs_head = dcs_head + jnp.where(lane == h, jnp.sum(m, axis=1, keepdims=True), 0.0)
                dcs_rows = dcs_rows + jnp.where(head_row == h, jnp.sum(m, axis=0, keepdims=True), 0.0)
            dhn = dht[:, sl]
            etot = jnp.exp(f["totx"])
            dxd = _dot(bg, dhn)
            dbg = _dot_nt(xdt * f["dsx"], dhn)
            dxdt = dxdt + dxd * f["dsx"]
            qq = dxd * xdt * f["dsx"]
            dcsx = dcsx - qq
            dtot = jnp.sum(qq, axis=0, keepdims=True) + jnp.sum(dhn * ht_g, axis=0, keepdims=True) * etot
            dht[:, sl] = etot * dhn + dht_prev
            dcg = dcg + _dot(dcb, bg)
            dbg = dbg + _dot_tn(dcb, cg)
            dxa_ref[:, sl] = dxdt * f["dtx"] + dy * dxp_g
            dxa_ref[:, slb] = dbg
            dxa_ref[:, slc] = dcg
            dcsx_s[:, sl] = dcsx
            ddtx_s[:, sl] = dxdt * xs_g
            accx[2:3, sl] = dtot
        reduce = (jnp.right_shift(_iota((SSD_W, LANE), 0), 6) == _iota((SSD_W, LANE), 1)).astype(F32)
        triu = (_iota((l, l), 1) >= _iota((l, l), 0)).astype(F32)
        dtot = _dot01_r(accx[...], reduce)[2:3, :]
        dcs_head = dcs_head - dcs_rows.T
        da_head = _dot01(triu, dcs_head + _dot01_r(dcsx_s[...], reduce, parts=2)) + dtot
        ddt = _dot01_r(ddtx_s[...], reduce, parts=2) + da_head * q["a_head"]
        small_ref[1:2, :] += jnp.sum(da_head * q["dt"], axis=0, keepdims=True)
        ddtr = ddt * _sigmoid(dtr + bias_ref[...])
        ddt_ref[...] = ddtr.astype(BF16)
        small_ref[0:1, :] += jnp.sum(ddtr, axis=0, keepdims=True)

        @pl.when(step == nc - 1)
        def _():
            small_ref[1:2, :] = small_ref[1:2, :] * q["a_head"]
            small_ref[2:3, :] = _dot01_r(accx[...], reduce)[0:1, :]

    rev = lambda c: nc - 1 - c
    par = lambda w: pl.BlockSpec((1, w), lambda c: (0, 0))
    outs, jouts = _hosted(
        body, jobs, grid=(nc,),
        in_specs=[pl.BlockSpec((CHUNK, XBC), lambda c: (rev(c), 0)),
                  pl.BlockSpec((CHUNK, LANE), lambda c: (rev(c), COL_DT // LANE)),
                  pl.BlockSpec((CHUNK, SSD_W), lambda c: (rev(c), COL_Z // SSD_W)),
                  pl.BlockSpec((CHUNK, SSD_W), lambda c: (rev(c), 1)),
                  pl.BlockSpec((None, N_STATE, SSD_W), lambda c: (rev(c), 0, 0)),
                  par(LANE), par(LANE), par(SSD_W), par(SSD_W)],
        out_specs=(pl.BlockSpec((CHUNK, XBC), lambda c: (rev(c), 0)),
                   pl.BlockSpec((CHUNK, LANE), lambda c: (rev(c), 0)),
                   pl.BlockSpec((CHUNK, SSD_W), lambda c: (rev(c), 0)),
                   par(SSD_W), pl.BlockSpec((SUBLANE, LANE), lambda c: (0, 0))),
        out_shape=(jax.ShapeDtypeStruct((s, XBC), F32), jax.ShapeDtypeStruct((s, LANE), BF16),
                   jax.ShapeDtypeStruct((s, SSD_W), BF16), jax.ShapeDtypeStruct((1, SSD_W), F32),
                   jax.ShapeDtypeStruct((SUBLANE, LANE), F32)),
        scratch_shapes=[pltpu.VMEM((N_STATE, SSD_W), F32), pltpu.VMEM((CHUNK, LANE), F32),
                        pltpu.VMEM((SUBLANE, SSD_W), F32), pltpu.VMEM((CHUNK, SSD_W), F32),
                        pltpu.VMEM((CHUNK, SSD_W), F32)],
        name=name, args=(xact, proj, proj, dycat, hprev, bias_pad, alog_pad, dxp, normw))
    return (tuple(outs), jouts) if jobs else tuple(outs)


def _blockdiag(w):
    w2 = w.reshape(N_HEAD // 2, 2, HEAD_P, HEAD_P)
    z = jnp.zeros((N_HEAD // 2, HEAD_P, HEAD_P), w.dtype)
    top = jnp.concatenate([w2[:, 0], z], axis=2)
    bot = jnp.concatenate([z, w2[:, 1]], axis=2)
    return jnp.concatenate([top, bot], axis=1)


def _unblockdiag(wbd):
    a = wbd[:, :HEAD_P, :HEAD_P]
    b = wbd[:, HEAD_P:, HEAD_P:]
    return jnp.stack([a, b], axis=1).reshape(N_HEAD, HEAD_P, HEAD_P)


def _pad_rows8(w):
    return jnp.concatenate([w, jnp.zeros((SUBLANE - w.shape[0], w.shape[1]), w.dtype)], axis=0)


def _pad_lane(v):
    return jnp.concatenate([v, jnp.zeros((1, LANE - v.shape[1]), v.dtype)], axis=1)


class _NoExchange:
    def ride(self, host):
        return []

    def done(self, jobs, outs, w):
        pass

    def grad(self, name, val):
        pass

    def small(self, raw):
        pass

    def pairs_now(self):
        pass


def _local_step(x, p, tgt, w, hooks=_NoExchange()):
    cw_l = _pad_rows8(w["lru_conv_w"])
    cw_s = _pad_rows8(w["ssd_conv_w"])
    wa_bd = _blockdiag(w["lru_gate_a_w"])
    wx_bd = _blockdiag(w["lru_gate_x_w"])
    ba = w["lru_gate_a_b"].reshape(1, LRU_W)
    bx = w["lru_gate_x_b"].reshape(1, LRU_W)
    bias_pad = _pad_lane(w["ssd_dt_bias"])
    alog_pad = _pad_lane(w["ssd_a_log"])
    dxp = jnp.repeat(w["ssd_d"], HEAD_P, axis=1)

    def host(fn, *a, name, **k):
        jobs = hooks.ride(name)
        res = fn(*a, name=name, jobs=jobs, **k)
        if jobs:
            res, jouts = res
            hooks.done(jobs, jouts, w)
        return res

    def grad(n, val):
        g[n] = val
        hooks.grad(n, val)

    xb = x.astype(BF16)
    proj = host(_mm, xb, w["w_in_t"], "nt", tm=2048, tn=512, name="in_proj")
    ymix, h_lru = host(_lru_fwd, proj, cw_l, w["lru_conv_b"], wa_bd, ba, wx_bd, bx, w["lru_a_param"], name="lru_fwd")
    xact = host(_conv_silu_fwd, proj, cw_s, w["ssd_conv_b"], col0=COL_XBC, width=XBC, ct=256, name="ssd_conv_fwd")
    ycat, hprev = host(_ssd_fwd, xact, proj, ymix, bias_pad, alog_pad, dxp, w["ssd_norm_w"], name="ssd_fwd")
    mix, x1, x1b = _mm_ln(ycat, w["w_out"], x, w["ln1_g"], w["ln1_b"], tm=512, name="out_proj")
    pre = _mm(x1b, w["w_ff1"], "nn", tm=2048, tn=512, out_dtype=BF16, name="ff1")
    ff, x2, x2b = _mm_ln(pre, w["w_ff2"], x1, w["ln2_g"], w["ln2_b"], tm=512, a_fn=_relu2, name="ff2")
    loss, dgpre, dple, dt3, dg3, db3 = _head(x2, x2b, p, w["w_ple_gate"], w["w_ple"], w["ln3_g"], w["ln3_b"], tgt,
                                             name="head")

    g = {}
    g["ln3_g"], g["ln3_b"] = dg3, db3
    grad("w_ple_gate", _mm(x2b, dgpre, "tn", tm=512, tn=1024, out_dtype=BF16, name="d_w_ple_gate"))
    grad("w_ple", _mm(p, dple, "tn", tm=256, tn=512, dest_major=True, out_dtype=BF16, name="d_w_ple"))
    dt2, dt2b, g["ln2_g"], g["ln2_b"] = host(_mm_ln_bwd, dgpre, w["w_ple_gate"], x1, ff, w["ln2_g"], dt3, ALPHA,
                                             tm=512, name="d_x2")
    grad("w_ff2", host(_mm, pre, dt2b, "tn", tm=512, tn=1024, a_fn=_relu2, out_dtype=BF16, name="d_w_ff2"))
    dpre = host(_mm, dt2b, w["w_ff2"], "nt", tm=2048, tn=512, extra=pre, out_dtype=BF16,
                epi=lambda acc, pv: acc * 2.0 * jnp.maximum(pv.astype(F32), 0.0), name="d_pre")
    grad("w_ff1", host(_mm, x1b, dpre, "tn", tm=1024, tn=512, dest_major=True, out_dtype=BF16, name="d_w_ff1"))
    dt1, dt1b, g["ln1_g"], g["ln1_b"] = host(_mm_ln_bwd, dpre, w["w_ff1"], x, mix, w["ln1_g"], dt2, ALPHA,
                                             tm=256, name="d_x1")
    grad("w_out", host(_mm, ycat, dt1b, "tn", tm=512, tn=1024, out_dtype=BF16, name="d_w_out"))
    dycat = host(_mm, dt1b, w["w_out"], "nt", tm=2048, tn=512, name="d_ycat")
    dxl, dgl, dcwb_l, dwa, dwx = host(_lru_bwd, proj, dycat, h_lru, cw_l, w["lru_conv_b"], wa_bd, ba, wx_bd, bx,
                                      w["lru_a_param"], name="lru_bwd")
    g["lru_gate_a_w"] = _unblockdiag(dwa)
    g["lru_gate_x_w"] = _unblockdiag(dwx)
    raw = dict(lru=dcwb_l, gate_a=g["lru_gate_a_w"].reshape(N_HEAD * HEAD_P, HEAD_P).astype(BF16),
               gate_x=g["lru_gate_x_w"].reshape(N_HEAD * HEAD_P, HEAD_P).astype(BF16))
    hooks.small(raw)
    dxact, ddt, dz, g["ssd_norm_w"], small = host(_ssd_bwd, xact, proj, dycat, hprev, bias_pad, alog_pad, dxp,
                                                   w["ssd_norm_w"], name="ssd_bwd")
    dxbc, dcwb_s = host(_conv_silu_bwd, proj, dxact, cw_s, w["ssd_conv_b"], col0=COL_XBC, width=XBC, ct=256,
                        name="ssd_conv_bwd")
    pieces, offsets = [dxl, dgl, dz, dxbc, ddt], [0, COL_G, COL_Z, COL_XBC, COL_DT]

    g["lru_conv_w"] = dcwb_l[0:4]
    g["lru_conv_b"] = dcwb_l[4:5]
    g["lru_gate_a_b"] = dcwb_l[5:6]
    g["lru_gate_x_b"] = dcwb_l[6:7]
    g["lru_a_param"] = dcwb_l[7:8]
    g["ssd_conv_w"] = dcwb_s[0:4]
    g["ssd_conv_b"] = dcwb_s[4:5]
    g["ssd_dt_bias"] = small[0:1, :N_HEAD]
    g["ssd_a_log"] = small[1:2, :N_HEAD]
    g["ssd_d"] = small[2:3, :N_HEAD]
    rows = jnp.concatenate([g[n] for n in ("ssd_norm_w", "ln1_g", "ln1_b", "ln2_g", "ln2_b", "ln3_g", "ln3_b")]
                           + [jnp.broadcast_to(loss[:, 0:1], (1, D_MODEL))], axis=0)
    late = dict(ssd=dcwb_s, heads=small, rows=rows)
    hooks.small(late)
    raw.update(late)
    dwt = None
    for q, (pc, off) in enumerate(zip(pieces, offsets)):
        dwt = host(_mm, pc, xb, "tn", tm=512, tn=1024, out_dtype=BF16, into=(dwt, off, D_IN),
                   name="d_w_in_%d" % q)
    grad("w_in", dwt)
    hooks.pairs_now()
    grad_x = host(_mm_pieces, pieces, offsets, w["w_in_t"], tm=256, extra=dt1, epi=lambda acc, e: acc + ALPHA * e,
                  name="d_x")
    return loss[0, 0], grad_x, g, raw


ANY_SPEC = pl.BlockSpec(memory_space=pl.ANY)


def _mesh_pos():
    return lax.axis_index("x"), lax.axis_index("y"), lax.axis_index("c")


def _remote(src, dst, send, recv, k, to):
    return pltpu.make_async_remote_copy(src_ref=src, dst_ref=dst, send_sem=send.at[k], recv_sem=recv.at[k],
                                        device_id=to, device_id_type=MESH_T)


class _Job:
    N_SEM = 9

    def __init__(self, kind, inp):
        self.kind, self.inp = kind, inp
        shape = {"gather": (N_DEV,) + inp.shape, "relay": (N_DEV,) + inp.shape, "pair": (4,) + inp.shape[1:],
                 "chip": (4 + 2,) + inp.shape[1:]}[kind]
        self.out = jax.ShapeDtypeStruct(shape, inp.dtype)
        self.rows = inp.shape[0] if kind == "relay" else inp.shape[1]
        self.top = (self.rows // 2) // 16 * 16

    def _blk(self, ref, k):
        return ref.at[k]

    def _relay_copies(self, inp, out, send, recv):
        x, y, c = _mesh_pos()
        sib, xn, yn, dg = (x, y, 1 - c), (1 - x, y, c), (x, 1 - y, c), (1 - x, 1 - y, c)
        blk = lambda p, cc=None: out.at[4 * p[0] + 2 * p[1] + (p[2] if cc is None else cc)]
        top = lambda r: r.at[pl.ds(0, self.top)]
        bot = lambda r: r.at[pl.ds(self.top, self.rows - self.top)]
        mine = blk((x, y, c))
        plan = [
            (inp, mine, sib, blk(sib)),
            (inp, mine, xn, blk(xn)),
            (inp, mine, yn, blk(yn)),
            (top(blk(xn)), top(blk(xn)), yn, top(blk(dg))),
            (bot(blk(yn)), bot(blk(yn)), xn, bot(blk(dg))),
            (blk(xn), blk(xn), sib, blk(xn, 1 - c)),
            (blk(yn), blk(yn), sib, blk(yn, 1 - c)),
            (top(blk(dg)), top(blk(dg)), sib, top(blk(dg, 1 - c))),
            (bot(blk(dg)), bot(blk(dg)), sib, bot(blk(dg, 1 - c))),
        ]
        me = (x, y, c)
        return [(_remote(s, d, send, recv, k, to), _remote(s, land, send, recv, k, me))
                for k, (s, d, to, land) in enumerate(plan)]

    def _chip_copies(self, inp, out, send, recv):
        x, y, c = _mesh_pos()
        xn, yn = (1 - x, y, c), (x, 1 - y, c)
        kme, kx, ky, kd = 2 * x + y, 2 * (1 - x) + y, 2 * x + (1 - y), 2 * (1 - x) + (1 - y)
        top = lambda r: r.at[pl.ds(0, self.top)]
        bot = lambda r: r.at[pl.ds(self.top, self.rows - self.top)]
        plan = [
            (inp.at[kx], out.at[kme], xn, out.at[kx]),
            (inp.at[ky], out.at[kme], yn, out.at[ky]),
            (top(inp.at[kd]), top(out.at[4]), xn, top(out.at[4])),
            (bot(inp.at[kd]), bot(out.at[5]), yn, bot(out.at[5])),
            (top(out.at[4]), top(out.at[kx]), yn, top(out.at[kd])),
            (bot(out.at[5]), bot(out.at[ky]), xn, bot(out.at[kd])),
        ]
        me = (x, y, c)
        return [(_remote(s, d, send, recv, k, to), _remote(s, land, send, recv, k, me))
                for k, (s, d, to, land) in enumerate(plan)]

    def _places(self):
        x, y, c = _mesh_pos()
        return (x, y, c), (x, y, 1 - c), [(1 - x, y), (x, 1 - y), (1 - x, 1 - y)]

    def start(self, inp, out, send, recv, loc):
        me, sibling, chips = self._places()
        x, y, c = me
        if self.kind == "relay":
            pltpu.make_async_copy(inp, out.at[4 * x + 2 * y + c], loc.at[0]).start()
            cps = self._relay_copies(inp, out, send, recv)
            for k in (0, 1, 2):
                cps[k][0].start()
        elif self.kind == "gather":
            mine = out.at[4 * x + 2 * y + c]
            pltpu.make_async_copy(inp, mine, loc.at[0]).start()
            _remote(inp, mine, send, recv, 0, sibling).start()
            for j, chip in enumerate(chips):
                _remote(inp, mine, send, recv, 1 + j, (*chip, c)).start()
        elif self.kind == "pair":
            for k in range(4):
                _remote(inp.at[2 * k + (1 - c)], out.at[k], send, recv, k, sibling).start()
        else:
            kme = 2 * x + y
            pltpu.make_async_copy(inp.at[kme], out.at[kme], loc.at[0]).start()
            cps = self._chip_copies(inp, out, send, recv)
            for k in (0, 1, 2, 3):
                cps[k][0].start()

    def mid(self, inp, out, send, recv, loc):
        if self.kind == "relay":
            cps = self._relay_copies(inp, out, send, recv)
            for k, onward in ((1, (3, 5)), (2, (4, 6))):
                cps[k][1].wait_recv()
                for q in onward:
                    cps[q][0].start()
            return
        if self.kind == "chip":
            cps = self._chip_copies(inp, out, send, recv)
            for k in (2, 3):
                cps[k][1].wait_recv()
                cps[k + 2][0].start()
            return
        if self.kind != "gather":
            return
        me, sibling, chips = self._places()
        c = me[2]
        for j, chip in enumerate(chips):
            landed = out.at[4 * chip[0] + 2 * chip[1] + c]
            _remote(landed, landed, send, recv, 1 + j, me).wait_recv()
            _remote(landed, landed, send, recv, 4 + j, sibling).start()

    def finish(self, inp, out, send, recv, loc):
        me, sibling, chips = self._places()
        x, y, c = me
        if self.kind == "relay":
            cps = self._relay_copies(inp, out, send, recv)
            for k, onward in ((3, 7), (4, 8)):
                cps[k][1].wait_recv()
                cps[onward][0].start()
            for k in (0, 5, 6, 7, 8):
                cps[k][1].wait_recv()
            for k in range(9):
                cps[k][0].wait_send()
            pltpu.make_async_copy(inp, out.at[4 * x + 2 * y + c], loc.at[0]).wait()
        elif self.kind == "gather":
            blk = lambda px, py, pc: out.at[4 * px + 2 * py + pc]
            mine = blk(*me)
            _remote(inp, blk(*sibling), send, recv, 0, me).wait_recv()
            for j, chip in enumerate(chips):
                _remote(inp, blk(*chip, 1 - c), send, recv, 4 + j, me).wait_recv()
            for k in range(7):
                _remote(inp, mine, send, recv, k, sibling).wait_send()
            pltpu.make_async_copy(inp, mine, loc.at[0]).wait()
        elif self.kind == "pair":
            for k in range(4):
                _remote(inp.at[2 * k + (1 - c)], out.at[k], send, recv, k, sibling).wait()
        else:
            kme = 2 * x + y
            cps = self._chip_copies(inp, out, send, recv)
            for k in (0, 1, 4, 5):
                cps[k][1].wait_recv()
            for k in range(6):
                cps[k][0].wait_send()
            pltpu.make_async_copy(inp.at[kme], out.at[kme], loc.at[0]).wait()


def _job_scratch(jobs):
    sem = pltpu.SemaphoreType.DMA
    return [s for _ in jobs for s in (sem((_Job.N_SEM,)), sem((_Job.N_SEM,)), sem((1,)))]


def _run_jobs(jobs, method, jins, jouts, jsems, only=None):
    for q, job in enumerate(jobs):
        if only is None or only[q]:
            getattr(job, method)(jins[q], jouts[q], *jsems[3 * q:3 * q + 3])


def _exchange(jobs, *, name):
    n = len(jobs)

    def body(*refs):
        jins, jouts, jsems = refs[:n], refs[n:2 * n], refs[2 * n:]
        _run_jobs(jobs, "start", jins, jouts, jsems)
        _run_jobs(jobs, "mid", jins, jouts, jsems)
        _run_jobs(jobs, "finish", jins, jouts, jsems)

    return _pcall(body, in_specs=[ANY_SPEC] * n, out_specs=[ANY_SPEC] * n, out_shape=[j.out for j in jobs],
                  scratch_shapes=_job_scratch(jobs), name=name)(*[j.inp for j in jobs])


def _hosted(body, jobs, *, grid, in_specs, out_specs, out_shape, args, name, scratch_shapes=(), aliases=None):
    in_specs, out_specs, out_shape = list(in_specs), list(out_specs), list(out_shape)
    scratch_shapes = list(scratch_shapes)
    n_in, n_out, n_scr, nj = len(in_specs), len(out_specs), len(scratch_shapes), len(jobs)
    sem = ("arbitrary",) * len(grid)
    kw = dict(input_output_aliases=aliases) if aliases else {}
    if not jobs:
        res = _pcall(body, grid=grid, in_specs=in_specs, out_specs=out_specs, out_shape=out_shape,
                     scratch_shapes=scratch_shapes, name=name, compiler_params=_cparams(sem), **kw)(*args)
        return list(res), []

    def full(*refs):
        ins, jins = refs[:n_in], refs[n_in:n_in + nj]
        o0 = n_in + nj
        outs, jouts = refs[o0:o0 + n_out], refs[o0 + n_out:o0 + n_out + nj]
        s0 = o0 + n_out + nj
        scr, jsems = refs[s0:s0 + n_scr], refs[s0 + n_scr:]
        step = pl.program_id(0)
        for ax in range(1, len(grid)):
            step = step * grid[ax] + pl.program_id(ax)
        total = math.prod(grid)
        early = [job.kind in ("relay", "chip") for job in jobs]
        mid_step = (3 * total) // 5
        split = any(early) and 0 < mid_step < total - 1

        @pl.when(step == 0)
        def _():
            _run_jobs(jobs, "start", jins, jouts, jsems)

        if split:
            @pl.when(step == mid_step)
            def _():
                _run_jobs(jobs, "mid", jins, jouts, jsems, only=early)

        body(*ins, *outs, *scr)

        @pl.when(step == total - 1)
        def _():
            _run_jobs(jobs, "mid", jins, jouts, jsems, only=[not e for e in early] if split else None)
            _run_jobs(jobs, "finish", jins, jouts, jsems)

    res = _pcall(full, grid=grid, in_specs=in_specs + [ANY_SPEC] * nj, out_specs=out_specs + [ANY_SPEC] * nj,
                 out_shape=out_shape + [j.out for j in jobs], scratch_shapes=scratch_shapes + _job_scratch(jobs),
                 name=name, compiler_params=_cparams(sem), **kw)(*args, *[j.inp for j in jobs])
    return list(res[:n_out]), list(res[n_out:])


def _pair_add(g8, r4, cidx, *, name):
    _, r, c = g8.shape
    tr = ROW_TILE if r % ROW_TILE == 0 else r

    def body(c_ref, g_ref, r_ref, o_ref):
        o_ref[...] = (g_ref[...].astype(F32) + r_ref[...].astype(F32)).astype(BF16)

    return _pcall(
        body,
        grid_spec=pltpu.PrefetchScalarGridSpec(
            num_scalar_prefetch=1, grid=(4, r // tr),
            in_specs=[pl.BlockSpec((None, tr, c), lambda k, i, cr: (2 * k + cr[0], i, 0)),
                      pl.BlockSpec((None, tr, c), lambda k, i, cr: (k, i, 0))],
            out_specs=pl.BlockSpec((None, tr, c), lambda k, i, cr: (k, i, 0))),
        out_shape=jax.ShapeDtypeStruct((4, r, c), BF16), name=name,
        compiler_params=_cparams(("parallel", "parallel")))(cidx, g8, r4)


def _adam_update(g, w_ref, m_ref, v_ref, g_ref, d_ref, mo_ref, vo_ref):
    c1 = 1.0 - ADAM_B1 ** ADAM_STEP
    c2 = 1.0 - ADAM_B2 ** ADAM_STEP
    m2 = ADAM_B1 * m_ref[...] + (1.0 - ADAM_B1) * g
    v2 = ADAM_B2 * v_ref[...] + (1.0 - ADAM_B2) * (g * g)
    g_ref[...] = g
    mo_ref[...] = m2
    vo_ref[...] = v2
    d_ref[...] = -ADAM_LR * ((m2 / c1) / (jnp.sqrt(v2 / c2) + ADAM_EPS) + ADAM_WD * w_ref[...])


def _adamw_rows(srcs, items, own_cols, me1, *, name):
    ns, ni, no = len(srcs), len(items), len(own_cols)
    full = lambda a: pl.BlockSpec(a.shape, lambda i, me: (0,) * a.ndim)
    in_specs = [full(a) for a in srcs]
    args = list(srcs)
    for (si, _r0, w, _m, _v) in own_cols:
        a = srcs[si]
        in_specs.append(pl.BlockSpec((N_DEV, a.shape[1], w.shape[1]), lambda i, me: (0, 0, me[0])))
        args.append(a)
    out_specs, out_shape = [], []
    for (_si, _r0, w, m, v) in list(items) + list(own_cols):
        in_specs += [full(w)] * 3
        args += [w, m, v]
        out_specs += [full(w)] * 4
        out_shape += [jax.ShapeDtypeStruct(w.shape, F32)] * 4

    def body(me_ref, *refs):
        src_refs, own_refs = refs[:ns], refs[ns:ns + no]
        wmv = refs[ns + no:ns + no + 3 * (ni + no)]
        outs = refs[ns + no + 3 * (ni + no):]
        for q, (si, r0, w, _m, _v) in enumerate(list(items) + list(own_cols)):
            nr, cw = w.shape
            gref = src_refs[si] if q < ni else own_refs[q - ni]
            g = gref[0, r0:r0 + nr, 0:cw]
            for d in range(1, N_DEV):
                g = g + gref[d, r0:r0 + nr, 0:cw]
            _adam_update(g, *wmv[3 * q:3 * q + 3], *outs[4 * q:4 * q + 4])

    res = _pcall(
        body,
        grid_spec=pltpu.PrefetchScalarGridSpec(num_scalar_prefetch=1, grid=(1,), in_specs=in_specs, out_specs=out_specs),
        out_shape=out_shape, name=name, compiler_params=_cparams(("arbitrary",)))(me1, *args)
    return [tuple(res[4 * q:4 * q + 4]) for q in range(ni + no)]


def _adamw(gsrc, w, m, v, *, name, nsum=None):
    _, r, c = gsrc.shape
    k = gsrc.shape[0] if nsum is None else nsum
    tr = ROW_TILE if r % ROW_TILE == 0 else r

    def body(gs_ref, w_ref, m_ref, v_ref, g_ref, d_ref, mo_ref, vo_ref):
        g = gs_ref[0].astype(F32)
        for q in range(1, k):
            g = g + gs_ref[q].astype(F32)
        _adam_update(g, w_ref, m_ref, v_ref, g_ref, d_ref, mo_ref, vo_ref)

    tc = c
    if tr == r and r > ROW_TILE and c % 256 == 0:
        tc = 256
    blk = pl.BlockSpec((tr, tc), lambda i, j: (i, j))
    sd = jax.ShapeDtypeStruct((r, c), F32)
    return _pcall(body, grid=(r // tr, c // tc),
                  in_specs=[pl.BlockSpec((k, tr, tc), lambda i, j: (0, i, j)), blk, blk, blk],
                  out_specs=(blk, blk, blk, blk), out_shape=(sd, sd, sd, sd), name=name,
                  compiler_params=_cparams(("parallel", "parallel")))(gsrc, w, m, v)


WEIGHTS = ['w_in', 'lru_conv_w', 'lru_conv_b', 'lru_gate_a_w', 'lru_gate_a_b', 'lru_gate_x_w', 'lru_gate_x_b',
           'lru_a_param', 'ssd_conv_w', 'ssd_conv_b', 'ssd_dt_bias', 'ssd_a_log', 'ssd_d', 'ssd_norm_w', 'w_out',
           'ln1_g', 'ln1_b', 'w_ff1', 'w_ff2', 'ln2_g', 'ln2_b', 'w_ple_gate', 'w_ple', 'ln3_g', 'ln3_b']
BIG = ['w_in', 'w_out', 'w_ff1', 'w_ff2', 'w_ple_gate', 'w_ple']
COL_SHARDED = ('w_ff1', 'w_ple')
CONV = ['lru_conv_w', 'ssd_conv_w']
REPL = [n for n in WEIGHTS if n not in BIG and n not in CONV]
CONV_CH = {'lru_conv_w': LRU_W, 'ssd_conv_w': XBC}


def _to_dest_major(name, gfull):
    if name in COL_SHARDED:
        r, cfull = gfull.shape
        return gfull.reshape(r, N_DEV, cfull // N_DEV).transpose(1, 0, 2)
    rfull, cdim = gfull.shape
    return gfull.reshape(N_DEV, rfull // N_DEV, cdim)


def _full_weight(name, gathered):
    if name in COL_SHARDED:
        _, r, cs = gathered.shape
        full = gathered.transpose(1, 0, 2).reshape(r, N_DEV * cs)
    else:
        _, rs, cdim = gathered.shape
        full = gathered.reshape(N_DEV * rs, cdim)
    if name == 'w_in':
        full = lax.dynamic_update_slice(jnp.zeros((D_IN_PAD, D_MODEL), full.dtype), full, (0, 0))
    return full


SMALL_SRC = ("lru", "ssd", "heads", "rows", "gate_a", "gate_x")
AG_HOSTS = {"in_proj": ("w_ff1",), "lru_fwd": ("w_ff2",), "ssd_conv_fwd": ("w_ple_gate", "w_ple"), "ssd_fwd": ("w_out",)}
PAIR_HOSTS = ("d_x2", "d_pre", "d_x1", "d_ycat")
CHIP_HOSTS = {"lru_bwd": ("w_ple_gate", "w_ple", "w_ff2"), "ssd_bwd": ("w_ff1",), "ssd_conv_bwd": ("w_out",),
              "d_x": ("w_in",)}
SMALL_HOSTS = {"ssd_bwd": ("lru", "gate_a", "gate_x"), "d_w_in_3": ("ssd", "heads", "rows")}


class _Schedule:
    def __init__(self, shards, cidx):
        self.shards, self.cidx = shards, cidx
        self.pair, self.chip, self.small_jobs = [], [], []
        self.dest, self.summed, self.gathered_small = {}, {}, {}
        self.tags = []

    def ride(self, host):
        tags = []
        if host in AG_HOSTS:
            tags = [("weight", n, self.shards[n]) for n in AG_HOSTS[host]]
        elif host in PAIR_HOSTS or host in CHIP_HOSTS or host == "flush":
            tags = [("pair", n, a) for n, a in self.pair]
            self.pair = []
            if host not in PAIR_HOSTS:
                take = [t for t in self.chip if host == "flush" or t[0] in CHIP_HOSTS[host]]
                tags += [("chip", n, a) for n, a in take]
                self.chip = [t for t in self.chip if not any(t is u for u in take)]
        if host in SMALL_HOSTS:
            tags += [("small", n, a) for n, a in self.small_jobs if n in SMALL_HOSTS[host]]
            self.small_jobs = [t for t in self.small_jobs if t[0] not in SMALL_HOSTS[host]]
        self.tags = tags
        return [_Job({"weight": "relay", "small": "gather"}.get(kind, kind), a) for kind, _n, a in tags]

    def done(self, jobs, outs, w):
        for (kind, n, _a), o in zip(self.tags, outs):
            if kind == "weight":
                w[n] = _full_weight(n, o)
            elif kind == "small":
                self.gathered_small[n] = o
            elif kind == "pair":
                self.chip.append((n, _pair_add(self.dest[n], o, self.cidx, name="rs_pair_add_" + n)))
            else:
                self.summed[n] = o

    def grad(self, name, val):
        self.dest[name] = val if val.ndim == 3 else _to_dest_major(name, val)
        self.pair.append((name, self.dest[name]))

    def small(self, raw):
        self.small_jobs += list(raw.items())

    def pairs_now(self):
        tags = [("pair", n, a) for n, a in self.pair]
        self.pair, self.tags = [], tags
        jobs = [_Job("pair", a) for _k, _n, a in tags]
        self.done(jobs, _exchange(jobs, name="rs_pairs_now"), None)

    def flush(self):
        step = 0
        while self.pair or self.chip:
            jobs = self.ride("flush")
            self.done(jobs, _exchange(jobs, name="rs_flush_%d" % step), None)
            step += 1


def kernel(x, p, w_in, lru_conv_w, lru_conv_b, lru_gate_a_w, lru_gate_a_b, lru_gate_x_w, lru_gate_x_b, lru_a_param, ssd_conv_w, ssd_conv_b, ssd_dt_bias, ssd_a_log, ssd_d, ssd_norm_w, w_out, ln1_g, ln1_b, w_ff1, w_ff2, ln2_g, ln2_b, w_ple_gate, w_ple, ln3_g, ln3_b, loss_target, m_w_in, m_lru_conv_w, m_lru_conv_b, m_lru_gate_a_w, m_lru_gate_a_b, m_lru_gate_x_w, m_lru_gate_x_b, m_lru_a_param, m_ssd_conv_w, m_ssd_conv_b, m_ssd_dt_bias, m_ssd_a_log, m_ssd_d, m_ssd_norm_w, m_w_out, m_ln1_g, m_ln1_b, m_w_ff1, m_w_ff2, m_ln2_g, m_ln2_b, m_w_ple_gate, m_w_ple, m_ln3_g, m_ln3_b, v_w_in, v_lru_conv_w, v_lru_conv_b, v_lru_gate_a_w, v_lru_gate_a_b, v_lru_gate_x_w, v_lru_gate_x_b, v_lru_a_param, v_ssd_conv_w, v_ssd_conv_b, v_ssd_dt_bias, v_ssd_a_log, v_ssd_d, v_ssd_norm_w, v_w_out, v_ln1_g, v_ln1_b, v_w_ff1, v_w_ff2, v_ln2_g, v_ln2_b, v_w_ple_gate, v_w_ple, v_ln3_g, v_ln3_b):
    given = dict(locals())
    def local(a, n):
        return jnp.swapaxes(a[0], 0, 1) if n == 'w_in' else a[0]

    wsh = {n: local(given[n], n) for n in WEIGHTS}
    msh = {n: local(given["m_" + n], n) for n in WEIGHTS}
    vsh = {n: local(given["v_" + n], n) for n in WEIGHTS}
    xi, yi, ci = _mesh_pos()
    me = 4 * xi + 2 * yi + ci

    shards = {n: wsh[n].astype(BF16) for n in BIG}
    conv_pack = jnp.concatenate([_pad_rows8(wsh[n]) for n in CONV], axis=1)
    g_in, gconv = _exchange([_Job("relay", shards['w_in']), _Job("gather", conv_pack)], name="ag_first")
    full = {'w_in_t': _full_weight('w_in', g_in)}
    c0 = 0
    for n in CONV:
        cw = CONV_CH[n] // N_DEV
        full[n] = gconv[:, :4, c0:c0 + cw].transpose(1, 0, 2).reshape(4, CONV_CH[n])
        c0 += cw
    for n in REPL:
        full[n] = given[n] if given[n].ndim == 2 else wsh[n]

    sched = _Schedule(shards, jnp.reshape(ci, (1,)).astype(jnp.int32))
    loss_local, grad_x, g, raw = _local_step(x[0], p[0, 0], loss_target[0], full, sched)
    sched.flush()
    summed, gat = sched.summed, sched.gathered_small
    loss = gat["rows"][0, 7, 0]
    for d in range(1, N_DEV):
        loss = loss + gat["rows"][d, 7, 0]

    outs = {}
    for n in BIG:
        outs[n] = _adamw(summed[n], wsh[n], msh[n], vsh[n], name="adamw_" + n, nsum=4)
    for n, k in (("lru_gate_a_w", "gate_a"), ("lru_gate_x_w", "gate_x")):
        flat = lambda a: a.reshape(N_HEAD * HEAD_P, HEAD_P)
        res = _adamw(gat[k], flat(wsh[n]), flat(msh[n]), flat(vsh[n]), name="adamw_" + n)
        outs[n] = tuple(r.reshape(N_HEAD, HEAD_P, HEAD_P) for r in res)
    row_items = [("lru_conv_b", 0, 4), ("lru_gate_a_b", 0, 5), ("lru_gate_x_b", 0, 6), ("lru_a_param", 0, 7),
                 ("ssd_conv_b", 1, 4), ("ssd_dt_bias", 2, 0), ("ssd_a_log", 2, 1), ("ssd_d", 2, 2),
                 ("ssd_norm_w", 3, 0), ("ln1_g", 3, 1), ("ln1_b", 3, 2), ("ln2_g", 3, 3), ("ln2_b", 3, 4),
                 ("ln3_g", 3, 5), ("ln3_b", 3, 6)]
    vec = lambda a: a.reshape(1, -1)
    items = [(si, r0, vec(given[n]), vec(given["m_" + n]), vec(given["v_" + n])) for n, si, r0 in row_items]
    own = [(si, 0, wsh[n], msh[n], vsh[n]) for n, si in (("lru_conv_w", 0), ("ssd_conv_w", 1))]
    me1 = jnp.reshape(me, (1,)).astype(jnp.int32)
    res = _adamw_rows([gat[k] for k in SMALL_SRC[:4]], items, own, me1, name="adamw_small")
    for (n, _si, _r0), r4 in zip(row_items, res[:len(row_items)]):
        outs[n] = r4
    for n, r4 in zip(CONV, res[len(row_items):]):
        outs[n] = r4

    def fin(n, k):
        a = jnp.swapaxes(outs[n][k], 0, 1) if n == 'w_in' else outs[n][k]
        return a.reshape(given[n].shape)

    return (loss, grad_x[None],
            *[fin(n, 0) for n in WEIGHTS], *[fin(n, 1) for n in WEIGHTS],
            *[fin(n, 2) for n in WEIGHTS], *[fin(n, 3) for n in WEIGHTS])
```

```python
import math

import jax
import jax.numpy as jnp
from jax import lax
from jax.experimental import pallas as pl
from jax.experimental.pallas import tpu as pltpu

F32 = jnp.float32
BF16 = jnp.bfloat16
HI = lax.Precision.HIGHEST

N_DEV = 8
D_MODEL = 1024
LRU_W = 1024
SSD_W = 1024
XBC = 2048
N_HEAD = 16
HEAD_P = 64
N_GROUP = 4
GROUP_W = 256
N_STATE = 128
CHUNK = 128
D_FF = 4096
PLE_DIM = 256
D_IN = 5136
D_IN_PAD = 5632
COL_G = 1024
COL_Z = 2048
COL_XBC = 3072
COL_DT = 5120
LRU_C = 8.0
ALPHA = 2.0 ** 0.25
LN_EPS = 1e-5
RMS_EPS = 1e-5
ADAM_LR = 0.001
ADAM_B1 = 0.9
ADAM_B2 = 0.999
ADAM_EPS = 1e-08
ADAM_WD = 0.01
ADAM_STEP = 10
GELU_C = math.sqrt(2.0 / math.pi)
LANE = 128
SUBLANE = 8
VMEM_LIMIT = 48 * 1024 * 1024
MESH_T = pl.DeviceIdType.MESH
NEG_BIG = -1e30


def _pcall(body, **kw):
    return pl.pallas_call(body, **kw)


def _cparams(sem):
    return pltpu.CompilerParams(dimension_semantics=sem, vmem_limit_bytes=VMEM_LIMIT)


def _dot(a, b):
    return jnp.dot(a.astype(BF16), b.astype(BF16), preferred_element_type=F32)


def _dot_nt(a, b):
    return lax.dot_general(a.astype(BF16), b.astype(BF16), (((1,), (1,)), ((), ())), preferred_element_type=F32)


def _dot_tn(a, b):
    return lax.dot_general(a.astype(BF16), b.astype(BF16), (((0,), (0,)), ((), ())), preferred_element_type=F32)


def _dotx(a, b):
    return jnp.dot(a, b, precision=HI, preferred_element_type=F32)


def _sigmoid(x):
    return jax.nn.sigmoid(x)


def _softplus(v):
    return jnp.maximum(v, 0.0) + jnp.log1p(jnp.exp(-jnp.abs(v)))


def _gelu(x):
    th = jnp.tanh(GELU_C * (x + 0.044715 * x * x * x))
    return 0.5 * x * (1.0 + th), th


def _gelu_grad(x, th):
    return 0.5 * (1.0 + th) + 0.5 * x * (1.0 - th * th) * GELU_C * (1.0 + 3.0 * 0.044715 * x * x)


def _iota(shape, dim):
    return lax.broadcasted_iota(jnp.int32, shape, dim)


def _mm(a, b, mode, *, tm, tn, name, a_fn=None, extra=None, epi=None, out_dtype=F32, dest_major=False, into=None,
        jobs=()):
    m = a.shape[1] if mode == "tn" else a.shape[0]
    n = b.shape[0] if mode == "nt" else b.shape[1]
    tm, tn = min(tm, m), min(tn, n)
    if dest_major:
        tn = n // N_DEV
    if mode == "nn":
        m, k = a.shape
        _, n = b.shape
        a_spec = pl.BlockSpec((tm, k), lambda i, j: (i, 0))
        b_spec = pl.BlockSpec((k, tn), lambda i, j: (0, j))
        dims = ((1,), (0,))
    elif mode == "nt":
        m, k = a.shape
        n, _ = b.shape
        a_spec = pl.BlockSpec((tm, k), lambda i, j: (i, 0))
        b_spec = pl.BlockSpec((tn, k), lambda i, j: (j, 0))
        dims = ((1,), (1,))
    else:
        k, m = a.shape
        _, n = b.shape
        a_spec = pl.BlockSpec((k, tm), lambda i, j: (0, i))
        b_spec = pl.BlockSpec((k, tn), lambda i, j: (0, j))
        dims = ((0,), (0,))
    assert m % tm == 0 and n % tn == 0, (name, m, n, tm, tn)
    o_spec = pl.BlockSpec((tm, tn), lambda i, j: (i, j))
    in_specs = [a_spec, b_spec]
    args = [a, b]
    if extra is not None:
        in_specs.append(o_spec)
        args.append(extra)

    def body(*refs):
        a_ref, b_ref, o_ref = refs[0], refs[1], refs[-1]
        av = a_ref[...]
        if a_fn is not None:
            av = a_fn(av)
        acc = lax.dot_general(av.astype(BF16), b_ref[...].astype(BF16), (dims, ((), ())), preferred_element_type=F32)
        if epi is not None:
            acc = epi(acc, refs[2][...])
        o_ref[...] = acc.astype(out_dtype)

    out_shape = jax.ShapeDtypeStruct((m, n), out_dtype)
    aliases = None
    if dest_major:
        assert extra is None
        o_spec = pl.BlockSpec((None, tm, tn), lambda i, j: (j, i, 0))
        out_shape = jax.ShapeDtypeStruct((N_DEV, m, tn), out_dtype)
    if into is not None:
        buf, row0, total = into
        assert extra is None and row0 % tm == 0
        o_spec = pl.BlockSpec((tm, tn), lambda i, j: (row0 // tm + i, j))
        out_shape = jax.ShapeDtypeStruct((total, n), out_dtype)
        if buf is not None:
            in_specs.append(ANY_SPEC)
            args.append(buf)
            aliases = {len(args) - 1: 0}
    (out,), jouts = _hosted(body, jobs, grid=(m // tm, n // tn), in_specs=in_specs, out_specs=[o_spec],
                            out_shape=[out_shape], args=args, name=name, aliases=aliases)
    return (out, jouts) if jobs else out


def _mm_pieces(pieces, offsets, b, *, tm, name, extra, epi, jobs=()):
    m = pieces[0].shape[0]
    kb, n = b.shape
    tm = min(tm, m)
    row = lambda wdt: pl.BlockSpec((tm, wdt), lambda i: (i, 0))
    in_specs = [row(pc.shape[1]) for pc in pieces] + [pl.BlockSpec((kb, n), lambda i: (0, 0)), row(n)]
    np_ = len(pieces)

    def body(*refs):
        b_ref, e_ref, o_ref = refs[np_], refs[np_ + 1], refs[np_ + 2]
        acc = jnp.zeros((tm, n), F32)
        for q in range(np_):
            kq = pieces[q].shape[1]
            acc = acc + jnp.dot(refs[q][...].astype(BF16), b_ref[offsets[q]:offsets[q] + kq, :].astype(BF16),
                                preferred_element_type=F32)
        o_ref[...] = epi(acc, e_ref[...])

    (out,), jouts = _hosted(body, jobs, grid=(m // tm,), in_specs=in_specs, out_specs=[row(n)],
                            out_shape=[jax.ShapeDtypeStruct((m, n), F32)], args=list(pieces) + [b, extra], name=name)
    return (out, jouts) if jobs else out


def _relu2(v):
    r = jnp.maximum(v, 0.0)
    return r * r


ROW_TILE = 256


def _ln_stats(t):
    mu = jnp.mean(t, axis=-1, keepdims=True)
    xc = t - mu
    var = jnp.mean(xc * xc, axis=-1, keepdims=True)
    rstd = lax.rsqrt(var + LN_EPS)
    return xc * rstd, rstd


def _ln_bwd_rows(dy, xhat, rstd, g):
    dxh = dy * g
    m1 = jnp.mean(dxh, axis=-1, keepdims=True)
    m2 = jnp.mean(dxh * xhat, axis=-1, keepdims=True)
    return rstd * (dxh - m1 - xhat * m2)


def _mm_ln(a, b, res, g, beta, *, tm, name, a_fn=None):
    m, k = a.shape
    d = b.shape[1]
    tm = min(tm, m)
    row = pl.BlockSpec((tm, d), lambda i: (i, 0))
    par = pl.BlockSpec((1, d), lambda i: (0, 0))

    def body(a_ref, b_ref, r_ref, g_ref, be_ref, br_ref, y_ref, yb_ref):
        av = a_ref[...]
        if a_fn is not None:
            av = a_fn(av)
        acc = jnp.dot(av.astype(BF16), b_ref[...].astype(BF16), preferred_element_type=F32)
        br_ref[...] = acc
        xhat, _ = _ln_stats(ALPHA * r_ref[...] + acc)
        y = xhat * g_ref[...] + be_ref[...]
        y_ref[...] = y
        yb_ref[...] = y.astype(BF16)

    sd = jax.ShapeDtypeStruct((m, d), F32)
    return _pcall(body, grid=(m // tm,),
                  in_specs=[pl.BlockSpec((tm, k), lambda i: (i, 0)), pl.BlockSpec((k, d), lambda i: (0, 0)), row, par, par],
                  out_specs=(row, row, row), out_shape=(sd, sd, jax.ShapeDtypeStruct((m, d), BF16)), name=name,
                  compiler_params=_cparams(("parallel",)))(a, b, res, g, beta)


def _mm_ln_bwd(a, b, res, branch, g, dy0, coef0, *, tm, name, jobs=()):
    m, k = a.shape
    d = b.shape[0]
    tm = min(tm, m)
    row = pl.BlockSpec((tm, d), lambda i: (i, 0))
    par = pl.BlockSpec((1, d), lambda i: (0, 0))

    def body(a_ref, b_ref, r_ref, br_ref, g_ref, dy0_ref, dt_ref, dtb_ref, dg_ref, db_ref):
        acc = lax.dot_general(a_ref[...].astype(BF16), b_ref[...].astype(BF16), (((1,), (1,)), ((), ())),
                              preferred_element_type=F32)
        dy = coef0 * dy0_ref[...] + acc
        xhat, rstd = _ln_stats(ALPHA * r_ref[...] + br_ref[...])
        dt = _ln_bwd_rows(dy, xhat, rstd, g_ref[...])
        dt_ref[...] = dt
        dtb_ref[...] = dt.astype(BF16)

        @pl.when(pl.program_id(0) == 0)
        def _():
            dg_ref[...] = jnp.zeros_like(dg_ref)
            db_ref[...] = jnp.zeros_like(db_ref)

        dg_ref[...] += jnp.sum(dy * xhat, axis=0, keepdims=True)
        db_ref[...] += jnp.sum(dy, axis=0, keepdims=True)

    pd = jax.ShapeDtypeStruct((1, d), F32)
    outs, jouts = _hosted(
        body, jobs, grid=(m // tm,),
        in_specs=[pl.BlockSpec((tm, k), lambda i: (i, 0)), pl.BlockSpec((d, k), lambda i: (0, 0)), row, row, par, row],
        out_specs=(row, row, par, par),
        out_shape=(jax.ShapeDtypeStruct((m, d), F32), jax.ShapeDtypeStruct((m, d), BF16), pd, pd),
        args=(a, b, res, branch, g, dy0), name=name)
    return (tuple(outs), jouts) if jobs else tuple(outs)


def _head(x2, x2b, p, wg, wp, g, beta, tgt, *, name):
    s, d = x2.shape
    tile = 2 * ROW_TILE
    row = pl.BlockSpec((tile, d), lambda i: (i, 0))
    par = pl.BlockSpec((1, d), lambda i: (0, 0))
    lsp = pl.BlockSpec((1, LANE), lambda i: (0, 0))
    whole = lambda a: pl.BlockSpec(a.shape, lambda i: (0, 0))

    def body(x2_ref, x2b_ref, p_ref, wg_ref, wp_ref, g_ref, be_ref, t_ref,
             loss_ref, dgp_ref, dple_ref, dt_ref, dg_ref, db_ref):
        gate = _sigmoid(_dot(x2b_ref[...], wg_ref[...]))
        ple_v = _dot(p_ref[...], wp_ref[...])
        xhat, rstd = _ln_stats(ALPHA * x2_ref[...] + gate * ple_v)
        err = xhat * g_ref[...] + be_ref[...] - t_ref[...]
        dy = err * (1.0 / d)
        dt = _ln_bwd_rows(dy, xhat, rstd, g_ref[...])
        dt_ref[...] = dt
        dgp_ref[...] = (dt * ple_v * gate * (1.0 - gate)).astype(BF16)
        dple_ref[...] = (dt * gate).astype(BF16)

        @pl.when(pl.program_id(0) == 0)
        def _():
            loss_ref[...] = jnp.zeros_like(loss_ref)
            dg_ref[...] = jnp.zeros_like(dg_ref)
            db_ref[...] = jnp.zeros_like(db_ref)

        loss_ref[...] += 0.5 * jnp.sum(jnp.mean(err * err, axis=-1, keepdims=True))
        dg_ref[...] += jnp.sum(dy * xhat, axis=0, keepdims=True)
        db_ref[...] += jnp.sum(dy, axis=0, keepdims=True)

    sd = jax.ShapeDtypeStruct((s, d), F32)
    sb = jax.ShapeDtypeStruct((s, d), BF16)
    pd = jax.ShapeDtypeStruct((1, d), F32)
    return _pcall(body, grid=(s // tile,),
                  in_specs=[row, row, pl.BlockSpec((tile, p.shape[1]), lambda i: (i, 0)), whole(wg), whole(wp), par, par,
                            row],
                  out_specs=(lsp, row, row, row, par, par),
                  out_shape=(jax.ShapeDtypeStruct((1, LANE), F32), sb, sb, sd, pd, pd),
                  name=name, compiler_params=_cparams(("arbitrary",)))(x2, x2b, p, wg, wp, g, beta, tgt)


CONV_R = 256
PAD = SUBLANE


def _shift_down(ext, s):
    if s == 0:
        return ext[PAD:, :]
    return pltpu.roll(ext, s, 0)[PAD:, :]


def _shift_up(ext, s):
    r = ext.shape[0] - PAD
    if s == 0:
        return ext[:r, :]
    return pltpu.roll(ext, r + PAD - s, 0)[:r, :]


def _conv_rows(xpad_ref, r0, w_ref):
    ext = xpad_ref[pl.ds(r0, CONV_R + PAD), :]
    acc = _shift_down(ext, 0) * w_ref[3:4, :]
    for k in range(3):
        acc = acc + _shift_down(ext, 3 - k) * w_ref[k:k + 1, :]
    return acc, ext


def _fill_front_padded(dst_ref, src_ref, s):
    dst_ref[0:PAD, :] = jnp.zeros((PAD, dst_ref.shape[1]), F32)

    def cp(q, _):
        r0 = pl.multiple_of(q * CONV_R, CONV_R)
        dst_ref[pl.ds(pl.multiple_of(PAD + r0, PAD), CONV_R), :] = src_ref[pl.ds(r0, CONV_R), :]
        return 0

    lax.fori_loop(0, s // CONV_R, cp, 0)


def _conv_silu_fwd(proj, w8, b, *, col0, width, ct, name, jobs=()):
    s = proj.shape[0]
    nb = col0 // ct

    def body(x_ref, w_ref, b_ref, o_ref, xpad):
        _fill_front_padded(xpad, x_ref, s)

        def step(q, _):
            r0 = pl.multiple_of(q * CONV_R, CONV_R)
            acc, _e = _conv_rows(xpad, r0, w_ref)
            pre = acc + b_ref[...]
            o_ref[pl.ds(r0, CONV_R), :] = pre * _sigmoid(pre)
            return 0

        lax.fori_loop(0, s // CONV_R, step, 0)

    (out,), jouts = _hosted(
        body, jobs, grid=(width // ct,),
        in_specs=[pl.BlockSpec((s, ct), lambda j: (0, nb + j)), pl.BlockSpec((SUBLANE, ct), lambda j: (0, j)),
                  pl.BlockSpec((1, ct), lambda j: (0, j))],
        out_specs=[pl.BlockSpec((s, ct), lambda j: (0, j))],
        out_shape=[jax.ShapeDtypeStruct((s, width), F32)],
        scratch_shapes=[pltpu.VMEM((s + PAD, ct), F32)], name=name, args=(proj, w8, b))
    return (out, jouts) if jobs else out


def _conv_bwd_rows(dpad_ref, r0, w_ref):
    return _conv_bwd_ext(dpad_ref[pl.ds(r0, CONV_R + PAD), :], w_ref)


def _conv_bwd_ext(ext, w_ref):
    acc = _shift_up(ext, 0) * w_ref[3:4, :]
    for k in range(3):
        acc = acc + _shift_up(ext, 3 - k) * w_ref[k:k + 1, :]
    return acc


def _conv_silu_bwd(proj, dact, w8, b, *, col0, width, ct, name, jobs=()):
    s = proj.shape[0]
    nb = col0 // ct

    def body(x_ref, d_ref, w_ref, b_ref, dx_ref, dwb_ref, xpad, dpad):
        _fill_front_padded(xpad, x_ref, s)
        dpad[pl.ds(s, PAD), :] = jnp.zeros((PAD, ct), F32)
        dwb_ref[...] = jnp.zeros_like(dwb_ref)

        def step(q, _):
            r0 = pl.multiple_of(q * CONV_R, CONV_R)
            acc, ext = _conv_rows(xpad, r0, w_ref)
            pre = acc + b_ref[...]
            sg = _sigmoid(pre)
            dpre = d_ref[pl.ds(r0, CONV_R), :] * sg * (1.0 + pre * (1.0 - sg))
            dpad[pl.ds(r0, CONV_R), :] = dpre
            for k in range(4):
                dwb_ref[k:k + 1, :] += jnp.sum(dpre * _shift_down(ext, 3 - k), axis=0, keepdims=True)
            dwb_ref[4:5, :] += jnp.sum(dpre, axis=0, keepdims=True)
            return 0

        lax.fori_loop(0, s // CONV_R, step, 0)

        def step2(q, _):
            r0 = pl.multiple_of(q * CONV_R, CONV_R)
            dx_ref[pl.ds(r0, CONV_R), :] = _conv_bwd_rows(dpad, r0, w_ref).astype(BF16)
            return 0

        lax.fori_loop(0, s // CONV_R, step2, 0)

    colb = pl.BlockSpec((s, ct), lambda j: (0, j))
    outs, jouts = _hosted(
        body, jobs, grid=(width // ct,),
        in_specs=[pl.BlockSpec((s, ct), lambda j: (0, nb + j)), colb, pl.BlockSpec((SUBLANE, ct), lambda j: (0, j)),
                  pl.BlockSpec((1, ct), lambda j: (0, j))],
        out_specs=(colb, pl.BlockSpec((SUBLANE, ct), lambda j: (0, j))),
        out_shape=(jax.ShapeDtypeStruct((s, width), BF16), jax.ShapeDtypeStruct((SUBLANE, width), F32)),
        scratch_shapes=[pltpu.VMEM((s + PAD, ct), F32), pltpu.VMEM((s + PAD, ct), F32)], name=name,
        args=(proj, dact, w8, b))
    return (tuple(outs), jouts) if jobs else tuple(outs)


LRU_CT = 128


def _row_of(v, r):
    return jnp.sum(jnp.where(_iota((v.shape[0], 1), 0) == r, v, 0.0), axis=0, keepdims=True)


def _scan_fwd(a, u):
    r = a.shape[0]
    row = _iota((r, 1), 0)
    d = 1
    while d < r:
        valid = row >= d
        u = jnp.where(valid, a * pltpu.roll(u, d, 0) + u, u)
        a = jnp.where(valid, a * pltpu.roll(a, d, 0), a)
        d *= 2
    return a, u


def _scan_rev(b, u):
    r = b.shape[0]
    row = _iota((r, 1), 0)
    d = 1
    while d < r:
        valid = row < r - d
        u = jnp.where(valid, b * pltpu.roll(u, r - d, 0) + u, u)
        b = jnp.where(valid, b * pltpu.roll(b, r - d, 0), b)
        d *= 2
    return b, u


def _lru_chunk(xpad, r0, cw_ref, cb, wa, ba, wx, bx, sp):
    acc, ext = _conv_rows(xpad, r0, cw_ref)
    xl = acc + cb
    r = _sigmoid(_dot(xl, wa) + ba)
    i = _sigmoid(_dot(xl, wx) + bx)
    la = -LRU_C * r * sp
    a = jnp.exp(la)
    a2 = jnp.exp(2.0 * la)
    mult = jnp.sqrt(-jnp.tanh(la) * (a2 + 1.0))
    first = (r0 + _iota((CONV_R, 1), 0)) == 0
    mult = jnp.where(first, 1.0, mult)
    return ext, xl, r, i, a, a2, mult, first


def _lru_specs(s):
    ct = LRU_CT
    nb_g = COL_G // ct
    return dict(
        x=pl.BlockSpec((s, ct), lambda j: (0, j)),
        g=pl.BlockSpec((s, ct), lambda j: (0, nb_g + j)),
        col=pl.BlockSpec((s, ct), lambda j: (0, j)),
        cw=pl.BlockSpec((SUBLANE, ct), lambda j: (0, j)),
        vec=pl.BlockSpec((1, ct), lambda j: (0, j)),
        gate=pl.BlockSpec((None, ct, ct), lambda j: (j, 0, 0)),
    )


def _lru_fwd(proj, cw8, cb, wa_bd, ba, wx_bd, bx, ap, *, name, jobs=()):
    s = proj.shape[0]
    ct = LRU_CT
    sp_ = _lru_specs(s)

    def body(x_ref, g_ref, cw_ref, cb_ref, wa_ref, ba_ref, wx_ref, bx_ref, ap_ref, y_ref, h_ref, xpad):
        _fill_front_padded(xpad, x_ref, s)
        sp = _softplus(-ap_ref[...])

        def step(q, carry):
            r0 = pl.multiple_of(q * CONV_R, CONV_R)
            _e, xl, _r, i, a, _a2, mult, _f = _lru_chunk(xpad, r0, cw_ref, cb_ref[...], wa_ref[...], ba_ref[...],
                                                       wx_ref[...], bx_ref[...], sp)
            acum, ucum = _scan_fwd(a, xl * i * mult)
            h = acum * carry + ucum
            h_ref[pl.ds(r0, CONV_R), :] = h
            ge, _th = _gelu(g_ref[pl.ds(r0, CONV_R), :])
            y_ref[pl.ds(r0, CONV_R), :] = (ge * h).astype(BF16)
            return _row_of(h, CONV_R - 1)

        lax.fori_loop(0, s // CONV_R, step, jnp.zeros((1, ct), F32))

    (ymix, hs), jouts = _hosted(
        body, jobs, grid=(LRU_W // ct,),
        in_specs=[sp_["x"], sp_["g"], sp_["cw"], sp_["vec"], sp_["gate"], sp_["vec"], sp_["gate"], sp_["vec"], sp_["vec"]],
        out_specs=(sp_["col"], sp_["col"]),
        out_shape=(jax.ShapeDtypeStruct((s, LRU_W + SSD_W), BF16), jax.ShapeDtypeStruct((s, LRU_W), F32)),
        scratch_shapes=[pltpu.VMEM((s + PAD, ct), F32)],
        name=name, args=(proj, proj, cw8, cb, wa_bd, ba, wx_bd, bx, ap))
    return ((ymix, hs), jouts) if jobs else (ymix, hs)


def _lru_bwd(proj, dy, hs, cw8, cb, wa_bd, ba, wx_bd, bx, ap, *, name, jobs=()):
    s = proj.shape[0]
    ct = LRU_CT
    sp_ = _lru_specs(s)

    nq = s // CONV_R

    def body(x_ref, g_ref, dy_ref, h_ref, cw_ref, cb_ref, wa_ref, ba_ref, wx_ref, bx_ref, ap_ref,
             dx_ref, dg_ref, dcwb_ref, dwa_ref, dwx_ref, xpad, hpad):
        _fill_front_padded(xpad, x_ref, s)
        _fill_front_padded(hpad, h_ref, s)
        apv = ap_ref[...]
        sp = _softplus(-apv)
        cb_v, wa, ba_v, wx, bx_v = cb_ref[...], wa_ref[...], ba_ref[...], wx_ref[...], bx_ref[...]
        dcwb_ref[...] = jnp.zeros_like(dcwb_ref)
        dwa_ref[...] = jnp.zeros_like(dwa_ref)
        dwx_ref[...] = jnp.zeros_like(dwx_ref)

        def back(k, carry):
            g_next, a_next, dxl_next = carry
            last_row = _iota((CONV_R, 1), 0) == CONV_R - 1
            r0 = pl.multiple_of((nq - 1 - k) * CONV_R, CONV_R)
            ext, xl, r, i, a, a2, mult, first = _lru_chunk(xpad, r0, cw_ref, cb_v, wa, ba_v, wx, bx_v, sp)
            gv = g_ref[pl.ds(r0, CONV_R), :]
            dyv = dy_ref[pl.ds(r0, CONV_R), :]
            hext = hpad[pl.ds(r0, CONV_R + PAD), :]
            ge, th = _gelu(gv)
            dg_ref[pl.ds(r0, CONV_R), :] = (dyv * _shift_down(hext, 0) * _gelu_grad(gv, th)).astype(BF16)
            b = jnp.where(last_row, a_next, pltpu.roll(a, CONV_R - 1, 0))
            bcum, dcum = _scan_rev(b, dyv * ge)
            gval = dcum + bcum * g_next
            hprev = _shift_down(hext, 1)
            da = gval * hprev
            dxl = gval * i * mult
            di = gval * xl * mult
            dmult = jnp.where(first, 0.0, gval * xl * i)
            dla = da * a - dmult * a2 / mult
            dr = dla * (-LRU_C) * sp
            dcwb_ref[7:8, :] += jnp.sum(dla * (-LRU_C) * r, axis=0, keepdims=True)
            dpr = dr * r * (1.0 - r)
            dpi = di * i * (1.0 - i)
            dxl = dxl + _dot_nt(dpr, wa) + _dot_nt(dpi, wx)
            dwa_ref[...] += _dot_tn(xl, dpr)
            dwx_ref[...] += _dot_tn(xl, dpi)
            dcwb_ref[5:6, :] += jnp.sum(dpr, axis=0, keepdims=True)
            dcwb_ref[6:7, :] += jnp.sum(dpi, axis=0, keepdims=True)
            for tap in range(4):
                dcwb_ref[tap:tap + 1, :] += jnp.sum(dxl * _shift_down(ext, 3 - tap), axis=0, keepdims=True)
            dcwb_ref[4:5, :] += jnp.sum(dxl, axis=0, keepdims=True)
            dx_ref[pl.ds(r0, CONV_R), :] = _conv_bwd_ext(jnp.concatenate([dxl, dxl_next], axis=0), cw_ref).astype(BF16)
            return _row_of(gval, 0), _row_of(a, 0), dxl[:PAD, :]

        zero = jnp.zeros((1, ct), F32)
        lax.fori_loop(0, nq, back, (zero, zero, jnp.zeros((PAD, ct), F32)))
        dcwb_ref[7:8, :] = dcwb_ref[7:8, :] * (-_sigmoid(-apv))

    nt = LRU_W // ct
    outs, jouts = _hosted(
        body, jobs, grid=(nt,),
        in_specs=[sp_["x"], sp_["g"], sp_["col"], sp_["col"], sp_["cw"], sp_["vec"], sp_["gate"], sp_["vec"], sp_["gate"],
                  sp_["vec"], sp_["vec"]],
        out_specs=(sp_["col"], sp_["col"], sp_["cw"], sp_["gate"], sp_["gate"]),
        out_shape=(jax.ShapeDtypeStruct((s, LRU_W), BF16), jax.ShapeDtypeStruct((s, LRU_W), BF16),
                   jax.ShapeDtypeStruct((SUBLANE, LRU_W), F32), jax.ShapeDtypeStruct((nt, ct, ct), F32),
                   jax.ShapeDtypeStruct((nt, ct, ct), F32)),
        scratch_shapes=[pltpu.VMEM((s + PAD, ct), F32), pltpu.VMEM((s + PAD, ct), F32)],
        name=name, args=(proj, proj, dy, hs, cw8, cb, wa_bd, ba, wx_bd, bx, ap))
    return (tuple(outs), jouts) if jobs else tuple(outs)


def _split3(v):
    hi = v.astype(BF16)
    r1 = v - hi.astype(F32)
    mid = r1.astype(BF16)
    lo = (r1 - mid.astype(F32)).astype(BF16)
    return hi, mid, lo


def _dot01(m01, v):
    mb = m01.astype(BF16)
    hi, mid, lo = _split3(v)
    f = lambda part: jnp.dot(mb, part, preferred_element_type=F32)
    return f(hi) + f(mid) + f(lo)


def _dot01_r(v, m01, parts=3):
    mb = m01.astype(BF16)
    acc = None
    for part in _split3(v)[:parts]:
        t = jnp.dot(part, mb, preferred_element_type=F32)
        acc = t if acc is None else acc + t
    return acc


def _ssd_prep(dtr, bias, alog_pad):
    l = CHUNK
    lane = _iota((1, LANE), 1)
    a_head = jnp.where(lane < N_HEAD, -jnp.exp(alog_pad), 0.0)
    dt = _softplus(dtr + bias)
    tril = (_iota((l, l), 1) <= _iota((l, l), 0)).astype(F32)
    a = dt * a_head
    cs = _dot01(tril, a)
    tot = jnp.sum(a, axis=0, keepdims=True)
    return dict(a_head=a_head, dt=dt, tril=tril, cs=cs, tot=tot)


def _col(v, h):
    lane = _iota(v.shape, 1)
    return jnp.sum(jnp.where(lane == h, v, 0.0), axis=1, keepdims=True)


def _decay_mat(cs, cst_ref, h, causal):
    row = cst_ref[h:h + 1, :]
    return jnp.exp(jnp.where(causal, _col(cs, h) - row, NEG_BIG))


def _head_mask(j, rows=CHUNK):
    lane = _iota((rows, GROUP_W), 1)
    return (lane >= j * HEAD_P) & (lane < (j + 1) * HEAD_P)


def _over_heads(v, g):
    r = v.shape[0]
    out = jnp.zeros((r, GROUP_W), F32)
    for j in range(4):
        out = jnp.where(_head_mask(j, r), _col(v, 4 * g + j), out)
    return out


def _ssd_group_fwd(q, g, xs_g, bg, cg, ht_g, cst_ref, causal, dx_g):
    dtx_g, csx_g, totx_g = _over_heads(q["dt"], g), _over_heads(q["cs"], g), _over_heads(q["tot"], g)
    xdt = xs_g * dtx_g
    ex = jnp.exp(csx_g)
    cb = _dot_nt(cg, bg)
    yoff = _dot(cg, ht_g) * ex
    ydiag = jnp.zeros((CHUNK, GROUP_W), F32)
    lms = []
    for j in range(4):
        lms.append(_decay_mat(q["cs"], cst_ref, 4 * g + j, causal))
        ydiag = jnp.where(_head_mask(j), _dot(cb * lms[j], xdt), ydiag)
    y = ydiag + yoff + xs_g * dx_g
    dsx = jnp.exp(totx_g - csx_g)
    return y, dict(xdt=xdt, ex=ex, cb=cb, yoff=yoff, dsx=dsx, dtx=dtx_g, totx=totx_g, lms=lms)


def _gated_norm_fwd(y_g, z_g, w_g):
    sz = _sigmoid(z_g)
    silu = z_g * sz
    yf = y_g * silu
    rs = lax.rsqrt(jnp.mean(yf * yf, axis=1, keepdims=True) + RMS_EPS)
    yn = yf * rs
    return yn * w_g, (sz, silu, rs, yn)


def _ssd_fwd(xact, proj, ymix, bias_pad, alog_pad, dxp, normw, *, name, jobs=()):
    s = xact.shape[0]
    nc = s // CHUNK

    def body(xa_ref, dt_ref, z_ref, _ymix_ref, bias_ref, alp_ref, dx_ref, nw_ref, y_ref, hp_ref, ht, cst):
        @pl.when(pl.program_id(0) == 0)
        def _():
            ht[...] = jnp.zeros_like(ht)

        hp_ref[...] = ht[...]
        q = _ssd_prep(dt_ref[...], bias_ref[...], alp_ref[...])
        cst[...] = q["cs"].T
        causal = q["tril"] > 0.0
        for g in range(N_GROUP):
            sl = slice(g * GROUP_W, (g + 1) * GROUP_W)
            xs_g = xa_ref[:, sl]
            bg = xa_ref[:, SSD_W + g * N_STATE:SSD_W + (g + 1) * N_STATE]
            cg = xa_ref[:, SSD_W + N_GROUP * N_STATE + g * N_STATE:SSD_W + N_GROUP * N_STATE + (g + 1) * N_STATE]
            ht_g = ht[:, sl]
            y, f = _ssd_group_fwd(q, g, xs_g, bg, cg, ht_g, cst, causal, dx_ref[:, sl])
            out, _ = _gated_norm_fwd(y, z_ref[:, sl], nw_ref[:, sl])
            y_ref[:, sl] = out.astype(BF16)
            ht[:, sl] = jnp.exp(f["totx"]) * ht_g + _dot_tn(bg, f["xdt"] * f["dsx"])

    par = lambda w: pl.BlockSpec((1, w), lambda c: (0, 0))
    (ycat, hprev), jouts = _hosted(
        body, jobs, grid=(nc,),
        in_specs=[pl.BlockSpec((CHUNK, XBC), lambda c: (c, 0)),
                  pl.BlockSpec((CHUNK, LANE), lambda c: (c, COL_DT // LANE)),
                  pl.BlockSpec((CHUNK, SSD_W), lambda c: (c, COL_Z // SSD_W)),
                  ANY_SPEC, par(LANE), par(LANE), par(SSD_W), par(SSD_W)],
        out_specs=(pl.BlockSpec((CHUNK, SSD_W), lambda c: (c, LRU_W // SSD_W)),
                   pl.BlockSpec((None, N_STATE, SSD_W), lambda c: (c, 0, 0))),
        out_shape=(jax.ShapeDtypeStruct(ymix.shape, ymix.dtype), jax.ShapeDtypeStruct((nc, N_STATE, SSD_W), F32)),
        scratch_shapes=[pltpu.VMEM((N_STATE, SSD_W), F32), pltpu.VMEM((CHUNK, LANE), F32)],
        aliases={3: 0}, name=name, args=(xact, proj, proj, ymix, bias_pad, alog_pad, dxp, normw))
    return ((ycat, hprev), jouts) if jobs else (ycat, hprev)


def _ssd_bwd(xact, proj, dycat, hprev, bias_pad, alog_pad, dxp, normw, *, name, jobs=()):
    s = xact.shape[0]
    nc = s // CHUNK
    l = CHUNK

    def body(xa_ref, dt_ref, z_ref, dy_ref, hp_ref, bias_ref, alp_ref, dx_ref, nw_ref,
             dxa_ref, ddt_ref, dz_ref, dnw_ref, small_ref, dht, cst, accx, dcsx_s, ddtx_s):
        step = pl.program_id(0)

        @pl.when(step == 0)
        def _():
            dht[...] = jnp.zeros_like(dht)
            accx[...] = jnp.zeros_like(accx)
            dnw_ref[...] = jnp.zeros_like(dnw_ref)
            small_ref[...] = jnp.zeros_like(small_ref)

        dtr = dt_ref[...]
        q = _ssd_prep(dtr, bias_ref[...], alp_ref[...])
        cst[...] = q["cs"].T
        causal = q["tril"] > 0.0
        lane = _iota((l, LANE), 1)
        head_row = _iota((LANE, l), 0)
        dcs_head = jnp.zeros((l, LANE), F32)
        dcs_rows = jnp.zeros((LANE, l), F32)
        for g in range(N_GROUP):
            sl = slice(g * GROUP_W, (g + 1) * GROUP_W)
            slb = slice(SSD_W + g * N_STATE, SSD_W + (g + 1) * N_STATE)
            slc = slice(SSD_W + N_GROUP * N_STATE + g * N_STATE, SSD_W + N_GROUP * N_STATE + (g + 1) * N_STATE)
            xs_g, bg, cg = xa_ref[:, sl], xa_ref[:, slb], xa_ref[:, slc]
            ht_g = hp_ref[:, sl]
            dxp_g = dx_ref[:, sl]
            y, f = _ssd_group_fwd(q, g, xs_g, bg, cg, ht_g, cst, causal, dxp_g)
            z_g, nw_g = z_ref[:, sl], nw_ref[:, sl]
            _o, (sz, silu, rs, yn) = _gated_norm_fwd(y, z_g, nw_g)
            dout = dy_ref[:, sl]
            dnw_ref[:, sl] += jnp.sum(dout * yn, axis=0, keepdims=True)
            dyn = dout * nw_g
            dyf = rs * (dyn - yn * jnp.mean(dyn * yn, axis=1, keepdims=True))
            dy = dyf * silu
            dz_ref[:, sl] = (dyf * y * sz * (1.0 + z_g * (1.0 - sz))).astype(BF16)
            accx[0:1, sl] += jnp.sum(dy * xs_g, axis=0, keepdims=True)
            dyo = dy * f["ex"]
            dcg = _dot_nt(dyo, ht_g)
            dht_prev = _dot_tn(cg, dyo)
            dcsx = dy * f["yoff"]
            xdt = f["xdt"]
            dxdt = jnp.zeros((l, GROUP_W), F32)
            dcb = jnp.zeros((l, l), F32)
            for j in range(4):
                h = 4 * g + j
                lm = f["lms"][j]
                sc = f["cb"] * lm
                mask = _head_mask(j)
                ds_ = jnp.where(causal, _dot_nt(jnp.where(mask, dy, 0.0), xdt), 0.0)
                dxdt = jnp.where(mask, _dot_tn(sc, dy), dxdt)
                dcb = dcb + ds_ * lm
                m = ds_ * sc
                dcs_head = dcs_head + jnp.where(lane == h, jnp.sum(m, axis=1, keepdims=True), 0.0)
                dcs_rows = dcs_rows + jnp.where(head_row == h, jnp.sum(m, axis=0, keepdims=True), 0.0)
            dhn = dht[:, sl]
            etot = jnp.exp(f["totx"])
            dxd = _dot(bg, dhn)
            dbg = _dot_nt(xdt * f["dsx"], dhn)
            dxdt = dxdt + dxd * f["dsx"]
            qq = dxd * xdt * f["dsx"]
            dcsx = dcsx - qq
            dtot = jnp.sum(qq, axis=0, keepdims=True) + jnp.sum(dhn * ht_g, axis=0, keepdims=True) * etot
            dht[:, sl] = etot * dhn + dht_prev
            dcg = dcg + _dot(dcb, bg)
            dbg = dbg + _dot_tn(dcb, cg)
            dxa_ref[:, sl] = dxdt * f["dtx"] + dy * dxp_g
            dxa_ref[:, slb] = dbg
            dxa_ref[:, slc] = dcg
            dcsx_s[:, sl] = dcsx
            ddtx_s[:, sl] = dxdt * xs_g
            accx[2:3, sl] = dtot
        reduce = (jnp.right_shift(_iota((SSD_W, LANE), 0), 6) == _iota((SSD_W, LANE), 1)).astype(F32)
        triu = (_iota((l, l), 1) >= _iota((l, l), 0)).astype(F32)
        dtot = _dot01_r(accx[...], reduce)[2:3, :]
        dcs_head = dcs_head - dcs_rows.T
        da_head = _dot01(triu, dcs_head + _dot01_r(dcsx_s[...], reduce, parts=2)) + dtot
        ddt = _dot01_r(ddtx_s[...], reduce, parts=2) + da_head * q["a_head"]
        small_ref[1:2, :] += jnp.sum(da_head * q["dt"], axis=0, keepdims=True)
        ddtr = ddt * _sigmoid(dtr + bias_ref[...])
        ddt_ref[...] = ddtr.astype(BF16)
        small_ref[0:1, :] += jnp.sum(ddtr, axis=0, keepdims=True)

        @pl.when(step == nc - 1)
        def _():
            small_ref[1:2, :] = small_ref[1:2, :] * q["a_head"]
            small_ref[2:3, :] = _dot01_r(accx[...], reduce)[0:1, :]

    rev = lambda c: nc - 1 - c
    par = lambda w: pl.BlockSpec((1, w), lambda c: (0, 0))
    outs, jouts = _hosted(
        body, jobs, grid=(nc,),
        in_specs=[pl.BlockSpec((CHUNK, XBC), lambda c: (rev(c), 0)),
                  pl.BlockSpec((CHUNK, LANE), lambda c: (rev(c), COL_DT // LANE)),
                  pl.BlockSpec((CHUNK, SSD_W), lambda c: (rev(c), COL_Z // SSD_W)),
                  pl.BlockSpec((CHUNK, SSD_W), lambda c: (rev(c), 1)),
                  pl.BlockSpec((None, N_STATE, SSD_W), lambda c: (rev(c), 0, 0)),
                  par(LANE), par(LANE), par(SSD_W), par(SSD_W)],
        out_specs=(pl.BlockSpec((CHUNK, XBC), lambda c: (rev(c), 0)),
                   pl.BlockSpec((CHUNK, LANE), lambda c: (rev(c), 0)),
                   pl.BlockSpec((CHUNK, SSD_W), lambda c: (rev(c), 0)),
                   par(SSD_W), pl.BlockSpec((SUBLANE, LANE), lambda c: (0, 0))),
        out_shape=(jax.ShapeDtypeStruct((s, XBC), F32), jax.ShapeDtypeStruct((s, LANE), BF16),
                   jax.ShapeDtypeStruct((s, SSD_W), BF16), jax.ShapeDtypeStruct((1, SSD_W), F32),
                   jax.ShapeDtypeStruct((SUBLANE, LANE), F32)),
        scratch_shapes=[pltpu.VMEM((N_STATE, SSD_W), F32), pltpu.VMEM((CHUNK, LANE), F32),
                        pltpu.VMEM((SUBLANE, SSD_W), F32), pltpu.VMEM((CHUNK, SSD_W), F32),
                        pltpu.VMEM((CHUNK, SSD_W), F32)],
        name=name, args=(xact, proj, proj, dycat, hprev, bias_pad, alog_pad, dxp, normw))
    return (tuple(outs), jouts) if jobs else tuple(outs)


def _blockdiag(w):
    w2 = w.reshape(N_HEAD // 2, 2, HEAD_P, HEAD_P)
    z = jnp.zeros((N_HEAD // 2, HEAD_P, HEAD_P), w.dtype)
    top = jnp.concatenate([w2[:, 0], z], axis=2)
    bot = jnp.concatenate([z, w2[:, 1]], axis=2)
    return jnp.concatenate([top, bot], axis=1)


def _unblockdiag(wbd):
    a = wbd[:, :HEAD_P, :HEAD_P]
    b = wbd[:, HEAD_P:, HEAD_P:]
    return jnp.stack([a, b], axis=1).reshape(N_HEAD, HEAD_P, HEAD_P)


def _pad_rows8(w):
    return jnp.concatenate([w, jnp.zeros((SUBLANE - w.shape[0], w.shape[1]), w.dtype)], axis=0)


def _pad_lane(v):
    return jnp.concatenate([v, jnp.zeros((1, LANE - v.shape[1]), v.dtype)], axis=1)


class _NoExchange:
    def ride(self, host):
        return []

    def done(self, jobs, outs, w):
        pass

    def grad(self, name, val):
        pass

    def small(self, raw):
        pass

    def pairs_now(self):
        pass


def _local_step(x, p, tgt, w, hooks=_NoExchange()):
    cw_l = _pad_rows8(w["lru_conv_w"])
    cw_s = _pad_rows8(w["ssd_conv_w"])
    wa_bd = _blockdiag(w["lru_gate_a_w"])
    wx_bd = _blockdiag(w["lru_gate_x_w"])
    ba = w["lru_gate_a_b"].reshape(1, LRU_W)
    bx = w["lru_gate_x_b"].reshape(1, LRU_W)
    bias_pad = _pad_lane(w["ssd_dt_bias"])
    alog_pad = _pad_lane(w["ssd_a_log"])
    dxp = jnp.repeat(w["ssd_d"], HEAD_P, axis=1)

    def host(fn, *a, name, **k):
        jobs = hooks.ride(name)
        res = fn(*a, name=name, jobs=jobs, **k)
        if jobs:
            res, jouts = res
            hooks.done(jobs, jouts, w)
        return res

    def grad(n, val):
        g[n] = val
        hooks.grad(n, val)

    xb = x.astype(BF16)
    proj = host(_mm, xb, w["w_in_t"], "nt", tm=2048, tn=512, name="in_proj")
    ymix, h_lru = host(_lru_fwd, proj, cw_l, w["lru_conv_b"], wa_bd, ba, wx_bd, bx, w["lru_a_param"], name="lru_fwd")
    xact = host(_conv_silu_fwd, proj, cw_s, w["ssd_conv_b"], col0=COL_XBC, width=XBC, ct=256, name="ssd_conv_fwd")
    ycat, hprev = host(_ssd_fwd, xact, proj, ymix, bias_pad, alog_pad, dxp, w["ssd_norm_w"], name="ssd_fwd")
    mix, x1, x1b = _mm_ln(ycat, w["w_out"], x, w["ln1_g"], w["ln1_b"], tm=512, name="out_proj")
    pre = _mm(x1b, w["w_ff1"], "nn", tm=2048, tn=512, out_dtype=BF16, name="ff1")
    ff, x2, x2b = _mm_ln(pre, w["w_ff2"], x1, w["ln2_g"], w["ln2_b"], tm=512, a_fn=_relu2, name="ff2")
    loss, dgpre, dple, dt3, dg3, db3 = _head(x2, x2b, p, w["w_ple_gate"], w["w_ple"], w["ln3_g"], w["ln3_b"], tgt,
                                             name="head")

    g = {}
    g["ln3_g"], g["ln3_b"] = dg3, db3
    grad("w_ple_gate", _mm(x2b, dgpre, "tn", tm=512, tn=1024, out_dtype=BF16, name="d_w_ple_gate"))
    grad("w_ple", _mm(p, dple, "tn", tm=256, tn=512, dest_major=True, out_dtype=BF16, name="d_w_ple"))
    dt2, dt2b, g["ln2_g"], g["ln2_b"] = host(_mm_ln_bwd, dgpre, w["w_ple_gate"], x1, ff, w["ln2_g"], dt3, ALPHA,
                                             tm=512, name="d_x2")
    grad("w_ff2", host(_mm, pre, dt2b, "tn", tm=512, tn=1024, a_fn=_relu2, out_dtype=BF16, name="d_w_ff2"))
    dpre = host(_mm, dt2b, w["w_ff2"], "nt", tm=2048, tn=512, extra=pre, out_dtype=BF16,
                epi=lambda acc, pv: acc * 2.0 * jnp.maximum(pv.astype(F32), 0.0), name="d_pre")
    grad("w_ff1", host(_mm, x1b, dpre, "tn", tm=1024, tn=512, dest_major=True, out_dtype=BF16, name="d_w_ff1"))
    dt1, dt1b, g["ln1_g"], g["ln1_b"] = host(_mm_ln_bwd, dpre, w["w_ff1"], x, mix, w["ln1_g"], dt2, ALPHA,
                                             tm=256, name="d_x1")
    grad("w_out", host(_mm, ycat, dt1b, "tn", tm=512, tn=1024, out_dtype=BF16, name="d_w_out"))
    dycat = host(_mm, dt1b, w["w_out"], "nt", tm=2048, tn=512, name="d_ycat")
    dxl, dgl, dcwb_l, dwa, dwx = host(_lru_bwd, proj, dycat, h_lru, cw_l, w["lru_conv_b"], wa_bd, ba, wx_bd, bx,
                                      w["lru_a_param"], name="lru_bwd")
    g["lru_gate_a_w"] = _unblockdiag(dwa)
    g["lru_gate_x_w"] = _unblockdiag(dwx)
    raw = dict(lru=dcwb_l, gate_a=g["lru_gate_a_w"].reshape(N_HEAD * HEAD_P, HEAD_P).astype(BF16),
               gate_x=g["lru_gate_x_w"].reshape(N_HEAD * HEAD_P, HEAD_P).astype(BF16))
    hooks.small(raw)
    dxact, ddt, dz, g["ssd_norm_w"], small = host(_ssd_bwd, xact, proj, dycat, hprev, bias_pad, alog_pad, dxp,
                                                   w["ssd_norm_w"], name="ssd_bwd")
    dxbc, dcwb_s = host(_conv_silu_bwd, proj, dxact, cw_s, w["ssd_conv_b"], col0=COL_XBC, width=XBC, ct=256,
                        name="ssd_conv_bwd")
    pieces, offsets = [dxl, dgl, dz, dxbc, ddt], [0, COL_G, COL_Z, COL_XBC, COL_DT]

    g["lru_conv_w"] = dcwb_l[0:4]
    g["lru_conv_b"] = dcwb_l[4:5]
    g["lru_gate_a_b"] = dcwb_l[5:6]
    g["lru_gate_x_b"] = dcwb_l[6:7]
    g["lru_a_param"] = dcwb_l[7:8]
    g["ssd_conv_w"] = dcwb_s[0:4]
    g["ssd_conv_b"] = dcwb_s[4:5]
    g["ssd_dt_bias"] = small[0:1, :N_HEAD]
    g["ssd_a_log"] = small[1:2, :N_HEAD]
    g["ssd_d"] = small[2:3, :N_HEAD]
    rows = jnp.concatenate([g[n] for n in ("ssd_norm_w", "ln1_g", "ln1_b", "ln2_g", "ln2_b", "ln3_g", "ln3_b")]
                           + [jnp.broadcast_to(loss[:, 0:1], (1, D_MODEL))], axis=0)
    late = dict(ssd=dcwb_s, heads=small, rows=rows)
    hooks.small(late)
    raw.update(late)
    dwt = None
    for q, (pc, off) in enumerate(zip(pieces, offsets)):
        dwt = host(_mm, pc, xb, "tn", tm=512, tn=1024, out_dtype=BF16, into=(dwt, off, D_IN),
                   name="d_w_in_%d" % q)
    grad("w_in", dwt)
    hooks.pairs_now()
    grad_x = host(_mm_pieces, pieces, offsets, w["w_in_t"], tm=256, extra=dt1, epi=lambda acc, e: acc + ALPHA * e,
                  name="d_x")
    return loss[0, 0], grad_x, g, raw


ANY_SPEC = pl.BlockSpec(memory_space=pl.ANY)


def _mesh_pos():
    return lax.axis_index("x"), lax.axis_index("y"), lax.axis_index("c")


def _remote(src, dst, send, recv, k, to):
    return pltpu.make_async_remote_copy(src_ref=src, dst_ref=dst, send_sem=send.at[k], recv_sem=recv.at[k],
                                        device_id=to, device_id_type=MESH_T)


class _Job:
    N_SEM = 9

    def __init__(self, kind, inp):
        self.kind, self.inp = kind, inp
        shape = {"gather": (N_DEV,) + inp.shape, "relay": (N_DEV,) + inp.shape, "pair": (4,) + inp.shape[1:],
                 "chip": inp.shape}[kind]
        self.out = jax.ShapeDtypeStruct(shape, inp.dtype)
        self.top = (inp.shape[0] // 2) // 16 * 16

    def _blk(self, ref, k):
        return ref.at[k]

    def _relay_copies(self, inp, out, send, recv):
        x, y, c = _mesh_pos()
        sib, xn, yn, dg = (x, y, 1 - c), (1 - x, y, c), (x, 1 - y, c), (1 - x, 1 - y, c)
        blk = lambda p, cc=None: out.at[4 * p[0] + 2 * p[1] + (p[2] if cc is None else cc)]
        top = lambda r: r.at[pl.ds(0, self.top)]
        bot = lambda r: r.at[pl.ds(self.top, self.inp.shape[0] - self.top)]
        mine = blk((x, y, c))
        plan = [
            (inp, mine, sib, blk(sib)),
            (inp, mine, xn, blk(xn)),
            (inp, mine, yn, blk(yn)),
            (top(blk(xn)), top(blk(xn)), yn, top(blk(dg))),
            (bot(blk(yn)), bot(blk(yn)), xn, bot(blk(dg))),
            (blk(xn), blk(xn), sib, blk(xn, 1 - c)),
            (blk(yn), blk(yn), sib, blk(yn, 1 - c)),
            (top(blk(dg)), top(blk(dg)), sib, top(blk(dg, 1 - c))),
            (bot(blk(dg)), bot(blk(dg)), sib, bot(blk(dg, 1 - c))),
        ]
        me = (x, y, c)
        return [(_remote(s, d, send, recv, k, to), _remote(s, land, send, recv, k, me))
                for k, (s, d, to, land) in enumerate(plan)]

    def _places(self):
        x, y, c = _mesh_pos()
        return (x, y, c), (x, y, 1 - c), [(1 - x, y), (x, 1 - y), (1 - x, 1 - y)]

    def start(self, inp, out, send, recv, loc):
        me, sibling, chips = self._places()
        x, y, c = me
        if self.kind == "relay":
            pltpu.make_async_copy(inp, out.at[4 * x + 2 * y + c], loc.at[0]).start()
            cps = self._relay_copies(inp, out, send, recv)
            for k in (0, 1, 2):
                cps[k][0].start()
        elif self.kind == "gather":
            mine = out.at[4 * x + 2 * y + c]
            pltpu.make_async_copy(inp, mine, loc.at[0]).start()
            _remote(inp, mine, send, recv, 0, sibling).start()
            for j, chip in enumerate(chips):
                _remote(inp, mine, send, recv, 1 + j, (*chip, c)).start()
        elif self.kind == "pair":
            for k in range(4):
                _remote(inp.at[2 * k + (1 - c)], out.at[k], send, recv, k, sibling).start()
        else:
            kme = 2 * x + y
            pltpu.make_async_copy(self._blk(inp, kme), self._blk(out, kme), loc.at[0]).start()
            for j, (tx, ty) in enumerate(chips):
                _remote(self._blk(inp, 2 * tx + ty), self._blk(out, kme), send, recv, j, (tx, ty, c)).start()

    def mid(self, inp, out, send, recv, loc):
        if self.kind == "relay":
            cps = self._relay_copies(inp, out, send, recv)
            for k, onward in ((1, (3, 5)), (2, (4, 6))):
                cps[k][1].wait_recv()
                for q in onward:
                    cps[q][0].start()
            return
        if self.kind != "gather":
            return
        me, sibling, chips = self._places()
        c = me[2]
        for j, chip in enumerate(chips):
            landed = out.at[4 * chip[0] + 2 * chip[1] + c]
            _remote(landed, landed, send, recv, 1 + j, me).wait_recv()
            _remote(landed, landed, send, recv, 4 + j, sibling).start()

    def finish(self, inp, out, send, recv, loc):
        me, sibling, chips = self._places()
        x, y, c = me
        if self.kind == "relay":
            cps = self._relay_copies(inp, out, send, recv)
            for k, onward in ((3, 7), (4, 8)):
                cps[k][1].wait_recv()
                cps[onward][0].start()
            for k in (0, 5, 6, 7, 8):
                cps[k][1].wait_recv()
            for k in range(9):
                cps[k][0].wait_send()
            pltpu.make_async_copy(inp, out.at[4 * x + 2 * y + c], loc.at[0]).wait()
        elif self.kind == "gather":
            blk = lambda px, py, pc: out.at[4 * px + 2 * py + pc]
            mine = blk(*me)
            _remote(inp, blk(*sibling), send, recv, 0, me).wait_recv()
            for j, chip in enumerate(chips):
                _remote(inp, blk(*chip, 1 - c), send, recv, 4 + j, me).wait_recv()
            for k in range(7):
                _remote(inp, mine, send, recv, k, sibling).wait_send()
            pltpu.make_async_copy(inp, mine, loc.at[0]).wait()
        elif self.kind == "pair":
            for k in range(4):
                _remote(inp.at[2 * k + (1 - c)], out.at[k], send, recv, k, sibling).wait()
        else:
            kme = 2 * x + y
            for j, (tx, ty) in enumerate(chips):
                _remote(self._blk(inp, kme), self._blk(out, 2 * tx + ty), send, recv, j, (tx, ty, c)).wait_recv()
            for j, (tx, ty) in enumerate(chips):
                _remote(self._blk(inp, 2 * tx + ty), self._blk(out, kme), send, recv, j, (tx, ty, c)).wait_send()
            pltpu.make_async_copy(self._blk(inp, kme), self._blk(out, kme), loc.at[0]).wait()


def _job_scratch(jobs):
    sem = pltpu.SemaphoreType.DMA
    return [s for _ in jobs for s in (sem((_Job.N_SEM,)), sem((_Job.N_SEM,)), sem((1,)))]


def _run_jobs(jobs, method, jins, jouts, jsems, only=None):
    for q, job in enumerate(jobs):
        if only is None or only[q]:
            getattr(job, method)(jins[q], jouts[q], *jsems[3 * q:3 * q + 3])


def _exchange(jobs, *, name):
    n = len(jobs)

    def body(*refs):
        jins, jouts, jsems = refs[:n], refs[n:2 * n], refs[2 * n:]
        _run_jobs(jobs, "start", jins, jouts, jsems)
        _run_jobs(jobs, "mid", jins, jouts, jsems)
        _run_jobs(jobs, "finish", jins, jouts, jsems)

    return _pcall(body, in_specs=[ANY_SPEC] * n, out_specs=[ANY_SPEC] * n, out_shape=[j.out for j in jobs],
                  scratch_shapes=_job_scratch(jobs), name=name)(*[j.inp for j in jobs])


def _hosted(body, jobs, *, grid, in_specs, out_specs, out_shape, args, name, scratch_shapes=(), aliases=None):
    in_specs, out_specs, out_shape = list(in_specs), list(out_specs), list(out_shape)
    scratch_shapes = list(scratch_shapes)
    n_in, n_out, n_scr, nj = len(in_specs), len(out_specs), len(scratch_shapes), len(jobs)
    sem = ("arbitrary",) * len(grid)
    kw = dict(input_output_aliases=aliases) if aliases else {}
    if not jobs:
        res = _pcall(body, grid=grid, in_specs=in_specs, out_specs=out_specs, out_shape=out_shape,
                     scratch_shapes=scratch_shapes, name=name, compiler_params=_cparams(sem), **kw)(*args)
        return list(res), []

    def full(*refs):
        ins, jins = refs[:n_in], refs[n_in:n_in + nj]
        o0 = n_in + nj
        outs, jouts = refs[o0:o0 + n_out], refs[o0 + n_out:o0 + n_out + nj]
        s0 = o0 + n_out + nj
        scr, jsems = refs[s0:s0 + n_scr], refs[s0 + n_scr:]
        step = pl.program_id(0)
        for ax in range(1, len(grid)):
            step = step * grid[ax] + pl.program_id(ax)
        total = math.prod(grid)
        early = [job.kind == "relay" for job in jobs]
        mid_step = (3 * total) // 5
        split = any(early) and 0 < mid_step < total - 1

        @pl.when(step == 0)
        def _():
            _run_jobs(jobs, "start", jins, jouts, jsems)

        if split:
            @pl.when(step == mid_step)
            def _():
                _run_jobs(jobs, "mid", jins, jouts, jsems, only=early)

        body(*ins, *outs, *scr)

        @pl.when(step == total - 1)
        def _():
            _run_jobs(jobs, "mid", jins, jouts, jsems, only=[not e for e in early] if split else None)
            _run_jobs(jobs, "finish", jins, jouts, jsems)

    res = _pcall(full, grid=grid, in_specs=in_specs + [ANY_SPEC] * nj, out_specs=out_specs + [ANY_SPEC] * nj,
                 out_shape=out_shape + [j.out for j in jobs], scratch_shapes=scratch_shapes + _job_scratch(jobs),
                 name=name, compiler_params=_cparams(sem), **kw)(*args, *[j.inp for j in jobs])
    return list(res[:n_out]), list(res[n_out:])


def _pair_add(g8, r4, cidx, *, name):
    _, r, c = g8.shape
    tr = ROW_TILE if r % ROW_TILE == 0 else r

    def body(c_ref, g_ref, r_ref, o_ref):
        o_ref[...] = (g_ref[...].astype(F32) + r_ref[...].astype(F32)).astype(BF16)

    return _pcall(
        body,
        grid_spec=pltpu.PrefetchScalarGridSpec(
            num_scalar_prefetch=1, grid=(4, r // tr),
            in_specs=[pl.BlockSpec((None, tr, c), lambda k, i, cr: (2 * k + cr[0], i, 0)),
                      pl.BlockSpec((None, tr, c), lambda k, i, cr: (k, i, 0))],
            out_specs=pl.BlockSpec((None, tr, c), lambda k, i, cr: (k, i, 0))),
        out_shape=jax.ShapeDtypeStruct((4, r, c), BF16), name=name,
        compiler_params=_cparams(("parallel", "parallel")))(cidx, g8, r4)


def _adam_update(g, w_ref, m_ref, v_ref, g_ref, d_ref, mo_ref, vo_ref):
    c1 = 1.0 - ADAM_B1 ** ADAM_STEP
    c2 = 1.0 - ADAM_B2 ** ADAM_STEP
    m2 = ADAM_B1 * m_ref[...] + (1.0 - ADAM_B1) * g
    v2 = ADAM_B2 * v_ref[...] + (1.0 - ADAM_B2) * (g * g)
    g_ref[...] = g
    mo_ref[...] = m2
    vo_ref[...] = v2
    d_ref[...] = -ADAM_LR * ((m2 / c1) / (jnp.sqrt(v2 / c2) + ADAM_EPS) + ADAM_WD * w_ref[...])


def _adamw_rows(srcs, items, own_cols, me1, loss_row, *, name):
    ns, ni, no = len(srcs), len(items), len(own_cols)
    full = lambda a: pl.BlockSpec(a.shape, lambda i, me: (0,) * a.ndim)
    in_specs = [full(a) for a in srcs]
    args = list(srcs)
    for (si, _r0, w, _m, _v) in own_cols:
        a = srcs[si]
        in_specs.append(pl.BlockSpec((N_DEV, a.shape[1], w.shape[1]), lambda i, me: (0, 0, me[0])))
        args.append(a)
    out_specs, out_shape = [], []
    for (_si, _r0, w, m, v) in list(items) + list(own_cols):
        in_specs += [full(w)] * 3
        args += [w, m, v]
        out_specs += [full(w)] * 4
        out_shape += [jax.ShapeDtypeStruct(w.shape, F32)] * 4
    out_specs.append(pl.BlockSpec((1, LANE), lambda i, me: (0, 0)))
    out_shape.append(jax.ShapeDtypeStruct((1, LANE), F32))

    def body(me_ref, *refs):
        src_refs, own_refs = refs[:ns], refs[ns:ns + no]
        wmv = refs[ns + no:ns + no + 3 * (ni + no)]
        outs = refs[ns + no + 3 * (ni + no):]
        lsrc, lrow = src_refs[loss_row[0]], loss_row[1]
        total = lsrc[0, lrow:lrow + 1, 0:LANE]
        for d in range(1, N_DEV):
            total = total + lsrc[d, lrow:lrow + 1, 0:LANE]
        outs[-1][...] = total
        for q, (si, r0, w, _m, _v) in enumerate(list(items) + list(own_cols)):
            nr, cw = w.shape
            gref = src_refs[si] if q < ni else own_refs[q - ni]
            g = gref[0, r0:r0 + nr, 0:cw]
            for d in range(1, N_DEV):
                g = g + gref[d, r0:r0 + nr, 0:cw]
            _adam_update(g, *wmv[3 * q:3 * q + 3], *outs[4 * q:4 * q + 4])

    res = _pcall(
        body,
        grid_spec=pltpu.PrefetchScalarGridSpec(num_scalar_prefetch=1, grid=(1,), in_specs=in_specs, out_specs=out_specs),
        out_shape=out_shape, name=name, compiler_params=_cparams(("arbitrary",)))(me1, *args)
    return [tuple(res[4 * q:4 * q + 4]) for q in range(ni + no)], res[-1]


def _adamw(gsrc, w, m, v, *, name):
    k, r, c = gsrc.shape
    tr = ROW_TILE if r % ROW_TILE == 0 else r

    def body(gs_ref, w_ref, m_ref, v_ref, g_ref, d_ref, mo_ref, vo_ref):
        g = gs_ref[0].astype(F32)
        for q in range(1, k):
            g = g + gs_ref[q].astype(F32)
        _adam_update(g, w_ref, m_ref, v_ref, g_ref, d_ref, mo_ref, vo_ref)

    tc = c
    if tr == r and r > ROW_TILE and c % 256 == 0:
        tc = 256
    blk = pl.BlockSpec((tr, tc), lambda i, j: (i, j))
    sd = jax.ShapeDtypeStruct((r, c), F32)
    return _pcall(body, grid=(r // tr, c // tc),
                  in_specs=[pl.BlockSpec((k, tr, tc), lambda i, j: (0, i, j)), blk, blk, blk],
                  out_specs=(blk, blk, blk, blk), out_shape=(sd, sd, sd, sd), name=name,
                  compiler_params=_cparams(("parallel", "parallel")))(gsrc, w, m, v)


WEIGHTS = ['w_in', 'lru_conv_w', 'lru_conv_b', 'lru_gate_a_w', 'lru_gate_a_b', 'lru_gate_x_w', 'lru_gate_x_b',
           'lru_a_param', 'ssd_conv_w', 'ssd_conv_b', 'ssd_dt_bias', 'ssd_a_log', 'ssd_d', 'ssd_norm_w', 'w_out',
           'ln1_g', 'ln1_b', 'w_ff1', 'w_ff2', 'ln2_g', 'ln2_b', 'w_ple_gate', 'w_ple', 'ln3_g', 'ln3_b']
BIG = ['w_in', 'w_out', 'w_ff1', 'w_ff2', 'w_ple_gate', 'w_ple']
COL_SHARDED = ('w_ff1', 'w_ple')
CONV = ['lru_conv_w', 'ssd_conv_w']
REPL = [n for n in WEIGHTS if n not in BIG and n not in CONV]
CONV_CH = {'lru_conv_w': LRU_W, 'ssd_conv_w': XBC}


def _to_dest_major(name, gfull):
    if name in COL_SHARDED:
        r, cfull = gfull.shape
        return gfull.reshape(r, N_DEV, cfull // N_DEV).transpose(1, 0, 2)
    rfull, cdim = gfull.shape
    return gfull.reshape(N_DEV, rfull // N_DEV, cdim)


def _full_weight(name, gathered):
    if name in COL_SHARDED:
        _, r, cs = gathered.shape
        full = gathered.transpose(1, 0, 2).reshape(r, N_DEV * cs)
    else:
        _, rs, cdim = gathered.shape
        full = gathered.reshape(N_DEV * rs, cdim)
    if name == 'w_in':
        full = lax.dynamic_update_slice(jnp.zeros((D_IN_PAD, D_MODEL), full.dtype), full, (0, 0))
    return full


SMALL_SRC = ("lru", "ssd", "heads", "rows", "gate_a", "gate_x")
AG_HOSTS = {"in_proj": ("w_ff1",), "lru_fwd": ("w_ff2",), "ssd_conv_fwd": ("w_ple_gate", "w_ple"), "ssd_fwd": ("w_out",)}
PAIR_HOSTS = ("d_x2", "d_pre", "d_x1", "d_ycat")
CHIP_HOSTS = {"lru_bwd": ("w_ple_gate", "w_ple", "w_ff2"), "ssd_bwd": ("w_ff1",), "ssd_conv_bwd": ("w_out",),
              "d_x": ("w_in",)}
SMALL_HOSTS = {"ssd_bwd": ("lru", "gate_a", "gate_x"), "d_w_in_3": ("ssd", "heads", "rows")}


class _Schedule:
    def __init__(self, shards, cidx):
        self.shards, self.cidx = shards, cidx
        self.pair, self.chip, self.small_jobs = [], [], []
        self.dest, self.summed, self.gathered_small = {}, {}, {}
        self.tags = []

    def ride(self, host):
        tags = []
        if host in AG_HOSTS:
            tags = [("weight", n, self.shards[n]) for n in AG_HOSTS[host]]
        elif host in PAIR_HOSTS or host in CHIP_HOSTS or host == "flush":
            tags = [("pair", n, a) for n, a in self.pair]
            self.pair = []
            if host not in PAIR_HOSTS:
                take = [t for t in self.chip if host == "flush" or t[0] in CHIP_HOSTS[host]]
                tags += [("chip", n, a) for n, a in take]
                self.chip = [t for t in self.chip if not any(t is u for u in take)]
        if host in SMALL_HOSTS:
            tags += [("small", n, a) for n, a in self.small_jobs if n in SMALL_HOSTS[host]]
            self.small_jobs = [t for t in self.small_jobs if t[0] not in SMALL_HOSTS[host]]
        self.tags = tags
        return [_Job({"weight": "relay", "small": "gather"}.get(kind, kind), a) for kind, _n, a in tags]

    def done(self, jobs, outs, w):
        for (kind, n, _a), o in zip(self.tags, outs):
            if kind == "weight":
                w[n] = _full_weight(n, o)
            elif kind == "small":
                self.gathered_small[n] = o
            elif kind == "pair":
                self.chip.append((n, _pair_add(self.dest[n], o, self.cidx, name="rs_pair_add_" + n)))
            else:
                self.summed[n] = o

    def grad(self, name, val):
        self.dest[name] = val if val.ndim == 3 else _to_dest_major(name, val)
        self.pair.append((name, self.dest[name]))

    def small(self, raw):
        self.small_jobs += list(raw.items())

    def pairs_now(self):
        tags = [("pair", n, a) for n, a in self.pair]
        self.pair, self.tags = [], tags
        jobs = [_Job("pair", a) for _k, _n, a in tags]
        self.done(jobs, _exchange(jobs, name="rs_pairs_now"), None)

    def flush(self):
        step = 0
        while self.pair or self.chip:
            jobs = self.ride("flush")
            self.done(jobs, _exchange(jobs, name="rs_flush_%d" % step), None)
            step += 1


def kernel(x, p, w_in, lru_conv_w, lru_conv_b, lru_gate_a_w, lru_gate_a_b, lru_gate_x_w, lru_gate_x_b, lru_a_param, ssd_conv_w, ssd_conv_b, ssd_dt_bias, ssd_a_log, ssd_d, ssd_norm_w, w_out, ln1_g, ln1_b, w_ff1, w_ff2, ln2_g, ln2_b, w_ple_gate, w_ple, ln3_g, ln3_b, loss_target, m_w_in, m_lru_conv_w, m_lru_conv_b, m_lru_gate_a_w, m_lru_gate_a_b, m_lru_gate_x_w, m_lru_gate_x_b, m_lru_a_param, m_ssd_conv_w, m_ssd_conv_b, m_ssd_dt_bias, m_ssd_a_log, m_ssd_d, m_ssd_norm_w, m_w_out, m_ln1_g, m_ln1_b, m_w_ff1, m_w_ff2, m_ln2_g, m_ln2_b, m_w_ple_gate, m_w_ple, m_ln3_g, m_ln3_b, v_w_in, v_lru_conv_w, v_lru_conv_b, v_lru_gate_a_w, v_lru_gate_a_b, v_lru_gate_x_w, v_lru_gate_x_b, v_lru_a_param, v_ssd_conv_w, v_ssd_conv_b, v_ssd_dt_bias, v_ssd_a_log, v_ssd_d, v_ssd_norm_w, v_w_out, v_ln1_g, v_ln1_b, v_w_ff1, v_w_ff2, v_ln2_g, v_ln2_b, v_w_ple_gate, v_w_ple, v_ln3_g, v_ln3_b):
    given = dict(locals())
    def local(a, n):
        return jnp.swapaxes(a[0], 0, 1) if n == 'w_in' else a[0]

    wsh = {n: local(given[n], n) for n in WEIGHTS}
    msh = {n: local(given["m_" + n], n) for n in WEIGHTS}
    vsh = {n: local(given["v_" + n], n) for n in WEIGHTS}
    xi, yi, ci = _mesh_pos()
    me = 4 * xi + 2 * yi + ci

    shards = {n: wsh[n].astype(BF16) for n in BIG}
    conv_pack = jnp.concatenate([_pad_rows8(wsh[n]) for n in CONV], axis=1)
    g_in, gconv = _exchange([_Job("relay", shards['w_in']), _Job("gather", conv_pack)], name="ag_first")
    full = {'w_in_t': _full_weight('w_in', g_in)}
    c0 = 0
    for n in CONV:
        cw = CONV_CH[n] // N_DEV
        full[n] = gconv[:, :4, c0:c0 + cw].transpose(1, 0, 2).reshape(4, CONV_CH[n])
        c0 += cw
    for n in REPL:
        full[n] = given[n] if given[n].ndim == 2 else wsh[n]

    sched = _Schedule(shards, jnp.reshape(ci, (1,)).astype(jnp.int32))
    loss_local, grad_x, g, raw = _local_step(x[0], p[0, 0], loss_target[0], full, sched)
    sched.flush()
    summed, gat = sched.summed, sched.gathered_small

    outs = {}
    for n in BIG:
        outs[n] = _adamw(summed[n], wsh[n], msh[n], vsh[n], name="adamw_" + n)
    for n, k in (("lru_gate_a_w", "gate_a"), ("lru_gate_x_w", "gate_x")):
        flat = lambda a: a.reshape(N_HEAD * HEAD_P, HEAD_P)
        res = _adamw(gat[k], flat(wsh[n]), flat(msh[n]), flat(vsh[n]), name="adamw_" + n)
        outs[n] = tuple(r.reshape(N_HEAD, HEAD_P, HEAD_P) for r in res)
    for n, row in (("lru_gate_a_b", 5), ("lru_gate_x_b", 6)):
        outs[n] = _adamw(gat["lru"][:, row].reshape(N_DEV, N_HEAD, HEAD_P), wsh[n], msh[n], vsh[n], name="adamw_" + n)
    row_items = [("lru_conv_b", 0, 4), ("lru_a_param", 0, 7),
                 ("ssd_conv_b", 1, 4), ("ssd_dt_bias", 2, 0), ("ssd_a_log", 2, 1), ("ssd_d", 2, 2),
                 ("ssd_norm_w", 3, 0), ("ln1_g", 3, 1), ("ln1_b", 3, 2), ("ln2_g", 3, 3), ("ln2_b", 3, 4),
                 ("ln3_g", 3, 5), ("ln3_b", 3, 6)]
    vec = lambda a: a.reshape(1, -1)
    items = [(si, r0, vec(given[n]), vec(given["m_" + n]), vec(given["v_" + n])) for n, si, r0 in row_items]
    own = [(si, 0, wsh[n], msh[n], vsh[n]) for n, si in (("lru_conv_w", 0), ("ssd_conv_w", 1))]
    me1 = jnp.reshape(me, (1,)).astype(jnp.int32)
    res, loss_row = _adamw_rows([gat[k] for k in SMALL_SRC[:4]], items, own, me1, (3, 7), name="adamw_small")
    loss = loss_row[0, 0]
    for (n, _si, _r0), r4 in zip(row_items, res[:len(row_items)]):
        outs[n] = r4
    for n, r4 in zip(CONV, res[len(row_items):]):
        outs[n] = r4

    def fin(n, k):
        a = jnp.swapaxes(outs[n][k], 0, 1) if n == 'w_in' else outs[n][k]
        return a.reshape(given[n].shape)

    return (loss, grad_x[None],
            *[fin(n, 0) for n in WEIGHTS], *[fin(n, 1) for n in WEIGHTS],
            *[fin(n, 2) for n in WEIGHTS], *[fin(n, 3) for n in WEIGHTS])
```

```python
import math

import jax
import jax.numpy as jnp
from jax import lax
from jax.experimental import pallas as pl
from jax.experimental.pallas import tpu as pltpu

F32 = jnp.float32
BF16 = jnp.bfloat16
HI = lax.Precision.HIGHEST

N_DEV = 8
D_MODEL = 1024
LRU_W = 1024
SSD_W = 1024
XBC = 2048
N_HEAD = 16
HEAD_P = 64
N_GROUP = 4
GROUP_W = 256
N_STATE = 128
CHUNK = 128
D_FF = 4096
PLE_DIM = 256
D_IN = 5136
D_IN_PAD = 5632
COL_G = 1024
COL_Z = 2048
COL_XBC = 3072
COL_DT = 5120
LRU_C = 8.0
ALPHA = 2.0 ** 0.25
LN_EPS = 1e-5
RMS_EPS = 1e-5
ADAM_LR = 0.001
ADAM_B1 = 0.9
ADAM_B2 = 0.999
ADAM_EPS = 1e-08
ADAM_WD = 0.01
ADAM_STEP = 10
GELU_C = math.sqrt(2.0 / math.pi)
LANE = 128
SUBLANE = 8
VMEM_LIMIT = 48 * 1024 * 1024
MESH_T = pl.DeviceIdType.MESH
NEG_BIG = -1e30


def _pcall(body, **kw):
    return pl.pallas_call(body, **kw)


def _cparams(sem):
    return pltpu.CompilerParams(dimension_semantics=sem, vmem_limit_bytes=VMEM_LIMIT)


def _dot(a, b):
    return jnp.dot(a.astype(BF16), b.astype(BF16), preferred_element_type=F32)


def _dot_nt(a, b):
    return lax.dot_general(a.astype(BF16), b.astype(BF16), (((1,), (1,)), ((), ())), preferred_element_type=F32)


def _dot_tn(a, b):
    return lax.dot_general(a.astype(BF16), b.astype(BF16), (((0,), (0,)), ((), ())), preferred_element_type=F32)


def _dotx(a, b):
    return jnp.dot(a, b, precision=HI, preferred_element_type=F32)


def _sigmoid(x):
    return jax.nn.sigmoid(x)


def _softplus(v):
    return jnp.maximum(v, 0.0) + jnp.log1p(jnp.exp(-jnp.abs(v)))


def _gelu(x):
    th = jnp.tanh(GELU_C * (x + 0.044715 * x * x * x))
    return 0.5 * x * (1.0 + th), th


def _gelu_grad(x, th):
    return 0.5 * (1.0 + th) + 0.5 * x * (1.0 - th * th) * GELU_C * (1.0 + 3.0 * 0.044715 * x * x)


def _iota(shape, dim):
    return lax.broadcasted_iota(jnp.int32, shape, dim)


def _mm(a, b, mode, *, tm, tn, name, a_fn=None, extra=None, epi=None, out_dtype=F32, dest_major=False, into=None,
        jobs=()):
    m = a.shape[1] if mode == "tn" else a.shape[0]
    n = b.shape[0] if mode == "nt" else b.shape[1]
    tm, tn = min(tm, m), min(tn, n)
    if dest_major:
        tn = n // N_DEV
    if mode == "nn":
        m, k = a.shape
        _, n = b.shape
        a_spec = pl.BlockSpec((tm, k), lambda i, j: (i, 0))
        b_spec = pl.BlockSpec((k, tn), lambda i, j: (0, j))
        dims = ((1,), (0,))
    elif mode == "nt":
        m, k = a.shape
        n, _ = b.shape
        a_spec = pl.BlockSpec((tm, k), lambda i, j: (i, 0))
        b_spec = pl.BlockSpec((tn, k), lambda i, j: (j, 0))
        dims = ((1,), (1,))
    else:
        k, m = a.shape
        _, n = b.shape
        a_spec = pl.BlockSpec((k, tm), lambda i, j: (0, i))
        b_spec = pl.BlockSpec((k, tn), lambda i, j: (0, j))
        dims = ((0,), (0,))
    assert m % tm == 0 and n % tn == 0, (name, m, n, tm, tn)
    o_spec = pl.BlockSpec((tm, tn), lambda i, j: (i, j))
    in_specs = [a_spec, b_spec]
    args = [a, b]
    if extra is not None:
        in_specs.append(o_spec)
        args.append(extra)

    def body(*refs):
        a_ref, b_ref, o_ref = refs[0], refs[1], refs[-1]
        av = a_ref[...]
        if a_fn is not None:
            av = a_fn(av)
        acc = lax.dot_general(av.astype(BF16), b_ref[...].astype(BF16), (dims, ((), ())), preferred_element_type=F32)
        if epi is not None:
            acc = epi(acc, refs[2][...])
        o_ref[...] = acc.astype(out_dtype)

    out_shape = jax.ShapeDtypeStruct((m, n), out_dtype)
    aliases = None
    if dest_major:
        assert extra is None
        o_spec = pl.BlockSpec((None, tm, tn), lambda i, j: (j, i, 0))
        out_shape = jax.ShapeDtypeStruct((N_DEV, m, tn), out_dtype)
    if into is not None:
        buf, row0, total = into
        assert extra is None and row0 % tm == 0
        o_spec = pl.BlockSpec((tm, tn), lambda i, j: (row0 // tm + i, j))
        out_shape = jax.ShapeDtypeStruct((total, n), out_dtype)
        if buf is not None:
            in_specs.append(ANY_SPEC)
            args.append(buf)
            aliases = {len(args) - 1: 0}
    (out,), jouts = _hosted(body, jobs, grid=(m // tm, n // tn), in_specs=in_specs, out_specs=[o_spec],
                            out_shape=[out_shape], args=args, name=name, aliases=aliases)
    return (out, jouts) if jobs else out


def _mm_pieces(pieces, offsets, b, *, tm, name, extra, epi, jobs=()):
    m = pieces[0].shape[0]
    kb, n = b.shape
    tm = min(tm, m)
    row = lambda wdt: pl.BlockSpec((tm, wdt), lambda i: (i, 0))
    in_specs = [row(pc.shape[1]) for pc in pieces] + [pl.BlockSpec((kb, n), lambda i: (0, 0)), row(n)]
    np_ = len(pieces)

    def body(*refs):
        b_ref, e_ref, o_ref = refs[np_], refs[np_ + 1], refs[np_ + 2]
        acc = jnp.zeros((tm, n), F32)
        for q in range(np_):
            kq = pieces[q].shape[1]
            acc = acc + jnp.dot(refs[q][...].astype(BF16), b_ref[offsets[q]:offsets[q] + kq, :].astype(BF16),
                                preferred_element_type=F32)
        o_ref[...] = epi(acc, e_ref[...])

    (out,), jouts = _hosted(body, jobs, grid=(m // tm,), in_specs=in_specs, out_specs=[row(n)],
                            out_shape=[jax.ShapeDtypeStruct((m, n), F32)], args=list(pieces) + [b, extra], name=name)
    return (out, jouts) if jobs else out


def _relu2(v):
    r = jnp.maximum(v, 0.0)
    return r * r


ROW_TILE = 256


def _ln_stats(t):
    mu = jnp.mean(t, axis=-1, keepdims=True)
    xc = t - mu
    var = jnp.mean(xc * xc, axis=-1, keepdims=True)
    rstd = lax.rsqrt(var + LN_EPS)
    return xc * rstd, rstd


def _ln_bwd_rows(dy, xhat, rstd, g):
    dxh = dy * g
    m1 = jnp.mean(dxh, axis=-1, keepdims=True)
    m2 = jnp.mean(dxh * xhat, axis=-1, keepdims=True)
    return rstd * (dxh - m1 - xhat * m2)


def _mm_ln(a, b, res, g, beta, *, tm, name, a_fn=None):
    m, k = a.shape
    d = b.shape[1]
    tm = min(tm, m)
    row = pl.BlockSpec((tm, d), lambda i: (i, 0))
    par = pl.BlockSpec((1, d), lambda i: (0, 0))

    def body(a_ref, b_ref, r_ref, g_ref, be_ref, br_ref, y_ref, yb_ref):
        av = a_ref[...]
        if a_fn is not None:
            av = a_fn(av)
        acc = jnp.dot(av.astype(BF16), b_ref[...].astype(BF16), preferred_element_type=F32)
        br_ref[...] = acc
        xhat, _ = _ln_stats(ALPHA * r_ref[...] + acc)
        y = xhat * g_ref[...] + be_ref[...]
        y_ref[...] = y
        yb_ref[...] = y.astype(BF16)

    sd = jax.ShapeDtypeStruct((m, d), F32)
    return _pcall(body, grid=(m // tm,),
                  in_specs=[pl.BlockSpec((tm, k), lambda i: (i, 0)), pl.BlockSpec((k, d), lambda i: (0, 0)), row, par, par],
                  out_specs=(row, row, row), out_shape=(sd, sd, jax.ShapeDtypeStruct((m, d), BF16)), name=name,
                  compiler_params=_cparams(("parallel",)))(a, b, res, g, beta)


def _mm_ln_bwd(a, b, res, branch, g, dy0, coef0, *, tm, name, jobs=()):
    m, k = a.shape
    d = b.shape[0]
    tm = min(tm, m)
    row = pl.BlockSpec((tm, d), lambda i: (i, 0))
    par = pl.BlockSpec((1, d), lambda i: (0, 0))

    def body(a_ref, b_ref, r_ref, br_ref, g_ref, dy0_ref, dt_ref, dtb_ref, dg_ref, db_ref):
        acc = lax.dot_general(a_ref[...].astype(BF16), b_ref[...].astype(BF16), (((1,), (1,)), ((), ())),
                              preferred_element_type=F32)
        dy = coef0 * dy0_ref[...] + acc
        xhat, rstd = _ln_stats(ALPHA * r_ref[...] + br_ref[...])
        dt = _ln_bwd_rows(dy, xhat, rstd, g_ref[...])
        dt_ref[...] = dt
        dtb_ref[...] = dt.astype(BF16)

        @pl.when(pl.program_id(0) == 0)
        def _():
            dg_ref[...] = jnp.zeros_like(dg_ref)
            db_ref[...] = jnp.zeros_like(db_ref)

        dg_ref[...] += jnp.sum(dy * xhat, axis=0, keepdims=True)
        db_ref[...] += jnp.sum(dy, axis=0, keepdims=True)

    pd = jax.ShapeDtypeStruct((1, d), F32)
    outs, jouts = _hosted(
        body, jobs, grid=(m // tm,),
        in_specs=[pl.BlockSpec((tm, k), lambda i: (i, 0)), pl.BlockSpec((d, k), lambda i: (0, 0)), row, row, par, row],
        out_specs=(row, row, par, par),
        out_shape=(jax.ShapeDtypeStruct((m, d), F32), jax.ShapeDtypeStruct((m, d), BF16), pd, pd),
        args=(a, b, res, branch, g, dy0), name=name)
    return (tuple(outs), jouts) if jobs else tuple(outs)


def _head(x2, x2b, p, wg, wp, g, beta, tgt, *, name):
    s, d = x2.shape
    tile = 2 * ROW_TILE
    row = pl.BlockSpec((tile, d), lambda i: (i, 0))
    par = pl.BlockSpec((1, d), lambda i: (0, 0))
    lsp = pl.BlockSpec((1, LANE), lambda i: (0, 0))
    whole = lambda a: pl.BlockSpec(a.shape, lambda i: (0, 0))

    def body(x2_ref, x2b_ref, p_ref, wg_ref, wp_ref, g_ref, be_ref, t_ref,
             loss_ref, dgp_ref, dple_ref, dt_ref, dg_ref, db_ref):
        gate = _sigmoid(_dot(x2b_ref[...], wg_ref[...]))
        ple_v = _dot(p_ref[...], wp_ref[...])
        xhat, rstd = _ln_stats(ALPHA * x2_ref[...] + gate * ple_v)
        err = xhat * g_ref[...] + be_ref[...] - t_ref[...]
        dy = err * (1.0 / d)
        dt = _ln_bwd_rows(dy, xhat, rstd, g_ref[...])
        dt_ref[...] = dt
        dgp_ref[...] = (dt * ple_v * gate * (1.0 - gate)).astype(BF16)
        dple_ref[...] = (dt * gate).astype(BF16)

        @pl.when(pl.program_id(0) == 0)
        def _():
            loss_ref[...] = jnp.zeros_like(loss_ref)
            dg_ref[...] = jnp.zeros_like(dg_ref)
            db_ref[...] = jnp.zeros_like(db_ref)

        loss_ref[...] += 0.5 * jnp.sum(jnp.mean(err * err, axis=-1, keepdims=True))
        dg_ref[...] += jnp.sum(dy * xhat, axis=0, keepdims=True)
        db_ref[...] += jnp.sum(dy, axis=0, keepdims=True)

    sd = jax.ShapeDtypeStruct((s, d), F32)
    sb = jax.ShapeDtypeStruct((s, d), BF16)
    pd = jax.ShapeDtypeStruct((1, d), F32)
    return _pcall(body, grid=(s // tile,),
                  in_specs=[row, row, pl.BlockSpec((tile, p.shape[1]), lambda i: (i, 0)), whole(wg), whole(wp), par, par,
                            row],
                  out_specs=(lsp, row, row, row, par, par),
                  out_shape=(jax.ShapeDtypeStruct((1, LANE), F32), sb, sb, sd, pd, pd),
                  name=name, compiler_params=_cparams(("arbitrary",)))(x2, x2b, p, wg, wp, g, beta, tgt)


CONV_R = 256
PAD = SUBLANE


def _shift_down(ext, s):
    if s == 0:
        return ext[PAD:, :]
    return pltpu.roll(ext, s, 0)[PAD:, :]


def _shift_up(ext, s):
    r = ext.shape[0] - PAD
    if s == 0:
        return ext[:r, :]
    return pltpu.roll(ext, r + PAD - s, 0)[:r, :]


def _conv_rows(xpad_ref, r0, w_ref):
    ext = xpad_ref[pl.ds(r0, CONV_R + PAD), :]
    acc = _shift_down(ext, 0) * w_ref[3:4, :]
    for k in range(3):
        acc = acc + _shift_down(ext, 3 - k) * w_ref[k:k + 1, :]
    return acc, ext


def _fill_front_padded(dst_ref, src_ref, s):
    dst_ref[0:PAD, :] = jnp.zeros((PAD, dst_ref.shape[1]), F32)

    def cp(q, _):
        r0 = pl.multiple_of(q * CONV_R, CONV_R)
        dst_ref[pl.ds(pl.multiple_of(PAD + r0, PAD), CONV_R), :] = src_ref[pl.ds(r0, CONV_R), :]
        return 0

    lax.fori_loop(0, s // CONV_R, cp, 0)


def _conv_silu_fwd(proj, w8, b, *, col0, width, ct, name, jobs=()):
    s = proj.shape[0]
    nb = col0 // ct

    def body(x_ref, w_ref, b_ref, o_ref, xpad):
        _fill_front_padded(xpad, x_ref, s)

        def step(q, _):
            r0 = pl.multiple_of(q * CONV_R, CONV_R)
            acc, _e = _conv_rows(xpad, r0, w_ref)
            pre = acc + b_ref[...]
            o_ref[pl.ds(r0, CONV_R), :] = pre * _sigmoid(pre)
            return 0

        lax.fori_loop(0, s // CONV_R, step, 0)

    (out,), jouts = _hosted(
        body, jobs, grid=(width // ct,),
        in_specs=[pl.BlockSpec((s, ct), lambda j: (0, nb + j)), pl.BlockSpec((SUBLANE, ct), lambda j: (0, j)),
                  pl.BlockSpec((1, ct), lambda j: (0, j))],
        out_specs=[pl.BlockSpec((s, ct), lambda j: (0, j))],
        out_shape=[jax.ShapeDtypeStruct((s, width), F32)],
        scratch_shapes=[pltpu.VMEM((s + PAD, ct), F32)], name=name, args=(proj, w8, b))
    return (out, jouts) if jobs else out


def _conv_bwd_rows(dpad_ref, r0, w_ref):
    return _conv_bwd_ext(dpad_ref[pl.ds(r0, CONV_R + PAD), :], w_ref)


def _conv_bwd_ext(ext, w_ref):
    acc = _shift_up(ext, 0) * w_ref[3:4, :]
    for k in range(3):
        acc = acc + _shift_up(ext, 3 - k) * w_ref[k:k + 1, :]
    return acc


def _conv_silu_bwd(proj, dact, w8, b, *, col0, width, ct, name, jobs=()):
    s = proj.shape[0]
    nb = col0 // ct

    def body(x_ref, d_ref, w_ref, b_ref, dx_ref, dwb_ref, xpad, dpad):
        _fill_front_padded(xpad, x_ref, s)
        dpad[pl.ds(s, PAD), :] = jnp.zeros((PAD, ct), F32)
        dwb_ref[...] = jnp.zeros_like(dwb_ref)

        def step(q, _):
            r0 = pl.multiple_of(q * CONV_R, CONV_R)
            acc, ext = _conv_rows(xpad, r0, w_ref)
            pre = acc + b_ref[...]
            sg = _sigmoid(pre)
            dpre = d_ref[pl.ds(r0, CONV_R), :] * sg * (1.0 + pre * (1.0 - sg))
            dpad[pl.ds(r0, CONV_R), :] = dpre
            for k in range(4):
                dwb_ref[k:k + 1, :] += jnp.sum(dpre * _shift_down(ext, 3 - k), axis=0, keepdims=True)
            dwb_ref[4:5, :] += jnp.sum(dpre, axis=0, keepdims=True)
            return 0

        lax.fori_loop(0, s // CONV_R, step, 0)

        def step2(q, _):
            r0 = pl.multiple_of(q * CONV_R, CONV_R)
            dx_ref[pl.ds(r0, CONV_R), :] = _conv_bwd_rows(dpad, r0, w_ref).astype(BF16)
            return 0

        lax.fori_loop(0, s // CONV_R, step2, 0)

    colb = pl.BlockSpec((s, ct), lambda j: (0, j))
    outs, jouts = _hosted(
        body, jobs, grid=(width // ct,),
        in_specs=[pl.BlockSpec((s, ct), lambda j: (0, nb + j)), colb, pl.BlockSpec((SUBLANE, ct), lambda j: (0, j)),
                  pl.BlockSpec((1, ct), lambda j: (0, j))],
        out_specs=(colb, pl.BlockSpec((SUBLANE, ct), lambda j: (0, j))),
        out_shape=(jax.ShapeDtypeStruct((s, width), BF16), jax.ShapeDtypeStruct((SUBLANE, width), F32)),
        scratch_shapes=[pltpu.VMEM((s + PAD, ct), F32), pltpu.VMEM((s + PAD, ct), F32)], name=name,
        args=(proj, dact, w8, b))
    return (tuple(outs), jouts) if jobs else tuple(outs)


LRU_CT = 128


def _row_of(v, r):
    return jnp.sum(jnp.where(_iota((v.shape[0], 1), 0) == r, v, 0.0), axis=0, keepdims=True)


def _scan_fwd(a, u):
    r = a.shape[0]
    row = _iota((r, 1), 0)
    d = 1
    while d < r:
        valid = row >= d
        u = jnp.where(valid, a * pltpu.roll(u, d, 0) + u, u)
        a = jnp.where(valid, a * pltpu.roll(a, d, 0), a)
        d *= 2
    return a, u


def _scan_rev(b, u):
    r = b.shape[0]
    row = _iota((r, 1), 0)
    d = 1
    while d < r:
        valid = row < r - d
        u = jnp.where(valid, b * pltpu.roll(u, r - d, 0) + u, u)
        b = jnp.where(valid, b * pltpu.roll(b, r - d, 0), b)
        d *= 2
    return b, u


def _lru_chunk(xpad, r0, cw_ref, cb, wa, ba, wx, bx, sp):
    acc, ext = _conv_rows(xpad, r0, cw_ref)
    xl = acc + cb
    r = _sigmoid(_dot(xl, wa) + ba)
    i = _sigmoid(_dot(xl, wx) + bx)
    la = -LRU_C * r * sp
    a = jnp.exp(la)
    a2 = jnp.exp(2.0 * la)
    mult = jnp.sqrt(-jnp.tanh(la) * (a2 + 1.0))
    first = (r0 + _iota((CONV_R, 1), 0)) == 0
    mult = jnp.where(first, 1.0, mult)
    return ext, xl, r, i, a, a2, mult, first


def _lru_specs(s):
    ct = LRU_CT
    nb_g = COL_G // ct
    return dict(
        x=pl.BlockSpec((s, ct), lambda j: (0, j)),
        g=pl.BlockSpec((s, ct), lambda j: (0, nb_g + j)),
        col=pl.BlockSpec((s, ct), lambda j: (0, j)),
        cw=pl.BlockSpec((SUBLANE, ct), lambda j: (0, j)),
        vec=pl.BlockSpec((1, ct), lambda j: (0, j)),
        gate=pl.BlockSpec((None, ct, ct), lambda j: (j, 0, 0)),
    )


def _lru_fwd(proj, cw8, cb, wa_bd, ba, wx_bd, bx, ap, *, name, jobs=()):
    s = proj.shape[0]
    ct = LRU_CT
    sp_ = _lru_specs(s)

    def body(x_ref, g_ref, cw_ref, cb_ref, wa_ref, ba_ref, wx_ref, bx_ref, ap_ref, y_ref, h_ref, xpad):
        _fill_front_padded(xpad, x_ref, s)
        sp = _softplus(-ap_ref[...])

        def step(q, carry):
            r0 = pl.multiple_of(q * CONV_R, CONV_R)
            _e, xl, _r, i, a, _a2, mult, _f = _lru_chunk(xpad, r0, cw_ref, cb_ref[...], wa_ref[...], ba_ref[...],
                                                       wx_ref[...], bx_ref[...], sp)
            acum, ucum = _scan_fwd(a, xl * i * mult)
            h = acum * carry + ucum
            h_ref[pl.ds(r0, CONV_R), :] = h
            ge, _th = _gelu(g_ref[pl.ds(r0, CONV_R), :])
            y_ref[pl.ds(r0, CONV_R), :] = (ge * h).astype(BF16)
            return _row_of(h, CONV_R - 1)

        lax.fori_loop(0, s // CONV_R, step, jnp.zeros((1, ct), F32))

    (ymix, hs), jouts = _hosted(
        body, jobs, grid=(LRU_W // ct,),
        in_specs=[sp_["x"], sp_["g"], sp_["cw"], sp_["vec"], sp_["gate"], sp_["vec"], sp_["gate"], sp_["vec"], sp_["vec"]],
        out_specs=(sp_["col"], sp_["col"]),
        out_shape=(jax.ShapeDtypeStruct((s, LRU_W + SSD_W), BF16), jax.ShapeDtypeStruct((s, LRU_W), F32)),
        scratch_shapes=[pltpu.VMEM((s + PAD, ct), F32)],
        name=name, args=(proj, proj, cw8, cb, wa_bd, ba, wx_bd, bx, ap))
    return ((ymix, hs), jouts) if jobs else (ymix, hs)


def _lru_bwd(proj, dy, hs, cw8, cb, wa_bd, ba, wx_bd, bx, ap, *, name, jobs=()):
    s = proj.shape[0]
    ct = LRU_CT
    sp_ = _lru_specs(s)

    nq = s // CONV_R

    def body(x_ref, g_ref, dy_ref, h_ref, cw_ref, cb_ref, wa_ref, ba_ref, wx_ref, bx_ref, ap_ref,
             dx_ref, dg_ref, dcwb_ref, dwa_ref, dwx_ref, xpad, hpad):
        _fill_front_padded(xpad, x_ref, s)
        _fill_front_padded(hpad, h_ref, s)
        apv = ap_ref[...]
        sp = _softplus(-apv)
        cb_v, wa, ba_v, wx, bx_v = cb_ref[...], wa_ref[...], ba_ref[...], wx_ref[...], bx_ref[...]
        dcwb_ref[...] = jnp.zeros_like(dcwb_ref)
        dwa_ref[...] = jnp.zeros_like(dwa_ref)
        dwx_ref[...] = jnp.zeros_like(dwx_ref)

        def back(k, carry):
            g_next, a_next, dxl_next = carry
            last_row = _iota((CONV_R, 1), 0) == CONV_R - 1
            r0 = pl.multiple_of((nq - 1 - k) * CONV_R, CONV_R)
            ext, xl, r, i, a, a2, mult, first = _lru_chunk(xpad, r0, cw_ref, cb_v, wa, ba_v, wx, bx_v, sp)
            gv = g_ref[pl.ds(r0, CONV_R), :]
            dyv = dy_ref[pl.ds(r0, CONV_R), :]
            hext = hpad[pl.ds(r0, CONV_R + PAD), :]
            ge, th = _gelu(gv)
            dg_ref[pl.ds(r0, CONV_R), :] = (dyv * _shift_down(hext, 0) * _gelu_grad(gv, th)).astype(BF16)
            b = jnp.where(last_row, a_next, pltpu.roll(a, CONV_R - 1, 0))
            bcum, dcum = _scan_rev(b, dyv * ge)
            gval = dcum + bcum * g_next
            hprev = _shift_down(hext, 1)
            da = gval * hprev
            dxl = gval * i * mult
            di = gval * xl * mult
            dmult = jnp.where(first, 0.0, gval * xl * i)
            dla = da * a - dmult * a2 / mult
            dr = dla * (-LRU_C) * sp
            dcwb_ref[7:8, :] += jnp.sum(dla * (-LRU_C) * r, axis=0, keepdims=True)
            dpr = dr * r * (1.0 - r)
            dpi = di * i * (1.0 - i)
            dxl = dxl + _dot_nt(dpr, wa) + _dot_nt(dpi, wx)
            dwa_ref[...] += _dot_tn(xl, dpr)
            dwx_ref[...] += _dot_tn(xl, dpi)
            dcwb_ref[5:6, :] += jnp.sum(dpr, axis=0, keepdims=True)
            dcwb_ref[6:7, :] += jnp.sum(dpi, axis=0, keepdims=True)
            for tap in range(4):
                dcwb_ref[tap:tap + 1, :] += jnp.sum(dxl * _shift_down(ext, 3 - tap), axis=0, keepdims=True)
            dcwb_ref[4:5, :] += jnp.sum(dxl, axis=0, keepdims=True)
            dx_ref[pl.ds(r0, CONV_R), :] = _conv_bwd_ext(jnp.concatenate([dxl, dxl_next], axis=0), cw_ref).astype(BF16)
            return _row_of(gval, 0), _row_of(a, 0), dxl[:PAD, :]

        zero = jnp.zeros((1, ct), F32)
        lax.fori_loop(0, nq, back, (zero, zero, jnp.zeros((PAD, ct), F32)))
        dcwb_ref[7:8, :] = dcwb_ref[7:8, :] * (-_sigmoid(-apv))

    nt = LRU_W // ct
    outs, jouts = _hosted(
        body, jobs, grid=(nt,),
        in_specs=[sp_["x"], sp_["g"], sp_["col"], sp_["col"], sp_["cw"], sp_["vec"], sp_["gate"], sp_["vec"], sp_["gate"],
                  sp_["vec"], sp_["vec"]],
        out_specs=(sp_["col"], sp_["col"], sp_["cw"], sp_["gate"], sp_["gate"]),
        out_shape=(jax.ShapeDtypeStruct((s, LRU_W), BF16), jax.ShapeDtypeStruct((s, LRU_W), BF16),
                   jax.ShapeDtypeStruct((SUBLANE, LRU_W), F32), jax.ShapeDtypeStruct((nt, ct, ct), F32),
                   jax.ShapeDtypeStruct((nt, ct, ct), F32)),
        scratch_shapes=[pltpu.VMEM((s + PAD, ct), F32), pltpu.VMEM((s + PAD, ct), F32)],
        name=name, args=(proj, proj, dy, hs, cw8, cb, wa_bd, ba, wx_bd, bx, ap))
    return (tuple(outs), jouts) if jobs else tuple(outs)


def _split3(v):
    hi = v.astype(BF16)
    r1 = v - hi.astype(F32)
    mid = r1.astype(BF16)
    lo = (r1 - mid.astype(F32)).astype(BF16)
    return hi, mid, lo


def _dot01(m01, v):
    mb = m01.astype(BF16)
    hi, mid, lo = _split3(v)
    f = lambda part: jnp.dot(mb, part, preferred_element_type=F32)
    return f(hi) + f(mid) + f(lo)


def _dot01_r(v, m01, parts=3):
    mb = m01.astype(BF16)
    acc = None
    for part in _split3(v)[:parts]:
        t = jnp.dot(part, mb, preferred_element_type=F32)
        acc = t if acc is None else acc + t
    return acc


def _ssd_prep(dtr, bias, alog_pad):
    l = CHUNK
    lane = _iota((1, LANE), 1)
    a_head = jnp.where(lane < N_HEAD, -jnp.exp(alog_pad), 0.0)
    dt = _softplus(dtr + bias)
    tril = (_iota((l, l), 1) <= _iota((l, l), 0)).astype(F32)
    a = dt * a_head
    cs = _dot01(tril, a)
    tot = jnp.sum(a, axis=0, keepdims=True)
    return dict(a_head=a_head, dt=dt, tril=tril, cs=cs, tot=tot)


def _col(v, h):
    lane = _iota(v.shape, 1)
    return jnp.sum(jnp.where(lane == h, v, 0.0), axis=1, keepdims=True)


def _decay_mat(cs, cst_ref, h, causal):
    row = cst_ref[h:h + 1, :]
    return jnp.exp(jnp.where(causal, _col(cs, h) - row, NEG_BIG))


def _head_mask(j, rows=CHUNK):
    lane = _iota((rows, GROUP_W), 1)
    return (lane >= j * HEAD_P) & (lane < (j + 1) * HEAD_P)


def _over_heads(v, g):
    r = v.shape[0]
    out = jnp.zeros((r, GROUP_W), F32)
    for j in range(4):
        out = jnp.where(_head_mask(j, r), _col(v, 4 * g + j), out)
    return out


def _ssd_group_fwd(q, g, xs_g, bg, cg, ht_g, cst_ref, causal, dx_g):
    dtx_g, csx_g, totx_g = _over_heads(q["dt"], g), _over_heads(q["cs"], g), _over_heads(q["tot"], g)
    xdt = xs_g * dtx_g
    ex = jnp.exp(csx_g)
    cb = _dot_nt(cg, bg)
    yoff = _dot(cg, ht_g) * ex
    ydiag = jnp.zeros((CHUNK, GROUP_W), F32)
    lms = []
    for j in range(4):
        lms.append(_decay_mat(q["cs"], cst_ref, 4 * g + j, causal))
        ydiag = jnp.where(_head_mask(j), _dot(cb * lms[j], xdt), ydiag)
    y = ydiag + yoff + xs_g * dx_g
    dsx = jnp.exp(totx_g - csx_g)
    return y, dict(xdt=xdt, ex=ex, cb=cb, yoff=yoff, dsx=dsx, dtx=dtx_g, totx=totx_g, lms=lms)


def _gated_norm_fwd(y_g, z_g, w_g):
    sz = _sigmoid(z_g)
    silu = z_g * sz
    yf = y_g * silu
    rs = lax.rsqrt(jnp.mean(yf * yf, axis=1, keepdims=True) + RMS_EPS)
    yn = yf * rs
    return yn * w_g, (sz, silu, rs, yn)


def _ssd_fwd(xact, proj, ymix, bias_pad, alog_pad, dxp, normw, *, name, jobs=()):
    s = xact.shape[0]
    nc = s // CHUNK

    def body(xa_ref, dt_ref, z_ref, _ymix_ref, bias_ref, alp_ref, dx_ref, nw_ref, y_ref, hp_ref, ht, cst):
        @pl.when(pl.program_id(0) == 0)
        def _():
            ht[...] = jnp.zeros_like(ht)

        hp_ref[...] = ht[...]
        q = _ssd_prep(dt_ref[...], bias_ref[...], alp_ref[...])
        cst[...] = q["cs"].T
        causal = q["tril"] > 0.0
        for g in range(N_GROUP):
            sl = slice(g * GROUP_W, (g + 1) * GROUP_W)
            xs_g = xa_ref[:, sl]
            bg = xa_ref[:, SSD_W + g * N_STATE:SSD_W + (g + 1) * N_STATE]
            cg = xa_ref[:, SSD_W + N_GROUP * N_STATE + g * N_STATE:SSD_W + N_GROUP * N_STATE + (g + 1) * N_STATE]
            ht_g = ht[:, sl]
            y, f = _ssd_group_fwd(q, g, xs_g, bg, cg, ht_g, cst, causal, dx_ref[:, sl])
            out, _ = _gated_norm_fwd(y, z_ref[:, sl], nw_ref[:, sl])
            y_ref[:, sl] = out.astype(BF16)
            ht[:, sl] = jnp.exp(f["totx"]) * ht_g + _dot_tn(bg, f["xdt"] * f["dsx"])

    par = lambda w: pl.BlockSpec((1, w), lambda c: (0, 0))
    (ycat, hprev), jouts = _hosted(
        body, jobs, grid=(nc,),
        in_specs=[pl.BlockSpec((CHUNK, XBC), lambda c: (c, 0)),
                  pl.BlockSpec((CHUNK, LANE), lambda c: (c, COL_DT // LANE)),
                  pl.BlockSpec((CHUNK, SSD_W), lambda c: (c, COL_Z // SSD_W)),
                  ANY_SPEC, par(LANE), par(LANE), par(SSD_W), par(SSD_W)],
        out_specs=(pl.BlockSpec((CHUNK, SSD_W), lambda c: (c, LRU_W // SSD_W)),
                   pl.BlockSpec((None, N_STATE, SSD_W), lambda c: (c, 0, 0))),
        out_shape=(jax.ShapeDtypeStruct(ymix.shape, ymix.dtype), jax.ShapeDtypeStruct((nc, N_STATE, SSD_W), F32)),
        scratch_shapes=[pltpu.VMEM((N_STATE, SSD_W), F32), pltpu.VMEM((CHUNK, LANE), F32)],
        aliases={3: 0}, name=name, args=(xact, proj, proj, ymix, bias_pad, alog_pad, dxp, normw))
    return ((ycat, hprev), jouts) if jobs else (ycat, hprev)


def _ssd_bwd(xact, proj, dycat, hprev, bias_pad, alog_pad, dxp, normw, *, name, jobs=()):
    s = xact.shape[0]
    nc = s // CHUNK
    l = CHUNK

    def body(xa_ref, dt_ref, z_ref, dy_ref, hp_ref, bias_ref, alp_ref, dx_ref, nw_ref,
             dxa_ref, ddt_ref, dz_ref, dnw_ref, small_ref, dht, cst, accx, dcsx_s, ddtx_s):
        step = pl.program_id(0)

        @pl.when(step == 0)
        def _():
            dht[...] = jnp.zeros_like(dht)
            accx[...] = jnp.zeros_like(accx)
            dnw_ref[...] = jnp.zeros_like(dnw_ref)
            small_ref[...] = jnp.zeros_like(small_ref)

        dtr = dt_ref[...]
        q = _ssd_prep(dtr, bias_ref[...], alp_ref[...])
        cst[...] = q["cs"].T
        causal = q["tril"] > 0.0
        lane = _iota((l, LANE), 1)
        head_row = _iota((LANE, l), 0)
        dcs_head = jnp.zeros((l, LANE), F32)
        dcs_rows = jnp.zeros((LANE, l), F32)
        for g in range(N_GROUP):
            sl = slice(g * GROUP_W, (g + 1) * GROUP_W)
            slb = slice(SSD_W + g * N_STATE, SSD_W + (g + 1) * N_STATE)
            slc = slice(SSD_W + N_GROUP * N_STATE + g * N_STATE, SSD_W + N_GROUP * N_STATE + (g + 1) * N_STATE)
            xs_g, bg, cg = xa_ref[:, sl], xa_ref[:, slb], xa_ref[:, slc]
            ht_g = hp_ref[:, sl]
            dxp_g = dx_ref[:, sl]
            y, f = _ssd_group_fwd(q, g, xs_g, bg, cg, ht_g, cst, causal, dxp_g)
            z_g, nw_g = z_ref[:, sl], nw_ref[:, sl]
            _o, (sz, silu, rs, yn) = _gated_norm_fwd(y, z_g, nw_g)
            dout = dy_ref[:, sl]
            dnw_ref[:, sl] += jnp.sum(dout * yn, axis=0, keepdims=True)
            dyn = dout * nw_g
            dyf = rs * (dyn - yn * jnp.mean(dyn * yn, axis=1, keepdims=True))
            dy = dyf * silu
            dz_ref[:, sl] = (dyf * y * sz * (1.0 + z_g * (1.0 - sz))).astype(BF16)
            accx[0:1, sl] += jnp.sum(dy * xs_g, axis=0, keepdims=True)
            dyo = dy * f["ex"]
            dcg = _dot_nt(dyo, ht_g)
            dht_prev = _dot_tn(cg, dyo)
            dcsx = dy * f["yoff"]
            xdt = f["xdt"]
            dxdt = jnp.zeros((l, GROUP_W), F32)
            dcb = jnp.zeros((l, l), F32)
            for j in range(4):
                h = 4 * g + j
                lm = f["lms"][j]
                sc = f["cb"] * lm
                mask = _head_mask(j)
                ds_ = jnp.where(causal, _dot_nt(jnp.where(mask, dy, 0.0), xdt), 0.0)
                dxdt = jnp.where(mask, _dot_tn(sc, dy), dxdt)
                dcb = dcb + ds_ * lm
                m = ds_ * sc
                dcs_head = dcs_head + jnp.where(lane == h, jnp.sum(m, axis=1, keepdims=True), 0.0)
                dcs_rows = dcs_rows + jnp.where(head_row == h, jnp.sum(m, axis=0, keepdims=True), 0.0)
            dhn = dht[:, sl]
            etot = jnp.exp(f["totx"])
            dxd = _dot(bg, dhn)
            dbg = _dot_nt(xdt * f["dsx"], dhn)
            dxdt = dxdt + dxd * f["dsx"]
            qq = dxd * xdt * f["dsx"]
            dcsx = dcsx - qq
            dtot = jnp.sum(qq, axis=0, keepdims=True) + jnp.sum(dhn * ht_g, axis=0, keepdims=True) * etot
            dht[:, sl] = etot * dhn + dht_prev
            dcg = dcg + _dot(dcb, bg)
            dbg = dbg + _dot_tn(dcb, cg)
            dxa_ref[:, sl] = dxdt * f["dtx"] + dy * dxp_g
            dxa_ref[:, slb] = dbg
            dxa_ref[:, slc] = dcg
            dcsx_s[:, sl] = dcsx
            ddtx_s[:, sl] = dxdt * xs_g
            accx[2:3, sl] = dtot
        reduce = (jnp.right_shift(_iota((SSD_W, LANE), 0), 6) == _iota((SSD_W, LANE), 1)).astype(F32)
        triu = (_iota((l, l), 1) >= _iota((l, l), 0)).astype(F32)
        dtot = _dot01_r(accx[...], reduce)[2:3, :]
        dcs_head = dcs_head - dcs_rows.T
        da_head = _dot01(triu, dcs_head + _dot01_r(dcsx_s[...], reduce, parts=2)) + dtot
        ddt = _dot01_r(ddtx_s[...], reduce, parts=2) + da_head * q["a_head"]
        small_ref[1:2, :] += jnp.sum(da_head * q["dt"], axis=0, keepdims=True)
        ddtr = ddt * _sigmoid(dtr + bias_ref[...])
        ddt_ref[...] = ddtr.astype(BF16)
        small_ref[0:1, :] += jnp.sum(ddtr, axis=0, keepdims=True)

        @pl.when(step == nc - 1)
        def _():
            small_ref[1:2, :] = small_ref[1:2, :] * q["a_head"]
            small_ref[2:3, :] = _dot01_r(accx[...], reduce)[0:1, :]

    rev = lambda c: nc - 1 - c
    par = lambda w: pl.BlockSpec((1, w), lambda c: (0, 0))
    outs, jouts = _hosted(
        body, jobs, grid=(nc,),
        in_specs=[pl.BlockSpec((CHUNK, XBC), lambda c: (rev(c), 0)),
                  pl.BlockSpec((CHUNK, LANE), lambda c: (rev(c), COL_DT // LANE)),
                  pl.BlockSpec((CHUNK, SSD_W), lambda c: (rev(c), COL_Z // SSD_W)),
                  pl.BlockSpec((CHUNK, SSD_W), lambda c: (rev(c), 1)),
                  pl.BlockSpec((None, N_STATE, SSD_W), lambda c: (rev(c), 0, 0)),
                  par(LANE), par(LANE), par(SSD_W), par(SSD_W)],
        out_specs=(pl.BlockSpec((CHUNK, XBC), lambda c: (rev(c), 0)),
                   pl.BlockSpec((CHUNK, LANE), lambda c: (rev(c), 0)),
                   pl.BlockSpec((CHUNK, SSD_W), lambda c: (rev(c), 0)),
                   par(SSD_W), pl.BlockSpec((SUBLANE, LANE), lambda c: (0, 0))),
        out_shape=(jax.ShapeDtypeStruct((s, XBC), F32), jax.ShapeDtypeStruct((s, LANE), BF16),
                   jax.ShapeDtypeStruct((s, SSD_W), BF16), jax.ShapeDtypeStruct((1, SSD_W), F32),
                   jax.ShapeDtypeStruct((SUBLANE, LANE), F32)),
        scratch_shapes=[pltpu.VMEM((N_STATE, SSD_W), F32), pltpu.VMEM((CHUNK, LANE), F32),
                        pltpu.VMEM((SUBLANE, SSD_W), F32), pltpu.VMEM((CHUNK, SSD_W), F32),
                        pltpu.VMEM((CHUNK, SSD_W), F32)],
        name=name, args=(xact, proj, proj, dycat, hprev, bias_pad, alog_pad, dxp, normw))
    return (tuple(outs), jouts) if jobs else tuple(outs)


def _blockdiag(w):
    w2 = w.reshape(N_HEAD // 2, 2, HEAD_P, HEAD_P)
    z = jnp.zeros((N_HEAD // 2, HEAD_P, HEAD_P), w.dtype)
    top = jnp.concatenate([w2[:, 0], z], axis=2)
    bot = jnp.concatenate([z, w2[:, 1]], axis=2)
    return jnp.concatenate([top, bot], axis=1)


def _unblockdiag(wbd):
    a = wbd[:, :HEAD_P, :HEAD_P]
    b = wbd[:, HEAD_P:, HEAD_P:]
    return jnp.stack([a, b], axis=1).reshape(N_HEAD, HEAD_P, HEAD_P)


def _pad_rows8(w):
    return jnp.concatenate([w, jnp.zeros((SUBLANE - w.shape[0], w.shape[1]), w.dtype)], axis=0)


def _pad_lane(v):
    return jnp.concatenate([v, jnp.zeros((1, LANE - v.shape[1]), v.dtype)], axis=1)


class _NoExchange:
    def ride(self, host):
        return []

    def done(self, jobs, outs, w):
        pass

    def grad(self, name, val):
        pass

    def small(self, raw):
        pass

    def pairs_now(self):
        pass


def _local_step(x, p, tgt, w, hooks=_NoExchange()):
    cw_l = _pad_rows8(w["lru_conv_w"])
    cw_s = _pad_rows8(w["ssd_conv_w"])
    wa_bd = _blockdiag(w["lru_gate_a_w"])
    wx_bd = _blockdiag(w["lru_gate_x_w"])
    ba = w["lru_gate_a_b"].reshape(1, LRU_W)
    bx = w["lru_gate_x_b"].reshape(1, LRU_W)
    bias_pad = _pad_lane(w["ssd_dt_bias"])
    alog_pad = _pad_lane(w["ssd_a_log"])
    dxp = jnp.repeat(w["ssd_d"], HEAD_P, axis=1)

    def host(fn, *a, name, **k):
        jobs = hooks.ride(name)
        res = fn(*a, name=name, jobs=jobs, **k)
        if jobs:
            res, jouts = res
            hooks.done(jobs, jouts, w)
        return res

    def grad(n, val):
        g[n] = val
        hooks.grad(n, val)

    xb = x.astype(BF16)
    proj = host(_mm, xb, w["w_in_t"], "nt", tm=2048, tn=512, name="in_proj")
    ymix, h_lru = host(_lru_fwd, proj, cw_l, w["lru_conv_b"], wa_bd, ba, wx_bd, bx, w["lru_a_param"], name="lru_fwd")
    xact = host(_conv_silu_fwd, proj, cw_s, w["ssd_conv_b"], col0=COL_XBC, width=XBC, ct=256, name="ssd_conv_fwd")
    ycat, hprev = host(_ssd_fwd, xact, proj, ymix, bias_pad, alog_pad, dxp, w["ssd_norm_w"], name="ssd_fwd")
    mix, x1, x1b = _mm_ln(ycat, w["w_out"], x, w["ln1_g"], w["ln1_b"], tm=512, name="out_proj")
    pre = host(_mm, x1b, w["w_ff1"], "nn", tm=2048, tn=512, out_dtype=BF16, name="ff1")
    ff, x2, x2b = _mm_ln(pre, w["w_ff2"], x1, w["ln2_g"], w["ln2_b"], tm=512, a_fn=_relu2, name="ff2")
    loss, dgpre, dple, dt3, dg3, db3 = _head(x2, x2b, p, w["w_ple_gate"], w["w_ple"], w["ln3_g"], w["ln3_b"], tgt,
                                             name="head")

    g = {}
    g["ln3_g"], g["ln3_b"] = dg3, db3
    grad("w_ple_gate", _mm(x2b, dgpre, "tn", tm=512, tn=1024, out_dtype=BF16, name="d_w_ple_gate"))
    grad("w_ple", _mm(p, dple, "tn", tm=256, tn=512, dest_major=True, out_dtype=BF16, name="d_w_ple"))
    dt2, dt2b, g["ln2_g"], g["ln2_b"] = host(_mm_ln_bwd, dgpre, w["w_ple_gate"], x1, ff, w["ln2_g"], dt3, ALPHA,
                                             tm=512, name="d_x2")
    grad("w_ff2", host(_mm, pre, dt2b, "tn", tm=512, tn=1024, a_fn=_relu2, out_dtype=BF16, name="d_w_ff2"))
    dpre = host(_mm, dt2b, w["w_ff2"], "nt", tm=2048, tn=512, extra=pre, out_dtype=BF16,
                epi=lambda acc, pv: acc * 2.0 * jnp.maximum(pv.astype(F32), 0.0), name="d_pre")
    grad("w_ff1", host(_mm, x1b, dpre, "tn", tm=1024, tn=512, dest_major=True, out_dtype=BF16, name="d_w_ff1"))
    dt1, dt1b, g["ln1_g"], g["ln1_b"] = host(_mm_ln_bwd, dpre, w["w_ff1"], x, mix, w["ln1_g"], dt2, ALPHA,
                                             tm=256, name="d_x1")
    grad("w_out", host(_mm, ycat, dt1b, "tn", tm=512, tn=1024, out_dtype=BF16, name="d_w_out"))
    dycat = host(_mm, dt1b, w["w_out"], "nt", tm=2048, tn=512, name="d_ycat")
    dxl, dgl, dcwb_l, dwa, dwx = host(_lru_bwd, proj, dycat, h_lru, cw_l, w["lru_conv_b"], wa_bd, ba, wx_bd, bx,
                                      w["lru_a_param"], name="lru_bwd")
    g["lru_gate_a_w"] = _unblockdiag(dwa)
    g["lru_gate_x_w"] = _unblockdiag(dwx)
    raw = dict(lru=dcwb_l, gate_a=g["lru_gate_a_w"].reshape(N_HEAD * HEAD_P, HEAD_P).astype(BF16),
               gate_x=g["lru_gate_x_w"].reshape(N_HEAD * HEAD_P, HEAD_P).astype(BF16))
    hooks.small(raw)
    dxact, ddt, dz, g["ssd_norm_w"], small = host(_ssd_bwd, xact, proj, dycat, hprev, bias_pad, alog_pad, dxp,
                                                   w["ssd_norm_w"], name="ssd_bwd")
    dxbc, dcwb_s = host(_conv_silu_bwd, proj, dxact, cw_s, w["ssd_conv_b"], col0=COL_XBC, width=XBC, ct=256,
                        name="ssd_conv_bwd")
    pieces, offsets = [dxl, dgl, dz, dxbc, ddt], [0, COL_G, COL_Z, COL_XBC, COL_DT]

    g["lru_conv_w"] = dcwb_l[0:4]
    g["lru_conv_b"] = dcwb_l[4:5]
    g["lru_gate_a_b"] = dcwb_l[5:6]
    g["lru_gate_x_b"] = dcwb_l[6:7]
    g["lru_a_param"] = dcwb_l[7:8]
    g["ssd_conv_w"] = dcwb_s[0:4]
    g["ssd_conv_b"] = dcwb_s[4:5]
    g["ssd_dt_bias"] = small[0:1, :N_HEAD]
    g["ssd_a_log"] = small[1:2, :N_HEAD]
    g["ssd_d"] = small[2:3, :N_HEAD]
    rows = jnp.concatenate([g[n] for n in ("ssd_norm_w", "ln1_g", "ln1_b", "ln2_g", "ln2_b", "ln3_g", "ln3_b")]
                           + [jnp.broadcast_to(loss[:, 0:1], (1, D_MODEL))], axis=0)
    late = dict(ssd=dcwb_s, heads=small, rows=rows)
    hooks.small(late)
    raw.update(late)
    dwt = None
    for q, (pc, off) in enumerate(zip(pieces, offsets)):
        dwt = host(_mm, pc, xb, "tn", tm=512, tn=1024, out_dtype=BF16, into=(dwt, off, D_IN),
                   name="d_w_in_%d" % q)
    grad("w_in", dwt)
    hooks.pairs_now()
    grad_x = host(_mm_pieces, pieces, offsets, w["w_in_t"], tm=256, extra=dt1, epi=lambda acc, e: acc + ALPHA * e,
                  name="d_x")
    return loss[0, 0], grad_x, g, raw


ANY_SPEC = pl.BlockSpec(memory_space=pl.ANY)


def _mesh_pos():
    return lax.axis_index("x"), lax.axis_index("y"), lax.axis_index("c")


def _remote(src, dst, send, recv, k, to):
    return pltpu.make_async_remote_copy(src_ref=src, dst_ref=dst, send_sem=send.at[k], recv_sem=recv.at[k],
                                        device_id=to, device_id_type=MESH_T)


class _Job:
    N_SEM = 9

    def __init__(self, kind, inp):
        self.kind, self.inp = kind, inp
        shape = {"gather": (N_DEV,) + inp.shape, "relay": (N_DEV,) + inp.shape, "pair": (4,) + inp.shape[1:],
                 "chip": inp.shape}[kind]
        self.out = jax.ShapeDtypeStruct(shape, inp.dtype)
        self.top = (inp.shape[0] // 2) // 16 * 16

    def _blk(self, ref, k):
        return ref.at[k]

    def _relay_copies(self, inp, out, send, recv):
        x, y, c = _mesh_pos()
        sib, xn, yn, dg = (x, y, 1 - c), (1 - x, y, c), (x, 1 - y, c), (1 - x, 1 - y, c)
        blk = lambda p, cc=None: out.at[4 * p[0] + 2 * p[1] + (p[2] if cc is None else cc)]
        top = lambda r: r.at[pl.ds(0, self.top)]
        bot = lambda r: r.at[pl.ds(self.top, self.inp.shape[0] - self.top)]
        mine = blk((x, y, c))
        plan = [
            (inp, mine, sib, blk(sib)),
            (inp, mine, xn, blk(xn)),
            (inp, mine, yn, blk(yn)),
            (top(blk(xn)), top(blk(xn)), yn, top(blk(dg))),
            (bot(blk(yn)), bot(blk(yn)), xn, bot(blk(dg))),
            (blk(xn), blk(xn), sib, blk(xn, 1 - c)),
            (blk(yn), blk(yn), sib, blk(yn, 1 - c)),
            (top(blk(dg)), top(blk(dg)), sib, top(blk(dg, 1 - c))),
            (bot(blk(dg)), bot(blk(dg)), sib, bot(blk(dg, 1 - c))),
        ]
        me = (x, y, c)
        return [(_remote(s, d, send, recv, k, to), _remote(s, land, send, recv, k, me))
                for k, (s, d, to, land) in enumerate(plan)]

    def _places(self):
        x, y, c = _mesh_pos()
        return (x, y, c), (x, y, 1 - c), [(1 - x, y), (x, 1 - y), (1 - x, 1 - y)]

    def start(self, inp, out, send, recv, loc):
        me, sibling, chips = self._places()
        x, y, c = me
        if self.kind == "relay":
            pltpu.make_async_copy(inp, out.at[4 * x + 2 * y + c], loc.at[0]).start()
            cps = self._relay_copies(inp, out, send, recv)
            for k in (0, 1, 2):
                cps[k][0].start()
        elif self.kind == "gather":
            mine = out.at[4 * x + 2 * y + c]
            pltpu.make_async_copy(inp, mine, loc.at[0]).start()
            _remote(inp, mine, send, recv, 0, sibling).start()
            for j, chip in enumerate(chips):
                _remote(inp, mine, send, recv, 1 + j, (*chip, c)).start()
        elif self.kind == "pair":
            for k in range(4):
                _remote(inp.at[2 * k + (1 - c)], out.at[k], send, recv, k, sibling).start()
        else:
            kme = 2 * x + y
            pltpu.make_async_copy(self._blk(inp, kme), self._blk(out, kme), loc.at[0]).start()
            for j, (tx, ty) in enumerate(chips):
                _remote(self._blk(inp, 2 * tx + ty), self._blk(out, kme), send, recv, j, (tx, ty, c)).start()

    def mid(self, inp, out, send, recv, loc):
        if self.kind == "relay":
            cps = self._relay_copies(inp, out, send, recv)
            for k, onward in ((1, (3, 5)), (2, (4, 6))):
                cps[k][1].wait_recv()
                for q in onward:
                    cps[q][0].start()
            return
        if self.kind != "gather":
            return
        me, sibling, chips = self._places()
        c = me[2]
        for j, chip in enumerate(chips):
            landed = out.at[4 * chip[0] + 2 * chip[1] + c]
            _remote(landed, landed, send, recv, 1 + j, me).wait_recv()
            _remote(landed, landed, send, recv, 4 + j, sibling).start()

    def finish(self, inp, out, send, recv, loc):
        me, sibling, chips = self._places()
        x, y, c = me
        if self.kind == "relay":
            cps = self._relay_copies(inp, out, send, recv)
            for k, onward in ((3, 7), (4, 8)):
                cps[k][1].wait_recv()
                cps[onward][0].start()
            for k in (0, 5, 6, 7, 8):
                cps[k][1].wait_recv()
            for k in range(9):
                cps[k][0].wait_send()
            pltpu.make_async_copy(inp, out.at[4 * x + 2 * y + c], loc.at[0]).wait()
        elif self.kind == "gather":
            blk = lambda px, py, pc: out.at[4 * px + 2 * py + pc]
            mine = blk(*me)
            _remote(inp, blk(*sibling), send, recv, 0, me).wait_recv()
            for j, chip in enumerate(chips):
                _remote(inp, blk(*chip, 1 - c), send, recv, 4 + j, me).wait_recv()
            for k in range(7):
                _remote(inp, mine, send, recv, k, sibling).wait_send()
            pltpu.make_async_copy(inp, mine, loc.at[0]).wait()
        elif self.kind == "pair":
            for k in range(4):
                _remote(inp.at[2 * k + (1 - c)], out.at[k], send, recv, k, sibling).wait()
        else:
            kme = 2 * x + y
            for j, (tx, ty) in enumerate(chips):
                _remote(self._blk(inp, kme), self._blk(out, 2 * tx + ty), send, recv, j, (tx, ty, c)).wait_recv()
            for j, (tx, ty) in enumerate(chips):
                _remote(self._blk(inp, 2 * tx + ty), self._blk(out, kme), send, recv, j, (tx, ty, c)).wait_send()
            pltpu.make_async_copy(self._blk(inp, kme), self._blk(out, kme), loc.at[0]).wait()


def _job_scratch(jobs):
    sem = pltpu.SemaphoreType.DMA
    return [s for _ in jobs for s in (sem((_Job.N_SEM,)), sem((_Job.N_SEM,)), sem((1,)))]


def _run_jobs(jobs, method, jins, jouts, jsems, only=None):
    for q, job in enumerate(jobs):
        if only is None or only[q]:
            getattr(job, method)(jins[q], jouts[q], *jsems[3 * q:3 * q + 3])


def _exchange(jobs, *, name):
    n = len(jobs)

    def body(*refs):
        jins, jouts, jsems = refs[:n], refs[n:2 * n], refs[2 * n:]
        _run_jobs(jobs, "start", jins, jouts, jsems)
        _run_jobs(jobs, "mid", jins, jouts, jsems)
        _run_jobs(jobs, "finish", jins, jouts, jsems)

    return _pcall(body, in_specs=[ANY_SPEC] * n, out_specs=[ANY_SPEC] * n, out_shape=[j.out for j in jobs],
                  scratch_shapes=_job_scratch(jobs), name=name)(*[j.inp for j in jobs])


def _hosted(body, jobs, *, grid, in_specs, out_specs, out_shape, args, name, scratch_shapes=(), aliases=None):
    in_specs, out_specs, out_shape = list(in_specs), list(out_specs), list(out_shape)
    scratch_shapes = list(scratch_shapes)
    n_in, n_out, n_scr, nj = len(in_specs), len(out_specs), len(scratch_shapes), len(jobs)
    sem = ("arbitrary",) * len(grid)
    kw = dict(input_output_aliases=aliases) if aliases else {}
    if not jobs:
        res = _pcall(body, grid=grid, in_specs=in_specs, out_specs=out_specs, out_shape=out_shape,
                     scratch_shapes=scratch_shapes, name=name, compiler_params=_cparams(sem), **kw)(*args)
        return list(res), []

    def full(*refs):
        ins, jins = refs[:n_in], refs[n_in:n_in + nj]
        o0 = n_in + nj
        outs, jouts = refs[o0:o0 + n_out], refs[o0 + n_out:o0 + n_out + nj]
        s0 = o0 + n_out + nj
        scr, jsems = refs[s0:s0 + n_scr], refs[s0 + n_scr:]
        step = pl.program_id(0)
        for ax in range(1, len(grid)):
            step = step * grid[ax] + pl.program_id(ax)
        total = math.prod(grid)
        early = [job.kind == "relay" for job in jobs]
        mid_step = (3 * total) // 5
        split = any(early) and 0 < mid_step < total - 1

        @pl.when(step == 0)
        def _():
            _run_jobs(jobs, "start", jins, jouts, jsems)

        if split:
            @pl.when(step == mid_step)
            def _():
                _run_jobs(jobs, "mid", jins, jouts, jsems, only=early)

        body(*ins, *outs, *scr)

        @pl.when(step == total - 1)
        def _():
            _run_jobs(jobs, "mid", jins, jouts, jsems, only=[not e for e in early] if split else None)
            _run_jobs(jobs, "finish", jins, jouts, jsems)

    res = _pcall(full, grid=grid, in_specs=in_specs + [ANY_SPEC] * nj, out_specs=out_specs + [ANY_SPEC] * nj,
                 out_shape=out_shape + [j.out for j in jobs], scratch_shapes=scratch_shapes + _job_scratch(jobs),
                 name=name, compiler_params=_cparams(sem), **kw)(*args, *[j.inp for j in jobs])
    return list(res[:n_out]), list(res[n_out:])


def _pair_add(g8, r4, cidx, *, name):
    _, r, c = g8.shape
    tr = ROW_TILE if r % ROW_TILE == 0 else r

    def body(c_ref, g_ref, r_ref, o_ref):
        o_ref[...] = (g_ref[...].astype(F32) + r_ref[...].astype(F32)).astype(BF16)

    return _pcall(
        body,
        grid_spec=pltpu.PrefetchScalarGridSpec(
            num_scalar_prefetch=1, grid=(4, r // tr),
            in_specs=[pl.BlockSpec((None, tr, c), lambda k, i, cr: (2 * k + cr[0], i, 0)),
                      pl.BlockSpec((None, tr, c), lambda k, i, cr: (k, i, 0))],
            out_specs=pl.BlockSpec((None, tr, c), lambda k, i, cr: (k, i, 0))),
        out_shape=jax.ShapeDtypeStruct((4, r, c), BF16), name=name,
        compiler_params=_cparams(("parallel", "parallel")))(cidx, g8, r4)


def _adam_update(g, w_ref, m_ref, v_ref, g_ref, d_ref, mo_ref, vo_ref):
    c1 = 1.0 - ADAM_B1 ** ADAM_STEP
    c2 = 1.0 - ADAM_B2 ** ADAM_STEP
    m2 = ADAM_B1 * m_ref[...] + (1.0 - ADAM_B1) * g
    v2 = ADAM_B2 * v_ref[...] + (1.0 - ADAM_B2) * (g * g)
    g_ref[...] = g
    mo_ref[...] = m2
    vo_ref[...] = v2
    d_ref[...] = -ADAM_LR * ((m2 / c1) / (jnp.sqrt(v2 / c2) + ADAM_EPS) + ADAM_WD * w_ref[...])


def _adamw_rows(srcs, items, own_cols, me1, loss_row, *, name):
    ns, ni, no = len(srcs), len(items), len(own_cols)
    full = lambda a: pl.BlockSpec(a.shape, lambda i, me: (0,) * a.ndim)
    in_specs = [full(a) for a in srcs]
    args = list(srcs)
    for (si, _r0, w, _m, _v) in own_cols:
        a = srcs[si]
        in_specs.append(pl.BlockSpec((N_DEV, a.shape[1], w.shape[1]), lambda i, me: (0, 0, me[0])))
        args.append(a)
    out_specs, out_shape = [], []
    for (_si, _r0, w, m, v) in list(items) + list(own_cols):
        in_specs += [full(w)] * 3
        args += [w, m, v]
        out_specs += [full(w)] * 4
        out_shape += [jax.ShapeDtypeStruct(w.shape, F32)] * 4
    out_specs.append(pl.BlockSpec((1, LANE), lambda i, me: (0, 0)))
    out_shape.append(jax.ShapeDtypeStruct((1, LANE), F32))

    def body(me_ref, *refs):
        src_refs, own_refs = refs[:ns], refs[ns:ns + no]
        wmv = refs[ns + no:ns + no + 3 * (ni + no)]
        outs = refs[ns + no + 3 * (ni + no):]
        lsrc, lrow = src_refs[loss_row[0]], loss_row[1]
        total = lsrc[0, lrow:lrow + 1, 0:LANE]
        for d in range(1, N_DEV):
            total = total + lsrc[d, lrow:lrow + 1, 0:LANE]
        outs[-1][...] = total
        for q, (si, r0, w, _m, _v) in enumerate(list(items) + list(own_cols)):
            nr, cw = w.shape
            gref = src_refs[si] if q < ni else own_refs[q - ni]
            g = gref[0, r0:r0 + nr, 0:cw]
            for d in range(1, N_DEV):
                g = g + gref[d, r0:r0 + nr, 0:cw]
            _adam_update(g, *wmv[3 * q:3 * q + 3], *outs[4 * q:4 * q + 4])

    res = _pcall(
        body,
        grid_spec=pltpu.PrefetchScalarGridSpec(num_scalar_prefetch=1, grid=(1,), in_specs=in_specs, out_specs=out_specs),
        out_shape=out_shape, name=name, compiler_params=_cparams(("arbitrary",)))(me1, *args)
    return [tuple(res[4 * q:4 * q + 4]) for q in range(ni + no)], res[-1]


def _adamw(gsrc, w, m, v, *, name):
    k, r, c = gsrc.shape
    tr = ROW_TILE if r % ROW_TILE == 0 else r

    def body(gs_ref, w_ref, m_ref, v_ref, g_ref, d_ref, mo_ref, vo_ref):
        g = gs_ref[0].astype(F32)
        for q in range(1, k):
            g = g + gs_ref[q].astype(F32)
        _adam_update(g, w_ref, m_ref, v_ref, g_ref, d_ref, mo_ref, vo_ref)

    tc = c
    if tr == r and r > ROW_TILE and c % 256 == 0:
        tc = 256
    blk = pl.BlockSpec((tr, tc), lambda i, j: (i, j))
    sd = jax.ShapeDtypeStruct((r, c), F32)
    return _pcall(body, grid=(r // tr, c // tc),
                  in_specs=[pl.BlockSpec((k, tr, tc), lambda i, j: (0, i, j)), blk, blk, blk],
                  out_specs=(blk, blk, blk, blk), out_shape=(sd, sd, sd, sd), name=name,
                  compiler_params=_cparams(("parallel", "parallel")))(gsrc, w, m, v)


WEIGHTS = ['w_in', 'lru_conv_w', 'lru_conv_b', 'lru_gate_a_w', 'lru_gate_a_b', 'lru_gate_x_w', 'lru_gate_x_b',
           'lru_a_param', 'ssd_conv_w', 'ssd_conv_b', 'ssd_dt_bias', 'ssd_a_log', 'ssd_d', 'ssd_norm_w', 'w_out',
           'ln1_g', 'ln1_b', 'w_ff1', 'w_ff2', 'ln2_g', 'ln2_b', 'w_ple_gate', 'w_ple', 'ln3_g', 'ln3_b']
BIG = ['w_in', 'w_out', 'w_ff1', 'w_ff2', 'w_ple_gate', 'w_ple']
COL_SHARDED = ('w_ff1', 'w_ple')
CONV = ['lru_conv_w', 'ssd_conv_w']
REPL = [n for n in WEIGHTS if n not in BIG and n not in CONV]
CONV_CH = {'lru_conv_w': LRU_W, 'ssd_conv_w': XBC}


def _to_dest_major(name, gfull):
    if name in COL_SHARDED:
        r, cfull = gfull.shape
        return gfull.reshape(r, N_DEV, cfull // N_DEV).transpose(1, 0, 2)
    rfull, cdim = gfull.shape
    return gfull.reshape(N_DEV, rfull // N_DEV, cdim)


def _full_weight(name, gathered):
    if name in COL_SHARDED:
        _, r, cs = gathered.shape
        full = gathered.transpose(1, 0, 2).reshape(r, N_DEV * cs)
    else:
        _, rs, cdim = gathered.shape
        full = gathered.reshape(N_DEV * rs, cdim)
    if name == 'w_in':
        full = lax.dynamic_update_slice(jnp.zeros((D_IN_PAD, D_MODEL), full.dtype), full, (0, 0))
    return full


SMALL_SRC = ("lru", "ssd", "heads", "rows", "gate_a", "gate_x")
AG_HOSTS = {"in_proj": ("w_ff1",), "lru_fwd": ("w_ff2",), "ssd_fwd": ("w_out",), "ff1": ("w_ple_gate", "w_ple")}
PAIR_HOSTS = ("d_x2", "d_pre", "d_x1", "d_ycat")
CHIP_HOSTS = {"lru_bwd": ("w_ple_gate", "w_ple", "w_ff2"), "ssd_bwd": ("w_ff1",), "ssd_conv_bwd": ("w_out",),
              "d_x": ("w_in",)}
SMALL_HOSTS = {"ssd_bwd": ("lru", "gate_a", "gate_x"), "d_w_in_3": ("ssd", "heads", "rows")}


class _Schedule:
    def __init__(self, shards, cidx):
        self.shards, self.cidx = shards, cidx
        self.pair, self.chip, self.small_jobs = [], [], []
        self.dest, self.summed, self.gathered_small = {}, {}, {}
        self.tags = []

    def ride(self, host):
        tags = []
        if host in AG_HOSTS:
            tags = [("weight", n, self.shards[n]) for n in AG_HOSTS[host]]
        elif host in PAIR_HOSTS or host in CHIP_HOSTS or host == "flush":
            tags = [("pair", n, a) for n, a in self.pair]
            self.pair = []
            if host not in PAIR_HOSTS:
                take = [t for t in self.chip if host == "flush" or t[0] in CHIP_HOSTS[host]]
                tags += [("chip", n, a) for n, a in take]
                self.chip = [t for t in self.chip if not any(t is u for u in take)]
        if host in SMALL_HOSTS:
            tags += [("small", n, a) for n, a in self.small_jobs if n in SMALL_HOSTS[host]]
            self.small_jobs = [t for t in self.small_jobs if t[0] not in SMALL_HOSTS[host]]
        self.tags = tags
        return [_Job({"weight": "relay", "small": "gather"}.get(kind, kind), a) for kind, _n, a in tags]

    def done(self, jobs, outs, w):
        for (kind, n, _a), o in zip(self.tags, outs):
            if kind == "weight":
                w[n] = _full_weight(n, o)
            elif kind == "small":
                self.gathered_small[n] = o
            elif kind == "pair":
                self.chip.append((n, _pair_add(self.dest[n], o, self.cidx, name="rs_pair_add_" + n)))
            else:
                self.summed[n] = o

    def grad(self, name, val):
        self.dest[name] = val if val.ndim == 3 else _to_dest_major(name, val)
        self.pair.append((name, self.dest[name]))

    def small(self, raw):
        self.small_jobs += list(raw.items())

    def pairs_now(self):
        tags = [("pair", n, a) for n, a in self.pair]
        self.pair, self.tags = [], tags
        jobs = [_Job("pair", a) for _k, _n, a in tags]
        self.done(jobs, _exchange(jobs, name="rs_pairs_now"), None)

    def flush(self):
        step = 0
        while self.pair or self.chip:
            jobs = self.ride("flush")
            self.done(jobs, _exchange(jobs, name="rs_flush_%d" % step), None)
            step += 1


def kernel(x, p, w_in, lru_conv_w, lru_conv_b, lru_gate_a_w, lru_gate_a_b, lru_gate_x_w, lru_gate_x_b, lru_a_param, ssd_conv_w, ssd_conv_b, ssd_dt_bias, ssd_a_log, ssd_d, ssd_norm_w, w_out, ln1_g, ln1_b, w_ff1, w_ff2, ln2_g, ln2_b, w_ple_gate, w_ple, ln3_g, ln3_b, loss_target, m_w_in, m_lru_conv_w, m_lru_conv_b, m_lru_gate_a_w, m_lru_gate_a_b, m_lru_gate_x_w, m_lru_gate_x_b, m_lru_a_param, m_ssd_conv_w, m_ssd_conv_b, m_ssd_dt_bias, m_ssd_a_log, m_ssd_d, m_ssd_norm_w, m_w_out, m_ln1_g, m_ln1_b, m_w_ff1, m_w_ff2, m_ln2_g, m_ln2_b, m_w_ple_gate, m_w_ple, m_ln3_g, m_ln3_b, v_w_in, v_lru_conv_w, v_lru_conv_b, v_lru_gate_a_w, v_lru_gate_a_b, v_lru_gate_x_w, v_lru_gate_x_b, v_lru_a_param, v_ssd_conv_w, v_ssd_conv_b, v_ssd_dt_bias, v_ssd_a_log, v_ssd_d, v_ssd_norm_w, v_w_out, v_ln1_g, v_ln1_b, v_w_ff1, v_w_ff2, v_ln2_g, v_ln2_b, v_w_ple_gate, v_w_ple, v_ln3_g, v_ln3_b):
    given = dict(locals())
    def local(a, n):
        return jnp.swapaxes(a[0], 0, 1) if n == 'w_in' else a[0]

    wsh = {n: local(given[n], n) for n in WEIGHTS}
    msh = {n: local(given["m_" + n], n) for n in WEIGHTS}
    vsh = {n: local(given["v_" + n], n) for n in WEIGHTS}
    xi, yi, ci = _mesh_pos()
    me = 4 * xi + 2 * yi + ci

    shards = {n: wsh[n].astype(BF16) for n in BIG}
    conv_pack = jnp.concatenate([_pad_rows8(wsh[n]) for n in CONV], axis=1)
    g_in, gconv = _exchange([_Job("relay", shards['w_in']), _Job("gather", conv_pack)], name="ag_first")
    full = {'w_in_t': _full_weight('w_in', g_in)}
    c0 = 0
    for n in CONV:
        cw = CONV_CH[n] // N_DEV
        full[n] = gconv[:, :4, c0:c0 + cw].transpose(1, 0, 2).reshape(4, CONV_CH[n])
        c0 += cw
    for n in REPL:
        full[n] = given[n] if given[n].ndim == 2 else wsh[n]

    sched = _Schedule(shards, jnp.reshape(ci, (1,)).astype(jnp.int32))
    loss_local, grad_x, g, raw = _local_step(x[0], p[0, 0], loss_target[0], full, sched)
    sched.flush()
    summed, gat = sched.summed, sched.gathered_small

    outs = {}
    for n in BIG:
        outs[n] = _adamw(summed[n], wsh[n], msh[n], vsh[n], name="adamw_" + n)
    for n, k in (("lru_gate_a_w", "gate_a"), ("lru_gate_x_w", "gate_x")):
        flat = lambda a: a.reshape(N_HEAD * HEAD_P, HEAD_P)
        res = _adamw(gat[k], flat(wsh[n]), flat(msh[n]), flat(vsh[n]), name="adamw_" + n)
        outs[n] = tuple(r.reshape(N_HEAD, HEAD_P, HEAD_P) for r in res)
    for n, row in (("lru_gate_a_b", 5), ("lru_gate_x_b", 6)):
        outs[n] = _adamw(gat["lru"][:, row].reshape(N_DEV, N_HEAD, HEAD_P), wsh[n], msh[n], vsh[n], name="adamw_" + n)
    row_items = [("lru_conv_b", 0, 4), ("lru_a_param", 0, 7),
                 ("ssd_conv_b", 1, 4), ("ssd_dt_bias", 2, 0), ("ssd_a_log", 2, 1), ("ssd_d", 2, 2),
                 ("ssd_norm_w", 3, 0), ("ln1_g", 3, 1), ("ln1_b", 3, 2), ("ln2_g", 3, 3), ("ln2_b", 3, 4),
                 ("ln3_g", 3, 5), ("ln3_b", 3, 6)]
    vec = lambda a: a.reshape(1, -1)
    items = [(si, r0, vec(given[n]), vec(given["m_" + n]), vec(given["v_" + n])) for n, si, r0 in row_items]
    own = [(si, 0, wsh[n], msh[n], vsh[n]) for n, si in (("lru_conv_w", 0), ("ssd_conv_w", 1))]
    me1 = jnp.reshape(me, (1,)).astype(jnp.int32)
    res, loss_row = _adamw_rows([gat[k] for k in SMALL_SRC[:4]], items, own, me1, (3, 7), name="adamw_small")
    loss = loss_row[0, 0]
    for (n, _si, _r0), r4 in zip(row_items, res[:len(row_items)]):
        outs[n] = r4
    for n, r4 in zip(CONV, res[len(row_items):]):
        outs[n] = r4

    def fin(n, k):
        a = jnp.swapaxes(outs[n][k], 0, 1) if n == 'w_in' else outs[n][k]
        return a.reshape(given[n].shape)

    return (loss, grad_x[None],
            *[fin(n, 0) for n in WEIGHTS], *[fin(n, 1) for n in WEIGHTS],
            *[fin(n, 2) for n in WEIGHTS], *[fin(n, 3) for n in WEIGHTS])
```

```python
import math

import jax
import jax.numpy as jnp
from jax import lax
from jax.experimental import pallas as pl
from jax.experimental.pallas import tpu as pltpu

F32 = jnp.float32
BF16 = jnp.bfloat16

N_DEV = 8
D_MODEL = 1024
LRU_W = 1024
SSD_W = 1024
XBC = 2048
N_HEAD = 16
HEAD_P = 64
N_GROUP = 4
GROUP_W = 256
N_STATE = 128
CHUNK = 128
D_IN = 5136
D_IN_PAD = 5632
COL_G = 1024
COL_Z = 2048
COL_XBC = 3072
COL_DT = 5120
LRU_C = 8.0
ALPHA = 2.0 ** 0.25
LN_EPS = 1e-5
RMS_EPS = 1e-5
ADAM_LR = 0.001
ADAM_B1 = 0.9
ADAM_B2 = 0.999
ADAM_EPS = 1e-08
ADAM_WD = 0.01
ADAM_STEP = 10
GELU_C = math.sqrt(2.0 / math.pi)
LANE = 128
SUBLANE = 8
VMEM_LIMIT = 48 * 1024 * 1024
MESH_T = pl.DeviceIdType.MESH
NEG_BIG = -1e30


def _pcall(body, **kw):
    return pl.pallas_call(body, **kw)


def _cparams(sem):
    return pltpu.CompilerParams(dimension_semantics=sem, vmem_limit_bytes=VMEM_LIMIT)


def _dot(a, b):
    return jnp.dot(a.astype(BF16), b.astype(BF16), preferred_element_type=F32)


def _dot_nt(a, b):
    return lax.dot_general(a.astype(BF16), b.astype(BF16), (((1,), (1,)), ((), ())), preferred_element_type=F32)


def _dot_tn(a, b):
    return lax.dot_general(a.astype(BF16), b.astype(BF16), (((0,), (0,)), ((), ())), preferred_element_type=F32)


def _sigmoid(x):
    return jax.nn.sigmoid(x)


def _softplus(v):
    return jnp.maximum(v, 0.0) + jnp.log1p(jnp.exp(-jnp.abs(v)))


def _gelu(x):
    th = jnp.tanh(GELU_C * (x + 0.044715 * x * x * x))
    return 0.5 * x * (1.0 + th), th


def _gelu_grad(x, th):
    return 0.5 * (1.0 + th) + 0.5 * x * (1.0 - th * th) * GELU_C * (1.0 + 3.0 * 0.044715 * x * x)


def _iota(shape, dim):
    return lax.broadcasted_iota(jnp.int32, shape, dim)


def _mm(a, b, mode, *, tm, tn, name, a_fn=None, extra=None, epi=None, out_dtype=F32, dest_major=False, into=None,
        jobs=()):
    m = a.shape[1] if mode == "tn" else a.shape[0]
    n = b.shape[0] if mode == "nt" else b.shape[1]
    tm, tn = min(tm, m), min(tn, n)
    if dest_major:
        tn = n // N_DEV
    if mode == "nn":
        m, k = a.shape
        _, n = b.shape
        a_spec = pl.BlockSpec((tm, k), lambda i, j: (i, 0))
        b_spec = pl.BlockSpec((k, tn), lambda i, j: (0, j))
        dims = ((1,), (0,))
    elif mode == "nt":
        m, k = a.shape
        n, _ = b.shape
        a_spec = pl.BlockSpec((tm, k), lambda i, j: (i, 0))
        b_spec = pl.BlockSpec((tn, k), lambda i, j: (j, 0))
        dims = ((1,), (1,))
    else:
        k, m = a.shape
        _, n = b.shape
        a_spec = pl.BlockSpec((k, tm), lambda i, j: (0, i))
        b_spec = pl.BlockSpec((k, tn), lambda i, j: (0, j))
        dims = ((0,), (0,))
    assert m % tm == 0 and n % tn == 0, (name, m, n, tm, tn)
    o_spec = pl.BlockSpec((tm, tn), lambda i, j: (i, j))
    in_specs = [a_spec, b_spec]
    args = [a, b]
    if extra is not None:
        in_specs.append(o_spec)
        args.append(extra)

    def body(*refs):
        a_ref, b_ref, o_ref = refs[0], refs[1], refs[-1]
        av = a_ref[...]
        if a_fn is not None:
            av = a_fn(av)
        acc = lax.dot_general(av.astype(BF16), b_ref[...].astype(BF16), (dims, ((), ())), preferred_element_type=F32)
        if epi is not None:
            acc = epi(acc, refs[2][...])
        o_ref[...] = acc.astype(out_dtype)

    out_shape = jax.ShapeDtypeStruct((m, n), out_dtype)
    aliases = None
    if dest_major:
        assert extra is None
        o_spec = pl.BlockSpec((None, tm, tn), lambda i, j: (j, i, 0))
        out_shape = jax.ShapeDtypeStruct((N_DEV, m, tn), out_dtype)
    if into is not None:
        buf, row0, total = into
        assert extra is None and row0 % tm == 0
        o_spec = pl.BlockSpec((tm, tn), lambda i, j: (row0 // tm + i, j))
        out_shape = jax.ShapeDtypeStruct((total, n), out_dtype)
        if buf is not None:
            in_specs.append(ANY_SPEC)
            args.append(buf)
            aliases = {len(args) - 1: 0}
    (out,), jouts = _hosted(body, jobs, grid=(m // tm, n // tn), in_specs=in_specs, out_specs=[o_spec],
                            out_shape=[out_shape], args=args, name=name, aliases=aliases)
    return (out, jouts) if jobs else out


def _mm_pieces(pieces, offsets, b, *, tm, name, extra, epi, jobs=()):
    m = pieces[0].shape[0]
    kb, n = b.shape
    tm = min(tm, m)
    row = lambda wdt: pl.BlockSpec((tm, wdt), lambda i: (i, 0))
    in_specs = [row(pc.shape[1]) for pc in pieces] + [pl.BlockSpec((kb, n), lambda i: (0, 0)), row(n)]
    np_ = len(pieces)

    def body(*refs):
        b_ref, e_ref, o_ref = refs[np_], refs[np_ + 1], refs[np_ + 2]
        acc = jnp.zeros((tm, n), F32)
        for q in range(np_):
            kq = pieces[q].shape[1]
            acc = acc + jnp.dot(refs[q][...].astype(BF16), b_ref[offsets[q]:offsets[q] + kq, :].astype(BF16),
                                preferred_element_type=F32)
        o_ref[...] = epi(acc, e_ref[...])

    (out,), jouts = _hosted(body, jobs, grid=(m // tm,), in_specs=in_specs, out_specs=[row(n)],
                            out_shape=[jax.ShapeDtypeStruct((m, n), F32)], args=list(pieces) + [b, extra], name=name)
    return (out, jouts) if jobs else out


def _relu2(v):
    r = jnp.maximum(v, 0.0)
    return r * r


ROW_TILE = 256


def _ln_stats(t):
    mu = jnp.mean(t, axis=-1, keepdims=True)
    xc = t - mu
    var = jnp.mean(xc * xc, axis=-1, keepdims=True)
    rstd = lax.rsqrt(var + LN_EPS)
    return xc * rstd, rstd


def _ln_bwd_rows(dy, xhat, rstd, g):
    dxh = dy * g
    m1 = jnp.mean(dxh, axis=-1, keepdims=True)
    m2 = jnp.mean(dxh * xhat, axis=-1, keepdims=True)
    return rstd * (dxh - m1 - xhat * m2)


def _mm_ln(a, b, res, g, beta, *, tm, name, a_fn=None):
    m, k = a.shape
    d = b.shape[1]
    tm = min(tm, m)
    row = pl.BlockSpec((tm, d), lambda i: (i, 0))
    par = pl.BlockSpec((1, d), lambda i: (0, 0))

    def body(a_ref, b_ref, r_ref, g_ref, be_ref, br_ref, y_ref, yb_ref):
        av = a_ref[...]
        if a_fn is not None:
            av = a_fn(av)
        acc = jnp.dot(av.astype(BF16), b_ref[...].astype(BF16), preferred_element_type=F32)
        br_ref[...] = acc
        xhat, _ = _ln_stats(ALPHA * r_ref[...] + acc)
        y = xhat * g_ref[...] + be_ref[...]
        y_ref[...] = y
        yb_ref[...] = y.astype(BF16)

    sd = jax.ShapeDtypeStruct((m, d), F32)
    return _pcall(body, grid=(m // tm,),
                  in_specs=[pl.BlockSpec((tm, k), lambda i: (i, 0)), pl.BlockSpec((k, d), lambda i: (0, 0)), row, par, par],
                  out_specs=(row, row, row), out_shape=(sd, sd, jax.ShapeDtypeStruct((m, d), BF16)), name=name,
                  compiler_params=_cparams(("parallel",)))(a, b, res, g, beta)


def _mm_ln_bwd(a, b, res, branch, g, dy0, coef0, *, tm, name, jobs=()):
    m, k = a.shape
    d = b.shape[0]
    tm = min(tm, m)
    row = pl.BlockSpec((tm, d), lambda i: (i, 0))
    par = pl.BlockSpec((1, d), lambda i: (0, 0))

    def body(a_ref, b_ref, r_ref, br_ref, g_ref, dy0_ref, dt_ref, dtb_ref, dg_ref, db_ref):
        acc = lax.dot_general(a_ref[...].astype(BF16), b_ref[...].astype(BF16), (((1,), (1,)), ((), ())),
                              preferred_element_type=F32)
        dy = coef0 * dy0_ref[...] + acc
        xhat, rstd = _ln_stats(ALPHA * r_ref[...] + br_ref[...])
        dt = _ln_bwd_rows(dy, xhat, rstd, g_ref[...])
        dt_ref[...] = dt
        dtb_ref[...] = dt.astype(BF16)

        @pl.when(pl.program_id(0) == 0)
        def _():
            dg_ref[...] = jnp.zeros_like(dg_ref)
            db_ref[...] = jnp.zeros_like(db_ref)

        dg_ref[...] += jnp.sum(dy * xhat, axis=0, keepdims=True)
        db_ref[...] += jnp.sum(dy, axis=0, keepdims=True)

    pd = jax.ShapeDtypeStruct((1, d), F32)
    outs, jouts = _hosted(
        body, jobs, grid=(m // tm,),
        in_specs=[pl.BlockSpec((tm, k), lambda i: (i, 0)), pl.BlockSpec((d, k), lambda i: (0, 0)), row, row, par, row],
        out_specs=(row, row, par, par),
        out_shape=(jax.ShapeDtypeStruct((m, d), F32), jax.ShapeDtypeStruct((m, d), BF16), pd, pd),
        args=(a, b, res, branch, g, dy0), name=name)
    return (tuple(outs), jouts) if jobs else tuple(outs)


def _head(x2, x2b, p, wg, wp, g, beta, tgt, *, name):
    s, d = x2.shape
    tile = 2 * ROW_TILE
    row = pl.BlockSpec((tile, d), lambda i: (i, 0))
    par = pl.BlockSpec((1, d), lambda i: (0, 0))
    lsp = pl.BlockSpec((1, LANE), lambda i: (0, 0))
    whole = lambda a: pl.BlockSpec(a.shape, lambda i: (0, 0))

    def body(x2_ref, x2b_ref, p_ref, wg_ref, wp_ref, g_ref, be_ref, t_ref,
             loss_ref, dgp_ref, dple_ref, dt_ref, dg_ref, db_ref):
        gate = _sigmoid(_dot(x2b_ref[...], wg_ref[...]))
        ple_v = _dot(p_ref[...], wp_ref[...])
        xhat, rstd = _ln_stats(ALPHA * x2_ref[...] + gate * ple_v)
        err = xhat * g_ref[...] + be_ref[...] - t_ref[...]
        dy = err * (1.0 / d)
        dt = _ln_bwd_rows(dy, xhat, rstd, g_ref[...])
        dt_ref[...] = dt
        dgp_ref[...] = (dt * ple_v * gate * (1.0 - gate)).astype(BF16)
        dple_ref[...] = (dt * gate).astype(BF16)

        @pl.when(pl.program_id(0) == 0)
        def _():
            loss_ref[...] = jnp.zeros_like(loss_ref)
            dg_ref[...] = jnp.zeros_like(dg_ref)
            db_ref[...] = jnp.zeros_like(db_ref)

        loss_ref[...] += 0.5 * jnp.sum(jnp.mean(err * err, axis=-1, keepdims=True))
        dg_ref[...] += jnp.sum(dy * xhat, axis=0, keepdims=True)
        db_ref[...] += jnp.sum(dy, axis=0, keepdims=True)

    sd = jax.ShapeDtypeStruct((s, d), F32)
    sb = jax.ShapeDtypeStruct((s, d), BF16)
    pd = jax.ShapeDtypeStruct((1, d), F32)
    return _pcall(body, grid=(s // tile,),
                  in_specs=[row, row, pl.BlockSpec((tile, p.shape[1]), lambda i: (i, 0)), whole(wg), whole(wp), par, par,
                            row],
                  out_specs=(lsp, row, row, row, par, par),
                  out_shape=(jax.ShapeDtypeStruct((1, LANE), F32), sb, sb, sd, pd, pd),
                  name=name, compiler_params=_cparams(("arbitrary",)))(x2, x2b, p, wg, wp, g, beta, tgt)


CONV_R = 256
PAD = SUBLANE


def _shift_down(ext, s):
    if s == 0:
        return ext[PAD:, :]
    return pltpu.roll(ext, s, 0)[PAD:, :]


def _shift_up(ext, s):
    r = ext.shape[0] - PAD
    if s == 0:
        return ext[:r, :]
    return pltpu.roll(ext, r + PAD - s, 0)[:r, :]


def _conv_rows(xpad_ref, r0, w_ref):
    ext = xpad_ref[pl.ds(r0, CONV_R + PAD), :]
    acc = _shift_down(ext, 0) * w_ref[3:4, :]
    for k in range(3):
        acc = acc + _shift_down(ext, 3 - k) * w_ref[k:k + 1, :]
    return acc, ext


def _fill_front_padded(dst_ref, src_ref, s):
    dst_ref[0:PAD, :] = jnp.zeros((PAD, dst_ref.shape[1]), F32)

    def cp(q, _):
        r0 = pl.multiple_of(q * CONV_R, CONV_R)
        dst_ref[pl.ds(pl.multiple_of(PAD + r0, PAD), CONV_R), :] = src_ref[pl.ds(r0, CONV_R), :]
        return 0

    lax.fori_loop(0, s // CONV_R, cp, 0)


def _conv_silu_fwd(proj, w8, b, *, col0, width, ct, name, jobs=()):
    s = proj.shape[0]
    nb = col0 // ct

    def body(x_ref, w_ref, b_ref, o_ref, xpad):
        _fill_front_padded(xpad, x_ref, s)

        def step(q, _):
            r0 = pl.multiple_of(q * CONV_R, CONV_R)
            acc, _e = _conv_rows(xpad, r0, w_ref)
            pre = acc + b_ref[...]
            o_ref[pl.ds(r0, CONV_R), :] = pre * _sigmoid(pre)
            return 0

        lax.fori_loop(0, s // CONV_R, step, 0)

    (out,), jouts = _hosted(
        body, jobs, grid=(width // ct,),
        in_specs=[pl.BlockSpec((s, ct), lambda j: (0, nb + j)), pl.BlockSpec((SUBLANE, ct), lambda j: (0, j)),
                  pl.BlockSpec((1, ct), lambda j: (0, j))],
        out_specs=[pl.BlockSpec((s, ct), lambda j: (0, j))],
        out_shape=[jax.ShapeDtypeStruct((s, width), F32)],
        scratch_shapes=[pltpu.VMEM((s + PAD, ct), F32)], name=name, args=(proj, w8, b))
    return (out, jouts) if jobs else out


def _conv_bwd_rows(dpad_ref, r0, w_ref):
    return _conv_bwd_ext(dpad_ref[pl.ds(r0, CONV_R + PAD), :], w_ref)


def _conv_bwd_ext(ext, w_ref):
    acc = _shift_up(ext, 0) * w_ref[3:4, :]
    for k in range(3):
        acc = acc + _shift_up(ext, 3 - k) * w_ref[k:k + 1, :]
    return acc


def _conv_silu_bwd(proj, dact, w8, b, *, col0, width, ct, name, jobs=()):
    s = proj.shape[0]
    nb = col0 // ct

    def body(x_ref, d_ref, w_ref, b_ref, dx_ref, dwb_ref, xpad, dpad):
        _fill_front_padded(xpad, x_ref, s)
        dpad[pl.ds(s, PAD), :] = jnp.zeros((PAD, ct), F32)
        dwb_ref[...] = jnp.zeros_like(dwb_ref)

        def step(q, _):
            r0 = pl.multiple_of(q * CONV_R, CONV_R)
            acc, ext = _conv_rows(xpad, r0, w_ref)
            pre = acc + b_ref[...]
            sg = _sigmoid(pre)
            dpre = d_ref[pl.ds(r0, CONV_R), :] * sg * (1.0 + pre * (1.0 - sg))
            dpad[pl.ds(r0, CONV_R), :] = dpre
            for k in range(4):
                dwb_ref[k:k + 1, :] += jnp.sum(dpre * _shift_down(ext, 3 - k), axis=0, keepdims=True)
            dwb_ref[4:5, :] += jnp.sum(dpre, axis=0, keepdims=True)
            return 0

        lax.fori_loop(0, s // CONV_R, step, 0)

        def step2(q, _):
            r0 = pl.multiple_of(q * CONV_R, CONV_R)
            dx_ref[pl.ds(r0, CONV_R), :] = _conv_bwd_rows(dpad, r0, w_ref).astype(BF16)
            return 0

        lax.fori_loop(0, s // CONV_R, step2, 0)

    colb = pl.BlockSpec((s, ct), lambda j: (0, j))
    outs, jouts = _hosted(
        body, jobs, grid=(width // ct,),
        in_specs=[pl.BlockSpec((s, ct), lambda j: (0, nb + j)), colb, pl.BlockSpec((SUBLANE, ct), lambda j: (0, j)),
                  pl.BlockSpec((1, ct), lambda j: (0, j))],
        out_specs=(colb, pl.BlockSpec((SUBLANE, ct), lambda j: (0, j))),
        out_shape=(jax.ShapeDtypeStruct((s, width), BF16), jax.ShapeDtypeStruct((SUBLANE, width), F32)),
        scratch_shapes=[pltpu.VMEM((s + PAD, ct), F32), pltpu.VMEM((s + PAD, ct), F32)], name=name,
        args=(proj, dact, w8, b))
    return (tuple(outs), jouts) if jobs else tuple(outs)


LRU_CT = 256


def _row_of(v, r):
    return jnp.sum(jnp.where(_iota((v.shape[0], 1), 0) == r, v, 0.0), axis=0, keepdims=True)


def _scan_fwd(a, u):
    r = a.shape[0]
    row = _iota((r, 1), 0)
    d = 1
    while d < r:
        valid = row >= d
        u = jnp.where(valid, a * pltpu.roll(u, d, 0) + u, u)
        a = jnp.where(valid, a * pltpu.roll(a, d, 0), a)
        d *= 2
    return a, u


def _scan_rev(b, u):
    r = b.shape[0]
    row = _iota((r, 1), 0)
    d = 1
    while d < r:
        valid = row < r - d
        u = jnp.where(valid, b * pltpu.roll(u, r - d, 0) + u, u)
        b = jnp.where(valid, b * pltpu.roll(b, r - d, 0), b)
        d *= 2
    return b, u


def _lru_chunk(xpad, r0, cw_ref, cb, wa, ba, wx, bx, sp):
    acc, ext = _conv_rows(xpad, r0, cw_ref)
    xl = acc + cb
    r = _sigmoid(_dot(xl, wa) + ba)
    i = _sigmoid(_dot(xl, wx) + bx)
    la = -LRU_C * r * sp
    a = jnp.exp(la)
    a2 = jnp.exp(2.0 * la)
    mult = jnp.sqrt(-jnp.tanh(la) * (a2 + 1.0))
    first = (r0 + _iota((CONV_R, 1), 0)) == 0
    mult = jnp.where(first, 1.0, mult)
    return ext, xl, r, i, a, a2, mult, first


def _lru_specs(s):
    ct = LRU_CT
    nb_g = COL_G // ct
    return dict(
        x=pl.BlockSpec((s, ct), lambda j: (0, j)),
        g=pl.BlockSpec((s, ct), lambda j: (0, nb_g + j)),
        col=pl.BlockSpec((s, ct), lambda j: (0, j)),
        cw=pl.BlockSpec((SUBLANE, ct), lambda j: (0, j)),
        vec=pl.BlockSpec((1, ct), lambda j: (0, j)),
        gate=pl.BlockSpec((None, ct, ct), lambda j: (j, 0, 0)),
    )


def _lru_fwd(proj, cw8, cb, wa_bd, ba, wx_bd, bx, ap, *, name, jobs=()):
    s = proj.shape[0]
    ct = LRU_CT
    sp_ = _lru_specs(s)

    def body(x_ref, g_ref, cw_ref, cb_ref, wa_ref, ba_ref, wx_ref, bx_ref, ap_ref, y_ref, h_ref, xpad):
        _fill_front_padded(xpad, x_ref, s)
        sp = _softplus(-ap_ref[...])

        def step(q, carry):
            r0 = pl.multiple_of(q * CONV_R, CONV_R)
            _e, xl, _r, i, a, _a2, mult, _f = _lru_chunk(xpad, r0, cw_ref, cb_ref[...], wa_ref[...], ba_ref[...],
                                                       wx_ref[...], bx_ref[...], sp)
            acum, ucum = _scan_fwd(a, xl * i * mult)
            h = acum * carry + ucum
            h_ref[pl.ds(r0, CONV_R), :] = h
            ge, _th = _gelu(g_ref[pl.ds(r0, CONV_R), :])
            y_ref[pl.ds(r0, CONV_R), :] = (ge * h).astype(BF16)
            return _row_of(h, CONV_R - 1)

        lax.fori_loop(0, s // CONV_R, step, jnp.zeros((1, ct), F32))

    (ymix, hs), jouts = _hosted(
        body, jobs, grid=(LRU_W // ct,),
        in_specs=[sp_["x"], sp_["g"], sp_["cw"], sp_["vec"], sp_["gate"], sp_["vec"], sp_["gate"], sp_["vec"], sp_["vec"]],
        out_specs=(sp_["col"], sp_["col"]),
        out_shape=(jax.ShapeDtypeStruct((s, LRU_W + SSD_W), BF16), jax.ShapeDtypeStruct((s, LRU_W), F32)),
        scratch_shapes=[pltpu.VMEM((s + PAD, ct), F32)],
        name=name, args=(proj, proj, cw8, cb, wa_bd, ba, wx_bd, bx, ap))
    return ((ymix, hs), jouts) if jobs else (ymix, hs)


def _lru_bwd(proj, dy, hs, cw8, cb, wa_bd, ba, wx_bd, bx, ap, *, name, jobs=()):
    s = proj.shape[0]
    ct = LRU_CT
    sp_ = _lru_specs(s)

    nq = s // CONV_R

    def body(x_ref, g_ref, dy_ref, h_ref, cw_ref, cb_ref, wa_ref, ba_ref, wx_ref, bx_ref, ap_ref,
             dx_ref, dg_ref, dcwb_ref, dwa_ref, dwx_ref, xpad, hpad):
        _fill_front_padded(xpad, x_ref, s)
        _fill_front_padded(hpad, h_ref, s)
        apv = ap_ref[...]
        sp = _softplus(-apv)
        cb_v, wa, ba_v, wx, bx_v = cb_ref[...], wa_ref[...], ba_ref[...], wx_ref[...], bx_ref[...]
        dcwb_ref[...] = jnp.zeros_like(dcwb_ref)
        dwa_ref[...] = jnp.zeros_like(dwa_ref)
        dwx_ref[...] = jnp.zeros_like(dwx_ref)

        def back(k, carry):
            g_next, a_next, dxl_next = carry
            last_row = _iota((CONV_R, 1), 0) == CONV_R - 1
            r0 = pl.multiple_of((nq - 1 - k) * CONV_R, CONV_R)
            ext, xl, r, i, a, a2, mult, first = _lru_chunk(xpad, r0, cw_ref, cb_v, wa, ba_v, wx, bx_v, sp)
            gv = g_ref[pl.ds(r0, CONV_R), :]
            dyv = dy_ref[pl.ds(r0, CONV_R), :]
            hext = hpad[pl.ds(r0, CONV_R + PAD), :]
            ge, th = _gelu(gv)
            dg_ref[pl.ds(r0, CONV_R), :] = (dyv * _shift_down(hext, 0) * _gelu_grad(gv, th)).astype(BF16)
            b = jnp.where(last_row, a_next, pltpu.roll(a, CONV_R - 1, 0))
            bcum, dcum = _scan_rev(b, dyv * ge)
            gval = dcum + bcum * g_next
            hprev = _shift_down(hext, 1)
            da = gval * hprev
            dxl = gval * i * mult
            di = gval * xl * mult
            dmult = jnp.where(first, 0.0, gval * xl * i)
            dla = da * a - dmult * a2 / mult
            dr = dla * (-LRU_C) * sp
            dcwb_ref[7:8, :] += jnp.sum(dla * (-LRU_C) * r, axis=0, keepdims=True)
            dpr = dr * r * (1.0 - r)
            dpi = di * i * (1.0 - i)
            dxl = dxl + _dot_nt(dpr, wa) + _dot_nt(dpi, wx)
            dwa_ref[...] += _dot_tn(xl, dpr)
            dwx_ref[...] += _dot_tn(xl, dpi)
            dcwb_ref[5:6, :] += jnp.sum(dpr, axis=0, keepdims=True)
            dcwb_ref[6:7, :] += jnp.sum(dpi, axis=0, keepdims=True)
            for tap in range(4):
                dcwb_ref[tap:tap + 1, :] += jnp.sum(dxl * _shift_down(ext, 3 - tap), axis=0, keepdims=True)
            dcwb_ref[4:5, :] += jnp.sum(dxl, axis=0, keepdims=True)
            dx_ref[pl.ds(r0, CONV_R), :] = _conv_bwd_ext(jnp.concatenate([dxl, dxl_next], axis=0), cw_ref).astype(BF16)
            return _row_of(gval, 0), _row_of(a, 0), dxl[:PAD, :]

        zero = jnp.zeros((1, ct), F32)
        lax.fori_loop(0, nq, back, (zero, zero, jnp.zeros((PAD, ct), F32)))
        dcwb_ref[7:8, :] = dcwb_ref[7:8, :] * (-_sigmoid(-apv))

    nt = LRU_W // ct
    outs, jouts = _hosted(
        body, jobs, grid=(nt,),
        in_specs=[sp_["x"], sp_["g"], sp_["col"], sp_["col"], sp_["cw"], sp_["vec"], sp_["gate"], sp_["vec"], sp_["gate"],
                  sp_["vec"], sp_["vec"]],
        out_specs=(sp_["col"], sp_["col"], sp_["cw"], sp_["gate"], sp_["gate"]),
        out_shape=(jax.ShapeDtypeStruct((s, LRU_W), BF16), jax.ShapeDtypeStruct((s, LRU_W), BF16),
                   jax.ShapeDtypeStruct((SUBLANE, LRU_W), F32), jax.ShapeDtypeStruct((nt, ct, ct), F32),
                   jax.ShapeDtypeStruct((nt, ct, ct), F32)),
        scratch_shapes=[pltpu.VMEM((s + PAD, ct), F32), pltpu.VMEM((s + PAD, ct), F32)],
        name=name, args=(proj, proj, dy, hs, cw8, cb, wa_bd, ba, wx_bd, bx, ap))
    return (tuple(outs), jouts) if jobs else tuple(outs)


def _split3(v):
    hi = v.astype(BF16)
    r1 = v - hi.astype(F32)
    mid = r1.astype(BF16)
    lo = (r1 - mid.astype(F32)).astype(BF16)
    return hi, mid, lo


def _dot01(m01, v):
    mb = m01.astype(BF16)
    hi, mid, lo = _split3(v)
    f = lambda part: jnp.dot(mb, part, preferred_element_type=F32)
    return f(hi) + f(mid) + f(lo)


def _dot01_r(v, m01, parts=3):
    mb = m01.astype(BF16)
    acc = None
    for part in _split3(v)[:parts]:
        t = jnp.dot(part, mb, preferred_element_type=F32)
        acc = t if acc is None else acc + t
    return acc


def _ssd_prep(dtr, bias, alog_pad):
    l = CHUNK
    lane = _iota((1, LANE), 1)
    a_head = jnp.where(lane < N_HEAD, -jnp.exp(alog_pad), 0.0)
    dt = _softplus(dtr + bias)
    tril = (_iota((l, l), 1) <= _iota((l, l), 0)).astype(F32)
    a = dt * a_head
    cs = _dot01(tril, a)
    tot = jnp.sum(a, axis=0, keepdims=True)
    return dict(a_head=a_head, dt=dt, tril=tril, cs=cs, tot=tot)


def _col(v, h):
    lane = _iota(v.shape, 1)
    return jnp.sum(jnp.where(lane == h, v, 0.0), axis=1, keepdims=True)


def _decay_mat(cs, cst_ref, h, causal):
    row = cst_ref[h:h + 1, :]
    return jnp.exp(jnp.where(causal, _col(cs, h) - row, NEG_BIG))


def _head_mask(j, rows=CHUNK):
    lane = _iota((rows, GROUP_W), 1)
    return (lane >= j * HEAD_P) & (lane < (j + 1) * HEAD_P)


def _over_heads(v, g):
    r = v.shape[0]
    out = jnp.zeros((r, GROUP_W), F32)
    for j in range(4):
        out = jnp.where(_head_mask(j, r), _col(v, 4 * g + j), out)
    return out


def _ssd_group_fwd(q, g, xs_g, bg, cg, ht_g, cst_ref, causal, dx_g):
    dtx_g, csx_g, totx_g = _over_heads(q["dt"], g), _over_heads(q["cs"], g), _over_heads(q["tot"], g)
    xdt = xs_g * dtx_g
    ex = jnp.exp(csx_g)
    cb = _dot_nt(cg, bg)
    yoff = _dot(cg, ht_g) * ex
    ydiag = jnp.zeros((CHUNK, GROUP_W), F32)
    lms = []
    for j in range(4):
        lms.append(_decay_mat(q["cs"], cst_ref, 4 * g + j, causal))
        ydiag = jnp.where(_head_mask(j), _dot(cb * lms[j], xdt), ydiag)
    y = ydiag + yoff + xs_g * dx_g
    dsx = jnp.exp(totx_g - csx_g)
    return y, dict(xdt=xdt, ex=ex, cb=cb, yoff=yoff, dsx=dsx, dtx=dtx_g, totx=totx_g, lms=lms)


def _gated_norm_fwd(y_g, z_g, w_g):
    sz = _sigmoid(z_g)
    silu = z_g * sz
    yf = y_g * silu
    rs = lax.rsqrt(jnp.mean(yf * yf, axis=1, keepdims=True) + RMS_EPS)
    yn = yf * rs
    return yn * w_g, (sz, silu, rs, yn)


def _ssd_fwd(xact, proj, ymix, bias_pad, alog_pad, dxp, normw, *, name, jobs=()):
    s = xact.shape[0]
    nc = s // CHUNK

    def body(xa_ref, dt_ref, z_ref, _ymix_ref, bias_ref, alp_ref, dx_ref, nw_ref, y_ref, hp_ref, ht, cst):
        @pl.when(pl.program_id(0) == 0)
        def _():
            ht[...] = jnp.zeros_like(ht)

        hp_ref[...] = ht[...]
        q = _ssd_prep(dt_ref[...], bias_ref[...], alp_ref[...])
        cst[...] = q["cs"].T
        causal = q["tril"] > 0.0
        for g in range(N_GROUP):
            sl = slice(g * GROUP_W, (g + 1) * GROUP_W)
            xs_g = xa_ref[:, sl]
            bg = xa_ref[:, SSD_W + g * N_STATE:SSD_W + (g + 1) * N_STATE]
            cg = xa_ref[:, SSD_W + N_GROUP * N_STATE + g * N_STATE:SSD_W + N_GROUP * N_STATE + (g + 1) * N_STATE]
            ht_g = ht[:, sl]
            y, f = _ssd_group_fwd(q, g, xs_g, bg, cg, ht_g, cst, causal, dx_ref[:, sl])
            out, _ = _gated_norm_fwd(y, z_ref[:, sl], nw_ref[:, sl])
            y_ref[:, sl] = out.astype(BF16)
            ht[:, sl] = jnp.exp(f["totx"]) * ht_g + _dot_tn(bg, f["xdt"] * f["dsx"])

    par = lambda w: pl.BlockSpec((1, w), lambda c: (0, 0))
    (ycat, hprev), jouts = _hosted(
        body, jobs, grid=(nc,),
        in_specs=[pl.BlockSpec((CHUNK, XBC), lambda c: (c, 0)),
                  pl.BlockSpec((CHUNK, LANE), lambda c: (c, COL_DT // LANE)),
                  pl.BlockSpec((CHUNK, SSD_W), lambda c: (c, COL_Z // SSD_W)),
                  ANY_SPEC, par(LANE), par(LANE), par(SSD_W), par(SSD_W)],
        out_specs=(pl.BlockSpec((CHUNK, SSD_W), lambda c: (c, LRU_W // SSD_W)),
                   pl.BlockSpec((None, N_STATE, SSD_W), lambda c: (c, 0, 0))),
        out_shape=(jax.ShapeDtypeStruct(ymix.shape, ymix.dtype), jax.ShapeDtypeStruct((nc, N_STATE, SSD_W), F32)),
        scratch_shapes=[pltpu.VMEM((N_STATE, SSD_W), F32), pltpu.VMEM((CHUNK, LANE), F32)],
        aliases={3: 0}, name=name, args=(xact, proj, proj, ymix, bias_pad, alog_pad, dxp, normw))
    return ((ycat, hprev), jouts) if jobs else (ycat, hprev)


def _ssd_bwd(xact, proj, dycat, hprev, bias_pad, alog_pad, dxp, normw, *, name, jobs=()):
    s = xact.shape[0]
    nc = s // CHUNK
    l = CHUNK

    def body(xa_ref, dt_ref, z_ref, dy_ref, hp_ref, bias_ref, alp_ref, dx_ref, nw_ref,
             dxa_ref, ddt_ref, dz_ref, dnw_ref, small_ref, dht, cst, accx, dcsx_s, ddtx_s):
        step = pl.program_id(0)

        @pl.when(step == 0)
        def _():
            dht[...] = jnp.zeros_like(dht)
            accx[...] = jnp.zeros_like(accx)
            dnw_ref[...] = jnp.zeros_like(dnw_ref)
            small_ref[...] = jnp.zeros_like(small_ref)

        dtr = dt_ref[...]
        q = _ssd_prep(dtr, bias_ref[...], alp_ref[...])
        cst[...] = q["cs"].T
        causal = q["tril"] > 0.0
        lane = _iota((l, LANE), 1)
        head_row = _iota((LANE, l), 0)
        dcs_head = jnp.zeros((l, LANE), F32)
        dcs_rows = jnp.zeros((LANE, l), F32)
        for g in range(N_GROUP):
            sl = slice(g * GROUP_W, (g + 1) * GROUP_W)
            slb = slice(SSD_W + g * N_STATE, SSD_W + (g + 1) * N_STATE)
            slc = slice(SSD_W + N_GROUP * N_STATE + g * N_STATE, SSD_W + N_GROUP * N_STATE + (g + 1) * N_STATE)
            xs_g, bg, cg = xa_ref[:, sl], xa_ref[:, slb], xa_ref[:, slc]
            ht_g = hp_ref[:, sl]
            dxp_g = dx_ref[:, sl]
            y, f = _ssd_group_fwd(q, g, xs_g, bg, cg, ht_g, cst, causal, dxp_g)
            z_g, nw_g = z_ref[:, sl], nw_ref[:, sl]
            _o, (sz, silu, rs, yn) = _gated_norm_fwd(y, z_g, nw_g)
            dout = dy_ref[:, sl]
            dnw_ref[:, sl] += jnp.sum(dout * yn, axis=0, keepdims=True)
            dyn = dout * nw_g
            dyf = rs * (dyn - yn * jnp.mean(dyn * yn, axis=1, keepdims=True))
            dy = dyf * silu
            dz_ref[:, sl] = (dyf * y * sz * (1.0 + z_g * (1.0 - sz))).astype(BF16)
            accx[0:1, sl] += jnp.sum(dy * xs_g, axis=0, keepdims=True)
            dyo = dy * f["ex"]
            dcg = _dot_nt(dyo, ht_g)
            dht_prev = _dot_tn(cg, dyo)
            dcsx = dy * f["yoff"]
            xdt = f["xdt"]
            dxdt = jnp.zeros((l, GROUP_W), F32)
            dcb = jnp.zeros((l, l), F32)
            for j in range(4):
                h = 4 * g + j
                lm = f["lms"][j]
                sc = f["cb"] * lm
                mask = _head_mask(j)
                ds_ = jnp.where(causal, _dot_nt(jnp.where(mask, dy, 0.0), xdt), 0.0)
                dxdt = jnp.where(mask, _dot_tn(sc, dy), dxdt)
                dcb = dcb + ds_ * lm
                m = ds_ * sc
                dcs_head = dcs_head + jnp.where(lane == h, jnp.sum(m, axis=1, keepdims=True), 0.0)
                dcs_rows = dcs_rows + jnp.where(head_row == h, jnp.sum(m, axis=0, keepdims=True), 0.0)
            dhn = dht[:, sl]
            etot = jnp.exp(f["totx"])
            dxd = _dot(bg, dhn)
            dbg = _dot_nt(xdt * f["dsx"], dhn)
            dxdt = dxdt + dxd * f["dsx"]
            qq = dxd * xdt * f["dsx"]
            dcsx = dcsx - qq
            dtot = jnp.sum(qq, axis=0, keepdims=True) + jnp.sum(dhn * ht_g, axis=0, keepdims=True) * etot
            dht[:, sl] = etot * dhn + dht_prev
            dcg = dcg + _dot(dcb, bg)
            dbg = dbg + _dot_tn(dcb, cg)
            dxa_ref[:, sl] = dxdt * f["dtx"] + dy * dxp_g
            dxa_ref[:, slb] = dbg
            dxa_ref[:, slc] = dcg
            dcsx_s[:, sl] = dcsx
            ddtx_s[:, sl] = dxdt * xs_g
            accx[2:3, sl] = dtot
        reduce = (jnp.right_shift(_iota((SSD_W, LANE), 0), 6) == _iota((SSD_W, LANE), 1)).astype(F32)
        triu = (_iota((l, l), 1) >= _iota((l, l), 0)).astype(F32)
        dtot = _dot01_r(accx[...], reduce)[2:3, :]
        dcs_head = dcs_head - dcs_rows.T
        da_head = _dot01(triu, dcs_head + _dot01_r(dcsx_s[...], reduce, parts=2)) + dtot
        ddt = _dot01_r(ddtx_s[...], reduce, parts=2) + da_head * q["a_head"]
        small_ref[1:2, :] += jnp.sum(da_head * q["dt"], axis=0, keepdims=True)
        ddtr = ddt * _sigmoid(dtr + bias_ref[...])
        ddt_ref[...] = ddtr.astype(BF16)
        small_ref[0:1, :] += jnp.sum(ddtr, axis=0, keepdims=True)

        @pl.when(step == nc - 1)
        def _():
            small_ref[1:2, :] = small_ref[1:2, :] * q["a_head"]
            small_ref[2:3, :] = _dot01_r(accx[...], reduce)[0:1, :]

    rev = lambda c: nc - 1 - c
    par = lambda w: pl.BlockSpec((1, w), lambda c: (0, 0))
    outs, jouts = _hosted(
        body, jobs, grid=(nc,),
        in_specs=[pl.BlockSpec((CHUNK, XBC), lambda c: (rev(c), 0)),
                  pl.BlockSpec((CHUNK, LANE), lambda c: (rev(c), COL_DT // LANE)),
                  pl.BlockSpec((CHUNK, SSD_W), lambda c: (rev(c), COL_Z // SSD_W)),
                  pl.BlockSpec((CHUNK, SSD_W), lambda c: (rev(c), 1)),
                  pl.BlockSpec((None, N_STATE, SSD_W), lambda c: (rev(c), 0, 0)),
                  par(LANE), par(LANE), par(SSD_W), par(SSD_W)],
        out_specs=(pl.BlockSpec((CHUNK, XBC), lambda c: (rev(c), 0)),
                   pl.BlockSpec((CHUNK, LANE), lambda c: (rev(c), 0)),
                   pl.BlockSpec((CHUNK, SSD_W), lambda c: (rev(c), 0)),
                   par(SSD_W), pl.BlockSpec((SUBLANE, LANE), lambda c: (0, 0))),
        out_shape=(jax.ShapeDtypeStruct((s, XBC), F32), jax.ShapeDtypeStruct((s, LANE), BF16),
                   jax.ShapeDtypeStruct((s, SSD_W), BF16), jax.ShapeDtypeStruct((1, SSD_W), F32),
                   jax.ShapeDtypeStruct((SUBLANE, LANE), F32)),
        scratch_shapes=[pltpu.VMEM((N_STATE, SSD_W), F32), pltpu.VMEM((CHUNK, LANE), F32),
                        pltpu.VMEM((SUBLANE, SSD_W), F32), pltpu.VMEM((CHUNK, SSD_W), F32),
                        pltpu.VMEM((CHUNK, SSD_W), F32)],
        name=name, args=(xact, proj, proj, dycat, hprev, bias_pad, alog_pad, dxp, normw))
    return (tuple(outs), jouts) if jobs else tuple(outs)


def _blockdiag(w):
    per = LRU_CT // HEAD_P
    w2 = w.reshape(N_HEAD // per, per, HEAD_P, HEAD_P)
    z = jnp.zeros((N_HEAD // per, HEAD_P, HEAD_P), w.dtype)
    rows = [jnp.concatenate([w2[:, i] if j == i else z for j in range(per)], axis=2) for i in range(per)]
    return jnp.concatenate(rows, axis=1)


def _unblockdiag(wbd):
    per = LRU_CT // HEAD_P
    parts = [wbd[:, i * HEAD_P:(i + 1) * HEAD_P, i * HEAD_P:(i + 1) * HEAD_P] for i in range(per)]
    return jnp.stack(parts, axis=1).reshape(N_HEAD, HEAD_P, HEAD_P)


def _pad_rows8(w):
    return jnp.concatenate([w, jnp.zeros((SUBLANE - w.shape[0], w.shape[1]), w.dtype)], axis=0)


def _pad_lane(v):
    return jnp.concatenate([v, jnp.zeros((1, LANE - v.shape[1]), v.dtype)], axis=1)


class _NoExchange:
    def ride(self, host):
        return []

    def done(self, jobs, outs, w):
        pass

    def grad(self, name, val):
        pass

    def small(self, raw):
        pass

    def pairs_now(self):
        pass


def _local_step(x, p, tgt, w, hooks=_NoExchange()):
    cw_l = _pad_rows8(w["lru_conv_w"])
    cw_s = _pad_rows8(w["ssd_conv_w"])
    wa_bd = _blockdiag(w["lru_gate_a_w"])
    wx_bd = _blockdiag(w["lru_gate_x_w"])
    ba = w["lru_gate_a_b"].reshape(1, LRU_W)
    bx = w["lru_gate_x_b"].reshape(1, LRU_W)
    bias_pad = _pad_lane(w["ssd_dt_bias"])
    alog_pad = _pad_lane(w["ssd_a_log"])
    dxp = jnp.repeat(w["ssd_d"], HEAD_P, axis=1)

    def host(fn, *a, name, **k):
        jobs = hooks.ride(name)
        res = fn(*a, name=name, jobs=jobs, **k)
        if jobs:
            res, jouts = res
            hooks.done(jobs, jouts, w)
        return res

    def grad(n, val):
        g[n] = val
        hooks.grad(n, val)

    xb = x.astype(BF16)
    proj = host(_mm, xb, w["w_in_t"], "nt", tm=2048, tn=512, name="in_proj")
    ymix, h_lru = host(_lru_fwd, proj, cw_l, w["lru_conv_b"], wa_bd, ba, wx_bd, bx, w["lru_a_param"], name="lru_fwd")
    xact = host(_conv_silu_fwd, proj, cw_s, w["ssd_conv_b"], col0=COL_XBC, width=XBC, ct=256, name="ssd_conv_fwd")
    ycat, hprev = host(_ssd_fwd, xact, proj, ymix, bias_pad, alog_pad, dxp, w["ssd_norm_w"], name="ssd_fwd")
    mix, x1, x1b = _mm_ln(ycat, w["w_out"], x, w["ln1_g"], w["ln1_b"], tm=512, name="out_proj")
    pre = host(_mm, x1b, w["w_ff1"], "nn", tm=2048, tn=512, out_dtype=BF16, name="ff1")
    ff, x2, x2b = _mm_ln(pre, w["w_ff2"], x1, w["ln2_g"], w["ln2_b"], tm=512, a_fn=_relu2, name="ff2")
    loss, dgpre, dple, dt3, dg3, db3 = _head(x2, x2b, p, w["w_ple_gate"], w["w_ple"], w["ln3_g"], w["ln3_b"], tgt,
                                             name="head")

    g = {}
    g["ln3_g"], g["ln3_b"] = dg3, db3
    grad("w_ple_gate", _mm(x2b, dgpre, "tn", tm=512, tn=1024, out_dtype=BF16, name="d_w_ple_gate"))
    grad("w_ple", _mm(p, dple, "tn", tm=256, tn=512, dest_major=True, out_dtype=BF16, name="d_w_ple"))
    dt2, dt2b, g["ln2_g"], g["ln2_b"] = host(_mm_ln_bwd, dgpre, w["w_ple_gate"], x1, ff, w["ln2_g"], dt3, ALPHA,
                                             tm=512, name="d_x2")
    grad("w_ff2", host(_mm, pre, dt2b, "tn", tm=512, tn=1024, a_fn=_relu2, out_dtype=BF16, name="d_w_ff2"))
    dpre = host(_mm, dt2b, w["w_ff2"], "nt", tm=2048, tn=512, extra=pre, out_dtype=BF16,
                epi=lambda acc, pv: acc * 2.0 * jnp.maximum(pv.astype(F32), 0.0), name="d_pre")
    grad("w_ff1", host(_mm, x1b, dpre, "tn", tm=1024, tn=512, dest_major=True, out_dtype=BF16, name="d_w_ff1"))
    dt1, dt1b, g["ln1_g"], g["ln1_b"] = host(_mm_ln_bwd, dpre, w["w_ff1"], x, mix, w["ln1_g"], dt2, ALPHA,
                                             tm=256, name="d_x1")
    grad("w_out", host(_mm, ycat, dt1b, "tn", tm=512, tn=1024, out_dtype=BF16, name="d_w_out"))
    dycat = host(_mm, dt1b, w["w_out"], "nt", tm=2048, tn=512, name="d_ycat")
    dxl, dgl, dcwb_l, dwa, dwx = host(_lru_bwd, proj, dycat, h_lru, cw_l, w["lru_conv_b"], wa_bd, ba, wx_bd, bx,
                                      w["lru_a_param"], name="lru_bwd")
    g["lru_gate_a_w"] = _unblockdiag(dwa)
    g["lru_gate_x_w"] = _unblockdiag(dwx)
    raw = dict(lru=dcwb_l, gate_a=g["lru_gate_a_w"].reshape(N_HEAD * HEAD_P, HEAD_P).astype(BF16),
               gate_x=g["lru_gate_x_w"].reshape(N_HEAD * HEAD_P, HEAD_P).astype(BF16))
    hooks.small(raw)
    dxact, ddt, dz, g["ssd_norm_w"], small = host(_ssd_bwd, xact, proj, dycat, hprev, bias_pad, alog_pad, dxp,
                                                   w["ssd_norm_w"], name="ssd_bwd")
    dxbc, dcwb_s = host(_conv_silu_bwd, proj, dxact, cw_s, w["ssd_conv_b"], col0=COL_XBC, width=XBC, ct=256,
                        name="ssd_conv_bwd")
    pieces, offsets = [dxl, dgl, dz, dxbc, ddt], [0, COL_G, COL_Z, COL_XBC, COL_DT]

    g["lru_conv_w"] = dcwb_l[0:4]
    g["lru_conv_b"] = dcwb_l[4:5]
    g["lru_gate_a_b"] = dcwb_l[5:6]
    g["lru_gate_x_b"] = dcwb_l[6:7]
    g["lru_a_param"] = dcwb_l[7:8]
    g["ssd_conv_w"] = dcwb_s[0:4]
    g["ssd_conv_b"] = dcwb_s[4:5]
    g["ssd_dt_bias"] = small[0:1, :N_HEAD]
    g["ssd_a_log"] = small[1:2, :N_HEAD]
    g["ssd_d"] = small[2:3, :N_HEAD]
    rows = jnp.concatenate([g[n] for n in ("ssd_norm_w", "ln1_g", "ln1_b", "ln2_g", "ln2_b", "ln3_g", "ln3_b")]
                           + [jnp.broadcast_to(loss[:, 0:1], (1, D_MODEL))], axis=0)
    late = dict(ssd=dcwb_s, heads=small, rows=rows)
    hooks.small(late)
    raw.update(late)
    dwt = None
    for q, (pc, off) in enumerate(zip(pieces, offsets)):
        dwt = host(_mm, pc, xb, "tn", tm=512, tn=1024, out_dtype=BF16, into=(dwt, off, D_IN),
                   name="d_w_in_%d" % q)
    grad("w_in", dwt)
    hooks.pairs_now()
    grad_x = host(_mm_pieces, pieces, offsets, w["w_in_t"], tm=256, extra=dt1, epi=lambda acc, e: acc + ALPHA * e,
                  name="d_x")
    return loss[0, 0], grad_x, g, raw


ANY_SPEC = pl.BlockSpec(memory_space=pl.ANY)


def _mesh_pos():
    return lax.axis_index("x"), lax.axis_index("y"), lax.axis_index("c")


def _remote(src, dst, send, recv, k, to):
    return pltpu.make_async_remote_copy(src_ref=src, dst_ref=dst, send_sem=send.at[k], recv_sem=recv.at[k],
                                        device_id=to, device_id_type=MESH_T)


class _Job:
    N_SEM = 9

    def __init__(self, kind, inp):
        self.kind, self.inp = kind, inp
        shape = {"gather": (N_DEV,) + inp.shape, "relay": (N_DEV,) + inp.shape, "pair": (4,) + inp.shape[1:],
                 "chip": inp.shape}[kind]
        self.out = jax.ShapeDtypeStruct(shape, inp.dtype)
        self.top = (inp.shape[0] // 2) // 16 * 16

    def _relay_copies(self, inp, out, send, recv):
        x, y, c = _mesh_pos()
        sib, xn, yn, dg = (x, y, 1 - c), (1 - x, y, c), (x, 1 - y, c), (1 - x, 1 - y, c)
        blk = lambda p, cc=None: out.at[4 * p[0] + 2 * p[1] + (p[2] if cc is None else cc)]
        top = lambda r: r.at[pl.ds(0, self.top)]
        bot = lambda r: r.at[pl.ds(self.top, self.inp.shape[0] - self.top)]
        mine = blk((x, y, c))
        plan = [
            (inp, mine, sib, blk(sib)),
            (inp, mine, xn, blk(xn)),
            (inp, mine, yn, blk(yn)),
            (top(blk(xn)), top(blk(xn)), yn, top(blk(dg))),
            (bot(blk(yn)), bot(blk(yn)), xn, bot(blk(dg))),
            (blk(xn), blk(xn), sib, blk(xn, 1 - c)),
            (blk(yn), blk(yn), sib, blk(yn, 1 - c)),
            (top(blk(dg)), top(blk(dg)), sib, top(blk(dg, 1 - c))),
            (bot(blk(dg)), bot(blk(dg)), sib, bot(blk(dg, 1 - c))),
        ]
        me = (x, y, c)
        return [(_remote(s, d, send, recv, k, to), _remote(s, land, send, recv, k, me))
                for k, (s, d, to, land) in enumerate(plan)]

    def _places(self):
        x, y, c = _mesh_pos()
        return (x, y, c), (x, y, 1 - c), [(1 - x, y), (x, 1 - y), (1 - x, 1 - y)]

    def start(self, inp, out, send, recv, loc):
        me, sibling, chips = self._places()
        x, y, c = me
        if self.kind == "relay":
            pltpu.make_async_copy(inp, out.at[4 * x + 2 * y + c], loc.at[0]).start()
            cps = self._relay_copies(inp, out, send, recv)
            for k in (0, 1, 2):
                cps[k][0].start()
        elif self.kind == "gather":
            mine = out.at[4 * x + 2 * y + c]
            pltpu.make_async_copy(inp, mine, loc.at[0]).start()
            _remote(inp, mine, send, recv, 0, sibling).start()
            for j, chip in enumerate(chips):
                _remote(inp, mine, send, recv, 1 + j, (*chip, c)).start()
        elif self.kind == "pair":
            for k in range(4):
                _remote(inp.at[2 * k + (1 - c)], out.at[k], send, recv, k, sibling).start()
        else:
            kme = 2 * x + y
            pltpu.make_async_copy(inp.at[kme], out.at[kme], loc.at[0]).start()
            for j, (tx, ty) in enumerate(chips):
                _remote(inp.at[2 * tx + ty], out.at[kme], send, recv, j, (tx, ty, c)).start()

    def mid(self, inp, out, send, recv, loc):
        if self.kind == "relay":
            cps = self._relay_copies(inp, out, send, recv)
            for k, onward in ((1, (3, 5)), (2, (4, 6))):
                cps[k][1].wait_recv()
                for q in onward:
                    cps[q][0].start()
            return
        if self.kind != "gather":
            return
        me, sibling, chips = self._places()
        c = me[2]
        for j, chip in enumerate(chips):
            landed = out.at[4 * chip[0] + 2 * chip[1] + c]
            _remote(landed, landed, send, recv, 1 + j, me).wait_recv()
            _remote(landed, landed, send, recv, 4 + j, sibling).start()

    def finish(self, inp, out, send, recv, loc):
        me, sibling, chips = self._places()
        x, y, c = me
        if self.kind == "relay":
            cps = self._relay_copies(inp, out, send, recv)
            for k, onward in ((3, 7), (4, 8)):
                cps[k][1].wait_recv()
                cps[onward][0].start()
            for k in (0, 5, 6, 7, 8):
                cps[k][1].wait_recv()
            for k in range(9):
                cps[k][0].wait_send()
            pltpu.make_async_copy(inp, out.at[4 * x + 2 * y + c], loc.at[0]).wait()
        elif self.kind == "gather":
            blk = lambda px, py, pc: out.at[4 * px + 2 * py + pc]
            mine = blk(*me)
            _remote(inp, blk(*sibling), send, recv, 0, me).wait_recv()
            for j, chip in enumerate(chips):
                _remote(inp, blk(*chip, 1 - c), send, recv, 4 + j, me).wait_recv()
            for k in range(7):
                _remote(inp, mine, send, recv, k, sibling).wait_send()
            pltpu.make_async_copy(inp, mine, loc.at[0]).wait()
        elif self.kind == "pair":
            for k in range(4):
                _remote(inp.at[2 * k + (1 - c)], out.at[k], send, recv, k, sibling).wait()
        else:
            kme = 2 * x + y
            for j, (tx, ty) in enumerate(chips):
                _remote(inp.at[kme], out.at[2 * tx + ty], send, recv, j, (tx, ty, c)).wait_recv()
            for j, (tx, ty) in enumerate(chips):
                _remote(inp.at[2 * tx + ty], out.at[kme], send, recv, j, (tx, ty, c)).wait_send()
            pltpu.make_async_copy(inp.at[kme], out.at[kme], loc.at[0]).wait()


def _job_scratch(jobs):
    sem = pltpu.SemaphoreType.DMA
    return [s for _ in jobs for s in (sem((_Job.N_SEM,)), sem((_Job.N_SEM,)), sem((1,)))]


def _run_jobs(jobs, method, jins, jouts, jsems, only=None):
    for q, job in enumerate(jobs):
        if only is None or only[q]:
            getattr(job, method)(jins[q], jouts[q], *jsems[3 * q:3 * q + 3])


def _exchange(jobs, *, name):
    n = len(jobs)

    def body(*refs):
        jins, jouts, jsems = refs[:n], refs[n:2 * n], refs[2 * n:]
        _run_jobs(jobs, "start", jins, jouts, jsems)
        _run_jobs(jobs, "mid", jins, jouts, jsems)
        _run_jobs(jobs, "finish", jins, jouts, jsems)

    return _pcall(body, in_specs=[ANY_SPEC] * n, out_specs=[ANY_SPEC] * n, out_shape=[j.out for j in jobs],
                  scratch_shapes=_job_scratch(jobs), name=name)(*[j.inp for j in jobs])


def _hosted(body, jobs, *, grid, in_specs, out_specs, out_shape, args, name, scratch_shapes=(), aliases=None):
    in_specs, out_specs, out_shape = list(in_specs), list(out_specs), list(out_shape)
    scratch_shapes = list(scratch_shapes)
    n_in, n_out, n_scr, nj = len(in_specs), len(out_specs), len(scratch_shapes), len(jobs)
    sem = ("arbitrary",) * len(grid)
    kw = dict(input_output_aliases=aliases) if aliases else {}
    if not jobs:
        res = _pcall(body, grid=grid, in_specs=in_specs, out_specs=out_specs, out_shape=out_shape,
                     scratch_shapes=scratch_shapes, name=name, compiler_params=_cparams(sem), **kw)(*args)
        return list(res), []

    def full(*refs):
        ins, jins = refs[:n_in], refs[n_in:n_in + nj]
        o0 = n_in + nj
        outs, jouts = refs[o0:o0 + n_out], refs[o0 + n_out:o0 + n_out + nj]
        s0 = o0 + n_out + nj
        scr, jsems = refs[s0:s0 + n_scr], refs[s0 + n_scr:]
        step = pl.program_id(0)
        for ax in range(1, len(grid)):
            step = step * grid[ax] + pl.program_id(ax)
        total = math.prod(grid)
        early = [job.kind == "relay" for job in jobs]
        mid_step = (3 * total) // 5
        split = any(early) and 0 < mid_step < total - 1

        @pl.when(step == 0)
        def _():
            _run_jobs(jobs, "start", jins, jouts, jsems)

        if split:
            @pl.when(step == mid_step)
            def _():
                _run_jobs(jobs, "mid", jins, jouts, jsems, only=early)

        body(*ins, *outs, *scr)

        @pl.when(step == total - 1)
        def _():
            _run_jobs(jobs, "mid", jins, jouts, jsems, only=[not e for e in early] if split else None)
            _run_jobs(jobs, "finish", jins, jouts, jsems)

    res = _pcall(full, grid=grid, in_specs=in_specs + [ANY_SPEC] * nj, out_specs=out_specs + [ANY_SPEC] * nj,
                 out_shape=out_shape + [j.out for j in jobs], scratch_shapes=scratch_shapes + _job_scratch(jobs),
                 name=name, compiler_params=_cparams(sem), **kw)(*args, *[j.inp for j in jobs])
    return list(res[:n_out]), list(res[n_out:])


def _pair_add(g8, r4, cidx, *, name):
    _, r, c = g8.shape
    tr = ROW_TILE if r % ROW_TILE == 0 else r

    def body(c_ref, g_ref, r_ref, o_ref):
        o_ref[...] = (g_ref[...].astype(F32) + r_ref[...].astype(F32)).astype(BF16)

    return _pcall(
        body,
        grid_spec=pltpu.PrefetchScalarGridSpec(
            num_scalar_prefetch=1, grid=(4, r // tr),
            in_specs=[pl.BlockSpec((None, tr, c), lambda k, i, cr: (2 * k + cr[0], i, 0)),
                      pl.BlockSpec((None, tr, c), lambda k, i, cr: (k, i, 0))],
            out_specs=pl.BlockSpec((None, tr, c), lambda k, i, cr: (k, i, 0))),
        out_shape=jax.ShapeDtypeStruct((4, r, c), BF16), name=name,
        compiler_params=_cparams(("parallel", "parallel")))(cidx, g8, r4)


def _adam_update(g, w_ref, m_ref, v_ref, g_ref, d_ref, mo_ref, vo_ref):
    c1 = 1.0 - ADAM_B1 ** ADAM_STEP
    c2 = 1.0 - ADAM_B2 ** ADAM_STEP
    m2 = ADAM_B1 * m_ref[...] + (1.0 - ADAM_B1) * g
    v2 = ADAM_B2 * v_ref[...] + (1.0 - ADAM_B2) * (g * g)
    g_ref[...] = g
    mo_ref[...] = m2
    vo_ref[...] = v2
    d_ref[...] = -ADAM_LR * ((m2 / c1) / (jnp.sqrt(v2 / c2) + ADAM_EPS) + ADAM_WD * w_ref[...])


def _adamw_rows(srcs, items, own_cols, me1, loss_row, *, name):
    ns, ni, no = len(srcs), len(items), len(own_cols)
    full = lambda a: pl.BlockSpec(a.shape, lambda i, me: (0,) * a.ndim)
    in_specs = [full(a) for a in srcs]
    args = list(srcs)
    for (si, _r0, w, _m, _v) in own_cols:
        a = srcs[si]
        in_specs.append(pl.BlockSpec((N_DEV, a.shape[1], w.shape[1]), lambda i, me: (0, 0, me[0])))
        args.append(a)
    out_specs, out_shape = [], []
    for (_si, _r0, w, m, v) in list(items) + list(own_cols):
        in_specs += [full(w)] * 3
        args += [w, m, v]
        out_specs += [full(w)] * 4
        out_shape += [jax.ShapeDtypeStruct(w.shape, F32)] * 4
    out_specs.append(pl.BlockSpec((1, LANE), lambda i, me: (0, 0)))
    out_shape.append(jax.ShapeDtypeStruct((1, LANE), F32))

    def body(me_ref, *refs):
        src_refs, own_refs = refs[:ns], refs[ns:ns + no]
        wmv = refs[ns + no:ns + no + 3 * (ni + no)]
        outs = refs[ns + no + 3 * (ni + no):]
        lsrc, lrow = src_refs[loss_row[0]], loss_row[1]
        total = lsrc[0, lrow:lrow + 1, 0:LANE]
        for d in range(1, N_DEV):
            total = total + lsrc[d, lrow:lrow + 1, 0:LANE]
        outs[-1][...] = total
        for q, (si, r0, w, _m, _v) in enumerate(list(items) + list(own_cols)):
            nr, cw = w.shape
            gref = src_refs[si] if q < ni else own_refs[q - ni]
            g = gref[0, r0:r0 + nr, 0:cw]
            for d in range(1, N_DEV):
                g = g + gref[d, r0:r0 + nr, 0:cw]
            _adam_update(g, *wmv[3 * q:3 * q + 3], *outs[4 * q:4 * q + 4])

    res = _pcall(
        body,
        grid_spec=pltpu.PrefetchScalarGridSpec(num_scalar_prefetch=1, grid=(1,), in_specs=in_specs, out_specs=out_specs),
        out_shape=out_shape, name=name, compiler_params=_cparams(("arbitrary",)))(me1, *args)
    return [tuple(res[4 * q:4 * q + 4]) for q in range(ni + no)], res[-1]


def _adamw(gsrc, w, m, v, *, name):
    k, r, c = gsrc.shape
    tr = ROW_TILE if r % ROW_TILE == 0 else r

    def body(gs_ref, w_ref, m_ref, v_ref, g_ref, d_ref, mo_ref, vo_ref):
        g = gs_ref[0].astype(F32)
        for q in range(1, k):
            g = g + gs_ref[q].astype(F32)
        _adam_update(g, w_ref, m_ref, v_ref, g_ref, d_ref, mo_ref, vo_ref)

    tc = c
    if tr == r and r > ROW_TILE and c % 256 == 0:
        tc = 256
    blk = pl.BlockSpec((tr, tc), lambda i, j: (i, j))
    sd = jax.ShapeDtypeStruct((r, c), F32)
    return _pcall(body, grid=(r // tr, c // tc),
                  in_specs=[pl.BlockSpec((k, tr, tc), lambda i, j: (0, i, j)), blk, blk, blk],
                  out_specs=(blk, blk, blk, blk), out_shape=(sd, sd, sd, sd), name=name,
                  compiler_params=_cparams(("parallel", "parallel")))(gsrc, w, m, v)


WEIGHTS = ['w_in', 'lru_conv_w', 'lru_conv_b', 'lru_gate_a_w', 'lru_gate_a_b', 'lru_gate_x_w', 'lru_gate_x_b',
           'lru_a_param', 'ssd_conv_w', 'ssd_conv_b', 'ssd_dt_bias', 'ssd_a_log', 'ssd_d', 'ssd_norm_w', 'w_out',
           'ln1_g', 'ln1_b', 'w_ff1', 'w_ff2', 'ln2_g', 'ln2_b', 'w_ple_gate', 'w_ple', 'ln3_g', 'ln3_b']
BIG = ['w_in', 'w_out', 'w_ff1', 'w_ff2', 'w_ple_gate', 'w_ple']
COL_SHARDED = ('w_ff1', 'w_ple')
CONV = ['lru_conv_w', 'ssd_conv_w']
REPL = [n for n in WEIGHTS if n not in BIG and n not in CONV]
CONV_CH = {'lru_conv_w': LRU_W, 'ssd_conv_w': XBC}


def _to_dest_major(name, gfull):
    if name in COL_SHARDED:
        r, cfull = gfull.shape
        return gfull.reshape(r, N_DEV, cfull // N_DEV).transpose(1, 0, 2)
    rfull, cdim = gfull.shape
    return gfull.reshape(N_DEV, rfull // N_DEV, cdim)


def _full_weight(name, gathered):
    if name in COL_SHARDED:
        _, r, cs = gathered.shape
        full = gathered.transpose(1, 0, 2).reshape(r, N_DEV * cs)
    else:
        _, rs, cdim = gathered.shape
        full = gathered.reshape(N_DEV * rs, cdim)
    if name == 'w_in':
        full = lax.dynamic_update_slice(jnp.zeros((D_IN_PAD, D_MODEL), full.dtype), full, (0, 0))
    return full


SMALL_SRC = ("lru", "ssd", "heads", "rows", "gate_a", "gate_x")
AG_HOSTS = {"in_proj": ("w_ff1",), "lru_fwd": ("w_ff2",), "ssd_fwd": ("w_out",), "ff1": ("w_ple_gate", "w_ple")}
PAIR_HOSTS = ("d_x2", "d_pre", "d_x1", "d_ycat")
CHIP_HOSTS = {"lru_bwd": ("w_ple_gate", "w_ple", "w_ff2"), "ssd_bwd": ("w_ff1",), "ssd_conv_bwd": ("w_out",),
              "d_x": ("w_in",)}
SMALL_HOSTS = {"ssd_bwd": ("lru", "gate_a", "gate_x"), "d_w_in_3": ("ssd", "heads", "rows")}


class _Schedule:
    def __init__(self, shards, cidx):
        self.shards, self.cidx = shards, cidx
        self.pair, self.chip, self.small_jobs = [], [], []
        self.dest, self.summed, self.gathered_small = {}, {}, {}
        self.tags = []

    def ride(self, host):
        tags = []
        if host in AG_HOSTS:
            tags = [("weight", n, self.shards[n]) for n in AG_HOSTS[host]]
        elif host in PAIR_HOSTS or host in CHIP_HOSTS or host == "flush":
            tags = [("pair", n, a) for n, a in self.pair]
            self.pair = []
            if host not in PAIR_HOSTS:
                take = [t for t in self.chip if host == "flush" or t[0] in CHIP_HOSTS[host]]
                tags += [("chip", n, a) for n, a in take]
                self.chip = [t for t in self.chip if not any(t is u for u in take)]
        if host in SMALL_HOSTS:
            tags += [("small", n, a) for n, a in self.small_jobs if n in SMALL_HOSTS[host]]
            self.small_jobs = [t for t in self.small_jobs if t[0] not in SMALL_HOSTS[host]]
        self.tags = tags
        return [_Job({"weight": "relay", "small": "gather"}.get(kind, kind), a) for kind, _n, a in tags]

    def done(self, jobs, outs, w):
        for (kind, n, _a), o in zip(self.tags, outs):
            if kind == "weight":
                w[n] = _full_weight(n, o)
            elif kind == "small":
                self.gathered_small[n] = o
            elif kind == "pair":
                self.chip.append((n, _pair_add(self.dest[n], o, self.cidx, name="rs_pair_add_" + n)))
            else:
                self.summed[n] = o

    def grad(self, name, val):
        self.dest[name] = val if val.ndim == 3 else _to_dest_major(name, val)
        self.pair.append((name, self.dest[name]))

    def small(self, raw):
        self.small_jobs += list(raw.items())

    def pairs_now(self):
        tags = [("pair", n, a) for n, a in self.pair]
        self.pair, self.tags = [], tags
        jobs = [_Job("pair", a) for _k, _n, a in tags]
        self.done(jobs, _exchange(jobs, name="rs_pairs_now"), None)

    def flush(self):
        step = 0
        while self.pair or self.chip:
            jobs = self.ride("flush")
            self.done(jobs, _exchange(jobs, name="rs_flush_%d" % step), None)
            step += 1


def kernel(x, p, w_in, lru_conv_w, lru_conv_b, lru_gate_a_w, lru_gate_a_b, lru_gate_x_w, lru_gate_x_b, lru_a_param, ssd_conv_w, ssd_conv_b, ssd_dt_bias, ssd_a_log, ssd_d, ssd_norm_w, w_out, ln1_g, ln1_b, w_ff1, w_ff2, ln2_g, ln2_b, w_ple_gate, w_ple, ln3_g, ln3_b, loss_target, m_w_in, m_lru_conv_w, m_lru_conv_b, m_lru_gate_a_w, m_lru_gate_a_b, m_lru_gate_x_w, m_lru_gate_x_b, m_lru_a_param, m_ssd_conv_w, m_ssd_conv_b, m_ssd_dt_bias, m_ssd_a_log, m_ssd_d, m_ssd_norm_w, m_w_out, m_ln1_g, m_ln1_b, m_w_ff1, m_w_ff2, m_ln2_g, m_ln2_b, m_w_ple_gate, m_w_ple, m_ln3_g, m_ln3_b, v_w_in, v_lru_conv_w, v_lru_conv_b, v_lru_gate_a_w, v_lru_gate_a_b, v_lru_gate_x_w, v_lru_gate_x_b, v_lru_a_param, v_ssd_conv_w, v_ssd_conv_b, v_ssd_dt_bias, v_ssd_a_log, v_ssd_d, v_ssd_norm_w, v_w_out, v_ln1_g, v_ln1_b, v_w_ff1, v_w_ff2, v_ln2_g, v_ln2_b, v_w_ple_gate, v_w_ple, v_ln3_g, v_ln3_b):
    given = dict(locals())
    def local(a, n):
        return jnp.swapaxes(a[0], 0, 1) if n == 'w_in' else a[0]

    wsh = {n: local(given[n], n) for n in WEIGHTS}
    msh = {n: local(given["m_" + n], n) for n in WEIGHTS}
    vsh = {n: local(given["v_" + n], n) for n in WEIGHTS}
    xi, yi, ci = _mesh_pos()
    me = 4 * xi + 2 * yi + ci

    shards = {n: wsh[n].astype(BF16) for n in BIG}
    conv_pack = jnp.concatenate([_pad_rows8(wsh[n]) for n in CONV], axis=1)
    g_in, gconv = _exchange([_Job("relay", shards['w_in']), _Job("gather", conv_pack)], name="ag_first")
    full = {'w_in_t': _full_weight('w_in', g_in)}
    c0 = 0
    for n in CONV:
        cw = CONV_CH[n] // N_DEV
        full[n] = gconv[:, :4, c0:c0 + cw].transpose(1, 0, 2).reshape(4, CONV_CH[n])
        c0 += cw
    for n in REPL:
        full[n] = given[n] if given[n].ndim == 2 else wsh[n]

    sched = _Schedule(shards, jnp.reshape(ci, (1,)).astype(jnp.int32))
    loss_local, grad_x, g, raw = _local_step(x[0], p[0, 0], loss_target[0], full, sched)
    sched.flush()
    summed, gat = sched.summed, sched.gathered_small

    outs = {}
    for n in BIG:
        outs[n] = _adamw(summed[n], wsh[n], msh[n], vsh[n], name="adamw_" + n)
    for n, k in (("lru_gate_a_w", "gate_a"), ("lru_gate_x_w", "gate_x")):
        flat = lambda a: a.reshape(N_HEAD * HEAD_P, HEAD_P)
        res = _adamw(gat[k], flat(wsh[n]), flat(msh[n]), flat(vsh[n]), name="adamw_" + n)
        outs[n] = tuple(r.reshape(N_HEAD, HEAD_P, HEAD_P) for r in res)
    for n, row in (("lru_gate_a_b", 5), ("lru_gate_x_b", 6)):
        outs[n] = _adamw(gat["lru"][:, row].reshape(N_DEV, N_HEAD, HEAD_P), wsh[n], msh[n], vsh[n], name="adamw_" + n)
    row_items = [("lru_conv_b", 0, 4), ("lru_a_param", 0, 7),
                 ("ssd_conv_b", 1, 4), ("ssd_dt_bias", 2, 0), ("ssd_a_log", 2, 1), ("ssd_d", 2, 2),
                 ("ssd_norm_w", 3, 0), ("ln1_g", 3, 1), ("ln1_b", 3, 2), ("ln2_g", 3, 3), ("ln2_b", 3, 4),
                 ("ln3_g", 3, 5), ("ln3_b", 3, 6)]
    vec = lambda a: a.reshape(1, -1)
    items = [(si, r0, vec(given[n]), vec(given["m_" + n]), vec(given["v_" + n])) for n, si, r0 in row_items]
    own = [(si, 0, wsh[n], msh[n], vsh[n]) for n, si in (("lru_conv_w", 0), ("ssd_conv_w", 1))]
    me1 = jnp.reshape(me, (1,)).astype(jnp.int32)
    res, loss_row = _adamw_rows([gat[k] for k in SMALL_SRC[:4]], items, own, me1, (3, 7), name="adamw_small")
    loss = loss_row[0, 0]
    for (n, _si, _r0), r4 in zip(row_items, res[:len(row_items)]):
        outs[n] = r4
    for n, r4 in zip(CONV, res[len(row_items):]):
        outs[n] = r4

    def fin(n, k):
        a = jnp.swapaxes(outs[n][k], 0, 1) if n == 'w_in' else outs[n][k]
        return a.reshape(given[n].shape)

    return (loss, grad_x[None],
            *[fin(n, 0) for n in WEIGHTS], *[fin(n, 1) for n in WEIGHTS],
            *[fin(n, 2) for n in WEIGHTS], *[fin(n, 3) for n in WEIGHTS])
```

```python
import math

import jax
import jax.numpy as jnp
from jax import lax
from jax.experimental import pallas as pl
from jax.experimental.pallas import tpu as pltpu

F32 = jnp.float32
BF16 = jnp.bfloat16

N_DEV = 8
D_MODEL = 1024
LRU_W = 1024
SSD_W = 1024
XBC = 2048
N_HEAD = 16
HEAD_P = 64
N_GROUP = 4
GROUP_W = 256
N_STATE = 128
CHUNK = 128
D_IN = 5136
D_IN_PAD = 5632
COL_G = 1024
COL_Z = 2048
COL_XBC = 3072
COL_DT = 5120
LRU_C = 8.0
ALPHA = 2.0 ** 0.25
LN_EPS = 1e-5
RMS_EPS = 1e-5
ADAM_LR = 0.001
ADAM_B1 = 0.9
ADAM_B2 = 0.999
ADAM_EPS = 1e-08
ADAM_WD = 0.01
ADAM_STEP = 10
GELU_C = math.sqrt(2.0 / math.pi)
LANE = 128
SUBLANE = 8
VMEM_LIMIT = 48 * 1024 * 1024
MESH_T = pl.DeviceIdType.MESH
NEG_BIG = -1e30


def _pcall(body, **kw):
    return pl.pallas_call(body, **kw)


def _cparams(sem):
    return pltpu.CompilerParams(dimension_semantics=sem, vmem_limit_bytes=VMEM_LIMIT)


def _dot(a, b):
    return jnp.dot(a.astype(BF16), b.astype(BF16), preferred_element_type=F32)


def _dot_nt(a, b):
    return lax.dot_general(a.astype(BF16), b.astype(BF16), (((1,), (1,)), ((), ())), preferred_element_type=F32)


def _dot_tn(a, b):
    return lax.dot_general(a.astype(BF16), b.astype(BF16), (((0,), (0,)), ((), ())), preferred_element_type=F32)


def _sigmoid(x):
    return jax.nn.sigmoid(x)


def _softplus(v):
    return jnp.maximum(v, 0.0) + jnp.log1p(jnp.exp(-jnp.abs(v)))


def _gelu(x):
    th = jnp.tanh(GELU_C * (x + 0.044715 * x * x * x))
    return 0.5 * x * (1.0 + th), th


def _gelu_grad(x, th):
    return 0.5 * (1.0 + th) + 0.5 * x * (1.0 - th * th) * GELU_C * (1.0 + 3.0 * 0.044715 * x * x)


def _iota(shape, dim):
    return lax.broadcasted_iota(jnp.int32, shape, dim)


def _mm(a, b, mode, *, tm, tn, name, a_fn=None, extra=None, epi=None, out_dtype=F32, dest_major=False, into=None,
        jobs=()):
    m = a.shape[1] if mode == "tn" else a.shape[0]
    n = b.shape[0] if mode == "nt" else b.shape[1]
    tm, tn = min(tm, m), min(tn, n)
    if dest_major:
        tn = n // N_DEV
    if mode == "nn":
        m, k = a.shape
        _, n = b.shape
        a_spec = pl.BlockSpec((tm, k), lambda i, j: (i, 0))
        b_spec = pl.BlockSpec((k, tn), lambda i, j: (0, j))
        dims = ((1,), (0,))
    elif mode == "nt":
        m, k = a.shape
        n, _ = b.shape
        a_spec = pl.BlockSpec((tm, k), lambda i, j: (i, 0))
        b_spec = pl.BlockSpec((tn, k), lambda i, j: (j, 0))
        dims = ((1,), (1,))
    else:
        k, m = a.shape
        _, n = b.shape
        a_spec = pl.BlockSpec((k, tm), lambda i, j: (0, i))
        b_spec = pl.BlockSpec((k, tn), lambda i, j: (0, j))
        dims = ((0,), (0,))
    assert m % tm == 0 and n % tn == 0, (name, m, n, tm, tn)
    o_spec = pl.BlockSpec((tm, tn), lambda i, j: (i, j))
    in_specs = [a_spec, b_spec]
    args = [a, b]
    if extra is not None:
        in_specs.append(o_spec)
        args.append(extra)

    def body(*refs):
        a_ref, b_ref, o_ref = refs[0], refs[1], refs[-1]
        av = a_ref[...]
        if a_fn is not None:
            av = a_fn(av)
        acc = lax.dot_general(av.astype(BF16), b_ref[...].astype(BF16), (dims, ((), ())), preferred_element_type=F32)
        if epi is not None:
            acc = epi(acc, refs[2][...])
        o_ref[...] = acc.astype(out_dtype)

    out_shape = jax.ShapeDtypeStruct((m, n), out_dtype)
    aliases = None
    if dest_major:
        assert extra is None
        o_spec = pl.BlockSpec((None, tm, tn), lambda i, j: (j, i, 0))
        out_shape = jax.ShapeDtypeStruct((N_DEV, m, tn), out_dtype)
    if into is not None:
        buf, row0, total = into
        assert extra is None and row0 % tm == 0
        o_spec = pl.BlockSpec((tm, tn), lambda i, j: (row0 // tm + i, j))
        out_shape = jax.ShapeDtypeStruct((total, n), out_dtype)
        if buf is not None:
            in_specs.append(ANY_SPEC)
            args.append(buf)
            aliases = {len(args) - 1: 0}
    (out,), jouts = _hosted(body, jobs, grid=(m // tm, n // tn), in_specs=in_specs, out_specs=[o_spec],
                            out_shape=[out_shape], args=args, name=name, aliases=aliases)
    return (out, jouts) if jobs else out


def _mm_pieces(pieces, offsets, b, *, tm, name, extra, epi, jobs=()):
    m = pieces[0].shape[0]
    kb, n = b.shape
    tm = min(tm, m)
    row = lambda wdt: pl.BlockSpec((tm, wdt), lambda i: (i, 0))
    in_specs = [row(pc.shape[1]) for pc in pieces] + [pl.BlockSpec((kb, n), lambda i: (0, 0)), row(n)]
    np_ = len(pieces)

    def body(*refs):
        b_ref, e_ref, o_ref = refs[np_], refs[np_ + 1], refs[np_ + 2]
        acc = jnp.zeros((tm, n), F32)
        for q in range(np_):
            kq = pieces[q].shape[1]
            acc = acc + jnp.dot(refs[q][...].astype(BF16), b_ref[offsets[q]:offsets[q] + kq, :].astype(BF16),
                                preferred_element_type=F32)
        o_ref[...] = epi(acc, e_ref[...])

    (out,), jouts = _hosted(body, jobs, grid=(m // tm,), in_specs=in_specs, out_specs=[row(n)],
                            out_shape=[jax.ShapeDtypeStruct((m, n), F32)], args=list(pieces) + [b, extra], name=name)
    return (out, jouts) if jobs else out


def _relu2(v):
    r = jnp.maximum(v, 0.0)
    return r * r


ROW_TILE = 256


def _ln_stats(t):
    mu = jnp.mean(t, axis=-1, keepdims=True)
    xc = t - mu
    var = jnp.mean(xc * xc, axis=-1, keepdims=True)
    rstd = lax.rsqrt(var + LN_EPS)
    return xc * rstd, rstd


def _ln_bwd_rows(dy, xhat, rstd, g):
    dxh = dy * g
    m1 = jnp.mean(dxh, axis=-1, keepdims=True)
    m2 = jnp.mean(dxh * xhat, axis=-1, keepdims=True)
    return rstd * (dxh - m1 - xhat * m2)


def _mm_ln(a, b, res, g, beta, *, tm, name, a_fn=None):
    m, k = a.shape
    d = b.shape[1]
    tm = min(tm, m)
    row = pl.BlockSpec((tm, d), lambda i: (i, 0))
    par = pl.BlockSpec((1, d), lambda i: (0, 0))

    def body(a_ref, b_ref, r_ref, g_ref, be_ref, br_ref, y_ref, yb_ref):
        av = a_ref[...]
        if a_fn is not None:
            av = a_fn(av)
        acc = jnp.dot(av.astype(BF16), b_ref[...].astype(BF16), preferred_element_type=F32)
        br_ref[...] = acc
        xhat, _ = _ln_stats(ALPHA * r_ref[...] + acc)
        y = xhat * g_ref[...] + be_ref[...]
        y_ref[...] = y
        yb_ref[...] = y.astype(BF16)

    sd = jax.ShapeDtypeStruct((m, d), F32)
    return _pcall(body, grid=(m // tm,),
                  in_specs=[pl.BlockSpec((tm, k), lambda i: (i, 0)), pl.BlockSpec((k, d), lambda i: (0, 0)), row, par, par],
                  out_specs=(row, row, row), out_shape=(sd, sd, jax.ShapeDtypeStruct((m, d), BF16)), name=name,
                  compiler_params=_cparams(("parallel",)))(a, b, res, g, beta)


def _mm_ln_bwd(a, b, res, branch, g, dy0, coef0, *, tm, name, jobs=()):
    m, k = a.shape
    d = b.shape[0]
    tm = min(tm, m)
    row = pl.BlockSpec((tm, d), lambda i: (i, 0))
    par = pl.BlockSpec((1, d), lambda i: (0, 0))

    def body(a_ref, b_ref, r_ref, br_ref, g_ref, dy0_ref, dt_ref, dtb_ref, dg_ref, db_ref):
        acc = lax.dot_general(a_ref[...].astype(BF16), b_ref[...].astype(BF16), (((1,), (1,)), ((), ())),
                              preferred_element_type=F32)
        dy = coef0 * dy0_ref[...] + acc
        xhat, rstd = _ln_stats(ALPHA * r_ref[...] + br_ref[...])
        dt = _ln_bwd_rows(dy, xhat, rstd, g_ref[...])
        dt_ref[...] = dt
        dtb_ref[...] = dt.astype(BF16)

        @pl.when(pl.program_id(0) == 0)
        def _():
            dg_ref[...] = jnp.zeros_like(dg_ref)
            db_ref[...] = jnp.zeros_like(db_ref)

        dg_ref[...] += jnp.sum(dy * xhat, axis=0, keepdims=True)
        db_ref[...] += jnp.sum(dy, axis=0, keepdims=True)

    pd = jax.ShapeDtypeStruct((1, d), F32)
    outs, jouts = _hosted(
        body, jobs, grid=(m // tm,),
        in_specs=[pl.BlockSpec((tm, k), lambda i: (i, 0)), pl.BlockSpec((d, k), lambda i: (0, 0)), row, row, par, row],
        out_specs=(row, row, par, par),
        out_shape=(jax.ShapeDtypeStruct((m, d), F32), jax.ShapeDtypeStruct((m, d), BF16), pd, pd),
        args=(a, b, res, branch, g, dy0), name=name)
    return (tuple(outs), jouts) if jobs else tuple(outs)


def _head(x2, x2b, p, wg, wp, g, beta, tgt, *, name):
    s, d = x2.shape
    tile = 2 * ROW_TILE
    row = pl.BlockSpec((tile, d), lambda i: (i, 0))
    par = pl.BlockSpec((1, d), lambda i: (0, 0))
    lsp = pl.BlockSpec((1, LANE), lambda i: (0, 0))
    whole = lambda a: pl.BlockSpec(a.shape, lambda i: (0, 0))

    def body(x2_ref, x2b_ref, p_ref, wg_ref, wp_ref, g_ref, be_ref, t_ref,
             loss_ref, dgp_ref, dple_ref, dt_ref, dg_ref, db_ref):
        gate = _sigmoid(_dot(x2b_ref[...], wg_ref[...]))
        ple_v = _dot(p_ref[...], wp_ref[...])
        xhat, rstd = _ln_stats(ALPHA * x2_ref[...] + gate * ple_v)
        err = xhat * g_ref[...] + be_ref[...] - t_ref[...]
        dy = err * (1.0 / d)
        dt = _ln_bwd_rows(dy, xhat, rstd, g_ref[...])
        dt_ref[...] = dt
        dgp_ref[...] = (dt * ple_v * gate * (1.0 - gate)).astype(BF16)
        dple_ref[...] = (dt * gate).astype(BF16)

        @pl.when(pl.program_id(0) == 0)
        def _():
            loss_ref[...] = jnp.zeros_like(loss_ref)
            dg_ref[...] = jnp.zeros_like(dg_ref)
            db_ref[...] = jnp.zeros_like(db_ref)

        loss_ref[...] += 0.5 * jnp.sum(jnp.mean(err * err, axis=-1, keepdims=True))
        dg_ref[...] += jnp.sum(dy * xhat, axis=0, keepdims=True)
        db_ref[...] += jnp.sum(dy, axis=0, keepdims=True)

    sd = jax.ShapeDtypeStruct((s, d), F32)
    sb = jax.ShapeDtypeStruct((s, d), BF16)
    pd = jax.ShapeDtypeStruct((1, d), F32)
    return _pcall(body, grid=(s // tile,),
                  in_specs=[row, row, pl.BlockSpec((tile, p.shape[1]), lambda i: (i, 0)), whole(wg), whole(wp), par, par,
                            row],
                  out_specs=(lsp, row, row, row, par, par),
                  out_shape=(jax.ShapeDtypeStruct((1, LANE), F32), sb, sb, sd, pd, pd),
                  name=name, compiler_params=_cparams(("arbitrary",)))(x2, x2b, p, wg, wp, g, beta, tgt)


CONV_R = 256
PAD = SUBLANE


def _shift_down(ext, s):
    if s == 0:
        return ext[PAD:, :]
    return pltpu.roll(ext, s, 0)[PAD:, :]


def _shift_up(ext, s):
    r = ext.shape[0] - PAD
    if s == 0:
        return ext[:r, :]
    return pltpu.roll(ext, r + PAD - s, 0)[:r, :]


def _conv_rows(xpad_ref, r0, w_ref):
    ext = xpad_ref[pl.ds(r0, CONV_R + PAD), :]
    acc = _shift_down(ext, 0) * w_ref[3:4, :]
    for k in range(3):
        acc = acc + _shift_down(ext, 3 - k) * w_ref[k:k + 1, :]
    return acc, ext


def _fill_front_padded(dst_ref, src_ref, s):
    dst_ref[0:PAD, :] = jnp.zeros((PAD, dst_ref.shape[1]), F32)

    def cp(q, _):
        r0 = pl.multiple_of(q * CONV_R, CONV_R)
        dst_ref[pl.ds(pl.multiple_of(PAD + r0, PAD), CONV_R), :] = src_ref[pl.ds(r0, CONV_R), :]
        return 0

    lax.fori_loop(0, s // CONV_R, cp, 0)


def _conv_silu_fwd(proj, w8, b, *, col0, width, ct, name, jobs=()):
    s = proj.shape[0]
    nb = col0 // ct

    def body(x_ref, w_ref, b_ref, o_ref, xpad):
        _fill_front_padded(xpad, x_ref, s)

        def step(q, _):
            r0 = pl.multiple_of(q * CONV_R, CONV_R)
            acc, _e = _conv_rows(xpad, r0, w_ref)
            pre = acc + b_ref[...]
            o_ref[pl.ds(r0, CONV_R), :] = pre * _sigmoid(pre)
            return 0

        lax.fori_loop(0, s // CONV_R, step, 0)

    (out,), jouts = _hosted(
        body, jobs, grid=(width // ct,),
        in_specs=[pl.BlockSpec((s, ct), lambda j: (0, nb + j)), pl.BlockSpec((SUBLANE, ct), lambda j: (0, j)),
                  pl.BlockSpec((1, ct), lambda j: (0, j))],
        out_specs=[pl.BlockSpec((s, ct), lambda j: (0, j))],
        out_shape=[jax.ShapeDtypeStruct((s, width), F32)],
        scratch_shapes=[pltpu.VMEM((s + PAD, ct), F32)], name=name, args=(proj, w8, b))
    return (out, jouts) if jobs else out


def _conv_bwd_rows(dpad_ref, r0, w_ref):
    return _conv_bwd_ext(dpad_ref[pl.ds(r0, CONV_R + PAD), :], w_ref)


def _conv_bwd_ext(ext, w_ref):
    acc = _shift_up(ext, 0) * w_ref[3:4, :]
    for k in range(3):
        acc = acc + _shift_up(ext, 3 - k) * w_ref[k:k + 1, :]
    return acc


def _conv_silu_bwd(proj, dact, w8, b, *, col0, width, ct, name, jobs=()):
    s = proj.shape[0]
    nb = col0 // ct

    def body(x_ref, d_ref, w_ref, b_ref, dx_ref, dwb_ref, xpad, dpad):
        _fill_front_padded(xpad, x_ref, s)
        dpad[pl.ds(s, PAD), :] = jnp.zeros((PAD, ct), F32)
        dwb_ref[...] = jnp.zeros_like(dwb_ref)

        def step(q, _):
            r0 = pl.multiple_of(q * CONV_R, CONV_R)
            acc, ext = _conv_rows(xpad, r0, w_ref)
            pre = acc + b_ref[...]
            sg = _sigmoid(pre)
            dpre = d_ref[pl.ds(r0, CONV_R), :] * sg * (1.0 + pre * (1.0 - sg))
            dpad[pl.ds(r0, CONV_R), :] = dpre
            for k in range(4):
                dwb_ref[k:k + 1, :] += jnp.sum(dpre * _shift_down(ext, 3 - k), axis=0, keepdims=True)
            dwb_ref[4:5, :] += jnp.sum(dpre, axis=0, keepdims=True)
            return 0

        lax.fori_loop(0, s // CONV_R, step, 0)

        def step2(q, _):
            r0 = pl.multiple_of(q * CONV_R, CONV_R)
            dx_ref[pl.ds(r0, CONV_R), :] = _conv_bwd_rows(dpad, r0, w_ref).astype(BF16)
            return 0

        lax.fori_loop(0, s // CONV_R, step2, 0)

    colb = pl.BlockSpec((s, ct), lambda j: (0, j))
    outs, jouts = _hosted(
        body, jobs, grid=(width // ct,),
        in_specs=[pl.BlockSpec((s, ct), lambda j: (0, nb + j)), colb, pl.BlockSpec((SUBLANE, ct), lambda j: (0, j)),
                  pl.BlockSpec((1, ct), lambda j: (0, j))],
        out_specs=(colb, pl.BlockSpec((SUBLANE, ct), lambda j: (0, j))),
        out_shape=(jax.ShapeDtypeStruct((s, width), BF16), jax.ShapeDtypeStruct((SUBLANE, width), F32)),
        scratch_shapes=[pltpu.VMEM((s + PAD, ct), F32), pltpu.VMEM((s + PAD, ct), F32)], name=name,
        args=(proj, dact, w8, b))
    return (tuple(outs), jouts) if jobs else tuple(outs)


LRU_CT = 256


def _row_of(v, r):
    return jnp.sum(jnp.where(_iota((v.shape[0], 1), 0) == r, v, 0.0), axis=0, keepdims=True)


def _scan_fwd(a, u):
    r = a.shape[0]
    row = _iota((r, 1), 0)
    d = 1
    while d < r:
        valid = row >= d
        u = jnp.where(valid, a * pltpu.roll(u, d, 0) + u, u)
        a = jnp.where(valid, a * pltpu.roll(a, d, 0), a)
        d *= 2
    return a, u


def _scan_rev(b, u):
    r = b.shape[0]
    row = _iota((r, 1), 0)
    d = 1
    while d < r:
        valid = row < r - d
        u = jnp.where(valid, b * pltpu.roll(u, r - d, 0) + u, u)
        b = jnp.where(valid, b * pltpu.roll(b, r - d, 0), b)
        d *= 2
    return b, u


def _lru_chunk(xpad, r0, cw_ref, cb, wa, ba, wx, bx, sp):
    acc, ext = _conv_rows(xpad, r0, cw_ref)
    xl = acc + cb
    r = _sigmoid(_dot(xl, wa) + ba)
    i = _sigmoid(_dot(xl, wx) + bx)
    la = -LRU_C * r * sp
    a = jnp.exp(la)
    a2 = jnp.exp(2.0 * la)
    mult = jnp.sqrt(-jnp.tanh(la) * (a2 + 1.0))
    first = (r0 + _iota((CONV_R, 1), 0)) == 0
    mult = jnp.where(first, 1.0, mult)
    return ext, xl, r, i, a, a2, mult, first


def _lru_specs(s):
    ct = LRU_CT
    nb_g = COL_G // ct
    return dict(
        x=pl.BlockSpec((s, ct), lambda j: (0, j)),
        g=pl.BlockSpec((s, ct), lambda j: (0, nb_g + j)),
        col=pl.BlockSpec((s, ct), lambda j: (0, j)),
        cw=pl.BlockSpec((SUBLANE, ct), lambda j: (0, j)),
        vec=pl.BlockSpec((1, ct), lambda j: (0, j)),
        gate=pl.BlockSpec((None, ct, ct), lambda j: (j, 0, 0)),
    )


def _lru_fwd(proj, cw8, cb, wa_bd, ba, wx_bd, bx, ap, *, name, jobs=()):
    s = proj.shape[0]
    ct = LRU_CT
    sp_ = _lru_specs(s)

    def body(x_ref, g_ref, cw_ref, cb_ref, wa_ref, ba_ref, wx_ref, bx_ref, ap_ref, y_ref, h_ref, xpad):
        _fill_front_padded(xpad, x_ref, s)
        sp = _softplus(-ap_ref[...])

        def step(q, carry):
            r0 = pl.multiple_of(q * CONV_R, CONV_R)
            _e, xl, _r, i, a, _a2, mult, _f = _lru_chunk(xpad, r0, cw_ref, cb_ref[...], wa_ref[...], ba_ref[...],
                                                       wx_ref[...], bx_ref[...], sp)
            acum, ucum = _scan_fwd(a, xl * i * mult)
            h = acum * carry + ucum
            h_ref[pl.ds(r0, CONV_R), :] = h
            ge, _th = _gelu(g_ref[pl.ds(r0, CONV_R), :])
            y_ref[pl.ds(r0, CONV_R), :] = (ge * h).astype(BF16)
            return _row_of(h, CONV_R - 1)

        lax.fori_loop(0, s // CONV_R, step, jnp.zeros((1, ct), F32))

    (ymix, hs), jouts = _hosted(
        body, jobs, grid=(LRU_W // ct,),
        in_specs=[sp_["x"], sp_["g"], sp_["cw"], sp_["vec"], sp_["gate"], sp_["vec"], sp_["gate"], sp_["vec"], sp_["vec"]],
        out_specs=(sp_["col"], sp_["col"]),
        out_shape=(jax.ShapeDtypeStruct((s, LRU_W + SSD_W), BF16), jax.ShapeDtypeStruct((s, LRU_W), F32)),
        scratch_shapes=[pltpu.VMEM((s + PAD, ct), F32)],
        name=name, args=(proj, proj, cw8, cb, wa_bd, ba, wx_bd, bx, ap))
    return ((ymix, hs), jouts) if jobs else (ymix, hs)


def _lru_bwd(proj, dy, hs, cw8, cb, wa_bd, ba, wx_bd, bx, ap, *, name, jobs=()):
    s = proj.shape[0]
    ct = LRU_CT
    sp_ = _lru_specs(s)

    nq = s // CONV_R

    def body(x_ref, g_ref, dy_ref, h_ref, cw_ref, cb_ref, wa_ref, ba_ref, wx_ref, bx_ref, ap_ref,
             dx_ref, dg_ref, dcwb_ref, dwa_ref, dwx_ref, xpad, hpad):
        _fill_front_padded(xpad, x_ref, s)
        _fill_front_padded(hpad, h_ref, s)
        apv = ap_ref[...]
        sp = _softplus(-apv)
        cb_v, wa, ba_v, wx, bx_v = cb_ref[...], wa_ref[...], ba_ref[...], wx_ref[...], bx_ref[...]
        dcwb_ref[...] = jnp.zeros_like(dcwb_ref)
        dwa_ref[...] = jnp.zeros_like(dwa_ref)
        dwx_ref[...] = jnp.zeros_like(dwx_ref)

        def back(k, carry):
            g_next, a_next, dxl_next = carry
            last_row = _iota((CONV_R, 1), 0) == CONV_R - 1
            r0 = pl.multiple_of((nq - 1 - k) * CONV_R, CONV_R)
            ext, xl, r, i, a, a2, mult, first = _lru_chunk(xpad, r0, cw_ref, cb_v, wa, ba_v, wx, bx_v, sp)
            gv = g_ref[pl.ds(r0, CONV_R), :]
            dyv = dy_ref[pl.ds(r0, CONV_R), :]
            hext = hpad[pl.ds(r0, CONV_R + PAD), :]
            ge, th = _gelu(gv)
            dg_ref[pl.ds(r0, CONV_R), :] = (dyv * _shift_down(hext, 0) * _gelu_grad(gv, th)).astype(BF16)
            b = jnp.where(last_row, a_next, pltpu.roll(a, CONV_R - 1, 0))
            bcum, dcum = _scan_rev(b, dyv * ge)
            gval = dcum + bcum * g_next
            hprev = _shift_down(hext, 1)
            da = gval * hprev
            dxl = gval * i * mult
            di = gval * xl * mult
            dmult = jnp.where(first, 0.0, gval * xl * i)
            dla = da * a - dmult * a2 / mult
            dr = dla * (-LRU_C) * sp
            dcwb_ref[7:8, :] += jnp.sum(dla * (-LRU_C) * r, axis=0, keepdims=True)
            dpr = dr * r * (1.0 - r)
            dpi = di * i * (1.0 - i)
            dxl = dxl + _dot_nt(dpr, wa) + _dot_nt(dpi, wx)
            dwa_ref[...] += _dot_tn(xl, dpr)
            dwx_ref[...] += _dot_tn(xl, dpi)
            dcwb_ref[5:6, :] += jnp.sum(dpr, axis=0, keepdims=True)
            dcwb_ref[6:7, :] += jnp.sum(dpi, axis=0, keepdims=True)
            for tap in range(4):
                dcwb_ref[tap:tap + 1, :] += jnp.sum(dxl * _shift_down(ext, 3 - tap), axis=0, keepdims=True)
            dcwb_ref[4:5, :] += jnp.sum(dxl, axis=0, keepdims=True)
            dx_ref[pl.ds(r0, CONV_R), :] = _conv_bwd_ext(jnp.concatenate([dxl, dxl_next], axis=0), cw_ref).astype(BF16)
            return _row_of(gval, 0), _row_of(a, 0), dxl[:PAD, :]

        zero = jnp.zeros((1, ct), F32)
        lax.fori_loop(0, nq, back, (zero, zero, jnp.zeros((PAD, ct), F32)))
        dcwb_ref[7:8, :] = dcwb_ref[7:8, :] * (-_sigmoid(-apv))

    nt = LRU_W // ct
    outs, jouts = _hosted(
        body, jobs, grid=(nt,),
        in_specs=[sp_["x"], sp_["g"], sp_["col"], sp_["col"], sp_["cw"], sp_["vec"], sp_["gate"], sp_["vec"], sp_["gate"],
                  sp_["vec"], sp_["vec"]],
        out_specs=(sp_["col"], sp_["col"], sp_["cw"], sp_["gate"], sp_["gate"]),
        out_shape=(jax.ShapeDtypeStruct((s, LRU_W), BF16), jax.ShapeDtypeStruct((s, LRU_W), BF16),
                   jax.ShapeDtypeStruct((SUBLANE, LRU_W), F32), jax.ShapeDtypeStruct((nt, ct, ct), F32),
                   jax.ShapeDtypeStruct((nt, ct, ct), F32)),
        scratch_shapes=[pltpu.VMEM((s + PAD, ct), F32), pltpu.VMEM((s + PAD, ct), F32)],
        name=name, args=(proj, proj, dy, hs, cw8, cb, wa_bd, ba, wx_bd, bx, ap))
    return (tuple(outs), jouts) if jobs else tuple(outs)


def _split3(v):
    hi = v.astype(BF16)
    r1 = v - hi.astype(F32)
    mid = r1.astype(BF16)
    lo = (r1 - mid.astype(F32)).astype(BF16)
    return hi, mid, lo


def _dot01(m01, v):
    mb = m01.astype(BF16)
    hi, mid, lo = _split3(v)
    f = lambda part: jnp.dot(mb, part, preferred_element_type=F32)
    return f(hi) + f(mid) + f(lo)


def _dot01_r(v, m01, parts=3):
    mb = m01.astype(BF16)
    acc = None
    for part in _split3(v)[:parts]:
        t = jnp.dot(part, mb, preferred_element_type=F32)
        acc = t if acc is None else acc + t
    return acc


def _ssd_prep(dtr, bias, alog_pad):
    l = CHUNK
    lane = _iota((1, LANE), 1)
    a_head = jnp.where(lane < N_HEAD, -jnp.exp(alog_pad), 0.0)
    dt = _softplus(dtr + bias)
    tril = (_iota((l, l), 1) <= _iota((l, l), 0)).astype(F32)
    a = dt * a_head
    cs = _dot01(tril, a)
    tot = jnp.sum(a, axis=0, keepdims=True)
    return dict(a_head=a_head, dt=dt, tril=tril, cs=cs, tot=tot)


def _col(v, h):
    lane = _iota(v.shape, 1)
    return jnp.sum(jnp.where(lane == h, v, 0.0), axis=1, keepdims=True)


def _decay_mat(cs, cst_ref, h, causal):
    row = cst_ref[h:h + 1, :]
    return jnp.exp(jnp.where(causal, _col(cs, h) - row, NEG_BIG))


def _head_mask(j, rows=CHUNK):
    lane = _iota((rows, GROUP_W), 1)
    return (lane >= j * HEAD_P) & (lane < (j + 1) * HEAD_P)


def _over_heads(v, g):
    r = v.shape[0]
    out = jnp.zeros((r, GROUP_W), F32)
    for j in range(4):
        out = jnp.where(_head_mask(j, r), _col(v, 4 * g + j), out)
    return out


def _ssd_group_fwd(q, g, xs_g, bg, cg, ht_g, cst_ref, causal, dx_g):
    dtx_g, csx_g, totx_g = _over_heads(q["dt"], g), _over_heads(q["cs"], g), _over_heads(q["tot"], g)
    xdt = xs_g * dtx_g
    ex = jnp.exp(csx_g)
    cb = _dot_nt(cg, bg)
    yoff = _dot(cg, ht_g) * ex
    ydiag = jnp.zeros((CHUNK, GROUP_W), F32)
    lms = []
    for j in range(4):
        lms.append(_decay_mat(q["cs"], cst_ref, 4 * g + j, causal))
        ydiag = jnp.where(_head_mask(j), _dot(cb * lms[j], xdt), ydiag)
    y = ydiag + yoff + xs_g * dx_g
    dsx = jnp.exp(totx_g - csx_g)
    return y, dict(xdt=xdt, ex=ex, cb=cb, yoff=yoff, dsx=dsx, dtx=dtx_g, totx=totx_g, lms=lms)


def _gated_norm_fwd(y_g, z_g, w_g):
    sz = _sigmoid(z_g)
    silu = z_g * sz
    yf = y_g * silu
    rs = lax.rsqrt(jnp.mean(yf * yf, axis=1, keepdims=True) + RMS_EPS)
    yn = yf * rs
    return yn * w_g, (sz, silu, rs, yn)


def _ssd_fwd(xact, proj, ymix, bias_pad, alog_pad, dxp, normw, *, name, jobs=()):
    s = xact.shape[0]
    nc = s // CHUNK

    def body(xa_ref, dt_ref, z_ref, _ymix_ref, bias_ref, alp_ref, dx_ref, nw_ref, y_ref, hp_ref, ht, cst):
        @pl.when(pl.program_id(0) == 0)
        def _():
            ht[...] = jnp.zeros_like(ht)

        hp_ref[...] = ht[...]
        q = _ssd_prep(dt_ref[...], bias_ref[...], alp_ref[...])
        cst[...] = q["cs"].T
        causal = q["tril"] > 0.0
        for g in range(N_GROUP):
            sl = slice(g * GROUP_W, (g + 1) * GROUP_W)
            xs_g = xa_ref[:, sl]
            bg = xa_ref[:, SSD_W + g * N_STATE:SSD_W + (g + 1) * N_STATE]
            cg = xa_ref[:, SSD_W + N_GROUP * N_STATE + g * N_STATE:SSD_W + N_GROUP * N_STATE + (g + 1) * N_STATE]
            ht_g = ht[:, sl]
            y, f = _ssd_group_fwd(q, g, xs_g, bg, cg, ht_g, cst, causal, dx_ref[:, sl])
            out, _ = _gated_norm_fwd(y, z_ref[:, sl], nw_ref[:, sl])
            y_ref[:, sl] = out.astype(BF16)
            ht[:, sl] = jnp.exp(f["totx"]) * ht_g + _dot_tn(bg, f["xdt"] * f["dsx"])

    par = lambda w: pl.BlockSpec((1, w), lambda c: (0, 0))
    (ycat, hprev), jouts = _hosted(
        body, jobs, grid=(nc,),
        in_specs=[pl.BlockSpec((CHUNK, XBC), lambda c: (c, 0)),
                  pl.BlockSpec((CHUNK, LANE), lambda c: (c, COL_DT // LANE)),
                  pl.BlockSpec((CHUNK, SSD_W), lambda c: (c, COL_Z // SSD_W)),
                  ANY_SPEC, par(LANE), par(LANE), par(SSD_W), par(SSD_W)],
        out_specs=(pl.BlockSpec((CHUNK, SSD_W), lambda c: (c, LRU_W // SSD_W)),
                   pl.BlockSpec((None, N_STATE, SSD_W), lambda c: (c, 0, 0))),
        out_shape=(jax.ShapeDtypeStruct(ymix.shape, ymix.dtype), jax.ShapeDtypeStruct((nc, N_STATE, SSD_W), F32)),
        scratch_shapes=[pltpu.VMEM((N_STATE, SSD_W), F32), pltpu.VMEM((CHUNK, LANE), F32)],
        aliases={3: 0}, name=name, args=(xact, proj, proj, ymix, bias_pad, alog_pad, dxp, normw))
    return ((ycat, hprev), jouts) if jobs else (ycat, hprev)


def _ssd_bwd(xact, proj, dycat, hprev, bias_pad, alog_pad, dxp, normw, *, name, jobs=()):
    s = xact.shape[0]
    nc = s // CHUNK
    l = CHUNK

    def body(xa_ref, dt_ref, z_ref, dy_ref, hp_ref, bias_ref, alp_ref, dx_ref, nw_ref,
             dxa_ref, ddt_ref, dz_ref, dnw_ref, small_ref, dht, cst, accx, dcsx_s, ddtx_s):
        step = pl.program_id(0)

        @pl.when(step == 0)
        def _():
            dht[...] = jnp.zeros_like(dht)
            accx[...] = jnp.zeros_like(accx)
            dnw_ref[...] = jnp.zeros_like(dnw_ref)
            small_ref[...] = jnp.zeros_like(small_ref)

        dtr = dt_ref[...]
        q = _ssd_prep(dtr, bias_ref[...], alp_ref[...])
        cst[...] = q["cs"].T
        causal = q["tril"] > 0.0
        lane = _iota((l, LANE), 1)
        head_row = _iota((LANE, l), 0)
        dcs_head = jnp.zeros((l, LANE), F32)
        dcs_rows = jnp.zeros((LANE, l), F32)
        for g in range(N_GROUP):
            sl = slice(g * GROUP_W, (g + 1) * GROUP_W)
            slb = slice(SSD_W + g * N_STATE, SSD_W + (g + 1) * N_STATE)
            slc = slice(SSD_W + N_GROUP * N_STATE + g * N_STATE, SSD_W + N_GROUP * N_STATE + (g + 1) * N_STATE)
            xs_g, bg, cg = xa_ref[:, sl], xa_ref[:, slb], xa_ref[:, slc]
            ht_g = hp_ref[:, sl]
            dxp_g = dx_ref[:, sl]
            y, f = _ssd_group_fwd(q, g, xs_g, bg, cg, ht_g, cst, causal, dxp_g)
            z_g, nw_g = z_ref[:, sl], nw_ref[:, sl]
            _o, (sz, silu, rs, yn) = _gated_norm_fwd(y, z_g, nw_g)
            dout = dy_ref[:, sl]
            dnw_ref[:, sl] += jnp.sum(dout * yn, axis=0, keepdims=True)
            dyn = dout * nw_g
            dyf = rs * (dyn - yn * jnp.mean(dyn * yn, axis=1, keepdims=True))
            dy = dyf * silu
            dz_ref[:, sl] = (dyf * y * sz * (1.0 + z_g * (1.0 - sz))).astype(BF16)
            accx[0:1, sl] += jnp.sum(dy * xs_g, axis=0, keepdims=True)
            dyo = dy * f["ex"]
            dcg = _dot_nt(dyo, ht_g)
            dht_prev = _dot_tn(cg, dyo)
            dcsx = dy * f["yoff"]
            xdt = f["xdt"]
            dxdt = jnp.zeros((l, GROUP_W), F32)
            dcb = jnp.zeros((l, l), F32)
            for j in range(4):
                h = 4 * g + j
                lm = f["lms"][j]
                sc = f["cb"] * lm
                mask = _head_mask(j)
                ds_ = jnp.where(causal, _dot_nt(jnp.where(mask, dy, 0.0), xdt), 0.0)
                dxdt = jnp.where(mask, _dot_tn(sc, dy), dxdt)
                dcb = dcb + ds_ * lm
                m = ds_ * sc
                dcs_head = dcs_head + jnp.where(lane == h, jnp.sum(m, axis=1, keepdims=True), 0.0)
                dcs_rows = dcs_rows + jnp.where(head_row == h, jnp.sum(m, axis=0, keepdims=True), 0.0)
            dhn = dht[:, sl]
            etot = jnp.exp(f["totx"])
            dxd = _dot(bg, dhn)
            dbg = _dot_nt(xdt * f["dsx"], dhn)
            dxdt = dxdt + dxd * f["dsx"]
            qq = dxd * xdt * f["dsx"]
            dcsx = dcsx - qq
            dtot = jnp.sum(qq, axis=0, keepdims=True) + jnp.sum(dhn * ht_g, axis=0, keepdims=True) * etot
            dht[:, sl] = etot * dhn + dht_prev
            dcg = dcg + _dot(dcb, bg)
            dbg = dbg + _dot_tn(dcb, cg)
            dxa_ref[:, sl] = dxdt * f["dtx"] + dy * dxp_g
            dxa_ref[:, slb] = dbg
            dxa_ref[:, slc] = dcg
            dcsx_s[:, sl] = dcsx
            ddtx_s[:, sl] = dxdt * xs_g
            accx[2:3, sl] = dtot
        reduce = (jnp.right_shift(_iota((SSD_W, LANE), 0), 6) == _iota((SSD_W, LANE), 1)).astype(F32)
        triu = (_iota((l, l), 1) >= _iota((l, l), 0)).astype(F32)
        dtot = _dot01_r(accx[...], reduce)[2:3, :]
        dcs_head = dcs_head - dcs_rows.T
        da_head = _dot01(triu, dcs_head + _dot01_r(dcsx_s[...], reduce, parts=2)) + dtot
        ddt = _dot01_r(ddtx_s[...], reduce, parts=2) + da_head * q["a_head"]
        small_ref[1:2, :] += jnp.sum(da_head * q["dt"], axis=0, keepdims=True)
        ddtr = ddt * _sigmoid(dtr + bias_ref[...])
        ddt_ref[...] = ddtr.astype(BF16)
        small_ref[0:1, :] += jnp.sum(ddtr, axis=0, keepdims=True)

        @pl.when(step == nc - 1)
        def _():
            small_ref[1:2, :] = small_ref[1:2, :] * q["a_head"]
            small_ref[2:3, :] = _dot01_r(accx[...], reduce)[0:1, :]

    rev = lambda c: nc - 1 - c
    par = lambda w: pl.BlockSpec((1, w), lambda c: (0, 0))
    outs, jouts = _hosted(
        body, jobs, grid=(nc,),
        in_specs=[pl.BlockSpec((CHUNK, XBC), lambda c: (rev(c), 0)),
                  pl.BlockSpec((CHUNK, LANE), lambda c: (rev(c), COL_DT // LANE)),
                  pl.BlockSpec((CHUNK, SSD_W), lambda c: (rev(c), COL_Z // SSD_W)),
                  pl.BlockSpec((CHUNK, SSD_W), lambda c: (rev(c), 1)),
                  pl.BlockSpec((None, N_STATE, SSD_W), lambda c: (rev(c), 0, 0)),
                  par(LANE), par(LANE), par(SSD_W), par(SSD_W)],
        out_specs=(pl.BlockSpec((CHUNK, XBC), lambda c: (rev(c), 0)),
                   pl.BlockSpec((CHUNK, LANE), lambda c: (rev(c), 0)),
                   pl.BlockSpec((CHUNK, SSD_W), lambda c: (rev(c), 0)),
                   par(SSD_W), pl.BlockSpec((SUBLANE, LANE), lambda c: (0, 0))),
        out_shape=(jax.ShapeDtypeStruct((s, XBC), F32), jax.ShapeDtypeStruct((s, LANE), BF16),
                   jax.ShapeDtypeStruct((s, SSD_W), BF16), jax.ShapeDtypeStruct((1, SSD_W), F32),
                   jax.ShapeDtypeStruct((SUBLANE, LANE), F32)),
        scratch_shapes=[pltpu.VMEM((N_STATE, SSD_W), F32), pltpu.VMEM((CHUNK, LANE), F32),
                        pltpu.VMEM((SUBLANE, SSD_W), F32), pltpu.VMEM((CHUNK, SSD_W), F32),
                        pltpu.VMEM((CHUNK, SSD_W), F32)],
        name=name, args=(xact, proj, proj, dycat, hprev, bias_pad, alog_pad, dxp, normw))
    return (tuple(outs), jouts) if jobs else tuple(outs)


def _blockdiag(w):
    per = LRU_CT // HEAD_P
    w2 = w.reshape(N_HEAD // per, per, HEAD_P, HEAD_P)
    z = jnp.zeros((N_HEAD // per, HEAD_P, HEAD_P), w.dtype)
    rows = [jnp.concatenate([w2[:, i] if j == i else z for j in range(per)], axis=2) for i in range(per)]
    return jnp.concatenate(rows, axis=1)


def _unblockdiag(wbd):
    per = LRU_CT // HEAD_P
    parts = [wbd[:, i * HEAD_P:(i + 1) * HEAD_P, i * HEAD_P:(i + 1) * HEAD_P] for i in range(per)]
    return jnp.stack(parts, axis=1).reshape(N_HEAD, HEAD_P, HEAD_P)


def _pad_rows8(w):
    return jnp.concatenate([w, jnp.zeros((SUBLANE - w.shape[0], w.shape[1]), w.dtype)], axis=0)


def _pad_lane(v):
    return jnp.concatenate([v, jnp.zeros((1, LANE - v.shape[1]), v.dtype)], axis=1)


class _NoExchange:
    def ride(self, host):
        return []

    def done(self, jobs, outs, w):
        pass

    def grad(self, name, val):
        pass

    def small(self, raw):
        pass

    def pairs_now(self):
        pass


def _local_step(x, p, tgt, w, hooks=_NoExchange()):
    cw_l = _pad_rows8(w["lru_conv_w"])
    cw_s = _pad_rows8(w["ssd_conv_w"])
    wa_bd = _blockdiag(w["lru_gate_a_w"])
    wx_bd = _blockdiag(w["lru_gate_x_w"])
    ba = w["lru_gate_a_b"].reshape(1, LRU_W)
    bx = w["lru_gate_x_b"].reshape(1, LRU_W)
    bias_pad = _pad_lane(w["ssd_dt_bias"])
    alog_pad = _pad_lane(w["ssd_a_log"])
    dxp = jnp.repeat(w["ssd_d"], HEAD_P, axis=1)

    def host(fn, *a, name, **k):
        jobs = hooks.ride(name)
        res = fn(*a, name=name, jobs=jobs, **k)
        if jobs:
            res, jouts = res
            hooks.done(jobs, jouts, w)
        return res

    def grad(n, val):
        g[n] = val
        hooks.grad(n, val)

    xb = x.astype(BF16)
    proj = host(_mm, xb, w["w_in_t"], "nt", tm=2048, tn=512, name="in_proj")
    ymix, h_lru = host(_lru_fwd, proj, cw_l, w["lru_conv_b"], wa_bd, ba, wx_bd, bx, w["lru_a_param"], name="lru_fwd")
    xact = host(_conv_silu_fwd, proj, cw_s, w["ssd_conv_b"], col0=COL_XBC, width=XBC, ct=256, name="ssd_conv_fwd")
    ycat, hprev = host(_ssd_fwd, xact, proj, ymix, bias_pad, alog_pad, dxp, w["ssd_norm_w"], name="ssd_fwd")
    mix, x1, x1b = _mm_ln(ycat, w["w_out"], x, w["ln1_g"], w["ln1_b"], tm=512, name="out_proj")
    pre = host(_mm, x1b, w["w_ff1"], "nn", tm=2048, tn=512, out_dtype=BF16, name="ff1")
    ff, x2, x2b = _mm_ln(pre, w["w_ff2"], x1, w["ln2_g"], w["ln2_b"], tm=512, a_fn=_relu2, name="ff2")
    loss, dgpre, dple, dt3, dg3, db3 = _head(x2, x2b, p, w["w_ple_gate"], w["w_ple"], w["ln3_g"], w["ln3_b"], tgt,
                                             name="head")

    g = {}
    g["ln3_g"], g["ln3_b"] = dg3, db3
    grad("w_ple_gate", _mm(x2b, dgpre, "tn", tm=512, tn=1024, out_dtype=BF16, name="d_w_ple_gate"))
    grad("w_ple", _mm(p, dple, "tn", tm=256, tn=512, dest_major=True, out_dtype=BF16, name="d_w_ple"))
    dt2, dt2b, g["ln2_g"], g["ln2_b"] = host(_mm_ln_bwd, dgpre, w["w_ple_gate"], x1, ff, w["ln2_g"], dt3, ALPHA,
                                             tm=512, name="d_x2")
    grad("w_ff2", host(_mm, pre, dt2b, "tn", tm=512, tn=1024, a_fn=_relu2, out_dtype=BF16, name="d_w_ff2"))
    dpre = host(_mm, dt2b, w["w_ff2"], "nt", tm=2048, tn=512, extra=pre, out_dtype=BF16,
                epi=lambda acc, pv: acc * 2.0 * jnp.maximum(pv.astype(F32), 0.0), name="d_pre")
    grad("w_ff1", host(_mm, x1b, dpre, "tn", tm=1024, tn=512, dest_major=True, out_dtype=BF16, name="d_w_ff1"))
    dt1, dt1b, g["ln1_g"], g["ln1_b"] = host(_mm_ln_bwd, dpre, w["w_ff1"], x, mix, w["ln1_g"], dt2, ALPHA,
                                             tm=256, name="d_x1")
    grad("w_out", host(_mm, ycat, dt1b, "tn", tm=512, tn=1024, out_dtype=BF16, name="d_w_out"))
    dycat = host(_mm, dt1b, w["w_out"], "nt", tm=2048, tn=512, name="d_ycat")
    dxl, dgl, dcwb_l, dwa, dwx = host(_lru_bwd, proj, dycat, h_lru, cw_l, w["lru_conv_b"], wa_bd, ba, wx_bd, bx,
                                      w["lru_a_param"], name="lru_bwd")
    g["lru_gate_a_w"] = _unblockdiag(dwa)
    g["lru_gate_x_w"] = _unblockdiag(dwx)
    raw = dict(lru=dcwb_l, gate_a=g["lru_gate_a_w"].reshape(N_HEAD * HEAD_P, HEAD_P).astype(BF16),
               gate_x=g["lru_gate_x_w"].reshape(N_HEAD * HEAD_P, HEAD_P).astype(BF16))
    hooks.small(raw)
    dxact, ddt, dz, g["ssd_norm_w"], small = host(_ssd_bwd, xact, proj, dycat, hprev, bias_pad, alog_pad, dxp,
                                                   w["ssd_norm_w"], name="ssd_bwd")
    dxbc, dcwb_s = host(_conv_silu_bwd, proj, dxact, cw_s, w["ssd_conv_b"], col0=COL_XBC, width=XBC, ct=256,
                        name="ssd_conv_bwd")
    pieces, offsets = [dxl, dgl, dz, dxbc, ddt], [0, COL_G, COL_Z, COL_XBC, COL_DT]

    g["lru_conv_w"] = dcwb_l[0:4]
    g["lru_conv_b"] = dcwb_l[4:5]
    g["lru_gate_a_b"] = dcwb_l[5:6]
    g["lru_gate_x_b"] = dcwb_l[6:7]
    g["lru_a_param"] = dcwb_l[7:8]
    g["ssd_conv_w"] = dcwb_s[0:4]
    g["ssd_conv_b"] = dcwb_s[4:5]
    g["ssd_dt_bias"] = small[0:1, :N_HEAD]
    g["ssd_a_log"] = small[1:2, :N_HEAD]
    g["ssd_d"] = small[2:3, :N_HEAD]
    rows = jnp.concatenate([g[n] for n in ("ssd_norm_w", "ln1_g", "ln1_b", "ln2_g", "ln2_b", "ln3_g", "ln3_b")]
                           + [jnp.broadcast_to(loss[:, 0:1], (1, D_MODEL))], axis=0)
    late = dict(ssd=dcwb_s, heads=small, rows=rows)
    hooks.small(late)
    raw.update(late)
    dwt = None
    for q, (pc, off) in enumerate(zip(pieces, offsets)):
        dwt = host(_mm, pc, xb, "tn", tm=512, tn=1024, out_dtype=BF16, into=(dwt, off, D_IN),
                   name="d_w_in_%d" % q)
    grad("w_in", dwt)
    hooks.pairs_now()
    grad_x = host(_mm_pieces, pieces, offsets, w["w_in_t"], tm=256, extra=dt1, epi=lambda acc, e: acc + ALPHA * e,
                  name="d_x")
    return loss[0, 0], grad_x, g, raw


ANY_SPEC = pl.BlockSpec(memory_space=pl.ANY)


def _mesh_pos():
    return lax.axis_index("x"), lax.axis_index("y"), lax.axis_index("c")


def _remote(src, dst, send, recv, k, to):
    return pltpu.make_async_remote_copy(src_ref=src, dst_ref=dst, send_sem=send.at[k], recv_sem=recv.at[k],
                                        device_id=to, device_id_type=MESH_T)


class _Job:
    N_SEM = 9

    def __init__(self, kind, inp):
        self.kind, self.inp = kind, inp
        shape = {"gather": (N_DEV,) + inp.shape, "relay": (N_DEV,) + inp.shape, "pair": (4,) + inp.shape[1:],
                 "chip": inp.shape}[kind]
        self.out = jax.ShapeDtypeStruct(shape, inp.dtype)
        self.top = (inp.shape[0] // 2) // 16 * 16

    def _relay_copies(self, inp, out, send, recv):
        x, y, c = _mesh_pos()
        sib, xn, yn, dg = (x, y, 1 - c), (1 - x, y, c), (x, 1 - y, c), (1 - x, 1 - y, c)
        blk = lambda p, cc=None: out.at[4 * p[0] + 2 * p[1] + (p[2] if cc is None else cc)]
        top = lambda r: r.at[pl.ds(0, self.top)]
        bot = lambda r: r.at[pl.ds(self.top, self.inp.shape[0] - self.top)]
        mine = blk((x, y, c))
        plan = [
            (inp, mine, sib, blk(sib)),
            (inp, mine, xn, blk(xn)),
            (inp, mine, yn, blk(yn)),
            (top(blk(xn)), top(blk(xn)), yn, top(blk(dg))),
            (bot(blk(yn)), bot(blk(yn)), xn, bot(blk(dg))),
            (blk(xn), blk(xn), sib, blk(xn, 1 - c)),
            (blk(yn), blk(yn), sib, blk(yn, 1 - c)),
            (top(blk(dg)), top(blk(dg)), sib, top(blk(dg, 1 - c))),
            (bot(blk(dg)), bot(blk(dg)), sib, bot(blk(dg, 1 - c))),
        ]
        me = (x, y, c)
        return [(_remote(s, d, send, recv, k, to), _remote(s, land, send, recv, k, me))
                for k, (s, d, to, land) in enumerate(plan)]

    def _places(self):
        x, y, c = _mesh_pos()
        return (x, y, c), (x, y, 1 - c), [(1 - x, y), (x, 1 - y), (1 - x, 1 - y)]

    def start(self, inp, out, send, recv, loc):
        me, sibling, chips = self._places()
        x, y, c = me
        if self.kind == "relay":
            pltpu.make_async_copy(inp, out.at[4 * x + 2 * y + c], loc.at[0]).start()
            cps = self._relay_copies(inp, out, send, recv)
            for k in (0, 1, 2):
                cps[k][0].start()
        elif self.kind == "gather":
            mine = out.at[4 * x + 2 * y + c]
            pltpu.make_async_copy(inp, mine, loc.at[0]).start()
            _remote(inp, mine, send, recv, 0, sibling).start()
            for j, chip in enumerate(chips):
                _remote(inp, mine, send, recv, 1 + j, (*chip, c)).start()
        elif self.kind == "pair":
            for k in range(4):
                _remote(inp.at[2 * k + (1 - c)], out.at[k], send, recv, k, sibling).start()
        else:
            kme = 2 * x + y
            pltpu.make_async_copy(inp.at[kme], out.at[kme], loc.at[0]).start()
            for j, (tx, ty) in enumerate(chips):
                _remote(inp.at[2 * tx + ty], out.at[kme], send, recv, j, (tx, ty, c)).start()

    def mid(self, inp, out, send, recv, loc):
        if self.kind == "relay":
            cps = self._relay_copies(inp, out, send, recv)
            for k, onward in ((1, (3, 5)), (2, (4, 6))):
                cps[k][1].wait_recv()
                for q in onward:
                    cps[q][0].start()
            return
        if self.kind != "gather":
            return
        me, sibling, chips = self._places()
        c = me[2]
        for j, chip in enumerate(chips):
            landed = out.at[4 * chip[0] + 2 * chip[1] + c]
            _remote(landed, landed, send, recv, 1 + j, me).wait_recv()
            _remote(landed, landed, send, recv, 4 + j, sibling).start()

    def finish(self, inp, out, send, recv, loc):
        me, sibling, chips = self._places()
        x, y, c = me
        if self.kind == "relay":
            cps = self._relay_copies(inp, out, send, recv)
            for k, onward in ((3, 7), (4, 8)):
                cps[k][1].wait_recv()
                cps[onward][0].start()
            for k in (0, 5, 6, 7, 8):
                cps[k][1].wait_recv()
            for k in range(9):
                cps[k][0].wait_send()
            pltpu.make_async_copy(inp, out.at[4 * x + 2 * y + c], loc.at[0]).wait()
        elif self.kind == "gather":
            blk = lambda px, py, pc: out.at[4 * px + 2 * py + pc]
            mine = blk(*me)
            _remote(inp, blk(*sibling), send, recv, 0, me).wait_recv()
            for j, chip in enumerate(chips):
                _remote(inp, blk(*chip, 1 - c), send, recv, 4 + j, me).wait_recv()
            for k in range(7):
                _remote(inp, mine, send, recv, k, sibling).wait_send()
            pltpu.make_async_copy(inp, mine, loc.at[0]).wait()
        elif self.kind == "pair":
            for k in range(4):
                _remote(inp.at[2 * k + (1 - c)], out.at[k], send, recv, k, sibling).wait()
        else:
            kme = 2 * x + y
            for j, (tx, ty) in enumerate(chips):
                _remote(inp.at[kme], out.at[2 * tx + ty], send, recv, j, (tx, ty, c)).wait_recv()
            for j, (tx, ty) in enumerate(chips):
                _remote(inp.at[2 * tx + ty], out.at[kme], send, recv, j, (tx, ty, c)).wait_send()
            pltpu.make_async_copy(inp.at[kme], out.at[kme], loc.at[0]).wait()


def _job_scratch(jobs):
    sem = pltpu.SemaphoreType.DMA
    return [s for _ in jobs for s in (sem((_Job.N_SEM,)), sem((_Job.N_SEM,)), sem((1,)))]


def _run_jobs(jobs, method, jins, jouts, jsems, only=None):
    for q, job in enumerate(jobs):
        if only is None or only[q]:
            getattr(job, method)(jins[q], jouts[q], *jsems[3 * q:3 * q + 3])


def _exchange(jobs, *, name):
    n = len(jobs)

    def body(*refs):
        jins, jouts, jsems = refs[:n], refs[n:2 * n], refs[2 * n:]
        _run_jobs(jobs, "start", jins, jouts, jsems)
        _run_jobs(jobs, "mid", jins, jouts, jsems)
        _run_jobs(jobs, "finish", jins, jouts, jsems)

    return _pcall(body, in_specs=[ANY_SPEC] * n, out_specs=[ANY_SPEC] * n, out_shape=[j.out for j in jobs],
                  scratch_shapes=_job_scratch(jobs), name=name)(*[j.inp for j in jobs])


def _hosted(body, jobs, *, grid, in_specs, out_specs, out_shape, args, name, scratch_shapes=(), aliases=None):
    in_specs, out_specs, out_shape = list(in_specs), list(out_specs), list(out_shape)
    scratch_shapes = list(scratch_shapes)
    n_in, n_out, n_scr, nj = len(in_specs), len(out_specs), len(scratch_shapes), len(jobs)
    sem = ("arbitrary",) * len(grid)
    kw = dict(input_output_aliases=aliases) if aliases else {}
    if not jobs:
        res = _pcall(body, grid=grid, in_specs=in_specs, out_specs=out_specs, out_shape=out_shape,
                     scratch_shapes=scratch_shapes, name=name, compiler_params=_cparams(sem), **kw)(*args)
        return list(res), []

    def full(*refs):
        ins, jins = refs[:n_in], refs[n_in:n_in + nj]
        o0 = n_in + nj
        outs, jouts = refs[o0:o0 + n_out], refs[o0 + n_out:o0 + n_out + nj]
        s0 = o0 + n_out + nj
        scr, jsems = refs[s0:s0 + n_scr], refs[s0 + n_scr:]
        step = pl.program_id(0)
        for ax in range(1, len(grid)):
            step = step * grid[ax] + pl.program_id(ax)
        total = math.prod(grid)
        early = [job.kind == "relay" for job in jobs]
        mid_step = (3 * total) // 5
        split = any(early) and 0 < mid_step < total - 1

        @pl.when(step == 0)
        def _():
            _run_jobs(jobs, "start", jins, jouts, jsems)

        if split:
            @pl.when(step == mid_step)
            def _():
                _run_jobs(jobs, "mid", jins, jouts, jsems, only=early)

        body(*ins, *outs, *scr)

        @pl.when(step == total - 1)
        def _():
            _run_jobs(jobs, "mid", jins, jouts, jsems, only=[not e for e in early] if split else None)
            _run_jobs(jobs, "finish", jins, jouts, jsems)

    res = _pcall(full, grid=grid, in_specs=in_specs + [ANY_SPEC] * nj, out_specs=out_specs + [ANY_SPEC] * nj,
                 out_shape=out_shape + [j.out for j in jobs], scratch_shapes=scratch_shapes + _job_scratch(jobs),
                 name=name, compiler_params=_cparams(sem), **kw)(*args, *[j.inp for j in jobs])
    return list(res[:n_out]), list(res[n_out:])


def _pair_add(g8, r4, cidx, *, name):
    _, r, c = g8.shape
    tr = ROW_TILE if r % ROW_TILE == 0 else r

    def body(c_ref, g_ref, r_ref, o_ref):
        o_ref[...] = (g_ref[...].astype(F32) + r_ref[...].astype(F32)).astype(BF16)

    return _pcall(
        body,
        grid_spec=pltpu.PrefetchScalarGridSpec(
            num_scalar_prefetch=1, grid=(4, r // tr),
            in_specs=[pl.BlockSpec((None, tr, c), lambda k, i, cr: (2 * k + cr[0], i, 0)),
                      pl.BlockSpec((None, tr, c), lambda k, i, cr: (k, i, 0))],
            out_specs=pl.BlockSpec((None, tr, c), lambda k, i, cr: (k, i, 0))),
        out_shape=jax.ShapeDtypeStruct((4, r, c), BF16), name=name,
        compiler_params=_cparams(("parallel", "parallel")))(cidx, g8, r4)


def _adam_update(g, w_ref, m_ref, v_ref, g_ref, d_ref, mo_ref, vo_ref):
    c1 = 1.0 - ADAM_B1 ** ADAM_STEP
    c2 = 1.0 - ADAM_B2 ** ADAM_STEP
    m2 = ADAM_B1 * m_ref[...] + (1.0 - ADAM_B1) * g
    v2 = ADAM_B2 * v_ref[...] + (1.0 - ADAM_B2) * (g * g)
    g_ref[...] = g
    mo_ref[...] = m2
    vo_ref[...] = v2
    d_ref[...] = -ADAM_LR * ((m2 / c1) / (jnp.sqrt(v2 / c2) + ADAM_EPS) + ADAM_WD * w_ref[...])


def _adamw_rows(srcs, items, own_cols, me1, loss_row, *, name):
    ns, ni, no = len(srcs), len(items), len(own_cols)
    full = lambda a: pl.BlockSpec(a.shape, lambda i, me: (0,) * a.ndim)
    in_specs = [full(a) for a in srcs]
    args = list(srcs)
    for (si, _r0, w, _m, _v) in own_cols:
        a = srcs[si]
        in_specs.append(pl.BlockSpec((N_DEV, a.shape[1], w.shape[1]), lambda i, me: (0, 0, me[0])))
        args.append(a)
    out_specs, out_shape = [], []
    for (_si, _r0, w, m, v) in list(items) + list(own_cols):
        in_specs += [full(w)] * 3
        args += [w, m, v]
        out_specs += [full(w)] * 4
        out_shape += [jax.ShapeDtypeStruct(w.shape, F32)] * 4
    out_specs.append(pl.BlockSpec((1, LANE), lambda i, me: (0, 0)))
    out_shape.append(jax.ShapeDtypeStruct((1, LANE), F32))

    def body(me_ref, *refs):
        src_refs, own_refs = refs[:ns], refs[ns:ns + no]
        wmv = refs[ns + no:ns + no + 3 * (ni + no)]
        outs = refs[ns + no + 3 * (ni + no):]
        lsrc, lrow = src_refs[loss_row[0]], loss_row[1]
        total = lsrc[0, lrow:lrow + 1, 0:LANE]
        for d in range(1, N_DEV):
            total = total + lsrc[d, lrow:lrow + 1, 0:LANE]
        outs[-1][...] = total
        for q, (si, r0, w, _m, _v) in enumerate(list(items) + list(own_cols)):
            nr, cw = w.shape
            gref = src_refs[si] if q < ni else own_refs[q - ni]
            g = gref[0, r0:r0 + nr, 0:cw]
            for d in range(1, N_DEV):
                g = g + gref[d, r0:r0 + nr, 0:cw]
            _adam_update(g, *wmv[3 * q:3 * q + 3], *outs[4 * q:4 * q + 4])

    res = _pcall(
        body,
        grid_spec=pltpu.PrefetchScalarGridSpec(num_scalar_prefetch=1, grid=(1,), in_specs=in_specs, out_specs=out_specs),
        out_shape=out_shape, name=name, compiler_params=_cparams(("arbitrary",)))(me1, *args)
    return [tuple(res[4 * q:4 * q + 4]) for q in range(ni + no)], res[-1]


def _adamw(gsrc, w, m, v, *, name):
    k, r, c = gsrc.shape
    tr = ROW_TILE if r % ROW_TILE == 0 else r

    def body(gs_ref, w_ref, m_ref, v_ref, g_ref, d_ref, mo_ref, vo_ref):
        g = gs_ref[0].astype(F32)
        for q in range(1, k):
            g = g + gs_ref[q].astype(F32)
        _adam_update(g, w_ref, m_ref, v_ref, g_ref, d_ref, mo_ref, vo_ref)

    tc = c
    if tr == r and r > ROW_TILE and c % 256 == 0:
        tc = 256
    blk = pl.BlockSpec((tr, tc), lambda i, j: (i, j))
    sd = jax.ShapeDtypeStruct((r, c), F32)
    return _pcall(body, grid=(r // tr, c // tc),
                  in_specs=[pl.BlockSpec((k, tr, tc), lambda i, j: (0, i, j)), blk, blk, blk],
                  out_specs=(blk, blk, blk, blk), out_shape=(sd, sd, sd, sd), name=name,
                  compiler_params=_cparams(("parallel", "parallel")))(gsrc, w, m, v)


WEIGHTS = ['w_in', 'lru_conv_w', 'lru_conv_b', 'lru_gate_a_w', 'lru_gate_a_b', 'lru_gate_x_w', 'lru_gate_x_b',
           'lru_a_param', 'ssd_conv_w', 'ssd_conv_b', 'ssd_dt_bias', 'ssd_a_log', 'ssd_d', 'ssd_norm_w', 'w_out',
           'ln1_g', 'ln1_b', 'w_ff1', 'w_ff2', 'ln2_g', 'ln2_b', 'w_ple_gate', 'w_ple', 'ln3_g', 'ln3_b']
BIG = ['w_in', 'w_out', 'w_ff1', 'w_ff2', 'w_ple_gate', 'w_ple']
COL_SHARDED = ('w_ff1', 'w_ple')
CONV = ['lru_conv_w', 'ssd_conv_w']
REPL = [n for n in WEIGHTS if n not in BIG and n not in CONV]
CONV_CH = {'lru_conv_w': LRU_W, 'ssd_conv_w': XBC}


def _to_dest_major(name, gfull):
    if name in COL_SHARDED:
        r, cfull = gfull.shape
        return gfull.reshape(r, N_DEV, cfull // N_DEV).transpose(1, 0, 2)
    rfull, cdim = gfull.shape
    return gfull.reshape(N_DEV, rfull // N_DEV, cdim)


def _full_weight(name, gathered):
    if name in COL_SHARDED:
        _, r, cs = gathered.shape
        full = gathered.transpose(1, 0, 2).reshape(r, N_DEV * cs)
    else:
        _, rs, cdim = gathered.shape
        full = gathered.reshape(N_DEV * rs, cdim)
    if name == 'w_in':
        full = lax.dynamic_update_slice(jnp.zeros((D_IN_PAD, D_MODEL), full.dtype), full, (0, 0))
    return full


SMALL_SRC = ("lru", "ssd", "heads", "rows", "gate_a", "gate_x")
AG_HOSTS = {"in_proj": ("w_ff1",), "lru_fwd": ("w_ff2",), "ssd_fwd": ("w_out",), "ff1": ("w_ple_gate", "w_ple")}
PAIR_HOSTS = ("d_x2", "d_pre", "d_x1", "d_ycat")
CHIP_HOSTS = {"lru_bwd": ("w_ff1",), "ssd_bwd": ("w_ple_gate", "w_ple", "w_ff2"), "ssd_conv_bwd": ("w_out",),
              "d_x": ("w_in",)}
SMALL_HOSTS = {"ssd_bwd": ("lru", "gate_a", "gate_x"), "d_w_in_3": ("ssd", "heads", "rows")}


class _Schedule:
    def __init__(self, shards, cidx):
        self.shards, self.cidx = shards, cidx
        self.pair, self.chip, self.small_jobs = [], [], []
        self.dest, self.summed, self.gathered_small = {}, {}, {}
        self.tags = []

    def ride(self, host):
        tags = []
        if host in AG_HOSTS:
            tags = [("weight", n, self.shards[n]) for n in AG_HOSTS[host]]
        elif host in PAIR_HOSTS or host in CHIP_HOSTS or host == "flush":
            tags = [("pair", n, a) for n, a in self.pair]
            self.pair = []
            if host not in PAIR_HOSTS:
                take = [t for t in self.chip if host == "flush" or t[0] in CHIP_HOSTS[host]]
                tags += [("chip", n, a) for n, a in take]
                self.chip = [t for t in self.chip if not any(t is u for u in take)]
        if host in SMALL_HOSTS:
            tags += [("small", n, a) for n, a in self.small_jobs if n in SMALL_HOSTS[host]]
            self.small_jobs = [t for t in self.small_jobs if t[0] not in SMALL_HOSTS[host]]
        self.tags = tags
        return [_Job({"weight": "relay", "small": "gather"}.get(kind, kind), a) for kind, _n, a in tags]

    def done(self, jobs, outs, w):
        for (kind, n, _a), o in zip(self.tags, outs):
            if kind == "weight":
                w[n] = _full_weight(n, o)
            elif kind == "small":
                self.gathered_small[n] = o
            elif kind == "pair":
                self.chip.append((n, _pair_add(self.dest[n], o, self.cidx, name="rs_pair_add_" + n)))
            else:
                self.summed[n] = o

    def grad(self, name, val):
        self.dest[name] = val if val.ndim == 3 else _to_dest_major(name, val)
        self.pair.append((name, self.dest[name]))

    def small(self, raw):
        self.small_jobs += list(raw.items())

    def pairs_now(self):
        tags = [("pair", n, a) for n, a in self.pair]
        self.pair, self.tags = [], tags
        jobs = [_Job("pair", a) for _k, _n, a in tags]
        self.done(jobs, _exchange(jobs, name="rs_pairs_now"), None)

    def flush(self):
        step = 0
        while self.pair or self.chip:
            jobs = self.ride("flush")
            self.done(jobs, _exchange(jobs, name="rs_flush_%d" % step), None)
            step += 1


def kernel(x, p, w_in, lru_conv_w, lru_conv_b, lru_gate_a_w, lru_gate_a_b, lru_gate_x_w, lru_gate_x_b, lru_a_param, ssd_conv_w, ssd_conv_b, ssd_dt_bias, ssd_a_log, ssd_d, ssd_norm_w, w_out, ln1_g, ln1_b, w_ff1, w_ff2, ln2_g, ln2_b, w_ple_gate, w_ple, ln3_g, ln3_b, loss_target, m_w_in, m_lru_conv_w, m_lru_conv_b, m_lru_gate_a_w, m_lru_gate_a_b, m_lru_gate_x_w, m_lru_gate_x_b, m_lru_a_param, m_ssd_conv_w, m_ssd_conv_b, m_ssd_dt_bias, m_ssd_a_log, m_ssd_d, m_ssd_norm_w, m_w_out, m_ln1_g, m_ln1_b, m_w_ff1, m_w_ff2, m_ln2_g, m_ln2_b, m_w_ple_gate, m_w_ple, m_ln3_g, m_ln3_b, v_w_in, v_lru_conv_w, v_lru_conv_b, v_lru_gate_a_w, v_lru_gate_a_b, v_lru_gate_x_w, v_lru_gate_x_b, v_lru_a_param, v_ssd_conv_w, v_ssd_conv_b, v_ssd_dt_bias, v_ssd_a_log, v_ssd_d, v_ssd_norm_w, v_w_out, v_ln1_g, v_ln1_b, v_w_ff1, v_w_ff2, v_ln2_g, v_ln2_b, v_w_ple_gate, v_w_ple, v_ln3_g, v_ln3_b):
    given = dict(locals())
    def local(a, n):
        return jnp.swapaxes(a[0], 0, 1) if n == 'w_in' else a[0]

    wsh = {n: local(given[n], n) for n in WEIGHTS}
    msh = {n: local(given["m_" + n], n) for n in WEIGHTS}
    vsh = {n: local(given["v_" + n], n) for n in WEIGHTS}
    xi, yi, ci = _mesh_pos()
    me = 4 * xi + 2 * yi + ci

    shards = {n: wsh[n].astype(BF16) for n in BIG}
    conv_pack = jnp.concatenate([_pad_rows8(wsh[n]) for n in CONV], axis=1)
    g_in, gconv = _exchange([_Job("relay", shards['w_in']), _Job("gather", conv_pack)], name="ag_first")
    full = {'w_in_t': _full_weight('w_in', g_in)}
    c0 = 0
    for n in CONV:
        cw = CONV_CH[n] // N_DEV
        full[n] = gconv[:, :4, c0:c0 + cw].transpose(1, 0, 2).reshape(4, CONV_CH[n])
        c0 += cw
    for n in REPL:
        full[n] = given[n] if given[n].ndim == 2 else wsh[n]

    sched = _Schedule(shards, jnp.reshape(ci, (1,)).astype(jnp.int32))
    loss_local, grad_x, g, raw = _local_step(x[0], p[0, 0], loss_target[0], full, sched)
    sched.flush()
    summed, gat = sched.summed, sched.gathered_small

    outs = {}
    for n in BIG:
        outs[n] = _adamw(summed[n], wsh[n], msh[n], vsh[n], name="adamw_" + n)
    for n, k in (("lru_gate_a_w", "gate_a"), ("lru_gate_x_w", "gate_x")):
        flat = lambda a: a.reshape(N_HEAD * HEAD_P, HEAD_P)
        res = _adamw(gat[k], flat(wsh[n]), flat(msh[n]), flat(vsh[n]), name="adamw_" + n)
        outs[n] = tuple(r.reshape(N_HEAD, HEAD_P, HEAD_P) for r in res)
    for n, row in (("lru_gate_a_b", 5), ("lru_gate_x_b", 6)):
        outs[n] = _adamw(gat["lru"][:, row].reshape(N_DEV, N_HEAD, HEAD_P), wsh[n], msh[n], vsh[n], name="adamw_" + n)
    row_items = [("lru_conv_b", 0, 4), ("lru_a_param", 0, 7),
                 ("ssd_conv_b", 1, 4), ("ssd_dt_bias", 2, 0), ("ssd_a_log", 2, 1), ("ssd_d", 2, 2),
                 ("ssd_norm_w", 3, 0), ("ln1_g", 3, 1), ("ln1_b", 3, 2), ("ln2_g", 3, 3), ("ln2_b", 3, 4),
                 ("ln3_g", 3, 5), ("ln3_b", 3, 6)]
    vec = lambda a: a.reshape(1, -1)
    items = [(si, r0, vec(given[n]), vec(given["m_" + n]), vec(given["v_" + n])) for n, si, r0 in row_items]
    own = [(si, 0, wsh[n], msh[n], vsh[n]) for n, si in (("lru_conv_w", 0), ("ssd_conv_w", 1))]
    me1 = jnp.reshape(me, (1,)).astype(jnp.int32)
    res, loss_row = _adamw_rows([gat[k] for k in SMALL_SRC[:4]], items, own, me1, (3, 7), name="adamw_small")
    loss = loss_row[0, 0]
    for (n, _si, _r0), r4 in zip(row_items, res[:len(row_items)]):
        outs[n] = r4
    for n, r4 in zip(CONV, res[len(row_items):]):
        outs[n] = r4

    def fin(n, k):
        a = jnp.swapaxes(outs[n][k], 0, 1) if n == 'w_in' else outs[n][k]
        return a.reshape(given[n].shape)

    return (loss, grad_x[None],
            *[fin(n, 0) for n in WEIGHTS], *[fin(n, 1) for n in WEIGHTS],
            *[fin(n, 2) for n in WEIGHTS], *[fin(n, 3) for n in WEIGHTS])
```

```python
import math

import jax
import jax.numpy as jnp
from jax import lax
from jax.experimental import pallas as pl
from jax.experimental.pallas import tpu as pltpu

F32 = jnp.float32
BF16 = jnp.bfloat16

N_DEV = 8
D_MODEL = 1024
LRU_W = 1024
SSD_W = 1024
XBC = 2048
N_HEAD = 16
HEAD_P = 64
N_GROUP = 4
GROUP_W = 256
N_STATE = 128
CHUNK = 128
D_IN = 5136
D_IN_PAD = 5632
COL_G = 1024
COL_Z = 2048
COL_XBC = 3072
COL_DT = 5120
LRU_C = 8.0
ALPHA = 2.0 ** 0.25
LN_EPS = 1e-5
RMS_EPS = 1e-5
ADAM_LR = 0.001
ADAM_B1 = 0.9
ADAM_B2 = 0.999
ADAM_EPS = 1e-08
ADAM_WD = 0.01
ADAM_STEP = 10
GELU_C = math.sqrt(2.0 / math.pi)
LANE = 128
SUBLANE = 8
VMEM_LIMIT = 48 * 1024 * 1024
MESH_T = pl.DeviceIdType.MESH
NEG_BIG = -1e30


def _pcall(body, **kw):
    return pl.pallas_call(body, **kw)


def _cparams(sem):
    return pltpu.CompilerParams(dimension_semantics=sem, vmem_limit_bytes=VMEM_LIMIT)


def _dot(a, b):
    return jnp.dot(a.astype(BF16), b.astype(BF16), preferred_element_type=F32)


def _dot_nt(a, b):
    return lax.dot_general(a.astype(BF16), b.astype(BF16), (((1,), (1,)), ((), ())), preferred_element_type=F32)


def _dot_tn(a, b):
    return lax.dot_general(a.astype(BF16), b.astype(BF16), (((0,), (0,)), ((), ())), preferred_element_type=F32)


def _sigmoid(x):
    return jax.nn.sigmoid(x)


def _softplus(v):
    return jnp.maximum(v, 0.0) + jnp.log1p(jnp.exp(-jnp.abs(v)))


def _gelu(x):
    th = jnp.tanh(GELU_C * (x + 0.044715 * x * x * x))
    return 0.5 * x * (1.0 + th), th


def _gelu_grad(x, th):
    return 0.5 * (1.0 + th) + 0.5 * x * (1.0 - th * th) * GELU_C * (1.0 + 3.0 * 0.044715 * x * x)


def _iota(shape, dim):
    return lax.broadcasted_iota(jnp.int32, shape, dim)


def _mm(a, b, mode, *, tm, tn, name, a_fn=None, extra=None, epi=None, out_dtype=F32, dest_major=False, into=None,
        jobs=()):
    m = a.shape[1] if mode == "tn" else a.shape[0]
    n = b.shape[0] if mode == "nt" else b.shape[1]
    tm, tn = min(tm, m), min(tn, n)
    if dest_major:
        tn = n // N_DEV
    if mode == "nn":
        m, k = a.shape
        _, n = b.shape
        a_spec = pl.BlockSpec((tm, k), lambda i, j: (i, 0))
        b_spec = pl.BlockSpec((k, tn), lambda i, j: (0, j))
        dims = ((1,), (0,))
    elif mode == "nt":
        m, k = a.shape
        n, _ = b.shape
        a_spec = pl.BlockSpec((tm, k), lambda i, j: (i, 0))
        b_spec = pl.BlockSpec((tn, k), lambda i, j: (j, 0))
        dims = ((1,), (1,))
    else:
        k, m = a.shape
        _, n = b.shape
        a_spec = pl.BlockSpec((k, tm), lambda i, j: (0, i))
        b_spec = pl.BlockSpec((k, tn), lambda i, j: (0, j))
        dims = ((0,), (0,))
    assert m % tm == 0 and n % tn == 0, (name, m, n, tm, tn)
    o_spec = pl.BlockSpec((tm, tn), lambda i, j: (i, j))
    in_specs = [a_spec, b_spec]
    args = [a, b]
    if extra is not None:
        in_specs.append(o_spec)
        args.append(extra)

    def body(*refs):
        a_ref, b_ref, o_ref = refs[0], refs[1], refs[-1]
        av = a_ref[...]
        if a_fn is not None:
            av = a_fn(av)
        acc = lax.dot_general(av.astype(BF16), b_ref[...].astype(BF16), (dims, ((), ())), preferred_element_type=F32)
        if epi is not None:
            acc = epi(acc, refs[2][...])
        o_ref[...] = acc.astype(out_dtype)

    out_shape = jax.ShapeDtypeStruct((m, n), out_dtype)
    aliases = None
    if dest_major:
        assert extra is None
        o_spec = pl.BlockSpec((None, tm, tn), lambda i, j: (j, i, 0))
        out_shape = jax.ShapeDtypeStruct((N_DEV, m, tn), out_dtype)
    if into is not None:
        buf, row0, total = into
        assert extra is None and row0 % tm == 0
        o_spec = pl.BlockSpec((tm, tn), lambda i, j: (row0 // tm + i, j))
        out_shape = jax.ShapeDtypeStruct((total, n), out_dtype)
        if buf is not None:
            in_specs.append(ANY_SPEC)
            args.append(buf)
            aliases = {len(args) - 1: 0}
    (out,), jouts = _hosted(body, jobs, grid=(m // tm, n // tn), in_specs=in_specs, out_specs=[o_spec],
                            out_shape=[out_shape], args=args, name=name, aliases=aliases)
    return (out, jouts) if jobs else out


def _mm_pieces(pieces, offsets, b, *, tm, name, extra, epi, jobs=()):
    m = pieces[0].shape[0]
    kb, n = b.shape
    tm = min(tm, m)
    row = lambda wdt: pl.BlockSpec((tm, wdt), lambda i: (i, 0))
    in_specs = [row(pc.shape[1]) for pc in pieces] + [pl.BlockSpec((kb, n), lambda i: (0, 0)), row(n)]
    np_ = len(pieces)

    def body(*refs):
        b_ref, e_ref, o_ref = refs[np_], refs[np_ + 1], refs[np_ + 2]
        acc = jnp.zeros((tm, n), F32)
        for q in range(np_):
            kq = pieces[q].shape[1]
            acc = acc + jnp.dot(refs[q][...].astype(BF16), b_ref[offsets[q]:offsets[q] + kq, :].astype(BF16),
                                preferred_element_type=F32)
        o_ref[...] = epi(acc, e_ref[...])

    (out,), jouts = _hosted(body, jobs, grid=(m // tm,), in_specs=in_specs, out_specs=[row(n)],
                            out_shape=[jax.ShapeDtypeStruct((m, n), F32)], args=list(pieces) + [b, extra], name=name)
    return (out, jouts) if jobs else out


def _relu2(v):
    r = jnp.maximum(v, 0.0)
    return r * r


ROW_TILE = 256


def _ln_stats(t):
    mu = jnp.mean(t, axis=-1, keepdims=True)
    xc = t - mu
    var = jnp.mean(xc * xc, axis=-1, keepdims=True)
    rstd = lax.rsqrt(var + LN_EPS)
    return xc * rstd, rstd


def _ln_bwd_rows(dy, xhat, rstd, g):
    dxh = dy * g
    m1 = jnp.mean(dxh, axis=-1, keepdims=True)
    m2 = jnp.mean(dxh * xhat, axis=-1, keepdims=True)
    return rstd * (dxh - m1 - xhat * m2)


def _mm_ln(a, b, res, g, beta, *, tm, name, a_fn=None):
    m, k = a.shape
    d = b.shape[1]
    tm = min(tm, m)
    row = pl.BlockSpec((tm, d), lambda i: (i, 0))
    par = pl.BlockSpec((1, d), lambda i: (0, 0))

    def body(a_ref, b_ref, r_ref, g_ref, be_ref, br_ref, y_ref, yb_ref):
        av = a_ref[...]
        if a_fn is not None:
            av = a_fn(av)
        acc = jnp.dot(av.astype(BF16), b_ref[...].astype(BF16), preferred_element_type=F32)
        br_ref[...] = acc
        xhat, _ = _ln_stats(ALPHA * r_ref[...] + acc)
        y = xhat * g_ref[...] + be_ref[...]
        y_ref[...] = y
        yb_ref[...] = y.astype(BF16)

    sd = jax.ShapeDtypeStruct((m, d), F32)
    return _pcall(body, grid=(m // tm,),
                  in_specs=[pl.BlockSpec((tm, k), lambda i: (i, 0)), pl.BlockSpec((k, d), lambda i: (0, 0)), row, par, par],
                  out_specs=(row, row, row), out_shape=(sd, sd, jax.ShapeDtypeStruct((m, d), BF16)), name=name,
                  compiler_params=_cparams(("parallel",)))(a, b, res, g, beta)


def _mm_ln_bwd(a, b, res, branch, g, dy0, coef0, *, tm, name, jobs=()):
    m, k = a.shape
    d = b.shape[0]
    tm = min(tm, m)
    row = pl.BlockSpec((tm, d), lambda i: (i, 0))
    par = pl.BlockSpec((1, d), lambda i: (0, 0))

    def body(a_ref, b_ref, r_ref, br_ref, g_ref, dy0_ref, dt_ref, dtb_ref, dg_ref, db_ref):
        acc = lax.dot_general(a_ref[...].astype(BF16), b_ref[...].astype(BF16), (((1,), (1,)), ((), ())),
                              preferred_element_type=F32)
        dy = coef0 * dy0_ref[...] + acc
        xhat, rstd = _ln_stats(ALPHA * r_ref[...] + br_ref[...])
        dt = _ln_bwd_rows(dy, xhat, rstd, g_ref[...])
        dt_ref[...] = dt
        dtb_ref[...] = dt.astype(BF16)

        @pl.when(pl.program_id(0) == 0)
        def _():
            dg_ref[...] = jnp.zeros_like(dg_ref)
            db_ref[...] = jnp.zeros_like(db_ref)

        dg_ref[...] += jnp.sum(dy * xhat, axis=0, keepdims=True)
        db_ref[...] += jnp.sum(dy, axis=0, keepdims=True)

    pd = jax.ShapeDtypeStruct((1, d), F32)
    outs, jouts = _hosted(
        body, jobs, grid=(m // tm,),
        in_specs=[pl.BlockSpec((tm, k), lambda i: (i, 0)), pl.BlockSpec((d, k), lambda i: (0, 0)), row, row, par, row],
        out_specs=(row, row, par, par),
        out_shape=(jax.ShapeDtypeStruct((m, d), F32), jax.ShapeDtypeStruct((m, d), BF16), pd, pd),
        args=(a, b, res, branch, g, dy0), name=name)
    return (tuple(outs), jouts) if jobs else tuple(outs)


def _head(x2, x2b, p, wg, wp, g, beta, tgt, *, name):
    s, d = x2.shape
    tile = 2 * ROW_TILE
    row = pl.BlockSpec((tile, d), lambda i: (i, 0))
    par = pl.BlockSpec((1, d), lambda i: (0, 0))
    lsp = pl.BlockSpec((1, LANE), lambda i: (0, 0))
    whole = lambda a: pl.BlockSpec(a.shape, lambda i: (0, 0))

    def body(x2_ref, x2b_ref, p_ref, wg_ref, wp_ref, g_ref, be_ref, t_ref,
             loss_ref, dgp_ref, dple_ref, dt_ref, dg_ref, db_ref):
        gate = _sigmoid(_dot(x2b_ref[...], wg_ref[...]))
        ple_v = _dot(p_ref[...], wp_ref[...])
        xhat, rstd = _ln_stats(ALPHA * x2_ref[...] + gate * ple_v)
        err = xhat * g_ref[...] + be_ref[...] - t_ref[...]
        dy = err * (1.0 / d)
        dt = _ln_bwd_rows(dy, xhat, rstd, g_ref[...])
        dt_ref[...] = dt
        dgp_ref[...] = (dt * ple_v * gate * (1.0 - gate)).astype(BF16)
        dple_ref[...] = (dt * gate).astype(BF16)

        @pl.when(pl.program_id(0) == 0)
        def _():
            loss_ref[...] = jnp.zeros_like(loss_ref)
            dg_ref[...] = jnp.zeros_like(dg_ref)
            db_ref[...] = jnp.zeros_like(db_ref)

        loss_ref[...] += 0.5 * jnp.sum(jnp.mean(err * err, axis=-1, keepdims=True))
        dg_ref[...] += jnp.sum(dy * xhat, axis=0, keepdims=True)
        db_ref[...] += jnp.sum(dy, axis=0, keepdims=True)

    sd = jax.ShapeDtypeStruct((s, d), F32)
    sb = jax.ShapeDtypeStruct((s, d), BF16)
    pd = jax.ShapeDtypeStruct((1, d), F32)
    return _pcall(body, grid=(s // tile,),
                  in_specs=[row, row, pl.BlockSpec((tile, p.shape[1]), lambda i: (i, 0)), whole(wg), whole(wp), par, par,
                            row],
                  out_specs=(lsp, row, row, row, par, par),
                  out_shape=(jax.ShapeDtypeStruct((1, LANE), F32), sb, sb, sd, pd, pd),
                  name=name, compiler_params=_cparams(("arbitrary",)))(x2, x2b, p, wg, wp, g, beta, tgt)


CONV_R = 256
PAD = SUBLANE


def _shift_down(ext, s):
    if s == 0:
        return ext[PAD:, :]
    return pltpu.roll(ext, s, 0)[PAD:, :]


def _shift_up(ext, s):
    r = ext.shape[0] - PAD
    if s == 0:
        return ext[:r, :]
    return pltpu.roll(ext, r + PAD - s, 0)[:r, :]


def _conv_rows(xpad_ref, r0, w_ref):
    ext = xpad_ref[pl.ds(r0, CONV_R + PAD), :]
    acc = _shift_down(ext, 0) * w_ref[3:4, :]
    for k in range(3):
        acc = acc + _shift_down(ext, 3 - k) * w_ref[k:k + 1, :]
    return acc, ext


def _fill_front_padded(dst_ref, src_ref, s):
    dst_ref[0:PAD, :] = jnp.zeros((PAD, dst_ref.shape[1]), F32)

    def cp(q, _):
        r0 = pl.multiple_of(q * CONV_R, CONV_R)
        dst_ref[pl.ds(pl.multiple_of(PAD + r0, PAD), CONV_R), :] = src_ref[pl.ds(r0, CONV_R), :]
        return 0

    lax.fori_loop(0, s // CONV_R, cp, 0)


def _conv_silu_fwd(proj, w8, b, *, col0, width, ct, name, jobs=()):
    s = proj.shape[0]
    nb = col0 // ct

    def body(x_ref, w_ref, b_ref, o_ref, xpad):
        _fill_front_padded(xpad, x_ref, s)

        def step(q, _):
            r0 = pl.multiple_of(q * CONV_R, CONV_R)
            acc, _e = _conv_rows(xpad, r0, w_ref)
            pre = acc + b_ref[...]
            o_ref[pl.ds(r0, CONV_R), :] = pre * _sigmoid(pre)
            return 0

        lax.fori_loop(0, s // CONV_R, step, 0)

    (out,), jouts = _hosted(
        body, jobs, grid=(width // ct,),
        in_specs=[pl.BlockSpec((s, ct), lambda j: (0, nb + j)), pl.BlockSpec((SUBLANE, ct), lambda j: (0, j)),
                  pl.BlockSpec((1, ct), lambda j: (0, j))],
        out_specs=[pl.BlockSpec((s, ct), lambda j: (0, j))],
        out_shape=[jax.ShapeDtypeStruct((s, width), F32)],
        scratch_shapes=[pltpu.VMEM((s + PAD, ct), F32)], name=name, args=(proj, w8, b))
    return (out, jouts) if jobs else out


def _conv_bwd_rows(dpad_ref, r0, w_ref):
    return _conv_bwd_ext(dpad_ref[pl.ds(r0, CONV_R + PAD), :], w_ref)


def _conv_bwd_ext(ext, w_ref):
    acc = _shift_up(ext, 0) * w_ref[3:4, :]
    for k in range(3):
        acc = acc + _shift_up(ext, 3 - k) * w_ref[k:k + 1, :]
    return acc


def _conv_silu_bwd(proj, dact, w8, b, *, col0, width, ct, name, jobs=()):
    s = proj.shape[0]
    nb = col0 // ct

    def body(x_ref, d_ref, w_ref, b_ref, dx_ref, dwb_ref, xpad, dpad):
        _fill_front_padded(xpad, x_ref, s)
        dpad[pl.ds(s, PAD), :] = jnp.zeros((PAD, ct), F32)
        dwb_ref[...] = jnp.zeros_like(dwb_ref)

        def step(q, _):
            r0 = pl.multiple_of(q * CONV_R, CONV_R)
            acc, ext = _conv_rows(xpad, r0, w_ref)
            pre = acc + b_ref[...]
            sg = _sigmoid(pre)
            dpre = d_ref[pl.ds(r0, CONV_R), :] * sg * (1.0 + pre * (1.0 - sg))
            dpad[pl.ds(r0, CONV_R), :] = dpre
            for k in range(4):
                dwb_ref[k:k + 1, :] += jnp.sum(dpre * _shift_down(ext, 3 - k), axis=0, keepdims=True)
            dwb_ref[4:5, :] += jnp.sum(dpre, axis=0, keepdims=True)
            return 0

        lax.fori_loop(0, s // CONV_R, step, 0)

        def step2(q, _):
            r0 = pl.multiple_of(q * CONV_R, CONV_R)
            dx_ref[pl.ds(r0, CONV_R), :] = _conv_bwd_rows(dpad, r0, w_ref).astype(BF16)
            return 0

        lax.fori_loop(0, s // CONV_R, step2, 0)

    colb = pl.BlockSpec((s, ct), lambda j: (0, j))
    outs, jouts = _hosted(
        body, jobs, grid=(width // ct,),
        in_specs=[pl.BlockSpec((s, ct), lambda j: (0, nb + j)), colb, pl.BlockSpec((SUBLANE, ct), lambda j: (0, j)),
                  pl.BlockSpec((1, ct), lambda j: (0, j))],
        out_specs=(colb, pl.BlockSpec((SUBLANE, ct), lambda j: (0, j))),
        out_shape=(jax.ShapeDtypeStruct((s, width), BF16), jax.ShapeDtypeStruct((SUBLANE, width), F32)),
        scratch_shapes=[pltpu.VMEM((s + PAD, ct), F32), pltpu.VMEM((s + PAD, ct), F32)], name=name,
        args=(proj, dact, w8, b))
    return (tuple(outs), jouts) if jobs else tuple(outs)


LRU_CT = 256


def _row_of(v, r):
    return jnp.sum(jnp.where(_iota((v.shape[0], 1), 0) == r, v, 0.0), axis=0, keepdims=True)


def _scan_fwd(a, u):
    r = a.shape[0]
    row = _iota((r, 1), 0)
    d = 1
    while d < r:
        valid = row >= d
        u = jnp.where(valid, a * pltpu.roll(u, d, 0) + u, u)
        a = jnp.where(valid, a * pltpu.roll(a, d, 0), a)
        d *= 2
    return a, u


def _scan_rev(b, u):
    r = b.shape[0]
    row = _iota((r, 1), 0)
    d = 1
    while d < r:
        valid = row < r - d
        u = jnp.where(valid, b * pltpu.roll(u, r - d, 0) + u, u)
        b = jnp.where(valid, b * pltpu.roll(b, r - d, 0), b)
        d *= 2
    return b, u


def _lru_chunk(xpad, r0, cw_ref, cb, wa, ba, wx, bx, sp):
    acc, ext = _conv_rows(xpad, r0, cw_ref)
    xl = acc + cb
    r = _sigmoid(_dot(xl, wa) + ba)
    i = _sigmoid(_dot(xl, wx) + bx)
    la = -LRU_C * r * sp
    a = jnp.exp(la)
    a2 = jnp.exp(2.0 * la)
    mult = jnp.sqrt(-jnp.tanh(la) * (a2 + 1.0))
    first = (r0 + _iota((CONV_R, 1), 0)) == 0
    mult = jnp.where(first, 1.0, mult)
    return ext, xl, r, i, a, a2, mult, first


def _lru_specs(s):
    ct = LRU_CT
    nb_g = COL_G // ct
    return dict(
        x=pl.BlockSpec((s, ct), lambda j: (0, j)),
        g=pl.BlockSpec((s, ct), lambda j: (0, nb_g + j)),
        col=pl.BlockSpec((s, ct), lambda j: (0, j)),
        cw=pl.BlockSpec((SUBLANE, ct), lambda j: (0, j)),
        vec=pl.BlockSpec((1, ct), lambda j: (0, j)),
        gate=pl.BlockSpec((None, ct, ct), lambda j: (j, 0, 0)),
    )


def _lru_fwd(proj, cw8, cb, wa_bd, ba, wx_bd, bx, ap, *, name, jobs=()):
    s = proj.shape[0]
    ct = LRU_CT
    sp_ = _lru_specs(s)

    def body(x_ref, g_ref, cw_ref, cb_ref, wa_ref, ba_ref, wx_ref, bx_ref, ap_ref, y_ref, h_ref, xpad):
        _fill_front_padded(xpad, x_ref, s)
        sp = _softplus(-ap_ref[...])

        def step(q, carry):
            r0 = pl.multiple_of(q * CONV_R, CONV_R)
            _e, xl, _r, i, a, _a2, mult, _f = _lru_chunk(xpad, r0, cw_ref, cb_ref[...], wa_ref[...], ba_ref[...],
                                                       wx_ref[...], bx_ref[...], sp)
            acum, ucum = _scan_fwd(a, xl * i * mult)
            h = acum * carry + ucum
            h_ref[pl.ds(r0, CONV_R), :] = h
            ge, _th = _gelu(g_ref[pl.ds(r0, CONV_R), :])
            y_ref[pl.ds(r0, CONV_R), :] = (ge * h).astype(BF16)
            return _row_of(h, CONV_R - 1)

        lax.fori_loop(0, s // CONV_R, step, jnp.zeros((1, ct), F32))

    (ymix, hs), jouts = _hosted(
        body, jobs, grid=(LRU_W // ct,),
        in_specs=[sp_["x"], sp_["g"], sp_["cw"], sp_["vec"], sp_["gate"], sp_["vec"], sp_["gate"], sp_["vec"], sp_["vec"]],
        out_specs=(sp_["col"], sp_["col"]),
        out_shape=(jax.ShapeDtypeStruct((s, LRU_W + SSD_W), BF16), jax.ShapeDtypeStruct((s, LRU_W), F32)),
        scratch_shapes=[pltpu.VMEM((s + PAD, ct), F32)],
        name=name, args=(proj, proj, cw8, cb, wa_bd, ba, wx_bd, bx, ap))
    return ((ymix, hs), jouts) if jobs else (ymix, hs)


def _lru_bwd(proj, dy, hs, cw8, cb, wa_bd, ba, wx_bd, bx, ap, *, name, jobs=()):
    s = proj.shape[0]
    ct = LRU_CT
    sp_ = _lru_specs(s)

    nq = s // CONV_R

    def body(x_ref, g_ref, dy_ref, h_ref, cw_ref, cb_ref, wa_ref, ba_ref, wx_ref, bx_ref, ap_ref,
             dx_ref, dg_ref, dcwb_ref, dwa_ref, dwx_ref, xpad, hpad):
        _fill_front_padded(xpad, x_ref, s)
        _fill_front_padded(hpad, h_ref, s)
        apv = ap_ref[...]
        sp = _softplus(-apv)
        cb_v, wa, ba_v, wx, bx_v = cb_ref[...], wa_ref[...], ba_ref[...], wx_ref[...], bx_ref[...]
        dcwb_ref[...] = jnp.zeros_like(dcwb_ref)
        dwa_ref[...] = jnp.zeros_like(dwa_ref)
        dwx_ref[...] = jnp.zeros_like(dwx_ref)

        def back(k, carry):
            g_next, a_next, dxl_next = carry
            last_row = _iota((CONV_R, 1), 0) == CONV_R - 1
            r0 = pl.multiple_of((nq - 1 - k) * CONV_R, CONV_R)
            ext, xl, r, i, a, a2, mult, first = _lru_chunk(xpad, r0, cw_ref, cb_v, wa, ba_v, wx, bx_v, sp)
            gv = g_ref[pl.ds(r0, CONV_R), :]
            dyv = dy_ref[pl.ds(r0, CONV_R), :]
            hext = hpad[pl.ds(r0, CONV_R + PAD), :]
            ge, th = _gelu(gv)
            dg_ref[pl.ds(r0, CONV_R), :] = (dyv * _shift_down(hext, 0) * _gelu_grad(gv, th)).astype(BF16)
            b = jnp.where(last_row, a_next, pltpu.roll(a, CONV_R - 1, 0))
            bcum, dcum = _scan_rev(b, dyv * ge)
            gval = dcum + bcum * g_next
            hprev = _shift_down(hext, 1)
            da = gval * hprev
            dxl = gval * i * mult
            di = gval * xl * mult
            dmult = jnp.where(first, 0.0, gval * xl * i)
            dla = da * a - dmult * a2 / mult
            dr = dla * (-LRU_C) * sp
            dcwb_ref[7:8, :] += jnp.sum(dla * (-LRU_C) * r, axis=0, keepdims=True)
            dpr = dr * r * (1.0 - r)
            dpi = di * i * (1.0 - i)
            dxl = dxl + _dot_nt(dpr, wa) + _dot_nt(dpi, wx)
            dwa_ref[...] += _dot_tn(xl, dpr)
            dwx_ref[...] += _dot_tn(xl, dpi)
            dcwb_ref[5:6, :] += jnp.sum(dpr, axis=0, keepdims=True)
            dcwb_ref[6:7, :] += jnp.sum(dpi, axis=0, keepdims=True)
            for tap in range(4):
                dcwb_ref[tap:tap + 1, :] += jnp.sum(dxl * _shift_down(ext, 3 - tap), axis=0, keepdims=True)
            dcwb_ref[4:5, :] += jnp.sum(dxl, axis=0, keepdims=True)
            dx_ref[pl.ds(r0, CONV_R), :] = _conv_bwd_ext(jnp.concatenate([dxl, dxl_next], axis=0), cw_ref).astype(BF16)
            return _row_of(gval, 0), _row_of(a, 0), dxl[:PAD, :]

        zero = jnp.zeros((1, ct), F32)
        lax.fori_loop(0, nq, back, (zero, zero, jnp.zeros((PAD, ct), F32)))
        dcwb_ref[7:8, :] = dcwb_ref[7:8, :] * (-_sigmoid(-apv))

    nt = LRU_W // ct
    outs, jouts = _hosted(
        body, jobs, grid=(nt,),
        in_specs=[sp_["x"], sp_["g"], sp_["col"], sp_["col"], sp_["cw"], sp_["vec"], sp_["gate"], sp_["vec"], sp_["gate"],
                  sp_["vec"], sp_["vec"]],
        out_specs=(sp_["col"], sp_["col"], sp_["cw"], sp_["gate"], sp_["gate"]),
        out_shape=(jax.ShapeDtypeStruct((s, LRU_W), BF16), jax.ShapeDtypeStruct((s, LRU_W), BF16),
                   jax.ShapeDtypeStruct((SUBLANE, LRU_W), F32), jax.ShapeDtypeStruct((nt, ct, ct), F32),
                   jax.ShapeDtypeStruct((nt, ct, ct), F32)),
        scratch_shapes=[pltpu.VMEM((s + PAD, ct), F32), pltpu.VMEM((s + PAD, ct), F32)],
        name=name, args=(proj, proj, dy, hs, cw8, cb, wa_bd, ba, wx_bd, bx, ap))
    return (tuple(outs), jouts) if jobs else tuple(outs)


def _split3(v):
    hi = v.astype(BF16)
    r1 = v - hi.astype(F32)
    mid = r1.astype(BF16)
    lo = (r1 - mid.astype(F32)).astype(BF16)
    return hi, mid, lo


def _dot01(m01, v):
    mb = m01.astype(BF16)
    hi, mid, lo = _split3(v)
    f = lambda part: jnp.dot(mb, part, preferred_element_type=F32)
    return f(hi) + f(mid) + f(lo)


def _dot01_r(v, m01, parts=3):
    mb = m01.astype(BF16)
    acc = None
    for part in _split3(v)[:parts]:
        t = jnp.dot(part, mb, preferred_element_type=F32)
        acc = t if acc is None else acc + t
    return acc


def _ssd_prep(dtr, bias, alog_pad):
    l = CHUNK
    lane = _iota((1, LANE), 1)
    a_head = jnp.where(lane < N_HEAD, -jnp.exp(alog_pad), 0.0)
    dt = _softplus(dtr + bias)
    tril = (_iota((l, l), 1) <= _iota((l, l), 0)).astype(F32)
    a = dt * a_head
    cs = _dot01(tril, a)
    tot = jnp.sum(a, axis=0, keepdims=True)
    return dict(a_head=a_head, dt=dt, tril=tril, cs=cs, tot=tot)


def _col(v, h):
    lane = _iota(v.shape, 1)
    return jnp.sum(jnp.where(lane == h, v, 0.0), axis=1, keepdims=True)


def _decay_mat(cs, cst_ref, h, causal):
    row = cst_ref[h:h + 1, :]
    return jnp.exp(jnp.where(causal, _col(cs, h) - row, NEG_BIG))


def _head_mask(j, rows=CHUNK):
    lane = _iota((rows, GROUP_W), 1)
    return (lane >= j * HEAD_P) & (lane < (j + 1) * HEAD_P)


def _over_heads(v, g):
    r = v.shape[0]
    out = jnp.zeros((r, GROUP_W), F32)
    for j in range(4):
        out = jnp.where(_head_mask(j, r), _col(v, 4 * g + j), out)
    return out


def _ssd_group_fwd(q, g, xs_g, bg, cg, ht_g, cst_ref, causal, dx_g):
    dtx_g, csx_g, totx_g = _over_heads(q["dt"], g), _over_heads(q["cs"], g), _over_heads(q["tot"], g)
    xdt = xs_g * dtx_g
    ex = jnp.exp(csx_g)
    cb = _dot_nt(cg, bg)
    yoff = _dot(cg, ht_g) * ex
    ydiag = jnp.zeros((CHUNK, GROUP_W), F32)
    lms = []
    for j in range(4):
        lms.append(_decay_mat(q["cs"], cst_ref, 4 * g + j, causal))
        ydiag = jnp.where(_head_mask(j), _dot(cb * lms[j], xdt), ydiag)
    y = ydiag + yoff + xs_g * dx_g
    dsx = jnp.exp(totx_g - csx_g)
    return y, dict(xdt=xdt, ex=ex, cb=cb, yoff=yoff, dsx=dsx, dtx=dtx_g, totx=totx_g, lms=lms)


def _gated_norm_fwd(y_g, z_g, w_g):
    sz = _sigmoid(z_g)
    silu = z_g * sz
    yf = y_g * silu
    rs = lax.rsqrt(jnp.mean(yf * yf, axis=1, keepdims=True) + RMS_EPS)
    yn = yf * rs
    return yn * w_g, (sz, silu, rs, yn)


def _ssd_fwd(xact, proj, ymix, bias_pad, alog_pad, dxp, normw, *, name, jobs=()):
    s = xact.shape[0]
    nc = s // CHUNK

    def body(xa_ref, dt_ref, z_ref, _ymix_ref, bias_ref, alp_ref, dx_ref, nw_ref, y_ref, hp_ref, ht, cst):
        @pl.when(pl.program_id(0) == 0)
        def _():
            ht[...] = jnp.zeros_like(ht)

        hp_ref[...] = ht[...]
        q = _ssd_prep(dt_ref[...], bias_ref[...], alp_ref[...])
        cst[...] = q["cs"].T
        causal = q["tril"] > 0.0
        for g in range(N_GROUP):
            sl = slice(g * GROUP_W, (g + 1) * GROUP_W)
            xs_g = xa_ref[:, sl]
            bg = xa_ref[:, SSD_W + g * N_STATE:SSD_W + (g + 1) * N_STATE]
            cg = xa_ref[:, SSD_W + N_GROUP * N_STATE + g * N_STATE:SSD_W + N_GROUP * N_STATE + (g + 1) * N_STATE]
            ht_g = ht[:, sl]
            y, f = _ssd_group_fwd(q, g, xs_g, bg, cg, ht_g, cst, causal, dx_ref[:, sl])
            out, _ = _gated_norm_fwd(y, z_ref[:, sl], nw_ref[:, sl])
            y_ref[:, sl] = out.astype(BF16)
            ht[:, sl] = jnp.exp(f["totx"]) * ht_g + _dot_tn(bg, f["xdt"] * f["dsx"])

    par = lambda w: pl.BlockSpec((1, w), lambda c: (0, 0))
    (ycat, hprev), jouts = _hosted(
        body, jobs, grid=(nc,),
        in_specs=[pl.BlockSpec((CHUNK, XBC), lambda c: (c, 0)),
                  pl.BlockSpec((CHUNK, LANE), lambda c: (c, COL_DT // LANE)),
                  pl.BlockSpec((CHUNK, SSD_W), lambda c: (c, COL_Z // SSD_W)),
                  ANY_SPEC, par(LANE), par(LANE), par(SSD_W), par(SSD_W)],
        out_specs=(pl.BlockSpec((CHUNK, SSD_W), lambda c: (c, LRU_W // SSD_W)),
                   pl.BlockSpec((None, N_STATE, SSD_W), lambda c: (c, 0, 0))),
        out_shape=(jax.ShapeDtypeStruct(ymix.shape, ymix.dtype), jax.ShapeDtypeStruct((nc, N_STATE, SSD_W), F32)),
        scratch_shapes=[pltpu.VMEM((N_STATE, SSD_W), F32), pltpu.VMEM((CHUNK, LANE), F32)],
        aliases={3: 0}, name=name, args=(xact, proj, proj, ymix, bias_pad, alog_pad, dxp, normw))
    return ((ycat, hprev), jouts) if jobs else (ycat, hprev)


def _ssd_bwd(xact, proj, dycat, hprev, bias_pad, alog_pad, dxp, normw, *, name, jobs=()):
    s = xact.shape[0]
    nc = s // CHUNK
    l = CHUNK

    def body(xa_ref, dt_ref, z_ref, dy_ref, hp_ref, bias_ref, alp_ref, dx_ref, nw_ref,
             dxa_ref, ddt_ref, dz_ref, dnw_ref, small_ref, dht, cst, accx, dcsx_s, ddtx_s):
        step = pl.program_id(0)

        @pl.when(step == 0)
        def _():
            dht[...] = jnp.zeros_like(dht)
            accx[...] = jnp.zeros_like(accx)
            dnw_ref[...] = jnp.zeros_like(dnw_ref)
            small_ref[...] = jnp.zeros_like(small_ref)

        dtr = dt_ref[...]
        q = _ssd_prep(dtr, bias_ref[...], alp_ref[...])
        cst[...] = q["cs"].T
        causal = q["tril"] > 0.0
        lane = _iota((l, LANE), 1)
        head_row = _iota((LANE, l), 0)
        dcs_head = jnp.zeros((l, LANE), F32)
        dcs_rows = jnp.zeros((LANE, l), F32)
        for g in range(N_GROUP):
            sl = slice(g * GROUP_W, (g + 1) * GROUP_W)
            slb = slice(SSD_W + g * N_STATE, SSD_W + (g + 1) * N_STATE)
            slc = slice(SSD_W + N_GROUP * N_STATE + g * N_STATE, SSD_W + N_GROUP * N_STATE + (g + 1) * N_STATE)
            xs_g, bg, cg = xa_ref[:, sl], xa_ref[:, slb], xa_ref[:, slc]
            ht_g = hp_ref[:, sl]
            dxp_g = dx_ref[:, sl]
            y, f = _ssd_group_fwd(q, g, xs_g, bg, cg, ht_g, cst, causal, dxp_g)
            z_g, nw_g = z_ref[:, sl], nw_ref[:, sl]
            _o, (sz, silu, rs, yn) = _gated_norm_fwd(y, z_g, nw_g)
            dout = dy_ref[:, sl]
            dnw_ref[:, sl] += jnp.sum(dout * yn, axis=0, keepdims=True)
            dyn = dout * nw_g
            dyf = rs * (dyn - yn * jnp.mean(dyn * yn, axis=1, keepdims=True))
            dy = dyf * silu
            dz_ref[:, sl] = (dyf * y * sz * (1.0 + z_g * (1.0 - sz))).astype(BF16)
            accx[0:1, sl] += jnp.sum(dy * xs_g, axis=0, keepdims=True)
            dyo = dy * f["ex"]
            dcg = _dot_nt(dyo, ht_g)
            dht_prev = _dot_tn(cg, dyo)
            dcsx = dy * f["yoff"]
            xdt = f["xdt"]
            dxdt = jnp.zeros((l, GROUP_W), F32)
            dcb = jnp.zeros((l, l), F32)
            for j in range(4):
                h = 4 * g + j
                lm = f["lms"][j]
                sc = f["cb"] * lm
                mask = _head_mask(j)
                ds_ = jnp.where(causal, _dot_nt(jnp.where(mask, dy, 0.0), xdt), 0.0)
                dxdt = jnp.where(mask, _dot_tn(sc, dy), dxdt)
                dcb = dcb + ds_ * lm
                m = ds_ * sc
                dcs_head = dcs_head + jnp.where(lane == h, jnp.sum(m, axis=1, keepdims=True), 0.0)
                dcs_rows = dcs_rows + jnp.where(head_row == h, jnp.sum(m, axis=0, keepdims=True), 0.0)
            dhn = dht[:, sl]
            etot = jnp.exp(f["totx"])
            dxd = _dot(bg, dhn)
            dbg = _dot_nt(xdt * f["dsx"], dhn)
            dxdt = dxdt + dxd * f["dsx"]
            qq = dxd * xdt * f["dsx"]
            dcsx = dcsx - qq
            dtot = jnp.sum(qq, axis=0, keepdims=True) + jnp.sum(dhn * ht_g, axis=0, keepdims=True) * etot
            dht[:, sl] = etot * dhn + dht_prev
            dcg = dcg + _dot(dcb, bg)
            dbg = dbg + _dot_tn(dcb, cg)
            dxa_ref[:, sl] = dxdt * f["dtx"] + dy * dxp_g
            dxa_ref[:, slb] = dbg
            dxa_ref[:, slc] = dcg
            dcsx_s[:, sl] = dcsx
            ddtx_s[:, sl] = dxdt * xs_g
            accx[2:3, sl] = dtot
        reduce = (jnp.right_shift(_iota((SSD_W, LANE), 0), 6) == _iota((SSD_W, LANE), 1)).astype(F32)
        triu = (_iota((l, l), 1) >= _iota((l, l), 0)).astype(F32)
        dtot = _dot01_r(accx[...], reduce)[2:3, :]
        dcs_head = dcs_head - dcs_rows.T
        da_head = _dot01(triu, dcs_head + _dot01_r(dcsx_s[...], reduce, parts=2)) + dtot
        ddt = _dot01_r(ddtx_s[...], reduce, parts=2) + da_head * q["a_head"]
        small_ref[1:2, :] += jnp.sum(da_head * q["dt"], axis=0, keepdims=True)
        ddtr = ddt * _sigmoid(dtr + bias_ref[...])
        ddt_ref[...] = ddtr.astype(BF16)
        small_ref[0:1, :] += jnp.sum(ddtr, axis=0, keepdims=True)

        @pl.when(step == nc - 1)
        def _():
            small_ref[1:2, :] = small_ref[1:2, :] * q["a_head"]
            small_ref[2:3, :] = _dot01_r(accx[...], reduce)[0:1, :]

    rev = lambda c: nc - 1 - c
    par = lambda w: pl.BlockSpec((1, w), lambda c: (0, 0))
    outs, jouts = _hosted(
        body, jobs, grid=(nc,),
        in_specs=[pl.BlockSpec((CHUNK, XBC), lambda c: (rev(c), 0)),
                  pl.BlockSpec((CHUNK, LANE), lambda c: (rev(c), COL_DT // LANE)),
                  pl.BlockSpec((CHUNK, SSD_W), lambda c: (rev(c), COL_Z // SSD_W)),
                  pl.BlockSpec((CHUNK, SSD_W), lambda c: (rev(c), 1)),
                  pl.BlockSpec((None, N_STATE, SSD_W), lambda c: (rev(c), 0, 0)),
                  par(LANE), par(LANE), par(SSD_W), par(SSD_W)],
        out_specs=(pl.BlockSpec((CHUNK, XBC), lambda c: (rev(c), 0)),
                   pl.BlockSpec((CHUNK, LANE), lambda c: (rev(c), 0)),
                   pl.BlockSpec((CHUNK, SSD_W), lambda c: (rev(c), 0)),
                   par(SSD_W), pl.BlockSpec((SUBLANE, LANE), lambda c: (0, 0))),
        out_shape=(jax.ShapeDtypeStruct((s, XBC), F32), jax.ShapeDtypeStruct((s, LANE), BF16),
                   jax.ShapeDtypeStruct((s, SSD_W), BF16), jax.ShapeDtypeStruct((1, SSD_W), F32),
                   jax.ShapeDtypeStruct((SUBLANE, LANE), F32)),
        scratch_shapes=[pltpu.VMEM((N_STATE, SSD_W), F32), pltpu.VMEM((CHUNK, LANE), F32),
                        pltpu.VMEM((SUBLANE, SSD_W), F32), pltpu.VMEM((CHUNK, SSD_W), F32),
                        pltpu.VMEM((CHUNK, SSD_W), F32)],
        name=name, args=(xact, proj, proj, dycat, hprev, bias_pad, alog_pad, dxp, normw))
    return (tuple(outs), jouts) if jobs else tuple(outs)


def _blockdiag(w):
    per = LRU_CT // HEAD_P
    w2 = w.reshape(N_HEAD // per, per, HEAD_P, HEAD_P)
    z = jnp.zeros((N_HEAD // per, HEAD_P, HEAD_P), w.dtype)
    rows = [jnp.concatenate([w2[:, i] if j == i else z for j in range(per)], axis=2) for i in range(per)]
    return jnp.concatenate(rows, axis=1)


def _unblockdiag(wbd):
    per = LRU_CT // HEAD_P
    parts = [wbd[:, i * HEAD_P:(i + 1) * HEAD_P, i * HEAD_P:(i + 1) * HEAD_P] for i in range(per)]
    return jnp.stack(parts, axis=1).reshape(N_HEAD, HEAD_P, HEAD_P)


def _pad_rows8(w):
    return jnp.concatenate([w, jnp.zeros((SUBLANE - w.shape[0], w.shape[1]), w.dtype)], axis=0)


def _pad_lane(v):
    return jnp.concatenate([v, jnp.zeros((1, LANE - v.shape[1]), v.dtype)], axis=1)


class _NoExchange:
    def ride(self, host):
        return []

    def done(self, jobs, outs, w):
        pass

    def grad(self, name, val):
        pass

    def small(self, raw):
        pass

    def pairs_now(self):
        pass


def _local_step(x, p, tgt, w, hooks=_NoExchange()):
    cw_l = _pad_rows8(w["lru_conv_w"])
    cw_s = _pad_rows8(w["ssd_conv_w"])
    wa_bd = _blockdiag(w["lru_gate_a_w"])
    wx_bd = _blockdiag(w["lru_gate_x_w"])
    ba = w["lru_gate_a_b"].reshape(1, LRU_W)
    bx = w["lru_gate_x_b"].reshape(1, LRU_W)
    bias_pad = _pad_lane(w["ssd_dt_bias"])
    alog_pad = _pad_lane(w["ssd_a_log"])
    dxp = jnp.repeat(w["ssd_d"], HEAD_P, axis=1)

    def host(fn, *a, name, **k):
        jobs = hooks.ride(name)
        res = fn(*a, name=name, jobs=jobs, **k)
        if jobs:
            res, jouts = res
            hooks.done(jobs, jouts, w)
        return res

    def grad(n, val):
        g[n] = val
        hooks.grad(n, val)

    xb = x.astype(BF16)
    proj = host(_mm, xb, w["w_in_t"], "nt", tm=2048, tn=512, name="in_proj")
    ymix, h_lru = host(_lru_fwd, proj, cw_l, w["lru_conv_b"], wa_bd, ba, wx_bd, bx, w["lru_a_param"], name="lru_fwd")
    xact = host(_conv_silu_fwd, proj, cw_s, w["ssd_conv_b"], col0=COL_XBC, width=XBC, ct=256, name="ssd_conv_fwd")
    ycat, hprev = host(_ssd_fwd, xact, proj, ymix, bias_pad, alog_pad, dxp, w["ssd_norm_w"], name="ssd_fwd")
    mix, x1, x1b = _mm_ln(ycat, w["w_out"], x, w["ln1_g"], w["ln1_b"], tm=512, name="out_proj")
    pre = host(_mm, x1b, w["w_ff1"], "nn", tm=2048, tn=512, out_dtype=BF16, name="ff1")
    ff, x2, x2b = _mm_ln(pre, w["w_ff2"], x1, w["ln2_g"], w["ln2_b"], tm=512, a_fn=_relu2, name="ff2")
    loss, dgpre, dple, dt3, dg3, db3 = _head(x2, x2b, p, w["w_ple_gate"], w["w_ple"], w["ln3_g"], w["ln3_b"], tgt,
                                             name="head")

    g = {}
    g["ln3_g"], g["ln3_b"] = dg3, db3
    grad("w_ple_gate", _mm(x2b, dgpre, "tn", tm=512, tn=1024, out_dtype=BF16, name="d_w_ple_gate"))
    grad("w_ple", _mm(p, dple, "tn", tm=256, tn=512, dest_major=True, out_dtype=BF16, name="d_w_ple"))
    dt2, dt2b, g["ln2_g"], g["ln2_b"] = host(_mm_ln_bwd, dgpre, w["w_ple_gate"], x1, ff, w["ln2_g"], dt3, ALPHA,
                                             tm=512, name="d_x2")
    grad("w_ff2", host(_mm, pre, dt2b, "tn", tm=512, tn=1024, a_fn=_relu2, out_dtype=BF16, name="d_w_ff2"))
    dpre = host(_mm, dt2b, w["w_ff2"], "nt", tm=2048, tn=512, extra=pre, out_dtype=BF16,
                epi=lambda acc, pv: acc * 2.0 * jnp.maximum(pv.astype(F32), 0.0), name="d_pre")
    grad("w_ff1", host(_mm, x1b, dpre, "tn", tm=1024, tn=512, dest_major=True, out_dtype=BF16, name="d_w_ff1"))
    dt1, dt1b, g["ln1_g"], g["ln1_b"] = host(_mm_ln_bwd, dpre, w["w_ff1"], x, mix, w["ln1_g"], dt2, ALPHA,
                                             tm=256, name="d_x1")
    grad("w_out", host(_mm, ycat, dt1b, "tn", tm=512, tn=1024, out_dtype=BF16, name="d_w_out"))
    dycat = host(_mm, dt1b, w["w_out"], "nt", tm=2048, tn=512, name="d_ycat")
    dxl, dgl, dcwb_l, dwa, dwx = host(_lru_bwd, proj, dycat, h_lru, cw_l, w["lru_conv_b"], wa_bd, ba, wx_bd, bx,
                                      w["lru_a_param"], name="lru_bwd")
    g["lru_gate_a_w"] = _unblockdiag(dwa)
    g["lru_gate_x_w"] = _unblockdiag(dwx)
    raw = dict(lru=dcwb_l, gate_a=g["lru_gate_a_w"].reshape(N_HEAD * HEAD_P, HEAD_P).astype(BF16),
               gate_x=g["lru_gate_x_w"].reshape(N_HEAD * HEAD_P, HEAD_P).astype(BF16))
    hooks.small(raw)
    dxact, ddt, dz, g["ssd_norm_w"], small = host(_ssd_bwd, xact, proj, dycat, hprev, bias_pad, alog_pad, dxp,
                                                   w["ssd_norm_w"], name="ssd_bwd")
    dxbc, dcwb_s = host(_conv_silu_bwd, proj, dxact, cw_s, w["ssd_conv_b"], col0=COL_XBC, width=XBC, ct=256,
                        name="ssd_conv_bwd")
    pieces, offsets = [dxl, dgl, dz, dxbc, ddt], [0, COL_G, COL_Z, COL_XBC, COL_DT]

    g["lru_conv_w"] = dcwb_l[0:4]
    g["lru_conv_b"] = dcwb_l[4:5]
    g["lru_gate_a_b"] = dcwb_l[5:6]
    g["lru_gate_x_b"] = dcwb_l[6:7]
    g["lru_a_param"] = dcwb_l[7:8]
    g["ssd_conv_w"] = dcwb_s[0:4]
    g["ssd_conv_b"] = dcwb_s[4:5]
    g["ssd_dt_bias"] = small[0:1, :N_HEAD]
    g["ssd_a_log"] = small[1:2, :N_HEAD]
    g["ssd_d"] = small[2:3, :N_HEAD]
    rows = jnp.concatenate([g[n] for n in ("ssd_norm_w", "ln1_g", "ln1_b", "ln2_g", "ln2_b", "ln3_g", "ln3_b")]
                           + [jnp.broadcast_to(loss[:, 0:1], (1, D_MODEL))], axis=0)
    late = dict(ssd=dcwb_s, heads=small, rows=rows)
    hooks.small(late)
    raw.update(late)
    dwt = None
    for q, (pc, off) in enumerate(zip(pieces, offsets)):
        dwt = host(_mm, pc, xb, "tn", tm=512, tn=1024, out_dtype=BF16, into=(dwt, off, D_IN),
                   name="d_w_in_%d" % q)
    grad("w_in", dwt)
    hooks.pairs_now()
    grad_x = host(_mm_pieces, pieces, offsets, w["w_in_t"], tm=256, extra=dt1, epi=lambda acc, e: acc + ALPHA * e,
                  name="d_x")
    return loss[0, 0], grad_x, g, raw


ANY_SPEC = pl.BlockSpec(memory_space=pl.ANY)


def _mesh_pos():
    return lax.axis_index("x"), lax.axis_index("y"), lax.axis_index("c")


def _remote(src, dst, send, recv, k, to):
    return pltpu.make_async_remote_copy(src_ref=src, dst_ref=dst, send_sem=send.at[k], recv_sem=recv.at[k],
                                        device_id=to, device_id_type=MESH_T)


class _Job:
    N_SEM = 9

    def __init__(self, kind, inp):
        self.kind, self.inp = kind, inp
        shape = {"gather": (N_DEV,) + inp.shape, "relay": (N_DEV,) + inp.shape, "pair": (4,) + inp.shape[1:],
                 "chip": inp.shape}[kind]
        self.out = jax.ShapeDtypeStruct(shape, inp.dtype)
        self.top = (inp.shape[0] // 2) // 16 * 16

    def _relay_copies(self, inp, out, send, recv):
        x, y, c = _mesh_pos()
        sib, xn, yn, dg = (x, y, 1 - c), (1 - x, y, c), (x, 1 - y, c), (1 - x, 1 - y, c)
        blk = lambda p, cc=None: out.at[4 * p[0] + 2 * p[1] + (p[2] if cc is None else cc)]
        top = lambda r: r.at[pl.ds(0, self.top)]
        bot = lambda r: r.at[pl.ds(self.top, self.inp.shape[0] - self.top)]
        mine = blk((x, y, c))
        plan = [
            (inp, mine, sib, blk(sib)),
            (inp, mine, xn, blk(xn)),
            (inp, mine, yn, blk(yn)),
            (top(blk(xn)), top(blk(xn)), yn, top(blk(dg))),
            (bot(blk(yn)), bot(blk(yn)), xn, bot(blk(dg))),
            (blk(xn), blk(xn), sib, blk(xn, 1 - c)),
            (blk(yn), blk(yn), sib, blk(yn, 1 - c)),
            (top(blk(dg)), top(blk(dg)), sib, top(blk(dg, 1 - c))),
            (bot(blk(dg)), bot(blk(dg)), sib, bot(blk(dg, 1 - c))),
        ]
        me = (x, y, c)
        return [(_remote(s, d, send, recv, k, to), _remote(s, land, send, recv, k, me))
                for k, (s, d, to, land) in enumerate(plan)]

    def _places(self):
        x, y, c = _mesh_pos()
        return (x, y, c), (x, y, 1 - c), [(1 - x, y), (x, 1 - y), (1 - x, 1 - y)]

    def start(self, inp, out, send, recv, loc):
        me, sibling, chips = self._places()
        x, y, c = me
        if self.kind == "relay":
            pltpu.make_async_copy(inp, out.at[4 * x + 2 * y + c], loc.at[0]).start()
            cps = self._relay_copies(inp, out, send, recv)
            for k in (0, 1, 2):
                cps[k][0].start()
        elif self.kind == "gather":
            mine = out.at[4 * x + 2 * y + c]
            pltpu.make_async_copy(inp, mine, loc.at[0]).start()
            _remote(inp, mine, send, recv, 0, sibling).start()
            for j, chip in enumerate(chips):
                _remote(inp, mine, send, recv, 1 + j, (*chip, c)).start()
        elif self.kind == "pair":
            for k in range(4):
                _remote(inp.at[2 * k + (1 - c)], out.at[k], send, recv, k, sibling).start()
        else:
            kme = 2 * x + y
            pltpu.make_async_copy(inp.at[kme], out.at[kme], loc.at[0]).start()
            for j, (tx, ty) in enumerate(chips):
                _remote(inp.at[2 * tx + ty], out.at[kme], send, recv, j, (tx, ty, c)).start()

    def mid(self, inp, out, send, recv, loc):
        if self.kind == "relay":
            cps = self._relay_copies(inp, out, send, recv)
            for k, onward in ((1, (3, 5)), (2, (4, 6))):
                cps[k][1].wait_recv()
                for q in onward:
                    cps[q][0].start()
            return
        if self.kind != "gather":
            return
        me, sibling, chips = self._places()
        c = me[2]
        for j, chip in enumerate(chips):
            landed = out.at[4 * chip[0] + 2 * chip[1] + c]
            _remote(landed, landed, send, recv, 1 + j, me).wait_recv()
            _remote(landed, landed, send, recv, 4 + j, sibling).start()

    def finish(self, inp, out, send, recv, loc):
        me, sibling, chips = self._places()
        x, y, c = me
        if self.kind == "relay":
            cps = self._relay_copies(inp, out, send, recv)
            for k, onward in ((3, 7), (4, 8)):
                cps[k][1].wait_recv()
                cps[onward][0].start()
            for k in (0, 5, 6, 7, 8):
                cps[k][1].wait_recv()
            for k in range(9):
                cps[k][0].wait_send()
            pltpu.make_async_copy(inp, out.at[4 * x + 2 * y + c], loc.at[0]).wait()
        elif self.kind == "gather":
            blk = lambda px, py, pc: out.at[4 * px + 2 * py + pc]
            mine = blk(*me)
            _remote(inp, blk(*sibling), send, recv, 0, me).wait_recv()
            for j, chip in enumerate(chips):
                _remote(inp, blk(*chip, 1 - c), send, recv, 4 + j, me).wait_recv()
            for k in range(7):
                _remote(inp, mine, send, recv, k, sibling).wait_send()
            pltpu.make_async_copy(inp, mine, loc.at[0]).wait()
        elif self.kind == "pair":
            for k in range(4):
                _remote(inp.at[2 * k + (1 - c)], out.at[k], send, recv, k, sibling).wait()
        else:
            kme = 2 * x + y
            for j, (tx, ty) in enumerate(chips):
                _remote(inp.at[kme], out.at[2 * tx + ty], send, recv, j, (tx, ty, c)).wait_recv()
            for j, (tx, ty) in enumerate(chips):
                _remote(inp.at[2 * tx + ty], out.at[kme], send, recv, j, (tx, ty, c)).wait_send()
            pltpu.make_async_copy(inp.at[kme], out.at[kme], loc.at[0]).wait()


def _job_scratch(jobs):
    sem = pltpu.SemaphoreType.DMA
    return [s for _ in jobs for s in (sem((_Job.N_SEM,)), sem((_Job.N_SEM,)), sem((1,)))]


def _run_jobs(jobs, method, jins, jouts, jsems, only=None):
    for q, job in enumerate(jobs):
        if only is None or only[q]:
            getattr(job, method)(jins[q], jouts[q], *jsems[3 * q:3 * q + 3])


def _exchange(jobs, *, name):
    n = len(jobs)

    def body(*refs):
        jins, jouts, jsems = refs[:n], refs[n:2 * n], refs[2 * n:]
        _run_jobs(jobs, "start", jins, jouts, jsems)
        _run_jobs(jobs, "mid", jins, jouts, jsems)
        _run_jobs(jobs, "finish", jins, jouts, jsems)

    return _pcall(body, in_specs=[ANY_SPEC] * n, out_specs=[ANY_SPEC] * n, out_shape=[j.out for j in jobs],
                  scratch_shapes=_job_scratch(jobs), name=name)(*[j.inp for j in jobs])


def _hosted(body, jobs, *, grid, in_specs, out_specs, out_shape, args, name, scratch_shapes=(), aliases=None):
    in_specs, out_specs, out_shape = list(in_specs), list(out_specs), list(out_shape)
    scratch_shapes = list(scratch_shapes)
    n_in, n_out, n_scr, nj = len(in_specs), len(out_specs), len(scratch_shapes), len(jobs)
    sem = ("arbitrary",) * len(grid)
    kw = dict(input_output_aliases=aliases) if aliases else {}
    if not jobs:
        res = _pcall(body, grid=grid, in_specs=in_specs, out_specs=out_specs, out_shape=out_shape,
                     scratch_shapes=scratch_shapes, name=name, compiler_params=_cparams(sem), **kw)(*args)
        return list(res), []

    def full(*refs):
        ins, jins = refs[:n_in], refs[n_in:n_in + nj]
        o0 = n_in + nj
        outs, jouts = refs[o0:o0 + n_out], refs[o0 + n_out:o0 + n_out + nj]
        s0 = o0 + n_out + nj
        scr, jsems = refs[s0:s0 + n_scr], refs[s0 + n_scr:]
        step = pl.program_id(0)
        for ax in range(1, len(grid)):
            step = step * grid[ax] + pl.program_id(ax)
        total = math.prod(grid)
        early = [job.kind == "relay" for job in jobs]
        mid_step = (3 * total) // 5
        split = any(early) and 0 < mid_step < total - 1

        @pl.when(step == 0)
        def _():
            _run_jobs(jobs, "start", jins, jouts, jsems)

        if split:
            @pl.when(step == mid_step)
            def _():
                _run_jobs(jobs, "mid", jins, jouts, jsems, only=early)

        body(*ins, *outs, *scr)

        @pl.when(step == total - 1)
        def _():
            _run_jobs(jobs, "mid", jins, jouts, jsems, only=[not e for e in early] if split else None)
            _run_jobs(jobs, "finish", jins, jouts, jsems)

    res = _pcall(full, grid=grid, in_specs=in_specs + [ANY_SPEC] * nj, out_specs=out_specs + [ANY_SPEC] * nj,
                 out_shape=out_shape + [j.out for j in jobs], scratch_shapes=scratch_shapes + _job_scratch(jobs),
                 name=name, compiler_params=_cparams(sem), **kw)(*args, *[j.inp for j in jobs])
    return list(res[:n_out]), list(res[n_out:])


def _pair_add(g8, r4, cidx, *, name):
    _, r, c = g8.shape
    tr = ROW_TILE if r % ROW_TILE == 0 else r

    def body(c_ref, g_ref, r_ref, o_ref):
        o_ref[...] = (g_ref[...].astype(F32) + r_ref[...].astype(F32)).astype(BF16)

    return _pcall(
        body,
        grid_spec=pltpu.PrefetchScalarGridSpec(
            num_scalar_prefetch=1, grid=(4, r // tr),
            in_specs=[pl.BlockSpec((None, tr, c), lambda k, i, cr: (2 * k + cr[0], i, 0)),
                      pl.BlockSpec((None, tr, c), lambda k, i, cr: (k, i, 0))],
            out_specs=pl.BlockSpec((None, tr, c), lambda k, i, cr: (k, i, 0))),
        out_shape=jax.ShapeDtypeStruct((4, r, c), BF16), name=name,
        compiler_params=_cparams(("parallel", "parallel")))(cidx, g8, r4)


def _adam_update(g, w_ref, m_ref, v_ref, g_ref, d_ref, mo_ref, vo_ref):
    c1 = 1.0 - ADAM_B1 ** ADAM_STEP
    c2 = 1.0 - ADAM_B2 ** ADAM_STEP
    m2 = ADAM_B1 * m_ref[...] + (1.0 - ADAM_B1) * g
    v2 = ADAM_B2 * v_ref[...] + (1.0 - ADAM_B2) * (g * g)
    g_ref[...] = g
    mo_ref[...] = m2
    vo_ref[...] = v2
    d_ref[...] = -ADAM_LR * ((m2 / c1) / (jnp.sqrt(v2 / c2) + ADAM_EPS) + ADAM_WD * w_ref[...])


def _adamw_rows(srcs, items, own_cols, me1, loss_row, *, name):
    ns, ni, no = len(srcs), len(items), len(own_cols)
    full = lambda a: pl.BlockSpec(a.shape, lambda i, me: (0,) * a.ndim)
    in_specs = [full(a) for a in srcs]
    args = list(srcs)
    for (si, _r0, w, _m, _v) in own_cols:
        a = srcs[si]
        in_specs.append(pl.BlockSpec((N_DEV, a.shape[1], w.shape[1]), lambda i, me: (0, 0, me[0])))
        args.append(a)
    out_specs, out_shape = [], []
    for (_si, _r0, w, m, v) in list(items) + list(own_cols):
        in_specs += [full(w)] * 3
        args += [w, m, v]
        out_specs += [full(w)] * 4
        out_shape += [jax.ShapeDtypeStruct(w.shape, F32)] * 4
    out_specs.append(pl.BlockSpec((1, LANE), lambda i, me: (0, 0)))
    out_shape.append(jax.ShapeDtypeStruct((1, LANE), F32))

    def body(me_ref, *refs):
        src_refs, own_refs = refs[:ns], refs[ns:ns + no]
        wmv = refs[ns + no:ns + no + 3 * (ni + no)]
        outs = refs[ns + no + 3 * (ni + no):]
        lsrc, lrow = src_refs[loss_row[0]], loss_row[1]
        total = lsrc[0, lrow:lrow + 1, 0:LANE]
        for d in range(1, N_DEV):
            total = total + lsrc[d, lrow:lrow + 1, 0:LANE]
        outs[-1][...] = total
        for q, (si, r0, w, _m, _v) in enumerate(list(items) + list(own_cols)):
            nr, cw = w.shape
            gref = src_refs[si] if q < ni else own_refs[q - ni]
            g = gref[0, r0:r0 + nr, 0:cw]
            for d in range(1, N_DEV):
                g = g + gref[d, r0:r0 + nr, 0:cw]
            _adam_update(g, *wmv[3 * q:3 * q + 3], *outs[4 * q:4 * q + 4])

    res = _pcall(
        body,
        grid_spec=pltpu.PrefetchScalarGridSpec(num_scalar_prefetch=1, grid=(1,), in_specs=in_specs, out_specs=out_specs),
        out_shape=out_shape, name=name, compiler_params=_cparams(("arbitrary",)))(me1, *args)
    return [tuple(res[4 * q:4 * q + 4]) for q in range(ni + no)], res[-1]


def _adamw(gsrc, w, m, v, *, name):
    k, r, c = gsrc.shape
    tr = ROW_TILE if r % ROW_TILE == 0 else r

    def body(gs_ref, w_ref, m_ref, v_ref, g_ref, d_ref, mo_ref, vo_ref):
        g = gs_ref[0].astype(F32)
        for q in range(1, k):
            g = g + gs_ref[q].astype(F32)
        _adam_update(g, w_ref, m_ref, v_ref, g_ref, d_ref, mo_ref, vo_ref)

    tc = c
    if tr == r and r > ROW_TILE and c % 256 == 0:
        tc = 256
    blk = pl.BlockSpec((tr, tc), lambda i, j: (i, j))
    sd = jax.ShapeDtypeStruct((r, c), F32)
    return _pcall(body, grid=(r // tr, c // tc),
                  in_specs=[pl.BlockSpec((k, tr, tc), lambda i, j: (0, i, j)), blk, blk, blk],
                  out_specs=(blk, blk, blk, blk), out_shape=(sd, sd, sd, sd), name=name,
                  compiler_params=_cparams(("parallel", "parallel")))(gsrc, w, m, v)


WEIGHTS = ['w_in', 'lru_conv_w', 'lru_conv_b', 'lru_gate_a_w', 'lru_gate_a_b', 'lru_gate_x_w', 'lru_gate_x_b',
           'lru_a_param', 'ssd_conv_w', 'ssd_conv_b', 'ssd_dt_bias', 'ssd_a_log', 'ssd_d', 'ssd_norm_w', 'w_out',
           'ln1_g', 'ln1_b', 'w_ff1', 'w_ff2', 'ln2_g', 'ln2_b', 'w_ple_gate', 'w_ple', 'ln3_g', 'ln3_b']
BIG = ['w_in', 'w_out', 'w_ff1', 'w_ff2', 'w_ple_gate', 'w_ple']
COL_SHARDED = ('w_ff1', 'w_ple')
CONV = ['lru_conv_w', 'ssd_conv_w']
REPL = [n for n in WEIGHTS if n not in BIG and n not in CONV]
CONV_CH = {'lru_conv_w': LRU_W, 'ssd_conv_w': XBC}


def _to_dest_major(name, gfull):
    if name in COL_SHARDED:
        r, cfull = gfull.shape
        return gfull.reshape(r, N_DEV, cfull // N_DEV).transpose(1, 0, 2)
    rfull, cdim = gfull.shape
    return gfull.reshape(N_DEV, rfull // N_DEV, cdim)


def _full_weight(name, gathered):
    if name in COL_SHARDED:
        _, r, cs = gathered.shape
        full = gathered.transpose(1, 0, 2).reshape(r, N_DEV * cs)
    else:
        _, rs, cdim = gathered.shape
        full = gathered.reshape(N_DEV * rs, cdim)
    if name == 'w_in':
        full = lax.dynamic_update_slice(jnp.zeros((D_IN_PAD, D_MODEL), full.dtype), full, (0, 0))
    return full


SMALL_SRC = ("lru", "ssd", "heads", "rows", "gate_a", "gate_x")
AG_HOSTS = {"in_proj": ("w_ff1",), "lru_fwd": ("w_ff2",), "ssd_fwd": ("w_out",), "ff1": ("w_ple_gate", "w_ple")}
PAIR_HOSTS = ("d_x2", "d_x1", "d_ycat")
CHIP_HOSTS = {"d_pre": ("w_ple_gate", "w_ple"), "lru_bwd": ("w_ff2",), "ssd_bwd": ("w_ff1",), "ssd_conv_bwd": ("w_out",),
              "d_x": ("w_in",)}
SMALL_HOSTS = {"ssd_bwd": ("lru", "gate_a", "gate_x"), "d_w_in_3": ("ssd", "heads", "rows")}


class _Schedule:
    def __init__(self, shards, cidx):
        self.shards, self.cidx = shards, cidx
        self.pair, self.chip, self.small_jobs = [], [], []
        self.dest, self.summed, self.gathered_small = {}, {}, {}
        self.tags = []

    def ride(self, host):
        tags = []
        if host in AG_HOSTS:
            tags = [("weight", n, self.shards[n]) for n in AG_HOSTS[host]]
        elif host in PAIR_HOSTS or host in CHIP_HOSTS or host == "flush":
            tags = [("pair", n, a) for n, a in self.pair]
            self.pair = []
            if host not in PAIR_HOSTS:
                take = [t for t in self.chip if host == "flush" or t[0] in CHIP_HOSTS[host]]
                tags += [("chip", n, a) for n, a in take]
                self.chip = [t for t in self.chip if not any(t is u for u in take)]
        if host in SMALL_HOSTS:
            tags += [("small", n, a) for n, a in self.small_jobs if n in SMALL_HOSTS[host]]
            self.small_jobs = [t for t in self.small_jobs if t[0] not in SMALL_HOSTS[host]]
        self.tags = tags
        return [_Job({"weight": "relay", "small": "gather"}.get(kind, kind), a) for kind, _n, a in tags]

    def done(self, jobs, outs, w):
        for (kind, n, _a), o in zip(self.tags, outs):
            if kind == "weight":
                w[n] = _full_weight(n, o)
            elif kind == "small":
                self.gathered_small[n] = o
            elif kind == "pair":
                self.chip.append((n, _pair_add(self.dest[n], o, self.cidx, name="rs_pair_add_" + n)))
            else:
                self.summed[n] = o

    def grad(self, name, val):
        self.dest[name] = val if val.ndim == 3 else _to_dest_major(name, val)
        self.pair.append((name, self.dest[name]))

    def small(self, raw):
        self.small_jobs += list(raw.items())

    def pairs_now(self):
        tags = [("pair", n, a) for n, a in self.pair]
        self.pair, self.tags = [], tags
        jobs = [_Job("pair", a) for _k, _n, a in tags]
        self.done(jobs, _exchange(jobs, name="rs_pairs_now"), None)

    def flush(self):
        step = 0
        while self.pair or self.chip:
            jobs = self.ride("flush")
            self.done(jobs, _exchange(jobs, name="rs_flush_%d" % step), None)
            step += 1


def kernel(x, p, w_in, lru_conv_w, lru_conv_b, lru_gate_a_w, lru_gate_a_b, lru_gate_x_w, lru_gate_x_b, lru_a_param, ssd_conv_w, ssd_conv_b, ssd_dt_bias, ssd_a_log, ssd_d, ssd_norm_w, w_out, ln1_g, ln1_b, w_ff1, w_ff2, ln2_g, ln2_b, w_ple_gate, w_ple, ln3_g, ln3_b, loss_target, m_w_in, m_lru_conv_w, m_lru_conv_b, m_lru_gate_a_w, m_lru_gate_a_b, m_lru_gate_x_w, m_lru_gate_x_b, m_lru_a_param, m_ssd_conv_w, m_ssd_conv_b, m_ssd_dt_bias, m_ssd_a_log, m_ssd_d, m_ssd_norm_w, m_w_out, m_ln1_g, m_ln1_b, m_w_ff1, m_w_ff2, m_ln2_g, m_ln2_b, m_w_ple_gate, m_w_ple, m_ln3_g, m_ln3_b, v_w_in, v_lru_conv_w, v_lru_conv_b, v_lru_gate_a_w, v_lru_gate_a_b, v_lru_gate_x_w, v_lru_gate_x_b, v_lru_a_param, v_ssd_conv_w, v_ssd_conv_b, v_ssd_dt_bias, v_ssd_a_log, v_ssd_d, v_ssd_norm_w, v_w_out, v_ln1_g, v_ln1_b, v_w_ff1, v_w_ff2, v_ln2_g, v_ln2_b, v_w_ple_gate, v_w_ple, v_ln3_g, v_ln3_b):
    given = dict(locals())
    def local(a, n):
        return jnp.swapaxes(a[0], 0, 1) if n == 'w_in' else a[0]

    wsh = {n: local(given[n], n) for n in WEIGHTS}
    msh = {n: local(given["m_" + n], n) for n in WEIGHTS}
    vsh = {n: local(given["v_" + n], n) for n in WEIGHTS}
    xi, yi, ci = _mesh_pos()
    me = 4 * xi + 2 * yi + ci

    shards = {n: wsh[n].astype(BF16) for n in BIG}
    conv_pack = jnp.concatenate([_pad_rows8(wsh[n]) for n in CONV], axis=1)
    g_in, gconv = _exchange([_Job("relay", shards['w_in']), _Job("gather", conv_pack)], name="ag_first")
    full = {'w_in_t': _full_weight('w_in', g_in)}
    c0 = 0
    for n in CONV:
        cw = CONV_CH[n] // N_DEV
        full[n] = gconv[:, :4, c0:c0 + cw].transpose(1, 0, 2).reshape(4, CONV_CH[n])
        c0 += cw
    for n in REPL:
        full[n] = given[n] if given[n].ndim == 2 else wsh[n]

    sched = _Schedule(shards, jnp.reshape(ci, (1,)).astype(jnp.int32))
    loss_local, grad_x, g, raw = _local_step(x[0], p[0, 0], loss_target[0], full, sched)
    sched.flush()
    summed, gat = sched.summed, sched.gathered_small

    outs = {}
    for n in BIG:
        outs[n] = _adamw(summed[n], wsh[n], msh[n], vsh[n], name="adamw_" + n)
    for n, k in (("lru_gate_a_w", "gate_a"), ("lru_gate_x_w", "gate_x")):
        flat = lambda a: a.reshape(N_HEAD * HEAD_P, HEAD_P)
        res = _adamw(gat[k], flat(wsh[n]), flat(msh[n]), flat(vsh[n]), name="adamw_" + n)
        outs[n] = tuple(r.reshape(N_HEAD, HEAD_P, HEAD_P) for r in res)
    for n, row in (("lru_gate_a_b", 5), ("lru_gate_x_b", 6)):
        outs[n] = _adamw(gat["lru"][:, row].reshape(N_DEV, N_HEAD, HEAD_P), wsh[n], msh[n], vsh[n], name="adamw_" + n)
    row_items = [("lru_conv_b", 0, 4), ("lru_a_param", 0, 7),
                 ("ssd_conv_b", 1, 4), ("ssd_dt_bias", 2, 0), ("ssd_a_log", 2, 1), ("ssd_d", 2, 2),
                 ("ssd_norm_w", 3, 0), ("ln1_g", 3, 1), ("ln1_b", 3, 2), ("ln2_g", 3, 3), ("ln2_b", 3, 4),
                 ("ln3_g", 3, 5), ("ln3_b", 3, 6)]
    vec = lambda a: a.reshape(1, -1)
    items = [(si, r0, vec(given[n]), vec(given["m_" + n]), vec(given["v_" + n])) for n, si, r0 in row_items]
    own = [(si, 0, wsh[n], msh[n], vsh[n]) for n, si in (("lru_conv_w", 0), ("ssd_conv_w", 1))]
    me1 = jnp.reshape(me, (1,)).astype(jnp.int32)
    res, loss_row = _adamw_rows([gat[k] for k in SMALL_SRC[:4]], items, own, me1, (3, 7), name="adamw_small")
    loss = loss_row[0, 0]
    for (n, _si, _r0), r4 in zip(row_items, res[:len(row_items)]):
        outs[n] = r4
    for n, r4 in zip(CONV, res[len(row_items):]):
        outs[n] = r4

    def fin(n, k):
        a = jnp.swapaxes(outs[n][k], 0, 1) if n == 'w_in' else outs[n][k]
        return a.reshape(given[n].shape)

    return (loss, grad_x[None],
            *[fin(n, 0) for n in WEIGHTS], *[fin(n, 1) for n in WEIGHTS],
            *[fin(n, 2) for n in WEIGHTS], *[fin(n, 3) for n in WEIGHTS])
```

```python
import math

import jax
import jax.numpy as jnp
from jax import lax
from jax.experimental import pallas as pl
from jax.experimental.pallas import tpu as pltpu

F32 = jnp.float32
BF16 = jnp.bfloat16

N_DEV = 8
D_MODEL = 1024
LRU_W = 1024
SSD_W = 1024
XBC = 2048
N_HEAD = 16
HEAD_P = 64
N_GROUP = 4
GROUP_W = 256
N_STATE = 128
CHUNK = 128
D_IN = 5136
D_IN_PAD = 5632
COL_G = 1024
COL_Z = 2048
COL_XBC = 3072
COL_DT = 5120
LRU_C = 8.0
ALPHA = 2.0 ** 0.25
LN_EPS = 1e-5
RMS_EPS = 1e-5
ADAM_LR = 0.001
ADAM_B1 = 0.9
ADAM_B2 = 0.999
ADAM_EPS = 1e-08
ADAM_WD = 0.01
ADAM_STEP = 10
GELU_C = math.sqrt(2.0 / math.pi)
LANE = 128
SUBLANE = 8
VMEM_LIMIT = 48 * 1024 * 1024
MESH_T = pl.DeviceIdType.MESH
NEG_BIG = -1e30


def _pcall(body, **kw):
    return pl.pallas_call(body, **kw)


def _cparams(sem):
    return pltpu.CompilerParams(dimension_semantics=sem, vmem_limit_bytes=VMEM_LIMIT)


def _dot(a, b):
    return jnp.dot(a.astype(BF16), b.astype(BF16), preferred_element_type=F32)


def _dot_nt(a, b):
    return lax.dot_general(a.astype(BF16), b.astype(BF16), (((1,), (1,)), ((), ())), preferred_element_type=F32)


def _dot_tn(a, b):
    return lax.dot_general(a.astype(BF16), b.astype(BF16), (((0,), (0,)), ((), ())), preferred_element_type=F32)


def _sigmoid(x):
    return jax.nn.sigmoid(x)


def _softplus(v):
    return jnp.maximum(v, 0.0) + jnp.log1p(jnp.exp(-jnp.abs(v)))


def _gelu(x):
    th = jnp.tanh(GELU_C * (x + 0.044715 * x * x * x))
    return 0.5 * x * (1.0 + th), th


def _gelu_grad(x, th):
    return 0.5 * (1.0 + th) + 0.5 * x * (1.0 - th * th) * GELU_C * (1.0 + 3.0 * 0.044715 * x * x)


def _iota(shape, dim):
    return lax.broadcasted_iota(jnp.int32, shape, dim)


def _mm(a, b, mode, *, tm, tn, name, a_fn=None, extra=None, epi=None, out_dtype=F32, dest_major=False, into=None,
        jobs=()):
    m = a.shape[1] if mode == "tn" else a.shape[0]
    n = b.shape[0] if mode == "nt" else b.shape[1]
    tm, tn = min(tm, m), min(tn, n)
    if dest_major:
        tn = n // N_DEV
    if mode == "nn":
        m, k = a.shape
        _, n = b.shape
        a_spec = pl.BlockSpec((tm, k), lambda i, j: (i, 0))
        b_spec = pl.BlockSpec((k, tn), lambda i, j: (0, j))
        dims = ((1,), (0,))
    elif mode == "nt":
        m, k = a.shape
        n, _ = b.shape
        a_spec = pl.BlockSpec((tm, k), lambda i, j: (i, 0))
        b_spec = pl.BlockSpec((tn, k), lambda i, j: (j, 0))
        dims = ((1,), (1,))
    else:
        k, m = a.shape
        _, n = b.shape
        a_spec = pl.BlockSpec((k, tm), lambda i, j: (0, i))
        b_spec = pl.BlockSpec((k, tn), lambda i, j: (0, j))
        dims = ((0,), (0,))
    assert m % tm == 0 and n % tn == 0, (name, m, n, tm, tn)
    o_spec = pl.BlockSpec((tm, tn), lambda i, j: (i, j))
    in_specs = [a_spec, b_spec]
    args = [a, b]
    if extra is not None:
        in_specs.append(o_spec)
        args.append(extra)

    def body(*refs):
        a_ref, b_ref, o_ref = refs[0], refs[1], refs[-1]
        av = a_ref[...]
        if a_fn is not None:
            av = a_fn(av)
        acc = lax.dot_general(av.astype(BF16), b_ref[...].astype(BF16), (dims, ((), ())), preferred_element_type=F32)
        if epi is not None:
            acc = epi(acc, refs[2][...])
        o_ref[...] = acc.astype(out_dtype)

    out_shape = jax.ShapeDtypeStruct((m, n), out_dtype)
    aliases = None
    if dest_major:
        assert extra is None
        o_spec = pl.BlockSpec((None, tm, tn), lambda i, j: (j, i, 0))
        out_shape = jax.ShapeDtypeStruct((N_DEV, m, tn), out_dtype)
    if into is not None:
        buf, row0, total = into
        assert extra is None and row0 % tm == 0
        o_spec = pl.BlockSpec((tm, tn), lambda i, j: (row0 // tm + i, j))
        out_shape = jax.ShapeDtypeStruct((total, n), out_dtype)
        if buf is not None:
            in_specs.append(ANY_SPEC)
            args.append(buf)
            aliases = {len(args) - 1: 0}
    (out,), jouts = _hosted(body, jobs, grid=(m // tm, n // tn), in_specs=in_specs, out_specs=[o_spec],
                            out_shape=[out_shape], args=args, name=name, aliases=aliases)
    return (out, jouts) if jobs else out


def _mm_pieces(pieces, offsets, b, *, tm, name, extra, epi, jobs=()):
    m = pieces[0].shape[0]
    kb, n = b.shape
    tm = min(tm, m)
    row = lambda wdt: pl.BlockSpec((tm, wdt), lambda i: (i, 0))
    in_specs = [row(pc.shape[1]) for pc in pieces] + [pl.BlockSpec((kb, n), lambda i: (0, 0)), row(n)]
    np_ = len(pieces)

    def body(*refs):
        b_ref, e_ref, o_ref = refs[np_], refs[np_ + 1], refs[np_ + 2]
        acc = jnp.zeros((tm, n), F32)
        for q in range(np_):
            kq = pieces[q].shape[1]
            acc = acc + jnp.dot(refs[q][...].astype(BF16), b_ref[offsets[q]:offsets[q] + kq, :].astype(BF16),
                                preferred_element_type=F32)
        o_ref[...] = epi(acc, e_ref[...])

    (out,), jouts = _hosted(body, jobs, grid=(m // tm,), in_specs=in_specs, out_specs=[row(n)],
                            out_shape=[jax.ShapeDtypeStruct((m, n), F32)], args=list(pieces) + [b, extra], name=name)
    return (out, jouts) if jobs else out


def _relu2(v):
    r = jnp.maximum(v, 0.0)
    return r * r


ROW_TILE = 256


def _ln_stats(t):
    mu = jnp.mean(t, axis=-1, keepdims=True)
    xc = t - mu
    var = jnp.mean(xc * xc, axis=-1, keepdims=True)
    rstd = lax.rsqrt(var + LN_EPS)
    return xc * rstd, rstd


def _ln_bwd_rows(dy, xhat, rstd, g):
    dxh = dy * g
    m1 = jnp.mean(dxh, axis=-1, keepdims=True)
    m2 = jnp.mean(dxh * xhat, axis=-1, keepdims=True)
    return rstd * (dxh - m1 - xhat * m2)


def _mm_ln(a, b, res, g, beta, *, tm, name, a_fn=None):
    m, k = a.shape
    d = b.shape[1]
    tm = min(tm, m)
    row = pl.BlockSpec((tm, d), lambda i: (i, 0))
    par = pl.BlockSpec((1, d), lambda i: (0, 0))

    def body(a_ref, b_ref, r_ref, g_ref, be_ref, br_ref, y_ref, yb_ref):
        av = a_ref[...]
        if a_fn is not None:
            av = a_fn(av)
        acc = jnp.dot(av.astype(BF16), b_ref[...].astype(BF16), preferred_element_type=F32)
        br_ref[...] = acc
        xhat, _ = _ln_stats(ALPHA * r_ref[...] + acc)
        y = xhat * g_ref[...] + be_ref[...]
        y_ref[...] = y
        yb_ref[...] = y.astype(BF16)

    sd = jax.ShapeDtypeStruct((m, d), F32)
    return _pcall(body, grid=(m // tm,),
                  in_specs=[pl.BlockSpec((tm, k), lambda i: (i, 0)), pl.BlockSpec((k, d), lambda i: (0, 0)), row, par, par],
                  out_specs=(row, row, row), out_shape=(sd, sd, jax.ShapeDtypeStruct((m, d), BF16)), name=name,
                  compiler_params=_cparams(("parallel",)))(a, b, res, g, beta)


def _mm_ln_bwd(a, b, res, branch, g, dy0, coef0, *, tm, name, jobs=()):
    m, k = a.shape
    d = b.shape[0]
    tm = min(tm, m)
    row = pl.BlockSpec((tm, d), lambda i: (i, 0))
    par = pl.BlockSpec((1, d), lambda i: (0, 0))

    def body(a_ref, b_ref, r_ref, br_ref, g_ref, dy0_ref, dt_ref, dtb_ref, dg_ref, db_ref):
        acc = lax.dot_general(a_ref[...].astype(BF16), b_ref[...].astype(BF16), (((1,), (1,)), ((), ())),
                              preferred_element_type=F32)
        dy = coef0 * dy0_ref[...] + acc
        xhat, rstd = _ln_stats(ALPHA * r_ref[...] + br_ref[...])
        dt = _ln_bwd_rows(dy, xhat, rstd, g_ref[...])
        dt_ref[...] = dt
        dtb_ref[...] = dt.astype(BF16)

        @pl.when(pl.program_id(0) == 0)
        def _():
            dg_ref[...] = jnp.zeros_like(dg_ref)
            db_ref[...] = jnp.zeros_like(db_ref)

        dg_ref[...] += jnp.sum(dy * xhat, axis=0, keepdims=True)
        db_ref[...] += jnp.sum(dy, axis=0, keepdims=True)

    pd = jax.ShapeDtypeStruct((1, d), F32)
    outs, jouts = _hosted(
        body, jobs, grid=(m // tm,),
        in_specs=[pl.BlockSpec((tm, k), lambda i: (i, 0)), pl.BlockSpec((d, k), lambda i: (0, 0)), row, row, par, row],
        out_specs=(row, row, par, par),
        out_shape=(jax.ShapeDtypeStruct((m, d), F32), jax.ShapeDtypeStruct((m, d), BF16), pd, pd),
        args=(a, b, res, branch, g, dy0), name=name)
    return (tuple(outs), jouts) if jobs else tuple(outs)


def _head(x2, x2b, p, wg, wp, g, beta, tgt, *, name):
    s, d = x2.shape
    tile = 2 * ROW_TILE
    row = pl.BlockSpec((tile, d), lambda i: (i, 0))
    par = pl.BlockSpec((1, d), lambda i: (0, 0))
    lsp = pl.BlockSpec((1, LANE), lambda i: (0, 0))
    whole = lambda a: pl.BlockSpec(a.shape, lambda i: (0, 0))

    def body(x2_ref, x2b_ref, p_ref, wg_ref, wp_ref, g_ref, be_ref, t_ref,
             loss_ref, dgp_ref, dple_ref, dt_ref, dg_ref, db_ref):
        gate = _sigmoid(_dot(x2b_ref[...], wg_ref[...]))
        ple_v = _dot(p_ref[...], wp_ref[...])
        xhat, rstd = _ln_stats(ALPHA * x2_ref[...] + gate * ple_v)
        err = xhat * g_ref[...] + be_ref[...] - t_ref[...]
        dy = err * (1.0 / d)
        dt = _ln_bwd_rows(dy, xhat, rstd, g_ref[...])
        dt_ref[...] = dt
        dgp_ref[...] = (dt * ple_v * gate * (1.0 - gate)).astype(BF16)
        dple_ref[...] = (dt * gate).astype(BF16)

        @pl.when(pl.program_id(0) == 0)
        def _():
            loss_ref[...] = jnp.zeros_like(loss_ref)
            dg_ref[...] = jnp.zeros_like(dg_ref)
            db_ref[...] = jnp.zeros_like(db_ref)

        loss_ref[...] += 0.5 * jnp.sum(jnp.mean(err * err, axis=-1, keepdims=True))
        dg_ref[...] += jnp.sum(dy * xhat, axis=0, keepdims=True)
        db_ref[...] += jnp.sum(dy, axis=0, keepdims=True)

    sd = jax.ShapeDtypeStruct((s, d), F32)
    sb = jax.ShapeDtypeStruct((s, d), BF16)
    pd = jax.ShapeDtypeStruct((1, d), F32)
    return _pcall(body, grid=(s // tile,),
                  in_specs=[row, row, pl.BlockSpec((tile, p.shape[1]), lambda i: (i, 0)), whole(wg), whole(wp), par, par,
                            row],
                  out_specs=(lsp, row, row, row, par, par),
                  out_shape=(jax.ShapeDtypeStruct((1, LANE), F32), sb, sb, sd, pd, pd),
                  name=name, compiler_params=_cparams(("arbitrary",)))(x2, x2b, p, wg, wp, g, beta, tgt)


CONV_R = 256
PAD = SUBLANE


def _shift_down(ext, s):
    if s == 0:
        return ext[PAD:, :]
    return pltpu.roll(ext, s, 0)[PAD:, :]


def _shift_up(ext, s):
    r = ext.shape[0] - PAD
    if s == 0:
        return ext[:r, :]
    return pltpu.roll(ext, r + PAD - s, 0)[:r, :]


def _conv_rows(xpad_ref, r0, w_ref):
    ext = xpad_ref[pl.ds(r0, CONV_R + PAD), :]
    acc = _shift_down(ext, 0) * w_ref[3:4, :]
    for k in range(3):
        acc = acc + _shift_down(ext, 3 - k) * w_ref[k:k + 1, :]
    return acc, ext


def _fill_front_padded(dst_ref, src_ref, s):
    dst_ref[0:PAD, :] = jnp.zeros((PAD, dst_ref.shape[1]), F32)

    def cp(q, _):
        r0 = pl.multiple_of(q * CONV_R, CONV_R)
        dst_ref[pl.ds(pl.multiple_of(PAD + r0, PAD), CONV_R), :] = src_ref[pl.ds(r0, CONV_R), :]
        return 0

    lax.fori_loop(0, s // CONV_R, cp, 0)


def _conv_silu_fwd(proj, w8, b, *, col0, width, ct, name, jobs=()):
    s = proj.shape[0]
    nb = col0 // ct

    def body(x_ref, w_ref, b_ref, o_ref, xpad):
        _fill_front_padded(xpad, x_ref, s)

        def step(q, _):
            r0 = pl.multiple_of(q * CONV_R, CONV_R)
            acc, _e = _conv_rows(xpad, r0, w_ref)
            pre = acc + b_ref[...]
            o_ref[pl.ds(r0, CONV_R), :] = pre * _sigmoid(pre)
            return 0

        lax.fori_loop(0, s // CONV_R, step, 0)

    (out,), jouts = _hosted(
        body, jobs, grid=(width // ct,),
        in_specs=[pl.BlockSpec((s, ct), lambda j: (0, nb + j)), pl.BlockSpec((SUBLANE, ct), lambda j: (0, j)),
                  pl.BlockSpec((1, ct), lambda j: (0, j))],
        out_specs=[pl.BlockSpec((s, ct), lambda j: (0, j))],
        out_shape=[jax.ShapeDtypeStruct((s, width), F32)],
        scratch_shapes=[pltpu.VMEM((s + PAD, ct), F32)], name=name, args=(proj, w8, b))
    return (out, jouts) if jobs else out


def _conv_bwd_rows(dpad_ref, r0, w_ref):
    return _conv_bwd_ext(dpad_ref[pl.ds(r0, CONV_R + PAD), :], w_ref)


def _conv_bwd_ext(ext, w_ref):
    acc = _shift_up(ext, 0) * w_ref[3:4, :]
    for k in range(3):
        acc = acc + _shift_up(ext, 3 - k) * w_ref[k:k + 1, :]
    return acc


def _conv_silu_bwd(proj, dact, w8, b, *, col0, width, ct, name, jobs=()):
    s = proj.shape[0]
    nb = col0 // ct

    def body(x_ref, d_ref, w_ref, b_ref, dx_ref, dwb_ref, xpad, dpad):
        _fill_front_padded(xpad, x_ref, s)
        dpad[pl.ds(s, PAD), :] = jnp.zeros((PAD, ct), F32)
        dwb_ref[...] = jnp.zeros_like(dwb_ref)

        def step(q, _):
            r0 = pl.multiple_of(q * CONV_R, CONV_R)
            acc, ext = _conv_rows(xpad, r0, w_ref)
            pre = acc + b_ref[...]
            sg = _sigmoid(pre)
            dpre = d_ref[pl.ds(r0, CONV_R), :] * sg * (1.0 + pre * (1.0 - sg))
            dpad[pl.ds(r0, CONV_R), :] = dpre
            for k in range(4):
                dwb_ref[k:k + 1, :] += jnp.sum(dpre * _shift_down(ext, 3 - k), axis=0, keepdims=True)
            dwb_ref[4:5, :] += jnp.sum(dpre, axis=0, keepdims=True)
            return 0

        lax.fori_loop(0, s // CONV_R, step, 0)

        def step2(q, _):
            r0 = pl.multiple_of(q * CONV_R, CONV_R)
            dx_ref[pl.ds(r0, CONV_R), :] = _conv_bwd_rows(dpad, r0, w_ref).astype(BF16)
            return 0

        lax.fori_loop(0, s // CONV_R, step2, 0)

    colb = pl.BlockSpec((s, ct), lambda j: (0, j))
    outs, jouts = _hosted(
        body, jobs, grid=(width // ct,),
        in_specs=[pl.BlockSpec((s, ct), lambda j: (0, nb + j)), colb, pl.BlockSpec((SUBLANE, ct), lambda j: (0, j)),
                  pl.BlockSpec((1, ct), lambda j: (0, j))],
        out_specs=(colb, pl.BlockSpec((SUBLANE, ct), lambda j: (0, j))),
        out_shape=(jax.ShapeDtypeStruct((s, width), BF16), jax.ShapeDtypeStruct((SUBLANE, width), F32)),
        scratch_shapes=[pltpu.VMEM((s + PAD, ct), F32), pltpu.VMEM((s + PAD, ct), F32)], name=name,
        args=(proj, dact, w8, b))
    return (tuple(outs), jouts) if jobs else tuple(outs)


LRU_CT = 256


def _row_of(v, r):
    return jnp.sum(jnp.where(_iota((v.shape[0], 1), 0) == r, v, 0.0), axis=0, keepdims=True)


def _scan_fwd(a, u):
    r = a.shape[0]
    row = _iota((r, 1), 0)
    d = 1
    while d < r:
        valid = row >= d
        u = jnp.where(valid, a * pltpu.roll(u, d, 0) + u, u)
        a = jnp.where(valid, a * pltpu.roll(a, d, 0), a)
        d *= 2
    return a, u


def _scan_rev(b, u):
    r = b.shape[0]
    row = _iota((r, 1), 0)
    d = 1
    while d < r:
        valid = row < r - d
        u = jnp.where(valid, b * pltpu.roll(u, r - d, 0) + u, u)
        b = jnp.where(valid, b * pltpu.roll(b, r - d, 0), b)
        d *= 2
    return b, u


def _lru_chunk(xpad, r0, cw_ref, cb, wa, ba, wx, bx, sp):
    acc, ext = _conv_rows(xpad, r0, cw_ref)
    xl = acc + cb
    r = _sigmoid(_dot(xl, wa) + ba)
    i = _sigmoid(_dot(xl, wx) + bx)
    la = -LRU_C * r * sp
    a = jnp.exp(la)
    a2 = jnp.exp(2.0 * la)
    mult = jnp.sqrt(-jnp.tanh(la) * (a2 + 1.0))
    first = (r0 + _iota((CONV_R, 1), 0)) == 0
    mult = jnp.where(first, 1.0, mult)
    return ext, xl, r, i, a, a2, mult, first


def _lru_specs(s):
    ct = LRU_CT
    nb_g = COL_G // ct
    return dict(
        x=pl.BlockSpec((s, ct), lambda j: (0, j)),
        g=pl.BlockSpec((s, ct), lambda j: (0, nb_g + j)),
        col=pl.BlockSpec((s, ct), lambda j: (0, j)),
        cw=pl.BlockSpec((SUBLANE, ct), lambda j: (0, j)),
        vec=pl.BlockSpec((1, ct), lambda j: (0, j)),
        gate=pl.BlockSpec((None, ct, ct), lambda j: (j, 0, 0)),
    )


def _lru_fwd(proj, cw8, cb, wa_bd, ba, wx_bd, bx, ap, *, name, jobs=()):
    s = proj.shape[0]
    ct = LRU_CT
    sp_ = _lru_specs(s)

    def body(x_ref, g_ref, cw_ref, cb_ref, wa_ref, ba_ref, wx_ref, bx_ref, ap_ref, y_ref, h_ref, xpad):
        _fill_front_padded(xpad, x_ref, s)
        sp = _softplus(-ap_ref[...])

        def step(q, carry):
            r0 = pl.multiple_of(q * CONV_R, CONV_R)
            _e, xl, _r, i, a, _a2, mult, _f = _lru_chunk(xpad, r0, cw_ref, cb_ref[...], wa_ref[...], ba_ref[...],
                                                       wx_ref[...], bx_ref[...], sp)
            acum, ucum = _scan_fwd(a, xl * i * mult)
            h = acum * carry + ucum
            h_ref[pl.ds(r0, CONV_R), :] = h
            ge, _th = _gelu(g_ref[pl.ds(r0, CONV_R), :])
            y_ref[pl.ds(r0, CONV_R), :] = (ge * h).astype(BF16)
            return _row_of(h, CONV_R - 1)

        lax.fori_loop(0, s // CONV_R, step, jnp.zeros((1, ct), F32))

    (ymix, hs), jouts = _hosted(
        body, jobs, grid=(LRU_W // ct,),
        in_specs=[sp_["x"], sp_["g"], sp_["cw"], sp_["vec"], sp_["gate"], sp_["vec"], sp_["gate"], sp_["vec"], sp_["vec"]],
        out_specs=(sp_["col"], sp_["col"]),
        out_shape=(jax.ShapeDtypeStruct((s, LRU_W + SSD_W), BF16), jax.ShapeDtypeStruct((s, LRU_W), F32)),
        scratch_shapes=[pltpu.VMEM((s + PAD, ct), F32)],
        name=name, args=(proj, proj, cw8, cb, wa_bd, ba, wx_bd, bx, ap))
    return ((ymix, hs), jouts) if jobs else (ymix, hs)


def _lru_bwd(proj, dy, hs, cw8, cb, wa_bd, ba, wx_bd, bx, ap, *, name, jobs=()):
    s = proj.shape[0]
    ct = LRU_CT
    sp_ = _lru_specs(s)

    nq = s // CONV_R

    def body(x_ref, g_ref, dy_ref, h_ref, cw_ref, cb_ref, wa_ref, ba_ref, wx_ref, bx_ref, ap_ref,
             dx_ref, dg_ref, dcwb_ref, dwa_ref, dwx_ref, xpad, hpad):
        _fill_front_padded(xpad, x_ref, s)
        _fill_front_padded(hpad, h_ref, s)
        apv = ap_ref[...]
        sp = _softplus(-apv)
        cb_v, wa, ba_v, wx, bx_v = cb_ref[...], wa_ref[...], ba_ref[...], wx_ref[...], bx_ref[...]
        dcwb_ref[...] = jnp.zeros_like(dcwb_ref)
        dwa_ref[...] = jnp.zeros_like(dwa_ref)
        dwx_ref[...] = jnp.zeros_like(dwx_ref)

        def back(k, carry):
            g_next, a_next, dxl_next = carry
            last_row = _iota((CONV_R, 1), 0) == CONV_R - 1
            r0 = pl.multiple_of((nq - 1 - k) * CONV_R, CONV_R)
            ext, xl, r, i, a, a2, mult, first = _lru_chunk(xpad, r0, cw_ref, cb_v, wa, ba_v, wx, bx_v, sp)
            gv = g_ref[pl.ds(r0, CONV_R), :]
            dyv = dy_ref[pl.ds(r0, CONV_R), :]
            hext = hpad[pl.ds(r0, CONV_R + PAD), :]
            ge, th = _gelu(gv)
            dg_ref[pl.ds(r0, CONV_R), :] = (dyv * _shift_down(hext, 0) * _gelu_grad(gv, th)).astype(BF16)
            b = jnp.where(last_row, a_next, pltpu.roll(a, CONV_R - 1, 0))
            bcum, dcum = _scan_rev(b, dyv * ge)
            gval = dcum + bcum * g_next
            hprev = _shift_down(hext, 1)
            da = gval * hprev
            dxl = gval * i * mult
            di = gval * xl * mult
            dmult = jnp.where(first, 0.0, gval * xl * i)
            dla = da * a - dmult * a2 / mult
            dr = dla * (-LRU_C) * sp
            dcwb_ref[7:8, :] += jnp.sum(dla * (-LRU_C) * r, axis=0, keepdims=True)
            dpr = dr * r * (1.0 - r)
            dpi = di * i * (1.0 - i)
            dxl = dxl + _dot_nt(dpr, wa) + _dot_nt(dpi, wx)
            dwa_ref[...] += _dot_tn(xl, dpr)
            dwx_ref[...] += _dot_tn(xl, dpi)
            dcwb_ref[5:6, :] += jnp.sum(dpr, axis=0, keepdims=True)
            dcwb_ref[6:7, :] += jnp.sum(dpi, axis=0, keepdims=True)
            for tap in range(4):
                dcwb_ref[tap:tap + 1, :] += jnp.sum(dxl * _shift_down(ext, 3 - tap), axis=0, keepdims=True)
            dcwb_ref[4:5, :] += jnp.sum(dxl, axis=0, keepdims=True)
            dx_ref[pl.ds(r0, CONV_R), :] = _conv_bwd_ext(jnp.concatenate([dxl, dxl_next], axis=0), cw_ref).astype(BF16)
            return _row_of(gval, 0), _row_of(a, 0), dxl[:PAD, :]

        zero = jnp.zeros((1, ct), F32)
        lax.fori_loop(0, nq, back, (zero, zero, jnp.zeros((PAD, ct), F32)))
        dcwb_ref[7:8, :] = dcwb_ref[7:8, :] * (-_sigmoid(-apv))

    nt = LRU_W // ct
    outs, jouts = _hosted(
        body, jobs, grid=(nt,),
        in_specs=[sp_["x"], sp_["g"], sp_["col"], sp_["col"], sp_["cw"], sp_["vec"], sp_["gate"], sp_["vec"], sp_["gate"],
                  sp_["vec"], sp_["vec"]],
        out_specs=(sp_["col"], sp_["col"], sp_["cw"], sp_["gate"], sp_["gate"]),
        out_shape=(jax.ShapeDtypeStruct((s, LRU_W), BF16), jax.ShapeDtypeStruct((s, LRU_W), BF16),
                   jax.ShapeDtypeStruct((SUBLANE, LRU_W), F32), jax.ShapeDtypeStruct((nt, ct, ct), F32),
                   jax.ShapeDtypeStruct((nt, ct, ct), F32)),
        scratch_shapes=[pltpu.VMEM((s + PAD, ct), F32), pltpu.VMEM((s + PAD, ct), F32)],
        name=name, args=(proj, proj, dy, hs, cw8, cb, wa_bd, ba, wx_bd, bx, ap))
    return (tuple(outs), jouts) if jobs else tuple(outs)


def _split3(v):
    hi = v.astype(BF16)
    r1 = v - hi.astype(F32)
    mid = r1.astype(BF16)
    lo = (r1 - mid.astype(F32)).astype(BF16)
    return hi, mid, lo


def _dot01(m01, v):
    mb = m01.astype(BF16)
    hi, mid, lo = _split3(v)
    f = lambda part: jnp.dot(mb, part, preferred_element_type=F32)
    return f(hi) + f(mid) + f(lo)


def _dot01_r(v, m01, parts=3):
    mb = m01.astype(BF16)
    acc = None
    for part in _split3(v)[:parts]:
        t = jnp.dot(part, mb, preferred_element_type=F32)
        acc = t if acc is None else acc + t
    return acc


def _ssd_prep(dtr, bias, alog_pad):
    l = CHUNK
    lane = _iota((1, LANE), 1)
    a_head = jnp.where(lane < N_HEAD, -jnp.exp(alog_pad), 0.0)
    dt = _softplus(dtr + bias)
    tril = (_iota((l, l), 1) <= _iota((l, l), 0)).astype(F32)
    a = dt * a_head
    cs = _dot01(tril, a)
    tot = jnp.sum(a, axis=0, keepdims=True)
    return dict(a_head=a_head, dt=dt, tril=tril, cs=cs, tot=tot)


def _col(v, h):
    lane = _iota(v.shape, 1)
    return jnp.sum(jnp.where(lane == h, v, 0.0), axis=1, keepdims=True)


def _decay_mat(cs, cst_ref, h, causal):
    row = cst_ref[h:h + 1, :]
    return jnp.exp(jnp.where(causal, _col(cs, h) - row, NEG_BIG))


def _head_mask(j, rows=CHUNK):
    lane = _iota((rows, GROUP_W), 1)
    return (lane >= j * HEAD_P) & (lane < (j + 1) * HEAD_P)


def _over_heads(v, g):
    r = v.shape[0]
    out = jnp.zeros((r, GROUP_W), F32)
    for j in range(4):
        out = jnp.where(_head_mask(j, r), _col(v, 4 * g + j), out)
    return out


def _ssd_group_fwd(q, g, xs_g, bg, cg, ht_g, cst_ref, causal, dx_g):
    dtx_g, csx_g, totx_g = _over_heads(q["dt"], g), _over_heads(q["cs"], g), _over_heads(q["tot"], g)
    xdt = xs_g * dtx_g
    ex = jnp.exp(csx_g)
    cb = _dot_nt(cg, bg)
    yoff = _dot(cg, ht_g) * ex
    ydiag = jnp.zeros((CHUNK, GROUP_W), F32)
    lms = []
    for j in range(4):
        lms.append(_decay_mat(q["cs"], cst_ref, 4 * g + j, causal))
        ydiag = jnp.where(_head_mask(j), _dot(cb * lms[j], xdt), ydiag)
    y = ydiag + yoff + xs_g * dx_g
    dsx = jnp.exp(totx_g - csx_g)
    return y, dict(xdt=xdt, ex=ex, cb=cb, yoff=yoff, dsx=dsx, dtx=dtx_g, totx=totx_g, lms=lms)


def _gated_norm_fwd(y_g, z_g, w_g):
    sz = _sigmoid(z_g)
    silu = z_g * sz
    yf = y_g * silu
    rs = lax.rsqrt(jnp.mean(yf * yf, axis=1, keepdims=True) + RMS_EPS)
    yn = yf * rs
    return yn * w_g, (sz, silu, rs, yn)


def _ssd_fwd(xact, proj, ymix, bias_pad, alog_pad, dxp, normw, *, name, jobs=()):
    s = xact.shape[0]
    nc = s // CHUNK

    def body(xa_ref, dt_ref, z_ref, _ymix_ref, bias_ref, alp_ref, dx_ref, nw_ref, y_ref, hp_ref, ht, cst):
        @pl.when(pl.program_id(0) == 0)
        def _():
            ht[...] = jnp.zeros_like(ht)

        hp_ref[...] = ht[...]
        q = _ssd_prep(dt_ref[...], bias_ref[...], alp_ref[...])
        cst[...] = q["cs"].T
        causal = q["tril"] > 0.0
        for g in range(N_GROUP):
            sl = slice(g * GROUP_W, (g + 1) * GROUP_W)
            xs_g = xa_ref[:, sl]
            bg = xa_ref[:, SSD_W + g * N_STATE:SSD_W + (g + 1) * N_STATE]
            cg = xa_ref[:, SSD_W + N_GROUP * N_STATE + g * N_STATE:SSD_W + N_GROUP * N_STATE + (g + 1) * N_STATE]
            ht_g = ht[:, sl]
            y, f = _ssd_group_fwd(q, g, xs_g, bg, cg, ht_g, cst, causal, dx_ref[:, sl])
            out, _ = _gated_norm_fwd(y, z_ref[:, sl], nw_ref[:, sl])
            y_ref[:, sl] = out.astype(BF16)
            ht[:, sl] = jnp.exp(f["totx"]) * ht_g + _dot_tn(bg, f["xdt"] * f["dsx"])

    par = lambda w: pl.BlockSpec((1, w), lambda c: (0, 0))
    (ycat, hprev), jouts = _hosted(
        body, jobs, grid=(nc,),
        in_specs=[pl.BlockSpec((CHUNK, XBC), lambda c: (c, 0)),
                  pl.BlockSpec((CHUNK, LANE), lambda c: (c, COL_DT // LANE)),
                  pl.BlockSpec((CHUNK, SSD_W), lambda c: (c, COL_Z // SSD_W)),
                  ANY_SPEC, par(LANE), par(LANE), par(SSD_W), par(SSD_W)],
        out_specs=(pl.BlockSpec((CHUNK, SSD_W), lambda c: (c, LRU_W // SSD_W)),
                   pl.BlockSpec((None, N_STATE, SSD_W), lambda c: (c, 0, 0))),
        out_shape=(jax.ShapeDtypeStruct(ymix.shape, ymix.dtype), jax.ShapeDtypeStruct((nc, N_STATE, SSD_W), F32)),
        scratch_shapes=[pltpu.VMEM((N_STATE, SSD_W), F32), pltpu.VMEM((CHUNK, LANE), F32)],
        aliases={3: 0}, name=name, args=(xact, proj, proj, ymix, bias_pad, alog_pad, dxp, normw))
    return ((ycat, hprev), jouts) if jobs else (ycat, hprev)


def _ssd_bwd(xact, proj, dycat, hprev, bias_pad, alog_pad, dxp, normw, *, name, jobs=()):
    s = xact.shape[0]
    nc = s // CHUNK
    l = CHUNK

    def body(xa_ref, dt_ref, z_ref, dy_ref, hp_ref, bias_ref, alp_ref, dx_ref, nw_ref,
             dxa_ref, ddt_ref, dz_ref, dnw_ref, small_ref, dht, cst, accx, dcsx_s, ddtx_s):
        step = pl.program_id(0)

        @pl.when(step == 0)
        def _():
            dht[...] = jnp.zeros_like(dht)
            accx[...] = jnp.zeros_like(accx)
            dnw_ref[...] = jnp.zeros_like(dnw_ref)
            small_ref[...] = jnp.zeros_like(small_ref)

        dtr = dt_ref[...]
        q = _ssd_prep(dtr, bias_ref[...], alp_ref[...])
        cst[...] = q["cs"].T
        causal = q["tril"] > 0.0
        lane = _iota((l, LANE), 1)
        head_row = _iota((LANE, l), 0)
        dcs_head = jnp.zeros((l, LANE), F32)
        dcs_rows = jnp.zeros((LANE, l), F32)
        for g in range(N_GROUP):
            sl = slice(g * GROUP_W, (g + 1) * GROUP_W)
            slb = slice(SSD_W + g * N_STATE, SSD_W + (g + 1) * N_STATE)
            slc = slice(SSD_W + N_GROUP * N_STATE + g * N_STATE, SSD_W + N_GROUP * N_STATE + (g + 1) * N_STATE)
            xs_g, bg, cg = xa_ref[:, sl], xa_ref[:, slb], xa_ref[:, slc]
            ht_g = hp_ref[:, sl]
            dxp_g = dx_ref[:, sl]
            y, f = _ssd_group_fwd(q, g, xs_g, bg, cg, ht_g, cst, causal, dxp_g)
            z_g, nw_g = z_ref[:, sl], nw_ref[:, sl]
            _o, (sz, silu, rs, yn) = _gated_norm_fwd(y, z_g, nw_g)
            dout = dy_ref[:, sl]
            dnw_ref[:, sl] += jnp.sum(dout * yn, axis=0, keepdims=True)
            dyn = dout * nw_g
            dyf = rs * (dyn - yn * jnp.mean(dyn * yn, axis=1, keepdims=True))
            dy = dyf * silu
            dz_ref[:, sl] = (dyf * y * sz * (1.0 + z_g * (1.0 - sz))).astype(BF16)
            accx[0:1, sl] += jnp.sum(dy * xs_g, axis=0, keepdims=True)
            dyo = dy * f["ex"]
            dcg = _dot_nt(dyo, ht_g)
            dht_prev = _dot_tn(cg, dyo)
            dcsx = dy * f["yoff"]
            xdt = f["xdt"]
            dxdt = jnp.zeros((l, GROUP_W), F32)
            dcb = jnp.zeros((l, l), F32)
            for j in range(4):
                h = 4 * g + j
                lm = f["lms"][j]
                sc = f["cb"] * lm
                mask = _head_mask(j)
                ds_ = jnp.where(causal, _dot_nt(jnp.where(mask, dy, 0.0), xdt), 0.0)
                dxdt = jnp.where(mask, _dot_tn(sc, dy), dxdt)
                dcb = dcb + ds_ * lm
                m = ds_ * sc
                dcs_head = dcs_head + jnp.where(lane == h, jnp.sum(m, axis=1, keepdims=True), 0.0)
                dcs_rows = dcs_rows + jnp.where(head_row == h, jnp.sum(m, axis=0, keepdims=True), 0.0)
            dhn = dht[:, sl]
            etot = jnp.exp(f["totx"])
            dxd = _dot(bg, dhn)
            dbg = _dot_nt(xdt * f["dsx"], dhn)
            dxdt = dxdt + dxd * f["dsx"]
            qq = dxd * xdt * f["dsx"]
            dcsx = dcsx - qq
            dtot = jnp.sum(qq, axis=0, keepdims=True) + jnp.sum(dhn * ht_g, axis=0, keepdims=True) * etot
            dht[:, sl] = etot * dhn + dht_prev
            dcg = dcg + _dot(dcb, bg)
            dbg = dbg + _dot_tn(dcb, cg)
            dxa_ref[:, sl] = dxdt * f["dtx"] + dy * dxp_g
            dxa_ref[:, slb] = dbg
            dxa_ref[:, slc] = dcg
            dcsx_s[:, sl] = dcsx
            ddtx_s[:, sl] = dxdt * xs_g
            accx[2:3, sl] = dtot
        reduce = (jnp.right_shift(_iota((SSD_W, LANE), 0), 6) == _iota((SSD_W, LANE), 1)).astype(F32)
        triu = (_iota((l, l), 1) >= _iota((l, l), 0)).astype(F32)
        dtot = _dot01_r(accx[...], reduce)[2:3, :]
        dcs_head = dcs_head - dcs_rows.T
        da_head = _dot01(triu, dcs_head + _dot01_r(dcsx_s[...], reduce, parts=2)) + dtot
        ddt = _dot01_r(ddtx_s[...], reduce, parts=2) + da_head * q["a_head"]
        small_ref[1:2, :] += jnp.sum(da_head * q["dt"], axis=0, keepdims=True)
        ddtr = ddt * _sigmoid(dtr + bias_ref[...])
        ddt_ref[...] = ddtr.astype(BF16)
        small_ref[0:1, :] += jnp.sum(ddtr, axis=0, keepdims=True)

        @pl.when(step == nc - 1)
        def _():
            small_ref[1:2, :] = small_ref[1:2, :] * q["a_head"]
            small_ref[2:3, :] = _dot01_r(accx[...], reduce)[0:1, :]

    rev = lambda c: nc - 1 - c
    par = lambda w: pl.BlockSpec((1, w), lambda c: (0, 0))
    outs, jouts = _hosted(
        body, jobs, grid=(nc,),
        in_specs=[pl.BlockSpec((CHUNK, XBC), lambda c: (rev(c), 0)),
                  pl.BlockSpec((CHUNK, LANE), lambda c: (rev(c), COL_DT // LANE)),
                  pl.BlockSpec((CHUNK, SSD_W), lambda c: (rev(c), COL_Z // SSD_W)),
                  pl.BlockSpec((CHUNK, SSD_W), lambda c: (rev(c), 1)),
                  pl.BlockSpec((None, N_STATE, SSD_W), lambda c: (rev(c), 0, 0)),
                  par(LANE), par(LANE), par(SSD_W), par(SSD_W)],
        out_specs=(pl.BlockSpec((CHUNK, XBC), lambda c: (rev(c), 0)),
                   pl.BlockSpec((CHUNK, LANE), lambda c: (rev(c), 0)),
                   pl.BlockSpec((CHUNK, SSD_W), lambda c: (rev(c), 0)),
                   par(SSD_W), pl.BlockSpec((SUBLANE, LANE), lambda c: (0, 0))),
        out_shape=(jax.ShapeDtypeStruct((s, XBC), F32), jax.ShapeDtypeStruct((s, LANE), BF16),
                   jax.ShapeDtypeStruct((s, SSD_W), BF16), jax.ShapeDtypeStruct((1, SSD_W), F32),
                   jax.ShapeDtypeStruct((SUBLANE, LANE), F32)),
        scratch_shapes=[pltpu.VMEM((N_STATE, SSD_W), F32), pltpu.VMEM((CHUNK, LANE), F32),
                        pltpu.VMEM((SUBLANE, SSD_W), F32), pltpu.VMEM((CHUNK, SSD_W), F32),
                        pltpu.VMEM((CHUNK, SSD_W), F32)],
        name=name, args=(xact, proj, proj, dycat, hprev, bias_pad, alog_pad, dxp, normw))
    return (tuple(outs), jouts) if jobs else tuple(outs)


def _blockdiag(w):
    per = LRU_CT // HEAD_P
    w2 = w.reshape(N_HEAD // per, per, HEAD_P, HEAD_P)
    z = jnp.zeros((N_HEAD // per, HEAD_P, HEAD_P), w.dtype)
    rows = [jnp.concatenate([w2[:, i] if j == i else z for j in range(per)], axis=2) for i in range(per)]
    return jnp.concatenate(rows, axis=1)


def _unblockdiag(wbd):
    per = LRU_CT // HEAD_P
    parts = [wbd[:, i * HEAD_P:(i + 1) * HEAD_P, i * HEAD_P:(i + 1) * HEAD_P] for i in range(per)]
    return jnp.stack(parts, axis=1).reshape(N_HEAD, HEAD_P, HEAD_P)


def _pad_rows8(w):
    return jnp.concatenate([w, jnp.zeros((SUBLANE - w.shape[0], w.shape[1]), w.dtype)], axis=0)


def _pad_lane(v):
    return jnp.concatenate([v, jnp.zeros((1, LANE - v.shape[1]), v.dtype)], axis=1)


class _NoExchange:
    def ride(self, host):
        return []

    def done(self, jobs, outs, w):
        pass

    def grad(self, name, val):
        pass

    def small(self, raw):
        pass

    def pairs_now(self):
        pass


def _local_step(x, p, tgt, w, hooks=_NoExchange()):
    cw_l = _pad_rows8(w["lru_conv_w"])
    cw_s = _pad_rows8(w["ssd_conv_w"])
    wa_bd = _blockdiag(w["lru_gate_a_w"])
    wx_bd = _blockdiag(w["lru_gate_x_w"])
    ba = w["lru_gate_a_b"].reshape(1, LRU_W)
    bx = w["lru_gate_x_b"].reshape(1, LRU_W)
    bias_pad = _pad_lane(w["ssd_dt_bias"])
    alog_pad = _pad_lane(w["ssd_a_log"])
    dxp = jnp.repeat(w["ssd_d"], HEAD_P, axis=1)

    def host(fn, *a, name, **k):
        jobs = hooks.ride(name)
        res = fn(*a, name=name, jobs=jobs, **k)
        if jobs:
            res, jouts = res
            hooks.done(jobs, jouts, w)
        return res

    def grad(n, val):
        g[n] = val
        hooks.grad(n, val)

    xb = x.astype(BF16)
    proj = host(_mm, xb, w["w_in_t"], "nt", tm=2048, tn=512, name="in_proj")
    ymix, h_lru = host(_lru_fwd, proj, cw_l, w["lru_conv_b"], wa_bd, ba, wx_bd, bx, w["lru_a_param"], name="lru_fwd")
    xact = host(_conv_silu_fwd, proj, cw_s, w["ssd_conv_b"], col0=COL_XBC, width=XBC, ct=256, name="ssd_conv_fwd")
    ycat, hprev = host(_ssd_fwd, xact, proj, ymix, bias_pad, alog_pad, dxp, w["ssd_norm_w"], name="ssd_fwd")
    mix, x1, x1b = _mm_ln(ycat, w["w_out"], x, w["ln1_g"], w["ln1_b"], tm=512, name="out_proj")
    pre = host(_mm, x1b, w["w_ff1"], "nn", tm=2048, tn=512, out_dtype=BF16, name="ff1")
    ff, x2, x2b = _mm_ln(pre, w["w_ff2"], x1, w["ln2_g"], w["ln2_b"], tm=512, a_fn=_relu2, name="ff2")
    loss, dgpre, dple, dt3, dg3, db3 = _head(x2, x2b, p, w["w_ple_gate"], w["w_ple"], w["ln3_g"], w["ln3_b"], tgt,
                                             name="head")

    g = {}
    g["ln3_g"], g["ln3_b"] = dg3, db3
    grad("w_ple_gate", _mm(x2b, dgpre, "tn", tm=512, tn=1024, out_dtype=BF16, name="d_w_ple_gate"))
    grad("w_ple", _mm(p, dple, "tn", tm=256, tn=512, dest_major=True, out_dtype=BF16, name="d_w_ple"))
    dt2, dt2b, g["ln2_g"], g["ln2_b"] = host(_mm_ln_bwd, dgpre, w["w_ple_gate"], x1, ff, w["ln2_g"], dt3, ALPHA,
                                             tm=512, name="d_x2")
    grad("w_ff2", host(_mm, pre, dt2b, "tn", tm=512, tn=1024, a_fn=_relu2, out_dtype=BF16, name="d_w_ff2"))
    dpre = host(_mm, dt2b, w["w_ff2"], "nt", tm=2048, tn=512, extra=pre, out_dtype=BF16,
                epi=lambda acc, pv: acc * 2.0 * jnp.maximum(pv.astype(F32), 0.0), name="d_pre")
    grad("w_ff1", host(_mm, x1b, dpre, "tn", tm=1024, tn=512, dest_major=True, out_dtype=BF16, name="d_w_ff1"))
    dt1, dt1b, g["ln1_g"], g["ln1_b"] = host(_mm_ln_bwd, dpre, w["w_ff1"], x, mix, w["ln1_g"], dt2, ALPHA,
                                             tm=256, name="d_x1")
    grad("w_out", host(_mm, ycat, dt1b, "tn", tm=512, tn=1024, out_dtype=BF16, name="d_w_out"))
    dycat = host(_mm, dt1b, w["w_out"], "nt", tm=2048, tn=512, name="d_ycat")
    dxl, dgl, dcwb_l, dwa, dwx = host(_lru_bwd, proj, dycat, h_lru, cw_l, w["lru_conv_b"], wa_bd, ba, wx_bd, bx,
                                      w["lru_a_param"], name="lru_bwd")
    g["lru_gate_a_w"] = _unblockdiag(dwa)
    g["lru_gate_x_w"] = _unblockdiag(dwx)
    raw = dict(lru=dcwb_l, gate_a=g["lru_gate_a_w"].reshape(N_HEAD * HEAD_P, HEAD_P).astype(BF16),
               gate_x=g["lru_gate_x_w"].reshape(N_HEAD * HEAD_P, HEAD_P).astype(BF16))
    hooks.small(raw)
    dxact, ddt, dz, g["ssd_norm_w"], small = host(_ssd_bwd, xact, proj, dycat, hprev, bias_pad, alog_pad, dxp,
                                                   w["ssd_norm_w"], name="ssd_bwd")
    dxbc, dcwb_s = host(_conv_silu_bwd, proj, dxact, cw_s, w["ssd_conv_b"], col0=COL_XBC, width=XBC, ct=256,
                        name="ssd_conv_bwd")
    pieces, offsets = [dxl, dgl, dz, dxbc, ddt], [0, COL_G, COL_Z, COL_XBC, COL_DT]

    g["lru_conv_w"] = dcwb_l[0:4]
    g["lru_conv_b"] = dcwb_l[4:5]
    g["lru_gate_a_b"] = dcwb_l[5:6]
    g["lru_gate_x_b"] = dcwb_l[6:7]
    g["lru_a_param"] = dcwb_l[7:8]
    g["ssd_conv_w"] = dcwb_s[0:4]
    g["ssd_conv_b"] = dcwb_s[4:5]
    g["ssd_dt_bias"] = small[0:1, :N_HEAD]
    g["ssd_a_log"] = small[1:2, :N_HEAD]
    g["ssd_d"] = small[2:3, :N_HEAD]
    rows = jnp.concatenate([g[n] for n in ("ssd_norm_w", "ln1_g", "ln1_b", "ln2_g", "ln2_b", "ln3_g", "ln3_b")]
                           + [jnp.broadcast_to(loss[:, 0:1], (1, D_MODEL))], axis=0)
    late = dict(ssd=dcwb_s, heads=small, rows=rows)
    hooks.small(late)
    raw.update(late)
    dwt = None
    for q, (pc, off) in enumerate(zip(pieces, offsets)):
        dwt = host(_mm, pc, xb, "tn", tm=512, tn=1024, out_dtype=BF16, into=(dwt, off, D_IN),
                   name="d_w_in_%d" % q)
    grad("w_in", dwt)
    hooks.pairs_now()
    grad_x = host(_mm_pieces, pieces, offsets, w["w_in_t"], tm=256, extra=dt1, epi=lambda acc, e: acc + ALPHA * e,
                  name="d_x")
    return loss[0, 0], grad_x, g, raw


ANY_SPEC = pl.BlockSpec(memory_space=pl.ANY)


def _mesh_pos():
    return lax.axis_index("x"), lax.axis_index("y"), lax.axis_index("c")


def _remote(src, dst, send, recv, k, to):
    return pltpu.make_async_remote_copy(src_ref=src, dst_ref=dst, send_sem=send.at[k], recv_sem=recv.at[k],
                                        device_id=to, device_id_type=MESH_T)


class _Job:
    N_SEM = 9

    def __init__(self, kind, inp):
        self.kind, self.inp = kind, inp
        shape = {"gather": (N_DEV,) + inp.shape, "relay": (N_DEV,) + inp.shape, "pair": (4,) + inp.shape[1:],
                 "chip": inp.shape}[kind]
        self.out = jax.ShapeDtypeStruct(shape, inp.dtype)
        self.top = (inp.shape[0] // 2) // 16 * 16

    def _relay_copies(self, inp, out, send, recv):
        x, y, c = _mesh_pos()
        sib, xn, yn, dg = (x, y, 1 - c), (1 - x, y, c), (x, 1 - y, c), (1 - x, 1 - y, c)
        blk = lambda p, cc=None: out.at[4 * p[0] + 2 * p[1] + (p[2] if cc is None else cc)]
        top = lambda r: r.at[pl.ds(0, self.top)]
        bot = lambda r: r.at[pl.ds(self.top, self.inp.shape[0] - self.top)]
        mine = blk((x, y, c))
        plan = [
            (inp, mine, sib, blk(sib)),
            (inp, mine, xn, blk(xn)),
            (inp, mine, yn, blk(yn)),
            (top(blk(xn)), top(blk(xn)), yn, top(blk(dg))),
            (bot(blk(yn)), bot(blk(yn)), xn, bot(blk(dg))),
            (blk(xn), blk(xn), sib, blk(xn, 1 - c)),
            (blk(yn), blk(yn), sib, blk(yn, 1 - c)),
            (top(blk(dg)), top(blk(dg)), sib, top(blk(dg, 1 - c))),
            (bot(blk(dg)), bot(blk(dg)), sib, bot(blk(dg, 1 - c))),
        ]
        me = (x, y, c)
        return [(_remote(s, d, send, recv, k, to), _remote(s, land, send, recv, k, me))
                for k, (s, d, to, land) in enumerate(plan)]

    def _places(self):
        x, y, c = _mesh_pos()
        return (x, y, c), (x, y, 1 - c), [(1 - x, y), (x, 1 - y), (1 - x, 1 - y)]

    def start(self, inp, out, send, recv, loc):
        me, sibling, chips = self._places()
        x, y, c = me
        if self.kind == "relay":
            pltpu.make_async_copy(inp, out.at[4 * x + 2 * y + c], loc.at[0]).start()
            cps = self._relay_copies(inp, out, send, recv)
            for k in (0, 1, 2):
                cps[k][0].start()
        elif self.kind == "gather":
            mine = out.at[4 * x + 2 * y + c]
            pltpu.make_async_copy(inp, mine, loc.at[0]).start()
            _remote(inp, mine, send, recv, 0, sibling).start()
            for j, chip in enumerate(chips):
                _remote(inp, mine, send, recv, 1 + j, (*chip, c)).start()
        elif self.kind == "pair":
            for k in range(4):
                _remote(inp.at[2 * k + (1 - c)], out.at[k], send, recv, k, sibling).start()
        else:
            kme = 2 * x + y
            pltpu.make_async_copy(inp.at[kme], out.at[kme], loc.at[0]).start()
            for j, (tx, ty) in enumerate(chips):
                _remote(inp.at[2 * tx + ty], out.at[kme], send, recv, j, (tx, ty, c)).start()

    def mid(self, inp, out, send, recv, loc):
        if self.kind == "relay":
            cps = self._relay_copies(inp, out, send, recv)
            for k, onward in ((1, (3, 5)), (2, (4, 6))):
                cps[k][1].wait_recv()
                for q in onward:
                    cps[q][0].start()
            return
        if self.kind != "gather":
            return
        me, sibling, chips = self._places()
        c = me[2]
        for j, chip in enumerate(chips):
            landed = out.at[4 * chip[0] + 2 * chip[1] + c]
            _remote(landed, landed, send, recv, 1 + j, me).wait_recv()
            _remote(landed, landed, send, recv, 4 + j, sibling).start()

    def finish(self, inp, out, send, recv, loc):
        me, sibling, chips = self._places()
        x, y, c = me
        if self.kind == "relay":
            cps = self._relay_copies(inp, out, send, recv)
            for k, onward in ((3, 7), (4, 8)):
                cps[k][1].wait_recv()
                cps[onward][0].start()
            for k in (0, 5, 6, 7, 8):
                cps[k][1].wait_recv()
            for k in range(9):
                cps[k][0].wait_send()
            pltpu.make_async_copy(inp, out.at[4 * x + 2 * y + c], loc.at[0]).wait()
        elif self.kind == "gather":
            blk = lambda px, py, pc: out.at[4 * px + 2 * py + pc]
            mine = blk(*me)
            _remote(inp, blk(*sibling), send, recv, 0, me).wait_recv()
            for j, chip in enumerate(chips):
                _remote(inp, blk(*chip, 1 - c), send, recv, 4 + j, me).wait_recv()
            for k in range(7):
                _remote(inp, mine, send, recv, k, sibling).wait_send()
            pltpu.make_async_copy(inp, mine, loc.at[0]).wait()
        elif self.kind == "pair":
            for k in range(4):
                _remote(inp.at[2 * k + (1 - c)], out.at[k], send, recv, k, sibling).wait()
        else:
            kme = 2 * x + y
            for j, (tx, ty) in enumerate(chips):
                _remote(inp.at[kme], out.at[2 * tx + ty], send, recv, j, (tx, ty, c)).wait_recv()
            for j, (tx, ty) in enumerate(chips):
                _remote(inp.at[2 * tx + ty], out.at[kme], send, recv, j, (tx, ty, c)).wait_send()
            pltpu.make_async_copy(inp.at[kme], out.at[kme], loc.at[0]).wait()


def _job_scratch(jobs):
    sem = pltpu.SemaphoreType.DMA
    return [s for _ in jobs for s in (sem((_Job.N_SEM,)), sem((_Job.N_SEM,)), sem((1,)))]


def _run_jobs(jobs, method, jins, jouts, jsems, only=None):
    for q, job in enumerate(jobs):
        if only is None or only[q]:
            getattr(job, method)(jins[q], jouts[q], *jsems[3 * q:3 * q + 3])


def _exchange(jobs, *, name):
    n = len(jobs)

    def body(*refs):
        jins, jouts, jsems = refs[:n], refs[n:2 * n], refs[2 * n:]
        _run_jobs(jobs, "start", jins, jouts, jsems)
        _run_jobs(jobs, "mid", jins, jouts, jsems)
        _run_jobs(jobs, "finish", jins, jouts, jsems)

    return _pcall(body, in_specs=[ANY_SPEC] * n, out_specs=[ANY_SPEC] * n, out_shape=[j.out for j in jobs],
                  scratch_shapes=_job_scratch(jobs), name=name)(*[j.inp for j in jobs])


def _hosted(body, jobs, *, grid, in_specs, out_specs, out_shape, args, name, scratch_shapes=(), aliases=None):
    in_specs, out_specs, out_shape = list(in_specs), list(out_specs), list(out_shape)
    scratch_shapes = list(scratch_shapes)
    n_in, n_out, n_scr, nj = len(in_specs), len(out_specs), len(scratch_shapes), len(jobs)
    sem = ("arbitrary",) * len(grid)
    kw = dict(input_output_aliases=aliases) if aliases else {}
    if not jobs:
        res = _pcall(body, grid=grid, in_specs=in_specs, out_specs=out_specs, out_shape=out_shape,
                     scratch_shapes=scratch_shapes, name=name, compiler_params=_cparams(sem), **kw)(*args)
        return list(res), []

    def full(*refs):
        ins, jins = refs[:n_in], refs[n_in:n_in + nj]
        o0 = n_in + nj
        outs, jouts = refs[o0:o0 + n_out], refs[o0 + n_out:o0 + n_out + nj]
        s0 = o0 + n_out + nj
        scr, jsems = refs[s0:s0 + n_scr], refs[s0 + n_scr:]
        step = pl.program_id(0)
        for ax in range(1, len(grid)):
            step = step * grid[ax] + pl.program_id(ax)
        total = math.prod(grid)
        early = [job.kind == "relay" for job in jobs]
        mid_step = (3 * total) // 5
        split = any(early) and 0 < mid_step < total - 1

        @pl.when(step == 0)
        def _():
            _run_jobs(jobs, "start", jins, jouts, jsems)

        if split:
            @pl.when(step == mid_step)
            def _():
                _run_jobs(jobs, "mid", jins, jouts, jsems, only=early)

        body(*ins, *outs, *scr)

        @pl.when(step == total - 1)
        def _():
            _run_jobs(jobs, "mid", jins, jouts, jsems, only=[not e for e in early] if split else None)
            _run_jobs(jobs, "finish", jins, jouts, jsems)

    res = _pcall(full, grid=grid, in_specs=in_specs + [ANY_SPEC] * nj, out_specs=out_specs + [ANY_SPEC] * nj,
                 out_shape=out_shape + [j.out for j in jobs], scratch_shapes=scratch_shapes + _job_scratch(jobs),
                 name=name, compiler_params=_cparams(sem), **kw)(*args, *[j.inp for j in jobs])
    return list(res[:n_out]), list(res[n_out:])


def _pair_add(g8, r4, cidx, *, name):
    _, r, c = g8.shape
    tr = ROW_TILE if r % ROW_TILE == 0 else r

    def body(c_ref, g_ref, r_ref, o_ref):
        o_ref[...] = (g_ref[...].astype(F32) + r_ref[...].astype(F32)).astype(BF16)

    return _pcall(
        body,
        grid_spec=pltpu.PrefetchScalarGridSpec(
            num_scalar_prefetch=1, grid=(4, r // tr),
            in_specs=[pl.BlockSpec((None, tr, c), lambda k, i, cr: (2 * k + cr[0], i, 0)),
                      pl.BlockSpec((None, tr, c), lambda k, i, cr: (k, i, 0))],
            out_specs=pl.BlockSpec((None, tr, c), lambda k, i, cr: (k, i, 0))),
        out_shape=jax.ShapeDtypeStruct((4, r, c), BF16), name=name,
        compiler_params=_cparams(("parallel", "parallel")))(cidx, g8, r4)


def _adam_update(g, w_ref, m_ref, v_ref, g_ref, d_ref, mo_ref, vo_ref):
    c1 = 1.0 - ADAM_B1 ** ADAM_STEP
    c2 = 1.0 - ADAM_B2 ** ADAM_STEP
    m2 = ADAM_B1 * m_ref[...] + (1.0 - ADAM_B1) * g
    v2 = ADAM_B2 * v_ref[...] + (1.0 - ADAM_B2) * (g * g)
    g_ref[...] = g
    mo_ref[...] = m2
    vo_ref[...] = v2
    d_ref[...] = -ADAM_LR * ((m2 / c1) / (jnp.sqrt(v2 / c2) + ADAM_EPS) + ADAM_WD * w_ref[...])


def _adamw_rows(srcs, items, own_cols, me1, loss_row, *, name):
    ns, ni, no = len(srcs), len(items), len(own_cols)
    full = lambda a: pl.BlockSpec(a.shape, lambda i, me: (0,) * a.ndim)
    in_specs = [full(a) for a in srcs]
    args = list(srcs)
    for (si, _r0, w, _m, _v) in own_cols:
        a = srcs[si]
        in_specs.append(pl.BlockSpec((N_DEV, a.shape[1], w.shape[1]), lambda i, me: (0, 0, me[0])))
        args.append(a)
    out_specs, out_shape = [], []
    for (_si, _r0, w, m, v) in list(items) + list(own_cols):
        in_specs += [full(w)] * 3
        args += [w, m, v]
        out_specs += [full(w)] * 4
        out_shape += [jax.ShapeDtypeStruct(w.shape, F32)] * 4
    out_specs.append(pl.BlockSpec((1, LANE), lambda i, me: (0, 0)))
    out_shape.append(jax.ShapeDtypeStruct((1, LANE), F32))

    def body(me_ref, *refs):
        src_refs, own_refs = refs[:ns], refs[ns:ns + no]
        wmv = refs[ns + no:ns + no + 3 * (ni + no)]
        outs = refs[ns + no + 3 * (ni + no):]
        lsrc, lrow = src_refs[loss_row[0]], loss_row[1]
        total = lsrc[0, lrow:lrow + 1, 0:LANE]
        for d in range(1, N_DEV):
            total = total + lsrc[d, lrow:lrow + 1, 0:LANE]
        outs[-1][...] = total
        for q, (si, r0, w, _m, _v) in enumerate(list(items) + list(own_cols)):
            nr, cw = w.shape
            gref = src_refs[si] if q < ni else own_refs[q - ni]
            g = gref[0, r0:r0 + nr, 0:cw]
            for d in range(1, N_DEV):
                g = g + gref[d, r0:r0 + nr, 0:cw]
            _adam_update(g, *wmv[3 * q:3 * q + 3], *outs[4 * q:4 * q + 4])

    res = _pcall(
        body,
        grid_spec=pltpu.PrefetchScalarGridSpec(num_scalar_prefetch=1, grid=(1,), in_specs=in_specs, out_specs=out_specs),
        out_shape=out_shape, name=name, compiler_params=_cparams(("arbitrary",)))(me1, *args)
    return [tuple(res[4 * q:4 * q + 4]) for q in range(ni + no)], res[-1]


def _adamw(gsrc, w, m, v, *, name):
    k, r, c = gsrc.shape
    tr = ROW_TILE if r % ROW_TILE == 0 else r

    def body(gs_ref, w_ref, m_ref, v_ref, g_ref, d_ref, mo_ref, vo_ref):
        g = gs_ref[0].astype(F32)
        for q in range(1, k):
            g = g + gs_ref[q].astype(F32)
        _adam_update(g, w_ref, m_ref, v_ref, g_ref, d_ref, mo_ref, vo_ref)

    tc = c
    if tr == r and r > ROW_TILE and c % 256 == 0:
        tc = 256
    blk = pl.BlockSpec((tr, tc), lambda i, j: (i, j))
    sd = jax.ShapeDtypeStruct((r, c), F32)
    return _pcall(body, grid=(r // tr, c // tc),
                  in_specs=[pl.BlockSpec((k, tr, tc), lambda i, j: (0, i, j)), blk, blk, blk],
                  out_specs=(blk, blk, blk, blk), out_shape=(sd, sd, sd, sd), name=name,
                  compiler_params=_cparams(("parallel", "parallel")))(gsrc, w, m, v)


WEIGHTS = ['w_in', 'lru_conv_w', 'lru_conv_b', 'lru_gate_a_w', 'lru_gate_a_b', 'lru_gate_x_w', 'lru_gate_x_b',
           'lru_a_param', 'ssd_conv_w', 'ssd_conv_b', 'ssd_dt_bias', 'ssd_a_log', 'ssd_d', 'ssd_norm_w', 'w_out',
           'ln1_g', 'ln1_b', 'w_ff1', 'w_ff2', 'ln2_g', 'ln2_b', 'w_ple_gate', 'w_ple', 'ln3_g', 'ln3_b']
BIG = ['w_in', 'w_out', 'w_ff1', 'w_ff2', 'w_ple_gate', 'w_ple']
COL_SHARDED = ('w_ff1', 'w_ple')
CONV = ['lru_conv_w', 'ssd_conv_w']
REPL = [n for n in WEIGHTS if n not in BIG and n not in CONV]
CONV_CH = {'lru_conv_w': LRU_W, 'ssd_conv_w': XBC}


def _to_dest_major(name, gfull):
    if name in COL_SHARDED:
        r, cfull = gfull.shape
        return gfull.reshape(r, N_DEV, cfull // N_DEV).transpose(1, 0, 2)
    rfull, cdim = gfull.shape
    return gfull.reshape(N_DEV, rfull // N_DEV, cdim)


def _full_weight(name, gathered):
    if name in COL_SHARDED:
        _, r, cs = gathered.shape
        full = gathered.transpose(1, 0, 2).reshape(r, N_DEV * cs)
    else:
        _, rs, cdim = gathered.shape
        full = gathered.reshape(N_DEV * rs, cdim)
    if name == 'w_in':
        full = lax.dynamic_update_slice(jnp.zeros((D_IN_PAD, D_MODEL), full.dtype), full, (0, 0))
    return full


SMALL_SRC = ("lru", "ssd", "heads", "rows", "gate_a", "gate_x")
AG_HOSTS = {"in_proj": ("w_ff1",), "lru_fwd": ("w_ff2",), "ssd_fwd": ("w_out", "w_ple_gate", "w_ple")}
PAIR_HOSTS = ("d_x2", "d_x1", "d_ycat")
CHIP_HOSTS = {"d_pre": ("w_ple_gate", "w_ple"), "lru_bwd": ("w_ff2",), "ssd_bwd": ("w_ff1",), "ssd_conv_bwd": ("w_out",),
              "d_x": ("w_in",)}
SMALL_HOSTS = {"ssd_bwd": ("lru", "gate_a", "gate_x"), "d_w_in_3": ("ssd", "heads", "rows")}


class _Schedule:
    def __init__(self, shards, cidx):
        self.shards, self.cidx = shards, cidx
        self.pair, self.chip, self.small_jobs = [], [], []
        self.dest, self.summed, self.gathered_small = {}, {}, {}
        self.tags = []

    def ride(self, host):
        tags = []
        if host in AG_HOSTS:
            tags = [("weight", n, self.shards[n]) for n in AG_HOSTS[host]]
        elif host in PAIR_HOSTS or host in CHIP_HOSTS or host == "flush":
            tags = [("pair", n, a) for n, a in self.pair]
            self.pair = []
            if host not in PAIR_HOSTS:
                take = [t for t in self.chip if host == "flush" or t[0] in CHIP_HOSTS[host]]
                tags += [("chip", n, a) for n, a in take]
                self.chip = [t for t in self.chip if not any(t is u for u in take)]
        if host in SMALL_HOSTS:
            tags += [("small", n, a) for n, a in self.small_jobs if n in SMALL_HOSTS[host]]
            self.small_jobs = [t for t in self.small_jobs if t[0] not in SMALL_HOSTS[host]]
        self.tags = tags
        return [_Job({"weight": "relay", "small": "gather"}.get(kind, kind), a) for kind, _n, a in tags]

    def done(self, jobs, outs, w):
        for (kind, n, _a), o in zip(self.tags, outs):
            if kind == "weight":
                w[n] = _full_weight(n, o)
            elif kind == "small":
                self.gathered_small[n] = o
            elif kind == "pair":
                self.chip.append((n, _pair_add(self.dest[n], o, self.cidx, name="rs_pair_add_" + n)))
            else:
                self.summed[n] = o

    def grad(self, name, val):
        self.dest[name] = val if val.ndim == 3 else _to_dest_major(name, val)
        self.pair.append((name, self.dest[name]))

    def small(self, raw):
        self.small_jobs += list(raw.items())

    def pairs_now(self):
        tags = [("pair", n, a) for n, a in self.pair]
        self.pair, self.tags = [], tags
        jobs = [_Job("pair", a) for _k, _n, a in tags]
        self.done(jobs, _exchange(jobs, name="rs_pairs_now"), None)

    def flush(self):
        step = 0
        while self.pair or self.chip:
            jobs = self.ride("flush")
            self.done(jobs, _exchange(jobs, name="rs_flush_%d" % step), None)
            step += 1


def kernel(x, p, w_in, lru_conv_w, lru_conv_b, lru_gate_a_w, lru_gate_a_b, lru_gate_x_w, lru_gate_x_b, lru_a_param, ssd_conv_w, ssd_conv_b, ssd_dt_bias, ssd_a_log, ssd_d, ssd_norm_w, w_out, ln1_g, ln1_b, w_ff1, w_ff2, ln2_g, ln2_b, w_ple_gate, w_ple, ln3_g, ln3_b, loss_target, m_w_in, m_lru_conv_w, m_lru_conv_b, m_lru_gate_a_w, m_lru_gate_a_b, m_lru_gate_x_w, m_lru_gate_x_b, m_lru_a_param, m_ssd_conv_w, m_ssd_conv_b, m_ssd_dt_bias, m_ssd_a_log, m_ssd_d, m_ssd_norm_w, m_w_out, m_ln1_g, m_ln1_b, m_w_ff1, m_w_ff2, m_ln2_g, m_ln2_b, m_w_ple_gate, m_w_ple, m_ln3_g, m_ln3_b, v_w_in, v_lru_conv_w, v_lru_conv_b, v_lru_gate_a_w, v_lru_gate_a_b, v_lru_gate_x_w, v_lru_gate_x_b, v_lru_a_param, v_ssd_conv_w, v_ssd_conv_b, v_ssd_dt_bias, v_ssd_a_log, v_ssd_d, v_ssd_norm_w, v_w_out, v_ln1_g, v_ln1_b, v_w_ff1, v_w_ff2, v_ln2_g, v_ln2_b, v_w_ple_gate, v_w_ple, v_ln3_g, v_ln3_b):
    given = dict(locals())
    def local(a, n):
        return jnp.swapaxes(a[0], 0, 1) if n == 'w_in' else a[0]

    wsh = {n: local(given[n], n) for n in WEIGHTS}
    msh = {n: local(given["m_" + n], n) for n in WEIGHTS}
    vsh = {n: local(given["v_" + n], n) for n in WEIGHTS}
    xi, yi, ci = _mesh_pos()
    me = 4 * xi + 2 * yi + ci

    shards = {n: wsh[n].astype(BF16) for n in BIG}
    conv_pack = jnp.concatenate([_pad_rows8(wsh[n]) for n in CONV], axis=1)
    g_in, gconv = _exchange([_Job("relay", shards['w_in']), _Job("gather", conv_pack)], name="ag_first")
    full = {'w_in_t': _full_weight('w_in', g_in)}
    c0 = 0
    for n in CONV:
        cw = CONV_CH[n] // N_DEV
        full[n] = gconv[:, :4, c0:c0 + cw].transpose(1, 0, 2).reshape(4, CONV_CH[n])
        c0 += cw
    for n in REPL:
        full[n] = given[n] if given[n].ndim == 2 else wsh[n]

    sched = _Schedule(shards, jnp.reshape(ci, (1,)).astype(jnp.int32))
    loss_local, grad_x, g, raw = _local_step(x[0], p[0, 0], loss_target[0], full, sched)
    sched.flush()
    summed, gat = sched.summed, sched.gathered_small

    outs = {}
    for n in BIG:
        outs[n] = _adamw(summed[n], wsh[n], msh[n], vsh[n], name="adamw_" + n)
    for n, k in (("lru_gate_a_w", "gate_a"), ("lru_gate_x_w", "gate_x")):
        flat = lambda a: a.reshape(N_HEAD * HEAD_P, HEAD_P)
        res = _adamw(gat[k], flat(wsh[n]), flat(msh[n]), flat(vsh[n]), name="adamw_" + n)
        outs[n] = tuple(r.reshape(N_HEAD, HEAD_P, HEAD_P) for r in res)
    for n, row in (("lru_gate_a_b", 5), ("lru_gate_x_b", 6)):
        outs[n] = _adamw(gat["lru"][:, row].reshape(N_DEV, N_HEAD, HEAD_P), wsh[n], msh[n], vsh[n], name="adamw_" + n)
    row_items = [("lru_conv_b", 0, 4), ("lru_a_param", 0, 7),
                 ("ssd_conv_b", 1, 4), ("ssd_dt_bias", 2, 0), ("ssd_a_log", 2, 1), ("ssd_d", 2, 2),
                 ("ssd_norm_w", 3, 0), ("ln1_g", 3, 1), ("ln1_b", 3, 2), ("ln2_g", 3, 3), ("ln2_b", 3, 4),
                 ("ln3_g", 3, 5), ("ln3_b", 3, 6)]
    vec = lambda a: a.reshape(1, -1)
    items = [(si, r0, vec(given[n]), vec(given["m_" + n]), vec(given["v_" + n])) for n, si, r0 in row_items]
    own = [(si, 0, wsh[n], msh[n], vsh[n]) for n, si in (("lru_conv_w", 0), ("ssd_conv_w", 1))]
    me1 = jnp.reshape(me, (1,)).astype(jnp.int32)
    res, loss_row = _adamw_rows([gat[k] for k in SMALL_SRC[:4]], items, own, me1, (3, 7), name="adamw_small")
    loss = loss_row[0, 0]
    for (n, _si, _r0), r4 in zip(row_items, res[:len(row_items)]):
        outs[n] = r4
    for n, r4 in zip(CONV, res[len(row_items):]):
        outs[n] = r4

    def fin(n, k):
        a = jnp.swapaxes(outs[n][k], 0, 1) if n == 'w_in' else outs[n][k]
        return a.reshape(given[n].shape)

    return (loss, grad_x[None],
            *[fin(n, 0) for n in WEIGHTS], *[fin(n, 1) for n in WEIGHTS],
            *[fin(n, 2) for n in WEIGHTS], *[fin(n, 3) for n in WEIGHTS])
```

```python
import math

import jax
import jax.numpy as jnp
from jax import lax
from jax.experimental import pallas as pl
from jax.experimental.pallas import tpu as pltpu

F32 = jnp.float32
BF16 = jnp.bfloat16

N_DEV = 8
D_MODEL = 1024
LRU_W = 1024
SSD_W = 1024
XBC = 2048
N_HEAD = 16
HEAD_P = 64
N_GROUP = 4
GROUP_W = 256
N_STATE = 128
CHUNK = 128
D_IN = 5136
D_IN_PAD = 5632
COL_G = 1024
COL_Z = 2048
COL_XBC = 3072
COL_DT = 5120
LRU_C = 8.0
ALPHA = 2.0 ** 0.25
LN_EPS = 1e-5
RMS_EPS = 1e-5
ADAM_LR = 0.001
ADAM_B1 = 0.9
ADAM_B2 = 0.999
ADAM_EPS = 1e-08
ADAM_WD = 0.01
ADAM_STEP = 10
GELU_C = math.sqrt(2.0 / math.pi)
LANE = 128
SUBLANE = 8
VMEM_LIMIT = 48 * 1024 * 1024
MESH_T = pl.DeviceIdType.MESH
NEG_BIG = -1e30


def _pcall(body, **kw):
    return pl.pallas_call(body, **kw)


def _cparams(sem):
    return pltpu.CompilerParams(dimension_semantics=sem, vmem_limit_bytes=VMEM_LIMIT)


def _dot(a, b):
    return jnp.dot(a.astype(BF16), b.astype(BF16), preferred_element_type=F32)


def _dot_nt(a, b):
    return lax.dot_general(a.astype(BF16), b.astype(BF16), (((1,), (1,)), ((), ())), preferred_element_type=F32)


def _dot_tn(a, b):
    return lax.dot_general(a.astype(BF16), b.astype(BF16), (((0,), (0,)), ((), ())), preferred_element_type=F32)


def _sigmoid(x):
    return jax.nn.sigmoid(x)


def _softplus(v):
    return jnp.maximum(v, 0.0) + jnp.log1p(jnp.exp(-jnp.abs(v)))


def _gelu(x):
    th = jnp.tanh(GELU_C * (x + 0.044715 * x * x * x))
    return 0.5 * x * (1.0 + th), th


def _gelu_grad(x, th):
    return 0.5 * (1.0 + th) + 0.5 * x * (1.0 - th * th) * GELU_C * (1.0 + 3.0 * 0.044715 * x * x)


def _iota(shape, dim):
    return lax.broadcasted_iota(jnp.int32, shape, dim)


def _mm(a, b, mode, *, tm, tn, name, a_fn=None, extra=None, epi=None, out_dtype=F32, dest_major=False, into=None,
        jobs=()):
    m = a.shape[1] if mode == "tn" else a.shape[0]
    n = b.shape[0] if mode == "nt" else b.shape[1]
    tm, tn = min(tm, m), min(tn, n)
    if dest_major:
        tn = n // N_DEV
    if mode == "nn":
        m, k = a.shape
        _, n = b.shape
        a_spec = pl.BlockSpec((tm, k), lambda i, j: (i, 0))
        b_spec = pl.BlockSpec((k, tn), lambda i, j: (0, j))
        dims = ((1,), (0,))
    elif mode == "nt":
        m, k = a.shape
        n, _ = b.shape
        a_spec = pl.BlockSpec((tm, k), lambda i, j: (i, 0))
        b_spec = pl.BlockSpec((tn, k), lambda i, j: (j, 0))
        dims = ((1,), (1,))
    else:
        k, m = a.shape
        _, n = b.shape
        a_spec = pl.BlockSpec((k, tm), lambda i, j: (0, i))
        b_spec = pl.BlockSpec((k, tn), lambda i, j: (0, j))
        dims = ((0,), (0,))
    assert m % tm == 0 and n % tn == 0, (name, m, n, tm, tn)
    o_spec = pl.BlockSpec((tm, tn), lambda i, j: (i, j))
    in_specs = [a_spec, b_spec]
    args = [a, b]
    if extra is not None:
        in_specs.append(o_spec)
        args.append(extra)

    def body(*refs):
        a_ref, b_ref, o_ref = refs[0], refs[1], refs[-1]
        av = a_ref[...]
        if a_fn is not None:
            av = a_fn(av)
        acc = lax.dot_general(av.astype(BF16), b_ref[...].astype(BF16), (dims, ((), ())), preferred_element_type=F32)
        if epi is not None:
            acc = epi(acc, refs[2][...])
        o_ref[...] = acc.astype(out_dtype)

    out_shape = jax.ShapeDtypeStruct((m, n), out_dtype)
    aliases = None
    if dest_major:
        assert extra is None
        o_spec = pl.BlockSpec((None, tm, tn), lambda i, j: (j, i, 0))
        out_shape = jax.ShapeDtypeStruct((N_DEV, m, tn), out_dtype)
    if into is not None:
        buf, row0, total = into
        assert extra is None and row0 % tm == 0
        o_spec = pl.BlockSpec((tm, tn), lambda i, j: (row0 // tm + i, j))
        out_shape = jax.ShapeDtypeStruct((total, n), out_dtype)
        if buf is not None:
            in_specs.append(ANY_SPEC)
            args.append(buf)
            aliases = {len(args) - 1: 0}
    (out,), jouts = _hosted(body, jobs, grid=(m // tm, n // tn), in_specs=in_specs, out_specs=[o_spec],
                            out_shape=[out_shape], args=args, name=name, aliases=aliases)
    return (out, jouts) if jobs else out


def _mm_pieces(pieces, offsets, b, *, tm, name, extra, epi, jobs=()):
    m = pieces[0].shape[0]
    kb, n = b.shape
    tm = min(tm, m)
    row = lambda wdt: pl.BlockSpec((tm, wdt), lambda i: (i, 0))
    in_specs = [row(pc.shape[1]) for pc in pieces] + [pl.BlockSpec((kb, n), lambda i: (0, 0)), row(n)]
    np_ = len(pieces)

    def body(*refs):
        b_ref, e_ref, o_ref = refs[np_], refs[np_ + 1], refs[np_ + 2]
        acc = jnp.zeros((tm, n), F32)
        for q in range(np_):
            kq = pieces[q].shape[1]
            acc = acc + jnp.dot(refs[q][...].astype(BF16), b_ref[offsets[q]:offsets[q] + kq, :].astype(BF16),
                                preferred_element_type=F32)
        o_ref[...] = epi(acc, e_ref[...])

    (out,), jouts = _hosted(body, jobs, grid=(m // tm,), in_specs=in_specs, out_specs=[row(n)],
                            out_shape=[jax.ShapeDtypeStruct((m, n), F32)], args=list(pieces) + [b, extra], name=name)
    return (out, jouts) if jobs else out


def _relu2(v):
    r = jnp.maximum(v, 0.0)
    return r * r


ROW_TILE = 256


def _ln_stats(t):
    mu = jnp.mean(t, axis=-1, keepdims=True)
    xc = t - mu
    var = jnp.mean(xc * xc, axis=-1, keepdims=True)
    rstd = lax.rsqrt(var + LN_EPS)
    return xc * rstd, rstd


def _ln_bwd_rows(dy, xhat, rstd, g):
    dxh = dy * g
    m1 = jnp.mean(dxh, axis=-1, keepdims=True)
    m2 = jnp.mean(dxh * xhat, axis=-1, keepdims=True)
    return rstd * (dxh - m1 - xhat * m2)


def _mm_ln(a, b, res, g, beta, *, tm, name, a_fn=None):
    m, k = a.shape
    d = b.shape[1]
    tm = min(tm, m)
    row = pl.BlockSpec((tm, d), lambda i: (i, 0))
    par = pl.BlockSpec((1, d), lambda i: (0, 0))

    def body(a_ref, b_ref, r_ref, g_ref, be_ref, br_ref, y_ref, yb_ref):
        av = a_ref[...]
        if a_fn is not None:
            av = a_fn(av)
        acc = jnp.dot(av.astype(BF16), b_ref[...].astype(BF16), preferred_element_type=F32)
        br_ref[...] = acc
        xhat, _ = _ln_stats(ALPHA * r_ref[...] + acc)
        y = xhat * g_ref[...] + be_ref[...]
        y_ref[...] = y
        yb_ref[...] = y.astype(BF16)

    sd = jax.ShapeDtypeStruct((m, d), F32)
    return _pcall(body, grid=(m // tm,),
                  in_specs=[pl.BlockSpec((tm, k), lambda i: (i, 0)), pl.BlockSpec((k, d), lambda i: (0, 0)), row, par, par],
                  out_specs=(row, row, row), out_shape=(sd, sd, jax.ShapeDtypeStruct((m, d), BF16)), name=name,
                  compiler_params=_cparams(("parallel",)))(a, b, res, g, beta)


def _mm_ln_bwd(a, b, res, branch, g, dy0, coef0, *, tm, name, jobs=()):
    m, k = a.shape
    d = b.shape[0]
    tm = min(tm, m)
    row = pl.BlockSpec((tm, d), lambda i: (i, 0))
    par = pl.BlockSpec((1, d), lambda i: (0, 0))

    def body(a_ref, b_ref, r_ref, br_ref, g_ref, dy0_ref, dt_ref, dtb_ref, dg_ref, db_ref):
        acc = lax.dot_general(a_ref[...].astype(BF16), b_ref[...].astype(BF16), (((1,), (1,)), ((), ())),
                              preferred_element_type=F32)
        dy = coef0 * dy0_ref[...] + acc
        xhat, rstd = _ln_stats(ALPHA * r_ref[...] + br_ref[...])
        dt = _ln_bwd_rows(dy, xhat, rstd, g_ref[...])
        dt_ref[...] = dt
        dtb_ref[...] = dt.astype(BF16)

        @pl.when(pl.program_id(0) == 0)
        def _():
            dg_ref[...] = jnp.zeros_like(dg_ref)
            db_ref[...] = jnp.zeros_like(db_ref)

        dg_ref[...] += jnp.sum(dy * xhat, axis=0, keepdims=True)
        db_ref[...] += jnp.sum(dy, axis=0, keepdims=True)

    pd = jax.ShapeDtypeStruct((1, d), F32)
    outs, jouts = _hosted(
        body, jobs, grid=(m // tm,),
        in_specs=[pl.BlockSpec((tm, k), lambda i: (i, 0)), pl.BlockSpec((d, k), lambda i: (0, 0)), row, row, par, row],
        out_specs=(row, row, par, par),
        out_shape=(jax.ShapeDtypeStruct((m, d), F32), jax.ShapeDtypeStruct((m, d), BF16), pd, pd),
        args=(a, b, res, branch, g, dy0), name=name)
    return (tuple(outs), jouts) if jobs else tuple(outs)


def _head(x2, x2b, p, wg, wp, g, beta, tgt, *, name):
    s, d = x2.shape
    tile = 2 * ROW_TILE
    row = pl.BlockSpec((tile, d), lambda i: (i, 0))
    par = pl.BlockSpec((1, d), lambda i: (0, 0))
    lsp = pl.BlockSpec((1, LANE), lambda i: (0, 0))
    whole = lambda a: pl.BlockSpec(a.shape, lambda i: (0, 0))

    def body(x2_ref, x2b_ref, p_ref, wg_ref, wp_ref, g_ref, be_ref, t_ref,
             loss_ref, dgp_ref, dple_ref, dt_ref, dg_ref, db_ref):
        gate = _sigmoid(_dot(x2b_ref[...], wg_ref[...]))
        ple_v = _dot(p_ref[...], wp_ref[...])
        xhat, rstd = _ln_stats(ALPHA * x2_ref[...] + gate * ple_v)
        err = xhat * g_ref[...] + be_ref[...] - t_ref[...]
        dy = err * (1.0 / d)
        dt = _ln_bwd_rows(dy, xhat, rstd, g_ref[...])
        dt_ref[...] = dt
        dgp_ref[...] = (dt * ple_v * gate * (1.0 - gate)).astype(BF16)
        dple_ref[...] = (dt * gate).astype(BF16)

        @pl.when(pl.program_id(0) == 0)
        def _():
            loss_ref[...] = jnp.zeros_like(loss_ref)
            dg_ref[...] = jnp.zeros_like(dg_ref)
            db_ref[...] = jnp.zeros_like(db_ref)

        loss_ref[...] += 0.5 * jnp.sum(jnp.mean(err * err, axis=-1, keepdims=True))
        dg_ref[...] += jnp.sum(dy * xhat, axis=0, keepdims=True)
        db_ref[...] += jnp.sum(dy, axis=0, keepdims=True)

    sd = jax.ShapeDtypeStruct((s, d), F32)
    sb = jax.ShapeDtypeStruct((s, d), BF16)
    pd = jax.ShapeDtypeStruct((1, d), F32)
    return _pcall(body, grid=(s // tile,),
                  in_specs=[row, row, pl.BlockSpec((tile, p.shape[1]), lambda i: (i, 0)), whole(wg), whole(wp), par, par,
                            row],
                  out_specs=(lsp, row, row, row, par, par),
                  out_shape=(jax.ShapeDtypeStruct((1, LANE), F32), sb, sb, sd, pd, pd),
                  name=name, compiler_params=_cparams(("arbitrary",)))(x2, x2b, p, wg, wp, g, beta, tgt)


CONV_R = 256
PAD = SUBLANE


def _shift_down(ext, s):
    if s == 0:
        return ext[PAD:, :]
    return pltpu.roll(ext, s, 0)[PAD:, :]


def _shift_up(ext, s):
    r = ext.shape[0] - PAD
    if s == 0:
        return ext[:r, :]
    return pltpu.roll(ext, r + PAD - s, 0)[:r, :]


def _conv_rows(xpad_ref, r0, w_ref):
    ext = xpad_ref[pl.ds(r0, CONV_R + PAD), :]
    acc = _shift_down(ext, 0) * w_ref[3:4, :]
    for k in range(3):
        acc = acc + _shift_down(ext, 3 - k) * w_ref[k:k + 1, :]
    return acc, ext


def _fill_front_padded(dst_ref, src_ref, s):
    dst_ref[0:PAD, :] = jnp.zeros((PAD, dst_ref.shape[1]), F32)

    def cp(q, _):
        r0 = pl.multiple_of(q * CONV_R, CONV_R)
        dst_ref[pl.ds(pl.multiple_of(PAD + r0, PAD), CONV_R), :] = src_ref[pl.ds(r0, CONV_R), :]
        return 0

    lax.fori_loop(0, s // CONV_R, cp, 0)


def _conv_silu_fwd(proj, w8, b, *, col0, width, ct, name, jobs=()):
    s = proj.shape[0]
    nb = col0 // ct

    def body(x_ref, w_ref, b_ref, o_ref, xpad):
        _fill_front_padded(xpad, x_ref, s)

        def step(q, _):
            r0 = pl.multiple_of(q * CONV_R, CONV_R)
            acc, _e = _conv_rows(xpad, r0, w_ref)
            pre = acc + b_ref[...]
            o_ref[pl.ds(r0, CONV_R), :] = pre * _sigmoid(pre)
            return 0

        lax.fori_loop(0, s // CONV_R, step, 0)

    (out,), jouts = _hosted(
        body, jobs, grid=(width // ct,),
        in_specs=[pl.BlockSpec((s, ct), lambda j: (0, nb + j)), pl.BlockSpec((SUBLANE, ct), lambda j: (0, j)),
                  pl.BlockSpec((1, ct), lambda j: (0, j))],
        out_specs=[pl.BlockSpec((s, ct), lambda j: (0, j))],
        out_shape=[jax.ShapeDtypeStruct((s, width), F32)],
        scratch_shapes=[pltpu.VMEM((s + PAD, ct), F32)], name=name, args=(proj, w8, b))
    return (out, jouts) if jobs else out


def _conv_bwd_rows(dpad_ref, r0, w_ref):
    return _conv_bwd_ext(dpad_ref[pl.ds(r0, CONV_R + PAD), :], w_ref)


def _conv_bwd_ext(ext, w_ref):
    acc = _shift_up(ext, 0) * w_ref[3:4, :]
    for k in range(3):
        acc = acc + _shift_up(ext, 3 - k) * w_ref[k:k + 1, :]
    return acc


def _conv_silu_bwd(proj, dact, w8, b, *, col0, width, ct, name, jobs=()):
    s = proj.shape[0]
    nb = col0 // ct

    def body(x_ref, d_ref, w_ref, b_ref, dx_ref, dwb_ref, xpad, dpad):
        _fill_front_padded(xpad, x_ref, s)
        dpad[pl.ds(s, PAD), :] = jnp.zeros((PAD, ct), F32)
        dwb_ref[...] = jnp.zeros_like(dwb_ref)

        def step(q, _):
            r0 = pl.multiple_of(q * CONV_R, CONV_R)
            acc, ext = _conv_rows(xpad, r0, w_ref)
            pre = acc + b_ref[...]
            sg = _sigmoid(pre)
            dpre = d_ref[pl.ds(r0, CONV_R), :] * sg * (1.0 + pre * (1.0 - sg))
            dpad[pl.ds(r0, CONV_R), :] = dpre
            for k in range(4):
                dwb_ref[k:k + 1, :] += jnp.sum(dpre * _shift_down(ext, 3 - k), axis=0, keepdims=True)
            dwb_ref[4:5, :] += jnp.sum(dpre, axis=0, keepdims=True)
            return 0

        lax.fori_loop(0, s // CONV_R, step, 0)

        def step2(q, _):
            r0 = pl.multiple_of(q * CONV_R, CONV_R)
            dx_ref[pl.ds(r0, CONV_R), :] = _conv_bwd_rows(dpad, r0, w_ref).astype(BF16)
            return 0

        lax.fori_loop(0, s // CONV_R, step2, 0)

    colb = pl.BlockSpec((s, ct), lambda j: (0, j))
    outs, jouts = _hosted(
        body, jobs, grid=(width // ct,),
        in_specs=[pl.BlockSpec((s, ct), lambda j: (0, nb + j)), colb, pl.BlockSpec((SUBLANE, ct), lambda j: (0, j)),
                  pl.BlockSpec((1, ct), lambda j: (0, j))],
        out_specs=(colb, pl.BlockSpec((SUBLANE, ct), lambda j: (0, j))),
        out_shape=(jax.ShapeDtypeStruct((s, width), BF16), jax.ShapeDtypeStruct((SUBLANE, width), F32)),
        scratch_shapes=[pltpu.VMEM((s + PAD, ct), F32), pltpu.VMEM((s + PAD, ct), F32)], name=name,
        args=(proj, dact, w8, b))
    return (tuple(outs), jouts) if jobs else tuple(outs)


LRU_CT = 256


def _row_of(v, r):
    return jnp.sum(jnp.where(_iota((v.shape[0], 1), 0) == r, v, 0.0), axis=0, keepdims=True)


def _scan_fwd(a, u):
    r = a.shape[0]
    row = _iota((r, 1), 0)
    d = 1
    while d < r:
        valid = row >= d
        u = jnp.where(valid, a * pltpu.roll(u, d, 0) + u, u)
        a = jnp.where(valid, a * pltpu.roll(a, d, 0), a)
        d *= 2
    return a, u


def _scan_rev(b, u):
    r = b.shape[0]
    row = _iota((r, 1), 0)
    d = 1
    while d < r:
        valid = row < r - d
        u = jnp.where(valid, b * pltpu.roll(u, r - d, 0) + u, u)
        b = jnp.where(valid, b * pltpu.roll(b, r - d, 0), b)
        d *= 2
    return b, u


def _lru_chunk(xpad, r0, cw_ref, cb, wa, ba, wx, bx, sp):
    acc, ext = _conv_rows(xpad, r0, cw_ref)
    xl = acc + cb
    r = _sigmoid(_dot(xl, wa) + ba)
    i = _sigmoid(_dot(xl, wx) + bx)
    la = -LRU_C * r * sp
    a = jnp.exp(la)
    a2 = jnp.exp(2.0 * la)
    mult = jnp.sqrt(-jnp.tanh(la) * (a2 + 1.0))
    first = (r0 + _iota((CONV_R, 1), 0)) == 0
    mult = jnp.where(first, 1.0, mult)
    return ext, xl, r, i, a, a2, mult, first


def _lru_specs(s):
    ct = LRU_CT
    nb_g = COL_G // ct
    return dict(
        x=pl.BlockSpec((s, ct), lambda j: (0, j)),
        g=pl.BlockSpec((s, ct), lambda j: (0, nb_g + j)),
        col=pl.BlockSpec((s, ct), lambda j: (0, j)),
        cw=pl.BlockSpec((SUBLANE, ct), lambda j: (0, j)),
        vec=pl.BlockSpec((1, ct), lambda j: (0, j)),
        gate=pl.BlockSpec((None, ct, ct), lambda j: (j, 0, 0)),
    )


def _lru_fwd(proj, cw8, cb, wa_bd, ba, wx_bd, bx, ap, *, name, jobs=()):
    s = proj.shape[0]
    ct = LRU_CT
    sp_ = _lru_specs(s)

    def body(x_ref, g_ref, cw_ref, cb_ref, wa_ref, ba_ref, wx_ref, bx_ref, ap_ref, y_ref, h_ref, xpad):
        _fill_front_padded(xpad, x_ref, s)
        sp = _softplus(-ap_ref[...])

        def step(q, carry):
            r0 = pl.multiple_of(q * CONV_R, CONV_R)
            _e, xl, _r, i, a, _a2, mult, _f = _lru_chunk(xpad, r0, cw_ref, cb_ref[...], wa_ref[...], ba_ref[...],
                                                       wx_ref[...], bx_ref[...], sp)
            acum, ucum = _scan_fwd(a, xl * i * mult)
            h = acum * carry + ucum
            h_ref[pl.ds(r0, CONV_R), :] = h
            ge, _th = _gelu(g_ref[pl.ds(r0, CONV_R), :])
            y_ref[pl.ds(r0, CONV_R), :] = (ge * h).astype(BF16)
            return _row_of(h, CONV_R - 1)

        lax.fori_loop(0, s // CONV_R, step, jnp.zeros((1, ct), F32))

    (ymix, hs), jouts = _hosted(
        body, jobs, grid=(LRU_W // ct,),
        in_specs=[sp_["x"], sp_["g"], sp_["cw"], sp_["vec"], sp_["gate"], sp_["vec"], sp_["gate"], sp_["vec"], sp_["vec"]],
        out_specs=(sp_["col"], sp_["col"]),
        out_shape=(jax.ShapeDtypeStruct((s, LRU_W + SSD_W), BF16), jax.ShapeDtypeStruct((s, LRU_W), F32)),
        scratch_shapes=[pltpu.VMEM((s + PAD, ct), F32)],
        name=name, args=(proj, proj, cw8, cb, wa_bd, ba, wx_bd, bx, ap))
    return ((ymix, hs), jouts) if jobs else (ymix, hs)


def _lru_bwd(proj, dy, hs, cw8, cb, wa_bd, ba, wx_bd, bx, ap, *, name, jobs=()):
    s = proj.shape[0]
    ct = LRU_CT
    sp_ = _lru_specs(s)

    nq = s // CONV_R

    def body(x_ref, g_ref, dy_ref, h_ref, cw_ref, cb_ref, wa_ref, ba_ref, wx_ref, bx_ref, ap_ref,
             dx_ref, dg_ref, dcwb_ref, dwa_ref, dwx_ref, xpad, hpad):
        _fill_front_padded(xpad, x_ref, s)
        _fill_front_padded(hpad, h_ref, s)
        apv = ap_ref[...]
        sp = _softplus(-apv)
        cb_v, wa, ba_v, wx, bx_v = cb_ref[...], wa_ref[...], ba_ref[...], wx_ref[...], bx_ref[...]
        dcwb_ref[...] = jnp.zeros_like(dcwb_ref)
        dwa_ref[...] = jnp.zeros_like(dwa_ref)
        dwx_ref[...] = jnp.zeros_like(dwx_ref)

        def back(k, carry):
            g_next, a_next, dxl_next = carry
            last_row = _iota((CONV_R, 1), 0) == CONV_R - 1
            r0 = pl.multiple_of((nq - 1 - k) * CONV_R, CONV_R)
            ext, xl, r, i, a, a2, mult, first = _lru_chunk(xpad, r0, cw_ref, cb_v, wa, ba_v, wx, bx_v, sp)
            gv = g_ref[pl.ds(r0, CONV_R), :]
            dyv = dy_ref[pl.ds(r0, CONV_R), :]
            hext = hpad[pl.ds(r0, CONV_R + PAD), :]
            ge, th = _gelu(gv)
            dg_ref[pl.ds(r0, CONV_R), :] = (dyv * _shift_down(hext, 0) * _gelu_grad(gv, th)).astype(BF16)
            b = jnp.where(last_row, a_next, pltpu.roll(a, CONV_R - 1, 0))
            bcum, dcum = _scan_rev(b, dyv * ge)
            gval = dcum + bcum * g_next
            hprev = _shift_down(hext, 1)
            da = gval * hprev
            dxl = gval * i * mult
            di = gval * xl * mult
            dmult = jnp.where(first, 0.0, gval * xl * i)
            dla = da * a - dmult * a2 / mult
            dr = dla * (-LRU_C) * sp
            dcwb_ref[7:8, :] += jnp.sum(dla * (-LRU_C) * r, axis=0, keepdims=True)
            dpr = dr * r * (1.0 - r)
            dpi = di * i * (1.0 - i)
            dxl = dxl + _dot_nt(dpr, wa) + _dot_nt(dpi, wx)
            dwa_ref[...] += _dot_tn(xl, dpr)
            dwx_ref[...] += _dot_tn(xl, dpi)
            dcwb_ref[5:6, :] += jnp.sum(dpr, axis=0, keepdims=True)
            dcwb_ref[6:7, :] += jnp.sum(dpi, axis=0, keepdims=True)
            for tap in range(4):
                dcwb_ref[tap:tap + 1, :] += jnp.sum(dxl * _shift_down(ext, 3 - tap), axis=0, keepdims=True)
            dcwb_ref[4:5, :] += jnp.sum(dxl, axis=0, keepdims=True)
            dx_ref[pl.ds(r0, CONV_R), :] = _conv_bwd_ext(jnp.concatenate([dxl, dxl_next], axis=0), cw_ref).astype(BF16)
            return _row_of(gval, 0), _row_of(a, 0), dxl[:PAD, :]

        zero = jnp.zeros((1, ct), F32)
        lax.fori_loop(0, nq, back, (zero, zero, jnp.zeros((PAD, ct), F32)))
        dcwb_ref[7:8, :] = dcwb_ref[7:8, :] * (-_sigmoid(-apv))

    nt = LRU_W // ct
    outs, jouts = _hosted(
        body, jobs, grid=(nt,),
        in_specs=[sp_["x"], sp_["g"], sp_["col"], sp_["col"], sp_["cw"], sp_["vec"], sp_["gate"], sp_["vec"], sp_["gate"],
                  sp_["vec"], sp_["vec"]],
        out_specs=(sp_["col"], sp_["col"], sp_["cw"], sp_["gate"], sp_["gate"]),
        out_shape=(jax.ShapeDtypeStruct((s, LRU_W), BF16), jax.ShapeDtypeStruct((s, LRU_W), BF16),
                   jax.ShapeDtypeStruct((SUBLANE, LRU_W), F32), jax.ShapeDtypeStruct((nt, ct, ct), F32),
                   jax.ShapeDtypeStruct((nt, ct, ct), F32)),
        scratch_shapes=[pltpu.VMEM((s + PAD, ct), F32), pltpu.VMEM((s + PAD, ct), F32)],
        name=name, args=(proj, proj, dy, hs, cw8, cb, wa_bd, ba, wx_bd, bx, ap))
    return (tuple(outs), jouts) if jobs else tuple(outs)


def _split3(v):
    hi = v.astype(BF16)
    r1 = v - hi.astype(F32)
    mid = r1.astype(BF16)
    lo = (r1 - mid.astype(F32)).astype(BF16)
    return hi, mid, lo


def _dot01(m01, v):
    mb = m01.astype(BF16)
    hi, mid, lo = _split3(v)
    f = lambda part: jnp.dot(mb, part, preferred_element_type=F32)
    return f(hi) + f(mid) + f(lo)


def _dot01_r(v, m01, parts=3):
    mb = m01.astype(BF16)
    acc = None
    for part in _split3(v)[:parts]:
        t = jnp.dot(part, mb, preferred_element_type=F32)
        acc = t if acc is None else acc + t
    return acc


def _ssd_prep(dtr, bias, alog_pad):
    l = CHUNK
    lane = _iota((1, LANE), 1)
    a_head = jnp.where(lane < N_HEAD, -jnp.exp(alog_pad), 0.0)
    dt = _softplus(dtr + bias)
    tril = (_iota((l, l), 1) <= _iota((l, l), 0)).astype(F32)
    a = dt * a_head
    cs = _dot01(tril, a)
    tot = jnp.sum(a, axis=0, keepdims=True)
    return dict(a_head=a_head, dt=dt, tril=tril, cs=cs, tot=tot)


def _col(v, h):
    lane = _iota(v.shape, 1)
    return jnp.sum(jnp.where(lane == h, v, 0.0), axis=1, keepdims=True)


def _decay_mat(cs, cst_ref, h, causal):
    row = cst_ref[h:h + 1, :]
    return jnp.exp(jnp.where(causal, _col(cs, h) - row, NEG_BIG))


def _head_mask(j, rows=CHUNK):
    lane = _iota((rows, GROUP_W), 1)
    return (lane >= j * HEAD_P) & (lane < (j + 1) * HEAD_P)


def _over_heads(v, g):
    r = v.shape[0]
    out = jnp.zeros((r, GROUP_W), F32)
    for j in range(4):
        out = jnp.where(_head_mask(j, r), _col(v, 4 * g + j), out)
    return out


def _ssd_group_fwd(q, g, xs_g, bg, cg, ht_g, cst_ref, causal, dx_g):
    dtx_g, csx_g, totx_g = _over_heads(q["dt"], g), _over_heads(q["cs"], g), _over_heads(q["tot"], g)
    xdt = xs_g * dtx_g
    ex = jnp.exp(csx_g)
    cb = _dot_nt(cg, bg)
    yoff = _dot(cg, ht_g) * ex
    ydiag = jnp.zeros((CHUNK, GROUP_W), F32)
    lms = []
    for j in range(4):
        lms.append(_decay_mat(q["cs"], cst_ref, 4 * g + j, causal))
        ydiag = jnp.where(_head_mask(j), _dot(cb * lms[j], xdt), ydiag)
    y = ydiag + yoff + xs_g * dx_g
    dsx = jnp.exp(totx_g - csx_g)
    return y, dict(xdt=xdt, ex=ex, cb=cb, yoff=yoff, dsx=dsx, dtx=dtx_g, totx=totx_g, lms=lms)


def _gated_norm_fwd(y_g, z_g, w_g):
    sz = _sigmoid(z_g)
    silu = z_g * sz
    yf = y_g * silu
    rs = lax.rsqrt(jnp.mean(yf * yf, axis=1, keepdims=True) + RMS_EPS)
    yn = yf * rs
    return yn * w_g, (sz, silu, rs, yn)


def _ssd_fwd(xact, proj, ymix, bias_pad, alog_pad, dxp, normw, *, name, jobs=()):
    s = xact.shape[0]
    nc = s // CHUNK

    def body(xa_ref, dt_ref, z_ref, _ymix_ref, bias_ref, alp_ref, dx_ref, nw_ref, y_ref, hp_ref, ht, cst):
        @pl.when(pl.program_id(0) == 0)
        def _():
            ht[...] = jnp.zeros_like(ht)

        hp_ref[...] = ht[...]
        q = _ssd_prep(dt_ref[...], bias_ref[...], alp_ref[...])
        cst[...] = q["cs"].T
        causal = q["tril"] > 0.0
        for g in range(N_GROUP):
            sl = slice(g * GROUP_W, (g + 1) * GROUP_W)
            xs_g = xa_ref[:, sl]
            bg = xa_ref[:, SSD_W + g * N_STATE:SSD_W + (g + 1) * N_STATE]
            cg = xa_ref[:, SSD_W + N_GROUP * N_STATE + g * N_STATE:SSD_W + N_GROUP * N_STATE + (g + 1) * N_STATE]
            ht_g = ht[:, sl]
            y, f = _ssd_group_fwd(q, g, xs_g, bg, cg, ht_g, cst, causal, dx_ref[:, sl])
            out, _ = _gated_norm_fwd(y, z_ref[:, sl], nw_ref[:, sl])
            y_ref[:, sl] = out.astype(BF16)
            ht[:, sl] = jnp.exp(f["totx"]) * ht_g + _dot_tn(bg, f["xdt"] * f["dsx"])

    par = lambda w: pl.BlockSpec((1, w), lambda c: (0, 0))
    (ycat, hprev), jouts = _hosted(
        body, jobs, grid=(nc,),
        in_specs=[pl.BlockSpec((CHUNK, XBC), lambda c: (c, 0)),
                  pl.BlockSpec((CHUNK, LANE), lambda c: (c, COL_DT // LANE)),
                  pl.BlockSpec((CHUNK, SSD_W), lambda c: (c, COL_Z // SSD_W)),
                  ANY_SPEC, par(LANE), par(LANE), par(SSD_W), par(SSD_W)],
        out_specs=(pl.BlockSpec((CHUNK, SSD_W), lambda c: (c, LRU_W // SSD_W)),
                   pl.BlockSpec((None, N_STATE, SSD_W), lambda c: (c, 0, 0))),
        out_shape=(jax.ShapeDtypeStruct(ymix.shape, ymix.dtype), jax.ShapeDtypeStruct((nc, N_STATE, SSD_W), F32)),
        scratch_shapes=[pltpu.VMEM((N_STATE, SSD_W), F32), pltpu.VMEM((CHUNK, LANE), F32)],
        aliases={3: 0}, name=name, args=(xact, proj, proj, ymix, bias_pad, alog_pad, dxp, normw))
    return ((ycat, hprev), jouts) if jobs else (ycat, hprev)


def _ssd_bwd(xact, proj, dycat, hprev, bias_pad, alog_pad, dxp, normw, *, name, jobs=()):
    s = xact.shape[0]
    nc = s // CHUNK
    l = CHUNK

    def body(xa_ref, dt_ref, z_ref, dy_ref, hp_ref, bias_ref, alp_ref, dx_ref, nw_ref,
             dxa_ref, ddt_ref, dz_ref, dnw_ref, small_ref, dht, cst, accx, dcsx_s, ddtx_s):
        step = pl.program_id(0)

        @pl.when(step == 0)
        def _():
            dht[...] = jnp.zeros_like(dht)
            accx[...] = jnp.zeros_like(accx)
            dnw_ref[...] = jnp.zeros_like(dnw_ref)
            small_ref[...] = jnp.zeros_like(small_ref)

        dtr = dt_ref[...]
        q = _ssd_prep(dtr, bias_ref[...], alp_ref[...])
        cst[...] = q["cs"].T
        causal = q["tril"] > 0.0
        lane = _iota((l, LANE), 1)
        head_row = _iota((LANE, l), 0)
        dcs_head = jnp.zeros((l, LANE), F32)
        dcs_rows = jnp.zeros((LANE, l), F32)
        for g in range(N_GROUP):
            sl = slice(g * GROUP_W, (g + 1) * GROUP_W)
            slb = slice(SSD_W + g * N_STATE, SSD_W + (g + 1) * N_STATE)
            slc = slice(SSD_W + N_GROUP * N_STATE + g * N_STATE, SSD_W + N_GROUP * N_STATE + (g + 1) * N_STATE)
            xs_g, bg, cg = xa_ref[:, sl], xa_ref[:, slb], xa_ref[:, slc]
            ht_g = hp_ref[:, sl]
            dxp_g = dx_ref[:, sl]
            y, f = _ssd_group_fwd(q, g, xs_g, bg, cg, ht_g, cst, causal, dxp_g)
            z_g, nw_g = z_ref[:, sl], nw_ref[:, sl]
            _o, (sz, silu, rs, yn) = _gated_norm_fwd(y, z_g, nw_g)
            dout = dy_ref[:, sl]
            dnw_ref[:, sl] += jnp.sum(dout * yn, axis=0, keepdims=True)
            dyn = dout * nw_g
            dyf = rs * (dyn - yn * jnp.mean(dyn * yn, axis=1, keepdims=True))
            dy = dyf * silu
            dz_ref[:, sl] = (dyf * y * sz * (1.0 + z_g * (1.0 - sz))).astype(BF16)
            accx[0:1, sl] += jnp.sum(dy * xs_g, axis=0, keepdims=True)
            dyo = dy * f["ex"]
            dcg = _dot_nt(dyo, ht_g)
            dht_prev = _dot_tn(cg, dyo)
            dcsx = dy * f["yoff"]
            xdt = f["xdt"]
            dxdt = jnp.zeros((l, GROUP_W), F32)
            dcb = jnp.zeros((l, l), F32)
            for j in range(4):
                h = 4 * g + j
                lm = f["lms"][j]
                sc = f["cb"] * lm
                mask = _head_mask(j)
                ds_ = jnp.where(causal, _dot_nt(jnp.where(mask, dy, 0.0), xdt), 0.0)
                dxdt = jnp.where(mask, _dot_tn(sc, dy), dxdt)
                dcb = dcb + ds_ * lm
                m = ds_ * sc
                dcs_head = dcs_head + jnp.where(lane == h, jnp.sum(m, axis=1, keepdims=True), 0.0)
                dcs_rows = dcs_rows + jnp.where(head_row == h, jnp.sum(m, axis=0, keepdims=True), 0.0)
            dhn = dht[:, sl]
            etot = jnp.exp(f["totx"])
            dxd = _dot(bg, dhn)
            dbg = _dot_nt(xdt * f["dsx"], dhn)
            dxdt = dxdt + dxd * f["dsx"]
            qq = dxd * xdt * f["dsx"]
            dcsx = dcsx - qq
            dtot = jnp.sum(qq, axis=0, keepdims=True) + jnp.sum(dhn * ht_g, axis=0, keepdims=True) * etot
            dht[:, sl] = etot * dhn + dht_prev
            dcg = dcg + _dot(dcb, bg)
            dbg = dbg + _dot_tn(dcb, cg)
            dxa_ref[:, sl] = dxdt * f["dtx"] + dy * dxp_g
            dxa_ref[:, slb] = dbg
            dxa_ref[:, slc] = dcg
            dcsx_s[:, sl] = dcsx
            ddtx_s[:, sl] = dxdt * xs_g
            accx[2:3, sl] = dtot
        reduce = (jnp.right_shift(_iota((SSD_W, LANE), 0), 6) == _iota((SSD_W, LANE), 1)).astype(F32)
        triu = (_iota((l, l), 1) >= _iota((l, l), 0)).astype(F32)
        dtot = _dot01_r(accx[...], reduce)[2:3, :]
        dcs_head = dcs_head - dcs_rows.T
        da_head = _dot01(triu, dcs_head + _dot01_r(dcsx_s[...], reduce, parts=2)) + dtot
        ddt = _dot01_r(ddtx_s[...], reduce, parts=2) + da_head * q["a_head"]
        small_ref[1:2, :] += jnp.sum(da_head * q["dt"], axis=0, keepdims=True)
        ddtr = ddt * _sigmoid(dtr + bias_ref[...])
        ddt_ref[...] = ddtr.astype(BF16)
        small_ref[0:1, :] += jnp.sum(ddtr, axis=0, keepdims=True)

        @pl.when(step == nc - 1)
        def _():
            small_ref[1:2, :] = small_ref[1:2, :] * q["a_head"]
            small_ref[2:3, :] = _dot01_r(accx[...], reduce)[0:1, :]

    rev = lambda c: nc - 1 - c
    par = lambda w: pl.BlockSpec((1, w), lambda c: (0, 0))
    outs, jouts = _hosted(
        body, jobs, grid=(nc,),
        in_specs=[pl.BlockSpec((CHUNK, XBC), lambda c: (rev(c), 0)),
                  pl.BlockSpec((CHUNK, LANE), lambda c: (rev(c), COL_DT // LANE)),
                  pl.BlockSpec((CHUNK, SSD_W), lambda c: (rev(c), COL_Z // SSD_W)),
                  pl.BlockSpec((CHUNK, SSD_W), lambda c: (rev(c), 1)),
                  pl.BlockSpec((None, N_STATE, SSD_W), lambda c: (rev(c), 0, 0)),
                  par(LANE), par(LANE), par(SSD_W), par(SSD_W)],
        out_specs=(pl.BlockSpec((CHUNK, XBC), lambda c: (rev(c), 0)),
                   pl.BlockSpec((CHUNK, LANE), lambda c: (rev(c), 0)),
                   pl.BlockSpec((CHUNK, SSD_W), lambda c: (rev(c), 0)),
                   par(SSD_W), pl.BlockSpec((SUBLANE, LANE), lambda c: (0, 0))),
        out_shape=(jax.ShapeDtypeStruct((s, XBC), F32), jax.ShapeDtypeStruct((s, LANE), BF16),
                   jax.ShapeDtypeStruct((s, SSD_W), BF16), jax.ShapeDtypeStruct((1, SSD_W), F32),
                   jax.ShapeDtypeStruct((SUBLANE, LANE), F32)),
        scratch_shapes=[pltpu.VMEM((N_STATE, SSD_W), F32), pltpu.VMEM((CHUNK, LANE), F32),
                        pltpu.VMEM((SUBLANE, SSD_W), F32), pltpu.VMEM((CHUNK, SSD_W), F32),
                        pltpu.VMEM((CHUNK, SSD_W), F32)],
        name=name, args=(xact, proj, proj, dycat, hprev, bias_pad, alog_pad, dxp, normw))
    return (tuple(outs), jouts) if jobs else tuple(outs)


def _blockdiag(w):
    per = LRU_CT // HEAD_P
    w2 = w.reshape(N_HEAD // per, per, HEAD_P, HEAD_P)
    z = jnp.zeros((N_HEAD // per, HEAD_P, HEAD_P), w.dtype)
    rows = [jnp.concatenate([w2[:, i] if j == i else z for j in range(per)], axis=2) for i in range(per)]
    return jnp.concatenate(rows, axis=1)


def _unblockdiag(wbd):
    per = LRU_CT // HEAD_P
    parts = [wbd[:, i * HEAD_P:(i + 1) * HEAD_P, i * HEAD_P:(i + 1) * HEAD_P] for i in range(per)]
    return jnp.stack(parts, axis=1).reshape(N_HEAD, HEAD_P, HEAD_P)


def _pad_rows8(w):
    return jnp.concatenate([w, jnp.zeros((SUBLANE - w.shape[0], w.shape[1]), w.dtype)], axis=0)


def _pad_lane(v):
    return jnp.concatenate([v, jnp.zeros((1, LANE - v.shape[1]), v.dtype)], axis=1)


class _NoExchange:
    def ride(self, host):
        return []

    def done(self, jobs, outs, w):
        pass

    def grad(self, name, val):
        pass

    def small(self, raw):
        pass

    def pairs_now(self):
        pass


def _local_step(x, p, tgt, w, hooks=_NoExchange()):
    cw_l = _pad_rows8(w["lru_conv_w"])
    cw_s = _pad_rows8(w["ssd_conv_w"])
    wa_bd = _blockdiag(w["lru_gate_a_w"])
    wx_bd = _blockdiag(w["lru_gate_x_w"])
    ba = w["lru_gate_a_b"].reshape(1, LRU_W)
    bx = w["lru_gate_x_b"].reshape(1, LRU_W)
    bias_pad = _pad_lane(w["ssd_dt_bias"])
    alog_pad = _pad_lane(w["ssd_a_log"])
    dxp = jnp.repeat(w["ssd_d"], HEAD_P, axis=1)

    def host(fn, *a, name, **k):
        jobs = hooks.ride(name)
        res = fn(*a, name=name, jobs=jobs, **k)
        if jobs:
            res, jouts = res
            hooks.done(jobs, jouts, w)
        return res

    def grad(n, val):
        g[n] = val
        hooks.grad(n, val)

    xb = x.astype(BF16)
    proj = host(_mm, xb, w["w_in_t"], "nt", tm=2048, tn=512, name="in_proj")
    ymix, h_lru = host(_lru_fwd, proj, cw_l, w["lru_conv_b"], wa_bd, ba, wx_bd, bx, w["lru_a_param"], name="lru_fwd")
    xact = host(_conv_silu_fwd, proj, cw_s, w["ssd_conv_b"], col0=COL_XBC, width=XBC, ct=256, name="ssd_conv_fwd")
    ycat, hprev = host(_ssd_fwd, xact, proj, ymix, bias_pad, alog_pad, dxp, w["ssd_norm_w"], name="ssd_fwd")
    mix, x1, x1b = _mm_ln(ycat, w["w_out"], x, w["ln1_g"], w["ln1_b"], tm=512, name="out_proj")
    pre = host(_mm, x1b, w["w_ff1"], "nn", tm=2048, tn=512, out_dtype=BF16, name="ff1")
    ff, x2, x2b = _mm_ln(pre, w["w_ff2"], x1, w["ln2_g"], w["ln2_b"], tm=512, a_fn=_relu2, name="ff2")
    loss, dgpre, dple, dt3, dg3, db3 = _head(x2, x2b, p, w["w_ple_gate"], w["w_ple"], w["ln3_g"], w["ln3_b"], tgt,
                                             name="head")

    g = {}
    g["ln3_g"], g["ln3_b"] = dg3, db3
    grad("w_ple_gate", _mm(x2b, dgpre, "tn", tm=512, tn=1024, out_dtype=BF16, name="d_w_ple_gate"))
    grad("w_ple", _mm(p, dple, "tn", tm=256, tn=512, dest_major=True, out_dtype=BF16, name="d_w_ple"))
    dt2, dt2b, g["ln2_g"], g["ln2_b"] = host(_mm_ln_bwd, dgpre, w["w_ple_gate"], x1, ff, w["ln2_g"], dt3, ALPHA,
                                             tm=512, name="d_x2")
    grad("w_ff2", host(_mm, pre, dt2b, "tn", tm=512, tn=1024, a_fn=_relu2, out_dtype=BF16, name="d_w_ff2"))
    dpre = host(_mm, dt2b, w["w_ff2"], "nt", tm=2048, tn=512, extra=pre, out_dtype=BF16,
                epi=lambda acc, pv: acc * 2.0 * jnp.maximum(pv.astype(F32), 0.0), name="d_pre")
    grad("w_ff1", host(_mm, x1b, dpre, "tn", tm=1024, tn=512, dest_major=True, out_dtype=BF16, name="d_w_ff1"))
    dt1, dt1b, g["ln1_g"], g["ln1_b"] = host(_mm_ln_bwd, dpre, w["w_ff1"], x, mix, w["ln1_g"], dt2, ALPHA,
                                             tm=256, name="d_x1")
    grad("w_out", host(_mm, ycat, dt1b, "tn", tm=512, tn=1024, out_dtype=BF16, name="d_w_out"))
    dycat = host(_mm, dt1b, w["w_out"], "nt", tm=2048, tn=512, name="d_ycat")
    dxl, dgl, dcwb_l, dwa, dwx = host(_lru_bwd, proj, dycat, h_lru, cw_l, w["lru_conv_b"], wa_bd, ba, wx_bd, bx,
                                      w["lru_a_param"], name="lru_bwd")
    g["lru_gate_a_w"] = _unblockdiag(dwa)
    g["lru_gate_x_w"] = _unblockdiag(dwx)
    raw = dict(lru=dcwb_l, gate_a=g["lru_gate_a_w"].reshape(N_HEAD * HEAD_P, HEAD_P).astype(BF16),
               gate_x=g["lru_gate_x_w"].reshape(N_HEAD * HEAD_P, HEAD_P).astype(BF16))
    hooks.small(raw)
    dxact, ddt, dz, g["ssd_norm_w"], small = host(_ssd_bwd, xact, proj, dycat, hprev, bias_pad, alog_pad, dxp,
                                                   w["ssd_norm_w"], name="ssd_bwd")
    dxbc, dcwb_s = host(_conv_silu_bwd, proj, dxact, cw_s, w["ssd_conv_b"], col0=COL_XBC, width=XBC, ct=256,
                        name="ssd_conv_bwd")
    pieces, offsets = [dxl, dgl, dz, dxbc, ddt], [0, COL_G, COL_Z, COL_XBC, COL_DT]

    g["lru_conv_w"] = dcwb_l[0:4]
    g["lru_conv_b"] = dcwb_l[4:5]
    g["lru_gate_a_b"] = dcwb_l[5:6]
    g["lru_gate_x_b"] = dcwb_l[6:7]
    g["lru_a_param"] = dcwb_l[7:8]
    g["ssd_conv_w"] = dcwb_s[0:4]
    g["ssd_conv_b"] = dcwb_s[4:5]
    g["ssd_dt_bias"] = small[0:1, :N_HEAD]
    g["ssd_a_log"] = small[1:2, :N_HEAD]
    g["ssd_d"] = small[2:3, :N_HEAD]
    rows = jnp.concatenate([g[n] for n in ("ssd_norm_w", "ln1_g", "ln1_b", "ln2_g", "ln2_b", "ln3_g", "ln3_b")]
                           + [jnp.broadcast_to(loss[:, 0:1], (1, D_MODEL))], axis=0)
    late = dict(ssd=dcwb_s, heads=small, rows=rows)
    hooks.small(late)
    raw.update(late)
    dwt = None
    for q, (pc, off) in enumerate(zip(pieces, offsets)):
        dwt = host(_mm, pc, xb, "tn", tm=512, tn=1024, out_dtype=BF16, into=(dwt, off, D_IN),
                   name="d_w_in_%d" % q)
    grad("w_in", dwt)
    hooks.pairs_now()
    grad_x = host(_mm_pieces, pieces, offsets, w["w_in_t"], tm=256, extra=dt1, epi=lambda acc, e: acc + ALPHA * e,
                  name="d_x")
    return loss[0, 0], grad_x, g, raw


ANY_SPEC = pl.BlockSpec(memory_space=pl.ANY)


def _mesh_pos():
    return lax.axis_index("x"), lax.axis_index("y"), lax.axis_index("c")


def _remote(src, dst, send, recv, k, to):
    return pltpu.make_async_remote_copy(src_ref=src, dst_ref=dst, send_sem=send.at[k], recv_sem=recv.at[k],
                                        device_id=to, device_id_type=MESH_T)


class _Job:
    N_SEM = 9

    def __init__(self, kind, inp):
        self.kind, self.inp = kind, inp
        shape = {"gather": (N_DEV,) + inp.shape, "relay": (N_DEV,) + inp.shape, "pair": (4,) + inp.shape[1:],
                 "chip": inp.shape}[kind]
        self.out = jax.ShapeDtypeStruct(shape, inp.dtype)
        self.top = (inp.shape[0] // 2) // 16 * 16

    def _relay_copies(self, inp, out, send, recv):
        x, y, c = _mesh_pos()
        sib, xn, yn, dg = (x, y, 1 - c), (1 - x, y, c), (x, 1 - y, c), (1 - x, 1 - y, c)
        blk = lambda p, cc=None: out.at[4 * p[0] + 2 * p[1] + (p[2] if cc is None else cc)]
        top = lambda r: r.at[pl.ds(0, self.top)]
        bot = lambda r: r.at[pl.ds(self.top, self.inp.shape[0] - self.top)]
        mine = blk((x, y, c))
        plan = [
            (inp, mine, sib, blk(sib)),
            (inp, mine, xn, blk(xn)),
            (inp, mine, yn, blk(yn)),
            (top(blk(xn)), top(blk(xn)), yn, top(blk(dg))),
            (bot(blk(yn)), bot(blk(yn)), xn, bot(blk(dg))),
            (blk(xn), blk(xn), sib, blk(xn, 1 - c)),
            (blk(yn), blk(yn), sib, blk(yn, 1 - c)),
            (top(blk(dg)), top(blk(dg)), sib, top(blk(dg, 1 - c))),
            (bot(blk(dg)), bot(blk(dg)), sib, bot(blk(dg, 1 - c))),
        ]
        me = (x, y, c)
        return [(_remote(s, d, send, recv, k, to), _remote(s, land, send, recv, k, me))
                for k, (s, d, to, land) in enumerate(plan)]

    def _places(self):
        x, y, c = _mesh_pos()
        return (x, y, c), (x, y, 1 - c), [(1 - x, y), (x, 1 - y), (1 - x, 1 - y)]

    def start(self, inp, out, send, recv, loc):
        me, sibling, chips = self._places()
        x, y, c = me
        if self.kind == "relay":
            pltpu.make_async_copy(inp, out.at[4 * x + 2 * y + c], loc.at[0]).start()
            cps = self._relay_copies(inp, out, send, recv)
            for k in (0, 1, 2):
                cps[k][0].start()
        elif self.kind == "gather":
            mine = out.at[4 * x + 2 * y + c]
            pltpu.make_async_copy(inp, mine, loc.at[0]).start()
            _remote(inp, mine, send, recv, 0, sibling).start()
            for j, chip in enumerate(chips):
                _remote(inp, mine, send, recv, 1 + j, (*chip, c)).start()
        elif self.kind == "pair":
            for k in range(4):
                _remote(inp.at[2 * k + (1 - c)], out.at[k], send, recv, k, sibling).start()
        else:
            kme = 2 * x + y
            pltpu.make_async_copy(inp.at[kme], out.at[kme], loc.at[0]).start()
            for j, (tx, ty) in enumerate(chips):
                _remote(inp.at[2 * tx + ty], out.at[kme], send, recv, j, (tx, ty, c)).start()

    def mid(self, inp, out, send, recv, loc):
        if self.kind == "relay":
            cps = self._relay_copies(inp, out, send, recv)
            for k, onward in ((1, (3, 5)), (2, (4, 6))):
                cps[k][1].wait_recv()
                for q in onward:
                    cps[q][0].start()
            return
        if self.kind != "gather":
            return
        me, sibling, chips = self._places()
        c = me[2]
        for j, chip in enumerate(chips):
            landed = out.at[4 * chip[0] + 2 * chip[1] + c]
            _remote(landed, landed, send, recv, 1 + j, me).wait_recv()
            _remote(landed, landed, send, recv, 4 + j, sibling).start()

    def finish(self, inp, out, send, recv, loc):
        me, sibling, chips = self._places()
        x, y, c = me
        if self.kind == "relay":
            cps = self._relay_copies(inp, out, send, recv)
            for k, onward in ((3, 7), (4, 8)):
                cps[k][1].wait_recv()
                cps[onward][0].start()
            for k in (0, 5, 6, 7, 8):
                cps[k][1].wait_recv()
            for k in range(9):
                cps[k][0].wait_send()
            pltpu.make_async_copy(inp, out.at[4 * x + 2 * y + c], loc.at[0]).wait()
        elif self.kind == "gather":
            blk = lambda px, py, pc: out.at[4 * px + 2 * py + pc]
            mine = blk(*me)
            _remote(inp, blk(*sibling), send, recv, 0, me).wait_recv()
            for j, chip in enumerate(chips):
                _remote(inp, blk(*chip, 1 - c), send, recv, 4 + j, me).wait_recv()
            for k in range(7):
                _remote(inp, mine, send, recv, k, sibling).wait_send()
            pltpu.make_async_copy(inp, mine, loc.at[0]).wait()
        elif self.kind == "pair":
            for k in range(4):
                _remote(inp.at[2 * k + (1 - c)], out.at[k], send, recv, k, sibling).wait()
        else:
            kme = 2 * x + y
            for j, (tx, ty) in enumerate(chips):
                _remote(inp.at[kme], out.at[2 * tx + ty], send, recv, j, (tx, ty, c)).wait_recv()
            for j, (tx, ty) in enumerate(chips):
                _remote(inp.at[2 * tx + ty], out.at[kme], send, recv, j, (tx, ty, c)).wait_send()
            pltpu.make_async_copy(inp.at[kme], out.at[kme], loc.at[0]).wait()


def _job_scratch(jobs):
    sem = pltpu.SemaphoreType.DMA
    return [s for _ in jobs for s in (sem((_Job.N_SEM,)), sem((_Job.N_SEM,)), sem((1,)))]


def _run_jobs(jobs, method, jins, jouts, jsems, only=None):
    for q, job in enumerate(jobs):
        if only is None or only[q]:
            getattr(job, method)(jins[q], jouts[q], *jsems[3 * q:3 * q + 3])


def _exchange(jobs, *, name):
    n = len(jobs)

    def body(*refs):
        jins, jouts, jsems = refs[:n], refs[n:2 * n], refs[2 * n:]
        _run_jobs(jobs, "start", jins, jouts, jsems)
        _run_jobs(jobs, "mid", jins, jouts, jsems)
        _run_jobs(jobs, "finish", jins, jouts, jsems)

    return _pcall(body, in_specs=[ANY_SPEC] * n, out_specs=[ANY_SPEC] * n, out_shape=[j.out for j in jobs],
                  scratch_shapes=_job_scratch(jobs), name=name)(*[j.inp for j in jobs])


def _hosted(body, jobs, *, grid, in_specs, out_specs, out_shape, args, name, scratch_shapes=(), aliases=None):
    in_specs, out_specs, out_shape = list(in_specs), list(out_specs), list(out_shape)
    scratch_shapes = list(scratch_shapes)
    n_in, n_out, n_scr, nj = len(in_specs), len(out_specs), len(scratch_shapes), len(jobs)
    sem = ("arbitrary",) * len(grid)
    kw = dict(input_output_aliases=aliases) if aliases else {}
    if not jobs:
        res = _pcall(body, grid=grid, in_specs=in_specs, out_specs=out_specs, out_shape=out_shape,
                     scratch_shapes=scratch_shapes, name=name, compiler_params=_cparams(sem), **kw)(*args)
        return list(res), []

    def full(*refs):
        ins, jins = refs[:n_in], refs[n_in:n_in + nj]
        o0 = n_in + nj
        outs, jouts = refs[o0:o0 + n_out], refs[o0 + n_out:o0 + n_out + nj]
        s0 = o0 + n_out + nj
        scr, jsems = refs[s0:s0 + n_scr], refs[s0 + n_scr:]
        step = pl.program_id(0)
        for ax in range(1, len(grid)):
            step = step * grid[ax] + pl.program_id(ax)
        total = math.prod(grid)
        early = [job.kind == "relay" for job in jobs]
        mid_step = (3 * total) // 5
        split = any(early) and 0 < mid_step < total - 1

        @pl.when(step == 0)
        def _():
            _run_jobs(jobs, "start", jins, jouts, jsems)

        if split:
            @pl.when(step == mid_step)
            def _():
                _run_jobs(jobs, "mid", jins, jouts, jsems, only=early)

        body(*ins, *outs, *scr)

        @pl.when(step == total - 1)
        def _():
            _run_jobs(jobs, "mid", jins, jouts, jsems, only=[not e for e in early] if split else None)
            _run_jobs(jobs, "finish", jins, jouts, jsems)

    res = _pcall(full, grid=grid, in_specs=in_specs + [ANY_SPEC] * nj, out_specs=out_specs + [ANY_SPEC] * nj,
                 out_shape=out_shape + [j.out for j in jobs], scratch_shapes=scratch_shapes + _job_scratch(jobs),
                 name=name, compiler_params=_cparams(sem), **kw)(*args, *[j.inp for j in jobs])
    return list(res[:n_out]), list(res[n_out:])


def _pair_add(g8, r4, cidx, *, name):
    _, r, c = g8.shape
    tr = ROW_TILE if r % ROW_TILE == 0 else r

    def body(c_ref, g_ref, r_ref, o_ref):
        o_ref[...] = (g_ref[...].astype(F32) + r_ref[...].astype(F32)).astype(BF16)

    return _pcall(
        body,
        grid_spec=pltpu.PrefetchScalarGridSpec(
            num_scalar_prefetch=1, grid=(4, r // tr),
            in_specs=[pl.BlockSpec((None, tr, c), lambda k, i, cr: (2 * k + cr[0], i, 0)),
                      pl.BlockSpec((None, tr, c), lambda k, i, cr: (k, i, 0))],
            out_specs=pl.BlockSpec((None, tr, c), lambda k, i, cr: (k, i, 0))),
        out_shape=jax.ShapeDtypeStruct((4, r, c), BF16), name=name,
        compiler_params=_cparams(("parallel", "parallel")))(cidx, g8, r4)


def _adam_update(g, w_ref, m_ref, v_ref, g_ref, d_ref, mo_ref, vo_ref):
    c1 = 1.0 - ADAM_B1 ** ADAM_STEP
    c2 = 1.0 - ADAM_B2 ** ADAM_STEP
    m2 = ADAM_B1 * m_ref[...] + (1.0 - ADAM_B1) * g
    v2 = ADAM_B2 * v_ref[...] + (1.0 - ADAM_B2) * (g * g)
    g_ref[...] = g
    mo_ref[...] = m2
    vo_ref[...] = v2
    d_ref[...] = -ADAM_LR * ((m2 / c1) / (jnp.sqrt(v2 / c2) + ADAM_EPS) + ADAM_WD * w_ref[...])


def _adamw_rows(srcs, items, own_cols, me1, loss_row, *, name):
    ns, ni, no = len(srcs), len(items), len(own_cols)
    full = lambda a: pl.BlockSpec(a.shape, lambda i, me: (0,) * a.ndim)
    in_specs = [full(a) for a in srcs]
    args = list(srcs)
    for (si, _r0, w, _m, _v) in own_cols:
        a = srcs[si]
        in_specs.append(pl.BlockSpec((N_DEV, a.shape[1], w.shape[1]), lambda i, me: (0, 0, me[0])))
        args.append(a)
    out_specs, out_shape = [], []
    for (_si, _r0, w, m, v) in list(items) + list(own_cols):
        in_specs += [full(w)] * 3
        args += [w, m, v]
        out_specs += [full(w)] * 4
        out_shape += [jax.ShapeDtypeStruct(w.shape, F32)] * 4
    out_specs.append(pl.BlockSpec((1, LANE), lambda i, me: (0, 0)))
    out_shape.append(jax.ShapeDtypeStruct((1, LANE), F32))

    def body(me_ref, *refs):
        src_refs, own_refs = refs[:ns], refs[ns:ns + no]
        wmv = refs[ns + no:ns + no + 3 * (ni + no)]
        outs = refs[ns + no + 3 * (ni + no):]
        lsrc, lrow = src_refs[loss_row[0]], loss_row[1]
        total = lsrc[0, lrow:lrow + 1, 0:LANE]
        for d in range(1, N_DEV):
            total = total + lsrc[d, lrow:lrow + 1, 0:LANE]
        outs[-1][...] = total
        for q, (si, r0, w, _m, _v) in enumerate(list(items) + list(own_cols)):
            nr, cw = w.shape
            gref = src_refs[si] if q < ni else own_refs[q - ni]
            g = gref[0, r0:r0 + nr, 0:cw]
            for d in range(1, N_DEV):
                g = g + gref[d, r0:r0 + nr, 0:cw]
            _adam_update(g, *wmv[3 * q:3 * q + 3], *outs[4 * q:4 * q + 4])

    res = _pcall(
        body,
        grid_spec=pltpu.PrefetchScalarGridSpec(num_scalar_prefetch=1, grid=(1,), in_specs=in_specs, out_specs=out_specs),
        out_shape=out_shape, name=name, compiler_params=_cparams(("arbitrary",)))(me1, *args)
    return [tuple(res[4 * q:4 * q + 4]) for q in range(ni + no)], res[-1]


def _adamw(gsrc, w, m, v, *, name):
    k, r, c = gsrc.shape
    tr = ROW_TILE if r % ROW_TILE == 0 else r

    def body(gs_ref, w_ref, m_ref, v_ref, g_ref, d_ref, mo_ref, vo_ref):
        g = gs_ref[0].astype(F32)
        for q in range(1, k):
            g = g + gs_ref[q].astype(F32)
        _adam_update(g, w_ref, m_ref, v_ref, g_ref, d_ref, mo_ref, vo_ref)

    tc = c
    if tr == r and r > ROW_TILE and c % 256 == 0:
        tc = 256
    blk = pl.BlockSpec((tr, tc), lambda i, j: (i, j))
    sd = jax.ShapeDtypeStruct((r, c), F32)
    return _pcall(body, grid=(r // tr, c // tc),
                  in_specs=[pl.BlockSpec((k, tr, tc), lambda i, j: (0, i, j)), blk, blk, blk],
                  out_specs=(blk, blk, blk, blk), out_shape=(sd, sd, sd, sd), name=name,
                  compiler_params=_cparams(("parallel", "parallel")))(gsrc, w, m, v)


WEIGHTS = ['w_in', 'lru_conv_w', 'lru_conv_b', 'lru_gate_a_w', 'lru_gate_a_b', 'lru_gate_x_w', 'lru_gate_x_b',
           'lru_a_param', 'ssd_conv_w', 'ssd_conv_b', 'ssd_dt_bias', 'ssd_a_log', 'ssd_d', 'ssd_norm_w', 'w_out',
           'ln1_g', 'ln1_b', 'w_ff1', 'w_ff2', 'ln2_g', 'ln2_b', 'w_ple_gate', 'w_ple', 'ln3_g', 'ln3_b']
BIG = ['w_in', 'w_out', 'w_ff1', 'w_ff2', 'w_ple_gate', 'w_ple']
COL_SHARDED = ('w_ff1', 'w_ple')
CONV = ['lru_conv_w', 'ssd_conv_w']
REPL = [n for n in WEIGHTS if n not in BIG and n not in CONV]
CONV_CH = {'lru_conv_w': LRU_W, 'ssd_conv_w': XBC}


def _to_dest_major(name, gfull):
    if name in COL_SHARDED:
        r, cfull = gfull.shape
        return gfull.reshape(r, N_DEV, cfull // N_DEV).transpose(1, 0, 2)
    rfull, cdim = gfull.shape
    return gfull.reshape(N_DEV, rfull // N_DEV, cdim)


def _full_weight(name, gathered):
    if name in COL_SHARDED:
        _, r, cs = gathered.shape
        full = gathered.transpose(1, 0, 2).reshape(r, N_DEV * cs)
    else:
        _, rs, cdim = gathered.shape
        full = gathered.reshape(N_DEV * rs, cdim)
    if name == 'w_in':
        full = lax.dynamic_update_slice(jnp.zeros((D_IN_PAD, D_MODEL), full.dtype), full, (0, 0))
    return full


SMALL_SRC = ("lru", "ssd", "heads", "rows", "gate_a", "gate_x")
AG_HOSTS = {"in_proj": ("w_ff1",), "lru_fwd": ("w_out", "w_ple_gate", "w_ple"), "ssd_fwd": ("w_ff2",)}
PAIR_HOSTS = ("d_x2", "d_x1", "d_ycat")
CHIP_HOSTS = {"d_pre": ("w_ple_gate", "w_ple"), "lru_bwd": ("w_ff2",), "ssd_bwd": ("w_ff1",), "ssd_conv_bwd": ("w_out",),
              "d_x": ("w_in",)}
SMALL_HOSTS = {"ssd_bwd": ("lru", "gate_a", "gate_x"), "d_w_in_3": ("ssd", "heads", "rows")}


class _Schedule:
    def __init__(self, shards, cidx):
        self.shards, self.cidx = shards, cidx
        self.pair, self.chip, self.small_jobs = [], [], []
        self.dest, self.summed, self.gathered_small = {}, {}, {}
        self.tags = []

    def ride(self, host):
        tags = []
        if host in AG_HOSTS:
            tags = [("weight", n, self.shards[n]) for n in AG_HOSTS[host]]
        elif host in PAIR_HOSTS or host in CHIP_HOSTS or host == "flush":
            tags = [("pair", n, a) for n, a in self.pair]
            self.pair = []
            if host not in PAIR_HOSTS:
                take = [t for t in self.chip if host == "flush" or t[0] in CHIP_HOSTS[host]]
                tags += [("chip", n, a) for n, a in take]
                self.chip = [t for t in self.chip if not any(t is u for u in take)]
        if host in SMALL_HOSTS:
            tags += [("small", n, a) for n, a in self.small_jobs if n in SMALL_HOSTS[host]]
            self.small_jobs = [t for t in self.small_jobs if t[0] not in SMALL_HOSTS[host]]
        self.tags = tags
        return [_Job({"weight": "relay", "small": "gather"}.get(kind, kind), a) for kind, _n, a in tags]

    def done(self, jobs, outs, w):
        for (kind, n, _a), o in zip(self.tags, outs):
            if kind == "weight":
                w[n] = _full_weight(n, o)
            elif kind == "small":
                self.gathered_small[n] = o
            elif kind == "pair":
                self.chip.append((n, _pair_add(self.dest[n], o, self.cidx, name="rs_pair_add_" + n)))
            else:
                self.summed[n] = o

    def grad(self, name, val):
        self.dest[name] = val if val.ndim == 3 else _to_dest_major(name, val)
        self.pair.append((name, self.dest[name]))

    def small(self, raw):
        self.small_jobs += list(raw.items())

    def pairs_now(self):
        tags = [("pair", n, a) for n, a in self.pair]
        self.pair, self.tags = [], tags
        jobs = [_Job("pair", a) for _k, _n, a in tags]
        self.done(jobs, _exchange(jobs, name="rs_pairs_now"), None)

    def flush(self):
        step = 0
        while self.pair or self.chip:
            jobs = self.ride("flush")
            self.done(jobs, _exchange(jobs, name="rs_flush_%d" % step), None)
            step += 1


def kernel(x, p, w_in, lru_conv_w, lru_conv_b, lru_gate_a_w, lru_gate_a_b, lru_gate_x_w, lru_gate_x_b, lru_a_param, ssd_conv_w, ssd_conv_b, ssd_dt_bias, ssd_a_log, ssd_d, ssd_norm_w, w_out, ln1_g, ln1_b, w_ff1, w_ff2, ln2_g, ln2_b, w_ple_gate, w_ple, ln3_g, ln3_b, loss_target, m_w_in, m_lru_conv_w, m_lru_conv_b, m_lru_gate_a_w, m_lru_gate_a_b, m_lru_gate_x_w, m_lru_gate_x_b, m_lru_a_param, m_ssd_conv_w, m_ssd_conv_b, m_ssd_dt_bias, m_ssd_a_log, m_ssd_d, m_ssd_norm_w, m_w_out, m_ln1_g, m_ln1_b, m_w_ff1, m_w_ff2, m_ln2_g, m_ln2_b, m_w_ple_gate, m_w_ple, m_ln3_g, m_ln3_b, v_w_in, v_lru_conv_w, v_lru_conv_b, v_lru_gate_a_w, v_lru_gate_a_b, v_lru_gate_x_w, v_lru_gate_x_b, v_lru_a_param, v_ssd_conv_w, v_ssd_conv_b, v_ssd_dt_bias, v_ssd_a_log, v_ssd_d, v_ssd_norm_w, v_w_out, v_ln1_g, v_ln1_b, v_w_ff1, v_w_ff2, v_ln2_g, v_ln2_b, v_w_ple_gate, v_w_ple, v_ln3_g, v_ln3_b):
    given = dict(locals())
    def local(a, n):
        return jnp.swapaxes(a[0], 0, 1) if n == 'w_in' else a[0]

    wsh = {n: local(given[n], n) for n in WEIGHTS}
    msh = {n: local(given["m_" + n], n) for n in WEIGHTS}
    vsh = {n: local(given["v_" + n], n) for n in WEIGHTS}
    xi, yi, ci = _mesh_pos()
    me = 4 * xi + 2 * yi + ci

    shards = {n: wsh[n].astype(BF16) for n in BIG}
    conv_pack = jnp.concatenate([_pad_rows8(wsh[n]) for n in CONV], axis=1)
    g_in, gconv = _exchange([_Job("relay", shards['w_in']), _Job("gather", conv_pack)], name="ag_first")
    full = {'w_in_t': _full_weight('w_in', g_in)}
    c0 = 0
    for n in CONV:
        cw = CONV_CH[n] // N_DEV
        full[n] = gconv[:, :4, c0:c0 + cw].transpose(1, 0, 2).reshape(4, CONV_CH[n])
        c0 += cw
    for n in REPL:
        full[n] = given[n] if given[n].ndim == 2 else wsh[n]

    sched = _Schedule(shards, jnp.reshape(ci, (1,)).astype(jnp.int32))
    loss_local, grad_x, g, raw = _local_step(x[0], p[0, 0], loss_target[0], full, sched)
    sched.flush()
    summed, gat = sched.summed, sched.gathered_small

    outs = {}
    for n in BIG:
        outs[n] = _adamw(summed[n], wsh[n], msh[n], vsh[n], name="adamw_" + n)
    for n, k in (("lru_gate_a_w", "gate_a"), ("lru_gate_x_w", "gate_x")):
        flat = lambda a: a.reshape(N_HEAD * HEAD_P, HEAD_P)
        res = _adamw(gat[k], flat(wsh[n]), flat(msh[n]), flat(vsh[n]), name="adamw_" + n)
        outs[n] = tuple(r.reshape(N_HEAD, HEAD_P, HEAD_P) for r in res)
    for n, row in (("lru_gate_a_b", 5), ("lru_gate_x_b", 6)):
        outs[n] = _adamw(gat["lru"][:, row].reshape(N_DEV, N_HEAD, HEAD_P), wsh[n], msh[n], vsh[n], name="adamw_" + n)
    row_items = [("lru_conv_b", 0, 4), ("lru_a_param", 0, 7),
                 ("ssd_conv_b", 1, 4), ("ssd_dt_bias", 2, 0), ("ssd_a_log", 2, 1), ("ssd_d", 2, 2),
                 ("ssd_norm_w", 3, 0), ("ln1_g", 3, 1), ("ln1_b", 3, 2), ("ln2_g", 3, 3), ("ln2_b", 3, 4),
                 ("ln3_g", 3, 5), ("ln3_b", 3, 6)]
    vec = lambda a: a.reshape(1, -1)
    items = [(si, r0, vec(given[n]), vec(given["m_" + n]), vec(given["v_" + n])) for n, si, r0 in row_items]
    own = [(si, 0, wsh[n], msh[n], vsh[n]) for n, si in (("lru_conv_w", 0), ("ssd_conv_w", 1))]
    me1 = jnp.reshape(me, (1,)).astype(jnp.int32)
    res, loss_row = _adamw_rows([gat[k] for k in SMALL_SRC[:4]], items, own, me1, (3, 7), name="adamw_small")
    loss = loss_row[0, 0]
    for (n, _si, _r0), r4 in zip(row_items, res[:len(row_items)]):
        outs[n] = r4
    for n, r4 in zip(CONV, res[len(row_items):]):
        outs[n] = r4

    def fin(n, k):
        a = jnp.swapaxes(outs[n][k], 0, 1) if n == 'w_in' else outs[n][k]
        return a.reshape(given[n].shape)

    return (loss, grad_x[None],
            *[fin(n, 0) for n in WEIGHTS], *[fin(n, 1) for n in WEIGHTS],
            *[fin(n, 2) for n in WEIGHTS], *[fin(n, 3) for n in WEIGHTS])
```
